```python
import jax, jax.numpy as jnp
from jax import lax
import numpy as np

D_MODEL = 2048
BATCH = 8
SEQ = 2048
DEPTH = 1

CHUNK = 64
EPS = 1e-6

POOL_WINDOWS = (2, 4, 8, 16)
POOL_GROUPS = len(POOL_WINDOWS)
POOL_WIDTH = D_MODEL // 2
POOL_GROUP = POOL_WIDTH // POOL_GROUPS
POOL_OUT_GROUP = D_MODEL // POOL_GROUPS

GLA_HEADS = 4
GLA_DK = D_MODEL // 2 // GLA_HEADS
GLA_DV = D_MODEL // GLA_HEADS
QK_WIDTH = GLA_HEADS * GLA_DK
V_WIDTH = GLA_HEADS * GLA_DV
GATE_RANK = 16
GATE_TAU = 16.0

N_BRANCHES = 2
GATE_WIDTH = N_BRANCHES * D_MODEL

D_FF = 4 * D_MODEL

SPLITS = (
    POOL_WIDTH,
    POOL_WIDTH + QK_WIDTH,
    POOL_WIDTH + 2 * QK_WIDTH,
    POOL_WIDTH + 2 * QK_WIDTH + V_WIDTH,
    POOL_WIDTH + 2 * QK_WIDTH + 2 * V_WIDTH,
    POOL_WIDTH + 2 * QK_WIDTH + 2 * V_WIDTH + GATE_RANK,
)
IN_WIDTH = POOL_WIDTH + 2 * QK_WIDTH + 2 * V_WIDTH + GATE_RANK + GATE_WIDTH

kernel_name = "hybrid_pool_gla_gated_block"


def rmsnorm(x, g):
    xf = x.astype(jnp.float32)
    y = xf * lax.rsqrt(jnp.mean(xf * xf, axis=-1, keepdims=True) + EPS)
    return (y * g.astype(jnp.float32)).astype(x.dtype)


def pool_mixer(u, w_groups, scale):
    b, s, _ = u.shape
    uf = u.astype(jnp.float32)
    csum = jnp.cumsum(uf, axis=1)
    count = jnp.arange(1, s + 1, dtype=jnp.float32)[None, :, None]
    diffs = []
    for gi, w in enumerate(POOL_WINDOWS):
        sl = slice(gi * POOL_GROUP, (gi + 1) * POOL_GROUP)
        cg = csum[..., sl]
        c_prev = jnp.pad(cg, ((0, 0), (w, 0), (0, 0)))[:, :s]
        mean = (cg - c_prev) / jnp.minimum(count, float(w))
        diffs.append(mean - uf[..., sl])
    d = jnp.stack(diffs, axis=2).astype(u.dtype)
    y = jnp.einsum('bsgc,gce->bsge', d, w_groups).reshape(b, s, D_MODEL)
    return y * scale


def gla_mixer(q, k, v, g, a_low, w_alpha, b_alpha, norm_g):
    b, s, _ = q.shape
    nc = s // CHUNK
    log_a = jax.nn.log_sigmoid((a_low @ w_alpha + b_alpha).astype(jnp.float32)) / GATE_TAU

    def chunked(t, d):
        return t.astype(jnp.float32).reshape(b, nc, CHUNK, GLA_HEADS, d).transpose(1, 0, 3, 2, 4)

    qc = chunked(q, GLA_DK) * (GLA_DK ** -0.5)
    kc = chunked(k, GLA_DK)
    vc = chunked(v, GLA_DV)
    cum = jnp.cumsum(chunked(log_a, GLA_DK), axis=3)
    last = cum[:, :, :, -1:, :]
    k_dec = kc * jnp.exp(last - cum)
    chunk_decay = jnp.exp(last[:, :, :, 0, :])

    def step(state, inp):
        q_c, k_c, v_c, a_c = inp
        state = a_c[..., None] * state + jnp.einsum('bhcd,bhce->bhde', k_c, v_c)
        return state, jnp.einsum('bhcd,bhde->bhce', q_c, state)

    s0 = jnp.zeros((b, GLA_HEADS, GLA_DK, GLA_DV), jnp.float32)
    _, o = lax.scan(step, s0, (qc, k_dec, vc, chunk_decay))
    o = o.transpose(1, 0, 3, 2, 4).reshape(b, s, GLA_HEADS, GLA_DV)
    o = o * lax.rsqrt(jnp.mean(o * o, axis=-1, keepdims=True) + EPS) * norm_g.astype(jnp.float32)
    o = o.reshape(b, s, V_WIDTH) * jax.nn.silu(g.astype(jnp.float32))
    return o.astype(q.dtype)


def _fwd_setup_inputs(seed: int = 0) -> dict:
    key = jax.random.key(seed)
    ks = jax.random.split(key, 16)
    f32 = jnp.float32

    def nrm(k, shape, scale):
        return jax.random.normal(k, shape, f32) * scale

    L = DEPTH
    return {
        "x": jax.random.normal(ks[0], (BATCH, SEQ, D_MODEL), f32),
        "norm_mix_g": 1.0 + nrm(ks[1], (L, D_MODEL), 0.02),
        "w_in": nrm(ks[2], (L, D_MODEL, IN_WIDTH), D_MODEL ** -0.5),
        "pool_w": nrm(ks[3], (L, POOL_GROUPS, POOL_GROUP, POOL_OUT_GROUP), POOL_GROUP ** -0.5),
        "pool_scale": 1.0 + nrm(ks[4], (L, D_MODEL), 0.02),
        "w_alpha": nrm(ks[5], (L, GATE_RANK, QK_WIDTH), GATE_RANK ** -0.5),
        "b_alpha": nrm(ks[6], (L, QK_WIDTH), 0.02),
        "gla_norm_g": 1.0 + nrm(ks[7], (L, GLA_HEADS, GLA_DV), 0.02),
        "w_gla_out": nrm(ks[8], (L, V_WIDTH, D_MODEL), V_WIDTH ** -0.5),
        "w_out": nrm(ks[9], (L, D_MODEL, D_MODEL), D_MODEL ** -0.5),
        "norm_mlp_g": 1.0 + nrm(ks[10], (L, D_MODEL), 0.02),
        "w_mlp_up": nrm(ks[11], (L, D_MODEL, D_FF), D_MODEL ** -0.5),
        "w_mlp_down": nrm(ks[12], (L, D_FF, D_MODEL), D_FF ** -0.5),
        "norm_final_g": 1.0 + nrm(ks[13], (D_MODEL,), 0.02),
    }


def _fwd_reference(x, norm_mix_g, w_in, pool_w, pool_scale, w_alpha, b_alpha, gla_norm_g,
              w_gla_out, w_out, norm_mlp_g, w_mlp_up, w_mlp_down, norm_final_g):
    for l in range(DEPTH):
        h = rmsnorm(x, norm_mix_g[l])
        proj = h @ w_in[l]
        u, q, k, v, g, a_low, gate_logits = jnp.split(proj, SPLITS, axis=-1)
        y_pool = pool_mixer(u, pool_w[l], pool_scale[l])
        y_gla = gla_mixer(q, k, v, g, a_low, w_alpha[l], b_alpha[l], gla_norm_g[l]) @ w_gla_out[l]
        gate_pool, gate_gla = jnp.split(jax.nn.sigmoid(gate_logits), N_BRANCHES, axis=-1)
        mixed = gate_pool * y_pool + gate_gla * y_gla
        x = x + mixed @ w_out[l]
        h = rmsnorm(x, norm_mlp_g[l])
        x = x + jnp.square(jax.nn.relu(h @ w_mlp_up[l])) @ w_mlp_down[l]
    return rmsnorm(x, norm_final_g)


import jax as _jax
import jax.numpy as _jnp

TWIN_FORMAT = 'train_step'
FWD_PARAMS = ['x', 'norm_mix_g', 'w_in', 'pool_w', 'pool_scale', 'w_alpha', 'b_alpha', 'gla_norm_g', 'w_gla_out', 'w_out', 'norm_mlp_g', 'w_mlp_up', 'w_mlp_down', 'norm_final_g']
TWIN_WEIGHTS = ['norm_mix_g', 'w_in', 'pool_w', 'pool_scale', 'w_alpha', 'b_alpha', 'gla_norm_g', 'w_gla_out', 'w_out', 'norm_mlp_g', 'w_mlp_up', 'w_mlp_down', 'norm_final_g']
TWIN_DIFF_INPUT = 'x'
TWIN_INPUTS = ['x', 'norm_mix_g', 'w_in', 'pool_w', 'pool_scale', 'w_alpha', 'b_alpha', 'gla_norm_g', 'w_gla_out', 'w_out', 'norm_mlp_g', 'w_mlp_up', 'w_mlp_down', 'norm_final_g', 'loss_target', 'm_norm_mix_g', 'm_w_in', 'm_pool_w', 'm_pool_scale', 'm_w_alpha', 'm_b_alpha', 'm_gla_norm_g', 'm_w_gla_out', 'm_w_out', 'm_norm_mlp_g', 'm_w_mlp_up', 'm_w_mlp_down', 'm_norm_final_g', 'v_norm_mix_g', 'v_w_in', 'v_pool_w', 'v_pool_scale', 'v_w_alpha', 'v_b_alpha', 'v_gla_norm_g', 'v_w_gla_out', 'v_w_out', 'v_norm_mlp_g', 'v_w_mlp_up', 'v_w_mlp_down', 'v_norm_final_g']
TWIN_OUTPUTS = ['loss', 'grad_x', 'grad_norm_mix_g', 'grad_w_in', 'grad_pool_w', 'grad_pool_scale', 'grad_w_alpha', 'grad_b_alpha', 'grad_gla_norm_g', 'grad_w_gla_out', 'grad_w_out', 'grad_norm_mlp_g', 'grad_w_mlp_up', 'grad_w_mlp_down', 'grad_norm_final_g', 'delta_norm_mix_g', 'delta_w_in', 'delta_pool_w', 'delta_pool_scale', 'delta_w_alpha', 'delta_b_alpha', 'delta_gla_norm_g', 'delta_w_gla_out', 'delta_w_out', 'delta_norm_mlp_g', 'delta_w_mlp_up', 'delta_w_mlp_down', 'delta_norm_final_g', 'new_m_norm_mix_g', 'new_m_w_in', 'new_m_pool_w', 'new_m_pool_scale', 'new_m_w_alpha', 'new_m_b_alpha', 'new_m_gla_norm_g', 'new_m_w_gla_out', 'new_m_w_out', 'new_m_norm_mlp_g', 'new_m_w_mlp_up', 'new_m_w_mlp_down', 'new_m_norm_final_g', 'new_v_norm_mix_g', 'new_v_w_in', 'new_v_pool_w', 'new_v_pool_scale', 'new_v_w_alpha', 'new_v_b_alpha', 'new_v_gla_norm_g', 'new_v_w_gla_out', 'new_v_w_out', 'new_v_norm_mlp_g', 'new_v_w_mlp_up', 'new_v_w_mlp_down', 'new_v_norm_final_g']
TWIN_LEAF_KINDS = {'loss': 'loss', 'grad_x': 'grad_x', 'grad_norm_mix_g': 'grad_w', 'grad_w_in': 'grad_w', 'grad_pool_w': 'grad_w', 'grad_pool_scale': 'grad_w', 'grad_w_alpha': 'grad_w', 'grad_b_alpha': 'grad_w', 'grad_gla_norm_g': 'grad_w', 'grad_w_gla_out': 'grad_w', 'grad_w_out': 'grad_w', 'grad_norm_mlp_g': 'grad_w', 'grad_w_mlp_up': 'grad_w', 'grad_w_mlp_down': 'grad_w', 'grad_norm_final_g': 'grad_w', 'delta_norm_mix_g': 'delta_w', 'delta_w_in': 'delta_w', 'delta_pool_w': 'delta_w', 'delta_pool_scale': 'delta_w', 'delta_w_alpha': 'delta_w', 'delta_b_alpha': 'delta_w', 'delta_gla_norm_g': 'delta_w', 'delta_w_gla_out': 'delta_w', 'delta_w_out': 'delta_w', 'delta_norm_mlp_g': 'delta_w', 'delta_w_mlp_up': 'delta_w', 'delta_w_mlp_down': 'delta_w', 'delta_norm_final_g': 'delta_w', 'new_m_norm_mix_g': 'new_m', 'new_m_w_in': 'new_m', 'new_m_pool_w': 'new_m', 'new_m_pool_scale': 'new_m', 'new_m_w_alpha': 'new_m', 'new_m_b_alpha': 'new_m', 'new_m_gla_norm_g': 'new_m', 'new_m_w_gla_out': 'new_m', 'new_m_w_out': 'new_m', 'new_m_norm_mlp_g': 'new_m', 'new_m_w_mlp_up': 'new_m', 'new_m_w_mlp_down': 'new_m', 'new_m_norm_final_g': 'new_m', 'new_v_norm_mix_g': 'new_v', 'new_v_w_in': 'new_v', 'new_v_pool_w': 'new_v', 'new_v_pool_scale': 'new_v', 'new_v_w_alpha': 'new_v', 'new_v_b_alpha': 'new_v', 'new_v_gla_norm_g': 'new_v', 'new_v_w_gla_out': 'new_v', 'new_v_w_out': 'new_v', 'new_v_norm_mlp_g': 'new_v', 'new_v_w_mlp_up': 'new_v', 'new_v_w_mlp_down': 'new_v', 'new_v_norm_final_g': 'new_v'}


def _forward(args):
    return _fwd_reference(*[args[k] for k in FWD_PARAMS])


def _output_shape():
    out = _jax.eval_shape(lambda: _forward(_fwd_setup_inputs(0)))
    return out.shape, out.dtype

N_MICROBATCH = 1
ADAM_LR = 0.001
ADAM_B1 = 0.9
ADAM_B2 = 0.999
ADAM_EPS = 1e-08
ADAM_WD = 0.01
ADAM_STEP = 10
PER_EXAMPLE_BATCH_AXIS = {'x': 0, 'loss_target': 0}
SHARED_INPUTS = []
_WEIGHT_DTYPES = {'norm_mix_g': _jnp.float32, 'w_in': _jnp.float32, 'pool_w': _jnp.float32, 'pool_scale': _jnp.float32, 'w_alpha': _jnp.float32, 'b_alpha': _jnp.float32, 'gla_norm_g': _jnp.float32, 'w_gla_out': _jnp.float32, 'w_out': _jnp.float32, 'norm_mlp_g': _jnp.float32, 'w_mlp_up': _jnp.float32, 'w_mlp_down': _jnp.float32, 'norm_final_g': _jnp.float32}
MOMENT_SCALE = {'norm_mix_g': 5.050832e-02, 'w_in': 2.122350e-02, 'pool_w': 2.830231e-02, 'pool_scale': 2.743399e-02, 'w_alpha': 2.982854e-03, 'b_alpha': 1.223158e-02, 'gla_norm_g': 1.911145e-02, 'w_gla_out': 1.909553e-02, 'w_out': 3.396435e-02, 'norm_mlp_g': 5.345018e-02, 'w_mlp_up': 2.641030e-02, 'w_mlp_down': 5.035270e-02, 'norm_final_g': 8.072097e+00}


def _to_microbatches(a, axis):
    t = _jnp.moveaxis(a, axis, 0)
    t = t.reshape((N_MICROBATCH, t.shape[0] // N_MICROBATCH) + t.shape[1:])
    return _jnp.moveaxis(t, 1, axis + 1)


def setup_inputs(seed: int = 0) -> dict:
    inp = _fwd_setup_inputs(seed)
    key = _jax.random.fold_in(_jax.random.key(seed), 7919)
    shape, _ = _output_shape()
    out = dict(inp)
    out["loss_target"] = _jax.random.normal(_jax.random.fold_in(key, 0), shape, _jnp.float32)
    for i, name in enumerate(TWIN_WEIGHTS):
        w = inp[name].astype(_jnp.float32)
        if MOMENT_SCALE is None:
            s = _jnp.sqrt(_jnp.mean(_jnp.square(w)) + 1e-30)
        else:
            s = MOMENT_SCALE[name]
        km, kv = _jax.random.split(_jax.random.fold_in(key, i + 1))
        out[name] = w
        out["m_" + name] = s * _jax.random.normal(km, w.shape, _jnp.float32)
        out["v_" + name] = (s * s) * _jax.random.uniform(kv, w.shape, _jnp.float32, 0.5, 1.5)
    if N_MICROBATCH > 1:
        for name, axis in PER_EXAMPLE_BATCH_AXIS.items():
            out[name] = _to_microbatches(out[name], axis)
    return {'x': out['x'], 'norm_mix_g': out['norm_mix_g'], 'w_in': out['w_in'], 'pool_w': out['pool_w'], 'pool_scale': out['pool_scale'], 'w_alpha': out['w_alpha'], 'b_alpha': out['b_alpha'], 'gla_norm_g': out['gla_norm_g'], 'w_gla_out': out['w_gla_out'], 'w_out': out['w_out'], 'norm_mlp_g': out['norm_mlp_g'], 'w_mlp_up': out['w_mlp_up'], 'w_mlp_down': out['w_mlp_down'], 'norm_final_g': out['norm_final_g'], 'loss_target': out['loss_target'], 'm_norm_mix_g': out['m_norm_mix_g'], 'm_w_in': out['m_w_in'], 'm_pool_w': out['m_pool_w'], 'm_pool_scale': out['m_pool_scale'], 'm_w_alpha': out['m_w_alpha'], 'm_b_alpha': out['m_b_alpha'], 'm_gla_norm_g': out['m_gla_norm_g'], 'm_w_gla_out': out['m_w_gla_out'], 'm_w_out': out['m_w_out'], 'm_norm_mlp_g': out['m_norm_mlp_g'], 'm_w_mlp_up': out['m_w_mlp_up'], 'm_w_mlp_down': out['m_w_mlp_down'], 'm_norm_final_g': out['m_norm_final_g'], 'v_norm_mix_g': out['v_norm_mix_g'], 'v_w_in': out['v_w_in'], 'v_pool_w': out['v_pool_w'], 'v_pool_scale': out['v_pool_scale'], 'v_w_alpha': out['v_w_alpha'], 'v_b_alpha': out['v_b_alpha'], 'v_gla_norm_g': out['v_gla_norm_g'], 'v_w_gla_out': out['v_w_gla_out'], 'v_w_out': out['v_w_out'], 'v_norm_mlp_g': out['v_norm_mlp_g'], 'v_w_mlp_up': out['v_w_mlp_up'], 'v_w_mlp_down': out['v_w_mlp_down'], 'v_norm_final_g': out['v_norm_final_g']}


def _loss(weights, diff, rest, loss_target):
    with _jax.named_scope("forward"):
        args = {**rest, TWIN_DIFF_INPUT: diff, **{k: w.astype(_WEIGHT_DTYPES[k]) for k, w in weights.items()}}
        y = _forward(args)
    with _jax.named_scope("loss_head"):
        err = _jnp.square(y.astype(_jnp.float32) - loss_target)
        return 0.5 * _jnp.sum(_jnp.mean(err, axis=-1)) if err.ndim else 0.5 * err


def _adamw(w, g, m, v):
    m = ADAM_B1 * m + (1.0 - ADAM_B1) * g
    v = ADAM_B2 * v + (1.0 - ADAM_B2) * _jnp.square(g)
    m_hat = m / (1.0 - ADAM_B1 ** ADAM_STEP)
    v_hat = v / (1.0 - ADAM_B2 ** ADAM_STEP)
    delta = -ADAM_LR * (m_hat / (_jnp.sqrt(v_hat) + ADAM_EPS) + ADAM_WD * w)
    return delta, m, v


def reference(x, norm_mix_g, w_in, pool_w, pool_scale, w_alpha, b_alpha, gla_norm_g, w_gla_out, w_out, norm_mlp_g, w_mlp_up, w_mlp_down, norm_final_g, loss_target, m_norm_mix_g, m_w_in, m_pool_w, m_pool_scale, m_w_alpha, m_b_alpha, m_gla_norm_g, m_w_gla_out, m_w_out, m_norm_mlp_g, m_w_mlp_up, m_w_mlp_down, m_norm_final_g, v_norm_mix_g, v_w_in, v_pool_w, v_pool_scale, v_w_alpha, v_b_alpha, v_gla_norm_g, v_w_gla_out, v_w_out, v_norm_mlp_g, v_w_mlp_up, v_w_mlp_down, v_norm_final_g):
    given = dict(x=x, norm_mix_g=norm_mix_g, w_in=w_in, pool_w=pool_w, pool_scale=pool_scale, w_alpha=w_alpha, b_alpha=b_alpha, gla_norm_g=gla_norm_g, w_gla_out=w_gla_out, w_out=w_out, norm_mlp_g=norm_mlp_g, w_mlp_up=w_mlp_up, w_mlp_down=w_mlp_down, norm_final_g=norm_final_g, loss_target=loss_target, m_norm_mix_g=m_norm_mix_g, m_w_in=m_w_in, m_pool_w=m_pool_w, m_pool_scale=m_pool_scale, m_w_alpha=m_w_alpha, m_b_alpha=m_b_alpha, m_gla_norm_g=m_gla_norm_g, m_w_gla_out=m_w_gla_out, m_w_out=m_w_out, m_norm_mlp_g=m_norm_mlp_g, m_w_mlp_up=m_w_mlp_up, m_w_mlp_down=m_w_mlp_down, m_norm_final_g=m_norm_final_g, v_norm_mix_g=v_norm_mix_g, v_w_in=v_w_in, v_pool_w=v_pool_w, v_pool_scale=v_pool_scale, v_w_alpha=v_w_alpha, v_b_alpha=v_b_alpha, v_gla_norm_g=v_gla_norm_g, v_w_gla_out=v_w_gla_out, v_w_out=v_w_out, v_norm_mlp_g=v_norm_mlp_g, v_w_mlp_up=v_w_mlp_up, v_w_mlp_down=v_w_mlp_down, v_norm_final_g=v_norm_final_g)
    weights = {n: given[n] for n in TWIN_WEIGHTS}
    shared = {n: given[n] for n in SHARED_INPUTS}
    per_example = {n: given[n] for n in ['x']}
    grad_fn = _jax.value_and_grad(_loss, argnums=(0, 1))

    def one_microbatch(ex, loss_target):
        ex = dict(ex)
        diff = ex.pop(TWIN_DIFF_INPUT)
        return grad_fn(weights, diff, {**shared, **ex}, loss_target)

    if N_MICROBATCH == 1:
        loss, (grad_w, grad_x) = one_microbatch(per_example, given["loss_target"])
    else:
        def body(carry, xs):
            loss_sum, grad_sum = carry
            l_k, (gw_k, gx_k) = one_microbatch(xs[0], xs[1])
            with _jax.named_scope("update"):
                return (loss_sum + l_k, _jax.tree.map(_jnp.add, grad_sum, gw_k)), gx_k

        init = (_jnp.zeros((), _jnp.float32), _jax.tree.map(_jnp.zeros_like, weights))
        (loss, grad_w), grad_x = _jax.lax.scan(body, init, (per_example, given["loss_target"]))
    with _jax.named_scope("update"):
        delta_w, new_m, new_v = {}, {}, {}
        for n in TWIN_WEIGHTS:
            delta_w[n], new_m[n], new_v[n] = _adamw(weights[n], grad_w[n], given["m_" + n], given["v_" + n])
    return (loss, grad_x, *[grad_w[n] for n in TWIN_WEIGHTS], *[delta_w[n] for n in TWIN_WEIGHTS],
            *[new_m[n] for n in TWIN_WEIGHTS], *[new_v[n] for n in TWIN_WEIGHTS])
```

```python
import functools

import jax
import jax.numpy as jnp
from jax import lax
from jax.experimental import pallas as pl
from jax.experimental.pallas import tpu as pltpu

F32 = jnp.float32
BF16 = jnp.bfloat16
SDS = jax.ShapeDtypeStruct
MESH = pl.DeviceIdType.MESH
ANY = pl.BlockSpec(memory_space=pl.ANY)

T = 2048
D = 2048
DFF = 8192
NCHIP = 4
IN_WIDTH = 11280
IN_SHARD = IN_WIDTH // NCHIP
CHUNK = 64
NCHUNK = T // CHUNK
HEADS = 4
DK = 256
DV = 512
QK = HEADS * DK
EPS = 1e-6
POOL_WINDOWS = (2, 4, 8, 16)
PG = 256
PO = 512

OV, OG, OGP, OGG, OU, OQ, OKK, OA = 0, 2048, 4096, 6144, 8192, 9216, 10240, 11264
NCAT = 11520
APAD = 128

VMEM_CAP = 56 * 1024 * 1024

ADAM_LR, ADAM_B1, ADAM_B2, ADAM_EPS, ADAM_WD, ADAM_STEP = 0.001, 0.9, 0.999, 1e-08, 0.01, 10


def _cparams(vmem_bytes=None, sem=None):
    kw = {}
    if vmem_bytes is not None:
        kw["vmem_limit_bytes"] = int(min(max(vmem_bytes, 32 * 1024 * 1024), VMEM_CAP))
    if sem is not None:
        kw["dimension_semantics"] = sem
    return pltpu.CompilerParams(**kw)


def _nbytes(shape, dtype):
    n = 1
    for s in shape:
        if s is not None:
            n *= s
    return n * jnp.dtype(dtype).itemsize


def _sigmoid(x):
    return 1.0 / (1.0 + jnp.exp(-x))


def matmul(name, a, b, *, a_spec, b_spec, cdims, grid, acc_shape, outs, extras=(), epi):
    nj, ni, nk = grid
    ne, no = len(extras), len(outs)

    def body(*refs):
        a_ref, b_ref = refs[0], refs[1]
        ex = refs[2:2 + ne]
        out_refs = refs[2 + ne:2 + ne + no]
        i = pl.program_id(1)
        part = lax.dot_general(a_ref[...], b_ref[...], (cdims, ((), ())), preferred_element_type=F32)
        if nk == 1:
            epi(part, ex, out_refs, i)
        else:
            acc_ref = refs[2 + ne + no]
            k = pl.program_id(2)

            @pl.when(k == 0)
            def _():
                acc_ref[...] = part

            @pl.when(k > 0)
            def _():
                acc_ref[...] += part

            @pl.when(k == nk - 1)
            def _():
                epi(acc_ref[...], ex, out_refs, i)

    in_specs = [pl.BlockSpec(*a_spec), pl.BlockSpec(*b_spec)] + [pl.BlockSpec(bs, im) for _, bs, im in extras]
    out_specs = [pl.BlockSpec(bs, im) for _, _, bs, im in outs]
    out_shape = [SDS(s, dt) for s, dt, _, _ in outs]
    vm = 2 * (_nbytes(a_spec[0], a.dtype) + _nbytes(b_spec[0], b.dtype))
    vm += 2 * sum(_nbytes(bs, arr.dtype) for arr, bs, _ in extras)
    vm += 2 * sum(_nbytes(bs, dt) for _, dt, bs, _ in outs)
    vm += 6 * _nbytes(acc_shape, F32)
    scratch = [pltpu.VMEM(acc_shape, F32)] if nk > 1 else []
    return pl.pallas_call(
        body, name=name, grid=grid, in_specs=in_specs, out_specs=out_specs, out_shape=out_shape,
        scratch_shapes=scratch,
        compiler_params=_cparams(vm, ("arbitrary", "arbitrary", "arbitrary")),
    )(a, b, *[arr for arr, _, _ in extras])


NN = ((1,), (0,))
NT = ((1,), (1,))
TN = ((0,), (0,))


def _row_acc(out_ref, val, i):
    @pl.when(i == 0)
    def _():
        out_ref[...] = val

    @pl.when(i > 0)
    def _():
        out_ref[...] += val


def _rms_bwd(xn, r, dxn):
    return r * (dxn - xn * jnp.mean(dxn * xn, axis=-1, keepdims=True))


def norm1(x, g):
    tm = 256

    def body(x_ref, g_ref, h_ref):
        xv = x_ref[...]
        r = lax.rsqrt(jnp.mean(xv * xv, axis=-1, keepdims=True) + EPS)
        h_ref[...] = (xv * r * g_ref[...]).astype(BF16)

    return pl.pallas_call(
        body, name="norm1", grid=(T // tm,),
        in_specs=[pl.BlockSpec((tm, D), lambda i: (i, 0)), pl.BlockSpec((1, D), lambda i: (0, 0))],
        out_specs=pl.BlockSpec((tm, D), lambda i: (i, 0)), out_shape=SDS((T, D), BF16),
        compiler_params=_cparams(32 * 1024 * 1024, ("arbitrary",)),
    )(x, g)


def mm_in(h1, wcat):
    tm, tn = 512, 1280

    def epi(acc, ex, outs, i):
        outs[0][...] = acc.astype(BF16)

    return matmul("mm_in", h1, wcat, a_spec=((tm, D), lambda j, i, k: (i, 0)), b_spec=((D, tn), lambda j, i, k: (0, j)),
                  cdims=NN, grid=(NCAT // tn, T // tm, 1), acc_shape=(tm, tn),
                  outs=[((T, NCAT), BF16, (tm, tn), lambda j, i, k: (i, j))], epi=epi)[0]


def _window_sum(x, w, up):
    n = x.shape[0]
    row = lax.broadcasted_iota(jnp.int32, x.shape, 0)
    s, sh = x, 1
    while sh < w:
        if up:
            s = s + jnp.where(row < n - sh, pltpu.roll(s, n - sh, axis=0), 0.0)
        else:
            s = s + jnp.where(row >= sh, pltpu.roll(s, sh, axis=0), 0.0)
        sh *= 2
    return s


def _inv_count(shape, w):
    row = lax.broadcasted_iota(jnp.int32, shape, 0)
    return 1.0 / jnp.minimum(row + 1, w).astype(F32)


def pool_fwd(pcat, pw):
    def body(u_ref, pw_ref, d_ref, y_ref):
        for gi, w in enumerate(POOL_WINDOWS):
            ug = u_ref[:, gi * PG:(gi + 1) * PG].astype(F32)
            dg = _window_sum(ug, w, False) * _inv_count(ug.shape, w) - ug
            db = dg.astype(BF16)
            d_ref[:, gi * PG:(gi + 1) * PG] = db
            y_ref[:, gi * PO:(gi + 1) * PO] = jnp.dot(db, pw_ref[gi], preferred_element_type=F32).astype(BF16)

    return pl.pallas_call(
        body, name="pool_fwd", grid=(1,),
        in_specs=[pl.BlockSpec((T, 4 * PG), lambda i: (0, OU // (4 * PG))), pl.BlockSpec((4, PG, PO), lambda i: (0, 0, 0))],
        out_specs=[pl.BlockSpec((T, 4 * PG), lambda i: (0, 0)), pl.BlockSpec((T, D), lambda i: (0, 0))],
        out_shape=[SDS((T, 4 * PG), BF16), SDS((T, D), BF16)],
        compiler_params=_cparams(48 * 1024 * 1024, ("arbitrary",)),
    )(pcat, pw)


def pool_bwd(dylin, d, pw):
    def body(dy_ref, d_ref, pw_ref, du_ref, dpw_ref):
        for gi, w in enumerate(POOL_WINDOWS):
            dyl = dy_ref[:, gi * PO:(gi + 1) * PO]
            dd = lax.dot_general(dyl, pw_ref[gi], (NT, ((), ())), preferred_element_type=F32)
            du = _window_sum(dd * _inv_count(dd.shape, w), w, True) - dd
            du_ref[:, gi * PG:(gi + 1) * PG] = du.astype(BF16)
            dpw_ref[gi] = lax.dot_general(d_ref[:, gi * PG:(gi + 1) * PG], dyl, (TN, ((), ())),
                                          preferred_element_type=F32).astype(BF16)

    return pl.pallas_call(
        body, name="pool_bwd", grid=(1,),
        in_specs=[pl.BlockSpec((T, D), lambda i: (0, 0)), pl.BlockSpec((T, 4 * PG), lambda i: (0, 0)),
                  pl.BlockSpec((4, PG, PO), lambda i: (0, 0, 0))],
        out_specs=[pl.BlockSpec((T, 4 * PG), lambda i: (0, 0)), pl.BlockSpec((4, PG, PO), lambda i: (0, 0, 0))],
        out_shape=[SDS((T, 4 * PG), BF16), SDS((4, PG, PO), BF16)],
        compiler_params=_cparams(48 * 1024 * 1024, ("arbitrary",)),
    )(dylin, d, pw)


def _gate_decay(alow, wa, ba):
    a = jnp.dot(alow, wa, preferred_element_type=F32) + ba
    ls = jax.nn.log_sigmoid(a) * (1.0 / 16.0)
    r = lax.broadcasted_iota(jnp.int32, (CHUNK, CHUNK), 0)
    c = lax.broadcasted_iota(jnp.int32, (CHUNK, CHUNK), 1)
    tri = jnp.where(c <= r, 1.0, 0.0).astype(F32)
    cum = jnp.dot(tri, ls, preferred_element_type=F32, precision=lax.Precision.HIGHEST)
    last = cum[CHUNK - 1:CHUNK, :]
    return a, jnp.exp(last - cum), jnp.exp(last)


def gla_fwd(pcat, wa, ba, ng):
    def body(q_ref, k_ref, v_ref, g_ref, al_ref, wa_ref, ba_ref, ng_ref, og_ref, o_ref, st_ref, s_scr):
        @pl.when(pl.program_id(0) == 0)
        def _():
            s_scr[...] = jnp.zeros_like(s_scr)

        _, e, decay = _gate_decay(al_ref[...], wa_ref[...], ba_ref[...])
        kd = (k_ref[...].astype(F32) * e).astype(BF16)
        qs = (q_ref[...].astype(F32) * (DK ** -0.5)).astype(BF16)
        for h in range(HEADS):
            ck = slice(h * DK, (h + 1) * DK)
            cv = slice(h * DV, (h + 1) * DV)
            s_new = s_scr[h] * decay[:, ck] + lax.dot_general(v_ref[:, cv], kd[:, ck], (TN, ((), ())),
                                                               preferred_element_type=F32)
            s_scr[h] = s_new
            sb = s_new.astype(BF16)
            st_ref[h] = sb
            oh = lax.dot_general(qs[:, ck], sb, (NT, ((), ())), preferred_element_type=F32)
            o_ref[:, cv] = oh.astype(BF16)
            on = oh * lax.rsqrt(jnp.mean(oh * oh, axis=-1, keepdims=True) + EPS) * ng_ref[:, cv]
            gv = g_ref[:, cv].astype(F32)
            og_ref[:, cv] = (on * (gv * _sigmoid(gv))).astype(BF16)

    row = lambda c: (c, 0)
    return pl.pallas_call(
        body, name="gla_fwd", grid=(NCHUNK,),
        in_specs=[pl.BlockSpec((CHUNK, QK), lambda c: (c, OQ // QK)), pl.BlockSpec((CHUNK, QK), lambda c: (c, OKK // QK)),
                  pl.BlockSpec((CHUNK, D), lambda c: (c, OV // D)), pl.BlockSpec((CHUNK, D), lambda c: (c, OG // D)),
                  pl.BlockSpec((CHUNK, APAD), lambda c: (c, OA // APAD)),
                  pl.BlockSpec((APAD, QK), lambda c: (0, 0)), pl.BlockSpec((1, QK), lambda c: (0, 0)),
                  pl.BlockSpec((1, D), lambda c: (0, 0))],
        out_specs=[pl.BlockSpec((CHUNK, D), row), pl.BlockSpec((CHUNK, D), row),
                   pl.BlockSpec((None, HEADS, DV, DK), lambda c: (c, 0, 0, 0))],
        out_shape=[SDS((T, D), BF16), SDS((T, D), BF16), SDS((NCHUNK, HEADS, DV, DK), BF16)],
        scratch_shapes=[pltpu.VMEM((HEADS, DV, DK), F32)],
        compiler_params=_cparams(32 * 1024 * 1024, ("arbitrary",)),
    )(pcat, pcat, pcat, pcat, pcat, wa, ba, ng)


def gla_bwd(do, pcat, states, wa, ba):
    def body(do_ref, q_ref, k_ref, v_ref, al_ref, sc_ref, sp_ref, wa_ref, ba_ref,
             dq_ref, dk_ref, dv_ref, dal_ref, dwa_ref, dba_ref, ds_scr):
        i = pl.program_id(0)

        @pl.when(i == 0)
        def _():
            ds_scr[...] = jnp.zeros_like(ds_scr)

        has_prev = jnp.where(i < NCHUNK - 1, 1.0, 0.0).astype(F32)
        a, e, decay = _gate_decay(al_ref[...], wa_ref[...], ba_ref[...])
        kf = k_ref[...].astype(F32)
        kdf = kf * e
        kd = kdf.astype(BF16)
        qs = (q_ref[...].astype(F32) * (DK ** -0.5)).astype(BF16)
        dkd_parts, ddecay_parts = [], []
        for h in range(HEADS):
            ck = slice(h * DK, (h + 1) * DK)
            cv = slice(h * DV, (h + 1) * DV)
            doh = do_ref[:, cv]
            ds = ds_scr[h] + lax.dot_general(doh, qs[:, ck], (TN, ((), ())), preferred_element_type=F32)
            dsb = ds.astype(BF16)
            dq_ref[:, ck] = (jnp.dot(doh, sc_ref[h], preferred_element_type=F32) * (DK ** -0.5)).astype(BF16)
            dkd_parts.append(jnp.dot(v_ref[:, cv], dsb, preferred_element_type=F32))
            dv_ref[:, cv] = lax.dot_general(kd[:, ck], dsb, (NT, ((), ())), preferred_element_type=F32).astype(BF16)
            ddecay_parts.append(jnp.sum(ds * sp_ref[h].astype(F32), axis=0, keepdims=True) * has_prev)
            ds_scr[h] = ds * decay[:, ck]
        dkd = jnp.concatenate(dkd_parts, axis=1)
        ddecay = jnp.concatenate(ddecay_parts, axis=1)
        dk_ref[...] = (dkd * e).astype(BF16)
        dearg = dkd * kdf
        dlast = jnp.sum(dearg, axis=0, keepdims=True) + ddecay * decay
        r = lax.broadcasted_iota(jnp.int32, (CHUNK, CHUNK), 0)
        c = lax.broadcasted_iota(jnp.int32, (CHUNK, CHUNK), 1)
        triu = jnp.where(c >= r, 1.0, 0.0).astype(F32)
        dls = dlast - jnp.dot(triu, dearg, preferred_element_type=F32, precision=lax.Precision.HIGHEST)
        da = dls * (1.0 / 16.0) * (1.0 - _sigmoid(a))
        dab = da.astype(BF16)
        dal_ref[...] = lax.dot_general(dab, wa_ref[...], (NT, ((), ())), preferred_element_type=F32).astype(BF16)
        dwa = lax.dot_general(al_ref[...], dab, (TN, ((), ())), preferred_element_type=F32)
        dba = jnp.sum(da, axis=0, keepdims=True)

        @pl.when(i == 0)
        def _():
            dwa_ref[...] = dwa
            dba_ref[...] = dba

        @pl.when(i > 0)
        def _():
            dwa_ref[...] += dwa
            dba_ref[...] += dba

    rev = lambda i: NCHUNK - 1 - i
    return pl.pallas_call(
        body, name="gla_bwd", grid=(NCHUNK,),
        in_specs=[pl.BlockSpec((CHUNK, D), lambda i: (rev(i), 0)),
                  pl.BlockSpec((CHUNK, QK), lambda i: (rev(i), OQ // QK)), pl.BlockSpec((CHUNK, QK), lambda i: (rev(i), OKK // QK)),
                  pl.BlockSpec((CHUNK, D), lambda i: (rev(i), OV // D)), pl.BlockSpec((CHUNK, APAD), lambda i: (rev(i), OA // APAD)),
                  pl.BlockSpec((None, HEADS, DV, DK), lambda i: (rev(i), 0, 0, 0)),
                  pl.BlockSpec((None, HEADS, DV, DK), lambda i: (jnp.maximum(rev(i) - 1, 0), 0, 0, 0)),
                  pl.BlockSpec((APAD, QK), lambda i: (0, 0)), pl.BlockSpec((1, QK), lambda i: (0, 0))],
        out_specs=[pl.BlockSpec((CHUNK, QK), lambda i: (rev(i), 0)), pl.BlockSpec((CHUNK, QK), lambda i: (rev(i), 0)),
                   pl.BlockSpec((CHUNK, D), lambda i: (rev(i), 0)), pl.BlockSpec((CHUNK, APAD), lambda i: (rev(i), 0)),
                   pl.BlockSpec((APAD, QK), lambda i: (0, 0)), pl.BlockSpec((1, QK), lambda i: (0, 0))],
        out_shape=[SDS((T, QK), BF16), SDS((T, QK), BF16), SDS((T, D), BF16), SDS((T, APAD), BF16),
                   SDS((APAD, QK), F32), SDS((1, QK), F32)],
        scratch_shapes=[pltpu.VMEM((HEADS, DV, DK), F32)],
        compiler_params=_cparams(32 * 1024 * 1024, ("arbitrary",)),
    )(do, pcat, pcat, pcat, pcat, states, states, wa, ba)


TMF = 256
_rowblk = ((TMF, D), lambda j, i, k: (i, 0))
_vec = ((1, D), lambda j, i, k: (0, 0))


def _full_spec(col):
    return ((TMF, D), lambda j, i, k: (i, col))


def mm_gla_out(og, w, ylin, pcat, pscale):
    def epi(acc, ex, outs, i):
        ylin_ref, lgp_ref, lgg_ref, ps_ref = ex
        gp = _sigmoid(lgp_ref[...].astype(F32))
        gg = _sigmoid(lgg_ref[...].astype(F32))
        outs[0][...] = (gp * (ylin_ref[...].astype(F32) * ps_ref[...]) + gg * acc).astype(BF16)
        outs[1][...] = acc.astype(BF16)

    return matmul("mm_gla_out", og, w, a_spec=_rowblk, b_spec=((D, D), lambda j, i, k: (0, 0)), cdims=NN,
                  grid=(1, T // TMF, 1), acc_shape=(TMF, D),
                  extras=[(ylin, *_rowblk), (pcat, *_full_spec(OGP // D)), (pcat, *_full_spec(OGG // D)), (pscale, *_vec)],
                  outs=[((T, D), BF16, *_rowblk), ((T, D), BF16, *_rowblk)], epi=epi)


def mm_out(mixed, w, x, g2):
    def epi(acc, ex, outs, i):
        x_ref, g_ref = ex
        x2 = x_ref[...] + acc
        r = lax.rsqrt(jnp.mean(x2 * x2, axis=-1, keepdims=True) + EPS)
        outs[0][...] = x2
        outs[1][...] = (x2 * r * g_ref[...]).astype(BF16)

    return matmul("mm_out", mixed, w, a_spec=_rowblk, b_spec=((D, D), lambda j, i, k: (0, 0)), cdims=NN,
                  grid=(1, T // TMF, 1), acc_shape=(TMF, D), extras=[(x, *_rowblk), (g2, *_vec)],
                  outs=[((T, D), F32, *_rowblk), ((T, D), BF16, *_rowblk)], epi=epi)


def mm_up(h2, wup):
    def epi(acc, ex, outs, i):
        r = jnp.maximum(acc, 0.0)
        outs[0][...] = r.astype(BF16)
        outs[1][...] = (r * r).astype(BF16)

    blk = ((TMF, D), lambda j, i, k: (i, j))
    return matmul("mm_up", h2, wup, a_spec=_rowblk, b_spec=((None, D, D), lambda j, i, k: (j, 0, 0)), cdims=NN,
                  grid=(NCHIP, T // TMF, 1), acc_shape=(TMF, D),
                  outs=[((T, DFF), BF16, *blk), ((T, DFF), BF16, *blk)], epi=epi)


def mm_down(act, wdown, x2, tgt, gf):
    tk = 2048

    def epi(acc, ex, outs, i):
        x2_ref, t_ref, g_ref = ex
        dx_ref, dxb_ref, gnf_ref, loss_ref = outs
        x3 = x2_ref[...] + acc
        r = lax.rsqrt(jnp.mean(x3 * x3, axis=-1, keepdims=True) + EPS)
        xn = x3 * r
        err = xn * g_ref[...] - t_ref[...]
        lsum = 0.5 * jnp.sum(jnp.mean(err * err, axis=-1, keepdims=True), axis=0, keepdims=True)
        dy = err * (1.0 / D)
        _row_acc(gnf_ref, jnp.sum(dy * xn, axis=0, keepdims=True), i)
        _row_acc(loss_ref, jnp.broadcast_to(lsum, (1, 128)), i)
        dx3 = _rms_bwd(xn, r, dy * g_ref[...])
        dx_ref[...] = dx3
        dxb_ref[...] = dx3.astype(BF16)

    return matmul("mm_down", act, wdown, a_spec=((TMF, tk), lambda j, i, k: (i, k)), b_spec=((tk, D), lambda j, i, k: (k, 0)),
                  cdims=NN, grid=(1, T // TMF, DFF // tk), acc_shape=(TMF, D),
                  extras=[(x2, *_rowblk), (tgt, *_rowblk), (gf, *_vec)],
                  outs=[((T, D), F32, *_rowblk), ((T, D), BF16, *_rowblk), ((1, D), F32, *_vec),
                        ((1, 128), F32, (1, 128), lambda j, i, k: (0, 0))], epi=epi)


def mm_dact(dx3b, wdown, rup):
    def epi(acc, ex, outs, i):
        outs[0][...] = (acc * 2.0 * ex[0][...].astype(F32)).astype(BF16)

    blk = ((TMF, D), lambda j, i, k: (i, j))
    return matmul("mm_dact", dx3b, wdown, a_spec=_rowblk, b_spec=((D, D), lambda j, i, k: (j, 0)), cdims=NT,
                  grid=(DFF // D, T // TMF, 1), acc_shape=(TMF, D), extras=[(rup, *blk)],
                  outs=[((T, DFF), BF16, *blk)], epi=epi)[0]


def mm_wgrad(name, a, b, m, n, out_shape, out_block, out_map, tm, tn):
    def epi(acc, ex, outs, i):
        outs[0][...] = acc.astype(BF16)

    return matmul(name, a, b, a_spec=((T, tm), lambda j, i, k: (0, i)), b_spec=((T, tn), lambda j, i, k: (0, j)),
                  cdims=TN, grid=(n // tn, m // tm, 1), acc_shape=(tm, tn),
                  outs=[(out_shape, BF16, out_block, out_map)], epi=epi)[0]


def mm_dh2(dup, wup, x2, dx3, g2):
    def epi(acc, ex, outs, i):
        x2_ref, dx3_ref, g_ref = ex
        x2 = x2_ref[...]
        r = lax.rsqrt(jnp.mean(x2 * x2, axis=-1, keepdims=True) + EPS)
        xn = x2 * r
        _row_acc(outs[2], jnp.sum(acc * xn, axis=0, keepdims=True), i)
        dx2 = dx3_ref[...] + _rms_bwd(xn, r, acc * g_ref[...])
        outs[0][...] = dx2
        outs[1][...] = dx2.astype(BF16)

    return matmul("mm_dh2", dup, wup, a_spec=((TMF, D), lambda j, i, k: (i, k)),
                  b_spec=((None, D, D), lambda j, i, k: (k, 0, 0)), cdims=NT, grid=(1, T // TMF, NCHIP), acc_shape=(TMF, D),
                  extras=[(x2, *_rowblk), (dx3, *_rowblk), (g2, *_vec)],
                  outs=[((T, D), F32, *_rowblk), ((T, D), BF16, *_rowblk), ((1, D), F32, *_vec)], epi=epi)


def mm_dmixed(dx2b, wout, pcat, ylin, ygla, pscale):
    def epi(acc, ex, outs, i):
        lgp_ref, lgg_ref, ylin_ref, ygla_ref, ps_ref = ex
        gp = _sigmoid(lgp_ref[...].astype(F32))
        gg = _sigmoid(lgg_ref[...].astype(F32))
        yl = ylin_ref[...].astype(F32)
        ps = ps_ref[...]
        agp = acc * gp
        outs[0][...] = (agp * ps).astype(BF16)
        outs[1][...] = (acc * gg).astype(BF16)
        outs[2][...] = (agp * (yl * ps) * (1.0 - gp)).astype(BF16)
        outs[3][...] = (acc * ygla_ref[...].astype(F32) * gg * (1.0 - gg)).astype(BF16)
        _row_acc(outs[4], jnp.sum(agp * yl, axis=0, keepdims=True), i)

    return matmul("mm_dmixed", dx2b, wout, a_spec=_rowblk, b_spec=((D, D), lambda j, i, k: (0, 0)), cdims=NT,
                  grid=(1, T // TMF, 1), acc_shape=(TMF, D),
                  extras=[(pcat, *_full_spec(OGP // D)), (pcat, *_full_spec(OGG // D)), (ylin, *_rowblk), (ygla, *_rowblk),
                          (pscale, *_vec)],
                  outs=[((T, D), BF16, *_rowblk)] * 4 + [((1, D), F32, *_vec)], epi=epi)


def mm_dog(dygla, wgo, o, pcat, ng):
    def epi(acc, ex, outs, i):
        o_ref, g_ref, ng_ref = ex
        do_ref, dg_ref, gng_ref = outs
        gparts = []
        for h in range(HEADS):
            cv = slice(h * DV, (h + 1) * DV)
            oh = o_ref[:, cv].astype(F32)
            r = lax.rsqrt(jnp.mean(oh * oh, axis=-1, keepdims=True) + EPS)
            on = oh * r
            gv = g_ref[:, cv].astype(F32)
            sg = _sigmoid(gv)
            dgain = acc[:, cv] * (gv * sg)
            gparts.append(jnp.sum(dgain * on, axis=0, keepdims=True))
            ngh = ng_ref[:, cv]
            do_ref[:, cv] = _rms_bwd(on, r, dgain * ngh).astype(BF16)
            dg_ref[:, cv] = (acc[:, cv] * (on * ngh) * (sg * (1.0 + gv * (1.0 - sg)))).astype(BF16)
        _row_acc(gng_ref, jnp.concatenate(gparts, axis=1), i)

    return matmul("mm_dog", dygla, wgo, a_spec=_rowblk, b_spec=((D, D), lambda j, i, k: (0, 0)), cdims=NT,
                  grid=(1, T // TMF, 1), acc_shape=(TMF, D),
                  extras=[(o, *_rowblk), (pcat, *_full_spec(OG // D)), (ng, *_vec)],
                  outs=[((T, D), BF16, *_rowblk), ((T, D), BF16, *_rowblk), ((1, D), F32, *_vec)], epi=epi)


def mm_dh1(dpcat, wcat, x, dx2, g1):
    tk = 1280

    def epi(acc, ex, outs, i):
        x_ref, dx2_ref, g_ref = ex
        xv = x_ref[...]
        r = lax.rsqrt(jnp.mean(xv * xv, axis=-1, keepdims=True) + EPS)
        xn = xv * r
        _row_acc(outs[1], jnp.sum(acc * xn, axis=0, keepdims=True), i)
        outs[0][...] = dx2_ref[...] + _rms_bwd(xn, r, acc * g_ref[...])

    return matmul("mm_dh1", dpcat, wcat, a_spec=((TMF, tk), lambda j, i, k: (i, k)), b_spec=((D, tk), lambda j, i, k: (0, k)),
                  cdims=NT, grid=(1, T // TMF, NCAT // tk), acc_shape=(TMF, D),
                  extras=[(x, *_rowblk), (dx2, *_rowblk), (g1, *_vec)],
                  outs=[((T, D), F32, *_rowblk), ((1, D), F32, *_vec)], epi=epi)


def _tile_rows(rows, cols, n_arrays):
    tm = rows
    while tm % 32 == 0 and 2 * n_arrays * tm * cols * 4 > 24 * 1024 * 1024:
        tm //= 2
    return tm


def add_pairs(name, a, b):
    _, r, c = a.shape
    tm = _tile_rows(r, c, 3)

    def body(a_ref, b_ref, o_ref):
        o_ref[...] = (a_ref[...].astype(F32) + b_ref[...].astype(F32)).astype(BF16)

    spec = pl.BlockSpec((None, tm, c), lambda j, i: (j, i, 0))
    return pl.pallas_call(body, name=name, grid=(4, r // tm), in_specs=[spec, spec], out_specs=spec,
                          out_shape=SDS(a.shape, BF16), compiler_params=_cparams(40 * 1024 * 1024, ("arbitrary", "arbitrary")))(a, b)


def sum_chips(name, a):
    _, r, c = a.shape
    tm = _tile_rows(r, c, 4)

    def body(a_ref, o_ref):
        s = a_ref[0].astype(F32)
        for t in range(1, NCHIP):
            s = s + a_ref[t].astype(F32)
        o_ref[...] = s

    return pl.pallas_call(body, name=name, grid=(r // tm,), in_specs=[pl.BlockSpec((4, tm, c), lambda i: (0, i, 0))],
                          out_specs=pl.BlockSpec((tm, c), lambda i: (i, 0)), out_shape=SDS((r, c), F32),
                          compiler_params=_cparams(40 * 1024 * 1024, ("arbitrary",)))(a)


def adamw(name, w, g, m, v):
    r, c = w.shape
    tm = _tile_rows(r, c, 8) if r % 32 == 0 else r

    def body(w_ref, g_ref, m_ref, v_ref, go_ref, d_ref, mo_ref, vo_ref):
        gv = g_ref[...]
        mn = ADAM_B1 * m_ref[...] + (1.0 - ADAM_B1) * gv
        vn = ADAM_B2 * v_ref[...] + (1.0 - ADAM_B2) * (gv * gv)
        mh = mn / (1.0 - ADAM_B1 ** ADAM_STEP)
        vh = vn / (1.0 - ADAM_B2 ** ADAM_STEP)
        go_ref[...] = gv
        d_ref[...] = -ADAM_LR * (mh / (jnp.sqrt(vh) + ADAM_EPS) + ADAM_WD * w_ref[...])
        mo_ref[...] = mn
        vo_ref[...] = vn

    spec = pl.BlockSpec((tm, c), lambda i: (i, 0))
    return pl.pallas_call(body, name=name, grid=(r // tm,), in_specs=[spec] * 4, out_specs=[spec] * 4,
                          out_shape=[SDS((r, c), F32)] * 4, compiler_params=_cparams(48 * 1024 * 1024, ("arbitrary",)))(w, g, m, v)


def _place():
    x, y, c = lax.axis_index("x"), lax.axis_index("y"), lax.axis_index("c")
    chips = [(1 - x, y), (x, 1 - y), (1 - x, 1 - y)]
    return x, y, c, chips


def gather_weights(shards):
    n = len(shards)

    def body(*refs):
        src, dst = refs[:n], refs[n:2 * n]
        send, recv, lsem = refs[2 * n:]
        x, y, c, chips = _place()
        me = 2 * x + y
        sib = (x, y, 1 - c)
        started = []
        for a in range(n):
            loc = pltpu.make_async_copy(src[a], dst[a].at[me], lsem.at[a])
            loc.start()
            started.append(loc)
        sends = []
        for a in range(n):
            for j, (cx, cy) in enumerate(chips):
                cp = pltpu.make_async_remote_copy(src[a].at[c], dst[a].at[me, c], send.at[6 * a + j], recv.at[6 * a + j],
                                                  device_id=(cx, cy, c), device_id_type=MESH)
                cp.start()
                sends.append(cp)
        for a in range(n):
            for j, (cx, cy) in enumerate(chips):
                blk = dst[a].at[2 * cx + cy, c]
                pltpu.make_async_remote_copy(blk, blk, send.at[6 * a + j], recv.at[6 * a + j],
                                             device_id=(cx, cy, c), device_id_type=MESH).wait_recv()
                fw = pltpu.make_async_remote_copy(blk, blk, send.at[6 * a + 3 + j], recv.at[6 * a + 3 + j],
                                                  device_id=sib, device_id_type=MESH)
                fw.start()
                sends.append(fw)
        for a in range(n):
            for j, (cx, cy) in enumerate(chips):
                blk = dst[a].at[2 * cx + cy, 1 - c]
                pltpu.make_async_remote_copy(blk, blk, send.at[6 * a + 3 + j], recv.at[6 * a + 3 + j],
                                             device_id=sib, device_id_type=MESH).wait_recv()
        for cp in sends:
            cp.wait_send()
        for loc in started:
            loc.wait()

    return pl.pallas_call(
        body, name="gather_weights", in_specs=[ANY] * n, out_specs=[ANY] * n,
        out_shape=[SDS((NCHIP,) + s.shape, s.dtype) for s in shards],
        scratch_shapes=[pltpu.SemaphoreType.DMA((6 * n,)), pltpu.SemaphoreType.DMA((6 * n,)), pltpu.SemaphoreType.DMA((n,))],
    )(*shards)


def exchange_halves(parts):
    n = len(parts)

    def body(*refs):
        src, own, got = refs[:n], refs[n:2 * n], refs[2 * n:3 * n]
        send, recv, lsem = refs[3 * n:]
        x, y, c, _ = _place()
        sib = (x, y, 1 - c)
        cps, locs = [], []
        for a in range(n):
            cp = pltpu.make_async_remote_copy(src[a].at[1 - c], got[a], send.at[a], recv.at[a], device_id=sib, device_id_type=MESH)
            cp.start()
            cps.append(cp)
            loc = pltpu.make_async_copy(src[a].at[c], own[a], lsem.at[a])
            loc.start()
            locs.append(loc)
        for cp in cps:
            cp.wait()
        for loc in locs:
            loc.wait()

    shp = [SDS(p.shape[1:], p.dtype) for p in parts]
    outs = pl.pallas_call(
        body, name="exchange_halves", in_specs=[ANY] * n, out_specs=[ANY] * (2 * n), out_shape=shp + shp,
        scratch_shapes=[pltpu.SemaphoreType.DMA((n,)), pltpu.SemaphoreType.DMA((n,)), pltpu.SemaphoreType.DMA((n,))],
    )(*parts)
    return outs[:n], outs[n:]


def scatter_chips(parts):
    n = len(parts)

    def body(*refs):
        src, dst = refs[:n], refs[n:2 * n]
        send, recv, lsem = refs[2 * n:]
        x, y, c, chips = _place()
        me = 2 * x + y
        cps, locs = [], []
        for a in range(n):
            loc = pltpu.make_async_copy(src[a].at[me], dst[a].at[me], lsem.at[a])
            loc.start()
            locs.append(loc)
            for j, (cx, cy) in enumerate(chips):
                cp = pltpu.make_async_remote_copy(src[a].at[2 * cx + cy], dst[a].at[me], send.at[3 * a + j], recv.at[3 * a + j],
                                                  device_id=(cx, cy, c), device_id_type=MESH)
                cp.start()
                cps.append(cp)
        for a in range(n):
            for j, (cx, cy) in enumerate(chips):
                blk = dst[a].at[2 * cx + cy]
                pltpu.make_async_remote_copy(blk, blk, send.at[3 * a + j], recv.at[3 * a + j],
                                             device_id=(cx, cy, c), device_id_type=MESH).wait_recv()
        for cp in cps:
            cp.wait_send()
        for loc in locs:
            loc.wait()

    return pl.pallas_call(
        body, name="scatter_chips", in_specs=[ANY] * n, out_specs=[ANY] * n, out_shape=[SDS(p.shape, p.dtype) for p in parts],
        scratch_shapes=[pltpu.SemaphoreType.DMA((3 * n,)), pltpu.SemaphoreType.DMA((3 * n,)), pltpu.SemaphoreType.DMA((n,))],
    )(*parts)


def join_halves(halves):
    n = len(halves)

    def body(*refs):
        src, dst = refs[:n], refs[n:2 * n]
        send, recv, lsem = refs[2 * n:]
        x, y, c, _ = _place()
        sib = (x, y, 1 - c)
        cps, locs = [], []
        for a in range(n):
            cp = pltpu.make_async_remote_copy(src[a], dst[a].at[c], send.at[a], recv.at[a], device_id=sib, device_id_type=MESH)
            cp.start()
            cps.append(cp)
            loc = pltpu.make_async_copy(src[a], dst[a].at[c], lsem.at[a])
            loc.start()
            locs.append(loc)
        for a in range(n):
            blk = dst[a].at[1 - c]
            pltpu.make_async_remote_copy(blk, blk, send.at[a], recv.at[a], device_id=sib, device_id_type=MESH).wait_recv()
        for cp in cps:
            cp.wait_send()
        for loc in locs:
            loc.wait()

    return pl.pallas_call(
        body, name="join_halves", in_specs=[ANY] * n, out_specs=[ANY] * n,
        out_shape=[SDS((2,) + h.shape, h.dtype) for h in halves],
        scratch_shapes=[pltpu.SemaphoreType.DMA((n,)), pltpu.SemaphoreType.DMA((n,)), pltpu.SemaphoreType.DMA((n,))],
    )(*halves)


def gather_small(name, xs, reduce):
    m, ncol = xs.shape

    def body(x_ref, out_ref, all_ref, send, recv, lsem):
        x, y, c, chips = _place()
        me, sib = (x, y, c), (x, y, 1 - c)

        def rows(px, py, pc):
            return all_ref.at[pl.ds((4 * px + 2 * py + pc) * m, m), :]

        def copy(k, block, to, src=None):
            return pltpu.make_async_remote_copy(rows(*block) if src is None else src, rows(*block), send.at[k], recv.at[k],
                                                device_id=to, device_id_type=MESH)

        mine = pltpu.make_async_copy(x_ref, rows(*me), lsem)
        mine.start()
        first = [copy(0, me, sib, src=x_ref)] + [copy(1 + j, me, (*chip, c), src=x_ref) for j, chip in enumerate(chips)]
        for cp in first:
            cp.start()
        passed = [copy(4 + j, (*chip, c), sib) for j, chip in enumerate(chips)]
        for j, chip in enumerate(chips):
            copy(1 + j, (*chip, c), me).wait_recv()
            passed[j].start()
        copy(0, sib, me).wait_recv()
        for j, chip in enumerate(chips):
            copy(4 + j, (*chip, 1 - c), me).wait_recv()
        for cp in first + passed:
            cp.wait_send()
        mine.wait()
        if reduce:
            s = all_ref[0:m, :]
            for dev in range(1, 8):
                s = s + all_ref[dev * m:(dev + 1) * m, :]
            out_ref[...] = s
        else:
            out_ref[...] = all_ref[...]

    vm = pl.BlockSpec(memory_space=pltpu.VMEM)
    return pl.pallas_call(
        body, name=name, in_specs=[vm], out_specs=vm, out_shape=SDS((m, ncol) if reduce else (8 * m, ncol), F32),
        scratch_shapes=[pltpu.VMEM((8 * m, ncol), F32), pltpu.SemaphoreType.DMA((7,)), pltpu.SemaphoreType.DMA((7,)),
                        pltpu.SemaphoreType.DMA],
    )(xs)


def _to_cat(nat):
    pad = jnp.zeros(nat.shape[:-1] + (NCAT - OA - 16,), nat.dtype)
    return jnp.concatenate([nat[..., 3072:7168], nat[..., 7184:11280], nat[..., 0:3072], nat[..., 7168:7184], pad], axis=-1)


def _from_cat(cat):
    return jnp.concatenate([cat[..., OU:OA], cat[..., OV:OGP], cat[..., OA:OA + 16], cat[..., OGP:OU]], axis=-1)


def _pad_rows(a, rows):
    return jnp.concatenate([a, jnp.zeros((rows - a.shape[0],) + a.shape[1:], a.dtype)], axis=0)


def local_step(x2d, tgt, gf, g1, wcat, pw, pool_scale, wa_pad, b_alpha, ng, w_go, w_o, g2, w_up, w_dn):
    h1 = norm1(x2d, g1)
    pcat = mm_in(h1, wcat)
    dpool, ylin = pool_fwd(pcat, pw)
    og, o, states = gla_fwd(pcat, wa_pad, b_alpha, ng)
    mixed, ygla = mm_gla_out(og, w_go, ylin, pcat, pool_scale)
    x2, h2 = mm_out(mixed, w_o, x2d, g2)
    rup, act = mm_up(h2, w_up)
    dx3, dx3b, g_nf, loss_row = mm_down(act, w_dn, x2, tgt, gf)

    dup = mm_dact(dx3b, w_dn, rup)
    gw_down = mm_wgrad("mm_dw_down", act, dx3b, DFF, D, (2, NCHIP, D // 2, D), (None, None, 512, D),
                       lambda j, i, k: ((i // 2) % 2, i // 4, i % 2, 0), 512, D)
    dx2, dx2b, g_mlp = mm_dh2(dup, w_up, x2, dx3, g2)
    gw_up = mm_wgrad("mm_dw_up", h2, dup, D, DFF, (2, NCHIP, D // 2, D), (None, None, 512, D),
                     lambda j, i, k: (i // 2, j, i % 2, 0), 512, D)
    dylin, dygla, dlgp, dlgg, g_ps = mm_dmixed(dx2b, w_o, pcat, ylin, ygla, pool_scale)
    gw_out = mm_wgrad("mm_dw_out", mixed, dx2b, D, D, (2, NCHIP, 256, D), (None, None, 256, D),
                      lambda j, i, k: (i % 2, i // 2, 0, 0), 256, D)
    do, dg, g_ng = mm_dog(dygla, w_go, o, pcat, ng)
    gw_go = mm_wgrad("mm_dw_gla_out", og, dygla, D, D, (2, NCHIP, 256, D), (None, None, 256, D),
                     lambda j, i, k: (i % 2, i // 2, 0, 0), 256, D)
    dq, dk, dv, dalow, g_wa, g_ba = gla_bwd(do, pcat, states, wa_pad, b_alpha)
    du, dpw = pool_bwd(dylin, dpool, pw)
    dpcat = jnp.concatenate([dv, dg, dlgp, dlgg, du, dq, dk, dalow, jnp.zeros((T, NCAT - OA - APAD), BF16)], axis=1)
    gw_cat = mm_wgrad("mm_dw_in", h1, dpcat, D, NCAT, (D, NCAT), (512, 1280), lambda j, i, k: (i, j), 512, 1280)
    grad_x, g_mix = mm_dh1(dpcat, wcat, x2d, dx2, g1)
    return (loss_row[0, 0], grad_x, gw_cat, dpw, gw_go, gw_out, gw_up, gw_down, g_mix, g_ps, g_mlp, g_nf, g_ng, g_ba, g_wa)


def kernel(x, norm_mix_g, w_in, pool_w, pool_scale, w_alpha, b_alpha, gla_norm_g, w_gla_out, w_out, norm_mlp_g, w_mlp_up, w_mlp_down, norm_final_g, loss_target, m_norm_mix_g, m_w_in, m_pool_w, m_pool_scale, m_w_alpha, m_b_alpha, m_gla_norm_g, m_w_gla_out, m_w_out, m_norm_mlp_g, m_w_mlp_up, m_w_mlp_down, m_norm_final_g, v_norm_mix_g, v_w_in, v_pool_w, v_pool_scale, v_w_alpha, v_b_alpha, v_gla_norm_g, v_w_gla_out, v_w_out, v_norm_mlp_g, v_w_mlp_up, v_w_mlp_down, v_norm_final_g):
    chip = 2 * lax.axis_index("x") + lax.axis_index("y")
    x2d = x.reshape(T, D)
    tgt = loss_target.reshape(T, D)
    gf = norm_final_g.reshape(1, D)

    def halves(w2d):
        r, c = w2d.shape
        return w2d.astype(BF16).reshape(2, r // 2, c)

    pool_shard = pool_w.reshape(4 * PG, PO // NCHIP)
    big = [w_in[0], w_gla_out[0], w_out[0], w_mlp_up[0], w_mlp_down[0], pool_shard]
    g_in, g_go, g_out, g_up, g_down, g_pool = gather_weights([halves(w) for w in big])
    wcat = _to_cat(jnp.concatenate([g_in[j].reshape(D, IN_SHARD) for j in range(NCHIP)], axis=1))
    w_go = g_go.reshape(D, D)
    w_o = g_out.reshape(D, D)
    w_up = g_up.reshape(NCHIP, D, D)
    w_dn = g_down.reshape(DFF, D)
    pw = jnp.concatenate([g_pool[j].reshape(4, PG, PO // NCHIP) for j in range(NCHIP)], axis=2)

    small_w = jnp.concatenate([w_alpha[0].reshape(4, QK), gla_norm_g[0].reshape(1, 512),
                               jnp.zeros((1, 512), F32)], axis=None).reshape(5, QK)
    small_w = _pad_rows(small_w, 8)
    sw_all = gather_small("gather_small_w", small_w, False).reshape(8, 8, QK)
    wa_full = jnp.concatenate([sw_all[2 * j, 0:4].reshape(16, DK) for j in range(NCHIP)], axis=1)
    ng_full = jnp.concatenate([sw_all[2 * j, 4, 0:512].reshape(HEADS, DV // NCHIP) for j in range(NCHIP)], axis=1)
    wa_pad = _pad_rows(wa_full, APAD).astype(BF16)
    ng = ng_full.reshape(1, D)

    (loss_local, grad_x, gw_cat, dpw, gw_go, gw_out, gw_up, gw_down, g_mix, g_ps, g_mlp, g_nf, g_ng, g_ba, g_wa) = local_step(
        x2d, tgt, gf, norm_mix_g, wcat, pw, pool_scale, wa_pad, b_alpha, ng, w_go, w_o, norm_mlp_g, w_up, w_dn)
    loss = lax.psum(loss_local, ("x", "y", "c"))

    gw_in_nat = _from_cat(gw_cat)
    gw_in = jnp.stack([gw_in_nat[:, j * IN_SHARD:(j + 1) * IN_SHARD].reshape(2, D // 2, IN_SHARD)
                       for j in range(NCHIP)], axis=1)
    gw_pool = jnp.stack([dpw[:, :, j * 128:(j + 1) * 128].reshape(2, 2 * PG, 128) for j in range(NCHIP)], axis=1)

    parts = [gw_in, gw_go, gw_out, gw_up, gw_down, gw_pool]
    own, theirs = exchange_halves(parts)
    names = ["in", "gla_out", "out", "up", "down", "pool"]
    chip_sums = [add_pairs("add_pair_" + nm, a, b) for nm, a, b in zip(names, own, theirs)]
    landed = scatter_chips(chip_sums)
    reduced = [sum_chips("sum_chips_" + nm, a) for nm, a in zip(names, landed)]
    full = join_halves(reduced)

    def step(nm, w, g, m, v):
        return adamw("adamw_" + nm, w.reshape(g.shape), g, m.reshape(g.shape), v.reshape(g.shape))

    big_m = [m_w_in, m_w_gla_out, m_w_out, m_w_mlp_up, m_w_mlp_down, m_pool_w]
    big_v = [v_w_in, v_w_gla_out, v_w_out, v_w_mlp_up, v_w_mlp_down, v_pool_w]
    big_res = {}
    for nm, w, g, m, v in zip(names, big, full, big_m, big_v):
        big_res[nm] = step(nm, w, g.reshape(w.shape), m, v)

    ROWS = 16
    packed = jnp.concatenate([g_mix, g_ps, g_mlp, g_nf, g_ng, jnp.concatenate([g_ba, jnp.zeros((1, QK), F32)], axis=1),
                              g_wa[0:16].reshape(8, D), jnp.zeros((2, D), F32)], axis=0)
    tot = gather_small("reduce_small_g", packed, True)
    t_wa = lax.dynamic_slice(tot[6:14].reshape(16, QK), (0, chip * DK), (16, DK))
    t_ng = lax.dynamic_slice(tot[4].reshape(HEADS, DV), (0, chip * (DV // NCHIP)), (HEADS, DV // NCHIP))

    def pack_small(mix, ps, mlp, nf, ba, wa, gn):
        return jnp.concatenate([mix.reshape(1, D), ps.reshape(1, D), mlp.reshape(1, D), nf.reshape(1, D),
                                jnp.concatenate([ba.reshape(1, QK), jnp.zeros((1, QK), F32)], axis=1),
                                wa.reshape(2, D), jnp.concatenate([gn.reshape(1, 512), jnp.zeros((1, D - 512), F32)], axis=1),
                                jnp.zeros((ROWS - 8, D), F32)], axis=0)

    sg = pack_small(tot[0], tot[1], tot[2], tot[3], tot[5, 0:QK], t_wa, t_ng)
    sw = pack_small(norm_mix_g, pool_scale, norm_mlp_g, norm_final_g, b_alpha, w_alpha, gla_norm_g)
    sm = pack_small(m_norm_mix_g, m_pool_scale, m_norm_mlp_g, m_norm_final_g, m_b_alpha, m_w_alpha, m_gla_norm_g)
    sv = pack_small(v_norm_mix_g, v_pool_scale, v_norm_mlp_g, v_norm_final_g, v_b_alpha, v_w_alpha, v_gla_norm_g)
    small_res = adamw("adamw_small", sw, sg, sm, sv)

    def unpack(p):
        return {"norm_mix_g": p[0].reshape(1, D), "pool_scale": p[1].reshape(1, D), "norm_mlp_g": p[2].reshape(1, D),
                "norm_final_g": p[3].reshape(D), "b_alpha": p[4, 0:QK].reshape(1, QK), "w_alpha": p[5:7].reshape(1, 16, DK),
                "gla_norm_g": p[7, 0:512].reshape(1, HEADS, DV // NCHIP)}

    order = ["norm_mix_g", "w_in", "pool_w", "pool_scale", "w_alpha", "b_alpha", "gla_norm_g", "w_gla_out", "w_out",
             "norm_mlp_g", "w_mlp_up", "w_mlp_down", "norm_final_g"]
    big_key = {"w_in": ("in", w_in.shape), "pool_w": ("pool", pool_w.shape), "w_gla_out": ("gla_out", w_gla_out.shape),
               "w_out": ("out", w_out.shape), "w_mlp_up": ("up", w_mlp_up.shape), "w_mlp_down": ("down", w_mlp_down.shape)}
    result = [loss, grad_x.reshape(1, T, D)]
    for kind in range(4):
        small = unpack(small_res[kind])
        for nm in order:
            if nm in big_key:
                key, shp = big_key[nm]
                result.append(big_res[key][kind].reshape(shp))
            else:
                result.append(small[nm])
    return tuple(result)
```

```python
import itertools

import jax
import jax.numpy as jnp
from jax import lax
from jax.experimental import pallas as pl
from jax.experimental.pallas import tpu as pltpu

F32 = jnp.float32
BF16 = jnp.bfloat16
SDS = jax.ShapeDtypeStruct
MESH = pl.DeviceIdType.MESH
ANY = pl.BlockSpec(memory_space=pl.ANY)

T = 2048
D = 2048
DFF = 8192
NCHIP = 4
IN_WIDTH = 11280
IN_SHARD = IN_WIDTH // NCHIP
CHUNK = 64
NCHUNK = T // CHUNK
HEADS = 4
DK = 256
DV = 512
QK = HEADS * DK
EPS = 1e-6
POOL_WINDOWS = (2, 4, 8, 16)
PG = 256
PO = 512

OV, OG, OGP, OGG, OU, OQ, OKK, OA = 0, 2048, 4096, 6144, 8192, 9216, 10240, 11264
NCAT = 11520
APAD = 128

VMEM_CAP = 56 * 1024 * 1024

PIECE_BYTES = 3 * 512 * 1024

ADAM_LR, ADAM_B1, ADAM_B2, ADAM_EPS, ADAM_WD, ADAM_STEP = 0.001, 0.9, 0.999, 1e-08, 0.01, 10


def _cparams(vmem_bytes=None, sem=None):
    kw = {}
    if vmem_bytes is not None:
        kw["vmem_limit_bytes"] = int(min(max(vmem_bytes, 32 * 1024 * 1024), VMEM_CAP))
    if sem is not None:
        kw["dimension_semantics"] = sem
    return pltpu.CompilerParams(**kw)


def _nbytes(shape, dtype):
    n = 1
    for s in shape:
        if s is not None:
            n *= s
    return n * jnp.dtype(dtype).itemsize


def _sigmoid(x):
    return 1.0 / (1.0 + jnp.exp(-x))


def matmul(name, a, b, *, a_spec, b_spec, cdims, grid, acc_shape, outs, extras=(), epi):
    nj, ni, nk = grid
    ne, no = len(extras), len(outs)

    def body(*refs):
        a_ref, b_ref = refs[0], refs[1]
        ex = refs[2:2 + ne]
        out_refs = refs[2 + ne:2 + ne + no]
        i = pl.program_id(1)
        part = lax.dot_general(a_ref[...], b_ref[...], (cdims, ((), ())), preferred_element_type=F32)
        if nk == 1:
            epi(part, ex, out_refs, i)
        else:
            acc_ref = refs[2 + ne + no]
            k = pl.program_id(2)

            @pl.when(k == 0)
            def _():
                acc_ref[...] = part

            @pl.when(k > 0)
            def _():
                acc_ref[...] += part

            @pl.when(k == nk - 1)
            def _():
                epi(acc_ref[...], ex, out_refs, i)

    in_specs = [pl.BlockSpec(*a_spec), pl.BlockSpec(*b_spec)] + [pl.BlockSpec(bs, im) for _, bs, im in extras]
    out_specs = [pl.BlockSpec(bs, im) for _, _, bs, im in outs]
    out_shape = [SDS(s, dt) for s, dt, _, _ in outs]
    vm = 2 * (_nbytes(a_spec[0], a.dtype) + _nbytes(b_spec[0], b.dtype))
    vm += 2 * sum(_nbytes(bs, arr.dtype) for arr, bs, _ in extras)
    vm += 2 * sum(_nbytes(bs, dt) for _, dt, bs, _ in outs)
    vm += 6 * _nbytes(acc_shape, F32)
    scratch = [pltpu.VMEM(acc_shape, F32)] if nk > 1 else []
    return pl.pallas_call(
        body, name=name, grid=grid, in_specs=in_specs, out_specs=out_specs, out_shape=out_shape,
        scratch_shapes=scratch,
        compiler_params=_cparams(vm, ("arbitrary", "arbitrary", "arbitrary")),
    )(a, b, *[arr for arr, _, _ in extras])


NN = ((1,), (0,))
NT = ((1,), (1,))
TN = ((0,), (0,))


def _row_acc(out_ref, val, i):
    @pl.when(i == 0)
    def _():
        out_ref[...] = val

    @pl.when(i > 0)
    def _():
        out_ref[...] += val


def _rms_bwd(xn, r, dxn):
    return r * (dxn - xn * jnp.mean(dxn * xn, axis=-1, keepdims=True))


def norm1(x, g):
    tm = 256

    def body(x_ref, g_ref, h_ref):
        xv = x_ref[...]
        r = lax.rsqrt(jnp.mean(xv * xv, axis=-1, keepdims=True) + EPS)
        h_ref[...] = (xv * r * g_ref[...]).astype(BF16)

    return pl.pallas_call(
        body, name="norm1", grid=(T // tm,),
        in_specs=[pl.BlockSpec((tm, D), lambda i: (i, 0)), pl.BlockSpec((1, D), lambda i: (0, 0))],
        out_specs=pl.BlockSpec((tm, D), lambda i: (i, 0)), out_shape=SDS((T, D), BF16),
        compiler_params=_cparams(32 * 1024 * 1024, ("arbitrary",)),
    )(x, g)


def mm_in(h1, wcat):
    tm, tn = 512, 1280

    def epi(acc, ex, outs, i):
        outs[0][...] = acc.astype(BF16)

    return matmul("mm_in", h1, wcat, a_spec=((tm, D), lambda j, i, k: (i, 0)), b_spec=((D, tn), lambda j, i, k: (0, j)),
                  cdims=NN, grid=(NCAT // tn, T // tm, 1), acc_shape=(tm, tn),
                  outs=[((T, NCAT), BF16, (tm, tn), lambda j, i, k: (i, j))], epi=epi)[0]


def _window_sum(x, w, up):
    n = x.shape[0]
    row = lax.broadcasted_iota(jnp.int32, x.shape, 0)
    s, sh = x, 1
    while sh < w:
        if up:
            s = s + jnp.where(row < n - sh, pltpu.roll(s, n - sh, axis=0), 0.0)
        else:
            s = s + jnp.where(row >= sh, pltpu.roll(s, sh, axis=0), 0.0)
        sh *= 2
    return s


def _inv_count(shape, w):
    row = lax.broadcasted_iota(jnp.int32, shape, 0)
    return 1.0 / jnp.minimum(row + 1, w).astype(F32)


def pool_fwd(pcat, pw):
    def body(u_ref, pw_ref, d_ref, y_ref):
        for gi, w in enumerate(POOL_WINDOWS):
            ug = u_ref[:, gi * PG:(gi + 1) * PG].astype(F32)
            dg = _window_sum(ug, w, False) * _inv_count(ug.shape, w) - ug
            db = dg.astype(BF16)
            d_ref[:, gi * PG:(gi + 1) * PG] = db
            y_ref[:, gi * PO:(gi + 1) * PO] = jnp.dot(db, pw_ref[gi], preferred_element_type=F32).astype(BF16)

    return pl.pallas_call(
        body, name="pool_fwd", grid=(1,),
        in_specs=[pl.BlockSpec((T, 4 * PG), lambda i: (0, OU // (4 * PG))), pl.BlockSpec((4, PG, PO), lambda i: (0, 0, 0))],
        out_specs=[pl.BlockSpec((T, 4 * PG), lambda i: (0, 0)), pl.BlockSpec((T, D), lambda i: (0, 0))],
        out_shape=[SDS((T, 4 * PG), BF16), SDS((T, D), BF16)],
        compiler_params=_cparams(48 * 1024 * 1024, ("arbitrary",)),
    )(pcat, pw)


def pool_bwd(dylin, d, pw):
    def body(dy_ref, d_ref, pw_ref, du_ref, dpw_ref):
        for gi, w in enumerate(POOL_WINDOWS):
            dyl = dy_ref[:, gi * PO:(gi + 1) * PO]
            dd = lax.dot_general(dyl, pw_ref[gi], (NT, ((), ())), preferred_element_type=F32)
            du = _window_sum(dd * _inv_count(dd.shape, w), w, True) - dd
            du_ref[:, gi * PG:(gi + 1) * PG] = du.astype(BF16)
            dpw_ref[gi] = lax.dot_general(d_ref[:, gi * PG:(gi + 1) * PG], dyl, (TN, ((), ())),
                                          preferred_element_type=F32).astype(BF16)

    return pl.pallas_call(
        body, name="pool_bwd", grid=(1,),
        in_specs=[pl.BlockSpec((T, D), lambda i: (0, 0)), pl.BlockSpec((T, 4 * PG), lambda i: (0, 0)),
                  pl.BlockSpec((4, PG, PO), lambda i: (0, 0, 0))],
        out_specs=[pl.BlockSpec((T, 4 * PG), lambda i: (0, 0)), pl.BlockSpec((4, PG, PO), lambda i: (0, 0, 0))],
        out_shape=[SDS((T, 4 * PG), BF16), SDS((4, PG, PO), BF16)],
        compiler_params=_cparams(48 * 1024 * 1024, ("arbitrary",)),
    )(dylin, d, pw)


def _gate_decay(alow, wa, ba):
    a = jnp.dot(alow, wa, preferred_element_type=F32) + ba
    ls = jax.nn.log_sigmoid(a) * (1.0 / 16.0)
    r = lax.broadcasted_iota(jnp.int32, (CHUNK, CHUNK), 0)
    c = lax.broadcasted_iota(jnp.int32, (CHUNK, CHUNK), 1)
    tri = jnp.where(c <= r, 1.0, 0.0).astype(F32)
    cum = jnp.dot(tri, ls, preferred_element_type=F32, precision=lax.Precision.HIGHEST)
    last = cum[CHUNK - 1:CHUNK, :]
    return a, jnp.exp(last - cum), jnp.exp(last)


def gla_fwd(pcat, wa, ba, ng):
    def body(q_ref, k_ref, v_ref, g_ref, al_ref, wa_ref, ba_ref, ng_ref, og_ref, o_ref, st_ref, s_scr):
        @pl.when(pl.program_id(0) == 0)
        def _():
            s_scr[...] = jnp.zeros_like(s_scr)

        _, e, decay = _gate_decay(al_ref[...], wa_ref[...], ba_ref[...])
        kd = (k_ref[...].astype(F32) * e).astype(BF16)
        qs = (q_ref[...].astype(F32) * (DK ** -0.5)).astype(BF16)
        for h in range(HEADS):
            ck = slice(h * DK, (h + 1) * DK)
            cv = slice(h * DV, (h + 1) * DV)
            s_new = s_scr[h] * decay[:, ck] + lax.dot_general(v_ref[:, cv], kd[:, ck], (TN, ((), ())),
                                                               preferred_element_type=F32)
            s_scr[h] = s_new
            sb = s_new.astype(BF16)
            st_ref[h] = sb
            oh = lax.dot_general(qs[:, ck], sb, (NT, ((), ())), preferred_element_type=F32)
            o_ref[:, cv] = oh.astype(BF16)
            on = oh * lax.rsqrt(jnp.mean(oh * oh, axis=-1, keepdims=True) + EPS) * ng_ref[:, cv]
            gv = g_ref[:, cv].astype(F32)
            og_ref[:, cv] = (on * (gv * _sigmoid(gv))).astype(BF16)

    row = lambda c: (c, 0)
    return pl.pallas_call(
        body, name="gla_fwd", grid=(NCHUNK,),
        in_specs=[pl.BlockSpec((CHUNK, QK), lambda c: (c, OQ // QK)), pl.BlockSpec((CHUNK, QK), lambda c: (c, OKK // QK)),
                  pl.BlockSpec((CHUNK, D), lambda c: (c, OV // D)), pl.BlockSpec((CHUNK, D), lambda c: (c, OG // D)),
                  pl.BlockSpec((CHUNK, APAD), lambda c: (c, OA // APAD)),
                  pl.BlockSpec((APAD, QK), lambda c: (0, 0)), pl.BlockSpec((1, QK), lambda c: (0, 0)),
                  pl.BlockSpec((1, D), lambda c: (0, 0))],
        out_specs=[pl.BlockSpec((CHUNK, D), row), pl.BlockSpec((CHUNK, D), row),
                   pl.BlockSpec((None, HEADS, DV, DK), lambda c: (c, 0, 0, 0))],
        out_shape=[SDS((T, D), BF16), SDS((T, D), BF16), SDS((NCHUNK, HEADS, DV, DK), BF16)],
        scratch_shapes=[pltpu.VMEM((HEADS, DV, DK), F32)],
        compiler_params=_cparams(32 * 1024 * 1024, ("arbitrary",)),
    )(pcat, pcat, pcat, pcat, pcat, wa, ba, ng)


def gla_bwd(do, pcat, states, wa, ba):
    def body(do_ref, q_ref, k_ref, v_ref, al_ref, sc_ref, sp_ref, wa_ref, ba_ref,
             dq_ref, dk_ref, dv_ref, dal_ref, dwa_ref, dba_ref, ds_scr):
        i = pl.program_id(0)

        @pl.when(i == 0)
        def _():
            ds_scr[...] = jnp.zeros_like(ds_scr)

        has_prev = jnp.where(i < NCHUNK - 1, 1.0, 0.0).astype(F32)
        a, e, decay = _gate_decay(al_ref[...], wa_ref[...], ba_ref[...])
        kf = k_ref[...].astype(F32)
        kdf = kf * e
        kd = kdf.astype(BF16)
        qs = (q_ref[...].astype(F32) * (DK ** -0.5)).astype(BF16)
        dkd_parts, ddecay_parts = [], []
        for h in range(HEADS):
            ck = slice(h * DK, (h + 1) * DK)
            cv = slice(h * DV, (h + 1) * DV)
            doh = do_ref[:, cv]
            ds = ds_scr[h] + lax.dot_general(doh, qs[:, ck], (TN, ((), ())), preferred_element_type=F32)
            dsb = ds.astype(BF16)
            dq_ref[:, ck] = (jnp.dot(doh, sc_ref[h], preferred_element_type=F32) * (DK ** -0.5)).astype(BF16)
            dkd_parts.append(jnp.dot(v_ref[:, cv], dsb, preferred_element_type=F32))
            dv_ref[:, cv] = lax.dot_general(kd[:, ck], dsb, (NT, ((), ())), preferred_element_type=F32).astype(BF16)
            ddecay_parts.append(jnp.sum(ds * sp_ref[h].astype(F32), axis=0, keepdims=True) * has_prev)
            ds_scr[h] = ds * decay[:, ck]
        dkd = jnp.concatenate(dkd_parts, axis=1)
        ddecay = jnp.concatenate(ddecay_parts, axis=1)
        dk_ref[...] = (dkd * e).astype(BF16)
        dearg = dkd * kdf
        dlast = jnp.sum(dearg, axis=0, keepdims=True) + ddecay * decay
        r = lax.broadcasted_iota(jnp.int32, (CHUNK, CHUNK), 0)
        c = lax.broadcasted_iota(jnp.int32, (CHUNK, CHUNK), 1)
        triu = jnp.where(c >= r, 1.0, 0.0).astype(F32)
        dls = dlast - jnp.dot(triu, dearg, preferred_element_type=F32, precision=lax.Precision.HIGHEST)
        da = dls * (1.0 / 16.0) * (1.0 - _sigmoid(a))
        dab = da.astype(BF16)
        dal_ref[...] = lax.dot_general(dab, wa_ref[...], (NT, ((), ())), preferred_element_type=F32).astype(BF16)
        dwa = lax.dot_general(al_ref[...], dab, (TN, ((), ())), preferred_element_type=F32)
        dba = jnp.sum(da, axis=0, keepdims=True)

        @pl.when(i == 0)
        def _():
            dwa_ref[...] = dwa
            dba_ref[...] = dba

        @pl.when(i > 0)
        def _():
            dwa_ref[...] += dwa
            dba_ref[...] += dba

    rev = lambda i: NCHUNK - 1 - i
    return pl.pallas_call(
        body, name="gla_bwd", grid=(NCHUNK,),
        in_specs=[pl.BlockSpec((CHUNK, D), lambda i: (rev(i), 0)),
                  pl.BlockSpec((CHUNK, QK), lambda i: (rev(i), OQ // QK)), pl.BlockSpec((CHUNK, QK), lambda i: (rev(i), OKK // QK)),
                  pl.BlockSpec((CHUNK, D), lambda i: (rev(i), OV // D)), pl.BlockSpec((CHUNK, APAD), lambda i: (rev(i), OA // APAD)),
                  pl.BlockSpec((None, HEADS, DV, DK), lambda i: (rev(i), 0, 0, 0)),
                  pl.BlockSpec((None, HEADS, DV, DK), lambda i: (jnp.maximum(rev(i) - 1, 0), 0, 0, 0)),
                  pl.BlockSpec((APAD, QK), lambda i: (0, 0)), pl.BlockSpec((1, QK), lambda i: (0, 0))],
        out_specs=[pl.BlockSpec((CHUNK, QK), lambda i: (rev(i), 0)), pl.BlockSpec((CHUNK, QK), lambda i: (rev(i), 0)),
                   pl.BlockSpec((CHUNK, D), lambda i: (rev(i), 0)), pl.BlockSpec((CHUNK, APAD), lambda i: (rev(i), 0)),
                   pl.BlockSpec((APAD, QK), lambda i: (0, 0)), pl.BlockSpec((1, QK), lambda i: (0, 0))],
        out_shape=[SDS((T, QK), BF16), SDS((T, QK), BF16), SDS((T, D), BF16), SDS((T, APAD), BF16),
                   SDS((APAD, QK), F32), SDS((1, QK), F32)],
        scratch_shapes=[pltpu.VMEM((HEADS, DV, DK), F32)],
        compiler_params=_cparams(32 * 1024 * 1024, ("arbitrary",)),
    )(do, pcat, pcat, pcat, pcat, states, states, wa, ba)


TMF = 256
_rowblk = ((TMF, D), lambda j, i, k: (i, 0))
_vec = ((1, D), lambda j, i, k: (0, 0))


def _full_spec(col):
    return ((TMF, D), lambda j, i, k: (i, col))


def mm_gla_out(og, w, ylin, pcat, pscale):
    def epi(acc, ex, outs, i):
        ylin_ref, lgp_ref, lgg_ref, ps_ref = ex
        gp = _sigmoid(lgp_ref[...].astype(F32))
        gg = _sigmoid(lgg_ref[...].astype(F32))
        outs[0][...] = (gp * (ylin_ref[...].astype(F32) * ps_ref[...]) + gg * acc).astype(BF16)
        outs[1][...] = acc.astype(BF16)

    return matmul("mm_gla_out", og, w, a_spec=_rowblk, b_spec=((D, D), lambda j, i, k: (0, 0)), cdims=NN,
                  grid=(1, T // TMF, 1), acc_shape=(TMF, D),
                  extras=[(ylin, *_rowblk), (pcat, *_full_spec(OGP // D)), (pcat, *_full_spec(OGG // D)), (pscale, *_vec)],
                  outs=[((T, D), BF16, *_rowblk), ((T, D), BF16, *_rowblk)], epi=epi)


def mm_out(mixed, w, x, g2):
    def epi(acc, ex, outs, i):
        x_ref, g_ref = ex
        x2 = x_ref[...] + acc
        r = lax.rsqrt(jnp.mean(x2 * x2, axis=-1, keepdims=True) + EPS)
        outs[0][...] = x2
        outs[1][...] = (x2 * r * g_ref[...]).astype(BF16)

    return matmul("mm_out", mixed, w, a_spec=_rowblk, b_spec=((D, D), lambda j, i, k: (0, 0)), cdims=NN,
                  grid=(1, T // TMF, 1), acc_shape=(TMF, D), extras=[(x, *_rowblk), (g2, *_vec)],
                  outs=[((T, D), F32, *_rowblk), ((T, D), BF16, *_rowblk)], epi=epi)


def mm_up(h2, wup):
    def epi(acc, ex, outs, i):
        r = jnp.maximum(acc, 0.0)
        outs[0][...] = r.astype(BF16)
        outs[1][...] = (r * r).astype(BF16)

    blk = ((TMF, D), lambda j, i, k: (i, j))
    return matmul("mm_up", h2, wup, a_spec=_rowblk, b_spec=((None, D, D), lambda j, i, k: (j, 0, 0)), cdims=NN,
                  grid=(NCHIP, T // TMF, 1), acc_shape=(TMF, D),
                  outs=[((T, DFF), BF16, *blk), ((T, DFF), BF16, *blk)], epi=epi)


def mm_down(act, wdown, x2, tgt, gf):
    tk = 2048

    def epi(acc, ex, outs, i):
        x2_ref, t_ref, g_ref = ex
        dx_ref, dxb_ref, gnf_ref, loss_ref = outs
        x3 = x2_ref[...] + acc
        r = lax.rsqrt(jnp.mean(x3 * x3, axis=-1, keepdims=True) + EPS)
        xn = x3 * r
        err = xn * g_ref[...] - t_ref[...]
        lsum = 0.5 * jnp.sum(jnp.mean(err * err, axis=-1, keepdims=True), axis=0, keepdims=True)
        dy = err * (1.0 / D)
        _row_acc(gnf_ref, jnp.sum(dy * xn, axis=0, keepdims=True), i)
        _row_acc(loss_ref, jnp.broadcast_to(lsum, (1, 128)), i)
        dx3 = _rms_bwd(xn, r, dy * g_ref[...])
        dx_ref[...] = dx3
        dxb_ref[...] = dx3.astype(BF16)

    return matmul("mm_down", act, wdown, a_spec=((TMF, tk), lambda j, i, k: (i, k)), b_spec=((tk, D), lambda j, i, k: (k, 0)),
                  cdims=NN, grid=(1, T // TMF, DFF // tk), acc_shape=(TMF, D),
                  extras=[(x2, *_rowblk), (tgt, *_rowblk), (gf, *_vec)],
                  outs=[((T, D), F32, *_rowblk), ((T, D), BF16, *_rowblk), ((1, D), F32, *_vec),
                        ((1, 128), F32, (1, 128), lambda j, i, k: (0, 0))], epi=epi)


def mm_dact(dx3b, wdown, rup):
    def epi(acc, ex, outs, i):
        outs[0][...] = (acc * 2.0 * ex[0][...].astype(F32)).astype(BF16)

    blk = ((TMF, D), lambda j, i, k: (i, j))
    return matmul("mm_dact", dx3b, wdown, a_spec=_rowblk, b_spec=((D, D), lambda j, i, k: (j, 0)), cdims=NT,
                  grid=(DFF // D, T // TMF, 1), acc_shape=(TMF, D), extras=[(rup, *blk)],
                  outs=[((T, DFF), BF16, *blk)], epi=epi)[0]


def mm_wgrad(name, a, b, m, n, out_shape, out_block, out_map, tm, tn):
    def epi(acc, ex, outs, i):
        outs[0][...] = acc.astype(BF16)

    return matmul(name, a, b, a_spec=((T, tm), lambda j, i, k: (0, i)), b_spec=((T, tn), lambda j, i, k: (0, j)),
                  cdims=TN, grid=(n // tn, m // tm, 1), acc_shape=(tm, tn),
                  outs=[(out_shape, BF16, out_block, out_map)], epi=epi)[0]


def mm_dh2(dup, wup, x2, dx3, g2):
    def epi(acc, ex, outs, i):
        x2_ref, dx3_ref, g_ref = ex
        x2 = x2_ref[...]
        r = lax.rsqrt(jnp.mean(x2 * x2, axis=-1, keepdims=True) + EPS)
        xn = x2 * r
        _row_acc(outs[2], jnp.sum(acc * xn, axis=0, keepdims=True), i)
        dx2 = dx3_ref[...] + _rms_bwd(xn, r, acc * g_ref[...])
        outs[0][...] = dx2
        outs[1][...] = dx2.astype(BF16)

    return matmul("mm_dh2", dup, wup, a_spec=((TMF, D), lambda j, i, k: (i, k)),
                  b_spec=((None, D, D), lambda j, i, k: (k, 0, 0)), cdims=NT, grid=(1, T // TMF, NCHIP), acc_shape=(TMF, D),
                  extras=[(x2, *_rowblk), (dx3, *_rowblk), (g2, *_vec)],
                  outs=[((T, D), F32, *_rowblk), ((T, D), BF16, *_rowblk), ((1, D), F32, *_vec)], epi=epi)


def mm_dmixed(dx2b, wout, pcat, ylin, ygla, pscale):
    def epi(acc, ex, outs, i):
        lgp_ref, lgg_ref, ylin_ref, ygla_ref, ps_ref = ex
        gp = _sigmoid(lgp_ref[...].astype(F32))
        gg = _sigmoid(lgg_ref[...].astype(F32))
        yl = ylin_ref[...].astype(F32)
        ps = ps_ref[...]
        agp = acc * gp
        outs[0][...] = (agp * ps).astype(BF16)
        outs[1][...] = (acc * gg).astype(BF16)
        outs[2][...] = (agp * (yl * ps) * (1.0 - gp)).astype(BF16)
        outs[3][...] = (acc * ygla_ref[...].astype(F32) * gg * (1.0 - gg)).astype(BF16)
        _row_acc(outs[4], jnp.sum(agp * yl, axis=0, keepdims=True), i)

    return matmul("mm_dmixed", dx2b, wout, a_spec=_rowblk, b_spec=((D, D), lambda j, i, k: (0, 0)), cdims=NT,
                  grid=(1, T // TMF, 1), acc_shape=(TMF, D),
                  extras=[(pcat, *_full_spec(OGP // D)), (pcat, *_full_spec(OGG // D)), (ylin, *_rowblk), (ygla, *_rowblk),
                          (pscale, *_vec)],
                  outs=[((T, D), BF16, *_rowblk)] * 4 + [((1, D), F32, *_vec)], epi=epi)


def mm_dog(dygla, wgo, o, pcat, ng):
    def epi(acc, ex, outs, i):
        o_ref, g_ref, ng_ref = ex
        do_ref, dg_ref, gng_ref = outs
        gparts = []
        for h in range(HEADS):
            cv = slice(h * DV, (h + 1) * DV)
            oh = o_ref[:, cv].astype(F32)
            r = lax.rsqrt(jnp.mean(oh * oh, axis=-1, keepdims=True) + EPS)
            on = oh * r
            gv = g_ref[:, cv].astype(F32)
            sg = _sigmoid(gv)
            dgain = acc[:, cv] * (gv * sg)
            gparts.append(jnp.sum(dgain * on, axis=0, keepdims=True))
            ngh = ng_ref[:, cv]
            do_ref[:, cv] = _rms_bwd(on, r, dgain * ngh).astype(BF16)
            dg_ref[:, cv] = (acc[:, cv] * (on * ngh) * (sg * (1.0 + gv * (1.0 - sg)))).astype(BF16)
        _row_acc(gng_ref, jnp.concatenate(gparts, axis=1), i)

    return matmul("mm_dog", dygla, wgo, a_spec=_rowblk, b_spec=((D, D), lambda j, i, k: (0, 0)), cdims=NT,
                  grid=(1, T // TMF, 1), acc_shape=(TMF, D),
                  extras=[(o, *_rowblk), (pcat, *_full_spec(OG // D)), (ng, *_vec)],
                  outs=[((T, D), BF16, *_rowblk), ((T, D), BF16, *_rowblk), ((1, D), F32, *_vec)], epi=epi)


def mm_dh1(dpcat, wcat, x, dx2, g1):
    tk = 1280

    def epi(acc, ex, outs, i):
        x_ref, dx2_ref, g_ref = ex
        xv = x_ref[...]
        r = lax.rsqrt(jnp.mean(xv * xv, axis=-1, keepdims=True) + EPS)
        xn = xv * r
        _row_acc(outs[1], jnp.sum(acc * xn, axis=0, keepdims=True), i)
        outs[0][...] = dx2_ref[...] + _rms_bwd(xn, r, acc * g_ref[...])

    return matmul("mm_dh1", dpcat, wcat, a_spec=((TMF, tk), lambda j, i, k: (i, k)), b_spec=((D, tk), lambda j, i, k: (0, k)),
                  cdims=NT, grid=(1, T // TMF, NCAT // tk), acc_shape=(TMF, D),
                  extras=[(x, *_rowblk), (dx2, *_rowblk), (g1, *_vec)],
                  outs=[((T, D), F32, *_rowblk), ((1, D), F32, *_vec)], epi=epi)


def _tile_rows(rows, cols, n_arrays):
    tm = rows
    while tm % 32 == 0 and 2 * n_arrays * tm * cols * 4 > 24 * 1024 * 1024:
        tm //= 2
    return tm


def add_pairs(name, parts, theirs, core):
    _, _, r, c = parts.shape
    tm = _tile_rows(r, c, 3)

    def body(core_ref, a_ref, b_ref, o_ref):
        o_ref[...] = (a_ref[...].astype(F32) + b_ref[...].astype(F32)).astype(BF16)

    spec = pl.BlockSpec((None, tm, c), lambda j, i, core_ref: (j, i, 0))
    grid_spec = pltpu.PrefetchScalarGridSpec(
        num_scalar_prefetch=1, grid=(NCHIP, r // tm),
        in_specs=[pl.BlockSpec((None, None, tm, c), lambda j, i, core_ref: (core_ref[0], j, i, 0)), spec], out_specs=spec)
    return pl.pallas_call(body, name=name, grid_spec=grid_spec, out_shape=SDS((NCHIP, r, c), BF16),
                          compiler_params=_cparams(40 * 1024 * 1024, ("arbitrary", "arbitrary")))(core, parts, theirs)


def sum_chips(name, sums, landed, chip):
    _, r, c = sums.shape
    tm = _tile_rows(r, c, 4)

    def body(chip_ref, own_ref, l_ref, o_ref):
        s = own_ref[...].astype(F32)
        for t in range(NCHIP - 1):
            s = s + l_ref[t].astype(F32)
        o_ref[...] = s

    grid_spec = pltpu.PrefetchScalarGridSpec(
        num_scalar_prefetch=1, grid=(r // tm,),
        in_specs=[pl.BlockSpec((None, tm, c), lambda i, chip_ref: (chip_ref[0], i, 0)),
                  pl.BlockSpec((NCHIP - 1, tm, c), lambda i, chip_ref: (0, i, 0))],
        out_specs=pl.BlockSpec((tm, c), lambda i, chip_ref: (i, 0)))
    return pl.pallas_call(body, name=name, grid_spec=grid_spec, out_shape=SDS((r, c), F32),
                          compiler_params=_cparams(40 * 1024 * 1024, ("arbitrary",)))(chip, sums, landed)


def _adamw_math(wv, gv, mv, vv):
    mn = ADAM_B1 * mv + (1.0 - ADAM_B1) * gv
    vn = ADAM_B2 * vv + (1.0 - ADAM_B2) * (gv * gv)
    mh = mn / (1.0 - ADAM_B1 ** ADAM_STEP)
    vh = vn / (1.0 - ADAM_B2 ** ADAM_STEP)
    return -ADAM_LR * (mh / (jnp.sqrt(vh) + ADAM_EPS) + ADAM_WD * wv), mn, vn


def adamw(name, w, g, m, v):
    def body(w_ref, g_ref, m_ref, v_ref, go_ref, d_ref, mo_ref, vo_ref):
        gv = g_ref[...]
        go_ref[...] = gv
        d_ref[...], mo_ref[...], vo_ref[...] = _adamw_math(w_ref[...], gv, m_ref[...], v_ref[...])

    return pl.pallas_call(body, name=name, out_shape=[SDS(w.shape, F32)] * 4)(w, g, m, v)


def adamw_halves(name, w, g_own, g_sib, m, v, core):
    _, r, c = w.shape
    tm = _tile_rows(r, c, 10)

    def body(core_ref, w_ref, go_ref, gs_ref, m_ref, v_ref, g_out, d_out, m_out, v_out):
        gv = jnp.where(pl.program_id(0) == core_ref[0], go_ref[...], gs_ref[...])
        g_out[...] = gv
        d_out[...], m_out[...], v_out[...] = _adamw_math(w_ref[...], gv, m_ref[...], v_ref[...])

    full = pl.BlockSpec((None, tm, c), lambda h, i, core_ref: (h, i, 0))
    own = pl.BlockSpec((tm, c), lambda h, i, core_ref: (jnp.where(h == core_ref[0], i, 0), 0))
    sib = pl.BlockSpec((tm, c), lambda h, i, core_ref: (jnp.where(h == core_ref[0], 0, i), 0))
    grid_spec = pltpu.PrefetchScalarGridSpec(num_scalar_prefetch=1, grid=(2, r // tm),
                                             in_specs=[full, own, sib, full, full], out_specs=[full] * 4)
    return pl.pallas_call(body, name=name, grid_spec=grid_spec, out_shape=[SDS(w.shape, F32)] * 4,
                          compiler_params=_cparams(48 * 1024 * 1024, ("arbitrary", "arbitrary")))(core, w, g_own, g_sib, m, v)


def pack_rows(name, parts, rows):
    width = parts[0].shape[1]
    n = len(parts)

    def body(*refs):
        out_ref = refs[n]
        out_ref[...] = jnp.zeros_like(out_ref)
        off = 0
        for p in refs[:n]:
            out_ref[off:off + p.shape[0], :] = p[...]
            off += p.shape[0]

    return pl.pallas_call(body, name=name, out_shape=SDS((rows, width), F32))(*parts)


def _place():
    x, y, c = lax.axis_index("x"), lax.axis_index("y"), lax.axis_index("c")
    chips = [(1 - x, y), (x, 1 - y), (1 - x, 1 - y)]
    return x, y, c, chips


def _row_split(shape, dtype):
    r, c = shape
    n = 1
    while r % (2 * n) == 0 and (r // (2 * n)) % 16 == 0 and (r // n) * c * jnp.dtype(dtype).itemsize > PIECE_BYTES:
        n *= 2
    return [pl.ds(s * (r // n), r // n) for s in range(n)]


def _pieces(ref):
    *lead, r, c = ref.shape
    split = _row_split((r, c), ref.dtype)
    return [ref.at[(*idx, s)] for idx in itertools.product(*[range(d) for d in lead]) for s in split]


def gather_weights(shards):
    n = len(shards)
    units = [(a, p) for a in range(n) for p in range(len(_row_split(shards[a].shape[1:], shards[a].dtype)))]
    nu = len(units)

    def body(*refs):
        src, dst = refs[:n], refs[n:2 * n]
        send, recv, lsem = refs[2 * n:]
        x, y, c, chips = _place()
        me = 2 * x + y
        sib = (x, y, 1 - c)
        for a in range(n):
            for sp, dp in zip(_pieces(src[a]), _pieces(dst[a].at[me])):
                pltpu.make_async_copy(sp, dp, lsem.at[a]).start()

        def rows(a, p):
            return _row_split(shards[a].shape[1:], shards[a].dtype)[p]

        def ici(u, j):
            a, p = units[u]
            cx, cy = chips[j]
            return pltpu.make_async_remote_copy(src[a].at[c, rows(a, p)], dst[a].at[me, c, rows(a, p)],
                                                send.at[6 * u + j], recv.at[6 * u + j], device_id=(cx, cy, c), device_id_type=MESH)

        def landed(u, j, half, k):
            a, p = units[u]
            cx, cy = chips[j]
            blk = dst[a].at[2 * cx + cy, half, rows(a, p)]
            return pltpu.make_async_remote_copy(blk, blk, send.at[6 * u + k], recv.at[6 * u + k], device_id=sib, device_id_type=MESH)

        window = 6
        for u in range(nu):
            if u >= window:
                for j in range(3):
                    ici(u - window, j).wait_send()
            for j in range(3):
                ici(u, j).start()
            if u >= window:
                for j in range(3):
                    ici(u - window, j).wait_recv()
                    landed(u - window, j, c, 3 + j).start()
        for u in range(max(nu - window, 0), nu):
            for j in range(3):
                ici(u, j).wait_send()
                ici(u, j).wait_recv()
                landed(u, j, c, 3 + j).start()
        for u in range(nu):
            for j in range(3):
                landed(u, j, 1 - c, 3 + j).wait_recv()
                landed(u, j, c, 3 + j).wait_send()
        for a in range(n):
            pltpu.make_async_copy(src[a], dst[a].at[me], lsem.at[a]).wait()

    return pl.pallas_call(
        body, name="gather_weights", in_specs=[ANY] * n, out_specs=[ANY] * n,
        out_shape=[SDS((NCHIP,) + s.shape, s.dtype) for s in shards],
        scratch_shapes=[pltpu.SemaphoreType.DMA((6 * nu,)), pltpu.SemaphoreType.DMA((6 * nu,)), pltpu.SemaphoreType.DMA((n,))],
    )(*shards)


def exchange_halves(parts):
    n = len(parts)

    def body(*refs):
        src, got = refs[:n], refs[n:2 * n]
        send, recv = refs[2 * n:]
        x, y, c, _ = _place()
        sib = (x, y, 1 - c)
        for a in range(n):
            for sp, dp in zip(_pieces(src[a].at[1 - c]), _pieces(got[a])):
                pltpu.make_async_remote_copy(sp, dp, send.at[a], recv.at[a], device_id=sib, device_id_type=MESH).start()
        for a in range(n):
            pltpu.make_async_remote_copy(src[a].at[1 - c], got[a], send.at[a], recv.at[a], device_id=sib, device_id_type=MESH).wait()

    return pl.pallas_call(
        body, name="exchange_halves", in_specs=[ANY] * n, out_specs=[ANY] * n,
        out_shape=[SDS(p.shape[1:], p.dtype) for p in parts],
        scratch_shapes=[pltpu.SemaphoreType.DMA((n,)), pltpu.SemaphoreType.DMA((n,))],
    )(*parts)


def scatter_chips(parts):
    n = len(parts)

    def body(*refs):
        src, dst = refs[:n], refs[n:2 * n]
        send, recv = refs[2 * n:]
        x, y, c, chips = _place()

        def whole(a, j):
            cx, cy = chips[j]
            return pltpu.make_async_remote_copy(src[a].at[2 * cx + cy], dst[a].at[j], send.at[3 * a + j], recv.at[3 * a + j],
                                                device_id=(cx, cy, c), device_id_type=MESH)

        for a in range(n):
            for j, (cx, cy) in enumerate(chips):
                for sp, dp in zip(_pieces(src[a].at[2 * cx + cy]), _pieces(dst[a].at[j])):
                    pltpu.make_async_remote_copy(sp, dp, send.at[3 * a + j], recv.at[3 * a + j],
                                                 device_id=(cx, cy, c), device_id_type=MESH).start()
        for a in range(n):
            for j in range(3):
                whole(a, j).wait()

    return pl.pallas_call(
        body, name="scatter_chips", in_specs=[ANY] * n, out_specs=[ANY] * n,
        out_shape=[SDS((NCHIP - 1,) + p.shape[1:], p.dtype) for p in parts],
        scratch_shapes=[pltpu.SemaphoreType.DMA((3 * n,)), pltpu.SemaphoreType.DMA((3 * n,))],
    )(*parts)


def join_halves(halves):
    n = len(halves)

    def body(*refs):
        src, dst = refs[:n], refs[n:2 * n]
        send, recv = refs[2 * n:]
        x, y, c, _ = _place()
        sib = (x, y, 1 - c)
        for a in range(n):
            for sp, dp in zip(_pieces(src[a]), _pieces(dst[a])):
                pltpu.make_async_remote_copy(sp, dp, send.at[a], recv.at[a], device_id=sib, device_id_type=MESH).start()
        for a in range(n):
            pltpu.make_async_remote_copy(src[a], dst[a], send.at[a], recv.at[a], device_id=sib, device_id_type=MESH).wait()

    return pl.pallas_call(
        body, name="join_halves", in_specs=[ANY] * n, out_specs=[ANY] * n,
        out_shape=[SDS(h.shape, h.dtype) for h in halves],
        scratch_shapes=[pltpu.SemaphoreType.DMA((n,)), pltpu.SemaphoreType.DMA((n,))],
    )(*halves)


def gather_small(name, xs, reduce):
    m, ncol = xs.shape

    def body(x_ref, out_ref, all_ref, send, recv, lsem):
        x, y, c, chips = _place()
        me, sib = (x, y, c), (x, y, 1 - c)

        def rows(px, py, pc):
            return all_ref.at[pl.ds((4 * px + 2 * py + pc) * m, m), :]

        def copy(k, block, to, src=None):
            return pltpu.make_async_remote_copy(rows(*block) if src is None else src, rows(*block), send.at[k], recv.at[k],
                                                device_id=to, device_id_type=MESH)

        mine = pltpu.make_async_copy(x_ref, rows(*me), lsem)
        mine.start()
        first = [copy(0, me, sib, src=x_ref)] + [copy(1 + j, me, (*chip, c), src=x_ref) for j, chip in enumerate(chips)]
        for cp in first:
            cp.start()
        passed = [copy(4 + j, (*chip, c), sib) for j, chip in enumerate(chips)]
        for j, chip in enumerate(chips):
            copy(1 + j, (*chip, c), me).wait_recv()
            passed[j].start()
        copy(0, sib, me).wait_recv()
        for j, chip in enumerate(chips):
            copy(4 + j, (*chip, 1 - c), me).wait_recv()
        for cp in first + passed:
            cp.wait_send()
        mine.wait()
        if reduce:
            s = all_ref[0:m, :]
            for dev in range(1, 8):
                s = s + all_ref[dev * m:(dev + 1) * m, :]
            out_ref[...] = s
        else:
            out_ref[...] = all_ref[...]

    vm = pl.BlockSpec(memory_space=pltpu.VMEM)
    return pl.pallas_call(
        body, name=name, in_specs=[vm], out_specs=vm, out_shape=SDS((m, ncol) if reduce else (8 * m, ncol), F32),
        scratch_shapes=[pltpu.VMEM((8 * m, ncol), F32), pltpu.SemaphoreType.DMA((7,)), pltpu.SemaphoreType.DMA((7,)),
                        pltpu.SemaphoreType.DMA],
    )(xs)


def _to_cat(nat):
    pad = jnp.zeros(nat.shape[:-1] + (NCAT - OA - 16,), nat.dtype)
    return jnp.concatenate([nat[..., 3072:7168], nat[..., 7184:11280], nat[..., 0:3072], nat[..., 7168:7184], pad], axis=-1)


def _from_cat(cat):
    return jnp.concatenate([cat[..., OU:OA], cat[..., OV:OGP], cat[..., OA:OA + 16], cat[..., OGP:OU]], axis=-1)


def _pad_rows(a, rows):
    return jnp.concatenate([a, jnp.zeros((rows - a.shape[0],) + a.shape[1:], a.dtype)], axis=0)


def local_step(x2d, tgt, gf, g1, wcat, pw, pool_scale, wa_pad, b_alpha, ng, w_go, w_o, g2, w_up, w_dn):
    h1 = norm1(x2d, g1)
    pcat = mm_in(h1, wcat)
    dpool, ylin = pool_fwd(pcat, pw)
    og, o, states = gla_fwd(pcat, wa_pad, b_alpha, ng)
    mixed, ygla = mm_gla_out(og, w_go, ylin, pcat, pool_scale)
    x2, h2 = mm_out(mixed, w_o, x2d, g2)
    rup, act = mm_up(h2, w_up)
    dx3, dx3b, g_nf, loss_row = mm_down(act, w_dn, x2, tgt, gf)

    dup = mm_dact(dx3b, w_dn, rup)
    gw_down = mm_wgrad("mm_dw_down", act, dx3b, DFF, D, (2, NCHIP, D // 2, D), (None, None, 512, D),
                       lambda j, i, k: ((i // 2) % 2, i // 4, i % 2, 0), 512, D)
    dx2, dx2b, g_mlp = mm_dh2(dup, w_up, x2, dx3, g2)
    gw_up = mm_wgrad("mm_dw_up", h2, dup, D, DFF, (2, NCHIP, D // 2, D), (None, None, 512, D),
                     lambda j, i, k: (i // 2, j, i % 2, 0), 512, D)
    dylin, dygla, dlgp, dlgg, g_ps = mm_dmixed(dx2b, w_o, pcat, ylin, ygla, pool_scale)
    gw_out = mm_wgrad("mm_dw_out", mixed, dx2b, D, D, (2, NCHIP, 256, D), (None, None, 256, D),
                      lambda j, i, k: (i % 2, i // 2, 0, 0), 256, D)
    do, dg, g_ng = mm_dog(dygla, w_go, o, pcat, ng)
    gw_go = mm_wgrad("mm_dw_gla_out", og, dygla, D, D, (2, NCHIP, 256, D), (None, None, 256, D),
                     lambda j, i, k: (i % 2, i // 2, 0, 0), 256, D)
    dq, dk, dv, dalow, g_wa, g_ba = gla_bwd(do, pcat, states, wa_pad, b_alpha)
    du, dpw = pool_bwd(dylin, dpool, pw)
    dpcat = jnp.concatenate([dv, dg, dlgp, dlgg, du, dq, dk, dalow, jnp.zeros((T, NCAT - OA - APAD), BF16)], axis=1)
    gw_cat = mm_wgrad("mm_dw_in", h1, dpcat, D, NCAT, (D, NCAT), (512, 1280), lambda j, i, k: (i, j), 512, 1280)
    grad_x, g_mix = mm_dh1(dpcat, wcat, x2d, dx2, g1)
    return (loss_row[0, 0], grad_x, gw_cat, dpw, gw_go, gw_out, gw_up, gw_down, g_mix, g_ps, g_mlp, g_nf, g_ng, g_ba, g_wa)


def kernel(x, norm_mix_g, w_in, pool_w, pool_scale, w_alpha, b_alpha, gla_norm_g, w_gla_out, w_out, norm_mlp_g, w_mlp_up, w_mlp_down, norm_final_g, loss_target, m_norm_mix_g, m_w_in, m_pool_w, m_pool_scale, m_w_alpha, m_b_alpha, m_gla_norm_g, m_w_gla_out, m_w_out, m_norm_mlp_g, m_w_mlp_up, m_w_mlp_down, m_norm_final_g, v_norm_mix_g, v_w_in, v_pool_w, v_pool_scale, v_w_alpha, v_b_alpha, v_gla_norm_g, v_w_gla_out, v_w_out, v_norm_mlp_g, v_w_mlp_up, v_w_mlp_down, v_norm_final_g):
    chip = 2 * lax.axis_index("x") + lax.axis_index("y")
    chip_i = chip.astype(jnp.int32).reshape(1)
    core_i = lax.axis_index("c").astype(jnp.int32).reshape(1)
    x2d = x.reshape(T, D)
    tgt = loss_target.reshape(T, D)
    gf = norm_final_g.reshape(1, D)

    def halves(w2d):
        r, c = w2d.shape
        return w2d.astype(BF16).reshape(2, r // 2, c)

    pool_shard = pool_w.reshape(4 * PG, PO // NCHIP)
    big = [w_in[0], w_gla_out[0], w_out[0], w_mlp_up[0], w_mlp_down[0], pool_shard]
    g_in, g_go, g_out, g_up, g_down, g_pool = gather_weights([halves(w) for w in big])
    wcat = _to_cat(jnp.concatenate([g_in[j].reshape(D, IN_SHARD) for j in range(NCHIP)], axis=1))
    w_go = g_go.reshape(D, D)
    w_o = g_out.reshape(D, D)
    w_up = g_up.reshape(NCHIP, D, D)
    w_dn = g_down.reshape(DFF, D)
    pw = jnp.concatenate([g_pool[j].reshape(4, PG, PO // NCHIP) for j in range(NCHIP)], axis=2)

    small_w = pack_rows("pack_small_w", [w_alpha[0].reshape(4, QK),
                                         jnp.concatenate([gla_norm_g[0].reshape(1, 512), jnp.zeros((1, 512), F32)], axis=1)], 8)
    sw_all = gather_small("gather_small_w", small_w, False).reshape(8, 8, QK)
    wa_full = jnp.concatenate([sw_all[2 * j, 0:4].reshape(16, DK) for j in range(NCHIP)], axis=1)
    ng_full = jnp.concatenate([sw_all[2 * j, 4, 0:512].reshape(HEADS, DV // NCHIP) for j in range(NCHIP)], axis=1)
    wa_pad = _pad_rows(wa_full, APAD).astype(BF16)
    ng = ng_full.reshape(1, D)

    (loss_local, grad_x, gw_cat, dpw, gw_go, gw_out, gw_up, gw_down, g_mix, g_ps, g_mlp, g_nf, g_ng, g_ba, g_wa) = local_step(
        x2d, tgt, gf, norm_mix_g, wcat, pw, pool_scale, wa_pad, b_alpha, ng, w_go, w_o, norm_mlp_g, w_up, w_dn)
    loss = lax.psum(loss_local, ("x", "y", "c"))

    gw_in_nat = _from_cat(gw_cat)
    gw_in = jnp.stack([gw_in_nat[:, j * IN_SHARD:(j + 1) * IN_SHARD].reshape(2, D // 2, IN_SHARD)
                       for j in range(NCHIP)], axis=1)
    gw_pool = jnp.stack([dpw[:, :, j * 128:(j + 1) * 128].reshape(2, 2 * PG, 128) for j in range(NCHIP)], axis=1)

    parts = [gw_in, gw_go, gw_out, gw_up, gw_down, gw_pool]
    names = ["in", "gla_out", "out", "up", "down", "pool"]
    theirs = exchange_halves(parts)
    chip_sums = [add_pairs("add_pair_" + nm, a, b, core_i) for nm, a, b in zip(names, parts, theirs)]
    landed = scatter_chips(chip_sums)
    reduced = [sum_chips("sum_chips_" + nm, a, b, chip_i) for nm, a, b in zip(names, chip_sums, landed)]
    from_sib = join_halves(reduced)

    big_m = [m_w_in, m_w_gla_out, m_w_out, m_w_mlp_up, m_w_mlp_down, m_pool_w]
    big_v = [v_w_in, v_w_gla_out, v_w_out, v_w_mlp_up, v_w_mlp_down, v_pool_w]
    big_res = {}
    for nm, w, g_own, g_sib, m, v in zip(names, big, reduced, from_sib, big_m, big_v):
        shp = (2,) + g_own.shape
        big_res[nm] = adamw_halves("adamw_" + nm, w.reshape(shp), g_own, g_sib, m.reshape(shp), v.reshape(shp), core_i)

    ROWS = 16

    def wide(a, n):
        return jnp.concatenate([a.reshape(1, n), jnp.zeros((1, D - n), F32)], axis=1)

    packed = pack_rows("pack_small_g", [g_mix, g_ps, g_mlp, g_nf, g_ng, wide(g_ba, QK), g_wa[0:16].reshape(8, D)], ROWS)
    tot = gather_small("reduce_small_g", packed, True)
    t_wa = lax.dynamic_slice(tot[6:14].reshape(16, QK), (0, chip * DK), (16, DK))
    t_ng = lax.dynamic_slice(tot[4].reshape(HEADS, DV), (0, chip * (DV // NCHIP)), (HEADS, DV // NCHIP))

    def pack_small(nm, mix, ps, mlp, nf, ba, wa, gn):
        return pack_rows(nm, [mix.reshape(1, D), ps.reshape(1, D), mlp.reshape(1, D), nf.reshape(1, D), wide(ba, QK),
                              wa.reshape(2, D), wide(gn, 512)], ROWS)

    sg = pack_small("pack_g", tot[0], tot[1], tot[2], tot[3], tot[5, 0:QK], t_wa, t_ng)
    sw = pack_small("pack_w", norm_mix_g, pool_scale, norm_mlp_g, norm_final_g, b_alpha, w_alpha, gla_norm_g)
    sm = pack_small("pack_m", m_norm_mix_g, m_pool_scale, m_norm_mlp_g, m_norm_final_g, m_b_alpha, m_w_alpha, m_gla_norm_g)
    sv = pack_small("pack_v", v_norm_mix_g, v_pool_scale, v_norm_mlp_g, v_norm_final_g, v_b_alpha, v_w_alpha, v_gla_norm_g)
    small_res = adamw("adamw_small", sw, sg, sm, sv)

    def unpack(p):
        return {"norm_mix_g": p[0].reshape(1, D), "pool_scale": p[1].reshape(1, D), "norm_mlp_g": p[2].reshape(1, D),
                "norm_final_g": p[3].reshape(D), "b_alpha": p[4, 0:QK].reshape(1, QK), "w_alpha": p[5:7].reshape(1, 16, DK),
                "gla_norm_g": p[7, 0:512].reshape(1, HEADS, DV // NCHIP)}

    order = ["norm_mix_g", "w_in", "pool_w", "pool_scale", "w_alpha", "b_alpha", "gla_norm_g", "w_gla_out", "w_out",
             "norm_mlp_g", "w_mlp_up", "w_mlp_down", "norm_final_g"]
    big_key = {"w_in": ("in", w_in.shape), "pool_w": ("pool", pool_w.shape), "w_gla_out": ("gla_out", w_gla_out.shape),
               "w_out": ("out", w_out.shape), "w_mlp_up": ("up", w_mlp_up.shape), "w_mlp_down": ("down", w_mlp_down.shape)}
    result = [loss, grad_x.reshape(1, T, D)]
    for kind in range(4):
        small = unpack(small_res[kind])
        for nm in order:
            if nm in big_key:
                key, shp = big_key[nm]
                result.append(big_res[key][kind].reshape(shp))
            else:
                result.append(small[nm])
    return tuple(result)
```

```python
import itertools

import jax
import jax.numpy as jnp
from jax import lax
from jax.experimental import pallas as pl
from jax.experimental.pallas import tpu as pltpu

F32 = jnp.float32
BF16 = jnp.bfloat16
SDS = jax.ShapeDtypeStruct
MESH = pl.DeviceIdType.MESH
ANY = pl.BlockSpec(memory_space=pl.ANY)

T = 2048
D = 2048
DFF = 8192
NCHIP = 4
IN_WIDTH = 11280
IN_SHARD = IN_WIDTH // NCHIP
CHUNK = 64
NCHUNK = T // CHUNK
HEADS = 4
DK = 256
DV = 512
QK = HEADS * DK
EPS = 1e-6
POOL_WINDOWS = (2, 4, 8, 16)
PG = 256
PO = 512

OV, OG, OGP, OGG, OU, OQ, OKK, OA = 0, 2048, 4096, 6144, 8192, 9216, 10240, 11264
NCAT = 11520
APAD = 128

VMEM_CAP = 56 * 1024 * 1024

PIECE_BYTES = 3 * 512 * 1024

ADAM_LR, ADAM_B1, ADAM_B2, ADAM_EPS, ADAM_WD, ADAM_STEP = 0.001, 0.9, 0.999, 1e-08, 0.01, 10


def _cparams(vmem_bytes=None, sem=None):
    kw = {}
    if vmem_bytes is not None:
        kw["vmem_limit_bytes"] = int(min(max(vmem_bytes, 32 * 1024 * 1024), VMEM_CAP))
    if sem is not None:
        kw["dimension_semantics"] = sem
    return pltpu.CompilerParams(**kw)


def _nbytes(shape, dtype):
    n = 1
    for s in shape:
        if s is not None:
            n *= s
    return n * jnp.dtype(dtype).itemsize


def _sigmoid(x):
    return 1.0 / (1.0 + jnp.exp(-x))


def matmul(name, a, b, *, a_spec, b_spec, cdims, grid, acc_shape, outs, extras=(), epi, after=None):
    nj, ni, nk = grid
    ne, no = len(extras), len(outs)
    first_out = 2 + ne + (0 if after is None else 1)

    def body(*refs):
        a_ref, b_ref = refs[0], refs[1]
        ex = refs[2:2 + ne]
        out_refs = refs[first_out:first_out + no]
        i = pl.program_id(1)
        part = lax.dot_general(a_ref[...], b_ref[...], (cdims, ((), ())), preferred_element_type=F32)
        if nk == 1:
            epi(part, ex, out_refs, i)
        else:
            acc_ref = refs[first_out + no]
            k = pl.program_id(2)

            @pl.when(k == 0)
            def _():
                acc_ref[...] = part

            @pl.when(k > 0)
            def _():
                acc_ref[...] += part

            @pl.when(k == nk - 1)
            def _():
                epi(acc_ref[...], ex, out_refs, i)

    in_specs = [pl.BlockSpec(*a_spec), pl.BlockSpec(*b_spec)] + [pl.BlockSpec(bs, im) for _, bs, im in extras]
    in_specs += [] if after is None else [ANY]
    out_specs = [pl.BlockSpec(bs, im) for _, _, bs, im in outs]
    out_shape = [SDS(s, dt) for s, dt, _, _ in outs]
    vm = 2 * (_nbytes(a_spec[0], a.dtype) + _nbytes(b_spec[0], b.dtype))
    vm += 2 * sum(_nbytes(bs, arr.dtype) for arr, bs, _ in extras)
    vm += 2 * sum(_nbytes(bs, dt) for _, dt, bs, _ in outs)
    vm += 6 * _nbytes(acc_shape, F32)
    scratch = [pltpu.VMEM(acc_shape, F32)] if nk > 1 else []
    return pl.pallas_call(
        body, name=name, grid=grid, in_specs=in_specs, out_specs=out_specs, out_shape=out_shape,
        scratch_shapes=scratch,
        compiler_params=_cparams(vm, ("arbitrary", "arbitrary", "arbitrary")),
    )(a, b, *[arr for arr, _, _ in extras], *([] if after is None else [after]))


NN =((1,), (0,))
NT = ((1,), (1,))
TN = ((0,), (0,))


def _row_acc(out_ref, val, i):
    @pl.when(i == 0)
    def _():
        out_ref[...] = val

    @pl.when(i > 0)
    def _():
        out_ref[...] += val


def _rms_bwd(xn, r, dxn):
    return r * (dxn - xn * jnp.mean(dxn * xn, axis=-1, keepdims=True))


def norm1(x, g):
    tm = 256

    def body(x_ref, g_ref, h_ref):
        xv = x_ref[...]
        r = lax.rsqrt(jnp.mean(xv * xv, axis=-1, keepdims=True) + EPS)
        h_ref[...] = (xv * r * g_ref[...]).astype(BF16)

    return pl.pallas_call(
        body, name="norm1", grid=(T // tm,),
        in_specs=[pl.BlockSpec((tm, D), lambda i: (i, 0)), pl.BlockSpec((1, D), lambda i: (0, 0))],
        out_specs=pl.BlockSpec((tm, D), lambda i: (i, 0)), out_shape=SDS((T, D), BF16),
        compiler_params=_cparams(32 * 1024 * 1024, ("arbitrary",)),
    )(x, g)


def mm_in(h1, wcat):
    tm, tn = 512, 1280

    def epi(acc, ex, outs, i):
        outs[0][...] = acc.astype(BF16)

    return matmul("mm_in", h1, wcat, a_spec=((tm, D), lambda j, i, k: (i, 0)), b_spec=((D, tn), lambda j, i, k: (0, j)),
                  cdims=NN, grid=(NCAT // tn, T // tm, 1), acc_shape=(tm, tn),
                  outs=[((T, NCAT), BF16, (tm, tn), lambda j, i, k: (i, j))], epi=epi)[0]


def _window_sum(x, w, up):
    n = x.shape[0]
    row = lax.broadcasted_iota(jnp.int32, x.shape, 0)
    s, sh = x, 1
    while sh < w:
        if up:
            s = s + jnp.where(row < n - sh, pltpu.roll(s, n - sh, axis=0), 0.0)
        else:
            s = s + jnp.where(row >= sh, pltpu.roll(s, sh, axis=0), 0.0)
        sh *= 2
    return s


def _inv_count(shape, w):
    row = lax.broadcasted_iota(jnp.int32, shape, 0)
    return 1.0 / jnp.minimum(row + 1, w).astype(F32)


def pool_fwd(pcat, pw):
    def body(u_ref, pw_ref, d_ref, y_ref):
        for gi, w in enumerate(POOL_WINDOWS):
            ug = u_ref[:, gi * PG:(gi + 1) * PG].astype(F32)
            dg = _window_sum(ug, w, False) * _inv_count(ug.shape, w) - ug
            db = dg.astype(BF16)
            d_ref[:, gi * PG:(gi + 1) * PG] = db
            y_ref[:, gi * PO:(gi + 1) * PO] = jnp.dot(db, pw_ref[gi], preferred_element_type=F32).astype(BF16)

    return pl.pallas_call(
        body, name="pool_fwd", grid=(1,),
        in_specs=[pl.BlockSpec((T, 4 * PG), lambda i: (0, OU // (4 * PG))), pl.BlockSpec((4, PG, PO), lambda i: (0, 0, 0))],
        out_specs=[pl.BlockSpec((T, 4 * PG), lambda i: (0, 0)), pl.BlockSpec((T, D), lambda i: (0, 0))],
        out_shape=[SDS((T, 4 * PG), BF16), SDS((T, D), BF16)],
        compiler_params=_cparams(48 * 1024 * 1024, ("arbitrary",)),
    )(pcat, pw)


def pool_bwd(dylin, d, pw):
    def body(dy_ref, d_ref, pw_ref, du_ref, dpw_ref):
        for gi, w in enumerate(POOL_WINDOWS):
            dyl = dy_ref[:, gi * PO:(gi + 1) * PO]
            dd = lax.dot_general(dyl, pw_ref[gi], (NT, ((), ())), preferred_element_type=F32)
            du = _window_sum(dd * _inv_count(dd.shape, w), w, True) - dd
            du_ref[:, gi * PG:(gi + 1) * PG] = du.astype(BF16)
            dpw_ref[gi] = lax.dot_general(d_ref[:, gi * PG:(gi + 1) * PG], dyl, (TN, ((), ())),
                                          preferred_element_type=F32).astype(BF16)

    return pl.pallas_call(
        body, name="pool_bwd", grid=(1,),
        in_specs=[pl.BlockSpec((T, D), lambda i: (0, 0)), pl.BlockSpec((T, 4 * PG), lambda i: (0, 0)),
                  pl.BlockSpec((4, PG, PO), lambda i: (0, 0, 0))],
        out_specs=[pl.BlockSpec((T, 4 * PG), lambda i: (0, 0)), pl.BlockSpec((4, PG, PO), lambda i: (0, 0, 0))],
        out_shape=[SDS((T, 4 * PG), BF16), SDS((4, PG, PO), BF16)],
        compiler_params=_cparams(48 * 1024 * 1024, ("arbitrary",)),
    )(dylin, d, pw)


def _gate_decay(alow, wa, ba):
    a = jnp.dot(alow, wa, preferred_element_type=F32) + ba
    ls = jax.nn.log_sigmoid(a) * (1.0 / 16.0)
    r = lax.broadcasted_iota(jnp.int32, (CHUNK, CHUNK), 0)
    c = lax.broadcasted_iota(jnp.int32, (CHUNK, CHUNK), 1)
    tri = jnp.where(c <= r, 1.0, 0.0).astype(F32)
    cum = jnp.dot(tri, ls, preferred_element_type=F32, precision=lax.Precision.HIGHEST)
    last = cum[CHUNK - 1:CHUNK, :]
    return a, jnp.exp(last - cum), jnp.exp(last)


def gla_fwd(pcat, wa, ba, ng):
    def body(q_ref, k_ref, v_ref, g_ref, al_ref, wa_ref, ba_ref, ng_ref, og_ref, o_ref, st_ref, s_scr):
        @pl.when(pl.program_id(0) == 0)
        def _():
            s_scr[...] = jnp.zeros_like(s_scr)

        _, e, decay = _gate_decay(al_ref[...], wa_ref[...], ba_ref[...])
        kd = (k_ref[...].astype(F32) * e).astype(BF16)
        qs = (q_ref[...].astype(F32) * (DK ** -0.5)).astype(BF16)
        for h in range(HEADS):
            ck = slice(h * DK, (h + 1) * DK)
            cv = slice(h * DV, (h + 1) * DV)
            s_new = s_scr[h] * decay[:, ck] + lax.dot_general(v_ref[:, cv], kd[:, ck], (TN, ((), ())),
                                                               preferred_element_type=F32)
            s_scr[h] = s_new
            sb = s_new.astype(BF16)
            st_ref[h] = sb
            oh = lax.dot_general(qs[:, ck], sb, (NT, ((), ())), preferred_element_type=F32)
            o_ref[:, cv] = oh.astype(BF16)
            on = oh * lax.rsqrt(jnp.mean(oh * oh, axis=-1, keepdims=True) + EPS) * ng_ref[:, cv]
            gv = g_ref[:, cv].astype(F32)
            og_ref[:, cv] = (on * (gv * _sigmoid(gv))).astype(BF16)

    row = lambda c: (c, 0)
    return pl.pallas_call(
        body, name="gla_fwd", grid=(NCHUNK,),
        in_specs=[pl.BlockSpec((CHUNK, QK), lambda c: (c, OQ // QK)), pl.BlockSpec((CHUNK, QK), lambda c: (c, OKK // QK)),
                  pl.BlockSpec((CHUNK, D), lambda c: (c, OV // D)), pl.BlockSpec((CHUNK, D), lambda c: (c, OG // D)),
                  pl.BlockSpec((CHUNK, APAD), lambda c: (c, OA // APAD)),
                  pl.BlockSpec((APAD, QK), lambda c: (0, 0)), pl.BlockSpec((1, QK), lambda c: (0, 0)),
                  pl.BlockSpec((1, D), lambda c: (0, 0))],
        out_specs=[pl.BlockSpec((CHUNK, D), row), pl.BlockSpec((CHUNK, D), row),
                   pl.BlockSpec((None, HEADS, DV, DK), lambda c: (c, 0, 0, 0))],
        out_shape=[SDS((T, D), BF16), SDS((T, D), BF16), SDS((NCHUNK, HEADS, DV, DK), BF16)],
        scratch_shapes=[pltpu.VMEM((HEADS, DV, DK), F32)],
        compiler_params=_cparams(32 * 1024 * 1024, ("arbitrary",)),
    )(pcat, pcat, pcat, pcat, pcat, wa, ba, ng)


def gla_bwd(do, pcat, states, wa, ba, after):
    def body(do_ref, q_ref, k_ref, v_ref, al_ref, sc_ref, sp_ref, wa_ref, ba_ref, after_ref,
             dq_ref, dk_ref, dv_ref, dal_ref, dwa_ref, dba_ref, ds_scr):
        i = pl.program_id(0)

        @pl.when(i == 0)
        def _():
            ds_scr[...] = jnp.zeros_like(ds_scr)

        has_prev = jnp.where(i < NCHUNK - 1, 1.0, 0.0).astype(F32)
        a, e, decay = _gate_decay(al_ref[...], wa_ref[...], ba_ref[...])
        kf = k_ref[...].astype(F32)
        kdf = kf * e
        kd = kdf.astype(BF16)
        qs = (q_ref[...].astype(F32) * (DK ** -0.5)).astype(BF16)
        dkd_parts, ddecay_parts = [], []
        for h in range(HEADS):
            ck = slice(h * DK, (h + 1) * DK)
            cv = slice(h * DV, (h + 1) * DV)
            doh = do_ref[:, cv]
            ds = ds_scr[h] + lax.dot_general(doh, qs[:, ck], (TN, ((), ())), preferred_element_type=F32)
            dsb = ds.astype(BF16)
            dq_ref[:, ck] = (jnp.dot(doh, sc_ref[h], preferred_element_type=F32) * (DK ** -0.5)).astype(BF16)
            dkd_parts.append(jnp.dot(v_ref[:, cv], dsb, preferred_element_type=F32))
            dv_ref[:, cv] = lax.dot_general(kd[:, ck], dsb, (NT, ((), ())), preferred_element_type=F32).astype(BF16)
            ddecay_parts.append(jnp.sum(ds * sp_ref[h].astype(F32), axis=0, keepdims=True) * has_prev)
            ds_scr[h] = ds * decay[:, ck]
        dkd = jnp.concatenate(dkd_parts, axis=1)
        ddecay = jnp.concatenate(ddecay_parts, axis=1)
        dk_ref[...] = (dkd * e).astype(BF16)
        dearg = dkd * kdf
        dlast = jnp.sum(dearg, axis=0, keepdims=True) + ddecay * decay
        r = lax.broadcasted_iota(jnp.int32, (CHUNK, CHUNK), 0)
        c = lax.broadcasted_iota(jnp.int32, (CHUNK, CHUNK), 1)
        triu = jnp.where(c >= r, 1.0, 0.0).astype(F32)
        dls = dlast - jnp.dot(triu, dearg, preferred_element_type=F32, precision=lax.Precision.HIGHEST)
        da = dls * (1.0 / 16.0) * (1.0 - _sigmoid(a))
        dab = da.astype(BF16)
        dal_ref[...] = lax.dot_general(dab, wa_ref[...], (NT, ((), ())), preferred_element_type=F32).astype(BF16)
        dwa = lax.dot_general(al_ref[...], dab, (TN, ((), ())), preferred_element_type=F32)
        dba = jnp.sum(da, axis=0, keepdims=True)

        @pl.when(i == 0)
        def _():
            dwa_ref[...] = dwa
            dba_ref[...] = dba

        @pl.when(i > 0)
        def _():
            dwa_ref[...] += dwa
            dba_ref[...] += dba

    rev = lambda i: NCHUNK - 1 - i
    return pl.pallas_call(
        body, name="gla_bwd", grid=(NCHUNK,),
        in_specs=[pl.BlockSpec((CHUNK, D), lambda i: (rev(i), 0)),
                  pl.BlockSpec((CHUNK, QK), lambda i: (rev(i), OQ // QK)), pl.BlockSpec((CHUNK, QK), lambda i: (rev(i), OKK // QK)),
                  pl.BlockSpec((CHUNK, D), lambda i: (rev(i), OV // D)), pl.BlockSpec((CHUNK, APAD), lambda i: (rev(i), OA // APAD)),
                  pl.BlockSpec((None, HEADS, DV, DK), lambda i: (rev(i), 0, 0, 0)),
                  pl.BlockSpec((None, HEADS, DV, DK), lambda i: (jnp.maximum(rev(i) - 1, 0), 0, 0, 0)),
                  pl.BlockSpec((APAD, QK), lambda i: (0, 0)), pl.BlockSpec((1, QK), lambda i: (0, 0)), ANY],
        out_specs=[pl.BlockSpec((CHUNK, QK), lambda i: (rev(i), 0)), pl.BlockSpec((CHUNK, QK), lambda i: (rev(i), 0)),
                   pl.BlockSpec((CHUNK, D), lambda i: (rev(i), 0)), pl.BlockSpec((CHUNK, APAD), lambda i: (rev(i), 0)),
                   pl.BlockSpec((APAD, QK), lambda i: (0, 0)), pl.BlockSpec((1, QK), lambda i: (0, 0))],
        out_shape=[SDS((T, QK), BF16), SDS((T, QK), BF16), SDS((T, D), BF16), SDS((T, APAD), BF16),
                   SDS((APAD, QK), F32), SDS((1, QK), F32)],
        scratch_shapes=[pltpu.VMEM((HEADS, DV, DK), F32)],
        compiler_params=_cparams(32 * 1024 * 1024, ("arbitrary",)),
    )(do, pcat, pcat, pcat, pcat, states, states, wa, ba, after)


TMF = 256
_rowblk = ((TMF, D), lambda j, i, k: (i, 0))
_vec = ((1, D), lambda j, i, k: (0, 0))


def _full_spec(col):
    return ((TMF, D), lambda j, i, k: (i, col))


TBIG = 1024


def square_matmul(name, a, b, *, a_spec, b_spec, cdims, nk, after=None):
    def epi(acc, ex, outs, i):
        outs[0][...] = acc

    return matmul(name, a, b, a_spec=a_spec, b_spec=b_spec, cdims=cdims, grid=(D // TBIG, T // TBIG, nk),
                  acc_shape=(TBIG, TBIG), outs=[((T, D), F32, (TBIG, TBIG), lambda j, i, k: (i, j))], epi=epi,
                  after=after)[0]


def rowwise(name, y, *, extras, outs, epi):
    ne = len(extras)

    def body(*refs):
        epi(refs[0][...], refs[1:1 + ne], refs[1 + ne:], pl.program_id(1))

    in_specs = [pl.BlockSpec(*_rowblk)] + [pl.BlockSpec(bs, im) for _, bs, im in extras]
    return pl.pallas_call(
        body, name=name, grid=(1, T // TMF, 1), in_specs=in_specs,
        out_specs=[pl.BlockSpec(bs, im) for _, _, bs, im in outs], out_shape=[SDS(s, dt) for s, dt, _, _ in outs],
        compiler_params=_cparams(40 * 1024 * 1024, ("arbitrary", "arbitrary", "arbitrary")),
    )(y, *[arr for arr, _, _ in extras])


def mm_gla_out(og, w, ylin, pcat, pscale):
    def epi(acc, ex, outs, i):
        ylin_ref, lgp_ref, lgg_ref, ps_ref = ex
        gp = _sigmoid(lgp_ref[...].astype(F32))
        gg = _sigmoid(lgg_ref[...].astype(F32))
        outs[0][...] = (gp * (ylin_ref[...].astype(F32) * ps_ref[...]) + gg * acc).astype(BF16)
        outs[1][...] = acc.astype(BF16)

    return matmul("mm_gla_out", og, w, a_spec=_rowblk, b_spec=((D, D), lambda j, i, k: (0, 0)), cdims=NN,
                  grid=(1, T // TMF, 1), acc_shape=(TMF, D),
                  extras=[(ylin, *_rowblk), (pcat, *_full_spec(OGP // D)), (pcat, *_full_spec(OGG // D)), (pscale, *_vec)],
                  outs=[((T, D), BF16, *_rowblk), ((T, D), BF16, *_rowblk)], epi=epi)


def mm_out(mixed, w, x, g2):
    def epi(acc, ex, outs, i):
        x_ref, g_ref = ex
        x2 = x_ref[...] + acc
        r = lax.rsqrt(jnp.mean(x2 * x2, axis=-1, keepdims=True) + EPS)
        outs[0][...] = x2
        outs[1][...] = (x2 * r * g_ref[...]).astype(BF16)

    return matmul("mm_out", mixed, w, a_spec=_rowblk, b_spec=((D, D), lambda j, i, k: (0, 0)), cdims=NN,
                  grid=(1, T // TMF, 1), acc_shape=(TMF, D), extras=[(x, *_rowblk), (g2, *_vec)],
                  outs=[((T, D), F32, *_rowblk), ((T, D), BF16, *_rowblk)], epi=epi)


def mm_up(h2, wup):
    def epi(acc, ex, outs, i):
        r = jnp.maximum(acc, 0.0)
        outs[0][...] = r.astype(BF16)
        outs[1][...] = (r * r).astype(BF16)

    blk = ((TMF, D), lambda j, i, k: (i, j))
    return matmul("mm_up", h2, wup, a_spec=_rowblk, b_spec=((None, D, D), lambda j, i, k: (j, 0, 0)), cdims=NN,
                  grid=(NCHIP, T // TMF, 1), acc_shape=(TMF, D),
                  outs=[((T, DFF), BF16, *blk), ((T, DFF), BF16, *blk)], epi=epi)


def mm_down(act, wdown, x2, tgt, gf):
    tk = 2048

    def epi(acc, ex, outs, i):
        x2_ref, t_ref, g_ref = ex
        dx_ref, dxb_ref, gnf_ref, loss_ref = outs
        x3 = x2_ref[...] + acc
        r = lax.rsqrt(jnp.mean(x3 * x3, axis=-1, keepdims=True) + EPS)
        xn = x3 * r
        err = xn * g_ref[...] - t_ref[...]
        lsum = 0.5 * jnp.sum(jnp.mean(err * err, axis=-1, keepdims=True), axis=0, keepdims=True)
        dy = err * (1.0 / D)
        _row_acc(gnf_ref, jnp.sum(dy * xn, axis=0, keepdims=True), i)
        _row_acc(loss_ref, jnp.broadcast_to(lsum, (1, 128)), i)
        dx3 = _rms_bwd(xn, r, dy * g_ref[...])
        dx_ref[...] = dx3
        dxb_ref[...] = dx3.astype(BF16)

    y = square_matmul("mm_down", act, wdown, a_spec=((TBIG, tk), lambda j, i, k: (i, k)),
                      b_spec=((tk, TBIG), lambda j, i, k: (k, j)), cdims=NN, nk=DFF // tk)
    return rowwise("rows_final", y, extras=[(x2, *_rowblk), (tgt, *_rowblk), (gf, *_vec)],
                   outs=[((T, D), F32, *_rowblk), ((T, D), BF16, *_rowblk), ((1, D), F32, *_vec),
                         ((1, 128), F32, (1, 128), lambda j, i, k: (0, 0))], epi=epi)


def mm_dact(dx3b, wdown, rup, after=None):
    def epi(acc, ex, outs, i):
        outs[0][...] = (acc * 2.0 * ex[0][...].astype(F32)).astype(BF16)

    blk = ((TMF, D), lambda j, i, k: (i, j))
    return matmul("mm_dact", dx3b, wdown, a_spec=_rowblk, b_spec=((D, D), lambda j, i, k: (j, 0)), cdims=NT,
                  grid=(DFF // D, T // TMF, 1), acc_shape=(TMF, D), extras=[(rup, *blk)],
                  outs=[((T, DFF), BF16, *blk)], epi=epi, after=after)[0]


def mm_wgrad(name, a, b, m, n, out_shape, out_block, out_map, tm, tn):
    def epi(acc, ex, outs, i):
        outs[0][...] = acc.astype(BF16)

    return matmul(name, a, b, a_spec=((T, tm), lambda j, i, k: (0, i)), b_spec=((T, tn), lambda j, i, k: (0, j)),
                  cdims=TN, grid=(n // tn, m // tm, 1), acc_shape=(tm, tn),
                  outs=[(out_shape, BF16, out_block, out_map)], epi=epi)[0]


def mm_dh2(dup, wup, x2, dx3, g2):
    def epi(acc, ex, outs, i):
        x2_ref, dx3_ref, g_ref = ex
        x2 = x2_ref[...]
        r = lax.rsqrt(jnp.mean(x2 * x2, axis=-1, keepdims=True) + EPS)
        xn = x2 * r
        _row_acc(outs[2], jnp.sum(acc * xn, axis=0, keepdims=True), i)
        dx2 = dx3_ref[...] + _rms_bwd(xn, r, acc * g_ref[...])
        outs[0][...] = dx2
        outs[1][...] = dx2.astype(BF16)

    y = square_matmul("mm_dh2", dup, wup, a_spec=((TBIG, D), lambda j, i, k: (i, k)),
                      b_spec=((None, TBIG, D), lambda j, i, k: (k, j, 0)), cdims=NT, nk=NCHIP)
    return rowwise("rows_dh2", y, extras=[(x2, *_rowblk), (dx3, *_rowblk), (g2, *_vec)],
                   outs=[((T, D), F32, *_rowblk), ((T, D), BF16, *_rowblk), ((1, D), F32, *_vec)], epi=epi)


def mm_dmixed(dx2b, wout, pcat, ylin, ygla, pscale, after=None):
    def epi(acc, ex, outs, i):
        lgp_ref, lgg_ref, ylin_ref, ygla_ref, ps_ref = ex
        gp = _sigmoid(lgp_ref[...].astype(F32))
        gg = _sigmoid(lgg_ref[...].astype(F32))
        yl = ylin_ref[...].astype(F32)
        ps = ps_ref[...]
        agp = acc * gp
        outs[0][...] = (agp * ps).astype(BF16)
        outs[1][...] = (acc * gg).astype(BF16)
        outs[2][...] = (agp * (yl * ps) * (1.0 - gp)).astype(BF16)
        outs[3][...] = (acc * ygla_ref[...].astype(F32) * gg * (1.0 - gg)).astype(BF16)
        _row_acc(outs[4], jnp.sum(agp * yl, axis=0, keepdims=True), i)

    return matmul("mm_dmixed", dx2b, wout, a_spec=_rowblk, b_spec=((D, D), lambda j, i, k: (0, 0)), cdims=NT,
                  grid=(1, T // TMF, 1), acc_shape=(TMF, D),
                  extras=[(pcat, *_full_spec(OGP // D)), (pcat, *_full_spec(OGG // D)), (ylin, *_rowblk), (ygla, *_rowblk),
                          (pscale, *_vec)],
                  outs=[((T, D), BF16, *_rowblk)] * 4 + [((1, D), F32, *_vec)], epi=epi, after=after)


def mm_dog(dygla, wgo, o, pcat, ng):
    def epi(acc, ex, outs, i):
        o_ref, g_ref, ng_ref = ex
        do_ref, dg_ref, gng_ref = outs
        gparts = []
        for h in range(HEADS):
            cv = slice(h * DV, (h + 1) * DV)
            oh = o_ref[:, cv].astype(F32)
            r = lax.rsqrt(jnp.mean(oh * oh, axis=-1, keepdims=True) + EPS)
            on = oh * r
            gv = g_ref[:, cv].astype(F32)
            sg = _sigmoid(gv)
            dgain = acc[:, cv] * (gv * sg)
            gparts.append(jnp.sum(dgain * on, axis=0, keepdims=True))
            ngh = ng_ref[:, cv]
            do_ref[:, cv] = _rms_bwd(on, r, dgain * ngh).astype(BF16)
            dg_ref[:, cv] = (acc[:, cv] * (on * ngh) * (sg * (1.0 + gv * (1.0 - sg)))).astype(BF16)
        _row_acc(gng_ref, jnp.concatenate(gparts, axis=1), i)

    return matmul("mm_dog", dygla, wgo, a_spec=_rowblk, b_spec=((D, D), lambda j, i, k: (0, 0)), cdims=NT,
                  grid=(1, T // TMF, 1), acc_shape=(TMF, D),
                  extras=[(o, *_rowblk), (pcat, *_full_spec(OG // D)), (ng, *_vec)],
                  outs=[((T, D), BF16, *_rowblk), ((T, D), BF16, *_rowblk), ((1, D), F32, *_vec)], epi=epi)


def mm_dh1(dpcat, wcat, x, dx2, g1, after=None):
    tk = 1280

    def epi(acc, ex, outs, i):
        x_ref, dx2_ref, g_ref = ex
        xv = x_ref[...]
        r = lax.rsqrt(jnp.mean(xv * xv, axis=-1, keepdims=True) + EPS)
        xn = xv * r
        _row_acc(outs[1], jnp.sum(acc * xn, axis=0, keepdims=True), i)
        outs[0][...] = dx2_ref[...] + _rms_bwd(xn, r, acc * g_ref[...])

    y = square_matmul("mm_dh1", dpcat, wcat, a_spec=((TBIG, tk), lambda j, i, k: (i, k)),
                      b_spec=((TBIG, tk), lambda j, i, k: (j, k)), cdims=NT, nk=NCAT // tk, after=after)
    return rowwise("rows_dh1", y, extras=[(x, *_rowblk), (dx2, *_rowblk), (g1, *_vec)],
                   outs=[((T, D), F32, *_rowblk), ((1, D), F32, *_vec)], epi=epi)


def _tile_rows(rows, cols, n_arrays):
    tm = rows
    while tm % 32 == 0 and 2 * n_arrays * tm * cols * 4 > 24 * 1024 * 1024:
        tm //= 2
    return tm


def add_pairs(name, parts, theirs, core):
    _, _, r, c = parts.shape
    tm = _tile_rows(r, c, 3)

    def body(core_ref, a_ref, b_ref, o_ref):
        o_ref[...] = (a_ref[...].astype(F32) + b_ref[...].astype(F32)).astype(BF16)

    spec = pl.BlockSpec((None, tm, c), lambda j, i, core_ref: (j, i, 0))
    grid_spec = pltpu.PrefetchScalarGridSpec(
        num_scalar_prefetch=1, grid=(NCHIP, r // tm),
        in_specs=[pl.BlockSpec((None, None, tm, c), lambda j, i, core_ref: (core_ref[0], j, i, 0)), spec], out_specs=spec)
    return pl.pallas_call(body, name=name, grid_spec=grid_spec, out_shape=SDS((NCHIP, r, c), BF16),
                          compiler_params=_cparams(40 * 1024 * 1024, ("arbitrary", "arbitrary")))(core, parts, theirs)


def sum_chips(name, sums, landed, chip):
    _, r, c = sums.shape
    tm = _tile_rows(r, c, 4)

    def body(chip_ref, own_ref, l_ref, o_ref):
        s = own_ref[...].astype(F32)
        for t in range(NCHIP - 1):
            s = s + l_ref[t].astype(F32)
        o_ref[...] = s

    grid_spec = pltpu.PrefetchScalarGridSpec(
        num_scalar_prefetch=1, grid=(r // tm,),
        in_specs=[pl.BlockSpec((None, tm, c), lambda i, chip_ref: (chip_ref[0], i, 0)),
                  pl.BlockSpec((NCHIP - 1, tm, c), lambda i, chip_ref: (0, i, 0))],
        out_specs=pl.BlockSpec((tm, c), lambda i, chip_ref: (i, 0)))
    return pl.pallas_call(body, name=name, grid_spec=grid_spec, out_shape=SDS((r, c), F32),
                          compiler_params=_cparams(40 * 1024 * 1024, ("arbitrary",)))(chip, sums, landed)


def _adamw_math(wv, gv, mv, vv):
    mn = ADAM_B1 * mv + (1.0 - ADAM_B1) * gv
    vn = ADAM_B2 * vv + (1.0 - ADAM_B2) * (gv * gv)
    mh = mn / (1.0 - ADAM_B1 ** ADAM_STEP)
    vh = vn / (1.0 - ADAM_B2 ** ADAM_STEP)
    return -ADAM_LR * (mh / (jnp.sqrt(vh) + ADAM_EPS) + ADAM_WD * wv), mn, vn


def adamw(name, w, g, m, v):
    def body(w_ref, g_ref, m_ref, v_ref, go_ref, d_ref, mo_ref, vo_ref):
        gv = g_ref[...]
        go_ref[...] = gv
        d_ref[...], mo_ref[...], vo_ref[...] = _adamw_math(w_ref[...], gv, m_ref[...], v_ref[...])

    return pl.pallas_call(body, name=name, out_shape=[SDS(w.shape, F32)] * 4)(w, g, m, v)


def adamw_halves(name, w, g_own, g_sib, m, v, core):
    _, r, c = w.shape
    tm = _tile_rows(r, c, 10)

    def body(core_ref, w_ref, go_ref, gs_ref, m_ref, v_ref, g_out, d_out, m_out, v_out):
        gv = jnp.where(pl.program_id(0) == core_ref[0], go_ref[...], gs_ref[...])
        g_out[...] = gv
        d_out[...], m_out[...], v_out[...] = _adamw_math(w_ref[...], gv, m_ref[...], v_ref[...])

    full = pl.BlockSpec((None, tm, c), lambda h, i, core_ref: (h, i, 0))
    own = pl.BlockSpec((tm, c), lambda h, i, core_ref: (jnp.where(h == core_ref[0], i, 0), 0))
    sib = pl.BlockSpec((tm, c), lambda h, i, core_ref: (jnp.where(h == core_ref[0], 0, i), 0))
    grid_spec = pltpu.PrefetchScalarGridSpec(num_scalar_prefetch=1, grid=(2, r // tm),
                                             in_specs=[full, own, sib, full, full], out_specs=[full] * 4)
    return pl.pallas_call(body, name=name, grid_spec=grid_spec, out_shape=[SDS(w.shape, F32)] * 4,
                          compiler_params=_cparams(48 * 1024 * 1024, ("arbitrary", "arbitrary")))(core, w, g_own, g_sib, m, v)


def pack_rows(name, parts, rows):
    width = parts[0].shape[1]
    n = len(parts)

    def body(*refs):
        out_ref = refs[n]
        out_ref[...] = jnp.zeros_like(out_ref)
        off = 0
        for p in refs[:n]:
            out_ref[off:off + p.shape[0], :] = p[...]
            off += p.shape[0]

    return pl.pallas_call(body, name=name, out_shape=SDS((rows, width), F32))(*parts)


def _place():
    x, y, c = lax.axis_index("x"), lax.axis_index("y"), lax.axis_index("c")
    chips = [(1 - x, y), (x, 1 - y), (1 - x, 1 - y)]
    return x, y, c, chips


def _row_split(shape, dtype):
    r, c = shape
    n = 1
    while r % (2 * n) == 0 and (r // (2 * n)) % 16 == 0 and (r // n) * c * jnp.dtype(dtype).itemsize > PIECE_BYTES:
        n *= 2
    return [pl.ds(s * (r // n), r // n) for s in range(n)]


def _pieces(ref):
    *lead, r, c = ref.shape
    split = _row_split((r, c), ref.dtype)
    return [ref.at[(*idx, s)] for idx in itertools.product(*[range(d) for d in lead]) for s in split]


def gather_weights(shards):
    n = len(shards)
    units = [(a, p) for a in range(n) for p in range(len(_row_split(shards[a].shape[1:], shards[a].dtype)))]
    nu = len(units)

    def body(*refs):
        src, dst = refs[:n], refs[n:2 * n]
        send, recv = refs[2 * n:]
        x, y, c, chips = _place()
        me = 2 * x + y
        sib = (x, y, 1 - c)

        def rows(a, p):
            return _row_split(shards[a].shape[1:], shards[a].dtype)[p]

        def ici(u, j):
            a, p = units[u]
            cx, cy = chips[j]
            return pltpu.make_async_remote_copy(src[a].at[c, rows(a, p)], dst[a].at[me, c, rows(a, p)],
                                                send.at[6 * u + j], recv.at[6 * u + j], device_id=(cx, cy, c), device_id_type=MESH)

        def landed(u, j, half, k):
            a, p = units[u]
            cx, cy = chips[j]
            blk = dst[a].at[2 * cx + cy, half, rows(a, p)]
            return pltpu.make_async_remote_copy(blk, blk, send.at[6 * u + k], recv.at[6 * u + k], device_id=sib, device_id_type=MESH)

        window = 6
        for u in range(nu):
            if u >= window:
                for j in range(3):
                    ici(u - window, j).wait_send()
            for j in range(3):
                ici(u, j).start()
            if u >= window:
                for j in range(3):
                    ici(u - window, j).wait_recv()
                    landed(u - window, j, c, 3 + j).start()
        for u in range(max(nu - window, 0), nu):
            for j in range(3):
                ici(u, j).wait_send()
                ici(u, j).wait_recv()
                landed(u, j, c, 3 + j).start()
        for u in range(nu):
            for j in range(3):
                landed(u, j, 1 - c, 3 + j).wait_recv()
                landed(u, j, c, 3 + j).wait_send()

    got = pl.pallas_call(
        body, name="gather_weights", in_specs=[ANY] * n, out_specs=[ANY] * n,
        out_shape=[SDS((NCHIP,) + s.shape, s.dtype) for s in shards],
        scratch_shapes=[pltpu.SemaphoreType.DMA((6 * nu,)), pltpu.SemaphoreType.DMA((6 * nu,))],
    )(*shards)
    me = 2 * lax.axis_index("x") + lax.axis_index("y")
    return [lax.dynamic_update_index_in_dim(g, s, me, 0) for g, s in zip(got, shards)]


def exchange_halves(name, parts):
    n = len(parts)

    def body(*refs):
        src, got = refs[:n], refs[n:2 * n]
        send, recv = refs[2 * n:]
        x, y, c, _ = _place()
        sib = (x, y, 1 - c)
        for a in range(n):
            for sp, dp in zip(_pieces(src[a].at[1 - c]), _pieces(got[a])):
                pltpu.make_async_remote_copy(sp, dp, send.at[a], recv.at[a], device_id=sib, device_id_type=MESH).start()
        for a in range(n):
            pltpu.make_async_remote_copy(src[a].at[1 - c], got[a], send.at[a], recv.at[a], device_id=sib, device_id_type=MESH).wait()

    return pl.pallas_call(
        body, name=name, in_specs=[ANY] * n, out_specs=[ANY] * n,
        out_shape=[SDS(p.shape[1:], p.dtype) for p in parts],
        scratch_shapes=[pltpu.SemaphoreType.DMA((n,)), pltpu.SemaphoreType.DMA((n,))],
    )(*parts)


HBM = pl.BlockSpec(memory_space=pltpu.HBM)
SEM = pl.BlockSpec(memory_space=pltpu.SEMAPHORE)
EFFECT = pltpu.SideEffectType.DATAFLOW_SIDE_EFFECTING


def scatter_start(name, parts):
    n = len(parts)

    def body(*refs):
        src, land = refs[:n], refs[n:2 * n]
        send, recv = refs[2 * n], refs[2 * n + 1]
        token = refs[4 * n + 2]
        x, y, c, chips = _place()
        for a in range(n):
            for j, (cx, cy) in enumerate(chips):
                for sp, dp in zip(_pieces(src[a].at[2 * cx + cy]), _pieces(land[a].at[j])):
                    pltpu.make_async_remote_copy(sp, dp, send.at[3 * a + j], recv.at[3 * a + j],
                                                 device_id=(cx, cy, c), device_id_type=MESH).start()
        token[...] = jnp.zeros_like(token)

    lands = [pltpu.with_memory_space_constraint(lax.empty((NCHIP - 1,) + p.shape[1:], p.dtype), pltpu.HBM) for p in parts]
    srcs = [pltpu.with_memory_space_constraint(p, pltpu.HBM) for p in parts]
    outs = pl.pallas_call(
        body, name=name,
        out_shape=(pltpu.SemaphoreType.DMA((3 * n,)), pltpu.SemaphoreType.DMA((3 * n,)),
                   *[pltpu.HBM(p.shape, p.dtype) for p in parts], *[pltpu.HBM(l.shape, l.dtype) for l in lands],
                   SDS((8, 128), F32)),
        in_specs=[HBM] * (2 * n), out_specs=(SEM, SEM, *([HBM] * (2 * n)), pl.BlockSpec(memory_space=pltpu.VMEM)),
        input_output_aliases={i: 2 + i for i in range(2 * n)},
        compiler_params=pltpu.CompilerParams(has_side_effects=EFFECT),
    )(*srcs, *lands)
    return outs[0], outs[1], list(outs[2:2 + n]), list(outs[2 + n:2 + 2 * n]), outs[2 + 2 * n]


def scatter_wait(name, send, recv, parts, lands, after):
    n = len(parts)

    def body(*refs):
        src, land = refs[:n], refs[n:2 * n]
        send_ref, recv_ref = refs[2 * n], refs[2 * n + 1]
        x, y, c, chips = _place()
        for a in range(n):
            for j, (cx, cy) in enumerate(chips):
                cp = pltpu.make_async_remote_copy(src[a].at[2 * cx + cy], land[a].at[j], send_ref.at[3 * a + j], recv_ref.at[3 * a + j],
                                                  device_id=(cx, cy, c), device_id_type=MESH)
                cp.wait_send()
                cp.wait_recv()

    outs = pl.pallas_call(
        body, name=name,
        out_shape=(*[pltpu.HBM(p.shape, p.dtype) for p in parts], *[pltpu.HBM(l.shape, l.dtype) for l in lands]),
        in_specs=[HBM] * (2 * n) + [SEM, SEM, ANY], out_specs=[HBM] * (2 * n),
        input_output_aliases={i: i for i in range(2 * n)},
        compiler_params=pltpu.CompilerParams(has_side_effects=EFFECT),
    )(*parts, *lands, send, recv, after)
    return list(outs[:n]), list(outs[n:])


def join_halves(name, halves):
    n = len(halves)

    def body(*refs):
        src, dst = refs[:n], refs[n:2 * n]
        send, recv = refs[2 * n:]
        x, y, c, _ = _place()
        sib = (x, y, 1 - c)
        for a in range(n):
            for sp, dp in zip(_pieces(src[a]), _pieces(dst[a])):
                pltpu.make_async_remote_copy(sp, dp, send.at[a], recv.at[a], device_id=sib, device_id_type=MESH).start()
        for a in range(n):
            pltpu.make_async_remote_copy(src[a], dst[a], send.at[a], recv.at[a], device_id=sib, device_id_type=MESH).wait()

    return pl.pallas_call(
        body, name=name, in_specs=[ANY] * n, out_specs=[ANY] * n,
        out_shape=[SDS(h.shape, h.dtype) for h in halves],
        scratch_shapes=[pltpu.SemaphoreType.DMA((n,)), pltpu.SemaphoreType.DMA((n,))],
    )(*halves)


def gather_small(name, xs, reduce):
    m, ncol = xs.shape

    def body(x_ref, out_ref, all_ref, send, recv, lsem):
        x, y, c, chips = _place()
        me, sib = (x, y, c), (x, y, 1 - c)

        def rows(px, py, pc):
            return all_ref.at[pl.ds((4 * px + 2 * py + pc) * m, m), :]

        def copy(k, block, to, src=None):
            return pltpu.make_async_remote_copy(rows(*block) if src is None else src, rows(*block), send.at[k], recv.at[k],
                                                device_id=to, device_id_type=MESH)

        mine = pltpu.make_async_copy(x_ref, rows(*me), lsem)
        mine.start()
        first = [copy(0, me, sib, src=x_ref)] + [copy(1 + j, me, (*chip, c), src=x_ref) for j, chip in enumerate(chips)]
        for cp in first:
            cp.start()
        passed = [copy(4 + j, (*chip, c), sib) for j, chip in enumerate(chips)]
        for j, chip in enumerate(chips):
            copy(1 + j, (*chip, c), me).wait_recv()
            passed[j].start()
        copy(0, sib, me).wait_recv()
        for j, chip in enumerate(chips):
            copy(4 + j, (*chip, 1 - c), me).wait_recv()
        for cp in first + passed:
            cp.wait_send()
        mine.wait()
        if reduce:
            s = all_ref[0:m, :]
            for dev in range(1, 8):
                s = s + all_ref[dev * m:(dev + 1) * m, :]
            out_ref[...] = s
        else:
            out_ref[...] = all_ref[...]

    vm = pl.BlockSpec(memory_space=pltpu.VMEM)
    return pl.pallas_call(
        body, name=name, in_specs=[vm], out_specs=vm, out_shape=SDS((m, ncol) if reduce else (8 * m, ncol), F32),
        scratch_shapes=[pltpu.VMEM((8 * m, ncol), F32), pltpu.SemaphoreType.DMA((7,)), pltpu.SemaphoreType.DMA((7,)),
                        pltpu.SemaphoreType.DMA],
    )(xs)


def _to_cat(nat):
    pad = jnp.zeros(nat.shape[:-1] + (NCAT - OA - 16,), nat.dtype)
    return jnp.concatenate([nat[..., 3072:7168], nat[..., 7184:11280], nat[..., 0:3072], nat[..., 7168:7184], pad], axis=-1)


def _from_cat(cat):
    return jnp.concatenate([cat[..., OU:OA], cat[..., OV:OGP], cat[..., OA:OA + 16], cat[..., OGP:OU]], axis=-1)


def _pad_rows(a, rows):
    return jnp.concatenate([a, jnp.zeros((rows - a.shape[0],) + a.shape[1:], a.dtype)], axis=0)


def local_step(x2d, tgt, gf, g1, wcat, pw, pool_scale, wa_pad, b_alpha, ng, w_go, w_o, g2, w_up, w_dn, on_grad=None):
    emit = on_grad if on_grad is not None else (lambda group, grads: None)
    h1 = norm1(x2d, g1)
    pcat = mm_in(h1, wcat)
    dpool, ylin = pool_fwd(pcat, pw)
    og, o, states = gla_fwd(pcat, wa_pad, b_alpha, ng)
    mixed, ygla = mm_gla_out(og, w_go, ylin, pcat, pool_scale)
    x2, h2 = mm_out(mixed, w_o, x2d, g2)
    rup, act = mm_up(h2, w_up)
    dx3, dx3b, g_nf, loss_row = mm_down(act, w_dn, x2, tgt, gf)

    gw_down = mm_wgrad("mm_dw_down", act, dx3b, DFF, D, (2, NCHIP, D // 2, D), (None, None, 512, D),
                       lambda j, i, k: ((i // 2) % 2, i // 4, i % 2, 0), 512, D)
    token = emit("down", {"down": gw_down})
    dup = mm_dact(dx3b, w_dn, rup, after=token)
    dx2, dx2b, g_mlp = mm_dh2(dup, w_up, x2, dx3, g2)
    gw_up = mm_wgrad("mm_dw_up", h2, dup, D, DFF, (2, NCHIP, D // 2, D), (None, None, 512, D),
                     lambda j, i, k: (i // 2, j, i % 2, 0), 512, D)
    token = emit("up", {"up": gw_up})
    dylin, dygla, dlgp, dlgg, g_ps = mm_dmixed(dx2b, w_o, pcat, ylin, ygla, pool_scale, after=token)
    gw_out = mm_wgrad("mm_dw_out", mixed, dx2b, D, D, (2, NCHIP, 256, D), (None, None, 256, D),
                      lambda j, i, k: (i % 2, i // 2, 0, 0), 256, D)
    do, dg, g_ng = mm_dog(dygla, w_go, o, pcat, ng)
    gw_go = mm_wgrad("mm_dw_gla_out", og, dygla, D, D, (2, NCHIP, 256, D), (None, None, 256, D),
                     lambda j, i, k: (i % 2, i // 2, 0, 0), 256, D)
    token = emit("mix", {"out": gw_out, "gla_out": gw_go})
    dq, dk, dv, dalow, g_wa, g_ba = gla_bwd(do, pcat, states, wa_pad, b_alpha, b_alpha if token is None else token)
    du, dpw = pool_bwd(dylin, dpool, pw)
    dpcat = jnp.concatenate([dv, dg, dlgp, dlgg, du, dq, dk, dalow, jnp.zeros((T, NCAT - OA - APAD), BF16)], axis=1)
    gw_cat = mm_wgrad("mm_dw_in", h1, dpcat, D, NCAT, (D, NCAT), (512, 1280), lambda j, i, k: (i, j), 512, 1280)
    token = emit("in", {"in_cat": gw_cat, "pool": dpw})
    grad_x, g_mix = mm_dh1(dpcat, wcat, x2d, dx2, g1, after=token)
    return (loss_row[0, 0], grad_x, g_mix, g_ps, g_mlp, g_nf, g_ng, g_ba, g_wa, token,
            gw_cat, dpw, gw_go, gw_out, gw_up, gw_down)


def kernel(x, norm_mix_g, w_in, pool_w, pool_scale, w_alpha, b_alpha, gla_norm_g, w_gla_out, w_out, norm_mlp_g, w_mlp_up, w_mlp_down, norm_final_g, loss_target, m_norm_mix_g, m_w_in, m_pool_w, m_pool_scale, m_w_alpha, m_b_alpha, m_gla_norm_g, m_w_gla_out, m_w_out, m_norm_mlp_g, m_w_mlp_up, m_w_mlp_down, m_norm_final_g, v_norm_mix_g, v_w_in, v_pool_w, v_pool_scale, v_w_alpha, v_b_alpha, v_gla_norm_g, v_w_gla_out, v_w_out, v_norm_mlp_g, v_w_mlp_up, v_w_mlp_down, v_norm_final_g):
    chip = 2 * lax.axis_index("x") + lax.axis_index("y")
    chip_i = chip.astype(jnp.int32).reshape(1)
    core_i = lax.axis_index("c").astype(jnp.int32).reshape(1)
    x2d = x.reshape(T, D)
    tgt = loss_target.reshape(T, D)
    gf = norm_final_g.reshape(1, D)

    def halves(w2d):
        r, c = w2d.shape
        return w2d.astype(BF16).reshape(2, r // 2, c)

    pool_shard = pool_w.reshape(4 * PG, PO // NCHIP)
    big = [w_in[0], w_gla_out[0], w_out[0], w_mlp_up[0], w_mlp_down[0], pool_shard]
    g_in, g_go, g_out, g_up, g_down, g_pool = gather_weights([halves(w) for w in big])
    wcat = _to_cat(jnp.concatenate([g_in[j].reshape(D, IN_SHARD) for j in range(NCHIP)], axis=1))
    w_go = g_go.reshape(D, D)
    w_o = g_out.reshape(D, D)
    w_up = g_up.reshape(NCHIP, D, D)
    w_dn = g_down.reshape(DFF, D)
    pw = jnp.concatenate([g_pool[j].reshape(4, PG, PO // NCHIP) for j in range(NCHIP)], axis=2)

    small_w = pack_rows("pack_small_w", [w_alpha[0].reshape(4, QK),
                                         jnp.concatenate([gla_norm_g[0].reshape(1, 512), jnp.zeros((1, 512), F32)], axis=1)], 8)
    sw_all = gather_small("gather_small_w", small_w, False).reshape(8, 8, QK)
    wa_full = jnp.concatenate([sw_all[2 * j, 0:4].reshape(16, DK) for j in range(NCHIP)], axis=1)
    ng_full = jnp.concatenate([sw_all[2 * j, 4, 0:512].reshape(HEADS, DV // NCHIP) for j in range(NCHIP)], axis=1)
    wa_pad = _pad_rows(wa_full, APAD).astype(BF16)
    ng = ng_full.reshape(1, D)

    pending = {}

    def on_grad(group, grads):
        if group == "in":
            nat = _from_cat(grads["in_cat"])
            gw_in = jnp.stack([nat[:, j * IN_SHARD:(j + 1) * IN_SHARD].reshape(2, D // 2, IN_SHARD)
                               for j in range(NCHIP)], axis=1)
            gw_pool = jnp.stack([grads["pool"][:, :, j * 128:(j + 1) * 128].reshape(2, 2 * PG, 128)
                                 for j in range(NCHIP)], axis=1)
            grads = {"in": gw_in, "pool": gw_pool}
        nms, parts = list(grads.keys()), list(grads.values())
        theirs = exchange_halves("exchange_" + group, parts)
        sums = [add_pairs("add_pair_" + nm, a, b, core_i) for nm, a, b in zip(nms, parts, theirs)]
        send, recv, sums, lands, token = scatter_start("scatter_start_" + group, sums)
        pending[group] = (nms, send, recv, sums, lands)
        return token

    (loss_local, grad_x, g_mix, g_ps, g_mlp, g_nf, g_ng, g_ba, g_wa, last_token) = local_step(
        x2d, tgt, gf, norm_mix_g, wcat, pw, pool_scale, wa_pad, b_alpha, ng, w_go, w_o, norm_mlp_g, w_up, w_dn, on_grad)[:10]
    loss = lax.psum(loss_local, ("x", "y", "c"))

    wmv = {"in": (big[0], m_w_in, v_w_in), "gla_out": (big[1], m_w_gla_out, v_w_gla_out), "out": (big[2], m_w_out, v_w_out),
           "up": (big[3], m_w_mlp_up, v_w_mlp_up), "down": (big[4], m_w_mlp_down, v_w_mlp_down), "pool": (big[5], m_pool_w, v_pool_w)}
    big_res = {}

    def finish(group, after):
        nms, send, recv, sums, lands = pending[group]
        sums, lands = scatter_wait("scatter_wait_" + group, send, recv, sums, lands, after)
        reduced = [sum_chips("sum_chips_" + nm, a, b, chip_i) for nm, a, b in zip(nms, sums, lands)]
        from_sib = join_halves("join_" + group, reduced)
        for nm, g_own, g_sib in zip(nms, reduced, from_sib):
            w, m, v = wmv[nm]
            shp = (2,) + g_own.shape
            big_res[nm] = adamw_halves("adamw_" + nm, w.reshape(shp), g_own, g_sib, m.reshape(shp), v.reshape(shp), core_i)

    for group in ("down", "up", "mix"):
        finish(group, last_token)
    finish("in", grad_x)

    ROWS = 16

    def wide(a, n):
        return jnp.concatenate([a.reshape(1, n), jnp.zeros((1, D - n), F32)], axis=1)

    packed = pack_rows("pack_small_g", [g_mix, g_ps, g_mlp, g_nf, g_ng, wide(g_ba, QK), g_wa[0:16].reshape(8, D)], ROWS)
    tot = gather_small("reduce_small_g", packed, True)
    t_wa = lax.dynamic_slice(tot[6:14].reshape(16, QK), (0, chip * DK), (16, DK))
    t_ng = lax.dynamic_slice(tot[4].reshape(HEADS, DV), (0, chip * (DV // NCHIP)), (HEADS, DV // NCHIP))

    def pack_small(nm, mix, ps, mlp, nf, ba, wa, gn):
        return pack_rows(nm, [mix.reshape(1, D), ps.reshape(1, D), mlp.reshape(1, D), nf.reshape(1, D), wide(ba, QK),
                              wa.reshape(2, D), wide(gn, 512)], ROWS)

    sg = pack_small("pack_g", tot[0], tot[1], tot[2], tot[3], tot[5, 0:QK], t_wa, t_ng)
    sw = pack_small("pack_w", norm_mix_g, pool_scale, norm_mlp_g, norm_final_g, b_alpha, w_alpha, gla_norm_g)
    sm = pack_small("pack_m", m_norm_mix_g, m_pool_scale, m_norm_mlp_g, m_norm_final_g, m_b_alpha, m_w_alpha, m_gla_norm_g)
    sv = pack_small("pack_v", v_norm_mix_g, v_pool_scale, v_norm_mlp_g, v_norm_final_g, v_b_alpha, v_w_alpha, v_gla_norm_g)
    small_res = adamw("adamw_small", sw, sg, sm, sv)

    def unpack(p):
        return {"norm_mix_g": p[0].reshape(1, D), "pool_scale": p[1].reshape(1, D), "norm_mlp_g": p[2].reshape(1, D),
                "norm_final_g": p[3].reshape(D), "b_alpha": p[4, 0:QK].reshape(1, QK), "w_alpha": p[5:7].reshape(1, 16, DK),
                "gla_norm_g": p[7, 0:512].reshape(1, HEADS, DV // NCHIP)}

    order = ["norm_mix_g", "w_in", "pool_w", "pool_scale", "w_alpha", "b_alpha", "gla_norm_g", "w_gla_out", "w_out",
             "norm_mlp_g", "w_mlp_up", "w_mlp_down", "norm_final_g"]
    big_key = {"w_in": ("in", w_in.shape), "pool_w": ("pool", pool_w.shape), "w_gla_out": ("gla_out", w_gla_out.shape),
               "w_out": ("out", w_out.shape), "w_mlp_up": ("up", w_mlp_up.shape), "w_mlp_down": ("down", w_mlp_down.shape)}
    result = [loss, grad_x.reshape(1, T, D)]
    for kind in range(4):
        small = unpack(small_res[kind])
        for nm in order:
            if nm in big_key:
                key, shp = big_key[nm]
                result.append(big_res[key][kind].reshape(shp))
            else:
                result.append(small[nm])
    return tuple(result)
```

```python
import itertools

import jax
import jax.numpy as jnp
from jax import lax
from jax.experimental import pallas as pl
from jax.experimental.pallas import tpu as pltpu

F32 = jnp.float32
BF16 = jnp.bfloat16
SDS = jax.ShapeDtypeStruct
MESH = pl.DeviceIdType.MESH
ANY = pl.BlockSpec(memory_space=pl.ANY)

T = 2048
D = 2048
DFF = 8192
NCHIP = 4
IN_WIDTH = 11280
IN_SHARD = IN_WIDTH // NCHIP
CHUNK = 64
NCHUNK = T // CHUNK
HEADS = 4
DK = 256
DV = 512
QK = HEADS * DK
EPS = 1e-6
POOL_WINDOWS = (2, 4, 8, 16)
PG = 256
PO = 512

OV, OG, OGP, OGG, OU, OQ, OKK, OA = 0, 2048, 4096, 6144, 8192, 9216, 10240, 11264
NCAT = 11520
APAD = 128

VMEM_CAP = 56 * 1024 * 1024

PIECE_BYTES = 3 * 512 * 1024

ADAM_LR, ADAM_B1, ADAM_B2, ADAM_EPS, ADAM_WD, ADAM_STEP = 0.001, 0.9, 0.999, 1e-08, 0.01, 10


def _cparams(vmem_bytes=None, sem=None):
    kw = {}
    if vmem_bytes is not None:
        kw["vmem_limit_bytes"] = int(min(max(vmem_bytes, 32 * 1024 * 1024), VMEM_CAP))
    if sem is not None:
        kw["dimension_semantics"] = sem
    return pltpu.CompilerParams(**kw)


def _nbytes(shape, dtype):
    n = 1
    for s in shape:
        if s is not None:
            n *= s
    return n * jnp.dtype(dtype).itemsize


def _sigmoid(x):
    return 1.0 / (1.0 + jnp.exp(-x))


def matmul(name, a, b, *, a_spec, b_spec, cdims, grid, acc_shape, outs, extras=(), epi, after=None):
    nj, ni, nk = grid
    ne, no = len(extras), len(outs)
    first_out = 2 + ne + (0 if after is None else 1)

    def body(*refs):
        a_ref, b_ref = refs[0], refs[1]
        ex = refs[2:2 + ne]
        out_refs = refs[first_out:first_out + no]
        i = pl.program_id(1)
        part = lax.dot_general(a_ref[...], b_ref[...], (cdims, ((), ())), preferred_element_type=F32)
        if nk == 1:
            epi(part, ex, out_refs, i)
        else:
            acc_ref = refs[first_out + no]
            k = pl.program_id(2)

            @pl.when(k == 0)
            def _():
                acc_ref[...] = part

            @pl.when(k > 0)
            def _():
                acc_ref[...] += part

            @pl.when(k == nk - 1)
            def _():
                epi(acc_ref[...], ex, out_refs, i)

    in_specs = [pl.BlockSpec(*a_spec), pl.BlockSpec(*b_spec)] + [pl.BlockSpec(bs, im) for _, bs, im in extras]
    in_specs += [] if after is None else [ANY]
    out_specs = [pl.BlockSpec(bs, im) for _, _, bs, im in outs]
    out_shape = [SDS(s, dt) for s, dt, _, _ in outs]
    vm = 2 * (_nbytes(a_spec[0], a.dtype) + _nbytes(b_spec[0], b.dtype))
    vm += 2 * sum(_nbytes(bs, arr.dtype) for arr, bs, _ in extras)
    vm += 2 * sum(_nbytes(bs, dt) for _, dt, bs, _ in outs)
    vm += 6 * _nbytes(acc_shape, F32)
    scratch = [pltpu.VMEM(acc_shape, F32)] if nk > 1 else []
    return pl.pallas_call(
        body, name=name, grid=grid, in_specs=in_specs, out_specs=out_specs, out_shape=out_shape,
        scratch_shapes=scratch,
        compiler_params=_cparams(vm, ("arbitrary", "arbitrary", "arbitrary")),
    )(a, b, *[arr for arr, _, _ in extras], *([] if after is None else [after]))


NN =((1,), (0,))
NT = ((1,), (1,))
TN = ((0,), (0,))


def _row_acc(out_ref, val, i):
    @pl.when(i == 0)
    def _():
        out_ref[...] = val

    @pl.when(i > 0)
    def _():
        out_ref[...] += val


def _rms_bwd(xn, r, dxn):
    return r * (dxn - xn * jnp.mean(dxn * xn, axis=-1, keepdims=True))


def norm1(x, g):
    tm = 256

    def body(x_ref, g_ref, h_ref):
        xv = x_ref[...]
        r = lax.rsqrt(jnp.mean(xv * xv, axis=-1, keepdims=True) + EPS)
        h_ref[...] = (xv * r * g_ref[...]).astype(BF16)

    return pl.pallas_call(
        body, name="norm1", grid=(T // tm,),
        in_specs=[pl.BlockSpec((tm, D), lambda i: (i, 0)), pl.BlockSpec((1, D), lambda i: (0, 0))],
        out_specs=pl.BlockSpec((tm, D), lambda i: (i, 0)), out_shape=SDS((T, D), BF16),
        compiler_params=_cparams(32 * 1024 * 1024, ("arbitrary",)),
    )(x, g)


def mm_in(h1, wcat):
    tm, tn = 512, 1280

    def epi(acc, ex, outs, i):
        outs[0][...] = acc.astype(BF16)

    return matmul("mm_in", h1, wcat, a_spec=((tm, D), lambda j, i, k: (i, 0)), b_spec=((D, tn), lambda j, i, k: (0, j)),
                  cdims=NN, grid=(NCAT // tn, T // tm, 1), acc_shape=(tm, tn),
                  outs=[((T, NCAT), BF16, (tm, tn), lambda j, i, k: (i, j))], epi=epi)[0]


def _window_sum(x, w, up):
    n = x.shape[0]
    row = lax.broadcasted_iota(jnp.int32, x.shape, 0)
    s, sh = x, 1
    while sh < w:
        if up:
            s = s + jnp.where(row < n - sh, pltpu.roll(s, n - sh, axis=0), 0.0)
        else:
            s = s + jnp.where(row >= sh, pltpu.roll(s, sh, axis=0), 0.0)
        sh *= 2
    return s


def _inv_count(shape, w):
    row = lax.broadcasted_iota(jnp.int32, shape, 0)
    return 1.0 / jnp.minimum(row + 1, w).astype(F32)


def pool_fwd(pcat, pw):
    def body(u_ref, pw_ref, d_ref, y_ref):
        for gi, w in enumerate(POOL_WINDOWS):
            ug = u_ref[:, gi * PG:(gi + 1) * PG].astype(F32)
            dg = _window_sum(ug, w, False) * _inv_count(ug.shape, w) - ug
            db = dg.astype(BF16)
            d_ref[:, gi * PG:(gi + 1) * PG] = db
            y_ref[:, gi * PO:(gi + 1) * PO] = jnp.dot(db, pw_ref[gi], preferred_element_type=F32).astype(BF16)

    return pl.pallas_call(
        body, name="pool_fwd", grid=(1,),
        in_specs=[pl.BlockSpec((T, 4 * PG), lambda i: (0, OU // (4 * PG))), pl.BlockSpec((4, PG, PO), lambda i: (0, 0, 0))],
        out_specs=[pl.BlockSpec((T, 4 * PG), lambda i: (0, 0)), pl.BlockSpec((T, D), lambda i: (0, 0))],
        out_shape=[SDS((T, 4 * PG), BF16), SDS((T, D), BF16)],
        compiler_params=_cparams(48 * 1024 * 1024, ("arbitrary",)),
    )(pcat, pw)


def pool_bwd(dylin, d, pw):
    def body(dy_ref, d_ref, pw_ref, du_ref, dpw_ref):
        for gi, w in enumerate(POOL_WINDOWS):
            dyl = dy_ref[:, gi * PO:(gi + 1) * PO]
            dd = lax.dot_general(dyl, pw_ref[gi], (NT, ((), ())), preferred_element_type=F32)
            du = _window_sum(dd * _inv_count(dd.shape, w), w, True) - dd
            du_ref[:, gi * PG:(gi + 1) * PG] = du.astype(BF16)
            dpw_ref[gi] = lax.dot_general(d_ref[:, gi * PG:(gi + 1) * PG], dyl, (TN, ((), ())),
                                          preferred_element_type=F32).astype(BF16)

    return pl.pallas_call(
        body, name="pool_bwd", grid=(1,),
        in_specs=[pl.BlockSpec((T, D), lambda i: (0, 0)), pl.BlockSpec((T, 4 * PG), lambda i: (0, 0)),
                  pl.BlockSpec((4, PG, PO), lambda i: (0, 0, 0))],
        out_specs=[pl.BlockSpec((T, 4 * PG), lambda i: (0, 0)), pl.BlockSpec((4, PG, PO), lambda i: (0, 0, 0))],
        out_shape=[SDS((T, 4 * PG), BF16), SDS((4, PG, PO), BF16)],
        compiler_params=_cparams(48 * 1024 * 1024, ("arbitrary",)),
    )(dylin, d, pw)


def _gate_decay(alow, wa, ba):
    a = jnp.dot(alow, wa, preferred_element_type=F32) + ba
    ls = jax.nn.log_sigmoid(a) * (1.0 / 16.0)
    r = lax.broadcasted_iota(jnp.int32, (CHUNK, CHUNK), 0)
    c = lax.broadcasted_iota(jnp.int32, (CHUNK, CHUNK), 1)
    tri = jnp.where(c <= r, 1.0, 0.0).astype(F32)
    cum = jnp.dot(tri, ls, preferred_element_type=F32, precision=lax.Precision.HIGHEST)
    last = cum[CHUNK - 1:CHUNK, :]
    return a, jnp.exp(last - cum), jnp.exp(last)


def gla_fwd(pcat, wa, ba, ng):
    def body(q_ref, k_ref, v_ref, g_ref, al_ref, wa_ref, ba_ref, ng_ref, og_ref, o_ref, st_ref, s_scr):
        @pl.when(pl.program_id(0) == 0)
        def _():
            s_scr[...] = jnp.zeros_like(s_scr)

        _, e, decay = _gate_decay(al_ref[...], wa_ref[...], ba_ref[...])
        kd = (k_ref[...].astype(F32) * e).astype(BF16)
        qs = (q_ref[...].astype(F32) * (DK ** -0.5)).astype(BF16)
        for h in range(HEADS):
            ck = slice(h * DK, (h + 1) * DK)
            cv = slice(h * DV, (h + 1) * DV)
            s_new = s_scr[h] * decay[:, ck] + lax.dot_general(v_ref[:, cv], kd[:, ck], (TN, ((), ())),
                                                               preferred_element_type=F32)
            s_scr[h] = s_new
            sb = s_new.astype(BF16)
            st_ref[h] = sb
            oh = lax.dot_general(qs[:, ck], sb, (NT, ((), ())), preferred_element_type=F32)
            o_ref[:, cv] = oh.astype(BF16)
            on = oh * lax.rsqrt(jnp.mean(oh * oh, axis=-1, keepdims=True) + EPS) * ng_ref[:, cv]
            gv = g_ref[:, cv].astype(F32)
            og_ref[:, cv] = (on * (gv * _sigmoid(gv))).astype(BF16)

    row = lambda c: (c, 0)
    return pl.pallas_call(
        body, name="gla_fwd", grid=(NCHUNK,),
        in_specs=[pl.BlockSpec((CHUNK, QK), lambda c: (c, OQ // QK)), pl.BlockSpec((CHUNK, QK), lambda c: (c, OKK // QK)),
                  pl.BlockSpec((CHUNK, D), lambda c: (c, OV // D)), pl.BlockSpec((CHUNK, D), lambda c: (c, OG // D)),
                  pl.BlockSpec((CHUNK, APAD), lambda c: (c, OA // APAD)),
                  pl.BlockSpec((APAD, QK), lambda c: (0, 0)), pl.BlockSpec((1, QK), lambda c: (0, 0)),
                  pl.BlockSpec((1, D), lambda c: (0, 0))],
        out_specs=[pl.BlockSpec((CHUNK, D), row), pl.BlockSpec((CHUNK, D), row),
                   pl.BlockSpec((None, HEADS, DV, DK), lambda c: (c, 0, 0, 0))],
        out_shape=[SDS((T, D), BF16), SDS((T, D), BF16), SDS((NCHUNK, HEADS, DV, DK), BF16)],
        scratch_shapes=[pltpu.VMEM((HEADS, DV, DK), F32)],
        compiler_params=_cparams(32 * 1024 * 1024, ("arbitrary",)),
    )(pcat, pcat, pcat, pcat, pcat, wa, ba, ng)


def gla_bwd(do, pcat, states, wa, ba, after):
    def body(do_ref, q_ref, k_ref, v_ref, al_ref, sc_ref, sp_ref, wa_ref, ba_ref, after_ref,
             dq_ref, dk_ref, dv_ref, dal_ref, dwa_ref, dba_ref, ds_scr):
        i = pl.program_id(0)

        @pl.when(i == 0)
        def _():
            ds_scr[...] = jnp.zeros_like(ds_scr)

        has_prev = jnp.where(i < NCHUNK - 1, 1.0, 0.0).astype(F32)
        a, e, decay = _gate_decay(al_ref[...], wa_ref[...], ba_ref[...])
        kf = k_ref[...].astype(F32)
        kdf = kf * e
        kd = kdf.astype(BF16)
        qs = (q_ref[...].astype(F32) * (DK ** -0.5)).astype(BF16)
        dkd_parts, ddecay_parts = [], []
        for h in range(HEADS):
            ck = slice(h * DK, (h + 1) * DK)
            cv = slice(h * DV, (h + 1) * DV)
            doh = do_ref[:, cv]
            ds = ds_scr[h] + lax.dot_general(doh, qs[:, ck], (TN, ((), ())), preferred_element_type=F32)
            dsb = ds.astype(BF16)
            dq_ref[:, ck] = (jnp.dot(doh, sc_ref[h], preferred_element_type=F32) * (DK ** -0.5)).astype(BF16)
            dkd_parts.append(jnp.dot(v_ref[:, cv], dsb, preferred_element_type=F32))
            dv_ref[:, cv] = lax.dot_general(kd[:, ck], dsb, (NT, ((), ())), preferred_element_type=F32).astype(BF16)
            ddecay_parts.append(jnp.sum(ds * sp_ref[h].astype(F32), axis=0, keepdims=True) * has_prev)
            ds_scr[h] = ds * decay[:, ck]
        dkd = jnp.concatenate(dkd_parts, axis=1)
        ddecay = jnp.concatenate(ddecay_parts, axis=1)
        dk_ref[...] = (dkd * e).astype(BF16)
        dearg = dkd * kdf
        dlast = jnp.sum(dearg, axis=0, keepdims=True) + ddecay * decay
        r = lax.broadcasted_iota(jnp.int32, (CHUNK, CHUNK), 0)
        c = lax.broadcasted_iota(jnp.int32, (CHUNK, CHUNK), 1)
        triu = jnp.where(c >= r, 1.0, 0.0).astype(F32)
        dls = dlast - jnp.dot(triu, dearg, preferred_element_type=F32, precision=lax.Precision.HIGHEST)
        da = dls * (1.0 / 16.0) * (1.0 - _sigmoid(a))
        dab = da.astype(BF16)
        dal_ref[...] = lax.dot_general(dab, wa_ref[...], (NT, ((), ())), preferred_element_type=F32).astype(BF16)
        dwa = lax.dot_general(al_ref[...], dab, (TN, ((), ())), preferred_element_type=F32)
        dba = jnp.sum(da, axis=0, keepdims=True)

        @pl.when(i == 0)
        def _():
            dwa_ref[...] = dwa
            dba_ref[...] = dba

        @pl.when(i > 0)
        def _():
            dwa_ref[...] += dwa
            dba_ref[...] += dba

    rev = lambda i: NCHUNK - 1 - i
    return pl.pallas_call(
        body, name="gla_bwd", grid=(NCHUNK,),
        in_specs=[pl.BlockSpec((CHUNK, D), lambda i: (rev(i), 0)),
                  pl.BlockSpec((CHUNK, QK), lambda i: (rev(i), OQ // QK)), pl.BlockSpec((CHUNK, QK), lambda i: (rev(i), OKK // QK)),
                  pl.BlockSpec((CHUNK, D), lambda i: (rev(i), OV // D)), pl.BlockSpec((CHUNK, APAD), lambda i: (rev(i), OA // APAD)),
                  pl.BlockSpec((None, HEADS, DV, DK), lambda i: (rev(i), 0, 0, 0)),
                  pl.BlockSpec((None, HEADS, DV, DK), lambda i: (jnp.maximum(rev(i) - 1, 0), 0, 0, 0)),
                  pl.BlockSpec((APAD, QK), lambda i: (0, 0)), pl.BlockSpec((1, QK), lambda i: (0, 0)), ANY],
        out_specs=[pl.BlockSpec((CHUNK, QK), lambda i: (rev(i), 0)), pl.BlockSpec((CHUNK, QK), lambda i: (rev(i), 0)),
                   pl.BlockSpec((CHUNK, D), lambda i: (rev(i), 0)), pl.BlockSpec((CHUNK, APAD), lambda i: (rev(i), 0)),
                   pl.BlockSpec((APAD, QK), lambda i: (0, 0)), pl.BlockSpec((1, QK), lambda i: (0, 0))],
        out_shape=[SDS((T, QK), BF16), SDS((T, QK), BF16), SDS((T, D), BF16), SDS((T, APAD), BF16),
                   SDS((APAD, QK), F32), SDS((1, QK), F32)],
        scratch_shapes=[pltpu.VMEM((HEADS, DV, DK), F32)],
        compiler_params=_cparams(32 * 1024 * 1024, ("arbitrary",)),
    )(do, pcat, pcat, pcat, pcat, states, states, wa, ba, after)


TMF = 256
_rowblk = ((TMF, D), lambda j, i, k: (i, 0))
_vec = ((1, D), lambda j, i, k: (0, 0))


def _full_spec(col):
    return ((TMF, D), lambda j, i, k: (i, col))


TBIG = 1024


def square_matmul(name, a, b, *, a_spec, b_spec, cdims, nk, after=None):
    def epi(acc, ex, outs, i):
        outs[0][...] = acc

    return matmul(name, a, b, a_spec=a_spec, b_spec=b_spec, cdims=cdims, grid=(D // TBIG, T // TBIG, nk),
                  acc_shape=(TBIG, TBIG), outs=[((T, D), F32, (TBIG, TBIG), lambda j, i, k: (i, j))], epi=epi,
                  after=after)[0]


def rowwise(name, y, *, extras, outs, epi):
    ne = len(extras)

    def body(*refs):
        epi(refs[0][...], refs[1:1 + ne], refs[1 + ne:], pl.program_id(1))

    in_specs = [pl.BlockSpec(*_rowblk)] + [pl.BlockSpec(bs, im) for _, bs, im in extras]
    return pl.pallas_call(
        body, name=name, grid=(1, T // TMF, 1), in_specs=in_specs,
        out_specs=[pl.BlockSpec(bs, im) for _, _, bs, im in outs], out_shape=[SDS(s, dt) for s, dt, _, _ in outs],
        compiler_params=_cparams(40 * 1024 * 1024, ("arbitrary", "arbitrary", "arbitrary")),
    )(y, *[arr for arr, _, _ in extras])


def mm_gla_out(og, w, ylin, pcat, pscale):
    def epi(acc, ex, outs, i):
        ylin_ref, lgp_ref, lgg_ref, ps_ref = ex
        gp = _sigmoid(lgp_ref[...].astype(F32))
        gg = _sigmoid(lgg_ref[...].astype(F32))
        outs[0][...] = (gp * (ylin_ref[...].astype(F32) * ps_ref[...]) + gg * acc).astype(BF16)
        outs[1][...] = acc.astype(BF16)

    return matmul("mm_gla_out", og, w, a_spec=_rowblk, b_spec=((D, D), lambda j, i, k: (0, 0)), cdims=NN,
                  grid=(1, T // TMF, 1), acc_shape=(TMF, D),
                  extras=[(ylin, *_rowblk), (pcat, *_full_spec(OGP // D)), (pcat, *_full_spec(OGG // D)), (pscale, *_vec)],
                  outs=[((T, D), BF16, *_rowblk), ((T, D), BF16, *_rowblk)], epi=epi)


def mm_out(mixed, w, x, g2):
    def epi(acc, ex, outs, i):
        x_ref, g_ref = ex
        x2 = x_ref[...] + acc
        r = lax.rsqrt(jnp.mean(x2 * x2, axis=-1, keepdims=True) + EPS)
        outs[0][...] = x2
        outs[1][...] = (x2 * r * g_ref[...]).astype(BF16)

    return matmul("mm_out", mixed, w, a_spec=_rowblk, b_spec=((D, D), lambda j, i, k: (0, 0)), cdims=NN,
                  grid=(1, T // TMF, 1), acc_shape=(TMF, D), extras=[(x, *_rowblk), (g2, *_vec)],
                  outs=[((T, D), F32, *_rowblk), ((T, D), BF16, *_rowblk)], epi=epi)


def mm_up(h2, wup):
    def epi(acc, ex, outs, i):
        r = jnp.maximum(acc, 0.0)
        outs[0][...] = r.astype(BF16)
        outs[1][...] = (r * r).astype(BF16)

    blk = ((TMF, D), lambda j, i, k: (i, j))
    return matmul("mm_up", h2, wup, a_spec=_rowblk, b_spec=((None, D, D), lambda j, i, k: (j, 0, 0)), cdims=NN,
                  grid=(NCHIP, T // TMF, 1), acc_shape=(TMF, D),
                  outs=[((T, DFF), BF16, *blk), ((T, DFF), BF16, *blk)], epi=epi)


def mm_down(act, wdown, x2, tgt, gf):
    tk = 2048

    def epi(acc, ex, outs, i):
        x2_ref, t_ref, g_ref = ex
        dx_ref, dxb_ref, gnf_ref, loss_ref = outs
        x3 = x2_ref[...] + acc
        r = lax.rsqrt(jnp.mean(x3 * x3, axis=-1, keepdims=True) + EPS)
        xn = x3 * r
        err = xn * g_ref[...] - t_ref[...]
        lsum = 0.5 * jnp.sum(jnp.mean(err * err, axis=-1, keepdims=True), axis=0, keepdims=True)
        dy = err * (1.0 / D)
        _row_acc(gnf_ref, jnp.sum(dy * xn, axis=0, keepdims=True), i)
        _row_acc(loss_ref, jnp.broadcast_to(lsum, (1, 128)), i)
        dx3 = _rms_bwd(xn, r, dy * g_ref[...])
        dx_ref[...] = dx3
        dxb_ref[...] = dx3.astype(BF16)

    y = square_matmul("mm_down", act, wdown, a_spec=((TBIG, tk), lambda j, i, k: (i, k)),
                      b_spec=((tk, TBIG), lambda j, i, k: (k, j)), cdims=NN, nk=DFF // tk)
    return rowwise("rows_final", y, extras=[(x2, *_rowblk), (tgt, *_rowblk), (gf, *_vec)],
                   outs=[((T, D), F32, *_rowblk), ((T, D), BF16, *_rowblk), ((1, D), F32, *_vec),
                         ((1, 128), F32, (1, 128), lambda j, i, k: (0, 0))], epi=epi)


def mm_dact(dx3b, wdown, rup, after=None):
    def epi(acc, ex, outs, i):
        outs[0][...] = (acc * 2.0 * ex[0][...].astype(F32)).astype(BF16)

    blk = ((TMF, D), lambda j, i, k: (i, j))
    return matmul("mm_dact", dx3b, wdown, a_spec=_rowblk, b_spec=((D, D), lambda j, i, k: (j, 0)), cdims=NT,
                  grid=(DFF // D, T // TMF, 1), acc_shape=(TMF, D), extras=[(rup, *blk)],
                  outs=[((T, DFF), BF16, *blk)], epi=epi, after=after)[0]


def mm_wgrad(name, a, b, m, n, out_shape, out_block, out_map, tm, tn):
    def epi(acc, ex, outs, i):
        outs[0][...] = acc.astype(BF16)

    return matmul(name, a, b, a_spec=((T, tm), lambda j, i, k: (0, i)), b_spec=((T, tn), lambda j, i, k: (0, j)),
                  cdims=TN, grid=(n // tn, m // tm, 1), acc_shape=(tm, tn),
                  outs=[(out_shape, BF16, out_block, out_map)], epi=epi)[0]


def mm_dh2(dup, wup, x2, dx3, g2):
    def epi(acc, ex, outs, i):
        x2_ref, dx3_ref, g_ref = ex
        x2 = x2_ref[...]
        r = lax.rsqrt(jnp.mean(x2 * x2, axis=-1, keepdims=True) + EPS)
        xn = x2 * r
        _row_acc(outs[2], jnp.sum(acc * xn, axis=0, keepdims=True), i)
        dx2 = dx3_ref[...] + _rms_bwd(xn, r, acc * g_ref[...])
        outs[0][...] = dx2
        outs[1][...] = dx2.astype(BF16)

    y = square_matmul("mm_dh2", dup, wup, a_spec=((TBIG, D), lambda j, i, k: (i, k)),
                      b_spec=((None, TBIG, D), lambda j, i, k: (k, j, 0)), cdims=NT, nk=NCHIP)
    return rowwise("rows_dh2", y, extras=[(x2, *_rowblk), (dx3, *_rowblk), (g2, *_vec)],
                   outs=[((T, D), F32, *_rowblk), ((T, D), BF16, *_rowblk), ((1, D), F32, *_vec)], epi=epi)


def mm_dmixed(dx2b, wout, pcat, ylin, ygla, pscale, after=None):
    def epi(acc, ex, outs, i):
        lgp_ref, lgg_ref, ylin_ref, ygla_ref, ps_ref = ex
        gp = _sigmoid(lgp_ref[...].astype(F32))
        gg = _sigmoid(lgg_ref[...].astype(F32))
        yl = ylin_ref[...].astype(F32)
        ps = ps_ref[...]
        agp = acc * gp
        outs[0][...] = (agp * ps).astype(BF16)
        outs[1][...] = (acc * gg).astype(BF16)
        outs[2][...] = (agp * (yl * ps) * (1.0 - gp)).astype(BF16)
        outs[3][...] = (acc * ygla_ref[...].astype(F32) * gg * (1.0 - gg)).astype(BF16)
        _row_acc(outs[4], jnp.sum(agp * yl, axis=0, keepdims=True), i)

    return matmul("mm_dmixed", dx2b, wout, a_spec=_rowblk, b_spec=((D, D), lambda j, i, k: (0, 0)), cdims=NT,
                  grid=(1, T // TMF, 1), acc_shape=(TMF, D),
                  extras=[(pcat, *_full_spec(OGP // D)), (pcat, *_full_spec(OGG // D)), (ylin, *_rowblk), (ygla, *_rowblk),
                          (pscale, *_vec)],
                  outs=[((T, D), BF16, *_rowblk)] * 4 + [((1, D), F32, *_vec)], epi=epi, after=after)


def mm_dog(dygla, wgo, o, pcat, ng):
    def epi(acc, ex, outs, i):
        o_ref, g_ref, ng_ref = ex
        do_ref, dg_ref, gng_ref = outs
        gparts = []
        for h in range(HEADS):
            cv = slice(h * DV, (h + 1) * DV)
            oh = o_ref[:, cv].astype(F32)
            r = lax.rsqrt(jnp.mean(oh * oh, axis=-1, keepdims=True) + EPS)
            on = oh * r
            gv = g_ref[:, cv].astype(F32)
            sg = _sigmoid(gv)
            dgain = acc[:, cv] * (gv * sg)
            gparts.append(jnp.sum(dgain * on, axis=0, keepdims=True))
            ngh = ng_ref[:, cv]
            do_ref[:, cv] = _rms_bwd(on, r, dgain * ngh).astype(BF16)
            dg_ref[:, cv] = (acc[:, cv] * (on * ngh) * (sg * (1.0 + gv * (1.0 - sg)))).astype(BF16)
        _row_acc(gng_ref, jnp.concatenate(gparts, axis=1), i)

    return matmul("mm_dog", dygla, wgo, a_spec=_rowblk, b_spec=((D, D), lambda j, i, k: (0, 0)), cdims=NT,
                  grid=(1, T // TMF, 1), acc_shape=(TMF, D),
                  extras=[(o, *_rowblk), (pcat, *_full_spec(OG // D)), (ng, *_vec)],
                  outs=[((T, D), BF16, *_rowblk), ((T, D), BF16, *_rowblk), ((1, D), F32, *_vec)], epi=epi)


def mm_dh1(dpcat, wcat, x, dx2, g1, after=None):
    tk = 1280

    def epi(acc, ex, outs, i):
        x_ref, dx2_ref, g_ref = ex
        xv = x_ref[...]
        r = lax.rsqrt(jnp.mean(xv * xv, axis=-1, keepdims=True) + EPS)
        xn = xv * r
        _row_acc(outs[1], jnp.sum(acc * xn, axis=0, keepdims=True), i)
        outs[0][...] = dx2_ref[...] + _rms_bwd(xn, r, acc * g_ref[...])

    y = square_matmul("mm_dh1", dpcat, wcat, a_spec=((TBIG, tk), lambda j, i, k: (i, k)),
                      b_spec=((TBIG, tk), lambda j, i, k: (j, k)), cdims=NT, nk=NCAT // tk, after=after)
    return rowwise("rows_dh1", y, extras=[(x, *_rowblk), (dx2, *_rowblk), (g1, *_vec)],
                   outs=[((T, D), F32, *_rowblk), ((1, D), F32, *_vec)], epi=epi)


def _tile_rows(rows, cols, n_arrays):
    tm = rows
    while tm % 32 == 0 and 2 * n_arrays * tm * cols * 4 > 24 * 1024 * 1024:
        tm //= 2
    return tm


def add_pairs(name, parts, theirs, core):
    _, _, r, c = parts.shape
    tm = _tile_rows(r, c, 3)

    def body(core_ref, a_ref, b_ref, o_ref):
        o_ref[...] = (a_ref[...].astype(F32) + b_ref[...].astype(F32)).astype(BF16)

    spec = pl.BlockSpec((None, tm, c), lambda j, i, core_ref: (j, i, 0))
    grid_spec = pltpu.PrefetchScalarGridSpec(
        num_scalar_prefetch=1, grid=(NCHIP, r // tm),
        in_specs=[pl.BlockSpec((None, None, tm, c), lambda j, i, core_ref: (core_ref[0], j, i, 0)), spec], out_specs=spec)
    return pl.pallas_call(body, name=name, grid_spec=grid_spec, out_shape=SDS((NCHIP, r, c), BF16),
                          compiler_params=_cparams(40 * 1024 * 1024, ("arbitrary", "arbitrary")))(core, parts, theirs)


def sum_chips(name, sums, landed, chip):
    _, r, c = sums.shape
    tm = _tile_rows(r, c, 4)

    def body(chip_ref, own_ref, l_ref, o_ref):
        s = own_ref[...].astype(F32)
        for t in range(NCHIP - 1):
            s = s + l_ref[t].astype(F32)
        o_ref[...] = s

    grid_spec = pltpu.PrefetchScalarGridSpec(
        num_scalar_prefetch=1, grid=(r // tm,),
        in_specs=[pl.BlockSpec((None, tm, c), lambda i, chip_ref: (chip_ref[0], i, 0)),
                  pl.BlockSpec((NCHIP - 1, tm, c), lambda i, chip_ref: (0, i, 0))],
        out_specs=pl.BlockSpec((tm, c), lambda i, chip_ref: (i, 0)))
    return pl.pallas_call(body, name=name, grid_spec=grid_spec, out_shape=SDS((r, c), F32),
                          compiler_params=_cparams(40 * 1024 * 1024, ("arbitrary",)))(chip, sums, landed)


def _adamw_math(wv, gv, mv, vv):
    mn = ADAM_B1 * mv + (1.0 - ADAM_B1) * gv
    vn = ADAM_B2 * vv + (1.0 - ADAM_B2) * (gv * gv)
    mh = mn / (1.0 - ADAM_B1 ** ADAM_STEP)
    vh = vn / (1.0 - ADAM_B2 ** ADAM_STEP)
    return -ADAM_LR * (mh / (jnp.sqrt(vh) + ADAM_EPS) + ADAM_WD * wv), mn, vn


def adamw(name, w, g, m, v):
    def body(w_ref, g_ref, m_ref, v_ref, go_ref, d_ref, mo_ref, vo_ref):
        gv = g_ref[...]
        go_ref[...] = gv
        d_ref[...], mo_ref[...], vo_ref[...] = _adamw_math(w_ref[...], gv, m_ref[...], v_ref[...])

    return pl.pallas_call(body, name=name, out_shape=[SDS(w.shape, F32)] * 4)(w, g, m, v)


def adamw_halves(name, w, g_own, g_sib, m, v, core):
    _, r, c = w.shape
    tm = _tile_rows(r, c, 10)

    def body(core_ref, w_ref, go_ref, gs_ref, m_ref, v_ref, g_out, d_out, m_out, v_out):
        gv = jnp.where(pl.program_id(0) == core_ref[0], go_ref[...], gs_ref[...])
        g_out[...] = gv
        d_out[...], m_out[...], v_out[...] = _adamw_math(w_ref[...], gv, m_ref[...], v_ref[...])

    full = pl.BlockSpec((None, tm, c), lambda h, i, core_ref: (h, i, 0))
    own = pl.BlockSpec((tm, c), lambda h, i, core_ref: (jnp.where(h == core_ref[0], i, 0), 0))
    sib = pl.BlockSpec((tm, c), lambda h, i, core_ref: (jnp.where(h == core_ref[0], 0, i), 0))
    grid_spec = pltpu.PrefetchScalarGridSpec(num_scalar_prefetch=1, grid=(2, r // tm),
                                             in_specs=[full, own, sib, full, full], out_specs=[full] * 4)
    return pl.pallas_call(body, name=name, grid_spec=grid_spec, out_shape=[SDS(w.shape, F32)] * 4,
                          compiler_params=_cparams(48 * 1024 * 1024, ("arbitrary", "arbitrary")))(core, w, g_own, g_sib, m, v)


def pack_rows(name, parts, rows):
    width = parts[0].shape[1]
    n = len(parts)

    def body(*refs):
        out_ref = refs[n]
        out_ref[...] = jnp.zeros_like(out_ref)
        off = 0
        for p in refs[:n]:
            out_ref[off:off + p.shape[0], :] = p[...]
            off += p.shape[0]

    return pl.pallas_call(body, name=name, out_shape=SDS((rows, width), F32))(*parts)


def _place():
    x, y, c = lax.axis_index("x"), lax.axis_index("y"), lax.axis_index("c")
    chips = [(1 - x, y), (x, 1 - y), (1 - x, 1 - y)]
    return x, y, c, chips


def _row_split(shape, dtype):
    r, c = shape
    n = 1
    while r % (2 * n) == 0 and (r // (2 * n)) % 16 == 0 and (r // n) * c * jnp.dtype(dtype).itemsize > PIECE_BYTES:
        n *= 2
    return [pl.ds(s * (r // n), r // n) for s in range(n)]


def _pieces(ref):
    *lead, r, c = ref.shape
    split = _row_split((r, c), ref.dtype)
    return [ref.at[(*idx, s)] for idx in itertools.product(*[range(d) for d in lead]) for s in split]


HBM = pl.BlockSpec(memory_space=pltpu.HBM)
SEM = pl.BlockSpec(memory_space=pltpu.SEMAPHORE)
EFFECT = pltpu.SideEffectType.DATAFLOW_SIDE_EFFECTING


def gather_start(name, shards):
    n = len(shards)

    def body(*refs):
        src, land = refs[:n], refs[n:2 * n]
        send, recv = refs[2 * n], refs[2 * n + 1]
        x, y, c, chips = _place()
        me = 2 * x + y
        for a in range(n):
            for j, (cx, cy) in enumerate(chips):
                for sp, dp in zip(_pieces(src[a].at[c]), _pieces(land[a].at[me, c])):
                    pltpu.make_async_remote_copy(sp, dp, send.at[3 * a + j], recv.at[3 * a + j],
                                                 device_id=(cx, cy, c), device_id_type=MESH).start()

    lands = [pltpu.with_memory_space_constraint(lax.empty((NCHIP,) + s.shape, s.dtype), pltpu.HBM) for s in shards]
    srcs = [pltpu.with_memory_space_constraint(s, pltpu.HBM) for s in shards]
    outs = pl.pallas_call(
        body, name=name,
        out_shape=(pltpu.SemaphoreType.DMA((3 * n,)), pltpu.SemaphoreType.DMA((3 * n,)),
                   *[pltpu.HBM(s.shape, s.dtype) for s in shards], *[pltpu.HBM(l.shape, l.dtype) for l in lands]),
        in_specs=[HBM] * (2 * n), out_specs=(SEM, SEM, *([HBM] * (2 * n))),
        input_output_aliases={i: 2 + i for i in range(2 * n)},
        compiler_params=pltpu.CompilerParams(has_side_effects=EFFECT),
    )(*srcs, *lands)
    return outs[0], outs[1], list(outs[2:2 + n]), list(outs[2 + n:2 + 2 * n])


def gather_wait(name, send, recv, shards, lands, after):
    n = len(shards)

    def body(*refs):
        src, land = refs[:n], refs[n:2 * n]
        send_ref, recv_ref = refs[2 * n], refs[2 * n + 1]
        x, y, c, chips = _place()
        for a in range(n):
            for j, (cx, cy) in enumerate(chips):
                cp = pltpu.make_async_remote_copy(src[a].at[c], land[a].at[2 * cx + cy, c], send_ref.at[3 * a + j],
                                                  recv_ref.at[3 * a + j], device_id=(cx, cy, c), device_id_type=MESH)
                cp.wait_send()
                cp.wait_recv()

    outs = pl.pallas_call(
        body, name=name,
        out_shape=(*[pltpu.HBM(s.shape, s.dtype) for s in shards], *[pltpu.HBM(l.shape, l.dtype) for l in lands]),
        in_specs=[HBM] * (2 * n) + [SEM, SEM, ANY], out_specs=[HBM] * (2 * n),
        input_output_aliases={i: i for i in range(2 * n)},
        compiler_params=pltpu.CompilerParams(has_side_effects=EFFECT),
    )(*shards, *lands, send, recv, after)
    return list(outs[:n]), list(outs[n:])


def forward_halves(name, shards, lands):
    n = len(lands)

    def body(*refs):
        had, buf = refs[:n], refs[n:2 * n]
        send, recv = refs[2 * n:]
        x, y, c, chips = _place()
        sib = (x, y, 1 - c)
        for a in range(n):
            for j, (cx, cy) in enumerate(chips):
                for sp, dp in zip(_pieces(had[a].at[2 * cx + cy, c]), _pieces(buf[a].at[2 * cx + cy, c])):
                    pltpu.make_async_remote_copy(sp, dp, send.at[3 * a + j], recv.at[3 * a + j], device_id=sib, device_id_type=MESH).start()
        for a in range(n):
            for j, (cx, cy) in enumerate(chips):
                pltpu.make_async_remote_copy(had[a].at[2 * cx + cy, c], buf[a].at[2 * cx + cy, 1 - c], send.at[3 * a + j],
                                             recv.at[3 * a + j], device_id=sib, device_id_type=MESH).wait()

    got = pl.pallas_call(
        body, name=name, in_specs=[ANY] * n, out_specs=[ANY] * n, out_shape=[SDS(l.shape, l.dtype) for l in lands],
        input_output_aliases={i: i for i in range(n)},
        scratch_shapes=[pltpu.SemaphoreType.DMA((3 * n,)), pltpu.SemaphoreType.DMA((3 * n,))],
    )(*lands)
    me = 2 * lax.axis_index("x") + lax.axis_index("y")
    return [lax.dynamic_update_index_in_dim(g, s, me, 0) for g, s in zip(got, shards)]


def exchange_halves(name, parts):
    n = len(parts)

    def body(*refs):
        src, got = refs[:n], refs[n:2 * n]
        send, recv = refs[2 * n:]
        x, y, c, _ = _place()
        sib = (x, y, 1 - c)
        for a in range(n):
            for sp, dp in zip(_pieces(src[a].at[1 - c]), _pieces(got[a])):
                pltpu.make_async_remote_copy(sp, dp, send.at[a], recv.at[a], device_id=sib, device_id_type=MESH).start()
        for a in range(n):
            pltpu.make_async_remote_copy(src[a].at[1 - c], got[a], send.at[a], recv.at[a], device_id=sib, device_id_type=MESH).wait()

    return pl.pallas_call(
        body, name=name, in_specs=[ANY] * n, out_specs=[ANY] * n,
        out_shape=[SDS(p.shape[1:], p.dtype) for p in parts],
        scratch_shapes=[pltpu.SemaphoreType.DMA((n,)), pltpu.SemaphoreType.DMA((n,))],
    )(*parts)


def scatter_start(name, parts):
    n = len(parts)

    def body(*refs):
        src, land = refs[:n], refs[n:2 * n]
        send, recv = refs[2 * n], refs[2 * n + 1]
        token = refs[4 * n + 2]
        x, y, c, chips = _place()
        for a in range(n):
            for j, (cx, cy) in enumerate(chips):
                for sp, dp in zip(_pieces(src[a].at[2 * cx + cy]), _pieces(land[a].at[j])):
                    pltpu.make_async_remote_copy(sp, dp, send.at[3 * a + j], recv.at[3 * a + j],
                                                 device_id=(cx, cy, c), device_id_type=MESH).start()
        token[...] = jnp.zeros_like(token)

    lands = [pltpu.with_memory_space_constraint(lax.empty((NCHIP - 1,) + p.shape[1:], p.dtype), pltpu.HBM) for p in parts]
    srcs = [pltpu.with_memory_space_constraint(p, pltpu.HBM) for p in parts]
    outs = pl.pallas_call(
        body, name=name,
        out_shape=(pltpu.SemaphoreType.DMA((3 * n,)), pltpu.SemaphoreType.DMA((3 * n,)),
                   *[pltpu.HBM(p.shape, p.dtype) for p in parts], *[pltpu.HBM(l.shape, l.dtype) for l in lands],
                   SDS((8, 128), F32)),
        in_specs=[HBM] * (2 * n), out_specs=(SEM, SEM, *([HBM] * (2 * n)), pl.BlockSpec(memory_space=pltpu.VMEM)),
        input_output_aliases={i: 2 + i for i in range(2 * n)},
        compiler_params=pltpu.CompilerParams(has_side_effects=EFFECT),
    )(*srcs, *lands)
    return outs[0], outs[1], list(outs[2:2 + n]), list(outs[2 + n:2 + 2 * n]), outs[2 + 2 * n]


def scatter_wait(name, send, recv, parts, lands, after):
    n = len(parts)

    def body(*refs):
        src, land = refs[:n], refs[n:2 * n]
        send_ref, recv_ref = refs[2 * n], refs[2 * n + 1]
        x, y, c, chips = _place()
        for a in range(n):
            for j, (cx, cy) in enumerate(chips):
                cp = pltpu.make_async_remote_copy(src[a].at[2 * cx + cy], land[a].at[j], send_ref.at[3 * a + j], recv_ref.at[3 * a + j],
                                                  device_id=(cx, cy, c), device_id_type=MESH)
                cp.wait_send()
                cp.wait_recv()

    outs = pl.pallas_call(
        body, name=name,
        out_shape=(*[pltpu.HBM(p.shape, p.dtype) for p in parts], *[pltpu.HBM(l.shape, l.dtype) for l in lands]),
        in_specs=[HBM] * (2 * n) + [SEM, SEM, ANY], out_specs=[HBM] * (2 * n),
        input_output_aliases={i: i for i in range(2 * n)},
        compiler_params=pltpu.CompilerParams(has_side_effects=EFFECT),
    )(*parts, *lands, send, recv, after)
    return list(outs[:n]), list(outs[n:])


def join_halves(name, halves):
    n = len(halves)

    def body(*refs):
        src, dst = refs[:n], refs[n:2 * n]
        send, recv = refs[2 * n:]
        x, y, c, _ = _place()
        sib = (x, y, 1 - c)
        for a in range(n):
            for sp, dp in zip(_pieces(src[a]), _pieces(dst[a])):
                pltpu.make_async_remote_copy(sp, dp, send.at[a], recv.at[a], device_id=sib, device_id_type=MESH).start()
        for a in range(n):
            pltpu.make_async_remote_copy(src[a], dst[a], send.at[a], recv.at[a], device_id=sib, device_id_type=MESH).wait()

    return pl.pallas_call(
        body, name=name, in_specs=[ANY] * n, out_specs=[ANY] * n,
        out_shape=[SDS(h.shape, h.dtype) for h in halves],
        scratch_shapes=[pltpu.SemaphoreType.DMA((n,)), pltpu.SemaphoreType.DMA((n,))],
    )(*halves)


def gather_small(name, xs, reduce):
    m, ncol = xs.shape

    def body(x_ref, out_ref, all_ref, send, recv, lsem):
        x, y, c, chips = _place()
        me, sib = (x, y, c), (x, y, 1 - c)

        def rows(px, py, pc):
            return all_ref.at[pl.ds((4 * px + 2 * py + pc) * m, m), :]

        def copy(k, block, to, src=None):
            return pltpu.make_async_remote_copy(rows(*block) if src is None else src, rows(*block), send.at[k], recv.at[k],
                                                device_id=to, device_id_type=MESH)

        mine = pltpu.make_async_copy(x_ref, rows(*me), lsem)
        mine.start()
        first = [copy(0, me, sib, src=x_ref)] + [copy(1 + j, me, (*chip, c), src=x_ref) for j, chip in enumerate(chips)]
        for cp in first:
            cp.start()
        passed = [copy(4 + j, (*chip, c), sib) for j, chip in enumerate(chips)]
        for j, chip in enumerate(chips):
            copy(1 + j, (*chip, c), me).wait_recv()
            passed[j].start()
        copy(0, sib, me).wait_recv()
        for j, chip in enumerate(chips):
            copy(4 + j, (*chip, 1 - c), me).wait_recv()
        for cp in first + passed:
            cp.wait_send()
        mine.wait()
        if reduce:
            s = all_ref[0:m, :]
            for dev in range(1, 8):
                s = s + all_ref[dev * m:(dev + 1) * m, :]
            out_ref[...] = s
        else:
            out_ref[...] = all_ref[...]

    vm = pl.BlockSpec(memory_space=pltpu.VMEM)
    return pl.pallas_call(
        body, name=name, in_specs=[vm], out_specs=vm, out_shape=SDS((m, ncol) if reduce else (8 * m, ncol), F32),
        scratch_shapes=[pltpu.VMEM((8 * m, ncol), F32), pltpu.SemaphoreType.DMA((7,)), pltpu.SemaphoreType.DMA((7,)),
                        pltpu.SemaphoreType.DMA],
    )(xs)


def _to_cat(nat):
    pad = jnp.zeros(nat.shape[:-1] + (NCAT - OA - 16,), nat.dtype)
    return jnp.concatenate([nat[..., 3072:7168], nat[..., 7184:11280], nat[..., 0:3072], nat[..., 7168:7184], pad], axis=-1)


def _from_cat(cat):
    return jnp.concatenate([cat[..., OU:OA], cat[..., OV:OGP], cat[..., OA:OA + 16], cat[..., OGP:OU]], axis=-1)


def _pad_rows(a, rows):
    return jnp.concatenate([a, jnp.zeros((rows - a.shape[0],) + a.shape[1:], a.dtype)], axis=0)


def local_step(x2d, tgt, gf, g1, pool_scale, wa_pad, b_alpha, ng, g2, get_w, on_grad=None):
    emit = on_grad if on_grad is not None else (lambda group, grads: None)
    h1 = norm1(x2d, g1)
    wcat, pw = get_w("in", h1)
    pcat = mm_in(h1, wcat)
    dpool, ylin = pool_fwd(pcat, pw)
    og, o, states = gla_fwd(pcat, wa_pad, b_alpha, ng)
    w_go, w_o = get_w("mid", og)
    mixed, ygla = mm_gla_out(og, w_go, ylin, pcat, pool_scale)
    x2, h2 = mm_out(mixed, w_o, x2d, g2)
    w_up = get_w("up", h2)
    rup, act = mm_up(h2, w_up)
    w_dn = get_w("down", act)
    dx3, dx3b, g_nf, loss_row = mm_down(act, w_dn, x2, tgt, gf)

    gw_down = mm_wgrad("mm_dw_down", act, dx3b, DFF, D, (2, NCHIP, D // 2, D), (None, None, 512, D),
                       lambda j, i, k: ((i // 2) % 2, i // 4, i % 2, 0), 512, D)
    token = emit("down", {"down": gw_down})
    dup = mm_dact(dx3b, w_dn, rup, after=token)
    dx2, dx2b, g_mlp = mm_dh2(dup, w_up, x2, dx3, g2)
    gw_up = mm_wgrad("mm_dw_up", h2, dup, D, DFF, (2, NCHIP, D // 2, D), (None, None, 512, D),
                     lambda j, i, k: (i // 2, j, i % 2, 0), 512, D)
    token = emit("up", {"up": gw_up})
    dylin, dygla, dlgp, dlgg, g_ps = mm_dmixed(dx2b, w_o, pcat, ylin, ygla, pool_scale, after=token)
    gw_out = mm_wgrad("mm_dw_out", mixed, dx2b, D, D, (2, NCHIP, 256, D), (None, None, 256, D),
                      lambda j, i, k: (i % 2, i // 2, 0, 0), 256, D)
    do, dg, g_ng = mm_dog(dygla, w_go, o, pcat, ng)
    gw_go = mm_wgrad("mm_dw_gla_out", og, dygla, D, D, (2, NCHIP, 256, D), (None, None, 256, D),
                     lambda j, i, k: (i % 2, i // 2, 0, 0), 256, D)
    token = emit("mix", {"out": gw_out, "gla_out": gw_go})
    dq, dk, dv, dalow, g_wa, g_ba = gla_bwd(do, pcat, states, wa_pad, b_alpha, b_alpha if token is None else token)
    du, dpw = pool_bwd(dylin, dpool, pw)
    dpcat = jnp.concatenate([dv, dg, dlgp, dlgg, du, dq, dk, dalow, jnp.zeros((T, NCAT - OA - APAD), BF16)], axis=1)
    gw_cat = mm_wgrad("mm_dw_in", h1, dpcat, D, NCAT, (D, NCAT), (512, 1280), lambda j, i, k: (i, j), 512, 1280)
    token = emit("in", {"in_cat": gw_cat, "pool": dpw})
    grad_x, g_mix = mm_dh1(dpcat, wcat, x2d, dx2, g1, after=token)
    return (loss_row[0, 0], grad_x, g_mix, g_ps, g_mlp, g_nf, g_ng, g_ba, g_wa, token,
            gw_cat, dpw, gw_go, gw_out, gw_up, gw_down)


def kernel(x, norm_mix_g, w_in, pool_w, pool_scale, w_alpha, b_alpha, gla_norm_g, w_gla_out, w_out, norm_mlp_g, w_mlp_up, w_mlp_down, norm_final_g, loss_target, m_norm_mix_g, m_w_in, m_pool_w, m_pool_scale, m_w_alpha, m_b_alpha, m_gla_norm_g, m_w_gla_out, m_w_out, m_norm_mlp_g, m_w_mlp_up, m_w_mlp_down, m_norm_final_g, v_norm_mix_g, v_w_in, v_pool_w, v_pool_scale, v_w_alpha, v_b_alpha, v_gla_norm_g, v_w_gla_out, v_w_out, v_norm_mlp_g, v_w_mlp_up, v_w_mlp_down, v_norm_final_g):
    chip = 2 * lax.axis_index("x") + lax.axis_index("y")
    chip_i = chip.astype(jnp.int32).reshape(1)
    core_i = lax.axis_index("c").astype(jnp.int32).reshape(1)
    x2d = x.reshape(T, D)
    tgt = loss_target.reshape(T, D)
    gf = norm_final_g.reshape(1, D)

    def halves(w2d):
        r, c = w2d.shape
        return w2d.astype(BF16).reshape(2, r // 2, c)

    pool_shard = pool_w.reshape(4 * PG, PO // NCHIP)
    big = [w_in[0], w_gla_out[0], w_out[0], w_mlp_up[0], w_mlp_down[0], pool_shard]
    groups = {"in": [big[0], big[5]], "mid": [big[1], big[2]], "up": [big[3]], "down": [big[4]]}
    started = {g: gather_start("gather_start_" + g, [halves(w) for w in ws]) for g, ws in groups.items()}

    def get_w(group, after):
        send, recv, shards, lands = started[group]
        if group == "in":
            after = started["down"][2][0]
        shards, lands = gather_wait("gather_wait_" + group, send, recv, shards, lands, after)
        whole = forward_halves("forward_" + group, shards, lands)
        if group == "in":
            g_in, g_pool = whole
            wcat = _to_cat(jnp.concatenate([g_in[j].reshape(D, IN_SHARD) for j in range(NCHIP)], axis=1))
            pw = jnp.concatenate([g_pool[j].reshape(4, PG, PO // NCHIP) for j in range(NCHIP)], axis=2)
            return wcat, pw
        if group == "mid":
            return whole[0].reshape(D, D), whole[1].reshape(D, D)
        if group == "up":
            return whole[0].reshape(NCHIP, D, D)
        return whole[0].reshape(DFF, D)

    small_w = pack_rows("pack_small_w", [w_alpha[0].reshape(4, QK),
                                         jnp.concatenate([gla_norm_g[0].reshape(1, 512), jnp.zeros((1, 512), F32)], axis=1)], 8)
    sw_all = gather_small("gather_small_w", small_w, False).reshape(8, 8, QK)
    wa_full = jnp.concatenate([sw_all[2 * j, 0:4].reshape(16, DK) for j in range(NCHIP)], axis=1)
    ng_full = jnp.concatenate([sw_all[2 * j, 4, 0:512].reshape(HEADS, DV // NCHIP) for j in range(NCHIP)], axis=1)
    wa_pad = _pad_rows(wa_full, APAD).astype(BF16)
    ng = ng_full.reshape(1, D)

    pending = {}
    wmv = {"in": (big[0], m_w_in, v_w_in), "gla_out": (big[1], m_w_gla_out, v_w_gla_out), "out": (big[2], m_w_out, v_w_out),
           "up": (big[3], m_w_mlp_up, v_w_mlp_up), "down": (big[4], m_w_mlp_down, v_w_mlp_down), "pool": (big[5], m_pool_w, v_pool_w)}
    big_res = {}

    def finish(group, after):
        nms, send, recv, sums, lands = pending[group]
        sums, lands = scatter_wait("scatter_wait_" + group, send, recv, sums, lands, after)
        reduced = [sum_chips("sum_chips_" + nm, a, b, chip_i) for nm, a, b in zip(nms, sums, lands)]
        from_sib = join_halves("join_" + group, reduced)
        for nm, g_own, g_sib in zip(nms, reduced, from_sib):
            w, m, v = wmv[nm]
            shp = (2,) + g_own.shape
            big_res[nm] = adamw_halves("adamw_" + nm, w.reshape(shp), g_own, g_sib, m.reshape(shp), v.reshape(shp), core_i)

    def on_grad(group, grads):
        if group == "in":
            nat = _from_cat(grads["in_cat"])
            gw_in = jnp.stack([nat[:, j * IN_SHARD:(j + 1) * IN_SHARD].reshape(2, D // 2, IN_SHARD)
                               for j in range(NCHIP)], axis=1)
            gw_pool = jnp.stack([grads["pool"][:, :, j * 128:(j + 1) * 128].reshape(2, 2 * PG, 128)
                                 for j in range(NCHIP)], axis=1)
            grads = {"in": gw_in, "pool": gw_pool}
        nms, parts = list(grads.keys()), list(grads.values())
        theirs = exchange_halves("exchange_" + group, parts)
        sums = [add_pairs("add_pair_" + nm, a, b, core_i) for nm, a, b in zip(nms, parts, theirs)]
        send, recv, sums, lands, token = scatter_start("scatter_start_" + group, sums)
        pending[group] = (nms, send, recv, sums, lands)
        if group != "in":
            return token
        for earlier in ("down", "up", "mix"):
            finish(earlier, token)
        return big_res["gla_out"][1]

    (loss_local, grad_x, g_mix, g_ps, g_mlp, g_nf, g_ng, g_ba, g_wa) = local_step(
        x2d, tgt, gf, norm_mix_g, pool_scale, wa_pad, b_alpha, ng, norm_mlp_g, get_w, on_grad)[:9]
    loss = lax.psum(loss_local, ("x", "y", "c"))
    finish("in", grad_x)

    ROWS = 16

    def wide(a, n):
        return jnp.concatenate([a.reshape(1, n), jnp.zeros((1, D - n), F32)], axis=1)

    packed = pack_rows("pack_small_g", [g_mix, g_ps, g_mlp, g_nf, g_ng, wide(g_ba, QK), g_wa[0:16].reshape(8, D)], ROWS)
    tot = gather_small("reduce_small_g", packed, True)
    t_wa = lax.dynamic_slice(tot[6:14].reshape(16, QK), (0, chip * DK), (16, DK))
    t_ng = lax.dynamic_slice(tot[4].reshape(HEADS, DV), (0, chip * (DV // NCHIP)), (HEADS, DV // NCHIP))

    def pack_small(nm, mix, ps, mlp, nf, ba, wa, gn):
        return pack_rows(nm, [mix.reshape(1, D), ps.reshape(1, D), mlp.reshape(1, D), nf.reshape(1, D), wide(ba, QK),
                              wa.reshape(2, D), wide(gn, 512)], ROWS)

    sg = pack_small("pack_g", tot[0], tot[1], tot[2], tot[3], tot[5, 0:QK], t_wa, t_ng)
    sw = pack_small("pack_w", norm_mix_g, pool_scale, norm_mlp_g, norm_final_g, b_alpha, w_alpha, gla_norm_g)
    sm = pack_small("pack_m", m_norm_mix_g, m_pool_scale, m_norm_mlp_g, m_norm_final_g, m_b_alpha, m_w_alpha, m_gla_norm_g)
    sv = pack_small("pack_v", v_norm_mix_g, v_pool_scale, v_norm_mlp_g, v_norm_final_g, v_b_alpha, v_w_alpha, v_gla_norm_g)
    small_res = adamw("adamw_small", sw, sg, sm, sv)

    def unpack(p):
        return {"norm_mix_g": p[0].reshape(1, D), "pool_scale": p[1].reshape(1, D), "norm_mlp_g": p[2].reshape(1, D),
                "norm_final_g": p[3].reshape(D), "b_alpha": p[4, 0:QK].reshape(1, QK), "w_alpha": p[5:7].reshape(1, 16, DK),
                "gla_norm_g": p[7, 0:512].reshape(1, HEADS, DV // NCHIP)}

    order = ["norm_mix_g", "w_in", "pool_w", "pool_scale", "w_alpha", "b_alpha", "gla_norm_g", "w_gla_out", "w_out",
             "norm_mlp_g", "w_mlp_up", "w_mlp_down", "norm_final_g"]
    big_key = {"w_in": ("in", w_in.shape), "pool_w": ("pool", pool_w.shape), "w_gla_out": ("gla_out", w_gla_out.shape),
               "w_out": ("out", w_out.shape), "w_mlp_up": ("up", w_mlp_up.shape), "w_mlp_down": ("down", w_mlp_down.shape)}
    result = [loss, grad_x.reshape(1, T, D)]
    for kind in range(4):
        small = unpack(small_res[kind])
        for nm in order:
            if nm in big_key:
                key, shp = big_key[nm]
                result.append(big_res[key][kind].reshape(shp))
            else:
                result.append(small[nm])
    return tuple(result)
```

```python
import itertools

import jax
import jax.numpy as jnp
from jax import lax
from jax.experimental import pallas as pl
from jax.experimental.pallas import tpu as pltpu

F32 = jnp.float32
BF16 = jnp.bfloat16
SDS = jax.ShapeDtypeStruct
MESH = pl.DeviceIdType.MESH
ANY = pl.BlockSpec(memory_space=pl.ANY)

T = 2048
D = 2048
DFF = 8192
NCHIP = 4
IN_WIDTH = 11280
IN_SHARD = IN_WIDTH // NCHIP
CHUNK = 64
NCHUNK = T // CHUNK
HEADS = 4
DK = 256
DV = 512
QK = HEADS * DK
EPS = 1e-6
POOL_WINDOWS = (2, 4, 8, 16)
PG = 256
PO = 512

OV, OG, OGP, OGG, OU, OQ, OKK, OA = 0, 2048, 4096, 6144, 8192, 9216, 10240, 11264
NCAT = 11520
APAD = 128

VMEM_CAP = 56 * 1024 * 1024

PIECE_BYTES = 3 * 512 * 1024

ADAM_LR, ADAM_B1, ADAM_B2, ADAM_EPS, ADAM_WD, ADAM_STEP = 0.001, 0.9, 0.999, 1e-08, 0.01, 10


def _cparams(vmem_bytes=None, sem=None):
    kw = {}
    if vmem_bytes is not None:
        kw["vmem_limit_bytes"] = int(min(max(vmem_bytes, 32 * 1024 * 1024), VMEM_CAP))
    if sem is not None:
        kw["dimension_semantics"] = sem
    return pltpu.CompilerParams(**kw)


def _nbytes(shape, dtype):
    n = 1
    for s in shape:
        if s is not None:
            n *= s
    return n * jnp.dtype(dtype).itemsize


def _sigmoid(x):
    return 1.0 / (1.0 + jnp.exp(-x))


def matmul(name, a, b, *, a_spec, b_spec, cdims, grid, acc_shape, outs, extras=(), epi, after=None):
    nj, ni, nk = grid
    ne, no = len(extras), len(outs)
    first_out = 2 + ne + (0 if after is None else 1)

    def body(*refs):
        a_ref, b_ref = refs[0], refs[1]
        ex = refs[2:2 + ne]
        out_refs = refs[first_out:first_out + no]
        i = pl.program_id(1)
        part = lax.dot_general(a_ref[...], b_ref[...], (cdims, ((), ())), preferred_element_type=F32)
        if nk == 1:
            epi(part, ex, out_refs, i)
        else:
            acc_ref = refs[first_out + no]
            k = pl.program_id(2)

            @pl.when(k == 0)
            def _():
                acc_ref[...] = part

            @pl.when(k > 0)
            def _():
                acc_ref[...] += part

            @pl.when(k == nk - 1)
            def _():
                epi(acc_ref[...], ex, out_refs, i)

    in_specs = [pl.BlockSpec(*a_spec), pl.BlockSpec(*b_spec)] + [pl.BlockSpec(bs, im) for _, bs, im in extras]
    in_specs += [] if after is None else [ANY]
    out_specs = [pl.BlockSpec(bs, im) for _, _, bs, im in outs]
    out_shape = [SDS(s, dt) for s, dt, _, _ in outs]
    vm = 2 * (_nbytes(a_spec[0], a.dtype) + _nbytes(b_spec[0], b.dtype))
    vm += 2 * sum(_nbytes(bs, arr.dtype) for arr, bs, _ in extras)
    vm += 2 * sum(_nbytes(bs, dt) for _, dt, bs, _ in outs)
    vm += 6 * _nbytes(acc_shape, F32)
    scratch = [pltpu.VMEM(acc_shape, F32)] if nk > 1 else []
    return pl.pallas_call(
        body, name=name, grid=grid, in_specs=in_specs, out_specs=out_specs, out_shape=out_shape,
        scratch_shapes=scratch,
        compiler_params=_cparams(vm, ("arbitrary", "arbitrary", "arbitrary")),
    )(a, b, *[arr for arr, _, _ in extras], *([] if after is None else [after]))


NN =((1,), (0,))
NT = ((1,), (1,))
TN = ((0,), (0,))


def _row_acc(out_ref, val, i):
    @pl.when(i == 0)
    def _():
        out_ref[...] = val

    @pl.when(i > 0)
    def _():
        out_ref[...] += val


def _rms_bwd(xn, r, dxn):
    return r * (dxn - xn * jnp.mean(dxn * xn, axis=-1, keepdims=True))


def norm1(x, g):
    tm = 256

    def body(x_ref, g_ref, h_ref):
        xv = x_ref[...]
        r = lax.rsqrt(jnp.mean(xv * xv, axis=-1, keepdims=True) + EPS)
        h_ref[...] = (xv * r * g_ref[...]).astype(BF16)

    return pl.pallas_call(
        body, name="norm1", grid=(T // tm,),
        in_specs=[pl.BlockSpec((tm, D), lambda i: (i, 0)), pl.BlockSpec((1, D), lambda i: (0, 0))],
        out_specs=pl.BlockSpec((tm, D), lambda i: (i, 0)), out_shape=SDS((T, D), BF16),
        compiler_params=_cparams(32 * 1024 * 1024, ("arbitrary",)),
    )(x, g)


def mm_in(h1, wcat):
    tm, tn = 512, 1280

    def epi(acc, ex, outs, i):
        outs[0][...] = acc.astype(BF16)

    return matmul("mm_in", h1, wcat, a_spec=((tm, D), lambda j, i, k: (i, 0)), b_spec=((D, tn), lambda j, i, k: (0, j)),
                  cdims=NN, grid=(NCAT // tn, T // tm, 1), acc_shape=(tm, tn),
                  outs=[((T, NCAT), BF16, (tm, tn), lambda j, i, k: (i, j))], epi=epi)[0]


def _window_sum(x, w, up):
    n = x.shape[0]
    row = lax.broadcasted_iota(jnp.int32, x.shape, 0)
    s, sh = x, 1
    while sh < w:
        if up:
            s = s + jnp.where(row < n - sh, pltpu.roll(s, n - sh, axis=0), 0.0)
        else:
            s = s + jnp.where(row >= sh, pltpu.roll(s, sh, axis=0), 0.0)
        sh *= 2
    return s


def _inv_count(shape, w):
    row = lax.broadcasted_iota(jnp.int32, shape, 0)
    return 1.0 / jnp.minimum(row + 1, w).astype(F32)


def pool_fwd(pcat, pw):
    def body(u_ref, pw_ref, d_ref, y_ref):
        for gi, w in enumerate(POOL_WINDOWS):
            ug = u_ref[:, gi * PG:(gi + 1) * PG].astype(F32)
            dg = _window_sum(ug, w, False) * _inv_count(ug.shape, w) - ug
            db = dg.astype(BF16)
            d_ref[:, gi * PG:(gi + 1) * PG] = db
            y_ref[:, gi * PO:(gi + 1) * PO] = jnp.dot(db, pw_ref[gi], preferred_element_type=F32).astype(BF16)

    return pl.pallas_call(
        body, name="pool_fwd", grid=(1,),
        in_specs=[pl.BlockSpec((T, 4 * PG), lambda i: (0, OU // (4 * PG))), pl.BlockSpec((4, PG, PO), lambda i: (0, 0, 0))],
        out_specs=[pl.BlockSpec((T, 4 * PG), lambda i: (0, 0)), pl.BlockSpec((T, D), lambda i: (0, 0))],
        out_shape=[SDS((T, 4 * PG), BF16), SDS((T, D), BF16)],
        compiler_params=_cparams(48 * 1024 * 1024, ("arbitrary",)),
    )(pcat, pw)


def pool_bwd(dylin, d, pw):
    def body(dy_ref, d_ref, pw_ref, du_ref, dpw_ref):
        for gi, w in enumerate(POOL_WINDOWS):
            dyl = dy_ref[:, gi * PO:(gi + 1) * PO]
            dd = lax.dot_general(dyl, pw_ref[gi], (NT, ((), ())), preferred_element_type=F32)
            du = _window_sum(dd * _inv_count(dd.shape, w), w, True) - dd
            du_ref[:, gi * PG:(gi + 1) * PG] = du.astype(BF16)
            dpw_ref[gi] = lax.dot_general(d_ref[:, gi * PG:(gi + 1) * PG], dyl, (TN, ((), ())),
                                          preferred_element_type=F32).astype(BF16)

    return pl.pallas_call(
        body, name="pool_bwd", grid=(1,),
        in_specs=[pl.BlockSpec((T, D), lambda i: (0, 0)), pl.BlockSpec((T, 4 * PG), lambda i: (0, 0)),
                  pl.BlockSpec((4, PG, PO), lambda i: (0, 0, 0))],
        out_specs=[pl.BlockSpec((T, 4 * PG), lambda i: (0, 0)), pl.BlockSpec((4, PG, PO), lambda i: (0, 0, 0))],
        out_shape=[SDS((T, 4 * PG), BF16), SDS((4, PG, PO), BF16)],
        compiler_params=_cparams(48 * 1024 * 1024, ("arbitrary",)),
    )(dylin, d, pw)


def _gate_decay(alow, wa, ba):
    a = jnp.dot(alow, wa, preferred_element_type=F32) + ba
    ls = jax.nn.log_sigmoid(a) * (1.0 / 16.0)
    r = lax.broadcasted_iota(jnp.int32, (CHUNK, CHUNK), 0)
    c = lax.broadcasted_iota(jnp.int32, (CHUNK, CHUNK), 1)
    tri = jnp.where(c <= r, 1.0, 0.0).astype(F32)
    cum = jnp.dot(tri, ls, preferred_element_type=F32, precision=lax.Precision.HIGHEST)
    last = cum[CHUNK - 1:CHUNK, :]
    return a, jnp.exp(last - cum), jnp.exp(last)


def gla_fwd(pcat, wa, ba, ng):
    def body(q_ref, k_ref, v_ref, g_ref, al_ref, wa_ref, ba_ref, ng_ref, og_ref, o_ref, st_ref, s_scr):
        @pl.when(pl.program_id(0) == 0)
        def _():
            s_scr[...] = jnp.zeros_like(s_scr)

        _, e, decay = _gate_decay(al_ref[...], wa_ref[...], ba_ref[...])
        kd = (k_ref[...].astype(F32) * e).astype(BF16)
        qs = (q_ref[...].astype(F32) * (DK ** -0.5)).astype(BF16)
        for h in range(HEADS):
            ck = slice(h * DK, (h + 1) * DK)
            cv = slice(h * DV, (h + 1) * DV)
            s_new = s_scr[h] * decay[:, ck] + lax.dot_general(v_ref[:, cv], kd[:, ck], (TN, ((), ())),
                                                               preferred_element_type=F32)
            s_scr[h] = s_new
            sb = s_new.astype(BF16)
            st_ref[h] = sb
            oh = lax.dot_general(qs[:, ck], sb, (NT, ((), ())), preferred_element_type=F32)
            o_ref[:, cv] = oh.astype(BF16)
            on = oh * lax.rsqrt(jnp.mean(oh * oh, axis=-1, keepdims=True) + EPS) * ng_ref[:, cv]
            gv = g_ref[:, cv].astype(F32)
            og_ref[:, cv] = (on * (gv * _sigmoid(gv))).astype(BF16)

    row = lambda c: (c, 0)
    return pl.pallas_call(
        body, name="gla_fwd", grid=(NCHUNK,),
        in_specs=[pl.BlockSpec((CHUNK, QK), lambda c: (c, OQ // QK)), pl.BlockSpec((CHUNK, QK), lambda c: (c, OKK // QK)),
                  pl.BlockSpec((CHUNK, D), lambda c: (c, OV // D)), pl.BlockSpec((CHUNK, D), lambda c: (c, OG // D)),
                  pl.BlockSpec((CHUNK, APAD), lambda c: (c, OA // APAD)),
                  pl.BlockSpec((APAD, QK), lambda c: (0, 0)), pl.BlockSpec((1, QK), lambda c: (0, 0)),
                  pl.BlockSpec((1, D), lambda c: (0, 0))],
        out_specs=[pl.BlockSpec((CHUNK, D), row), pl.BlockSpec((CHUNK, D), row),
                   pl.BlockSpec((None, HEADS, DV, DK), lambda c: (c, 0, 0, 0))],
        out_shape=[SDS((T, D), BF16), SDS((T, D), BF16), SDS((NCHUNK, HEADS, DV, DK), BF16)],
        scratch_shapes=[pltpu.VMEM((HEADS, DV, DK), F32)],
        compiler_params=_cparams(32 * 1024 * 1024, ("arbitrary",)),
    )(pcat, pcat, pcat, pcat, pcat, wa, ba, ng)


def gla_bwd(do, pcat, states, wa, ba, after):
    def body(do_ref, q_ref, k_ref, v_ref, al_ref, sc_ref, sp_ref, wa_ref, ba_ref, after_ref,
             dq_ref, dk_ref, dv_ref, dal_ref, dwa_ref, dba_ref, ds_scr):
        i = pl.program_id(0)

        @pl.when(i == 0)
        def _():
            ds_scr[...] = jnp.zeros_like(ds_scr)

        has_prev = jnp.where(i < NCHUNK - 1, 1.0, 0.0).astype(F32)
        a, e, decay = _gate_decay(al_ref[...], wa_ref[...], ba_ref[...])
        kf = k_ref[...].astype(F32)
        kdf = kf * e
        kd = kdf.astype(BF16)
        qs = (q_ref[...].astype(F32) * (DK ** -0.5)).astype(BF16)
        dkd_parts, ddecay_parts = [], []
        for h in range(HEADS):
            ck = slice(h * DK, (h + 1) * DK)
            cv = slice(h * DV, (h + 1) * DV)
            doh = do_ref[:, cv]
            ds = ds_scr[h] + lax.dot_general(doh, qs[:, ck], (TN, ((), ())), preferred_element_type=F32)
            dsb = ds.astype(BF16)
            dq_ref[:, ck] = (jnp.dot(doh, sc_ref[h], preferred_element_type=F32) * (DK ** -0.5)).astype(BF16)
            dkd_parts.append(jnp.dot(v_ref[:, cv], dsb, preferred_element_type=F32))
            dv_ref[:, cv] = lax.dot_general(kd[:, ck], dsb, (NT, ((), ())), preferred_element_type=F32).astype(BF16)
            ddecay_parts.append(jnp.sum(ds * sp_ref[h].astype(F32), axis=0, keepdims=True) * has_prev)
            ds_scr[h] = ds * decay[:, ck]
        dkd = jnp.concatenate(dkd_parts, axis=1)
        ddecay = jnp.concatenate(ddecay_parts, axis=1)
        dk_ref[...] = (dkd * e).astype(BF16)
        dearg = dkd * kdf
        dlast = jnp.sum(dearg, axis=0, keepdims=True) + ddecay * decay
        r = lax.broadcasted_iota(jnp.int32, (CHUNK, CHUNK), 0)
        c = lax.broadcasted_iota(jnp.int32, (CHUNK, CHUNK), 1)
        triu = jnp.where(c >= r, 1.0, 0.0).astype(F32)
        dls = dlast - jnp.dot(triu, dearg, preferred_element_type=F32, precision=lax.Precision.HIGHEST)
        da = dls * (1.0 / 16.0) * (1.0 - _sigmoid(a))
        dab = da.astype(BF16)
        dal_ref[...] = lax.dot_general(dab, wa_ref[...], (NT, ((), ())), preferred_element_type=F32).astype(BF16)
        dwa = lax.dot_general(al_ref[...], dab, (TN, ((), ())), preferred_element_type=F32)
        dba = jnp.sum(da, axis=0, keepdims=True)

        @pl.when(i == 0)
        def _():
            dwa_ref[...] = dwa
            dba_ref[...] = dba

        @pl.when(i > 0)
        def _():
            dwa_ref[...] += dwa
            dba_ref[...] += dba

    rev = lambda i: NCHUNK - 1 - i
    return pl.pallas_call(
        body, name="gla_bwd", grid=(NCHUNK,),
        in_specs=[pl.BlockSpec((CHUNK, D), lambda i: (rev(i), 0)),
                  pl.BlockSpec((CHUNK, QK), lambda i: (rev(i), OQ // QK)), pl.BlockSpec((CHUNK, QK), lambda i: (rev(i), OKK // QK)),
                  pl.BlockSpec((CHUNK, D), lambda i: (rev(i), OV // D)), pl.BlockSpec((CHUNK, APAD), lambda i: (rev(i), OA // APAD)),
                  pl.BlockSpec((None, HEADS, DV, DK), lambda i: (rev(i), 0, 0, 0)),
                  pl.BlockSpec((None, HEADS, DV, DK), lambda i: (jnp.maximum(rev(i) - 1, 0), 0, 0, 0)),
                  pl.BlockSpec((APAD, QK), lambda i: (0, 0)), pl.BlockSpec((1, QK), lambda i: (0, 0)), ANY],
        out_specs=[pl.BlockSpec((CHUNK, QK), lambda i: (rev(i), 0)), pl.BlockSpec((CHUNK, QK), lambda i: (rev(i), 0)),
                   pl.BlockSpec((CHUNK, D), lambda i: (rev(i), 0)), pl.BlockSpec((CHUNK, APAD), lambda i: (rev(i), 0)),
                   pl.BlockSpec((APAD, QK), lambda i: (0, 0)), pl.BlockSpec((1, QK), lambda i: (0, 0))],
        out_shape=[SDS((T, QK), BF16), SDS((T, QK), BF16), SDS((T, D), BF16), SDS((T, APAD), BF16),
                   SDS((APAD, QK), F32), SDS((1, QK), F32)],
        scratch_shapes=[pltpu.VMEM((HEADS, DV, DK), F32)],
        compiler_params=_cparams(32 * 1024 * 1024, ("arbitrary",)),
    )(do, pcat, pcat, pcat, pcat, states, states, wa, ba, after)


TMF = 256
_rowblk = ((TMF, D), lambda j, i, k: (i, 0))
_vec = ((1, D), lambda j, i, k: (0, 0))


def _full_spec(col):
    return ((TMF, D), lambda j, i, k: (i, col))


TBIG = 1024


def square_matmul(name, a, b, *, a_spec, b_spec, cdims, nk, after=None):
    def epi(acc, ex, outs, i):
        outs[0][...] = acc

    return matmul(name, a, b, a_spec=a_spec, b_spec=b_spec, cdims=cdims, grid=(D // TBIG, T // TBIG, nk),
                  acc_shape=(TBIG, TBIG), outs=[((T, D), F32, (TBIG, TBIG), lambda j, i, k: (i, j))], epi=epi,
                  after=after)[0]


def rowwise(name, y, *, extras, outs, epi):
    ne = len(extras)

    def body(*refs):
        epi(refs[0][...], refs[1:1 + ne], refs[1 + ne:], pl.program_id(1))

    in_specs = [pl.BlockSpec(*_rowblk)] + [pl.BlockSpec(bs, im) for _, bs, im in extras]
    return pl.pallas_call(
        body, name=name, grid=(1, T // TMF, 1), in_specs=in_specs,
        out_specs=[pl.BlockSpec(bs, im) for _, _, bs, im in outs], out_shape=[SDS(s, dt) for s, dt, _, _ in outs],
        compiler_params=_cparams(40 * 1024 * 1024, ("arbitrary", "arbitrary", "arbitrary")),
    )(y, *[arr for arr, _, _ in extras])


def mm_gla_out(og, w, ylin, pcat, pscale):
    def epi(acc, ex, outs, i):
        ylin_ref, lgp_ref, lgg_ref, ps_ref = ex
        gp = _sigmoid(lgp_ref[...].astype(F32))
        gg = _sigmoid(lgg_ref[...].astype(F32))
        outs[0][...] = (gp * (ylin_ref[...].astype(F32) * ps_ref[...]) + gg * acc).astype(BF16)
        outs[1][...] = acc.astype(BF16)

    return matmul("mm_gla_out", og, w, a_spec=_rowblk, b_spec=((D, D), lambda j, i, k: (0, 0)), cdims=NN,
                  grid=(1, T // TMF, 1), acc_shape=(TMF, D),
                  extras=[(ylin, *_rowblk), (pcat, *_full_spec(OGP // D)), (pcat, *_full_spec(OGG // D)), (pscale, *_vec)],
                  outs=[((T, D), BF16, *_rowblk), ((T, D), BF16, *_rowblk)], epi=epi)


def mm_out(mixed, w, x, g2):
    def epi(acc, ex, outs, i):
        x_ref, g_ref = ex
        x2 = x_ref[...] + acc
        r = lax.rsqrt(jnp.mean(x2 * x2, axis=-1, keepdims=True) + EPS)
        outs[0][...] = x2
        outs[1][...] = (x2 * r * g_ref[...]).astype(BF16)

    return matmul("mm_out", mixed, w, a_spec=_rowblk, b_spec=((D, D), lambda j, i, k: (0, 0)), cdims=NN,
                  grid=(1, T // TMF, 1), acc_shape=(TMF, D), extras=[(x, *_rowblk), (g2, *_vec)],
                  outs=[((T, D), F32, *_rowblk), ((T, D), BF16, *_rowblk)], epi=epi)


def mm_up(h2, wup):
    def epi(acc, ex, outs, i):
        r = jnp.maximum(acc, 0.0)
        outs[0][...] = r.astype(BF16)
        outs[1][...] = (r * r).astype(BF16)

    blk = ((TMF, D), lambda j, i, k: (i, j))
    return matmul("mm_up", h2, wup, a_spec=_rowblk, b_spec=((None, D, D), lambda j, i, k: (j, 0, 0)), cdims=NN,
                  grid=(NCHIP, T // TMF, 1), acc_shape=(TMF, D),
                  outs=[((T, DFF), BF16, *blk), ((T, DFF), BF16, *blk)], epi=epi)


def mm_down(act, wdown, x2, tgt, gf):
    tk = 2048

    def epi(acc, ex, outs, i):
        x2_ref, t_ref, g_ref = ex
        dx_ref, dxb_ref, gnf_ref, loss_ref = outs
        x3 = x2_ref[...] + acc
        r = lax.rsqrt(jnp.mean(x3 * x3, axis=-1, keepdims=True) + EPS)
        xn = x3 * r
        err = xn * g_ref[...] - t_ref[...]
        lsum = 0.5 * jnp.sum(jnp.mean(err * err, axis=-1, keepdims=True), axis=0, keepdims=True)
        dy = err * (1.0 / D)
        _row_acc(gnf_ref, jnp.sum(dy * xn, axis=0, keepdims=True), i)
        _row_acc(loss_ref, jnp.broadcast_to(lsum, (1, 128)), i)
        dx3 = _rms_bwd(xn, r, dy * g_ref[...])
        dx_ref[...] = dx3
        dxb_ref[...] = dx3.astype(BF16)

    y = square_matmul("mm_down", act, wdown, a_spec=((TBIG, tk), lambda j, i, k: (i, k)),
                      b_spec=((tk, TBIG), lambda j, i, k: (k, j)), cdims=NN, nk=DFF // tk)
    return rowwise("rows_final", y, extras=[(x2, *_rowblk), (tgt, *_rowblk), (gf, *_vec)],
                   outs=[((T, D), F32, *_rowblk), ((T, D), BF16, *_rowblk), ((1, D), F32, *_vec),
                         ((1, 128), F32, (1, 128), lambda j, i, k: (0, 0))], epi=epi)


def mm_dact(dx3b, wdown, rup, after=None):
    def epi(acc, ex, outs, i):
        outs[0][...] = (acc * 2.0 * ex[0][...].astype(F32)).astype(BF16)

    blk = ((TMF, D), lambda j, i, k: (i, j))
    return matmul("mm_dact", dx3b, wdown, a_spec=_rowblk, b_spec=((D, D), lambda j, i, k: (j, 0)), cdims=NT,
                  grid=(DFF // D, T // TMF, 1), acc_shape=(TMF, D), extras=[(rup, *blk)],
                  outs=[((T, DFF), BF16, *blk)], epi=epi, after=after)[0]


def mm_wgrad(name, a, b, m, n, out_shape, out_block, out_map, tm, tn):
    def epi(acc, ex, outs, i):
        outs[0][...] = acc.astype(BF16)

    return matmul(name, a, b, a_spec=((T, tm), lambda j, i, k: (0, i)), b_spec=((T, tn), lambda j, i, k: (0, j)),
                  cdims=TN, grid=(n // tn, m // tm, 1), acc_shape=(tm, tn),
                  outs=[(out_shape, BF16, out_block, out_map)], epi=epi)[0]


def mm_dh2(dup, wup, x2, dx3, g2):
    def epi(acc, ex, outs, i):
        x2_ref, dx3_ref, g_ref = ex
        x2 = x2_ref[...]
        r = lax.rsqrt(jnp.mean(x2 * x2, axis=-1, keepdims=True) + EPS)
        xn = x2 * r
        _row_acc(outs[2], jnp.sum(acc * xn, axis=0, keepdims=True), i)
        dx2 = dx3_ref[...] + _rms_bwd(xn, r, acc * g_ref[...])
        outs[0][...] = dx2
        outs[1][...] = dx2.astype(BF16)

    y = square_matmul("mm_dh2", dup, wup, a_spec=((TBIG, D), lambda j, i, k: (i, k)),
                      b_spec=((None, TBIG, D), lambda j, i, k: (k, j, 0)), cdims=NT, nk=NCHIP)
    return rowwise("rows_dh2", y, extras=[(x2, *_rowblk), (dx3, *_rowblk), (g2, *_vec)],
                   outs=[((T, D), F32, *_rowblk), ((T, D), BF16, *_rowblk), ((1, D), F32, *_vec)], epi=epi)


def mm_dmixed(dx2b, wout, pcat, ylin, ygla, pscale, after=None):
    def epi(acc, ex, outs, i):
        lgp_ref, lgg_ref, ylin_ref, ygla_ref, ps_ref = ex
        gp = _sigmoid(lgp_ref[...].astype(F32))
        gg = _sigmoid(lgg_ref[...].astype(F32))
        yl = ylin_ref[...].astype(F32)
        ps = ps_ref[...]
        agp = acc * gp
        outs[0][...] = (agp * ps).astype(BF16)
        outs[1][...] = (acc * gg).astype(BF16)
        outs[2][...] = (agp * (yl * ps) * (1.0 - gp)).astype(BF16)
        outs[3][...] = (acc * ygla_ref[...].astype(F32) * gg * (1.0 - gg)).astype(BF16)
        _row_acc(outs[4], jnp.sum(agp * yl, axis=0, keepdims=True), i)

    return matmul("mm_dmixed", dx2b, wout, a_spec=_rowblk, b_spec=((D, D), lambda j, i, k: (0, 0)), cdims=NT,
                  grid=(1, T // TMF, 1), acc_shape=(TMF, D),
                  extras=[(pcat, *_full_spec(OGP // D)), (pcat, *_full_spec(OGG // D)), (ylin, *_rowblk), (ygla, *_rowblk),
                          (pscale, *_vec)],
                  outs=[((T, D), BF16, *_rowblk)] * 4 + [((1, D), F32, *_vec)], epi=epi, after=after)


def mm_dog(dygla, wgo, o, pcat, ng):
    def epi(acc, ex, outs, i):
        o_ref, g_ref, ng_ref = ex
        do_ref, dg_ref, gng_ref = outs
        gparts = []
        for h in range(HEADS):
            cv = slice(h * DV, (h + 1) * DV)
            oh = o_ref[:, cv].astype(F32)
            r = lax.rsqrt(jnp.mean(oh * oh, axis=-1, keepdims=True) + EPS)
            on = oh * r
            gv = g_ref[:, cv].astype(F32)
            sg = _sigmoid(gv)
            dgain = acc[:, cv] * (gv * sg)
            gparts.append(jnp.sum(dgain * on, axis=0, keepdims=True))
            ngh = ng_ref[:, cv]
            do_ref[:, cv] = _rms_bwd(on, r, dgain * ngh).astype(BF16)
            dg_ref[:, cv] = (acc[:, cv] * (on * ngh) * (sg * (1.0 + gv * (1.0 - sg)))).astype(BF16)
        _row_acc(gng_ref, jnp.concatenate(gparts, axis=1), i)

    return matmul("mm_dog", dygla, wgo, a_spec=_rowblk, b_spec=((D, D), lambda j, i, k: (0, 0)), cdims=NT,
                  grid=(1, T // TMF, 1), acc_shape=(TMF, D),
                  extras=[(o, *_rowblk), (pcat, *_full_spec(OG // D)), (ng, *_vec)],
                  outs=[((T, D), BF16, *_rowblk), ((T, D), BF16, *_rowblk), ((1, D), F32, *_vec)], epi=epi)


def mm_dh1(dpcat, wcat, x, dx2, g1, after=None):
    tk = 1280

    def epi(acc, ex, outs, i):
        x_ref, dx2_ref, g_ref = ex
        xv = x_ref[...]
        r = lax.rsqrt(jnp.mean(xv * xv, axis=-1, keepdims=True) + EPS)
        xn = xv * r
        _row_acc(outs[1], jnp.sum(acc * xn, axis=0, keepdims=True), i)
        outs[0][...] = dx2_ref[...] + _rms_bwd(xn, r, acc * g_ref[...])

    y = square_matmul("mm_dh1", dpcat, wcat, a_spec=((TBIG, tk), lambda j, i, k: (i, k)),
                      b_spec=((TBIG, tk), lambda j, i, k: (j, k)), cdims=NT, nk=NCAT // tk, after=after)
    return rowwise("rows_dh1", y, extras=[(x, *_rowblk), (dx2, *_rowblk), (g1, *_vec)],
                   outs=[((T, D), F32, *_rowblk), ((1, D), F32, *_vec)], epi=epi)


def _tile_rows(rows, cols, n_arrays):
    tm = rows
    while tm % 32 == 0 and 2 * n_arrays * tm * cols * 4 > 24 * 1024 * 1024:
        tm //= 2
    return tm


def add_pairs(name, parts, theirs, core):
    _, _, r, c = parts.shape
    tm = _tile_rows(r, c, 3)

    def body(core_ref, a_ref, b_ref, o_ref):
        o_ref[...] = (a_ref[...].astype(F32) + b_ref[...].astype(F32)).astype(BF16)

    spec = pl.BlockSpec((None, tm, c), lambda j, i, core_ref: (j, i, 0))
    grid_spec = pltpu.PrefetchScalarGridSpec(
        num_scalar_prefetch=1, grid=(NCHIP, r // tm),
        in_specs=[pl.BlockSpec((None, None, tm, c), lambda j, i, core_ref: (core_ref[0], j, i, 0)), spec], out_specs=spec)
    return pl.pallas_call(body, name=name, grid_spec=grid_spec, out_shape=SDS((NCHIP, r, c), BF16),
                          compiler_params=_cparams(40 * 1024 * 1024, ("arbitrary", "arbitrary")))(core, parts, theirs)


def sum_chips(name, sums, landed, chip):
    _, r, c = sums.shape
    tm = _tile_rows(r, c, 4)

    def body(chip_ref, own_ref, l_ref, o_ref):
        s = own_ref[...].astype(F32)
        for t in range(NCHIP - 1):
            s = s + l_ref[t].astype(F32)
        o_ref[...] = s

    grid_spec = pltpu.PrefetchScalarGridSpec(
        num_scalar_prefetch=1, grid=(r // tm,),
        in_specs=[pl.BlockSpec((None, tm, c), lambda i, chip_ref: (chip_ref[0], i, 0)),
                  pl.BlockSpec((NCHIP - 1, tm, c), lambda i, chip_ref: (0, i, 0))],
        out_specs=pl.BlockSpec((tm, c), lambda i, chip_ref: (i, 0)))
    return pl.pallas_call(body, name=name, grid_spec=grid_spec, out_shape=SDS((r, c), F32),
                          compiler_params=_cparams(40 * 1024 * 1024, ("arbitrary",)))(chip, sums, landed)


def _adamw_math(wv, gv, mv, vv):
    mn = ADAM_B1 * mv + (1.0 - ADAM_B1) * gv
    vn = ADAM_B2 * vv + (1.0 - ADAM_B2) * (gv * gv)
    mh = mn / (1.0 - ADAM_B1 ** ADAM_STEP)
    vh = vn / (1.0 - ADAM_B2 ** ADAM_STEP)
    return -ADAM_LR * (mh / (jnp.sqrt(vh) + ADAM_EPS) + ADAM_WD * wv), mn, vn


def adamw(name, w, g, m, v):
    def body(w_ref, g_ref, m_ref, v_ref, go_ref, d_ref, mo_ref, vo_ref):
        gv = g_ref[...]
        go_ref[...] = gv
        d_ref[...], mo_ref[...], vo_ref[...] = _adamw_math(w_ref[...], gv, m_ref[...], v_ref[...])

    return pl.pallas_call(body, name=name, out_shape=[SDS(w.shape, F32)] * 4)(w, g, m, v)


def adamw_halves(name, w, g_own, g_sib, m, v, core):
    _, r, c = w.shape
    tm = _tile_rows(r, c, 10)

    def body(core_ref, w_ref, go_ref, gs_ref, m_ref, v_ref, g_out, d_out, m_out, v_out):
        gv = jnp.where(pl.program_id(0) == core_ref[0], go_ref[...], gs_ref[...])
        g_out[...] = gv
        d_out[...], m_out[...], v_out[...] = _adamw_math(w_ref[...], gv, m_ref[...], v_ref[...])

    full = pl.BlockSpec((None, tm, c), lambda h, i, core_ref: (h, i, 0))
    own = pl.BlockSpec((tm, c), lambda h, i, core_ref: (jnp.where(h == core_ref[0], i, 0), 0))
    sib = pl.BlockSpec((tm, c), lambda h, i, core_ref: (jnp.where(h == core_ref[0], 0, i), 0))
    grid_spec = pltpu.PrefetchScalarGridSpec(num_scalar_prefetch=1, grid=(2, r // tm),
                                             in_specs=[full, own, sib, full, full], out_specs=[full] * 4)
    return pl.pallas_call(body, name=name, grid_spec=grid_spec, out_shape=[SDS(w.shape, F32)] * 4,
                          compiler_params=_cparams(48 * 1024 * 1024, ("arbitrary", "arbitrary")))(core, w, g_own, g_sib, m, v)


def pack_rows(name, parts, rows):
    width = parts[0].shape[1]
    n = len(parts)

    def body(*refs):
        out_ref = refs[n]
        out_ref[...] = jnp.zeros_like(out_ref)
        off = 0
        for p in refs[:n]:
            out_ref[off:off + p.shape[0], :] = p[...]
            off += p.shape[0]

    return pl.pallas_call(body, name=name, out_shape=SDS((rows, width), F32))(*parts)


def _place():
    x, y, c = lax.axis_index("x"), lax.axis_index("y"), lax.axis_index("c")
    chips = [(1 - x, y), (x, 1 - y), (1 - x, 1 - y)]
    return x, y, c, chips


def _row_split(shape, dtype):
    r, c = shape
    n = 1
    while r % (2 * n) == 0 and (r // (2 * n)) % 16 == 0 and (r // n) * c * jnp.dtype(dtype).itemsize > PIECE_BYTES:
        n *= 2
    return [pl.ds(s * (r // n), r // n) for s in range(n)]


def _pieces(ref):
    *lead, r, c = ref.shape
    split = _row_split((r, c), ref.dtype)
    return [ref.at[(*idx, s)] for idx in itertools.product(*[range(d) for d in lead]) for s in split]


HBM = pl.BlockSpec(memory_space=pltpu.HBM)
SEM = pl.BlockSpec(memory_space=pltpu.SEMAPHORE)
EFFECT = pltpu.SideEffectType.DATAFLOW_SIDE_EFFECTING


def gather_start(name, shards, after=None):
    n = len(shards)
    extra = [] if after is None else [after]

    def body(*refs):
        src, land = refs[:n], refs[n:2 * n]
        send, recv = refs[2 * n + len(extra)], refs[2 * n + len(extra) + 1]
        x, y, c, chips = _place()
        me = 2 * x + y
        for a in range(n):
            for j, (cx, cy) in enumerate(chips):
                for sp, dp in zip(_pieces(src[a].at[c]), _pieces(land[a].at[me, c])):
                    pltpu.make_async_remote_copy(sp, dp, send.at[3 * a + j], recv.at[3 * a + j],
                                                 device_id=(cx, cy, c), device_id_type=MESH).start()

    lands = [pltpu.with_memory_space_constraint(lax.empty((NCHIP,) + s.shape, s.dtype), pltpu.HBM) for s in shards]
    srcs = [pltpu.with_memory_space_constraint(s, pltpu.HBM) for s in shards]
    outs = pl.pallas_call(
        body, name=name,
        out_shape=(pltpu.SemaphoreType.DMA((3 * n,)), pltpu.SemaphoreType.DMA((3 * n,)),
                   *[pltpu.HBM(s.shape, s.dtype) for s in shards], *[pltpu.HBM(l.shape, l.dtype) for l in lands]),
        in_specs=[HBM] * (2 * n) + [ANY] * len(extra), out_specs=(SEM, SEM, *([HBM] * (2 * n))),
        input_output_aliases={i: 2 + i for i in range(2 * n)},
        compiler_params=pltpu.CompilerParams(has_side_effects=EFFECT),
    )(*srcs, *lands, *extra)
    return outs[0], outs[1], list(outs[2:2 + n]), list(outs[2 + n:2 + 2 * n])


def gather_wait(name, send, recv, shards, lands, after):
    n = len(shards)

    def body(*refs):
        src, land = refs[:n], refs[n:2 * n]
        send_ref, recv_ref = refs[2 * n], refs[2 * n + 1]
        x, y, c, chips = _place()
        for a in range(n):
            for j, (cx, cy) in enumerate(chips):
                cp = pltpu.make_async_remote_copy(src[a].at[c], land[a].at[2 * cx + cy, c], send_ref.at[3 * a + j],
                                                  recv_ref.at[3 * a + j], device_id=(cx, cy, c), device_id_type=MESH)
                cp.wait_send()
                cp.wait_recv()

    outs = pl.pallas_call(
        body, name=name,
        out_shape=(*[pltpu.HBM(s.shape, s.dtype) for s in shards], *[pltpu.HBM(l.shape, l.dtype) for l in lands]),
        in_specs=[HBM] * (2 * n) + [SEM, SEM, ANY], out_specs=[HBM] * (2 * n),
        input_output_aliases={i: i for i in range(2 * n)},
        compiler_params=pltpu.CompilerParams(has_side_effects=EFFECT),
    )(*shards, *lands, send, recv, after)
    return list(outs[:n]), list(outs[n:])


def forward_halves(name, shards, lands):
    n = len(lands)

    def body(*refs):
        had, buf = refs[:n], refs[n:2 * n]
        send, recv = refs[2 * n:]
        x, y, c, chips = _place()
        sib = (x, y, 1 - c)
        for a in range(n):
            for j, (cx, cy) in enumerate(chips):
                for sp, dp in zip(_pieces(had[a].at[2 * cx + cy, c]), _pieces(buf[a].at[2 * cx + cy, c])):
                    pltpu.make_async_remote_copy(sp, dp, send.at[3 * a + j], recv.at[3 * a + j], device_id=sib, device_id_type=MESH).start()
        for a in range(n):
            for j, (cx, cy) in enumerate(chips):
                pltpu.make_async_remote_copy(had[a].at[2 * cx + cy, c], buf[a].at[2 * cx + cy, 1 - c], send.at[3 * a + j],
                                             recv.at[3 * a + j], device_id=sib, device_id_type=MESH).wait()

    got = pl.pallas_call(
        body, name=name, in_specs=[ANY] * n, out_specs=[ANY] * n, out_shape=[SDS(l.shape, l.dtype) for l in lands],
        input_output_aliases={i: i for i in range(n)},
        scratch_shapes=[pltpu.SemaphoreType.DMA((3 * n,)), pltpu.SemaphoreType.DMA((3 * n,))],
    )(*lands)
    me = 2 * lax.axis_index("x") + lax.axis_index("y")
    return [lax.dynamic_update_index_in_dim(g, s, me, 0) for g, s in zip(got, shards)]


def exchange_halves(name, parts):
    n = len(parts)

    def body(*refs):
        src, got = refs[:n], refs[n:2 * n]
        send, recv = refs[2 * n:]
        x, y, c, _ = _place()
        sib = (x, y, 1 - c)
        for a in range(n):
            for sp, dp in zip(_pieces(src[a].at[1 - c]), _pieces(got[a])):
                pltpu.make_async_remote_copy(sp, dp, send.at[a], recv.at[a], device_id=sib, device_id_type=MESH).start()
        for a in range(n):
            pltpu.make_async_remote_copy(src[a].at[1 - c], got[a], send.at[a], recv.at[a], device_id=sib, device_id_type=MESH).wait()

    return pl.pallas_call(
        body, name=name, in_specs=[ANY] * n, out_specs=[ANY] * n,
        out_shape=[SDS(p.shape[1:], p.dtype) for p in parts],
        scratch_shapes=[pltpu.SemaphoreType.DMA((n,)), pltpu.SemaphoreType.DMA((n,))],
    )(*parts)


def scatter_start(name, parts):
    n = len(parts)

    def body(*refs):
        src, land = refs[:n], refs[n:2 * n]
        send, recv = refs[2 * n], refs[2 * n + 1]
        token = refs[4 * n + 2]
        x, y, c, chips = _place()
        for a in range(n):
            for j, (cx, cy) in enumerate(chips):
                for sp, dp in zip(_pieces(src[a].at[2 * cx + cy]), _pieces(land[a].at[j])):
                    pltpu.make_async_remote_copy(sp, dp, send.at[3 * a + j], recv.at[3 * a + j],
                                                 device_id=(cx, cy, c), device_id_type=MESH).start()
        token[...] = jnp.zeros_like(token)

    lands = [pltpu.with_memory_space_constraint(lax.empty((NCHIP - 1,) + p.shape[1:], p.dtype), pltpu.HBM) for p in parts]
    srcs = [pltpu.with_memory_space_constraint(p, pltpu.HBM) for p in parts]
    outs = pl.pallas_call(
        body, name=name,
        out_shape=(pltpu.SemaphoreType.DMA((3 * n,)), pltpu.SemaphoreType.DMA((3 * n,)),
                   *[pltpu.HBM(p.shape, p.dtype) for p in parts], *[pltpu.HBM(l.shape, l.dtype) for l in lands],
                   SDS((8, 128), F32)),
        in_specs=[HBM] * (2 * n), out_specs=(SEM, SEM, *([HBM] * (2 * n)), pl.BlockSpec(memory_space=pltpu.VMEM)),
        input_output_aliases={i: 2 + i for i in range(2 * n)},
        compiler_params=pltpu.CompilerParams(has_side_effects=EFFECT),
    )(*srcs, *lands)
    return outs[0], outs[1], list(outs[2:2 + n]), list(outs[2 + n:2 + 2 * n]), outs[2 + 2 * n]


def scatter_wait(name, send, recv, parts, lands, after):
    n = len(parts)

    def body(*refs):
        src, land = refs[:n], refs[n:2 * n]
        send_ref, recv_ref = refs[2 * n], refs[2 * n + 1]
        x, y, c, chips = _place()
        for a in range(n):
            for j, (cx, cy) in enumerate(chips):
                cp = pltpu.make_async_remote_copy(src[a].at[2 * cx + cy], land[a].at[j], send_ref.at[3 * a + j], recv_ref.at[3 * a + j],
                                                  device_id=(cx, cy, c), device_id_type=MESH)
                cp.wait_send()
                cp.wait_recv()

    outs = pl.pallas_call(
        body, name=name,
        out_shape=(*[pltpu.HBM(p.shape, p.dtype) for p in parts], *[pltpu.HBM(l.shape, l.dtype) for l in lands]),
        in_specs=[HBM] * (2 * n) + [SEM, SEM, ANY], out_specs=[HBM] * (2 * n),
        input_output_aliases={i: i for i in range(2 * n)},
        compiler_params=pltpu.CompilerParams(has_side_effects=EFFECT),
    )(*parts, *lands, send, recv, after)
    return list(outs[:n]), list(outs[n:])


def join_halves(name, halves):
    n = len(halves)

    def body(*refs):
        src, dst = refs[:n], refs[n:2 * n]
        send, recv = refs[2 * n:]
        x, y, c, _ = _place()
        sib = (x, y, 1 - c)
        for a in range(n):
            for sp, dp in zip(_pieces(src[a]), _pieces(dst[a])):
                pltpu.make_async_remote_copy(sp, dp, send.at[a], recv.at[a], device_id=sib, device_id_type=MESH).start()
        for a in range(n):
            pltpu.make_async_remote_copy(src[a], dst[a], send.at[a], recv.at[a], device_id=sib, device_id_type=MESH).wait()

    return pl.pallas_call(
        body, name=name, in_specs=[ANY] * n, out_specs=[ANY] * n,
        out_shape=[SDS(h.shape, h.dtype) for h in halves],
        scratch_shapes=[pltpu.SemaphoreType.DMA((n,)), pltpu.SemaphoreType.DMA((n,))],
    )(*halves)


def gather_small(name, xs, reduce):
    m, ncol = xs.shape

    def body(x_ref, out_ref, all_ref, send, recv, lsem):
        x, y, c, chips = _place()
        me, sib = (x, y, c), (x, y, 1 - c)

        def rows(px, py, pc):
            return all_ref.at[pl.ds((4 * px + 2 * py + pc) * m, m), :]

        def copy(k, block, to, src=None):
            return pltpu.make_async_remote_copy(rows(*block) if src is None else src, rows(*block), send.at[k], recv.at[k],
                                                device_id=to, device_id_type=MESH)

        mine = pltpu.make_async_copy(x_ref, rows(*me), lsem)
        mine.start()
        first = [copy(0, me, sib, src=x_ref)] + [copy(1 + j, me, (*chip, c), src=x_ref) for j, chip in enumerate(chips)]
        for cp in first:
            cp.start()
        passed = [copy(4 + j, (*chip, c), sib) for j, chip in enumerate(chips)]
        for j, chip in enumerate(chips):
            copy(1 + j, (*chip, c), me).wait_recv()
            passed[j].start()
        copy(0, sib, me).wait_recv()
        for j, chip in enumerate(chips):
            copy(4 + j, (*chip, 1 - c), me).wait_recv()
        for cp in first + passed:
            cp.wait_send()
        mine.wait()
        if reduce:
            s = all_ref[0:m, :]
            for dev in range(1, 8):
                s = s + all_ref[dev * m:(dev + 1) * m, :]
            out_ref[...] = s
        else:
            out_ref[...] = all_ref[...]

    vm = pl.BlockSpec(memory_space=pltpu.VMEM)
    return pl.pallas_call(
        body, name=name, in_specs=[vm], out_specs=vm, out_shape=SDS((m, ncol) if reduce else (8 * m, ncol), F32),
        scratch_shapes=[pltpu.VMEM((8 * m, ncol), F32), pltpu.SemaphoreType.DMA((7,)), pltpu.SemaphoreType.DMA((7,)),
                        pltpu.SemaphoreType.DMA],
    )(xs)


RELAYOUT_ROWS = 128


def weights_to_cat(g_in):
    tm = RELAYOUT_ROWS

    def body(g_ref, o_ref):
        nat = jnp.concatenate([g_ref[j] for j in range(NCHIP)], axis=1)
        pad = jnp.zeros((tm, NCAT - OA - 16), BF16)
        o_ref[...] = jnp.concatenate([nat[:, 3072:7168], nat[:, 7184:11280], nat[:, 0:3072], nat[:, 7168:7184], pad], axis=1)

    return pl.pallas_call(
        body, name="weights_to_cat", grid=(D // tm,), in_specs=[pl.BlockSpec((NCHIP, tm, IN_SHARD), lambda i: (0, i, 0))],
        out_specs=pl.BlockSpec((tm, NCAT), lambda i: (i, 0)), out_shape=SDS((D, NCAT), BF16),
        compiler_params=_cparams(40 * 1024 * 1024, ("arbitrary",)),
    )(g_in)


def grads_from_cat(gw_cat):
    tm = RELAYOUT_ROWS
    nb = (D // 2) // tm

    def body(c_ref, o_ref):
        cat = c_ref[...]
        nat = jnp.concatenate([cat[:, OU:OA], cat[:, OV:OGP], cat[:, OA:OA + 16], cat[:, OGP:OU]], axis=1)
        for j in range(NCHIP):
            o_ref[j] = nat[:, j * IN_SHARD:(j + 1) * IN_SHARD]

    return pl.pallas_call(
        body, name="grads_from_cat", grid=(D // tm,), in_specs=[pl.BlockSpec((tm, NCAT), lambda i: (i, 0))],
        out_specs=pl.BlockSpec((None, NCHIP, tm, IN_SHARD), lambda i: (i // nb, 0, i % nb, 0)),
        out_shape=SDS((2, NCHIP, D // 2, IN_SHARD), BF16), compiler_params=_cparams(40 * 1024 * 1024, ("arbitrary",)),
    )(gw_cat)


def _pad_rows(a, rows):
    return jnp.concatenate([a, jnp.zeros((rows - a.shape[0],) + a.shape[1:], a.dtype)], axis=0)


def local_step(x2d, tgt, gf, g1, pool_scale, wa_pad, b_alpha, ng, g2, get_w, on_grad=None):
    emit = on_grad if on_grad is not None else (lambda group, grads: None)
    h1 = norm1(x2d, g1)
    wcat, pw = get_w("in", h1)
    pcat = mm_in(h1, wcat)
    dpool, ylin = pool_fwd(pcat, pw)
    og, o, states = gla_fwd(pcat, wa_pad, b_alpha, ng)
    w_go, w_o = get_w("mid", og)
    mixed, ygla = mm_gla_out(og, w_go, ylin, pcat, pool_scale)
    x2, h2 = mm_out(mixed, w_o, x2d, g2)
    w_up = get_w("up", h2)
    rup, act = mm_up(h2, w_up)
    w_dn = get_w("down", act)
    dx3, dx3b, g_nf, loss_row = mm_down(act, w_dn, x2, tgt, gf)

    gw_down = mm_wgrad("mm_dw_down", act, dx3b, DFF, D, (2, NCHIP, D // 2, D), (None, None, 512, D),
                       lambda j, i, k: ((i // 2) % 2, i // 4, i % 2, 0), 512, D)
    token = emit("down", {"down": gw_down})
    dup = mm_dact(dx3b, w_dn, rup, after=token)
    dx2, dx2b, g_mlp = mm_dh2(dup, w_up, x2, dx3, g2)
    gw_up = mm_wgrad("mm_dw_up", h2, dup, D, DFF, (2, NCHIP, D // 2, D), (None, None, 512, D),
                     lambda j, i, k: (i // 2, j, i % 2, 0), 512, D)
    token = emit("up", {"up": gw_up})
    dylin, dygla, dlgp, dlgg, g_ps = mm_dmixed(dx2b, w_o, pcat, ylin, ygla, pool_scale, after=token)
    gw_out = mm_wgrad("mm_dw_out", mixed, dx2b, D, D, (2, NCHIP, 256, D), (None, None, 256, D),
                      lambda j, i, k: (i % 2, i // 2, 0, 0), 256, D)
    do, dg, g_ng = mm_dog(dygla, w_go, o, pcat, ng)
    gw_go = mm_wgrad("mm_dw_gla_out", og, dygla, D, D, (2, NCHIP, 256, D), (None, None, 256, D),
                     lambda j, i, k: (i % 2, i // 2, 0, 0), 256, D)
    token = emit("mix", {"out": gw_out, "gla_out": gw_go})
    dq, dk, dv, dalow, g_wa, g_ba = gla_bwd(do, pcat, states, wa_pad, b_alpha, b_alpha if token is None else token)
    du, dpw = pool_bwd(dylin, dpool, pw)
    dpcat = jnp.concatenate([dv, dg, dlgp, dlgg, du, dq, dk, dalow, jnp.zeros((T, NCAT - OA - APAD), BF16)], axis=1)
    gw_cat = mm_wgrad("mm_dw_in", h1, dpcat, D, NCAT, (D, NCAT), (512, 1280), lambda j, i, k: (i, j), 512, 1280)
    token = emit("in", {"in_cat": gw_cat, "pool": dpw})
    grad_x, g_mix = mm_dh1(dpcat, wcat, x2d, dx2, g1, after=token)
    return (loss_row[0, 0], grad_x, g_mix, g_ps, g_mlp, g_nf, g_ng, g_ba, g_wa, token,
            gw_cat, dpw, gw_go, gw_out, gw_up, gw_down)


def kernel(x, norm_mix_g, w_in, pool_w, pool_scale, w_alpha, b_alpha, gla_norm_g, w_gla_out, w_out, norm_mlp_g, w_mlp_up, w_mlp_down, norm_final_g, loss_target, m_norm_mix_g, m_w_in, m_pool_w, m_pool_scale, m_w_alpha, m_b_alpha, m_gla_norm_g, m_w_gla_out, m_w_out, m_norm_mlp_g, m_w_mlp_up, m_w_mlp_down, m_norm_final_g, v_norm_mix_g, v_w_in, v_pool_w, v_pool_scale, v_w_alpha, v_b_alpha, v_gla_norm_g, v_w_gla_out, v_w_out, v_norm_mlp_g, v_w_mlp_up, v_w_mlp_down, v_norm_final_g):
    chip = 2 * lax.axis_index("x") + lax.axis_index("y")
    chip_i = chip.astype(jnp.int32).reshape(1)
    core_i = lax.axis_index("c").astype(jnp.int32).reshape(1)
    x2d = x.reshape(T, D)
    tgt = loss_target.reshape(T, D)
    gf = norm_final_g.reshape(1, D)

    def halves(w2d):
        r, c = w2d.shape
        return w2d.astype(BF16).reshape(2, r // 2, c)

    pool_shard = pool_w.reshape(4 * PG, PO // NCHIP)
    big = [w_in[0], w_gla_out[0], w_out[0], w_mlp_up[0], w_mlp_down[0], pool_shard]
    groups = {"in": [big[0], big[5]], "mid": [big[1], big[2]], "up": [big[3]], "down": [big[4]]}
    started = {}

    def start(group, after=None):
        started[group] = gather_start("gather_start_" + group, [halves(w) for w in groups[group]], after)

    start("in")

    def get_w(group, after):
        send, recv, shards, lands = started[group]
        shards, lands = gather_wait("gather_wait_" + group, send, recv, shards, lands, after)
        if group == "in":
            start("mid", lands[0])
            start("up", started["mid"][3][0])
        if group == "mid":
            start("down", lands[0])
        whole = forward_halves("forward_" + group, shards, lands)
        if group == "in":
            g_in, g_pool = whole
            wcat = weights_to_cat(g_in.reshape(NCHIP, D, IN_SHARD))
            pw = jnp.concatenate([g_pool[j].reshape(4, PG, PO // NCHIP) for j in range(NCHIP)], axis=2)
            return wcat, pw
        if group == "mid":
            return whole[0].reshape(D, D), whole[1].reshape(D, D)
        if group == "up":
            return whole[0].reshape(NCHIP, D, D)
        return whole[0].reshape(DFF, D)

    small_w = pack_rows("pack_small_w", [w_alpha[0].reshape(4, QK),
                                         jnp.concatenate([gla_norm_g[0].reshape(1, 512), jnp.zeros((1, 512), F32)], axis=1)], 8)
    sw_all = gather_small("gather_small_w", small_w, False).reshape(8, 8, QK)
    wa_full = jnp.concatenate([sw_all[2 * j, 0:4].reshape(16, DK) for j in range(NCHIP)], axis=1)
    ng_full = jnp.concatenate([sw_all[2 * j, 4, 0:512].reshape(HEADS, DV // NCHIP) for j in range(NCHIP)], axis=1)
    wa_pad = _pad_rows(wa_full, APAD).astype(BF16)
    ng = ng_full.reshape(1, D)

    pending = {}
    wmv = {"in": (big[0], m_w_in, v_w_in), "gla_out": (big[1], m_w_gla_out, v_w_gla_out), "out": (big[2], m_w_out, v_w_out),
           "up": (big[3], m_w_mlp_up, v_w_mlp_up), "down": (big[4], m_w_mlp_down, v_w_mlp_down), "pool": (big[5], m_pool_w, v_pool_w)}
    big_res = {}

    def finish(group, after):
        nms, send, recv, sums, lands = pending[group]
        sums, lands = scatter_wait("scatter_wait_" + group, send, recv, sums, lands, after)
        reduced = [sum_chips("sum_chips_" + nm, a, b, chip_i) for nm, a, b in zip(nms, sums, lands)]
        from_sib = join_halves("join_" + group, reduced)
        for nm, g_own, g_sib in zip(nms, reduced, from_sib):
            w, m, v = wmv[nm]
            shp = (2,) + g_own.shape
            big_res[nm] = adamw_halves("adamw_" + nm, w.reshape(shp), g_own, g_sib, m.reshape(shp), v.reshape(shp), core_i)

    def on_grad(group, grads):
        if group == "in":
            gw_in = grads_from_cat(grads["in_cat"])
            gw_pool = jnp.stack([grads["pool"][:, :, j * 128:(j + 1) * 128].reshape(2, 2 * PG, 128)
                                 for j in range(NCHIP)], axis=1)
            grads = {"in": gw_in, "pool": gw_pool}
        nms, parts = list(grads.keys()), list(grads.values())
        theirs = exchange_halves("exchange_" + group, parts)
        sums = [add_pairs("add_pair_" + nm, a, b, core_i) for nm, a, b in zip(nms, parts, theirs)]
        send, recv, sums, lands, token = scatter_start("scatter_start_" + group, sums)
        pending[group] = (nms, send, recv, sums, lands)
        if group != "in":
            return token
        for earlier in ("down", "up", "mix"):
            finish(earlier, token)
        return big_res["gla_out"][1]

    (loss_local, grad_x, g_mix, g_ps, g_mlp, g_nf, g_ng, g_ba, g_wa) = local_step(
        x2d, tgt, gf, norm_mix_g, pool_scale, wa_pad, b_alpha, ng, norm_mlp_g, get_w, on_grad)[:9]
    loss = lax.psum(loss_local, ("x", "y", "c"))
    finish("in", grad_x)

    ROWS = 16

    def wide(a, n):
        return jnp.concatenate([a.reshape(1, n), jnp.zeros((1, D - n), F32)], axis=1)

    packed = pack_rows("pack_small_g", [g_mix, g_ps, g_mlp, g_nf, g_ng, wide(g_ba, QK), g_wa[0:16].reshape(8, D)], ROWS)
    tot = gather_small("reduce_small_g", packed, True)
    t_wa = lax.dynamic_slice(tot[6:14].reshape(16, QK), (0, chip * DK), (16, DK))
    t_ng = lax.dynamic_slice(tot[4].reshape(HEADS, DV), (0, chip * (DV // NCHIP)), (HEADS, DV // NCHIP))

    def pack_small(nm, mix, ps, mlp, nf, ba, wa, gn):
        return pack_rows(nm, [mix.reshape(1, D), ps.reshape(1, D), mlp.reshape(1, D), nf.reshape(1, D), wide(ba, QK),
                              wa.reshape(2, D), wide(gn, 512)], ROWS)

    sg = pack_small("pack_g", tot[0], tot[1], tot[2], tot[3], tot[5, 0:QK], t_wa, t_ng)
    sw = pack_small("pack_w", norm_mix_g, pool_scale, norm_mlp_g, norm_final_g, b_alpha, w_alpha, gla_norm_g)
    sm = pack_small("pack_m", m_norm_mix_g, m_pool_scale, m_norm_mlp_g, m_norm_final_g, m_b_alpha, m_w_alpha, m_gla_norm_g)
    sv = pack_small("pack_v", v_norm_mix_g, v_pool_scale, v_norm_mlp_g, v_norm_final_g, v_b_alpha, v_w_alpha, v_gla_norm_g)
    small_res = adamw("adamw_small", sw, sg, sm, sv)

    def unpack(p):
        return {"norm_mix_g": p[0].reshape(1, D), "pool_scale": p[1].reshape(1, D), "norm_mlp_g": p[2].reshape(1, D),
                "norm_final_g": p[3].reshape(D), "b_alpha": p[4, 0:QK].reshape(1, QK), "w_alpha": p[5:7].reshape(1, 16, DK),
                "gla_norm_g": p[7, 0:512].reshape(1, HEADS, DV // NCHIP)}

    order = ["norm_mix_g", "w_in", "pool_w", "pool_scale", "w_alpha", "b_alpha", "gla_norm_g", "w_gla_out", "w_out",
             "norm_mlp_g", "w_mlp_up", "w_mlp_down", "norm_final_g"]
    big_key = {"w_in": ("in", w_in.shape), "pool_w": ("pool", pool_w.shape), "w_gla_out": ("gla_out", w_gla_out.shape),
               "w_out": ("out", w_out.shape), "w_mlp_up": ("up", w_mlp_up.shape), "w_mlp_down": ("down", w_mlp_down.shape)}
    result = [loss, grad_x.reshape(1, T, D)]
    for kind in range(4):
        small = unpack(small_res[kind])
        for nm in order:
            if nm in big_key:
                key, shp = big_key[nm]
                result.append(big_res[key][kind].reshape(shp))
            else:
                result.append(small[nm])
    return tuple(result)
```

```python
import itertools

import jax
import jax.numpy as jnp
from jax import lax
from jax.experimental import pallas as pl
from jax.experimental.pallas import tpu as pltpu

F32 = jnp.float32
BF16 = jnp.bfloat16
SDS = jax.ShapeDtypeStruct
MESH = pl.DeviceIdType.MESH
ANY = pl.BlockSpec(memory_space=pl.ANY)

T = 2048
D = 2048
DFF = 8192
NCHIP = 4
IN_WIDTH = 11280
IN_SHARD = IN_WIDTH // NCHIP
CHUNK = 64
NCHUNK = T // CHUNK
HEADS = 4
DK = 256
DV = 512
QK = HEADS * DK
EPS = 1e-6
POOL_WINDOWS = (2, 4, 8, 16)
PG = 256
PO = 512

OV, OG, OGP, OGG, OU, OQ, OKK, OA = 0, 2048, 4096, 6144, 8192, 9216, 10240, 11264
NCAT = 11520
APAD = 128

VMEM_CAP = 56 * 1024 * 1024

PIECE_BYTES = 3 * 512 * 1024

ADAM_LR, ADAM_B1, ADAM_B2, ADAM_EPS, ADAM_WD, ADAM_STEP = 0.001, 0.9, 0.999, 1e-08, 0.01, 10


def _cparams(vmem_bytes=None, sem=None):
    kw = {}
    if vmem_bytes is not None:
        kw["vmem_limit_bytes"] = int(min(max(vmem_bytes, 32 * 1024 * 1024), VMEM_CAP))
    if sem is not None:
        kw["dimension_semantics"] = sem
    return pltpu.CompilerParams(**kw)


def _nbytes(shape, dtype):
    n = 1
    for s in shape:
        if s is not None:
            n *= s
    return n * jnp.dtype(dtype).itemsize


def _sigmoid(x):
    return 1.0 / (1.0 + jnp.exp(-x))


def _as_list(after):
    if after is None:
        return []
    return list(after) if isinstance(after, (list, tuple)) else [after]


def matmul(name, a, b, *, a_spec, b_spec, cdims, grid, acc_shape, outs, extras=(), epi, after=None):
    nj, ni, nk = grid
    ne, no = len(extras), len(outs)
    afters = _as_list(after)
    first_out = 2 + ne + len(afters)

    def body(*refs):
        a_ref, b_ref = refs[0], refs[1]
        ex = refs[2:2 + ne]
        out_refs = refs[first_out:first_out + no]
        i = pl.program_id(1)
        part = lax.dot_general(a_ref[...], b_ref[...], (cdims, ((), ())), preferred_element_type=F32)
        if nk == 1:
            epi(part, ex, out_refs, i)
        else:
            acc_ref = refs[first_out + no]
            k = pl.program_id(2)

            @pl.when(k == 0)
            def _():
                acc_ref[...] = part

            @pl.when(k > 0)
            def _():
                acc_ref[...] += part

            @pl.when(k == nk - 1)
            def _():
                epi(acc_ref[...], ex, out_refs, i)

    in_specs = [pl.BlockSpec(*a_spec), pl.BlockSpec(*b_spec)] + [pl.BlockSpec(bs, im) for _, bs, im in extras]
    in_specs += [ANY] * len(afters)
    out_specs = [pl.BlockSpec(bs, im) for _, _, bs, im in outs]
    out_shape = [SDS(s, dt) for s, dt, _, _ in outs]
    vm = 2 * (_nbytes(a_spec[0], a.dtype) + _nbytes(b_spec[0], b.dtype))
    vm += 2 * sum(_nbytes(bs, arr.dtype) for arr, bs, _ in extras)
    vm += 2 * sum(_nbytes(bs, dt) for _, dt, bs, _ in outs)
    vm += 6 * _nbytes(acc_shape, F32)
    scratch = [pltpu.VMEM(acc_shape, F32)] if nk > 1 else []
    return pl.pallas_call(
        body, name=name, grid=grid, in_specs=in_specs, out_specs=out_specs, out_shape=out_shape,
        scratch_shapes=scratch,
        compiler_params=_cparams(vm, ("arbitrary", "arbitrary", "arbitrary")),
    )(a, b, *[arr for arr, _, _ in extras], *afters)


NN =((1,), (0,))
NT = ((1,), (1,))
TN = ((0,), (0,))


def _row_acc(out_ref, val, i):
    @pl.when(i == 0)
    def _():
        out_ref[...] = val

    @pl.when(i > 0)
    def _():
        out_ref[...] += val


def _rms_bwd(xn, r, dxn):
    return r * (dxn - xn * jnp.mean(dxn * xn, axis=-1, keepdims=True))


def norm1(x, g):
    tm = 256

    def body(x_ref, g_ref, h_ref):
        xv = x_ref[...]
        r = lax.rsqrt(jnp.mean(xv * xv, axis=-1, keepdims=True) + EPS)
        h_ref[...] = (xv * r * g_ref[...]).astype(BF16)

    return pl.pallas_call(
        body, name="norm1", grid=(T // tm,),
        in_specs=[pl.BlockSpec((tm, D), lambda i: (i, 0)), pl.BlockSpec((1, D), lambda i: (0, 0))],
        out_specs=pl.BlockSpec((tm, D), lambda i: (i, 0)), out_shape=SDS((T, D), BF16),
        compiler_params=_cparams(32 * 1024 * 1024, ("arbitrary",)),
    )(x, g)


def mm_in(h1, wcat):
    tm, tn = 512, 1280

    def epi(acc, ex, outs, i):
        outs[0][...] = acc.astype(BF16)

    return matmul("mm_in", h1, wcat, a_spec=((tm, D), lambda j, i, k: (i, 0)), b_spec=((D, tn), lambda j, i, k: (0, j)),
                  cdims=NN, grid=(NCAT // tn, T // tm, 1), acc_shape=(tm, tn),
                  outs=[((T, NCAT), BF16, (tm, tn), lambda j, i, k: (i, j))], epi=epi)[0]


def _window_sum(x, w, up):
    n = x.shape[0]
    row = lax.broadcasted_iota(jnp.int32, x.shape, 0)
    s, sh = x, 1
    while sh < w:
        if up:
            s = s + jnp.where(row < n - sh, pltpu.roll(s, n - sh, axis=0), 0.0)
        else:
            s = s + jnp.where(row >= sh, pltpu.roll(s, sh, axis=0), 0.0)
        sh *= 2
    return s


def _inv_count(shape, w):
    row = lax.broadcasted_iota(jnp.int32, shape, 0)
    return 1.0 / jnp.minimum(row + 1, w).astype(F32)


def pool_fwd(pcat, pw):
    def body(u_ref, pw_ref, d_ref, y_ref):
        for gi, w in enumerate(POOL_WINDOWS):
            ug = u_ref[:, gi * PG:(gi + 1) * PG].astype(F32)
            dg = _window_sum(ug, w, False) * _inv_count(ug.shape, w) - ug
            db = dg.astype(BF16)
            d_ref[:, gi * PG:(gi + 1) * PG] = db
            y_ref[:, gi * PO:(gi + 1) * PO] = jnp.dot(db, pw_ref[gi], preferred_element_type=F32).astype(BF16)

    return pl.pallas_call(
        body, name="pool_fwd", grid=(1,),
        in_specs=[pl.BlockSpec((T, 4 * PG), lambda i: (0, OU // (4 * PG))), pl.BlockSpec((4, PG, PO), lambda i: (0, 0, 0))],
        out_specs=[pl.BlockSpec((T, 4 * PG), lambda i: (0, 0)), pl.BlockSpec((T, D), lambda i: (0, 0))],
        out_shape=[SDS((T, 4 * PG), BF16), SDS((T, D), BF16)],
        compiler_params=_cparams(48 * 1024 * 1024, ("arbitrary",)),
    )(pcat, pw)


def pool_bwd(dylin, d, pw):
    def body(dy_ref, d_ref, pw_ref, du_ref, dpw_ref):
        for gi, w in enumerate(POOL_WINDOWS):
            dyl = dy_ref[:, gi * PO:(gi + 1) * PO]
            dd = lax.dot_general(dyl, pw_ref[gi], (NT, ((), ())), preferred_element_type=F32)
            du = _window_sum(dd * _inv_count(dd.shape, w), w, True) - dd
            du_ref[:, gi * PG:(gi + 1) * PG] = du.astype(BF16)
            dpw_ref[gi] = lax.dot_general(d_ref[:, gi * PG:(gi + 1) * PG], dyl, (TN, ((), ())),
                                          preferred_element_type=F32).astype(BF16)

    return pl.pallas_call(
        body, name="pool_bwd", grid=(1,),
        in_specs=[pl.BlockSpec((T, D), lambda i: (0, 0)), pl.BlockSpec((T, 4 * PG), lambda i: (0, 0)),
                  pl.BlockSpec((4, PG, PO), lambda i: (0, 0, 0))],
        out_specs=[pl.BlockSpec((T, 4 * PG), lambda i: (0, 0)), pl.BlockSpec((4, PG, PO), lambda i: (0, 0, 0))],
        out_shape=[SDS((T, 4 * PG), BF16), SDS((4, PG, PO), BF16)],
        compiler_params=_cparams(48 * 1024 * 1024, ("arbitrary",)),
    )(dylin, d, pw)


def _gate_decay(alow, wa, ba):
    a = jnp.dot(alow, wa, preferred_element_type=F32) + ba
    ls = jax.nn.log_sigmoid(a) * (1.0 / 16.0)
    r = lax.broadcasted_iota(jnp.int32, (CHUNK, CHUNK), 0)
    c = lax.broadcasted_iota(jnp.int32, (CHUNK, CHUNK), 1)
    tri = jnp.where(c <= r, 1.0, 0.0).astype(F32)
    cum = jnp.dot(tri, ls, preferred_element_type=F32, precision=lax.Precision.HIGHEST)
    last = cum[CHUNK - 1:CHUNK, :]
    return a, jnp.exp(last - cum), jnp.exp(last)


def gla_fwd(pcat, wa, ba, ng):
    def body(q_ref, k_ref, v_ref, g_ref, al_ref, wa_ref, ba_ref, ng_ref, og_ref, o_ref, st_ref, s_scr):
        @pl.when(pl.program_id(0) == 0)
        def _():
            s_scr[...] = jnp.zeros_like(s_scr)

        _, e, decay = _gate_decay(al_ref[...], wa_ref[...], ba_ref[...])
        kd = (k_ref[...].astype(F32) * e).astype(BF16)
        qs = (q_ref[...].astype(F32) * (DK ** -0.5)).astype(BF16)
        for h in range(HEADS):
            ck = slice(h * DK, (h + 1) * DK)
            cv = slice(h * DV, (h + 1) * DV)
            s_new = s_scr[h] * decay[:, ck] + lax.dot_general(v_ref[:, cv], kd[:, ck], (TN, ((), ())),
                                                               preferred_element_type=F32)
            s_scr[h] = s_new
            sb = s_new.astype(BF16)
            st_ref[h] = sb
            oh = lax.dot_general(qs[:, ck], sb, (NT, ((), ())), preferred_element_type=F32)
            o_ref[:, cv] = oh.astype(BF16)
            on = oh * lax.rsqrt(jnp.mean(oh * oh, axis=-1, keepdims=True) + EPS) * ng_ref[:, cv]
            gv = g_ref[:, cv].astype(F32)
            og_ref[:, cv] = (on * (gv * _sigmoid(gv))).astype(BF16)

    row = lambda c: (c, 0)
    return pl.pallas_call(
        body, name="gla_fwd", grid=(NCHUNK,),
        in_specs=[pl.BlockSpec((CHUNK, QK), lambda c: (c, OQ // QK)), pl.BlockSpec((CHUNK, QK), lambda c: (c, OKK // QK)),
                  pl.BlockSpec((CHUNK, D), lambda c: (c, OV // D)), pl.BlockSpec((CHUNK, D), lambda c: (c, OG // D)),
                  pl.BlockSpec((CHUNK, APAD), lambda c: (c, OA // APAD)),
                  pl.BlockSpec((APAD, QK), lambda c: (0, 0)), pl.BlockSpec((1, QK), lambda c: (0, 0)),
                  pl.BlockSpec((1, D), lambda c: (0, 0))],
        out_specs=[pl.BlockSpec((CHUNK, D), row), pl.BlockSpec((CHUNK, D), row),
                   pl.BlockSpec((None, HEADS, DV, DK), lambda c: (c, 0, 0, 0))],
        out_shape=[SDS((T, D), BF16), SDS((T, D), BF16), SDS((NCHUNK, HEADS, DV, DK), BF16)],
        scratch_shapes=[pltpu.VMEM((HEADS, DV, DK), F32)],
        compiler_params=_cparams(32 * 1024 * 1024, ("arbitrary",)),
    )(pcat, pcat, pcat, pcat, pcat, wa, ba, ng)


def gla_bwd(do, pcat, states, wa, ba, after):
    def body(do_ref, q_ref, k_ref, v_ref, al_ref, sc_ref, sp_ref, wa_ref, ba_ref, after_ref,
             dq_ref, dk_ref, dv_ref, dal_ref, dwa_ref, dba_ref, ds_scr):
        i = pl.program_id(0)

        @pl.when(i == 0)
        def _():
            ds_scr[...] = jnp.zeros_like(ds_scr)

        has_prev = jnp.where(i < NCHUNK - 1, 1.0, 0.0).astype(F32)
        a, e, decay = _gate_decay(al_ref[...], wa_ref[...], ba_ref[...])
        kf = k_ref[...].astype(F32)
        kdf = kf * e
        kd = kdf.astype(BF16)
        qs = (q_ref[...].astype(F32) * (DK ** -0.5)).astype(BF16)
        dkd_parts, ddecay_parts = [], []
        for h in range(HEADS):
            ck = slice(h * DK, (h + 1) * DK)
            cv = slice(h * DV, (h + 1) * DV)
            doh = do_ref[:, cv]
            ds = ds_scr[h] + lax.dot_general(doh, qs[:, ck], (TN, ((), ())), preferred_element_type=F32)
            dsb = ds.astype(BF16)
            dq_ref[:, ck] = (jnp.dot(doh, sc_ref[h], preferred_element_type=F32) * (DK ** -0.5)).astype(BF16)
            dkd_parts.append(jnp.dot(v_ref[:, cv], dsb, preferred_element_type=F32))
            dv_ref[:, cv] = lax.dot_general(kd[:, ck], dsb, (NT, ((), ())), preferred_element_type=F32).astype(BF16)
            ddecay_parts.append(jnp.sum(ds * sp_ref[h].astype(F32), axis=0, keepdims=True) * has_prev)
            ds_scr[h] = ds * decay[:, ck]
        dkd = jnp.concatenate(dkd_parts, axis=1)
        ddecay = jnp.concatenate(ddecay_parts, axis=1)
        dk_ref[...] = (dkd * e).astype(BF16)
        dearg = dkd * kdf
        dlast = jnp.sum(dearg, axis=0, keepdims=True) + ddecay * decay
        r = lax.broadcasted_iota(jnp.int32, (CHUNK, CHUNK), 0)
        c = lax.broadcasted_iota(jnp.int32, (CHUNK, CHUNK), 1)
        triu = jnp.where(c >= r, 1.0, 0.0).astype(F32)
        dls = dlast - jnp.dot(triu, dearg, preferred_element_type=F32, precision=lax.Precision.HIGHEST)
        da = dls * (1.0 / 16.0) * (1.0 - _sigmoid(a))
        dab = da.astype(BF16)
        dal_ref[...] = lax.dot_general(dab, wa_ref[...], (NT, ((), ())), preferred_element_type=F32).astype(BF16)
        dwa = lax.dot_general(al_ref[...], dab, (TN, ((), ())), preferred_element_type=F32)
        dba = jnp.sum(da, axis=0, keepdims=True)

        @pl.when(i == 0)
        def _():
            dwa_ref[...] = dwa
            dba_ref[...] = dba

        @pl.when(i > 0)
        def _():
            dwa_ref[...] += dwa
            dba_ref[...] += dba

    rev = lambda i: NCHUNK - 1 - i
    return pl.pallas_call(
        body, name="gla_bwd", grid=(NCHUNK,),
        in_specs=[pl.BlockSpec((CHUNK, D), lambda i: (rev(i), 0)),
                  pl.BlockSpec((CHUNK, QK), lambda i: (rev(i), OQ // QK)), pl.BlockSpec((CHUNK, QK), lambda i: (rev(i), OKK // QK)),
                  pl.BlockSpec((CHUNK, D), lambda i: (rev(i), OV // D)), pl.BlockSpec((CHUNK, APAD), lambda i: (rev(i), OA // APAD)),
                  pl.BlockSpec((None, HEADS, DV, DK), lambda i: (rev(i), 0, 0, 0)),
                  pl.BlockSpec((None, HEADS, DV, DK), lambda i: (jnp.maximum(rev(i) - 1, 0), 0, 0, 0)),
                  pl.BlockSpec((APAD, QK), lambda i: (0, 0)), pl.BlockSpec((1, QK), lambda i: (0, 0)), ANY],
        out_specs=[pl.BlockSpec((CHUNK, QK), lambda i: (rev(i), 0)), pl.BlockSpec((CHUNK, QK), lambda i: (rev(i), 0)),
                   pl.BlockSpec((CHUNK, D), lambda i: (rev(i), 0)), pl.BlockSpec((CHUNK, APAD), lambda i: (rev(i), 0)),
                   pl.BlockSpec((APAD, QK), lambda i: (0, 0)), pl.BlockSpec((1, QK), lambda i: (0, 0))],
        out_shape=[SDS((T, QK), BF16), SDS((T, QK), BF16), SDS((T, D), BF16), SDS((T, APAD), BF16),
                   SDS((APAD, QK), F32), SDS((1, QK), F32)],
        scratch_shapes=[pltpu.VMEM((HEADS, DV, DK), F32)],
        compiler_params=_cparams(32 * 1024 * 1024, ("arbitrary",)),
    )(do, pcat, pcat, pcat, pcat, states, states, wa, ba, after)


TMF = 256
_rowblk = ((TMF, D), lambda j, i, k: (i, 0))
_vec = ((1, D), lambda j, i, k: (0, 0))


def _full_spec(col):
    return ((TMF, D), lambda j, i, k: (i, col))


TBIG = 1024


def square_matmul(name, a, b, *, a_spec, b_spec, cdims, nk, after=None):
    def epi(acc, ex, outs, i):
        outs[0][...] = acc

    return matmul(name, a, b, a_spec=a_spec, b_spec=b_spec, cdims=cdims, grid=(D // TBIG, T // TBIG, nk),
                  acc_shape=(TBIG, TBIG), outs=[((T, D), F32, (TBIG, TBIG), lambda j, i, k: (i, j))], epi=epi,
                  after=after)[0]


def rowwise(name, y, *, extras, outs, epi):
    ne = len(extras)

    def body(*refs):
        epi(refs[0][...], refs[1:1 + ne], refs[1 + ne:], pl.program_id(1))

    in_specs = [pl.BlockSpec(*_rowblk)] + [pl.BlockSpec(bs, im) for _, bs, im in extras]
    return pl.pallas_call(
        body, name=name, grid=(1, T // TMF, 1), in_specs=in_specs,
        out_specs=[pl.BlockSpec(bs, im) for _, _, bs, im in outs], out_shape=[SDS(s, dt) for s, dt, _, _ in outs],
        compiler_params=_cparams(40 * 1024 * 1024, ("arbitrary", "arbitrary", "arbitrary")),
    )(y, *[arr for arr, _, _ in extras])


def mm_gla_out(og, w, ylin, pcat, pscale):
    def epi(acc, ex, outs, i):
        ylin_ref, lgp_ref, lgg_ref, ps_ref = ex
        gp = _sigmoid(lgp_ref[...].astype(F32))
        gg = _sigmoid(lgg_ref[...].astype(F32))
        outs[0][...] = (gp * (ylin_ref[...].astype(F32) * ps_ref[...]) + gg * acc).astype(BF16)
        outs[1][...] = acc.astype(BF16)

    return matmul("mm_gla_out", og, w, a_spec=_rowblk, b_spec=((D, D), lambda j, i, k: (0, 0)), cdims=NN,
                  grid=(1, T // TMF, 1), acc_shape=(TMF, D),
                  extras=[(ylin, *_rowblk), (pcat, *_full_spec(OGP // D)), (pcat, *_full_spec(OGG // D)), (pscale, *_vec)],
                  outs=[((T, D), BF16, *_rowblk), ((T, D), BF16, *_rowblk)], epi=epi)


def mm_out(mixed, w, x, g2):
    def epi(acc, ex, outs, i):
        x_ref, g_ref = ex
        x2 = x_ref[...] + acc
        r = lax.rsqrt(jnp.mean(x2 * x2, axis=-1, keepdims=True) + EPS)
        outs[0][...] = x2
        outs[1][...] = (x2 * r * g_ref[...]).astype(BF16)

    return matmul("mm_out", mixed, w, a_spec=_rowblk, b_spec=((D, D), lambda j, i, k: (0, 0)), cdims=NN,
                  grid=(1, T // TMF, 1), acc_shape=(TMF, D), extras=[(x, *_rowblk), (g2, *_vec)],
                  outs=[((T, D), F32, *_rowblk), ((T, D), BF16, *_rowblk)], epi=epi)


def mm_up(h2, wup):
    def epi(acc, ex, outs, i):
        r = jnp.maximum(acc, 0.0)
        outs[0][...] = r.astype(BF16)
        outs[1][...] = (r * r).astype(BF16)

    blk = ((TMF, D), lambda j, i, k: (i, j))
    return matmul("mm_up", h2, wup, a_spec=_rowblk, b_spec=((None, D, D), lambda j, i, k: (j, 0, 0)), cdims=NN,
                  grid=(NCHIP, T // TMF, 1), acc_shape=(TMF, D),
                  outs=[((T, DFF), BF16, *blk), ((T, DFF), BF16, *blk)], epi=epi)


def mm_down(act, wdown, x2, tgt, gf):
    tk = 2048

    def epi(acc, ex, outs, i):
        x2_ref, t_ref, g_ref = ex
        dx_ref, dxb_ref, gnf_ref, loss_ref = outs
        x3 = x2_ref[...] + acc
        r = lax.rsqrt(jnp.mean(x3 * x3, axis=-1, keepdims=True) + EPS)
        xn = x3 * r
        err = xn * g_ref[...] - t_ref[...]
        lsum = 0.5 * jnp.sum(jnp.mean(err * err, axis=-1, keepdims=True), axis=0, keepdims=True)
        dy = err * (1.0 / D)
        _row_acc(gnf_ref, jnp.sum(dy * xn, axis=0, keepdims=True), i)
        _row_acc(loss_ref, jnp.broadcast_to(lsum, (1, 128)), i)
        dx3 = _rms_bwd(xn, r, dy * g_ref[...])
        dx_ref[...] = dx3
        dxb_ref[...] = dx3.astype(BF16)

    y = square_matmul("mm_down", act, wdown, a_spec=((TBIG, tk), lambda j, i, k: (i, k)),
                      b_spec=((tk, TBIG), lambda j, i, k: (k, j)), cdims=NN, nk=DFF // tk)
    return rowwise("rows_final", y, extras=[(x2, *_rowblk), (tgt, *_rowblk), (gf, *_vec)],
                   outs=[((T, D), F32, *_rowblk), ((T, D), BF16, *_rowblk), ((1, D), F32, *_vec),
                         ((1, 128), F32, (1, 128), lambda j, i, k: (0, 0))], epi=epi)


def mm_dact(dx3b, wdown, rup, after=None):
    def epi(acc, ex, outs, i):
        outs[0][...] = (acc * 2.0 * ex[0][...].astype(F32)).astype(BF16)

    blk = ((TMF, D), lambda j, i, k: (i, j))
    return matmul("mm_dact", dx3b, wdown, a_spec=_rowblk, b_spec=((D, D), lambda j, i, k: (j, 0)), cdims=NT,
                  grid=(DFF // D, T // TMF, 1), acc_shape=(TMF, D), extras=[(rup, *blk)],
                  outs=[((T, DFF), BF16, *blk)], epi=epi, after=after)[0]


def mm_wgrad(name, a, b, m, n, out_shape, out_block, out_map, tm, tn):
    def epi(acc, ex, outs, i):
        outs[0][...] = acc.astype(BF16)

    return matmul(name, a, b, a_spec=((T, tm), lambda j, i, k: (0, i)), b_spec=((T, tn), lambda j, i, k: (0, j)),
                  cdims=TN, grid=(n // tn, m // tm, 1), acc_shape=(tm, tn),
                  outs=[(out_shape, BF16, out_block, out_map)], epi=epi)[0]


def mm_dh2(dup, wup, x2, dx3, g2):
    def epi(acc, ex, outs, i):
        x2_ref, dx3_ref, g_ref = ex
        x2 = x2_ref[...]
        r = lax.rsqrt(jnp.mean(x2 * x2, axis=-1, keepdims=True) + EPS)
        xn = x2 * r
        _row_acc(outs[2], jnp.sum(acc * xn, axis=0, keepdims=True), i)
        dx2 = dx3_ref[...] + _rms_bwd(xn, r, acc * g_ref[...])
        outs[0][...] = dx2
        outs[1][...] = dx2.astype(BF16)

    y = square_matmul("mm_dh2", dup, wup, a_spec=((TBIG, D), lambda j, i, k: (i, k)),
                      b_spec=((None, TBIG, D), lambda j, i, k: (k, j, 0)), cdims=NT, nk=NCHIP)
    return rowwise("rows_dh2", y, extras=[(x2, *_rowblk), (dx3, *_rowblk), (g2, *_vec)],
                   outs=[((T, D), F32, *_rowblk), ((T, D), BF16, *_rowblk), ((1, D), F32, *_vec)], epi=epi)


def mm_dmixed(dx2b, wout, pcat, ylin, ygla, pscale, after=None):
    def epi(acc, ex, outs, i):
        lgp_ref, lgg_ref, ylin_ref, ygla_ref, ps_ref = ex
        gp = _sigmoid(lgp_ref[...].astype(F32))
        gg = _sigmoid(lgg_ref[...].astype(F32))
        yl = ylin_ref[...].astype(F32)
        ps = ps_ref[...]
        agp = acc * gp
        outs[0][...] = (agp * ps).astype(BF16)
        outs[1][...] = (acc * gg).astype(BF16)
        outs[2][...] = (agp * (yl * ps) * (1.0 - gp)).astype(BF16)
        outs[3][...] = (acc * ygla_ref[...].astype(F32) * gg * (1.0 - gg)).astype(BF16)
        _row_acc(outs[4], jnp.sum(agp * yl, axis=0, keepdims=True), i)

    return matmul("mm_dmixed", dx2b, wout, a_spec=_rowblk, b_spec=((D, D), lambda j, i, k: (0, 0)), cdims=NT,
                  grid=(1, T // TMF, 1), acc_shape=(TMF, D),
                  extras=[(pcat, *_full_spec(OGP // D)), (pcat, *_full_spec(OGG // D)), (ylin, *_rowblk), (ygla, *_rowblk),
                          (pscale, *_vec)],
                  outs=[((T, D), BF16, *_rowblk)] * 4 + [((1, D), F32, *_vec)], epi=epi, after=after)


def mm_dog(dygla, wgo, o, pcat, ng):
    def epi(acc, ex, outs, i):
        o_ref, g_ref, ng_ref = ex
        do_ref, dg_ref, gng_ref = outs
        gparts = []
        for h in range(HEADS):
            cv = slice(h * DV, (h + 1) * DV)
            oh = o_ref[:, cv].astype(F32)
            r = lax.rsqrt(jnp.mean(oh * oh, axis=-1, keepdims=True) + EPS)
            on = oh * r
            gv = g_ref[:, cv].astype(F32)
            sg = _sigmoid(gv)
            dgain = acc[:, cv] * (gv * sg)
            gparts.append(jnp.sum(dgain * on, axis=0, keepdims=True))
            ngh = ng_ref[:, cv]
            do_ref[:, cv] = _rms_bwd(on, r, dgain * ngh).astype(BF16)
            dg_ref[:, cv] = (acc[:, cv] * (on * ngh) * (sg * (1.0 + gv * (1.0 - sg)))).astype(BF16)
        _row_acc(gng_ref, jnp.concatenate(gparts, axis=1), i)

    return matmul("mm_dog", dygla, wgo, a_spec=_rowblk, b_spec=((D, D), lambda j, i, k: (0, 0)), cdims=NT,
                  grid=(1, T // TMF, 1), acc_shape=(TMF, D),
                  extras=[(o, *_rowblk), (pcat, *_full_spec(OG // D)), (ng, *_vec)],
                  outs=[((T, D), BF16, *_rowblk), ((T, D), BF16, *_rowblk), ((1, D), F32, *_vec)], epi=epi)


def mm_dh1(dpcat, wcat, x, dx2, g1, after=None):
    tk = 1280

    def epi(acc, ex, outs, i):
        x_ref, dx2_ref, g_ref = ex
        xv = x_ref[...]
        r = lax.rsqrt(jnp.mean(xv * xv, axis=-1, keepdims=True) + EPS)
        xn = xv * r
        _row_acc(outs[1], jnp.sum(acc * xn, axis=0, keepdims=True), i)
        outs[0][...] = dx2_ref[...] + _rms_bwd(xn, r, acc * g_ref[...])

    y = square_matmul("mm_dh1", dpcat, wcat, a_spec=((TBIG, tk), lambda j, i, k: (i, k)),
                      b_spec=((TBIG, tk), lambda j, i, k: (j, k)), cdims=NT, nk=NCAT // tk, after=after)
    return rowwise("rows_dh1", y, extras=[(x, *_rowblk), (dx2, *_rowblk), (g1, *_vec)],
                   outs=[((T, D), F32, *_rowblk), ((1, D), F32, *_vec)], epi=epi)


def _tile_rows(rows, cols, n_arrays):
    tm = rows
    while tm % 32 == 0 and 2 * n_arrays * tm * cols * 4 > 24 * 1024 * 1024:
        tm //= 2
    return tm


def add_pairs(name, parts, theirs, core):
    _, _, r, c = parts.shape
    tm = _tile_rows(r, c, 3)

    def body(core_ref, a_ref, b_ref, o_ref):
        o_ref[...] = (a_ref[...].astype(F32) + b_ref[...].astype(F32)).astype(BF16)

    spec = pl.BlockSpec((None, tm, c), lambda j, i, core_ref: (j, i, 0))
    grid_spec = pltpu.PrefetchScalarGridSpec(
        num_scalar_prefetch=1, grid=(NCHIP, r // tm),
        in_specs=[pl.BlockSpec((None, None, tm, c), lambda j, i, core_ref: (core_ref[0], j, i, 0)), spec], out_specs=spec)
    return pl.pallas_call(body, name=name, grid_spec=grid_spec, out_shape=SDS((NCHIP, r, c), BF16),
                          compiler_params=_cparams(40 * 1024 * 1024, ("arbitrary", "arbitrary")))(core, parts, theirs)


def sum_chips(name, sums, landed, chip):
    _, r, c = sums.shape
    tm = _tile_rows(r, c, 4)

    def body(chip_ref, own_ref, l_ref, o_ref):
        s = own_ref[...].astype(F32)
        for t in range(NCHIP - 1):
            s = s + l_ref[t].astype(F32)
        o_ref[...] = s

    grid_spec = pltpu.PrefetchScalarGridSpec(
        num_scalar_prefetch=1, grid=(r // tm,),
        in_specs=[pl.BlockSpec((None, tm, c), lambda i, chip_ref: (chip_ref[0], i, 0)),
                  pl.BlockSpec((NCHIP - 1, tm, c), lambda i, chip_ref: (0, i, 0))],
        out_specs=pl.BlockSpec((tm, c), lambda i, chip_ref: (i, 0)))
    return pl.pallas_call(body, name=name, grid_spec=grid_spec, out_shape=SDS((r, c), F32),
                          compiler_params=_cparams(40 * 1024 * 1024, ("arbitrary",)))(chip, sums, landed)


def _adamw_math(wv, gv, mv, vv):
    mn = ADAM_B1 * mv + (1.0 - ADAM_B1) * gv
    vn = ADAM_B2 * vv + (1.0 - ADAM_B2) * (gv * gv)
    mh = mn / (1.0 - ADAM_B1 ** ADAM_STEP)
    vh = vn / (1.0 - ADAM_B2 ** ADAM_STEP)
    return -ADAM_LR * (mh / (jnp.sqrt(vh) + ADAM_EPS) + ADAM_WD * wv), mn, vn


def adamw(name, w, g, m, v):
    def body(w_ref, g_ref, m_ref, v_ref, go_ref, d_ref, mo_ref, vo_ref):
        gv = g_ref[...]
        go_ref[...] = gv
        d_ref[...], mo_ref[...], vo_ref[...] = _adamw_math(w_ref[...], gv, m_ref[...], v_ref[...])

    return pl.pallas_call(body, name=name, out_shape=[SDS(w.shape, F32)] * 4)(w, g, m, v)


def adamw_halves(name, w, g_own, g_sib, m, v, core):
    _, r, c = w.shape
    tm = _tile_rows(r, c, 10)

    def body(core_ref, w_ref, go_ref, gs_ref, m_ref, v_ref, g_out, d_out, m_out, v_out):
        gv = jnp.where(pl.program_id(0) == core_ref[0], go_ref[...], gs_ref[...])
        g_out[...] = gv
        d_out[...], m_out[...], v_out[...] = _adamw_math(w_ref[...], gv, m_ref[...], v_ref[...])

    full = pl.BlockSpec((None, tm, c), lambda h, i, core_ref: (h, i, 0))
    own = pl.BlockSpec((tm, c), lambda h, i, core_ref: (jnp.where(h == core_ref[0], i, 0), 0))
    sib = pl.BlockSpec((tm, c), lambda h, i, core_ref: (jnp.where(h == core_ref[0], 0, i), 0))
    grid_spec = pltpu.PrefetchScalarGridSpec(num_scalar_prefetch=1, grid=(2, r // tm),
                                             in_specs=[full, own, sib, full, full], out_specs=[full] * 4)
    return pl.pallas_call(body, name=name, grid_spec=grid_spec, out_shape=[SDS(w.shape, F32)] * 4,
                          compiler_params=_cparams(48 * 1024 * 1024, ("arbitrary", "arbitrary")))(core, w, g_own, g_sib, m, v)


def pack_rows(name, parts, rows):
    width = parts[0].shape[1]
    n = len(parts)

    def body(*refs):
        out_ref = refs[n]
        out_ref[...] = jnp.zeros_like(out_ref)
        off = 0
        for p in refs[:n]:
            out_ref[off:off + p.shape[0], :] = p[...]
            off += p.shape[0]

    return pl.pallas_call(body, name=name, out_shape=SDS((rows, width), F32))(*parts)


def _place():
    x, y, c = lax.axis_index("x"), lax.axis_index("y"), lax.axis_index("c")
    chips = [(1 - x, y), (x, 1 - y), (1 - x, 1 - y)]
    return x, y, c, chips


def _row_split(shape, dtype):
    r, c = shape
    n = 1
    while r % (2 * n) == 0 and (r // (2 * n)) % 16 == 0 and (r // n) * c * jnp.dtype(dtype).itemsize > PIECE_BYTES:
        n *= 2
    return [pl.ds(s * (r // n), r // n) for s in range(n)]


def _pieces(ref):
    *lead, r, c = ref.shape
    split = _row_split((r, c), ref.dtype)
    return [ref.at[(*idx, s)] for idx in itertools.product(*[range(d) for d in lead]) for s in split]


HBM = pl.BlockSpec(memory_space=pltpu.HBM)
SEM = pl.BlockSpec(memory_space=pltpu.SEMAPHORE)
EFFECT = pltpu.SideEffectType.DATAFLOW_SIDE_EFFECTING


def gather_start(name, shards, after=None):
    n = len(shards)
    extra = [] if after is None else [after]

    def body(*refs):
        src, land = refs[:n], refs[n:2 * n]
        send, recv = refs[2 * n + len(extra)], refs[2 * n + len(extra) + 1]
        x, y, c, chips = _place()
        me = 2 * x + y
        for a in range(n):
            for j, (cx, cy) in enumerate(chips):
                for sp, dp in zip(_pieces(src[a].at[c]), _pieces(land[a].at[me, c])):
                    pltpu.make_async_remote_copy(sp, dp, send.at[3 * a + j], recv.at[3 * a + j],
                                                 device_id=(cx, cy, c), device_id_type=MESH).start()

    lands = [pltpu.with_memory_space_constraint(lax.empty((NCHIP,) + s.shape, s.dtype), pltpu.HBM) for s in shards]
    srcs = [pltpu.with_memory_space_constraint(s, pltpu.HBM) for s in shards]
    outs = pl.pallas_call(
        body, name=name,
        out_shape=(pltpu.SemaphoreType.DMA((3 * n,)), pltpu.SemaphoreType.DMA((3 * n,)),
                   *[pltpu.HBM(s.shape, s.dtype) for s in shards], *[pltpu.HBM(l.shape, l.dtype) for l in lands]),
        in_specs=[HBM] * (2 * n) + [ANY] * len(extra), out_specs=(SEM, SEM, *([HBM] * (2 * n))),
        input_output_aliases={i: 2 + i for i in range(2 * n)},
        compiler_params=pltpu.CompilerParams(has_side_effects=EFFECT),
    )(*srcs, *lands, *extra)
    return outs[0], outs[1], list(outs[2:2 + n]), list(outs[2 + n:2 + 2 * n])


def gather_wait(name, send, recv, shards, lands, after):
    n = len(shards)
    afters = _as_list(after)

    def body(*refs):
        src, land = refs[:n], refs[n:2 * n]
        send_ref, recv_ref = refs[2 * n], refs[2 * n + 1]
        x, y, c, chips = _place()
        for a in range(n):
            for j, (cx, cy) in enumerate(chips):
                cp = pltpu.make_async_remote_copy(src[a].at[c], land[a].at[2 * cx + cy, c], send_ref.at[3 * a + j],
                                                  recv_ref.at[3 * a + j], device_id=(cx, cy, c), device_id_type=MESH)
                cp.wait_send()
                cp.wait_recv()

    outs = pl.pallas_call(
        body, name=name,
        out_shape=(*[pltpu.HBM(s.shape, s.dtype) for s in shards], *[pltpu.HBM(l.shape, l.dtype) for l in lands]),
        in_specs=[HBM] * (2 * n) + [SEM, SEM] + [ANY] * len(afters), out_specs=[HBM] * (2 * n),
        input_output_aliases={i: i for i in range(2 * n)},
        compiler_params=pltpu.CompilerParams(has_side_effects=EFFECT),
    )(*shards, *lands, send, recv, *afters)
    return list(outs[:n]), list(outs[n:])


def forward_halves(name, shards, lands):
    n = len(lands)

    def body(*refs):
        had, buf = refs[:n], refs[n:2 * n]
        send, recv = refs[2 * n:]
        x, y, c, chips = _place()
        sib = (x, y, 1 - c)
        for a in range(n):
            for j, (cx, cy) in enumerate(chips):
                for sp, dp in zip(_pieces(had[a].at[2 * cx + cy, c]), _pieces(buf[a].at[2 * cx + cy, c])):
                    pltpu.make_async_remote_copy(sp, dp, send.at[3 * a + j], recv.at[3 * a + j], device_id=sib, device_id_type=MESH).start()
        for a in range(n):
            for j, (cx, cy) in enumerate(chips):
                pltpu.make_async_remote_copy(had[a].at[2 * cx + cy, c], buf[a].at[2 * cx + cy, 1 - c], send.at[3 * a + j],
                                             recv.at[3 * a + j], device_id=sib, device_id_type=MESH).wait()

    got = pl.pallas_call(
        body, name=name, in_specs=[ANY] * n, out_specs=[ANY] * n, out_shape=[SDS(l.shape, l.dtype) for l in lands],
        input_output_aliases={i: i for i in range(n)},
        scratch_shapes=[pltpu.SemaphoreType.DMA((3 * n,)), pltpu.SemaphoreType.DMA((3 * n,))],
    )(*lands)
    me = 2 * lax.axis_index("x") + lax.axis_index("y")
    return [lax.dynamic_update_index_in_dim(g, s, me, 0) for g, s in zip(got, shards)]


def exchange_halves(name, parts):
    n = len(parts)

    def body(*refs):
        src, got = refs[:n], refs[n:2 * n]
        send, recv = refs[2 * n:]
        x, y, c, _ = _place()
        sib = (x, y, 1 - c)
        for a in range(n):
            for sp, dp in zip(_pieces(src[a].at[1 - c]), _pieces(got[a])):
                pltpu.make_async_remote_copy(sp, dp, send.at[a], recv.at[a], device_id=sib, device_id_type=MESH).start()
        for a in range(n):
            pltpu.make_async_remote_copy(src[a].at[1 - c], got[a], send.at[a], recv.at[a], device_id=sib, device_id_type=MESH).wait()

    return pl.pallas_call(
        body, name=name, in_specs=[ANY] * n, out_specs=[ANY] * n,
        out_shape=[SDS(p.shape[1:], p.dtype) for p in parts],
        scratch_shapes=[pltpu.SemaphoreType.DMA((n,)), pltpu.SemaphoreType.DMA((n,))],
    )(*parts)


def scatter_start(name, parts):
    n = len(parts)

    def body(*refs):
        src, land = refs[:n], refs[n:2 * n]
        send, recv = refs[2 * n], refs[2 * n + 1]
        token = refs[4 * n + 2]
        x, y, c, chips = _place()
        for a in range(n):
            for j, (cx, cy) in enumerate(chips):
                for sp, dp in zip(_pieces(src[a].at[2 * cx + cy]), _pieces(land[a].at[j])):
                    pltpu.make_async_remote_copy(sp, dp, send.at[3 * a + j], recv.at[3 * a + j],
                                                 device_id=(cx, cy, c), device_id_type=MESH).start()
        token[...] = jnp.zeros_like(token)

    lands = [pltpu.with_memory_space_constraint(lax.empty((NCHIP - 1,) + p.shape[1:], p.dtype), pltpu.HBM) for p in parts]
    srcs = [pltpu.with_memory_space_constraint(p, pltpu.HBM) for p in parts]
    outs = pl.pallas_call(
        body, name=name,
        out_shape=(pltpu.SemaphoreType.DMA((3 * n,)), pltpu.SemaphoreType.DMA((3 * n,)),
                   *[pltpu.HBM(p.shape, p.dtype) for p in parts], *[pltpu.HBM(l.shape, l.dtype) for l in lands],
                   SDS((8, 128), F32)),
        in_specs=[HBM] * (2 * n), out_specs=(SEM, SEM, *([HBM] * (2 * n)), pl.BlockSpec(memory_space=pltpu.VMEM)),
        input_output_aliases={i: 2 + i for i in range(2 * n)},
        compiler_params=pltpu.CompilerParams(has_side_effects=EFFECT),
    )(*srcs, *lands)
    return outs[0], outs[1], list(outs[2:2 + n]), list(outs[2 + n:2 + 2 * n]), outs[2 + 2 * n]


def scatter_wait(name, send, recv, parts, lands, after):
    n = len(parts)

    def body(*refs):
        src, land = refs[:n], refs[n:2 * n]
        send_ref, recv_ref = refs[2 * n], refs[2 * n + 1]
        x, y, c, chips = _place()
        for a in range(n):
            for j, (cx, cy) in enumerate(chips):
                cp = pltpu.make_async_remote_copy(src[a].at[2 * cx + cy], land[a].at[j], send_ref.at[3 * a + j], recv_ref.at[3 * a + j],
                                                  device_id=(cx, cy, c), device_id_type=MESH)
                cp.wait_send()
                cp.wait_recv()

    outs = pl.pallas_call(
        body, name=name,
        out_shape=(*[pltpu.HBM(p.shape, p.dtype) for p in parts], *[pltpu.HBM(l.shape, l.dtype) for l in lands]),
        in_specs=[HBM] * (2 * n) + [SEM, SEM, ANY], out_specs=[HBM] * (2 * n),
        input_output_aliases={i: i for i in range(2 * n)},
        compiler_params=pltpu.CompilerParams(has_side_effects=EFFECT),
    )(*parts, *lands, send, recv, after)
    return list(outs[:n]), list(outs[n:])


def join_halves(name, halves):
    n = len(halves)

    def body(*refs):
        src, dst = refs[:n], refs[n:2 * n]
        send, recv = refs[2 * n:]
        x, y, c, _ = _place()
        sib = (x, y, 1 - c)
        for a in range(n):
            for sp, dp in zip(_pieces(src[a]), _pieces(dst[a])):
                pltpu.make_async_remote_copy(sp, dp, send.at[a], recv.at[a], device_id=sib, device_id_type=MESH).start()
        for a in range(n):
            pltpu.make_async_remote_copy(src[a], dst[a], send.at[a], recv.at[a], device_id=sib, device_id_type=MESH).wait()

    return pl.pallas_call(
        body, name=name, in_specs=[ANY] * n, out_specs=[ANY] * n,
        out_shape=[SDS(h.shape, h.dtype) for h in halves],
        scratch_shapes=[pltpu.SemaphoreType.DMA((n,)), pltpu.SemaphoreType.DMA((n,))],
    )(*halves)


def gather_small(name, xs, reduce):
    m, ncol = xs.shape

    def body(x_ref, out_ref, all_ref, send, recv, lsem):
        x, y, c, chips = _place()
        me, sib = (x, y, c), (x, y, 1 - c)

        def rows(px, py, pc):
            return all_ref.at[pl.ds((4 * px + 2 * py + pc) * m, m), :]

        def copy(k, block, to, src=None):
            return pltpu.make_async_remote_copy(rows(*block) if src is None else src, rows(*block), send.at[k], recv.at[k],
                                                device_id=to, device_id_type=MESH)

        mine = pltpu.make_async_copy(x_ref, rows(*me), lsem)
        mine.start()
        first = [copy(0, me, sib, src=x_ref)] + [copy(1 + j, me, (*chip, c), src=x_ref) for j, chip in enumerate(chips)]
        for cp in first:
            cp.start()
        passed = [copy(4 + j, (*chip, c), sib) for j, chip in enumerate(chips)]
        for j, chip in enumerate(chips):
            copy(1 + j, (*chip, c), me).wait_recv()
            passed[j].start()
        copy(0, sib, me).wait_recv()
        for j, chip in enumerate(chips):
            copy(4 + j, (*chip, 1 - c), me).wait_recv()
        for cp in first + passed:
            cp.wait_send()
        mine.wait()
        if reduce:
            s = all_ref[0:m, :]
            for dev in range(1, 8):
                s = s + all_ref[dev * m:(dev + 1) * m, :]
            out_ref[...] = s
        else:
            out_ref[...] = all_ref[...]

    vm = pl.BlockSpec(memory_space=pltpu.VMEM)
    return pl.pallas_call(
        body, name=name, in_specs=[vm], out_specs=vm, out_shape=SDS((m, ncol) if reduce else (8 * m, ncol), F32),
        scratch_shapes=[pltpu.VMEM((8 * m, ncol), F32), pltpu.SemaphoreType.DMA((7,)), pltpu.SemaphoreType.DMA((7,)),
                        pltpu.SemaphoreType.DMA],
    )(xs)


RELAYOUT_ROWS = 128


def weights_to_cat(g_in):
    tm = RELAYOUT_ROWS

    def body(g_ref, o_ref):
        nat = jnp.concatenate([g_ref[j] for j in range(NCHIP)], axis=1)
        pad = jnp.zeros((tm, NCAT - OA - 16), BF16)
        o_ref[...] = jnp.concatenate([nat[:, 3072:7168], nat[:, 7184:11280], nat[:, 0:3072], nat[:, 7168:7184], pad], axis=1)

    return pl.pallas_call(
        body, name="weights_to_cat", grid=(D // tm,), in_specs=[pl.BlockSpec((NCHIP, tm, IN_SHARD), lambda i: (0, i, 0))],
        out_specs=pl.BlockSpec((tm, NCAT), lambda i: (i, 0)), out_shape=SDS((D, NCAT), BF16),
        compiler_params=_cparams(40 * 1024 * 1024, ("arbitrary",)),
    )(g_in)


def grads_from_cat(gw_cat):
    tm = RELAYOUT_ROWS
    nb = (D // 2) // tm

    def body(c_ref, o_ref):
        cat = c_ref[...]
        nat = jnp.concatenate([cat[:, OU:OA], cat[:, OV:OGP], cat[:, OA:OA + 16], cat[:, OGP:OU]], axis=1)
        for j in range(NCHIP):
            o_ref[j] = nat[:, j * IN_SHARD:(j + 1) * IN_SHARD]

    return pl.pallas_call(
        body, name="grads_from_cat", grid=(D // tm,), in_specs=[pl.BlockSpec((tm, NCAT), lambda i: (i, 0))],
        out_specs=pl.BlockSpec((None, NCHIP, tm, IN_SHARD), lambda i: (i // nb, 0, i % nb, 0)),
        out_shape=SDS((2, NCHIP, D // 2, IN_SHARD), BF16), compiler_params=_cparams(40 * 1024 * 1024, ("arbitrary",)),
    )(gw_cat)


def _pad_rows(a, rows):
    return jnp.concatenate([a, jnp.zeros((rows - a.shape[0],) + a.shape[1:], a.dtype)], axis=0)


def local_step(x2d, tgt, gf, g1, pool_scale, wa_pad, b_alpha, ng, g2, get_w, on_grad=None):
    emit = on_grad if on_grad is not None else (lambda group, grads: None)
    h1 = norm1(x2d, g1)
    wcat, pw = get_w("in", h1)
    pcat = mm_in(h1, wcat)
    dpool, ylin = pool_fwd(pcat, pw)
    og, o, states = gla_fwd(pcat, wa_pad, b_alpha, ng)
    w_go, w_o = get_w("mid", og)
    mixed, ygla = mm_gla_out(og, w_go, ylin, pcat, pool_scale)
    x2, h2 = mm_out(mixed, w_o, x2d, g2)
    w_up = get_w("up", h2)
    rup, act = mm_up(h2, w_up)
    w_dn = get_w("down", act)
    dx3, dx3b, g_nf, loss_row = mm_down(act, w_dn, x2, tgt, gf)

    gw_down = mm_wgrad("mm_dw_down", act, dx3b, DFF, D, (2, NCHIP, D // 2, D), (None, None, 512, D),
                       lambda j, i, k: ((i // 2) % 2, i // 4, i % 2, 0), 512, D)
    token = emit("down", {"down": gw_down})
    dup = mm_dact(dx3b, w_dn, rup, after=token)
    dx2, dx2b, g_mlp = mm_dh2(dup, w_up, x2, dx3, g2)
    gw_up = mm_wgrad("mm_dw_up", h2, dup, D, DFF, (2, NCHIP, D // 2, D), (None, None, 512, D),
                     lambda j, i, k: (i // 2, j, i % 2, 0), 512, D)
    token = emit("up", {"up": gw_up})
    dylin, dygla, dlgp, dlgg, g_ps = mm_dmixed(dx2b, w_o, pcat, ylin, ygla, pool_scale, after=token)
    gw_out = mm_wgrad("mm_dw_out", mixed, dx2b, D, D, (2, NCHIP, 256, D), (None, None, 256, D),
                      lambda j, i, k: (i % 2, i // 2, 0, 0), 256, D)
    do, dg, g_ng = mm_dog(dygla, w_go, o, pcat, ng)
    gw_go = mm_wgrad("mm_dw_gla_out", og, dygla, D, D, (2, NCHIP, 256, D), (None, None, 256, D),
                     lambda j, i, k: (i % 2, i // 2, 0, 0), 256, D)
    token = emit("mix", {"out": gw_out, "gla_out": gw_go})
    dq, dk, dv, dalow, g_wa, g_ba = gla_bwd(do, pcat, states, wa_pad, b_alpha, b_alpha if token is None else token)
    du, dpw = pool_bwd(dylin, dpool, pw)
    dpcat = jnp.concatenate([dv, dg, dlgp, dlgg, du, dq, dk, dalow, jnp.zeros((T, NCAT - OA - APAD), BF16)], axis=1)
    gw_cat = mm_wgrad("mm_dw_in", h1, dpcat, D, NCAT, (D, NCAT), (512, 1280), lambda j, i, k: (i, j), 512, 1280)
    token = emit("in", {"in_cat": gw_cat, "pool": dpw})
    grad_x, g_mix = mm_dh1(dpcat, wcat, x2d, dx2, g1, after=token)
    return (loss_row[0, 0], grad_x, g_mix, g_ps, g_mlp, g_nf, g_ng, g_ba, g_wa, token,
            gw_cat, dpw, gw_go, gw_out, gw_up, gw_down)


def kernel(x, norm_mix_g, w_in, pool_w, pool_scale, w_alpha, b_alpha, gla_norm_g, w_gla_out, w_out, norm_mlp_g, w_mlp_up, w_mlp_down, norm_final_g, loss_target, m_norm_mix_g, m_w_in, m_pool_w, m_pool_scale, m_w_alpha, m_b_alpha, m_gla_norm_g, m_w_gla_out, m_w_out, m_norm_mlp_g, m_w_mlp_up, m_w_mlp_down, m_norm_final_g, v_norm_mix_g, v_w_in, v_pool_w, v_pool_scale, v_w_alpha, v_b_alpha, v_gla_norm_g, v_w_gla_out, v_w_out, v_norm_mlp_g, v_w_mlp_up, v_w_mlp_down, v_norm_final_g):
    chip = 2 * lax.axis_index("x") + lax.axis_index("y")
    chip_i = chip.astype(jnp.int32).reshape(1)
    core_i = lax.axis_index("c").astype(jnp.int32).reshape(1)
    x2d = x.reshape(T, D)
    tgt = loss_target.reshape(T, D)
    gf = norm_final_g.reshape(1, D)

    def halves(w2d):
        r, c = w2d.shape
        return w2d.astype(BF16).reshape(2, r // 2, c)

    pool_shard = pool_w.reshape(4 * PG, PO // NCHIP)
    big = [w_in[0], w_gla_out[0], w_out[0], w_mlp_up[0], w_mlp_down[0], pool_shard]
    groups = {"in": [big[0], big[5]], "mid": [big[1], big[2]], "up": [big[3]], "down": [big[4]]}
    sent = {g: [halves(w) for w in ws] for g, ws in groups.items()}
    started = {}

    def start(group, after=None):
        started[group] = gather_start("gather_start_" + group, sent[group], after)

    start("in")
    w_in_r, m_in_r, v_in_r = [a.reshape(2, D // 2, IN_SHARD)
                              for a in lax.optimization_barrier((w_in, m_w_in, v_w_in, started["in"][2][0]))[:3]]

    def get_w(group, after):
        send, recv, shards, lands = started[group]
        if group == "in":
            after = [after, w_in_r, m_in_r, v_in_r, *sent["mid"], *sent["up"], *sent["down"], wa_pad]
        shards, lands = gather_wait("gather_wait_" + group, send, recv, shards, lands, after)
        if group == "in":
            start("mid", lands[0])
            start("up", started["mid"][3][0])
        if group == "mid":
            start("down", lands[0])
        whole = forward_halves("forward_" + group, shards, lands)
        if group == "in":
            g_in, g_pool = whole
            wcat = weights_to_cat(g_in.reshape(NCHIP, D, IN_SHARD))
            pw = jnp.concatenate([g_pool[j].reshape(4, PG, PO // NCHIP) for j in range(NCHIP)], axis=2)
            return wcat, pw
        if group == "mid":
            return whole[0].reshape(D, D), whole[1].reshape(D, D)
        if group == "up":
            return whole[0].reshape(NCHIP, D, D)
        return whole[0].reshape(DFF, D)

    small_w = pack_rows("pack_small_w", [w_alpha[0].reshape(4, QK),
                                         jnp.concatenate([gla_norm_g[0].reshape(1, 512), jnp.zeros((1, 512), F32)], axis=1)], 8)
    sw_all = gather_small("gather_small_w", small_w, False).reshape(8, 8, QK)
    wa_full = jnp.concatenate([sw_all[2 * j, 0:4].reshape(16, DK) for j in range(NCHIP)], axis=1)
    ng_full = jnp.concatenate([sw_all[2 * j, 4, 0:512].reshape(HEADS, DV // NCHIP) for j in range(NCHIP)], axis=1)
    wa_pad = _pad_rows(wa_full, APAD).astype(BF16)
    ng = ng_full.reshape(1, D)

    pending = {}
    wmv = {"in": (w_in_r, m_in_r, v_in_r), "gla_out": (big[1], m_w_gla_out, v_w_gla_out), "out": (big[2], m_w_out, v_w_out),
           "up": (big[3], m_w_mlp_up, v_w_mlp_up), "down": (big[4], m_w_mlp_down, v_w_mlp_down), "pool": (big[5], m_pool_w, v_pool_w)}
    big_res = {}

    def finish(group, after):
        nms, send, recv, sums, lands = pending[group]
        sums, lands = scatter_wait("scatter_wait_" + group, send, recv, sums, lands, after)
        reduced = [sum_chips("sum_chips_" + nm, a, b, chip_i) for nm, a, b in zip(nms, sums, lands)]
        from_sib = join_halves("join_" + group, reduced)
        for nm, g_own, g_sib in zip(nms, reduced, from_sib):
            w, m, v = wmv[nm]
            shp = (2,) + g_own.shape
            big_res[nm] = adamw_halves("adamw_" + nm, w.reshape(shp), g_own, g_sib, m.reshape(shp), v.reshape(shp), core_i)

    def on_grad(group, grads):
        if group == "in":
            gw_in = grads_from_cat(grads["in_cat"])
            gw_pool = jnp.stack([grads["pool"][:, :, j * 128:(j + 1) * 128].reshape(2, 2 * PG, 128)
                                 for j in range(NCHIP)], axis=1)
            grads = {"in": gw_in, "pool": gw_pool}
        nms, parts = list(grads.keys()), list(grads.values())
        theirs = exchange_halves("exchange_" + group, parts)
        sums = [add_pairs("add_pair_" + nm, a, b, core_i) for nm, a, b in zip(nms, parts, theirs)]
        send, recv, sums, lands, token = scatter_start("scatter_start_" + group, sums)
        pending[group] = (nms, send, recv, sums, lands)
        if group != "in":
            return token
        for earlier in ("down", "up", "mix"):
            finish(earlier, token)
        return [big_res[nm][1] for nm in ("down", "up", "out", "gla_out")]

    (loss_local, grad_x, g_mix, g_ps, g_mlp, g_nf, g_ng, g_ba, g_wa) = local_step(
        x2d, tgt, gf, norm_mix_g, pool_scale, wa_pad, b_alpha, ng, norm_mlp_g, get_w, on_grad)[:9]
    loss = lax.psum(loss_local, ("x", "y", "c"))
    finish("in", grad_x)

    ROWS = 16

    def wide(a, n):
        return jnp.concatenate([a.reshape(1, n), jnp.zeros((1, D - n), F32)], axis=1)

    packed = pack_rows("pack_small_g", [g_mix, g_ps, g_mlp, g_nf, g_ng, wide(g_ba, QK), g_wa[0:16].reshape(8, D)], ROWS)
    tot = gather_small("reduce_small_g", packed, True)
    t_wa = lax.dynamic_slice(tot[6:14].reshape(16, QK), (0, chip * DK), (16, DK))
    t_ng = lax.dynamic_slice(tot[4].reshape(HEADS, DV), (0, chip * (DV // NCHIP)), (HEADS, DV // NCHIP))

    def pack_small(nm, mix, ps, mlp, nf, ba, wa, gn):
        return pack_rows(nm, [mix.reshape(1, D), ps.reshape(1, D), mlp.reshape(1, D), nf.reshape(1, D), wide(ba, QK),
                              wa.reshape(2, D), wide(gn, 512)], ROWS)

    sg = pack_small("pack_g", tot[0], tot[1], tot[2], tot[3], tot[5, 0:QK], t_wa, t_ng)
    sw = pack_small("pack_w", norm_mix_g, pool_scale, norm_mlp_g, norm_final_g, b_alpha, w_alpha, gla_norm_g)
    sm = pack_small("pack_m", m_norm_mix_g, m_pool_scale, m_norm_mlp_g, m_norm_final_g, m_b_alpha, m_w_alpha, m_gla_norm_g)
    sv = pack_small("pack_v", v_norm_mix_g, v_pool_scale, v_norm_mlp_g, v_norm_final_g, v_b_alpha, v_w_alpha, v_gla_norm_g)
    small_res = adamw("adamw_small", sw, sg, sm, sv)

    def unpack(p):
        return {"norm_mix_g": p[0].reshape(1, D), "pool_scale": p[1].reshape(1, D), "norm_mlp_g": p[2].reshape(1, D),
                "norm_final_g": p[3].reshape(D), "b_alpha": p[4, 0:QK].reshape(1, QK), "w_alpha": p[5:7].reshape(1, 16, DK),
                "gla_norm_g": p[7, 0:512].reshape(1, HEADS, DV // NCHIP)}

    order = ["norm_mix_g", "w_in", "pool_w", "pool_scale", "w_alpha", "b_alpha", "gla_norm_g", "w_gla_out", "w_out",
             "norm_mlp_g", "w_mlp_up", "w_mlp_down", "norm_final_g"]
    big_key = {"w_in": ("in", w_in.shape), "pool_w": ("pool", pool_w.shape), "w_gla_out": ("gla_out", w_gla_out.shape),
               "w_out": ("out", w_out.shape), "w_mlp_up": ("up", w_mlp_up.shape), "w_mlp_down": ("down", w_mlp_down.shape)}
    result = [loss, grad_x.reshape(1, T, D)]
    for kind in range(4):
        small = unpack(small_res[kind])
        for nm in order:
            if nm in big_key:
                key, shp = big_key[nm]
                result.append(big_res[key][kind].reshape(shp))
            else:
                result.append(small[nm])
    return tuple(result)
```

```python
import itertools

import jax
import jax.numpy as jnp
from jax import lax
from jax.experimental import pallas as pl
from jax.experimental.pallas import tpu as pltpu

F32 = jnp.float32
BF16 = jnp.bfloat16
SDS = jax.ShapeDtypeStruct
MESH = pl.DeviceIdType.MESH
ANY = pl.BlockSpec(memory_space=pl.ANY)

T = 2048
D = 2048
DFF = 8192
NCHIP = 4
IN_WIDTH = 11280
IN_SHARD = IN_WIDTH // NCHIP
CHUNK = 64
NCHUNK = T // CHUNK
HEADS = 4
DK = 256
DV = 512
QK = HEADS * DK
EPS = 1e-6
POOL_WINDOWS = (2, 4, 8, 16)
PG = 256
PO = 512

OV, OG, OGP, OGG, OU, OQ, OKK, OA = 0, 2048, 4096, 6144, 8192, 9216, 10240, 11264
NCAT = 11520
APAD = 128

VMEM_CAP = 56 * 1024 * 1024

PIECE_BYTES = 384 * 1024

ADAM_LR, ADAM_B1, ADAM_B2, ADAM_EPS, ADAM_WD, ADAM_STEP = 0.001, 0.9, 0.999, 1e-08, 0.01, 10


def _cparams(vmem_bytes=None, sem=None):
    kw = {}
    if vmem_bytes is not None:
        kw["vmem_limit_bytes"] = int(min(max(vmem_bytes, 32 * 1024 * 1024), VMEM_CAP))
    if sem is not None:
        kw["dimension_semantics"] = sem
    return pltpu.CompilerParams(**kw)


def _nbytes(shape, dtype):
    n = 1
    for s in shape:
        if s is not None:
            n *= s
    return n * jnp.dtype(dtype).itemsize


def _sigmoid(x):
    return 1.0 / (1.0 + jnp.exp(-x))


def _as_list(after):
    if after is None:
        return []
    return list(after) if isinstance(after, (list, tuple)) else [after]


def matmul(name, a, b, *, a_spec, b_spec, cdims, grid, acc_shape, outs, extras=(), epi, after=None):
    nj, ni, nk = grid
    ne, no = len(extras), len(outs)
    afters = _as_list(after)
    first_out = 2 + ne + len(afters)

    def body(*refs):
        a_ref, b_ref = refs[0], refs[1]
        ex = refs[2:2 + ne]
        out_refs = refs[first_out:first_out + no]
        i = pl.program_id(1)
        part = lax.dot_general(a_ref[...], b_ref[...], (cdims, ((), ())), preferred_element_type=F32)
        if nk == 1:
            epi(part, ex, out_refs, i)
        else:
            acc_ref = refs[first_out + no]
            k = pl.program_id(2)

            @pl.when(k == 0)
            def _():
                acc_ref[...] = part

            @pl.when(k > 0)
            def _():
                acc_ref[...] += part

            @pl.when(k == nk - 1)
            def _():
                epi(acc_ref[...], ex, out_refs, i)

    in_specs = [pl.BlockSpec(*a_spec), pl.BlockSpec(*b_spec)] + [pl.BlockSpec(bs, im) for _, bs, im in extras]
    in_specs += [ANY] * len(afters)
    out_specs = [pl.BlockSpec(bs, im) for _, _, bs, im in outs]
    out_shape = [SDS(s, dt) for s, dt, _, _ in outs]
    vm = 2 * (_nbytes(a_spec[0], a.dtype) + _nbytes(b_spec[0], b.dtype))
    vm += 2 * sum(_nbytes(bs, arr.dtype) for arr, bs, _ in extras)
    vm += 2 * sum(_nbytes(bs, dt) for _, dt, bs, _ in outs)
    vm += 6 * _nbytes(acc_shape, F32)
    scratch = [pltpu.VMEM(acc_shape, F32)] if nk > 1 else []
    return pl.pallas_call(
        body, name=name, grid=grid, in_specs=in_specs, out_specs=out_specs, out_shape=out_shape,
        scratch_shapes=scratch,
        compiler_params=_cparams(vm, ("arbitrary", "arbitrary", "arbitrary")),
    )(a, b, *[arr for arr, _, _ in extras], *afters)


NN =((1,), (0,))
NT = ((1,), (1,))
TN = ((0,), (0,))


def _row_acc(out_ref, val, i):
    @pl.when(i == 0)
    def _():
        out_ref[...] = val

    @pl.when(i > 0)
    def _():
        out_ref[...] += val


def _rms_bwd(xn, r, dxn):
    return r * (dxn - xn * jnp.mean(dxn * xn, axis=-1, keepdims=True))


def norm1(x, g):
    tm = 256

    def body(x_ref, g_ref, h_ref):
        xv = x_ref[...]
        r = lax.rsqrt(jnp.mean(xv * xv, axis=-1, keepdims=True) + EPS)
        h_ref[...] = (xv * r * g_ref[...]).astype(BF16)

    return pl.pallas_call(
        body, name="norm1", grid=(T // tm,),
        in_specs=[pl.BlockSpec((tm, D), lambda i: (i, 0)), pl.BlockSpec((1, D), lambda i: (0, 0))],
        out_specs=pl.BlockSpec((tm, D), lambda i: (i, 0)), out_shape=SDS((T, D), BF16),
        compiler_params=_cparams(32 * 1024 * 1024, ("arbitrary",)),
    )(x, g)


def mm_in(h1, wcat):
    tm, tn = 512, 1280

    def epi(acc, ex, outs, i):
        outs[0][...] = acc.astype(BF16)

    return matmul("mm_in", h1, wcat, a_spec=((tm, D), lambda j, i, k: (i, 0)), b_spec=((D, tn), lambda j, i, k: (0, j)),
                  cdims=NN, grid=(NCAT // tn, T // tm, 1), acc_shape=(tm, tn),
                  outs=[((T, NCAT), BF16, (tm, tn), lambda j, i, k: (i, j))], epi=epi)[0]


def _window_sum(x, w, up):
    n = x.shape[0]
    row = lax.broadcasted_iota(jnp.int32, x.shape, 0)
    s, sh = x, 1
    while sh < w:
        if up:
            s = s + jnp.where(row < n - sh, pltpu.roll(s, n - sh, axis=0), 0.0)
        else:
            s = s + jnp.where(row >= sh, pltpu.roll(s, sh, axis=0), 0.0)
        sh *= 2
    return s


def _inv_count(shape, w):
    row = lax.broadcasted_iota(jnp.int32, shape, 0)
    return 1.0 / jnp.minimum(row + 1, w).astype(F32)


def pool_fwd(pcat, pw):
    def body(u_ref, pw_ref, d_ref, y_ref):
        for gi, w in enumerate(POOL_WINDOWS):
            ug = u_ref[:, gi * PG:(gi + 1) * PG].astype(F32)
            dg = _window_sum(ug, w, False) * _inv_count(ug.shape, w) - ug
            db = dg.astype(BF16)
            d_ref[:, gi * PG:(gi + 1) * PG] = db
            y_ref[:, gi * PO:(gi + 1) * PO] = jnp.dot(db, pw_ref[gi], preferred_element_type=F32).astype(BF16)

    return pl.pallas_call(
        body, name="pool_fwd", grid=(1,),
        in_specs=[pl.BlockSpec((T, 4 * PG), lambda i: (0, OU // (4 * PG))), pl.BlockSpec((4, PG, PO), lambda i: (0, 0, 0))],
        out_specs=[pl.BlockSpec((T, 4 * PG), lambda i: (0, 0)), pl.BlockSpec((T, D), lambda i: (0, 0))],
        out_shape=[SDS((T, 4 * PG), BF16), SDS((T, D), BF16)],
        compiler_params=_cparams(48 * 1024 * 1024, ("arbitrary",)),
    )(pcat, pw)


def pool_bwd(dylin, d, pw):
    def body(dy_ref, d_ref, pw_ref, du_ref, dpw_ref):
        for gi, w in enumerate(POOL_WINDOWS):
            dyl = dy_ref[:, gi * PO:(gi + 1) * PO]
            dd = lax.dot_general(dyl, pw_ref[gi], (NT, ((), ())), preferred_element_type=F32)
            du = _window_sum(dd * _inv_count(dd.shape, w), w, True) - dd
            du_ref[:, gi * PG:(gi + 1) * PG] = du.astype(BF16)
            dpw_ref[gi] = lax.dot_general(d_ref[:, gi * PG:(gi + 1) * PG], dyl, (TN, ((), ())),
                                          preferred_element_type=F32).astype(BF16)

    return pl.pallas_call(
        body, name="pool_bwd", grid=(1,),
        in_specs=[pl.BlockSpec((T, D), lambda i: (0, 0)), pl.BlockSpec((T, 4 * PG), lambda i: (0, 0)),
                  pl.BlockSpec((4, PG, PO), lambda i: (0, 0, 0))],
        out_specs=[pl.BlockSpec((T, 4 * PG), lambda i: (0, 0)), pl.BlockSpec((4, PG, PO), lambda i: (0, 0, 0))],
        out_shape=[SDS((T, 4 * PG), BF16), SDS((4, PG, PO), BF16)],
        compiler_params=_cparams(48 * 1024 * 1024, ("arbitrary",)),
    )(dylin, d, pw)


def _gate_decay(alow, wa, ba):
    a = jnp.dot(alow, wa, preferred_element_type=F32) + ba
    ls = jax.nn.log_sigmoid(a) * (1.0 / 16.0)
    r = lax.broadcasted_iota(jnp.int32, (CHUNK, CHUNK), 0)
    c = lax.broadcasted_iota(jnp.int32, (CHUNK, CHUNK), 1)
    tri = jnp.where(c <= r, 1.0, 0.0).astype(F32)
    cum = jnp.dot(tri, ls, preferred_element_type=F32, precision=lax.Precision.HIGHEST)
    last = cum[CHUNK - 1:CHUNK, :]
    return a, jnp.exp(last - cum), jnp.exp(last)


def gla_fwd(pcat, wa, ba, ng):
    def body(q_ref, k_ref, v_ref, g_ref, al_ref, wa_ref, ba_ref, ng_ref, og_ref, o_ref, st_ref, s_scr):
        @pl.when(pl.program_id(0) == 0)
        def _():
            s_scr[...] = jnp.zeros_like(s_scr)

        _, e, decay = _gate_decay(al_ref[...], wa_ref[...], ba_ref[...])
        kd = (k_ref[...].astype(F32) * e).astype(BF16)
        qs = (q_ref[...].astype(F32) * (DK ** -0.5)).astype(BF16)
        for h in range(HEADS):
            ck = slice(h * DK, (h + 1) * DK)
            cv = slice(h * DV, (h + 1) * DV)
            s_new = s_scr[h] * decay[:, ck] + lax.dot_general(v_ref[:, cv], kd[:, ck], (TN, ((), ())),
                                                               preferred_element_type=F32)
            s_scr[h] = s_new
            sb = s_new.astype(BF16)
            st_ref[h] = sb
            oh = lax.dot_general(qs[:, ck], sb, (NT, ((), ())), preferred_element_type=F32)
            o_ref[:, cv] = oh.astype(BF16)
            on = oh * lax.rsqrt(jnp.mean(oh * oh, axis=-1, keepdims=True) + EPS) * ng_ref[:, cv]
            gv = g_ref[:, cv].astype(F32)
            og_ref[:, cv] = (on * (gv * _sigmoid(gv))).astype(BF16)

    row = lambda c: (c, 0)
    return pl.pallas_call(
        body, name="gla_fwd", grid=(NCHUNK,),
        in_specs=[pl.BlockSpec((CHUNK, QK), lambda c: (c, OQ // QK)), pl.BlockSpec((CHUNK, QK), lambda c: (c, OKK // QK)),
                  pl.BlockSpec((CHUNK, D), lambda c: (c, OV // D)), pl.BlockSpec((CHUNK, D), lambda c: (c, OG // D)),
                  pl.BlockSpec((CHUNK, APAD), lambda c: (c, OA // APAD)),
                  pl.BlockSpec((APAD, QK), lambda c: (0, 0)), pl.BlockSpec((1, QK), lambda c: (0, 0)),
                  pl.BlockSpec((1, D), lambda c: (0, 0))],
        out_specs=[pl.BlockSpec((CHUNK, D), row), pl.BlockSpec((CHUNK, D), row),
                   pl.BlockSpec((None, HEADS, DV, DK), lambda c: (c, 0, 0, 0))],
        out_shape=[SDS((T, D), BF16), SDS((T, D), BF16), SDS((NCHUNK, HEADS, DV, DK), BF16)],
        scratch_shapes=[pltpu.VMEM((HEADS, DV, DK), F32)],
        compiler_params=_cparams(32 * 1024 * 1024, ("arbitrary",)),
    )(pcat, pcat, pcat, pcat, pcat, wa, ba, ng)


def gla_bwd(do, pcat, states, wa, ba, after):
    def body(do_ref, q_ref, k_ref, v_ref, al_ref, sc_ref, sp_ref, wa_ref, ba_ref, after_ref,
             dq_ref, dk_ref, dv_ref, dal_ref, dwa_ref, dba_ref, ds_scr):
        i = pl.program_id(0)

        @pl.when(i == 0)
        def _():
            ds_scr[...] = jnp.zeros_like(ds_scr)

        has_prev = jnp.where(i < NCHUNK - 1, 1.0, 0.0).astype(F32)
        a, e, decay = _gate_decay(al_ref[...], wa_ref[...], ba_ref[...])
        kf = k_ref[...].astype(F32)
        kdf = kf * e
        kd = kdf.astype(BF16)
        qs = (q_ref[...].astype(F32) * (DK ** -0.5)).astype(BF16)
        dkd_parts, ddecay_parts = [], []
        for h in range(HEADS):
            ck = slice(h * DK, (h + 1) * DK)
            cv = slice(h * DV, (h + 1) * DV)
            doh = do_ref[:, cv]
            ds = ds_scr[h] + lax.dot_general(doh, qs[:, ck], (TN, ((), ())), preferred_element_type=F32)
            dsb = ds.astype(BF16)
            dq_ref[:, ck] = (jnp.dot(doh, sc_ref[h], preferred_element_type=F32) * (DK ** -0.5)).astype(BF16)
            dkd_parts.append(jnp.dot(v_ref[:, cv], dsb, preferred_element_type=F32))
            dv_ref[:, cv] = lax.dot_general(kd[:, ck], dsb, (NT, ((), ())), preferred_element_type=F32).astype(BF16)
            ddecay_parts.append(jnp.sum(ds * sp_ref[h].astype(F32), axis=0, keepdims=True) * has_prev)
            ds_scr[h] = ds * decay[:, ck]
        dkd = jnp.concatenate(dkd_parts, axis=1)
        ddecay = jnp.concatenate(ddecay_parts, axis=1)
        dk_ref[...] = (dkd * e).astype(BF16)
        dearg = dkd * kdf
        dlast = jnp.sum(dearg, axis=0, keepdims=True) + ddecay * decay
        r = lax.broadcasted_iota(jnp.int32, (CHUNK, CHUNK), 0)
        c = lax.broadcasted_iota(jnp.int32, (CHUNK, CHUNK), 1)
        triu = jnp.where(c >= r, 1.0, 0.0).astype(F32)
        dls = dlast - jnp.dot(triu, dearg, preferred_element_type=F32, precision=lax.Precision.HIGHEST)
        da = dls * (1.0 / 16.0) * (1.0 - _sigmoid(a))
        dab = da.astype(BF16)
        dal_ref[...] = lax.dot_general(dab, wa_ref[...], (NT, ((), ())), preferred_element_type=F32).astype(BF16)
        dwa = lax.dot_general(al_ref[...], dab, (TN, ((), ())), preferred_element_type=F32)
        dba = jnp.sum(da, axis=0, keepdims=True)

        @pl.when(i == 0)
        def _():
            dwa_ref[...] = dwa
            dba_ref[...] = dba

        @pl.when(i > 0)
        def _():
            dwa_ref[...] += dwa
            dba_ref[...] += dba

    rev = lambda i: NCHUNK - 1 - i
    return pl.pallas_call(
        body, name="gla_bwd", grid=(NCHUNK,),
        in_specs=[pl.BlockSpec((CHUNK, D), lambda i: (rev(i), 0)),
                  pl.BlockSpec((CHUNK, QK), lambda i: (rev(i), OQ // QK)), pl.BlockSpec((CHUNK, QK), lambda i: (rev(i), OKK // QK)),
                  pl.BlockSpec((CHUNK, D), lambda i: (rev(i), OV // D)), pl.BlockSpec((CHUNK, APAD), lambda i: (rev(i), OA // APAD)),
                  pl.BlockSpec((None, HEADS, DV, DK), lambda i: (rev(i), 0, 0, 0)),
                  pl.BlockSpec((None, HEADS, DV, DK), lambda i: (jnp.maximum(rev(i) - 1, 0), 0, 0, 0)),
                  pl.BlockSpec((APAD, QK), lambda i: (0, 0)), pl.BlockSpec((1, QK), lambda i: (0, 0)), ANY],
        out_specs=[pl.BlockSpec((CHUNK, QK), lambda i: (rev(i), 0)), pl.BlockSpec((CHUNK, QK), lambda i: (rev(i), 0)),
                   pl.BlockSpec((CHUNK, D), lambda i: (rev(i), 0)), pl.BlockSpec((CHUNK, APAD), lambda i: (rev(i), 0)),
                   pl.BlockSpec((APAD, QK), lambda i: (0, 0)), pl.BlockSpec((1, QK), lambda i: (0, 0))],
        out_shape=[SDS((T, QK), BF16), SDS((T, QK), BF16), SDS((T, D), BF16), SDS((T, APAD), BF16),
                   SDS((APAD, QK), F32), SDS((1, QK), F32)],
        scratch_shapes=[pltpu.VMEM((HEADS, DV, DK), F32)],
        compiler_params=_cparams(32 * 1024 * 1024, ("arbitrary",)),
    )(do, pcat, pcat, pcat, pcat, states, states, wa, ba, after)


TMF = 256
TMW = 512
_rowblk = ((TMF, D), lambda j, i, k: (i, 0))
_vec = ((1, D), lambda j, i, k: (0, 0))


def _full_spec(col):
    return ((TMF, D), lambda j, i, k: (i, col))


TBIG = 1024


def square_matmul(name, a, b, *, a_spec, b_spec, cdims, nk, after=None):
    def epi(acc, ex, outs, i):
        outs[0][...] = acc

    return matmul(name, a, b, a_spec=a_spec, b_spec=b_spec, cdims=cdims, grid=(D // TBIG, T // TBIG, nk),
                  acc_shape=(TBIG, TBIG), outs=[((T, D), F32, (TBIG, TBIG), lambda j, i, k: (i, j))], epi=epi,
                  after=after)[0]


def rowwise(name, y, *, extras, outs, epi):
    ne = len(extras)

    def body(*refs):
        epi(refs[0][...], refs[1:1 + ne], refs[1 + ne:], pl.program_id(1))

    in_specs = [pl.BlockSpec(*_rowblk)] + [pl.BlockSpec(bs, im) for _, bs, im in extras]
    return pl.pallas_call(
        body, name=name, grid=(1, T // TMF, 1), in_specs=in_specs,
        out_specs=[pl.BlockSpec(bs, im) for _, _, bs, im in outs], out_shape=[SDS(s, dt) for s, dt, _, _ in outs],
        compiler_params=_cparams(40 * 1024 * 1024, ("arbitrary", "arbitrary", "arbitrary")),
    )(y, *[arr for arr, _, _ in extras])


def mm_gla_out(og, w, ylin, pcat, pscale):
    def epi(acc, ex, outs, i):
        ylin_ref, lgp_ref, lgg_ref, ps_ref = ex
        gp = _sigmoid(lgp_ref[...].astype(F32))
        gg = _sigmoid(lgg_ref[...].astype(F32))
        outs[0][...] = (gp * (ylin_ref[...].astype(F32) * ps_ref[...]) + gg * acc).astype(BF16)
        outs[1][...] = acc.astype(BF16)

    return matmul("mm_gla_out", og, w, a_spec=_rowblk, b_spec=((D, D), lambda j, i, k: (0, 0)), cdims=NN,
                  grid=(1, T // TMF, 1), acc_shape=(TMF, D),
                  extras=[(ylin, *_rowblk), (pcat, *_full_spec(OGP // D)), (pcat, *_full_spec(OGG // D)), (pscale, *_vec)],
                  outs=[((T, D), BF16, *_rowblk), ((T, D), BF16, *_rowblk)], epi=epi)


def mm_out(mixed, w, x, g2):
    def epi(acc, ex, outs, i):
        x_ref, g_ref = ex
        x2 = x_ref[...] + acc
        r = lax.rsqrt(jnp.mean(x2 * x2, axis=-1, keepdims=True) + EPS)
        outs[0][...] = x2
        outs[1][...] = (x2 * r * g_ref[...]).astype(BF16)

    return matmul("mm_out", mixed, w, a_spec=_rowblk, b_spec=((D, D), lambda j, i, k: (0, 0)), cdims=NN,
                  grid=(1, T // TMF, 1), acc_shape=(TMF, D), extras=[(x, *_rowblk), (g2, *_vec)],
                  outs=[((T, D), F32, *_rowblk), ((T, D), BF16, *_rowblk)], epi=epi)


def mm_up(h2, wup):
    def epi(acc, ex, outs, i):
        r = jnp.maximum(acc, 0.0)
        outs[0][...] = r.astype(BF16)
        outs[1][...] = (r * r).astype(BF16)

    blk = ((TMW, D), lambda j, i, k: (i, j))
    return matmul("mm_up", h2, wup, a_spec=((TMW, D), lambda j, i, k: (i, 0)), b_spec=((None, D, D), lambda j, i, k: (j, 0, 0)),
                  cdims=NN, grid=(NCHIP, T // TMW, 1), acc_shape=(TMW, D),
                  outs=[((T, DFF), BF16, *blk), ((T, DFF), BF16, *blk)], epi=epi)


def mm_down(act, wdown, x2, tgt, gf):
    tk = 2048

    def epi(acc, ex, outs, i):
        x2_ref, t_ref, g_ref = ex
        dx_ref, dxb_ref, gnf_ref, loss_ref = outs
        x3 = x2_ref[...] + acc
        r = lax.rsqrt(jnp.mean(x3 * x3, axis=-1, keepdims=True) + EPS)
        xn = x3 * r
        err = xn * g_ref[...] - t_ref[...]
        lsum = 0.5 * jnp.sum(jnp.mean(err * err, axis=-1, keepdims=True), axis=0, keepdims=True)
        dy = err * (1.0 / D)
        _row_acc(gnf_ref, jnp.sum(dy * xn, axis=0, keepdims=True), i)
        _row_acc(loss_ref, jnp.broadcast_to(lsum, (1, 128)), i)
        dx3 = _rms_bwd(xn, r, dy * g_ref[...])
        dx_ref[...] = dx3
        dxb_ref[...] = dx3.astype(BF16)

    y = square_matmul("mm_down", act, wdown, a_spec=((TBIG, tk), lambda j, i, k: (i, k)),
                      b_spec=((tk, TBIG), lambda j, i, k: (k, j)), cdims=NN, nk=DFF // tk)
    return rowwise("rows_final", y, extras=[(x2, *_rowblk), (tgt, *_rowblk), (gf, *_vec)],
                   outs=[((T, D), F32, *_rowblk), ((T, D), BF16, *_rowblk), ((1, D), F32, *_vec),
                         ((1, 128), F32, (1, 128), lambda j, i, k: (0, 0))], epi=epi)


def mm_dact(dx3b, wdown, rup, after=None):
    def epi(acc, ex, outs, i):
        outs[0][...] = (acc * 2.0 * ex[0][...].astype(F32)).astype(BF16)

    blk = ((TMW, D), lambda j, i, k: (i, j))
    return matmul("mm_dact", dx3b, wdown, a_spec=((TMW, D), lambda j, i, k: (i, 0)), b_spec=((D, D), lambda j, i, k: (j, 0)),
                  cdims=NT, grid=(DFF // D, T // TMW, 1), acc_shape=(TMW, D), extras=[(rup, *blk)],
                  outs=[((T, DFF), BF16, *blk)], epi=epi, after=after)[0]


def mm_wgrad(name, a, b, m, n, out_shape, out_block, out_map, tm, tn, after=None):
    def epi(acc, ex, outs, i):
        outs[0][...] = acc.astype(BF16)

    return matmul(name, a, b, a_spec=((T, tm), lambda j, i, k: (0, i)), b_spec=((T, tn), lambda j, i, k: (0, j)),
                  cdims=TN, grid=(n // tn, m // tm, 1), acc_shape=(tm, tn),
                  outs=[(out_shape, BF16, out_block, out_map)], epi=epi, after=after)[0]


def mm_dh2(dup, wup, x2, dx3, g2, after=None):
    def epi(acc, ex, outs, i):
        x2_ref, dx3_ref, g_ref = ex
        x2 = x2_ref[...]
        r = lax.rsqrt(jnp.mean(x2 * x2, axis=-1, keepdims=True) + EPS)
        xn = x2 * r
        _row_acc(outs[2], jnp.sum(acc * xn, axis=0, keepdims=True), i)
        dx2 = dx3_ref[...] + _rms_bwd(xn, r, acc * g_ref[...])
        outs[0][...] = dx2
        outs[1][...] = dx2.astype(BF16)

    y = square_matmul("mm_dh2", dup, wup, a_spec=((TBIG, D), lambda j, i, k: (i, k)),
                      b_spec=((None, TBIG, D), lambda j, i, k: (k, j, 0)), cdims=NT, nk=NCHIP, after=after)
    return rowwise("rows_dh2", y, extras=[(x2, *_rowblk), (dx3, *_rowblk), (g2, *_vec)],
                   outs=[((T, D), F32, *_rowblk), ((T, D), BF16, *_rowblk), ((1, D), F32, *_vec)], epi=epi)


def mm_dmixed(dx2b, wout, pcat, ylin, ygla, pscale, after=None):
    def epi(acc, ex, outs, i):
        lgp_ref, lgg_ref, ylin_ref, ygla_ref, ps_ref = ex
        gp = _sigmoid(lgp_ref[...].astype(F32))
        gg = _sigmoid(lgg_ref[...].astype(F32))
        yl = ylin_ref[...].astype(F32)
        ps = ps_ref[...]
        agp = acc * gp
        outs[0][...] = (agp * ps).astype(BF16)
        outs[1][...] = (acc * gg).astype(BF16)
        outs[2][...] = (agp * (yl * ps) * (1.0 - gp)).astype(BF16)
        outs[3][...] = (acc * ygla_ref[...].astype(F32) * gg * (1.0 - gg)).astype(BF16)
        _row_acc(outs[4], jnp.sum(agp * yl, axis=0, keepdims=True), i)

    return matmul("mm_dmixed", dx2b, wout, a_spec=_rowblk, b_spec=((D, D), lambda j, i, k: (0, 0)), cdims=NT,
                  grid=(1, T // TMF, 1), acc_shape=(TMF, D),
                  extras=[(pcat, *_full_spec(OGP // D)), (pcat, *_full_spec(OGG // D)), (ylin, *_rowblk), (ygla, *_rowblk),
                          (pscale, *_vec)],
                  outs=[((T, D), BF16, *_rowblk)] * 4 + [((1, D), F32, *_vec)], epi=epi, after=after)


def mm_dog(dygla, wgo, o, pcat, ng, after=None):
    def epi(acc, ex, outs, i):
        o_ref, g_ref, ng_ref = ex
        do_ref, dg_ref, gng_ref = outs
        gparts = []
        for h in range(HEADS):
            cv = slice(h * DV, (h + 1) * DV)
            oh = o_ref[:, cv].astype(F32)
            r = lax.rsqrt(jnp.mean(oh * oh, axis=-1, keepdims=True) + EPS)
            on = oh * r
            gv = g_ref[:, cv].astype(F32)
            sg = _sigmoid(gv)
            dgain = acc[:, cv] * (gv * sg)
            gparts.append(jnp.sum(dgain * on, axis=0, keepdims=True))
            ngh = ng_ref[:, cv]
            do_ref[:, cv] = _rms_bwd(on, r, dgain * ngh).astype(BF16)
            dg_ref[:, cv] = (acc[:, cv] * (on * ngh) * (sg * (1.0 + gv * (1.0 - sg)))).astype(BF16)
        _row_acc(gng_ref, jnp.concatenate(gparts, axis=1), i)

    return matmul("mm_dog", dygla, wgo, a_spec=_rowblk, b_spec=((D, D), lambda j, i, k: (0, 0)), cdims=NT,
                  grid=(1, T // TMF, 1), acc_shape=(TMF, D),
                  extras=[(o, *_rowblk), (pcat, *_full_spec(OG // D)), (ng, *_vec)],
                  outs=[((T, D), BF16, *_rowblk), ((T, D), BF16, *_rowblk), ((1, D), F32, *_vec)], epi=epi, after=after)


def mm_dh1(dpcat, wcat, x, dx2, g1, after=None):
    tk = 1280

    def epi(acc, ex, outs, i):
        x_ref, dx2_ref, g_ref = ex
        xv = x_ref[...]
        r = lax.rsqrt(jnp.mean(xv * xv, axis=-1, keepdims=True) + EPS)
        xn = xv * r
        _row_acc(outs[1], jnp.sum(acc * xn, axis=0, keepdims=True), i)
        outs[0][...] = dx2_ref[...] + _rms_bwd(xn, r, acc * g_ref[...])

    y = square_matmul("mm_dh1", dpcat, wcat, a_spec=((TBIG, tk), lambda j, i, k: (i, k)),
                      b_spec=((TBIG, tk), lambda j, i, k: (j, k)), cdims=NT, nk=NCAT // tk, after=after)
    return rowwise("rows_dh1", y, extras=[(x, *_rowblk), (dx2, *_rowblk), (g1, *_vec)],
                   outs=[((T, D), F32, *_rowblk), ((1, D), F32, *_vec)], epi=epi)


def _tile_rows(rows, cols, n_arrays):
    tm = rows
    while tm % 32 == 0 and 2 * n_arrays * tm * cols * 4 > 24 * 1024 * 1024:
        tm //= 2
    return tm


def add_pairs(name, parts, theirs, core):
    _, _, r, c = parts.shape
    tm = _tile_rows(r, c, 3)

    def body(core_ref, a_ref, b_ref, o_ref):
        o_ref[...] = (a_ref[...].astype(F32) + b_ref[...].astype(F32)).astype(BF16)

    spec = pl.BlockSpec((None, tm, c), lambda j, i, core_ref: (j, i, 0))
    grid_spec = pltpu.PrefetchScalarGridSpec(
        num_scalar_prefetch=1, grid=(NCHIP, r // tm),
        in_specs=[pl.BlockSpec((None, None, tm, c), lambda j, i, core_ref: (core_ref[0], j, i, 0)), spec], out_specs=spec)
    return pl.pallas_call(body, name=name, grid_spec=grid_spec, out_shape=SDS((NCHIP, r, c), BF16),
                          compiler_params=_cparams(40 * 1024 * 1024, ("arbitrary", "arbitrary")))(core, parts, theirs)


def sum_chips(name, sums, landed, chip):
    _, r, c = sums.shape
    tm = _tile_rows(r, c, 4)

    def body(chip_ref, own_ref, l_ref, o_ref):
        s = own_ref[...].astype(F32)
        for t in range(NCHIP - 1):
            s = s + l_ref[t].astype(F32)
        o_ref[...] = s

    grid_spec = pltpu.PrefetchScalarGridSpec(
        num_scalar_prefetch=1, grid=(r // tm,),
        in_specs=[pl.BlockSpec((None, tm, c), lambda i, chip_ref: (chip_ref[0], i, 0)),
                  pl.BlockSpec((NCHIP - 1, tm, c), lambda i, chip_ref: (0, i, 0))],
        out_specs=pl.BlockSpec((tm, c), lambda i, chip_ref: (i, 0)))
    return pl.pallas_call(body, name=name, grid_spec=grid_spec, out_shape=SDS((r, c), F32),
                          compiler_params=_cparams(40 * 1024 * 1024, ("arbitrary",)))(chip, sums, landed)


def _adamw_math(wv, gv, mv, vv):
    mn = ADAM_B1 * mv + (1.0 - ADAM_B1) * gv
    vn = ADAM_B2 * vv + (1.0 - ADAM_B2) * (gv * gv)
    mh = mn / (1.0 - ADAM_B1 ** ADAM_STEP)
    vh = vn / (1.0 - ADAM_B2 ** ADAM_STEP)
    return -ADAM_LR * (mh / (jnp.sqrt(vh) + ADAM_EPS) + ADAM_WD * wv), mn, vn


def adamw(name, w, g, m, v):
    def body(w_ref, g_ref, m_ref, v_ref, go_ref, d_ref, mo_ref, vo_ref):
        gv = g_ref[...]
        go_ref[...] = gv
        d_ref[...], mo_ref[...], vo_ref[...] = _adamw_math(w_ref[...], gv, m_ref[...], v_ref[...])

    return pl.pallas_call(body, name=name, out_shape=[SDS(w.shape, F32)] * 4)(w, g, m, v)


def adamw_halves(name, w, g_own, g_sib, m, v, core):
    _, r, c = w.shape
    tm = _tile_rows(r, c, 10)

    def body(core_ref, w_ref, go_ref, gs_ref, m_ref, v_ref, g_out, d_out, m_out, v_out):
        gv = jnp.where(pl.program_id(0) == core_ref[0], go_ref[...], gs_ref[...])
        g_out[...] = gv
        d_out[...], m_out[...], v_out[...] = _adamw_math(w_ref[...], gv, m_ref[...], v_ref[...])

    full = pl.BlockSpec((None, tm, c), lambda h, i, core_ref: (h, i, 0))
    own = pl.BlockSpec((tm, c), lambda h, i, core_ref: (jnp.where(h == core_ref[0], i, 0), 0))
    sib = pl.BlockSpec((tm, c), lambda h, i, core_ref: (jnp.where(h == core_ref[0], 0, i), 0))
    grid_spec = pltpu.PrefetchScalarGridSpec(num_scalar_prefetch=1, grid=(2, r // tm),
                                             in_specs=[full, own, sib, full, full], out_specs=[full] * 4)
    return pl.pallas_call(body, name=name, grid_spec=grid_spec, out_shape=[SDS(w.shape, F32)] * 4,
                          compiler_params=_cparams(48 * 1024 * 1024, ("arbitrary", "arbitrary")))(core, w, g_own, g_sib, m, v)


def pack_rows(name, parts, rows):
    width = parts[0].shape[1]
    n = len(parts)

    def body(*refs):
        out_ref = refs[n]
        out_ref[...] = jnp.zeros_like(out_ref)
        off = 0
        for p in refs[:n]:
            out_ref[off:off + p.shape[0], :] = p[...]
            off += p.shape[0]

    return pl.pallas_call(body, name=name, out_shape=SDS((rows, width), F32))(*parts)


def _place():
    x, y, c = lax.axis_index("x"), lax.axis_index("y"), lax.axis_index("c")
    chips = [(1 - x, y), (x, 1 - y), (1 - x, 1 - y)]
    return x, y, c, chips


def _row_split(shape, dtype):
    r, c = shape
    n = 1
    while r % (2 * n) == 0 and (r // (2 * n)) % 16 == 0 and (r // n) * c * jnp.dtype(dtype).itemsize > PIECE_BYTES:
        n *= 2
    return [pl.ds(s * (r // n), r // n) for s in range(n)]


def _pieces(ref):
    *lead, r, c = ref.shape
    split = _row_split((r, c), ref.dtype)
    return [ref.at[(*idx, s)] for idx in itertools.product(*[range(d) for d in lead]) for s in split]


HBM = pl.BlockSpec(memory_space=pltpu.HBM)
SEM = pl.BlockSpec(memory_space=pltpu.SEMAPHORE)
EFFECT = pltpu.SideEffectType.DATAFLOW_SIDE_EFFECTING


def gather_start(name, shards, after=None):
    n = len(shards)
    extra = [] if after is None else [after]

    def body(*refs):
        src, land = refs[:n], refs[n:2 * n]
        send, recv = refs[2 * n + len(extra)], refs[2 * n + len(extra) + 1]
        x, y, c, chips = _place()
        me = 2 * x + y
        for a in range(n):
            for j, (cx, cy) in enumerate(chips):
                for sp, dp in zip(_pieces(src[a].at[c]), _pieces(land[a].at[me, c])):
                    pltpu.make_async_remote_copy(sp, dp, send.at[3 * a + j], recv.at[3 * a + j],
                                                 device_id=(cx, cy, c), device_id_type=MESH).start()

    lands = [pltpu.with_memory_space_constraint(lax.empty((NCHIP,) + s.shape, s.dtype), pltpu.HBM) for s in shards]
    srcs = [pltpu.with_memory_space_constraint(s, pltpu.HBM) for s in shards]
    outs = pl.pallas_call(
        body, name=name,
        out_shape=(pltpu.SemaphoreType.DMA((3 * n,)), pltpu.SemaphoreType.DMA((3 * n,)),
                   *[pltpu.HBM(s.shape, s.dtype) for s in shards], *[pltpu.HBM(l.shape, l.dtype) for l in lands]),
        in_specs=[HBM] * (2 * n) + [ANY] * len(extra), out_specs=(SEM, SEM, *([HBM] * (2 * n))),
        input_output_aliases={i: 2 + i for i in range(2 * n)},
        compiler_params=pltpu.CompilerParams(has_side_effects=EFFECT),
    )(*srcs, *lands, *extra)
    return outs[0], outs[1], list(outs[2:2 + n]), list(outs[2 + n:2 + 2 * n])


def gather_wait(name, send, recv, shards, lands, after):
    n = len(shards)
    afters = _as_list(after)

    def body(*refs):
        src, land = refs[:n], refs[n:2 * n]
        send_ref, recv_ref = refs[2 * n], refs[2 * n + 1]
        x, y, c, chips = _place()
        for a in range(n):
            for j, (cx, cy) in enumerate(chips):
                cp = pltpu.make_async_remote_copy(src[a].at[c], land[a].at[2 * cx + cy, c], send_ref.at[3 * a + j],
                                                  recv_ref.at[3 * a + j], device_id=(cx, cy, c), device_id_type=MESH)
                cp.wait_send()
                cp.wait_recv()

    outs = pl.pallas_call(
        body, name=name,
        out_shape=(*[pltpu.HBM(s.shape, s.dtype) for s in shards], *[pltpu.HBM(l.shape, l.dtype) for l in lands]),
        in_specs=[HBM] * (2 * n) + [SEM, SEM] + [ANY] * len(afters), out_specs=[HBM] * (2 * n),
        input_output_aliases={i: i for i in range(2 * n)},
        compiler_params=pltpu.CompilerParams(has_side_effects=EFFECT),
    )(*shards, *lands, send, recv, *afters)
    return list(outs[:n]), list(outs[n:])


def forward_halves(name, shards, lands):
    n = len(lands)

    def body(*refs):
        had, buf = refs[:n], refs[n:2 * n]
        send, recv = refs[2 * n:]
        x, y, c, chips = _place()
        sib = (x, y, 1 - c)
        for a in range(n):
            for j, (cx, cy) in enumerate(chips):
                for sp, dp in zip(_pieces(had[a].at[2 * cx + cy, c]), _pieces(buf[a].at[2 * cx + cy, c])):
                    pltpu.make_async_remote_copy(sp, dp, send.at[3 * a + j], recv.at[3 * a + j], device_id=sib, device_id_type=MESH).start()
        for a in range(n):
            for j, (cx, cy) in enumerate(chips):
                pltpu.make_async_remote_copy(had[a].at[2 * cx + cy, c], buf[a].at[2 * cx + cy, 1 - c], send.at[3 * a + j],
                                             recv.at[3 * a + j], device_id=sib, device_id_type=MESH).wait()

    got = pl.pallas_call(
        body, name=name, in_specs=[ANY] * n, out_specs=[ANY] * n, out_shape=[SDS(l.shape, l.dtype) for l in lands],
        input_output_aliases={i: i for i in range(n)},
        scratch_shapes=[pltpu.SemaphoreType.DMA((3 * n,)), pltpu.SemaphoreType.DMA((3 * n,))],
    )(*lands)
    me = 2 * lax.axis_index("x") + lax.axis_index("y")
    return [lax.dynamic_update_index_in_dim(g, s, me, 0) for g, s in zip(got, shards)]


def exchange_start(name, parts):
    n = len(parts)

    def body(*refs):
        src, got = refs[:n], refs[n:2 * n]
        send, recv = refs[2 * n], refs[2 * n + 1]
        token = refs[4 * n + 2]
        x, y, c, _ = _place()
        sib = (x, y, 1 - c)
        for a in range(n):
            for sp, dp in zip(_pieces(src[a].at[1 - c]), _pieces(got[a])):
                pltpu.make_async_remote_copy(sp, dp, send.at[a], recv.at[a], device_id=sib, device_id_type=MESH).start()
        token[...] = jnp.zeros_like(token)

    lands = [pltpu.with_memory_space_constraint(lax.empty(p.shape[1:], p.dtype), pltpu.HBM) for p in parts]
    srcs = [pltpu.with_memory_space_constraint(p, pltpu.HBM) for p in parts]
    outs = pl.pallas_call(
        body, name=name,
        out_shape=(pltpu.SemaphoreType.DMA((n,)), pltpu.SemaphoreType.DMA((n,)),
                   *[pltpu.HBM(p.shape, p.dtype) for p in parts], *[pltpu.HBM(l.shape, l.dtype) for l in lands],
                   SDS((8, 128), F32)),
        in_specs=[HBM] * (2 * n), out_specs=(SEM, SEM, *([HBM] * (2 * n)), pl.BlockSpec(memory_space=pltpu.VMEM)),
        input_output_aliases={i: 2 + i for i in range(2 * n)},
        compiler_params=pltpu.CompilerParams(has_side_effects=EFFECT),
    )(*srcs, *lands)
    return outs[0], outs[1], list(outs[2:2 + n]), list(outs[2 + n:2 + 2 * n]), outs[2 + 2 * n]


def exchange_wait(name, send, recv, parts, lands, after):
    n = len(parts)
    afters = _as_list(after)

    def body(*refs):
        src, got = refs[:n], refs[n:2 * n]
        send_ref, recv_ref = refs[2 * n], refs[2 * n + 1]
        x, y, c, _ = _place()
        sib = (x, y, 1 - c)
        for a in range(n):
            cp = pltpu.make_async_remote_copy(src[a].at[1 - c], got[a], send_ref.at[a], recv_ref.at[a], device_id=sib, device_id_type=MESH)
            cp.wait_send()
            cp.wait_recv()

    outs = pl.pallas_call(
        body, name=name,
        out_shape=(*[pltpu.HBM(p.shape, p.dtype) for p in parts], *[pltpu.HBM(l.shape, l.dtype) for l in lands]),
        in_specs=[HBM] * (2 * n) + [SEM, SEM] + [ANY] * len(afters), out_specs=[HBM] * (2 * n),
        input_output_aliases={i: i for i in range(2 * n)},
        compiler_params=pltpu.CompilerParams(has_side_effects=EFFECT),
    )(*parts, *lands, send, recv, *afters)
    return list(outs[:n]), list(outs[n:])


def scatter_start(name, parts):
    n = len(parts)

    def body(*refs):
        src, land = refs[:n], refs[n:2 * n]
        send, recv = refs[2 * n], refs[2 * n + 1]
        token = refs[4 * n + 2]
        x, y, c, chips = _place()
        for a in range(n):
            for j, (cx, cy) in enumerate(chips):
                for sp, dp in zip(_pieces(src[a].at[2 * cx + cy]), _pieces(land[a].at[j])):
                    pltpu.make_async_remote_copy(sp, dp, send.at[3 * a + j], recv.at[3 * a + j],
                                                 device_id=(cx, cy, c), device_id_type=MESH).start()
        token[...] = jnp.zeros_like(token)

    lands = [pltpu.with_memory_space_constraint(lax.empty((NCHIP - 1,) + p.shape[1:], p.dtype), pltpu.HBM) for p in parts]
    srcs = [pltpu.with_memory_space_constraint(p, pltpu.HBM) for p in parts]
    outs = pl.pallas_call(
        body, name=name,
        out_shape=(pltpu.SemaphoreType.DMA((3 * n,)), pltpu.SemaphoreType.DMA((3 * n,)),
                   *[pltpu.HBM(p.shape, p.dtype) for p in parts], *[pltpu.HBM(l.shape, l.dtype) for l in lands],
                   SDS((8, 128), F32)),
        in_specs=[HBM] * (2 * n), out_specs=(SEM, SEM, *([HBM] * (2 * n)), pl.BlockSpec(memory_space=pltpu.VMEM)),
        input_output_aliases={i: 2 + i for i in range(2 * n)},
        compiler_params=pltpu.CompilerParams(has_side_effects=EFFECT),
    )(*srcs, *lands)
    return outs[0], outs[1], list(outs[2:2 + n]), list(outs[2 + n:2 + 2 * n]), outs[2 + 2 * n]


def scatter_wait(name, send, recv, parts, lands, after):
    n = len(parts)
    afters = _as_list(after)

    def body(*refs):
        src, land = refs[:n], refs[n:2 * n]
        send_ref, recv_ref = refs[2 * n], refs[2 * n + 1]
        x, y, c, chips = _place()
        for a in range(n):
            for j, (cx, cy) in enumerate(chips):
                cp = pltpu.make_async_remote_copy(src[a].at[2 * cx + cy], land[a].at[j], send_ref.at[3 * a + j], recv_ref.at[3 * a + j],
                                                  device_id=(cx, cy, c), device_id_type=MESH)
                cp.wait_send()
                cp.wait_recv()

    outs = pl.pallas_call(
        body, name=name,
        out_shape=(*[pltpu.HBM(p.shape, p.dtype) for p in parts], *[pltpu.HBM(l.shape, l.dtype) for l in lands]),
        in_specs=[HBM] * (2 * n) + [SEM, SEM] + [ANY] * len(afters), out_specs=[HBM] * (2 * n),
        input_output_aliases={i: i for i in range(2 * n)},
        compiler_params=pltpu.CompilerParams(has_side_effects=EFFECT),
    )(*parts, *lands, send, recv, *afters)
    return list(outs[:n]), list(outs[n:])


def join_halves(name, halves):
    n = len(halves)

    def body(*refs):
        src, dst = refs[:n], refs[n:2 * n]
        send, recv = refs[2 * n:]
        x, y, c, _ = _place()
        sib = (x, y, 1 - c)
        for a in range(n):
            for sp, dp in zip(_pieces(src[a]), _pieces(dst[a])):
                pltpu.make_async_remote_copy(sp, dp, send.at[a], recv.at[a], device_id=sib, device_id_type=MESH).start()
        for a in range(n):
            pltpu.make_async_remote_copy(src[a], dst[a], send.at[a], recv.at[a], device_id=sib, device_id_type=MESH).wait()

    return pl.pallas_call(
        body, name=name, in_specs=[ANY] * n, out_specs=[ANY] * n,
        out_shape=[SDS(h.shape, h.dtype) for h in halves],
        scratch_shapes=[pltpu.SemaphoreType.DMA((n,)), pltpu.SemaphoreType.DMA((n,))],
    )(*halves)


def gather_small(name, xs, reduce):
    m, ncol = xs.shape

    def body(x_ref, out_ref, all_ref, send, recv, lsem):
        x, y, c, chips = _place()
        me, sib = (x, y, c), (x, y, 1 - c)

        def rows(px, py, pc):
            return all_ref.at[pl.ds((4 * px + 2 * py + pc) * m, m), :]

        def copy(k, block, to, src=None):
            return pltpu.make_async_remote_copy(rows(*block) if src is None else src, rows(*block), send.at[k], recv.at[k],
                                                device_id=to, device_id_type=MESH)

        mine = pltpu.make_async_copy(x_ref, rows(*me), lsem)
        mine.start()
        first = [copy(0, me, sib, src=x_ref)] + [copy(1 + j, me, (*chip, c), src=x_ref) for j, chip in enumerate(chips)]
        for cp in first:
            cp.start()
        passed = [copy(4 + j, (*chip, c), sib) for j, chip in enumerate(chips)]
        for j, chip in enumerate(chips):
            copy(1 + j, (*chip, c), me).wait_recv()
            passed[j].start()
        copy(0, sib, me).wait_recv()
        for j, chip in enumerate(chips):
            copy(4 + j, (*chip, 1 - c), me).wait_recv()
        for cp in first + passed:
            cp.wait_send()
        mine.wait()
        if reduce:
            s = all_ref[0:m, :]
            for dev in range(1, 8):
                s = s + all_ref[dev * m:(dev + 1) * m, :]
            out_ref[...] = s
        else:
            out_ref[...] = all_ref[...]

    vm = pl.BlockSpec(memory_space=pltpu.VMEM)
    return pl.pallas_call(
        body, name=name, in_specs=[vm], out_specs=vm, out_shape=SDS((m, ncol) if reduce else (8 * m, ncol), F32),
        scratch_shapes=[pltpu.VMEM((8 * m, ncol), F32), pltpu.SemaphoreType.DMA((7,)), pltpu.SemaphoreType.DMA((7,)),
                        pltpu.SemaphoreType.DMA],
    )(xs)


RELAYOUT_ROWS = 128


def weights_to_cat(g_in):
    tm = RELAYOUT_ROWS

    def body(g_ref, o_ref):
        nat = jnp.concatenate([g_ref[j] for j in range(NCHIP)], axis=1)
        pad = jnp.zeros((tm, NCAT - OA - 16), BF16)
        o_ref[...] = jnp.concatenate([nat[:, 3072:7168], nat[:, 7184:11280], nat[:, 0:3072], nat[:, 7168:7184], pad], axis=1)

    return pl.pallas_call(
        body, name="weights_to_cat", grid=(D // tm,), in_specs=[pl.BlockSpec((NCHIP, tm, IN_SHARD), lambda i: (0, i, 0))],
        out_specs=pl.BlockSpec((tm, NCAT), lambda i: (i, 0)), out_shape=SDS((D, NCAT), BF16),
        compiler_params=_cparams(40 * 1024 * 1024, ("arbitrary",)),
    )(g_in)


def grads_from_cat(gw_cat):
    tm = RELAYOUT_ROWS
    nb = (D // 2) // tm

    def body(c_ref, o_ref):
        cat = c_ref[...]
        nat = jnp.concatenate([cat[:, OU:OA], cat[:, OV:OGP], cat[:, OA:OA + 16], cat[:, OGP:OU]], axis=1)
        for j in range(NCHIP):
            o_ref[j] = nat[:, j * IN_SHARD:(j + 1) * IN_SHARD]

    return pl.pallas_call(
        body, name="grads_from_cat", grid=(D // tm,), in_specs=[pl.BlockSpec((tm, NCAT), lambda i: (i, 0))],
        out_specs=pl.BlockSpec((None, NCHIP, tm, IN_SHARD), lambda i: (i // nb, 0, i % nb, 0)),
        out_shape=SDS((2, NCHIP, D // 2, IN_SHARD), BF16), compiler_params=_cparams(40 * 1024 * 1024, ("arbitrary",)),
    )(gw_cat)


def _pad_rows(a, rows):
    return jnp.concatenate([a, jnp.zeros((rows - a.shape[0],) + a.shape[1:], a.dtype)], axis=0)


def local_step(x2d, tgt, gf, g1, pool_scale, wa_pad, b_alpha, ng, g2, get_w, on_grad=None, on_settle=None):
    emit = on_grad if on_grad is not None else (lambda group, grads: None)
    settle = on_settle if on_settle is not None else (lambda group, after: None)
    h1 = norm1(x2d, g1)
    wcat, pw = get_w("in", h1)
    pcat = mm_in(h1, wcat)
    dpool, ylin = pool_fwd(pcat, pw)
    og, o, states = gla_fwd(pcat, wa_pad, b_alpha, ng)
    w_go, w_o = get_w("mid", og)
    mixed, ygla = mm_gla_out(og, w_go, ylin, pcat, pool_scale)
    x2, h2 = mm_out(mixed, w_o, x2d, g2)
    w_up = get_w("up", h2)
    rup, act = mm_up(h2, w_up)
    w_dn = get_w("down", act)
    dx3, dx3b, g_nf, loss_row = mm_down(act, w_dn, x2, tgt, gf)

    gw_down = mm_wgrad("mm_dw_down", act, dx3b, DFF, D, (2, NCHIP, D // 2, D), (None, None, 512, D),
                       lambda j, i, k: ((i // 2) % 2, i // 4, i % 2, 0), 512, D)
    token = emit("down", {"down": gw_down})
    dup = mm_dact(dx3b, w_dn, rup, after=token)
    token = settle("down", dup)
    dx2, dx2b, g_mlp = mm_dh2(dup, w_up, x2, dx3, g2, after=token)
    gw_up = mm_wgrad("mm_dw_up", h2, dup, D, DFF, (2, NCHIP, D // 2, D), (None, None, 512, D),
                     lambda j, i, k: (i // 2, j, i % 2, 0), 512, D)
    token = emit("up", {"up": gw_up})
    dylin, dygla, dlgp, dlgg, g_ps = mm_dmixed(dx2b, w_o, pcat, ylin, ygla, pool_scale, after=token)
    token = settle("up", dylin)
    gw_out = mm_wgrad("mm_dw_out", mixed, dx2b, D, D, (2, NCHIP, 256, D), (None, None, 256, D),
                      lambda j, i, k: (i % 2, i // 2, 0, 0), 256, D)
    do, dg, g_ng = mm_dog(dygla, w_go, o, pcat, ng, after=token)
    gw_go = mm_wgrad("mm_dw_gla_out", og, dygla, D, D, (2, NCHIP, 256, D), (None, None, 256, D),
                     lambda j, i, k: (i % 2, i // 2, 0, 0), 256, D)
    token = emit("mix", {"out": gw_out, "gla_out": gw_go})
    dq, dk, dv, dalow, g_wa, g_ba = gla_bwd(do, pcat, states, wa_pad, b_alpha, b_alpha if token is None else token)
    token = settle("mix", dq)
    du, dpw = pool_bwd(dylin, dpool, pw)
    dpcat = jnp.concatenate([dv, dg, dlgp, dlgg, du, dq, dk, dalow, jnp.zeros((T, NCAT - OA - APAD), BF16)], axis=1)
    gw_cat = mm_wgrad("mm_dw_in", h1, dpcat, D, NCAT, (D, NCAT), (512, 1280), lambda j, i, k: (i, j), 512, 1280, after=token)
    token = settle("in", emit("in", {"in_cat": gw_cat, "pool": dpw}))
    grad_x, g_mix = mm_dh1(dpcat, wcat, x2d, dx2, g1, after=token)
    return (loss_row[0, 0], grad_x, g_mix, g_ps, g_mlp, g_nf, g_ng, g_ba, g_wa, token,
            gw_cat, dpw, gw_go, gw_out, gw_up, gw_down)


def kernel(x, norm_mix_g, w_in, pool_w, pool_scale, w_alpha, b_alpha, gla_norm_g, w_gla_out, w_out, norm_mlp_g, w_mlp_up, w_mlp_down, norm_final_g, loss_target, m_norm_mix_g, m_w_in, m_pool_w, m_pool_scale, m_w_alpha, m_b_alpha, m_gla_norm_g, m_w_gla_out, m_w_out, m_norm_mlp_g, m_w_mlp_up, m_w_mlp_down, m_norm_final_g, v_norm_mix_g, v_w_in, v_pool_w, v_pool_scale, v_w_alpha, v_b_alpha, v_gla_norm_g, v_w_gla_out, v_w_out, v_norm_mlp_g, v_w_mlp_up, v_w_mlp_down, v_norm_final_g):
    chip = 2 * lax.axis_index("x") + lax.axis_index("y")
    chip_i = chip.astype(jnp.int32).reshape(1)
    core_i = lax.axis_index("c").astype(jnp.int32).reshape(1)
    x2d = x.reshape(T, D)
    tgt = loss_target.reshape(T, D)
    gf = norm_final_g.reshape(1, D)

    def halves(w2d):
        r, c = w2d.shape
        return w2d.astype(BF16).reshape(2, r // 2, c)

    pool_shard = pool_w.reshape(4 * PG, PO // NCHIP)
    big = [w_in[0], w_gla_out[0], w_out[0], w_mlp_up[0], w_mlp_down[0], pool_shard]
    groups = {"in": [big[0], big[5]], "mid": [big[1], big[2]], "up": [big[3]], "down": [big[4]]}
    sent = {g: [halves(w) for w in ws] for g, ws in groups.items()}
    started = {}

    def start(group, after=None):
        started[group] = gather_start("gather_start_" + group, sent[group], after)

    start("in")
    w_in_r, m_in_r, v_in_r = [a.reshape(2, D // 2, IN_SHARD)
                              for a in lax.optimization_barrier((w_in, m_w_in, v_w_in, started["in"][2][0]))[:3]]

    def get_w(group, after):
        send, recv, shards, lands = started[group]
        if group == "in":
            after = [after, w_in_r, m_in_r, v_in_r, *sent["mid"], *sent["up"], *sent["down"], wa_pad]
        shards, lands = gather_wait("gather_wait_" + group, send, recv, shards, lands, after)
        if group == "in":
            start("mid", lands[0])
            start("up", started["mid"][3][0])
        if group == "mid":
            start("down", lands[0])
        whole = forward_halves("forward_" + group, shards, lands)
        if group == "in":
            g_in, g_pool = whole
            wcat = weights_to_cat(g_in.reshape(NCHIP, D, IN_SHARD))
            pw = jnp.concatenate([g_pool[j].reshape(4, PG, PO // NCHIP) for j in range(NCHIP)], axis=2)
            return wcat, pw
        if group == "mid":
            return whole[0].reshape(D, D), whole[1].reshape(D, D)
        if group == "up":
            return whole[0].reshape(NCHIP, D, D)
        return whole[0].reshape(DFF, D)

    small_w = pack_rows("pack_small_w", [w_alpha[0].reshape(4, QK),
                                         jnp.concatenate([gla_norm_g[0].reshape(1, 512), jnp.zeros((1, 512), F32)], axis=1)], 8)
    sw_all = gather_small("gather_small_w", small_w, False).reshape(8, 8, QK)
    wa_full = jnp.concatenate([sw_all[2 * j, 0:4].reshape(16, DK) for j in range(NCHIP)], axis=1)
    ng_full = jnp.concatenate([sw_all[2 * j, 4, 0:512].reshape(HEADS, DV // NCHIP) for j in range(NCHIP)], axis=1)
    wa_pad = _pad_rows(wa_full, APAD).astype(BF16)
    ng = ng_full.reshape(1, D)

    pending = {}
    wmv = {"in": (w_in_r, m_in_r, v_in_r), "gla_out": (big[1], m_w_gla_out, v_w_gla_out), "out": (big[2], m_w_out, v_w_out),
           "up": (big[3], m_w_mlp_up, v_w_mlp_up), "down": (big[4], m_w_mlp_down, v_w_mlp_down), "pool": (big[5], m_pool_w, v_pool_w)}
    big_res = {}

    def finish(group, after):
        nms, send, recv, sums, lands = pending[group]
        sums, lands = scatter_wait("scatter_wait_" + group, send, recv, sums, lands, after)
        reduced = [sum_chips("sum_chips_" + nm, a, b, chip_i) for nm, a, b in zip(nms, sums, lands)]
        from_sib = join_halves("join_" + group, reduced)
        for nm, g_own, g_sib in zip(nms, reduced, from_sib):
            w, m, v = wmv[nm]
            shp = (2,) + g_own.shape
            big_res[nm] = adamw_halves("adamw_" + nm, w.reshape(shp), g_own, g_sib, m.reshape(shp), v.reshape(shp), core_i)

    def on_grad(group, grads):
        if group == "in":
            gw_in = grads_from_cat(grads["in_cat"])
            gw_pool = jnp.stack([grads["pool"][:, :, j * 128:(j + 1) * 128].reshape(2, 2 * PG, 128)
                                 for j in range(NCHIP)], axis=1)
            grads = {"in": gw_in, "pool": gw_pool}
        nms, parts = list(grads.keys()), list(grads.values())
        send, recv, parts, got, token = exchange_start("exchange_start_" + group, parts)
        pending[group] = (nms, send, recv, parts, got)
        return token

    def on_settle(group, after):
        if group == "in":
            finish("down", after)
            after = big_res["down"][1]
        nms, send, recv, parts, got = pending[group]
        parts, got = exchange_wait("exchange_wait_" + group, send, recv, parts, got, after)
        sums = [add_pairs("add_pair_" + nm, a, b, core_i) for nm, a, b in zip(nms, parts, got)]
        send, recv, sums, lands, token = scatter_start("scatter_start_" + group, sums)
        pending[group] = (nms, send, recv, sums, lands)
        if group != "in":
            return token
        for earlier in ("up", "mix"):
            finish(earlier, token)
        return [big_res[nm][1] for nm in ("up", "out", "gla_out")]

    (loss_local, grad_x, g_mix, g_ps, g_mlp, g_nf, g_ng, g_ba, g_wa) = local_step(
        x2d, tgt, gf, norm_mix_g, pool_scale, wa_pad, b_alpha, ng, norm_mlp_g, get_w, on_grad, on_settle)[:9]
    loss = lax.psum(loss_local, ("x", "y", "c"))
    finish("in", grad_x)

    ROWS = 16

    def wide(a, n):
        return jnp.concatenate([a.reshape(1, n), jnp.zeros((1, D - n), F32)], axis=1)

    packed = pack_rows("pack_small_g", [g_mix, g_ps, g_mlp, g_nf, g_ng, wide(g_ba, QK), g_wa[0:16].reshape(8, D)], ROWS)
    tot = gather_small("reduce_small_g", packed, True)
    t_wa = lax.dynamic_slice(tot[6:14].reshape(16, QK), (0, chip * DK), (16, DK))
    t_ng = lax.dynamic_slice(tot[4].reshape(HEADS, DV), (0, chip * (DV // NCHIP)), (HEADS, DV // NCHIP))

    def pack_small(nm, mix, ps, mlp, nf, ba, wa, gn):
        return pack_rows(nm, [mix.reshape(1, D), ps.reshape(1, D), mlp.reshape(1, D), nf.reshape(1, D), wide(ba, QK),
                              wa.reshape(2, D), wide(gn, 512)], ROWS)

    sg = pack_small("pack_g", tot[0], tot[1], tot[2], tot[3], tot[5, 0:QK], t_wa, t_ng)
    sw = pack_small("pack_w", norm_mix_g, pool_scale, norm_mlp_g, norm_final_g, b_alpha, w_alpha, gla_norm_g)
    sm = pack_small("pack_m", m_norm_mix_g, m_pool_scale, m_norm_mlp_g, m_norm_final_g, m_b_alpha, m_w_alpha, m_gla_norm_g)
    sv = pack_small("pack_v", v_norm_mix_g, v_pool_scale, v_norm_mlp_g, v_norm_final_g, v_b_alpha, v_w_alpha, v_gla_norm_g)
    small_res = adamw("adamw_small", sw, sg, sm, sv)

    def unpack(p):
        return {"norm_mix_g": p[0].reshape(1, D), "pool_scale": p[1].reshape(1, D), "norm_mlp_g": p[2].reshape(1, D),
                "norm_final_g": p[3].reshape(D), "b_alpha": p[4, 0:QK].reshape(1, QK), "w_alpha": p[5:7].reshape(1, 16, DK),
                "gla_norm_g": p[7, 0:512].reshape(1, HEADS, DV // NCHIP)}

    order = ["norm_mix_g", "w_in", "pool_w", "pool_scale", "w_alpha", "b_alpha", "gla_norm_g", "w_gla_out", "w_out",
             "norm_mlp_g", "w_mlp_up", "w_mlp_down", "norm_final_g"]
    big_key = {"w_in": ("in", w_in.shape), "pool_w": ("pool", pool_w.shape), "w_gla_out": ("gla_out", w_gla_out.shape),
               "w_out": ("out", w_out.shape), "w_mlp_up": ("up", w_mlp_up.shape), "w_mlp_down": ("down", w_mlp_down.shape)}
    result = [loss, grad_x.reshape(1, T, D)]
    for kind in range(4):
        small = unpack(small_res[kind])
        for nm in order:
            if nm in big_key:
                key, shp = big_key[nm]
                result.append(big_res[key][kind].reshape(shp))
            else:
                result.append(small[nm])
    return tuple(result)
```

```python
import itertools

import jax
import jax.numpy as jnp
from jax import lax
from jax.experimental import pallas as pl
from jax.experimental.pallas import tpu as pltpu

F32 = jnp.float32
BF16 = jnp.bfloat16
SDS = jax.ShapeDtypeStruct
MESH = pl.DeviceIdType.MESH
ANY = pl.BlockSpec(memory_space=pl.ANY)

T = 2048
D = 2048
DFF = 8192
NCHIP = 4
IN_WIDTH = 11280
IN_SHARD = IN_WIDTH // NCHIP
CHUNK = 64
NCHUNK = T // CHUNK
HEADS = 4
DK = 256
DV = 512
QK = HEADS * DK
EPS = 1e-6
POOL_WINDOWS = (2, 4, 8, 16)
PG = 256
PO = 512

OV, OG, OGP, OGG, OU, OQ, OKK, OA = 0, 2048, 4096, 6144, 8192, 9216, 10240, 11264
NCAT = 11520
APAD = 128

VMEM_CAP = 56 * 1024 * 1024

PIECE_BYTES = 384 * 1024

ADAM_LR, ADAM_B1, ADAM_B2, ADAM_EPS, ADAM_WD, ADAM_STEP = 0.001, 0.9, 0.999, 1e-08, 0.01, 10


def _cparams(vmem_bytes=None, sem=None):
    kw = {}
    if vmem_bytes is not None:
        kw["vmem_limit_bytes"] = int(min(max(vmem_bytes, 32 * 1024 * 1024), VMEM_CAP))
    if sem is not None:
        kw["dimension_semantics"] = sem
    return pltpu.CompilerParams(**kw)


def _nbytes(shape, dtype):
    n = 1
    for s in shape:
        if s is not None:
            n *= s
    return n * jnp.dtype(dtype).itemsize


def _sigmoid(x):
    return 1.0 / (1.0 + jnp.exp(-x))


def _as_list(after):
    if after is None:
        return []
    return list(after) if isinstance(after, (list, tuple)) else [after]


def matmul(name, a, b, *, a_spec, b_spec, cdims, grid, acc_shape, outs, extras=(), epi, after=None):
    nj, ni, nk = grid
    ne, no = len(extras), len(outs)
    afters = _as_list(after)
    first_out = 2 + ne + len(afters)

    def body(*refs):
        a_ref, b_ref = refs[0], refs[1]
        ex = refs[2:2 + ne]
        out_refs = refs[first_out:first_out + no]
        i = pl.program_id(1)
        part = lax.dot_general(a_ref[...], b_ref[...], (cdims, ((), ())), preferred_element_type=F32)
        if nk == 1:
            epi(part, ex, out_refs, i)
        else:
            acc_ref = refs[first_out + no]
            k = pl.program_id(2)

            @pl.when(k == 0)
            def _():
                acc_ref[...] = part

            @pl.when(k > 0)
            def _():
                acc_ref[...] += part

            @pl.when(k == nk - 1)
            def _():
                epi(acc_ref[...], ex, out_refs, i)

    in_specs = [pl.BlockSpec(*a_spec), pl.BlockSpec(*b_spec)] + [pl.BlockSpec(bs, im) for _, bs, im in extras]
    in_specs += [ANY] * len(afters)
    out_specs = [pl.BlockSpec(bs, im) for _, _, bs, im in outs]
    out_shape = [SDS(s, dt) for s, dt, _, _ in outs]
    vm = 2 * (_nbytes(a_spec[0], a.dtype) + _nbytes(b_spec[0], b.dtype))
    vm += 2 * sum(_nbytes(bs, arr.dtype) for arr, bs, _ in extras)
    vm += 2 * sum(_nbytes(bs, dt) for _, dt, bs, _ in outs)
    vm += 6 * _nbytes(acc_shape, F32)
    scratch = [pltpu.VMEM(acc_shape, F32)] if nk > 1 else []
    return pl.pallas_call(
        body, name=name, grid=grid, in_specs=in_specs, out_specs=out_specs, out_shape=out_shape,
        scratch_shapes=scratch,
        compiler_params=_cparams(vm, ("arbitrary", "arbitrary", "arbitrary")),
    )(a, b, *[arr for arr, _, _ in extras], *afters)


NN =((1,), (0,))
NT = ((1,), (1,))
TN = ((0,), (0,))


def _row_acc(out_ref, val, i):
    @pl.when(i == 0)
    def _():
        out_ref[...] = val

    @pl.when(i > 0)
    def _():
        out_ref[...] += val


def _rms_bwd(xn, r, dxn):
    return r * (dxn - xn * jnp.mean(dxn * xn, axis=-1, keepdims=True))


def norm1(x, g):
    tm = 256

    def body(x_ref, g_ref, h_ref):
        xv = x_ref[...]
        r = lax.rsqrt(jnp.mean(xv * xv, axis=-1, keepdims=True) + EPS)
        h_ref[...] = (xv * r * g_ref[...]).astype(BF16)

    return pl.pallas_call(
        body, name="norm1", grid=(T // tm,),
        in_specs=[pl.BlockSpec((tm, D), lambda i: (i, 0)), pl.BlockSpec((1, D), lambda i: (0, 0))],
        out_specs=pl.BlockSpec((tm, D), lambda i: (i, 0)), out_shape=SDS((T, D), BF16),
        compiler_params=_cparams(32 * 1024 * 1024, ("arbitrary",)),
    )(x, g)


def mm_in(h1, wcat):
    tm, tn = 1024, 1280

    def epi(acc, ex, outs, i):
        outs[0][...] = acc.astype(BF16)

    return matmul("mm_in", h1, wcat, a_spec=((tm, D), lambda j, i, k: (i, 0)), b_spec=((D, tn), lambda j, i, k: (0, j)),
                  cdims=NN, grid=(NCAT // tn, T // tm, 1), acc_shape=(tm, tn),
                  outs=[((T, NCAT), BF16, (tm, tn), lambda j, i, k: (i, j))], epi=epi)[0]


def _window_sum(x, w, up):
    n = x.shape[0]
    row = lax.broadcasted_iota(jnp.int32, x.shape, 0)
    s, sh = x, 1
    while sh < w:
        if up:
            s = s + jnp.where(row < n - sh, pltpu.roll(s, n - sh, axis=0), 0.0)
        else:
            s = s + jnp.where(row >= sh, pltpu.roll(s, sh, axis=0), 0.0)
        sh *= 2
    return s


def _inv_count(shape, w):
    row = lax.broadcasted_iota(jnp.int32, shape, 0)
    return 1.0 / jnp.minimum(row + 1, w).astype(F32)


def pool_fwd(pcat, pw):
    def body(u_ref, pw_ref, d_ref, y_ref):
        for gi, w in enumerate(POOL_WINDOWS):
            ug = u_ref[:, gi * PG:(gi + 1) * PG].astype(F32)
            dg = _window_sum(ug, w, False) * _inv_count(ug.shape, w) - ug
            db = dg.astype(BF16)
            d_ref[:, gi * PG:(gi + 1) * PG] = db
            y_ref[:, gi * PO:(gi + 1) * PO] = jnp.dot(db, pw_ref[gi], preferred_element_type=F32).astype(BF16)

    return pl.pallas_call(
        body, name="pool_fwd", grid=(1,),
        in_specs=[pl.BlockSpec((T, 4 * PG), lambda i: (0, OU // (4 * PG))), pl.BlockSpec((4, PG, PO), lambda i: (0, 0, 0))],
        out_specs=[pl.BlockSpec((T, 4 * PG), lambda i: (0, 0)), pl.BlockSpec((T, D), lambda i: (0, 0))],
        out_shape=[SDS((T, 4 * PG), BF16), SDS((T, D), BF16)],
        compiler_params=_cparams(48 * 1024 * 1024, ("arbitrary",)),
    )(pcat, pw)


def pool_bwd(dylin, d, pw):
    def body(dy_ref, d_ref, pw_ref, du_ref, dpw_ref):
        for gi, w in enumerate(POOL_WINDOWS):
            dyl = dy_ref[:, gi * PO:(gi + 1) * PO]
            dd = lax.dot_general(dyl, pw_ref[gi], (NT, ((), ())), preferred_element_type=F32)
            du = _window_sum(dd * _inv_count(dd.shape, w), w, True) - dd
            du_ref[:, gi * PG:(gi + 1) * PG] = du.astype(BF16)
            dpw_ref[gi] = lax.dot_general(d_ref[:, gi * PG:(gi + 1) * PG], dyl, (TN, ((), ())),
                                          preferred_element_type=F32).astype(BF16)

    return pl.pallas_call(
        body, name="pool_bwd", grid=(1,),
        in_specs=[pl.BlockSpec((T, D), lambda i: (0, 0)), pl.BlockSpec((T, 4 * PG), lambda i: (0, 0)),
                  pl.BlockSpec((4, PG, PO), lambda i: (0, 0, 0))],
        out_specs=[pl.BlockSpec((T, 4 * PG), lambda i: (0, 0)), pl.BlockSpec((4, PG, PO), lambda i: (0, 0, 0))],
        out_shape=[SDS((T, 4 * PG), BF16), SDS((4, PG, PO), BF16)],
        compiler_params=_cparams(48 * 1024 * 1024, ("arbitrary",)),
    )(dylin, d, pw)


def _gate_decay(alow, wa, ba):
    a = jnp.dot(alow, wa, preferred_element_type=F32) + ba
    ls = jax.nn.log_sigmoid(a) * (1.0 / 16.0)
    r = lax.broadcasted_iota(jnp.int32, (CHUNK, CHUNK), 0)
    c = lax.broadcasted_iota(jnp.int32, (CHUNK, CHUNK), 1)
    tri = jnp.where(c <= r, 1.0, 0.0).astype(F32)
    cum = jnp.dot(tri, ls, preferred_element_type=F32, precision=lax.Precision.HIGHEST)
    last = cum[CHUNK - 1:CHUNK, :]
    return a, jnp.exp(last - cum), jnp.exp(last)


def gla_fwd(pcat, wa, ba, ng):
    def body(q_ref, k_ref, v_ref, g_ref, al_ref, wa_ref, ba_ref, ng_ref, og_ref, o_ref, st_ref, s_scr):
        @pl.when(pl.program_id(0) == 0)
        def _():
            s_scr[...] = jnp.zeros_like(s_scr)

        _, e, decay = _gate_decay(al_ref[...], wa_ref[...], ba_ref[...])
        kd = (k_ref[...].astype(F32) * e).astype(BF16)
        qs = (q_ref[...].astype(F32) * (DK ** -0.5)).astype(BF16)
        for h in range(HEADS):
            ck = slice(h * DK, (h + 1) * DK)
            cv = slice(h * DV, (h + 1) * DV)
            s_new = s_scr[h] * decay[:, ck] + lax.dot_general(v_ref[:, cv], kd[:, ck], (TN, ((), ())),
                                                               preferred_element_type=F32)
            s_scr[h] = s_new
            sb = s_new.astype(BF16)
            st_ref[h] = sb
            oh = lax.dot_general(qs[:, ck], sb, (NT, ((), ())), preferred_element_type=F32)
            o_ref[:, cv] = oh.astype(BF16)
            on = oh * lax.rsqrt(jnp.mean(oh * oh, axis=-1, keepdims=True) + EPS) * ng_ref[:, cv]
            gv = g_ref[:, cv].astype(F32)
            og_ref[:, cv] = (on * (gv * _sigmoid(gv))).astype(BF16)

    row = lambda c: (c, 0)
    return pl.pallas_call(
        body, name="gla_fwd", grid=(NCHUNK,),
        in_specs=[pl.BlockSpec((CHUNK, QK), lambda c: (c, OQ // QK)), pl.BlockSpec((CHUNK, QK), lambda c: (c, OKK // QK)),
                  pl.BlockSpec((CHUNK, D), lambda c: (c, OV // D)), pl.BlockSpec((CHUNK, D), lambda c: (c, OG // D)),
                  pl.BlockSpec((CHUNK, APAD), lambda c: (c, OA // APAD)),
                  pl.BlockSpec((APAD, QK), lambda c: (0, 0)), pl.BlockSpec((1, QK), lambda c: (0, 0)),
                  pl.BlockSpec((1, D), lambda c: (0, 0))],
        out_specs=[pl.BlockSpec((CHUNK, D), row), pl.BlockSpec((CHUNK, D), row),
                   pl.BlockSpec((None, HEADS, DV, DK), lambda c: (c, 0, 0, 0))],
        out_shape=[SDS((T, D), BF16), SDS((T, D), BF16), SDS((NCHUNK, HEADS, DV, DK), BF16)],
        scratch_shapes=[pltpu.VMEM((HEADS, DV, DK), F32)],
        compiler_params=_cparams(32 * 1024 * 1024, ("arbitrary",)),
    )(pcat, pcat, pcat, pcat, pcat, wa, ba, ng)


def gla_bwd(do, pcat, states, wa, ba, after):
    def body(do_ref, q_ref, k_ref, v_ref, al_ref, sc_ref, sp_ref, wa_ref, ba_ref, after_ref,
             dq_ref, dk_ref, dv_ref, dal_ref, dwa_ref, dba_ref, ds_scr):
        i = pl.program_id(0)

        @pl.when(i == 0)
        def _():
            ds_scr[...] = jnp.zeros_like(ds_scr)

        has_prev = jnp.where(i < NCHUNK - 1, 1.0, 0.0).astype(F32)
        a, e, decay = _gate_decay(al_ref[...], wa_ref[...], ba_ref[...])
        kf = k_ref[...].astype(F32)
        kdf = kf * e
        kd = kdf.astype(BF16)
        qs = (q_ref[...].astype(F32) * (DK ** -0.5)).astype(BF16)
        dkd_parts, ddecay_parts = [], []
        for h in range(HEADS):
            ck = slice(h * DK, (h + 1) * DK)
            cv = slice(h * DV, (h + 1) * DV)
            doh = do_ref[:, cv]
            ds = ds_scr[h] + lax.dot_general(doh, qs[:, ck], (TN, ((), ())), preferred_element_type=F32)
            dsb = ds.astype(BF16)
            dq_ref[:, ck] = (jnp.dot(doh, sc_ref[h], preferred_element_type=F32) * (DK ** -0.5)).astype(BF16)
            dkd_parts.append(jnp.dot(v_ref[:, cv], dsb, preferred_element_type=F32))
            dv_ref[:, cv] = lax.dot_general(kd[:, ck], dsb, (NT, ((), ())), preferred_element_type=F32).astype(BF16)
            ddecay_parts.append(jnp.sum(ds * sp_ref[h].astype(F32), axis=0, keepdims=True) * has_prev)
            ds_scr[h] = ds * decay[:, ck]
        dkd = jnp.concatenate(dkd_parts, axis=1)
        ddecay = jnp.concatenate(ddecay_parts, axis=1)
        dk_ref[...] = (dkd * e).astype(BF16)
        dearg = dkd * kdf
        dlast = jnp.sum(dearg, axis=0, keepdims=True) + ddecay * decay
        r = lax.broadcasted_iota(jnp.int32, (CHUNK, CHUNK), 0)
        c = lax.broadcasted_iota(jnp.int32, (CHUNK, CHUNK), 1)
        triu = jnp.where(c >= r, 1.0, 0.0).astype(F32)
        dls = dlast - jnp.dot(triu, dearg, preferred_element_type=F32, precision=lax.Precision.HIGHEST)
        da = dls * (1.0 / 16.0) * (1.0 - _sigmoid(a))
        dab = da.astype(BF16)
        dal_ref[...] = lax.dot_general(dab, wa_ref[...], (NT, ((), ())), preferred_element_type=F32).astype(BF16)
        dwa = lax.dot_general(al_ref[...], dab, (TN, ((), ())), preferred_element_type=F32)
        dba = jnp.sum(da, axis=0, keepdims=True)

        @pl.when(i == 0)
        def _():
            dwa_ref[...] = dwa
            dba_ref[...] = dba

        @pl.when(i > 0)
        def _():
            dwa_ref[...] += dwa
            dba_ref[...] += dba

    rev = lambda i: NCHUNK - 1 - i
    return pl.pallas_call(
        body, name="gla_bwd", grid=(NCHUNK,),
        in_specs=[pl.BlockSpec((CHUNK, D), lambda i: (rev(i), 0)),
                  pl.BlockSpec((CHUNK, QK), lambda i: (rev(i), OQ // QK)), pl.BlockSpec((CHUNK, QK), lambda i: (rev(i), OKK // QK)),
                  pl.BlockSpec((CHUNK, D), lambda i: (rev(i), OV // D)), pl.BlockSpec((CHUNK, APAD), lambda i: (rev(i), OA // APAD)),
                  pl.BlockSpec((None, HEADS, DV, DK), lambda i: (rev(i), 0, 0, 0)),
                  pl.BlockSpec((None, HEADS, DV, DK), lambda i: (jnp.maximum(rev(i) - 1, 0), 0, 0, 0)),
                  pl.BlockSpec((APAD, QK), lambda i: (0, 0)), pl.BlockSpec((1, QK), lambda i: (0, 0)), ANY],
        out_specs=[pl.BlockSpec((CHUNK, QK), lambda i: (rev(i), 0)), pl.BlockSpec((CHUNK, QK), lambda i: (rev(i), 0)),
                   pl.BlockSpec((CHUNK, D), lambda i: (rev(i), 0)), pl.BlockSpec((CHUNK, APAD), lambda i: (rev(i), 0)),
                   pl.BlockSpec((APAD, QK), lambda i: (0, 0)), pl.BlockSpec((1, QK), lambda i: (0, 0))],
        out_shape=[SDS((T, QK), BF16), SDS((T, QK), BF16), SDS((T, D), BF16), SDS((T, APAD), BF16),
                   SDS((APAD, QK), F32), SDS((1, QK), F32)],
        scratch_shapes=[pltpu.VMEM((HEADS, DV, DK), F32)],
        compiler_params=_cparams(32 * 1024 * 1024, ("arbitrary",)),
    )(do, pcat, pcat, pcat, pcat, states, states, wa, ba, after)


TMF = 256
TMW = 512
_rowblk = ((TMF, D), lambda j, i, k: (i, 0))
_vec = ((1, D), lambda j, i, k: (0, 0))


def _full_spec(col):
    return ((TMF, D), lambda j, i, k: (i, col))


TBIG = 1024


def square_matmul(name, a, b, *, a_spec, b_spec, cdims, nk, after=None):
    def epi(acc, ex, outs, i):
        outs[0][...] = acc

    return matmul(name, a, b, a_spec=a_spec, b_spec=b_spec, cdims=cdims, grid=(D // TBIG, T // TBIG, nk),
                  acc_shape=(TBIG, TBIG), outs=[((T, D), F32, (TBIG, TBIG), lambda j, i, k: (i, j))], epi=epi,
                  after=after)[0]


def rowwise(name, y, *, extras, outs, epi):
    ne = len(extras)

    def body(*refs):
        epi(refs[0][...], refs[1:1 + ne], refs[1 + ne:], pl.program_id(1))

    in_specs = [pl.BlockSpec(*_rowblk)] + [pl.BlockSpec(bs, im) for _, bs, im in extras]
    return pl.pallas_call(
        body, name=name, grid=(1, T // TMF, 1), in_specs=in_specs,
        out_specs=[pl.BlockSpec(bs, im) for _, _, bs, im in outs], out_shape=[SDS(s, dt) for s, dt, _, _ in outs],
        compiler_params=_cparams(40 * 1024 * 1024, ("arbitrary", "arbitrary", "arbitrary")),
    )(y, *[arr for arr, _, _ in extras])


def mm_gla_out(og, w, ylin, pcat, pscale):
    def epi(acc, ex, outs, i):
        ylin_ref, lgp_ref, lgg_ref, ps_ref = ex
        gp = _sigmoid(lgp_ref[...].astype(F32))
        gg = _sigmoid(lgg_ref[...].astype(F32))
        outs[0][...] = (gp * (ylin_ref[...].astype(F32) * ps_ref[...]) + gg * acc).astype(BF16)
        outs[1][...] = acc.astype(BF16)

    return matmul("mm_gla_out", og, w, a_spec=_rowblk, b_spec=((D, D), lambda j, i, k: (0, 0)), cdims=NN,
                  grid=(1, T // TMF, 1), acc_shape=(TMF, D),
                  extras=[(ylin, *_rowblk), (pcat, *_full_spec(OGP // D)), (pcat, *_full_spec(OGG // D)), (pscale, *_vec)],
                  outs=[((T, D), BF16, *_rowblk), ((T, D), BF16, *_rowblk)], epi=epi)


def mm_out(mixed, w, x, g2):
    def epi(acc, ex, outs, i):
        x_ref, g_ref = ex
        x2 = x_ref[...] + acc
        r = lax.rsqrt(jnp.mean(x2 * x2, axis=-1, keepdims=True) + EPS)
        outs[0][...] = x2
        outs[1][...] = (x2 * r * g_ref[...]).astype(BF16)

    return matmul("mm_out", mixed, w, a_spec=_rowblk, b_spec=((D, D), lambda j, i, k: (0, 0)), cdims=NN,
                  grid=(1, T // TMF, 1), acc_shape=(TMF, D), extras=[(x, *_rowblk), (g2, *_vec)],
                  outs=[((T, D), F32, *_rowblk), ((T, D), BF16, *_rowblk)], epi=epi)


def mm_up(h2, wup):
    def epi(acc, ex, outs, i):
        r = jnp.maximum(acc, 0.0)
        outs[0][...] = r.astype(BF16)
        outs[1][...] = (r * r).astype(BF16)

    blk = ((TMW, D), lambda j, i, k: (i, j))
    return matmul("mm_up", h2, wup, a_spec=((TMW, D), lambda j, i, k: (i, 0)), b_spec=((None, D, D), lambda j, i, k: (j, 0, 0)),
                  cdims=NN, grid=(NCHIP, T // TMW, 1), acc_shape=(TMW, D),
                  outs=[((T, DFF), BF16, *blk), ((T, DFF), BF16, *blk)], epi=epi)


def mm_down(act, wdown, x2, tgt, gf):
    tk = 2048

    def epi(acc, ex, outs, i):
        x2_ref, t_ref, g_ref = ex
        dx_ref, dxb_ref, gnf_ref, loss_ref = outs
        x3 = x2_ref[...] + acc
        r = lax.rsqrt(jnp.mean(x3 * x3, axis=-1, keepdims=True) + EPS)
        xn = x3 * r
        err = xn * g_ref[...] - t_ref[...]
        lsum = 0.5 * jnp.sum(jnp.mean(err * err, axis=-1, keepdims=True), axis=0, keepdims=True)
        dy = err * (1.0 / D)
        _row_acc(gnf_ref, jnp.sum(dy * xn, axis=0, keepdims=True), i)
        _row_acc(loss_ref, jnp.broadcast_to(lsum, (1, 128)), i)
        dx3 = _rms_bwd(xn, r, dy * g_ref[...])
        dx_ref[...] = dx3
        dxb_ref[...] = dx3.astype(BF16)

    y = square_matmul("mm_down", act, wdown, a_spec=((TBIG, tk), lambda j, i, k: (i, k)),
                      b_spec=((tk, TBIG), lambda j, i, k: (k, j)), cdims=NN, nk=DFF // tk)
    return rowwise("rows_final", y, extras=[(x2, *_rowblk), (tgt, *_rowblk), (gf, *_vec)],
                   outs=[((T, D), F32, *_rowblk), ((T, D), BF16, *_rowblk), ((1, D), F32, *_vec),
                         ((1, 128), F32, (1, 128), lambda j, i, k: (0, 0))], epi=epi)


def mm_dact(dx3b, wdown, rup, after=None):
    def epi(acc, ex, outs, i):
        outs[0][...] = (acc * 2.0 * ex[0][...].astype(F32)).astype(BF16)

    blk = ((TMW, D), lambda j, i, k: (i, j))
    return matmul("mm_dact", dx3b, wdown, a_spec=((TMW, D), lambda j, i, k: (i, 0)), b_spec=((D, D), lambda j, i, k: (j, 0)),
                  cdims=NT, grid=(DFF // D, T // TMW, 1), acc_shape=(TMW, D), extras=[(rup, *blk)],
                  outs=[((T, DFF), BF16, *blk)], epi=epi, after=after)[0]


def mm_wgrad(name, a, b, m, n, out_shape, out_block, out_map, tm, tn, after=None):
    def epi(acc, ex, outs, i):
        outs[0][...] = acc.astype(BF16)

    return matmul(name, a, b, a_spec=((T, tm), lambda j, i, k: (0, i)), b_spec=((T, tn), lambda j, i, k: (0, j)),
                  cdims=TN, grid=(n // tn, m // tm, 1), acc_shape=(tm, tn),
                  outs=[(out_shape, BF16, out_block, out_map)], epi=epi, after=after)[0]


def mm_dh2(dup, wup, x2, dx3, g2, after=None):
    def epi(acc, ex, outs, i):
        x2_ref, dx3_ref, g_ref = ex
        x2 = x2_ref[...]
        r = lax.rsqrt(jnp.mean(x2 * x2, axis=-1, keepdims=True) + EPS)
        xn = x2 * r
        _row_acc(outs[2], jnp.sum(acc * xn, axis=0, keepdims=True), i)
        dx2 = dx3_ref[...] + _rms_bwd(xn, r, acc * g_ref[...])
        outs[0][...] = dx2
        outs[1][...] = dx2.astype(BF16)

    y = square_matmul("mm_dh2", dup, wup, a_spec=((TBIG, D), lambda j, i, k: (i, k)),
                      b_spec=((None, TBIG, D), lambda j, i, k: (k, j, 0)), cdims=NT, nk=NCHIP, after=after)
    return rowwise("rows_dh2", y, extras=[(x2, *_rowblk), (dx3, *_rowblk), (g2, *_vec)],
                   outs=[((T, D), F32, *_rowblk), ((T, D), BF16, *_rowblk), ((1, D), F32, *_vec)], epi=epi)


def mm_dmixed(dx2b, wout, pcat, ylin, ygla, pscale, after=None):
    def epi(acc, ex, outs, i):
        lgp_ref, lgg_ref, ylin_ref, ygla_ref, ps_ref = ex
        gp = _sigmoid(lgp_ref[...].astype(F32))
        gg = _sigmoid(lgg_ref[...].astype(F32))
        yl = ylin_ref[...].astype(F32)
        ps = ps_ref[...]
        agp = acc * gp
        outs[0][...] = (agp * ps).astype(BF16)
        outs[1][...] = (acc * gg).astype(BF16)
        outs[2][...] = (agp * (yl * ps) * (1.0 - gp)).astype(BF16)
        outs[3][...] = (acc * ygla_ref[...].astype(F32) * gg * (1.0 - gg)).astype(BF16)
        _row_acc(outs[4], jnp.sum(agp * yl, axis=0, keepdims=True), i)

    return matmul("mm_dmixed", dx2b, wout, a_spec=_rowblk, b_spec=((D, D), lambda j, i, k: (0, 0)), cdims=NT,
                  grid=(1, T // TMF, 1), acc_shape=(TMF, D),
                  extras=[(pcat, *_full_spec(OGP // D)), (pcat, *_full_spec(OGG // D)), (ylin, *_rowblk), (ygla, *_rowblk),
                          (pscale, *_vec)],
                  outs=[((T, D), BF16, *_rowblk)] * 4 + [((1, D), F32, *_vec)], epi=epi, after=after)


def mm_dog(dygla, wgo, o, pcat, ng, after=None):
    def epi(acc, ex, outs, i):
        o_ref, g_ref, ng_ref = ex
        do_ref, dg_ref, gng_ref = outs
        gparts = []
        for h in range(HEADS):
            cv = slice(h * DV, (h + 1) * DV)
            oh = o_ref[:, cv].astype(F32)
            r = lax.rsqrt(jnp.mean(oh * oh, axis=-1, keepdims=True) + EPS)
            on = oh * r
            gv = g_ref[:, cv].astype(F32)
            sg = _sigmoid(gv)
            dgain = acc[:, cv] * (gv * sg)
            gparts.append(jnp.sum(dgain * on, axis=0, keepdims=True))
            ngh = ng_ref[:, cv]
            do_ref[:, cv] = _rms_bwd(on, r, dgain * ngh).astype(BF16)
            dg_ref[:, cv] = (acc[:, cv] * (on * ngh) * (sg * (1.0 + gv * (1.0 - sg)))).astype(BF16)
        _row_acc(gng_ref, jnp.concatenate(gparts, axis=1), i)

    return matmul("mm_dog", dygla, wgo, a_spec=_rowblk, b_spec=((D, D), lambda j, i, k: (0, 0)), cdims=NT,
                  grid=(1, T // TMF, 1), acc_shape=(TMF, D),
                  extras=[(o, *_rowblk), (pcat, *_full_spec(OG // D)), (ng, *_vec)],
                  outs=[((T, D), BF16, *_rowblk), ((T, D), BF16, *_rowblk), ((1, D), F32, *_vec)], epi=epi, after=after)


def mm_dh1(dpcat, wcat, x, dx2, g1, after=None):
    tk = 2304

    def epi(acc, ex, outs, i):
        x_ref, dx2_ref, g_ref = ex
        xv = x_ref[...]
        r = lax.rsqrt(jnp.mean(xv * xv, axis=-1, keepdims=True) + EPS)
        xn = xv * r
        _row_acc(outs[1], jnp.sum(acc * xn, axis=0, keepdims=True), i)
        outs[0][...] = dx2_ref[...] + _rms_bwd(xn, r, acc * g_ref[...])

    y = square_matmul("mm_dh1", dpcat, wcat, a_spec=((TBIG, tk), lambda j, i, k: (i, k)),
                      b_spec=((TBIG, tk), lambda j, i, k: (j, k)), cdims=NT, nk=NCAT // tk, after=after)
    return rowwise("rows_dh1", y, extras=[(x, *_rowblk), (dx2, *_rowblk), (g1, *_vec)],
                   outs=[((T, D), F32, *_rowblk), ((1, D), F32, *_vec)], epi=epi)


def _tile_rows(rows, cols, n_arrays):
    tm = rows
    while tm % 32 == 0 and 2 * n_arrays * tm * cols * 4 > 24 * 1024 * 1024:
        tm //= 2
    return tm


def add_pairs(name, parts, theirs, core):
    _, _, r, c = parts.shape
    tm = _tile_rows(r, c, 3)

    def body(core_ref, a_ref, b_ref, o_ref):
        o_ref[...] = (a_ref[...].astype(F32) + b_ref[...].astype(F32)).astype(BF16)

    spec = pl.BlockSpec((None, tm, c), lambda j, i, core_ref: (j, i, 0))
    grid_spec = pltpu.PrefetchScalarGridSpec(
        num_scalar_prefetch=1, grid=(NCHIP, r // tm),
        in_specs=[pl.BlockSpec((None, None, tm, c), lambda j, i, core_ref: (core_ref[0], j, i, 0)), spec], out_specs=spec)
    return pl.pallas_call(body, name=name, grid_spec=grid_spec, out_shape=SDS((NCHIP, r, c), BF16),
                          compiler_params=_cparams(40 * 1024 * 1024, ("arbitrary", "arbitrary")))(core, parts, theirs)


def sum_chips(name, sums, landed, chip):
    _, r, c = sums.shape
    tm = _tile_rows(r, c, 4)

    def body(chip_ref, own_ref, l_ref, o_ref):
        s = own_ref[...].astype(F32)
        for t in range(NCHIP - 1):
            s = s + l_ref[t].astype(F32)
        o_ref[...] = s

    grid_spec = pltpu.PrefetchScalarGridSpec(
        num_scalar_prefetch=1, grid=(r // tm,),
        in_specs=[pl.BlockSpec((None, tm, c), lambda i, chip_ref: (chip_ref[0], i, 0)),
                  pl.BlockSpec((NCHIP - 1, tm, c), lambda i, chip_ref: (0, i, 0))],
        out_specs=pl.BlockSpec((tm, c), lambda i, chip_ref: (i, 0)))
    return pl.pallas_call(body, name=name, grid_spec=grid_spec, out_shape=SDS((r, c), F32),
                          compiler_params=_cparams(40 * 1024 * 1024, ("arbitrary",)))(chip, sums, landed)


def _adamw_math(wv, gv, mv, vv):
    mn = ADAM_B1 * mv + (1.0 - ADAM_B1) * gv
    vn = ADAM_B2 * vv + (1.0 - ADAM_B2) * (gv * gv)
    mh = mn / (1.0 - ADAM_B1 ** ADAM_STEP)
    vh = vn / (1.0 - ADAM_B2 ** ADAM_STEP)
    return -ADAM_LR * (mh / (jnp.sqrt(vh) + ADAM_EPS) + ADAM_WD * wv), mn, vn


def adamw(name, w, g, m, v):
    def body(w_ref, g_ref, m_ref, v_ref, go_ref, d_ref, mo_ref, vo_ref):
        gv = g_ref[...]
        go_ref[...] = gv
        d_ref[...], mo_ref[...], vo_ref[...] = _adamw_math(w_ref[...], gv, m_ref[...], v_ref[...])

    return pl.pallas_call(body, name=name, out_shape=[SDS(w.shape, F32)] * 4)(w, g, m, v)


def adamw_halves(name, w, g_own, g_sib, m, v, core):
    _, r, c = w.shape
    tm = _tile_rows(r, c, 10)

    def body(core_ref, w_ref, go_ref, gs_ref, m_ref, v_ref, g_out, d_out, m_out, v_out):
        gv = jnp.where(pl.program_id(0) == core_ref[0], go_ref[...], gs_ref[...])
        g_out[...] = gv
        d_out[...], m_out[...], v_out[...] = _adamw_math(w_ref[...], gv, m_ref[...], v_ref[...])

    full = pl.BlockSpec((None, tm, c), lambda h, i, core_ref: (h, i, 0))
    own = pl.BlockSpec((tm, c), lambda h, i, core_ref: (jnp.where(h == core_ref[0], i, 0), 0))
    sib = pl.BlockSpec((tm, c), lambda h, i, core_ref: (jnp.where(h == core_ref[0], 0, i), 0))
    grid_spec = pltpu.PrefetchScalarGridSpec(num_scalar_prefetch=1, grid=(2, r // tm),
                                             in_specs=[full, own, sib, full, full], out_specs=[full] * 4)
    return pl.pallas_call(body, name=name, grid_spec=grid_spec, out_shape=[SDS(w.shape, F32)] * 4,
                          compiler_params=_cparams(48 * 1024 * 1024, ("arbitrary", "arbitrary")))(core, w, g_own, g_sib, m, v)


def pack_rows(name, parts, rows):
    width = parts[0].shape[1]
    n = len(parts)

    def body(*refs):
        out_ref = refs[n]
        out_ref[...] = jnp.zeros_like(out_ref)
        off = 0
        for p in refs[:n]:
            out_ref[off:off + p.shape[0], :] = p[...]
            off += p.shape[0]

    return pl.pallas_call(body, name=name, out_shape=SDS((rows, width), F32))(*parts)


def _place():
    x, y, c = lax.axis_index("x"), lax.axis_index("y"), lax.axis_index("c")
    chips = [(1 - x, y), (x, 1 - y), (1 - x, 1 - y)]
    return x, y, c, chips


def _row_split(shape, dtype):
    r, c = shape
    n = 1
    while r % (2 * n) == 0 and (r // (2 * n)) % 16 == 0 and (r // n) * c * jnp.dtype(dtype).itemsize > PIECE_BYTES:
        n *= 2
    return [pl.ds(s * (r // n), r // n) for s in range(n)]


def _pieces(ref):
    *lead, r, c = ref.shape
    split = _row_split((r, c), ref.dtype)
    return [ref.at[(*idx, s)] for idx in itertools.product(*[range(d) for d in lead]) for s in split]


HBM = pl.BlockSpec(memory_space=pltpu.HBM)
SEM = pl.BlockSpec(memory_space=pltpu.SEMAPHORE)
EFFECT = pltpu.SideEffectType.DATAFLOW_SIDE_EFFECTING


def gather_start(name, shards, after=None):
    n = len(shards)
    extra = [] if after is None else [after]

    def body(*refs):
        src, land = refs[:n], refs[n:2 * n]
        send, recv = refs[2 * n + len(extra)], refs[2 * n + len(extra) + 1]
        x, y, c, chips = _place()
        me = 2 * x + y
        for a in range(n):
            for j, (cx, cy) in enumerate(chips):
                for sp, dp in zip(_pieces(src[a].at[c]), _pieces(land[a].at[me, c])):
                    pltpu.make_async_remote_copy(sp, dp, send.at[3 * a + j], recv.at[3 * a + j],
                                                 device_id=(cx, cy, c), device_id_type=MESH).start()

    lands = [pltpu.with_memory_space_constraint(lax.empty((NCHIP,) + s.shape, s.dtype), pltpu.HBM) for s in shards]
    srcs = [pltpu.with_memory_space_constraint(s, pltpu.HBM) for s in shards]
    outs = pl.pallas_call(
        body, name=name,
        out_shape=(pltpu.SemaphoreType.DMA((3 * n,)), pltpu.SemaphoreType.DMA((3 * n,)),
                   *[pltpu.HBM(s.shape, s.dtype) for s in shards], *[pltpu.HBM(l.shape, l.dtype) for l in lands]),
        in_specs=[HBM] * (2 * n) + [ANY] * len(extra), out_specs=(SEM, SEM, *([HBM] * (2 * n))),
        input_output_aliases={i: 2 + i for i in range(2 * n)},
        compiler_params=pltpu.CompilerParams(has_side_effects=EFFECT),
    )(*srcs, *lands, *extra)
    return outs[0], outs[1], list(outs[2:2 + n]), list(outs[2 + n:2 + 2 * n])


def gather_wait(name, send, recv, shards, lands, after):
    n = len(shards)
    afters = _as_list(after)

    def body(*refs):
        src, land = refs[:n], refs[n:2 * n]
        send_ref, recv_ref = refs[2 * n], refs[2 * n + 1]
        x, y, c, chips = _place()
        for a in range(n):
            for j, (cx, cy) in enumerate(chips):
                cp = pltpu.make_async_remote_copy(src[a].at[c], land[a].at[2 * cx + cy, c], send_ref.at[3 * a + j],
                                                  recv_ref.at[3 * a + j], device_id=(cx, cy, c), device_id_type=MESH)
                cp.wait_send()
                cp.wait_recv()

    outs = pl.pallas_call(
        body, name=name,
        out_shape=(*[pltpu.HBM(s.shape, s.dtype) for s in shards], *[pltpu.HBM(l.shape, l.dtype) for l in lands]),
        in_specs=[HBM] * (2 * n) + [SEM, SEM] + [ANY] * len(afters), out_specs=[HBM] * (2 * n),
        input_output_aliases={i: i for i in range(2 * n)},
        compiler_params=pltpu.CompilerParams(has_side_effects=EFFECT),
    )(*shards, *lands, send, recv, *afters)
    return list(outs[:n]), list(outs[n:])


def forward_halves(name, shards, lands):
    n = len(lands)

    def body(*refs):
        had, buf = refs[:n], refs[n:2 * n]
        send, recv = refs[2 * n:]
        x, y, c, chips = _place()
        sib = (x, y, 1 - c)
        for a in range(n):
            for j, (cx, cy) in enumerate(chips):
                for sp, dp in zip(_pieces(had[a].at[2 * cx + cy, c]), _pieces(buf[a].at[2 * cx + cy, c])):
                    pltpu.make_async_remote_copy(sp, dp, send.at[3 * a + j], recv.at[3 * a + j], device_id=sib, device_id_type=MESH).start()
        for a in range(n):
            for j, (cx, cy) in enumerate(chips):
                pltpu.make_async_remote_copy(had[a].at[2 * cx + cy, c], buf[a].at[2 * cx + cy, 1 - c], send.at[3 * a + j],
                                             recv.at[3 * a + j], device_id=sib, device_id_type=MESH).wait()

    got = pl.pallas_call(
        body, name=name, in_specs=[ANY] * n, out_specs=[ANY] * n, out_shape=[SDS(l.shape, l.dtype) for l in lands],
        input_output_aliases={i: i for i in range(n)},
        scratch_shapes=[pltpu.SemaphoreType.DMA((3 * n,)), pltpu.SemaphoreType.DMA((3 * n,))],
    )(*lands)
    me = 2 * lax.axis_index("x") + lax.axis_index("y")
    return [lax.dynamic_update_index_in_dim(g, s, me, 0) for g, s in zip(got, shards)]


def exchange_start(name, parts):
    n = len(parts)

    def body(*refs):
        src, got = refs[:n], refs[n:2 * n]
        send, recv = refs[2 * n], refs[2 * n + 1]
        token = refs[4 * n + 2]
        x, y, c, _ = _place()
        sib = (x, y, 1 - c)
        for a in range(n):
            for sp, dp in zip(_pieces(src[a].at[1 - c]), _pieces(got[a])):
                pltpu.make_async_remote_copy(sp, dp, send.at[a], recv.at[a], device_id=sib, device_id_type=MESH).start()
        token[...] = jnp.zeros_like(token)

    lands = [pltpu.with_memory_space_constraint(lax.empty(p.shape[1:], p.dtype), pltpu.HBM) for p in parts]
    srcs = [pltpu.with_memory_space_constraint(p, pltpu.HBM) for p in parts]
    outs = pl.pallas_call(
        body, name=name,
        out_shape=(pltpu.SemaphoreType.DMA((n,)), pltpu.SemaphoreType.DMA((n,)),
                   *[pltpu.HBM(p.shape, p.dtype) for p in parts], *[pltpu.HBM(l.shape, l.dtype) for l in lands],
                   SDS((8, 128), F32)),
        in_specs=[HBM] * (2 * n), out_specs=(SEM, SEM, *([HBM] * (2 * n)), pl.BlockSpec(memory_space=pltpu.VMEM)),
        input_output_aliases={i: 2 + i for i in range(2 * n)},
        compiler_params=pltpu.CompilerParams(has_side_effects=EFFECT),
    )(*srcs, *lands)
    return outs[0], outs[1], list(outs[2:2 + n]), list(outs[2 + n:2 + 2 * n]), outs[2 + 2 * n]


def exchange_wait(name, send, recv, parts, lands, after):
    n = len(parts)
    afters = _as_list(after)

    def body(*refs):
        src, got = refs[:n], refs[n:2 * n]
        send_ref, recv_ref = refs[2 * n], refs[2 * n + 1]
        x, y, c, _ = _place()
        sib = (x, y, 1 - c)
        for a in range(n):
            cp = pltpu.make_async_remote_copy(src[a].at[1 - c], got[a], send_ref.at[a], recv_ref.at[a], device_id=sib, device_id_type=MESH)
            cp.wait_send()
            cp.wait_recv()

    outs = pl.pallas_call(
        body, name=name,
        out_shape=(*[pltpu.HBM(p.shape, p.dtype) for p in parts], *[pltpu.HBM(l.shape, l.dtype) for l in lands]),
        in_specs=[HBM] * (2 * n) + [SEM, SEM] + [ANY] * len(afters), out_specs=[HBM] * (2 * n),
        input_output_aliases={i: i for i in range(2 * n)},
        compiler_params=pltpu.CompilerParams(has_side_effects=EFFECT),
    )(*parts, *lands, send, recv, *afters)
    return list(outs[:n]), list(outs[n:])


def scatter_start(name, parts):
    n = len(parts)

    def body(*refs):
        src, land = refs[:n], refs[n:2 * n]
        send, recv = refs[2 * n], refs[2 * n + 1]
        token = refs[4 * n + 2]
        x, y, c, chips = _place()
        for a in range(n):
            for j, (cx, cy) in enumerate(chips):
                for sp, dp in zip(_pieces(src[a].at[2 * cx + cy]), _pieces(land[a].at[j])):
                    pltpu.make_async_remote_copy(sp, dp, send.at[3 * a + j], recv.at[3 * a + j],
                                                 device_id=(cx, cy, c), device_id_type=MESH).start()
        token[...] = jnp.zeros_like(token)

    lands = [pltpu.with_memory_space_constraint(lax.empty((NCHIP - 1,) + p.shape[1:], p.dtype), pltpu.HBM) for p in parts]
    srcs = [pltpu.with_memory_space_constraint(p, pltpu.HBM) for p in parts]
    outs = pl.pallas_call(
        body, name=name,
        out_shape=(pltpu.SemaphoreType.DMA((3 * n,)), pltpu.SemaphoreType.DMA((3 * n,)),
                   *[pltpu.HBM(p.shape, p.dtype) for p in parts], *[pltpu.HBM(l.shape, l.dtype) for l in lands],
                   SDS((8, 128), F32)),
        in_specs=[HBM] * (2 * n), out_specs=(SEM, SEM, *([HBM] * (2 * n)), pl.BlockSpec(memory_space=pltpu.VMEM)),
        input_output_aliases={i: 2 + i for i in range(2 * n)},
        compiler_params=pltpu.CompilerParams(has_side_effects=EFFECT),
    )(*srcs, *lands)
    return outs[0], outs[1], list(outs[2:2 + n]), list(outs[2 + n:2 + 2 * n]), outs[2 + 2 * n]


def scatter_wait(name, send, recv, parts, lands, after):
    n = len(parts)
    afters = _as_list(after)

    def body(*refs):
        src, land = refs[:n], refs[n:2 * n]
        send_ref, recv_ref = refs[2 * n], refs[2 * n + 1]
        x, y, c, chips = _place()
        for a in range(n):
            for j, (cx, cy) in enumerate(chips):
                cp = pltpu.make_async_remote_copy(src[a].at[2 * cx + cy], land[a].at[j], send_ref.at[3 * a + j], recv_ref.at[3 * a + j],
                                                  device_id=(cx, cy, c), device_id_type=MESH)
                cp.wait_send()
                cp.wait_recv()

    outs = pl.pallas_call(
        body, name=name,
        out_shape=(*[pltpu.HBM(p.shape, p.dtype) for p in parts], *[pltpu.HBM(l.shape, l.dtype) for l in lands]),
        in_specs=[HBM] * (2 * n) + [SEM, SEM] + [ANY] * len(afters), out_specs=[HBM] * (2 * n),
        input_output_aliases={i: i for i in range(2 * n)},
        compiler_params=pltpu.CompilerParams(has_side_effects=EFFECT),
    )(*parts, *lands, send, recv, *afters)
    return list(outs[:n]), list(outs[n:])


def join_start(name, halves):
    n = len(halves)

    def body(*refs):
        src, dst = refs[:n], refs[n:2 * n]
        send, recv = refs[2 * n], refs[2 * n + 1]
        token = refs[4 * n + 2]
        x, y, c, _ = _place()
        sib = (x, y, 1 - c)
        for a in range(n):
            for sp, dp in zip(_pieces(src[a]), _pieces(dst[a])):
                pltpu.make_async_remote_copy(sp, dp, send.at[a], recv.at[a], device_id=sib, device_id_type=MESH).start()
        token[...] = jnp.zeros_like(token)

    lands = [pltpu.with_memory_space_constraint(lax.empty(h.shape, h.dtype), pltpu.HBM) for h in halves]
    srcs = [pltpu.with_memory_space_constraint(h, pltpu.HBM) for h in halves]
    outs = pl.pallas_call(
        body, name=name,
        out_shape=(pltpu.SemaphoreType.DMA((n,)), pltpu.SemaphoreType.DMA((n,)),
                   *[pltpu.HBM(h.shape, h.dtype) for h in halves], *[pltpu.HBM(l.shape, l.dtype) for l in lands],
                   SDS((8, 128), F32)),
        in_specs=[HBM] * (2 * n), out_specs=(SEM, SEM, *([HBM] * (2 * n)), pl.BlockSpec(memory_space=pltpu.VMEM)),
        input_output_aliases={i: 2 + i for i in range(2 * n)},
        compiler_params=pltpu.CompilerParams(has_side_effects=EFFECT),
    )(*srcs, *lands)
    return outs[0], outs[1], list(outs[2:2 + n]), list(outs[2 + n:2 + 2 * n]), outs[2 + 2 * n]


def join_wait(name, send, recv, halves, lands, after):
    n = len(halves)
    afters = _as_list(after)

    def body(*refs):
        src, dst = refs[:n], refs[n:2 * n]
        send_ref, recv_ref = refs[2 * n], refs[2 * n + 1]
        x, y, c, _ = _place()
        sib = (x, y, 1 - c)
        for a in range(n):
            cp = pltpu.make_async_remote_copy(src[a], dst[a], send_ref.at[a], recv_ref.at[a], device_id=sib, device_id_type=MESH)
            cp.wait_send()
            cp.wait_recv()

    outs = pl.pallas_call(
        body, name=name,
        out_shape=(*[pltpu.HBM(h.shape, h.dtype) for h in halves], *[pltpu.HBM(l.shape, l.dtype) for l in lands]),
        in_specs=[HBM] * (2 * n) + [SEM, SEM] + [ANY] * len(afters), out_specs=[HBM] * (2 * n),
        input_output_aliases={i: i for i in range(2 * n)},
        compiler_params=pltpu.CompilerParams(has_side_effects=EFFECT),
    )(*halves, *lands, send, recv, *afters)
    return list(outs[:n]), list(outs[n:])


def gather_small(name, xs, reduce, after=None):
    m, ncol = xs.shape
    afters = _as_list(after)

    def body(x_ref, *rest):
        out_ref, all_ref, send, recv, lsem = rest[len(afters):]
        x, y, c, chips = _place()
        me, sib = (x, y, c), (x, y, 1 - c)

        def rows(px, py, pc):
            return all_ref.at[pl.ds((4 * px + 2 * py + pc) * m, m), :]

        def copy(k, block, to, src=None):
            return pltpu.make_async_remote_copy(rows(*block) if src is None else src, rows(*block), send.at[k], recv.at[k],
                                                device_id=to, device_id_type=MESH)

        mine = pltpu.make_async_copy(x_ref, rows(*me), lsem)
        mine.start()
        first = [copy(0, me, sib, src=x_ref)] + [copy(1 + j, me, (*chip, c), src=x_ref) for j, chip in enumerate(chips)]
        for cp in first:
            cp.start()
        passed = [copy(4 + j, (*chip, c), sib) for j, chip in enumerate(chips)]
        for j, chip in enumerate(chips):
            copy(1 + j, (*chip, c), me).wait_recv()
            passed[j].start()
        copy(0, sib, me).wait_recv()
        for j, chip in enumerate(chips):
            copy(4 + j, (*chip, 1 - c), me).wait_recv()
        for cp in first + passed:
            cp.wait_send()
        mine.wait()
        if reduce:
            s = all_ref[0:m, :]
            for dev in range(1, 8):
                s = s + all_ref[dev * m:(dev + 1) * m, :]
            out_ref[...] = s
        else:
            out_ref[...] = all_ref[...]

    vm = pl.BlockSpec(memory_space=pltpu.VMEM)
    return pl.pallas_call(
        body, name=name, in_specs=[vm] + [ANY] * len(afters), out_specs=vm,
        out_shape=SDS((m, ncol) if reduce else (8 * m, ncol), F32),
        scratch_shapes=[pltpu.VMEM((8 * m, ncol), F32), pltpu.SemaphoreType.DMA((7,)), pltpu.SemaphoreType.DMA((7,)),
                        pltpu.SemaphoreType.DMA],
    )(xs, *afters)


RELAYOUT_ROWS = 128


def weights_to_cat(g_in):
    tm = RELAYOUT_ROWS

    def body(g_ref, o_ref):
        nat = jnp.concatenate([g_ref[j] for j in range(NCHIP)], axis=1)
        pad = jnp.zeros((tm, NCAT - OA - 16), BF16)
        o_ref[...] = jnp.concatenate([nat[:, 3072:7168], nat[:, 7184:11280], nat[:, 0:3072], nat[:, 7168:7184], pad], axis=1)

    return pl.pallas_call(
        body, name="weights_to_cat", grid=(D // tm,), in_specs=[pl.BlockSpec((NCHIP, tm, IN_SHARD), lambda i: (0, i, 0))],
        out_specs=pl.BlockSpec((tm, NCAT), lambda i: (i, 0)), out_shape=SDS((D, NCAT), BF16),
        compiler_params=_cparams(40 * 1024 * 1024, ("arbitrary",)),
    )(g_in)


def grads_from_cat(gw_cat):
    tm = RELAYOUT_ROWS
    nb = (D // 2) // tm

    def body(c_ref, o_ref):
        cat = c_ref[...]
        nat = jnp.concatenate([cat[:, OU:OA], cat[:, OV:OGP], cat[:, OA:OA + 16], cat[:, OGP:OU]], axis=1)
        for j in range(NCHIP):
            o_ref[j] = nat[:, j * IN_SHARD:(j + 1) * IN_SHARD]

    return pl.pallas_call(
        body, name="grads_from_cat", grid=(D // tm,), in_specs=[pl.BlockSpec((tm, NCAT), lambda i: (i, 0))],
        out_specs=pl.BlockSpec((None, NCHIP, tm, IN_SHARD), lambda i: (i // nb, 0, i % nb, 0)),
        out_shape=SDS((2, NCHIP, D // 2, IN_SHARD), BF16), compiler_params=_cparams(40 * 1024 * 1024, ("arbitrary",)),
    )(gw_cat)


def _pad_rows(a, rows):
    return jnp.concatenate([a, jnp.zeros((rows - a.shape[0],) + a.shape[1:], a.dtype)], axis=0)


def local_step(x2d, tgt, gf, g1, pool_scale, wa_pad, b_alpha, ng, g2, get_w, on_grad=None, on_settle=None):
    emit = on_grad if on_grad is not None else (lambda group, grads: None)
    settle = on_settle if on_settle is not None else (lambda group, after: None)
    h1 = norm1(x2d, g1)
    wcat, pw = get_w("in", h1)
    pcat = mm_in(h1, wcat)
    dpool, ylin = pool_fwd(pcat, pw)
    og, o, states = gla_fwd(pcat, wa_pad, b_alpha, ng)
    w_go, w_o = get_w("mid", og)
    mixed, ygla = mm_gla_out(og, w_go, ylin, pcat, pool_scale)
    x2, h2 = mm_out(mixed, w_o, x2d, g2)
    w_up = get_w("up", h2)
    rup, act = mm_up(h2, w_up)
    w_dn = get_w("down", act)
    dx3, dx3b, g_nf, loss_row = mm_down(act, w_dn, x2, tgt, gf)

    gw_down = mm_wgrad("mm_dw_down", act, dx3b, DFF, D, (2, NCHIP, D // 2, D), (None, None, 512, D),
                       lambda j, i, k: ((i // 2) % 2, i // 4, i % 2, 0), 512, D)
    token = emit("down", {"down": gw_down})
    dup = mm_dact(dx3b, w_dn, rup, after=token)
    token = settle("down", dup)
    dx2, dx2b, g_mlp = mm_dh2(dup, w_up, x2, dx3, g2, after=token)
    gw_up = mm_wgrad("mm_dw_up", h2, dup, D, DFF, (2, NCHIP, D // 2, D), (None, None, 512, D),
                     lambda j, i, k: (i // 2, j, i % 2, 0), 512, D)
    token = emit("up", {"up": gw_up})
    dylin, dygla, dlgp, dlgg, g_ps = mm_dmixed(dx2b, w_o, pcat, ylin, ygla, pool_scale, after=token)
    token = settle("up", dylin)
    gw_out = mm_wgrad("mm_dw_out", mixed, dx2b, D, D, (2, NCHIP, 256, D), (None, None, 256, D),
                      lambda j, i, k: (i % 2, i // 2, 0, 0), 256, D)
    do, dg, g_ng = mm_dog(dygla, w_go, o, pcat, ng, after=token)
    gw_go = mm_wgrad("mm_dw_gla_out", og, dygla, D, D, (2, NCHIP, 256, D), (None, None, 256, D),
                     lambda j, i, k: (i % 2, i // 2, 0, 0), 256, D)
    token = emit("mix", {"out": gw_out, "gla_out": gw_go})
    dq, dk, dv, dalow, g_wa, g_ba = gla_bwd(do, pcat, states, wa_pad, b_alpha, b_alpha if token is None else token)
    token = settle("mix", dq)
    du, dpw = pool_bwd(dylin, dpool, pw)
    dpcat = jnp.concatenate([dv, dg, dlgp, dlgg, du, dq, dk, dalow, jnp.zeros((T, NCAT - OA - APAD), BF16)], axis=1)
    gw_cat = mm_wgrad("mm_dw_in", h1, dpcat, D, NCAT, (D, NCAT), (1024, 1280), lambda j, i, k: (i, j), 1024, 1280, after=token)
    token = settle("in", emit("in", {"in_cat": gw_cat, "pool": dpw}))
    grad_x, g_mix = mm_dh1(dpcat, wcat, x2d, dx2, g1, after=token)
    return (loss_row[0, 0], grad_x, g_mix, g_ps, g_mlp, g_nf, g_ng, g_ba, g_wa, token,
            gw_cat, dpw, gw_go, gw_out, gw_up, gw_down)


def kernel(x, norm_mix_g, w_in, pool_w, pool_scale, w_alpha, b_alpha, gla_norm_g, w_gla_out, w_out, norm_mlp_g, w_mlp_up, w_mlp_down, norm_final_g, loss_target, m_norm_mix_g, m_w_in, m_pool_w, m_pool_scale, m_w_alpha, m_b_alpha, m_gla_norm_g, m_w_gla_out, m_w_out, m_norm_mlp_g, m_w_mlp_up, m_w_mlp_down, m_norm_final_g, v_norm_mix_g, v_w_in, v_pool_w, v_pool_scale, v_w_alpha, v_b_alpha, v_gla_norm_g, v_w_gla_out, v_w_out, v_norm_mlp_g, v_w_mlp_up, v_w_mlp_down, v_norm_final_g):
    chip = 2 * lax.axis_index("x") + lax.axis_index("y")
    chip_i = chip.astype(jnp.int32).reshape(1)
    core_i = lax.axis_index("c").astype(jnp.int32).reshape(1)
    x2d = x.reshape(T, D)
    tgt = loss_target.reshape(T, D)
    gf = norm_final_g.reshape(1, D)

    def halves(w2d):
        r, c = w2d.shape
        return w2d.astype(BF16).reshape(2, r // 2, c)

    pool_shard = pool_w.reshape(4 * PG, PO // NCHIP)
    big = [w_in[0], w_gla_out[0], w_out[0], w_mlp_up[0], w_mlp_down[0], pool_shard]
    groups = {"in": [big[0], big[5]], "mid": [big[1], big[2]], "up": [big[3]], "down": [big[4]]}
    sent = {g: [halves(w) for w in ws] for g, ws in groups.items()}
    started = {}

    def start(group, after=None):
        started[group] = gather_start("gather_start_" + group, sent[group], after)

    start("in")
    w_in_r, m_in_r, v_in_r = [a.reshape(2, D // 2, IN_SHARD)
                              for a in lax.optimization_barrier((w_in, m_w_in, v_w_in, started["in"][2][0]))[:3]]

    def get_w(group, after):
        send, recv, shards, lands = started[group]
        if group == "in":
            after = [after, w_in_r, m_in_r, v_in_r, *sent["mid"], *sent["up"], *sent["down"], wa_pad]
        shards, lands = gather_wait("gather_wait_" + group, send, recv, shards, lands, after)
        if group == "in":
            start("mid", lands[0])
            start("up", started["mid"][3][0])
        if group == "mid":
            start("down", lands[0])
        whole = forward_halves("forward_" + group, shards, lands)
        if group == "in":
            g_in, g_pool = whole
            wcat = weights_to_cat(g_in.reshape(NCHIP, D, IN_SHARD))
            pw = jnp.concatenate([g_pool[j].reshape(4, PG, PO // NCHIP) for j in range(NCHIP)], axis=2)
            return wcat, pw
        if group == "mid":
            return whole[0].reshape(D, D), whole[1].reshape(D, D)
        if group == "up":
            return whole[0].reshape(NCHIP, D, D)
        return whole[0].reshape(DFF, D)

    small_w = pack_rows("pack_small_w", [w_alpha[0].reshape(4, QK),
                                         jnp.concatenate([gla_norm_g[0].reshape(1, 512), jnp.zeros((1, 512), F32)], axis=1)], 8)
    sw_all = gather_small("gather_small_w", small_w, False).reshape(8, 8, QK)
    wa_full = jnp.concatenate([sw_all[2 * j, 0:4].reshape(16, DK) for j in range(NCHIP)], axis=1)
    ng_full = jnp.concatenate([sw_all[2 * j, 4, 0:512].reshape(HEADS, DV // NCHIP) for j in range(NCHIP)], axis=1)
    wa_pad = _pad_rows(wa_full, APAD).astype(BF16)
    ng = ng_full.reshape(1, D)

    pending = {}
    wmv = {"in": (w_in_r, m_in_r, v_in_r), "gla_out": (big[1], m_w_gla_out, v_w_gla_out), "out": (big[2], m_w_out, v_w_out),
           "up": (big[3], m_w_mlp_up, v_w_mlp_up), "down": (big[4], m_w_mlp_down, v_w_mlp_down), "pool": (big[5], m_pool_w, v_pool_w)}
    big_res = {}

    def reduce_group(group, after):
        nms, send, recv, sums, lands = pending[group]
        sums, lands = scatter_wait("scatter_wait_" + group, send, recv, sums, lands, after)
        reduced = [sum_chips("sum_chips_" + nm, a, b, chip_i) for nm, a, b in zip(nms, sums, lands)]
        send, recv, reduced, lands, token = join_start("join_start_" + group, reduced)
        pending[group] = (nms, send, recv, reduced, lands)
        return token

    def update_group(group, after):
        nms, send, recv, reduced, lands = pending[group]
        reduced, from_sib = join_wait("join_wait_" + group, send, recv, reduced, lands, after)
        for nm, g_own, g_sib in zip(nms, reduced, from_sib):
            w, m, v = wmv[nm]
            shp = (2,) + g_own.shape
            big_res[nm] = adamw_halves("adamw_" + nm, w.reshape(shp), g_own, g_sib, m.reshape(shp), v.reshape(shp), core_i)

    def on_grad(group, grads):
        if group == "in":
            gw_in = grads_from_cat(grads["in_cat"])
            gw_pool = jnp.stack([grads["pool"][:, :, j * 128:(j + 1) * 128].reshape(2, 2 * PG, 128)
                                 for j in range(NCHIP)], axis=1)
            grads = {"in": gw_in, "pool": gw_pool}
        nms, parts = list(grads.keys()), list(grads.values())
        send, recv, parts, got, token = exchange_start("exchange_start_" + group, parts)
        pending[group] = (nms, send, recv, parts, got)
        return token

    def on_settle(group, after):
        if group == "in":
            after = reduce_group("down", after)
        nms, send, recv, parts, got = pending[group]
        parts, got = exchange_wait("exchange_wait_" + group, send, recv, parts, got, after)
        sums = [add_pairs("add_pair_" + nm, a, b, core_i) for nm, a, b in zip(nms, parts, got)]
        send, recv, sums, lands, token = scatter_start("scatter_start_" + group, sums)
        pending[group] = (nms, send, recv, sums, lands)
        if group != "in":
            return token
        token = reduce_group("up", token)
        token = reduce_group("mix", token)
        for earlier in ("down", "up", "mix"):
            update_group(earlier, token)
            token = big_res[pending[earlier][0][-1]][1]
        return [big_res[nm][1] for nm in ("down", "up", "out", "gla_out")]

    (loss_local, grad_x, g_mix, g_ps, g_mlp, g_nf, g_ng, g_ba, g_wa) = local_step(
        x2d, tgt, gf, norm_mix_g, pool_scale, wa_pad, b_alpha, ng, norm_mlp_g, get_w, on_grad, on_settle)[:9]
    loss = lax.psum(loss_local, ("x", "y", "c"))
    join_in_token = reduce_group("in", grad_x)

    ROWS = 16

    def wide(a, n):
        return jnp.concatenate([a.reshape(1, n), jnp.zeros((1, D - n), F32)], axis=1)

    packed = pack_rows("pack_small_g", [g_mix, g_ps, g_mlp, g_nf, g_ng, wide(g_ba, QK), g_wa[0:16].reshape(8, D)], ROWS)
    tot = gather_small("reduce_small_g", packed, True, join_in_token)
    t_wa = lax.dynamic_slice(tot[6:14].reshape(16, QK), (0, chip * DK), (16, DK))
    t_ng = lax.dynamic_slice(tot[4].reshape(HEADS, DV), (0, chip * (DV // NCHIP)), (HEADS, DV // NCHIP))

    def pack_small(nm, mix, ps, mlp, nf, ba, wa, gn):
        return pack_rows(nm, [mix.reshape(1, D), ps.reshape(1, D), mlp.reshape(1, D), nf.reshape(1, D), wide(ba, QK),
                              wa.reshape(2, D), wide(gn, 512)], ROWS)

    sg = pack_small("pack_g", tot[0], tot[1], tot[2], tot[3], tot[5, 0:QK], t_wa, t_ng)
    sw = pack_small("pack_w", norm_mix_g, pool_scale, norm_mlp_g, norm_final_g, b_alpha, w_alpha, gla_norm_g)
    sm = pack_small("pack_m", m_norm_mix_g, m_pool_scale, m_norm_mlp_g, m_norm_final_g, m_b_alpha, m_w_alpha, m_gla_norm_g)
    sv = pack_small("pack_v", v_norm_mix_g, v_pool_scale, v_norm_mlp_g, v_norm_final_g, v_b_alpha, v_w_alpha, v_gla_norm_g)
    small_res = adamw("adamw_small", sw, sg, sm, sv)
    update_group("in", small_res[1])

    def unpack(p):
        return {"norm_mix_g": p[0].reshape(1, D), "pool_scale": p[1].reshape(1, D), "norm_mlp_g": p[2].reshape(1, D),
                "norm_final_g": p[3].reshape(D), "b_alpha": p[4, 0:QK].reshape(1, QK), "w_alpha": p[5:7].reshape(1, 16, DK),
                "gla_norm_g": p[7, 0:512].reshape(1, HEADS, DV // NCHIP)}

    order = ["norm_mix_g", "w_in", "pool_w", "pool_scale", "w_alpha", "b_alpha", "gla_norm_g", "w_gla_out", "w_out",
             "norm_mlp_g", "w_mlp_up", "w_mlp_down", "norm_final_g"]
    big_key = {"w_in": ("in", w_in.shape), "pool_w": ("pool", pool_w.shape), "w_gla_out": ("gla_out", w_gla_out.shape),
               "w_out": ("out", w_out.shape), "w_mlp_up": ("up", w_mlp_up.shape), "w_mlp_down": ("down", w_mlp_down.shape)}
    result = [loss, grad_x.reshape(1, T, D)]
    for kind in range(4):
        small = unpack(small_res[kind])
        for nm in order:
            if nm in big_key:
                key, shp = big_key[nm]
                result.append(big_res[key][kind].reshape(shp))
            else:
                result.append(small[nm])
    return tuple(result)
```

```python
import itertools

import jax
import jax.numpy as jnp
from jax import lax
from jax.experimental import pallas as pl
from jax.experimental.pallas import tpu as pltpu

F32 = jnp.float32
BF16 = jnp.bfloat16
SDS = jax.ShapeDtypeStruct
MESH = pl.DeviceIdType.MESH
ANY = pl.BlockSpec(memory_space=pl.ANY)

T = 2048
D = 2048
DFF = 8192
NCHIP = 4
IN_WIDTH = 11280
IN_SHARD = IN_WIDTH // NCHIP
CHUNK = 64
NCHUNK = T // CHUNK
HEADS = 4
DK = 256
DV = 512
QK = HEADS * DK
EPS = 1e-6
POOL_WINDOWS = (2, 4, 8, 16)
PG = 256
PO = 512

OV, OG, OGP, OGG, OU, OQ, OKK, OA = 0, 2048, 4096, 6144, 8192, 9216, 10240, 11264
NCAT = 11520
APAD = 128

VMEM_CAP = 56 * 1024 * 1024

PIECE_BYTES = 384 * 1024

ADAM_LR, ADAM_B1, ADAM_B2, ADAM_EPS, ADAM_WD, ADAM_STEP = 0.001, 0.9, 0.999, 1e-08, 0.01, 10


def _cparams(vmem_bytes=None, sem=None):
    kw = {}
    if vmem_bytes is not None:
        kw["vmem_limit_bytes"] = int(min(max(vmem_bytes, 32 * 1024 * 1024), VMEM_CAP))
    if sem is not None:
        kw["dimension_semantics"] = sem
    return pltpu.CompilerParams(**kw)


def _nbytes(shape, dtype):
    n = 1
    for s in shape:
        if s is not None:
            n *= s
    return n * jnp.dtype(dtype).itemsize


def _sigmoid(x):
    return 1.0 / (1.0 + jnp.exp(-x))


def _as_list(after):
    if after is None:
        return []
    return list(after) if isinstance(after, (list, tuple)) else [after]


def matmul(name, a, b, *, a_spec, b_spec, cdims, grid, acc_shape, outs, extras=(), epi, after=None):
    nj, ni, nk = grid
    ne, no = len(extras), len(outs)
    afters = _as_list(after)
    first_out = 2 + ne + len(afters)

    def body(*refs):
        a_ref, b_ref = refs[0], refs[1]
        ex = refs[2:2 + ne]
        out_refs = refs[first_out:first_out + no]
        i = pl.program_id(1)
        part = lax.dot_general(a_ref[...], b_ref[...], (cdims, ((), ())), preferred_element_type=F32)
        if nk == 1:
            epi(part, ex, out_refs, i)
        else:
            acc_ref = refs[first_out + no]
            k = pl.program_id(2)

            @pl.when(k == 0)
            def _():
                acc_ref[...] = part

            @pl.when(k > 0)
            def _():
                acc_ref[...] += part

            @pl.when(k == nk - 1)
            def _():
                epi(acc_ref[...], ex, out_refs, i)

    in_specs = [pl.BlockSpec(*a_spec), pl.BlockSpec(*b_spec)] + [pl.BlockSpec(bs, im) for _, bs, im in extras]
    in_specs += [ANY] * len(afters)
    out_specs = [pl.BlockSpec(bs, im) for _, _, bs, im in outs]
    out_shape = [SDS(s, dt) for s, dt, _, _ in outs]
    vm = 2 * (_nbytes(a_spec[0], a.dtype) + _nbytes(b_spec[0], b.dtype))
    vm += 2 * sum(_nbytes(bs, arr.dtype) for arr, bs, _ in extras)
    vm += 2 * sum(_nbytes(bs, dt) for _, dt, bs, _ in outs)
    vm += 6 * _nbytes(acc_shape, F32)
    scratch = [pltpu.VMEM(acc_shape, F32)] if nk > 1 else []
    return pl.pallas_call(
        body, name=name, grid=grid, in_specs=in_specs, out_specs=out_specs, out_shape=out_shape,
        scratch_shapes=scratch,
        compiler_params=_cparams(vm, ("arbitrary", "arbitrary", "arbitrary")),
    )(a, b, *[arr for arr, _, _ in extras], *afters)


NN =((1,), (0,))
NT = ((1,), (1,))
TN = ((0,), (0,))


def _row_acc(out_ref, val, i):
    @pl.when(i == 0)
    def _():
        out_ref[...] = val

    @pl.when(i > 0)
    def _():
        out_ref[...] += val


def _rms_bwd(xn, r, dxn):
    return r * (dxn - xn * jnp.mean(dxn * xn, axis=-1, keepdims=True))


def norm1(x, g):
    tm = 256

    def body(x_ref, g_ref, h_ref):
        xv = x_ref[...]
        r = lax.rsqrt(jnp.mean(xv * xv, axis=-1, keepdims=True) + EPS)
        h_ref[...] = (xv * r * g_ref[...]).astype(BF16)

    return pl.pallas_call(
        body, name="norm1", grid=(T // tm,),
        in_specs=[pl.BlockSpec((tm, D), lambda i: (i, 0)), pl.BlockSpec((1, D), lambda i: (0, 0))],
        out_specs=pl.BlockSpec((tm, D), lambda i: (i, 0)), out_shape=SDS((T, D), BF16),
        compiler_params=_cparams(32 * 1024 * 1024, ("arbitrary",)),
    )(x, g)


def mm_in(h1, wcat):
    tm, tn = 1024, 1280

    def epi(acc, ex, outs, i):
        outs[0][...] = acc.astype(BF16)

    return matmul("mm_in", h1, wcat, a_spec=((tm, D), lambda j, i, k: (i, 0)), b_spec=((D, tn), lambda j, i, k: (0, j)),
                  cdims=NN, grid=(NCAT // tn, T // tm, 1), acc_shape=(tm, tn),
                  outs=[((T, NCAT), BF16, (tm, tn), lambda j, i, k: (i, j))], epi=epi)[0]


def _window_sum(x, w, up):
    n = x.shape[0]
    row = lax.broadcasted_iota(jnp.int32, x.shape, 0)
    s, sh = x, 1
    while sh < w:
        if up:
            s = s + jnp.where(row < n - sh, pltpu.roll(s, n - sh, axis=0), 0.0)
        else:
            s = s + jnp.where(row >= sh, pltpu.roll(s, sh, axis=0), 0.0)
        sh *= 2
    return s


def _inv_count(shape, w):
    row = lax.broadcasted_iota(jnp.int32, shape, 0)
    return 1.0 / jnp.minimum(row + 1, w).astype(F32)


def pool_fwd(pcat, pw):
    def body(u_ref, pw_ref, d_ref, y_ref):
        for gi, w in enumerate(POOL_WINDOWS):
            ug = u_ref[:, gi * PG:(gi + 1) * PG].astype(F32)
            dg = _window_sum(ug, w, False) * _inv_count(ug.shape, w) - ug
            db = dg.astype(BF16)
            d_ref[:, gi * PG:(gi + 1) * PG] = db
            y_ref[:, gi * PO:(gi + 1) * PO] = jnp.dot(db, pw_ref[gi], preferred_element_type=F32).astype(BF16)

    return pl.pallas_call(
        body, name="pool_fwd", grid=(1,),
        in_specs=[pl.BlockSpec((T, 4 * PG), lambda i: (0, OU // (4 * PG))), pl.BlockSpec((4, PG, PO), lambda i: (0, 0, 0))],
        out_specs=[pl.BlockSpec((T, 4 * PG), lambda i: (0, 0)), pl.BlockSpec((T, D), lambda i: (0, 0))],
        out_shape=[SDS((T, 4 * PG), BF16), SDS((T, D), BF16)],
        compiler_params=_cparams(48 * 1024 * 1024, ("arbitrary",)),
    )(pcat, pw)


def pool_bwd(dylin, d, pw):
    def body(dy_ref, d_ref, pw_ref, du_ref, dpw_ref):
        for gi, w in enumerate(POOL_WINDOWS):
            dyl = dy_ref[:, gi * PO:(gi + 1) * PO]
            dd = lax.dot_general(dyl, pw_ref[gi], (NT, ((), ())), preferred_element_type=F32)
            du = _window_sum(dd * _inv_count(dd.shape, w), w, True) - dd
            du_ref[:, gi * PG:(gi + 1) * PG] = du.astype(BF16)
            dpw_ref[gi] = lax.dot_general(d_ref[:, gi * PG:(gi + 1) * PG], dyl, (TN, ((), ())),
                                          preferred_element_type=F32).astype(BF16)

    return pl.pallas_call(
        body, name="pool_bwd", grid=(1,),
        in_specs=[pl.BlockSpec((T, D), lambda i: (0, 0)), pl.BlockSpec((T, 4 * PG), lambda i: (0, 0)),
                  pl.BlockSpec((4, PG, PO), lambda i: (0, 0, 0))],
        out_specs=[pl.BlockSpec((T, 4 * PG), lambda i: (0, 0)), pl.BlockSpec((4, PG, PO), lambda i: (0, 0, 0))],
        out_shape=[SDS((T, 4 * PG), BF16), SDS((4, PG, PO), BF16)],
        compiler_params=_cparams(48 * 1024 * 1024, ("arbitrary",)),
    )(dylin, d, pw)


def _gate_decay(alow, wa, ba):
    a = jnp.dot(alow, wa, preferred_element_type=F32) + ba
    ls = jax.nn.log_sigmoid(a) * (1.0 / 16.0)
    r = lax.broadcasted_iota(jnp.int32, (CHUNK, CHUNK), 0)
    c = lax.broadcasted_iota(jnp.int32, (CHUNK, CHUNK), 1)
    tri = jnp.where(c <= r, 1.0, 0.0).astype(F32)
    cum = jnp.dot(tri, ls, preferred_element_type=F32, precision=lax.Precision.HIGHEST)
    last = cum[CHUNK - 1:CHUNK, :]
    return a, jnp.exp(last - cum), jnp.exp(last)


def gla_fwd(pcat, wa, ba, ng):
    def body(q_ref, k_ref, v_ref, g_ref, al_ref, wa_ref, ba_ref, ng_ref, og_ref, o_ref, st_ref, s_scr):
        @pl.when(pl.program_id(0) == 0)
        def _():
            s_scr[...] = jnp.zeros_like(s_scr)

        _, e, decay = _gate_decay(al_ref[...], wa_ref[...], ba_ref[...])
        kd = (k_ref[...].astype(F32) * e).astype(BF16)
        qs = (q_ref[...].astype(F32) * (DK ** -0.5)).astype(BF16)
        for h in range(HEADS):
            ck = slice(h * DK, (h + 1) * DK)
            cv = slice(h * DV, (h + 1) * DV)
            s_new = s_scr[h] * decay[:, ck] + lax.dot_general(v_ref[:, cv], kd[:, ck], (TN, ((), ())),
                                                               preferred_element_type=F32)
            s_scr[h] = s_new
            sb = s_new.astype(BF16)
            st_ref[h] = sb
            oh = lax.dot_general(qs[:, ck], sb, (NT, ((), ())), preferred_element_type=F32)
            o_ref[:, cv] = oh.astype(BF16)
            on = oh * lax.rsqrt(jnp.mean(oh * oh, axis=-1, keepdims=True) + EPS) * ng_ref[:, cv]
            gv = g_ref[:, cv].astype(F32)
            og_ref[:, cv] = (on * (gv * _sigmoid(gv))).astype(BF16)

    row = lambda c: (c, 0)
    return pl.pallas_call(
        body, name="gla_fwd", grid=(NCHUNK,),
        in_specs=[pl.BlockSpec((CHUNK, QK), lambda c: (c, OQ // QK)), pl.BlockSpec((CHUNK, QK), lambda c: (c, OKK // QK)),
                  pl.BlockSpec((CHUNK, D), lambda c: (c, OV // D)), pl.BlockSpec((CHUNK, D), lambda c: (c, OG // D)),
                  pl.BlockSpec((CHUNK, APAD), lambda c: (c, OA // APAD)),
                  pl.BlockSpec((APAD, QK), lambda c: (0, 0)), pl.BlockSpec((1, QK), lambda c: (0, 0)),
                  pl.BlockSpec((1, D), lambda c: (0, 0))],
        out_specs=[pl.BlockSpec((CHUNK, D), row), pl.BlockSpec((CHUNK, D), row),
                   pl.BlockSpec((None, HEADS, DV, DK), lambda c: (c, 0, 0, 0))],
        out_shape=[SDS((T, D), BF16), SDS((T, D), BF16), SDS((NCHUNK, HEADS, DV, DK), BF16)],
        scratch_shapes=[pltpu.VMEM((HEADS, DV, DK), F32)],
        compiler_params=_cparams(32 * 1024 * 1024, ("arbitrary",)),
    )(pcat, pcat, pcat, pcat, pcat, wa, ba, ng)


def gla_bwd(do, pcat, states, wa, ba, after):
    def body(do_ref, q_ref, k_ref, v_ref, al_ref, sc_ref, sp_ref, wa_ref, ba_ref, after_ref,
             dq_ref, dk_ref, dv_ref, dal_ref, dwa_ref, dba_ref, ds_scr):
        i = pl.program_id(0)

        @pl.when(i == 0)
        def _():
            ds_scr[...] = jnp.zeros_like(ds_scr)

        has_prev = jnp.where(i < NCHUNK - 1, 1.0, 0.0).astype(F32)
        a, e, decay = _gate_decay(al_ref[...], wa_ref[...], ba_ref[...])
        kf = k_ref[...].astype(F32)
        kdf = kf * e
        kd = kdf.astype(BF16)
        qs = (q_ref[...].astype(F32) * (DK ** -0.5)).astype(BF16)
        dkd_parts, ddecay_parts = [], []
        for h in range(HEADS):
            ck = slice(h * DK, (h + 1) * DK)
            cv = slice(h * DV, (h + 1) * DV)
            doh = do_ref[:, cv]
            ds = ds_scr[h] + lax.dot_general(doh, qs[:, ck], (TN, ((), ())), preferred_element_type=F32)
            dsb = ds.astype(BF16)
            dq_ref[:, ck] = (jnp.dot(doh, sc_ref[h], preferred_element_type=F32) * (DK ** -0.5)).astype(BF16)
            dkd_parts.append(jnp.dot(v_ref[:, cv], dsb, preferred_element_type=F32))
            dv_ref[:, cv] = lax.dot_general(kd[:, ck], dsb, (NT, ((), ())), preferred_element_type=F32).astype(BF16)
            ddecay_parts.append(jnp.sum(ds * sp_ref[h].astype(F32), axis=0, keepdims=True) * has_prev)
            ds_scr[h] = ds * decay[:, ck]
        dkd = jnp.concatenate(dkd_parts, axis=1)
        ddecay = jnp.concatenate(ddecay_parts, axis=1)
        dk_ref[...] = (dkd * e).astype(BF16)
        dearg = dkd * kdf
        dlast = jnp.sum(dearg, axis=0, keepdims=True) + ddecay * decay
        r = lax.broadcasted_iota(jnp.int32, (CHUNK, CHUNK), 0)
        c = lax.broadcasted_iota(jnp.int32, (CHUNK, CHUNK), 1)
        triu = jnp.where(c >= r, 1.0, 0.0).astype(F32)
        dls = dlast - jnp.dot(triu, dearg, preferred_element_type=F32, precision=lax.Precision.HIGHEST)
        da = dls * (1.0 / 16.0) * (1.0 - _sigmoid(a))
        dab = da.astype(BF16)
        dal_ref[...] = lax.dot_general(dab, wa_ref[...], (NT, ((), ())), preferred_element_type=F32).astype(BF16)
        dwa = lax.dot_general(al_ref[...], dab, (TN, ((), ())), preferred_element_type=F32)
        dba = jnp.sum(da, axis=0, keepdims=True)

        @pl.when(i == 0)
        def _():
            dwa_ref[...] = dwa
            dba_ref[...] = dba

        @pl.when(i > 0)
        def _():
            dwa_ref[...] += dwa
            dba_ref[...] += dba

    rev = lambda i: NCHUNK - 1 - i
    return pl.pallas_call(
        body, name="gla_bwd", grid=(NCHUNK,),
        in_specs=[pl.BlockSpec((CHUNK, D), lambda i: (rev(i), 0)),
                  pl.BlockSpec((CHUNK, QK), lambda i: (rev(i), OQ // QK)), pl.BlockSpec((CHUNK, QK), lambda i: (rev(i), OKK // QK)),
                  pl.BlockSpec((CHUNK, D), lambda i: (rev(i), OV // D)), pl.BlockSpec((CHUNK, APAD), lambda i: (rev(i), OA // APAD)),
                  pl.BlockSpec((None, HEADS, DV, DK), lambda i: (rev(i), 0, 0, 0)),
                  pl.BlockSpec((None, HEADS, DV, DK), lambda i: (jnp.maximum(rev(i) - 1, 0), 0, 0, 0)),
                  pl.BlockSpec((APAD, QK), lambda i: (0, 0)), pl.BlockSpec((1, QK), lambda i: (0, 0)), ANY],
        out_specs=[pl.BlockSpec((CHUNK, QK), lambda i: (rev(i), 0)), pl.BlockSpec((CHUNK, QK), lambda i: (rev(i), 0)),
                   pl.BlockSpec((CHUNK, D), lambda i: (rev(i), 0)), pl.BlockSpec((CHUNK, APAD), lambda i: (rev(i), 0)),
                   pl.BlockSpec((APAD, QK), lambda i: (0, 0)), pl.BlockSpec((1, QK), lambda i: (0, 0))],
        out_shape=[SDS((T, QK), BF16), SDS((T, QK), BF16), SDS((T, D), BF16), SDS((T, APAD), BF16),
                   SDS((APAD, QK), F32), SDS((1, QK), F32)],
        scratch_shapes=[pltpu.VMEM((HEADS, DV, DK), F32)],
        compiler_params=_cparams(32 * 1024 * 1024, ("arbitrary",)),
    )(do, pcat, pcat, pcat, pcat, states, states, wa, ba, after)


TMF = 256
TMW = 512
_rowblk = ((TMF, D), lambda j, i, k: (i, 0))
_vec = ((1, D), lambda j, i, k: (0, 0))


def _full_spec(col):
    return ((TMF, D), lambda j, i, k: (i, col))


TBIG = 1024


def square_matmul(name, a, b, *, a_spec, b_spec, cdims, nk, after=None):
    def epi(acc, ex, outs, i):
        outs[0][...] = acc

    return matmul(name, a, b, a_spec=a_spec, b_spec=b_spec, cdims=cdims, grid=(D // TBIG, T // TBIG, nk),
                  acc_shape=(TBIG, TBIG), outs=[((T, D), F32, (TBIG, TBIG), lambda j, i, k: (i, j))], epi=epi,
                  after=after)[0]


def rowwise(name, y, *, extras, outs, epi):
    ne = len(extras)

    def body(*refs):
        epi(refs[0][...], refs[1:1 + ne], refs[1 + ne:], pl.program_id(1))

    in_specs = [pl.BlockSpec(*_rowblk)] + [pl.BlockSpec(bs, im) for _, bs, im in extras]
    return pl.pallas_call(
        body, name=name, grid=(1, T // TMF, 1), in_specs=in_specs,
        out_specs=[pl.BlockSpec(bs, im) for _, _, bs, im in outs], out_shape=[SDS(s, dt) for s, dt, _, _ in outs],
        compiler_params=_cparams(40 * 1024 * 1024, ("arbitrary", "arbitrary", "arbitrary")),
    )(y, *[arr for arr, _, _ in extras])


def mm_gla_out(og, w, ylin, pcat, pscale):
    def epi(acc, ex, outs, i):
        ylin_ref, lgp_ref, lgg_ref, ps_ref = ex
        gp = _sigmoid(lgp_ref[...].astype(F32))
        gg = _sigmoid(lgg_ref[...].astype(F32))
        outs[0][...] = (gp * (ylin_ref[...].astype(F32) * ps_ref[...]) + gg * acc).astype(BF16)
        outs[1][...] = acc.astype(BF16)

    return matmul("mm_gla_out", og, w, a_spec=_rowblk, b_spec=((D, D), lambda j, i, k: (0, 0)), cdims=NN,
                  grid=(1, T // TMF, 1), acc_shape=(TMF, D),
                  extras=[(ylin, *_rowblk), (pcat, *_full_spec(OGP // D)), (pcat, *_full_spec(OGG // D)), (pscale, *_vec)],
                  outs=[((T, D), BF16, *_rowblk), ((T, D), BF16, *_rowblk)], epi=epi)


def mm_out(mixed, w, x, g2):
    def epi(acc, ex, outs, i):
        x_ref, g_ref = ex
        x2 = x_ref[...] + acc
        r = lax.rsqrt(jnp.mean(x2 * x2, axis=-1, keepdims=True) + EPS)
        outs[0][...] = x2
        outs[1][...] = (x2 * r * g_ref[...]).astype(BF16)

    return matmul("mm_out", mixed, w, a_spec=_rowblk, b_spec=((D, D), lambda j, i, k: (0, 0)), cdims=NN,
                  grid=(1, T // TMF, 1), acc_shape=(TMF, D), extras=[(x, *_rowblk), (g2, *_vec)],
                  outs=[((T, D), F32, *_rowblk), ((T, D), BF16, *_rowblk)], epi=epi)


def mm_up(h2, wup):
    def epi(acc, ex, outs, i):
        r = jnp.maximum(acc, 0.0)
        outs[0][...] = r.astype(BF16)
        outs[1][...] = (r * r).astype(BF16)

    blk = ((TMW, D), lambda j, i, k: (i, j))
    return matmul("mm_up", h2, wup, a_spec=((TMW, D), lambda j, i, k: (i, 0)), b_spec=((None, D, D), lambda j, i, k: (j, 0, 0)),
                  cdims=NN, grid=(NCHIP, T // TMW, 1), acc_shape=(TMW, D),
                  outs=[((T, DFF), BF16, *blk), ((T, DFF), BF16, *blk)], epi=epi)


def mm_down(act, wdown, x2, tgt, gf):
    tk = 2048

    def epi(acc, ex, outs, i):
        x2_ref, t_ref, g_ref = ex
        dx_ref, dxb_ref, gnf_ref, loss_ref = outs
        x3 = x2_ref[...] + acc
        r = lax.rsqrt(jnp.mean(x3 * x3, axis=-1, keepdims=True) + EPS)
        xn = x3 * r
        err = xn * g_ref[...] - t_ref[...]
        lsum = 0.5 * jnp.sum(jnp.mean(err * err, axis=-1, keepdims=True), axis=0, keepdims=True)
        dy = err * (1.0 / D)
        _row_acc(gnf_ref, jnp.sum(dy * xn, axis=0, keepdims=True), i)
        _row_acc(loss_ref, jnp.broadcast_to(lsum, (1, 128)), i)
        dx3 = _rms_bwd(xn, r, dy * g_ref[...])
        dx_ref[...] = dx3
        dxb_ref[...] = dx3.astype(BF16)

    y = square_matmul("mm_down", act, wdown, a_spec=((TBIG, tk), lambda j, i, k: (i, k)),
                      b_spec=((tk, TBIG), lambda j, i, k: (k, j)), cdims=NN, nk=DFF // tk)
    return rowwise("rows_final", y, extras=[(x2, *_rowblk), (tgt, *_rowblk), (gf, *_vec)],
                   outs=[((T, D), F32, *_rowblk), ((T, D), BF16, *_rowblk), ((1, D), F32, *_vec),
                         ((1, 128), F32, (1, 128), lambda j, i, k: (0, 0))], epi=epi)


def mm_dact(dx3b, wdown, rup, after=None):
    def epi(acc, ex, outs, i):
        outs[0][...] = (acc * 2.0 * ex[0][...].astype(F32)).astype(BF16)

    blk = ((TMW, D), lambda j, i, k: (i, j))
    return matmul("mm_dact", dx3b, wdown, a_spec=((TMW, D), lambda j, i, k: (i, 0)), b_spec=((D, D), lambda j, i, k: (j, 0)),
                  cdims=NT, grid=(DFF // D, T // TMW, 1), acc_shape=(TMW, D), extras=[(rup, *blk)],
                  outs=[((T, DFF), BF16, *blk)], epi=epi, after=after)[0]


def mm_wgrad(name, a, b, m, n, out_shape, out_block, out_map, tm, tn, after=None):
    def epi(acc, ex, outs, i):
        outs[0][...] = acc.astype(BF16)

    return matmul(name, a, b, a_spec=((T, tm), lambda j, i, k: (0, i)), b_spec=((T, tn), lambda j, i, k: (0, j)),
                  cdims=TN, grid=(n // tn, m // tm, 1), acc_shape=(tm, tn),
                  outs=[(out_shape, BF16, out_block, out_map)], epi=epi, after=after)[0]


def mm_dh2(dup, wup, x2, dx3, g2, after=None):
    def epi(acc, ex, outs, i):
        x2_ref, dx3_ref, g_ref = ex
        x2 = x2_ref[...]
        r = lax.rsqrt(jnp.mean(x2 * x2, axis=-1, keepdims=True) + EPS)
        xn = x2 * r
        _row_acc(outs[2], jnp.sum(acc * xn, axis=0, keepdims=True), i)
        dx2 = dx3_ref[...] + _rms_bwd(xn, r, acc * g_ref[...])
        outs[0][...] = dx2
        outs[1][...] = dx2.astype(BF16)

    y = square_matmul("mm_dh2", dup, wup, a_spec=((TBIG, D), lambda j, i, k: (i, k)),
                      b_spec=((None, TBIG, D), lambda j, i, k: (k, j, 0)), cdims=NT, nk=NCHIP, after=after)
    return rowwise("rows_dh2", y, extras=[(x2, *_rowblk), (dx3, *_rowblk), (g2, *_vec)],
                   outs=[((T, D), F32, *_rowblk), ((T, D), BF16, *_rowblk), ((1, D), F32, *_vec)], epi=epi)


def mm_dmixed(dx2b, wout, pcat, ylin, ygla, pscale, after=None):
    def epi(acc, ex, outs, i):
        lgp_ref, lgg_ref, ylin_ref, ygla_ref, ps_ref = ex
        gp = _sigmoid(lgp_ref[...].astype(F32))
        gg = _sigmoid(lgg_ref[...].astype(F32))
        yl = ylin_ref[...].astype(F32)
        ps = ps_ref[...]
        agp = acc * gp
        outs[0][...] = (agp * ps).astype(BF16)
        outs[1][...] = (acc * gg).astype(BF16)
        outs[2][...] = (agp * (yl * ps) * (1.0 - gp)).astype(BF16)
        outs[3][...] = (acc * ygla_ref[...].astype(F32) * gg * (1.0 - gg)).astype(BF16)
        _row_acc(outs[4], jnp.sum(agp * yl, axis=0, keepdims=True), i)

    return matmul("mm_dmixed", dx2b, wout, a_spec=_rowblk, b_spec=((D, D), lambda j, i, k: (0, 0)), cdims=NT,
                  grid=(1, T // TMF, 1), acc_shape=(TMF, D),
                  extras=[(pcat, *_full_spec(OGP // D)), (pcat, *_full_spec(OGG // D)), (ylin, *_rowblk), (ygla, *_rowblk),
                          (pscale, *_vec)],
                  outs=[((T, D), BF16, *_rowblk)] * 4 + [((1, D), F32, *_vec)], epi=epi, after=after)


def mm_dog(dygla, wgo, o, pcat, ng, after=None):
    def epi(acc, ex, outs, i):
        o_ref, g_ref, ng_ref = ex
        do_ref, dg_ref, gng_ref = outs
        gparts = []
        for h in range(HEADS):
            cv = slice(h * DV, (h + 1) * DV)
            oh = o_ref[:, cv].astype(F32)
            r = lax.rsqrt(jnp.mean(oh * oh, axis=-1, keepdims=True) + EPS)
            on = oh * r
            gv = g_ref[:, cv].astype(F32)
            sg = _sigmoid(gv)
            dgain = acc[:, cv] * (gv * sg)
            gparts.append(jnp.sum(dgain * on, axis=0, keepdims=True))
            ngh = ng_ref[:, cv]
            do_ref[:, cv] = _rms_bwd(on, r, dgain * ngh).astype(BF16)
            dg_ref[:, cv] = (acc[:, cv] * (on * ngh) * (sg * (1.0 + gv * (1.0 - sg)))).astype(BF16)
        _row_acc(gng_ref, jnp.concatenate(gparts, axis=1), i)

    return matmul("mm_dog", dygla, wgo, a_spec=_rowblk, b_spec=((D, D), lambda j, i, k: (0, 0)), cdims=NT,
                  grid=(1, T // TMF, 1), acc_shape=(TMF, D),
                  extras=[(o, *_rowblk), (pcat, *_full_spec(OG // D)), (ng, *_vec)],
                  outs=[((T, D), BF16, *_rowblk), ((T, D), BF16, *_rowblk), ((1, D), F32, *_vec)], epi=epi, after=after)


def mm_dh1(dpcat, wcat, x, dx2, g1, after=None):
    tk = 2304

    def epi(acc, ex, outs, i):
        x_ref, dx2_ref, g_ref = ex
        xv = x_ref[...]
        r = lax.rsqrt(jnp.mean(xv * xv, axis=-1, keepdims=True) + EPS)
        xn = xv * r
        _row_acc(outs[1], jnp.sum(acc * xn, axis=0, keepdims=True), i)
        outs[0][...] = dx2_ref[...] + _rms_bwd(xn, r, acc * g_ref[...])

    y = square_matmul("mm_dh1", dpcat, wcat, a_spec=((TBIG, tk), lambda j, i, k: (i, k)),
                      b_spec=((TBIG, tk), lambda j, i, k: (j, k)), cdims=NT, nk=NCAT // tk, after=after)
    return rowwise("rows_dh1", y, extras=[(x, *_rowblk), (dx2, *_rowblk), (g1, *_vec)],
                   outs=[((T, D), F32, *_rowblk), ((1, D), F32, *_vec)], epi=epi)


def _tile_rows(rows, cols, n_arrays):
    tm = rows
    while tm % 32 == 0 and 2 * n_arrays * tm * cols * 4 > 24 * 1024 * 1024:
        tm //= 2
    return tm


def add_pairs(name, parts, theirs, core):
    _, _, r, c = parts.shape
    tm = _tile_rows(r, c, 3)

    def body(core_ref, a_ref, b_ref, o_ref):
        o_ref[...] = (a_ref[...].astype(F32) + b_ref[...].astype(F32)).astype(BF16)

    spec = pl.BlockSpec((None, tm, c), lambda j, i, core_ref: (j, i, 0))
    grid_spec = pltpu.PrefetchScalarGridSpec(
        num_scalar_prefetch=1, grid=(NCHIP, r // tm),
        in_specs=[pl.BlockSpec((None, None, tm, c), lambda j, i, core_ref: (core_ref[0], j, i, 0)), spec], out_specs=spec)
    return pl.pallas_call(body, name=name, grid_spec=grid_spec, out_shape=SDS((NCHIP, r, c), BF16),
                          compiler_params=_cparams(40 * 1024 * 1024, ("arbitrary", "arbitrary")))(core, parts, theirs)


def sum_chips(name, sums, landed, chip):
    _, r, c = sums.shape
    tm = _tile_rows(r, c, 4)

    def body(chip_ref, own_ref, l_ref, o_ref):
        s = own_ref[...].astype(F32)
        for t in range(NCHIP - 1):
            s = s + l_ref[t].astype(F32)
        o_ref[...] = s

    grid_spec = pltpu.PrefetchScalarGridSpec(
        num_scalar_prefetch=1, grid=(r // tm,),
        in_specs=[pl.BlockSpec((None, tm, c), lambda i, chip_ref: (chip_ref[0], i, 0)),
                  pl.BlockSpec((NCHIP - 1, tm, c), lambda i, chip_ref: (0, i, 0))],
        out_specs=pl.BlockSpec((tm, c), lambda i, chip_ref: (i, 0)))
    return pl.pallas_call(body, name=name, grid_spec=grid_spec, out_shape=SDS((r, c), F32),
                          compiler_params=_cparams(40 * 1024 * 1024, ("arbitrary",)))(chip, sums, landed)


def _adamw_math(wv, gv, mv, vv):
    mn = ADAM_B1 * mv + (1.0 - ADAM_B1) * gv
    vn = ADAM_B2 * vv + (1.0 - ADAM_B2) * (gv * gv)
    mh = mn / (1.0 - ADAM_B1 ** ADAM_STEP)
    vh = vn / (1.0 - ADAM_B2 ** ADAM_STEP)
    return -ADAM_LR * (mh / (jnp.sqrt(vh) + ADAM_EPS) + ADAM_WD * wv), mn, vn


def adamw(name, w, g, m, v):
    def body(w_ref, g_ref, m_ref, v_ref, go_ref, d_ref, mo_ref, vo_ref):
        gv = g_ref[...]
        go_ref[...] = gv
        d_ref[...], mo_ref[...], vo_ref[...] = _adamw_math(w_ref[...], gv, m_ref[...], v_ref[...])

    return pl.pallas_call(body, name=name, out_shape=[SDS(w.shape, F32)] * 4)(w, g, m, v)


def adamw_halves(name, w, g_own, g_sib, m, v, core):
    _, r, c = w.shape
    tm = _tile_rows(r, c, 10)

    def body(core_ref, w_ref, go_ref, gs_ref, m_ref, v_ref, g_out, d_out, m_out, v_out):
        gv = jnp.where(pl.program_id(0) == core_ref[0], go_ref[...], gs_ref[...])
        g_out[...] = gv
        d_out[...], m_out[...], v_out[...] = _adamw_math(w_ref[...], gv, m_ref[...], v_ref[...])

    full = pl.BlockSpec((None, tm, c), lambda h, i, core_ref: (h, i, 0))
    own = pl.BlockSpec((tm, c), lambda h, i, core_ref: (jnp.where(h == core_ref[0], i, 0), 0))
    sib = pl.BlockSpec((tm, c), lambda h, i, core_ref: (jnp.where(h == core_ref[0], 0, i), 0))
    grid_spec = pltpu.PrefetchScalarGridSpec(num_scalar_prefetch=1, grid=(2, r // tm),
                                             in_specs=[full, own, sib, full, full], out_specs=[full] * 4)
    return pl.pallas_call(body, name=name, grid_spec=grid_spec, out_shape=[SDS(w.shape, F32)] * 4,
                          compiler_params=_cparams(48 * 1024 * 1024, ("arbitrary", "arbitrary")))(core, w, g_own, g_sib, m, v)


def pack_rows(name, parts, rows, after=None):
    width = parts[0].shape[1]
    n = len(parts)
    afters = _as_list(after)

    def body(*refs):
        out_ref = refs[n + len(afters)]
        out_ref[...] = jnp.zeros_like(out_ref)
        off = 0
        for p in refs[:n]:
            out_ref[off:off + p.shape[0], :] = p[...]
            off += p.shape[0]

    vm = pl.BlockSpec(memory_space=pltpu.VMEM)
    return pl.pallas_call(body, name=name, in_specs=[vm] * n + [ANY] * len(afters), out_specs=vm,
                          out_shape=SDS((rows, width), F32))(*parts, *afters)


def _place():
    x, y, c = lax.axis_index("x"), lax.axis_index("y"), lax.axis_index("c")
    chips = [(1 - x, y), (x, 1 - y), (1 - x, 1 - y)]
    return x, y, c, chips


def _row_split(shape, dtype):
    r, c = shape
    n = 1
    while r % (2 * n) == 0 and (r // (2 * n)) % 16 == 0 and (r // n) * c * jnp.dtype(dtype).itemsize > PIECE_BYTES:
        n *= 2
    return [pl.ds(s * (r // n), r // n) for s in range(n)]


def _pieces(ref):
    *lead, r, c = ref.shape
    split = _row_split((r, c), ref.dtype)
    return [ref.at[(*idx, s)] for idx in itertools.product(*[range(d) for d in lead]) for s in split]


HBM = pl.BlockSpec(memory_space=pltpu.HBM)
SEM = pl.BlockSpec(memory_space=pltpu.SEMAPHORE)
EFFECT = pltpu.SideEffectType.DATAFLOW_SIDE_EFFECTING


def gather_start(name, shards, after=None):
    n = len(shards)
    extra = [] if after is None else [after]

    def body(*refs):
        src, land = refs[:n], refs[n:2 * n]
        send, recv = refs[2 * n + len(extra)], refs[2 * n + len(extra) + 1]
        x, y, c, chips = _place()
        me = 2 * x + y
        for a in range(n):
            for j, (cx, cy) in enumerate(chips):
                for sp, dp in zip(_pieces(src[a].at[c]), _pieces(land[a].at[me, c])):
                    pltpu.make_async_remote_copy(sp, dp, send.at[3 * a + j], recv.at[3 * a + j],
                                                 device_id=(cx, cy, c), device_id_type=MESH).start()

    lands = [pltpu.with_memory_space_constraint(lax.empty((NCHIP,) + s.shape, s.dtype), pltpu.HBM) for s in shards]
    srcs = [pltpu.with_memory_space_constraint(s, pltpu.HBM) for s in shards]
    outs = pl.pallas_call(
        body, name=name,
        out_shape=(pltpu.SemaphoreType.DMA((3 * n,)), pltpu.SemaphoreType.DMA((3 * n,)),
                   *[pltpu.HBM(s.shape, s.dtype) for s in shards], *[pltpu.HBM(l.shape, l.dtype) for l in lands]),
        in_specs=[HBM] * (2 * n) + [ANY] * len(extra), out_specs=(SEM, SEM, *([HBM] * (2 * n))),
        input_output_aliases={i: 2 + i for i in range(2 * n)},
        compiler_params=pltpu.CompilerParams(has_side_effects=EFFECT),
    )(*srcs, *lands, *extra)
    return outs[0], outs[1], list(outs[2:2 + n]), list(outs[2 + n:2 + 2 * n])


def gather_wait(name, send, recv, shards, lands, after):
    n = len(shards)
    afters = _as_list(after)

    def body(*refs):
        src, land = refs[:n], refs[n:2 * n]
        send_ref, recv_ref = refs[2 * n], refs[2 * n + 1]
        x, y, c, chips = _place()
        for a in range(n):
            for j, (cx, cy) in enumerate(chips):
                cp = pltpu.make_async_remote_copy(src[a].at[c], land[a].at[2 * cx + cy, c], send_ref.at[3 * a + j],
                                                  recv_ref.at[3 * a + j], device_id=(cx, cy, c), device_id_type=MESH)
                cp.wait_send()
                cp.wait_recv()

    outs = pl.pallas_call(
        body, name=name,
        out_shape=(*[pltpu.HBM(s.shape, s.dtype) for s in shards], *[pltpu.HBM(l.shape, l.dtype) for l in lands]),
        in_specs=[HBM] * (2 * n) + [SEM, SEM] + [ANY] * len(afters), out_specs=[HBM] * (2 * n),
        input_output_aliases={i: i for i in range(2 * n)},
        compiler_params=pltpu.CompilerParams(has_side_effects=EFFECT),
    )(*shards, *lands, send, recv, *afters)
    return list(outs[:n]), list(outs[n:])


def forward_halves(name, shards, lands):
    n = len(lands)

    def body(*refs):
        had, buf = refs[:n], refs[n:2 * n]
        send, recv = refs[2 * n:]
        x, y, c, chips = _place()
        sib = (x, y, 1 - c)
        for a in range(n):
            for j, (cx, cy) in enumerate(chips):
                for sp, dp in zip(_pieces(had[a].at[2 * cx + cy, c]), _pieces(buf[a].at[2 * cx + cy, c])):
                    pltpu.make_async_remote_copy(sp, dp, send.at[3 * a + j], recv.at[3 * a + j], device_id=sib, device_id_type=MESH).start()
        for a in range(n):
            for j, (cx, cy) in enumerate(chips):
                pltpu.make_async_remote_copy(had[a].at[2 * cx + cy, c], buf[a].at[2 * cx + cy, 1 - c], send.at[3 * a + j],
                                             recv.at[3 * a + j], device_id=sib, device_id_type=MESH).wait()

    got = pl.pallas_call(
        body, name=name, in_specs=[ANY] * n, out_specs=[ANY] * n, out_shape=[SDS(l.shape, l.dtype) for l in lands],
        input_output_aliases={i: i for i in range(n)},
        scratch_shapes=[pltpu.SemaphoreType.DMA((3 * n,)), pltpu.SemaphoreType.DMA((3 * n,))],
    )(*lands)
    me = 2 * lax.axis_index("x") + lax.axis_index("y")
    return [lax.dynamic_update_index_in_dim(g, s, me, 0) for g, s in zip(got, shards)]


def exchange_start(name, parts):
    n = len(parts)

    def body(*refs):
        src, got = refs[:n], refs[n:2 * n]
        send, recv = refs[2 * n], refs[2 * n + 1]
        token = refs[4 * n + 2]
        x, y, c, _ = _place()
        sib = (x, y, 1 - c)
        for a in range(n):
            for sp, dp in zip(_pieces(src[a].at[1 - c]), _pieces(got[a])):
                pltpu.make_async_remote_copy(sp, dp, send.at[a], recv.at[a], device_id=sib, device_id_type=MESH).start()
        token[...] = jnp.zeros_like(token)

    lands = [pltpu.with_memory_space_constraint(lax.empty(p.shape[1:], p.dtype), pltpu.HBM) for p in parts]
    srcs = [pltpu.with_memory_space_constraint(p, pltpu.HBM) for p in parts]
    outs = pl.pallas_call(
        body, name=name,
        out_shape=(pltpu.SemaphoreType.DMA((n,)), pltpu.SemaphoreType.DMA((n,)),
                   *[pltpu.HBM(p.shape, p.dtype) for p in parts], *[pltpu.HBM(l.shape, l.dtype) for l in lands],
                   SDS((8, 128), F32)),
        in_specs=[HBM] * (2 * n), out_specs=(SEM, SEM, *([HBM] * (2 * n)), pl.BlockSpec(memory_space=pltpu.VMEM)),
        input_output_aliases={i: 2 + i for i in range(2 * n)},
        compiler_params=pltpu.CompilerParams(has_side_effects=EFFECT),
    )(*srcs, *lands)
    return outs[0], outs[1], list(outs[2:2 + n]), list(outs[2 + n:2 + 2 * n]), outs[2 + 2 * n]


def exchange_wait(name, send, recv, parts, lands, after):
    n = len(parts)
    afters = _as_list(after)

    def body(*refs):
        src, got = refs[:n], refs[n:2 * n]
        send_ref, recv_ref = refs[2 * n], refs[2 * n + 1]
        x, y, c, _ = _place()
        sib = (x, y, 1 - c)
        for a in range(n):
            cp = pltpu.make_async_remote_copy(src[a].at[1 - c], got[a], send_ref.at[a], recv_ref.at[a], device_id=sib, device_id_type=MESH)
            cp.wait_send()
            cp.wait_recv()

    outs = pl.pallas_call(
        body, name=name,
        out_shape=(*[pltpu.HBM(p.shape, p.dtype) for p in parts], *[pltpu.HBM(l.shape, l.dtype) for l in lands]),
        in_specs=[HBM] * (2 * n) + [SEM, SEM] + [ANY] * len(afters), out_specs=[HBM] * (2 * n),
        input_output_aliases={i: i for i in range(2 * n)},
        compiler_params=pltpu.CompilerParams(has_side_effects=EFFECT),
    )(*parts, *lands, send, recv, *afters)
    return list(outs[:n]), list(outs[n:])


def scatter_start(name, parts):
    n = len(parts)

    def body(*refs):
        src, land = refs[:n], refs[n:2 * n]
        send, recv = refs[2 * n], refs[2 * n + 1]
        token = refs[4 * n + 2]
        x, y, c, chips = _place()
        for a in range(n):
            for j, (cx, cy) in enumerate(chips):
                for sp, dp in zip(_pieces(src[a].at[2 * cx + cy]), _pieces(land[a].at[j])):
                    pltpu.make_async_remote_copy(sp, dp, send.at[3 * a + j], recv.at[3 * a + j],
                                                 device_id=(cx, cy, c), device_id_type=MESH).start()
        token[...] = jnp.zeros_like(token)

    lands = [pltpu.with_memory_space_constraint(lax.empty((NCHIP - 1,) + p.shape[1:], p.dtype), pltpu.HBM) for p in parts]
    srcs = [pltpu.with_memory_space_constraint(p, pltpu.HBM) for p in parts]
    outs = pl.pallas_call(
        body, name=name,
        out_shape=(pltpu.SemaphoreType.DMA((3 * n,)), pltpu.SemaphoreType.DMA((3 * n,)),
                   *[pltpu.HBM(p.shape, p.dtype) for p in parts], *[pltpu.HBM(l.shape, l.dtype) for l in lands],
                   SDS((8, 128), F32)),
        in_specs=[HBM] * (2 * n), out_specs=(SEM, SEM, *([HBM] * (2 * n)), pl.BlockSpec(memory_space=pltpu.VMEM)),
        input_output_aliases={i: 2 + i for i in range(2 * n)},
        compiler_params=pltpu.CompilerParams(has_side_effects=EFFECT),
    )(*srcs, *lands)
    return outs[0], outs[1], list(outs[2:2 + n]), list(outs[2 + n:2 + 2 * n]), outs[2 + 2 * n]


def scatter_wait(name, send, recv, parts, lands, after):
    n = len(parts)
    afters = _as_list(after)

    def body(*refs):
        src, land = refs[:n], refs[n:2 * n]
        send_ref, recv_ref = refs[2 * n], refs[2 * n + 1]
        x, y, c, chips = _place()
        for a in range(n):
            for j, (cx, cy) in enumerate(chips):
                cp = pltpu.make_async_remote_copy(src[a].at[2 * cx + cy], land[a].at[j], send_ref.at[3 * a + j], recv_ref.at[3 * a + j],
                                                  device_id=(cx, cy, c), device_id_type=MESH)
                cp.wait_send()
                cp.wait_recv()

    outs = pl.pallas_call(
        body, name=name,
        out_shape=(*[pltpu.HBM(p.shape, p.dtype) for p in parts], *[pltpu.HBM(l.shape, l.dtype) for l in lands]),
        in_specs=[HBM] * (2 * n) + [SEM, SEM] + [ANY] * len(afters), out_specs=[HBM] * (2 * n),
        input_output_aliases={i: i for i in range(2 * n)},
        compiler_params=pltpu.CompilerParams(has_side_effects=EFFECT),
    )(*parts, *lands, send, recv, *afters)
    return list(outs[:n]), list(outs[n:])


def join_start(name, halves):
    n = len(halves)

    def body(*refs):
        src, dst = refs[:n], refs[n:2 * n]
        send, recv = refs[2 * n], refs[2 * n + 1]
        token = refs[4 * n + 2]
        x, y, c, _ = _place()
        sib = (x, y, 1 - c)
        for a in range(n):
            for sp, dp in zip(_pieces(src[a]), _pieces(dst[a])):
                pltpu.make_async_remote_copy(sp, dp, send.at[a], recv.at[a], device_id=sib, device_id_type=MESH).start()
        token[...] = jnp.zeros_like(token)

    lands = [pltpu.with_memory_space_constraint(lax.empty(h.shape, h.dtype), pltpu.HBM) for h in halves]
    srcs = [pltpu.with_memory_space_constraint(h, pltpu.HBM) for h in halves]
    outs = pl.pallas_call(
        body, name=name,
        out_shape=(pltpu.SemaphoreType.DMA((n,)), pltpu.SemaphoreType.DMA((n,)),
                   *[pltpu.HBM(h.shape, h.dtype) for h in halves], *[pltpu.HBM(l.shape, l.dtype) for l in lands],
                   SDS((8, 128), F32)),
        in_specs=[HBM] * (2 * n), out_specs=(SEM, SEM, *([HBM] * (2 * n)), pl.BlockSpec(memory_space=pltpu.VMEM)),
        input_output_aliases={i: 2 + i for i in range(2 * n)},
        compiler_params=pltpu.CompilerParams(has_side_effects=EFFECT),
    )(*srcs, *lands)
    return outs[0], outs[1], list(outs[2:2 + n]), list(outs[2 + n:2 + 2 * n]), outs[2 + 2 * n]


def join_wait(name, send, recv, halves, lands, after):
    n = len(halves)
    afters = _as_list(after)

    def body(*refs):
        src, dst = refs[:n], refs[n:2 * n]
        send_ref, recv_ref = refs[2 * n], refs[2 * n + 1]
        x, y, c, _ = _place()
        sib = (x, y, 1 - c)
        for a in range(n):
            cp = pltpu.make_async_remote_copy(src[a], dst[a], send_ref.at[a], recv_ref.at[a], device_id=sib, device_id_type=MESH)
            cp.wait_send()
            cp.wait_recv()

    outs = pl.pallas_call(
        body, name=name,
        out_shape=(*[pltpu.HBM(h.shape, h.dtype) for h in halves], *[pltpu.HBM(l.shape, l.dtype) for l in lands]),
        in_specs=[HBM] * (2 * n) + [SEM, SEM] + [ANY] * len(afters), out_specs=[HBM] * (2 * n),
        input_output_aliases={i: i for i in range(2 * n)},
        compiler_params=pltpu.CompilerParams(has_side_effects=EFFECT),
    )(*halves, *lands, send, recv, *afters)
    return list(outs[:n]), list(outs[n:])


def gather_small(name, xs, reduce, after=None):
    m, ncol = xs.shape
    afters = _as_list(after)

    def body(x_ref, *rest):
        out_ref, all_ref, send, recv, lsem = rest[len(afters):]
        x, y, c, chips = _place()
        me, sib = (x, y, c), (x, y, 1 - c)

        def rows(px, py, pc):
            return all_ref.at[pl.ds((4 * px + 2 * py + pc) * m, m), :]

        def copy(k, block, to, src=None):
            return pltpu.make_async_remote_copy(rows(*block) if src is None else src, rows(*block), send.at[k], recv.at[k],
                                                device_id=to, device_id_type=MESH)

        mine = pltpu.make_async_copy(x_ref, rows(*me), lsem)
        mine.start()
        first = [copy(0, me, sib, src=x_ref)] + [copy(1 + j, me, (*chip, c), src=x_ref) for j, chip in enumerate(chips)]
        for cp in first:
            cp.start()
        passed = [copy(4 + j, (*chip, c), sib) for j, chip in enumerate(chips)]
        for j, chip in enumerate(chips):
            copy(1 + j, (*chip, c), me).wait_recv()
            passed[j].start()
        copy(0, sib, me).wait_recv()
        for j, chip in enumerate(chips):
            copy(4 + j, (*chip, 1 - c), me).wait_recv()
        for cp in first + passed:
            cp.wait_send()
        mine.wait()
        if reduce:
            s = all_ref[0:m, :]
            for dev in range(1, 8):
                s = s + all_ref[dev * m:(dev + 1) * m, :]
            out_ref[...] = s
        else:
            out_ref[...] = all_ref[...]

    vm = pl.BlockSpec(memory_space=pltpu.VMEM)
    return pl.pallas_call(
        body, name=name, in_specs=[vm] + [ANY] * len(afters), out_specs=vm,
        out_shape=SDS((m, ncol) if reduce else (8 * m, ncol), F32),
        scratch_shapes=[pltpu.VMEM((8 * m, ncol), F32), pltpu.SemaphoreType.DMA((7,)), pltpu.SemaphoreType.DMA((7,)),
                        pltpu.SemaphoreType.DMA],
    )(xs, *afters)


RELAYOUT_ROWS = 128


def weights_to_cat(g_in):
    tm = RELAYOUT_ROWS

    def body(g_ref, o_ref):
        nat = jnp.concatenate([g_ref[j] for j in range(NCHIP)], axis=1)
        pad = jnp.zeros((tm, NCAT - OA - 16), BF16)
        o_ref[...] = jnp.concatenate([nat[:, 3072:7168], nat[:, 7184:11280], nat[:, 0:3072], nat[:, 7168:7184], pad], axis=1)

    return pl.pallas_call(
        body, name="weights_to_cat", grid=(D // tm,), in_specs=[pl.BlockSpec((NCHIP, tm, IN_SHARD), lambda i: (0, i, 0))],
        out_specs=pl.BlockSpec((tm, NCAT), lambda i: (i, 0)), out_shape=SDS((D, NCAT), BF16),
        compiler_params=_cparams(40 * 1024 * 1024, ("arbitrary",)),
    )(g_in)


def grads_from_cat(gw_cat):
    tm = RELAYOUT_ROWS
    nb = (D // 2) // tm

    def body(c_ref, o_ref):
        cat = c_ref[...]
        nat = jnp.concatenate([cat[:, OU:OA], cat[:, OV:OGP], cat[:, OA:OA + 16], cat[:, OGP:OU]], axis=1)
        for j in range(NCHIP):
            o_ref[j] = nat[:, j * IN_SHARD:(j + 1) * IN_SHARD]

    return pl.pallas_call(
        body, name="grads_from_cat", grid=(D // tm,), in_specs=[pl.BlockSpec((tm, NCAT), lambda i: (i, 0))],
        out_specs=pl.BlockSpec((None, NCHIP, tm, IN_SHARD), lambda i: (i // nb, 0, i % nb, 0)),
        out_shape=SDS((2, NCHIP, D // 2, IN_SHARD), BF16), compiler_params=_cparams(40 * 1024 * 1024, ("arbitrary",)),
    )(gw_cat)


def _pad_rows(a, rows):
    return jnp.concatenate([a, jnp.zeros((rows - a.shape[0],) + a.shape[1:], a.dtype)], axis=0)


def local_step(x2d, tgt, gf, g1, pool_scale, wa_pad, b_alpha, ng, g2, get_w, on_grad=None, on_settle=None):
    emit = on_grad if on_grad is not None else (lambda group, grads: None)
    settle = on_settle if on_settle is not None else (lambda group, after: None)
    h1 = norm1(x2d, g1)
    wcat, pw = get_w("in", h1)
    pcat = mm_in(h1, wcat)
    dpool, ylin = pool_fwd(pcat, pw)
    og, o, states = gla_fwd(pcat, wa_pad, b_alpha, ng)
    w_go, w_o = get_w("mid", og)
    mixed, ygla = mm_gla_out(og, w_go, ylin, pcat, pool_scale)
    x2, h2 = mm_out(mixed, w_o, x2d, g2)
    w_up = get_w("up", h2)
    rup, act = mm_up(h2, w_up)
    w_dn = get_w("down", act)
    dx3, dx3b, g_nf, loss_row = mm_down(act, w_dn, x2, tgt, gf)

    gw_down = mm_wgrad("mm_dw_down", act, dx3b, DFF, D, (2, NCHIP, D // 2, D), (None, None, 512, D),
                       lambda j, i, k: ((i // 2) % 2, i // 4, i % 2, 0), 512, D)
    token = emit("down", {"down": gw_down})
    dup = mm_dact(dx3b, w_dn, rup, after=token)
    token = settle("down", dup)
    dx2, dx2b, g_mlp = mm_dh2(dup, w_up, x2, dx3, g2, after=token)
    gw_up = mm_wgrad("mm_dw_up", h2, dup, D, DFF, (2, NCHIP, D // 2, D), (None, None, 512, D),
                     lambda j, i, k: (i // 2, j, i % 2, 0), 512, D)
    token = emit("up", {"up": gw_up})
    dylin, dygla, dlgp, dlgg, g_ps = mm_dmixed(dx2b, w_o, pcat, ylin, ygla, pool_scale, after=token)
    token = settle("up", dylin)
    gw_out = mm_wgrad("mm_dw_out", mixed, dx2b, D, D, (2, NCHIP, 256, D), (None, None, 256, D),
                      lambda j, i, k: (i % 2, i // 2, 0, 0), 256, D)
    do, dg, g_ng = mm_dog(dygla, w_go, o, pcat, ng, after=token)
    gw_go = mm_wgrad("mm_dw_gla_out", og, dygla, D, D, (2, NCHIP, 256, D), (None, None, 256, D),
                     lambda j, i, k: (i % 2, i // 2, 0, 0), 256, D)
    token = emit("mix", {"out": gw_out, "gla_out": gw_go})
    dq, dk, dv, dalow, g_wa, g_ba = gla_bwd(do, pcat, states, wa_pad, b_alpha, b_alpha if token is None else token)
    token = settle("mix", dq)
    du, dpw = pool_bwd(dylin, dpool, pw)
    dpcat = jnp.concatenate([dv, dg, dlgp, dlgg, du, dq, dk, dalow, jnp.zeros((T, NCAT - OA - APAD), BF16)], axis=1)
    gw_cat = mm_wgrad("mm_dw_in", h1, dpcat, D, NCAT, (D, NCAT), (1024, 1280), lambda j, i, k: (i, j), 1024, 1280, after=token)
    token = settle("in", emit("in", {"in_cat": gw_cat, "pool": dpw}))
    grad_x, g_mix = mm_dh1(dpcat, wcat, x2d, dx2, g1, after=token)
    return (loss_row[0, 0], grad_x, g_mix, g_ps, g_mlp, g_nf, g_ng, g_ba, g_wa, token,
            gw_cat, dpw, gw_go, gw_out, gw_up, gw_down)


def kernel(x, norm_mix_g, w_in, pool_w, pool_scale, w_alpha, b_alpha, gla_norm_g, w_gla_out, w_out, norm_mlp_g, w_mlp_up, w_mlp_down, norm_final_g, loss_target, m_norm_mix_g, m_w_in, m_pool_w, m_pool_scale, m_w_alpha, m_b_alpha, m_gla_norm_g, m_w_gla_out, m_w_out, m_norm_mlp_g, m_w_mlp_up, m_w_mlp_down, m_norm_final_g, v_norm_mix_g, v_w_in, v_pool_w, v_pool_scale, v_w_alpha, v_b_alpha, v_gla_norm_g, v_w_gla_out, v_w_out, v_norm_mlp_g, v_w_mlp_up, v_w_mlp_down, v_norm_final_g):
    chip = 2 * lax.axis_index("x") + lax.axis_index("y")
    chip_i = chip.astype(jnp.int32).reshape(1)
    core_i = lax.axis_index("c").astype(jnp.int32).reshape(1)
    x2d = x.reshape(T, D)
    tgt = loss_target.reshape(T, D)
    gf = norm_final_g.reshape(1, D)

    def halves(w2d):
        r, c = w2d.shape
        return w2d.astype(BF16).reshape(2, r // 2, c)

    pool_shard = pool_w.reshape(4 * PG, PO // NCHIP)
    big = [w_in[0], w_gla_out[0], w_out[0], w_mlp_up[0], w_mlp_down[0], pool_shard]
    groups = {"in": [big[0], big[5]], "mid": [big[1], big[2]], "up": [big[3]], "down": [big[4]]}
    sent = {g: [halves(w) for w in ws] for g, ws in groups.items()}
    started = {}

    def start(group, after=None):
        started[group] = gather_start("gather_start_" + group, sent[group], after)

    start("in")
    w_in_r, m_in_r, v_in_r = [a.reshape(2, D // 2, IN_SHARD)
                              for a in lax.optimization_barrier((w_in, m_w_in, v_w_in, started["in"][2][0]))[:3]]

    def get_w(group, after):
        send, recv, shards, lands = started[group]
        if group == "in":
            after = [after, w_in_r, m_in_r, v_in_r, *sent["mid"], *sent["up"], *sent["down"], wa_pad]
        shards, lands = gather_wait("gather_wait_" + group, send, recv, shards, lands, after)
        if group == "in":
            start("mid", lands[0])
            start("up", started["mid"][3][0])
        if group == "mid":
            start("down", lands[0])
        whole = forward_halves("forward_" + group, shards, lands)
        if group == "in":
            g_in, g_pool = whole
            wcat = weights_to_cat(g_in.reshape(NCHIP, D, IN_SHARD))
            pw = jnp.concatenate([g_pool[j].reshape(4, PG, PO // NCHIP) for j in range(NCHIP)], axis=2)
            return wcat, pw
        if group == "mid":
            return whole[0].reshape(D, D), whole[1].reshape(D, D)
        if group == "up":
            return whole[0].reshape(NCHIP, D, D)
        return whole[0].reshape(DFF, D)

    small_w = pack_rows("pack_small_w", [w_alpha[0].reshape(4, QK),
                                         jnp.concatenate([gla_norm_g[0].reshape(1, 512), jnp.zeros((1, 512), F32)], axis=1)], 8)
    sw_all = gather_small("gather_small_w", small_w, False).reshape(8, 8, QK)
    wa_full = jnp.concatenate([sw_all[2 * j, 0:4].reshape(16, DK) for j in range(NCHIP)], axis=1)
    ng_full = jnp.concatenate([sw_all[2 * j, 4, 0:512].reshape(HEADS, DV // NCHIP) for j in range(NCHIP)], axis=1)
    wa_pad = _pad_rows(wa_full, APAD).astype(BF16)
    ng = ng_full.reshape(1, D)

    pending = {}
    wmv = {"in": (w_in_r, m_in_r, v_in_r), "gla_out": (big[1], m_w_gla_out, v_w_gla_out), "out": (big[2], m_w_out, v_w_out),
           "up": (big[3], m_w_mlp_up, v_w_mlp_up), "down": (big[4], m_w_mlp_down, v_w_mlp_down), "pool": (big[5], m_pool_w, v_pool_w)}
    big_res = {}

    def reduce_group(group, after):
        nms, send, recv, sums, lands = pending[group]
        sums, lands = scatter_wait("scatter_wait_" + group, send, recv, sums, lands, after)
        reduced = [sum_chips("sum_chips_" + nm, a, b, chip_i) for nm, a, b in zip(nms, sums, lands)]
        send, recv, reduced, lands, token = join_start("join_start_" + group, reduced)
        pending[group] = (nms, send, recv, reduced, lands)
        return token

    def update_group(group, after):
        nms, send, recv, reduced, lands = pending[group]
        reduced, from_sib = join_wait("join_wait_" + group, send, recv, reduced, lands, after)
        for nm, g_own, g_sib in zip(nms, reduced, from_sib):
            w, m, v = wmv[nm]
            shp = (2,) + g_own.shape
            big_res[nm] = adamw_halves("adamw_" + nm, w.reshape(shp), g_own, g_sib, m.reshape(shp), v.reshape(shp), core_i)

    def on_grad(group, grads):
        if group == "in":
            gw_in = grads_from_cat(grads["in_cat"])
            gw_pool = jnp.stack([grads["pool"][:, :, j * 128:(j + 1) * 128].reshape(2, 2 * PG, 128)
                                 for j in range(NCHIP)], axis=1)
            grads = {"in": gw_in, "pool": gw_pool}
        nms, parts = list(grads.keys()), list(grads.values())
        send, recv, parts, got, token = exchange_start("exchange_start_" + group, parts)
        pending[group] = (nms, send, recv, parts, got)
        return token

    def on_settle(group, after):
        if group == "in":
            after = reduce_group("down", after)
        nms, send, recv, parts, got = pending[group]
        parts, got = exchange_wait("exchange_wait_" + group, send, recv, parts, got, after)
        sums = [add_pairs("add_pair_" + nm, a, b, core_i) for nm, a, b in zip(nms, parts, got)]
        send, recv, sums, lands, token = scatter_start("scatter_start_" + group, sums)
        pending[group] = (nms, send, recv, sums, lands)
        if group != "in":
            return token
        token = reduce_group("up", token)
        token = reduce_group("mix", token)
        for earlier in ("down", "up", "mix"):
            update_group(earlier, token)
            token = big_res[pending[earlier][0][-1]][1]
        return [big_res[nm][1] for nm in ("down", "up", "out", "gla_out")]

    (loss_local, grad_x, g_mix, g_ps, g_mlp, g_nf, g_ng, g_ba, g_wa) = local_step(
        x2d, tgt, gf, norm_mix_g, pool_scale, wa_pad, b_alpha, ng, norm_mlp_g, get_w, on_grad, on_settle)[:9]
    loss = lax.psum(loss_local, ("x", "y", "c"))
    join_in_token = reduce_group("in", grad_x)

    ROWS = 16

    def wide(a, n):
        return jnp.concatenate([a.reshape(1, n), jnp.zeros((1, D - n), F32)], axis=1)

    packed = pack_rows("pack_small_g", [g_mix, g_ps, g_mlp, g_nf, g_ng, wide(g_ba, QK), g_wa[0:16].reshape(8, D)], ROWS)
    tot = gather_small("reduce_small_g", packed, True, join_in_token)
    t_wa = lax.dynamic_slice(tot[6:14].reshape(16, QK), (0, chip * DK), (16, DK))
    t_ng = lax.dynamic_slice(tot[4].reshape(HEADS, DV), (0, chip * (DV // NCHIP)), (HEADS, DV // NCHIP))

    def pack_small(nm, mix, ps, mlp, nf, ba, wa, gn, after=None):
        return pack_rows(nm, [mix.reshape(1, D), ps.reshape(1, D), mlp.reshape(1, D), nf.reshape(1, D), wide(ba, QK),
                              wa.reshape(2, D), wide(gn, 512)], ROWS, after)

    update_group("in", tot)
    sg = pack_small("pack_g", tot[0], tot[1], tot[2], tot[3], tot[5, 0:QK], t_wa, t_ng, big_res["in"][3])
    sw = pack_small("pack_w", norm_mix_g, pool_scale, norm_mlp_g, norm_final_g, b_alpha, w_alpha, gla_norm_g)
    sm = pack_small("pack_m", m_norm_mix_g, m_pool_scale, m_norm_mlp_g, m_norm_final_g, m_b_alpha, m_w_alpha, m_gla_norm_g)
    sv = pack_small("pack_v", v_norm_mix_g, v_pool_scale, v_norm_mlp_g, v_norm_final_g, v_b_alpha, v_w_alpha, v_gla_norm_g)
    small_res = adamw("adamw_small", sw, sg, sm, sv)

    def unpack(p):
        return {"norm_mix_g": p[0].reshape(1, D), "pool_scale": p[1].reshape(1, D), "norm_mlp_g": p[2].reshape(1, D),
                "norm_final_g": p[3].reshape(D), "b_alpha": p[4, 0:QK].reshape(1, QK), "w_alpha": p[5:7].reshape(1, 16, DK),
                "gla_norm_g": p[7, 0:512].reshape(1, HEADS, DV // NCHIP)}

    order = ["norm_mix_g", "w_in", "pool_w", "pool_scale", "w_alpha", "b_alpha", "gla_norm_g", "w_gla_out", "w_out",
             "norm_mlp_g", "w_mlp_up", "w_mlp_down", "norm_final_g"]
    big_key = {"w_in": ("in", w_in.shape), "pool_w": ("pool", pool_w.shape), "w_gla_out": ("gla_out", w_gla_out.shape),
               "w_out": ("out", w_out.shape), "w_mlp_up": ("up", w_mlp_up.shape), "w_mlp_down": ("down", w_mlp_down.shape)}
    result = [loss, grad_x.reshape(1, T, D)]
    for kind in range(4):
        small = unpack(small_res[kind])
        for nm in order:
            if nm in big_key:
                key, shp = big_key[nm]
                result.append(big_res[key][kind].reshape(shp))
            else:
                result.append(small[nm])
    return tuple(result)
```

```python
import itertools

import jax
import jax.numpy as jnp
from jax import lax
from jax.experimental import pallas as pl
from jax.experimental.pallas import tpu as pltpu

F32 = jnp.float32
BF16 = jnp.bfloat16
SDS = jax.ShapeDtypeStruct
MESH = pl.DeviceIdType.MESH
ANY = pl.BlockSpec(memory_space=pl.ANY)

T = 2048
D = 2048
DFF = 8192
NCHIP = 4
IN_WIDTH = 11280
IN_SHARD = IN_WIDTH // NCHIP
CHUNK = 64
NCHUNK = T // CHUNK
HEADS = 4
DK = 256
DV = 512
QK = HEADS * DK
EPS = 1e-6
POOL_WINDOWS = (2, 4, 8, 16)
PG = 256
PO = 512

OV, OG, OGP, OGG, OU, OQ, OKK, OA = 0, 2048, 4096, 6144, 8192, 9216, 10240, 11264
NCAT = 11520
APAD = 128

VMEM_CAP = 56 * 1024 * 1024

PIECE_BYTES = 384 * 1024

ADAM_LR, ADAM_B1, ADAM_B2, ADAM_EPS, ADAM_WD, ADAM_STEP = 0.001, 0.9, 0.999, 1e-08, 0.01, 10


def _cparams(vmem_bytes=None, sem=None):
    kw = {}
    if vmem_bytes is not None:
        kw["vmem_limit_bytes"] = int(min(max(vmem_bytes, 32 * 1024 * 1024), VMEM_CAP))
    if sem is not None:
        kw["dimension_semantics"] = sem
    return pltpu.CompilerParams(**kw)


def _nbytes(shape, dtype):
    n = 1
    for s in shape:
        if s is not None:
            n *= s
    return n * jnp.dtype(dtype).itemsize


def _sigmoid(x):
    return 1.0 / (1.0 + jnp.exp(-x))


def _as_list(after):
    if after is None:
        return []
    return list(after) if isinstance(after, (list, tuple)) else [after]


def matmul(name, a, b, *, a_spec, b_spec, cdims, grid, acc_shape, outs, extras=(), epi, after=None):
    nj, ni, nk = grid
    ne, no = len(extras), len(outs)
    afters = _as_list(after)
    first_out = 2 + ne + len(afters)

    def body(*refs):
        a_ref, b_ref = refs[0], refs[1]
        ex = refs[2:2 + ne]
        out_refs = refs[first_out:first_out + no]
        i = pl.program_id(1)
        part = lax.dot_general(a_ref[...], b_ref[...], (cdims, ((), ())), preferred_element_type=F32)
        if nk == 1:
            epi(part, ex, out_refs, i)
        else:
            acc_ref = refs[first_out + no]
            k = pl.program_id(2)

            @pl.when(k == 0)
            def _():
                acc_ref[...] = part

            @pl.when(k > 0)
            def _():
                acc_ref[...] += part

            @pl.when(k == nk - 1)
            def _():
                epi(acc_ref[...], ex, out_refs, i)

    in_specs = [pl.BlockSpec(*a_spec), pl.BlockSpec(*b_spec)] + [pl.BlockSpec(bs, im) for _, bs, im in extras]
    in_specs += [ANY] * len(afters)
    out_specs = [pl.BlockSpec(bs, im) for _, _, bs, im in outs]
    out_shape = [SDS(s, dt) for s, dt, _, _ in outs]
    vm = 2 * (_nbytes(a_spec[0], a.dtype) + _nbytes(b_spec[0], b.dtype))
    vm += 2 * sum(_nbytes(bs, arr.dtype) for arr, bs, _ in extras)
    vm += 2 * sum(_nbytes(bs, dt) for _, dt, bs, _ in outs)
    vm += 6 * _nbytes(acc_shape, F32)
    scratch = [pltpu.VMEM(acc_shape, F32)] if nk > 1 else []
    return pl.pallas_call(
        body, name=name, grid=grid, in_specs=in_specs, out_specs=out_specs, out_shape=out_shape,
        scratch_shapes=scratch,
        compiler_params=_cparams(vm, ("arbitrary", "arbitrary", "arbitrary")),
    )(a, b, *[arr for arr, _, _ in extras], *afters)


NN =((1,), (0,))
NT = ((1,), (1,))
TN = ((0,), (0,))


def _row_acc(out_ref, val, i):
    @pl.when(i == 0)
    def _():
        out_ref[...] = val

    @pl.when(i > 0)
    def _():
        out_ref[...] += val


def _rms_bwd(xn, r, dxn):
    return r * (dxn - xn * jnp.mean(dxn * xn, axis=-1, keepdims=True))


def norm1(x, g):
    tm = 256

    def body(x_ref, g_ref, h_ref):
        xv = x_ref[...]
        r = lax.rsqrt(jnp.mean(xv * xv, axis=-1, keepdims=True) + EPS)
        h_ref[...] = (xv * r * g_ref[...]).astype(BF16)

    return pl.pallas_call(
        body, name="norm1", grid=(T // tm,),
        in_specs=[pl.BlockSpec((tm, D), lambda i: (i, 0)), pl.BlockSpec((1, D), lambda i: (0, 0))],
        out_specs=pl.BlockSpec((tm, D), lambda i: (i, 0)), out_shape=SDS((T, D), BF16),
        compiler_params=_cparams(32 * 1024 * 1024, ("arbitrary",)),
    )(x, g)


def mm_in(h1, wcat):
    tm, tn = 1024, 1280

    def epi(acc, ex, outs, i):
        outs[0][...] = acc.astype(BF16)

    return matmul("mm_in", h1, wcat, a_spec=((tm, D), lambda j, i, k: (i, 0)), b_spec=((D, tn), lambda j, i, k: (0, j)),
                  cdims=NN, grid=(NCAT // tn, T // tm, 1), acc_shape=(tm, tn),
                  outs=[((T, NCAT), BF16, (tm, tn), lambda j, i, k: (i, j))], epi=epi)[0]


def _window_sum(x, w, up):
    n = x.shape[0]
    row = lax.broadcasted_iota(jnp.int32, x.shape, 0)
    s, sh = x, 1
    while sh < w:
        if up:
            s = s + jnp.where(row < n - sh, pltpu.roll(s, n - sh, axis=0), 0.0)
        else:
            s = s + jnp.where(row >= sh, pltpu.roll(s, sh, axis=0), 0.0)
        sh *= 2
    return s


def _inv_count(shape, w):
    row = lax.broadcasted_iota(jnp.int32, shape, 0)
    return 1.0 / jnp.minimum(row + 1, w).astype(F32)


def pool_fwd(pcat, pw):
    def body(u_ref, pw_ref, d_ref, y_ref):
        for gi, w in enumerate(POOL_WINDOWS):
            ug = u_ref[:, gi * PG:(gi + 1) * PG].astype(F32)
            dg = _window_sum(ug, w, False) * _inv_count(ug.shape, w) - ug
            db = dg.astype(BF16)
            d_ref[:, gi * PG:(gi + 1) * PG] = db
            y_ref[:, gi * PO:(gi + 1) * PO] = jnp.dot(db, pw_ref[gi], preferred_element_type=F32).astype(BF16)

    return pl.pallas_call(
        body, name="pool_fwd", grid=(1,),
        in_specs=[pl.BlockSpec((T, 4 * PG), lambda i: (0, OU // (4 * PG))), pl.BlockSpec((4, PG, PO), lambda i: (0, 0, 0))],
        out_specs=[pl.BlockSpec((T, 4 * PG), lambda i: (0, 0)), pl.BlockSpec((T, D), lambda i: (0, 0))],
        out_shape=[SDS((T, 4 * PG), BF16), SDS((T, D), BF16)],
        compiler_params=_cparams(48 * 1024 * 1024, ("arbitrary",)),
    )(pcat, pw)


def pool_bwd(dylin, d, pw):
    def body(dy_ref, d_ref, pw_ref, du_ref, dpw_ref):
        for gi, w in enumerate(POOL_WINDOWS):
            dyl = dy_ref[:, gi * PO:(gi + 1) * PO]
            dd = lax.dot_general(dyl, pw_ref[gi], (NT, ((), ())), preferred_element_type=F32)
            du = _window_sum(dd * _inv_count(dd.shape, w), w, True) - dd
            du_ref[:, gi * PG:(gi + 1) * PG] = du.astype(BF16)
            dpw_ref[gi] = lax.dot_general(d_ref[:, gi * PG:(gi + 1) * PG], dyl, (TN, ((), ())),
                                          preferred_element_type=F32).astype(BF16)

    return pl.pallas_call(
        body, name="pool_bwd", grid=(1,),
        in_specs=[pl.BlockSpec((T, D), lambda i: (0, 0)), pl.BlockSpec((T, 4 * PG), lambda i: (0, 0)),
                  pl.BlockSpec((4, PG, PO), lambda i: (0, 0, 0))],
        out_specs=[pl.BlockSpec((T, 4 * PG), lambda i: (0, 0)), pl.BlockSpec((4, PG, PO), lambda i: (0, 0, 0))],
        out_shape=[SDS((T, 4 * PG), BF16), SDS((4, PG, PO), BF16)],
        compiler_params=_cparams(48 * 1024 * 1024, ("arbitrary",)),
    )(dylin, d, pw)


def _gate_decay(alow, wa, ba):
    a = jnp.dot(alow, wa, preferred_element_type=F32) + ba
    ls = jax.nn.log_sigmoid(a) * (1.0 / 16.0)
    r = lax.broadcasted_iota(jnp.int32, (CHUNK, CHUNK), 0)
    c = lax.broadcasted_iota(jnp.int32, (CHUNK, CHUNK), 1)
    tri = jnp.where(c <= r, 1.0, 0.0).astype(F32)
    cum = jnp.dot(tri, ls, preferred_element_type=F32, precision=lax.Precision.HIGHEST)
    last = cum[CHUNK - 1:CHUNK, :]
    return a, jnp.exp(last - cum), jnp.exp(last)


def gla_fwd(pcat, wa, ba, ng):
    def body(q_ref, k_ref, v_ref, g_ref, al_ref, wa_ref, ba_ref, ng_ref, og_ref, o_ref, st_ref, s_scr):
        @pl.when(pl.program_id(0) == 0)
        def _():
            s_scr[...] = jnp.zeros_like(s_scr)

        _, e, decay = _gate_decay(al_ref[...], wa_ref[...], ba_ref[...])
        kd = (k_ref[...].astype(F32) * e).astype(BF16)
        qs = (q_ref[...].astype(F32) * (DK ** -0.5)).astype(BF16)
        for h in range(HEADS):
            ck = slice(h * DK, (h + 1) * DK)
            cv = slice(h * DV, (h + 1) * DV)
            s_new = s_scr[h] * decay[:, ck] + lax.dot_general(v_ref[:, cv], kd[:, ck], (TN, ((), ())),
                                                               preferred_element_type=F32)
            s_scr[h] = s_new
            sb = s_new.astype(BF16)
            st_ref[h] = sb
            oh = lax.dot_general(qs[:, ck], sb, (NT, ((), ())), preferred_element_type=F32)
            o_ref[:, cv] = oh.astype(BF16)
            on = oh * lax.rsqrt(jnp.mean(oh * oh, axis=-1, keepdims=True) + EPS) * ng_ref[:, cv]
            gv = g_ref[:, cv].astype(F32)
            og_ref[:, cv] = (on * (gv * _sigmoid(gv))).astype(BF16)

    row = lambda c: (c, 0)
    return pl.pallas_call(
        body, name="gla_fwd", grid=(NCHUNK,),
        in_specs=[pl.BlockSpec((CHUNK, QK), lambda c: (c, OQ // QK)), pl.BlockSpec((CHUNK, QK), lambda c: (c, OKK // QK)),
                  pl.BlockSpec((CHUNK, D), lambda c: (c, OV // D)), pl.BlockSpec((CHUNK, D), lambda c: (c, OG // D)),
                  pl.BlockSpec((CHUNK, APAD), lambda c: (c, OA // APAD)),
                  pl.BlockSpec((APAD, QK), lambda c: (0, 0)), pl.BlockSpec((1, QK), lambda c: (0, 0)),
                  pl.BlockSpec((1, D), lambda c: (0, 0))],
        out_specs=[pl.BlockSpec((CHUNK, D), row), pl.BlockSpec((CHUNK, D), row),
                   pl.BlockSpec((None, HEADS, DV, DK), lambda c: (c, 0, 0, 0))],
        out_shape=[SDS((T, D), BF16), SDS((T, D), BF16), SDS((NCHUNK, HEADS, DV, DK), BF16)],
        scratch_shapes=[pltpu.VMEM((HEADS, DV, DK), F32)],
        compiler_params=_cparams(32 * 1024 * 1024, ("arbitrary",)),
    )(pcat, pcat, pcat, pcat, pcat, wa, ba, ng)


def gla_bwd(do, pcat, states, wa, ba, after):
    def body(do_ref, q_ref, k_ref, v_ref, al_ref, sc_ref, sp_ref, wa_ref, ba_ref, after_ref,
             dq_ref, dk_ref, dv_ref, dal_ref, dwa_ref, dba_ref, ds_scr):
        i = pl.program_id(0)

        @pl.when(i == 0)
        def _():
            ds_scr[...] = jnp.zeros_like(ds_scr)

        has_prev = jnp.where(i < NCHUNK - 1, 1.0, 0.0).astype(F32)
        a, e, decay = _gate_decay(al_ref[...], wa_ref[...], ba_ref[...])
        kf = k_ref[...].astype(F32)
        kdf = kf * e
        kd = kdf.astype(BF16)
        qs = (q_ref[...].astype(F32) * (DK ** -0.5)).astype(BF16)
        dkd_parts, ddecay_parts = [], []
        for h in range(HEADS):
            ck = slice(h * DK, (h + 1) * DK)
            cv = slice(h * DV, (h + 1) * DV)
            doh = do_ref[:, cv]
            ds = ds_scr[h] + lax.dot_general(doh, qs[:, ck], (TN, ((), ())), preferred_element_type=F32)
            dsb = ds.astype(BF16)
            dq_ref[:, ck] = (jnp.dot(doh, sc_ref[h], preferred_element_type=F32) * (DK ** -0.5)).astype(BF16)
            dkd_parts.append(jnp.dot(v_ref[:, cv], dsb, preferred_element_type=F32))
            dv_ref[:, cv] = lax.dot_general(kd[:, ck], dsb, (NT, ((), ())), preferred_element_type=F32).astype(BF16)
            ddecay_parts.append(jnp.sum(ds * sp_ref[h].astype(F32), axis=0, keepdims=True) * has_prev)
            ds_scr[h] = ds * decay[:, ck]
        dkd = jnp.concatenate(dkd_parts, axis=1)
        ddecay = jnp.concatenate(ddecay_parts, axis=1)
        dk_ref[...] = (dkd * e).astype(BF16)
        dearg = dkd * kdf
        dlast = jnp.sum(dearg, axis=0, keepdims=True) + ddecay * decay
        r = lax.broadcasted_iota(jnp.int32, (CHUNK, CHUNK), 0)
        c = lax.broadcasted_iota(jnp.int32, (CHUNK, CHUNK), 1)
        triu = jnp.where(c >= r, 1.0, 0.0).astype(F32)
        dls = dlast - jnp.dot(triu, dearg, preferred_element_type=F32, precision=lax.Precision.HIGHEST)
        da = dls * (1.0 / 16.0) * (1.0 - _sigmoid(a))
        dab = da.astype(BF16)
        dal_ref[...] = lax.dot_general(dab, wa_ref[...], (NT, ((), ())), preferred_element_type=F32).astype(BF16)
        dwa = lax.dot_general(al_ref[...], dab, (TN, ((), ())), preferred_element_type=F32)
        dba = jnp.sum(da, axis=0, keepdims=True)

        @pl.when(i == 0)
        def _():
            dwa_ref[...] = dwa
            dba_ref[...] = dba

        @pl.when(i > 0)
        def _():
            dwa_ref[...] += dwa
            dba_ref[...] += dba

    rev = lambda i: NCHUNK - 1 - i
    return pl.pallas_call(
        body, name="gla_bwd", grid=(NCHUNK,),
        in_specs=[pl.BlockSpec((CHUNK, D), lambda i: (rev(i), 0)),
                  pl.BlockSpec((CHUNK, QK), lambda i: (rev(i), OQ // QK)), pl.BlockSpec((CHUNK, QK), lambda i: (rev(i), OKK // QK)),
                  pl.BlockSpec((CHUNK, D), lambda i: (rev(i), OV // D)), pl.BlockSpec((CHUNK, APAD), lambda i: (rev(i), OA // APAD)),
                  pl.BlockSpec((None, HEADS, DV, DK), lambda i: (rev(i), 0, 0, 0)),
                  pl.BlockSpec((None, HEADS, DV, DK), lambda i: (jnp.maximum(rev(i) - 1, 0), 0, 0, 0)),
                  pl.BlockSpec((APAD, QK), lambda i: (0, 0)), pl.BlockSpec((1, QK), lambda i: (0, 0)), ANY],
        out_specs=[pl.BlockSpec((CHUNK, QK), lambda i: (rev(i), 0)), pl.BlockSpec((CHUNK, QK), lambda i: (rev(i), 0)),
                   pl.BlockSpec((CHUNK, D), lambda i: (rev(i), 0)), pl.BlockSpec((CHUNK, APAD), lambda i: (rev(i), 0)),
                   pl.BlockSpec((APAD, QK), lambda i: (0, 0)), pl.BlockSpec((1, QK), lambda i: (0, 0))],
        out_shape=[SDS((T, QK), BF16), SDS((T, QK), BF16), SDS((T, D), BF16), SDS((T, APAD), BF16),
                   SDS((APAD, QK), F32), SDS((1, QK), F32)],
        scratch_shapes=[pltpu.VMEM((HEADS, DV, DK), F32)],
        compiler_params=_cparams(32 * 1024 * 1024, ("arbitrary",)),
    )(do, pcat, pcat, pcat, pcat, states, states, wa, ba, after)


TMF = 256
TMW = 512
_rowblk = ((TMF, D), lambda j, i, k: (i, 0))
_vec = ((1, D), lambda j, i, k: (0, 0))


def _full_spec(col):
    return ((TMF, D), lambda j, i, k: (i, col))


TBIG = 1024


def square_matmul(name, a, b, *, a_spec, b_spec, cdims, nk, after=None):
    def epi(acc, ex, outs, i):
        outs[0][...] = acc

    return matmul(name, a, b, a_spec=a_spec, b_spec=b_spec, cdims=cdims, grid=(D // TBIG, T // TBIG, nk),
                  acc_shape=(TBIG, TBIG), outs=[((T, D), F32, (TBIG, TBIG), lambda j, i, k: (i, j))], epi=epi,
                  after=after)[0]


def rowwise(name, y, *, extras, outs, epi):
    ne = len(extras)

    def body(*refs):
        epi(refs[0][...], refs[1:1 + ne], refs[1 + ne:], pl.program_id(1))

    in_specs = [pl.BlockSpec(*_rowblk)] + [pl.BlockSpec(bs, im) for _, bs, im in extras]
    return pl.pallas_call(
        body, name=name, grid=(1, T // TMF, 1), in_specs=in_specs,
        out_specs=[pl.BlockSpec(bs, im) for _, _, bs, im in outs], out_shape=[SDS(s, dt) for s, dt, _, _ in outs],
        compiler_params=_cparams(40 * 1024 * 1024, ("arbitrary", "arbitrary", "arbitrary")),
    )(y, *[arr for arr, _, _ in extras])


def mm_gla_out(og, w, ylin, pcat, pscale):
    def epi(acc, ex, outs, i):
        ylin_ref, lgp_ref, lgg_ref, ps_ref = ex
        gp = _sigmoid(lgp_ref[...].astype(F32))
        gg = _sigmoid(lgg_ref[...].astype(F32))
        outs[0][...] = (gp * (ylin_ref[...].astype(F32) * ps_ref[...]) + gg * acc).astype(BF16)
        outs[1][...] = acc.astype(BF16)

    return matmul("mm_gla_out", og, w, a_spec=_rowblk, b_spec=((D, D), lambda j, i, k: (0, 0)), cdims=NN,
                  grid=(1, T // TMF, 1), acc_shape=(TMF, D),
                  extras=[(ylin, *_rowblk), (pcat, *_full_spec(OGP // D)), (pcat, *_full_spec(OGG // D)), (pscale, *_vec)],
                  outs=[((T, D), BF16, *_rowblk), ((T, D), BF16, *_rowblk)], epi=epi)


def mm_out(mixed, w, x, g2):
    def epi(acc, ex, outs, i):
        x_ref, g_ref = ex
        x2 = x_ref[...] + acc
        r = lax.rsqrt(jnp.mean(x2 * x2, axis=-1, keepdims=True) + EPS)
        outs[0][...] = x2
        outs[1][...] = (x2 * r * g_ref[...]).astype(BF16)

    return matmul("mm_out", mixed, w, a_spec=_rowblk, b_spec=((D, D), lambda j, i, k: (0, 0)), cdims=NN,
                  grid=(1, T // TMF, 1), acc_shape=(TMF, D), extras=[(x, *_rowblk), (g2, *_vec)],
                  outs=[((T, D), F32, *_rowblk), ((T, D), BF16, *_rowblk)], epi=epi)


def mm_up(h2, wup):
    def epi(acc, ex, outs, i):
        r = jnp.maximum(acc, 0.0)
        outs[0][...] = r.astype(BF16)
        outs[1][...] = (r * r).astype(BF16)

    blk = ((TMW, D), lambda j, i, k: (i, j))
    return matmul("mm_up", h2, wup, a_spec=((TMW, D), lambda j, i, k: (i, 0)), b_spec=((None, D, D), lambda j, i, k: (j, 0, 0)),
                  cdims=NN, grid=(NCHIP, T // TMW, 1), acc_shape=(TMW, D),
                  outs=[((T, DFF), BF16, *blk), ((T, DFF), BF16, *blk)], epi=epi)


def mm_down(act, wdown, x2, tgt, gf):
    tk = 2048

    def epi(acc, ex, outs, i):
        x2_ref, t_ref, g_ref = ex
        dx_ref, dxb_ref, gnf_ref, loss_ref = outs
        x3 = x2_ref[...] + acc
        r = lax.rsqrt(jnp.mean(x3 * x3, axis=-1, keepdims=True) + EPS)
        xn = x3 * r
        err = xn * g_ref[...] - t_ref[...]
        lsum = 0.5 * jnp.sum(jnp.mean(err * err, axis=-1, keepdims=True), axis=0, keepdims=True)
        dy = err * (1.0 / D)
        _row_acc(gnf_ref, jnp.sum(dy * xn, axis=0, keepdims=True), i)
        _row_acc(loss_ref, jnp.broadcast_to(lsum, (1, 128)), i)
        dx3 = _rms_bwd(xn, r, dy * g_ref[...])
        dx_ref[...] = dx3
        dxb_ref[...] = dx3.astype(BF16)

    y = square_matmul("mm_down", act, wdown, a_spec=((TBIG, tk), lambda j, i, k: (i, k)),
                      b_spec=((tk, TBIG), lambda j, i, k: (k, j)), cdims=NN, nk=DFF // tk)
    return rowwise("rows_final", y, extras=[(x2, *_rowblk), (tgt, *_rowblk), (gf, *_vec)],
                   outs=[((T, D), F32, *_rowblk), ((T, D), BF16, *_rowblk), ((1, D), F32, *_vec),
                         ((1, 128), F32, (1, 128), lambda j, i, k: (0, 0))], epi=epi)


def mm_dact(dx3b, wdown, rup, after=None):
    def epi(acc, ex, outs, i):
        outs[0][...] = (acc * 2.0 * ex[0][...].astype(F32)).astype(BF16)

    blk = ((TMW, D), lambda j, i, k: (i, j))
    return matmul("mm_dact", dx3b, wdown, a_spec=((TMW, D), lambda j, i, k: (i, 0)), b_spec=((D, D), lambda j, i, k: (j, 0)),
                  cdims=NT, grid=(DFF // D, T // TMW, 1), acc_shape=(TMW, D), extras=[(rup, *blk)],
                  outs=[((T, DFF), BF16, *blk)], epi=epi, after=after)[0]


def mm_wgrad(name, a, b, m, n, out_shape, out_block, out_map, tm, tn, after=None):
    def epi(acc, ex, outs, i):
        outs[0][...] = acc.astype(BF16)

    return matmul(name, a, b, a_spec=((T, tm), lambda j, i, k: (0, i)), b_spec=((T, tn), lambda j, i, k: (0, j)),
                  cdims=TN, grid=(n // tn, m // tm, 1), acc_shape=(tm, tn),
                  outs=[(out_shape, BF16, out_block, out_map)], epi=epi, after=after)[0]


def mm_dh2(dup, wup, x2, dx3, g2, after=None):
    def epi(acc, ex, outs, i):
        x2_ref, dx3_ref, g_ref = ex
        x2 = x2_ref[...]
        r = lax.rsqrt(jnp.mean(x2 * x2, axis=-1, keepdims=True) + EPS)
        xn = x2 * r
        _row_acc(outs[2], jnp.sum(acc * xn, axis=0, keepdims=True), i)
        dx2 = dx3_ref[...] + _rms_bwd(xn, r, acc * g_ref[...])
        outs[0][...] = dx2
        outs[1][...] = dx2.astype(BF16)

    y = square_matmul("mm_dh2", dup, wup, a_spec=((TBIG, D), lambda j, i, k: (i, k)),
                      b_spec=((None, TBIG, D), lambda j, i, k: (k, j, 0)), cdims=NT, nk=NCHIP, after=after)
    return rowwise("rows_dh2", y, extras=[(x2, *_rowblk), (dx3, *_rowblk), (g2, *_vec)],
                   outs=[((T, D), F32, *_rowblk), ((T, D), BF16, *_rowblk), ((1, D), F32, *_vec)], epi=epi)


def mm_dmixed(dx2b, wout, pcat, ylin, ygla, pscale, after=None):
    def epi(acc, ex, outs, i):
        lgp_ref, lgg_ref, ylin_ref, ygla_ref, ps_ref = ex
        gp = _sigmoid(lgp_ref[...].astype(F32))
        gg = _sigmoid(lgg_ref[...].astype(F32))
        yl = ylin_ref[...].astype(F32)
        ps = ps_ref[...]
        agp = acc * gp
        outs[0][...] = (agp * ps).astype(BF16)
        outs[1][...] = (acc * gg).astype(BF16)
        outs[2][...] = (agp * (yl * ps) * (1.0 - gp)).astype(BF16)
        outs[3][...] = (acc * ygla_ref[...].astype(F32) * gg * (1.0 - gg)).astype(BF16)
        _row_acc(outs[4], jnp.sum(agp * yl, axis=0, keepdims=True), i)

    return matmul("mm_dmixed", dx2b, wout, a_spec=_rowblk, b_spec=((D, D), lambda j, i, k: (0, 0)), cdims=NT,
                  grid=(1, T // TMF, 1), acc_shape=(TMF, D),
                  extras=[(pcat, *_full_spec(OGP // D)), (pcat, *_full_spec(OGG // D)), (ylin, *_rowblk), (ygla, *_rowblk),
                          (pscale, *_vec)],
                  outs=[((T, D), BF16, *_rowblk)] * 4 + [((1, D), F32, *_vec)], epi=epi, after=after)


def mm_dog(dygla, wgo, o, pcat, ng, after=None):
    def epi(acc, ex, outs, i):
        o_ref, g_ref, ng_ref = ex
        do_ref, dg_ref, gng_ref = outs
        gparts = []
        for h in range(HEADS):
            cv = slice(h * DV, (h + 1) * DV)
            oh = o_ref[:, cv].astype(F32)
            r = lax.rsqrt(jnp.mean(oh * oh, axis=-1, keepdims=True) + EPS)
            on = oh * r
            gv = g_ref[:, cv].astype(F32)
            sg = _sigmoid(gv)
            dgain = acc[:, cv] * (gv * sg)
            gparts.append(jnp.sum(dgain * on, axis=0, keepdims=True))
            ngh = ng_ref[:, cv]
            do_ref[:, cv] = _rms_bwd(on, r, dgain * ngh).astype(BF16)
            dg_ref[:, cv] = (acc[:, cv] * (on * ngh) * (sg * (1.0 + gv * (1.0 - sg)))).astype(BF16)
        _row_acc(gng_ref, jnp.concatenate(gparts, axis=1), i)

    return matmul("mm_dog", dygla, wgo, a_spec=_rowblk, b_spec=((D, D), lambda j, i, k: (0, 0)), cdims=NT,
                  grid=(1, T // TMF, 1), acc_shape=(TMF, D),
                  extras=[(o, *_rowblk), (pcat, *_full_spec(OG // D)), (ng, *_vec)],
                  outs=[((T, D), BF16, *_rowblk), ((T, D), BF16, *_rowblk), ((1, D), F32, *_vec)], epi=epi, after=after)


def mm_dh1(dpcat, wcat, x, dx2, g1, after=None):
    tk = 2304

    def epi(acc, ex, outs, i):
        x_ref, dx2_ref, g_ref = ex
        xv = x_ref[...]
        r = lax.rsqrt(jnp.mean(xv * xv, axis=-1, keepdims=True) + EPS)
        xn = xv * r
        _row_acc(outs[1], jnp.sum(acc * xn, axis=0, keepdims=True), i)
        outs[0][...] = dx2_ref[...] + _rms_bwd(xn, r, acc * g_ref[...])

    y = square_matmul("mm_dh1", dpcat, wcat, a_spec=((TBIG, tk), lambda j, i, k: (i, k)),
                      b_spec=((TBIG, tk), lambda j, i, k: (j, k)), cdims=NT, nk=NCAT // tk, after=after)
    return rowwise("rows_dh1", y, extras=[(x, *_rowblk), (dx2, *_rowblk), (g1, *_vec)],
                   outs=[((T, D), F32, *_rowblk), ((1, D), F32, *_vec)], epi=epi)


def _tile_rows(rows, cols, n_arrays):
    tm = rows
    while tm % 32 == 0 and 2 * n_arrays * tm * cols * 4 > 24 * 1024 * 1024:
        tm //= 2
    return tm


def add_pairs(name, parts, theirs, core):
    _, _, r, c = parts.shape
    tm = _tile_rows(r, c, 3)

    def body(core_ref, a_ref, b_ref, o_ref):
        o_ref[...] = (a_ref[...].astype(F32) + b_ref[...].astype(F32)).astype(BF16)

    spec = pl.BlockSpec((None, tm, c), lambda j, i, core_ref: (j, i, 0))
    grid_spec = pltpu.PrefetchScalarGridSpec(
        num_scalar_prefetch=1, grid=(NCHIP, r // tm),
        in_specs=[pl.BlockSpec((None, None, tm, c), lambda j, i, core_ref: (core_ref[0], j, i, 0)), spec], out_specs=spec)
    return pl.pallas_call(body, name=name, grid_spec=grid_spec, out_shape=SDS((NCHIP, r, c), BF16),
                          compiler_params=_cparams(40 * 1024 * 1024, ("arbitrary", "arbitrary")))(core, parts, theirs)


def sum_chips(name, sums, landed, chip):
    _, r, c = sums.shape
    tm = _tile_rows(r, c, 4)

    def body(chip_ref, own_ref, l_ref, o_ref):
        s = own_ref[...].astype(F32)
        for t in range(NCHIP - 1):
            s = s + l_ref[t].astype(F32)
        o_ref[...] = s

    grid_spec = pltpu.PrefetchScalarGridSpec(
        num_scalar_prefetch=1, grid=(r // tm,),
        in_specs=[pl.BlockSpec((None, tm, c), lambda i, chip_ref: (chip_ref[0], i, 0)),
                  pl.BlockSpec((NCHIP - 1, tm, c), lambda i, chip_ref: (0, i, 0))],
        out_specs=pl.BlockSpec((tm, c), lambda i, chip_ref: (i, 0)))
    return pl.pallas_call(body, name=name, grid_spec=grid_spec, out_shape=SDS((r, c), F32),
                          compiler_params=_cparams(40 * 1024 * 1024, ("arbitrary",)))(chip, sums, landed)


def _adamw_math(wv, gv, mv, vv):
    mn = ADAM_B1 * mv + (1.0 - ADAM_B1) * gv
    vn = ADAM_B2 * vv + (1.0 - ADAM_B2) * (gv * gv)
    mh = mn / (1.0 - ADAM_B1 ** ADAM_STEP)
    vh = vn / (1.0 - ADAM_B2 ** ADAM_STEP)
    return -ADAM_LR * (mh / (jnp.sqrt(vh) + ADAM_EPS) + ADAM_WD * wv), mn, vn


def adamw(name, w, g, m, v):
    def body(w_ref, g_ref, m_ref, v_ref, go_ref, d_ref, mo_ref, vo_ref):
        gv = g_ref[...]
        go_ref[...] = gv
        d_ref[...], mo_ref[...], vo_ref[...] = _adamw_math(w_ref[...], gv, m_ref[...], v_ref[...])

    return pl.pallas_call(body, name=name, out_shape=[SDS(w.shape, F32)] * 4)(w, g, m, v)


def adamw_halves(name, w, g_own, g_sib, m, v, core):
    _, r, c = w.shape
    tm = _tile_rows(r, c, 10)

    def body(core_ref, w_ref, go_ref, gs_ref, m_ref, v_ref, g_out, d_out, m_out, v_out):
        gv = jnp.where(pl.program_id(0) == core_ref[0], go_ref[...], gs_ref[...])
        g_out[...] = gv
        d_out[...], m_out[...], v_out[...] = _adamw_math(w_ref[...], gv, m_ref[...], v_ref[...])

    full = pl.BlockSpec((None, tm, c), lambda h, i, core_ref: (h, i, 0))
    own = pl.BlockSpec((tm, c), lambda h, i, core_ref: (jnp.where(h == core_ref[0], i, 0), 0))
    sib = pl.BlockSpec((tm, c), lambda h, i, core_ref: (jnp.where(h == core_ref[0], 0, i), 0))
    grid_spec = pltpu.PrefetchScalarGridSpec(num_scalar_prefetch=1, grid=(2, r // tm),
                                             in_specs=[full, own, sib, full, full], out_specs=[full] * 4)
    return pl.pallas_call(body, name=name, grid_spec=grid_spec, out_shape=[SDS(w.shape, F32)] * 4,
                          compiler_params=_cparams(48 * 1024 * 1024, ("arbitrary", "arbitrary")))(core, w, g_own, g_sib, m, v)


def pack_rows(name, parts, rows, after=None):
    width = parts[0].shape[1]
    n = len(parts)
    afters = _as_list(after)

    def body(*refs):
        out_ref = refs[n + len(afters)]
        out_ref[...] = jnp.zeros_like(out_ref)
        off = 0
        for p in refs[:n]:
            out_ref[off:off + p.shape[0], :] = p[...]
            off += p.shape[0]

    vm = pl.BlockSpec(memory_space=pltpu.VMEM)
    return pl.pallas_call(body, name=name, in_specs=[vm] * n + [ANY] * len(afters), out_specs=vm,
                          out_shape=SDS((rows, width), F32))(*parts, *afters)


def _place():
    x, y, c = lax.axis_index("x"), lax.axis_index("y"), lax.axis_index("c")
    chips = [(1 - x, y), (x, 1 - y), (1 - x, 1 - y)]
    return x, y, c, chips


def _row_split(shape, dtype):
    r, c = shape
    n = 1
    while r % (2 * n) == 0 and (r // (2 * n)) % 16 == 0 and (r // n) * c * jnp.dtype(dtype).itemsize > PIECE_BYTES:
        n *= 2
    return [pl.ds(s * (r // n), r // n) for s in range(n)]


def _pieces(ref):
    *lead, r, c = ref.shape
    split = _row_split((r, c), ref.dtype)
    return [ref.at[(*idx, s)] for idx in itertools.product(*[range(d) for d in lead]) for s in split]


HBM = pl.BlockSpec(memory_space=pltpu.HBM)
SEM = pl.BlockSpec(memory_space=pltpu.SEMAPHORE)
EFFECT = pltpu.SideEffectType.DATAFLOW_SIDE_EFFECTING


def gather_start(name, shards, after=None):
    n = len(shards)
    afters = _as_list(after)

    def body(*refs):
        src, land = refs[:n], refs[n:2 * n]
        send, recv = refs[2 * n + len(afters)], refs[2 * n + len(afters) + 1]
        x, y, c, chips = _place()
        me = 2 * x + y
        for a in range(n):
            for j, (cx, cy) in enumerate(chips[:2]):
                for sp, dp in zip(_pieces(src[a].at[c]), _pieces(land[a].at[me, c])):
                    pltpu.make_async_remote_copy(sp, dp, send.at[2 * a + j], recv.at[2 * a + j],
                                                 device_id=(cx, cy, c), device_id_type=MESH).start()

    lands = [pltpu.with_memory_space_constraint(lax.empty((NCHIP,) + s.shape, s.dtype), pltpu.HBM) for s in shards]
    srcs = [pltpu.with_memory_space_constraint(s, pltpu.HBM) for s in shards]
    outs = pl.pallas_call(
        body, name=name,
        out_shape=(pltpu.SemaphoreType.DMA((2 * n,)), pltpu.SemaphoreType.DMA((2 * n,)),
                   *[pltpu.HBM(s.shape, s.dtype) for s in shards], *[pltpu.HBM(l.shape, l.dtype) for l in lands]),
        in_specs=[HBM] * (2 * n) + [ANY] * len(afters), out_specs=(SEM, SEM, *([HBM] * (2 * n))),
        input_output_aliases={i: 2 + i for i in range(2 * n)},
        compiler_params=pltpu.CompilerParams(has_side_effects=EFFECT),
    )(*srcs, *lands, *afters)
    return outs[0], outs[1], list(outs[2:2 + n]), list(outs[2 + n:2 + 2 * n])


def gather_wait(name, send, recv, shards, lands, after):
    n = len(shards)
    afters = _as_list(after)

    def body(*refs):
        src, land = refs[:n], refs[n:2 * n]
        send_ref, recv_ref = refs[2 * n], refs[2 * n + 1]
        x, y, c, chips = _place()
        for a in range(n):
            for j, (cx, cy) in enumerate(chips[:2]):
                cp = pltpu.make_async_remote_copy(src[a].at[c], land[a].at[2 * cx + cy, c], send_ref.at[2 * a + j],
                                                  recv_ref.at[2 * a + j], device_id=(cx, cy, c), device_id_type=MESH)
                cp.wait_send()
                cp.wait_recv()

    outs = pl.pallas_call(
        body, name=name,
        out_shape=(*[pltpu.HBM(s.shape, s.dtype) for s in shards], *[pltpu.HBM(l.shape, l.dtype) for l in lands]),
        in_specs=[HBM] * (2 * n) + [SEM, SEM] + [ANY] * len(afters), out_specs=[HBM] * (2 * n),
        input_output_aliases={i: i for i in range(2 * n)},
        compiler_params=pltpu.CompilerParams(has_side_effects=EFFECT),
    )(*shards, *lands, send, recv, *afters)
    return list(outs[:n]), list(outs[n:])


def _relay_blocks(land, c, chips):
    (xx, xy), (yx, yy), (dx, dy) = chips
    rows = land.shape[2] // 2
    upper, lower = pl.ds(0, rows), pl.ds(rows, rows)
    return [(land.at[2 * yx + yy, c, lower], land.at[2 * dx + dy, c, lower]),
            (land.at[2 * xx + xy, c, upper], land.at[2 * dx + dy, c, upper])]


def relay_start(name, lands, after=None):
    n = len(lands)
    afters = _as_list(after)

    def body(*refs):
        had, land = refs[:n], refs[n + len(afters) + 2:2 * n + len(afters) + 2]
        send, recv = refs[n + len(afters)], refs[n + len(afters) + 1]
        x, y, c, chips = _place()
        for a in range(n):
            for j, ((sent, _), (dst, _)) in enumerate(zip(_relay_blocks(had[a], c, chips), _relay_blocks(land[a], c, chips))):
                cx, cy = chips[j]
                for sp, dp in zip(_pieces(sent), _pieces(dst)):
                    pltpu.make_async_remote_copy(sp, dp, send.at[2 * a + j], recv.at[2 * a + j],
                                                 device_id=(cx, cy, c), device_id_type=MESH).start()

    outs = pl.pallas_call(
        body, name=name,
        out_shape=(pltpu.SemaphoreType.DMA((2 * n,)), pltpu.SemaphoreType.DMA((2 * n,)),
                   *[pltpu.HBM(l.shape, l.dtype) for l in lands]),
        in_specs=[HBM] * n + [ANY] * len(afters), out_specs=(SEM, SEM, *([HBM] * n)),
        input_output_aliases={i: 2 + i for i in range(n)},
        compiler_params=pltpu.CompilerParams(has_side_effects=EFFECT),
    )(*lands, *afters)
    return outs[0], outs[1], list(outs[2:])


def relay_wait(name, send, recv, lands, after):
    n = len(lands)
    afters = _as_list(after)

    def body(*refs):
        land = refs[:n]
        send_ref, recv_ref = refs[n], refs[n + 1]
        x, y, c, chips = _place()
        for a in range(n):
            for j, (sent, got) in enumerate(_relay_blocks(land[a], c, chips)):
                cx, cy = chips[j]
                cp = pltpu.make_async_remote_copy(sent, got, send_ref.at[2 * a + j], recv_ref.at[2 * a + j],
                                                  device_id=(cx, cy, c), device_id_type=MESH)
                cp.wait_send()
                cp.wait_recv()

    outs = pl.pallas_call(
        body, name=name, out_shape=tuple(pltpu.HBM(l.shape, l.dtype) for l in lands),
        in_specs=[HBM] * n + [SEM, SEM] + [ANY] * len(afters), out_specs=[HBM] * n,
        input_output_aliases={i: i for i in range(n)},
        compiler_params=pltpu.CompilerParams(has_side_effects=EFFECT),
    )(*lands, send, recv, *afters)
    return list(outs)


def forward_halves(name, shards, lands):
    n = len(lands)

    def body(*refs):
        had, buf = refs[:n], refs[n:2 * n]
        send, recv = refs[2 * n:]
        x, y, c, chips = _place()
        sib = (x, y, 1 - c)
        for a in range(n):
            for j, (cx, cy) in enumerate(chips):
                for sp, dp in zip(_pieces(had[a].at[2 * cx + cy, c]), _pieces(buf[a].at[2 * cx + cy, c])):
                    pltpu.make_async_remote_copy(sp, dp, send.at[3 * a + j], recv.at[3 * a + j], device_id=sib, device_id_type=MESH).start()
        for a in range(n):
            for j, (cx, cy) in enumerate(chips):
                pltpu.make_async_remote_copy(had[a].at[2 * cx + cy, c], buf[a].at[2 * cx + cy, 1 - c], send.at[3 * a + j],
                                             recv.at[3 * a + j], device_id=sib, device_id_type=MESH).wait()

    got = pl.pallas_call(
        body, name=name, in_specs=[ANY] * n, out_specs=[ANY] * n, out_shape=[SDS(l.shape, l.dtype) for l in lands],
        input_output_aliases={i: i for i in range(n)},
        scratch_shapes=[pltpu.SemaphoreType.DMA((3 * n,)), pltpu.SemaphoreType.DMA((3 * n,))],
    )(*lands)
    me = 2 * lax.axis_index("x") + lax.axis_index("y")
    return [lax.dynamic_update_index_in_dim(g, s, me, 0) for g, s in zip(got, shards)]


def exchange_start(name, parts):
    n = len(parts)

    def body(*refs):
        src, got = refs[:n], refs[n:2 * n]
        send, recv = refs[2 * n], refs[2 * n + 1]
        token = refs[4 * n + 2]
        x, y, c, _ = _place()
        sib = (x, y, 1 - c)
        for a in range(n):
            for sp, dp in zip(_pieces(src[a].at[1 - c]), _pieces(got[a])):
                pltpu.make_async_remote_copy(sp, dp, send.at[a], recv.at[a], device_id=sib, device_id_type=MESH).start()
        token[...] = jnp.zeros_like(token)

    lands = [pltpu.with_memory_space_constraint(lax.empty(p.shape[1:], p.dtype), pltpu.HBM) for p in parts]
    srcs = [pltpu.with_memory_space_constraint(p, pltpu.HBM) for p in parts]
    outs = pl.pallas_call(
        body, name=name,
        out_shape=(pltpu.SemaphoreType.DMA((n,)), pltpu.SemaphoreType.DMA((n,)),
                   *[pltpu.HBM(p.shape, p.dtype) for p in parts], *[pltpu.HBM(l.shape, l.dtype) for l in lands],
                   SDS((8, 128), F32)),
        in_specs=[HBM] * (2 * n), out_specs=(SEM, SEM, *([HBM] * (2 * n)), pl.BlockSpec(memory_space=pltpu.VMEM)),
        input_output_aliases={i: 2 + i for i in range(2 * n)},
        compiler_params=pltpu.CompilerParams(has_side_effects=EFFECT),
    )(*srcs, *lands)
    return outs[0], outs[1], list(outs[2:2 + n]), list(outs[2 + n:2 + 2 * n]), outs[2 + 2 * n]


def exchange_wait(name, send, recv, parts, lands, after):
    n = len(parts)
    afters = _as_list(after)

    def body(*refs):
        src, got = refs[:n], refs[n:2 * n]
        send_ref, recv_ref = refs[2 * n], refs[2 * n + 1]
        x, y, c, _ = _place()
        sib = (x, y, 1 - c)
        for a in range(n):
            cp = pltpu.make_async_remote_copy(src[a].at[1 - c], got[a], send_ref.at[a], recv_ref.at[a], device_id=sib, device_id_type=MESH)
            cp.wait_send()
            cp.wait_recv()

    outs = pl.pallas_call(
        body, name=name,
        out_shape=(*[pltpu.HBM(p.shape, p.dtype) for p in parts], *[pltpu.HBM(l.shape, l.dtype) for l in lands]),
        in_specs=[HBM] * (2 * n) + [SEM, SEM] + [ANY] * len(afters), out_specs=[HBM] * (2 * n),
        input_output_aliases={i: i for i in range(2 * n)},
        compiler_params=pltpu.CompilerParams(has_side_effects=EFFECT),
    )(*parts, *lands, send, recv, *afters)
    return list(outs[:n]), list(outs[n:])


def scatter_start(name, parts):
    n = len(parts)

    def body(*refs):
        src, land = refs[:n], refs[n:2 * n]
        send, recv = refs[2 * n], refs[2 * n + 1]
        token = refs[4 * n + 2]
        x, y, c, chips = _place()
        for a in range(n):
            for j, (cx, cy) in enumerate(chips):
                for sp, dp in zip(_pieces(src[a].at[2 * cx + cy]), _pieces(land[a].at[j])):
                    pltpu.make_async_remote_copy(sp, dp, send.at[3 * a + j], recv.at[3 * a + j],
                                                 device_id=(cx, cy, c), device_id_type=MESH).start()
        token[...] = jnp.zeros_like(token)

    lands = [pltpu.with_memory_space_constraint(lax.empty((NCHIP - 1,) + p.shape[1:], p.dtype), pltpu.HBM) for p in parts]
    srcs = [pltpu.with_memory_space_constraint(p, pltpu.HBM) for p in parts]
    outs = pl.pallas_call(
        body, name=name,
        out_shape=(pltpu.SemaphoreType.DMA((3 * n,)), pltpu.SemaphoreType.DMA((3 * n,)),
                   *[pltpu.HBM(p.shape, p.dtype) for p in parts], *[pltpu.HBM(l.shape, l.dtype) for l in lands],
                   SDS((8, 128), F32)),
        in_specs=[HBM] * (2 * n), out_specs=(SEM, SEM, *([HBM] * (2 * n)), pl.BlockSpec(memory_space=pltpu.VMEM)),
        input_output_aliases={i: 2 + i for i in range(2 * n)},
        compiler_params=pltpu.CompilerParams(has_side_effects=EFFECT),
    )(*srcs, *lands)
    return outs[0], outs[1], list(outs[2:2 + n]), list(outs[2 + n:2 + 2 * n]), outs[2 + 2 * n]


def scatter_wait(name, send, recv, parts, lands, after):
    n = len(parts)
    afters = _as_list(after)

    def body(*refs):
        src, land = refs[:n], refs[n:2 * n]
        send_ref, recv_ref = refs[2 * n], refs[2 * n + 1]
        x, y, c, chips = _place()
        for a in range(n):
            for j, (cx, cy) in enumerate(chips):
                cp = pltpu.make_async_remote_copy(src[a].at[2 * cx + cy], land[a].at[j], send_ref.at[3 * a + j], recv_ref.at[3 * a + j],
                                                  device_id=(cx, cy, c), device_id_type=MESH)
                cp.wait_send()
                cp.wait_recv()

    outs = pl.pallas_call(
        body, name=name,
        out_shape=(*[pltpu.HBM(p.shape, p.dtype) for p in parts], *[pltpu.HBM(l.shape, l.dtype) for l in lands]),
        in_specs=[HBM] * (2 * n) + [SEM, SEM] + [ANY] * len(afters), out_specs=[HBM] * (2 * n),
        input_output_aliases={i: i for i in range(2 * n)},
        compiler_params=pltpu.CompilerParams(has_side_effects=EFFECT),
    )(*parts, *lands, send, recv, *afters)
    return list(outs[:n]), list(outs[n:])


def join_start(name, halves):
    n = len(halves)

    def body(*refs):
        src, dst = refs[:n], refs[n:2 * n]
        send, recv = refs[2 * n], refs[2 * n + 1]
        token = refs[4 * n + 2]
        x, y, c, _ = _place()
        sib = (x, y, 1 - c)
        for a in range(n):
            for sp, dp in zip(_pieces(src[a]), _pieces(dst[a])):
                pltpu.make_async_remote_copy(sp, dp, send.at[a], recv.at[a], device_id=sib, device_id_type=MESH).start()
        token[...] = jnp.zeros_like(token)

    lands = [pltpu.with_memory_space_constraint(lax.empty(h.shape, h.dtype), pltpu.HBM) for h in halves]
    srcs = [pltpu.with_memory_space_constraint(h, pltpu.HBM) for h in halves]
    outs = pl.pallas_call(
        body, name=name,
        out_shape=(pltpu.SemaphoreType.DMA((n,)), pltpu.SemaphoreType.DMA((n,)),
                   *[pltpu.HBM(h.shape, h.dtype) for h in halves], *[pltpu.HBM(l.shape, l.dtype) for l in lands],
                   SDS((8, 128), F32)),
        in_specs=[HBM] * (2 * n), out_specs=(SEM, SEM, *([HBM] * (2 * n)), pl.BlockSpec(memory_space=pltpu.VMEM)),
        input_output_aliases={i: 2 + i for i in range(2 * n)},
        compiler_params=pltpu.CompilerParams(has_side_effects=EFFECT),
    )(*srcs, *lands)
    return outs[0], outs[1], list(outs[2:2 + n]), list(outs[2 + n:2 + 2 * n]), outs[2 + 2 * n]


def join_wait(name, send, recv, halves, lands, after):
    n = len(halves)
    afters = _as_list(after)

    def body(*refs):
        src, dst = refs[:n], refs[n:2 * n]
        send_ref, recv_ref = refs[2 * n], refs[2 * n + 1]
        x, y, c, _ = _place()
        sib = (x, y, 1 - c)
        for a in range(n):
            cp = pltpu.make_async_remote_copy(src[a], dst[a], send_ref.at[a], recv_ref.at[a], device_id=sib, device_id_type=MESH)
            cp.wait_send()
            cp.wait_recv()

    outs = pl.pallas_call(
        body, name=name,
        out_shape=(*[pltpu.HBM(h.shape, h.dtype) for h in halves], *[pltpu.HBM(l.shape, l.dtype) for l in lands]),
        in_specs=[HBM] * (2 * n) + [SEM, SEM] + [ANY] * len(afters), out_specs=[HBM] * (2 * n),
        input_output_aliases={i: i for i in range(2 * n)},
        compiler_params=pltpu.CompilerParams(has_side_effects=EFFECT),
    )(*halves, *lands, send, recv, *afters)
    return list(outs[:n]), list(outs[n:])


def gather_small(name, xs, reduce, after=None):
    m, ncol = xs.shape
    afters = _as_list(after)

    def body(x_ref, *rest):
        out_ref, all_ref, send, recv, lsem = rest[len(afters):]
        x, y, c, chips = _place()
        me, sib = (x, y, c), (x, y, 1 - c)

        def rows(px, py, pc):
            return all_ref.at[pl.ds((4 * px + 2 * py + pc) * m, m), :]

        def copy(k, block, to, src=None):
            return pltpu.make_async_remote_copy(rows(*block) if src is None else src, rows(*block), send.at[k], recv.at[k],
                                                device_id=to, device_id_type=MESH)

        mine = pltpu.make_async_copy(x_ref, rows(*me), lsem)
        mine.start()
        first = [copy(0, me, sib, src=x_ref)] + [copy(1 + j, me, (*chip, c), src=x_ref) for j, chip in enumerate(chips)]
        for cp in first:
            cp.start()
        passed = [copy(4 + j, (*chip, c), sib) for j, chip in enumerate(chips)]
        for j, chip in enumerate(chips):
            copy(1 + j, (*chip, c), me).wait_recv()
            passed[j].start()
        copy(0, sib, me).wait_recv()
        for j, chip in enumerate(chips):
            copy(4 + j, (*chip, 1 - c), me).wait_recv()
        for cp in first + passed:
            cp.wait_send()
        mine.wait()
        if reduce:
            s = all_ref[0:m, :]
            for dev in range(1, 8):
                s = s + all_ref[dev * m:(dev + 1) * m, :]
            out_ref[...] = s
        else:
            out_ref[...] = all_ref[...]

    vm = pl.BlockSpec(memory_space=pltpu.VMEM)
    return pl.pallas_call(
        body, name=name, in_specs=[vm] + [ANY] * len(afters), out_specs=vm,
        out_shape=SDS((m, ncol) if reduce else (8 * m, ncol), F32),
        scratch_shapes=[pltpu.VMEM((8 * m, ncol), F32), pltpu.SemaphoreType.DMA((7,)), pltpu.SemaphoreType.DMA((7,)),
                        pltpu.SemaphoreType.DMA],
    )(xs, *afters)


RELAYOUT_ROWS = 128


def weights_to_cat(g_in):
    tm = RELAYOUT_ROWS

    def body(g_ref, o_ref):
        nat = jnp.concatenate([g_ref[j] for j in range(NCHIP)], axis=1)
        pad = jnp.zeros((tm, NCAT - OA - 16), BF16)
        o_ref[...] = jnp.concatenate([nat[:, 3072:7168], nat[:, 7184:11280], nat[:, 0:3072], nat[:, 7168:7184], pad], axis=1)

    return pl.pallas_call(
        body, name="weights_to_cat", grid=(D // tm,), in_specs=[pl.BlockSpec((NCHIP, tm, IN_SHARD), lambda i: (0, i, 0))],
        out_specs=pl.BlockSpec((tm, NCAT), lambda i: (i, 0)), out_shape=SDS((D, NCAT), BF16),
        compiler_params=_cparams(40 * 1024 * 1024, ("arbitrary",)),
    )(g_in)


def grads_from_cat(gw_cat):
    tm = RELAYOUT_ROWS
    nb = (D // 2) // tm

    def body(c_ref, o_ref):
        cat = c_ref[...]
        nat = jnp.concatenate([cat[:, OU:OA], cat[:, OV:OGP], cat[:, OA:OA + 16], cat[:, OGP:OU]], axis=1)
        for j in range(NCHIP):
            o_ref[j] = nat[:, j * IN_SHARD:(j + 1) * IN_SHARD]

    return pl.pallas_call(
        body, name="grads_from_cat", grid=(D // tm,), in_specs=[pl.BlockSpec((tm, NCAT), lambda i: (i, 0))],
        out_specs=pl.BlockSpec((None, NCHIP, tm, IN_SHARD), lambda i: (i // nb, 0, i % nb, 0)),
        out_shape=SDS((2, NCHIP, D // 2, IN_SHARD), BF16), compiler_params=_cparams(40 * 1024 * 1024, ("arbitrary",)),
    )(gw_cat)


def _pad_rows(a, rows):
    return jnp.concatenate([a, jnp.zeros((rows - a.shape[0],) + a.shape[1:], a.dtype)], axis=0)


def local_step(x2d, tgt, gf, g1, pool_scale, wa_pad, b_alpha, ng, g2, get_w, on_grad=None, on_settle=None, tick=None):
    emit = on_grad if on_grad is not None else (lambda group, grads: None)
    settle = on_settle if on_settle is not None else (lambda group, after: None)
    h1 = norm1(x2d, g1)
    wcat, pw = get_w("in", h1)
    pcat = mm_in(h1, wcat)
    dpool, ylin = pool_fwd(pcat, pw)
    if tick is not None:
        tick("pool", ylin)
    og, o, states = gla_fwd(pcat, wa_pad, b_alpha, ng)
    w_go, w_o = get_w("mid", og)
    mixed, ygla = mm_gla_out(og, w_go, ylin, pcat, pool_scale)
    x2, h2 = mm_out(mixed, w_o, x2d, g2)
    w_up = get_w("up", h2)
    rup, act = mm_up(h2, w_up)
    w_dn = get_w("down", act)
    dx3, dx3b, g_nf, loss_row = mm_down(act, w_dn, x2, tgt, gf)

    gw_down = mm_wgrad("mm_dw_down", act, dx3b, DFF, D, (2, NCHIP, D // 2, D), (None, None, 512, D),
                       lambda j, i, k: ((i // 2) % 2, i // 4, i % 2, 0), 512, D)
    token = emit("down", {"down": gw_down})
    dup = mm_dact(dx3b, w_dn, rup, after=token)
    token = settle("down", dup)
    dx2, dx2b, g_mlp = mm_dh2(dup, w_up, x2, dx3, g2, after=token)
    gw_up = mm_wgrad("mm_dw_up", h2, dup, D, DFF, (2, NCHIP, D // 2, D), (None, None, 512, D),
                     lambda j, i, k: (i // 2, j, i % 2, 0), 512, D)
    token = emit("up", {"up": gw_up})
    dylin, dygla, dlgp, dlgg, g_ps = mm_dmixed(dx2b, w_o, pcat, ylin, ygla, pool_scale, after=token)
    token = settle("up", dylin)
    gw_out = mm_wgrad("mm_dw_out", mixed, dx2b, D, D, (2, NCHIP, 256, D), (None, None, 256, D),
                      lambda j, i, k: (i % 2, i // 2, 0, 0), 256, D)
    do, dg, g_ng = mm_dog(dygla, w_go, o, pcat, ng, after=token)
    gw_go = mm_wgrad("mm_dw_gla_out", og, dygla, D, D, (2, NCHIP, 256, D), (None, None, 256, D),
                     lambda j, i, k: (i % 2, i // 2, 0, 0), 256, D)
    token = emit("mix", {"out": gw_out, "gla_out": gw_go})
    dq, dk, dv, dalow, g_wa, g_ba = gla_bwd(do, pcat, states, wa_pad, b_alpha, b_alpha if token is None else token)
    token = settle("mix", dq)
    du, dpw = pool_bwd(dylin, dpool, pw)
    dpcat = jnp.concatenate([dv, dg, dlgp, dlgg, du, dq, dk, dalow, jnp.zeros((T, NCAT - OA - APAD), BF16)], axis=1)
    gw_cat = mm_wgrad("mm_dw_in", h1, dpcat, D, NCAT, (D, NCAT), (1024, 1280), lambda j, i, k: (i, j), 1024, 1280, after=token)
    token = settle("in", emit("in", {"in_cat": gw_cat, "pool": dpw}))
    grad_x, g_mix = mm_dh1(dpcat, wcat, x2d, dx2, g1, after=token)
    return (loss_row[0, 0], grad_x, g_mix, g_ps, g_mlp, g_nf, g_ng, g_ba, g_wa, token,
            gw_cat, dpw, gw_go, gw_out, gw_up, gw_down)


def kernel(x, norm_mix_g, w_in, pool_w, pool_scale, w_alpha, b_alpha, gla_norm_g, w_gla_out, w_out, norm_mlp_g, w_mlp_up, w_mlp_down, norm_final_g, loss_target, m_norm_mix_g, m_w_in, m_pool_w, m_pool_scale, m_w_alpha, m_b_alpha, m_gla_norm_g, m_w_gla_out, m_w_out, m_norm_mlp_g, m_w_mlp_up, m_w_mlp_down, m_norm_final_g, v_norm_mix_g, v_w_in, v_pool_w, v_pool_scale, v_w_alpha, v_b_alpha, v_gla_norm_g, v_w_gla_out, v_w_out, v_norm_mlp_g, v_w_mlp_up, v_w_mlp_down, v_norm_final_g):
    chip = 2 * lax.axis_index("x") + lax.axis_index("y")
    chip_i = chip.astype(jnp.int32).reshape(1)
    core_i = lax.axis_index("c").astype(jnp.int32).reshape(1)
    x2d = x.reshape(T, D)
    tgt = loss_target.reshape(T, D)
    gf = norm_final_g.reshape(1, D)

    def halves(w2d):
        r, c = w2d.shape
        return w2d.astype(BF16).reshape(2, r // 2, c)

    pool_shard = pool_w.reshape(4 * PG, PO // NCHIP)
    big = [w_in[0], w_gla_out[0], w_out[0], w_mlp_up[0], w_mlp_down[0], pool_shard]
    groups = {"in": [big[0], big[5]], "mid": [big[1], big[2]], "up": [big[3]], "down": [big[4]]}
    sent = {g: [halves(w) for w in ws] for g, ws in groups.items()}
    flight = {}

    def start(group, after=None):
        flight[group] = gather_start("gather_start_" + group, sent[group], after)

    def relay(group, after):
        send, recv, shards, lands = flight[group]
        shards, lands = gather_wait("gather_wait_" + group, send, recv, shards, lands, after)
        send, recv, lands = relay_start("relay_start_" + group, lands)
        flight[group] = (send, recv, shards, lands)

    def fetch(group, after):
        send, recv, shards, lands = flight[group]
        lands = relay_wait("relay_wait_" + group, send, recv, lands, after)
        return forward_halves("forward_" + group, shards, lands)

    start("in")
    w_in_r, m_in_r, v_in_r = [a.reshape(2, D // 2, IN_SHARD)
                              for a in lax.optimization_barrier((w_in, m_w_in, v_w_in, flight["in"][2][0]))[:3]]
    relay("in", [w_in_r, m_in_r, *sent["mid"], *sent["up"]])
    start("mid", flight["in"][3][0])

    def tick(point, after):
        if point == "pool":
            relay("mid", after)
            start("down", flight["mid"][3][0])

    def get_w(group, after):
        if group == "in":
            after = [after, v_in_r, *sent["down"], wa_pad]
        whole = fetch(group, after)
        if group == "in":
            start("up", whole[0])
        if group == "mid":
            relay("up", whole[0])
        if group == "up":
            relay("down", whole[0])
        if group == "in":
            g_in, g_pool = whole
            wcat = weights_to_cat(g_in.reshape(NCHIP, D, IN_SHARD))
            pw = jnp.concatenate([g_pool[j].reshape(4, PG, PO // NCHIP) for j in range(NCHIP)], axis=2)
            return wcat, pw
        if group == "mid":
            return whole[0].reshape(D, D), whole[1].reshape(D, D)
        if group == "up":
            return whole[0].reshape(NCHIP, D, D)
        return whole[0].reshape(DFF, D)

    small_w = pack_rows("pack_small_w", [w_alpha[0].reshape(4, QK),
                                         jnp.concatenate([gla_norm_g[0].reshape(1, 512), jnp.zeros((1, 512), F32)], axis=1)], 8)
    sw_all = gather_small("gather_small_w", small_w, False).reshape(8, 8, QK)
    wa_full = jnp.concatenate([sw_all[2 * j, 0:4].reshape(16, DK) for j in range(NCHIP)], axis=1)
    ng_full = jnp.concatenate([sw_all[2 * j, 4, 0:512].reshape(HEADS, DV // NCHIP) for j in range(NCHIP)], axis=1)
    wa_pad = _pad_rows(wa_full, APAD).astype(BF16)
    ng = ng_full.reshape(1, D)

    pending = {}
    wmv = {"in": (w_in_r, m_in_r, v_in_r), "gla_out": (big[1], m_w_gla_out, v_w_gla_out), "out": (big[2], m_w_out, v_w_out),
           "up": (big[3], m_w_mlp_up, v_w_mlp_up), "down": (big[4], m_w_mlp_down, v_w_mlp_down), "pool": (big[5], m_pool_w, v_pool_w)}
    big_res = {}

    def reduce_group(group, after):
        nms, send, recv, sums, lands = pending[group]
        sums, lands = scatter_wait("scatter_wait_" + group, send, recv, sums, lands, after)
        reduced = [sum_chips("sum_chips_" + nm, a, b, chip_i) for nm, a, b in zip(nms, sums, lands)]
        send, recv, reduced, lands, token = join_start("join_start_" + group, reduced)
        pending[group] = (nms, send, recv, reduced, lands)
        return token

    def update_group(group, after):
        nms, send, recv, reduced, lands = pending[group]
        reduced, from_sib = join_wait("join_wait_" + group, send, recv, reduced, lands, after)
        for nm, g_own, g_sib in zip(nms, reduced, from_sib):
            w, m, v = wmv[nm]
            shp = (2,) + g_own.shape
            big_res[nm] = adamw_halves("adamw_" + nm, w.reshape(shp), g_own, g_sib, m.reshape(shp), v.reshape(shp), core_i)

    def on_grad(group, grads):
        if group == "in":
            gw_in = grads_from_cat(grads["in_cat"])
            gw_pool = jnp.stack([grads["pool"][:, :, j * 128:(j + 1) * 128].reshape(2, 2 * PG, 128)
                                 for j in range(NCHIP)], axis=1)
            grads = {"in": gw_in, "pool": gw_pool}
        nms, parts = list(grads.keys()), list(grads.values())
        send, recv, parts, got, token = exchange_start("exchange_start_" + group, parts)
        pending[group] = (nms, send, recv, parts, got)
        return token

    def on_settle(group, after):
        if group == "in":
            after = reduce_group("down", after)
        nms, send, recv, parts, got = pending[group]
        parts, got = exchange_wait("exchange_wait_" + group, send, recv, parts, got, after)
        sums = [add_pairs("add_pair_" + nm, a, b, core_i) for nm, a, b in zip(nms, parts, got)]
        send, recv, sums, lands, token = scatter_start("scatter_start_" + group, sums)
        pending[group] = (nms, send, recv, sums, lands)
        if group != "in":
            return token
        token = reduce_group("up", token)
        token = reduce_group("mix", token)
        for earlier in ("down", "up", "mix"):
            update_group(earlier, token)
            token = big_res[pending[earlier][0][-1]][1]
        return [big_res[nm][1] for nm in ("down", "up", "out", "gla_out")]

    (loss_local, grad_x, g_mix, g_ps, g_mlp, g_nf, g_ng, g_ba, g_wa) = local_step(
        x2d, tgt, gf, norm_mix_g, pool_scale, wa_pad, b_alpha, ng, norm_mlp_g, get_w, on_grad, on_settle, tick)[:9]
    loss = lax.psum(loss_local, ("x", "y", "c"))
    join_in_token = reduce_group("in", grad_x)

    ROWS = 16

    def wide(a, n):
        return jnp.concatenate([a.reshape(1, n), jnp.zeros((1, D - n), F32)], axis=1)

    packed = pack_rows("pack_small_g", [g_mix, g_ps, g_mlp, g_nf, g_ng, wide(g_ba, QK), g_wa[0:16].reshape(8, D)], ROWS)
    tot = gather_small("reduce_small_g", packed, True, join_in_token)
    t_wa = lax.dynamic_slice(tot[6:14].reshape(16, QK), (0, chip * DK), (16, DK))
    t_ng = lax.dynamic_slice(tot[4].reshape(HEADS, DV), (0, chip * (DV // NCHIP)), (HEADS, DV // NCHIP))

    def pack_small(nm, mix, ps, mlp, nf, ba, wa, gn, after=None):
        return pack_rows(nm, [mix.reshape(1, D), ps.reshape(1, D), mlp.reshape(1, D), nf.reshape(1, D), wide(ba, QK),
                              wa.reshape(2, D), wide(gn, 512)], ROWS, after)

    update_group("in", tot)
    sg = pack_small("pack_g", tot[0], tot[1], tot[2], tot[3], tot[5, 0:QK], t_wa, t_ng, big_res["in"][3])
    sw = pack_small("pack_w", norm_mix_g, pool_scale, norm_mlp_g, norm_final_g, b_alpha, w_alpha, gla_norm_g)
    sm = pack_small("pack_m", m_norm_mix_g, m_pool_scale, m_norm_mlp_g, m_norm_final_g, m_b_alpha, m_w_alpha, m_gla_norm_g)
    sv = pack_small("pack_v", v_norm_mix_g, v_pool_scale, v_norm_mlp_g, v_norm_final_g, v_b_alpha, v_w_alpha, v_gla_norm_g)
    small_res = adamw("adamw_small", sw, sg, sm, sv)

    def unpack(p):
        return {"norm_mix_g": p[0].reshape(1, D), "pool_scale": p[1].reshape(1, D), "norm_mlp_g": p[2].reshape(1, D),
                "norm_final_g": p[3].reshape(D), "b_alpha": p[4, 0:QK].reshape(1, QK), "w_alpha": p[5:7].reshape(1, 16, DK),
                "gla_norm_g": p[7, 0:512].reshape(1, HEADS, DV // NCHIP)}

    order = ["norm_mix_g", "w_in", "pool_w", "pool_scale", "w_alpha", "b_alpha", "gla_norm_g", "w_gla_out", "w_out",
             "norm_mlp_g", "w_mlp_up", "w_mlp_down", "norm_final_g"]
    big_key = {"w_in": ("in", w_in.shape), "pool_w": ("pool", pool_w.shape), "w_gla_out": ("gla_out", w_gla_out.shape),
               "w_out": ("out", w_out.shape), "w_mlp_up": ("up", w_mlp_up.shape), "w_mlp_down": ("down", w_mlp_down.shape)}
    result = [loss, grad_x.reshape(1, T, D)]
    for kind in range(4):
        small = unpack(small_res[kind])
        for nm in order:
            if nm in big_key:
                key, shp = big_key[nm]
                result.append(big_res[key][kind].reshape(shp))
            else:
                result.append(small[nm])
    return tuple(result)
```

```python
import itertools

import jax
import jax.numpy as jnp
from jax import lax
from jax.experimental import pallas as pl
from jax.experimental.pallas import tpu as pltpu

F32 = jnp.float32
BF16 = jnp.bfloat16
SDS = jax.ShapeDtypeStruct
MESH = pl.DeviceIdType.MESH
ANY = pl.BlockSpec(memory_space=pl.ANY)

T = 2048
D = 2048
DFF = 8192
NCHIP = 4
IN_WIDTH = 11280
IN_SHARD = IN_WIDTH // NCHIP
CHUNK = 64
NCHUNK = T // CHUNK
HEADS = 4
DK = 256
DV = 512
QK = HEADS * DK
EPS = 1e-6
POOL_WINDOWS = (2, 4, 8, 16)
PG = 256
PO = 512

OV, OG, OGP, OGG, OU, OQ, OKK, OA = 0, 2048, 4096, 6144, 8192, 9216, 10240, 11264
NCAT = 11520
APAD = 128

VMEM_CAP = 56 * 1024 * 1024

PIECE_BYTES = 384 * 1024

ADAM_LR, ADAM_B1, ADAM_B2, ADAM_EPS, ADAM_WD, ADAM_STEP = 0.001, 0.9, 0.999, 1e-08, 0.01, 10


def _cparams(vmem_bytes=None, sem=None):
    kw = {}
    if vmem_bytes is not None:
        kw["vmem_limit_bytes"] = int(min(max(vmem_bytes, 32 * 1024 * 1024), VMEM_CAP))
    if sem is not None:
        kw["dimension_semantics"] = sem
    return pltpu.CompilerParams(**kw)


def _nbytes(shape, dtype):
    n = 1
    for s in shape:
        if s is not None:
            n *= s
    return n * jnp.dtype(dtype).itemsize


def _sigmoid(x):
    return 1.0 / (1.0 + jnp.exp(-x))


def _as_list(after):
    if after is None:
        return []
    return list(after) if isinstance(after, (list, tuple)) else [after]


def matmul(name, a, b, *, a_spec, b_spec, cdims, grid, acc_shape, outs, extras=(), epi, after=None):
    nj, ni, nk = grid
    ne, no = len(extras), len(outs)
    afters = _as_list(after)
    first_out = 2 + ne + len(afters)

    def body(*refs):
        a_ref, b_ref = refs[0], refs[1]
        ex = refs[2:2 + ne]
        out_refs = refs[first_out:first_out + no]
        i = pl.program_id(1)
        part = lax.dot_general(a_ref[...], b_ref[...], (cdims, ((), ())), preferred_element_type=F32)
        if nk == 1:
            epi(part, ex, out_refs, i)
        else:
            acc_ref = refs[first_out + no]
            k = pl.program_id(2)

            @pl.when(k == 0)
            def _():
                acc_ref[...] = part

            @pl.when(k > 0)
            def _():
                acc_ref[...] += part

            @pl.when(k == nk - 1)
            def _():
                epi(acc_ref[...], ex, out_refs, i)

    in_specs = [pl.BlockSpec(*a_spec), pl.BlockSpec(*b_spec)] + [pl.BlockSpec(bs, im) for _, bs, im in extras]
    in_specs += [ANY] * len(afters)
    out_specs = [pl.BlockSpec(bs, im) for _, _, bs, im in outs]
    out_shape = [SDS(s, dt) for s, dt, _, _ in outs]
    vm = 2 * (_nbytes(a_spec[0], a.dtype) + _nbytes(b_spec[0], b.dtype))
    vm += 2 * sum(_nbytes(bs, arr.dtype) for arr, bs, _ in extras)
    vm += 2 * sum(_nbytes(bs, dt) for _, dt, bs, _ in outs)
    vm += 6 * _nbytes(acc_shape, F32)
    scratch = [pltpu.VMEM(acc_shape, F32)] if nk > 1 else []
    return pl.pallas_call(
        body, name=name, grid=grid, in_specs=in_specs, out_specs=out_specs, out_shape=out_shape,
        scratch_shapes=scratch,
        compiler_params=_cparams(vm, ("arbitrary", "arbitrary", "arbitrary")),
    )(a, b, *[arr for arr, _, _ in extras], *afters)


NN =((1,), (0,))
NT = ((1,), (1,))
TN = ((0,), (0,))


def _row_acc(out_ref, val, i):
    @pl.when(i == 0)
    def _():
        out_ref[...] = val

    @pl.when(i > 0)
    def _():
        out_ref[...] += val


def _rms_bwd(xn, r, dxn):
    return r * (dxn - xn * jnp.mean(dxn * xn, axis=-1, keepdims=True))


def norm1(x, g):
    tm = 256

    def body(x_ref, g_ref, h_ref):
        xv = x_ref[...]
        r = lax.rsqrt(jnp.mean(xv * xv, axis=-1, keepdims=True) + EPS)
        h_ref[...] = (xv * r * g_ref[...]).astype(BF16)

    return pl.pallas_call(
        body, name="norm1", grid=(T // tm,),
        in_specs=[pl.BlockSpec((tm, D), lambda i: (i, 0)), pl.BlockSpec((1, D), lambda i: (0, 0))],
        out_specs=pl.BlockSpec((tm, D), lambda i: (i, 0)), out_shape=SDS((T, D), BF16),
        compiler_params=_cparams(32 * 1024 * 1024, ("arbitrary",)),
    )(x, g)


def mm_in(h1, wcat):
    tm, tn = 1024, 1280

    def epi(acc, ex, outs, i):
        outs[0][...] = acc.astype(BF16)

    return matmul("mm_in", h1, wcat, a_spec=((tm, D), lambda j, i, k: (i, 0)), b_spec=((D, tn), lambda j, i, k: (0, j)),
                  cdims=NN, grid=(NCAT // tn, T // tm, 1), acc_shape=(tm, tn),
                  outs=[((T, NCAT), BF16, (tm, tn), lambda j, i, k: (i, j))], epi=epi)[0]


def _window_sum(x, w, up):
    n = x.shape[0]
    row = lax.broadcasted_iota(jnp.int32, x.shape, 0)
    s, sh = x, 1
    while sh < w:
        if up:
            s = s + jnp.where(row < n - sh, pltpu.roll(s, n - sh, axis=0), 0.0)
        else:
            s = s + jnp.where(row >= sh, pltpu.roll(s, sh, axis=0), 0.0)
        sh *= 2
    return s


def _inv_count(shape, w):
    row = lax.broadcasted_iota(jnp.int32, shape, 0)
    return 1.0 / jnp.minimum(row + 1, w).astype(F32)


def pool_fwd(pcat, pw):
    def body(u_ref, pw_ref, d_ref, y_ref):
        for gi, w in enumerate(POOL_WINDOWS):
            ug = u_ref[:, gi * PG:(gi + 1) * PG].astype(F32)
            dg = _window_sum(ug, w, False) * _inv_count(ug.shape, w) - ug
            db = dg.astype(BF16)
            d_ref[:, gi * PG:(gi + 1) * PG] = db
            y_ref[:, gi * PO:(gi + 1) * PO] = jnp.dot(db, pw_ref[gi], preferred_element_type=F32).astype(BF16)

    return pl.pallas_call(
        body, name="pool_fwd", grid=(1,),
        in_specs=[pl.BlockSpec((T, 4 * PG), lambda i: (0, OU // (4 * PG))), pl.BlockSpec((4, PG, PO), lambda i: (0, 0, 0))],
        out_specs=[pl.BlockSpec((T, 4 * PG), lambda i: (0, 0)), pl.BlockSpec((T, D), lambda i: (0, 0))],
        out_shape=[SDS((T, 4 * PG), BF16), SDS((T, D), BF16)],
        compiler_params=_cparams(48 * 1024 * 1024, ("arbitrary",)),
    )(pcat, pw)


def pool_bwd(dylin, d, pw):
    def body(dy_ref, d_ref, pw_ref, du_ref, dpw_ref):
        for gi, w in enumerate(POOL_WINDOWS):
            dyl = dy_ref[:, gi * PO:(gi + 1) * PO]
            dd = lax.dot_general(dyl, pw_ref[gi], (NT, ((), ())), preferred_element_type=F32)
            du = _window_sum(dd * _inv_count(dd.shape, w), w, True) - dd
            du_ref[:, gi * PG:(gi + 1) * PG] = du.astype(BF16)
            dpw_ref[gi] = lax.dot_general(d_ref[:, gi * PG:(gi + 1) * PG], dyl, (TN, ((), ())),
                                          preferred_element_type=F32).astype(BF16)

    return pl.pallas_call(
        body, name="pool_bwd", grid=(1,),
        in_specs=[pl.BlockSpec((T, D), lambda i: (0, 0)), pl.BlockSpec((T, 4 * PG), lambda i: (0, 0)),
                  pl.BlockSpec((4, PG, PO), lambda i: (0, 0, 0))],
        out_specs=[pl.BlockSpec((T, 4 * PG), lambda i: (0, 0)), pl.BlockSpec((4, PG, PO), lambda i: (0, 0, 0))],
        out_shape=[SDS((T, 4 * PG), BF16), SDS((4, PG, PO), BF16)],
        compiler_params=_cparams(48 * 1024 * 1024, ("arbitrary",)),
    )(dylin, d, pw)


def _gate_decay(alow, wa, ba):
    a = jnp.dot(alow, wa, preferred_element_type=F32) + ba
    ls = jax.nn.log_sigmoid(a) * (1.0 / 16.0)
    r = lax.broadcasted_iota(jnp.int32, (CHUNK, CHUNK), 0)
    c = lax.broadcasted_iota(jnp.int32, (CHUNK, CHUNK), 1)
    tri = jnp.where(c <= r, 1.0, 0.0).astype(F32)
    cum = jnp.dot(tri, ls, preferred_element_type=F32, precision=lax.Precision.HIGHEST)
    last = cum[CHUNK - 1:CHUNK, :]
    return a, jnp.exp(last - cum), jnp.exp(last)


def gla_fwd(pcat, wa, ba, ng, after=None):
    afters = _as_list(after)

    def body(q_ref, k_ref, v_ref, g_ref, al_ref, wa_ref, ba_ref, ng_ref, *rest):
        og_ref, o_ref, st_ref, s_scr = rest[len(afters):]

        @pl.when(pl.program_id(0) == 0)
        def _():
            s_scr[...] = jnp.zeros_like(s_scr)

        _, e, decay = _gate_decay(al_ref[...], wa_ref[...], ba_ref[...])
        kd = (k_ref[...].astype(F32) * e).astype(BF16)
        qs = (q_ref[...].astype(F32) * (DK ** -0.5)).astype(BF16)
        for h in range(HEADS):
            ck = slice(h * DK, (h + 1) * DK)
            cv = slice(h * DV, (h + 1) * DV)
            s_new = s_scr[h] * decay[:, ck] + lax.dot_general(v_ref[:, cv], kd[:, ck], (TN, ((), ())),
                                                               preferred_element_type=F32)
            s_scr[h] = s_new
            sb = s_new.astype(BF16)
            st_ref[h] = sb
            oh = lax.dot_general(qs[:, ck], sb, (NT, ((), ())), preferred_element_type=F32)
            o_ref[:, cv] = oh.astype(BF16)
            on = oh * lax.rsqrt(jnp.mean(oh * oh, axis=-1, keepdims=True) + EPS) * ng_ref[:, cv]
            gv = g_ref[:, cv].astype(F32)
            og_ref[:, cv] = (on * (gv * _sigmoid(gv))).astype(BF16)

    row = lambda c: (c, 0)
    return pl.pallas_call(
        body, name="gla_fwd", grid=(NCHUNK,),
        in_specs=[pl.BlockSpec((CHUNK, QK), lambda c: (c, OQ // QK)), pl.BlockSpec((CHUNK, QK), lambda c: (c, OKK // QK)),
                  pl.BlockSpec((CHUNK, D), lambda c: (c, OV // D)), pl.BlockSpec((CHUNK, D), lambda c: (c, OG // D)),
                  pl.BlockSpec((CHUNK, APAD), lambda c: (c, OA // APAD)),
                  pl.BlockSpec((APAD, QK), lambda c: (0, 0)), pl.BlockSpec((1, QK), lambda c: (0, 0)),
                  pl.BlockSpec((1, D), lambda c: (0, 0))] + [ANY] * len(afters),
        out_specs=[pl.BlockSpec((CHUNK, D), row), pl.BlockSpec((CHUNK, D), row),
                   pl.BlockSpec((None, HEADS, DV, DK), lambda c: (c, 0, 0, 0))],
        out_shape=[SDS((T, D), BF16), SDS((T, D), BF16), SDS((NCHUNK, HEADS, DV, DK), BF16)],
        scratch_shapes=[pltpu.VMEM((HEADS, DV, DK), F32)],
        compiler_params=_cparams(32 * 1024 * 1024, ("arbitrary",)),
    )(pcat, pcat, pcat, pcat, pcat, wa, ba, ng, *afters)


def gla_bwd(do, pcat, states, wa, ba, after):
    def body(do_ref, q_ref, k_ref, v_ref, al_ref, sc_ref, sp_ref, wa_ref, ba_ref, after_ref,
             dq_ref, dk_ref, dv_ref, dal_ref, dwa_ref, dba_ref, ds_scr):
        i = pl.program_id(0)

        @pl.when(i == 0)
        def _():
            ds_scr[...] = jnp.zeros_like(ds_scr)

        has_prev = jnp.where(i < NCHUNK - 1, 1.0, 0.0).astype(F32)
        a, e, decay = _gate_decay(al_ref[...], wa_ref[...], ba_ref[...])
        kf = k_ref[...].astype(F32)
        kdf = kf * e
        kd = kdf.astype(BF16)
        qs = (q_ref[...].astype(F32) * (DK ** -0.5)).astype(BF16)
        dkd_parts, ddecay_parts = [], []
        for h in range(HEADS):
            ck = slice(h * DK, (h + 1) * DK)
            cv = slice(h * DV, (h + 1) * DV)
            doh = do_ref[:, cv]
            ds = ds_scr[h] + lax.dot_general(doh, qs[:, ck], (TN, ((), ())), preferred_element_type=F32)
            dsb = ds.astype(BF16)
            dq_ref[:, ck] = (jnp.dot(doh, sc_ref[h], preferred_element_type=F32) * (DK ** -0.5)).astype(BF16)
            dkd_parts.append(jnp.dot(v_ref[:, cv], dsb, preferred_element_type=F32))
            dv_ref[:, cv] = lax.dot_general(kd[:, ck], dsb, (NT, ((), ())), preferred_element_type=F32).astype(BF16)
            ddecay_parts.append(jnp.sum(ds * sp_ref[h].astype(F32), axis=0, keepdims=True) * has_prev)
            ds_scr[h] = ds * decay[:, ck]
        dkd = jnp.concatenate(dkd_parts, axis=1)
        ddecay = jnp.concatenate(ddecay_parts, axis=1)
        dk_ref[...] = (dkd * e).astype(BF16)
        dearg = dkd * kdf
        dlast = jnp.sum(dearg, axis=0, keepdims=True) + ddecay * decay
        r = lax.broadcasted_iota(jnp.int32, (CHUNK, CHUNK), 0)
        c = lax.broadcasted_iota(jnp.int32, (CHUNK, CHUNK), 1)
        triu = jnp.where(c >= r, 1.0, 0.0).astype(F32)
        dls = dlast - jnp.dot(triu, dearg, preferred_element_type=F32, precision=lax.Precision.HIGHEST)
        da = dls * (1.0 / 16.0) * (1.0 - _sigmoid(a))
        dab = da.astype(BF16)
        dal_ref[...] = lax.dot_general(dab, wa_ref[...], (NT, ((), ())), preferred_element_type=F32).astype(BF16)
        dwa = lax.dot_general(al_ref[...], dab, (TN, ((), ())), preferred_element_type=F32)
        dba = jnp.sum(da, axis=0, keepdims=True)

        @pl.when(i == 0)
        def _():
            dwa_ref[...] = dwa
            dba_ref[...] = dba

        @pl.when(i > 0)
        def _():
            dwa_ref[...] += dwa
            dba_ref[...] += dba

    rev = lambda i: NCHUNK - 1 - i
    return pl.pallas_call(
        body, name="gla_bwd", grid=(NCHUNK,),
        in_specs=[pl.BlockSpec((CHUNK, D), lambda i: (rev(i), 0)),
                  pl.BlockSpec((CHUNK, QK), lambda i: (rev(i), OQ // QK)), pl.BlockSpec((CHUNK, QK), lambda i: (rev(i), OKK // QK)),
                  pl.BlockSpec((CHUNK, D), lambda i: (rev(i), OV // D)), pl.BlockSpec((CHUNK, APAD), lambda i: (rev(i), OA // APAD)),
                  pl.BlockSpec((None, HEADS, DV, DK), lambda i: (rev(i), 0, 0, 0)),
                  pl.BlockSpec((None, HEADS, DV, DK), lambda i: (jnp.maximum(rev(i) - 1, 0), 0, 0, 0)),
                  pl.BlockSpec((APAD, QK), lambda i: (0, 0)), pl.BlockSpec((1, QK), lambda i: (0, 0)), ANY],
        out_specs=[pl.BlockSpec((CHUNK, QK), lambda i: (rev(i), 0)), pl.BlockSpec((CHUNK, QK), lambda i: (rev(i), 0)),
                   pl.BlockSpec((CHUNK, D), lambda i: (rev(i), 0)), pl.BlockSpec((CHUNK, APAD), lambda i: (rev(i), 0)),
                   pl.BlockSpec((APAD, QK), lambda i: (0, 0)), pl.BlockSpec((1, QK), lambda i: (0, 0))],
        out_shape=[SDS((T, QK), BF16), SDS((T, QK), BF16), SDS((T, D), BF16), SDS((T, APAD), BF16),
                   SDS((APAD, QK), F32), SDS((1, QK), F32)],
        scratch_shapes=[pltpu.VMEM((HEADS, DV, DK), F32)],
        compiler_params=_cparams(32 * 1024 * 1024, ("arbitrary",)),
    )(do, pcat, pcat, pcat, pcat, states, states, wa, ba, after)


TMF = 256
TMW = 512
_rowblk = ((TMF, D), lambda j, i, k: (i, 0))
_vec = ((1, D), lambda j, i, k: (0, 0))


def _full_spec(col):
    return ((TMF, D), lambda j, i, k: (i, col))


TBIG = 1024


def square_matmul(name, a, b, *, a_spec, b_spec, cdims, nk, after=None):
    def epi(acc, ex, outs, i):
        outs[0][...] = acc

    return matmul(name, a, b, a_spec=a_spec, b_spec=b_spec, cdims=cdims, grid=(D // TBIG, T // TBIG, nk),
                  acc_shape=(TBIG, TBIG), outs=[((T, D), F32, (TBIG, TBIG), lambda j, i, k: (i, j))], epi=epi,
                  after=after)[0]


def rowwise(name, y, *, extras, outs, epi):
    ne = len(extras)

    def body(*refs):
        epi(refs[0][...], refs[1:1 + ne], refs[1 + ne:], pl.program_id(1))

    in_specs = [pl.BlockSpec(*_rowblk)] + [pl.BlockSpec(bs, im) for _, bs, im in extras]
    return pl.pallas_call(
        body, name=name, grid=(1, T // TMF, 1), in_specs=in_specs,
        out_specs=[pl.BlockSpec(bs, im) for _, _, bs, im in outs], out_shape=[SDS(s, dt) for s, dt, _, _ in outs],
        compiler_params=_cparams(40 * 1024 * 1024, ("arbitrary", "arbitrary", "arbitrary")),
    )(y, *[arr for arr, _, _ in extras])


def mm_gla_out(og, w, ylin, pcat, pscale):
    def epi(acc, ex, outs, i):
        ylin_ref, lgp_ref, lgg_ref, ps_ref = ex
        gp = _sigmoid(lgp_ref[...].astype(F32))
        gg = _sigmoid(lgg_ref[...].astype(F32))
        outs[0][...] = (gp * (ylin_ref[...].astype(F32) * ps_ref[...]) + gg * acc).astype(BF16)
        outs[1][...] = acc.astype(BF16)

    return matmul("mm_gla_out", og, w, a_spec=_rowblk, b_spec=((D, D), lambda j, i, k: (0, 0)), cdims=NN,
                  grid=(1, T // TMF, 1), acc_shape=(TMF, D),
                  extras=[(ylin, *_rowblk), (pcat, *_full_spec(OGP // D)), (pcat, *_full_spec(OGG // D)), (pscale, *_vec)],
                  outs=[((T, D), BF16, *_rowblk), ((T, D), BF16, *_rowblk)], epi=epi)


def mm_out(mixed, w, x, g2):
    def epi(acc, ex, outs, i):
        x_ref, g_ref = ex
        x2 = x_ref[...] + acc
        r = lax.rsqrt(jnp.mean(x2 * x2, axis=-1, keepdims=True) + EPS)
        outs[0][...] = x2
        outs[1][...] = (x2 * r * g_ref[...]).astype(BF16)

    return matmul("mm_out", mixed, w, a_spec=_rowblk, b_spec=((D, D), lambda j, i, k: (0, 0)), cdims=NN,
                  grid=(1, T // TMF, 1), acc_shape=(TMF, D), extras=[(x, *_rowblk), (g2, *_vec)],
                  outs=[((T, D), F32, *_rowblk), ((T, D), BF16, *_rowblk)], epi=epi)


def mm_up(h2, wup):
    def epi(acc, ex, outs, i):
        r = jnp.maximum(acc, 0.0)
        outs[0][...] = r.astype(BF16)
        outs[1][...] = (r * r).astype(BF16)

    blk = ((TMW, D), lambda j, i, k: (i, j))
    return matmul("mm_up", h2, wup, a_spec=((TMW, D), lambda j, i, k: (i, 0)), b_spec=((None, D, D), lambda j, i, k: (j, 0, 0)),
                  cdims=NN, grid=(NCHIP, T // TMW, 1), acc_shape=(TMW, D),
                  outs=[((T, DFF), BF16, *blk), ((T, DFF), BF16, *blk)], epi=epi)


def mm_down(act, wdown, x2, tgt, gf):
    tk = 2048

    def epi(acc, ex, outs, i):
        x2_ref, t_ref, g_ref = ex
        dx_ref, dxb_ref, gnf_ref, loss_ref = outs
        x3 = x2_ref[...] + acc
        r = lax.rsqrt(jnp.mean(x3 * x3, axis=-1, keepdims=True) + EPS)
        xn = x3 * r
        err = xn * g_ref[...] - t_ref[...]
        lsum = 0.5 * jnp.sum(jnp.mean(err * err, axis=-1, keepdims=True), axis=0, keepdims=True)
        dy = err * (1.0 / D)
        _row_acc(gnf_ref, jnp.sum(dy * xn, axis=0, keepdims=True), i)
        _row_acc(loss_ref, jnp.broadcast_to(lsum, (1, 128)), i)
        dx3 = _rms_bwd(xn, r, dy * g_ref[...])
        dx_ref[...] = dx3
        dxb_ref[...] = dx3.astype(BF16)

    y = square_matmul("mm_down", act, wdown, a_spec=((TBIG, tk), lambda j, i, k: (i, k)),
                      b_spec=((tk, TBIG), lambda j, i, k: (k, j)), cdims=NN, nk=DFF // tk)
    return rowwise("rows_final", y, extras=[(x2, *_rowblk), (tgt, *_rowblk), (gf, *_vec)],
                   outs=[((T, D), F32, *_rowblk), ((T, D), BF16, *_rowblk), ((1, D), F32, *_vec),
                         ((1, 128), F32, (1, 128), lambda j, i, k: (0, 0))], epi=epi)


def mm_dact(dx3b, wdown, rup, after=None):
    def epi(acc, ex, outs, i):
        outs[0][...] = (acc * 2.0 * ex[0][...].astype(F32)).astype(BF16)

    blk = ((TMW, D), lambda j, i, k: (i, j))
    return matmul("mm_dact", dx3b, wdown, a_spec=((TMW, D), lambda j, i, k: (i, 0)), b_spec=((D, D), lambda j, i, k: (j, 0)),
                  cdims=NT, grid=(DFF // D, T // TMW, 1), acc_shape=(TMW, D), extras=[(rup, *blk)],
                  outs=[((T, DFF), BF16, *blk)], epi=epi, after=after)[0]


def mm_wgrad(name, a, b, m, n, out_shape, out_block, out_map, tm, tn, after=None):
    def epi(acc, ex, outs, i):
        outs[0][...] = acc.astype(BF16)

    return matmul(name, a, b, a_spec=((T, tm), lambda j, i, k: (0, i)), b_spec=((T, tn), lambda j, i, k: (0, j)),
                  cdims=TN, grid=(n // tn, m // tm, 1), acc_shape=(tm, tn),
                  outs=[(out_shape, BF16, out_block, out_map)], epi=epi, after=after)[0]


def mm_dh2(dup, wup, x2, dx3, g2, after=None):
    def epi(acc, ex, outs, i):
        x2_ref, dx3_ref, g_ref = ex
        x2 = x2_ref[...]
        r = lax.rsqrt(jnp.mean(x2 * x2, axis=-1, keepdims=True) + EPS)
        xn = x2 * r
        _row_acc(outs[2], jnp.sum(acc * xn, axis=0, keepdims=True), i)
        dx2 = dx3_ref[...] + _rms_bwd(xn, r, acc * g_ref[...])
        outs[0][...] = dx2
        outs[1][...] = dx2.astype(BF16)

    y = square_matmul("mm_dh2", dup, wup, a_spec=((TBIG, D), lambda j, i, k: (i, k)),
                      b_spec=((None, TBIG, D), lambda j, i, k: (k, j, 0)), cdims=NT, nk=NCHIP, after=after)
    return rowwise("rows_dh2", y, extras=[(x2, *_rowblk), (dx3, *_rowblk), (g2, *_vec)],
                   outs=[((T, D), F32, *_rowblk), ((T, D), BF16, *_rowblk), ((1, D), F32, *_vec)], epi=epi)


def mm_dmixed(dx2b, wout, pcat, ylin, ygla, pscale, after=None):
    def epi(acc, ex, outs, i):
        lgp_ref, lgg_ref, ylin_ref, ygla_ref, ps_ref = ex
        gp = _sigmoid(lgp_ref[...].astype(F32))
        gg = _sigmoid(lgg_ref[...].astype(F32))
        yl = ylin_ref[...].astype(F32)
        ps = ps_ref[...]
        agp = acc * gp
        outs[0][...] = (agp * ps).astype(BF16)
        outs[1][...] = (acc * gg).astype(BF16)
        outs[2][...] = (agp * (yl * ps) * (1.0 - gp)).astype(BF16)
        outs[3][...] = (acc * ygla_ref[...].astype(F32) * gg * (1.0 - gg)).astype(BF16)
        _row_acc(outs[4], jnp.sum(agp * yl, axis=0, keepdims=True), i)

    return matmul("mm_dmixed", dx2b, wout, a_spec=_rowblk, b_spec=((D, D), lambda j, i, k: (0, 0)), cdims=NT,
                  grid=(1, T // TMF, 1), acc_shape=(TMF, D),
                  extras=[(pcat, *_full_spec(OGP // D)), (pcat, *_full_spec(OGG // D)), (ylin, *_rowblk), (ygla, *_rowblk),
                          (pscale, *_vec)],
                  outs=[((T, D), BF16, *_rowblk)] * 4 + [((1, D), F32, *_vec)], epi=epi, after=after)


def mm_dog(dygla, wgo, o, pcat, ng, after=None):
    def epi(acc, ex, outs, i):
        o_ref, g_ref, ng_ref = ex
        do_ref, dg_ref, gng_ref = outs
        gparts = []
        for h in range(HEADS):
            cv = slice(h * DV, (h + 1) * DV)
            oh = o_ref[:, cv].astype(F32)
            r = lax.rsqrt(jnp.mean(oh * oh, axis=-1, keepdims=True) + EPS)
            on = oh * r
            gv = g_ref[:, cv].astype(F32)
            sg = _sigmoid(gv)
            dgain = acc[:, cv] * (gv * sg)
            gparts.append(jnp.sum(dgain * on, axis=0, keepdims=True))
            ngh = ng_ref[:, cv]
            do_ref[:, cv] = _rms_bwd(on, r, dgain * ngh).astype(BF16)
            dg_ref[:, cv] = (acc[:, cv] * (on * ngh) * (sg * (1.0 + gv * (1.0 - sg)))).astype(BF16)
        _row_acc(gng_ref, jnp.concatenate(gparts, axis=1), i)

    return matmul("mm_dog", dygla, wgo, a_spec=_rowblk, b_spec=((D, D), lambda j, i, k: (0, 0)), cdims=NT,
                  grid=(1, T // TMF, 1), acc_shape=(TMF, D),
                  extras=[(o, *_rowblk), (pcat, *_full_spec(OG // D)), (ng, *_vec)],
                  outs=[((T, D), BF16, *_rowblk), ((T, D), BF16, *_rowblk), ((1, D), F32, *_vec)], epi=epi, after=after)


def mm_dh1(dpcat, wcat, x, dx2, g1, after=None):
    tk = 2304

    def epi(acc, ex, outs, i):
        x_ref, dx2_ref, g_ref = ex
        xv = x_ref[...]
        r = lax.rsqrt(jnp.mean(xv * xv, axis=-1, keepdims=True) + EPS)
        xn = xv * r
        _row_acc(outs[1], jnp.sum(acc * xn, axis=0, keepdims=True), i)
        outs[0][...] = dx2_ref[...] + _rms_bwd(xn, r, acc * g_ref[...])

    y = square_matmul("mm_dh1", dpcat, wcat, a_spec=((TBIG, tk), lambda j, i, k: (i, k)),
                      b_spec=((TBIG, tk), lambda j, i, k: (j, k)), cdims=NT, nk=NCAT // tk, after=after)
    return rowwise("rows_dh1", y, extras=[(x, *_rowblk), (dx2, *_rowblk), (g1, *_vec)],
                   outs=[((T, D), F32, *_rowblk), ((1, D), F32, *_vec)], epi=epi)


def _tile_rows(rows, cols, n_arrays):
    tm = rows
    while tm % 32 == 0 and 2 * n_arrays * tm * cols * 4 > 24 * 1024 * 1024:
        tm //= 2
    return tm


def add_pairs(name, parts, theirs, core):
    _, _, r, c = parts.shape
    tm = _tile_rows(r, c, 3)

    def body(core_ref, a_ref, b_ref, o_ref):
        o_ref[...] = (a_ref[...].astype(F32) + b_ref[...].astype(F32)).astype(BF16)

    spec = pl.BlockSpec((None, tm, c), lambda j, i, core_ref: (j, i, 0))
    grid_spec = pltpu.PrefetchScalarGridSpec(
        num_scalar_prefetch=1, grid=(NCHIP, r // tm),
        in_specs=[pl.BlockSpec((None, None, tm, c), lambda j, i, core_ref: (core_ref[0], j, i, 0)), spec], out_specs=spec)
    return pl.pallas_call(body, name=name, grid_spec=grid_spec, out_shape=SDS((NCHIP, r, c), BF16),
                          compiler_params=_cparams(40 * 1024 * 1024, ("arbitrary", "arbitrary")))(core, parts, theirs)


def sum_chips(name, sums, landed, chip):
    _, r, c = sums.shape
    tm = _tile_rows(r, c, 4)

    def body(chip_ref, own_ref, l_ref, o_ref):
        s = own_ref[...].astype(F32)
        for t in range(NCHIP - 1):
            s = s + l_ref[t].astype(F32)
        o_ref[...] = s

    grid_spec = pltpu.PrefetchScalarGridSpec(
        num_scalar_prefetch=1, grid=(r // tm,),
        in_specs=[pl.BlockSpec((None, tm, c), lambda i, chip_ref: (chip_ref[0], i, 0)),
                  pl.BlockSpec((NCHIP - 1, tm, c), lambda i, chip_ref: (0, i, 0))],
        out_specs=pl.BlockSpec((tm, c), lambda i, chip_ref: (i, 0)))
    return pl.pallas_call(body, name=name, grid_spec=grid_spec, out_shape=SDS((r, c), F32),
                          compiler_params=_cparams(40 * 1024 * 1024, ("arbitrary",)))(chip, sums, landed)


def _adamw_math(wv, gv, mv, vv):
    mn = ADAM_B1 * mv + (1.0 - ADAM_B1) * gv
    vn = ADAM_B2 * vv + (1.0 - ADAM_B2) * (gv * gv)
    mh = mn / (1.0 - ADAM_B1 ** ADAM_STEP)
    vh = vn / (1.0 - ADAM_B2 ** ADAM_STEP)
    return -ADAM_LR * (mh / (jnp.sqrt(vh) + ADAM_EPS) + ADAM_WD * wv), mn, vn


def adamw(name, w, g, m, v):
    def body(w_ref, g_ref, m_ref, v_ref, go_ref, d_ref, mo_ref, vo_ref):
        gv = g_ref[...]
        go_ref[...] = gv
        d_ref[...], mo_ref[...], vo_ref[...] = _adamw_math(w_ref[...], gv, m_ref[...], v_ref[...])

    return pl.pallas_call(body, name=name, out_shape=[SDS(w.shape, F32)] * 4)(w, g, m, v)


def adamw_halves(name, w, g_own, g_sib, m, v, core):
    _, r, c = w.shape
    tm = _tile_rows(r, c, 10)

    def body(core_ref, w_ref, go_ref, gs_ref, m_ref, v_ref, g_out, d_out, m_out, v_out):
        gv = jnp.where(pl.program_id(0) == core_ref[0], go_ref[...], gs_ref[...])
        g_out[...] = gv
        d_out[...], m_out[...], v_out[...] = _adamw_math(w_ref[...], gv, m_ref[...], v_ref[...])

    full = pl.BlockSpec((None, tm, c), lambda h, i, core_ref: (h, i, 0))
    own = pl.BlockSpec((tm, c), lambda h, i, core_ref: (jnp.where(h == core_ref[0], i, 0), 0))
    sib = pl.BlockSpec((tm, c), lambda h, i, core_ref: (jnp.where(h == core_ref[0], 0, i), 0))
    grid_spec = pltpu.PrefetchScalarGridSpec(num_scalar_prefetch=1, grid=(2, r // tm),
                                             in_specs=[full, own, sib, full, full], out_specs=[full] * 4)
    return pl.pallas_call(body, name=name, grid_spec=grid_spec, out_shape=[SDS(w.shape, F32)] * 4,
                          compiler_params=_cparams(48 * 1024 * 1024, ("arbitrary", "arbitrary")))(core, w, g_own, g_sib, m, v)


def pack_rows(name, parts, rows, after=None):
    width = parts[0].shape[1]
    n = len(parts)
    afters = _as_list(after)

    def body(*refs):
        out_ref = refs[n + len(afters)]
        out_ref[...] = jnp.zeros_like(out_ref)
        off = 0
        for p in refs[:n]:
            out_ref[off:off + p.shape[0], :] = p[...]
            off += p.shape[0]

    vm = pl.BlockSpec(memory_space=pltpu.VMEM)
    return pl.pallas_call(body, name=name, in_specs=[vm] * n + [ANY] * len(afters), out_specs=vm,
                          out_shape=SDS((rows, width), F32))(*parts, *afters)


def _place():
    x, y, c = lax.axis_index("x"), lax.axis_index("y"), lax.axis_index("c")
    chips = [(1 - x, y), (x, 1 - y), (1 - x, 1 - y)]
    return x, y, c, chips


def _row_split(shape, dtype):
    r, c = shape
    n = 1
    while r % (2 * n) == 0 and (r // (2 * n)) % 16 == 0 and (r // n) * c * jnp.dtype(dtype).itemsize > PIECE_BYTES:
        n *= 2
    return [pl.ds(s * (r // n), r // n) for s in range(n)]


def _pieces(ref):
    *lead, r, c = ref.shape
    split = _row_split((r, c), ref.dtype)
    return [ref.at[(*idx, s)] for idx in itertools.product(*[range(d) for d in lead]) for s in split]


HBM = pl.BlockSpec(memory_space=pltpu.HBM)
SEM = pl.BlockSpec(memory_space=pltpu.SEMAPHORE)
EFFECT = pltpu.SideEffectType.DATAFLOW_SIDE_EFFECTING


def gather_start(name, shards, after=None):
    n = len(shards)
    afters = _as_list(after)

    def body(*refs):
        src, land = refs[:n], refs[n:2 * n]
        send, recv = refs[2 * n + len(afters)], refs[2 * n + len(afters) + 1]
        x, y, c, chips = _place()
        me = 2 * x + y
        for a in range(n):
            for j, (cx, cy) in enumerate(chips[:2]):
                for sp, dp in zip(_pieces(src[a].at[c]), _pieces(land[a].at[me, c])):
                    pltpu.make_async_remote_copy(sp, dp, send.at[2 * a + j], recv.at[2 * a + j],
                                                 device_id=(cx, cy, c), device_id_type=MESH).start()

    lands = [pltpu.with_memory_space_constraint(lax.empty((NCHIP,) + s.shape, s.dtype), pltpu.HBM) for s in shards]
    srcs = [pltpu.with_memory_space_constraint(s, pltpu.HBM) for s in shards]
    outs = pl.pallas_call(
        body, name=name,
        out_shape=(pltpu.SemaphoreType.DMA((2 * n,)), pltpu.SemaphoreType.DMA((2 * n,)),
                   *[pltpu.HBM(s.shape, s.dtype) for s in shards], *[pltpu.HBM(l.shape, l.dtype) for l in lands]),
        in_specs=[HBM] * (2 * n) + [ANY] * len(afters), out_specs=(SEM, SEM, *([HBM] * (2 * n))),
        input_output_aliases={i: 2 + i for i in range(2 * n)},
        compiler_params=pltpu.CompilerParams(has_side_effects=EFFECT),
    )(*srcs, *lands, *afters)
    return outs[0], outs[1], list(outs[2:2 + n]), list(outs[2 + n:2 + 2 * n])


def gather_wait(name, send, recv, shards, lands, after):
    n = len(shards)
    afters = _as_list(after)

    def body(*refs):
        src, land = refs[:n], refs[n:2 * n]
        send_ref, recv_ref = refs[2 * n], refs[2 * n + 1]
        x, y, c, chips = _place()
        for a in range(n):
            for j, (cx, cy) in enumerate(chips[:2]):
                cp = pltpu.make_async_remote_copy(src[a].at[c], land[a].at[2 * cx + cy, c], send_ref.at[2 * a + j],
                                                  recv_ref.at[2 * a + j], device_id=(cx, cy, c), device_id_type=MESH)
                cp.wait_send()
                cp.wait_recv()

    outs = pl.pallas_call(
        body, name=name,
        out_shape=(*[pltpu.HBM(s.shape, s.dtype) for s in shards], *[pltpu.HBM(l.shape, l.dtype) for l in lands]),
        in_specs=[HBM] * (2 * n) + [SEM, SEM] + [ANY] * len(afters), out_specs=[HBM] * (2 * n),
        input_output_aliases={i: i for i in range(2 * n)},
        compiler_params=pltpu.CompilerParams(has_side_effects=EFFECT),
    )(*shards, *lands, send, recv, *afters)
    return list(outs[:n]), list(outs[n:])


def _relay_blocks(land, c, chips):
    (xx, xy), (yx, yy), (dx, dy) = chips
    rows = land.shape[2] // 2
    upper, lower = pl.ds(0, rows), pl.ds(rows, rows)
    return [(land.at[2 * yx + yy, c, lower], land.at[2 * dx + dy, c, lower]),
            (land.at[2 * xx + xy, c, upper], land.at[2 * dx + dy, c, upper])]


def relay_start(name, lands, after=None):
    n = len(lands)
    afters = _as_list(after)

    def body(*refs):
        had, land = refs[:n], refs[n + len(afters) + 2:2 * n + len(afters) + 2]
        send, recv = refs[n + len(afters)], refs[n + len(afters) + 1]
        x, y, c, chips = _place()
        for a in range(n):
            for j, ((sent, _), (dst, _)) in enumerate(zip(_relay_blocks(had[a], c, chips), _relay_blocks(land[a], c, chips))):
                cx, cy = chips[j]
                for sp, dp in zip(_pieces(sent), _pieces(dst)):
                    pltpu.make_async_remote_copy(sp, dp, send.at[2 * a + j], recv.at[2 * a + j],
                                                 device_id=(cx, cy, c), device_id_type=MESH).start()

    outs = pl.pallas_call(
        body, name=name,
        out_shape=(pltpu.SemaphoreType.DMA((2 * n,)), pltpu.SemaphoreType.DMA((2 * n,)),
                   *[pltpu.HBM(l.shape, l.dtype) for l in lands]),
        in_specs=[HBM] * n + [ANY] * len(afters), out_specs=(SEM, SEM, *([HBM] * n)),
        input_output_aliases={i: 2 + i for i in range(n)},
        compiler_params=pltpu.CompilerParams(has_side_effects=EFFECT),
    )(*lands, *afters)
    return outs[0], outs[1], list(outs[2:])


def relay_wait(name, send, recv, lands, after):
    n = len(lands)
    afters = _as_list(after)

    def body(*refs):
        land = refs[:n]
        send_ref, recv_ref = refs[n], refs[n + 1]
        x, y, c, chips = _place()
        for a in range(n):
            for j, (sent, got) in enumerate(_relay_blocks(land[a], c, chips)):
                cx, cy = chips[j]
                cp = pltpu.make_async_remote_copy(sent, got, send_ref.at[2 * a + j], recv_ref.at[2 * a + j],
                                                  device_id=(cx, cy, c), device_id_type=MESH)
                cp.wait_send()
                cp.wait_recv()

    outs = pl.pallas_call(
        body, name=name, out_shape=tuple(pltpu.HBM(l.shape, l.dtype) for l in lands),
        in_specs=[HBM] * n + [SEM, SEM] + [ANY] * len(afters), out_specs=[HBM] * n,
        input_output_aliases={i: i for i in range(n)},
        compiler_params=pltpu.CompilerParams(has_side_effects=EFFECT),
    )(*lands, send, recv, *afters)
    return list(outs)


def forward_halves(name, shards, lands):
    n = len(lands)

    def body(*refs):
        had, buf = refs[:n], refs[n:2 * n]
        send, recv = refs[2 * n:]
        x, y, c, chips = _place()
        sib = (x, y, 1 - c)
        for a in range(n):
            for j, (cx, cy) in enumerate(chips):
                for sp, dp in zip(_pieces(had[a].at[2 * cx + cy, c]), _pieces(buf[a].at[2 * cx + cy, c])):
                    pltpu.make_async_remote_copy(sp, dp, send.at[3 * a + j], recv.at[3 * a + j], device_id=sib, device_id_type=MESH).start()
        for a in range(n):
            for j, (cx, cy) in enumerate(chips):
                pltpu.make_async_remote_copy(had[a].at[2 * cx + cy, c], buf[a].at[2 * cx + cy, 1 - c], send.at[3 * a + j],
                                             recv.at[3 * a + j], device_id=sib, device_id_type=MESH).wait()

    got = pl.pallas_call(
        body, name=name, in_specs=[ANY] * n, out_specs=[ANY] * n, out_shape=[SDS(l.shape, l.dtype) for l in lands],
        input_output_aliases={i: i for i in range(n)},
        scratch_shapes=[pltpu.SemaphoreType.DMA((3 * n,)), pltpu.SemaphoreType.DMA((3 * n,))],
    )(*lands)
    me = 2 * lax.axis_index("x") + lax.axis_index("y")
    return [lax.dynamic_update_index_in_dim(g, s, me, 0) for g, s in zip(got, shards)]


def exchange_start(name, parts):
    n = len(parts)

    def body(*refs):
        src, got = refs[:n], refs[n:2 * n]
        send, recv = refs[2 * n], refs[2 * n + 1]
        token = refs[4 * n + 2]
        x, y, c, _ = _place()
        sib = (x, y, 1 - c)
        for a in range(n):
            for sp, dp in zip(_pieces(src[a].at[1 - c]), _pieces(got[a])):
                pltpu.make_async_remote_copy(sp, dp, send.at[a], recv.at[a], device_id=sib, device_id_type=MESH).start()
        token[...] = jnp.zeros_like(token)

    lands = [pltpu.with_memory_space_constraint(lax.empty(p.shape[1:], p.dtype), pltpu.HBM) for p in parts]
    srcs = [pltpu.with_memory_space_constraint(p, pltpu.HBM) for p in parts]
    outs = pl.pallas_call(
        body, name=name,
        out_shape=(pltpu.SemaphoreType.DMA((n,)), pltpu.SemaphoreType.DMA((n,)),
                   *[pltpu.HBM(p.shape, p.dtype) for p in parts], *[pltpu.HBM(l.shape, l.dtype) for l in lands],
                   SDS((8, 128), F32)),
        in_specs=[HBM] * (2 * n), out_specs=(SEM, SEM, *([HBM] * (2 * n)), pl.BlockSpec(memory_space=pltpu.VMEM)),
        input_output_aliases={i: 2 + i for i in range(2 * n)},
        compiler_params=pltpu.CompilerParams(has_side_effects=EFFECT),
    )(*srcs, *lands)
    return outs[0], outs[1], list(outs[2:2 + n]), list(outs[2 + n:2 + 2 * n]), outs[2 + 2 * n]


def exchange_wait(name, send, recv, parts, lands, after):
    n = len(parts)
    afters = _as_list(after)

    def body(*refs):
        src, got = refs[:n], refs[n:2 * n]
        send_ref, recv_ref = refs[2 * n], refs[2 * n + 1]
        x, y, c, _ = _place()
        sib = (x, y, 1 - c)
        for a in range(n):
            cp = pltpu.make_async_remote_copy(src[a].at[1 - c], got[a], send_ref.at[a], recv_ref.at[a], device_id=sib, device_id_type=MESH)
            cp.wait_send()
            cp.wait_recv()

    outs = pl.pallas_call(
        body, name=name,
        out_shape=(*[pltpu.HBM(p.shape, p.dtype) for p in parts], *[pltpu.HBM(l.shape, l.dtype) for l in lands]),
        in_specs=[HBM] * (2 * n) + [SEM, SEM] + [ANY] * len(afters), out_specs=[HBM] * (2 * n),
        input_output_aliases={i: i for i in range(2 * n)},
        compiler_params=pltpu.CompilerParams(has_side_effects=EFFECT),
    )(*parts, *lands, send, recv, *afters)
    return list(outs[:n]), list(outs[n:])


def scatter_start(name, parts):
    n = len(parts)

    def body(*refs):
        src, land = refs[:n], refs[n:2 * n]
        send, recv = refs[2 * n], refs[2 * n + 1]
        token = refs[4 * n + 2]
        x, y, c, chips = _place()
        for a in range(n):
            for j, (cx, cy) in enumerate(chips):
                for sp, dp in zip(_pieces(src[a].at[2 * cx + cy]), _pieces(land[a].at[j])):
                    pltpu.make_async_remote_copy(sp, dp, send.at[3 * a + j], recv.at[3 * a + j],
                                                 device_id=(cx, cy, c), device_id_type=MESH).start()
        token[...] = jnp.zeros_like(token)

    lands = [pltpu.with_memory_space_constraint(lax.empty((NCHIP - 1,) + p.shape[1:], p.dtype), pltpu.HBM) for p in parts]
    srcs = [pltpu.with_memory_space_constraint(p, pltpu.HBM) for p in parts]
    outs = pl.pallas_call(
        body, name=name,
        out_shape=(pltpu.SemaphoreType.DMA((3 * n,)), pltpu.SemaphoreType.DMA((3 * n,)),
                   *[pltpu.HBM(p.shape, p.dtype) for p in parts], *[pltpu.HBM(l.shape, l.dtype) for l in lands],
                   SDS((8, 128), F32)),
        in_specs=[HBM] * (2 * n), out_specs=(SEM, SEM, *([HBM] * (2 * n)), pl.BlockSpec(memory_space=pltpu.VMEM)),
        input_output_aliases={i: 2 + i for i in range(2 * n)},
        compiler_params=pltpu.CompilerParams(has_side_effects=EFFECT),
    )(*srcs, *lands)
    return outs[0], outs[1], list(outs[2:2 + n]), list(outs[2 + n:2 + 2 * n]), outs[2 + 2 * n]


def scatter_wait(name, send, recv, parts, lands, after):
    n = len(parts)
    afters = _as_list(after)

    def body(*refs):
        src, land = refs[:n], refs[n:2 * n]
        send_ref, recv_ref = refs[2 * n], refs[2 * n + 1]
        x, y, c, chips = _place()
        for a in range(n):
            for j, (cx, cy) in enumerate(chips):
                cp = pltpu.make_async_remote_copy(src[a].at[2 * cx + cy], land[a].at[j], send_ref.at[3 * a + j], recv_ref.at[3 * a + j],
                                                  device_id=(cx, cy, c), device_id_type=MESH)
                cp.wait_send()
                cp.wait_recv()

    outs = pl.pallas_call(
        body, name=name,
        out_shape=(*[pltpu.HBM(p.shape, p.dtype) for p in parts], *[pltpu.HBM(l.shape, l.dtype) for l in lands]),
        in_specs=[HBM] * (2 * n) + [SEM, SEM] + [ANY] * len(afters), out_specs=[HBM] * (2 * n),
        input_output_aliases={i: i for i in range(2 * n)},
        compiler_params=pltpu.CompilerParams(has_side_effects=EFFECT),
    )(*parts, *lands, send, recv, *afters)
    return list(outs[:n]), list(outs[n:])


def join_start(name, halves):
    n = len(halves)

    def body(*refs):
        src, dst = refs[:n], refs[n:2 * n]
        send, recv = refs[2 * n], refs[2 * n + 1]
        token = refs[4 * n + 2]
        x, y, c, _ = _place()
        sib = (x, y, 1 - c)
        for a in range(n):
            for sp, dp in zip(_pieces(src[a]), _pieces(dst[a])):
                pltpu.make_async_remote_copy(sp, dp, send.at[a], recv.at[a], device_id=sib, device_id_type=MESH).start()
        token[...] = jnp.zeros_like(token)

    lands = [pltpu.with_memory_space_constraint(lax.empty(h.shape, h.dtype), pltpu.HBM) for h in halves]
    srcs = [pltpu.with_memory_space_constraint(h, pltpu.HBM) for h in halves]
    outs = pl.pallas_call(
        body, name=name,
        out_shape=(pltpu.SemaphoreType.DMA((n,)), pltpu.SemaphoreType.DMA((n,)),
                   *[pltpu.HBM(h.shape, h.dtype) for h in halves], *[pltpu.HBM(l.shape, l.dtype) for l in lands],
                   SDS((8, 128), F32)),
        in_specs=[HBM] * (2 * n), out_specs=(SEM, SEM, *([HBM] * (2 * n)), pl.BlockSpec(memory_space=pltpu.VMEM)),
        input_output_aliases={i: 2 + i for i in range(2 * n)},
        compiler_params=pltpu.CompilerParams(has_side_effects=EFFECT),
    )(*srcs, *lands)
    return outs[0], outs[1], list(outs[2:2 + n]), list(outs[2 + n:2 + 2 * n]), outs[2 + 2 * n]


def join_wait(name, send, recv, halves, lands, after):
    n = len(halves)
    afters = _as_list(after)

    def body(*refs):
        src, dst = refs[:n], refs[n:2 * n]
        send_ref, recv_ref = refs[2 * n], refs[2 * n + 1]
        x, y, c, _ = _place()
        sib = (x, y, 1 - c)
        for a in range(n):
            cp = pltpu.make_async_remote_copy(src[a], dst[a], send_ref.at[a], recv_ref.at[a], device_id=sib, device_id_type=MESH)
            cp.wait_send()
            cp.wait_recv()

    outs = pl.pallas_call(
        body, name=name,
        out_shape=(*[pltpu.HBM(h.shape, h.dtype) for h in halves], *[pltpu.HBM(l.shape, l.dtype) for l in lands]),
        in_specs=[HBM] * (2 * n) + [SEM, SEM] + [ANY] * len(afters), out_specs=[HBM] * (2 * n),
        input_output_aliases={i: i for i in range(2 * n)},
        compiler_params=pltpu.CompilerParams(has_side_effects=EFFECT),
    )(*halves, *lands, send, recv, *afters)
    return list(outs[:n]), list(outs[n:])


def gather_small(name, xs, reduce, after=None):
    m, ncol = xs.shape
    afters = _as_list(after)

    def body(x_ref, *rest):
        out_ref, all_ref, send, recv, lsem = rest[len(afters):]
        x, y, c, chips = _place()
        me, sib = (x, y, c), (x, y, 1 - c)

        def rows(px, py, pc):
            return all_ref.at[pl.ds((4 * px + 2 * py + pc) * m, m), :]

        def copy(k, block, to, src=None):
            return pltpu.make_async_remote_copy(rows(*block) if src is None else src, rows(*block), send.at[k], recv.at[k],
                                                device_id=to, device_id_type=MESH)

        mine = pltpu.make_async_copy(x_ref, rows(*me), lsem)
        mine.start()
        first = [copy(0, me, sib, src=x_ref)] + [copy(1 + j, me, (*chip, c), src=x_ref) for j, chip in enumerate(chips)]
        for cp in first:
            cp.start()
        passed = [copy(4 + j, (*chip, c), sib) for j, chip in enumerate(chips)]
        for j, chip in enumerate(chips):
            copy(1 + j, (*chip, c), me).wait_recv()
            passed[j].start()
        copy(0, sib, me).wait_recv()
        for j, chip in enumerate(chips):
            copy(4 + j, (*chip, 1 - c), me).wait_recv()
        for cp in first + passed:
            cp.wait_send()
        mine.wait()
        if reduce:
            s = all_ref[0:m, :]
            for dev in range(1, 8):
                s = s + all_ref[dev * m:(dev + 1) * m, :]
            out_ref[...] = s
        else:
            out_ref[...] = all_ref[...]

    vm = pl.BlockSpec(memory_space=pltpu.VMEM)
    return pl.pallas_call(
        body, name=name, in_specs=[vm] + [ANY] * len(afters), out_specs=vm,
        out_shape=SDS((m, ncol) if reduce else (8 * m, ncol), F32),
        scratch_shapes=[pltpu.VMEM((8 * m, ncol), F32), pltpu.SemaphoreType.DMA((7,)), pltpu.SemaphoreType.DMA((7,)),
                        pltpu.SemaphoreType.DMA],
    )(xs, *afters)


RELAYOUT_ROWS = 128


def weights_to_cat(g_in, after=None):
    tm = RELAYOUT_ROWS
    afters = _as_list(after)

    def body(g_ref, *rest):
        o_ref = rest[len(afters)]
        nat = jnp.concatenate([g_ref[j] for j in range(NCHIP)], axis=1)
        pad = jnp.zeros((tm, NCAT - OA - 16), BF16)
        o_ref[...] = jnp.concatenate([nat[:, 3072:7168], nat[:, 7184:11280], nat[:, 0:3072], nat[:, 7168:7184], pad], axis=1)

    return pl.pallas_call(
        body, name="weights_to_cat", grid=(D // tm,),
        in_specs=[pl.BlockSpec((NCHIP, tm, IN_SHARD), lambda i: (0, i, 0))] + [ANY] * len(afters),
        out_specs=pl.BlockSpec((tm, NCAT), lambda i: (i, 0)), out_shape=SDS((D, NCAT), BF16),
        compiler_params=_cparams(40 * 1024 * 1024, ("arbitrary",)),
    )(g_in, *afters)


def grads_from_cat(gw_cat):
    tm = RELAYOUT_ROWS
    nb = (D // 2) // tm

    def body(c_ref, o_ref):
        cat = c_ref[...]
        nat = jnp.concatenate([cat[:, OU:OA], cat[:, OV:OGP], cat[:, OA:OA + 16], cat[:, OGP:OU]], axis=1)
        for j in range(NCHIP):
            o_ref[j] = nat[:, j * IN_SHARD:(j + 1) * IN_SHARD]

    return pl.pallas_call(
        body, name="grads_from_cat", grid=(D // tm,), in_specs=[pl.BlockSpec((tm, NCAT), lambda i: (i, 0))],
        out_specs=pl.BlockSpec((None, NCHIP, tm, IN_SHARD), lambda i: (i // nb, 0, i % nb, 0)),
        out_shape=SDS((2, NCHIP, D // 2, IN_SHARD), BF16), compiler_params=_cparams(40 * 1024 * 1024, ("arbitrary",)),
    )(gw_cat)


def _pad_rows(a, rows):
    return jnp.concatenate([a, jnp.zeros((rows - a.shape[0],) + a.shape[1:], a.dtype)], axis=0)


def local_step(x2d, tgt, gf, g1, pool_scale, wa_pad, b_alpha, ng, g2, get_w, on_grad=None, on_settle=None, tick=None):
    emit = on_grad if on_grad is not None else (lambda group, grads: None)
    settle = on_settle if on_settle is not None else (lambda group, after: None)
    h1 = norm1(x2d, g1)
    wcat, pw = get_w("in", h1)
    pcat = mm_in(h1, wcat)
    dpool, ylin = pool_fwd(pcat, pw)
    pinned = tick("pool", ylin) if tick is not None else None
    og, o, states = gla_fwd(pcat, wa_pad, b_alpha, ng, pinned)
    w_go, w_o = get_w("mid", og)
    mixed, ygla = mm_gla_out(og, w_go, ylin, pcat, pool_scale)
    x2, h2 = mm_out(mixed, w_o, x2d, g2)
    w_up = get_w("up", h2)
    rup, act = mm_up(h2, w_up)
    w_dn = get_w("down", act)
    dx3, dx3b, g_nf, loss_row = mm_down(act, w_dn, x2, tgt, gf)

    gw_down = mm_wgrad("mm_dw_down", act, dx3b, DFF, D, (2, NCHIP, D // 2, D), (None, None, 512, D),
                       lambda j, i, k: ((i // 2) % 2, i // 4, i % 2, 0), 512, D)
    token = emit("down", {"down": gw_down})
    dup = mm_dact(dx3b, w_dn, rup, after=token)
    token = settle("down", dup)
    dx2, dx2b, g_mlp = mm_dh2(dup, w_up, x2, dx3, g2, after=token)
    gw_up = mm_wgrad("mm_dw_up", h2, dup, D, DFF, (2, NCHIP, D // 2, D), (None, None, 512, D),
                     lambda j, i, k: (i // 2, j, i % 2, 0), 512, D)
    token = emit("up", {"up": gw_up})
    dylin, dygla, dlgp, dlgg, g_ps = mm_dmixed(dx2b, w_o, pcat, ylin, ygla, pool_scale, after=token)
    token = settle("up", dylin)
    gw_out = mm_wgrad("mm_dw_out", mixed, dx2b, D, D, (2, NCHIP, 256, D), (None, None, 256, D),
                      lambda j, i, k: (i % 2, i // 2, 0, 0), 256, D)
    do, dg, g_ng = mm_dog(dygla, w_go, o, pcat, ng, after=token)
    gw_go = mm_wgrad("mm_dw_gla_out", og, dygla, D, D, (2, NCHIP, 256, D), (None, None, 256, D),
                     lambda j, i, k: (i % 2, i // 2, 0, 0), 256, D)
    token = emit("mix", {"out": gw_out, "gla_out": gw_go})
    dq, dk, dv, dalow, g_wa, g_ba = gla_bwd(do, pcat, states, wa_pad, b_alpha, b_alpha if token is None else token)
    token = settle("mix", dq)
    du, dpw = pool_bwd(dylin, dpool, pw)
    dpcat = jnp.concatenate([dv, dg, dlgp, dlgg, du, dq, dk, dalow, jnp.zeros((T, NCAT - OA - APAD), BF16)], axis=1)
    gw_cat = mm_wgrad("mm_dw_in", h1, dpcat, D, NCAT, (D, NCAT), (1024, 1280), lambda j, i, k: (i, j), 1024, 1280, after=token)
    token = settle("in", emit("in", {"in_cat": gw_cat, "pool": dpw}))
    grad_x, g_mix = mm_dh1(dpcat, wcat, x2d, dx2, g1, after=token)
    return (loss_row[0, 0], grad_x, g_mix, g_ps, g_mlp, g_nf, g_ng, g_ba, g_wa, token,
            gw_cat, dpw, gw_go, gw_out, gw_up, gw_down)


def kernel(x, norm_mix_g, w_in, pool_w, pool_scale, w_alpha, b_alpha, gla_norm_g, w_gla_out, w_out, norm_mlp_g, w_mlp_up, w_mlp_down, norm_final_g, loss_target, m_norm_mix_g, m_w_in, m_pool_w, m_pool_scale, m_w_alpha, m_b_alpha, m_gla_norm_g, m_w_gla_out, m_w_out, m_norm_mlp_g, m_w_mlp_up, m_w_mlp_down, m_norm_final_g, v_norm_mix_g, v_w_in, v_pool_w, v_pool_scale, v_w_alpha, v_b_alpha, v_gla_norm_g, v_w_gla_out, v_w_out, v_norm_mlp_g, v_w_mlp_up, v_w_mlp_down, v_norm_final_g):
    chip = 2 * lax.axis_index("x") + lax.axis_index("y")
    chip_i = chip.astype(jnp.int32).reshape(1)
    core_i = lax.axis_index("c").astype(jnp.int32).reshape(1)
    tgt = loss_target.reshape(T, D)
    gf = norm_final_g.reshape(1, D)

    def halves(w2d):
        r, c = w2d.shape
        return w2d.astype(BF16).reshape(2, r // 2, c)

    pool_shard = pool_w.reshape(4 * PG, PO // NCHIP)
    sent = {"in": [halves(w_in[0]), halves(pool_shard)]}
    flight = {}

    def start(group, after=None):
        flight[group] = gather_start("gather_start_" + group, sent[group], after)

    def relay(group, after):
        send, recv, shards, lands = flight[group]
        shards, lands = gather_wait("gather_wait_" + group, send, recv, shards, lands, after)
        send, recv, lands = relay_start("relay_start_" + group, lands)
        flight[group] = (send, recv, shards, lands)

    def fetch(group, after):
        send, recv, shards, lands = flight[group]
        lands = relay_wait("relay_wait_" + group, send, recv, lands, after)
        return forward_halves("forward_" + group, shards, lands)

    start("in")
    w_in_f, m_in_f, v_in_f, w_go_f, w_o_f, w_up_f, w_dn_f, x_f, wal_f, gng_f = lax.optimization_barrier(
        (w_in, m_w_in, v_w_in, w_gla_out, w_out, w_mlp_up, w_mlp_down, x, w_alpha, gla_norm_g, flight["in"][2][0]))[:10]
    w_in_r, m_in_r = w_in_f.reshape(2, D // 2, IN_SHARD), m_in_f.reshape(2, D // 2, IN_SHARD)
    sent["mid"] = [halves(w_go_f[0]), halves(w_o_f[0])]
    relay("in", [w_in_r, m_in_r, *sent["mid"]])
    start("mid", flight["in"][3][0])
    v_in_f, w_up_f, w_dn_f, x_f, wal_f, gng_f = lax.optimization_barrier(
        (v_in_f, w_up_f, w_dn_f, x_f, wal_f, gng_f, flight["mid"][2][0]))[:6]
    v_in_r = v_in_f.reshape(2, D // 2, IN_SHARD)
    sent["up"], sent["down"] = [halves(w_up_f[0])], [halves(w_dn_f[0])]
    x2d = x_f.reshape(T, D)
    big = [w_in_r, w_go_f[0], w_o_f[0], w_up_f[0], w_dn_f[0], pool_shard]

    def tick(point, after):
        if point == "pool":
            relay("mid", after)
            start("down", flight["mid"][3][0])
            return [flight["mid"][3][0], flight["down"][3][0]]

    def get_w(group, after):
        if group == "in":
            after = [after, v_in_r, *sent["up"], *sent["down"], wa_pad]
        if group == "mid":
            relay("up", after)
            after = flight["up"][3][0]
        if group == "up":
            relay("down", after)
            after = flight["down"][3][0]
        whole = fetch(group, after)
        if group == "in":
            start("up", whole[0])
            g_in, g_pool = whole
            wcat = weights_to_cat(g_in.reshape(NCHIP, D, IN_SHARD), flight["up"][3][0])
            pw = jnp.concatenate([g_pool[j].reshape(4, PG, PO // NCHIP) for j in range(NCHIP)], axis=2)
            return wcat, pw
        if group == "mid":
            return whole[0].reshape(D, D), whole[1].reshape(D, D)
        if group == "up":
            return whole[0].reshape(NCHIP, D, D)
        return whole[0].reshape(DFF, D)

    small_w = pack_rows("pack_small_w", [wal_f[0].reshape(4, QK),
                                         jnp.concatenate([gng_f[0].reshape(1, 512), jnp.zeros((1, 512), F32)], axis=1)], 8)
    sw_all = gather_small("gather_small_w", small_w, False).reshape(8, 8, QK)
    wa_full = jnp.concatenate([sw_all[2 * j, 0:4].reshape(16, DK) for j in range(NCHIP)], axis=1)
    ng_full = jnp.concatenate([sw_all[2 * j, 4, 0:512].reshape(HEADS, DV // NCHIP) for j in range(NCHIP)], axis=1)
    wa_pad = _pad_rows(wa_full, APAD).astype(BF16)
    ng = ng_full.reshape(1, D)

    pending = {}
    wmv = {"in": (w_in_r, m_in_r, v_in_r), "gla_out": (big[1], m_w_gla_out, v_w_gla_out), "out": (big[2], m_w_out, v_w_out),
           "up": (big[3], m_w_mlp_up, v_w_mlp_up), "down": (big[4], m_w_mlp_down, v_w_mlp_down), "pool": (big[5], m_pool_w, v_pool_w)}
    big_res = {}

    def reduce_group(group, after):
        nms, send, recv, sums, lands = pending[group]
        sums, lands = scatter_wait("scatter_wait_" + group, send, recv, sums, lands, after)
        reduced = [sum_chips("sum_chips_" + nm, a, b, chip_i) for nm, a, b in zip(nms, sums, lands)]
        send, recv, reduced, lands, token = join_start("join_start_" + group, reduced)
        pending[group] = (nms, send, recv, reduced, lands)
        return token

    def update_group(group, after):
        nms, send, recv, reduced, lands = pending[group]
        reduced, from_sib = join_wait("join_wait_" + group, send, recv, reduced, lands, after)
        for nm, g_own, g_sib in zip(nms, reduced, from_sib):
            w, m, v = wmv[nm]
            shp = (2,) + g_own.shape
            big_res[nm] = adamw_halves("adamw_" + nm, w.reshape(shp), g_own, g_sib, m.reshape(shp), v.reshape(shp), core_i)

    def on_grad(group, grads):
        if group == "in":
            gw_in = grads_from_cat(grads["in_cat"])
            gw_pool = jnp.stack([grads["pool"][:, :, j * 128:(j + 1) * 128].reshape(2, 2 * PG, 128)
                                 for j in range(NCHIP)], axis=1)
            grads = {"in": gw_in, "pool": gw_pool}
        nms, parts = list(grads.keys()), list(grads.values())
        send, recv, parts, got, token = exchange_start("exchange_start_" + group, parts)
        pending[group] = (nms, send, recv, parts, got)
        return token

    def on_settle(group, after):
        if group == "in":
            after = reduce_group("down", after)
        nms, send, recv, parts, got = pending[group]
        parts, got = exchange_wait("exchange_wait_" + group, send, recv, parts, got, after)
        sums = [add_pairs("add_pair_" + nm, a, b, core_i) for nm, a, b in zip(nms, parts, got)]
        send, recv, sums, lands, token = scatter_start("scatter_start_" + group, sums)
        pending[group] = (nms, send, recv, sums, lands)
        if group != "in":
            return token
        token = reduce_group("up", token)
        token = reduce_group("mix", token)
        for earlier in ("down", "up", "mix"):
            update_group(earlier, token)
            token = big_res[pending[earlier][0][-1]][1]
        return [big_res[nm][1] for nm in ("down", "up", "out", "gla_out")]

    (loss_local, grad_x, g_mix, g_ps, g_mlp, g_nf, g_ng, g_ba, g_wa) = local_step(
        x2d, tgt, gf, norm_mix_g, pool_scale, wa_pad, b_alpha, ng, norm_mlp_g, get_w, on_grad, on_settle, tick)[:9]
    loss = lax.psum(loss_local, ("x", "y", "c"))
    join_in_token = reduce_group("in", grad_x)

    ROWS = 16

    def wide(a, n):
        return jnp.concatenate([a.reshape(1, n), jnp.zeros((1, D - n), F32)], axis=1)

    packed = pack_rows("pack_small_g", [g_mix, g_ps, g_mlp, g_nf, g_ng, wide(g_ba, QK), g_wa[0:16].reshape(8, D)], ROWS)
    tot = gather_small("reduce_small_g", packed, True, join_in_token)
    t_wa = lax.dynamic_slice(tot[6:14].reshape(16, QK), (0, chip * DK), (16, DK))
    t_ng = lax.dynamic_slice(tot[4].reshape(HEADS, DV), (0, chip * (DV // NCHIP)), (HEADS, DV // NCHIP))

    def pack_small(nm, mix, ps, mlp, nf, ba, wa, gn, after=None):
        return pack_rows(nm, [mix.reshape(1, D), ps.reshape(1, D), mlp.reshape(1, D), nf.reshape(1, D), wide(ba, QK),
                              wa.reshape(2, D), wide(gn, 512)], ROWS, after)

    update_group("in", tot)
    sg = pack_small("pack_g", tot[0], tot[1], tot[2], tot[3], tot[5, 0:QK], t_wa, t_ng, big_res["in"][3])
    sw = pack_small("pack_w", norm_mix_g, pool_scale, norm_mlp_g, norm_final_g, b_alpha, w_alpha, gla_norm_g)
    sm = pack_small("pack_m", m_norm_mix_g, m_pool_scale, m_norm_mlp_g, m_norm_final_g, m_b_alpha, m_w_alpha, m_gla_norm_g)
    sv = pack_small("pack_v", v_norm_mix_g, v_pool_scale, v_norm_mlp_g, v_norm_final_g, v_b_alpha, v_w_alpha, v_gla_norm_g)
    small_res = adamw("adamw_small", sw, sg, sm, sv)

    def unpack(p):
        return {"norm_mix_g": p[0].reshape(1, D), "pool_scale": p[1].reshape(1, D), "norm_mlp_g": p[2].reshape(1, D),
                "norm_final_g": p[3].reshape(D), "b_alpha": p[4, 0:QK].reshape(1, QK), "w_alpha": p[5:7].reshape(1, 16, DK),
                "gla_norm_g": p[7, 0:512].reshape(1, HEADS, DV // NCHIP)}

    order = ["norm_mix_g", "w_in", "pool_w", "pool_scale", "w_alpha", "b_alpha", "gla_norm_g", "w_gla_out", "w_out",
             "norm_mlp_g", "w_mlp_up", "w_mlp_down", "norm_final_g"]
    big_key = {"w_in": ("in", w_in.shape), "pool_w": ("pool", pool_w.shape), "w_gla_out": ("gla_out", w_gla_out.shape),
               "w_out": ("out", w_out.shape), "w_mlp_up": ("up", w_mlp_up.shape), "w_mlp_down": ("down", w_mlp_down.shape)}
    result = [loss, grad_x.reshape(1, T, D)]
    for kind in range(4):
        small = unpack(small_res[kind])
        for nm in order:
            if nm in big_key:
                key, shp = big_key[nm]
                result.append(big_res[key][kind].reshape(shp))
            else:
                result.append(small[nm])
    return tuple(result)
```

```python
import itertools

import jax
import jax.numpy as jnp
from jax import lax
from jax.experimental import pallas as pl
from jax.experimental.pallas import tpu as pltpu

F32 = jnp.float32
BF16 = jnp.bfloat16
SDS = jax.ShapeDtypeStruct
MESH = pl.DeviceIdType.MESH
ANY = pl.BlockSpec(memory_space=pl.ANY)

T = 2048
D = 2048
DFF = 8192
NCHIP = 4
IN_WIDTH = 11280
IN_SHARD = IN_WIDTH // NCHIP
CHUNK = 64
NCHUNK = T // CHUNK
HEADS = 4
DK = 256
DV = 512
QK = HEADS * DK
EPS = 1e-6
POOL_WINDOWS = (2, 4, 8, 16)
PG = 256
PO = 512

OV, OG, OGP, OGG, OU, OQ, OKK, OA = 0, 2048, 4096, 6144, 8192, 9216, 10240, 11264
NCAT = 11520
APAD = 128

VMEM_CAP = 56 * 1024 * 1024

PIECE_BYTES = 384 * 1024

ADAM_LR, ADAM_B1, ADAM_B2, ADAM_EPS, ADAM_WD, ADAM_STEP = 0.001, 0.9, 0.999, 1e-08, 0.01, 10


def _cparams(vmem_bytes=None, sem=None):
    kw = {}
    if vmem_bytes is not None:
        kw["vmem_limit_bytes"] = int(min(max(vmem_bytes, 32 * 1024 * 1024), VMEM_CAP))
    if sem is not None:
        kw["dimension_semantics"] = sem
    return pltpu.CompilerParams(**kw)


def _nbytes(shape, dtype):
    n = 1
    for s in shape:
        if s is not None:
            n *= s
    return n * jnp.dtype(dtype).itemsize


def _sigmoid(x):
    return 0.5 * jnp.tanh(0.5 * x) + 0.5


EPI_COLS = 512


def _as_list(after):
    if after is None:
        return []
    return list(after) if isinstance(after, (list, tuple)) else [after]


def matmul(name, a, b, *, a_spec, b_spec, cdims, grid, acc_shape, outs, extras=(), epi, after=None):
    nj, ni, nk = grid
    ne, no = len(extras), len(outs)
    afters = _as_list(after)
    first_out = 2 + ne + len(afters)

    def body(*refs):
        a_ref, b_ref = refs[0], refs[1]
        ex = refs[2:2 + ne]
        out_refs = refs[first_out:first_out + no]
        i = pl.program_id(1)
        part = lax.dot_general(a_ref[...], b_ref[...], (cdims, ((), ())), preferred_element_type=F32)
        if nk == 1:
            epi(part, ex, out_refs, i)
        else:
            acc_ref = refs[first_out + no]
            k = pl.program_id(2)

            @pl.when(k == 0)
            def _():
                acc_ref[...] = part

            @pl.when(k > 0)
            def _():
                acc_ref[...] += part

            @pl.when(k == nk - 1)
            def _():
                epi(acc_ref[...], ex, out_refs, i)

    in_specs = [pl.BlockSpec(*a_spec), pl.BlockSpec(*b_spec)] + [pl.BlockSpec(bs, im) for _, bs, im in extras]
    in_specs += [ANY] * len(afters)
    out_specs = [pl.BlockSpec(bs, im) for _, _, bs, im in outs]
    out_shape = [SDS(s, dt) for s, dt, _, _ in outs]
    vm = 2 * (_nbytes(a_spec[0], a.dtype) + _nbytes(b_spec[0], b.dtype))
    vm += 2 * sum(_nbytes(bs, arr.dtype) for arr, bs, _ in extras)
    vm += 2 * sum(_nbytes(bs, dt) for _, dt, bs, _ in outs)
    vm += 6 * _nbytes(acc_shape, F32)
    scratch = [pltpu.VMEM(acc_shape, F32)] if nk > 1 else []
    return pl.pallas_call(
        body, name=name, grid=grid, in_specs=in_specs, out_specs=out_specs, out_shape=out_shape,
        scratch_shapes=scratch,
        compiler_params=_cparams(vm, ("arbitrary", "arbitrary", "arbitrary")),
    )(a, b, *[arr for arr, _, _ in extras], *afters)


NN =((1,), (0,))
NT = ((1,), (1,))
TN = ((0,), (0,))


def _row_acc(out_ref, val, i):
    @pl.when(i == 0)
    def _():
        out_ref[...] = val

    @pl.when(i > 0)
    def _():
        out_ref[...] += val


def _rms_bwd(xn, r, dxn):
    return r * (dxn - xn * jnp.mean(dxn * xn, axis=-1, keepdims=True))


def norm1(x, g):
    tm = 256

    def body(x_ref, g_ref, h_ref):
        xv = x_ref[...]
        r = lax.rsqrt(jnp.mean(xv * xv, axis=-1, keepdims=True) + EPS)
        h_ref[...] = (xv * r * g_ref[...]).astype(BF16)

    return pl.pallas_call(
        body, name="norm1", grid=(T // tm,),
        in_specs=[pl.BlockSpec((tm, D), lambda i: (i, 0)), pl.BlockSpec((1, D), lambda i: (0, 0))],
        out_specs=pl.BlockSpec((tm, D), lambda i: (i, 0)), out_shape=SDS((T, D), BF16),
        compiler_params=_cparams(32 * 1024 * 1024, ("arbitrary",)),
    )(x, g)


def mm_in(h1, wcat):
    tm, tn = 1024, 1280

    def epi(acc, ex, outs, i):
        outs[0][...] = acc.astype(BF16)

    return matmul("mm_in", h1, wcat, a_spec=((tm, D), lambda j, i, k: (i, 0)), b_spec=((D, tn), lambda j, i, k: (0, j)),
                  cdims=NN, grid=(NCAT // tn, T // tm, 1), acc_shape=(tm, tn),
                  outs=[((T, NCAT), BF16, (tm, tn), lambda j, i, k: (i, j))], epi=epi)[0]


def _window_sum(x, w, up):
    n = x.shape[0]
    row = lax.broadcasted_iota(jnp.int32, x.shape, 0)
    s, sh = x, 1
    while sh < w:
        if up:
            s = s + jnp.where(row < n - sh, pltpu.roll(s, n - sh, axis=0), 0.0)
        else:
            s = s + jnp.where(row >= sh, pltpu.roll(s, sh, axis=0), 0.0)
        sh *= 2
    return s


def _inv_count(shape, w):
    row = lax.broadcasted_iota(jnp.int32, shape, 0)
    return 1.0 / jnp.minimum(row + 1, w).astype(F32)


def pool_fwd(pcat, pw):
    def body(u_ref, pw_ref, d_ref, y_ref):
        for gi, w in enumerate(POOL_WINDOWS):
            ug = u_ref[:, gi * PG:(gi + 1) * PG].astype(F32)
            dg = _window_sum(ug, w, False) * _inv_count(ug.shape, w) - ug
            db = dg.astype(BF16)
            d_ref[:, gi * PG:(gi + 1) * PG] = db
            y_ref[:, gi * PO:(gi + 1) * PO] = jnp.dot(db, pw_ref[gi], preferred_element_type=F32).astype(BF16)

    return pl.pallas_call(
        body, name="pool_fwd", grid=(1,),
        in_specs=[pl.BlockSpec((T, 4 * PG), lambda i: (0, OU // (4 * PG))), pl.BlockSpec((4, PG, PO), lambda i: (0, 0, 0))],
        out_specs=[pl.BlockSpec((T, 4 * PG), lambda i: (0, 0)), pl.BlockSpec((T, D), lambda i: (0, 0))],
        out_shape=[SDS((T, 4 * PG), BF16), SDS((T, D), BF16)],
        compiler_params=_cparams(48 * 1024 * 1024, ("arbitrary",)),
    )(pcat, pw)


def pool_bwd(dylin, d, pw):
    def body(dy_ref, d_ref, pw_ref, du_ref, dpw_ref):
        for gi, w in enumerate(POOL_WINDOWS):
            dyl = dy_ref[:, gi * PO:(gi + 1) * PO]
            dd = lax.dot_general(dyl, pw_ref[gi], (NT, ((), ())), preferred_element_type=F32)
            du = _window_sum(dd * _inv_count(dd.shape, w), w, True) - dd
            du_ref[:, gi * PG:(gi + 1) * PG] = du.astype(BF16)
            dpw_ref[gi] = lax.dot_general(d_ref[:, gi * PG:(gi + 1) * PG], dyl, (TN, ((), ())),
                                          preferred_element_type=F32).astype(BF16)

    return pl.pallas_call(
        body, name="pool_bwd", grid=(1,),
        in_specs=[pl.BlockSpec((T, D), lambda i: (0, 0)), pl.BlockSpec((T, 4 * PG), lambda i: (0, 0)),
                  pl.BlockSpec((4, PG, PO), lambda i: (0, 0, 0))],
        out_specs=[pl.BlockSpec((T, 4 * PG), lambda i: (0, 0)), pl.BlockSpec((4, PG, PO), lambda i: (0, 0, 0))],
        out_shape=[SDS((T, 4 * PG), BF16), SDS((4, PG, PO), BF16)],
        compiler_params=_cparams(48 * 1024 * 1024, ("arbitrary",)),
    )(dylin, d, pw)


def _gate_decay(alow, wa, ba):
    a = jnp.dot(alow, wa, preferred_element_type=F32) + ba
    ls = jax.nn.log_sigmoid(a) * (1.0 / 16.0)
    r = lax.broadcasted_iota(jnp.int32, (CHUNK, CHUNK), 0)
    c = lax.broadcasted_iota(jnp.int32, (CHUNK, CHUNK), 1)
    tri = jnp.where(c <= r, 1.0, 0.0).astype(F32)
    cum = jnp.dot(tri, ls, preferred_element_type=F32, precision=lax.Precision.HIGHEST)
    last = cum[CHUNK - 1:CHUNK, :]
    return a, jnp.exp(last - cum), jnp.exp(last)


def gla_fwd(pcat, wa, ba, ng, after=None):
    afters = _as_list(after)

    def body(q_ref, k_ref, v_ref, g_ref, al_ref, wa_ref, ba_ref, ng_ref, *rest):
        og_ref, o_ref, st_ref, s_scr = rest[len(afters):]

        @pl.when(pl.program_id(0) == 0)
        def _():
            s_scr[...] = jnp.zeros_like(s_scr)

        _, e, decay = _gate_decay(al_ref[...], wa_ref[...], ba_ref[...])
        kd = (k_ref[...].astype(F32) * e).astype(BF16)
        qs = (q_ref[...].astype(F32) * (DK ** -0.5)).astype(BF16)
        for h in range(HEADS):
            ck = slice(h * DK, (h + 1) * DK)
            cv = slice(h * DV, (h + 1) * DV)
            s_new = s_scr[h] * decay[:, ck] + lax.dot_general(v_ref[:, cv], kd[:, ck], (TN, ((), ())),
                                                               preferred_element_type=F32)
            s_scr[h] = s_new
            sb = s_new.astype(BF16)
            st_ref[h] = sb
            oh = lax.dot_general(qs[:, ck], sb, (NT, ((), ())), preferred_element_type=F32)
            o_ref[:, cv] = oh.astype(BF16)
            on = oh * lax.rsqrt(jnp.mean(oh * oh, axis=-1, keepdims=True) + EPS) * ng_ref[:, cv]
            gv = g_ref[:, cv].astype(F32)
            og_ref[:, cv] = (on * (gv * _sigmoid(gv))).astype(BF16)

    row = lambda c: (c, 0)
    return pl.pallas_call(
        body, name="gla_fwd", grid=(NCHUNK,),
        in_specs=[pl.BlockSpec((CHUNK, QK), lambda c: (c, OQ // QK)), pl.BlockSpec((CHUNK, QK), lambda c: (c, OKK // QK)),
                  pl.BlockSpec((CHUNK, D), lambda c: (c, OV // D)), pl.BlockSpec((CHUNK, D), lambda c: (c, OG // D)),
                  pl.BlockSpec((CHUNK, APAD), lambda c: (c, OA // APAD)),
                  pl.BlockSpec((APAD, QK), lambda c: (0, 0)), pl.BlockSpec((1, QK), lambda c: (0, 0)),
                  pl.BlockSpec((1, D), lambda c: (0, 0))] + [ANY] * len(afters),
        out_specs=[pl.BlockSpec((CHUNK, D), row), pl.BlockSpec((CHUNK, D), row),
                   pl.BlockSpec((None, HEADS, DV, DK), lambda c: (c, 0, 0, 0))],
        out_shape=[SDS((T, D), BF16), SDS((T, D), BF16), SDS((NCHUNK, HEADS, DV, DK), BF16)],
        scratch_shapes=[pltpu.VMEM((HEADS, DV, DK), F32)],
        compiler_params=_cparams(32 * 1024 * 1024, ("arbitrary",)),
    )(pcat, pcat, pcat, pcat, pcat, wa, ba, ng, *afters)


def gla_bwd(do, pcat, states, wa, ba, after):
    def body(do_ref, q_ref, k_ref, v_ref, al_ref, sc_ref, sp_ref, wa_ref, ba_ref, after_ref,
             dq_ref, dk_ref, dv_ref, dal_ref, dwa_ref, dba_ref, ds_scr):
        i = pl.program_id(0)

        @pl.when(i == 0)
        def _():
            ds_scr[...] = jnp.zeros_like(ds_scr)

        has_prev = jnp.where(i < NCHUNK - 1, 1.0, 0.0).astype(F32)
        a, e, decay = _gate_decay(al_ref[...], wa_ref[...], ba_ref[...])
        kf = k_ref[...].astype(F32)
        kdf = kf * e
        kd = kdf.astype(BF16)
        qs = (q_ref[...].astype(F32) * (DK ** -0.5)).astype(BF16)
        dkd_parts, ddecay_parts = [], []
        for h in range(HEADS):
            ck = slice(h * DK, (h + 1) * DK)
            cv = slice(h * DV, (h + 1) * DV)
            doh = do_ref[:, cv]
            ds = ds_scr[h] + lax.dot_general(doh, qs[:, ck], (TN, ((), ())), preferred_element_type=F32)
            dsb = ds.astype(BF16)
            dq_ref[:, ck] = (jnp.dot(doh, sc_ref[h], preferred_element_type=F32) * (DK ** -0.5)).astype(BF16)
            dkd_parts.append(jnp.dot(v_ref[:, cv], dsb, preferred_element_type=F32))
            dv_ref[:, cv] = lax.dot_general(kd[:, ck], dsb, (NT, ((), ())), preferred_element_type=F32).astype(BF16)
            ddecay_parts.append(jnp.sum(ds * sp_ref[h].astype(F32), axis=0, keepdims=True) * has_prev)
            ds_scr[h] = ds * decay[:, ck]
        dkd = jnp.concatenate(dkd_parts, axis=1)
        ddecay = jnp.concatenate(ddecay_parts, axis=1)
        dk_ref[...] = (dkd * e).astype(BF16)
        dearg = dkd * kdf
        dlast = jnp.sum(dearg, axis=0, keepdims=True) + ddecay * decay
        r = lax.broadcasted_iota(jnp.int32, (CHUNK, CHUNK), 0)
        c = lax.broadcasted_iota(jnp.int32, (CHUNK, CHUNK), 1)
        triu = jnp.where(c >= r, 1.0, 0.0).astype(F32)
        dls = dlast - jnp.dot(triu, dearg, preferred_element_type=F32, precision=lax.Precision.HIGHEST)
        da = dls * (1.0 / 16.0) * (1.0 - _sigmoid(a))
        dab = da.astype(BF16)
        dal_ref[...] = lax.dot_general(dab, wa_ref[...], (NT, ((), ())), preferred_element_type=F32).astype(BF16)
        dwa = lax.dot_general(al_ref[...], dab, (TN, ((), ())), preferred_element_type=F32)
        dba = jnp.sum(da, axis=0, keepdims=True)

        @pl.when(i == 0)
        def _():
            dwa_ref[...] = dwa
            dba_ref[...] = dba

        @pl.when(i > 0)
        def _():
            dwa_ref[...] += dwa
            dba_ref[...] += dba

    rev = lambda i: NCHUNK - 1 - i
    return pl.pallas_call(
        body, name="gla_bwd", grid=(NCHUNK,),
        in_specs=[pl.BlockSpec((CHUNK, D), lambda i: (rev(i), 0)),
                  pl.BlockSpec((CHUNK, QK), lambda i: (rev(i), OQ // QK)), pl.BlockSpec((CHUNK, QK), lambda i: (rev(i), OKK // QK)),
                  pl.BlockSpec((CHUNK, D), lambda i: (rev(i), OV // D)), pl.BlockSpec((CHUNK, APAD), lambda i: (rev(i), OA // APAD)),
                  pl.BlockSpec((None, HEADS, DV, DK), lambda i: (rev(i), 0, 0, 0)),
                  pl.BlockSpec((None, HEADS, DV, DK), lambda i: (jnp.maximum(rev(i) - 1, 0), 0, 0, 0)),
                  pl.BlockSpec((APAD, QK), lambda i: (0, 0)), pl.BlockSpec((1, QK), lambda i: (0, 0)), ANY],
        out_specs=[pl.BlockSpec((CHUNK, QK), lambda i: (rev(i), 0)), pl.BlockSpec((CHUNK, QK), lambda i: (rev(i), 0)),
                   pl.BlockSpec((CHUNK, D), lambda i: (rev(i), 0)), pl.BlockSpec((CHUNK, APAD), lambda i: (rev(i), 0)),
                   pl.BlockSpec((APAD, QK), lambda i: (0, 0)), pl.BlockSpec((1, QK), lambda i: (0, 0))],
        out_shape=[SDS((T, QK), BF16), SDS((T, QK), BF16), SDS((T, D), BF16), SDS((T, APAD), BF16),
                   SDS((APAD, QK), F32), SDS((1, QK), F32)],
        scratch_shapes=[pltpu.VMEM((HEADS, DV, DK), F32)],
        compiler_params=_cparams(32 * 1024 * 1024, ("arbitrary",)),
    )(do, pcat, pcat, pcat, pcat, states, states, wa, ba, after)


TMF = 256
TMW = 512
_rowblk = ((TMF, D), lambda j, i, k: (i, 0))
_vec = ((1, D), lambda j, i, k: (0, 0))


def _full_spec(col):
    return ((TMF, D), lambda j, i, k: (i, col))


TBIG = 1024


def square_matmul(name, a, b, *, a_spec, b_spec, cdims, nk, after=None):
    def epi(acc, ex, outs, i):
        outs[0][...] = acc

    return matmul(name, a, b, a_spec=a_spec, b_spec=b_spec, cdims=cdims, grid=(D // TBIG, T // TBIG, nk),
                  acc_shape=(TBIG, TBIG), outs=[((T, D), F32, (TBIG, TBIG), lambda j, i, k: (i, j))], epi=epi,
                  after=after)[0]


def rowwise(name, y, *, extras, outs, epi):
    ne = len(extras)

    def body(*refs):
        epi(refs[0][...], refs[1:1 + ne], refs[1 + ne:], pl.program_id(1))

    in_specs = [pl.BlockSpec(*_rowblk)] + [pl.BlockSpec(bs, im) for _, bs, im in extras]
    return pl.pallas_call(
        body, name=name, grid=(1, T // TMF, 1), in_specs=in_specs,
        out_specs=[pl.BlockSpec(bs, im) for _, _, bs, im in outs], out_shape=[SDS(s, dt) for s, dt, _, _ in outs],
        compiler_params=_cparams(40 * 1024 * 1024, ("arbitrary", "arbitrary", "arbitrary")),
    )(y, *[arr for arr, _, _ in extras])


def mm_gla_out(og, w, ylin, pcat, pscale):
    def epi(acc, ex, outs, i):
        ylin_ref, lgp_ref, lgg_ref, ps_ref = ex
        for c0 in range(0, D, EPI_COLS):
            cs = slice(c0, c0 + EPI_COLS)
            gp = _sigmoid(lgp_ref[:, cs].astype(F32))
            gg = _sigmoid(lgg_ref[:, cs].astype(F32))
            a = acc[:, cs]
            outs[0][:, cs] = (gp * (ylin_ref[:, cs].astype(F32) * ps_ref[:, cs]) + gg * a).astype(BF16)
            outs[1][:, cs] = a.astype(BF16)

    return matmul("mm_gla_out", og, w, a_spec=_rowblk, b_spec=((D, D), lambda j, i, k: (0, 0)), cdims=NN,
                  grid=(1, T // TMF, 1), acc_shape=(TMF, D),
                  extras=[(ylin, *_rowblk), (pcat, *_full_spec(OGP // D)), (pcat, *_full_spec(OGG // D)), (pscale, *_vec)],
                  outs=[((T, D), BF16, *_rowblk), ((T, D), BF16, *_rowblk)], epi=epi)


def mm_out(mixed, w, x, g2):
    def epi(acc, ex, outs, i):
        x_ref, g_ref = ex
        x2 = x_ref[...] + acc
        r = lax.rsqrt(jnp.mean(x2 * x2, axis=-1, keepdims=True) + EPS)
        outs[0][...] = x2
        outs[1][...] = (x2 * r * g_ref[...]).astype(BF16)

    return matmul("mm_out", mixed, w, a_spec=_rowblk, b_spec=((D, D), lambda j, i, k: (0, 0)), cdims=NN,
                  grid=(1, T // TMF, 1), acc_shape=(TMF, D), extras=[(x, *_rowblk), (g2, *_vec)],
                  outs=[((T, D), F32, *_rowblk), ((T, D), BF16, *_rowblk)], epi=epi)


def mm_up(h2, wup):
    def epi(acc, ex, outs, i):
        r = jnp.maximum(acc, 0.0)
        outs[0][...] = r.astype(BF16)
        outs[1][...] = (r * r).astype(BF16)

    blk = ((TMW, D), lambda j, i, k: (i, j))
    return matmul("mm_up", h2, wup, a_spec=((TMW, D), lambda j, i, k: (i, 0)), b_spec=((None, D, D), lambda j, i, k: (j, 0, 0)),
                  cdims=NN, grid=(NCHIP, T // TMW, 1), acc_shape=(TMW, D),
                  outs=[((T, DFF), BF16, *blk), ((T, DFF), BF16, *blk)], epi=epi)


def mm_down(act, wdown, x2, tgt, gf):
    tk = 2048

    def epi(acc, ex, outs, i):
        x2_ref, t_ref, g_ref = ex
        dx_ref, dxb_ref, gnf_ref, loss_ref = outs
        x3 = x2_ref[...] + acc
        r = lax.rsqrt(jnp.mean(x3 * x3, axis=-1, keepdims=True) + EPS)
        xn = x3 * r
        err = xn * g_ref[...] - t_ref[...]
        lsum = 0.5 * jnp.sum(jnp.mean(err * err, axis=-1, keepdims=True), axis=0, keepdims=True)
        dy = err * (1.0 / D)
        _row_acc(gnf_ref, jnp.sum(dy * xn, axis=0, keepdims=True), i)
        _row_acc(loss_ref, jnp.broadcast_to(lsum, (1, 128)), i)
        dx3 = _rms_bwd(xn, r, dy * g_ref[...])
        dx_ref[...] = dx3
        dxb_ref[...] = dx3.astype(BF16)

    y = square_matmul("mm_down", act, wdown, a_spec=((TBIG, tk), lambda j, i, k: (i, k)),
                      b_spec=((tk, TBIG), lambda j, i, k: (k, j)), cdims=NN, nk=DFF // tk)
    return rowwise("rows_final", y, extras=[(x2, *_rowblk), (tgt, *_rowblk), (gf, *_vec)],
                   outs=[((T, D), F32, *_rowblk), ((T, D), BF16, *_rowblk), ((1, D), F32, *_vec),
                         ((1, 128), F32, (1, 128), lambda j, i, k: (0, 0))], epi=epi)


def mm_dact(dx3b, wdown, rup, after=None):
    def epi(acc, ex, outs, i):
        outs[0][...] = (acc * 2.0 * ex[0][...].astype(F32)).astype(BF16)

    blk = ((TMW, D), lambda j, i, k: (i, j))
    return matmul("mm_dact", dx3b, wdown, a_spec=((TMW, D), lambda j, i, k: (i, 0)), b_spec=((D, D), lambda j, i, k: (j, 0)),
                  cdims=NT, grid=(DFF // D, T // TMW, 1), acc_shape=(TMW, D), extras=[(rup, *blk)],
                  outs=[((T, DFF), BF16, *blk)], epi=epi, after=after)[0]


def mm_wgrad(name, a, b, m, n, out_shape, out_block, out_map, tm, tn, after=None):
    def epi(acc, ex, outs, i):
        outs[0][...] = acc.astype(BF16)

    return matmul(name, a, b, a_spec=((T, tm), lambda j, i, k: (0, i)), b_spec=((T, tn), lambda j, i, k: (0, j)),
                  cdims=TN, grid=(n // tn, m // tm, 1), acc_shape=(tm, tn),
                  outs=[(out_shape, BF16, out_block, out_map)], epi=epi, after=after)[0]


def mm_dh2(dup, wup, x2, dx3, g2, after=None):
    def epi(acc, ex, outs, i):
        x2_ref, dx3_ref, g_ref = ex
        x2 = x2_ref[...]
        r = lax.rsqrt(jnp.mean(x2 * x2, axis=-1, keepdims=True) + EPS)
        xn = x2 * r
        _row_acc(outs[2], jnp.sum(acc * xn, axis=0, keepdims=True), i)
        dx2 = dx3_ref[...] + _rms_bwd(xn, r, acc * g_ref[...])
        outs[0][...] = dx2
        outs[1][...] = dx2.astype(BF16)

    y = square_matmul("mm_dh2", dup, wup, a_spec=((TBIG, D), lambda j, i, k: (i, k)),
                      b_spec=((None, TBIG, D), lambda j, i, k: (k, j, 0)), cdims=NT, nk=NCHIP, after=after)
    return rowwise("rows_dh2", y, extras=[(x2, *_rowblk), (dx3, *_rowblk), (g2, *_vec)],
                   outs=[((T, D), F32, *_rowblk), ((T, D), BF16, *_rowblk), ((1, D), F32, *_vec)], epi=epi)


def mm_dmixed(dx2b, wout, pcat, ylin, ygla, pscale, after=None):
    def epi(acc, ex, outs, i):
        lgp_ref, lgg_ref, ylin_ref, ygla_ref, ps_ref = ex
        dps = []
        for c0 in range(0, D, EPI_COLS):
            cs = slice(c0, c0 + EPI_COLS)
            gp = _sigmoid(lgp_ref[:, cs].astype(F32))
            gg = _sigmoid(lgg_ref[:, cs].astype(F32))
            yl = ylin_ref[:, cs].astype(F32)
            ps = ps_ref[:, cs]
            a = acc[:, cs]
            agp = a * gp
            outs[0][:, cs] = (agp * ps).astype(BF16)
            outs[1][:, cs] = (a * gg).astype(BF16)
            outs[2][:, cs] = (agp * (yl * ps) * (1.0 - gp)).astype(BF16)
            outs[3][:, cs] = (a * ygla_ref[:, cs].astype(F32) * gg * (1.0 - gg)).astype(BF16)
            dps.append(jnp.sum(agp * yl, axis=0, keepdims=True))
        _row_acc(outs[4], jnp.concatenate(dps, axis=1), i)

    return matmul("mm_dmixed", dx2b, wout, a_spec=_rowblk, b_spec=((D, D), lambda j, i, k: (0, 0)), cdims=NT,
                  grid=(1, T // TMF, 1), acc_shape=(TMF, D),
                  extras=[(pcat, *_full_spec(OGP // D)), (pcat, *_full_spec(OGG // D)), (ylin, *_rowblk), (ygla, *_rowblk),
                          (pscale, *_vec)],
                  outs=[((T, D), BF16, *_rowblk)] * 4 + [((1, D), F32, *_vec)], epi=epi, after=after)


def mm_dog(dygla, wgo, o, pcat, ng, after=None):
    def epi(acc, ex, outs, i):
        o_ref, g_ref, ng_ref = ex
        do_ref, dg_ref, gng_ref = outs
        gparts = []
        for h in range(HEADS):
            cv = slice(h * DV, (h + 1) * DV)
            oh = o_ref[:, cv].astype(F32)
            r = lax.rsqrt(jnp.mean(oh * oh, axis=-1, keepdims=True) + EPS)
            on = oh * r
            gv = g_ref[:, cv].astype(F32)
            sg = _sigmoid(gv)
            a = acc[:, cv]
            dgain = a * (gv * sg)
            gparts.append(jnp.sum(dgain * on, axis=0, keepdims=True))
            ngh = ng_ref[:, cv]
            do_ref[:, cv] = _rms_bwd(on, r, dgain * ngh).astype(BF16)
            dg_ref[:, cv] = (a * (on * ngh) * (sg * (1.0 + gv * (1.0 - sg)))).astype(BF16)
        _row_acc(gng_ref, jnp.concatenate(gparts, axis=1), i)

    return matmul("mm_dog", dygla, wgo, a_spec=_rowblk, b_spec=((D, D), lambda j, i, k: (0, 0)), cdims=NT,
                  grid=(1, T // TMF, 1), acc_shape=(TMF, D),
                  extras=[(o, *_rowblk), (pcat, *_full_spec(OG // D)), (ng, *_vec)],
                  outs=[((T, D), BF16, *_rowblk), ((T, D), BF16, *_rowblk), ((1, D), F32, *_vec)], epi=epi, after=after)


def mm_dh1(dpcat, wcat, x, dx2, g1, after=None):
    tk = 2304

    def epi(acc, ex, outs, i):
        x_ref, dx2_ref, g_ref = ex
        xv = x_ref[...]
        r = lax.rsqrt(jnp.mean(xv * xv, axis=-1, keepdims=True) + EPS)
        xn = xv * r
        _row_acc(outs[1], jnp.sum(acc * xn, axis=0, keepdims=True), i)
        outs[0][...] = dx2_ref[...] + _rms_bwd(xn, r, acc * g_ref[...])

    y = square_matmul("mm_dh1", dpcat, wcat, a_spec=((TBIG, tk), lambda j, i, k: (i, k)),
                      b_spec=((TBIG, tk), lambda j, i, k: (j, k)), cdims=NT, nk=NCAT // tk, after=after)
    return rowwise("rows_dh1", y, extras=[(x, *_rowblk), (dx2, *_rowblk), (g1, *_vec)],
                   outs=[((T, D), F32, *_rowblk), ((1, D), F32, *_vec)], epi=epi)


def _tile_rows(rows, cols, n_arrays):
    tm = rows
    while tm % 32 == 0 and 2 * n_arrays * tm * cols * 4 > 24 * 1024 * 1024:
        tm //= 2
    return tm


def add_pairs(name, parts, theirs, core):
    _, _, r, c = parts.shape
    tm = _tile_rows(r, c, 3)

    def body(core_ref, a_ref, b_ref, o_ref):
        o_ref[...] = (a_ref[...].astype(F32) + b_ref[...].astype(F32)).astype(BF16)

    spec = pl.BlockSpec((None, tm, c), lambda j, i, core_ref: (j, i, 0))
    grid_spec = pltpu.PrefetchScalarGridSpec(
        num_scalar_prefetch=1, grid=(NCHIP, r // tm),
        in_specs=[pl.BlockSpec((None, None, tm, c), lambda j, i, core_ref: (core_ref[0], j, i, 0)), spec], out_specs=spec)
    return pl.pallas_call(body, name=name, grid_spec=grid_spec, out_shape=SDS((NCHIP, r, c), BF16),
                          compiler_params=_cparams(40 * 1024 * 1024, ("arbitrary", "arbitrary")))(core, parts, theirs)


def sum_chips(name, sums, landed, chip):
    _, r, c = sums.shape
    tm = _tile_rows(r, c, 4)

    def body(chip_ref, own_ref, l_ref, o_ref):
        s = own_ref[...].astype(F32)
        for t in range(NCHIP - 1):
            s = s + l_ref[t].astype(F32)
        o_ref[...] = s

    grid_spec = pltpu.PrefetchScalarGridSpec(
        num_scalar_prefetch=1, grid=(r // tm,),
        in_specs=[pl.BlockSpec((None, tm, c), lambda i, chip_ref: (chip_ref[0], i, 0)),
                  pl.BlockSpec((NCHIP - 1, tm, c), lambda i, chip_ref: (0, i, 0))],
        out_specs=pl.BlockSpec((tm, c), lambda i, chip_ref: (i, 0)))
    return pl.pallas_call(body, name=name, grid_spec=grid_spec, out_shape=SDS((r, c), F32),
                          compiler_params=_cparams(40 * 1024 * 1024, ("arbitrary",)))(chip, sums, landed)


def _adamw_math(wv, gv, mv, vv):
    mn = ADAM_B1 * mv + (1.0 - ADAM_B1) * gv
    vn = ADAM_B2 * vv + (1.0 - ADAM_B2) * (gv * gv)
    mh = mn / (1.0 - ADAM_B1 ** ADAM_STEP)
    vh = vn / (1.0 - ADAM_B2 ** ADAM_STEP)
    return -ADAM_LR * (mh / (jnp.sqrt(vh) + ADAM_EPS) + ADAM_WD * wv), mn, vn


def adamw(name, w, g, m, v):
    def body(w_ref, g_ref, m_ref, v_ref, go_ref, d_ref, mo_ref, vo_ref):
        gv = g_ref[...]
        go_ref[...] = gv
        d_ref[...], mo_ref[...], vo_ref[...] = _adamw_math(w_ref[...], gv, m_ref[...], v_ref[...])

    return pl.pallas_call(body, name=name, out_shape=[SDS(w.shape, F32)] * 4)(w, g, m, v)


def adamw_halves(name, w, g_own, g_sib, m, v, core):
    _, r, c = w.shape
    tm = _tile_rows(r, c, 10)

    def body(core_ref, w_ref, go_ref, gs_ref, m_ref, v_ref, g_out, d_out, m_out, v_out):
        gv = jnp.where(pl.program_id(0) == core_ref[0], go_ref[...], gs_ref[...])
        g_out[...] = gv
        d_out[...], m_out[...], v_out[...] = _adamw_math(w_ref[...], gv, m_ref[...], v_ref[...])

    full = pl.BlockSpec((None, tm, c), lambda h, i, core_ref: (h, i, 0))
    own = pl.BlockSpec((tm, c), lambda h, i, core_ref: (jnp.where(h == core_ref[0], i, 0), 0))
    sib = pl.BlockSpec((tm, c), lambda h, i, core_ref: (jnp.where(h == core_ref[0], 0, i), 0))
    grid_spec = pltpu.PrefetchScalarGridSpec(num_scalar_prefetch=1, grid=(2, r // tm),
                                             in_specs=[full, own, sib, full, full], out_specs=[full] * 4)
    return pl.pallas_call(body, name=name, grid_spec=grid_spec, out_shape=[SDS(w.shape, F32)] * 4,
                          compiler_params=_cparams(48 * 1024 * 1024, ("arbitrary", "arbitrary")))(core, w, g_own, g_sib, m, v)


def cast_bf16(name, w):
    _, r, c = w.shape
    tm = _tile_rows(r, c, 2)

    def body(w_ref, o_ref):
        o_ref[...] = w_ref[...].astype(BF16)

    spec = pl.BlockSpec((None, tm, c), lambda h, i: (h, i, 0))
    return pl.pallas_call(body, name=name, grid=(2, r // tm), in_specs=[spec], out_specs=spec, out_shape=SDS(w.shape, BF16),
                          compiler_params=_cparams(40 * 1024 * 1024, ("arbitrary", "arbitrary")))(w)


def pack_rows(name, parts, rows, after=None):
    width = parts[0].shape[1]
    n = len(parts)
    afters = _as_list(after)

    def body(*refs):
        out_ref = refs[n + len(afters)]
        out_ref[...] = jnp.zeros_like(out_ref)
        off = 0
        for p in refs[:n]:
            out_ref[off:off + p.shape[0], :] = p[...]
            off += p.shape[0]

    vm = pl.BlockSpec(memory_space=pltpu.VMEM)
    return pl.pallas_call(body, name=name, in_specs=[vm] * n + [ANY] * len(afters), out_specs=vm,
                          out_shape=SDS((rows, width), F32))(*parts, *afters)


def _place():
    x, y, c = lax.axis_index("x"), lax.axis_index("y"), lax.axis_index("c")
    chips = [(1 - x, y), (x, 1 - y), (1 - x, 1 - y)]
    return x, y, c, chips


def _row_split(shape, dtype):
    r, c = shape
    n = 1
    while r % (2 * n) == 0 and (r // (2 * n)) % 16 == 0 and (r // n) * c * jnp.dtype(dtype).itemsize > PIECE_BYTES:
        n *= 2
    return [pl.ds(s * (r // n), r // n) for s in range(n)]


def _pieces(ref):
    *lead, r, c = ref.shape
    split = _row_split((r, c), ref.dtype)
    return [ref.at[(*idx, s)] for idx in itertools.product(*[range(d) for d in lead]) for s in split]


HBM = pl.BlockSpec(memory_space=pltpu.HBM)
SEM = pl.BlockSpec(memory_space=pltpu.SEMAPHORE)
EFFECT = pltpu.SideEffectType.DATAFLOW_SIDE_EFFECTING


def gather_start(name, shards, after=None):
    n = len(shards)
    afters = _as_list(after)

    def body(*refs):
        src, land = refs[:n], refs[n:2 * n]
        send, recv = refs[2 * n + len(afters)], refs[2 * n + len(afters) + 1]
        x, y, c, chips = _place()
        me = 2 * x + y
        for a in range(n):
            for j, (cx, cy) in enumerate(chips[:2]):
                for sp, dp in zip(_pieces(src[a].at[c]), _pieces(land[a].at[me, c])):
                    pltpu.make_async_remote_copy(sp, dp, send.at[2 * a + j], recv.at[2 * a + j],
                                                 device_id=(cx, cy, c), device_id_type=MESH).start()

    lands = [pltpu.with_memory_space_constraint(lax.empty((NCHIP,) + s.shape, s.dtype), pltpu.HBM) for s in shards]
    srcs = [pltpu.with_memory_space_constraint(s, pltpu.HBM) for s in shards]
    outs = pl.pallas_call(
        body, name=name,
        out_shape=(pltpu.SemaphoreType.DMA((2 * n,)), pltpu.SemaphoreType.DMA((2 * n,)),
                   *[pltpu.HBM(s.shape, s.dtype) for s in shards], *[pltpu.HBM(l.shape, l.dtype) for l in lands]),
        in_specs=[HBM] * (2 * n) + [ANY] * len(afters), out_specs=(SEM, SEM, *([HBM] * (2 * n))),
        input_output_aliases={i: 2 + i for i in range(2 * n)},
        compiler_params=pltpu.CompilerParams(has_side_effects=EFFECT),
    )(*srcs, *lands, *afters)
    return outs[0], outs[1], list(outs[2:2 + n]), list(outs[2 + n:2 + 2 * n])


def gather_wait(name, send, recv, shards, lands, after):
    n = len(shards)
    afters = _as_list(after)

    def body(*refs):
        src, land = refs[:n], refs[n:2 * n]
        send_ref, recv_ref = refs[2 * n], refs[2 * n + 1]
        x, y, c, chips = _place()
        for a in range(n):
            for j, (cx, cy) in enumerate(chips[:2]):
                cp = pltpu.make_async_remote_copy(src[a].at[c], land[a].at[2 * cx + cy, c], send_ref.at[2 * a + j],
                                                  recv_ref.at[2 * a + j], device_id=(cx, cy, c), device_id_type=MESH)
                cp.wait_send()
                cp.wait_recv()

    outs = pl.pallas_call(
        body, name=name,
        out_shape=(*[pltpu.HBM(s.shape, s.dtype) for s in shards], *[pltpu.HBM(l.shape, l.dtype) for l in lands]),
        in_specs=[HBM] * (2 * n) + [SEM, SEM] + [ANY] * len(afters), out_specs=[HBM] * (2 * n),
        input_output_aliases={i: i for i in range(2 * n)},
        compiler_params=pltpu.CompilerParams(has_side_effects=EFFECT),
    )(*shards, *lands, send, recv, *afters)
    return list(outs[:n]), list(outs[n:])


def _relay_blocks(land, c, chips):
    (xx, xy), (yx, yy), (dx, dy) = chips
    rows = land.shape[2] // 2
    upper, lower = pl.ds(0, rows), pl.ds(rows, rows)
    return [(land.at[2 * yx + yy, c, lower], land.at[2 * dx + dy, c, lower]),
            (land.at[2 * xx + xy, c, upper], land.at[2 * dx + dy, c, upper])]


def relay_start(name, lands, after=None):
    n = len(lands)
    afters = _as_list(after)

    def body(*refs):
        had, land = refs[:n], refs[n + len(afters) + 2:2 * n + len(afters) + 2]
        send, recv = refs[n + len(afters)], refs[n + len(afters) + 1]
        x, y, c, chips = _place()
        for a in range(n):
            for j, ((sent, _), (dst, _)) in enumerate(zip(_relay_blocks(had[a], c, chips), _relay_blocks(land[a], c, chips))):
                cx, cy = chips[j]
                for sp, dp in zip(_pieces(sent), _pieces(dst)):
                    pltpu.make_async_remote_copy(sp, dp, send.at[2 * a + j], recv.at[2 * a + j],
                                                 device_id=(cx, cy, c), device_id_type=MESH).start()

    outs = pl.pallas_call(
        body, name=name,
        out_shape=(pltpu.SemaphoreType.DMA((2 * n,)), pltpu.SemaphoreType.DMA((2 * n,)),
                   *[pltpu.HBM(l.shape, l.dtype) for l in lands]),
        in_specs=[HBM] * n + [ANY] * len(afters), out_specs=(SEM, SEM, *([HBM] * n)),
        input_output_aliases={i: 2 + i for i in range(n)},
        compiler_params=pltpu.CompilerParams(has_side_effects=EFFECT),
    )(*lands, *afters)
    return outs[0], outs[1], list(outs[2:])


def relay_wait(name, send, recv, lands, after):
    n = len(lands)
    afters = _as_list(after)

    def body(*refs):
        land = refs[:n]
        send_ref, recv_ref = refs[n], refs[n + 1]
        x, y, c, chips = _place()
        for a in range(n):
            for j, (sent, got) in enumerate(_relay_blocks(land[a], c, chips)):
                cx, cy = chips[j]
                cp = pltpu.make_async_remote_copy(sent, got, send_ref.at[2 * a + j], recv_ref.at[2 * a + j],
                                                  device_id=(cx, cy, c), device_id_type=MESH)
                cp.wait_send()
                cp.wait_recv()

    outs = pl.pallas_call(
        body, name=name, out_shape=tuple(pltpu.HBM(l.shape, l.dtype) for l in lands),
        in_specs=[HBM] * n + [SEM, SEM] + [ANY] * len(afters), out_specs=[HBM] * n,
        input_output_aliases={i: i for i in range(n)},
        compiler_params=pltpu.CompilerParams(has_side_effects=EFFECT),
    )(*lands, send, recv, *afters)
    return list(outs)


def forward_halves(name, shards, lands):
    n = len(lands)

    def body(*refs):
        had, buf = refs[:n], refs[n:2 * n]
        send, recv = refs[2 * n:]
        x, y, c, chips = _place()
        sib = (x, y, 1 - c)
        for a in range(n):
            for j, (cx, cy) in enumerate(chips):
                for sp, dp in zip(_pieces(had[a].at[2 * cx + cy, c]), _pieces(buf[a].at[2 * cx + cy, c])):
                    pltpu.make_async_remote_copy(sp, dp, send.at[3 * a + j], recv.at[3 * a + j], device_id=sib, device_id_type=MESH).start()
        for a in range(n):
            for j, (cx, cy) in enumerate(chips):
                pltpu.make_async_remote_copy(had[a].at[2 * cx + cy, c], buf[a].at[2 * cx + cy, 1 - c], send.at[3 * a + j],
                                             recv.at[3 * a + j], device_id=sib, device_id_type=MESH).wait()

    got = pl.pallas_call(
        body, name=name, in_specs=[ANY] * n, out_specs=[ANY] * n, out_shape=[SDS(l.shape, l.dtype) for l in lands],
        input_output_aliases={i: i for i in range(n)},
        scratch_shapes=[pltpu.SemaphoreType.DMA((3 * n,)), pltpu.SemaphoreType.DMA((3 * n,))],
    )(*lands)
    me = 2 * lax.axis_index("x") + lax.axis_index("y")
    return [lax.dynamic_update_index_in_dim(g, s, me, 0) for g, s in zip(got, shards)]


def exchange_start(name, parts):
    n = len(parts)

    def body(*refs):
        src, got = refs[:n], refs[n:2 * n]
        send, recv = refs[2 * n], refs[2 * n + 1]
        token = refs[4 * n + 2]
        x, y, c, _ = _place()
        sib = (x, y, 1 - c)
        for a in range(n):
            for sp, dp in zip(_pieces(src[a].at[1 - c]), _pieces(got[a])):
                pltpu.make_async_remote_copy(sp, dp, send.at[a], recv.at[a], device_id=sib, device_id_type=MESH).start()
        token[...] = jnp.zeros_like(token)

    lands = [pltpu.with_memory_space_constraint(lax.empty(p.shape[1:], p.dtype), pltpu.HBM) for p in parts]
    srcs = [pltpu.with_memory_space_constraint(p, pltpu.HBM) for p in parts]
    outs = pl.pallas_call(
        body, name=name,
        out_shape=(pltpu.SemaphoreType.DMA((n,)), pltpu.SemaphoreType.DMA((n,)),
                   *[pltpu.HBM(p.shape, p.dtype) for p in parts], *[pltpu.HBM(l.shape, l.dtype) for l in lands],
                   SDS((8, 128), F32)),
        in_specs=[HBM] * (2 * n), out_specs=(SEM, SEM, *([HBM] * (2 * n)), pl.BlockSpec(memory_space=pltpu.VMEM)),
        input_output_aliases={i: 2 + i for i in range(2 * n)},
        compiler_params=pltpu.CompilerParams(has_side_effects=EFFECT),
    )(*srcs, *lands)
    return outs[0], outs[1], list(outs[2:2 + n]), list(outs[2 + n:2 + 2 * n]), outs[2 + 2 * n]


def exchange_wait(name, send, recv, parts, lands, after):
    n = len(parts)
    afters = _as_list(after)

    def body(*refs):
        src, got = refs[:n], refs[n:2 * n]
        send_ref, recv_ref = refs[2 * n], refs[2 * n + 1]
        x, y, c, _ = _place()
        sib = (x, y, 1 - c)
        for a in range(n):
            cp = pltpu.make_async_remote_copy(src[a].at[1 - c], got[a], send_ref.at[a], recv_ref.at[a], device_id=sib, device_id_type=MESH)
            cp.wait_send()
            cp.wait_recv()

    outs = pl.pallas_call(
        body, name=name,
        out_shape=(*[pltpu.HBM(p.shape, p.dtype) for p in parts], *[pltpu.HBM(l.shape, l.dtype) for l in lands]),
        in_specs=[HBM] * (2 * n) + [SEM, SEM] + [ANY] * len(afters), out_specs=[HBM] * (2 * n),
        input_output_aliases={i: i for i in range(2 * n)},
        compiler_params=pltpu.CompilerParams(has_side_effects=EFFECT),
    )(*parts, *lands, send, recv, *afters)
    return list(outs[:n]), list(outs[n:])


def scatter_start(name, parts):
    n = len(parts)

    def body(*refs):
        src, land = refs[:n], refs[n:2 * n]
        send, recv = refs[2 * n], refs[2 * n + 1]
        token = refs[4 * n + 2]
        x, y, c, chips = _place()
        for a in range(n):
            for j, (cx, cy) in enumerate(chips):
                for sp, dp in zip(_pieces(src[a].at[2 * cx + cy]), _pieces(land[a].at[j])):
                    pltpu.make_async_remote_copy(sp, dp, send.at[3 * a + j], recv.at[3 * a + j],
                                                 device_id=(cx, cy, c), device_id_type=MESH).start()
        token[...] = jnp.zeros_like(token)

    lands = [pltpu.with_memory_space_constraint(lax.empty((NCHIP - 1,) + p.shape[1:], p.dtype), pltpu.HBM) for p in parts]
    srcs = [pltpu.with_memory_space_constraint(p, pltpu.HBM) for p in parts]
    outs = pl.pallas_call(
        body, name=name,
        out_shape=(pltpu.SemaphoreType.DMA((3 * n,)), pltpu.SemaphoreType.DMA((3 * n,)),
                   *[pltpu.HBM(p.shape, p.dtype) for p in parts], *[pltpu.HBM(l.shape, l.dtype) for l in lands],
                   SDS((8, 128), F32)),
        in_specs=[HBM] * (2 * n), out_specs=(SEM, SEM, *([HBM] * (2 * n)), pl.BlockSpec(memory_space=pltpu.VMEM)),
        input_output_aliases={i: 2 + i for i in range(2 * n)},
        compiler_params=pltpu.CompilerParams(has_side_effects=EFFECT),
    )(*srcs, *lands)
    return outs[0], outs[1], list(outs[2:2 + n]), list(outs[2 + n:2 + 2 * n]), outs[2 + 2 * n]


def scatter_wait(name, send, recv, parts, lands, after):
    n = len(parts)
    afters = _as_list(after)

    def body(*refs):
        src, land = refs[:n], refs[n:2 * n]
        send_ref, recv_ref = refs[2 * n], refs[2 * n + 1]
        x, y, c, chips = _place()
        for a in range(n):
            for j, (cx, cy) in enumerate(chips):
                cp = pltpu.make_async_remote_copy(src[a].at[2 * cx + cy], land[a].at[j], send_ref.at[3 * a + j], recv_ref.at[3 * a + j],
                                                  device_id=(cx, cy, c), device_id_type=MESH)
                cp.wait_send()
                cp.wait_recv()

    outs = pl.pallas_call(
        body, name=name,
        out_shape=(*[pltpu.HBM(p.shape, p.dtype) for p in parts], *[pltpu.HBM(l.shape, l.dtype) for l in lands]),
        in_specs=[HBM] * (2 * n) + [SEM, SEM] + [ANY] * len(afters), out_specs=[HBM] * (2 * n),
        input_output_aliases={i: i for i in range(2 * n)},
        compiler_params=pltpu.CompilerParams(has_side_effects=EFFECT),
    )(*parts, *lands, send, recv, *afters)
    return list(outs[:n]), list(outs[n:])


def join_start(name, halves):
    n = len(halves)

    def body(*refs):
        src, dst = refs[:n], refs[n:2 * n]
        send, recv = refs[2 * n], refs[2 * n + 1]
        token = refs[4 * n + 2]
        x, y, c, _ = _place()
        sib = (x, y, 1 - c)
        for a in range(n):
            for sp, dp in zip(_pieces(src[a]), _pieces(dst[a])):
                pltpu.make_async_remote_copy(sp, dp, send.at[a], recv.at[a], device_id=sib, device_id_type=MESH).start()
        token[...] = jnp.zeros_like(token)

    lands = [pltpu.with_memory_space_constraint(lax.empty(h.shape, h.dtype), pltpu.HBM) for h in halves]
    srcs = [pltpu.with_memory_space_constraint(h, pltpu.HBM) for h in halves]
    outs = pl.pallas_call(
        body, name=name,
        out_shape=(pltpu.SemaphoreType.DMA((n,)), pltpu.SemaphoreType.DMA((n,)),
                   *[pltpu.HBM(h.shape, h.dtype) for h in halves], *[pltpu.HBM(l.shape, l.dtype) for l in lands],
                   SDS((8, 128), F32)),
        in_specs=[HBM] * (2 * n), out_specs=(SEM, SEM, *([HBM] * (2 * n)), pl.BlockSpec(memory_space=pltpu.VMEM)),
        input_output_aliases={i: 2 + i for i in range(2 * n)},
        compiler_params=pltpu.CompilerParams(has_side_effects=EFFECT),
    )(*srcs, *lands)
    return outs[0], outs[1], list(outs[2:2 + n]), list(outs[2 + n:2 + 2 * n]), outs[2 + 2 * n]


def join_wait(name, send, recv, halves, lands, after):
    n = len(halves)
    afters = _as_list(after)

    def body(*refs):
        src, dst = refs[:n], refs[n:2 * n]
        send_ref, recv_ref = refs[2 * n], refs[2 * n + 1]
        x, y, c, _ = _place()
        sib = (x, y, 1 - c)
        for a in range(n):
            cp = pltpu.make_async_remote_copy(src[a], dst[a], send_ref.at[a], recv_ref.at[a], device_id=sib, device_id_type=MESH)
            cp.wait_send()
            cp.wait_recv()

    outs = pl.pallas_call(
        body, name=name,
        out_shape=(*[pltpu.HBM(h.shape, h.dtype) for h in halves], *[pltpu.HBM(l.shape, l.dtype) for l in lands]),
        in_specs=[HBM] * (2 * n) + [SEM, SEM] + [ANY] * len(afters), out_specs=[HBM] * (2 * n),
        input_output_aliases={i: i for i in range(2 * n)},
        compiler_params=pltpu.CompilerParams(has_side_effects=EFFECT),
    )(*halves, *lands, send, recv, *afters)
    return list(outs[:n]), list(outs[n:])


def gather_small(name, xs, reduce, after=None):
    m, ncol = xs.shape
    afters = _as_list(after)

    def body(x_ref, *rest):
        out_ref, all_ref, send, recv, lsem = rest[len(afters):]
        x, y, c, chips = _place()
        me, sib = (x, y, c), (x, y, 1 - c)

        def rows(px, py, pc):
            return all_ref.at[pl.ds((4 * px + 2 * py + pc) * m, m), :]

        def copy(k, block, to, src=None):
            return pltpu.make_async_remote_copy(rows(*block) if src is None else src, rows(*block), send.at[k], recv.at[k],
                                                device_id=to, device_id_type=MESH)

        mine = pltpu.make_async_copy(x_ref, rows(*me), lsem)
        mine.start()
        first = [copy(0, me, sib, src=x_ref)] + [copy(1 + j, me, (*chip, c), src=x_ref) for j, chip in enumerate(chips)]
        for cp in first:
            cp.start()
        passed = [copy(4 + j, (*chip, c), sib) for j, chip in enumerate(chips)]
        for j, chip in enumerate(chips):
            copy(1 + j, (*chip, c), me).wait_recv()
            passed[j].start()
        copy(0, sib, me).wait_recv()
        for j, chip in enumerate(chips):
            copy(4 + j, (*chip, 1 - c), me).wait_recv()
        for cp in first + passed:
            cp.wait_send()
        mine.wait()
        if reduce:
            s = all_ref[0:m, :]
            for dev in range(1, 8):
                s = s + all_ref[dev * m:(dev + 1) * m, :]
            out_ref[...] = s
        else:
            out_ref[...] = all_ref[...]

    vm = pl.BlockSpec(memory_space=pltpu.VMEM)
    return pl.pallas_call(
        body, name=name, in_specs=[vm] + [ANY] * len(afters), out_specs=vm,
        out_shape=SDS((m, ncol) if reduce else (8 * m, ncol), F32),
        scratch_shapes=[pltpu.VMEM((8 * m, ncol), F32), pltpu.SemaphoreType.DMA((7,)), pltpu.SemaphoreType.DMA((7,)),
                        pltpu.SemaphoreType.DMA],
    )(xs, *afters)


RELAYOUT_ROWS = 128


def weights_to_cat(g_in, after=None):
    tm = RELAYOUT_ROWS
    afters = _as_list(after)

    def body(g_ref, *rest):
        o_ref = rest[len(afters)]
        nat = jnp.concatenate([g_ref[j] for j in range(NCHIP)], axis=1)
        pad = jnp.zeros((tm, NCAT - OA - 16), BF16)
        o_ref[...] = jnp.concatenate([nat[:, 3072:7168], nat[:, 7184:11280], nat[:, 0:3072], nat[:, 7168:7184], pad], axis=1)

    return pl.pallas_call(
        body, name="weights_to_cat", grid=(D // tm,),
        in_specs=[pl.BlockSpec((NCHIP, tm, IN_SHARD), lambda i: (0, i, 0))] + [ANY] * len(afters),
        out_specs=pl.BlockSpec((tm, NCAT), lambda i: (i, 0)), out_shape=SDS((D, NCAT), BF16),
        compiler_params=_cparams(40 * 1024 * 1024, ("arbitrary",)),
    )(g_in, *afters)


def grads_from_cat(gw_cat):
    tm = RELAYOUT_ROWS
    nb = (D // 2) // tm

    def body(c_ref, o_ref):
        cat = c_ref[...]
        nat = jnp.concatenate([cat[:, OU:OA], cat[:, OV:OGP], cat[:, OA:OA + 16], cat[:, OGP:OU]], axis=1)
        for j in range(NCHIP):
            o_ref[j] = nat[:, j * IN_SHARD:(j + 1) * IN_SHARD]

    return pl.pallas_call(
        body, name="grads_from_cat", grid=(D // tm,), in_specs=[pl.BlockSpec((tm, NCAT), lambda i: (i, 0))],
        out_specs=pl.BlockSpec((None, NCHIP, tm, IN_SHARD), lambda i: (i // nb, 0, i % nb, 0)),
        out_shape=SDS((2, NCHIP, D // 2, IN_SHARD), BF16), compiler_params=_cparams(40 * 1024 * 1024, ("arbitrary",)),
    )(gw_cat)


def _pad_rows(a, rows):
    return jnp.concatenate([a, jnp.zeros((rows - a.shape[0],) + a.shape[1:], a.dtype)], axis=0)


def local_step(x2d, tgt, gf, g1, pool_scale, wa_pad, b_alpha, ng, g2, get_w, on_grad=None, on_settle=None, tick=None):
    emit = on_grad if on_grad is not None else (lambda group, grads: None)
    settle = on_settle if on_settle is not None else (lambda group, after: None)
    h1 = norm1(x2d, g1)
    wcat, pw = get_w("in", h1)
    pcat = mm_in(h1, wcat)
    dpool, ylin = pool_fwd(pcat, pw)
    pinned = tick("pool", ylin) if tick is not None else None
    og, o, states = gla_fwd(pcat, wa_pad, b_alpha, ng, pinned)
    w_go, w_o = get_w("mid", og)
    mixed, ygla = mm_gla_out(og, w_go, ylin, pcat, pool_scale)
    x2, h2 = mm_out(mixed, w_o, x2d, g2)
    w_up = get_w("up", h2)
    rup, act = mm_up(h2, w_up)
    w_dn = get_w("down", act)
    dx3, dx3b, g_nf, loss_row = mm_down(act, w_dn, x2, tgt, gf)

    gw_down = mm_wgrad("mm_dw_down", act, dx3b, DFF, D, (2, NCHIP, D // 2, D), (None, None, 512, D),
                       lambda j, i, k: ((i // 2) % 2, i // 4, i % 2, 0), 512, D)
    token = emit("down", {"down": gw_down})
    dup = mm_dact(dx3b, w_dn, rup, after=token)
    token = settle("down", dup)
    dx2, dx2b, g_mlp = mm_dh2(dup, w_up, x2, dx3, g2, after=token)
    gw_up = mm_wgrad("mm_dw_up", h2, dup, D, DFF, (2, NCHIP, D // 2, D), (None, None, 512, D),
                     lambda j, i, k: (i // 2, j, i % 2, 0), 512, D)
    token = emit("up", {"up": gw_up})
    dylin, dygla, dlgp, dlgg, g_ps = mm_dmixed(dx2b, w_o, pcat, ylin, ygla, pool_scale, after=token)
    token = settle("up", dylin)
    gw_out = mm_wgrad("mm_dw_out", mixed, dx2b, D, D, (2, NCHIP, 256, D), (None, None, 256, D),
                      lambda j, i, k: (i % 2, i // 2, 0, 0), 256, D)
    do, dg, g_ng = mm_dog(dygla, w_go, o, pcat, ng, after=token)
    gw_go = mm_wgrad("mm_dw_gla_out", og, dygla, D, D, (2, NCHIP, 256, D), (None, None, 256, D),
                     lambda j, i, k: (i % 2, i // 2, 0, 0), 256, D)
    token = emit("mix", {"out": gw_out, "gla_out": gw_go})
    dq, dk, dv, dalow, g_wa, g_ba = gla_bwd(do, pcat, states, wa_pad, b_alpha, b_alpha if token is None else token)
    token = settle("mix", dq)
    du, dpw = pool_bwd(dylin, dpool, pw)
    dpcat = jnp.concatenate([dv, dg, dlgp, dlgg, du, dq, dk, dalow, jnp.zeros((T, NCAT - OA - APAD), BF16)], axis=1)
    gw_cat = mm_wgrad("mm_dw_in", h1, dpcat, D, NCAT, (D, NCAT), (1024, 1280), lambda j, i, k: (i, j), 1024, 1280, after=token)
    token = settle("in", emit("in", {"in_cat": gw_cat, "pool": dpw}))
    grad_x, g_mix = mm_dh1(dpcat, wcat, x2d, dx2, g1, after=token)
    return (loss_row[0, 0], grad_x, g_mix, g_ps, g_mlp, g_nf, g_ng, g_ba, g_wa, token,
            gw_cat, dpw, gw_go, gw_out, gw_up, gw_down)


def kernel(x, norm_mix_g, w_in, pool_w, pool_scale, w_alpha, b_alpha, gla_norm_g, w_gla_out, w_out, norm_mlp_g, w_mlp_up, w_mlp_down, norm_final_g, loss_target, m_norm_mix_g, m_w_in, m_pool_w, m_pool_scale, m_w_alpha, m_b_alpha, m_gla_norm_g, m_w_gla_out, m_w_out, m_norm_mlp_g, m_w_mlp_up, m_w_mlp_down, m_norm_final_g, v_norm_mix_g, v_w_in, v_pool_w, v_pool_scale, v_w_alpha, v_b_alpha, v_gla_norm_g, v_w_gla_out, v_w_out, v_norm_mlp_g, v_w_mlp_up, v_w_mlp_down, v_norm_final_g):
    chip = 2 * lax.axis_index("x") + lax.axis_index("y")
    chip_i = chip.astype(jnp.int32).reshape(1)
    core_i = lax.axis_index("c").astype(jnp.int32).reshape(1)
    tgt = loss_target.reshape(T, D)
    gf = norm_final_g.reshape(1, D)

    def halves(w2d):
        r, c = w2d.shape
        return w2d.astype(BF16).reshape(2, r // 2, c)

    pool_shard = pool_w.reshape(4 * PG, PO // NCHIP)
    w_in_r = w_in.reshape(2, D // 2, IN_SHARD)
    sent = {"in": [cast_bf16("cast_w_in", w_in_r), halves(pool_shard)]}
    flight = {}

    def start(group, after=None):
        flight[group] = gather_start("gather_start_" + group, sent[group], after)

    def relay(group, after):
        send, recv, shards, lands = flight[group]
        shards, lands = gather_wait("gather_wait_" + group, send, recv, shards, lands, after)
        send, recv, lands = relay_start("relay_start_" + group, lands)
        flight[group] = (send, recv, shards, lands)

    def fetch(group, after):
        send, recv, shards, lands = flight[group]
        lands = relay_wait("relay_wait_" + group, send, recv, lands, after)
        return forward_halves("forward_" + group, shards, lands)

    start("in")
    m_in_f, v_in_f, w_go_f, w_o_f, w_up_f, w_dn_f, x_f, wal_f, gng_f = lax.optimization_barrier(
        (m_w_in, v_w_in, w_gla_out, w_out, w_mlp_up, w_mlp_down, x, w_alpha, gla_norm_g, flight["in"][2][0]))[:9]
    m_in_r, v_in_r = m_in_f.reshape(2, D // 2, IN_SHARD), v_in_f.reshape(2, D // 2, IN_SHARD)
    sent["mid"] = [halves(w_go_f[0]), halves(w_o_f[0])]
    relay("in", [m_in_r, v_in_r, *sent["mid"]])
    start("mid", flight["in"][3][0])
    w_up_f, w_dn_f, x_f, wal_f, gng_f = lax.optimization_barrier(
        (w_up_f, w_dn_f, x_f, wal_f, gng_f, flight["mid"][2][0]))[:5]
    sent["up"], sent["down"] = [halves(w_up_f[0])], [halves(w_dn_f[0])]
    x2d = x_f.reshape(T, D)
    big = [w_in_r, w_go_f[0], w_o_f[0], w_up_f[0], w_dn_f[0], pool_shard]

    def tick(point, after):
        if point == "pool":
            relay("mid", after)
            start("down", flight["mid"][3][0])
            return [flight["mid"][3][0], flight["down"][3][0]]

    def get_w(group, after):
        if group == "in":
            after = [after, *sent["up"], *sent["down"], wa_pad]
        if group == "mid":
            relay("up", after)
            after = flight["up"][3][0]
        if group == "up":
            relay("down", after)
            after = flight["down"][3][0]
        whole = fetch(group, after)
        if group == "in":
            start("up", whole[0])
            g_in, g_pool = whole
            wcat = weights_to_cat(g_in.reshape(NCHIP, D, IN_SHARD), flight["up"][3][0])
            pw = jnp.concatenate([g_pool[j].reshape(4, PG, PO // NCHIP) for j in range(NCHIP)], axis=2)
            return wcat, pw
        if group == "mid":
            return whole[0].reshape(D, D), whole[1].reshape(D, D)
        if group == "up":
            return whole[0].reshape(NCHIP, D, D)
        return whole[0].reshape(DFF, D)

    small_w = pack_rows("pack_small_w", [wal_f[0].reshape(4, QK),
                                         jnp.concatenate([gng_f[0].reshape(1, 512), jnp.zeros((1, 512), F32)], axis=1)], 8)
    sw_all = gather_small("gather_small_w", small_w, False).reshape(8, 8, QK)
    wa_full = jnp.concatenate([sw_all[2 * j, 0:4].reshape(16, DK) for j in range(NCHIP)], axis=1)
    ng_full = jnp.concatenate([sw_all[2 * j, 4, 0:512].reshape(HEADS, DV // NCHIP) for j in range(NCHIP)], axis=1)
    wa_pad = _pad_rows(wa_full, APAD).astype(BF16)
    ng = ng_full.reshape(1, D)

    pending = {}
    wmv = {"in": (w_in_r, m_in_r, v_in_r), "gla_out": (big[1], m_w_gla_out, v_w_gla_out), "out": (big[2], m_w_out, v_w_out),
           "up": (big[3], m_w_mlp_up, v_w_mlp_up), "down": (big[4], m_w_mlp_down, v_w_mlp_down), "pool": (big[5], m_pool_w, v_pool_w)}
    big_res = {}

    def reduce_group(group, after):
        nms, send, recv, sums, lands = pending[group]
        sums, lands = scatter_wait("scatter_wait_" + group, send, recv, sums, lands, after)
        reduced = [sum_chips("sum_chips_" + nm, a, b, chip_i) for nm, a, b in zip(nms, sums, lands)]
        send, recv, reduced, lands, token = join_start("join_start_" + group, reduced)
        pending[group] = (nms, send, recv, reduced, lands)
        return token

    def update_group(group, after):
        nms, send, recv, reduced, lands = pending[group]
        reduced, from_sib = join_wait("join_wait_" + group, send, recv, reduced, lands, after)
        for nm, g_own, g_sib in zip(nms, reduced, from_sib):
            w, m, v = wmv[nm]
            shp = (2,) + g_own.shape
            big_res[nm] = adamw_halves("adamw_" + nm, w.reshape(shp), g_own, g_sib, m.reshape(shp), v.reshape(shp), core_i)

    def on_grad(group, grads):
        if group == "in":
            gw_in = grads_from_cat(grads["in_cat"])
            gw_pool = jnp.stack([grads["pool"][:, :, j * 128:(j + 1) * 128].reshape(2, 2 * PG, 128)
                                 for j in range(NCHIP)], axis=1)
            grads = {"in": gw_in, "pool": gw_pool}
        nms, parts = list(grads.keys()), list(grads.values())
        send, recv, parts, got, token = exchange_start("exchange_start_" + group, parts)
        pending[group] = (nms, send, recv, parts, got)
        return token

    def on_settle(group, after):
        if group == "in":
            after = reduce_group("down", after)
        nms, send, recv, parts, got = pending[group]
        parts, got = exchange_wait("exchange_wait_" + group, send, recv, parts, got, after)
        sums = [add_pairs("add_pair_" + nm, a, b, core_i) for nm, a, b in zip(nms, parts, got)]
        send, recv, sums, lands, token = scatter_start("scatter_start_" + group, sums)
        pending[group] = (nms, send, recv, sums, lands)
        if group != "in":
            return token
        token = reduce_group("up", token)
        token = reduce_group("mix", token)
        for earlier in ("down", "up", "mix"):
            update_group(earlier, token)
            token = big_res[pending[earlier][0][-1]][1]
        return [big_res[nm][1] for nm in ("down", "up", "out", "gla_out")]

    (loss_local, grad_x, g_mix, g_ps, g_mlp, g_nf, g_ng, g_ba, g_wa) = local_step(
        x2d, tgt, gf, norm_mix_g, pool_scale, wa_pad, b_alpha, ng, norm_mlp_g, get_w, on_grad, on_settle, tick)[:9]
    loss = lax.psum(loss_local, ("x", "y", "c"))
    join_in_token = reduce_group("in", grad_x)

    ROWS = 16

    def wide(a, n):
        return jnp.concatenate([a.reshape(1, n), jnp.zeros((1, D - n), F32)], axis=1)

    packed = pack_rows("pack_small_g", [g_mix, g_ps, g_mlp, g_nf, g_ng, wide(g_ba, QK), g_wa[0:16].reshape(8, D)], ROWS)
    tot = gather_small("reduce_small_g", packed, True, join_in_token)
    t_wa = lax.dynamic_slice(tot[6:14].reshape(16, QK), (0, chip * DK), (16, DK))
    t_ng = lax.dynamic_slice(tot[4].reshape(HEADS, DV), (0, chip * (DV // NCHIP)), (HEADS, DV // NCHIP))

    def pack_small(nm, mix, ps, mlp, nf, ba, wa, gn, after=None):
        return pack_rows(nm, [mix.reshape(1, D), ps.reshape(1, D), mlp.reshape(1, D), nf.reshape(1, D), wide(ba, QK),
                              wa.reshape(2, D), wide(gn, 512)], ROWS, after)

    update_group("in", tot)
    sg = pack_small("pack_g", tot[0], tot[1], tot[2], tot[3], tot[5, 0:QK], t_wa, t_ng, big_res["in"][3])
    sw = pack_small("pack_w", norm_mix_g, pool_scale, norm_mlp_g, norm_final_g, b_alpha, w_alpha, gla_norm_g)
    sm = pack_small("pack_m", m_norm_mix_g, m_pool_scale, m_norm_mlp_g, m_norm_final_g, m_b_alpha, m_w_alpha, m_gla_norm_g)
    sv = pack_small("pack_v", v_norm_mix_g, v_pool_scale, v_norm_mlp_g, v_norm_final_g, v_b_alpha, v_w_alpha, v_gla_norm_g)
    small_res = adamw("adamw_small", sw, sg, sm, sv)

    def unpack(p):
        return {"norm_mix_g": p[0].reshape(1, D), "pool_scale": p[1].reshape(1, D), "norm_mlp_g": p[2].reshape(1, D),
                "norm_final_g": p[3].reshape(D), "b_alpha": p[4, 0:QK].reshape(1, QK), "w_alpha": p[5:7].reshape(1, 16, DK),
                "gla_norm_g": p[7, 0:512].reshape(1, HEADS, DV // NCHIP)}

    order = ["norm_mix_g", "w_in", "pool_w", "pool_scale", "w_alpha", "b_alpha", "gla_norm_g", "w_gla_out", "w_out",
             "norm_mlp_g", "w_mlp_up", "w_mlp_down", "norm_final_g"]
    big_key = {"w_in": ("in", w_in.shape), "pool_w": ("pool", pool_w.shape), "w_gla_out": ("gla_out", w_gla_out.shape),
               "w_out": ("out", w_out.shape), "w_mlp_up": ("up", w_mlp_up.shape), "w_mlp_down": ("down", w_mlp_down.shape)}
    result = [loss, grad_x.reshape(1, T, D)]
    for kind in range(4):
        small = unpack(small_res[kind])
        for nm in order:
            if nm in big_key:
                key, shp = big_key[nm]
                result.append(big_res[key][kind].reshape(shp))
            else:
                result.append(small[nm])
    return tuple(result)
```

```python
import itertools

import jax
import jax.numpy as jnp
from jax import lax
from jax.experimental import pallas as pl
from jax.experimental.pallas import tpu as pltpu

F32 = jnp.float32
BF16 = jnp.bfloat16
SDS = jax.ShapeDtypeStruct
MESH = pl.DeviceIdType.MESH
ANY = pl.BlockSpec(memory_space=pl.ANY)

T = 2048
D = 2048
DFF = 8192
NCHIP = 4
IN_WIDTH = 11280
IN_SHARD = IN_WIDTH // NCHIP
CHUNK = 64
NCHUNK = T // CHUNK
HEADS = 4
DK = 256
DV = 512
QK = HEADS * DK
EPS = 1e-6
POOL_WINDOWS = (2, 4, 8, 16)
PG = 256
PO = 512

OV, OG, OGP, OGG, OU, OQ, OKK, OA = 0, 2048, 4096, 6144, 8192, 9216, 10240, 11264
NCAT = 11520
APAD = 128

VMEM_CAP = 56 * 1024 * 1024

PIECE_BYTES = 384 * 1024

ADAM_LR, ADAM_B1, ADAM_B2, ADAM_EPS, ADAM_WD, ADAM_STEP = 0.001, 0.9, 0.999, 1e-08, 0.01, 10


def _cparams(vmem_bytes=None, sem=None):
    kw = {}
    if vmem_bytes is not None:
        kw["vmem_limit_bytes"] = int(min(max(vmem_bytes, 32 * 1024 * 1024), VMEM_CAP))
    if sem is not None:
        kw["dimension_semantics"] = sem
    return pltpu.CompilerParams(**kw)


def _nbytes(shape, dtype):
    n = 1
    for s in shape:
        if s is not None:
            n *= s
    return n * jnp.dtype(dtype).itemsize


def _sigmoid(x):
    return 0.5 * jnp.tanh(0.5 * x) + 0.5


EPI_COLS = 512


def _as_list(after):
    if after is None:
        return []
    return list(after) if isinstance(after, (list, tuple)) else [after]


def matmul(name, a, b, *, a_spec, b_spec, cdims, grid, acc_shape, outs, extras=(), epi, after=None):
    nj, ni, nk = grid
    ne, no = len(extras), len(outs)
    afters = _as_list(after)
    first_out = 2 + ne + len(afters)

    def body(*refs):
        a_ref, b_ref = refs[0], refs[1]
        ex = refs[2:2 + ne]
        out_refs = refs[first_out:first_out + no]
        i = pl.program_id(1)
        part = lax.dot_general(a_ref[...], b_ref[...], (cdims, ((), ())), preferred_element_type=F32)
        if nk == 1:
            epi(part, ex, out_refs, i)
        else:
            acc_ref = refs[first_out + no]
            k = pl.program_id(2)

            @pl.when(k == 0)
            def _():
                acc_ref[...] = part

            @pl.when(k > 0)
            def _():
                acc_ref[...] += part

            @pl.when(k == nk - 1)
            def _():
                epi(acc_ref[...], ex, out_refs, i)

    in_specs = [pl.BlockSpec(*a_spec), pl.BlockSpec(*b_spec)] + [pl.BlockSpec(bs, im) for _, bs, im in extras]
    in_specs += [ANY] * len(afters)
    out_specs = [pl.BlockSpec(bs, im) for _, _, bs, im in outs]
    out_shape = [SDS(s, dt) for s, dt, _, _ in outs]
    vm = 2 * (_nbytes(a_spec[0], a.dtype) + _nbytes(b_spec[0], b.dtype))
    vm += 2 * sum(_nbytes(bs, arr.dtype) for arr, bs, _ in extras)
    vm += 2 * sum(_nbytes(bs, dt) for _, dt, bs, _ in outs)
    vm += 6 * _nbytes(acc_shape, F32)
    scratch = [pltpu.VMEM(acc_shape, F32)] if nk > 1 else []
    return pl.pallas_call(
        body, name=name, grid=grid, in_specs=in_specs, out_specs=out_specs, out_shape=out_shape,
        scratch_shapes=scratch,
        compiler_params=_cparams(vm, ("arbitrary", "arbitrary", "arbitrary")),
    )(a, b, *[arr for arr, _, _ in extras], *afters)


NN =((1,), (0,))
NT = ((1,), (1,))
TN = ((0,), (0,))


def _row_acc(out_ref, val, i):
    @pl.when(i == 0)
    def _():
        out_ref[...] = val

    @pl.when(i > 0)
    def _():
        out_ref[...] += val


def _rms_bwd(xn, r, dxn):
    return r * (dxn - xn * jnp.mean(dxn * xn, axis=-1, keepdims=True))


def norm1(x, g):
    tm = 256

    def body(x_ref, g_ref, h_ref):
        xv = x_ref[...]
        r = lax.rsqrt(jnp.mean(xv * xv, axis=-1, keepdims=True) + EPS)
        h_ref[...] = (xv * r * g_ref[...]).astype(BF16)

    return pl.pallas_call(
        body, name="norm1", grid=(T // tm,),
        in_specs=[pl.BlockSpec((tm, D), lambda i: (i, 0)), pl.BlockSpec((1, D), lambda i: (0, 0))],
        out_specs=pl.BlockSpec((tm, D), lambda i: (i, 0)), out_shape=SDS((T, D), BF16),
        compiler_params=_cparams(32 * 1024 * 1024, ("arbitrary",)),
    )(x, g)


def mm_in(h1, wcat):
    tm, tn = 1024, 1280

    def epi(acc, ex, outs, i):
        outs[0][...] = acc.astype(BF16)

    return matmul("mm_in", h1, wcat, a_spec=((tm, D), lambda j, i, k: (i, 0)), b_spec=((D, tn), lambda j, i, k: (0, j)),
                  cdims=NN, grid=(NCAT // tn, T // tm, 1), acc_shape=(tm, tn),
                  outs=[((T, NCAT), BF16, (tm, tn), lambda j, i, k: (i, j))], epi=epi)[0]


def _window_sum(x, w, up):
    n = x.shape[0]
    row = lax.broadcasted_iota(jnp.int32, x.shape, 0)
    s, sh = x, 1
    while sh < w:
        if up:
            s = s + jnp.where(row < n - sh, pltpu.roll(s, n - sh, axis=0), 0.0)
        else:
            s = s + jnp.where(row >= sh, pltpu.roll(s, sh, axis=0), 0.0)
        sh *= 2
    return s


def _inv_count(shape, w):
    row = lax.broadcasted_iota(jnp.int32, shape, 0)
    return 1.0 / jnp.minimum(row + 1, w).astype(F32)


def pool_fwd(pcat, pw):
    def body(u_ref, pw_ref, d_ref, y_ref):
        for gi, w in enumerate(POOL_WINDOWS):
            ug = u_ref[:, gi * PG:(gi + 1) * PG].astype(F32)
            dg = _window_sum(ug, w, False) * _inv_count(ug.shape, w) - ug
            db = dg.astype(BF16)
            d_ref[:, gi * PG:(gi + 1) * PG] = db
            y_ref[:, gi * PO:(gi + 1) * PO] = jnp.dot(db, pw_ref[gi], preferred_element_type=F32).astype(BF16)

    return pl.pallas_call(
        body, name="pool_fwd", grid=(1,),
        in_specs=[pl.BlockSpec((T, 4 * PG), lambda i: (0, OU // (4 * PG))), pl.BlockSpec((4, PG, PO), lambda i: (0, 0, 0))],
        out_specs=[pl.BlockSpec((T, 4 * PG), lambda i: (0, 0)), pl.BlockSpec((T, D), lambda i: (0, 0))],
        out_shape=[SDS((T, 4 * PG), BF16), SDS((T, D), BF16)],
        compiler_params=_cparams(48 * 1024 * 1024, ("arbitrary",)),
    )(pcat, pw)


def pool_bwd(dylin, d, pw):
    def body(dy_ref, d_ref, pw_ref, du_ref, dpw_ref):
        for gi, w in enumerate(POOL_WINDOWS):
            dyl = dy_ref[:, gi * PO:(gi + 1) * PO]
            dd = lax.dot_general(dyl, pw_ref[gi], (NT, ((), ())), preferred_element_type=F32)
            du = _window_sum(dd * _inv_count(dd.shape, w), w, True) - dd
            du_ref[:, gi * PG:(gi + 1) * PG] = du.astype(BF16)
            dpw_ref[gi] = lax.dot_general(d_ref[:, gi * PG:(gi + 1) * PG], dyl, (TN, ((), ())),
                                          preferred_element_type=F32).astype(BF16)

    return pl.pallas_call(
        body, name="pool_bwd", grid=(1,),
        in_specs=[pl.BlockSpec((T, D), lambda i: (0, 0)), pl.BlockSpec((T, 4 * PG), lambda i: (0, 0)),
                  pl.BlockSpec((4, PG, PO), lambda i: (0, 0, 0))],
        out_specs=[pl.BlockSpec((T, 4 * PG), lambda i: (0, 0)), pl.BlockSpec((4, PG, PO), lambda i: (0, 0, 0))],
        out_shape=[SDS((T, 4 * PG), BF16), SDS((4, PG, PO), BF16)],
        compiler_params=_cparams(48 * 1024 * 1024, ("arbitrary",)),
    )(dylin, d, pw)


def _gate_decay(alow, wa, ba):
    a = jnp.dot(alow, wa, preferred_element_type=F32) + ba
    ls = jax.nn.log_sigmoid(a) * (1.0 / 16.0)
    r = lax.broadcasted_iota(jnp.int32, (CHUNK, CHUNK), 0)
    c = lax.broadcasted_iota(jnp.int32, (CHUNK, CHUNK), 1)
    tri = jnp.where(c <= r, 1.0, 0.0).astype(F32)
    cum = jnp.dot(tri, ls, preferred_element_type=F32, precision=lax.Precision.HIGHEST)
    last = cum[CHUNK - 1:CHUNK, :]
    return a, jnp.exp(last - cum), jnp.exp(last)


def gla_fwd(pcat, wa, ba, ng, after=None):
    afters = _as_list(after)

    def body(q_ref, k_ref, v_ref, g_ref, al_ref, wa_ref, ba_ref, ng_ref, *rest):
        og_ref, o_ref, st_ref, s_scr = rest[len(afters):]

        @pl.when(pl.program_id(0) == 0)
        def _():
            s_scr[...] = jnp.zeros_like(s_scr)

        _, e, decay = _gate_decay(al_ref[...], wa_ref[...], ba_ref[...])
        kd = (k_ref[...].astype(F32) * e).astype(BF16)
        qs = (q_ref[...].astype(F32) * (DK ** -0.5)).astype(BF16)
        for h in range(HEADS):
            ck = slice(h * DK, (h + 1) * DK)
            cv = slice(h * DV, (h + 1) * DV)
            s_new = s_scr[h] * decay[:, ck] + lax.dot_general(v_ref[:, cv], kd[:, ck], (TN, ((), ())),
                                                               preferred_element_type=F32)
            s_scr[h] = s_new
            sb = s_new.astype(BF16)
            st_ref[h] = sb
            oh = lax.dot_general(qs[:, ck], sb, (NT, ((), ())), preferred_element_type=F32)
            o_ref[:, cv] = oh.astype(BF16)
            on = oh * lax.rsqrt(jnp.mean(oh * oh, axis=-1, keepdims=True) + EPS) * ng_ref[:, cv]
            gv = g_ref[:, cv].astype(F32)
            og_ref[:, cv] = (on * (gv * _sigmoid(gv))).astype(BF16)

    row = lambda c: (c, 0)
    return pl.pallas_call(
        body, name="gla_fwd", grid=(NCHUNK,),
        in_specs=[pl.BlockSpec((CHUNK, QK), lambda c: (c, OQ // QK)), pl.BlockSpec((CHUNK, QK), lambda c: (c, OKK // QK)),
                  pl.BlockSpec((CHUNK, D), lambda c: (c, OV // D)), pl.BlockSpec((CHUNK, D), lambda c: (c, OG // D)),
                  pl.BlockSpec((CHUNK, APAD), lambda c: (c, OA // APAD)),
                  pl.BlockSpec((APAD, QK), lambda c: (0, 0)), pl.BlockSpec((1, QK), lambda c: (0, 0)),
                  pl.BlockSpec((1, D), lambda c: (0, 0))] + [ANY] * len(afters),
        out_specs=[pl.BlockSpec((CHUNK, D), row), pl.BlockSpec((CHUNK, D), row),
                   pl.BlockSpec((None, HEADS, DV, DK), lambda c: (c, 0, 0, 0))],
        out_shape=[SDS((T, D), BF16), SDS((T, D), BF16), SDS((NCHUNK, HEADS, DV, DK), BF16)],
        scratch_shapes=[pltpu.VMEM((HEADS, DV, DK), F32)],
        compiler_params=_cparams(32 * 1024 * 1024, ("arbitrary",)),
    )(pcat, pcat, pcat, pcat, pcat, wa, ba, ng, *afters)


def gla_bwd(do, pcat, states, wa, ba, after):
    def body(do_ref, q_ref, k_ref, v_ref, al_ref, sc_ref, sp_ref, wa_ref, ba_ref, after_ref,
             dq_ref, dk_ref, dv_ref, dal_ref, dwa_ref, dba_ref, ds_scr):
        i = pl.program_id(0)

        @pl.when(i == 0)
        def _():
            ds_scr[...] = jnp.zeros_like(ds_scr)

        has_prev = jnp.where(i < NCHUNK - 1, 1.0, 0.0).astype(F32)
        a, e, decay = _gate_decay(al_ref[...], wa_ref[...], ba_ref[...])
        kf = k_ref[...].astype(F32)
        kdf = kf * e
        kd = kdf.astype(BF16)
        qs = (q_ref[...].astype(F32) * (DK ** -0.5)).astype(BF16)
        dkd_parts, ddecay_parts = [], []
        for h in range(HEADS):
            ck = slice(h * DK, (h + 1) * DK)
            cv = slice(h * DV, (h + 1) * DV)
            doh = do_ref[:, cv]
            ds = ds_scr[h] + lax.dot_general(doh, qs[:, ck], (TN, ((), ())), preferred_element_type=F32)
            dsb = ds.astype(BF16)
            dq_ref[:, ck] = (jnp.dot(doh, sc_ref[h], preferred_element_type=F32) * (DK ** -0.5)).astype(BF16)
            dkd_parts.append(jnp.dot(v_ref[:, cv], dsb, preferred_element_type=F32))
            dv_ref[:, cv] = lax.dot_general(kd[:, ck], dsb, (NT, ((), ())), preferred_element_type=F32).astype(BF16)
            ddecay_parts.append(jnp.sum(ds * sp_ref[h].astype(F32), axis=0, keepdims=True) * has_prev)
            ds_scr[h] = ds * decay[:, ck]
        dkd = jnp.concatenate(dkd_parts, axis=1)
        ddecay = jnp.concatenate(ddecay_parts, axis=1)
        dk_ref[...] = (dkd * e).astype(BF16)
        dearg = dkd * kdf
        dlast = jnp.sum(dearg, axis=0, keepdims=True) + ddecay * decay
        r = lax.broadcasted_iota(jnp.int32, (CHUNK, CHUNK), 0)
        c = lax.broadcasted_iota(jnp.int32, (CHUNK, CHUNK), 1)
        triu = jnp.where(c >= r, 1.0, 0.0).astype(F32)
        dls = dlast - jnp.dot(triu, dearg, preferred_element_type=F32, precision=lax.Precision.HIGHEST)
        da = dls * (1.0 / 16.0) * (1.0 - _sigmoid(a))
        dab = da.astype(BF16)
        dal_ref[...] = lax.dot_general(dab, wa_ref[...], (NT, ((), ())), preferred_element_type=F32).astype(BF16)
        dwa = lax.dot_general(al_ref[...], dab, (TN, ((), ())), preferred_element_type=F32)
        dba = jnp.sum(da, axis=0, keepdims=True)

        @pl.when(i == 0)
        def _():
            dwa_ref[...] = dwa
            dba_ref[...] = dba

        @pl.when(i > 0)
        def _():
            dwa_ref[...] += dwa
            dba_ref[...] += dba

    rev = lambda i: NCHUNK - 1 - i
    return pl.pallas_call(
        body, name="gla_bwd", grid=(NCHUNK,),
        in_specs=[pl.BlockSpec((CHUNK, D), lambda i: (rev(i), 0)),
                  pl.BlockSpec((CHUNK, QK), lambda i: (rev(i), OQ // QK)), pl.BlockSpec((CHUNK, QK), lambda i: (rev(i), OKK // QK)),
                  pl.BlockSpec((CHUNK, D), lambda i: (rev(i), OV // D)), pl.BlockSpec((CHUNK, APAD), lambda i: (rev(i), OA // APAD)),
                  pl.BlockSpec((None, HEADS, DV, DK), lambda i: (rev(i), 0, 0, 0)),
                  pl.BlockSpec((None, HEADS, DV, DK), lambda i: (jnp.maximum(rev(i) - 1, 0), 0, 0, 0)),
                  pl.BlockSpec((APAD, QK), lambda i: (0, 0)), pl.BlockSpec((1, QK), lambda i: (0, 0)), ANY],
        out_specs=[pl.BlockSpec((CHUNK, QK), lambda i: (rev(i), 0)), pl.BlockSpec((CHUNK, QK), lambda i: (rev(i), 0)),
                   pl.BlockSpec((CHUNK, D), lambda i: (rev(i), 0)), pl.BlockSpec((CHUNK, APAD), lambda i: (rev(i), 0)),
                   pl.BlockSpec((APAD, QK), lambda i: (0, 0)), pl.BlockSpec((1, QK), lambda i: (0, 0))],
        out_shape=[SDS((T, QK), BF16), SDS((T, QK), BF16), SDS((T, D), BF16), SDS((T, APAD), BF16),
                   SDS((APAD, QK), F32), SDS((1, QK), F32)],
        scratch_shapes=[pltpu.VMEM((HEADS, DV, DK), F32)],
        compiler_params=_cparams(32 * 1024 * 1024, ("arbitrary",)),
    )(do, pcat, pcat, pcat, pcat, states, states, wa, ba, after)


TMF = 256
TMW = 512
_rowblk = ((TMF, D), lambda j, i, k: (i, 0))
_vec = ((1, D), lambda j, i, k: (0, 0))


def _full_spec(col):
    return ((TMF, D), lambda j, i, k: (i, col))


TBIG = 1024


def square_matmul(name, a, b, *, a_spec, b_spec, cdims, nk, after=None):
    def epi(acc, ex, outs, i):
        outs[0][...] = acc

    return matmul(name, a, b, a_spec=a_spec, b_spec=b_spec, cdims=cdims, grid=(D // TBIG, T // TBIG, nk),
                  acc_shape=(TBIG, TBIG), outs=[((T, D), F32, (TBIG, TBIG), lambda j, i, k: (i, j))], epi=epi,
                  after=after)[0]


def rowwise(name, y, *, extras, outs, epi):
    ne = len(extras)

    def body(*refs):
        epi(refs[0][...], refs[1:1 + ne], refs[1 + ne:], pl.program_id(1))

    in_specs = [pl.BlockSpec(*_rowblk)] + [pl.BlockSpec(bs, im) for _, bs, im in extras]
    return pl.pallas_call(
        body, name=name, grid=(1, T // TMF, 1), in_specs=in_specs,
        out_specs=[pl.BlockSpec(bs, im) for _, _, bs, im in outs], out_shape=[SDS(s, dt) for s, dt, _, _ in outs],
        compiler_params=_cparams(40 * 1024 * 1024, ("arbitrary", "arbitrary", "arbitrary")),
    )(y, *[arr for arr, _, _ in extras])


def mm_gla_out(og, w, ylin, pcat, pscale):
    def epi(acc, ex, outs, i):
        ylin_ref, lgp_ref, lgg_ref, ps_ref = ex
        for c0 in range(0, D, EPI_COLS):
            cs = slice(c0, c0 + EPI_COLS)
            gp = _sigmoid(lgp_ref[:, cs].astype(F32))
            gg = _sigmoid(lgg_ref[:, cs].astype(F32))
            a = acc[:, cs]
            outs[0][:, cs] = (gp * (ylin_ref[:, cs].astype(F32) * ps_ref[:, cs]) + gg * a).astype(BF16)
            outs[1][:, cs] = a.astype(BF16)

    return matmul("mm_gla_out", og, w, a_spec=_rowblk, b_spec=((D, D), lambda j, i, k: (0, 0)), cdims=NN,
                  grid=(1, T // TMF, 1), acc_shape=(TMF, D),
                  extras=[(ylin, *_rowblk), (pcat, *_full_spec(OGP // D)), (pcat, *_full_spec(OGG // D)), (pscale, *_vec)],
                  outs=[((T, D), BF16, *_rowblk), ((T, D), BF16, *_rowblk)], epi=epi)


def mm_out(mixed, w, x, g2):
    def epi(acc, ex, outs, i):
        x_ref, g_ref = ex
        x2 = x_ref[...] + acc
        r = lax.rsqrt(jnp.mean(x2 * x2, axis=-1, keepdims=True) + EPS)
        outs[0][...] = x2
        outs[1][...] = (x2 * r * g_ref[...]).astype(BF16)

    return matmul("mm_out", mixed, w, a_spec=_rowblk, b_spec=((D, D), lambda j, i, k: (0, 0)), cdims=NN,
                  grid=(1, T // TMF, 1), acc_shape=(TMF, D), extras=[(x, *_rowblk), (g2, *_vec)],
                  outs=[((T, D), F32, *_rowblk), ((T, D), BF16, *_rowblk)], epi=epi)


def mm_up(h2, wup):
    def epi(acc, ex, outs, i):
        r = jnp.maximum(acc, 0.0)
        outs[0][...] = r.astype(BF16)
        outs[1][...] = (r * r).astype(BF16)

    blk = ((TMW, D), lambda j, i, k: (i, j))
    return matmul("mm_up", h2, wup, a_spec=((TMW, D), lambda j, i, k: (i, 0)), b_spec=((None, D, D), lambda j, i, k: (j, 0, 0)),
                  cdims=NN, grid=(NCHIP, T // TMW, 1), acc_shape=(TMW, D),
                  outs=[((T, DFF), BF16, *blk), ((T, DFF), BF16, *blk)], epi=epi)


def mm_down(act, wdown, x2, tgt, gf):
    tk = 2048

    def epi(acc, ex, outs, i):
        x2_ref, t_ref, g_ref = ex
        dx_ref, dxb_ref, gnf_ref, loss_ref = outs
        x3 = x2_ref[...] + acc
        r = lax.rsqrt(jnp.mean(x3 * x3, axis=-1, keepdims=True) + EPS)
        xn = x3 * r
        err = xn * g_ref[...] - t_ref[...]
        lsum = 0.5 * jnp.sum(jnp.mean(err * err, axis=-1, keepdims=True), axis=0, keepdims=True)
        dy = err * (1.0 / D)
        _row_acc(gnf_ref, jnp.sum(dy * xn, axis=0, keepdims=True), i)
        _row_acc(loss_ref, jnp.broadcast_to(lsum, (1, 128)), i)
        dx3 = _rms_bwd(xn, r, dy * g_ref[...])
        dx_ref[...] = dx3
        dxb_ref[...] = dx3.astype(BF16)

    y = square_matmul("mm_down", act, wdown, a_spec=((TBIG, tk), lambda j, i, k: (i, k)),
                      b_spec=((tk, TBIG), lambda j, i, k: (k, j)), cdims=NN, nk=DFF // tk)
    return rowwise("rows_final", y, extras=[(x2, *_rowblk), (tgt, *_rowblk), (gf, *_vec)],
                   outs=[((T, D), F32, *_rowblk), ((T, D), BF16, *_rowblk), ((1, D), F32, *_vec),
                         ((1, 128), F32, (1, 128), lambda j, i, k: (0, 0))], epi=epi)


def mm_dact(dx3b, wdown, rup, after=None):
    def epi(acc, ex, outs, i):
        outs[0][...] = (acc * 2.0 * ex[0][...].astype(F32)).astype(BF16)

    blk = ((TMW, D), lambda j, i, k: (i, j))
    return matmul("mm_dact", dx3b, wdown, a_spec=((TMW, D), lambda j, i, k: (i, 0)), b_spec=((D, D), lambda j, i, k: (j, 0)),
                  cdims=NT, grid=(DFF // D, T // TMW, 1), acc_shape=(TMW, D), extras=[(rup, *blk)],
                  outs=[((T, DFF), BF16, *blk)], epi=epi, after=after)[0]


def mm_wgrad(name, a, b, m, n, out_shape, out_block, out_map, tm, tn, after=None):
    def epi(acc, ex, outs, i):
        outs[0][...] = acc.astype(BF16)

    return matmul(name, a, b, a_spec=((T, tm), lambda j, i, k: (0, i)), b_spec=((T, tn), lambda j, i, k: (0, j)),
                  cdims=TN, grid=(n // tn, m // tm, 1), acc_shape=(tm, tn),
                  outs=[(out_shape, BF16, out_block, out_map)], epi=epi, after=after)[0]


def mm_dh2(dup, wup, x2, dx3, g2, after=None):
    def epi(acc, ex, outs, i):
        x2_ref, dx3_ref, g_ref = ex
        x2 = x2_ref[...]
        r = lax.rsqrt(jnp.mean(x2 * x2, axis=-1, keepdims=True) + EPS)
        xn = x2 * r
        _row_acc(outs[2], jnp.sum(acc * xn, axis=0, keepdims=True), i)
        dx2 = dx3_ref[...] + _rms_bwd(xn, r, acc * g_ref[...])
        outs[0][...] = dx2
        outs[1][...] = dx2.astype(BF16)

    y = square_matmul("mm_dh2", dup, wup, a_spec=((TBIG, D), lambda j, i, k: (i, k)),
                      b_spec=((None, TBIG, D), lambda j, i, k: (k, j, 0)), cdims=NT, nk=NCHIP, after=after)
    return rowwise("rows_dh2", y, extras=[(x2, *_rowblk), (dx3, *_rowblk), (g2, *_vec)],
                   outs=[((T, D), F32, *_rowblk), ((T, D), BF16, *_rowblk), ((1, D), F32, *_vec)], epi=epi)


def mm_dmixed(dx2b, wout, pcat, ylin, ygla, pscale, after=None):
    def epi(acc, ex, outs, i):
        lgp_ref, lgg_ref, ylin_ref, ygla_ref, ps_ref = ex
        dps = []
        for c0 in range(0, D, EPI_COLS):
            cs = slice(c0, c0 + EPI_COLS)
            gp = _sigmoid(lgp_ref[:, cs].astype(F32))
            gg = _sigmoid(lgg_ref[:, cs].astype(F32))
            yl = ylin_ref[:, cs].astype(F32)
            ps = ps_ref[:, cs]
            a = acc[:, cs]
            agp = a * gp
            outs[0][:, cs] = (agp * ps).astype(BF16)
            outs[1][:, cs] = (a * gg).astype(BF16)
            outs[2][:, cs] = (agp * (yl * ps) * (1.0 - gp)).astype(BF16)
            outs[3][:, cs] = (a * ygla_ref[:, cs].astype(F32) * gg * (1.0 - gg)).astype(BF16)
            dps.append(jnp.sum(agp * yl, axis=0, keepdims=True))
        _row_acc(outs[4], jnp.concatenate(dps, axis=1), i)

    return matmul("mm_dmixed", dx2b, wout, a_spec=_rowblk, b_spec=((D, D), lambda j, i, k: (0, 0)), cdims=NT,
                  grid=(1, T // TMF, 1), acc_shape=(TMF, D),
                  extras=[(pcat, *_full_spec(OGP // D)), (pcat, *_full_spec(OGG // D)), (ylin, *_rowblk), (ygla, *_rowblk),
                          (pscale, *_vec)],
                  outs=[((T, D), BF16, *_rowblk)] * 4 + [((1, D), F32, *_vec)], epi=epi, after=after)


def mm_dog(dygla, wgo, o, pcat, ng, after=None):
    def epi(acc, ex, outs, i):
        o_ref, g_ref, ng_ref = ex
        do_ref, dg_ref, gng_ref = outs
        gparts = []
        for h in range(HEADS):
            cv = slice(h * DV, (h + 1) * DV)
            oh = o_ref[:, cv].astype(F32)
            r = lax.rsqrt(jnp.mean(oh * oh, axis=-1, keepdims=True) + EPS)
            on = oh * r
            gv = g_ref[:, cv].astype(F32)
            sg = _sigmoid(gv)
            a = acc[:, cv]
            dgain = a * (gv * sg)
            gparts.append(jnp.sum(dgain * on, axis=0, keepdims=True))
            ngh = ng_ref[:, cv]
            do_ref[:, cv] = _rms_bwd(on, r, dgain * ngh).astype(BF16)
            dg_ref[:, cv] = (a * (on * ngh) * (sg * (1.0 + gv * (1.0 - sg)))).astype(BF16)
        _row_acc(gng_ref, jnp.concatenate(gparts, axis=1), i)

    return matmul("mm_dog", dygla, wgo, a_spec=_rowblk, b_spec=((D, D), lambda j, i, k: (0, 0)), cdims=NT,
                  grid=(1, T // TMF, 1), acc_shape=(TMF, D),
                  extras=[(o, *_rowblk), (pcat, *_full_spec(OG // D)), (ng, *_vec)],
                  outs=[((T, D), BF16, *_rowblk), ((T, D), BF16, *_rowblk), ((1, D), F32, *_vec)], epi=epi, after=after)


def mm_dh1(dpcat, wcat, x, dx2, g1, after=None):
    tk = 2304

    def epi(acc, ex, outs, i):
        x_ref, dx2_ref, g_ref = ex
        xv = x_ref[...]
        r = lax.rsqrt(jnp.mean(xv * xv, axis=-1, keepdims=True) + EPS)
        xn = xv * r
        _row_acc(outs[1], jnp.sum(acc * xn, axis=0, keepdims=True), i)
        outs[0][...] = dx2_ref[...] + _rms_bwd(xn, r, acc * g_ref[...])

    y = square_matmul("mm_dh1", dpcat, wcat, a_spec=((TBIG, tk), lambda j, i, k: (i, k)),
                      b_spec=((TBIG, tk), lambda j, i, k: (j, k)), cdims=NT, nk=NCAT // tk, after=after)
    return rowwise("rows_dh1", y, extras=[(x, *_rowblk), (dx2, *_rowblk), (g1, *_vec)],
                   outs=[((T, D), F32, *_rowblk), ((1, D), F32, *_vec)], epi=epi)


def _tile_rows(rows, cols, n_arrays):
    tm = rows
    while tm % 32 == 0 and 2 * n_arrays * tm * cols * 4 > 24 * 1024 * 1024:
        tm //= 2
    return tm


def add_pairs(name, parts, theirs, core):
    _, _, r, c = parts.shape
    tm = _tile_rows(r, c, 3)

    def body(core_ref, a_ref, b_ref, o_ref):
        o_ref[...] = (a_ref[...].astype(F32) + b_ref[...].astype(F32)).astype(BF16)

    spec = pl.BlockSpec((None, tm, c), lambda j, i, core_ref: (j, i, 0))
    grid_spec = pltpu.PrefetchScalarGridSpec(
        num_scalar_prefetch=1, grid=(NCHIP, r // tm),
        in_specs=[pl.BlockSpec((None, None, tm, c), lambda j, i, core_ref: (core_ref[0], j, i, 0)), spec], out_specs=spec)
    return pl.pallas_call(body, name=name, grid_spec=grid_spec, out_shape=SDS((NCHIP, r, c), BF16),
                          compiler_params=_cparams(40 * 1024 * 1024, ("arbitrary", "arbitrary")))(core, parts, theirs)


def sum_chips(name, sums, landed, chip):
    _, r, c = sums.shape
    tm = _tile_rows(r, c, 4)

    def body(chip_ref, own_ref, l_ref, o_ref):
        s = own_ref[...].astype(F32)
        for t in range(NCHIP - 1):
            s = s + l_ref[t].astype(F32)
        o_ref[...] = s

    grid_spec = pltpu.PrefetchScalarGridSpec(
        num_scalar_prefetch=1, grid=(r // tm,),
        in_specs=[pl.BlockSpec((None, tm, c), lambda i, chip_ref: (chip_ref[0], i, 0)),
                  pl.BlockSpec((NCHIP - 1, tm, c), lambda i, chip_ref: (0, i, 0))],
        out_specs=pl.BlockSpec((tm, c), lambda i, chip_ref: (i, 0)))
    return pl.pallas_call(body, name=name, grid_spec=grid_spec, out_shape=SDS((r, c), F32),
                          compiler_params=_cparams(40 * 1024 * 1024, ("arbitrary",)))(chip, sums, landed)


def _adamw_math(wv, gv, mv, vv):
    mn = ADAM_B1 * mv + (1.0 - ADAM_B1) * gv
    vn = ADAM_B2 * vv + (1.0 - ADAM_B2) * (gv * gv)
    mh = mn / (1.0 - ADAM_B1 ** ADAM_STEP)
    vh = vn / (1.0 - ADAM_B2 ** ADAM_STEP)
    return -ADAM_LR * (mh / (jnp.sqrt(vh) + ADAM_EPS) + ADAM_WD * wv), mn, vn


def adamw(name, w, g, m, v):
    def body(w_ref, g_ref, m_ref, v_ref, go_ref, d_ref, mo_ref, vo_ref):
        gv = g_ref[...]
        go_ref[...] = gv
        d_ref[...], mo_ref[...], vo_ref[...] = _adamw_math(w_ref[...], gv, m_ref[...], v_ref[...])

    return pl.pallas_call(body, name=name, out_shape=[SDS(w.shape, F32)] * 4)(w, g, m, v)


def adamw_halves(name, w, g_own, g_sib, m, v, core):
    _, r, c = w.shape
    tm = _tile_rows(r, c, 10)

    def body(core_ref, w_ref, go_ref, gs_ref, m_ref, v_ref, g_out, d_out, m_out, v_out):
        gv = jnp.where(pl.program_id(0) == core_ref[0], go_ref[...], gs_ref[...])
        g_out[...] = gv
        d_out[...], m_out[...], v_out[...] = _adamw_math(w_ref[...], gv, m_ref[...], v_ref[...])

    full = pl.BlockSpec((None, tm, c), lambda h, i, core_ref: (h, i, 0))
    own = pl.BlockSpec((tm, c), lambda h, i, core_ref: (jnp.where(h == core_ref[0], i, 0), 0))
    sib = pl.BlockSpec((tm, c), lambda h, i, core_ref: (jnp.where(h == core_ref[0], 0, i), 0))
    grid_spec = pltpu.PrefetchScalarGridSpec(num_scalar_prefetch=1, grid=(2, r // tm),
                                             in_specs=[full, own, sib, full, full], out_specs=[full] * 4)
    return pl.pallas_call(body, name=name, grid_spec=grid_spec, out_shape=[SDS(w.shape, F32)] * 4,
                          compiler_params=_cparams(48 * 1024 * 1024, ("arbitrary", "arbitrary")))(core, w, g_own, g_sib, m, v)


def cast_bf16(name, w):
    _, r, c = w.shape
    tm = _tile_rows(r, c, 2)

    def body(w_ref, o_ref):
        o_ref[...] = w_ref[...].astype(BF16)

    spec = pl.BlockSpec((None, tm, c), lambda h, i: (h, i, 0))
    return pl.pallas_call(body, name=name, grid=(2, r // tm), in_specs=[spec], out_specs=spec, out_shape=SDS(w.shape, BF16),
                          compiler_params=_cparams(40 * 1024 * 1024, ("arbitrary", "arbitrary")))(w)


def pack_rows(name, parts, rows, after=None):
    width = parts[0].shape[1]
    n = len(parts)
    afters = _as_list(after)

    def body(*refs):
        out_ref = refs[n + len(afters)]
        out_ref[...] = jnp.zeros_like(out_ref)
        off = 0
        for p in refs[:n]:
            out_ref[off:off + p.shape[0], :] = p[...]
            off += p.shape[0]

    vm = pl.BlockSpec(memory_space=pltpu.VMEM)
    return pl.pallas_call(body, name=name, in_specs=[vm] * n + [ANY] * len(afters), out_specs=vm,
                          out_shape=SDS((rows, width), F32))(*parts, *afters)


def _place():
    x, y, c = lax.axis_index("x"), lax.axis_index("y"), lax.axis_index("c")
    chips = [(1 - x, y), (x, 1 - y), (1 - x, 1 - y)]
    return x, y, c, chips


def _row_split(shape, dtype):
    r, c = shape
    n = 1
    while r % (2 * n) == 0 and (r // (2 * n)) % 16 == 0 and (r // n) * c * jnp.dtype(dtype).itemsize > PIECE_BYTES:
        n *= 2
    return [pl.ds(s * (r // n), r // n) for s in range(n)]


def _pieces(ref):
    *lead, r, c = ref.shape
    split = _row_split((r, c), ref.dtype)
    return [ref.at[(*idx, s)] for idx in itertools.product(*[range(d) for d in lead]) for s in split]


HBM = pl.BlockSpec(memory_space=pltpu.HBM)
SEM = pl.BlockSpec(memory_space=pltpu.SEMAPHORE)
EFFECT = pltpu.SideEffectType.DATAFLOW_SIDE_EFFECTING


def gather_start(name, shards, after=None):
    n = len(shards)
    afters = _as_list(after)

    def body(*refs):
        src, land = refs[:n], refs[n:2 * n]
        send, recv = refs[2 * n + len(afters)], refs[2 * n + len(afters) + 1]
        x, y, c, chips = _place()
        me = 2 * x + y
        for a in range(n):
            for j, (cx, cy) in enumerate(chips[:2]):
                for sp, dp in zip(_pieces(src[a].at[c]), _pieces(land[a].at[me, c])):
                    pltpu.make_async_remote_copy(sp, dp, send.at[2 * a + j], recv.at[2 * a + j],
                                                 device_id=(cx, cy, c), device_id_type=MESH).start()

    lands = [pltpu.with_memory_space_constraint(lax.empty((NCHIP,) + s.shape, s.dtype), pltpu.HBM) for s in shards]
    srcs = [pltpu.with_memory_space_constraint(s, pltpu.HBM) for s in shards]
    outs = pl.pallas_call(
        body, name=name,
        out_shape=(pltpu.SemaphoreType.DMA((2 * n,)), pltpu.SemaphoreType.DMA((2 * n,)),
                   *[pltpu.HBM(s.shape, s.dtype) for s in shards], *[pltpu.HBM(l.shape, l.dtype) for l in lands]),
        in_specs=[HBM] * (2 * n) + [ANY] * len(afters), out_specs=(SEM, SEM, *([HBM] * (2 * n))),
        input_output_aliases={i: 2 + i for i in range(2 * n)},
        compiler_params=pltpu.CompilerParams(has_side_effects=EFFECT),
    )(*srcs, *lands, *afters)
    return outs[0], outs[1], list(outs[2:2 + n]), list(outs[2 + n:2 + 2 * n])


def gather_wait(name, send, recv, shards, lands, after):
    n = len(shards)
    afters = _as_list(after)

    def body(*refs):
        src, land = refs[:n], refs[n:2 * n]
        send_ref, recv_ref = refs[2 * n], refs[2 * n + 1]
        x, y, c, chips = _place()
        for a in range(n):
            for j, (cx, cy) in enumerate(chips[:2]):
                cp = pltpu.make_async_remote_copy(src[a].at[c], land[a].at[2 * cx + cy, c], send_ref.at[2 * a + j],
                                                  recv_ref.at[2 * a + j], device_id=(cx, cy, c), device_id_type=MESH)
                cp.wait_send()
                cp.wait_recv()

    outs = pl.pallas_call(
        body, name=name,
        out_shape=(*[pltpu.HBM(s.shape, s.dtype) for s in shards], *[pltpu.HBM(l.shape, l.dtype) for l in lands]),
        in_specs=[HBM] * (2 * n) + [SEM, SEM] + [ANY] * len(afters), out_specs=[HBM] * (2 * n),
        input_output_aliases={i: i for i in range(2 * n)},
        compiler_params=pltpu.CompilerParams(has_side_effects=EFFECT),
    )(*shards, *lands, send, recv, *afters)
    return list(outs[:n]), list(outs[n:])


def _relay_blocks(land, c, chips):
    (xx, xy), (yx, yy), (dx, dy) = chips
    rows = land.shape[2] // 2
    upper, lower = pl.ds(0, rows), pl.ds(rows, rows)
    return [(land.at[2 * yx + yy, c, lower], land.at[2 * dx + dy, c, lower]),
            (land.at[2 * xx + xy, c, upper], land.at[2 * dx + dy, c, upper])]


def relay_start(name, lands, after=None):
    n = len(lands)
    afters = _as_list(after)

    def body(*refs):
        had, land = refs[:n], refs[n + len(afters) + 2:2 * n + len(afters) + 2]
        send, recv = refs[n + len(afters)], refs[n + len(afters) + 1]
        x, y, c, chips = _place()
        for a in range(n):
            for j, ((sent, _), (dst, _)) in enumerate(zip(_relay_blocks(had[a], c, chips), _relay_blocks(land[a], c, chips))):
                cx, cy = chips[j]
                for sp, dp in zip(_pieces(sent), _pieces(dst)):
                    pltpu.make_async_remote_copy(sp, dp, send.at[2 * a + j], recv.at[2 * a + j],
                                                 device_id=(cx, cy, c), device_id_type=MESH).start()

    outs = pl.pallas_call(
        body, name=name,
        out_shape=(pltpu.SemaphoreType.DMA((2 * n,)), pltpu.SemaphoreType.DMA((2 * n,)),
                   *[pltpu.HBM(l.shape, l.dtype) for l in lands]),
        in_specs=[HBM] * n + [ANY] * len(afters), out_specs=(SEM, SEM, *([HBM] * n)),
        input_output_aliases={i: 2 + i for i in range(n)},
        compiler_params=pltpu.CompilerParams(has_side_effects=EFFECT),
    )(*lands, *afters)
    return outs[0], outs[1], list(outs[2:])


def relay_wait(name, send, recv, lands, after):
    n = len(lands)
    afters = _as_list(after)

    def body(*refs):
        land = refs[:n]
        send_ref, recv_ref = refs[n], refs[n + 1]
        x, y, c, chips = _place()
        for a in range(n):
            for j, (sent, got) in enumerate(_relay_blocks(land[a], c, chips)):
                cx, cy = chips[j]
                cp = pltpu.make_async_remote_copy(sent, got, send_ref.at[2 * a + j], recv_ref.at[2 * a + j],
                                                  device_id=(cx, cy, c), device_id_type=MESH)
                cp.wait_send()
                cp.wait_recv()

    outs = pl.pallas_call(
        body, name=name, out_shape=tuple(pltpu.HBM(l.shape, l.dtype) for l in lands),
        in_specs=[HBM] * n + [SEM, SEM] + [ANY] * len(afters), out_specs=[HBM] * n,
        input_output_aliases={i: i for i in range(n)},
        compiler_params=pltpu.CompilerParams(has_side_effects=EFFECT),
    )(*lands, send, recv, *afters)
    return list(outs)


def forward_halves(name, shards, lands):
    n = len(lands)

    def body(*refs):
        had, buf = refs[:n], refs[n:2 * n]
        send, recv = refs[2 * n:]
        x, y, c, chips = _place()
        sib = (x, y, 1 - c)
        for a in range(n):
            for j, (cx, cy) in enumerate(chips):
                for sp, dp in zip(_pieces(had[a].at[2 * cx + cy, c]), _pieces(buf[a].at[2 * cx + cy, c])):
                    pltpu.make_async_remote_copy(sp, dp, send.at[3 * a + j], recv.at[3 * a + j], device_id=sib, device_id_type=MESH).start()
        for a in range(n):
            for j, (cx, cy) in enumerate(chips):
                pltpu.make_async_remote_copy(had[a].at[2 * cx + cy, c], buf[a].at[2 * cx + cy, 1 - c], send.at[3 * a + j],
                                             recv.at[3 * a + j], device_id=sib, device_id_type=MESH).wait()

    got = pl.pallas_call(
        body, name=name, in_specs=[ANY] * n, out_specs=[ANY] * n, out_shape=[SDS(l.shape, l.dtype) for l in lands],
        input_output_aliases={i: i for i in range(n)},
        scratch_shapes=[pltpu.SemaphoreType.DMA((3 * n,)), pltpu.SemaphoreType.DMA((3 * n,))],
    )(*lands)
    me = 2 * lax.axis_index("x") + lax.axis_index("y")
    return [lax.dynamic_update_index_in_dim(g, s, me, 0) for g, s in zip(got, shards)]


def exchange_start(name, parts):
    n = len(parts)

    def body(*refs):
        src, got = refs[:n], refs[n:2 * n]
        send, recv = refs[2 * n], refs[2 * n + 1]
        token = refs[4 * n + 2]
        x, y, c, _ = _place()
        sib = (x, y, 1 - c)
        for a in range(n):
            for sp, dp in zip(_pieces(src[a].at[1 - c]), _pieces(got[a])):
                pltpu.make_async_remote_copy(sp, dp, send.at[a], recv.at[a], device_id=sib, device_id_type=MESH).start()
        token[...] = jnp.zeros_like(token)

    lands = [pltpu.with_memory_space_constraint(lax.empty(p.shape[1:], p.dtype), pltpu.HBM) for p in parts]
    srcs = [pltpu.with_memory_space_constraint(p, pltpu.HBM) for p in parts]
    outs = pl.pallas_call(
        body, name=name,
        out_shape=(pltpu.SemaphoreType.DMA((n,)), pltpu.SemaphoreType.DMA((n,)),
                   *[pltpu.HBM(p.shape, p.dtype) for p in parts], *[pltpu.HBM(l.shape, l.dtype) for l in lands],
                   SDS((8, 128), F32)),
        in_specs=[HBM] * (2 * n), out_specs=(SEM, SEM, *([HBM] * (2 * n)), pl.BlockSpec(memory_space=pltpu.VMEM)),
        input_output_aliases={i: 2 + i for i in range(2 * n)},
        compiler_params=pltpu.CompilerParams(has_side_effects=EFFECT),
    )(*srcs, *lands)
    return outs[0], outs[1], list(outs[2:2 + n]), list(outs[2 + n:2 + 2 * n]), outs[2 + 2 * n]


def exchange_wait(name, send, recv, parts, lands, after):
    n = len(parts)
    afters = _as_list(after)

    def body(*refs):
        src, got = refs[:n], refs[n:2 * n]
        send_ref, recv_ref = refs[2 * n], refs[2 * n + 1]
        x, y, c, _ = _place()
        sib = (x, y, 1 - c)
        for a in range(n):
            cp = pltpu.make_async_remote_copy(src[a].at[1 - c], got[a], send_ref.at[a], recv_ref.at[a], device_id=sib, device_id_type=MESH)
            cp.wait_send()
            cp.wait_recv()

    outs = pl.pallas_call(
        body, name=name,
        out_shape=(*[pltpu.HBM(p.shape, p.dtype) for p in parts], *[pltpu.HBM(l.shape, l.dtype) for l in lands]),
        in_specs=[HBM] * (2 * n) + [SEM, SEM] + [ANY] * len(afters), out_specs=[HBM] * (2 * n),
        input_output_aliases={i: i for i in range(2 * n)},
        compiler_params=pltpu.CompilerParams(has_side_effects=EFFECT),
    )(*parts, *lands, send, recv, *afters)
    return list(outs[:n]), list(outs[n:])


def scatter_start(name, parts):
    n = len(parts)

    def body(*refs):
        src, land = refs[:n], refs[n:2 * n]
        send, recv = refs[2 * n], refs[2 * n + 1]
        token = refs[4 * n + 2]
        x, y, c, chips = _place()
        for a in range(n):
            for j, (cx, cy) in enumerate(chips):
                for sp, dp in zip(_pieces(src[a].at[2 * cx + cy]), _pieces(land[a].at[j])):
                    pltpu.make_async_remote_copy(sp, dp, send.at[3 * a + j], recv.at[3 * a + j],
                                                 device_id=(cx, cy, c), device_id_type=MESH).start()
        token[...] = jnp.zeros_like(token)

    lands = [pltpu.with_memory_space_constraint(lax.empty((NCHIP - 1,) + p.shape[1:], p.dtype), pltpu.HBM) for p in parts]
    srcs = [pltpu.with_memory_space_constraint(p, pltpu.HBM) for p in parts]
    outs = pl.pallas_call(
        body, name=name,
        out_shape=(pltpu.SemaphoreType.DMA((3 * n,)), pltpu.SemaphoreType.DMA((3 * n,)),
                   *[pltpu.HBM(p.shape, p.dtype) for p in parts], *[pltpu.HBM(l.shape, l.dtype) for l in lands],
                   SDS((8, 128), F32)),
        in_specs=[HBM] * (2 * n), out_specs=(SEM, SEM, *([HBM] * (2 * n)), pl.BlockSpec(memory_space=pltpu.VMEM)),
        input_output_aliases={i: 2 + i for i in range(2 * n)},
        compiler_params=pltpu.CompilerParams(has_side_effects=EFFECT),
    )(*srcs, *lands)
    return outs[0], outs[1], list(outs[2:2 + n]), list(outs[2 + n:2 + 2 * n]), outs[2 + 2 * n]


def scatter_wait(name, send, recv, parts, lands, after):
    n = len(parts)
    afters = _as_list(after)

    def body(*refs):
        src, land = refs[:n], refs[n:2 * n]
        send_ref, recv_ref = refs[2 * n], refs[2 * n + 1]
        x, y, c, chips = _place()
        for a in range(n):
            for j, (cx, cy) in enumerate(chips):
                cp = pltpu.make_async_remote_copy(src[a].at[2 * cx + cy], land[a].at[j], send_ref.at[3 * a + j], recv_ref.at[3 * a + j],
                                                  device_id=(cx, cy, c), device_id_type=MESH)
                cp.wait_send()
                cp.wait_recv()

    outs = pl.pallas_call(
        body, name=name,
        out_shape=(*[pltpu.HBM(p.shape, p.dtype) for p in parts], *[pltpu.HBM(l.shape, l.dtype) for l in lands]),
        in_specs=[HBM] * (2 * n) + [SEM, SEM] + [ANY] * len(afters), out_specs=[HBM] * (2 * n),
        input_output_aliases={i: i for i in range(2 * n)},
        compiler_params=pltpu.CompilerParams(has_side_effects=EFFECT),
    )(*parts, *lands, send, recv, *afters)
    return list(outs[:n]), list(outs[n:])


def join_start(name, halves):
    n = len(halves)

    def body(*refs):
        src, dst = refs[:n], refs[n:2 * n]
        send, recv = refs[2 * n], refs[2 * n + 1]
        token = refs[4 * n + 2]
        x, y, c, _ = _place()
        sib = (x, y, 1 - c)
        for a in range(n):
            for sp, dp in zip(_pieces(src[a]), _pieces(dst[a])):
                pltpu.make_async_remote_copy(sp, dp, send.at[a], recv.at[a], device_id=sib, device_id_type=MESH).start()
        token[...] = jnp.zeros_like(token)

    lands = [pltpu.with_memory_space_constraint(lax.empty(h.shape, h.dtype), pltpu.HBM) for h in halves]
    srcs = [pltpu.with_memory_space_constraint(h, pltpu.HBM) for h in halves]
    outs = pl.pallas_call(
        body, name=name,
        out_shape=(pltpu.SemaphoreType.DMA((n,)), pltpu.SemaphoreType.DMA((n,)),
                   *[pltpu.HBM(h.shape, h.dtype) for h in halves], *[pltpu.HBM(l.shape, l.dtype) for l in lands],
                   SDS((8, 128), F32)),
        in_specs=[HBM] * (2 * n), out_specs=(SEM, SEM, *([HBM] * (2 * n)), pl.BlockSpec(memory_space=pltpu.VMEM)),
        input_output_aliases={i: 2 + i for i in range(2 * n)},
        compiler_params=pltpu.CompilerParams(has_side_effects=EFFECT),
    )(*srcs, *lands)
    return outs[0], outs[1], list(outs[2:2 + n]), list(outs[2 + n:2 + 2 * n]), outs[2 + 2 * n]


def join_wait(name, send, recv, halves, lands, after):
    n = len(halves)
    afters = _as_list(after)

    def body(*refs):
        src, dst = refs[:n], refs[n:2 * n]
        send_ref, recv_ref = refs[2 * n], refs[2 * n + 1]
        x, y, c, _ = _place()
        sib = (x, y, 1 - c)
        for a in range(n):
            cp = pltpu.make_async_remote_copy(src[a], dst[a], send_ref.at[a], recv_ref.at[a], device_id=sib, device_id_type=MESH)
            cp.wait_send()
            cp.wait_recv()

    outs = pl.pallas_call(
        body, name=name,
        out_shape=(*[pltpu.HBM(h.shape, h.dtype) for h in halves], *[pltpu.HBM(l.shape, l.dtype) for l in lands]),
        in_specs=[HBM] * (2 * n) + [SEM, SEM] + [ANY] * len(afters), out_specs=[HBM] * (2 * n),
        input_output_aliases={i: i for i in range(2 * n)},
        compiler_params=pltpu.CompilerParams(has_side_effects=EFFECT),
    )(*halves, *lands, send, recv, *afters)
    return list(outs[:n]), list(outs[n:])


def gather_small(name, xs, reduce, after=None):
    m, ncol = xs.shape
    afters = _as_list(after)

    def body(x_ref, *rest):
        out_ref, all_ref, send, recv, lsem = rest[len(afters):]
        x, y, c, chips = _place()
        me, sib = (x, y, c), (x, y, 1 - c)

        def rows(px, py, pc):
            return all_ref.at[pl.ds((4 * px + 2 * py + pc) * m, m), :]

        def copy(k, block, to, src=None):
            return pltpu.make_async_remote_copy(rows(*block) if src is None else src, rows(*block), send.at[k], recv.at[k],
                                                device_id=to, device_id_type=MESH)

        mine = pltpu.make_async_copy(x_ref, rows(*me), lsem)
        mine.start()
        first = [copy(0, me, sib, src=x_ref)] + [copy(1 + j, me, (*chip, c), src=x_ref) for j, chip in enumerate(chips)]
        for cp in first:
            cp.start()
        passed = [copy(4 + j, (*chip, c), sib) for j, chip in enumerate(chips)]
        for j, chip in enumerate(chips):
            copy(1 + j, (*chip, c), me).wait_recv()
            passed[j].start()
        copy(0, sib, me).wait_recv()
        for j, chip in enumerate(chips):
            copy(4 + j, (*chip, 1 - c), me).wait_recv()
        for cp in first + passed:
            cp.wait_send()
        mine.wait()
        if reduce:
            s = all_ref[0:m, :]
            for dev in range(1, 8):
                s = s + all_ref[dev * m:(dev + 1) * m, :]
            out_ref[...] = s
        else:
            out_ref[...] = all_ref[...]

    vm = pl.BlockSpec(memory_space=pltpu.VMEM)
    return pl.pallas_call(
        body, name=name, in_specs=[vm] + [ANY] * len(afters), out_specs=vm,
        out_shape=SDS((m, ncol) if reduce else (8 * m, ncol), F32),
        scratch_shapes=[pltpu.VMEM((8 * m, ncol), F32), pltpu.SemaphoreType.DMA((7,)), pltpu.SemaphoreType.DMA((7,)),
                        pltpu.SemaphoreType.DMA],
    )(xs, *afters)


RELAYOUT_ROWS = 128


def weights_to_cat(g_in, after=None):
    tm = RELAYOUT_ROWS
    afters = _as_list(after)

    def body(g_ref, *rest):
        o_ref = rest[len(afters)]
        nat = jnp.concatenate([g_ref[j] for j in range(NCHIP)], axis=1)
        pad = jnp.zeros((tm, NCAT - OA - 16), BF16)
        o_ref[...] = jnp.concatenate([nat[:, 3072:7168], nat[:, 7184:11280], nat[:, 0:3072], nat[:, 7168:7184], pad], axis=1)

    return pl.pallas_call(
        body, name="weights_to_cat", grid=(D // tm,),
        in_specs=[pl.BlockSpec((NCHIP, tm, IN_SHARD), lambda i: (0, i, 0))] + [ANY] * len(afters),
        out_specs=pl.BlockSpec((tm, NCAT), lambda i: (i, 0)), out_shape=SDS((D, NCAT), BF16),
        compiler_params=_cparams(40 * 1024 * 1024, ("arbitrary",)),
    )(g_in, *afters)


def grads_from_cat(gw_cat):
    tm = RELAYOUT_ROWS
    nb = (D // 2) // tm

    def body(c_ref, o_ref):
        cat = c_ref[...]
        nat = jnp.concatenate([cat[:, OU:OA], cat[:, OV:OGP], cat[:, OA:OA + 16], cat[:, OGP:OU]], axis=1)
        for j in range(NCHIP):
            o_ref[j] = nat[:, j * IN_SHARD:(j + 1) * IN_SHARD]

    return pl.pallas_call(
        body, name="grads_from_cat", grid=(D // tm,), in_specs=[pl.BlockSpec((tm, NCAT), lambda i: (i, 0))],
        out_specs=pl.BlockSpec((None, NCHIP, tm, IN_SHARD), lambda i: (i // nb, 0, i % nb, 0)),
        out_shape=SDS((2, NCHIP, D // 2, IN_SHARD), BF16), compiler_params=_cparams(40 * 1024 * 1024, ("arbitrary",)),
    )(gw_cat)


def _pad_rows(a, rows):
    return jnp.concatenate([a, jnp.zeros((rows - a.shape[0],) + a.shape[1:], a.dtype)], axis=0)


def local_step(x2d, tgt, gf, g1, pool_scale, wa_pad, b_alpha, ng, g2, get_w, on_grad=None, on_settle=None, tick=None):
    emit = on_grad if on_grad is not None else (lambda group, grads: None)
    settle = on_settle if on_settle is not None else (lambda group, after: None)
    h1 = norm1(x2d, g1)
    wcat, pw = get_w("in", h1)
    pcat = mm_in(h1, wcat)
    dpool, ylin = pool_fwd(pcat, pw)
    pinned = tick("pool", ylin) if tick is not None else None
    og, o, states = gla_fwd(pcat, wa_pad, b_alpha, ng, pinned)
    w_go, w_o = get_w("mid", og)
    mixed, ygla = mm_gla_out(og, w_go, ylin, pcat, pool_scale)
    x2, h2 = mm_out(mixed, w_o, x2d, g2)
    w_up = get_w("up", h2)
    rup, act = mm_up(h2, w_up)
    w_dn = get_w("down", act)
    dx3, dx3b, g_nf, loss_row = mm_down(act, w_dn, x2, tgt, gf)

    gw_down = mm_wgrad("mm_dw_down", act, dx3b, DFF, D, (2, NCHIP, D // 2, D), (None, None, 512, D),
                       lambda j, i, k: ((i // 2) % 2, i // 4, i % 2, 0), 512, D)
    token = emit("down", {"down": gw_down})
    dup = mm_dact(dx3b, w_dn, rup, after=token)
    token = settle("down", dup)
    dx2, dx2b, g_mlp = mm_dh2(dup, w_up, x2, dx3, g2, after=token)
    gw_up = mm_wgrad("mm_dw_up", h2, dup, D, DFF, (2, NCHIP, D // 2, D), (None, None, 512, D),
                     lambda j, i, k: (i // 2, j, i % 2, 0), 512, D)
    token = emit("up", {"up": gw_up})
    dylin, dygla, dlgp, dlgg, g_ps = mm_dmixed(dx2b, w_o, pcat, ylin, ygla, pool_scale, after=token)
    token = settle("up", dylin)
    gw_out = mm_wgrad("mm_dw_out", mixed, dx2b, D, D, (2, NCHIP, 256, D), (None, None, 256, D),
                      lambda j, i, k: (i % 2, i // 2, 0, 0), 256, D)
    do, dg, g_ng = mm_dog(dygla, w_go, o, pcat, ng, after=token)
    gw_go = mm_wgrad("mm_dw_gla_out", og, dygla, D, D, (2, NCHIP, 256, D), (None, None, 256, D),
                     lambda j, i, k: (i % 2, i // 2, 0, 0), 256, D)
    token = emit("mix", {"out": gw_out, "gla_out": gw_go})
    dq, dk, dv, dalow, g_wa, g_ba = gla_bwd(do, pcat, states, wa_pad, b_alpha, b_alpha if token is None else token)
    token = settle("mix", dq)
    du, dpw = pool_bwd(dylin, dpool, pw)
    dpcat = jnp.concatenate([dv, dg, dlgp, dlgg, du, dq, dk, dalow, jnp.zeros((T, NCAT - OA - APAD), BF16)], axis=1)
    gw_cat = mm_wgrad("mm_dw_in", h1, dpcat, D, NCAT, (D, NCAT), (1024, 1280), lambda j, i, k: (i, j), 1024, 1280, after=token)
    token = settle("in", emit("in", {"in_cat": gw_cat, "pool": dpw}))
    grad_x, g_mix = mm_dh1(dpcat, wcat, x2d, dx2, g1, after=token)
    return (loss_row[0, 0], grad_x, g_mix, g_ps, g_mlp, g_nf, g_ng, g_ba, g_wa, token,
            gw_cat, dpw, gw_go, gw_out, gw_up, gw_down)


def kernel(x, norm_mix_g, w_in, pool_w, pool_scale, w_alpha, b_alpha, gla_norm_g, w_gla_out, w_out, norm_mlp_g, w_mlp_up, w_mlp_down, norm_final_g, loss_target, m_norm_mix_g, m_w_in, m_pool_w, m_pool_scale, m_w_alpha, m_b_alpha, m_gla_norm_g, m_w_gla_out, m_w_out, m_norm_mlp_g, m_w_mlp_up, m_w_mlp_down, m_norm_final_g, v_norm_mix_g, v_w_in, v_pool_w, v_pool_scale, v_w_alpha, v_b_alpha, v_gla_norm_g, v_w_gla_out, v_w_out, v_norm_mlp_g, v_w_mlp_up, v_w_mlp_down, v_norm_final_g):
    chip = 2 * lax.axis_index("x") + lax.axis_index("y")
    chip_i = chip.astype(jnp.int32).reshape(1)
    core_i = lax.axis_index("c").astype(jnp.int32).reshape(1)
    tgt = loss_target.reshape(T, D)
    gf = norm_final_g.reshape(1, D)

    def halves(w2d):
        r, c = w2d.shape
        return w2d.astype(BF16).reshape(2, r // 2, c)

    pool_shard = pool_w.reshape(4 * PG, PO // NCHIP)
    w_in_r = w_in.reshape(2, D // 2, IN_SHARD)
    sent = {"in": [cast_bf16("cast_w_in", w_in_r), halves(pool_shard)]}
    flight = {}

    def start(group, after=None):
        flight[group] = gather_start("gather_start_" + group, sent[group], after)

    def relay(group, after):
        send, recv, shards, lands = flight[group]
        shards, lands = gather_wait("gather_wait_" + group, send, recv, shards, lands, after)
        send, recv, lands = relay_start("relay_start_" + group, lands)
        flight[group] = (send, recv, shards, lands)

    def fetch(group, after, then=None):
        send, recv, shards, lands = flight[group]
        lands = relay_wait("relay_wait_" + group, send, recv, lands, after)
        if then is not None:
            then(lands[0])
        return forward_halves("forward_" + group, shards, lands)

    start("in")
    m_in_f, v_in_f, w_go_f, w_o_f, w_up_f, w_dn_f, x_f, wal_f, gng_f = lax.optimization_barrier(
        (m_w_in, v_w_in, w_gla_out, w_out, w_mlp_up, w_mlp_down, x, w_alpha, gla_norm_g, flight["in"][2][0]))[:9]
    m_in_r, v_in_r = m_in_f.reshape(2, D // 2, IN_SHARD), v_in_f.reshape(2, D // 2, IN_SHARD)
    sent["mid"] = [halves(w_go_f[0]), halves(w_o_f[0])]
    relay("in", [m_in_r, v_in_r, *sent["mid"]])
    w_up_f, w_dn_f, x_f, wal_f, gng_f = lax.optimization_barrier(
        (w_up_f, w_dn_f, x_f, wal_f, gng_f, flight["in"][3][0]))[:5]
    sent["up"], sent["down"] = [halves(w_up_f[0])], [halves(w_dn_f[0])]
    x2d = x_f.reshape(T, D)
    big = [w_in_r, w_go_f[0], w_o_f[0], w_up_f[0], w_dn_f[0], pool_shard]

    def tick(point, after):
        if point == "pool":
            relay("mid", after)
            start("down", flight["mid"][3][0])
            return [flight["mid"][3][0], flight["down"][3][0]]

    def get_w(group, after):
        if group == "in":
            after = [after, *sent["up"], *sent["down"], wa_pad]
        if group == "mid":
            relay("up", after)
            after = flight["up"][3][0]
        if group == "up":
            relay("down", after)
            after = flight["down"][3][0]
        if group == "in":
            def next_groups(landed):
                start("mid", landed)
                start("up", flight["mid"][3][0])

            g_in, g_pool = fetch(group, after, next_groups)
            wcat = weights_to_cat(g_in.reshape(NCHIP, D, IN_SHARD), flight["up"][3][0])
            pw = jnp.concatenate([g_pool[j].reshape(4, PG, PO // NCHIP) for j in range(NCHIP)], axis=2)
            return wcat, pw
        whole = fetch(group, after)
        if group == "mid":
            return whole[0].reshape(D, D), whole[1].reshape(D, D)
        if group == "up":
            return whole[0].reshape(NCHIP, D, D)
        return whole[0].reshape(DFF, D)

    small_w = pack_rows("pack_small_w", [wal_f[0].reshape(4, QK),
                                         jnp.concatenate([gng_f[0].reshape(1, 512), jnp.zeros((1, 512), F32)], axis=1)], 8)
    sw_all = gather_small("gather_small_w", small_w, False).reshape(8, 8, QK)
    wa_full = jnp.concatenate([sw_all[2 * j, 0:4].reshape(16, DK) for j in range(NCHIP)], axis=1)
    ng_full = jnp.concatenate([sw_all[2 * j, 4, 0:512].reshape(HEADS, DV // NCHIP) for j in range(NCHIP)], axis=1)
    wa_pad = _pad_rows(wa_full, APAD).astype(BF16)
    ng = ng_full.reshape(1, D)

    pending = {}
    wmv = {"in": (w_in_r, m_in_r, v_in_r), "gla_out": (big[1], m_w_gla_out, v_w_gla_out), "out": (big[2], m_w_out, v_w_out),
           "up": (big[3], m_w_mlp_up, v_w_mlp_up), "down": (big[4], m_w_mlp_down, v_w_mlp_down), "pool": (big[5], m_pool_w, v_pool_w)}
    big_res = {}

    def reduce_group(group, after):
        nms, send, recv, sums, lands = pending[group]
        sums, lands = scatter_wait("scatter_wait_" + group, send, recv, sums, lands, after)
        reduced = [sum_chips("sum_chips_" + nm, a, b, chip_i) for nm, a, b in zip(nms, sums, lands)]
        send, recv, reduced, lands, token = join_start("join_start_" + group, reduced)
        pending[group] = (nms, send, recv, reduced, lands)
        return token

    def update_group(group, after):
        nms, send, recv, reduced, lands = pending[group]
        reduced, from_sib = join_wait("join_wait_" + group, send, recv, reduced, lands, after)
        for nm, g_own, g_sib in zip(nms, reduced, from_sib):
            w, m, v = wmv[nm]
            shp = (2,) + g_own.shape
            big_res[nm] = adamw_halves("adamw_" + nm, w.reshape(shp), g_own, g_sib, m.reshape(shp), v.reshape(shp), core_i)

    def on_grad(group, grads):
        if group == "in":
            gw_in = grads_from_cat(grads["in_cat"])
            gw_pool = jnp.stack([grads["pool"][:, :, j * 128:(j + 1) * 128].reshape(2, 2 * PG, 128)
                                 for j in range(NCHIP)], axis=1)
            grads = {"in": gw_in, "pool": gw_pool}
        nms, parts = list(grads.keys()), list(grads.values())
        send, recv, parts, got, token = exchange_start("exchange_start_" + group, parts)
        pending[group] = (nms, send, recv, parts, got)
        return token

    def on_settle(group, after):
        if group == "in":
            after = reduce_group("down", after)
        nms, send, recv, parts, got = pending[group]
        parts, got = exchange_wait("exchange_wait_" + group, send, recv, parts, got, after)
        sums = [add_pairs("add_pair_" + nm, a, b, core_i) for nm, a, b in zip(nms, parts, got)]
        send, recv, sums, lands, token = scatter_start("scatter_start_" + group, sums)
        pending[group] = (nms, send, recv, sums, lands)
        if group != "in":
            return token
        token = reduce_group("up", token)
        token = reduce_group("mix", token)
        for earlier in ("down", "up", "mix"):
            update_group(earlier, token)
            token = big_res[pending[earlier][0][-1]][1]
        return [big_res[nm][1] for nm in ("down", "up", "out", "gla_out")]

    (loss_local, grad_x, g_mix, g_ps, g_mlp, g_nf, g_ng, g_ba, g_wa) = local_step(
        x2d, tgt, gf, norm_mix_g, pool_scale, wa_pad, b_alpha, ng, norm_mlp_g, get_w, on_grad, on_settle, tick)[:9]
    loss = lax.psum(loss_local, ("x", "y", "c"))
    join_in_token = reduce_group("in", grad_x)

    ROWS = 16

    def wide(a, n):
        return jnp.concatenate([a.reshape(1, n), jnp.zeros((1, D - n), F32)], axis=1)

    packed = pack_rows("pack_small_g", [g_mix, g_ps, g_mlp, g_nf, g_ng, wide(g_ba, QK), g_wa[0:16].reshape(8, D)], ROWS)
    tot = gather_small("reduce_small_g", packed, True, join_in_token)
    t_wa = lax.dynamic_slice(tot[6:14].reshape(16, QK), (0, chip * DK), (16, DK))
    t_ng = lax.dynamic_slice(tot[4].reshape(HEADS, DV), (0, chip * (DV // NCHIP)), (HEADS, DV // NCHIP))

    def pack_small(nm, mix, ps, mlp, nf, ba, wa, gn, after=None):
        return pack_rows(nm, [mix.reshape(1, D), ps.reshape(1, D), mlp.reshape(1, D), nf.reshape(1, D), wide(ba, QK),
                              wa.reshape(2, D), wide(gn, 512)], ROWS, after)

    update_group("in", tot)
    sg = pack_small("pack_g", tot[0], tot[1], tot[2], tot[3], tot[5, 0:QK], t_wa, t_ng, big_res["in"][3])
    sw = pack_small("pack_w", norm_mix_g, pool_scale, norm_mlp_g, norm_final_g, b_alpha, w_alpha, gla_norm_g)
    sm = pack_small("pack_m", m_norm_mix_g, m_pool_scale, m_norm_mlp_g, m_norm_final_g, m_b_alpha, m_w_alpha, m_gla_norm_g)
    sv = pack_small("pack_v", v_norm_mix_g, v_pool_scale, v_norm_mlp_g, v_norm_final_g, v_b_alpha, v_w_alpha, v_gla_norm_g)
    small_res = adamw("adamw_small", sw, sg, sm, sv)

    def unpack(p):
        return {"norm_mix_g": p[0].reshape(1, D), "pool_scale": p[1].reshape(1, D), "norm_mlp_g": p[2].reshape(1, D),
                "norm_final_g": p[3].reshape(D), "b_alpha": p[4, 0:QK].reshape(1, QK), "w_alpha": p[5:7].reshape(1, 16, DK),
                "gla_norm_g": p[7, 0:512].reshape(1, HEADS, DV // NCHIP)}

    order = ["norm_mix_g", "w_in", "pool_w", "pool_scale", "w_alpha", "b_alpha", "gla_norm_g", "w_gla_out", "w_out",
             "norm_mlp_g", "w_mlp_up", "w_mlp_down", "norm_final_g"]
    big_key = {"w_in": ("in", w_in.shape), "pool_w": ("pool", pool_w.shape), "w_gla_out": ("gla_out", w_gla_out.shape),
               "w_out": ("out", w_out.shape), "w_mlp_up": ("up", w_mlp_up.shape), "w_mlp_down": ("down", w_mlp_down.shape)}
    result = [loss, grad_x.reshape(1, T, D)]
    for kind in range(4):
        small = unpack(small_res[kind])
        for nm in order:
            if nm in big_key:
                key, shp = big_key[nm]
                result.append(big_res[key][kind].reshape(shp))
            else:
                result.append(small[nm])
    return tuple(result)
```

```python
import itertools

import jax
import jax.numpy as jnp
from jax import lax
from jax.experimental import pallas as pl
from jax.experimental.pallas import tpu as pltpu

F32 = jnp.float32
BF16 = jnp.bfloat16
SDS = jax.ShapeDtypeStruct
MESH = pl.DeviceIdType.MESH
ANY = pl.BlockSpec(memory_space=pl.ANY)

T = 2048
D = 2048
DFF = 8192
NCHIP = 4
IN_WIDTH = 11280
IN_SHARD = IN_WIDTH // NCHIP
CHUNK = 64
NCHUNK = T // CHUNK
HEADS = 4
DK = 256
DV = 512
QK = HEADS * DK
EPS = 1e-6
POOL_WINDOWS = (2, 4, 8, 16)
PG = 256
PO = 512

OV, OG, OGP, OGG, OU, OQ, OKK, OA = 0, 2048, 4096, 6144, 8192, 9216, 10240, 11264
NCAT = 11520
APAD = 128

VMEM_CAP = 56 * 1024 * 1024

PIECE_BYTES = 384 * 1024

ADAM_LR, ADAM_B1, ADAM_B2, ADAM_EPS, ADAM_WD, ADAM_STEP = 0.001, 0.9, 0.999, 1e-08, 0.01, 10


def _cparams(vmem_bytes=None, sem=None):
    kw = {}
    if vmem_bytes is not None:
        kw["vmem_limit_bytes"] = int(min(max(vmem_bytes, 32 * 1024 * 1024), VMEM_CAP))
    if sem is not None:
        kw["dimension_semantics"] = sem
    return pltpu.CompilerParams(**kw)


def _nbytes(shape, dtype):
    n = 1
    for s in shape:
        if s is not None:
            n *= s
    return n * jnp.dtype(dtype).itemsize


def _sigmoid(x):
    return 0.5 * jnp.tanh(0.5 * x) + 0.5


EPI_COLS = 512


def _as_list(after):
    if after is None:
        return []
    return list(after) if isinstance(after, (list, tuple)) else [after]


def matmul(name, a, b, *, a_spec, b_spec, cdims, grid, acc_shape, outs, extras=(), epi, after=None):
    nj, ni, nk = grid
    ne, no = len(extras), len(outs)
    afters = _as_list(after)
    first_out = 2 + ne + len(afters)

    def body(*refs):
        a_ref, b_ref = refs[0], refs[1]
        ex = refs[2:2 + ne]
        out_refs = refs[first_out:first_out + no]
        i = pl.program_id(1)
        part = lax.dot_general(a_ref[...], b_ref[...], (cdims, ((), ())), preferred_element_type=F32)
        if nk == 1:
            epi(part, ex, out_refs, i)
        else:
            acc_ref = refs[first_out + no]
            k = pl.program_id(2)

            @pl.when(k == 0)
            def _():
                acc_ref[...] = part

            @pl.when(k > 0)
            def _():
                acc_ref[...] += part

            @pl.when(k == nk - 1)
            def _():
                epi(acc_ref[...], ex, out_refs, i)

    in_specs = [pl.BlockSpec(*a_spec), pl.BlockSpec(*b_spec)] + [pl.BlockSpec(bs, im) for _, bs, im in extras]
    in_specs += [ANY] * len(afters)
    out_specs = [pl.BlockSpec(bs, im) for _, _, bs, im in outs]
    out_shape = [SDS(s, dt) for s, dt, _, _ in outs]
    vm = 2 * (_nbytes(a_spec[0], a.dtype) + _nbytes(b_spec[0], b.dtype))
    vm += 2 * sum(_nbytes(bs, arr.dtype) for arr, bs, _ in extras)
    vm += 2 * sum(_nbytes(bs, dt) for _, dt, bs, _ in outs)
    vm += 6 * _nbytes(acc_shape, F32)
    scratch = [pltpu.VMEM(acc_shape, F32)] if nk > 1 else []
    return pl.pallas_call(
        body, name=name, grid=grid, in_specs=in_specs, out_specs=out_specs, out_shape=out_shape,
        scratch_shapes=scratch,
        compiler_params=_cparams(vm, ("arbitrary", "arbitrary", "arbitrary")),
    )(a, b, *[arr for arr, _, _ in extras], *afters)


NN =((1,), (0,))
NT = ((1,), (1,))
TN = ((0,), (0,))


def _row_acc(out_ref, val, i):
    @pl.when(i == 0)
    def _():
        out_ref[...] = val

    @pl.when(i > 0)
    def _():
        out_ref[...] += val


def _rms_bwd(xn, r, dxn):
    return r * (dxn - xn * jnp.mean(dxn * xn, axis=-1, keepdims=True))


def norm1(x, g):
    tm = 256

    def body(x_ref, g_ref, h_ref):
        xv = x_ref[...]
        r = lax.rsqrt(jnp.mean(xv * xv, axis=-1, keepdims=True) + EPS)
        h_ref[...] = (xv * r * g_ref[...]).astype(BF16)

    return pl.pallas_call(
        body, name="norm1", grid=(T // tm,),
        in_specs=[pl.BlockSpec((tm, D), lambda i: (i, 0)), pl.BlockSpec((1, D), lambda i: (0, 0))],
        out_specs=pl.BlockSpec((tm, D), lambda i: (i, 0)), out_shape=SDS((T, D), BF16),
        compiler_params=_cparams(32 * 1024 * 1024, ("arbitrary",)),
    )(x, g)


def mm_in(h1, wcat):
    tm, tn = 1024, 1280

    def epi(acc, ex, outs, i):
        outs[0][...] = acc.astype(BF16)

    return matmul("mm_in", h1, wcat, a_spec=((tm, D), lambda j, i, k: (i, 0)), b_spec=((D, tn), lambda j, i, k: (0, j)),
                  cdims=NN, grid=(NCAT // tn, T // tm, 1), acc_shape=(tm, tn),
                  outs=[((T, NCAT), BF16, (tm, tn), lambda j, i, k: (i, j))], epi=epi)[0]


def _window_sum(x, w, up):
    n = x.shape[0]
    row = lax.broadcasted_iota(jnp.int32, x.shape, 0)
    s, sh = x, 1
    while sh < w:
        if up:
            s = s + jnp.where(row < n - sh, pltpu.roll(s, n - sh, axis=0), 0.0)
        else:
            s = s + jnp.where(row >= sh, pltpu.roll(s, sh, axis=0), 0.0)
        sh *= 2
    return s


def _inv_count(shape, w):
    row = lax.broadcasted_iota(jnp.int32, shape, 0)
    return 1.0 / jnp.minimum(row + 1, w).astype(F32)


def pool_fwd(pcat, pw):
    def body(u_ref, pw_ref, d_ref, y_ref):
        for gi, w in enumerate(POOL_WINDOWS):
            ug = u_ref[:, gi * PG:(gi + 1) * PG].astype(F32)
            dg = _window_sum(ug, w, False) * _inv_count(ug.shape, w) - ug
            db = dg.astype(BF16)
            d_ref[:, gi * PG:(gi + 1) * PG] = db
            y_ref[:, gi * PO:(gi + 1) * PO] = jnp.dot(db, pw_ref[gi], preferred_element_type=F32).astype(BF16)

    return pl.pallas_call(
        body, name="pool_fwd", grid=(1,),
        in_specs=[pl.BlockSpec((T, 4 * PG), lambda i: (0, OU // (4 * PG))), pl.BlockSpec((4, PG, PO), lambda i: (0, 0, 0))],
        out_specs=[pl.BlockSpec((T, 4 * PG), lambda i: (0, 0)), pl.BlockSpec((T, D), lambda i: (0, 0))],
        out_shape=[SDS((T, 4 * PG), BF16), SDS((T, D), BF16)],
        compiler_params=_cparams(48 * 1024 * 1024, ("arbitrary",)),
    )(pcat, pw)


def pool_bwd(dylin, d, pw):
    def body(dy_ref, d_ref, pw_ref, du_ref, dpw_ref):
        for gi, w in enumerate(POOL_WINDOWS):
            dyl = dy_ref[:, gi * PO:(gi + 1) * PO]
            dd = lax.dot_general(dyl, pw_ref[gi], (NT, ((), ())), preferred_element_type=F32)
            du = _window_sum(dd * _inv_count(dd.shape, w), w, True) - dd
            du_ref[:, gi * PG:(gi + 1) * PG] = du.astype(BF16)
            dpw_ref[gi] = lax.dot_general(d_ref[:, gi * PG:(gi + 1) * PG], dyl, (TN, ((), ())),
                                          preferred_element_type=F32).astype(BF16)

    return pl.pallas_call(
        body, name="pool_bwd", grid=(1,),
        in_specs=[pl.BlockSpec((T, D), lambda i: (0, 0)), pl.BlockSpec((T, 4 * PG), lambda i: (0, 0)),
                  pl.BlockSpec((4, PG, PO), lambda i: (0, 0, 0))],
        out_specs=[pl.BlockSpec((T, 4 * PG), lambda i: (0, 0)), pl.BlockSpec((4, PG, PO), lambda i: (0, 0, 0))],
        out_shape=[SDS((T, 4 * PG), BF16), SDS((4, PG, PO), BF16)],
        compiler_params=_cparams(48 * 1024 * 1024, ("arbitrary",)),
    )(dylin, d, pw)


def _gate_decay(alow, wa, ba):
    a = jnp.dot(alow, wa, preferred_element_type=F32) + ba
    ls = jax.nn.log_sigmoid(a) * (1.0 / 16.0)
    r = lax.broadcasted_iota(jnp.int32, (CHUNK, CHUNK), 0)
    c = lax.broadcasted_iota(jnp.int32, (CHUNK, CHUNK), 1)
    tri = jnp.where(c <= r, 1.0, 0.0).astype(F32)
    cum = jnp.dot(tri, ls, preferred_element_type=F32, precision=lax.Precision.HIGHEST)
    last = cum[CHUNK - 1:CHUNK, :]
    return a, jnp.exp(last - cum), jnp.exp(last)


def gla_fwd(pcat, wa, ba, ng, after=None):
    afters = _as_list(after)

    def body(q_ref, k_ref, v_ref, g_ref, al_ref, wa_ref, ba_ref, ng_ref, *rest):
        og_ref, o_ref, st_ref, s_scr = rest[len(afters):]

        @pl.when(pl.program_id(0) == 0)
        def _():
            s_scr[...] = jnp.zeros_like(s_scr)

        _, e, decay = _gate_decay(al_ref[...], wa_ref[...], ba_ref[...])
        kd = (k_ref[...].astype(F32) * e).astype(BF16)
        qs = (q_ref[...].astype(F32) * (DK ** -0.5)).astype(BF16)
        for h in range(HEADS):
            ck = slice(h * DK, (h + 1) * DK)
            cv = slice(h * DV, (h + 1) * DV)
            s_new = s_scr[h] * decay[:, ck] + lax.dot_general(v_ref[:, cv], kd[:, ck], (TN, ((), ())),
                                                               preferred_element_type=F32)
            s_scr[h] = s_new
            sb = s_new.astype(BF16)
            st_ref[h] = sb
            oh = lax.dot_general(qs[:, ck], sb, (NT, ((), ())), preferred_element_type=F32)
            o_ref[:, cv] = oh.astype(BF16)
            on = oh * lax.rsqrt(jnp.mean(oh * oh, axis=-1, keepdims=True) + EPS) * ng_ref[:, cv]
            gv = g_ref[:, cv].astype(F32)
            og_ref[:, cv] = (on * (gv * _sigmoid(gv))).astype(BF16)

    row = lambda c: (c, 0)
    return pl.pallas_call(
        body, name="gla_fwd", grid=(NCHUNK,),
        in_specs=[pl.BlockSpec((CHUNK, QK), lambda c: (c, OQ // QK)), pl.BlockSpec((CHUNK, QK), lambda c: (c, OKK // QK)),
                  pl.BlockSpec((CHUNK, D), lambda c: (c, OV // D)), pl.BlockSpec((CHUNK, D), lambda c: (c, OG // D)),
                  pl.BlockSpec((CHUNK, APAD), lambda c: (c, OA // APAD)),
                  pl.BlockSpec((APAD, QK), lambda c: (0, 0)), pl.BlockSpec((1, QK), lambda c: (0, 0)),
                  pl.BlockSpec((1, D), lambda c: (0, 0))] + [ANY] * len(afters),
        out_specs=[pl.BlockSpec((CHUNK, D), row), pl.BlockSpec((CHUNK, D), row),
                   pl.BlockSpec((None, HEADS, DV, DK), lambda c: (c, 0, 0, 0))],
        out_shape=[SDS((T, D), BF16), SDS((T, D), BF16), SDS((NCHUNK, HEADS, DV, DK), BF16)],
        scratch_shapes=[pltpu.VMEM((HEADS, DV, DK), F32)],
        compiler_params=_cparams(32 * 1024 * 1024, ("arbitrary",)),
    )(pcat, pcat, pcat, pcat, pcat, wa, ba, ng, *afters)


def gla_bwd(do, pcat, states, wa, ba, after):
    def body(do_ref, q_ref, k_ref, v_ref, al_ref, sc_ref, sp_ref, wa_ref, ba_ref, after_ref,
             dq_ref, dk_ref, dv_ref, dal_ref, dwa_ref, dba_ref, ds_scr):
        i = pl.program_id(0)

        @pl.when(i == 0)
        def _():
            ds_scr[...] = jnp.zeros_like(ds_scr)

        has_prev = jnp.where(i < NCHUNK - 1, 1.0, 0.0).astype(F32)
        a, e, decay = _gate_decay(al_ref[...], wa_ref[...], ba_ref[...])
        kf = k_ref[...].astype(F32)
        kdf = kf * e
        kd = kdf.astype(BF16)
        qs = (q_ref[...].astype(F32) * (DK ** -0.5)).astype(BF16)
        dkd_parts, ddecay_parts = [], []
        for h in range(HEADS):
            ck = slice(h * DK, (h + 1) * DK)
            cv = slice(h * DV, (h + 1) * DV)
            doh = do_ref[:, cv]
            ds = ds_scr[h] + lax.dot_general(doh, qs[:, ck], (TN, ((), ())), preferred_element_type=F32)
            dsb = ds.astype(BF16)
            dq_ref[:, ck] = (jnp.dot(doh, sc_ref[h], preferred_element_type=F32) * (DK ** -0.5)).astype(BF16)
            dkd_parts.append(jnp.dot(v_ref[:, cv], dsb, preferred_element_type=F32))
            dv_ref[:, cv] = lax.dot_general(kd[:, ck], dsb, (NT, ((), ())), preferred_element_type=F32).astype(BF16)
            ddecay_parts.append(jnp.sum(ds * sp_ref[h].astype(F32), axis=0, keepdims=True) * has_prev)
            ds_scr[h] = ds * decay[:, ck]
        dkd = jnp.concatenate(dkd_parts, axis=1)
        ddecay = jnp.concatenate(ddecay_parts, axis=1)
        dk_ref[...] = (dkd * e).astype(BF16)
        dearg = dkd * kdf
        dlast = jnp.sum(dearg, axis=0, keepdims=True) + ddecay * decay
        r = lax.broadcasted_iota(jnp.int32, (CHUNK, CHUNK), 0)
        c = lax.broadcasted_iota(jnp.int32, (CHUNK, CHUNK), 1)
        triu = jnp.where(c >= r, 1.0, 0.0).astype(F32)
        dls = dlast - jnp.dot(triu, dearg, preferred_element_type=F32, precision=lax.Precision.HIGHEST)
        da = dls * (1.0 / 16.0) * (1.0 - _sigmoid(a))
        dab = da.astype(BF16)
        dal_ref[...] = lax.dot_general(dab, wa_ref[...], (NT, ((), ())), preferred_element_type=F32).astype(BF16)
        dwa = lax.dot_general(al_ref[...], dab, (TN, ((), ())), preferred_element_type=F32)
        dba = jnp.sum(da, axis=0, keepdims=True)

        @pl.when(i == 0)
        def _():
            dwa_ref[...] = dwa
            dba_ref[...] = dba

        @pl.when(i > 0)
        def _():
            dwa_ref[...] += dwa
            dba_ref[...] += dba

    rev = lambda i: NCHUNK - 1 - i
    return pl.pallas_call(
        body, name="gla_bwd", grid=(NCHUNK,),
        in_specs=[pl.BlockSpec((CHUNK, D), lambda i: (rev(i), 0)),
                  pl.BlockSpec((CHUNK, QK), lambda i: (rev(i), OQ // QK)), pl.BlockSpec((CHUNK, QK), lambda i: (rev(i), OKK // QK)),
                  pl.BlockSpec((CHUNK, D), lambda i: (rev(i), OV // D)), pl.BlockSpec((CHUNK, APAD), lambda i: (rev(i), OA // APAD)),
                  pl.BlockSpec((None, HEADS, DV, DK), lambda i: (rev(i), 0, 0, 0)),
                  pl.BlockSpec((None, HEADS, DV, DK), lambda i: (jnp.maximum(rev(i) - 1, 0), 0, 0, 0)),
                  pl.BlockSpec((APAD, QK), lambda i: (0, 0)), pl.BlockSpec((1, QK), lambda i: (0, 0)), ANY],
        out_specs=[pl.BlockSpec((CHUNK, QK), lambda i: (rev(i), 0)), pl.BlockSpec((CHUNK, QK), lambda i: (rev(i), 0)),
                   pl.BlockSpec((CHUNK, D), lambda i: (rev(i), 0)), pl.BlockSpec((CHUNK, APAD), lambda i: (rev(i), 0)),
                   pl.BlockSpec((APAD, QK), lambda i: (0, 0)), pl.BlockSpec((1, QK), lambda i: (0, 0))],
        out_shape=[SDS((T, QK), BF16), SDS((T, QK), BF16), SDS((T, D), BF16), SDS((T, APAD), BF16),
                   SDS((APAD, QK), F32), SDS((1, QK), F32)],
        scratch_shapes=[pltpu.VMEM((HEADS, DV, DK), F32)],
        compiler_params=_cparams(32 * 1024 * 1024, ("arbitrary",)),
    )(do, pcat, pcat, pcat, pcat, states, states, wa, ba, after)


TMF = 256
TMW = 512
_rowblk = ((TMF, D), lambda j, i, k: (i, 0))
_vec = ((1, D), lambda j, i, k: (0, 0))


def _full_spec(col):
    return ((TMF, D), lambda j, i, k: (i, col))


TBIG = 1024


def square_matmul(name, a, b, *, a_spec, b_spec, cdims, nk, after=None):
    def epi(acc, ex, outs, i):
        outs[0][...] = acc

    return matmul(name, a, b, a_spec=a_spec, b_spec=b_spec, cdims=cdims, grid=(D // TBIG, T // TBIG, nk),
                  acc_shape=(TBIG, TBIG), outs=[((T, D), F32, (TBIG, TBIG), lambda j, i, k: (i, j))], epi=epi,
                  after=after)[0]


def rowwise(name, y, *, extras, outs, epi):
    ne = len(extras)

    def body(*refs):
        epi(refs[0][...], refs[1:1 + ne], refs[1 + ne:], pl.program_id(1))

    in_specs = [pl.BlockSpec(*_rowblk)] + [pl.BlockSpec(bs, im) for _, bs, im in extras]
    return pl.pallas_call(
        body, name=name, grid=(1, T // TMF, 1), in_specs=in_specs,
        out_specs=[pl.BlockSpec(bs, im) for _, _, bs, im in outs], out_shape=[SDS(s, dt) for s, dt, _, _ in outs],
        compiler_params=_cparams(40 * 1024 * 1024, ("arbitrary", "arbitrary", "arbitrary")),
    )(y, *[arr for arr, _, _ in extras])


def mm_gla_out(og, w, ylin, pcat, pscale):
    def epi(acc, ex, outs, i):
        ylin_ref, lgp_ref, lgg_ref, ps_ref = ex
        for c0 in range(0, D, EPI_COLS):
            cs = slice(c0, c0 + EPI_COLS)
            gp = _sigmoid(lgp_ref[:, cs].astype(F32))
            gg = _sigmoid(lgg_ref[:, cs].astype(F32))
            a = acc[:, cs]
            outs[0][:, cs] = (gp * (ylin_ref[:, cs].astype(F32) * ps_ref[:, cs]) + gg * a).astype(BF16)
            outs[1][:, cs] = a.astype(BF16)

    return matmul("mm_gla_out", og, w, a_spec=_rowblk, b_spec=((D, D), lambda j, i, k: (0, 0)), cdims=NN,
                  grid=(1, T // TMF, 1), acc_shape=(TMF, D),
                  extras=[(ylin, *_rowblk), (pcat, *_full_spec(OGP // D)), (pcat, *_full_spec(OGG // D)), (pscale, *_vec)],
                  outs=[((T, D), BF16, *_rowblk), ((T, D), BF16, *_rowblk)], epi=epi)


def mm_out(mixed, w, x, g2):
    def epi(acc, ex, outs, i):
        x_ref, g_ref = ex
        x2 = x_ref[...] + acc
        r = lax.rsqrt(jnp.mean(x2 * x2, axis=-1, keepdims=True) + EPS)
        outs[0][...] = x2
        outs[1][...] = (x2 * r * g_ref[...]).astype(BF16)

    return matmul("mm_out", mixed, w, a_spec=_rowblk, b_spec=((D, D), lambda j, i, k: (0, 0)), cdims=NN,
                  grid=(1, T // TMF, 1), acc_shape=(TMF, D), extras=[(x, *_rowblk), (g2, *_vec)],
                  outs=[((T, D), F32, *_rowblk), ((T, D), BF16, *_rowblk)], epi=epi)


def mm_up(h2, wup):
    def epi(acc, ex, outs, i):
        r = jnp.maximum(acc, 0.0)
        outs[0][...] = r.astype(BF16)
        outs[1][...] = (r * r).astype(BF16)

    blk = ((TMW, D), lambda j, i, k: (i, j))
    return matmul("mm_up", h2, wup, a_spec=((TMW, D), lambda j, i, k: (i, 0)), b_spec=((None, D, D), lambda j, i, k: (j, 0, 0)),
                  cdims=NN, grid=(NCHIP, T // TMW, 1), acc_shape=(TMW, D),
                  outs=[((T, DFF), BF16, *blk), ((T, DFF), BF16, *blk)], epi=epi)


def mm_down(act, wdown, x2, tgt, gf):
    tk = 4096

    def epi(acc, ex, outs, i):
        x2_ref, t_ref, g_ref = ex
        dx_ref, dxb_ref, gnf_ref, loss_ref = outs
        x3 = x2_ref[...] + acc
        r = lax.rsqrt(jnp.mean(x3 * x3, axis=-1, keepdims=True) + EPS)
        xn = x3 * r
        err = xn * g_ref[...] - t_ref[...]
        lsum = 0.5 * jnp.sum(jnp.mean(err * err, axis=-1, keepdims=True), axis=0, keepdims=True)
        dy = err * (1.0 / D)
        _row_acc(gnf_ref, jnp.sum(dy * xn, axis=0, keepdims=True), i)
        _row_acc(loss_ref, jnp.broadcast_to(lsum, (1, 128)), i)
        dx3 = _rms_bwd(xn, r, dy * g_ref[...])
        dx_ref[...] = dx3
        dxb_ref[...] = dx3.astype(BF16)

    y = square_matmul("mm_down", act, wdown, a_spec=((TBIG, tk), lambda j, i, k: (i, k)),
                      b_spec=((tk, TBIG), lambda j, i, k: (k, j)), cdims=NN, nk=DFF // tk)
    return rowwise("rows_final", y, extras=[(x2, *_rowblk), (tgt, *_rowblk), (gf, *_vec)],
                   outs=[((T, D), F32, *_rowblk), ((T, D), BF16, *_rowblk), ((1, D), F32, *_vec),
                         ((1, 128), F32, (1, 128), lambda j, i, k: (0, 0))], epi=epi)


def mm_dact(dx3b, wdown, rup, after=None):
    def epi(acc, ex, outs, i):
        outs[0][...] = (acc * 2.0 * ex[0][...].astype(F32)).astype(BF16)

    blk = ((TMW, D), lambda j, i, k: (i, j))
    return matmul("mm_dact", dx3b, wdown, a_spec=((TMW, D), lambda j, i, k: (i, 0)), b_spec=((D, D), lambda j, i, k: (j, 0)),
                  cdims=NT, grid=(DFF // D, T // TMW, 1), acc_shape=(TMW, D), extras=[(rup, *blk)],
                  outs=[((T, DFF), BF16, *blk)], epi=epi, after=after)[0]


def mm_wgrad(name, a, b, m, n, out_shape, out_block, out_map, tm, tn, after=None):
    def epi(acc, ex, outs, i):
        outs[0][...] = acc.astype(BF16)

    return matmul(name, a, b, a_spec=((T, tm), lambda j, i, k: (0, i)), b_spec=((T, tn), lambda j, i, k: (0, j)),
                  cdims=TN, grid=(n // tn, m // tm, 1), acc_shape=(tm, tn),
                  outs=[(out_shape, BF16, out_block, out_map)], epi=epi, after=after)[0]


def mm_dh2(dup, wup, x2, dx3, g2, after=None):
    def epi(acc, ex, outs, i):
        x2_ref, dx3_ref, g_ref = ex
        x2 = x2_ref[...]
        r = lax.rsqrt(jnp.mean(x2 * x2, axis=-1, keepdims=True) + EPS)
        xn = x2 * r
        _row_acc(outs[2], jnp.sum(acc * xn, axis=0, keepdims=True), i)
        dx2 = dx3_ref[...] + _rms_bwd(xn, r, acc * g_ref[...])
        outs[0][...] = dx2
        outs[1][...] = dx2.astype(BF16)

    y = square_matmul("mm_dh2", dup, wup, a_spec=((TBIG, D), lambda j, i, k: (i, k)),
                      b_spec=((None, TBIG, D), lambda j, i, k: (k, j, 0)), cdims=NT, nk=NCHIP, after=after)
    return rowwise("rows_dh2", y, extras=[(x2, *_rowblk), (dx3, *_rowblk), (g2, *_vec)],
                   outs=[((T, D), F32, *_rowblk), ((T, D), BF16, *_rowblk), ((1, D), F32, *_vec)], epi=epi)


def mm_dmixed(dx2b, wout, pcat, ylin, ygla, pscale, after=None):
    def epi(acc, ex, outs, i):
        lgp_ref, lgg_ref, ylin_ref, ygla_ref, ps_ref = ex
        dps = []
        for c0 in range(0, D, EPI_COLS):
            cs = slice(c0, c0 + EPI_COLS)
            gp = _sigmoid(lgp_ref[:, cs].astype(F32))
            gg = _sigmoid(lgg_ref[:, cs].astype(F32))
            yl = ylin_ref[:, cs].astype(F32)
            ps = ps_ref[:, cs]
            a = acc[:, cs]
            agp = a * gp
            outs[0][:, cs] = (agp * ps).astype(BF16)
            outs[1][:, cs] = (a * gg).astype(BF16)
            outs[2][:, cs] = (agp * (yl * ps) * (1.0 - gp)).astype(BF16)
            outs[3][:, cs] = (a * ygla_ref[:, cs].astype(F32) * gg * (1.0 - gg)).astype(BF16)
            dps.append(jnp.sum(agp * yl, axis=0, keepdims=True))
        _row_acc(outs[4], jnp.concatenate(dps, axis=1), i)

    return matmul("mm_dmixed", dx2b, wout, a_spec=_rowblk, b_spec=((D, D), lambda j, i, k: (0, 0)), cdims=NT,
                  grid=(1, T // TMF, 1), acc_shape=(TMF, D),
                  extras=[(pcat, *_full_spec(OGP // D)), (pcat, *_full_spec(OGG // D)), (ylin, *_rowblk), (ygla, *_rowblk),
                          (pscale, *_vec)],
                  outs=[((T, D), BF16, *_rowblk)] * 4 + [((1, D), F32, *_vec)], epi=epi, after=after)


def mm_dog(dygla, wgo, o, pcat, ng, after=None):
    def epi(acc, ex, outs, i):
        o_ref, g_ref, ng_ref = ex
        do_ref, dg_ref, gng_ref = outs
        gparts = []
        for h in range(HEADS):
            cv = slice(h * DV, (h + 1) * DV)
            oh = o_ref[:, cv].astype(F32)
            r = lax.rsqrt(jnp.mean(oh * oh, axis=-1, keepdims=True) + EPS)
            on = oh * r
            gv = g_ref[:, cv].astype(F32)
            sg = _sigmoid(gv)
            a = acc[:, cv]
            dgain = a * (gv * sg)
            gparts.append(jnp.sum(dgain * on, axis=0, keepdims=True))
            ngh = ng_ref[:, cv]
            do_ref[:, cv] = _rms_bwd(on, r, dgain * ngh).astype(BF16)
            dg_ref[:, cv] = (a * (on * ngh) * (sg * (1.0 + gv * (1.0 - sg)))).astype(BF16)
        _row_acc(gng_ref, jnp.concatenate(gparts, axis=1), i)

    return matmul("mm_dog", dygla, wgo, a_spec=_rowblk, b_spec=((D, D), lambda j, i, k: (0, 0)), cdims=NT,
                  grid=(1, T // TMF, 1), acc_shape=(TMF, D),
                  extras=[(o, *_rowblk), (pcat, *_full_spec(OG // D)), (ng, *_vec)],
                  outs=[((T, D), BF16, *_rowblk), ((T, D), BF16, *_rowblk), ((1, D), F32, *_vec)], epi=epi, after=after)


def mm_dh1(dpcat, wcat, x, dx2, g1, after=None):
    tk = 3840

    def epi(acc, ex, outs, i):
        x_ref, dx2_ref, g_ref = ex
        xv = x_ref[...]
        r = lax.rsqrt(jnp.mean(xv * xv, axis=-1, keepdims=True) + EPS)
        xn = xv * r
        _row_acc(outs[1], jnp.sum(acc * xn, axis=0, keepdims=True), i)
        outs[0][...] = dx2_ref[...] + _rms_bwd(xn, r, acc * g_ref[...])

    y = square_matmul("mm_dh1", dpcat, wcat, a_spec=((TBIG, tk), lambda j, i, k: (i, k)),
                      b_spec=((TBIG, tk), lambda j, i, k: (j, k)), cdims=NT, nk=NCAT // tk, after=after)
    return rowwise("rows_dh1", y, extras=[(x, *_rowblk), (dx2, *_rowblk), (g1, *_vec)],
                   outs=[((T, D), F32, *_rowblk), ((1, D), F32, *_vec)], epi=epi)


def _tile_rows(rows, cols, n_arrays):
    tm = rows
    while tm % 32 == 0 and 2 * n_arrays * tm * cols * 4 > 24 * 1024 * 1024:
        tm //= 2
    return tm


def add_pairs(name, parts, theirs, core):
    _, _, r, c = parts.shape
    tm = _tile_rows(r, c, 3)

    def body(core_ref, a_ref, b_ref, o_ref):
        o_ref[...] = (a_ref[...].astype(F32) + b_ref[...].astype(F32)).astype(BF16)

    spec = pl.BlockSpec((None, tm, c), lambda j, i, core_ref: (j, i, 0))
    grid_spec = pltpu.PrefetchScalarGridSpec(
        num_scalar_prefetch=1, grid=(NCHIP, r // tm),
        in_specs=[pl.BlockSpec((None, None, tm, c), lambda j, i, core_ref: (core_ref[0], j, i, 0)), spec], out_specs=spec)
    return pl.pallas_call(body, name=name, grid_spec=grid_spec, out_shape=SDS((NCHIP, r, c), BF16),
                          compiler_params=_cparams(40 * 1024 * 1024, ("arbitrary", "arbitrary")))(core, parts, theirs)


def sum_chips(name, sums, landed, chip):
    _, r, c = sums.shape
    tm = _tile_rows(r, c, 4)

    def body(chip_ref, own_ref, l_ref, o_ref):
        s = own_ref[...].astype(F32)
        for t in range(NCHIP - 1):
            s = s + l_ref[t].astype(F32)
        o_ref[...] = s

    grid_spec = pltpu.PrefetchScalarGridSpec(
        num_scalar_prefetch=1, grid=(r // tm,),
        in_specs=[pl.BlockSpec((None, tm, c), lambda i, chip_ref: (chip_ref[0], i, 0)),
                  pl.BlockSpec((NCHIP - 1, tm, c), lambda i, chip_ref: (0, i, 0))],
        out_specs=pl.BlockSpec((tm, c), lambda i, chip_ref: (i, 0)))
    return pl.pallas_call(body, name=name, grid_spec=grid_spec, out_shape=SDS((r, c), F32),
                          compiler_params=_cparams(40 * 1024 * 1024, ("arbitrary",)))(chip, sums, landed)


def _adamw_math(wv, gv, mv, vv):
    mn = ADAM_B1 * mv + (1.0 - ADAM_B1) * gv
    vn = ADAM_B2 * vv + (1.0 - ADAM_B2) * (gv * gv)
    mh = mn / (1.0 - ADAM_B1 ** ADAM_STEP)
    vh = vn / (1.0 - ADAM_B2 ** ADAM_STEP)
    return -ADAM_LR * (mh / (jnp.sqrt(vh) + ADAM_EPS) + ADAM_WD * wv), mn, vn


def adamw(name, w, g, m, v):
    def body(w_ref, g_ref, m_ref, v_ref, go_ref, d_ref, mo_ref, vo_ref):
        gv = g_ref[...]
        go_ref[...] = gv
        d_ref[...], mo_ref[...], vo_ref[...] = _adamw_math(w_ref[...], gv, m_ref[...], v_ref[...])

    return pl.pallas_call(body, name=name, out_shape=[SDS(w.shape, F32)] * 4)(w, g, m, v)


def adamw_halves(name, w, g_own, g_sib, m, v, core):
    _, r, c = w.shape
    tm = _tile_rows(r, c, 10)

    def body(core_ref, w_ref, go_ref, gs_ref, m_ref, v_ref, g_out, d_out, m_out, v_out):
        gv = jnp.where(pl.program_id(0) == core_ref[0], go_ref[...], gs_ref[...])
        g_out[...] = gv
        d_out[...], m_out[...], v_out[...] = _adamw_math(w_ref[...], gv, m_ref[...], v_ref[...])

    full = pl.BlockSpec((None, tm, c), lambda h, i, core_ref: (h, i, 0))
    own = pl.BlockSpec((tm, c), lambda h, i, core_ref: (jnp.where(h == core_ref[0], i, 0), 0))
    sib = pl.BlockSpec((tm, c), lambda h, i, core_ref: (jnp.where(h == core_ref[0], 0, i), 0))
    grid_spec = pltpu.PrefetchScalarGridSpec(num_scalar_prefetch=1, grid=(2, r // tm),
                                             in_specs=[full, own, sib, full, full], out_specs=[full] * 4)
    return pl.pallas_call(body, name=name, grid_spec=grid_spec, out_shape=[SDS(w.shape, F32)] * 4,
                          compiler_params=_cparams(48 * 1024 * 1024, ("arbitrary", "arbitrary")))(core, w, g_own, g_sib, m, v)


def cast_bf16(name, w):
    _, r, c = w.shape
    tm = _tile_rows(r, c, 2)

    def body(w_ref, o_ref):
        o_ref[...] = w_ref[...].astype(BF16)

    spec = pl.BlockSpec((None, tm, c), lambda h, i: (h, i, 0))
    return pl.pallas_call(body, name=name, grid=(2, r // tm), in_specs=[spec], out_specs=spec, out_shape=SDS(w.shape, BF16),
                          compiler_params=_cparams(40 * 1024 * 1024, ("arbitrary", "arbitrary")))(w)


def pack_rows(name, parts, rows, after=None):
    width = parts[0].shape[1]
    n = len(parts)
    afters = _as_list(after)

    def body(*refs):
        out_ref = refs[n + len(afters)]
        out_ref[...] = jnp.zeros_like(out_ref)
        off = 0
        for p in refs[:n]:
            out_ref[off:off + p.shape[0], :] = p[...]
            off += p.shape[0]

    vm = pl.BlockSpec(memory_space=pltpu.VMEM)
    return pl.pallas_call(body, name=name, in_specs=[vm] * n + [ANY] * len(afters), out_specs=vm,
                          out_shape=SDS((rows, width), F32))(*parts, *afters)


def _place():
    x, y, c = lax.axis_index("x"), lax.axis_index("y"), lax.axis_index("c")
    chips = [(1 - x, y), (x, 1 - y), (1 - x, 1 - y)]
    return x, y, c, chips


def _row_split(shape, dtype):
    r, c = shape
    n = 1
    while r % (2 * n) == 0 and (r // (2 * n)) % 16 == 0 and (r // n) * c * jnp.dtype(dtype).itemsize > PIECE_BYTES:
        n *= 2
    return [pl.ds(s * (r // n), r // n) for s in range(n)]


def _pieces(ref):
    *lead, r, c = ref.shape
    split = _row_split((r, c), ref.dtype)
    return [ref.at[(*idx, s)] for idx in itertools.product(*[range(d) for d in lead]) for s in split]


HBM = pl.BlockSpec(memory_space=pltpu.HBM)
SEM = pl.BlockSpec(memory_space=pltpu.SEMAPHORE)
EFFECT = pltpu.SideEffectType.DATAFLOW_SIDE_EFFECTING


def gather_start(name, shards, after=None):
    n = len(shards)
    afters = _as_list(after)

    def body(*refs):
        src, land = refs[:n], refs[n:2 * n]
        send, recv = refs[2 * n + len(afters)], refs[2 * n + len(afters) + 1]
        x, y, c, chips = _place()
        me = 2 * x + y
        for a in range(n):
            for j, (cx, cy) in enumerate(chips[:2]):
                for sp, dp in zip(_pieces(src[a].at[c]), _pieces(land[a].at[me, c])):
                    pltpu.make_async_remote_copy(sp, dp, send.at[2 * a + j], recv.at[2 * a + j],
                                                 device_id=(cx, cy, c), device_id_type=MESH).start()

    lands = [pltpu.with_memory_space_constraint(lax.empty((NCHIP,) + s.shape, s.dtype), pltpu.HBM) for s in shards]
    srcs = [pltpu.with_memory_space_constraint(s, pltpu.HBM) for s in shards]
    outs = pl.pallas_call(
        body, name=name,
        out_shape=(pltpu.SemaphoreType.DMA((2 * n,)), pltpu.SemaphoreType.DMA((2 * n,)),
                   *[pltpu.HBM(s.shape, s.dtype) for s in shards], *[pltpu.HBM(l.shape, l.dtype) for l in lands]),
        in_specs=[HBM] * (2 * n) + [ANY] * len(afters), out_specs=(SEM, SEM, *([HBM] * (2 * n))),
        input_output_aliases={i: 2 + i for i in range(2 * n)},
        compiler_params=pltpu.CompilerParams(has_side_effects=EFFECT),
    )(*srcs, *lands, *afters)
    return outs[0], outs[1], list(outs[2:2 + n]), list(outs[2 + n:2 + 2 * n])


def gather_wait(name, send, recv, shards, lands, after):
    n = len(shards)
    afters = _as_list(after)

    def body(*refs):
        src, land = refs[:n], refs[n:2 * n]
        send_ref, recv_ref = refs[2 * n], refs[2 * n + 1]
        x, y, c, chips = _place()
        for a in range(n):
            for j, (cx, cy) in enumerate(chips[:2]):
                cp = pltpu.make_async_remote_copy(src[a].at[c], land[a].at[2 * cx + cy, c], send_ref.at[2 * a + j],
                                                  recv_ref.at[2 * a + j], device_id=(cx, cy, c), device_id_type=MESH)
                cp.wait_send()
                cp.wait_recv()

    outs = pl.pallas_call(
        body, name=name,
        out_shape=(*[pltpu.HBM(s.shape, s.dtype) for s in shards], *[pltpu.HBM(l.shape, l.dtype) for l in lands]),
        in_specs=[HBM] * (2 * n) + [SEM, SEM] + [ANY] * len(afters), out_specs=[HBM] * (2 * n),
        input_output_aliases={i: i for i in range(2 * n)},
        compiler_params=pltpu.CompilerParams(has_side_effects=EFFECT),
    )(*shards, *lands, send, recv, *afters)
    return list(outs[:n]), list(outs[n:])


def _relay_blocks(land, c, chips):
    (xx, xy), (yx, yy), (dx, dy) = chips
    rows = land.shape[2] // 2
    upper, lower = pl.ds(0, rows), pl.ds(rows, rows)
    return [(land.at[2 * yx + yy, c, lower], land.at[2 * dx + dy, c, lower]),
            (land.at[2 * xx + xy, c, upper], land.at[2 * dx + dy, c, upper])]


def relay_start(name, lands, after=None):
    n = len(lands)
    afters = _as_list(after)

    def body(*refs):
        had, land = refs[:n], refs[n + len(afters) + 2:2 * n + len(afters) + 2]
        send, recv = refs[n + len(afters)], refs[n + len(afters) + 1]
        x, y, c, chips = _place()
        for a in range(n):
            for j, ((sent, _), (dst, _)) in enumerate(zip(_relay_blocks(had[a], c, chips), _relay_blocks(land[a], c, chips))):
                cx, cy = chips[j]
                for sp, dp in zip(_pieces(sent), _pieces(dst)):
                    pltpu.make_async_remote_copy(sp, dp, send.at[2 * a + j], recv.at[2 * a + j],
                                                 device_id=(cx, cy, c), device_id_type=MESH).start()

    outs = pl.pallas_call(
        body, name=name,
        out_shape=(pltpu.SemaphoreType.DMA((2 * n,)), pltpu.SemaphoreType.DMA((2 * n,)),
                   *[pltpu.HBM(l.shape, l.dtype) for l in lands]),
        in_specs=[HBM] * n + [ANY] * len(afters), out_specs=(SEM, SEM, *([HBM] * n)),
        input_output_aliases={i: 2 + i for i in range(n)},
        compiler_params=pltpu.CompilerParams(has_side_effects=EFFECT),
    )(*lands, *afters)
    return outs[0], outs[1], list(outs[2:])


def relay_wait(name, send, recv, lands, after):
    n = len(lands)
    afters = _as_list(after)

    def body(*refs):
        land = refs[:n]
        send_ref, recv_ref = refs[n], refs[n + 1]
        x, y, c, chips = _place()
        for a in range(n):
            for j, (sent, got) in enumerate(_relay_blocks(land[a], c, chips)):
                cx, cy = chips[j]
                cp = pltpu.make_async_remote_copy(sent, got, send_ref.at[2 * a + j], recv_ref.at[2 * a + j],
                                                  device_id=(cx, cy, c), device_id_type=MESH)
                cp.wait_send()
                cp.wait_recv()

    outs = pl.pallas_call(
        body, name=name, out_shape=tuple(pltpu.HBM(l.shape, l.dtype) for l in lands),
        in_specs=[HBM] * n + [SEM, SEM] + [ANY] * len(afters), out_specs=[HBM] * n,
        input_output_aliases={i: i for i in range(n)},
        compiler_params=pltpu.CompilerParams(has_side_effects=EFFECT),
    )(*lands, send, recv, *afters)
    return list(outs)


def forward_halves(name, shards, lands):
    n = len(lands)

    def body(*refs):
        had, buf = refs[:n], refs[n:2 * n]
        send, recv = refs[2 * n:]
        x, y, c, chips = _place()
        sib = (x, y, 1 - c)
        for a in range(n):
            for j, (cx, cy) in enumerate(chips):
                for sp, dp in zip(_pieces(had[a].at[2 * cx + cy, c]), _pieces(buf[a].at[2 * cx + cy, c])):
                    pltpu.make_async_remote_copy(sp, dp, send.at[3 * a + j], recv.at[3 * a + j], device_id=sib, device_id_type=MESH).start()
        for a in range(n):
            for j, (cx, cy) in enumerate(chips):
                pltpu.make_async_remote_copy(had[a].at[2 * cx + cy, c], buf[a].at[2 * cx + cy, 1 - c], send.at[3 * a + j],
                                             recv.at[3 * a + j], device_id=sib, device_id_type=MESH).wait()

    got = pl.pallas_call(
        body, name=name, in_specs=[ANY] * n, out_specs=[ANY] * n, out_shape=[SDS(l.shape, l.dtype) for l in lands],
        input_output_aliases={i: i for i in range(n)},
        scratch_shapes=[pltpu.SemaphoreType.DMA((3 * n,)), pltpu.SemaphoreType.DMA((3 * n,))],
    )(*lands)
    me = 2 * lax.axis_index("x") + lax.axis_index("y")
    return [lax.dynamic_update_index_in_dim(g, s, me, 0) for g, s in zip(got, shards)]


def exchange_start(name, parts):
    n = len(parts)

    def body(*refs):
        src, got = refs[:n], refs[n:2 * n]
        send, recv = refs[2 * n], refs[2 * n + 1]
        token = refs[4 * n + 2]
        x, y, c, _ = _place()
        sib = (x, y, 1 - c)
        for a in range(n):
            for sp, dp in zip(_pieces(src[a].at[1 - c]), _pieces(got[a])):
                pltpu.make_async_remote_copy(sp, dp, send.at[a], recv.at[a], device_id=sib, device_id_type=MESH).start()
        token[...] = jnp.zeros_like(token)

    lands = [pltpu.with_memory_space_constraint(lax.empty(p.shape[1:], p.dtype), pltpu.HBM) for p in parts]
    srcs = [pltpu.with_memory_space_constraint(p, pltpu.HBM) for p in parts]
    outs = pl.pallas_call(
        body, name=name,
        out_shape=(pltpu.SemaphoreType.DMA((n,)), pltpu.SemaphoreType.DMA((n,)),
                   *[pltpu.HBM(p.shape, p.dtype) for p in parts], *[pltpu.HBM(l.shape, l.dtype) for l in lands],
                   SDS((8, 128), F32)),
        in_specs=[HBM] * (2 * n), out_specs=(SEM, SEM, *([HBM] * (2 * n)), pl.BlockSpec(memory_space=pltpu.VMEM)),
        input_output_aliases={i: 2 + i for i in range(2 * n)},
        compiler_params=pltpu.CompilerParams(has_side_effects=EFFECT),
    )(*srcs, *lands)
    return outs[0], outs[1], list(outs[2:2 + n]), list(outs[2 + n:2 + 2 * n]), outs[2 + 2 * n]


def exchange_wait(name, send, recv, parts, lands, after):
    n = len(parts)
    afters = _as_list(after)

    def body(*refs):
        src, got = refs[:n], refs[n:2 * n]
        send_ref, recv_ref = refs[2 * n], refs[2 * n + 1]
        x, y, c, _ = _place()
        sib = (x, y, 1 - c)
        for a in range(n):
            cp = pltpu.make_async_remote_copy(src[a].at[1 - c], got[a], send_ref.at[a], recv_ref.at[a], device_id=sib, device_id_type=MESH)
            cp.wait_send()
            cp.wait_recv()

    outs = pl.pallas_call(
        body, name=name,
        out_shape=(*[pltpu.HBM(p.shape, p.dtype) for p in parts], *[pltpu.HBM(l.shape, l.dtype) for l in lands]),
        in_specs=[HBM] * (2 * n) + [SEM, SEM] + [ANY] * len(afters), out_specs=[HBM] * (2 * n),
        input_output_aliases={i: i for i in range(2 * n)},
        compiler_params=pltpu.CompilerParams(has_side_effects=EFFECT),
    )(*parts, *lands, send, recv, *afters)
    return list(outs[:n]), list(outs[n:])


def scatter_start(name, parts):
    n = len(parts)

    def body(*refs):
        src, land = refs[:n], refs[n:2 * n]
        send, recv = refs[2 * n], refs[2 * n + 1]
        token = refs[4 * n + 2]
        x, y, c, chips = _place()
        for a in range(n):
            for j, (cx, cy) in enumerate(chips):
                for sp, dp in zip(_pieces(src[a].at[2 * cx + cy]), _pieces(land[a].at[j])):
                    pltpu.make_async_remote_copy(sp, dp, send.at[3 * a + j], recv.at[3 * a + j],
                                                 device_id=(cx, cy, c), device_id_type=MESH).start()
        token[...] = jnp.zeros_like(token)

    lands = [pltpu.with_memory_space_constraint(lax.empty((NCHIP - 1,) + p.shape[1:], p.dtype), pltpu.HBM) for p in parts]
    srcs = [pltpu.with_memory_space_constraint(p, pltpu.HBM) for p in parts]
    outs = pl.pallas_call(
        body, name=name,
        out_shape=(pltpu.SemaphoreType.DMA((3 * n,)), pltpu.SemaphoreType.DMA((3 * n,)),
                   *[pltpu.HBM(p.shape, p.dtype) for p in parts], *[pltpu.HBM(l.shape, l.dtype) for l in lands],
                   SDS((8, 128), F32)),
        in_specs=[HBM] * (2 * n), out_specs=(SEM, SEM, *([HBM] * (2 * n)), pl.BlockSpec(memory_space=pltpu.VMEM)),
        input_output_aliases={i: 2 + i for i in range(2 * n)},
        compiler_params=pltpu.CompilerParams(has_side_effects=EFFECT),
    )(*srcs, *lands)
    return outs[0], outs[1], list(outs[2:2 + n]), list(outs[2 + n:2 + 2 * n]), outs[2 + 2 * n]


def scatter_wait(name, send, recv, parts, lands, after):
    n = len(parts)
    afters = _as_list(after)

    def body(*refs):
        src, land = refs[:n], refs[n:2 * n]
        send_ref, recv_ref = refs[2 * n], refs[2 * n + 1]
        x, y, c, chips = _place()
        for a in range(n):
            for j, (cx, cy) in enumerate(chips):
                cp = pltpu.make_async_remote_copy(src[a].at[2 * cx + cy], land[a].at[j], send_ref.at[3 * a + j], recv_ref.at[3 * a + j],
                                                  device_id=(cx, cy, c), device_id_type=MESH)
                cp.wait_send()
                cp.wait_recv()

    outs = pl.pallas_call(
        body, name=name,
        out_shape=(*[pltpu.HBM(p.shape, p.dtype) for p in parts], *[pltpu.HBM(l.shape, l.dtype) for l in lands]),
        in_specs=[HBM] * (2 * n) + [SEM, SEM] + [ANY] * len(afters), out_specs=[HBM] * (2 * n),
        input_output_aliases={i: i for i in range(2 * n)},
        compiler_params=pltpu.CompilerParams(has_side_effects=EFFECT),
    )(*parts, *lands, send, recv, *afters)
    return list(outs[:n]), list(outs[n:])


def join_start(name, halves):
    n = len(halves)

    def body(*refs):
        src, dst = refs[:n], refs[n:2 * n]
        send, recv = refs[2 * n], refs[2 * n + 1]
        token = refs[4 * n + 2]
        x, y, c, _ = _place()
        sib = (x, y, 1 - c)
        for a in range(n):
            for sp, dp in zip(_pieces(src[a]), _pieces(dst[a])):
                pltpu.make_async_remote_copy(sp, dp, send.at[a], recv.at[a], device_id=sib, device_id_type=MESH).start()
        token[...] = jnp.zeros_like(token)

    lands = [pltpu.with_memory_space_constraint(lax.empty(h.shape, h.dtype), pltpu.HBM) for h in halves]
    srcs = [pltpu.with_memory_space_constraint(h, pltpu.HBM) for h in halves]
    outs = pl.pallas_call(
        body, name=name,
        out_shape=(pltpu.SemaphoreType.DMA((n,)), pltpu.SemaphoreType.DMA((n,)),
                   *[pltpu.HBM(h.shape, h.dtype) for h in halves], *[pltpu.HBM(l.shape, l.dtype) for l in lands],
                   SDS((8, 128), F32)),
        in_specs=[HBM] * (2 * n), out_specs=(SEM, SEM, *([HBM] * (2 * n)), pl.BlockSpec(memory_space=pltpu.VMEM)),
        input_output_aliases={i: 2 + i for i in range(2 * n)},
        compiler_params=pltpu.CompilerParams(has_side_effects=EFFECT),
    )(*srcs, *lands)
    return outs[0], outs[1], list(outs[2:2 + n]), list(outs[2 + n:2 + 2 * n]), outs[2 + 2 * n]


def join_wait(name, send, recv, halves, lands, after):
    n = len(halves)
    afters = _as_list(after)

    def body(*refs):
        src, dst = refs[:n], refs[n:2 * n]
        send_ref, recv_ref = refs[2 * n], refs[2 * n + 1]
        x, y, c, _ = _place()
        sib = (x, y, 1 - c)
        for a in range(n):
            cp = pltpu.make_async_remote_copy(src[a], dst[a], send_ref.at[a], recv_ref.at[a], device_id=sib, device_id_type=MESH)
            cp.wait_send()
            cp.wait_recv()

    outs = pl.pallas_call(
        body, name=name,
        out_shape=(*[pltpu.HBM(h.shape, h.dtype) for h in halves], *[pltpu.HBM(l.shape, l.dtype) for l in lands]),
        in_specs=[HBM] * (2 * n) + [SEM, SEM] + [ANY] * len(afters), out_specs=[HBM] * (2 * n),
        input_output_aliases={i: i for i in range(2 * n)},
        compiler_params=pltpu.CompilerParams(has_side_effects=EFFECT),
    )(*halves, *lands, send, recv, *afters)
    return list(outs[:n]), list(outs[n:])


def gather_small(name, xs, reduce, after=None):
    m, ncol = xs.shape
    afters = _as_list(after)

    def body(x_ref, *rest):
        out_ref, all_ref, send, recv, lsem = rest[len(afters):]
        x, y, c, chips = _place()
        me, sib = (x, y, c), (x, y, 1 - c)

        def rows(px, py, pc):
            return all_ref.at[pl.ds((4 * px + 2 * py + pc) * m, m), :]

        def copy(k, block, to, src=None):
            return pltpu.make_async_remote_copy(rows(*block) if src is None else src, rows(*block), send.at[k], recv.at[k],
                                                device_id=to, device_id_type=MESH)

        mine = pltpu.make_async_copy(x_ref, rows(*me), lsem)
        mine.start()
        first = [copy(0, me, sib, src=x_ref)] + [copy(1 + j, me, (*chip, c), src=x_ref) for j, chip in enumerate(chips)]
        for cp in first:
            cp.start()
        passed = [copy(4 + j, (*chip, c), sib) for j, chip in enumerate(chips)]
        for j, chip in enumerate(chips):
            copy(1 + j, (*chip, c), me).wait_recv()
            passed[j].start()
        copy(0, sib, me).wait_recv()
        for j, chip in enumerate(chips):
            copy(4 + j, (*chip, 1 - c), me).wait_recv()
        for cp in first + passed:
            cp.wait_send()
        mine.wait()
        if reduce:
            s = all_ref[0:m, :]
            for dev in range(1, 8):
                s = s + all_ref[dev * m:(dev + 1) * m, :]
            out_ref[...] = s
        else:
            out_ref[...] = all_ref[...]

    vm = pl.BlockSpec(memory_space=pltpu.VMEM)
    return pl.pallas_call(
        body, name=name, in_specs=[vm] + [ANY] * len(afters), out_specs=vm,
        out_shape=SDS((m, ncol) if reduce else (8 * m, ncol), F32),
        scratch_shapes=[pltpu.VMEM((8 * m, ncol), F32), pltpu.SemaphoreType.DMA((7,)), pltpu.SemaphoreType.DMA((7,)),
                        pltpu.SemaphoreType.DMA],
    )(xs, *afters)


RELAYOUT_ROWS = 128


def weights_to_cat(g_in, after=None):
    tm = RELAYOUT_ROWS
    afters = _as_list(after)

    def body(g_ref, *rest):
        o_ref = rest[len(afters)]
        nat = jnp.concatenate([g_ref[j] for j in range(NCHIP)], axis=1)
        pad = jnp.zeros((tm, NCAT - OA - 16), BF16)
        o_ref[...] = jnp.concatenate([nat[:, 3072:7168], nat[:, 7184:11280], nat[:, 0:3072], nat[:, 7168:7184], pad], axis=1)

    return pl.pallas_call(
        body, name="weights_to_cat", grid=(D // tm,),
        in_specs=[pl.BlockSpec((NCHIP, tm, IN_SHARD), lambda i: (0, i, 0))] + [ANY] * len(afters),
        out_specs=pl.BlockSpec((tm, NCAT), lambda i: (i, 0)), out_shape=SDS((D, NCAT), BF16),
        compiler_params=_cparams(40 * 1024 * 1024, ("arbitrary",)),
    )(g_in, *afters)


def grads_from_cat(gw_cat):
    tm = RELAYOUT_ROWS
    nb = (D // 2) // tm

    def body(c_ref, o_ref):
        cat = c_ref[...]
        nat = jnp.concatenate([cat[:, OU:OA], cat[:, OV:OGP], cat[:, OA:OA + 16], cat[:, OGP:OU]], axis=1)
        for j in range(NCHIP):
            o_ref[j] = nat[:, j * IN_SHARD:(j + 1) * IN_SHARD]

    return pl.pallas_call(
        body, name="grads_from_cat", grid=(D // tm,), in_specs=[pl.BlockSpec((tm, NCAT), lambda i: (i, 0))],
        out_specs=pl.BlockSpec((None, NCHIP, tm, IN_SHARD), lambda i: (i // nb, 0, i % nb, 0)),
        out_shape=SDS((2, NCHIP, D // 2, IN_SHARD), BF16), compiler_params=_cparams(40 * 1024 * 1024, ("arbitrary",)),
    )(gw_cat)


def _pad_rows(a, rows):
    return jnp.concatenate([a, jnp.zeros((rows - a.shape[0],) + a.shape[1:], a.dtype)], axis=0)


def local_step(x2d, tgt, gf, g1, pool_scale, wa_pad, b_alpha, ng, g2, get_w, on_grad=None, on_settle=None, tick=None):
    emit = on_grad if on_grad is not None else (lambda group, grads: None)
    settle = on_settle if on_settle is not None else (lambda group, after: None)
    h1 = norm1(x2d, g1)
    wcat, pw = get_w("in", h1)
    pcat = mm_in(h1, wcat)
    dpool, ylin = pool_fwd(pcat, pw)
    pinned = tick("pool", ylin) if tick is not None else None
    og, o, states = gla_fwd(pcat, wa_pad, b_alpha, ng, pinned)
    w_go, w_o = get_w("mid", og)
    mixed, ygla = mm_gla_out(og, w_go, ylin, pcat, pool_scale)
    x2, h2 = mm_out(mixed, w_o, x2d, g2)
    w_up = get_w("up", h2)
    rup, act = mm_up(h2, w_up)
    w_dn = get_w("down", act)
    dx3, dx3b, g_nf, loss_row = mm_down(act, w_dn, x2, tgt, gf)

    gw_down = mm_wgrad("mm_dw_down", act, dx3b, DFF, D, (2, NCHIP, D // 2, D), (None, None, 512, D),
                       lambda j, i, k: ((i // 2) % 2, i // 4, i % 2, 0), 512, D)
    token = emit("down", {"down": gw_down})
    dup = mm_dact(dx3b, w_dn, rup, after=token)
    token = settle("down", dup)
    dx2, dx2b, g_mlp = mm_dh2(dup, w_up, x2, dx3, g2, after=token)
    gw_up = mm_wgrad("mm_dw_up", h2, dup, D, DFF, (2, NCHIP, D // 2, D), (None, None, 512, D),
                     lambda j, i, k: (i // 2, j, i % 2, 0), 512, D)
    token = emit("up", {"up": gw_up})
    dylin, dygla, dlgp, dlgg, g_ps = mm_dmixed(dx2b, w_o, pcat, ylin, ygla, pool_scale, after=token)
    token = settle("up", dylin)
    gw_out = mm_wgrad("mm_dw_out", mixed, dx2b, D, D, (2, NCHIP, 256, D), (None, None, 256, D),
                      lambda j, i, k: (i % 2, i // 2, 0, 0), 256, D)
    do, dg, g_ng = mm_dog(dygla, w_go, o, pcat, ng, after=token)
    gw_go = mm_wgrad("mm_dw_gla_out", og, dygla, D, D, (2, NCHIP, 256, D), (None, None, 256, D),
                     lambda j, i, k: (i % 2, i // 2, 0, 0), 256, D)
    token = emit("mix", {"out": gw_out, "gla_out": gw_go})
    dq, dk, dv, dalow, g_wa, g_ba = gla_bwd(do, pcat, states, wa_pad, b_alpha, b_alpha if token is None else token)
    token = settle("mix", dq)
    du, dpw = pool_bwd(dylin, dpool, pw)
    dpcat = jnp.concatenate([dv, dg, dlgp, dlgg, du, dq, dk, dalow, jnp.zeros((T, NCAT - OA - APAD), BF16)], axis=1)
    gw_cat = mm_wgrad("mm_dw_in", h1, dpcat, D, NCAT, (D, NCAT), (1024, 1280), lambda j, i, k: (i, j), 1024, 1280, after=token)
    token = settle("in", emit("in", {"in_cat": gw_cat, "pool": dpw}))
    grad_x, g_mix = mm_dh1(dpcat, wcat, x2d, dx2, g1, after=token)
    return (loss_row[0, 0], grad_x, g_mix, g_ps, g_mlp, g_nf, g_ng, g_ba, g_wa, token,
            gw_cat, dpw, gw_go, gw_out, gw_up, gw_down)


def kernel(x, norm_mix_g, w_in, pool_w, pool_scale, w_alpha, b_alpha, gla_norm_g, w_gla_out, w_out, norm_mlp_g, w_mlp_up, w_mlp_down, norm_final_g, loss_target, m_norm_mix_g, m_w_in, m_pool_w, m_pool_scale, m_w_alpha, m_b_alpha, m_gla_norm_g, m_w_gla_out, m_w_out, m_norm_mlp_g, m_w_mlp_up, m_w_mlp_down, m_norm_final_g, v_norm_mix_g, v_w_in, v_pool_w, v_pool_scale, v_w_alpha, v_b_alpha, v_gla_norm_g, v_w_gla_out, v_w_out, v_norm_mlp_g, v_w_mlp_up, v_w_mlp_down, v_norm_final_g):
    chip = 2 * lax.axis_index("x") + lax.axis_index("y")
    chip_i = chip.astype(jnp.int32).reshape(1)
    core_i = lax.axis_index("c").astype(jnp.int32).reshape(1)
    tgt = loss_target.reshape(T, D)
    gf = norm_final_g.reshape(1, D)

    def halves(w2d):
        r, c = w2d.shape
        return w2d.astype(BF16).reshape(2, r // 2, c)

    pool_shard = pool_w.reshape(4 * PG, PO // NCHIP)
    w_in_r = w_in.reshape(2, D // 2, IN_SHARD)
    sent = {"in": [cast_bf16("cast_w_in", w_in_r), halves(pool_shard)]}
    flight = {}

    def start(group, after=None):
        flight[group] = gather_start("gather_start_" + group, sent[group], after)

    def relay(group, after):
        send, recv, shards, lands = flight[group]
        shards, lands = gather_wait("gather_wait_" + group, send, recv, shards, lands, after)
        send, recv, lands = relay_start("relay_start_" + group, lands)
        flight[group] = (send, recv, shards, lands)

    def fetch(group, after, then=None):
        send, recv, shards, lands = flight[group]
        lands = relay_wait("relay_wait_" + group, send, recv, lands, after)
        if then is not None:
            then(lands[0])
        return forward_halves("forward_" + group, shards, lands)

    start("in")
    m_in_f, v_in_f, w_go_f, w_o_f, w_up_f, w_dn_f, x_f, wal_f, gng_f = lax.optimization_barrier(
        (m_w_in, v_w_in, w_gla_out, w_out, w_mlp_up, w_mlp_down, x, w_alpha, gla_norm_g, flight["in"][2][0]))[:9]
    m_in_r, v_in_r = m_in_f.reshape(2, D // 2, IN_SHARD), v_in_f.reshape(2, D // 2, IN_SHARD)
    sent["mid"] = [halves(w_go_f[0]), halves(w_o_f[0])]
    relay("in", [m_in_r, v_in_r, *sent["mid"]])
    w_up_f, w_dn_f, x_f, wal_f, gng_f = lax.optimization_barrier(
        (w_up_f, w_dn_f, x_f, wal_f, gng_f, flight["in"][3][0]))[:5]
    sent["up"], sent["down"] = [halves(w_up_f[0])], [halves(w_dn_f[0])]
    x2d = x_f.reshape(T, D)
    big = [w_in_r, w_go_f[0], w_o_f[0], w_up_f[0], w_dn_f[0], pool_shard]

    def tick(point, after):
        if point == "pool":
            relay("mid", after)
            start("down", flight["mid"][3][0])
            return [flight["mid"][3][0], flight["down"][3][0]]

    def get_w(group, after):
        if group == "in":
            after = [after, *sent["up"], *sent["down"], wa_pad]
        if group == "mid":
            relay("up", after)
            after = flight["up"][3][0]
        if group == "up":
            relay("down", after)
            after = flight["down"][3][0]
        if group == "in":
            def next_groups(landed):
                start("mid", landed)
                start("up", flight["mid"][3][0])

            g_in, g_pool = fetch(group, after, next_groups)
            wcat = weights_to_cat(g_in.reshape(NCHIP, D, IN_SHARD), flight["up"][3][0])
            pw = jnp.concatenate([g_pool[j].reshape(4, PG, PO // NCHIP) for j in range(NCHIP)], axis=2)
            return wcat, pw
        whole = fetch(group, after)
        if group == "mid":
            return whole[0].reshape(D, D), whole[1].reshape(D, D)
        if group == "up":
            return whole[0].reshape(NCHIP, D, D)
        return whole[0].reshape(DFF, D)

    small_w = pack_rows("pack_small_w", [wal_f[0].reshape(4, QK),
                                         jnp.concatenate([gng_f[0].reshape(1, 512), jnp.zeros((1, 512), F32)], axis=1)], 8)
    sw_all = gather_small("gather_small_w", small_w, False).reshape(8, 8, QK)
    wa_full = jnp.concatenate([sw_all[2 * j, 0:4].reshape(16, DK) for j in range(NCHIP)], axis=1)
    ng_full = jnp.concatenate([sw_all[2 * j, 4, 0:512].reshape(HEADS, DV // NCHIP) for j in range(NCHIP)], axis=1)
    wa_pad = _pad_rows(wa_full, APAD).astype(BF16)
    ng = ng_full.reshape(1, D)

    pending = {}
    wmv = {"in": (w_in_r, m_in_r, v_in_r), "gla_out": (big[1], m_w_gla_out, v_w_gla_out), "out": (big[2], m_w_out, v_w_out),
           "up": (big[3], m_w_mlp_up, v_w_mlp_up), "down": (big[4], m_w_mlp_down, v_w_mlp_down), "pool": (big[5], m_pool_w, v_pool_w)}
    big_res = {}

    def reduce_group(group, after):
        nms, send, recv, sums, lands = pending[group]
        sums, lands = scatter_wait("scatter_wait_" + group, send, recv, sums, lands, after)
        reduced = [sum_chips("sum_chips_" + nm, a, b, chip_i) for nm, a, b in zip(nms, sums, lands)]
        send, recv, reduced, lands, token = join_start("join_start_" + group, reduced)
        pending[group] = (nms, send, recv, reduced, lands)
        return token

    def update_group(group, after):
        nms, send, recv, reduced, lands = pending[group]
        reduced, from_sib = join_wait("join_wait_" + group, send, recv, reduced, lands, after)
        for nm, g_own, g_sib in zip(nms, reduced, from_sib):
            w, m, v = wmv[nm]
            shp = (2,) + g_own.shape
            big_res[nm] = adamw_halves("adamw_" + nm, w.reshape(shp), g_own, g_sib, m.reshape(shp), v.reshape(shp), core_i)

    def on_grad(group, grads):
        if group == "in":
            gw_in = grads_from_cat(grads["in_cat"])
            gw_pool = jnp.stack([grads["pool"][:, :, j * 128:(j + 1) * 128].reshape(2, 2 * PG, 128)
                                 for j in range(NCHIP)], axis=1)
            grads = {"in": gw_in, "pool": gw_pool}
        nms, parts = list(grads.keys()), list(grads.values())
        send, recv, parts, got, token = exchange_start("exchange_start_" + group, parts)
        pending[group] = (nms, send, recv, parts, got)
        return token

    def on_settle(group, after):
        if group == "in":
            after = reduce_group("down", after)
        nms, send, recv, parts, got = pending[group]
        parts, got = exchange_wait("exchange_wait_" + group, send, recv, parts, got, after)
        sums = [add_pairs("add_pair_" + nm, a, b, core_i) for nm, a, b in zip(nms, parts, got)]
        send, recv, sums, lands, token = scatter_start("scatter_start_" + group, sums)
        pending[group] = (nms, send, recv, sums, lands)
        if group != "in":
            return token
        token = reduce_group("up", token)
        token = reduce_group("mix", token)
        for earlier in ("down", "up", "mix"):
            update_group(earlier, token)
            token = big_res[pending[earlier][0][-1]][1]
        return [big_res[nm][1] for nm in ("down", "up", "out", "gla_out")]

    (loss_local, grad_x, g_mix, g_ps, g_mlp, g_nf, g_ng, g_ba, g_wa) = local_step(
        x2d, tgt, gf, norm_mix_g, pool_scale, wa_pad, b_alpha, ng, norm_mlp_g, get_w, on_grad, on_settle, tick)[:9]
    loss = lax.psum(loss_local, ("x", "y", "c"))
    join_in_token = reduce_group("in", grad_x)

    ROWS = 16

    def wide(a, n):
        return jnp.concatenate([a.reshape(1, n), jnp.zeros((1, D - n), F32)], axis=1)

    packed = pack_rows("pack_small_g", [g_mix, g_ps, g_mlp, g_nf, g_ng, wide(g_ba, QK), g_wa[0:16].reshape(8, D)], ROWS)
    tot = gather_small("reduce_small_g", packed, True, join_in_token)
    t_wa = lax.dynamic_slice(tot[6:14].reshape(16, QK), (0, chip * DK), (16, DK))
    t_ng = lax.dynamic_slice(tot[4].reshape(HEADS, DV), (0, chip * (DV // NCHIP)), (HEADS, DV // NCHIP))

    def pack_small(nm, mix, ps, mlp, nf, ba, wa, gn, after=None):
        return pack_rows(nm, [mix.reshape(1, D), ps.reshape(1, D), mlp.reshape(1, D), nf.reshape(1, D), wide(ba, QK),
                              wa.reshape(2, D), wide(gn, 512)], ROWS, after)

    update_group("in", tot)
    sg = pack_small("pack_g", tot[0], tot[1], tot[2], tot[3], tot[5, 0:QK], t_wa, t_ng, big_res["in"][3])
    sw = pack_small("pack_w", norm_mix_g, pool_scale, norm_mlp_g, norm_final_g, b_alpha, w_alpha, gla_norm_g)
    sm = pack_small("pack_m", m_norm_mix_g, m_pool_scale, m_norm_mlp_g, m_norm_final_g, m_b_alpha, m_w_alpha, m_gla_norm_g)
    sv = pack_small("pack_v", v_norm_mix_g, v_pool_scale, v_norm_mlp_g, v_norm_final_g, v_b_alpha, v_w_alpha, v_gla_norm_g)
    small_res = adamw("adamw_small", sw, sg, sm, sv)

    def unpack(p):
        return {"norm_mix_g": p[0].reshape(1, D), "pool_scale": p[1].reshape(1, D), "norm_mlp_g": p[2].reshape(1, D),
                "norm_final_g": p[3].reshape(D), "b_alpha": p[4, 0:QK].reshape(1, QK), "w_alpha": p[5:7].reshape(1, 16, DK),
                "gla_norm_g": p[7, 0:512].reshape(1, HEADS, DV // NCHIP)}

    order = ["norm_mix_g", "w_in", "pool_w", "pool_scale", "w_alpha", "b_alpha", "gla_norm_g", "w_gla_out", "w_out",
             "norm_mlp_g", "w_mlp_up", "w_mlp_down", "norm_final_g"]
    big_key = {"w_in": ("in", w_in.shape), "pool_w": ("pool", pool_w.shape), "w_gla_out": ("gla_out", w_gla_out.shape),
               "w_out": ("out", w_out.shape), "w_mlp_up": ("up", w_mlp_up.shape), "w_mlp_down": ("down", w_mlp_down.shape)}
    result = [loss, grad_x.reshape(1, T, D)]
    for kind in range(4):
        small = unpack(small_res[kind])
        for nm in order:
            if nm in big_key:
                key, shp = big_key[nm]
                result.append(big_res[key][kind].reshape(shp))
            else:
                result.append(small[nm])
    return tuple(result)
```

```python
import itertools

import jax
import jax.numpy as jnp
from jax import lax
from jax.experimental import pallas as pl
from jax.experimental.pallas import tpu as pltpu

F32 = jnp.float32
BF16 = jnp.bfloat16
SDS = jax.ShapeDtypeStruct
MESH = pl.DeviceIdType.MESH
ANY = pl.BlockSpec(memory_space=pl.ANY)

T = 2048
D = 2048
DFF = 8192
NCHIP = 4
IN_WIDTH = 11280
IN_SHARD = IN_WIDTH // NCHIP
CHUNK = 64
NCHUNK = T // CHUNK
HEADS = 4
DK = 256
DV = 512
QK = HEADS * DK
EPS = 1e-6
POOL_WINDOWS = (2, 4, 8, 16)
PG = 256
PO = 512

OV, OG, OGP, OGG, OU, OQ, OKK, OA = 0, 2048, 4096, 6144, 8192, 9216, 10240, 11264
NCAT = 11520
APAD = 128

VMEM_CAP = 56 * 1024 * 1024

PIECE_BYTES = 384 * 1024

ADAM_LR, ADAM_B1, ADAM_B2, ADAM_EPS, ADAM_WD, ADAM_STEP = 0.001, 0.9, 0.999, 1e-08, 0.01, 10


def _cparams(vmem_bytes=None, sem=None):
    kw = {}
    if vmem_bytes is not None:
        kw["vmem_limit_bytes"] = int(min(max(vmem_bytes, 32 * 1024 * 1024), VMEM_CAP))
    if sem is not None:
        kw["dimension_semantics"] = sem
    return pltpu.CompilerParams(**kw)


def _nbytes(shape, dtype):
    n = 1
    for s in shape:
        if s is not None:
            n *= s
    return n * jnp.dtype(dtype).itemsize


def _sigmoid(x):
    return 0.5 * jnp.tanh(0.5 * x) + 0.5


EPI_COLS = 512


def _as_list(after):
    if after is None:
        return []
    return list(after) if isinstance(after, (list, tuple)) else [after]


def matmul(name, a, b, *, a_spec, b_spec, cdims, grid, acc_shape, outs, extras=(), epi, after=None):
    nj, ni, nk = grid
    ne, no = len(extras), len(outs)
    afters = _as_list(after)
    first_out = 2 + ne + len(afters)

    def body(*refs):
        a_ref, b_ref = refs[0], refs[1]
        ex = refs[2:2 + ne]
        out_refs = refs[first_out:first_out + no]
        i = pl.program_id(1)
        part = lax.dot_general(a_ref[...], b_ref[...], (cdims, ((), ())), preferred_element_type=F32)
        if nk == 1:
            epi(part, ex, out_refs, i)
        else:
            acc_ref = refs[first_out + no]
            k = pl.program_id(2)

            @pl.when(k == 0)
            def _():
                acc_ref[...] = part

            @pl.when(k > 0)
            def _():
                acc_ref[...] += part

            @pl.when(k == nk - 1)
            def _():
                epi(acc_ref[...], ex, out_refs, i)

    in_specs = [pl.BlockSpec(*a_spec), pl.BlockSpec(*b_spec)] + [pl.BlockSpec(bs, im) for _, bs, im in extras]
    in_specs += [ANY] * len(afters)
    out_specs = [pl.BlockSpec(bs, im) for _, _, bs, im in outs]
    out_shape = [SDS(s, dt) for s, dt, _, _ in outs]
    vm = 2 * (_nbytes(a_spec[0], a.dtype) + _nbytes(b_spec[0], b.dtype))
    vm += 2 * sum(_nbytes(bs, arr.dtype) for arr, bs, _ in extras)
    vm += 2 * sum(_nbytes(bs, dt) for _, dt, bs, _ in outs)
    vm += 6 * _nbytes(acc_shape, F32)
    scratch = [pltpu.VMEM(acc_shape, F32)] if nk > 1 else []
    return pl.pallas_call(
        body, name=name, grid=grid, in_specs=in_specs, out_specs=out_specs, out_shape=out_shape,
        scratch_shapes=scratch,
        compiler_params=_cparams(vm, ("arbitrary", "arbitrary", "arbitrary")),
    )(a, b, *[arr for arr, _, _ in extras], *afters)


NN =((1,), (0,))
NT = ((1,), (1,))
TN = ((0,), (0,))


def _row_acc(out_ref, val, i):
    @pl.when(i == 0)
    def _():
        out_ref[...] = val

    @pl.when(i > 0)
    def _():
        out_ref[...] += val


def _rms_bwd(xn, r, dxn):
    return r * (dxn - xn * jnp.mean(dxn * xn, axis=-1, keepdims=True))


def norm1(x, g):
    tm = 256

    def body(x_ref, g_ref, h_ref):
        xv = x_ref[...]
        r = lax.rsqrt(jnp.mean(xv * xv, axis=-1, keepdims=True) + EPS)
        h_ref[...] = (xv * r * g_ref[...]).astype(BF16)

    return pl.pallas_call(
        body, name="norm1", grid=(T // tm,),
        in_specs=[pl.BlockSpec((tm, D), lambda i: (i, 0)), pl.BlockSpec((1, D), lambda i: (0, 0))],
        out_specs=pl.BlockSpec((tm, D), lambda i: (i, 0)), out_shape=SDS((T, D), BF16),
        compiler_params=_cparams(32 * 1024 * 1024, ("arbitrary",)),
    )(x, g)


def mm_in(h1, wcat):
    tm, tn = 1024, 1280

    def epi(acc, ex, outs, i):
        outs[0][...] = acc.astype(BF16)

    return matmul("mm_in", h1, wcat, a_spec=((tm, D), lambda j, i, k: (i, 0)), b_spec=((D, tn), lambda j, i, k: (0, j)),
                  cdims=NN, grid=(NCAT // tn, T // tm, 1), acc_shape=(tm, tn),
                  outs=[((T, NCAT), BF16, (tm, tn), lambda j, i, k: (i, j))], epi=epi)[0]


def _window_sum(x, w, up):
    n = x.shape[0]
    row = lax.broadcasted_iota(jnp.int32, x.shape, 0)
    s, sh = x, 1
    while sh < w:
        if up:
            s = s + jnp.where(row < n - sh, pltpu.roll(s, n - sh, axis=0), 0.0)
        else:
            s = s + jnp.where(row >= sh, pltpu.roll(s, sh, axis=0), 0.0)
        sh *= 2
    return s


def _inv_count(shape, w):
    row = lax.broadcasted_iota(jnp.int32, shape, 0)
    return 1.0 / jnp.minimum(row + 1, w).astype(F32)


def pool_fwd(pcat, pw):
    def body(u_ref, pw_ref, d_ref, y_ref):
        for gi, w in enumerate(POOL_WINDOWS):
            ug = u_ref[:, gi * PG:(gi + 1) * PG].astype(F32)
            dg = _window_sum(ug, w, False) * _inv_count(ug.shape, w) - ug
            db = dg.astype(BF16)
            d_ref[:, gi * PG:(gi + 1) * PG] = db
            y_ref[:, gi * PO:(gi + 1) * PO] = jnp.dot(db, pw_ref[gi], preferred_element_type=F32).astype(BF16)

    return pl.pallas_call(
        body, name="pool_fwd", grid=(1,),
        in_specs=[pl.BlockSpec((T, 4 * PG), lambda i: (0, OU // (4 * PG))), pl.BlockSpec((4, PG, PO), lambda i: (0, 0, 0))],
        out_specs=[pl.BlockSpec((T, 4 * PG), lambda i: (0, 0)), pl.BlockSpec((T, D), lambda i: (0, 0))],
        out_shape=[SDS((T, 4 * PG), BF16), SDS((T, D), BF16)],
        compiler_params=_cparams(48 * 1024 * 1024, ("arbitrary",)),
    )(pcat, pw)


def pool_bwd(dylin, d, pw):
    def body(dy_ref, d_ref, pw_ref, du_ref, dpw_ref):
        for gi, w in enumerate(POOL_WINDOWS):
            dyl = dy_ref[:, gi * PO:(gi + 1) * PO]
            dd = lax.dot_general(dyl, pw_ref[gi], (NT, ((), ())), preferred_element_type=F32)
            du = _window_sum(dd * _inv_count(dd.shape, w), w, True) - dd
            du_ref[:, gi * PG:(gi + 1) * PG] = du.astype(BF16)
            dpw_ref[gi] = lax.dot_general(d_ref[:, gi * PG:(gi + 1) * PG], dyl, (TN, ((), ())),
                                          preferred_element_type=F32).astype(BF16)

    return pl.pallas_call(
        body, name="pool_bwd", grid=(1,),
        in_specs=[pl.BlockSpec((T, D), lambda i: (0, 0)), pl.BlockSpec((T, 4 * PG), lambda i: (0, 0)),
                  pl.BlockSpec((4, PG, PO), lambda i: (0, 0, 0))],
        out_specs=[pl.BlockSpec((T, 4 * PG), lambda i: (0, 0)), pl.BlockSpec((4, PG, PO), lambda i: (0, 0, 0))],
        out_shape=[SDS((T, 4 * PG), BF16), SDS((4, PG, PO), BF16)],
        compiler_params=_cparams(48 * 1024 * 1024, ("arbitrary",)),
    )(dylin, d, pw)


def _gate_decay(alow, wa, ba):
    a = jnp.dot(alow, wa, preferred_element_type=F32) + ba
    ls = jax.nn.log_sigmoid(a) * (1.0 / 16.0)
    r = lax.broadcasted_iota(jnp.int32, (CHUNK, CHUNK), 0)
    c = lax.broadcasted_iota(jnp.int32, (CHUNK, CHUNK), 1)
    tri = jnp.where(c <= r, 1.0, 0.0).astype(F32)
    cum = jnp.dot(tri, ls, preferred_element_type=F32, precision=lax.Precision.HIGHEST)
    last = cum[CHUNK - 1:CHUNK, :]
    return a, jnp.exp(last - cum), jnp.exp(last)


def gla_fwd(pcat, wa, ba, ng, after=None):
    afters = _as_list(after)

    def body(q_ref, k_ref, v_ref, g_ref, al_ref, wa_ref, ba_ref, ng_ref, *rest):
        og_ref, o_ref, st_ref, s_scr = rest[len(afters):]

        @pl.when(pl.program_id(0) == 0)
        def _():
            s_scr[...] = jnp.zeros_like(s_scr)

        _, e, decay = _gate_decay(al_ref[...], wa_ref[...], ba_ref[...])
        kd = (k_ref[...].astype(F32) * e).astype(BF16)
        qs = (q_ref[...].astype(F32) * (DK ** -0.5)).astype(BF16)
        for h in range(HEADS):
            ck = slice(h * DK, (h + 1) * DK)
            cv = slice(h * DV, (h + 1) * DV)
            s_new = s_scr[h] * decay[:, ck] + lax.dot_general(v_ref[:, cv], kd[:, ck], (TN, ((), ())),
                                                               preferred_element_type=F32)
            s_scr[h] = s_new
            sb = s_new.astype(BF16)
            st_ref[h] = sb
            oh = lax.dot_general(qs[:, ck], sb, (NT, ((), ())), preferred_element_type=F32)
            o_ref[:, cv] = oh.astype(BF16)
            on = oh * lax.rsqrt(jnp.mean(oh * oh, axis=-1, keepdims=True) + EPS) * ng_ref[:, cv]
            gv = g_ref[:, cv].astype(F32)
            og_ref[:, cv] = (on * (gv * _sigmoid(gv))).astype(BF16)

    row = lambda c: (c, 0)
    return pl.pallas_call(
        body, name="gla_fwd", grid=(NCHUNK,),
        in_specs=[pl.BlockSpec((CHUNK, QK), lambda c: (c, OQ // QK)), pl.BlockSpec((CHUNK, QK), lambda c: (c, OKK // QK)),
                  pl.BlockSpec((CHUNK, D), lambda c: (c, OV // D)), pl.BlockSpec((CHUNK, D), lambda c: (c, OG // D)),
                  pl.BlockSpec((CHUNK, APAD), lambda c: (c, OA // APAD)),
                  pl.BlockSpec((APAD, QK), lambda c: (0, 0)), pl.BlockSpec((1, QK), lambda c: (0, 0)),
                  pl.BlockSpec((1, D), lambda c: (0, 0))] + [ANY] * len(afters),
        out_specs=[pl.BlockSpec((CHUNK, D), row), pl.BlockSpec((CHUNK, D), row),
                   pl.BlockSpec((None, HEADS, DV, DK), lambda c: (c, 0, 0, 0))],
        out_shape=[SDS((T, D), BF16), SDS((T, D), BF16), SDS((NCHUNK, HEADS, DV, DK), BF16)],
        scratch_shapes=[pltpu.VMEM((HEADS, DV, DK), F32)],
        compiler_params=_cparams(32 * 1024 * 1024, ("arbitrary",)),
    )(pcat, pcat, pcat, pcat, pcat, wa, ba, ng, *afters)


def gla_bwd(do, pcat, states, wa, ba, after):
    def body(do_ref, q_ref, k_ref, v_ref, al_ref, sc_ref, sp_ref, wa_ref, ba_ref, after_ref,
             dq_ref, dk_ref, dv_ref, dal_ref, dwa_ref, dba_ref, ds_scr):
        i = pl.program_id(0)

        @pl.when(i == 0)
        def _():
            ds_scr[...] = jnp.zeros_like(ds_scr)

        has_prev = jnp.where(i < NCHUNK - 1, 1.0, 0.0).astype(F32)
        a, e, decay = _gate_decay(al_ref[...], wa_ref[...], ba_ref[...])
        kf = k_ref[...].astype(F32)
        kdf = kf * e
        kd = kdf.astype(BF16)
        qs = (q_ref[...].astype(F32) * (DK ** -0.5)).astype(BF16)
        dkd_parts, ddecay_parts = [], []
        for h in range(HEADS):
            ck = slice(h * DK, (h + 1) * DK)
            cv = slice(h * DV, (h + 1) * DV)
            doh = do_ref[:, cv]
            ds = ds_scr[h] + lax.dot_general(doh, qs[:, ck], (TN, ((), ())), preferred_element_type=F32)
            dsb = ds.astype(BF16)
            dq_ref[:, ck] = (jnp.dot(doh, sc_ref[h], preferred_element_type=F32) * (DK ** -0.5)).astype(BF16)
            dkd_parts.append(jnp.dot(v_ref[:, cv], dsb, preferred_element_type=F32))
            dv_ref[:, cv] = lax.dot_general(kd[:, ck], dsb, (NT, ((), ())), preferred_element_type=F32).astype(BF16)
            ddecay_parts.append(jnp.sum(ds * sp_ref[h].astype(F32), axis=0, keepdims=True) * has_prev)
            ds_scr[h] = ds * decay[:, ck]
        dkd = jnp.concatenate(dkd_parts, axis=1)
        ddecay = jnp.concatenate(ddecay_parts, axis=1)
        dk_ref[...] = (dkd * e).astype(BF16)
        dearg = dkd * kdf
        dlast = jnp.sum(dearg, axis=0, keepdims=True) + ddecay * decay
        r = lax.broadcasted_iota(jnp.int32, (CHUNK, CHUNK), 0)
        c = lax.broadcasted_iota(jnp.int32, (CHUNK, CHUNK), 1)
        triu = jnp.where(c >= r, 1.0, 0.0).astype(F32)
        dls = dlast - jnp.dot(triu, dearg, preferred_element_type=F32, precision=lax.Precision.HIGHEST)
        da = dls * (1.0 / 16.0) * (1.0 - _sigmoid(a))
        dab = da.astype(BF16)
        dal_ref[...] = lax.dot_general(dab, wa_ref[...], (NT, ((), ())), preferred_element_type=F32).astype(BF16)
        dwa = lax.dot_general(al_ref[...], dab, (TN, ((), ())), preferred_element_type=F32)
        dba = jnp.sum(da, axis=0, keepdims=True)

        @pl.when(i == 0)
        def _():
            dwa_ref[...] = dwa
            dba_ref[...] = dba

        @pl.when(i > 0)
        def _():
            dwa_ref[...] += dwa
            dba_ref[...] += dba

    rev = lambda i: NCHUNK - 1 - i
    return pl.pallas_call(
        body, name="gla_bwd", grid=(NCHUNK,),
        in_specs=[pl.BlockSpec((CHUNK, D), lambda i: (rev(i), 0)),
                  pl.BlockSpec((CHUNK, QK), lambda i: (rev(i), OQ // QK)), pl.BlockSpec((CHUNK, QK), lambda i: (rev(i), OKK // QK)),
                  pl.BlockSpec((CHUNK, D), lambda i: (rev(i), OV // D)), pl.BlockSpec((CHUNK, APAD), lambda i: (rev(i), OA // APAD)),
                  pl.BlockSpec((None, HEADS, DV, DK), lambda i: (rev(i), 0, 0, 0)),
                  pl.BlockSpec((None, HEADS, DV, DK), lambda i: (jnp.maximum(rev(i) - 1, 0), 0, 0, 0)),
                  pl.BlockSpec((APAD, QK), lambda i: (0, 0)), pl.BlockSpec((1, QK), lambda i: (0, 0)), ANY],
        out_specs=[pl.BlockSpec((CHUNK, QK), lambda i: (rev(i), 0)), pl.BlockSpec((CHUNK, QK), lambda i: (rev(i), 0)),
                   pl.BlockSpec((CHUNK, D), lambda i: (rev(i), 0)), pl.BlockSpec((CHUNK, APAD), lambda i: (rev(i), 0)),
                   pl.BlockSpec((APAD, QK), lambda i: (0, 0)), pl.BlockSpec((1, QK), lambda i: (0, 0))],
        out_shape=[SDS((T, QK), BF16), SDS((T, QK), BF16), SDS((T, D), BF16), SDS((T, APAD), BF16),
                   SDS((APAD, QK), F32), SDS((1, QK), F32)],
        scratch_shapes=[pltpu.VMEM((HEADS, DV, DK), F32)],
        compiler_params=_cparams(32 * 1024 * 1024, ("arbitrary",)),
    )(do, pcat, pcat, pcat, pcat, states, states, wa, ba, after)


TMF = 256
TMW = 512
_rowblk = ((TMF, D), lambda j, i, k: (i, 0))
_vec = ((1, D), lambda j, i, k: (0, 0))


def _full_spec(col):
    return ((TMF, D), lambda j, i, k: (i, col))


TBIG = 1024


def square_matmul(name, a, b, *, a_spec, b_spec, cdims, nk, after=None):
    def epi(acc, ex, outs, i):
        outs[0][...] = acc

    return matmul(name, a, b, a_spec=a_spec, b_spec=b_spec, cdims=cdims, grid=(D // TBIG, T // TBIG, nk),
                  acc_shape=(TBIG, TBIG), outs=[((T, D), F32, (TBIG, TBIG), lambda j, i, k: (i, j))], epi=epi,
                  after=after)[0]


def rowwise(name, y, *, extras, outs, epi):
    ne = len(extras)

    def body(*refs):
        epi(refs[0][...], refs[1:1 + ne], refs[1 + ne:], pl.program_id(1))

    in_specs = [pl.BlockSpec(*_rowblk)] + [pl.BlockSpec(bs, im) for _, bs, im in extras]
    return pl.pallas_call(
        body, name=name, grid=(1, T // TMF, 1), in_specs=in_specs,
        out_specs=[pl.BlockSpec(bs, im) for _, _, bs, im in outs], out_shape=[SDS(s, dt) for s, dt, _, _ in outs],
        compiler_params=_cparams(40 * 1024 * 1024, ("arbitrary", "arbitrary", "arbitrary")),
    )(y, *[arr for arr, _, _ in extras])


def mm_gla_out(og, w, ylin, pcat, pscale):
    def epi(acc, ex, outs, i):
        ylin_ref, lgp_ref, lgg_ref, ps_ref = ex
        for c0 in range(0, D, EPI_COLS):
            cs = slice(c0, c0 + EPI_COLS)
            gp = _sigmoid(lgp_ref[:, cs].astype(F32))
            gg = _sigmoid(lgg_ref[:, cs].astype(F32))
            a = acc[:, cs]
            outs[0][:, cs] = (gp * (ylin_ref[:, cs].astype(F32) * ps_ref[:, cs]) + gg * a).astype(BF16)
            outs[1][:, cs] = a.astype(BF16)

    return matmul("mm_gla_out", og, w, a_spec=_rowblk, b_spec=((D, D), lambda j, i, k: (0, 0)), cdims=NN,
                  grid=(1, T // TMF, 1), acc_shape=(TMF, D),
                  extras=[(ylin, *_rowblk), (pcat, *_full_spec(OGP // D)), (pcat, *_full_spec(OGG // D)), (pscale, *_vec)],
                  outs=[((T, D), BF16, *_rowblk), ((T, D), BF16, *_rowblk)], epi=epi)


def mm_out(mixed, w, x, g2):
    def epi(acc, ex, outs, i):
        x_ref, g_ref = ex
        x2 = x_ref[...] + acc
        r = lax.rsqrt(jnp.mean(x2 * x2, axis=-1, keepdims=True) + EPS)
        outs[0][...] = x2
        outs[1][...] = (x2 * r * g_ref[...]).astype(BF16)

    return matmul("mm_out", mixed, w, a_spec=_rowblk, b_spec=((D, D), lambda j, i, k: (0, 0)), cdims=NN,
                  grid=(1, T // TMF, 1), acc_shape=(TMF, D), extras=[(x, *_rowblk), (g2, *_vec)],
                  outs=[((T, D), F32, *_rowblk), ((T, D), BF16, *_rowblk)], epi=epi)


def mm_up(h2, wup):
    def epi(acc, ex, outs, i):
        r = jnp.maximum(acc, 0.0)
        outs[0][...] = r.astype(BF16)
        outs[1][...] = (r * r).astype(BF16)

    blk = ((TMW, D), lambda j, i, k: (i, j))
    return matmul("mm_up", h2, wup, a_spec=((TMW, D), lambda j, i, k: (i, 0)), b_spec=((None, D, D), lambda j, i, k: (j, 0, 0)),
                  cdims=NN, grid=(NCHIP, T // TMW, 1), acc_shape=(TMW, D),
                  outs=[((T, DFF), BF16, *blk), ((T, DFF), BF16, *blk)], epi=epi)


def mm_down(act, wdown, x2, tgt, gf):
    tk = 4096

    def epi(acc, ex, outs, i):
        x2_ref, t_ref, g_ref = ex
        dx_ref, dxb_ref, gnf_ref, loss_ref = outs
        x3 = x2_ref[...] + acc
        r = lax.rsqrt(jnp.mean(x3 * x3, axis=-1, keepdims=True) + EPS)
        xn = x3 * r
        err = xn * g_ref[...] - t_ref[...]
        lsum = 0.5 * jnp.sum(jnp.mean(err * err, axis=-1, keepdims=True), axis=0, keepdims=True)
        dy = err * (1.0 / D)
        _row_acc(gnf_ref, jnp.sum(dy * xn, axis=0, keepdims=True), i)
        _row_acc(loss_ref, jnp.broadcast_to(lsum, (1, 128)), i)
        dx3 = _rms_bwd(xn, r, dy * g_ref[...])
        dx_ref[...] = dx3
        dxb_ref[...] = dx3.astype(BF16)

    y = square_matmul("mm_down", act, wdown, a_spec=((TBIG, tk), lambda j, i, k: (i, k)),
                      b_spec=((tk, TBIG), lambda j, i, k: (k, j)), cdims=NN, nk=DFF // tk)
    return rowwise("rows_final", y, extras=[(x2, *_rowblk), (tgt, *_rowblk), (gf, *_vec)],
                   outs=[((T, D), F32, *_rowblk), ((T, D), BF16, *_rowblk), ((1, D), F32, *_vec),
                         ((1, 128), F32, (1, 128), lambda j, i, k: (0, 0))], epi=epi)


def mm_dact(dx3b, wdown, rup, after=None):
    def epi(acc, ex, outs, i):
        outs[0][...] = (acc * 2.0 * ex[0][...].astype(F32)).astype(BF16)

    blk = ((TMW, D), lambda j, i, k: (i, j))
    return matmul("mm_dact", dx3b, wdown, a_spec=((TMW, D), lambda j, i, k: (i, 0)), b_spec=((D, D), lambda j, i, k: (j, 0)),
                  cdims=NT, grid=(DFF // D, T // TMW, 1), acc_shape=(TMW, D), extras=[(rup, *blk)],
                  outs=[((T, DFF), BF16, *blk)], epi=epi, after=after)[0]


def mm_wgrad(name, a, b, m, n, out_shape, out_block, out_map, tm, tn, after=None):
    def epi(acc, ex, outs, i):
        outs[0][...] = acc.astype(BF16)

    return matmul(name, a, b, a_spec=((T, tm), lambda j, i, k: (0, i)), b_spec=((T, tn), lambda j, i, k: (0, j)),
                  cdims=TN, grid=(n // tn, m // tm, 1), acc_shape=(tm, tn),
                  outs=[(out_shape, BF16, out_block, out_map)], epi=epi, after=after)[0]


def mm_dh2(dup, wup, x2, dx3, g2, after=None):
    def epi(acc, ex, outs, i):
        x2_ref, dx3_ref, g_ref = ex
        x2 = x2_ref[...]
        r = lax.rsqrt(jnp.mean(x2 * x2, axis=-1, keepdims=True) + EPS)
        xn = x2 * r
        _row_acc(outs[2], jnp.sum(acc * xn, axis=0, keepdims=True), i)
        dx2 = dx3_ref[...] + _rms_bwd(xn, r, acc * g_ref[...])
        outs[0][...] = dx2
        outs[1][...] = dx2.astype(BF16)

    y = square_matmul("mm_dh2", dup, wup, a_spec=((TBIG, D), lambda j, i, k: (i, k)),
                      b_spec=((None, TBIG, D), lambda j, i, k: (k, j, 0)), cdims=NT, nk=NCHIP, after=after)
    return rowwise("rows_dh2", y, extras=[(x2, *_rowblk), (dx3, *_rowblk), (g2, *_vec)],
                   outs=[((T, D), F32, *_rowblk), ((T, D), BF16, *_rowblk), ((1, D), F32, *_vec)], epi=epi)


def mm_dmixed(dx2b, wout, pcat, ylin, ygla, pscale, after=None):
    def epi(acc, ex, outs, i):
        lgp_ref, lgg_ref, ylin_ref, ygla_ref, ps_ref = ex
        dps = []
        for c0 in range(0, D, EPI_COLS):
            cs = slice(c0, c0 + EPI_COLS)
            gp = _sigmoid(lgp_ref[:, cs].astype(F32))
            gg = _sigmoid(lgg_ref[:, cs].astype(F32))
            yl = ylin_ref[:, cs].astype(F32)
            ps = ps_ref[:, cs]
            a = acc[:, cs]
            agp = a * gp
            outs[0][:, cs] = (agp * ps).astype(BF16)
            outs[1][:, cs] = (a * gg).astype(BF16)
            outs[2][:, cs] = (agp * (yl * ps) * (1.0 - gp)).astype(BF16)
            outs[3][:, cs] = (a * ygla_ref[:, cs].astype(F32) * gg * (1.0 - gg)).astype(BF16)
            dps.append(jnp.sum(agp * yl, axis=0, keepdims=True))
        _row_acc(outs[4], jnp.concatenate(dps, axis=1), i)

    return matmul("mm_dmixed", dx2b, wout, a_spec=_rowblk, b_spec=((D, D), lambda j, i, k: (0, 0)), cdims=NT,
                  grid=(1, T // TMF, 1), acc_shape=(TMF, D),
                  extras=[(pcat, *_full_spec(OGP // D)), (pcat, *_full_spec(OGG // D)), (ylin, *_rowblk), (ygla, *_rowblk),
                          (pscale, *_vec)],
                  outs=[((T, D), BF16, *_rowblk)] * 4 + [((1, D), F32, *_vec)], epi=epi, after=after)


def mm_dog(dygla, wgo, o, pcat, ng, after=None):
    def epi(acc, ex, outs, i):
        o_ref, g_ref, ng_ref = ex
        do_ref, dg_ref, gng_ref = outs
        gparts = []
        for h in range(HEADS):
            cv = slice(h * DV, (h + 1) * DV)
            oh = o_ref[:, cv].astype(F32)
            r = lax.rsqrt(jnp.mean(oh * oh, axis=-1, keepdims=True) + EPS)
            on = oh * r
            gv = g_ref[:, cv].astype(F32)
            sg = _sigmoid(gv)
            a = acc[:, cv]
            dgain = a * (gv * sg)
            gparts.append(jnp.sum(dgain * on, axis=0, keepdims=True))
            ngh = ng_ref[:, cv]
            do_ref[:, cv] = _rms_bwd(on, r, dgain * ngh).astype(BF16)
            dg_ref[:, cv] = (a * (on * ngh) * (sg * (1.0 + gv * (1.0 - sg)))).astype(BF16)
        _row_acc(gng_ref, jnp.concatenate(gparts, axis=1), i)

    return matmul("mm_dog", dygla, wgo, a_spec=_rowblk, b_spec=((D, D), lambda j, i, k: (0, 0)), cdims=NT,
                  grid=(1, T // TMF, 1), acc_shape=(TMF, D),
                  extras=[(o, *_rowblk), (pcat, *_full_spec(OG // D)), (ng, *_vec)],
                  outs=[((T, D), BF16, *_rowblk), ((T, D), BF16, *_rowblk), ((1, D), F32, *_vec)], epi=epi, after=after)


def mm_dh1(dpcat, wcat, x, dx2, g1, after=None):
    tk = 3840

    def epi(acc, ex, outs, i):
        x_ref, dx2_ref, g_ref = ex
        xv = x_ref[...]
        r = lax.rsqrt(jnp.mean(xv * xv, axis=-1, keepdims=True) + EPS)
        xn = xv * r
        _row_acc(outs[1], jnp.sum(acc * xn, axis=0, keepdims=True), i)
        outs[0][...] = dx2_ref[...] + _rms_bwd(xn, r, acc * g_ref[...])

    y = square_matmul("mm_dh1", dpcat, wcat, a_spec=((TBIG, tk), lambda j, i, k: (i, k)),
                      b_spec=((TBIG, tk), lambda j, i, k: (j, k)), cdims=NT, nk=NCAT // tk, after=after)
    return rowwise("rows_dh1", y, extras=[(x, *_rowblk), (dx2, *_rowblk), (g1, *_vec)],
                   outs=[((T, D), F32, *_rowblk), ((1, D), F32, *_vec)], epi=epi)


def _tile_rows(rows, cols, n_arrays):
    tm = rows
    while tm % 32 == 0 and 2 * n_arrays * tm * cols * 4 > 24 * 1024 * 1024:
        tm //= 2
    return tm


def add_pairs(name, parts, theirs, core):
    _, _, r, c = parts.shape
    tm = _tile_rows(r, c, 3)

    def body(core_ref, a_ref, b_ref, o_ref):
        o_ref[...] = (a_ref[...].astype(F32) + b_ref[...].astype(F32)).astype(BF16)

    spec = pl.BlockSpec((None, tm, c), lambda j, i, core_ref: (j, i, 0))
    grid_spec = pltpu.PrefetchScalarGridSpec(
        num_scalar_prefetch=1, grid=(NCHIP, r // tm),
        in_specs=[pl.BlockSpec((None, None, tm, c), lambda j, i, core_ref: (core_ref[0], j, i, 0)), spec], out_specs=spec)
    return pl.pallas_call(body, name=name, grid_spec=grid_spec, out_shape=SDS((NCHIP, r, c), BF16),
                          compiler_params=_cparams(40 * 1024 * 1024, ("arbitrary", "arbitrary")))(core, parts, theirs)


def sum_chips(name, sums, landed, chip):
    _, r, c = sums.shape
    tm = _tile_rows(r, c, 4)

    def body(chip_ref, own_ref, l_ref, o_ref):
        s = own_ref[...].astype(F32)
        for t in range(NCHIP - 1):
            s = s + l_ref[t].astype(F32)
        o_ref[...] = s

    grid_spec = pltpu.PrefetchScalarGridSpec(
        num_scalar_prefetch=1, grid=(r // tm,),
        in_specs=[pl.BlockSpec((None, tm, c), lambda i, chip_ref: (chip_ref[0], i, 0)),
                  pl.BlockSpec((NCHIP - 1, tm, c), lambda i, chip_ref: (0, i, 0))],
        out_specs=pl.BlockSpec((tm, c), lambda i, chip_ref: (i, 0)))
    return pl.pallas_call(body, name=name, grid_spec=grid_spec, out_shape=SDS((r, c), F32),
                          compiler_params=_cparams(40 * 1024 * 1024, ("arbitrary",)))(chip, sums, landed)


def _adamw_math(wv, gv, mv, vv):
    mn = ADAM_B1 * mv + (1.0 - ADAM_B1) * gv
    vn = ADAM_B2 * vv + (1.0 - ADAM_B2) * (gv * gv)
    mh = mn / (1.0 - ADAM_B1 ** ADAM_STEP)
    vh = vn / (1.0 - ADAM_B2 ** ADAM_STEP)
    return -ADAM_LR * (mh / (jnp.sqrt(vh) + ADAM_EPS) + ADAM_WD * wv), mn, vn


def adamw(name, w, g, m, v):
    def body(w_ref, g_ref, m_ref, v_ref, go_ref, d_ref, mo_ref, vo_ref):
        gv = g_ref[...]
        go_ref[...] = gv
        d_ref[...], mo_ref[...], vo_ref[...] = _adamw_math(w_ref[...], gv, m_ref[...], v_ref[...])

    return pl.pallas_call(body, name=name, out_shape=[SDS(w.shape, F32)] * 4)(w, g, m, v)


def adamw_halves(name, w, g_own, g_sib, m, v, core):
    _, r, c = w.shape
    tm = _tile_rows(r, c, 10)

    def body(core_ref, w_ref, go_ref, gs_ref, m_ref, v_ref, g_out, d_out, m_out, v_out):
        gv = jnp.where(pl.program_id(0) == core_ref[0], go_ref[...], gs_ref[...])
        g_out[...] = gv
        d_out[...], m_out[...], v_out[...] = _adamw_math(w_ref[...], gv, m_ref[...], v_ref[...])

    full = pl.BlockSpec((None, tm, c), lambda h, i, core_ref: (h, i, 0))
    own = pl.BlockSpec((tm, c), lambda h, i, core_ref: (jnp.where(h == core_ref[0], i, 0), 0))
    sib = pl.BlockSpec((tm, c), lambda h, i, core_ref: (jnp.where(h == core_ref[0], 0, i), 0))
    grid_spec = pltpu.PrefetchScalarGridSpec(num_scalar_prefetch=1, grid=(2, r // tm),
                                             in_specs=[full, own, sib, full, full], out_specs=[full] * 4)
    return pl.pallas_call(body, name=name, grid_spec=grid_spec, out_shape=[SDS(w.shape, F32)] * 4,
                          compiler_params=_cparams(48 * 1024 * 1024, ("arbitrary", "arbitrary")))(core, w, g_own, g_sib, m, v)


def cast_bf16(name, w):
    _, r, c = w.shape
    tm = _tile_rows(r, c, 2)

    def body(w_ref, o_ref):
        o_ref[...] = w_ref[...].astype(BF16)

    spec = pl.BlockSpec((None, tm, c), lambda h, i: (h, i, 0))
    return pl.pallas_call(body, name=name, grid=(2, r // tm), in_specs=[spec], out_specs=spec, out_shape=SDS(w.shape, BF16),
                          compiler_params=_cparams(40 * 1024 * 1024, ("arbitrary", "arbitrary")))(w)


def pack_rows(name, parts, rows, after=None):
    width = parts[0].shape[1]
    n = len(parts)
    afters = _as_list(after)

    def body(*refs):
        out_ref = refs[n + len(afters)]
        out_ref[...] = jnp.zeros_like(out_ref)
        off = 0
        for p in refs[:n]:
            out_ref[off:off + p.shape[0], :] = p[...]
            off += p.shape[0]

    vm = pl.BlockSpec(memory_space=pltpu.VMEM)
    return pl.pallas_call(body, name=name, in_specs=[vm] * n + [ANY] * len(afters), out_specs=vm,
                          out_shape=SDS((rows, width), F32))(*parts, *afters)


def _place():
    x, y, c = lax.axis_index("x"), lax.axis_index("y"), lax.axis_index("c")
    chips = [(1 - x, y), (x, 1 - y), (1 - x, 1 - y)]
    return x, y, c, chips


def _row_split(shape, dtype):
    r, c = shape
    n = 1
    while r % (2 * n) == 0 and (r // (2 * n)) % 16 == 0 and (r // n) * c * jnp.dtype(dtype).itemsize > PIECE_BYTES:
        n *= 2
    return [pl.ds(s * (r // n), r // n) for s in range(n)]


def _pieces(ref):
    *lead, r, c = ref.shape
    split = _row_split((r, c), ref.dtype)
    return [ref.at[(*idx, s)] for idx in itertools.product(*[range(d) for d in lead]) for s in split]


HBM = pl.BlockSpec(memory_space=pltpu.HBM)
SEM = pl.BlockSpec(memory_space=pltpu.SEMAPHORE)
EFFECT = pltpu.SideEffectType.DATAFLOW_SIDE_EFFECTING


def gather_start(name, shards, after=None):
    n = len(shards)
    afters = _as_list(after)

    def body(*refs):
        src, land = refs[:n], refs[n:2 * n]
        send, recv = refs[2 * n + len(afters)], refs[2 * n + len(afters) + 1]
        x, y, c, chips = _place()
        me = 2 * x + y
        for a in range(n):
            for j, (cx, cy) in enumerate(chips[:2]):
                for sp, dp in zip(_pieces(src[a].at[c]), _pieces(land[a].at[me, c])):
                    pltpu.make_async_remote_copy(sp, dp, send.at[2 * a + j], recv.at[2 * a + j],
                                                 device_id=(cx, cy, c), device_id_type=MESH).start()

    lands = [pltpu.with_memory_space_constraint(lax.empty((NCHIP,) + s.shape, s.dtype), pltpu.HBM) for s in shards]
    srcs = [pltpu.with_memory_space_constraint(s, pltpu.HBM) for s in shards]
    outs = pl.pallas_call(
        body, name=name,
        out_shape=(pltpu.SemaphoreType.DMA((2 * n,)), pltpu.SemaphoreType.DMA((2 * n,)),
                   *[pltpu.HBM(s.shape, s.dtype) for s in shards], *[pltpu.HBM(l.shape, l.dtype) for l in lands]),
        in_specs=[HBM] * (2 * n) + [ANY] * len(afters), out_specs=(SEM, SEM, *([HBM] * (2 * n))),
        input_output_aliases={i: 2 + i for i in range(2 * n)},
        compiler_params=pltpu.CompilerParams(has_side_effects=EFFECT),
    )(*srcs, *lands, *afters)
    return outs[0], outs[1], list(outs[2:2 + n]), list(outs[2 + n:2 + 2 * n])


def _relay_blocks(land, c, chips):
    (xx, xy), (yx, yy), (dx, dy) = chips
    rows = land.shape[2] // 2
    upper, lower = pl.ds(0, rows), pl.ds(rows, rows)
    return [(land.at[2 * yx + yy, c, lower], land.at[2 * dx + dy, c, lower]),
            (land.at[2 * xx + xy, c, upper], land.at[2 * dx + dy, c, upper])]


def relay_turn(name, send, recv, shards, lands, after):
    n = len(shards)
    afters = _as_list(after)

    def body(*refs):
        src, had = refs[:n], refs[n:2 * n]
        send_ref, recv_ref = refs[2 * n], refs[2 * n + 1]
        rsend, rrecv = refs[2 * n + 2 + len(afters)], refs[2 * n + 3 + len(afters)]
        land = refs[3 * n + 4 + len(afters):4 * n + 4 + len(afters)]
        x, y, c, chips = _place()
        for a in range(n):
            for j, (cx, cy) in enumerate(chips[:2]):
                cp = pltpu.make_async_remote_copy(src[a].at[c], had[a].at[2 * cx + cy, c], send_ref.at[2 * a + j],
                                                  recv_ref.at[2 * a + j], device_id=(cx, cy, c), device_id_type=MESH)
                cp.wait_send()
                cp.wait_recv()
        for a in range(n):
            for j, ((sent, _), (dst, _)) in enumerate(zip(_relay_blocks(had[a], c, chips), _relay_blocks(land[a], c, chips))):
                cx, cy = chips[j]
                for sp, dp in zip(_pieces(sent), _pieces(dst)):
                    pltpu.make_async_remote_copy(sp, dp, rsend.at[2 * a + j], rrecv.at[2 * a + j],
                                                 device_id=(cx, cy, c), device_id_type=MESH).start()

    outs = pl.pallas_call(
        body, name=name,
        out_shape=(pltpu.SemaphoreType.DMA((2 * n,)), pltpu.SemaphoreType.DMA((2 * n,)),
                   *[pltpu.HBM(s.shape, s.dtype) for s in shards], *[pltpu.HBM(l.shape, l.dtype) for l in lands]),
        in_specs=[HBM] * (2 * n) + [SEM, SEM] + [ANY] * len(afters), out_specs=(SEM, SEM, *([HBM] * (2 * n))),
        input_output_aliases={i: 2 + i for i in range(2 * n)},
        compiler_params=pltpu.CompilerParams(has_side_effects=EFFECT),
    )(*shards, *lands, send, recv, *afters)
    return outs[0], outs[1], list(outs[2:2 + n]), list(outs[2 + n:2 + 2 * n])


def relay_wait(name, send, recv, lands, after):
    n = len(lands)
    afters = _as_list(after)

    def body(*refs):
        land = refs[:n]
        send_ref, recv_ref = refs[n], refs[n + 1]
        x, y, c, chips = _place()
        for a in range(n):
            for j, (sent, got) in enumerate(_relay_blocks(land[a], c, chips)):
                cx, cy = chips[j]
                cp = pltpu.make_async_remote_copy(sent, got, send_ref.at[2 * a + j], recv_ref.at[2 * a + j],
                                                  device_id=(cx, cy, c), device_id_type=MESH)
                cp.wait_send()
                cp.wait_recv()

    outs = pl.pallas_call(
        body, name=name, out_shape=tuple(pltpu.HBM(l.shape, l.dtype) for l in lands),
        in_specs=[HBM] * n + [SEM, SEM] + [ANY] * len(afters), out_specs=[HBM] * n,
        input_output_aliases={i: i for i in range(n)},
        compiler_params=pltpu.CompilerParams(has_side_effects=EFFECT),
    )(*lands, send, recv, *afters)
    return list(outs)


def forward_halves(name, shards, lands):
    n = len(lands)

    def body(*refs):
        had, buf = refs[:n], refs[n:2 * n]
        send, recv = refs[2 * n:]
        x, y, c, chips = _place()
        sib = (x, y, 1 - c)
        for a in range(n):
            for j, (cx, cy) in enumerate(chips):
                for sp, dp in zip(_pieces(had[a].at[2 * cx + cy, c]), _pieces(buf[a].at[2 * cx + cy, c])):
                    pltpu.make_async_remote_copy(sp, dp, send.at[3 * a + j], recv.at[3 * a + j], device_id=sib, device_id_type=MESH).start()
        for a in range(n):
            for j, (cx, cy) in enumerate(chips):
                pltpu.make_async_remote_copy(had[a].at[2 * cx + cy, c], buf[a].at[2 * cx + cy, 1 - c], send.at[3 * a + j],
                                             recv.at[3 * a + j], device_id=sib, device_id_type=MESH).wait()

    got = pl.pallas_call(
        body, name=name, in_specs=[ANY] * n, out_specs=[ANY] * n, out_shape=[SDS(l.shape, l.dtype) for l in lands],
        input_output_aliases={i: i for i in range(n)},
        scratch_shapes=[pltpu.SemaphoreType.DMA((3 * n,)), pltpu.SemaphoreType.DMA((3 * n,))],
    )(*lands)
    me = 2 * lax.axis_index("x") + lax.axis_index("y")
    return [lax.dynamic_update_index_in_dim(g, s, me, 0) for g, s in zip(got, shards)]


def exchange_start(name, parts):
    n = len(parts)

    def body(*refs):
        src, got = refs[:n], refs[n:2 * n]
        send, recv = refs[2 * n], refs[2 * n + 1]
        token = refs[4 * n + 2]
        x, y, c, _ = _place()
        sib = (x, y, 1 - c)
        for a in range(n):
            for sp, dp in zip(_pieces(src[a].at[1 - c]), _pieces(got[a])):
                pltpu.make_async_remote_copy(sp, dp, send.at[a], recv.at[a], device_id=sib, device_id_type=MESH).start()
        token[...] = jnp.zeros_like(token)

    lands = [pltpu.with_memory_space_constraint(lax.empty(p.shape[1:], p.dtype), pltpu.HBM) for p in parts]
    srcs = [pltpu.with_memory_space_constraint(p, pltpu.HBM) for p in parts]
    outs = pl.pallas_call(
        body, name=name,
        out_shape=(pltpu.SemaphoreType.DMA((n,)), pltpu.SemaphoreType.DMA((n,)),
                   *[pltpu.HBM(p.shape, p.dtype) for p in parts], *[pltpu.HBM(l.shape, l.dtype) for l in lands],
                   SDS((8, 128), F32)),
        in_specs=[HBM] * (2 * n), out_specs=(SEM, SEM, *([HBM] * (2 * n)), pl.BlockSpec(memory_space=pltpu.VMEM)),
        input_output_aliases={i: 2 + i for i in range(2 * n)},
        compiler_params=pltpu.CompilerParams(has_side_effects=EFFECT),
    )(*srcs, *lands)
    return outs[0], outs[1], list(outs[2:2 + n]), list(outs[2 + n:2 + 2 * n]), outs[2 + 2 * n]


def exchange_wait(name, send, recv, parts, lands, after):
    n = len(parts)
    afters = _as_list(after)

    def body(*refs):
        src, got = refs[:n], refs[n:2 * n]
        send_ref, recv_ref = refs[2 * n], refs[2 * n + 1]
        x, y, c, _ = _place()
        sib = (x, y, 1 - c)
        for a in range(n):
            cp = pltpu.make_async_remote_copy(src[a].at[1 - c], got[a], send_ref.at[a], recv_ref.at[a], device_id=sib, device_id_type=MESH)
            cp.wait_send()
            cp.wait_recv()

    outs = pl.pallas_call(
        body, name=name,
        out_shape=(*[pltpu.HBM(p.shape, p.dtype) for p in parts], *[pltpu.HBM(l.shape, l.dtype) for l in lands]),
        in_specs=[HBM] * (2 * n) + [SEM, SEM] + [ANY] * len(afters), out_specs=[HBM] * (2 * n),
        input_output_aliases={i: i for i in range(2 * n)},
        compiler_params=pltpu.CompilerParams(has_side_effects=EFFECT),
    )(*parts, *lands, send, recv, *afters)
    return list(outs[:n]), list(outs[n:])


def scatter_start(name, parts):
    n = len(parts)

    def body(*refs):
        src, land = refs[:n], refs[n:2 * n]
        send, recv = refs[2 * n], refs[2 * n + 1]
        token = refs[4 * n + 2]
        x, y, c, chips = _place()
        for a in range(n):
            for j, (cx, cy) in enumerate(chips):
                for sp, dp in zip(_pieces(src[a].at[2 * cx + cy]), _pieces(land[a].at[j])):
                    pltpu.make_async_remote_copy(sp, dp, send.at[3 * a + j], recv.at[3 * a + j],
                                                 device_id=(cx, cy, c), device_id_type=MESH).start()
        token[...] = jnp.zeros_like(token)

    lands = [pltpu.with_memory_space_constraint(lax.empty((NCHIP - 1,) + p.shape[1:], p.dtype), pltpu.HBM) for p in parts]
    srcs = [pltpu.with_memory_space_constraint(p, pltpu.HBM) for p in parts]
    outs = pl.pallas_call(
        body, name=name,
        out_shape=(pltpu.SemaphoreType.DMA((3 * n,)), pltpu.SemaphoreType.DMA((3 * n,)),
                   *[pltpu.HBM(p.shape, p.dtype) for p in parts], *[pltpu.HBM(l.shape, l.dtype) for l in lands],
                   SDS((8, 128), F32)),
        in_specs=[HBM] * (2 * n), out_specs=(SEM, SEM, *([HBM] * (2 * n)), pl.BlockSpec(memory_space=pltpu.VMEM)),
        input_output_aliases={i: 2 + i for i in range(2 * n)},
        compiler_params=pltpu.CompilerParams(has_side_effects=EFFECT),
    )(*srcs, *lands)
    return outs[0], outs[1], list(outs[2:2 + n]), list(outs[2 + n:2 + 2 * n]), outs[2 + 2 * n]


def scatter_wait(name, send, recv, parts, lands, after):
    n = len(parts)
    afters = _as_list(after)

    def body(*refs):
        src, land = refs[:n], refs[n:2 * n]
        send_ref, recv_ref = refs[2 * n], refs[2 * n + 1]
        x, y, c, chips = _place()
        for a in range(n):
            for j, (cx, cy) in enumerate(chips):
                cp = pltpu.make_async_remote_copy(src[a].at[2 * cx + cy], land[a].at[j], send_ref.at[3 * a + j], recv_ref.at[3 * a + j],
                                                  device_id=(cx, cy, c), device_id_type=MESH)
                cp.wait_send()
                cp.wait_recv()

    outs = pl.pallas_call(
        body, name=name,
        out_shape=(*[pltpu.HBM(p.shape, p.dtype) for p in parts], *[pltpu.HBM(l.shape, l.dtype) for l in lands]),
        in_specs=[HBM] * (2 * n) + [SEM, SEM] + [ANY] * len(afters), out_specs=[HBM] * (2 * n),
        input_output_aliases={i: i for i in range(2 * n)},
        compiler_params=pltpu.CompilerParams(has_side_effects=EFFECT),
    )(*parts, *lands, send, recv, *afters)
    return list(outs[:n]), list(outs[n:])


def join_start(name, halves):
    n = len(halves)

    def body(*refs):
        src, dst = refs[:n], refs[n:2 * n]
        send, recv = refs[2 * n], refs[2 * n + 1]
        token = refs[4 * n + 2]
        x, y, c, _ = _place()
        sib = (x, y, 1 - c)
        for a in range(n):
            for sp, dp in zip(_pieces(src[a]), _pieces(dst[a])):
                pltpu.make_async_remote_copy(sp, dp, send.at[a], recv.at[a], device_id=sib, device_id_type=MESH).start()
        token[...] = jnp.zeros_like(token)

    lands = [pltpu.with_memory_space_constraint(lax.empty(h.shape, h.dtype), pltpu.HBM) for h in halves]
    srcs = [pltpu.with_memory_space_constraint(h, pltpu.HBM) for h in halves]
    outs = pl.pallas_call(
        body, name=name,
        out_shape=(pltpu.SemaphoreType.DMA((n,)), pltpu.SemaphoreType.DMA((n,)),
                   *[pltpu.HBM(h.shape, h.dtype) for h in halves], *[pltpu.HBM(l.shape, l.dtype) for l in lands],
                   SDS((8, 128), F32)),
        in_specs=[HBM] * (2 * n), out_specs=(SEM, SEM, *([HBM] * (2 * n)), pl.BlockSpec(memory_space=pltpu.VMEM)),
        input_output_aliases={i: 2 + i for i in range(2 * n)},
        compiler_params=pltpu.CompilerParams(has_side_effects=EFFECT),
    )(*srcs, *lands)
    return outs[0], outs[1], list(outs[2:2 + n]), list(outs[2 + n:2 + 2 * n]), outs[2 + 2 * n]


def join_wait(name, send, recv, halves, lands, after):
    n = len(halves)
    afters = _as_list(after)

    def body(*refs):
        src, dst = refs[:n], refs[n:2 * n]
        send_ref, recv_ref = refs[2 * n], refs[2 * n + 1]
        x, y, c, _ = _place()
        sib = (x, y, 1 - c)
        for a in range(n):
            cp = pltpu.make_async_remote_copy(src[a], dst[a], send_ref.at[a], recv_ref.at[a], device_id=sib, device_id_type=MESH)
            cp.wait_send()
            cp.wait_recv()

    outs = pl.pallas_call(
        body, name=name,
        out_shape=(*[pltpu.HBM(h.shape, h.dtype) for h in halves], *[pltpu.HBM(l.shape, l.dtype) for l in lands]),
        in_specs=[HBM] * (2 * n) + [SEM, SEM] + [ANY] * len(afters), out_specs=[HBM] * (2 * n),
        input_output_aliases={i: i for i in range(2 * n)},
        compiler_params=pltpu.CompilerParams(has_side_effects=EFFECT),
    )(*halves, *lands, send, recv, *afters)
    return list(outs[:n]), list(outs[n:])


def gather_small(name, xs, reduce, after=None):
    m, ncol = xs.shape
    afters = _as_list(after)

    def body(x_ref, *rest):
        out_ref, all_ref, send, recv, lsem = rest[len(afters):]
        x, y, c, chips = _place()
        me, sib = (x, y, c), (x, y, 1 - c)

        def rows(px, py, pc):
            return all_ref.at[pl.ds((4 * px + 2 * py + pc) * m, m), :]

        def copy(k, block, to, src=None):
            return pltpu.make_async_remote_copy(rows(*block) if src is None else src, rows(*block), send.at[k], recv.at[k],
                                                device_id=to, device_id_type=MESH)

        mine = pltpu.make_async_copy(x_ref, rows(*me), lsem)
        mine.start()
        first = [copy(0, me, sib, src=x_ref)] + [copy(1 + j, me, (*chip, c), src=x_ref) for j, chip in enumerate(chips)]
        for cp in first:
            cp.start()
        passed = [copy(4 + j, (*chip, c), sib) for j, chip in enumerate(chips)]
        for j, chip in enumerate(chips):
            copy(1 + j, (*chip, c), me).wait_recv()
            passed[j].start()
        copy(0, sib, me).wait_recv()
        for j, chip in enumerate(chips):
            copy(4 + j, (*chip, 1 - c), me).wait_recv()
        for cp in first + passed:
            cp.wait_send()
        mine.wait()
        if reduce:
            s = all_ref[0:m, :]
            for dev in range(1, 8):
                s = s + all_ref[dev * m:(dev + 1) * m, :]
            out_ref[...] = s
        else:
            out_ref[...] = all_ref[...]

    vm = pl.BlockSpec(memory_space=pltpu.VMEM)
    return pl.pallas_call(
        body, name=name, in_specs=[vm] + [ANY] * len(afters), out_specs=vm,
        out_shape=SDS((m, ncol) if reduce else (8 * m, ncol), F32),
        scratch_shapes=[pltpu.VMEM((8 * m, ncol), F32), pltpu.SemaphoreType.DMA((7,)), pltpu.SemaphoreType.DMA((7,)),
                        pltpu.SemaphoreType.DMA],
    )(xs, *afters)


RELAYOUT_ROWS = 128


def weights_to_cat(g_in, after=None):
    tm = RELAYOUT_ROWS
    afters = _as_list(after)

    def body(g_ref, *rest):
        o_ref = rest[len(afters)]
        nat = jnp.concatenate([g_ref[j] for j in range(NCHIP)], axis=1)
        pad = jnp.zeros((tm, NCAT - OA - 16), BF16)
        o_ref[...] = jnp.concatenate([nat[:, 3072:7168], nat[:, 7184:11280], nat[:, 0:3072], nat[:, 7168:7184], pad], axis=1)

    return pl.pallas_call(
        body, name="weights_to_cat", grid=(D // tm,),
        in_specs=[pl.BlockSpec((NCHIP, tm, IN_SHARD), lambda i: (0, i, 0))] + [ANY] * len(afters),
        out_specs=pl.BlockSpec((tm, NCAT), lambda i: (i, 0)), out_shape=SDS((D, NCAT), BF16),
        compiler_params=_cparams(40 * 1024 * 1024, ("arbitrary",)),
    )(g_in, *afters)


def grads_from_cat(gw_cat):
    tm = RELAYOUT_ROWS
    nb = (D // 2) // tm

    def body(c_ref, o_ref):
        cat = c_ref[...]
        nat = jnp.concatenate([cat[:, OU:OA], cat[:, OV:OGP], cat[:, OA:OA + 16], cat[:, OGP:OU]], axis=1)
        for j in range(NCHIP):
            o_ref[j] = nat[:, j * IN_SHARD:(j + 1) * IN_SHARD]

    return pl.pallas_call(
        body, name="grads_from_cat", grid=(D // tm,), in_specs=[pl.BlockSpec((tm, NCAT), lambda i: (i, 0))],
        out_specs=pl.BlockSpec((None, NCHIP, tm, IN_SHARD), lambda i: (i // nb, 0, i % nb, 0)),
        out_shape=SDS((2, NCHIP, D // 2, IN_SHARD), BF16), compiler_params=_cparams(40 * 1024 * 1024, ("arbitrary",)),
    )(gw_cat)


def _pad_rows(a, rows):
    return jnp.concatenate([a, jnp.zeros((rows - a.shape[0],) + a.shape[1:], a.dtype)], axis=0)


def local_step(x2d, tgt, gf, g1, pool_scale, wa_pad, b_alpha, ng, g2, get_w, on_grad=None, on_settle=None, tick=None):
    emit = on_grad if on_grad is not None else (lambda group, grads: None)
    settle = on_settle if on_settle is not None else (lambda group, after: None)
    h1 = norm1(x2d, g1)
    wcat, pw = get_w("in", h1)
    pcat = mm_in(h1, wcat)
    dpool, ylin = pool_fwd(pcat, pw)
    pinned = tick("pool", ylin) if tick is not None else None
    og, o, states = gla_fwd(pcat, wa_pad, b_alpha, ng, pinned)
    w_go, w_o = get_w("mid", og)
    mixed, ygla = mm_gla_out(og, w_go, ylin, pcat, pool_scale)
    x2, h2 = mm_out(mixed, w_o, x2d, g2)
    w_up = get_w("up", h2)
    rup, act = mm_up(h2, w_up)
    w_dn = get_w("down", act)
    dx3, dx3b, g_nf, loss_row = mm_down(act, w_dn, x2, tgt, gf)

    gw_down = mm_wgrad("mm_dw_down", act, dx3b, DFF, D, (2, NCHIP, D // 2, D), (None, None, 512, D),
                       lambda j, i, k: ((i // 2) % 2, i // 4, i % 2, 0), 512, D)
    token = emit("down", {"down": gw_down})
    dup = mm_dact(dx3b, w_dn, rup, after=token)
    token = settle("down", dup)
    dx2, dx2b, g_mlp = mm_dh2(dup, w_up, x2, dx3, g2, after=token)
    gw_up = mm_wgrad("mm_dw_up", h2, dup, D, DFF, (2, NCHIP, D // 2, D), (None, None, 512, D),
                     lambda j, i, k: (i // 2, j, i % 2, 0), 512, D)
    token = emit("up", {"up": gw_up})
    dylin, dygla, dlgp, dlgg, g_ps = mm_dmixed(dx2b, w_o, pcat, ylin, ygla, pool_scale, after=token)
    token = settle("up", dylin)
    gw_out = mm_wgrad("mm_dw_out", mixed, dx2b, D, D, (2, NCHIP, 256, D), (None, None, 256, D),
                      lambda j, i, k: (i % 2, i // 2, 0, 0), 256, D)
    do, dg, g_ng = mm_dog(dygla, w_go, o, pcat, ng, after=token)
    gw_go = mm_wgrad("mm_dw_gla_out", og, dygla, D, D, (2, NCHIP, 256, D), (None, None, 256, D),
                     lambda j, i, k: (i % 2, i // 2, 0, 0), 256, D)
    token = emit("mix", {"out": gw_out, "gla_out": gw_go})
    dq, dk, dv, dalow, g_wa, g_ba = gla_bwd(do, pcat, states, wa_pad, b_alpha, b_alpha if token is None else token)
    token = settle("mix", dq)
    du, dpw = pool_bwd(dylin, dpool, pw)
    dpcat = jnp.concatenate([dv, dg, dlgp, dlgg, du, dq, dk, dalow, jnp.zeros((T, NCAT - OA - APAD), BF16)], axis=1)
    gw_cat = mm_wgrad("mm_dw_in", h1, dpcat, D, NCAT, (D, NCAT), (1024, 1280), lambda j, i, k: (i, j), 1024, 1280, after=token)
    token = settle("in", emit("in", {"in_cat": gw_cat, "pool": dpw}))
    grad_x, g_mix = mm_dh1(dpcat, wcat, x2d, dx2, g1, after=token)
    return (loss_row[0, 0], grad_x, g_mix, g_ps, g_mlp, g_nf, g_ng, g_ba, g_wa, token,
            gw_cat, dpw, gw_go, gw_out, gw_up, gw_down)


def kernel(x, norm_mix_g, w_in, pool_w, pool_scale, w_alpha, b_alpha, gla_norm_g, w_gla_out, w_out, norm_mlp_g, w_mlp_up, w_mlp_down, norm_final_g, loss_target, m_norm_mix_g, m_w_in, m_pool_w, m_pool_scale, m_w_alpha, m_b_alpha, m_gla_norm_g, m_w_gla_out, m_w_out, m_norm_mlp_g, m_w_mlp_up, m_w_mlp_down, m_norm_final_g, v_norm_mix_g, v_w_in, v_pool_w, v_pool_scale, v_w_alpha, v_b_alpha, v_gla_norm_g, v_w_gla_out, v_w_out, v_norm_mlp_g, v_w_mlp_up, v_w_mlp_down, v_norm_final_g):
    chip = 2 * lax.axis_index("x") + lax.axis_index("y")
    chip_i = chip.astype(jnp.int32).reshape(1)
    core_i = lax.axis_index("c").astype(jnp.int32).reshape(1)
    tgt = loss_target.reshape(T, D)
    gf = norm_final_g.reshape(1, D)

    def halves(w2d):
        r, c = w2d.shape
        return w2d.astype(BF16).reshape(2, r // 2, c)

    pool_shard = pool_w.reshape(4 * PG, PO // NCHIP)
    w_in_r = w_in.reshape(2, D // 2, IN_SHARD)
    sent = {"in": [cast_bf16("cast_w_in", w_in_r), halves(pool_shard)]}
    flight = {}

    def start(group, after=None):
        flight[group] = gather_start("gather_start_" + group, sent[group], after)

    def relay(group, after):
        send, recv, shards, lands = flight[group]
        flight[group] = relay_turn("relay_turn_" + group, send, recv, shards, lands, after)

    def fetch(group, after, then=None):
        send, recv, shards, lands = flight[group]
        lands = relay_wait("relay_wait_" + group, send, recv, lands, after)
        if then is not None:
            then(lands[0])
        return forward_halves("forward_" + group, shards, lands)

    start("in")
    m_in_f, v_in_f, w_go_f, w_o_f, w_up_f, w_dn_f, x_f, wal_f, gng_f = lax.optimization_barrier(
        (m_w_in, v_w_in, w_gla_out, w_out, w_mlp_up, w_mlp_down, x, w_alpha, gla_norm_g, flight["in"][2][0]))[:9]
    m_in_r, v_in_r = m_in_f.reshape(2, D // 2, IN_SHARD), v_in_f.reshape(2, D // 2, IN_SHARD)
    sent["mid"] = [halves(w_go_f[0]), halves(w_o_f[0])]
    relay("in", [m_in_r, v_in_r, *sent["mid"]])
    w_up_f, w_dn_f, x_f, wal_f, gng_f = lax.optimization_barrier(
        (w_up_f, w_dn_f, x_f, wal_f, gng_f, flight["in"][3][0]))[:5]
    sent["up"], sent["down"] = [halves(w_up_f[0])], [halves(w_dn_f[0])]
    x2d = x_f.reshape(T, D)
    big = [w_in_r, w_go_f[0], w_o_f[0], w_up_f[0], w_dn_f[0], pool_shard]

    def tick(point, after):
        if point == "pool":
            relay("mid", after)
            start("down", flight["mid"][3][0])
            return [flight["mid"][3][0], flight["down"][3][0]]

    def get_w(group, after):
        if group == "in":
            after = [after, *sent["up"], *sent["down"], wa_pad]
        if group == "mid":
            relay("up", after)
            after = flight["up"][3][0]
        if group == "up":
            relay("down", after)
            after = flight["down"][3][0]
        if group == "in":
            def next_groups(landed):
                start("mid", landed)
                start("up", flight["mid"][3][0])

            g_in, g_pool = fetch(group, after, next_groups)
            wcat = weights_to_cat(g_in.reshape(NCHIP, D, IN_SHARD), flight["up"][3][0])
            pw = jnp.concatenate([g_pool[j].reshape(4, PG, PO // NCHIP) for j in range(NCHIP)], axis=2)
            return wcat, pw
        whole = fetch(group, after)
        if group == "mid":
            return whole[0].reshape(D, D), whole[1].reshape(D, D)
        if group == "up":
            return whole[0].reshape(NCHIP, D, D)
        return whole[0].reshape(DFF, D)

    small_w = pack_rows("pack_small_w", [wal_f[0].reshape(4, QK),
                                         jnp.concatenate([gng_f[0].reshape(1, 512), jnp.zeros((1, 512), F32)], axis=1)], 8)
    sw_all = gather_small("gather_small_w", small_w, False).reshape(8, 8, QK)
    wa_full = jnp.concatenate([sw_all[2 * j, 0:4].reshape(16, DK) for j in range(NCHIP)], axis=1)
    ng_full = jnp.concatenate([sw_all[2 * j, 4, 0:512].reshape(HEADS, DV // NCHIP) for j in range(NCHIP)], axis=1)
    wa_pad = _pad_rows(wa_full, APAD).astype(BF16)
    ng = ng_full.reshape(1, D)

    pending = {}
    wmv = {"in": (w_in_r, m_in_r, v_in_r), "gla_out": (big[1], m_w_gla_out, v_w_gla_out), "out": (big[2], m_w_out, v_w_out),
           "up": (big[3], m_w_mlp_up, v_w_mlp_up), "down": (big[4], m_w_mlp_down, v_w_mlp_down), "pool": (big[5], m_pool_w, v_pool_w)}
    big_res = {}

    def reduce_group(group, after):
        nms, send, recv, sums, lands = pending[group]
        sums, lands = scatter_wait("scatter_wait_" + group, send, recv, sums, lands, after)
        reduced = [sum_chips("sum_chips_" + nm, a, b, chip_i) for nm, a, b in zip(nms, sums, lands)]
        send, recv, reduced, lands, token = join_start("join_start_" + group, reduced)
        pending[group] = (nms, send, recv, reduced, lands)
        return token

    def update_group(group, after):
        nms, send, recv, reduced, lands = pending[group]
        reduced, from_sib = join_wait("join_wait_" + group, send, recv, reduced, lands, after)
        for nm, g_own, g_sib in zip(nms, reduced, from_sib):
            w, m, v = wmv[nm]
            shp = (2,) + g_own.shape
            big_res[nm] = adamw_halves("adamw_" + nm, w.reshape(shp), g_own, g_sib, m.reshape(shp), v.reshape(shp), core_i)

    def on_grad(group, grads):
        if group == "in":
            gw_in = grads_from_cat(grads["in_cat"])
            gw_pool = jnp.stack([grads["pool"][:, :, j * 128:(j + 1) * 128].reshape(2, 2 * PG, 128)
                                 for j in range(NCHIP)], axis=1)
            grads = {"in": gw_in, "pool": gw_pool}
        nms, parts = list(grads.keys()), list(grads.values())
        send, recv, parts, got, token = exchange_start("exchange_start_" + group, parts)
        pending[group] = (nms, send, recv, parts, got)
        return token

    def on_settle(group, after):
        if group == "in":
            after = reduce_group("down", after)
        nms, send, recv, parts, got = pending[group]
        parts, got = exchange_wait("exchange_wait_" + group, send, recv, parts, got, after)
        sums = [add_pairs("add_pair_" + nm, a, b, core_i) for nm, a, b in zip(nms, parts, got)]
        send, recv, sums, lands, token = scatter_start("scatter_start_" + group, sums)
        pending[group] = (nms, send, recv, sums, lands)
        if group != "in":
            return token
        token = reduce_group("up", token)
        token = reduce_group("mix", token)
        for earlier in ("down", "up", "mix"):
            update_group(earlier, token)
            token = big_res[pending[earlier][0][-1]][1]
        return [big_res[nm][1] for nm in ("down", "up", "out", "gla_out")]

    (loss_local, grad_x, g_mix, g_ps, g_mlp, g_nf, g_ng, g_ba, g_wa) = local_step(
        x2d, tgt, gf, norm_mix_g, pool_scale, wa_pad, b_alpha, ng, norm_mlp_g, get_w, on_grad, on_settle, tick)[:9]
    loss = lax.psum(loss_local, ("x", "y", "c"))
    join_in_token = reduce_group("in", grad_x)

    ROWS = 16

    def wide(a, n):
        return jnp.concatenate([a.reshape(1, n), jnp.zeros((1, D - n), F32)], axis=1)

    packed = pack_rows("pack_small_g", [g_mix, g_ps, g_mlp, g_nf, g_ng, wide(g_ba, QK), g_wa[0:16].reshape(8, D)], ROWS)
    tot = gather_small("reduce_small_g", packed, True, join_in_token)
    t_wa = lax.dynamic_slice(tot[6:14].reshape(16, QK), (0, chip * DK), (16, DK))
    t_ng = lax.dynamic_slice(tot[4].reshape(HEADS, DV), (0, chip * (DV // NCHIP)), (HEADS, DV // NCHIP))

    def pack_small(nm, mix, ps, mlp, nf, ba, wa, gn, after=None):
        return pack_rows(nm, [mix.reshape(1, D), ps.reshape(1, D), mlp.reshape(1, D), nf.reshape(1, D), wide(ba, QK),
                              wa.reshape(2, D), wide(gn, 512)], ROWS, after)

    update_group("in", tot)
    sg = pack_small("pack_g", tot[0], tot[1], tot[2], tot[3], tot[5, 0:QK], t_wa, t_ng, big_res["in"][3])
    sw = pack_small("pack_w", norm_mix_g, pool_scale, norm_mlp_g, norm_final_g, b_alpha, w_alpha, gla_norm_g)
    sm = pack_small("pack_m", m_norm_mix_g, m_pool_scale, m_norm_mlp_g, m_norm_final_g, m_b_alpha, m_w_alpha, m_gla_norm_g)
    sv = pack_small("pack_v", v_norm_mix_g, v_pool_scale, v_norm_mlp_g, v_norm_final_g, v_b_alpha, v_w_alpha, v_gla_norm_g)
    small_res = adamw("adamw_small", sw, sg, sm, sv)

    def unpack(p):
        return {"norm_mix_g": p[0].reshape(1, D), "pool_scale": p[1].reshape(1, D), "norm_mlp_g": p[2].reshape(1, D),
                "norm_final_g": p[3].reshape(D), "b_alpha": p[4, 0:QK].reshape(1, QK), "w_alpha": p[5:7].reshape(1, 16, DK),
                "gla_norm_g": p[7, 0:512].reshape(1, HEADS, DV // NCHIP)}

    order = ["norm_mix_g", "w_in", "pool_w", "pool_scale", "w_alpha", "b_alpha", "gla_norm_g", "w_gla_out", "w_out",
             "norm_mlp_g", "w_mlp_up", "w_mlp_down", "norm_final_g"]
    big_key = {"w_in": ("in", w_in.shape), "pool_w": ("pool", pool_w.shape), "w_gla_out": ("gla_out", w_gla_out.shape),
               "w_out": ("out", w_out.shape), "w_mlp_up": ("up", w_mlp_up.shape), "w_mlp_down": ("down", w_mlp_down.shape)}
    result = [loss, grad_x.reshape(1, T, D)]
    for kind in range(4):
        small = unpack(small_res[kind])
        for nm in order:
            if nm in big_key:
                key, shp = big_key[nm]
                result.append(big_res[key][kind].reshape(shp))
            else:
                result.append(small[nm])
    return tuple(result)
```

```python
import itertools

import jax
import jax.numpy as jnp
from jax import lax
from jax.experimental import pallas as pl
from jax.experimental.pallas import tpu as pltpu

F32 = jnp.float32
BF16 = jnp.bfloat16
SDS = jax.ShapeDtypeStruct
MESH = pl.DeviceIdType.MESH
ANY = pl.BlockSpec(memory_space=pl.ANY)

T = 2048
D = 2048
DFF = 8192
NCHIP = 4
IN_WIDTH = 11280
IN_SHARD = IN_WIDTH // NCHIP
CHUNK = 64
NCHUNK = T // CHUNK
HEADS = 4
DK = 256
DV = 512
QK = HEADS * DK
EPS = 1e-6
POOL_WINDOWS = (2, 4, 8, 16)
PG = 256
PO = 512

OV, OG, OGP, OGG, OU, OQ, OKK, OA = 0, 2048, 4096, 6144, 8192, 9216, 10240, 11264
NCAT = 11520
APAD = 128

VMEM_CAP = 56 * 1024 * 1024

PIECE_BYTES = 384 * 1024

ADAM_LR, ADAM_B1, ADAM_B2, ADAM_EPS, ADAM_WD, ADAM_STEP = 0.001, 0.9, 0.999, 1e-08, 0.01, 10


def _cparams(vmem_bytes=None, sem=None):
    kw = {}
    if vmem_bytes is not None:
        kw["vmem_limit_bytes"] = int(min(max(vmem_bytes, 32 * 1024 * 1024), VMEM_CAP))
    if sem is not None:
        kw["dimension_semantics"] = sem
    return pltpu.CompilerParams(**kw)


def _nbytes(shape, dtype):
    n = 1
    for s in shape:
        if s is not None:
            n *= s
    return n * jnp.dtype(dtype).itemsize


def _sigmoid(x):
    return 0.5 * jnp.tanh(0.5 * x) + 0.5


EPI_COLS = 512


def _as_list(after):
    if after is None:
        return []
    return list(after) if isinstance(after, (list, tuple)) else [after]


def matmul(name, a, b, *, a_spec, b_spec, cdims, grid, acc_shape, outs, extras=(), epi, after=None):
    nj, ni, nk = grid
    ne, no = len(extras), len(outs)
    afters = _as_list(after)
    first_out = 2 + ne + len(afters)

    def body(*refs):
        a_ref, b_ref = refs[0], refs[1]
        ex = refs[2:2 + ne]
        out_refs = refs[first_out:first_out + no]
        i = pl.program_id(1)
        part = lax.dot_general(a_ref[...], b_ref[...], (cdims, ((), ())), preferred_element_type=F32)
        if nk == 1:
            epi(part, ex, out_refs, i)
        else:
            acc_ref = refs[first_out + no]
            k = pl.program_id(2)

            @pl.when(k == 0)
            def _():
                acc_ref[...] = part

            @pl.when(k > 0)
            def _():
                acc_ref[...] += part

            @pl.when(k == nk - 1)
            def _():
                epi(acc_ref[...], ex, out_refs, i)

    in_specs = [pl.BlockSpec(*a_spec), pl.BlockSpec(*b_spec)] + [pl.BlockSpec(bs, im) for _, bs, im in extras]
    in_specs += [ANY] * len(afters)
    out_specs = [pl.BlockSpec(bs, im) for _, _, bs, im in outs]
    out_shape = [SDS(s, dt) for s, dt, _, _ in outs]
    vm = 2 * (_nbytes(a_spec[0], a.dtype) + _nbytes(b_spec[0], b.dtype))
    vm += 2 * sum(_nbytes(bs, arr.dtype) for arr, bs, _ in extras)
    vm += 2 * sum(_nbytes(bs, dt) for _, dt, bs, _ in outs)
    vm += 6 * _nbytes(acc_shape, F32)
    scratch = [pltpu.VMEM(acc_shape, F32)] if nk > 1 else []
    return pl.pallas_call(
        body, name=name, grid=grid, in_specs=in_specs, out_specs=out_specs, out_shape=out_shape,
        scratch_shapes=scratch,
        compiler_params=_cparams(vm, ("arbitrary", "arbitrary", "arbitrary")),
    )(a, b, *[arr for arr, _, _ in extras], *afters)


NN =((1,), (0,))
NT = ((1,), (1,))
TN = ((0,), (0,))


def _row_acc(out_ref, val, i):
    @pl.when(i == 0)
    def _():
        out_ref[...] = val

    @pl.when(i > 0)
    def _():
        out_ref[...] += val


def _rms_bwd(xn, r, dxn):
    return r * (dxn - xn * jnp.mean(dxn * xn, axis=-1, keepdims=True))


def norm1(x, g):
    tm = 256

    def body(x_ref, g_ref, h_ref):
        xv = x_ref[...]
        r = lax.rsqrt(jnp.mean(xv * xv, axis=-1, keepdims=True) + EPS)
        h_ref[...] = (xv * r * g_ref[...]).astype(BF16)

    return pl.pallas_call(
        body, name="norm1", grid=(T // tm,),
        in_specs=[pl.BlockSpec((tm, D), lambda i: (i, 0)), pl.BlockSpec((1, D), lambda i: (0, 0))],
        out_specs=pl.BlockSpec((tm, D), lambda i: (i, 0)), out_shape=SDS((T, D), BF16),
        compiler_params=_cparams(32 * 1024 * 1024, ("arbitrary",)),
    )(x, g)


def mm_in(h1, wcat):
    tm, tn = 1024, 1280

    def epi(acc, ex, outs, i):
        outs[0][...] = acc.astype(BF16)

    return matmul("mm_in", h1, wcat, a_spec=((tm, D), lambda j, i, k: (i, 0)), b_spec=((D, tn), lambda j, i, k: (0, j)),
                  cdims=NN, grid=(NCAT // tn, T // tm, 1), acc_shape=(tm, tn),
                  outs=[((T, NCAT), BF16, (tm, tn), lambda j, i, k: (i, j))], epi=epi)[0]


def _window_sum(x, w, up):
    n = x.shape[0]
    row = lax.broadcasted_iota(jnp.int32, x.shape, 0)
    s, sh = x, 1
    while sh < w:
        if up:
            s = s + jnp.where(row < n - sh, pltpu.roll(s, n - sh, axis=0), 0.0)
        else:
            s = s + jnp.where(row >= sh, pltpu.roll(s, sh, axis=0), 0.0)
        sh *= 2
    return s


def _inv_count(shape, w):
    row = lax.broadcasted_iota(jnp.int32, shape, 0)
    return 1.0 / jnp.minimum(row + 1, w).astype(F32)


def pool_fwd(pcat, pw):
    def body(u_ref, pw_ref, d_ref, y_ref):
        for gi, w in enumerate(POOL_WINDOWS):
            ug = u_ref[:, gi * PG:(gi + 1) * PG].astype(F32)
            dg = _window_sum(ug, w, False) * _inv_count(ug.shape, w) - ug
            db = dg.astype(BF16)
            d_ref[:, gi * PG:(gi + 1) * PG] = db
            y_ref[:, gi * PO:(gi + 1) * PO] = jnp.dot(db, pw_ref[gi], preferred_element_type=F32).astype(BF16)

    return pl.pallas_call(
        body, name="pool_fwd", grid=(1,),
        in_specs=[pl.BlockSpec((T, 4 * PG), lambda i: (0, OU // (4 * PG))), pl.BlockSpec((4, PG, PO), lambda i: (0, 0, 0))],
        out_specs=[pl.BlockSpec((T, 4 * PG), lambda i: (0, 0)), pl.BlockSpec((T, D), lambda i: (0, 0))],
        out_shape=[SDS((T, 4 * PG), BF16), SDS((T, D), BF16)],
        compiler_params=_cparams(48 * 1024 * 1024, ("arbitrary",)),
    )(pcat, pw)


def pool_bwd(dylin, d, pw):
    def body(dy_ref, d_ref, pw_ref, du_ref, dpw_ref):
        for gi, w in enumerate(POOL_WINDOWS):
            dyl = dy_ref[:, gi * PO:(gi + 1) * PO]
            dd = lax.dot_general(dyl, pw_ref[gi], (NT, ((), ())), preferred_element_type=F32)
            du = _window_sum(dd * _inv_count(dd.shape, w), w, True) - dd
            du_ref[:, gi * PG:(gi + 1) * PG] = du.astype(BF16)
            dpw_ref[gi] = lax.dot_general(d_ref[:, gi * PG:(gi + 1) * PG], dyl, (TN, ((), ())),
                                          preferred_element_type=F32).astype(BF16)

    return pl.pallas_call(
        body, name="pool_bwd", grid=(1,),
        in_specs=[pl.BlockSpec((T, D), lambda i: (0, 0)), pl.BlockSpec((T, 4 * PG), lambda i: (0, 0)),
                  pl.BlockSpec((4, PG, PO), lambda i: (0, 0, 0))],
        out_specs=[pl.BlockSpec((T, 4 * PG), lambda i: (0, 0)), pl.BlockSpec((4, PG, PO), lambda i: (0, 0, 0))],
        out_shape=[SDS((T, 4 * PG), BF16), SDS((4, PG, PO), BF16)],
        compiler_params=_cparams(48 * 1024 * 1024, ("arbitrary",)),
    )(dylin, d, pw)


def _gate_decay(alow, wa, ba):
    a = jnp.dot(alow, wa, preferred_element_type=F32) + ba
    ls = jax.nn.log_sigmoid(a) * (1.0 / 16.0)
    r = lax.broadcasted_iota(jnp.int32, (CHUNK, CHUNK), 0)
    c = lax.broadcasted_iota(jnp.int32, (CHUNK, CHUNK), 1)
    tri = jnp.where(c <= r, 1.0, 0.0).astype(F32)
    cum = jnp.dot(tri, ls, preferred_element_type=F32, precision=lax.Precision.HIGHEST)
    last = cum[CHUNK - 1:CHUNK, :]
    return a, jnp.exp(last - cum), jnp.exp(last)


def gla_fwd(pcat, wa, ba, ng, after=None):
    afters = _as_list(after)

    def body(q_ref, k_ref, v_ref, g_ref, al_ref, wa_ref, ba_ref, ng_ref, *rest):
        og_ref, o_ref, st_ref, s_scr = rest[len(afters):]

        @pl.when(pl.program_id(0) == 0)
        def _():
            s_scr[...] = jnp.zeros_like(s_scr)

        _, e, decay = _gate_decay(al_ref[...], wa_ref[...], ba_ref[...])
        kd = (k_ref[...].astype(F32) * e).astype(BF16)
        qs = (q_ref[...].astype(F32) * (DK ** -0.5)).astype(BF16)
        for h in range(HEADS):
            ck = slice(h * DK, (h + 1) * DK)
            cv = slice(h * DV, (h + 1) * DV)
            s_new = s_scr[h] * decay[:, ck] + lax.dot_general(v_ref[:, cv], kd[:, ck], (TN, ((), ())),
                                                               preferred_element_type=F32)
            s_scr[h] = s_new
            sb = s_new.astype(BF16)
            st_ref[h] = sb
            oh = lax.dot_general(qs[:, ck], sb, (NT, ((), ())), preferred_element_type=F32)
            o_ref[:, cv] = oh.astype(BF16)
            on = oh * lax.rsqrt(jnp.mean(oh * oh, axis=-1, keepdims=True) + EPS) * ng_ref[:, cv]
            gv = g_ref[:, cv].astype(F32)
            og_ref[:, cv] = (on * (gv * _sigmoid(gv))).astype(BF16)

    row = lambda c: (c, 0)
    return pl.pallas_call(
        body, name="gla_fwd", grid=(NCHUNK,),
        in_specs=[pl.BlockSpec((CHUNK, QK), lambda c: (c, OQ // QK)), pl.BlockSpec((CHUNK, QK), lambda c: (c, OKK // QK)),
                  pl.BlockSpec((CHUNK, D), lambda c: (c, OV // D)), pl.BlockSpec((CHUNK, D), lambda c: (c, OG // D)),
                  pl.BlockSpec((CHUNK, APAD), lambda c: (c, OA // APAD)),
                  pl.BlockSpec((APAD, QK), lambda c: (0, 0)), pl.BlockSpec((1, QK), lambda c: (0, 0)),
                  pl.BlockSpec((1, D), lambda c: (0, 0))] + [ANY] * len(afters),
        out_specs=[pl.BlockSpec((CHUNK, D), row), pl.BlockSpec((CHUNK, D), row),
                   pl.BlockSpec((None, HEADS, DV, DK), lambda c: (c, 0, 0, 0))],
        out_shape=[SDS((T, D), BF16), SDS((T, D), BF16), SDS((NCHUNK, HEADS, DV, DK), BF16)],
        scratch_shapes=[pltpu.VMEM((HEADS, DV, DK), F32)],
        compiler_params=_cparams(32 * 1024 * 1024, ("arbitrary",)),
    )(pcat, pcat, pcat, pcat, pcat, wa, ba, ng, *afters)


def gla_bwd(do, pcat, states, wa, ba, after):
    def body(do_ref, q_ref, k_ref, v_ref, al_ref, sc_ref, sp_ref, wa_ref, ba_ref, after_ref,
             dq_ref, dk_ref, dv_ref, dal_ref, dwa_ref, dba_ref, ds_scr):
        i = pl.program_id(0)

        @pl.when(i == 0)
        def _():
            ds_scr[...] = jnp.zeros_like(ds_scr)

        has_prev = jnp.where(i < NCHUNK - 1, 1.0, 0.0).astype(F32)
        a, e, decay = _gate_decay(al_ref[...], wa_ref[...], ba_ref[...])
        kf = k_ref[...].astype(F32)
        kdf = kf * e
        kd = kdf.astype(BF16)
        qs = (q_ref[...].astype(F32) * (DK ** -0.5)).astype(BF16)
        dkd_parts, ddecay_parts = [], []
        for h in range(HEADS):
            ck = slice(h * DK, (h + 1) * DK)
            cv = slice(h * DV, (h + 1) * DV)
            doh = do_ref[:, cv]
            ds = ds_scr[h] + lax.dot_general(doh, qs[:, ck], (TN, ((), ())), preferred_element_type=F32)
            dsb = ds.astype(BF16)
            dq_ref[:, ck] = (jnp.dot(doh, sc_ref[h], preferred_element_type=F32) * (DK ** -0.5)).astype(BF16)
            dkd_parts.append(jnp.dot(v_ref[:, cv], dsb, preferred_element_type=F32))
            dv_ref[:, cv] = lax.dot_general(kd[:, ck], dsb, (NT, ((), ())), preferred_element_type=F32).astype(BF16)
            ddecay_parts.append(jnp.sum(ds * sp_ref[h].astype(F32), axis=0, keepdims=True) * has_prev)
            ds_scr[h] = ds * decay[:, ck]
        dkd = jnp.concatenate(dkd_parts, axis=1)
        ddecay = jnp.concatenate(ddecay_parts, axis=1)
        dk_ref[...] = (dkd * e).astype(BF16)
        dearg = dkd * kdf
        dlast = jnp.sum(dearg, axis=0, keepdims=True) + ddecay * decay
        r = lax.broadcasted_iota(jnp.int32, (CHUNK, CHUNK), 0)
        c = lax.broadcasted_iota(jnp.int32, (CHUNK, CHUNK), 1)
        triu = jnp.where(c >= r, 1.0, 0.0).astype(F32)
        dls = dlast - jnp.dot(triu, dearg, preferred_element_type=F32, precision=lax.Precision.HIGHEST)
        da = dls * (1.0 / 16.0) * (1.0 - _sigmoid(a))
        dab = da.astype(BF16)
        dal_ref[...] = lax.dot_general(dab, wa_ref[...], (NT, ((), ())), preferred_element_type=F32).astype(BF16)
        dwa = lax.dot_general(al_ref[...], dab, (TN, ((), ())), preferred_element_type=F32)
        dba = jnp.sum(da, axis=0, keepdims=True)

        @pl.when(i == 0)
        def _():
            dwa_ref[...] = dwa
            dba_ref[...] = dba

        @pl.when(i > 0)
        def _():
            dwa_ref[...] += dwa
            dba_ref[...] += dba

    rev = lambda i: NCHUNK - 1 - i
    return pl.pallas_call(
        body, name="gla_bwd", grid=(NCHUNK,),
        in_specs=[pl.BlockSpec((CHUNK, D), lambda i: (rev(i), 0)),
                  pl.BlockSpec((CHUNK, QK), lambda i: (rev(i), OQ // QK)), pl.BlockSpec((CHUNK, QK), lambda i: (rev(i), OKK // QK)),
                  pl.BlockSpec((CHUNK, D), lambda i: (rev(i), OV // D)), pl.BlockSpec((CHUNK, APAD), lambda i: (rev(i), OA // APAD)),
                  pl.BlockSpec((None, HEADS, DV, DK), lambda i: (rev(i), 0, 0, 0)),
                  pl.BlockSpec((None, HEADS, DV, DK), lambda i: (jnp.maximum(rev(i) - 1, 0), 0, 0, 0)),
                  pl.BlockSpec((APAD, QK), lambda i: (0, 0)), pl.BlockSpec((1, QK), lambda i: (0, 0)), ANY],
        out_specs=[pl.BlockSpec((CHUNK, QK), lambda i: (rev(i), 0)), pl.BlockSpec((CHUNK, QK), lambda i: (rev(i), 0)),
                   pl.BlockSpec((CHUNK, D), lambda i: (rev(i), 0)), pl.BlockSpec((CHUNK, APAD), lambda i: (rev(i), 0)),
                   pl.BlockSpec((APAD, QK), lambda i: (0, 0)), pl.BlockSpec((1, QK), lambda i: (0, 0))],
        out_shape=[SDS((T, QK), BF16), SDS((T, QK), BF16), SDS((T, D), BF16), SDS((T, APAD), BF16),
                   SDS((APAD, QK), F32), SDS((1, QK), F32)],
        scratch_shapes=[pltpu.VMEM((HEADS, DV, DK), F32)],
        compiler_params=_cparams(32 * 1024 * 1024, ("arbitrary",)),
    )(do, pcat, pcat, pcat, pcat, states, states, wa, ba, after)


TMF = 256
TMW = 512
_rowblk = ((TMF, D), lambda j, i, k: (i, 0))
_vec = ((1, D), lambda j, i, k: (0, 0))


def _full_spec(col):
    return ((TMF, D), lambda j, i, k: (i, col))


TBIG = 1024


def square_matmul(name, a, b, *, a_spec, b_spec, cdims, nk, after=None):
    def epi(acc, ex, outs, i):
        outs[0][...] = acc

    return matmul(name, a, b, a_spec=a_spec, b_spec=b_spec, cdims=cdims, grid=(D // TBIG, T // TBIG, nk),
                  acc_shape=(TBIG, TBIG), outs=[((T, D), F32, (TBIG, TBIG), lambda j, i, k: (i, j))], epi=epi,
                  after=after)[0]


def rowwise(name, y, *, extras, outs, epi):
    ne = len(extras)

    def body(*refs):
        epi(refs[0][...], refs[1:1 + ne], refs[1 + ne:], pl.program_id(1))

    in_specs = [pl.BlockSpec(*_rowblk)] + [pl.BlockSpec(bs, im) for _, bs, im in extras]
    return pl.pallas_call(
        body, name=name, grid=(1, T // TMF, 1), in_specs=in_specs,
        out_specs=[pl.BlockSpec(bs, im) for _, _, bs, im in outs], out_shape=[SDS(s, dt) for s, dt, _, _ in outs],
        compiler_params=_cparams(40 * 1024 * 1024, ("arbitrary", "arbitrary", "arbitrary")),
    )(y, *[arr for arr, _, _ in extras])


def mm_gla_out(og, w, ylin, pcat, pscale):
    def epi(acc, ex, outs, i):
        ylin_ref, lgp_ref, lgg_ref, ps_ref = ex
        for c0 in range(0, D, EPI_COLS):
            cs = slice(c0, c0 + EPI_COLS)
            gp = _sigmoid(lgp_ref[:, cs].astype(F32))
            gg = _sigmoid(lgg_ref[:, cs].astype(F32))
            a = acc[:, cs]
            outs[0][:, cs] = (gp * (ylin_ref[:, cs].astype(F32) * ps_ref[:, cs]) + gg * a).astype(BF16)
            outs[1][:, cs] = a.astype(BF16)

    return matmul("mm_gla_out", og, w, a_spec=_rowblk, b_spec=((D, D), lambda j, i, k: (0, 0)), cdims=NN,
                  grid=(1, T // TMF, 1), acc_shape=(TMF, D),
                  extras=[(ylin, *_rowblk), (pcat, *_full_spec(OGP // D)), (pcat, *_full_spec(OGG // D)), (pscale, *_vec)],
                  outs=[((T, D), BF16, *_rowblk), ((T, D), BF16, *_rowblk)], epi=epi)


def mm_out(mixed, w, x, g2):
    def epi(acc, ex, outs, i):
        x_ref, g_ref = ex
        x2 = x_ref[...] + acc
        r = lax.rsqrt(jnp.mean(x2 * x2, axis=-1, keepdims=True) + EPS)
        outs[0][...] = x2
        outs[1][...] = (x2 * r * g_ref[...]).astype(BF16)

    return matmul("mm_out", mixed, w, a_spec=_rowblk, b_spec=((D, D), lambda j, i, k: (0, 0)), cdims=NN,
                  grid=(1, T // TMF, 1), acc_shape=(TMF, D), extras=[(x, *_rowblk), (g2, *_vec)],
                  outs=[((T, D), F32, *_rowblk), ((T, D), BF16, *_rowblk)], epi=epi)


def mm_up(h2, wup):
    def epi(acc, ex, outs, i):
        r = jnp.maximum(acc, 0.0)
        outs[0][...] = r.astype(BF16)
        outs[1][...] = (r * r).astype(BF16)

    blk = ((TMW, D), lambda j, i, k: (i, j))
    return matmul("mm_up", h2, wup, a_spec=((TMW, D), lambda j, i, k: (i, 0)), b_spec=((None, D, D), lambda j, i, k: (j, 0, 0)),
                  cdims=NN, grid=(NCHIP, T // TMW, 1), acc_shape=(TMW, D),
                  outs=[((T, DFF), BF16, *blk), ((T, DFF), BF16, *blk)], epi=epi)


def mm_down(act, wdown, x2, tgt, gf):
    tk = 4096

    def epi(acc, ex, outs, i):
        x2_ref, t_ref, g_ref = ex
        dx_ref, dxb_ref, gnf_ref, loss_ref = outs
        x3 = x2_ref[...] + acc
        r = lax.rsqrt(jnp.mean(x3 * x3, axis=-1, keepdims=True) + EPS)
        xn = x3 * r
        err = xn * g_ref[...] - t_ref[...]
        lsum = 0.5 * jnp.sum(jnp.mean(err * err, axis=-1, keepdims=True), axis=0, keepdims=True)
        dy = err * (1.0 / D)
        _row_acc(gnf_ref, jnp.sum(dy * xn, axis=0, keepdims=True), i)
        _row_acc(loss_ref, jnp.broadcast_to(lsum, (1, 128)), i)
        dx3 = _rms_bwd(xn, r, dy * g_ref[...])
        dx_ref[...] = dx3
        dxb_ref[...] = dx3.astype(BF16)

    y = square_matmul("mm_down", act, wdown, a_spec=((TBIG, tk), lambda j, i, k: (i, k)),
                      b_spec=((tk, TBIG), lambda j, i, k: (k, j)), cdims=NN, nk=DFF // tk)
    return rowwise("rows_final", y, extras=[(x2, *_rowblk), (tgt, *_rowblk), (gf, *_vec)],
                   outs=[((T, D), F32, *_rowblk), ((T, D), BF16, *_rowblk), ((1, D), F32, *_vec),
                         ((1, 128), F32, (1, 128), lambda j, i, k: (0, 0))], epi=epi)


def mm_dact(dx3b, wdown, rup, after=None):
    def epi(acc, ex, outs, i):
        outs[0][...] = (acc * 2.0 * ex[0][...].astype(F32)).astype(BF16)

    blk = ((TMW, D), lambda j, i, k: (i, j))
    return matmul("mm_dact", dx3b, wdown, a_spec=((TMW, D), lambda j, i, k: (i, 0)), b_spec=((D, D), lambda j, i, k: (j, 0)),
                  cdims=NT, grid=(DFF // D, T // TMW, 1), acc_shape=(TMW, D), extras=[(rup, *blk)],
                  outs=[((T, DFF), BF16, *blk)], epi=epi, after=after)[0]


def mm_wgrad(name, a, b, m, n, out_shape, out_block, out_map, tm, tn, after=None):
    def epi(acc, ex, outs, i):
        outs[0][...] = acc.astype(BF16).reshape(outs[0].shape)

    return matmul(name, a, b, a_spec=((T, tm), lambda j, i, k: (0, i)), b_spec=((T, tn), lambda j, i, k: (0, j)),
                  cdims=TN, grid=(n // tn, m // tm, 1), acc_shape=(tm, tn),
                  outs=[(out_shape, BF16, out_block, out_map)], epi=epi, after=after)[0]


def mm_dh2(dup, wup, x2, dx3, g2, after=None):
    def epi(acc, ex, outs, i):
        x2_ref, dx3_ref, g_ref = ex
        x2 = x2_ref[...]
        r = lax.rsqrt(jnp.mean(x2 * x2, axis=-1, keepdims=True) + EPS)
        xn = x2 * r
        _row_acc(outs[2], jnp.sum(acc * xn, axis=0, keepdims=True), i)
        dx2 = dx3_ref[...] + _rms_bwd(xn, r, acc * g_ref[...])
        outs[0][...] = dx2
        outs[1][...] = dx2.astype(BF16)

    y = square_matmul("mm_dh2", dup, wup, a_spec=((TBIG, D), lambda j, i, k: (i, k)),
                      b_spec=((None, TBIG, D), lambda j, i, k: (k, j, 0)), cdims=NT, nk=NCHIP, after=after)
    return rowwise("rows_dh2", y, extras=[(x2, *_rowblk), (dx3, *_rowblk), (g2, *_vec)],
                   outs=[((T, D), F32, *_rowblk), ((T, D), BF16, *_rowblk), ((1, D), F32, *_vec)], epi=epi)


def mm_dmixed(dx2b, wout, pcat, ylin, ygla, pscale, after=None):
    def epi(acc, ex, outs, i):
        lgp_ref, lgg_ref, ylin_ref, ygla_ref, ps_ref = ex
        dps = []
        for c0 in range(0, D, EPI_COLS):
            cs = slice(c0, c0 + EPI_COLS)
            gp = _sigmoid(lgp_ref[:, cs].astype(F32))
            gg = _sigmoid(lgg_ref[:, cs].astype(F32))
            yl = ylin_ref[:, cs].astype(F32)
            ps = ps_ref[:, cs]
            a = acc[:, cs]
            agp = a * gp
            outs[0][:, cs] = (agp * ps).astype(BF16)
            outs[1][:, cs] = (a * gg).astype(BF16)
            outs[2][:, cs] = (agp * (yl * ps) * (1.0 - gp)).astype(BF16)
            outs[3][:, cs] = (a * ygla_ref[:, cs].astype(F32) * gg * (1.0 - gg)).astype(BF16)
            dps.append(jnp.sum(agp * yl, axis=0, keepdims=True))
        _row_acc(outs[4], jnp.concatenate(dps, axis=1), i)

    return matmul("mm_dmixed", dx2b, wout, a_spec=_rowblk, b_spec=((D, D), lambda j, i, k: (0, 0)), cdims=NT,
                  grid=(1, T // TMF, 1), acc_shape=(TMF, D),
                  extras=[(pcat, *_full_spec(OGP // D)), (pcat, *_full_spec(OGG // D)), (ylin, *_rowblk), (ygla, *_rowblk),
                          (pscale, *_vec)],
                  outs=[((T, D), BF16, *_rowblk)] * 4 + [((1, D), F32, *_vec)], epi=epi, after=after)


def mm_dog(dygla, wgo, o, pcat, ng, after=None):
    def epi(acc, ex, outs, i):
        o_ref, g_ref, ng_ref = ex
        do_ref, dg_ref, gng_ref = outs
        gparts = []
        for h in range(HEADS):
            cv = slice(h * DV, (h + 1) * DV)
            oh = o_ref[:, cv].astype(F32)
            r = lax.rsqrt(jnp.mean(oh * oh, axis=-1, keepdims=True) + EPS)
            on = oh * r
            gv = g_ref[:, cv].astype(F32)
            sg = _sigmoid(gv)
            a = acc[:, cv]
            dgain = a * (gv * sg)
            gparts.append(jnp.sum(dgain * on, axis=0, keepdims=True))
            ngh = ng_ref[:, cv]
            do_ref[:, cv] = _rms_bwd(on, r, dgain * ngh).astype(BF16)
            dg_ref[:, cv] = (a * (on * ngh) * (sg * (1.0 + gv * (1.0 - sg)))).astype(BF16)
        _row_acc(gng_ref, jnp.concatenate(gparts, axis=1), i)

    return matmul("mm_dog", dygla, wgo, a_spec=_rowblk, b_spec=((D, D), lambda j, i, k: (0, 0)), cdims=NT,
                  grid=(1, T // TMF, 1), acc_shape=(TMF, D),
                  extras=[(o, *_rowblk), (pcat, *_full_spec(OG // D)), (ng, *_vec)],
                  outs=[((T, D), BF16, *_rowblk), ((T, D), BF16, *_rowblk), ((1, D), F32, *_vec)], epi=epi, after=after)


def mm_dh1(dpcat, wcat, x, dx2, g1, after=None):
    tk = 3840

    def epi(acc, ex, outs, i):
        x_ref, dx2_ref, g_ref = ex
        xv = x_ref[...]
        r = lax.rsqrt(jnp.mean(xv * xv, axis=-1, keepdims=True) + EPS)
        xn = xv * r
        _row_acc(outs[1], jnp.sum(acc * xn, axis=0, keepdims=True), i)
        outs[0][...] = dx2_ref[...] + _rms_bwd(xn, r, acc * g_ref[...])

    y = square_matmul("mm_dh1", dpcat, wcat, a_spec=((TBIG, tk), lambda j, i, k: (i, k)),
                      b_spec=((TBIG, tk), lambda j, i, k: (j, k)), cdims=NT, nk=NCAT // tk, after=after)
    return rowwise("rows_dh1", y, extras=[(x, *_rowblk), (dx2, *_rowblk), (g1, *_vec)],
                   outs=[((T, D), F32, *_rowblk), ((1, D), F32, *_vec)], epi=epi)


def _tile_rows(rows, cols, n_arrays):
    tm = rows
    while tm % 32 == 0 and 2 * n_arrays * tm * cols * 4 > 24 * 1024 * 1024:
        tm //= 2
    return tm


def add_pairs(name, parts, theirs, core):
    _, _, r, c = parts.shape
    tm = _tile_rows(r, c, 3)

    def body(core_ref, a_ref, b_ref, o_ref):
        o_ref[...] = (a_ref[...].astype(F32) + b_ref[...].astype(F32)).astype(BF16)

    spec = pl.BlockSpec((None, tm, c), lambda j, i, core_ref: (j, i, 0))
    grid_spec = pltpu.PrefetchScalarGridSpec(
        num_scalar_prefetch=1, grid=(NCHIP, r // tm),
        in_specs=[pl.BlockSpec((None, None, tm, c), lambda j, i, core_ref: (core_ref[0], j, i, 0)), spec], out_specs=spec)
    return pl.pallas_call(body, name=name, grid_spec=grid_spec, out_shape=SDS((NCHIP, r, c), BF16),
                          compiler_params=_cparams(40 * 1024 * 1024, ("arbitrary", "arbitrary")))(core, parts, theirs)


def sum_chips(name, sums, landed, chip):
    _, r, c = sums.shape
    tm = _tile_rows(r, c, 4)

    def body(chip_ref, own_ref, l_ref, o_ref):
        s = own_ref[...].astype(F32)
        for t in range(NCHIP - 1):
            s = s + l_ref[t].astype(F32)
        o_ref[...] = s

    grid_spec = pltpu.PrefetchScalarGridSpec(
        num_scalar_prefetch=1, grid=(r // tm,),
        in_specs=[pl.BlockSpec((None, tm, c), lambda i, chip_ref: (chip_ref[0], i, 0)),
                  pl.BlockSpec((NCHIP - 1, tm, c), lambda i, chip_ref: (0, i, 0))],
        out_specs=pl.BlockSpec((tm, c), lambda i, chip_ref: (i, 0)))
    return pl.pallas_call(body, name=name, grid_spec=grid_spec, out_shape=SDS((r, c), F32),
                          compiler_params=_cparams(40 * 1024 * 1024, ("arbitrary",)))(chip, sums, landed)


def _adamw_math(wv, gv, mv, vv):
    mn = ADAM_B1 * mv + (1.0 - ADAM_B1) * gv
    vn = ADAM_B2 * vv + (1.0 - ADAM_B2) * (gv * gv)
    mh = mn / (1.0 - ADAM_B1 ** ADAM_STEP)
    vh = vn / (1.0 - ADAM_B2 ** ADAM_STEP)
    return -ADAM_LR * (mh / (jnp.sqrt(vh) + ADAM_EPS) + ADAM_WD * wv), mn, vn


def adamw(name, w, g, m, v):
    def body(w_ref, g_ref, m_ref, v_ref, go_ref, d_ref, mo_ref, vo_ref):
        gv = g_ref[...]
        go_ref[...] = gv
        d_ref[...], mo_ref[...], vo_ref[...] = _adamw_math(w_ref[...], gv, m_ref[...], v_ref[...])

    return pl.pallas_call(body, name=name, out_shape=[SDS(w.shape, F32)] * 4)(w, g, m, v)


def adamw_halves(name, w, g_own, g_sib, m, v, core):
    _, r, c = w.shape
    tm = _tile_rows(r, c, 10)

    def body(core_ref, w_ref, go_ref, gs_ref, m_ref, v_ref, g_out, d_out, m_out, v_out):
        gv = jnp.where(pl.program_id(0) == core_ref[0], go_ref[...], gs_ref[...])
        g_out[...] = gv
        d_out[...], m_out[...], v_out[...] = _adamw_math(w_ref[...], gv, m_ref[...], v_ref[...])

    full = pl.BlockSpec((None, tm, c), lambda h, i, core_ref: (h, i, 0))
    own = pl.BlockSpec((tm, c), lambda h, i, core_ref: (jnp.where(h == core_ref[0], i, 0), 0))
    sib = pl.BlockSpec((tm, c), lambda h, i, core_ref: (jnp.where(h == core_ref[0], 0, i), 0))
    grid_spec = pltpu.PrefetchScalarGridSpec(num_scalar_prefetch=1, grid=(2, r // tm),
                                             in_specs=[full, own, sib, full, full], out_specs=[full] * 4)
    return pl.pallas_call(body, name=name, grid_spec=grid_spec, out_shape=[SDS(w.shape, F32)] * 4,
                          compiler_params=_cparams(48 * 1024 * 1024, ("arbitrary", "arbitrary")))(core, w, g_own, g_sib, m, v)


def cast_bf16(name, w):
    _, r, c = w.shape
    tm = _tile_rows(r, c, 2)

    def body(w_ref, o_ref):
        o_ref[...] = w_ref[...].astype(BF16)

    spec = pl.BlockSpec((None, tm, c), lambda h, i: (h, i, 0))
    return pl.pallas_call(body, name=name, grid=(2, r // tm), in_specs=[spec], out_specs=spec, out_shape=SDS(w.shape, BF16),
                          compiler_params=_cparams(40 * 1024 * 1024, ("arbitrary", "arbitrary")))(w)


def pack_rows(name, parts, rows, after=None):
    width = parts[0].shape[1]
    n = len(parts)
    afters = _as_list(after)

    def body(*refs):
        out_ref = refs[n + len(afters)]
        out_ref[...] = jnp.zeros_like(out_ref)
        off = 0
        for p in refs[:n]:
            out_ref[off:off + p.shape[0], :] = p[...]
            off += p.shape[0]

    vm = pl.BlockSpec(memory_space=pltpu.VMEM)
    return pl.pallas_call(body, name=name, in_specs=[vm] * n + [ANY] * len(afters), out_specs=vm,
                          out_shape=SDS((rows, width), F32))(*parts, *afters)


def _place():
    x, y, c = lax.axis_index("x"), lax.axis_index("y"), lax.axis_index("c")
    chips = [(1 - x, y), (x, 1 - y), (1 - x, 1 - y)]
    return x, y, c, chips


def _row_split(shape, dtype):
    r, c = shape
    n = 1
    while r % (2 * n) == 0 and (r // (2 * n)) % 16 == 0 and (r // n) * c * jnp.dtype(dtype).itemsize > PIECE_BYTES:
        n *= 2
    return [pl.ds(s * (r // n), r // n) for s in range(n)]


def _pieces(ref):
    *lead, r, c = ref.shape
    split = _row_split((r, c), ref.dtype)
    return [ref.at[(*idx, s)] for idx in itertools.product(*[range(d) for d in lead]) for s in split]


HBM = pl.BlockSpec(memory_space=pltpu.HBM)
SEM = pl.BlockSpec(memory_space=pltpu.SEMAPHORE)
EFFECT = pltpu.SideEffectType.DATAFLOW_SIDE_EFFECTING


def gather_start(name, shards, after=None):
    n = len(shards)
    afters = _as_list(after)

    def body(*refs):
        src, land = refs[:n], refs[n:2 * n]
        send, recv = refs[2 * n + len(afters)], refs[2 * n + len(afters) + 1]
        x, y, c, chips = _place()
        me = 2 * x + y
        for a in range(n):
            for j, (cx, cy) in enumerate(chips[:2]):
                for sp, dp in zip(_pieces(src[a].at[c]), _pieces(land[a].at[me, c])):
                    pltpu.make_async_remote_copy(sp, dp, send.at[2 * a + j], recv.at[2 * a + j],
                                                 device_id=(cx, cy, c), device_id_type=MESH).start()

    lands = [pltpu.with_memory_space_constraint(lax.empty((NCHIP,) + s.shape, s.dtype), pltpu.HBM) for s in shards]
    srcs = [pltpu.with_memory_space_constraint(s, pltpu.HBM) for s in shards]
    outs = pl.pallas_call(
        body, name=name,
        out_shape=(pltpu.SemaphoreType.DMA((2 * n,)), pltpu.SemaphoreType.DMA((2 * n,)),
                   *[pltpu.HBM(s.shape, s.dtype) for s in shards], *[pltpu.HBM(l.shape, l.dtype) for l in lands]),
        in_specs=[HBM] * (2 * n) + [ANY] * len(afters), out_specs=(SEM, SEM, *([HBM] * (2 * n))),
        input_output_aliases={i: 2 + i for i in range(2 * n)},
        compiler_params=pltpu.CompilerParams(has_side_effects=EFFECT),
    )(*srcs, *lands, *afters)
    return outs[0], outs[1], list(outs[2:2 + n]), list(outs[2 + n:2 + 2 * n])


def _relay_blocks(land, c, chips):
    (xx, xy), (yx, yy), (dx, dy) = chips
    rows = land.shape[2] // 2
    upper, lower = pl.ds(0, rows), pl.ds(rows, rows)
    return [(land.at[2 * yx + yy, c, lower], land.at[2 * dx + dy, c, lower]),
            (land.at[2 * xx + xy, c, upper], land.at[2 * dx + dy, c, upper])]


def relay_turn(name, send, recv, shards, lands, after):
    n = len(shards)
    afters = _as_list(after)

    def body(*refs):
        src, had = refs[:n], refs[n:2 * n]
        send_ref, recv_ref = refs[2 * n], refs[2 * n + 1]
        rsend, rrecv = refs[2 * n + 2 + len(afters)], refs[2 * n + 3 + len(afters)]
        land = refs[3 * n + 4 + len(afters):4 * n + 4 + len(afters)]
        x, y, c, chips = _place()
        for a in range(n):
            for j, (cx, cy) in enumerate(chips[:2]):
                cp = pltpu.make_async_remote_copy(src[a].at[c], had[a].at[2 * cx + cy, c], send_ref.at[2 * a + j],
                                                  recv_ref.at[2 * a + j], device_id=(cx, cy, c), device_id_type=MESH)
                cp.wait_send()
                cp.wait_recv()
        for a in range(n):
            for j, ((sent, _), (dst, _)) in enumerate(zip(_relay_blocks(had[a], c, chips), _relay_blocks(land[a], c, chips))):
                cx, cy = chips[j]
                for sp, dp in zip(_pieces(sent), _pieces(dst)):
                    pltpu.make_async_remote_copy(sp, dp, rsend.at[2 * a + j], rrecv.at[2 * a + j],
                                                 device_id=(cx, cy, c), device_id_type=MESH).start()

    outs = pl.pallas_call(
        body, name=name,
        out_shape=(pltpu.SemaphoreType.DMA((2 * n,)), pltpu.SemaphoreType.DMA((2 * n,)),
                   *[pltpu.HBM(s.shape, s.dtype) for s in shards], *[pltpu.HBM(l.shape, l.dtype) for l in lands]),
        in_specs=[HBM] * (2 * n) + [SEM, SEM] + [ANY] * len(afters), out_specs=(SEM, SEM, *([HBM] * (2 * n))),
        input_output_aliases={i: 2 + i for i in range(2 * n)},
        compiler_params=pltpu.CompilerParams(has_side_effects=EFFECT),
    )(*shards, *lands, send, recv, *afters)
    return outs[0], outs[1], list(outs[2:2 + n]), list(outs[2 + n:2 + 2 * n])


def relay_wait(name, send, recv, lands, after):
    n = len(lands)
    afters = _as_list(after)

    def body(*refs):
        land = refs[:n]
        send_ref, recv_ref = refs[n], refs[n + 1]
        x, y, c, chips = _place()
        for a in range(n):
            for j, (sent, got) in enumerate(_relay_blocks(land[a], c, chips)):
                cx, cy = chips[j]
                cp = pltpu.make_async_remote_copy(sent, got, send_ref.at[2 * a + j], recv_ref.at[2 * a + j],
                                                  device_id=(cx, cy, c), device_id_type=MESH)
                cp.wait_send()
                cp.wait_recv()

    outs = pl.pallas_call(
        body, name=name, out_shape=tuple(pltpu.HBM(l.shape, l.dtype) for l in lands),
        in_specs=[HBM] * n + [SEM, SEM] + [ANY] * len(afters), out_specs=[HBM] * n,
        input_output_aliases={i: i for i in range(n)},
        compiler_params=pltpu.CompilerParams(has_side_effects=EFFECT),
    )(*lands, send, recv, *afters)
    return list(outs)


def forward_halves(name, shards, lands):
    n = len(lands)

    def body(*refs):
        had, buf = refs[:n], refs[n:2 * n]
        send, recv = refs[2 * n:]
        x, y, c, chips = _place()
        sib = (x, y, 1 - c)
        for a in range(n):
            for j, (cx, cy) in enumerate(chips):
                for sp, dp in zip(_pieces(had[a].at[2 * cx + cy, c]), _pieces(buf[a].at[2 * cx + cy, c])):
                    pltpu.make_async_remote_copy(sp, dp, send.at[3 * a + j], recv.at[3 * a + j], device_id=sib, device_id_type=MESH).start()
        for a in range(n):
            for j, (cx, cy) in enumerate(chips):
                pltpu.make_async_remote_copy(had[a].at[2 * cx + cy, c], buf[a].at[2 * cx + cy, 1 - c], send.at[3 * a + j],
                                             recv.at[3 * a + j], device_id=sib, device_id_type=MESH).wait()

    got = pl.pallas_call(
        body, name=name, in_specs=[ANY] * n, out_specs=[ANY] * n, out_shape=[SDS(l.shape, l.dtype) for l in lands],
        input_output_aliases={i: i for i in range(n)},
        scratch_shapes=[pltpu.SemaphoreType.DMA((3 * n,)), pltpu.SemaphoreType.DMA((3 * n,))],
    )(*lands)
    me = 2 * lax.axis_index("x") + lax.axis_index("y")
    return [lax.dynamic_update_index_in_dim(g, s, me, 0) for g, s in zip(got, shards)]


def exchange_start(name, parts):
    n = len(parts)

    def body(*refs):
        src, got = refs[:n], refs[n:2 * n]
        send, recv = refs[2 * n], refs[2 * n + 1]
        token = refs[4 * n + 2]
        x, y, c, _ = _place()
        sib = (x, y, 1 - c)
        for a in range(n):
            for sp, dp in zip(_pieces(src[a].at[1 - c]), _pieces(got[a])):
                pltpu.make_async_remote_copy(sp, dp, send.at[a], recv.at[a], device_id=sib, device_id_type=MESH).start()
        token[...] = jnp.zeros_like(token)

    lands = [pltpu.with_memory_space_constraint(lax.empty(p.shape[1:], p.dtype), pltpu.HBM) for p in parts]
    srcs = [pltpu.with_memory_space_constraint(p, pltpu.HBM) for p in parts]
    outs = pl.pallas_call(
        body, name=name,
        out_shape=(pltpu.SemaphoreType.DMA((n,)), pltpu.SemaphoreType.DMA((n,)),
                   *[pltpu.HBM(p.shape, p.dtype) for p in parts], *[pltpu.HBM(l.shape, l.dtype) for l in lands],
                   SDS((8, 128), F32)),
        in_specs=[HBM] * (2 * n), out_specs=(SEM, SEM, *([HBM] * (2 * n)), pl.BlockSpec(memory_space=pltpu.VMEM)),
        input_output_aliases={i: 2 + i for i in range(2 * n)},
        compiler_params=pltpu.CompilerParams(has_side_effects=EFFECT),
    )(*srcs, *lands)
    return outs[0], outs[1], list(outs[2:2 + n]), list(outs[2 + n:2 + 2 * n]), outs[2 + 2 * n]


def exchange_wait(name, send, recv, parts, lands, after):
    n = len(parts)
    afters = _as_list(after)

    def body(*refs):
        src, got = refs[:n], refs[n:2 * n]
        send_ref, recv_ref = refs[2 * n], refs[2 * n + 1]
        x, y, c, _ = _place()
        sib = (x, y, 1 - c)
        for a in range(n):
            cp = pltpu.make_async_remote_copy(src[a].at[1 - c], got[a], send_ref.at[a], recv_ref.at[a], device_id=sib, device_id_type=MESH)
            cp.wait_send()
            cp.wait_recv()

    outs = pl.pallas_call(
        body, name=name,
        out_shape=(*[pltpu.HBM(p.shape, p.dtype) for p in parts], *[pltpu.HBM(l.shape, l.dtype) for l in lands]),
        in_specs=[HBM] * (2 * n) + [SEM, SEM] + [ANY] * len(afters), out_specs=[HBM] * (2 * n),
        input_output_aliases={i: i for i in range(2 * n)},
        compiler_params=pltpu.CompilerParams(has_side_effects=EFFECT),
    )(*parts, *lands, send, recv, *afters)
    return list(outs[:n]), list(outs[n:])


def scatter_start(name, parts):
    n = len(parts)

    def body(*refs):
        src, land = refs[:n], refs[n:2 * n]
        send, recv = refs[2 * n], refs[2 * n + 1]
        token = refs[4 * n + 2]
        x, y, c, chips = _place()
        for a in range(n):
            for j, (cx, cy) in enumerate(chips):
                for sp, dp in zip(_pieces(src[a].at[2 * cx + cy]), _pieces(land[a].at[j])):
                    pltpu.make_async_remote_copy(sp, dp, send.at[3 * a + j], recv.at[3 * a + j],
                                                 device_id=(cx, cy, c), device_id_type=MESH).start()
        token[...] = jnp.zeros_like(token)

    lands = [pltpu.with_memory_space_constraint(lax.empty((NCHIP - 1,) + p.shape[1:], p.dtype), pltpu.HBM) for p in parts]
    srcs = [pltpu.with_memory_space_constraint(p, pltpu.HBM) for p in parts]
    outs = pl.pallas_call(
        body, name=name,
        out_shape=(pltpu.SemaphoreType.DMA((3 * n,)), pltpu.SemaphoreType.DMA((3 * n,)),
                   *[pltpu.HBM(p.shape, p.dtype) for p in parts], *[pltpu.HBM(l.shape, l.dtype) for l in lands],
                   SDS((8, 128), F32)),
        in_specs=[HBM] * (2 * n), out_specs=(SEM, SEM, *([HBM] * (2 * n)), pl.BlockSpec(memory_space=pltpu.VMEM)),
        input_output_aliases={i: 2 + i for i in range(2 * n)},
        compiler_params=pltpu.CompilerParams(has_side_effects=EFFECT),
    )(*srcs, *lands)
    return outs[0], outs[1], list(outs[2:2 + n]), list(outs[2 + n:2 + 2 * n]), outs[2 + 2 * n]


def scatter_wait(name, send, recv, parts, lands, after):
    n = len(parts)
    afters = _as_list(after)

    def body(*refs):
        src, land = refs[:n], refs[n:2 * n]
        send_ref, recv_ref = refs[2 * n], refs[2 * n + 1]
        x, y, c, chips = _place()
        for a in range(n):
            for j, (cx, cy) in enumerate(chips):
                cp = pltpu.make_async_remote_copy(src[a].at[2 * cx + cy], land[a].at[j], send_ref.at[3 * a + j], recv_ref.at[3 * a + j],
                                                  device_id=(cx, cy, c), device_id_type=MESH)
                cp.wait_send()
                cp.wait_recv()

    outs = pl.pallas_call(
        body, name=name,
        out_shape=(*[pltpu.HBM(p.shape, p.dtype) for p in parts], *[pltpu.HBM(l.shape, l.dtype) for l in lands]),
        in_specs=[HBM] * (2 * n) + [SEM, SEM] + [ANY] * len(afters), out_specs=[HBM] * (2 * n),
        input_output_aliases={i: i for i in range(2 * n)},
        compiler_params=pltpu.CompilerParams(has_side_effects=EFFECT),
    )(*parts, *lands, send, recv, *afters)
    return list(outs[:n]), list(outs[n:])


def join_start(name, halves):
    n = len(halves)

    def body(*refs):
        src, dst = refs[:n], refs[n:2 * n]
        send, recv = refs[2 * n], refs[2 * n + 1]
        token = refs[4 * n + 2]
        x, y, c, _ = _place()
        sib = (x, y, 1 - c)
        for a in range(n):
            for sp, dp in zip(_pieces(src[a]), _pieces(dst[a])):
                pltpu.make_async_remote_copy(sp, dp, send.at[a], recv.at[a], device_id=sib, device_id_type=MESH).start()
        token[...] = jnp.zeros_like(token)

    lands = [pltpu.with_memory_space_constraint(lax.empty(h.shape, h.dtype), pltpu.HBM) for h in halves]
    srcs = [pltpu.with_memory_space_constraint(h, pltpu.HBM) for h in halves]
    outs = pl.pallas_call(
        body, name=name,
        out_shape=(pltpu.SemaphoreType.DMA((n,)), pltpu.SemaphoreType.DMA((n,)),
                   *[pltpu.HBM(h.shape, h.dtype) for h in halves], *[pltpu.HBM(l.shape, l.dtype) for l in lands],
                   SDS((8, 128), F32)),
        in_specs=[HBM] * (2 * n), out_specs=(SEM, SEM, *([HBM] * (2 * n)), pl.BlockSpec(memory_space=pltpu.VMEM)),
        input_output_aliases={i: 2 + i for i in range(2 * n)},
        compiler_params=pltpu.CompilerParams(has_side_effects=EFFECT),
    )(*srcs, *lands)
    return outs[0], outs[1], list(outs[2:2 + n]), list(outs[2 + n:2 + 2 * n]), outs[2 + 2 * n]


def join_wait(name, send, recv, halves, lands, after):
    n = len(halves)
    afters = _as_list(after)

    def body(*refs):
        src, dst = refs[:n], refs[n:2 * n]
        send_ref, recv_ref = refs[2 * n], refs[2 * n + 1]
        x, y, c, _ = _place()
        sib = (x, y, 1 - c)
        for a in range(n):
            cp = pltpu.make_async_remote_copy(src[a], dst[a], send_ref.at[a], recv_ref.at[a], device_id=sib, device_id_type=MESH)
            cp.wait_send()
            cp.wait_recv()

    outs = pl.pallas_call(
        body, name=name,
        out_shape=(*[pltpu.HBM(h.shape, h.dtype) for h in halves], *[pltpu.HBM(l.shape, l.dtype) for l in lands]),
        in_specs=[HBM] * (2 * n) + [SEM, SEM] + [ANY] * len(afters), out_specs=[HBM] * (2 * n),
        input_output_aliases={i: i for i in range(2 * n)},
        compiler_params=pltpu.CompilerParams(has_side_effects=EFFECT),
    )(*halves, *lands, send, recv, *afters)
    return list(outs[:n]), list(outs[n:])


def gather_small(name, xs, reduce, after=None):
    m, ncol = xs.shape
    afters = _as_list(after)

    def body(x_ref, *rest):
        out_ref, all_ref, send, recv, lsem = rest[len(afters):]
        x, y, c, chips = _place()
        me, sib = (x, y, c), (x, y, 1 - c)

        def rows(px, py, pc):
            return all_ref.at[pl.ds((4 * px + 2 * py + pc) * m, m), :]

        def copy(k, block, to, src=None):
            return pltpu.make_async_remote_copy(rows(*block) if src is None else src, rows(*block), send.at[k], recv.at[k],
                                                device_id=to, device_id_type=MESH)

        mine = pltpu.make_async_copy(x_ref, rows(*me), lsem)
        mine.start()
        first = [copy(0, me, sib, src=x_ref)] + [copy(1 + j, me, (*chip, c), src=x_ref) for j, chip in enumerate(chips)]
        for cp in first:
            cp.start()
        passed = [copy(4 + j, (*chip, c), sib) for j, chip in enumerate(chips)]
        for j, chip in enumerate(chips):
            copy(1 + j, (*chip, c), me).wait_recv()
            passed[j].start()
        copy(0, sib, me).wait_recv()
        for j, chip in enumerate(chips):
            copy(4 + j, (*chip, 1 - c), me).wait_recv()
        for cp in first + passed:
            cp.wait_send()
        mine.wait()
        if reduce:
            s = all_ref[0:m, :]
            for dev in range(1, 8):
                s = s + all_ref[dev * m:(dev + 1) * m, :]
            out_ref[...] = s
        else:
            out_ref[...] = all_ref[...]

    vm = pl.BlockSpec(memory_space=pltpu.VMEM)
    return pl.pallas_call(
        body, name=name, in_specs=[vm] + [ANY] * len(afters), out_specs=vm,
        out_shape=SDS((m, ncol) if reduce else (8 * m, ncol), F32),
        scratch_shapes=[pltpu.VMEM((8 * m, ncol), F32), pltpu.SemaphoreType.DMA((7,)), pltpu.SemaphoreType.DMA((7,)),
                        pltpu.SemaphoreType.DMA],
    )(xs, *afters)


RELAYOUT_ROWS = 128


def weights_to_cat(g_in, after=None):
    tm = RELAYOUT_ROWS
    afters = _as_list(after)

    def body(g_ref, *rest):
        o_ref = rest[len(afters)]
        nat = jnp.concatenate([g_ref[j] for j in range(NCHIP)], axis=1)
        pad = jnp.zeros((tm, NCAT - OA - 16), BF16)
        o_ref[...] = jnp.concatenate([nat[:, 3072:7168], nat[:, 7184:11280], nat[:, 0:3072], nat[:, 7168:7184], pad], axis=1)

    return pl.pallas_call(
        body, name="weights_to_cat", grid=(D // tm,),
        in_specs=[pl.BlockSpec((NCHIP, tm, IN_SHARD), lambda i: (0, i, 0))] + [ANY] * len(afters),
        out_specs=pl.BlockSpec((tm, NCAT), lambda i: (i, 0)), out_shape=SDS((D, NCAT), BF16),
        compiler_params=_cparams(40 * 1024 * 1024, ("arbitrary",)),
    )(g_in, *afters)


def grads_from_cat(gw_cat):
    tm = RELAYOUT_ROWS
    nb = (D // 2) // tm

    def body(c_ref, o_ref):
        cat = c_ref[...]
        nat = jnp.concatenate([cat[:, OU:OA], cat[:, OV:OGP], cat[:, OA:OA + 16], cat[:, OGP:OU]], axis=1)
        for j in range(NCHIP):
            o_ref[j] = nat[:, j * IN_SHARD:(j + 1) * IN_SHARD]

    return pl.pallas_call(
        body, name="grads_from_cat", grid=(D // tm,), in_specs=[pl.BlockSpec((tm, NCAT), lambda i: (i, 0))],
        out_specs=pl.BlockSpec((None, NCHIP, tm, IN_SHARD), lambda i: (i // nb, 0, i % nb, 0)),
        out_shape=SDS((2, NCHIP, D // 2, IN_SHARD), BF16), compiler_params=_cparams(40 * 1024 * 1024, ("arbitrary",)),
    )(gw_cat)


def _pad_rows(a, rows):
    return jnp.concatenate([a, jnp.zeros((rows - a.shape[0],) + a.shape[1:], a.dtype)], axis=0)


def local_step(x2d, tgt, gf, g1, pool_scale, wa_pad, b_alpha, ng, g2, get_w, on_grad=None, on_settle=None, tick=None):
    emit = on_grad if on_grad is not None else (lambda group, grads: None)
    settle = on_settle if on_settle is not None else (lambda group, after: None)
    h1 = norm1(x2d, g1)
    wcat, pw = get_w("in", h1)
    pcat = mm_in(h1, wcat)
    dpool, ylin = pool_fwd(pcat, pw)
    pinned = tick("pool", ylin) if tick is not None else None
    og, o, states = gla_fwd(pcat, wa_pad, b_alpha, ng, pinned)
    w_go, w_o = get_w("mid", og)
    mixed, ygla = mm_gla_out(og, w_go, ylin, pcat, pool_scale)
    x2, h2 = mm_out(mixed, w_o, x2d, g2)
    w_up = get_w("up", h2)
    rup, act = mm_up(h2, w_up)
    w_dn = get_w("down", act)
    dx3, dx3b, g_nf, loss_row = mm_down(act, w_dn, x2, tgt, gf)

    gw_down = mm_wgrad("mm_dw_down", act, dx3b, DFF, D, (2, NCHIP, D // 2, D), (None, None, D // 2, D),
                       lambda j, i, k: (i % 2, i // 2, 0, 0), D // 2, D)
    token = emit("down", {"down": gw_down})
    dup = mm_dact(dx3b, w_dn, rup, after=token)
    token = settle("down", dup)
    dx2, dx2b, g_mlp = mm_dh2(dup, w_up, x2, dx3, g2, after=token)
    gw_up = mm_wgrad("mm_dw_up", h2, dup, D, DFF, (2, NCHIP, D // 2, D), (None, None, D // 2, D),
                     lambda j, i, k: (i, j, 0, 0), D // 2, D)
    token = emit("up", {"up": gw_up})
    dylin, dygla, dlgp, dlgg, g_ps = mm_dmixed(dx2b, w_o, pcat, ylin, ygla, pool_scale, after=token)
    token = settle("up", dylin)
    gw_out = mm_wgrad("mm_dw_out", mixed, dx2b, D, D, (2, NCHIP, 256, D), (2, None, 256, D),
                      lambda j, i, k: (0, i, 0, 0), 512, D)
    do, dg, g_ng = mm_dog(dygla, w_go, o, pcat, ng, after=token)
    gw_go = mm_wgrad("mm_dw_gla_out", og, dygla, D, D, (2, NCHIP, 256, D), (2, None, 256, D),
                     lambda j, i, k: (0, i, 0, 0), 512, D)
    token = emit("mix", {"out": gw_out, "gla_out": gw_go})
    dq, dk, dv, dalow, g_wa, g_ba = gla_bwd(do, pcat, states, wa_pad, b_alpha, b_alpha if token is None else token)
    token = settle("mix", dq)
    du, dpw = pool_bwd(dylin, dpool, pw)
    dpcat = jnp.concatenate([dv, dg, dlgp, dlgg, du, dq, dk, dalow, jnp.zeros((T, NCAT - OA - APAD), BF16)], axis=1)
    gw_cat = mm_wgrad("mm_dw_in", h1, dpcat, D, NCAT, (D, NCAT), (1024, 1280), lambda j, i, k: (i, j), 1024, 1280, after=token)
    token = settle("in", emit("in", {"in_cat": gw_cat, "pool": dpw}))
    grad_x, g_mix = mm_dh1(dpcat, wcat, x2d, dx2, g1, after=token)
    return (loss_row[0, 0], grad_x, g_mix, g_ps, g_mlp, g_nf, g_ng, g_ba, g_wa, token,
            gw_cat, dpw, gw_go, gw_out, gw_up, gw_down)


def kernel(x, norm_mix_g, w_in, pool_w, pool_scale, w_alpha, b_alpha, gla_norm_g, w_gla_out, w_out, norm_mlp_g, w_mlp_up, w_mlp_down, norm_final_g, loss_target, m_norm_mix_g, m_w_in, m_pool_w, m_pool_scale, m_w_alpha, m_b_alpha, m_gla_norm_g, m_w_gla_out, m_w_out, m_norm_mlp_g, m_w_mlp_up, m_w_mlp_down, m_norm_final_g, v_norm_mix_g, v_w_in, v_pool_w, v_pool_scale, v_w_alpha, v_b_alpha, v_gla_norm_g, v_w_gla_out, v_w_out, v_norm_mlp_g, v_w_mlp_up, v_w_mlp_down, v_norm_final_g):
    chip = 2 * lax.axis_index("x") + lax.axis_index("y")
    chip_i = chip.astype(jnp.int32).reshape(1)
    core_i = lax.axis_index("c").astype(jnp.int32).reshape(1)
    tgt = loss_target.reshape(T, D)
    gf = norm_final_g.reshape(1, D)

    def halves(w2d):
        r, c = w2d.shape
        return w2d.astype(BF16).reshape(2, r // 2, c)

    pool_shard = pool_w.reshape(4 * PG, PO // NCHIP)
    w_in_r = w_in.reshape(2, D // 2, IN_SHARD)
    sent = {"in": [cast_bf16("cast_w_in", w_in_r), halves(pool_shard)]}
    flight = {}

    def start(group, after=None):
        flight[group] = gather_start("gather_start_" + group, sent[group], after)

    def relay(group, after):
        send, recv, shards, lands = flight[group]
        flight[group] = relay_turn("relay_turn_" + group, send, recv, shards, lands, after)

    def fetch(group, after, then=None):
        send, recv, shards, lands = flight[group]
        lands = relay_wait("relay_wait_" + group, send, recv, lands, after)
        if then is not None:
            then(lands[0])
        return forward_halves("forward_" + group, shards, lands)

    start("in")
    m_in_f, v_in_f, w_go_f, w_o_f, w_up_f, w_dn_f, x_f, wal_f, gng_f = lax.optimization_barrier(
        (m_w_in, v_w_in, w_gla_out, w_out, w_mlp_up, w_mlp_down, x, w_alpha, gla_norm_g, flight["in"][2][0]))[:9]
    m_in_r, v_in_r = m_in_f.reshape(2, D // 2, IN_SHARD), v_in_f.reshape(2, D // 2, IN_SHARD)
    sent["mid"] = [halves(w_go_f[0]), halves(w_o_f[0])]
    relay("in", [m_in_r, v_in_r, *sent["mid"]])
    w_up_f, w_dn_f, x_f, wal_f, gng_f = lax.optimization_barrier(
        (w_up_f, w_dn_f, x_f, wal_f, gng_f, flight["in"][3][0]))[:5]
    sent["up"], sent["down"] = [halves(w_up_f[0])], [halves(w_dn_f[0])]
    x2d = x_f.reshape(T, D)
    big = [w_in_r, w_go_f[0], w_o_f[0], w_up_f[0], w_dn_f[0], pool_shard]

    def tick(point, after):
        if point == "pool":
            relay("mid", after)
            start("down", flight["mid"][3][0])
            return [flight["mid"][3][0], flight["down"][3][0]]

    def get_w(group, after):
        if group == "in":
            after = [after, *sent["up"], *sent["down"], wa_pad]
        if group == "mid":
            relay("up", after)
            after = flight["up"][3][0]
        if group == "up":
            relay("down", after)
            after = flight["down"][3][0]
        if group == "in":
            def next_groups(landed):
                start("mid", landed)
                start("up", flight["mid"][3][0])

            g_in, g_pool = fetch(group, after, next_groups)
            wcat = weights_to_cat(g_in.reshape(NCHIP, D, IN_SHARD), flight["up"][3][0])
            pw = jnp.concatenate([g_pool[j].reshape(4, PG, PO // NCHIP) for j in range(NCHIP)], axis=2)
            return wcat, pw
        whole = fetch(group, after)
        if group == "mid":
            return whole[0].reshape(D, D), whole[1].reshape(D, D)
        if group == "up":
            return whole[0].reshape(NCHIP, D, D)
        return whole[0].reshape(DFF, D)

    small_w = pack_rows("pack_small_w", [wal_f[0].reshape(4, QK),
                                         jnp.concatenate([gng_f[0].reshape(1, 512), jnp.zeros((1, 512), F32)], axis=1)], 8)
    sw_all = gather_small("gather_small_w", small_w, False).reshape(8, 8, QK)
    wa_full = jnp.concatenate([sw_all[2 * j, 0:4].reshape(16, DK) for j in range(NCHIP)], axis=1)
    ng_full = jnp.concatenate([sw_all[2 * j, 4, 0:512].reshape(HEADS, DV // NCHIP) for j in range(NCHIP)], axis=1)
    wa_pad = _pad_rows(wa_full, APAD).astype(BF16)
    ng = ng_full.reshape(1, D)

    pending = {}
    wmv = {"in": (w_in_r, m_in_r, v_in_r), "gla_out": (big[1], m_w_gla_out, v_w_gla_out), "out": (big[2], m_w_out, v_w_out),
           "up": (big[3], m_w_mlp_up, v_w_mlp_up), "down": (big[4], m_w_mlp_down, v_w_mlp_down), "pool": (big[5], m_pool_w, v_pool_w)}
    big_res = {}

    def reduce_group(group, after):
        nms, send, recv, sums, lands = pending[group]
        sums, lands = scatter_wait("scatter_wait_" + group, send, recv, sums, lands, after)
        reduced = [sum_chips("sum_chips_" + nm, a, b, chip_i) for nm, a, b in zip(nms, sums, lands)]
        send, recv, reduced, lands, token = join_start("join_start_" + group, reduced)
        pending[group] = (nms, send, recv, reduced, lands)
        return token

    def update_group(group, after):
        nms, send, recv, reduced, lands = pending[group]
        reduced, from_sib = join_wait("join_wait_" + group, send, recv, reduced, lands, after)
        for nm, g_own, g_sib in zip(nms, reduced, from_sib):
            w, m, v = wmv[nm]
            shp = (2,) + g_own.shape
            big_res[nm] = adamw_halves("adamw_" + nm, w.reshape(shp), g_own, g_sib, m.reshape(shp), v.reshape(shp), core_i)

    def on_grad(group, grads):
        if group == "in":
            gw_in = grads_from_cat(grads["in_cat"])
            gw_pool = jnp.stack([grads["pool"][:, :, j * 128:(j + 1) * 128].reshape(2, 2 * PG, 128)
                                 for j in range(NCHIP)], axis=1)
            grads = {"in": gw_in, "pool": gw_pool}
        nms, parts = list(grads.keys()), list(grads.values())
        send, recv, parts, got, token = exchange_start("exchange_start_" + group, parts)
        pending[group] = (nms, send, recv, parts, got)
        return token

    def on_settle(group, after):
        if group == "in":
            after = reduce_group("down", after)
        nms, send, recv, parts, got = pending[group]
        parts, got = exchange_wait("exchange_wait_" + group, send, recv, parts, got, after)
        sums = [add_pairs("add_pair_" + nm, a, b, core_i) for nm, a, b in zip(nms, parts, got)]
        send, recv, sums, lands, token = scatter_start("scatter_start_" + group, sums)
        pending[group] = (nms, send, recv, sums, lands)
        if group != "in":
            return token
        token = reduce_group("up", token)
        token = reduce_group("mix", token)
        for earlier in ("down", "up", "mix"):
            update_group(earlier, token)
            token = big_res[pending[earlier][0][-1]][1]
        return [big_res[nm][1] for nm in ("down", "up", "out", "gla_out")]

    (loss_local, grad_x, g_mix, g_ps, g_mlp, g_nf, g_ng, g_ba, g_wa) = local_step(
        x2d, tgt, gf, norm_mix_g, pool_scale, wa_pad, b_alpha, ng, norm_mlp_g, get_w, on_grad, on_settle, tick)[:9]
    loss = lax.psum(loss_local, ("x", "y", "c"))
    join_in_token = reduce_group("in", grad_x)

    ROWS = 16

    def wide(a, n):
        return jnp.concatenate([a.reshape(1, n), jnp.zeros((1, D - n), F32)], axis=1)

    packed = pack_rows("pack_small_g", [g_mix, g_ps, g_mlp, g_nf, g_ng, wide(g_ba, QK), g_wa[0:16].reshape(8, D)], ROWS)
    tot = gather_small("reduce_small_g", packed, True, join_in_token)
    t_wa = lax.dynamic_slice(tot[6:14].reshape(16, QK), (0, chip * DK), (16, DK))
    t_ng = lax.dynamic_slice(tot[4].reshape(HEADS, DV), (0, chip * (DV // NCHIP)), (HEADS, DV // NCHIP))

    def pack_small(nm, mix, ps, mlp, nf, ba, wa, gn, after=None):
        return pack_rows(nm, [mix.reshape(1, D), ps.reshape(1, D), mlp.reshape(1, D), nf.reshape(1, D), wide(ba, QK),
                              wa.reshape(2, D), wide(gn, 512)], ROWS, after)

    update_group("in", tot)
    sg = pack_small("pack_g", tot[0], tot[1], tot[2], tot[3], tot[5, 0:QK], t_wa, t_ng, big_res["in"][3])
    sw = pack_small("pack_w", norm_mix_g, pool_scale, norm_mlp_g, norm_final_g, b_alpha, w_alpha, gla_norm_g)
    sm = pack_small("pack_m", m_norm_mix_g, m_pool_scale, m_norm_mlp_g, m_norm_final_g, m_b_alpha, m_w_alpha, m_gla_norm_g)
    sv = pack_small("pack_v", v_norm_mix_g, v_pool_scale, v_norm_mlp_g, v_norm_final_g, v_b_alpha, v_w_alpha, v_gla_norm_g)
    small_res = adamw("adamw_small", sw, sg, sm, sv)

    def unpack(p):
        return {"norm_mix_g": p[0].reshape(1, D), "pool_scale": p[1].reshape(1, D), "norm_mlp_g": p[2].reshape(1, D),
                "norm_final_g": p[3].reshape(D), "b_alpha": p[4, 0:QK].reshape(1, QK), "w_alpha": p[5:7].reshape(1, 16, DK),
                "gla_norm_g": p[7, 0:512].reshape(1, HEADS, DV // NCHIP)}

    order = ["norm_mix_g", "w_in", "pool_w", "pool_scale", "w_alpha", "b_alpha", "gla_norm_g", "w_gla_out", "w_out",
             "norm_mlp_g", "w_mlp_up", "w_mlp_down", "norm_final_g"]
    big_key = {"w_in": ("in", w_in.shape), "pool_w": ("pool", pool_w.shape), "w_gla_out": ("gla_out", w_gla_out.shape),
               "w_out": ("out", w_out.shape), "w_mlp_up": ("up", w_mlp_up.shape), "w_mlp_down": ("down", w_mlp_down.shape)}
    result = [loss, grad_x.reshape(1, T, D)]
    for kind in range(4):
        small = unpack(small_res[kind])
        for nm in order:
            if nm in big_key:
                key, shp = big_key[nm]
                result.append(big_res[key][kind].reshape(shp))
            else:
                result.append(small[nm])
    return tuple(result)
```

```python
import itertools

import jax
import jax.numpy as jnp
from jax import lax
from jax.experimental import pallas as pl
from jax.experimental.pallas import tpu as pltpu

F32 = jnp.float32
BF16 = jnp.bfloat16
SDS = jax.ShapeDtypeStruct
MESH = pl.DeviceIdType.MESH
ANY = pl.BlockSpec(memory_space=pl.ANY)

T = 2048
D = 2048
DFF = 8192
NCHIP = 4
IN_WIDTH = 11280
IN_SHARD = IN_WIDTH // NCHIP
CHUNK = 64
NCHUNK = T // CHUNK
HEADS = 4
DK = 256
DV = 512
QK = HEADS * DK
EPS = 1e-6
POOL_WINDOWS = (2, 4, 8, 16)
PG = 256
PO = 512

OV, OG, OGP, OGG, OU, OQ, OKK, OA = 0, 2048, 4096, 6144, 8192, 9216, 10240, 11264
NCAT = 11520
APAD = 128

VMEM_CAP = 56 * 1024 * 1024

PIECE_BYTES = 384 * 1024

ADAM_LR, ADAM_B1, ADAM_B2, ADAM_EPS, ADAM_WD, ADAM_STEP = 0.001, 0.9, 0.999, 1e-08, 0.01, 10


def _cparams(vmem_bytes=None, sem=None):
    kw = {}
    if vmem_bytes is not None:
        kw["vmem_limit_bytes"] = int(min(max(vmem_bytes, 32 * 1024 * 1024), VMEM_CAP))
    if sem is not None:
        kw["dimension_semantics"] = sem
    return pltpu.CompilerParams(**kw)


def _nbytes(shape, dtype):
    n = 1
    for s in shape:
        if s is not None:
            n *= s
    return n * jnp.dtype(dtype).itemsize


def _sigmoid(x):
    return 0.5 * jnp.tanh(0.5 * x) + 0.5


EPI_COLS = 512


def _as_list(after):
    if after is None:
        return []
    return list(after) if isinstance(after, (list, tuple)) else [after]


def matmul(name, a, b, *, a_spec, b_spec, cdims, grid, acc_shape, outs, extras=(), epi, after=None):
    nj, ni, nk = grid
    ne, no = len(extras), len(outs)
    afters = _as_list(after)
    first_out = 2 + ne + len(afters)

    def body(*refs):
        a_ref, b_ref = refs[0], refs[1]
        ex = refs[2:2 + ne]
        out_refs = refs[first_out:first_out + no]
        i = pl.program_id(1)
        part = lax.dot_general(a_ref[...], b_ref[...], (cdims, ((), ())), preferred_element_type=F32)
        if nk == 1:
            epi(part, ex, out_refs, i)
        else:
            acc_ref = refs[first_out + no]
            k = pl.program_id(2)

            @pl.when(k == 0)
            def _():
                acc_ref[...] = part

            @pl.when(k > 0)
            def _():
                acc_ref[...] += part

            @pl.when(k == nk - 1)
            def _():
                epi(acc_ref[...], ex, out_refs, i)

    in_specs = [pl.BlockSpec(*a_spec), pl.BlockSpec(*b_spec)] + [pl.BlockSpec(bs, im) for _, bs, im in extras]
    in_specs += [ANY] * len(afters)
    out_specs = [pl.BlockSpec(bs, im) for _, _, bs, im in outs]
    out_shape = [SDS(s, dt) for s, dt, _, _ in outs]
    vm = 2 * (_nbytes(a_spec[0], a.dtype) + _nbytes(b_spec[0], b.dtype))
    vm += 2 * sum(_nbytes(bs, arr.dtype) for arr, bs, _ in extras)
    vm += 2 * sum(_nbytes(bs, dt) for _, dt, bs, _ in outs)
    vm += 6 * _nbytes(acc_shape, F32)
    scratch = [pltpu.VMEM(acc_shape, F32)] if nk > 1 else []
    return pl.pallas_call(
        body, name=name, grid=grid, in_specs=in_specs, out_specs=out_specs, out_shape=out_shape,
        scratch_shapes=scratch,
        compiler_params=_cparams(vm, ("arbitrary", "arbitrary", "arbitrary")),
    )(a, b, *[arr for arr, _, _ in extras], *afters)


NN =((1,), (0,))
NT = ((1,), (1,))
TN = ((0,), (0,))


def _row_acc(out_ref, val, i):
    @pl.when(i == 0)
    def _():
        out_ref[...] = val

    @pl.when(i > 0)
    def _():
        out_ref[...] += val


def _rms_bwd(xn, r, dxn):
    return r * (dxn - xn * jnp.mean(dxn * xn, axis=-1, keepdims=True))


def norm1(x, g):
    tm = 256

    def body(x_ref, g_ref, h_ref):
        xv = x_ref[...]
        r = lax.rsqrt(jnp.mean(xv * xv, axis=-1, keepdims=True) + EPS)
        h_ref[...] = (xv * r * g_ref[...]).astype(BF16)

    return pl.pallas_call(
        body, name="norm1", grid=(T // tm,),
        in_specs=[pl.BlockSpec((tm, D), lambda i: (i, 0)), pl.BlockSpec((1, D), lambda i: (0, 0))],
        out_specs=pl.BlockSpec((tm, D), lambda i: (i, 0)), out_shape=SDS((T, D), BF16),
        compiler_params=_cparams(32 * 1024 * 1024, ("arbitrary",)),
    )(x, g)


def mm_in(h1, wcat):
    tm, tn = 1024, 1280

    def epi(acc, ex, outs, i):
        outs[0][...] = acc.astype(BF16)

    return matmul("mm_in", h1, wcat, a_spec=((tm, D), lambda j, i, k: (i, 0)), b_spec=((D, tn), lambda j, i, k: (0, j)),
                  cdims=NN, grid=(NCAT // tn, T // tm, 1), acc_shape=(tm, tn),
                  outs=[((T, NCAT), BF16, (tm, tn), lambda j, i, k: (i, j))], epi=epi)[0]


def _window_sum(x, w, up):
    n = x.shape[0]
    row = lax.broadcasted_iota(jnp.int32, x.shape, 0)
    s, sh = x, 1
    while sh < w:
        if up:
            s = s + jnp.where(row < n - sh, pltpu.roll(s, n - sh, axis=0), 0.0)
        else:
            s = s + jnp.where(row >= sh, pltpu.roll(s, sh, axis=0), 0.0)
        sh *= 2
    return s


def _inv_count(shape, w):
    row = lax.broadcasted_iota(jnp.int32, shape, 0)
    return 1.0 / jnp.minimum(row + 1, w).astype(F32)


def pool_fwd(pcat, pw):
    def body(u_ref, pw_ref, d_ref, y_ref):
        for gi, w in enumerate(POOL_WINDOWS):
            ug = u_ref[:, gi * PG:(gi + 1) * PG].astype(F32)
            dg = _window_sum(ug, w, False) * _inv_count(ug.shape, w) - ug
            db = dg.astype(BF16)
            d_ref[:, gi * PG:(gi + 1) * PG] = db
            y_ref[:, gi * PO:(gi + 1) * PO] = jnp.dot(db, pw_ref[gi], preferred_element_type=F32).astype(BF16)

    return pl.pallas_call(
        body, name="pool_fwd", grid=(1,),
        in_specs=[pl.BlockSpec((T, 4 * PG), lambda i: (0, OU // (4 * PG))), pl.BlockSpec((4, PG, PO), lambda i: (0, 0, 0))],
        out_specs=[pl.BlockSpec((T, 4 * PG), lambda i: (0, 0)), pl.BlockSpec((T, D), lambda i: (0, 0))],
        out_shape=[SDS((T, 4 * PG), BF16), SDS((T, D), BF16)],
        compiler_params=_cparams(48 * 1024 * 1024, ("arbitrary",)),
    )(pcat, pw)


def pool_bwd(dylin, d, pw):
    def body(dy_ref, d_ref, pw_ref, du_ref, dpw_ref):
        for gi, w in enumerate(POOL_WINDOWS):
            dyl = dy_ref[:, gi * PO:(gi + 1) * PO]
            dd = lax.dot_general(dyl, pw_ref[gi], (NT, ((), ())), preferred_element_type=F32)
            du = _window_sum(dd * _inv_count(dd.shape, w), w, True) - dd
            du_ref[:, gi * PG:(gi + 1) * PG] = du.astype(BF16)
            dpw_ref[gi] = lax.dot_general(d_ref[:, gi * PG:(gi + 1) * PG], dyl, (TN, ((), ())),
                                          preferred_element_type=F32).astype(BF16)

    return pl.pallas_call(
        body, name="pool_bwd", grid=(1,),
        in_specs=[pl.BlockSpec((T, D), lambda i: (0, 0)), pl.BlockSpec((T, 4 * PG), lambda i: (0, 0)),
                  pl.BlockSpec((4, PG, PO), lambda i: (0, 0, 0))],
        out_specs=[pl.BlockSpec((T, 4 * PG), lambda i: (0, 0)), pl.BlockSpec((4, PG, PO), lambda i: (0, 0, 0))],
        out_shape=[SDS((T, 4 * PG), BF16), SDS((4, PG, PO), BF16)],
        compiler_params=_cparams(48 * 1024 * 1024, ("arbitrary",)),
    )(dylin, d, pw)


def _gate_decay(alow, wa, ba):
    a = jnp.dot(alow, wa, preferred_element_type=F32) + ba
    ls = jax.nn.log_sigmoid(a) * (1.0 / 16.0)
    r = lax.broadcasted_iota(jnp.int32, (CHUNK, CHUNK), 0)
    c = lax.broadcasted_iota(jnp.int32, (CHUNK, CHUNK), 1)
    tri = jnp.where(c <= r, 1.0, 0.0).astype(F32)
    cum = jnp.dot(tri, ls, preferred_element_type=F32, precision=lax.Precision.HIGHEST)
    last = cum[CHUNK - 1:CHUNK, :]
    return a, jnp.exp(last - cum), jnp.exp(last)


def gla_fwd(pcat, wa, ba, ng, after=None):
    afters = _as_list(after)

    def body(q_ref, k_ref, v_ref, g_ref, al_ref, wa_ref, ba_ref, ng_ref, *rest):
        og_ref, o_ref, st_ref, s_scr = rest[len(afters):]

        @pl.when(pl.program_id(0) == 0)
        def _():
            s_scr[...] = jnp.zeros_like(s_scr)

        _, e, decay = _gate_decay(al_ref[...], wa_ref[...], ba_ref[...])
        kd = (k_ref[...].astype(F32) * e).astype(BF16)
        qs = (q_ref[...].astype(F32) * (DK ** -0.5)).astype(BF16)
        for h in range(HEADS):
            ck = slice(h * DK, (h + 1) * DK)
            cv = slice(h * DV, (h + 1) * DV)
            s_new = s_scr[h] * decay[:, ck] + lax.dot_general(v_ref[:, cv], kd[:, ck], (TN, ((), ())),
                                                               preferred_element_type=F32)
            s_scr[h] = s_new
            sb = s_new.astype(BF16)
            st_ref[h] = sb
            oh = lax.dot_general(qs[:, ck], sb, (NT, ((), ())), preferred_element_type=F32)
            o_ref[:, cv] = oh.astype(BF16)
            on = oh * lax.rsqrt(jnp.mean(oh * oh, axis=-1, keepdims=True) + EPS) * ng_ref[:, cv]
            gv = g_ref[:, cv].astype(F32)
            og_ref[:, cv] = (on * (gv * _sigmoid(gv))).astype(BF16)

    row = lambda c: (c, 0)
    return pl.pallas_call(
        body, name="gla_fwd", grid=(NCHUNK,),
        in_specs=[pl.BlockSpec((CHUNK, QK), lambda c: (c, OQ // QK)), pl.BlockSpec((CHUNK, QK), lambda c: (c, OKK // QK)),
                  pl.BlockSpec((CHUNK, D), lambda c: (c, OV // D)), pl.BlockSpec((CHUNK, D), lambda c: (c, OG // D)),
                  pl.BlockSpec((CHUNK, APAD), lambda c: (c, OA // APAD)),
                  pl.BlockSpec((APAD, QK), lambda c: (0, 0)), pl.BlockSpec((1, QK), lambda c: (0, 0)),
                  pl.BlockSpec((1, D), lambda c: (0, 0))] + [ANY] * len(afters),
        out_specs=[pl.BlockSpec((CHUNK, D), row), pl.BlockSpec((CHUNK, D), row),
                   pl.BlockSpec((None, HEADS, DV, DK), lambda c: (c, 0, 0, 0))],
        out_shape=[SDS((T, D), BF16), SDS((T, D), BF16), SDS((NCHUNK, HEADS, DV, DK), BF16)],
        scratch_shapes=[pltpu.VMEM((HEADS, DV, DK), F32)],
        compiler_params=_cparams(32 * 1024 * 1024, ("arbitrary",)),
    )(pcat, pcat, pcat, pcat, pcat, wa, ba, ng, *afters)


def gla_bwd(do, pcat, states, wa, ba, after):
    def body(do_ref, q_ref, k_ref, v_ref, al_ref, sc_ref, sp_ref, wa_ref, ba_ref, after_ref,
             dq_ref, dk_ref, dv_ref, dal_ref, dwa_ref, dba_ref, ds_scr):
        i = pl.program_id(0)

        @pl.when(i == 0)
        def _():
            ds_scr[...] = jnp.zeros_like(ds_scr)

        has_prev = jnp.where(i < NCHUNK - 1, 1.0, 0.0).astype(F32)
        a, e, decay = _gate_decay(al_ref[...], wa_ref[...], ba_ref[...])
        kf = k_ref[...].astype(F32)
        kdf = kf * e
        kd = kdf.astype(BF16)
        qs = (q_ref[...].astype(F32) * (DK ** -0.5)).astype(BF16)
        dkd_parts, ddecay_parts = [], []
        for h in range(HEADS):
            ck = slice(h * DK, (h + 1) * DK)
            cv = slice(h * DV, (h + 1) * DV)
            doh = do_ref[:, cv]
            ds = ds_scr[h] + lax.dot_general(doh, qs[:, ck], (TN, ((), ())), preferred_element_type=F32)
            dsb = ds.astype(BF16)
            dq_ref[:, ck] = (jnp.dot(doh, sc_ref[h], preferred_element_type=F32) * (DK ** -0.5)).astype(BF16)
            dkd_parts.append(jnp.dot(v_ref[:, cv], dsb, preferred_element_type=F32))
            dv_ref[:, cv] = lax.dot_general(kd[:, ck], dsb, (NT, ((), ())), preferred_element_type=F32).astype(BF16)
            ddecay_parts.append(jnp.sum(ds * sp_ref[h].astype(F32), axis=0, keepdims=True) * has_prev)
            ds_scr[h] = ds * decay[:, ck]
        dkd = jnp.concatenate(dkd_parts, axis=1)
        ddecay = jnp.concatenate(ddecay_parts, axis=1)
        dk_ref[...] = (dkd * e).astype(BF16)
        dearg = dkd * kdf
        dlast = jnp.sum(dearg, axis=0, keepdims=True) + ddecay * decay
        r = lax.broadcasted_iota(jnp.int32, (CHUNK, CHUNK), 0)
        c = lax.broadcasted_iota(jnp.int32, (CHUNK, CHUNK), 1)
        triu = jnp.where(c >= r, 1.0, 0.0).astype(F32)
        dls = dlast - jnp.dot(triu, dearg, preferred_element_type=F32, precision=lax.Precision.HIGHEST)
        da = dls * (1.0 / 16.0) * (1.0 - _sigmoid(a))
        dab = da.astype(BF16)
        dal_ref[...] = lax.dot_general(dab, wa_ref[...], (NT, ((), ())), preferred_element_type=F32).astype(BF16)
        dwa = lax.dot_general(al_ref[...], dab, (TN, ((), ())), preferred_element_type=F32)
        dba = jnp.sum(da, axis=0, keepdims=True)

        @pl.when(i == 0)
        def _():
            dwa_ref[...] = dwa
            dba_ref[...] = dba

        @pl.when(i > 0)
        def _():
            dwa_ref[...] += dwa
            dba_ref[...] += dba

    rev = lambda i: NCHUNK - 1 - i
    return pl.pallas_call(
        body, name="gla_bwd", grid=(NCHUNK,),
        in_specs=[pl.BlockSpec((CHUNK, D), lambda i: (rev(i), 0)),
                  pl.BlockSpec((CHUNK, QK), lambda i: (rev(i), OQ // QK)), pl.BlockSpec((CHUNK, QK), lambda i: (rev(i), OKK // QK)),
                  pl.BlockSpec((CHUNK, D), lambda i: (rev(i), OV // D)), pl.BlockSpec((CHUNK, APAD), lambda i: (rev(i), OA // APAD)),
                  pl.BlockSpec((None, HEADS, DV, DK), lambda i: (rev(i), 0, 0, 0)),
                  pl.BlockSpec((None, HEADS, DV, DK), lambda i: (jnp.maximum(rev(i) - 1, 0), 0, 0, 0)),
                  pl.BlockSpec((APAD, QK), lambda i: (0, 0)), pl.BlockSpec((1, QK), lambda i: (0, 0)), ANY],
        out_specs=[pl.BlockSpec((CHUNK, QK), lambda i: (rev(i), 0)), pl.BlockSpec((CHUNK, QK), lambda i: (rev(i), 0)),
                   pl.BlockSpec((CHUNK, D), lambda i: (rev(i), 0)), pl.BlockSpec((CHUNK, APAD), lambda i: (rev(i), 0)),
                   pl.BlockSpec((APAD, QK), lambda i: (0, 0)), pl.BlockSpec((1, QK), lambda i: (0, 0))],
        out_shape=[SDS((T, QK), BF16), SDS((T, QK), BF16), SDS((T, D), BF16), SDS((T, APAD), BF16),
                   SDS((APAD, QK), F32), SDS((1, QK), F32)],
        scratch_shapes=[pltpu.VMEM((HEADS, DV, DK), F32)],
        compiler_params=_cparams(32 * 1024 * 1024, ("arbitrary",)),
    )(do, pcat, pcat, pcat, pcat, states, states, wa, ba, after)


TMF = 256
TMW = 512
_rowblk = ((TMF, D), lambda j, i, k: (i, 0))
_vec = ((1, D), lambda j, i, k: (0, 0))


def _full_spec(col):
    return ((TMF, D), lambda j, i, k: (i, col))


TBIG = 1024


def square_matmul(name, a, b, *, a_spec, b_spec, cdims, nk, after=None):
    def epi(acc, ex, outs, i):
        outs[0][...] = acc

    return matmul(name, a, b, a_spec=a_spec, b_spec=b_spec, cdims=cdims, grid=(D // TBIG, T // TBIG, nk),
                  acc_shape=(TBIG, TBIG), outs=[((T, D), F32, (TBIG, TBIG), lambda j, i, k: (i, j))], epi=epi,
                  after=after)[0]


def rowwise(name, y, *, extras, outs, epi):
    ne = len(extras)

    def body(*refs):
        epi(refs[0][...], refs[1:1 + ne], refs[1 + ne:], pl.program_id(1))

    in_specs = [pl.BlockSpec(*_rowblk)] + [pl.BlockSpec(bs, im) for _, bs, im in extras]
    return pl.pallas_call(
        body, name=name, grid=(1, T // TMF, 1), in_specs=in_specs,
        out_specs=[pl.BlockSpec(bs, im) for _, _, bs, im in outs], out_shape=[SDS(s, dt) for s, dt, _, _ in outs],
        compiler_params=_cparams(40 * 1024 * 1024, ("arbitrary", "arbitrary", "arbitrary")),
    )(y, *[arr for arr, _, _ in extras])


def mm_gla_out(og, w, ylin, pcat, pscale):
    def epi(acc, ex, outs, i):
        ylin_ref, lgp_ref, lgg_ref, ps_ref = ex
        for c0 in range(0, D, EPI_COLS):
            cs = slice(c0, c0 + EPI_COLS)
            gp = _sigmoid(lgp_ref[:, cs].astype(F32))
            gg = _sigmoid(lgg_ref[:, cs].astype(F32))
            a = acc[:, cs]
            outs[0][:, cs] = (gp * (ylin_ref[:, cs].astype(F32) * ps_ref[:, cs]) + gg * a).astype(BF16)
            outs[1][:, cs] = a.astype(BF16)

    return matmul("mm_gla_out", og, w, a_spec=_rowblk, b_spec=((D, D), lambda j, i, k: (0, 0)), cdims=NN,
                  grid=(1, T // TMF, 1), acc_shape=(TMF, D),
                  extras=[(ylin, *_rowblk), (pcat, *_full_spec(OGP // D)), (pcat, *_full_spec(OGG // D)), (pscale, *_vec)],
                  outs=[((T, D), BF16, *_rowblk), ((T, D), BF16, *_rowblk)], epi=epi)


def mm_out(mixed, w, x, g2):
    def epi(acc, ex, outs, i):
        x_ref, g_ref = ex
        x2 = x_ref[...] + acc
        r = lax.rsqrt(jnp.mean(x2 * x2, axis=-1, keepdims=True) + EPS)
        outs[0][...] = x2
        outs[1][...] = (x2 * r * g_ref[...]).astype(BF16)

    return matmul("mm_out", mixed, w, a_spec=_rowblk, b_spec=((D, D), lambda j, i, k: (0, 0)), cdims=NN,
                  grid=(1, T // TMF, 1), acc_shape=(TMF, D), extras=[(x, *_rowblk), (g2, *_vec)],
                  outs=[((T, D), F32, *_rowblk), ((T, D), BF16, *_rowblk)], epi=epi)


def mm_up(h2, wup):
    def epi(acc, ex, outs, i):
        r = jnp.maximum(acc, 0.0)
        outs[0][...] = r.astype(BF16)
        outs[1][...] = (r * r).astype(BF16)

    blk = ((TMW, D), lambda j, i, k: (i, j))
    return matmul("mm_up", h2, wup, a_spec=((TMW, D), lambda j, i, k: (i, 0)), b_spec=((None, D, D), lambda j, i, k: (j, 0, 0)),
                  cdims=NN, grid=(NCHIP, T // TMW, 1), acc_shape=(TMW, D),
                  outs=[((T, DFF), BF16, *blk), ((T, DFF), BF16, *blk)], epi=epi)


def mm_down(act, wdown, x2, tgt, gf):
    tk = 4096

    def epi(acc, ex, outs, i):
        x2_ref, t_ref, g_ref = ex
        dx_ref, dxb_ref, gnf_ref, loss_ref = outs
        x3 = x2_ref[...] + acc
        r = lax.rsqrt(jnp.mean(x3 * x3, axis=-1, keepdims=True) + EPS)
        xn = x3 * r
        err = xn * g_ref[...] - t_ref[...]
        lsum = 0.5 * jnp.sum(jnp.mean(err * err, axis=-1, keepdims=True), axis=0, keepdims=True)
        dy = err * (1.0 / D)
        _row_acc(gnf_ref, jnp.sum(dy * xn, axis=0, keepdims=True), i)
        _row_acc(loss_ref, jnp.broadcast_to(lsum, (1, 128)), i)
        dx3 = _rms_bwd(xn, r, dy * g_ref[...])
        dx_ref[...] = dx3
        dxb_ref[...] = dx3.astype(BF16)

    y = square_matmul("mm_down", act, wdown, a_spec=((TBIG, tk), lambda j, i, k: (i, k)),
                      b_spec=((tk, TBIG), lambda j, i, k: (k, j)), cdims=NN, nk=DFF // tk)
    return rowwise("rows_final", y, extras=[(x2, *_rowblk), (tgt, *_rowblk), (gf, *_vec)],
                   outs=[((T, D), F32, *_rowblk), ((T, D), BF16, *_rowblk), ((1, D), F32, *_vec),
                         ((1, 128), F32, (1, 128), lambda j, i, k: (0, 0))], epi=epi)


def mm_dact(dx3b, wdown, rup, after=None):
    def epi(acc, ex, outs, i):
        outs[0][...] = (acc * 2.0 * ex[0][...].astype(F32)).astype(BF16)

    blk = ((TMW, D), lambda j, i, k: (i, j))
    return matmul("mm_dact", dx3b, wdown, a_spec=((TMW, D), lambda j, i, k: (i, 0)), b_spec=((D, D), lambda j, i, k: (j, 0)),
                  cdims=NT, grid=(DFF // D, T // TMW, 1), acc_shape=(TMW, D), extras=[(rup, *blk)],
                  outs=[((T, DFF), BF16, *blk)], epi=epi, after=after)[0]


def mm_wgrad(name, a, b, m, n, out_shape, out_block, out_map, tm, tn, after=None):
    def epi(acc, ex, outs, i):
        outs[0][...] = acc.astype(BF16).reshape(outs[0].shape)

    return matmul(name, a, b, a_spec=((T, tm), lambda j, i, k: (0, i)), b_spec=((T, tn), lambda j, i, k: (0, j)),
                  cdims=TN, grid=(n // tn, m // tm, 1), acc_shape=(tm, tn),
                  outs=[(out_shape, BF16, out_block, out_map)], epi=epi, after=after)[0]


def mm_dh2(dup, wup, x2, dx3, g2, after=None):
    def epi(acc, ex, outs, i):
        x2_ref, dx3_ref, g_ref = ex
        x2 = x2_ref[...]
        r = lax.rsqrt(jnp.mean(x2 * x2, axis=-1, keepdims=True) + EPS)
        xn = x2 * r
        _row_acc(outs[2], jnp.sum(acc * xn, axis=0, keepdims=True), i)
        dx2 = dx3_ref[...] + _rms_bwd(xn, r, acc * g_ref[...])
        outs[0][...] = dx2
        outs[1][...] = dx2.astype(BF16)

    y = square_matmul("mm_dh2", dup, wup, a_spec=((TBIG, D), lambda j, i, k: (i, k)),
                      b_spec=((None, TBIG, D), lambda j, i, k: (k, j, 0)), cdims=NT, nk=NCHIP, after=after)
    return rowwise("rows_dh2", y, extras=[(x2, *_rowblk), (dx3, *_rowblk), (g2, *_vec)],
                   outs=[((T, D), F32, *_rowblk), ((T, D), BF16, *_rowblk), ((1, D), F32, *_vec)], epi=epi)


def mm_dmixed(dx2b, wout, pcat, ylin, ygla, pscale, after=None):
    def epi(acc, ex, outs, i):
        lgp_ref, lgg_ref, ylin_ref, ygla_ref, ps_ref = ex
        dps = []
        for c0 in range(0, D, EPI_COLS):
            cs = slice(c0, c0 + EPI_COLS)
            gp = _sigmoid(lgp_ref[:, cs].astype(F32))
            gg = _sigmoid(lgg_ref[:, cs].astype(F32))
            yl = ylin_ref[:, cs].astype(F32)
            ps = ps_ref[:, cs]
            a = acc[:, cs]
            agp = a * gp
            outs[0][:, cs] = (agp * ps).astype(BF16)
            outs[1][:, cs] = (a * gg).astype(BF16)
            outs[2][:, cs] = (agp * (yl * ps) * (1.0 - gp)).astype(BF16)
            outs[3][:, cs] = (a * ygla_ref[:, cs].astype(F32) * gg * (1.0 - gg)).astype(BF16)
            dps.append(jnp.sum(agp * yl, axis=0, keepdims=True))
        _row_acc(outs[4], jnp.concatenate(dps, axis=1), i)

    return matmul("mm_dmixed", dx2b, wout, a_spec=_rowblk, b_spec=((D, D), lambda j, i, k: (0, 0)), cdims=NT,
                  grid=(1, T // TMF, 1), acc_shape=(TMF, D),
                  extras=[(pcat, *_full_spec(OGP // D)), (pcat, *_full_spec(OGG // D)), (ylin, *_rowblk), (ygla, *_rowblk),
                          (pscale, *_vec)],
                  outs=[((T, D), BF16, *_rowblk)] * 4 + [((1, D), F32, *_vec)], epi=epi, after=after)


def mm_dog(dygla, wgo, o, pcat, ng, after=None):
    def epi(acc, ex, outs, i):
        o_ref, g_ref, ng_ref = ex
        do_ref, dg_ref, gng_ref = outs
        gparts = []
        for h in range(HEADS):
            cv = slice(h * DV, (h + 1) * DV)
            oh = o_ref[:, cv].astype(F32)
            r = lax.rsqrt(jnp.mean(oh * oh, axis=-1, keepdims=True) + EPS)
            on = oh * r
            gv = g_ref[:, cv].astype(F32)
            sg = _sigmoid(gv)
            a = acc[:, cv]
            dgain = a * (gv * sg)
            gparts.append(jnp.sum(dgain * on, axis=0, keepdims=True))
            ngh = ng_ref[:, cv]
            do_ref[:, cv] = _rms_bwd(on, r, dgain * ngh).astype(BF16)
            dg_ref[:, cv] = (a * (on * ngh) * (sg * (1.0 + gv * (1.0 - sg)))).astype(BF16)
        _row_acc(gng_ref, jnp.concatenate(gparts, axis=1), i)

    return matmul("mm_dog", dygla, wgo, a_spec=_rowblk, b_spec=((D, D), lambda j, i, k: (0, 0)), cdims=NT,
                  grid=(1, T // TMF, 1), acc_shape=(TMF, D),
                  extras=[(o, *_rowblk), (pcat, *_full_spec(OG // D)), (ng, *_vec)],
                  outs=[((T, D), BF16, *_rowblk), ((T, D), BF16, *_rowblk), ((1, D), F32, *_vec)], epi=epi, after=after)


def mm_dh1(dpcat, wcat, x, dx2, g1, after=None):
    tk = 3840

    def epi(acc, ex, outs, i):
        x_ref, dx2_ref, g_ref = ex
        xv = x_ref[...]
        r = lax.rsqrt(jnp.mean(xv * xv, axis=-1, keepdims=True) + EPS)
        xn = xv * r
        _row_acc(outs[1], jnp.sum(acc * xn, axis=0, keepdims=True), i)
        outs[0][...] = dx2_ref[...] + _rms_bwd(xn, r, acc * g_ref[...])

    y = square_matmul("mm_dh1", dpcat, wcat, a_spec=((TBIG, tk), lambda j, i, k: (i, k)),
                      b_spec=((TBIG, tk), lambda j, i, k: (j, k)), cdims=NT, nk=NCAT // tk, after=after)
    return rowwise("rows_dh1", y, extras=[(x, *_rowblk), (dx2, *_rowblk), (g1, *_vec)],
                   outs=[((T, D), F32, *_rowblk), ((1, D), F32, *_vec)], epi=epi)


def _tile_rows(rows, cols, n_arrays):
    tm = rows
    while tm % 32 == 0 and 2 * n_arrays * tm * cols * 4 > 36 * 1024 * 1024:
        tm //= 2
    return tm


def add_pairs(name, parts, theirs, core):
    _, _, r, c = parts.shape
    tm = _tile_rows(r, c, 3)

    def body(core_ref, a_ref, b_ref, o_ref):
        o_ref[...] = (a_ref[...].astype(F32) + b_ref[...].astype(F32)).astype(BF16)

    spec = pl.BlockSpec((None, tm, c), lambda j, i, core_ref: (j, i, 0))
    grid_spec = pltpu.PrefetchScalarGridSpec(
        num_scalar_prefetch=1, grid=(NCHIP, r // tm),
        in_specs=[pl.BlockSpec((None, None, tm, c), lambda j, i, core_ref: (core_ref[0], j, i, 0)), spec], out_specs=spec)
    return pl.pallas_call(body, name=name, grid_spec=grid_spec, out_shape=SDS((NCHIP, r, c), BF16),
                          compiler_params=_cparams(40 * 1024 * 1024, ("arbitrary", "arbitrary")))(core, parts, theirs)


def sum_chips(name, sums, landed, chip):
    _, r, c = sums.shape
    tm = _tile_rows(r, c, 4)

    def body(chip_ref, own_ref, l_ref, o_ref):
        s = own_ref[...].astype(F32)
        for t in range(NCHIP - 1):
            s = s + l_ref[t].astype(F32)
        o_ref[...] = s

    grid_spec = pltpu.PrefetchScalarGridSpec(
        num_scalar_prefetch=1, grid=(r // tm,),
        in_specs=[pl.BlockSpec((None, tm, c), lambda i, chip_ref: (chip_ref[0], i, 0)),
                  pl.BlockSpec((NCHIP - 1, tm, c), lambda i, chip_ref: (0, i, 0))],
        out_specs=pl.BlockSpec((tm, c), lambda i, chip_ref: (i, 0)))
    return pl.pallas_call(body, name=name, grid_spec=grid_spec, out_shape=SDS((r, c), F32),
                          compiler_params=_cparams(40 * 1024 * 1024, ("arbitrary",)))(chip, sums, landed)


def _adamw_math(wv, gv, mv, vv):
    mn = ADAM_B1 * mv + (1.0 - ADAM_B1) * gv
    vn = ADAM_B2 * vv + (1.0 - ADAM_B2) * (gv * gv)
    mh = mn / (1.0 - ADAM_B1 ** ADAM_STEP)
    vh = vn / (1.0 - ADAM_B2 ** ADAM_STEP)
    return -ADAM_LR * (mh / (jnp.sqrt(vh) + ADAM_EPS) + ADAM_WD * wv), mn, vn


def adamw(name, w, g, m, v):
    def body(w_ref, g_ref, m_ref, v_ref, go_ref, d_ref, mo_ref, vo_ref):
        gv = g_ref[...]
        go_ref[...] = gv
        d_ref[...], mo_ref[...], vo_ref[...] = _adamw_math(w_ref[...], gv, m_ref[...], v_ref[...])

    return pl.pallas_call(body, name=name, out_shape=[SDS(w.shape, F32)] * 4)(w, g, m, v)


def adamw_halves(name, w, g_own, g_sib, m, v, core):
    _, r, c = w.shape
    tm = _tile_rows(r, c, 10)

    def body(core_ref, w_ref, go_ref, gs_ref, m_ref, v_ref, g_out, d_out, m_out, v_out):
        gv = jnp.where(pl.program_id(0) == core_ref[0], go_ref[...], gs_ref[...])
        g_out[...] = gv
        d_out[...], m_out[...], v_out[...] = _adamw_math(w_ref[...], gv, m_ref[...], v_ref[...])

    full = pl.BlockSpec((None, tm, c), lambda h, i, core_ref: (h, i, 0))
    own = pl.BlockSpec((tm, c), lambda h, i, core_ref: (jnp.where(h == core_ref[0], i, 0), 0))
    sib = pl.BlockSpec((tm, c), lambda h, i, core_ref: (jnp.where(h == core_ref[0], 0, i), 0))
    grid_spec = pltpu.PrefetchScalarGridSpec(num_scalar_prefetch=1, grid=(2, r // tm),
                                             in_specs=[full, own, sib, full, full], out_specs=[full] * 4)
    return pl.pallas_call(body, name=name, grid_spec=grid_spec, out_shape=[SDS(w.shape, F32)] * 4,
                          compiler_params=_cparams(48 * 1024 * 1024, ("arbitrary", "arbitrary")))(core, w, g_own, g_sib, m, v)


def cast_bf16(name, w):
    _, r, c = w.shape
    tm = _tile_rows(r, c, 2)

    def body(w_ref, o_ref):
        o_ref[...] = w_ref[...].astype(BF16)

    spec = pl.BlockSpec((None, tm, c), lambda h, i: (h, i, 0))
    return pl.pallas_call(body, name=name, grid=(2, r // tm), in_specs=[spec], out_specs=spec, out_shape=SDS(w.shape, BF16),
                          compiler_params=_cparams(40 * 1024 * 1024, ("arbitrary", "arbitrary")))(w)


def pack_rows(name, parts, rows, after=None):
    width = parts[0].shape[1]
    n = len(parts)
    afters = _as_list(after)

    def body(*refs):
        out_ref = refs[n + len(afters)]
        out_ref[...] = jnp.zeros_like(out_ref)
        off = 0
        for p in refs[:n]:
            out_ref[off:off + p.shape[0], :] = p[...]
            off += p.shape[0]

    vm = pl.BlockSpec(memory_space=pltpu.VMEM)
    return pl.pallas_call(body, name=name, in_specs=[vm] * n + [ANY] * len(afters), out_specs=vm,
                          out_shape=SDS((rows, width), F32))(*parts, *afters)


def _place():
    x, y, c = lax.axis_index("x"), lax.axis_index("y"), lax.axis_index("c")
    chips = [(1 - x, y), (x, 1 - y), (1 - x, 1 - y)]
    return x, y, c, chips


def _row_split(shape, dtype):
    r, c = shape
    n = 1
    while r % (2 * n) == 0 and (r // (2 * n)) % 16 == 0 and (r // n) * c * jnp.dtype(dtype).itemsize > PIECE_BYTES:
        n *= 2
    return [pl.ds(s * (r // n), r // n) for s in range(n)]


def _pieces(ref):
    *lead, r, c = ref.shape
    split = _row_split((r, c), ref.dtype)
    return [ref.at[(*idx, s)] for idx in itertools.product(*[range(d) for d in lead]) for s in split]


HBM = pl.BlockSpec(memory_space=pltpu.HBM)
SEM = pl.BlockSpec(memory_space=pltpu.SEMAPHORE)
EFFECT = pltpu.SideEffectType.DATAFLOW_SIDE_EFFECTING


def gather_start(name, shards, after=None):
    n = len(shards)
    afters = _as_list(after)

    def body(*refs):
        src, land = refs[:n], refs[n:2 * n]
        send, recv = refs[2 * n + len(afters)], refs[2 * n + len(afters) + 1]
        x, y, c, chips = _place()
        me = 2 * x + y
        for a in range(n):
            for j, (cx, cy) in enumerate(chips[:2]):
                for sp, dp in zip(_pieces(src[a].at[c]), _pieces(land[a].at[me, c])):
                    pltpu.make_async_remote_copy(sp, dp, send.at[2 * a + j], recv.at[2 * a + j],
                                                 device_id=(cx, cy, c), device_id_type=MESH).start()

    lands = [pltpu.with_memory_space_constraint(lax.empty((NCHIP,) + s.shape, s.dtype), pltpu.HBM) for s in shards]
    srcs = [pltpu.with_memory_space_constraint(s, pltpu.HBM) for s in shards]
    outs = pl.pallas_call(
        body, name=name,
        out_shape=(pltpu.SemaphoreType.DMA((2 * n,)), pltpu.SemaphoreType.DMA((2 * n,)),
                   *[pltpu.HBM(s.shape, s.dtype) for s in shards], *[pltpu.HBM(l.shape, l.dtype) for l in lands]),
        in_specs=[HBM] * (2 * n) + [ANY] * len(afters), out_specs=(SEM, SEM, *([HBM] * (2 * n))),
        input_output_aliases={i: 2 + i for i in range(2 * n)},
        compiler_params=pltpu.CompilerParams(has_side_effects=EFFECT),
    )(*srcs, *lands, *afters)
    return outs[0], outs[1], list(outs[2:2 + n]), list(outs[2 + n:2 + 2 * n])


def _relay_blocks(land, c, chips):
    (xx, xy), (yx, yy), (dx, dy) = chips
    rows = land.shape[2] // 2
    upper, lower = pl.ds(0, rows), pl.ds(rows, rows)
    return [(land.at[2 * yx + yy, c, lower], land.at[2 * dx + dy, c, lower]),
            (land.at[2 * xx + xy, c, upper], land.at[2 * dx + dy, c, upper])]


def relay_turn(name, send, recv, shards, lands, after):
    n = len(shards)
    afters = _as_list(after)

    def body(*refs):
        src, had = refs[:n], refs[n:2 * n]
        send_ref, recv_ref = refs[2 * n], refs[2 * n + 1]
        rsend, rrecv = refs[2 * n + 2 + len(afters)], refs[2 * n + 3 + len(afters)]
        land = refs[3 * n + 4 + len(afters):4 * n + 4 + len(afters)]
        x, y, c, chips = _place()
        for a in range(n):
            for j, (cx, cy) in enumerate(chips[:2]):
                cp = pltpu.make_async_remote_copy(src[a].at[c], had[a].at[2 * cx + cy, c], send_ref.at[2 * a + j],
                                                  recv_ref.at[2 * a + j], device_id=(cx, cy, c), device_id_type=MESH)
                cp.wait_send()
                cp.wait_recv()
        for a in range(n):
            for j, ((sent, _), (dst, _)) in enumerate(zip(_relay_blocks(had[a], c, chips), _relay_blocks(land[a], c, chips))):
                cx, cy = chips[j]
                for sp, dp in zip(_pieces(sent), _pieces(dst)):
                    pltpu.make_async_remote_copy(sp, dp, rsend.at[2 * a + j], rrecv.at[2 * a + j],
                                                 device_id=(cx, cy, c), device_id_type=MESH).start()

    outs = pl.pallas_call(
        body, name=name,
        out_shape=(pltpu.SemaphoreType.DMA((2 * n,)), pltpu.SemaphoreType.DMA((2 * n,)),
                   *[pltpu.HBM(s.shape, s.dtype) for s in shards], *[pltpu.HBM(l.shape, l.dtype) for l in lands]),
        in_specs=[HBM] * (2 * n) + [SEM, SEM] + [ANY] * len(afters), out_specs=(SEM, SEM, *([HBM] * (2 * n))),
        input_output_aliases={i: 2 + i for i in range(2 * n)},
        compiler_params=pltpu.CompilerParams(has_side_effects=EFFECT),
    )(*shards, *lands, send, recv, *afters)
    return outs[0], outs[1], list(outs[2:2 + n]), list(outs[2 + n:2 + 2 * n])


def relay_wait(name, send, recv, lands, after):
    n = len(lands)
    afters = _as_list(after)

    def body(*refs):
        land = refs[:n]
        send_ref, recv_ref = refs[n], refs[n + 1]
        x, y, c, chips = _place()
        for a in range(n):
            for j, (sent, got) in enumerate(_relay_blocks(land[a], c, chips)):
                cx, cy = chips[j]
                cp = pltpu.make_async_remote_copy(sent, got, send_ref.at[2 * a + j], recv_ref.at[2 * a + j],
                                                  device_id=(cx, cy, c), device_id_type=MESH)
                cp.wait_send()
                cp.wait_recv()

    outs = pl.pallas_call(
        body, name=name, out_shape=tuple(pltpu.HBM(l.shape, l.dtype) for l in lands),
        in_specs=[HBM] * n + [SEM, SEM] + [ANY] * len(afters), out_specs=[HBM] * n,
        input_output_aliases={i: i for i in range(n)},
        compiler_params=pltpu.CompilerParams(has_side_effects=EFFECT),
    )(*lands, send, recv, *afters)
    return list(outs)


def forward_halves(name, shards, lands):
    n = len(lands)

    def body(*refs):
        had, buf = refs[:n], refs[n:2 * n]
        send, recv = refs[2 * n:]
        x, y, c, chips = _place()
        sib = (x, y, 1 - c)
        for a in range(n):
            for j, (cx, cy) in enumerate(chips):
                for sp, dp in zip(_pieces(had[a].at[2 * cx + cy, c]), _pieces(buf[a].at[2 * cx + cy, c])):
                    pltpu.make_async_remote_copy(sp, dp, send.at[3 * a + j], recv.at[3 * a + j], device_id=sib, device_id_type=MESH).start()
        for a in range(n):
            for j, (cx, cy) in enumerate(chips):
                pltpu.make_async_remote_copy(had[a].at[2 * cx + cy, c], buf[a].at[2 * cx + cy, 1 - c], send.at[3 * a + j],
                                             recv.at[3 * a + j], device_id=sib, device_id_type=MESH).wait()

    got = pl.pallas_call(
        body, name=name, in_specs=[ANY] * n, out_specs=[ANY] * n, out_shape=[SDS(l.shape, l.dtype) for l in lands],
        input_output_aliases={i: i for i in range(n)},
        scratch_shapes=[pltpu.SemaphoreType.DMA((3 * n,)), pltpu.SemaphoreType.DMA((3 * n,))],
    )(*lands)
    me = 2 * lax.axis_index("x") + lax.axis_index("y")
    return [lax.dynamic_update_index_in_dim(g, s, me, 0) for g, s in zip(got, shards)]


def exchange_start(name, parts):
    n = len(parts)

    def body(*refs):
        src, got = refs[:n], refs[n:2 * n]
        send, recv = refs[2 * n], refs[2 * n + 1]
        token = refs[4 * n + 2]
        x, y, c, _ = _place()
        sib = (x, y, 1 - c)
        for a in range(n):
            for sp, dp in zip(_pieces(src[a].at[1 - c]), _pieces(got[a])):
                pltpu.make_async_remote_copy(sp, dp, send.at[a], recv.at[a], device_id=sib, device_id_type=MESH).start()
        token[...] = jnp.zeros_like(token)

    lands = [pltpu.with_memory_space_constraint(lax.empty(p.shape[1:], p.dtype), pltpu.HBM) for p in parts]
    srcs = [pltpu.with_memory_space_constraint(p, pltpu.HBM) for p in parts]
    outs = pl.pallas_call(
        body, name=name,
        out_shape=(pltpu.SemaphoreType.DMA((n,)), pltpu.SemaphoreType.DMA((n,)),
                   *[pltpu.HBM(p.shape, p.dtype) for p in parts], *[pltpu.HBM(l.shape, l.dtype) for l in lands],
                   SDS((8, 128), F32)),
        in_specs=[HBM] * (2 * n), out_specs=(SEM, SEM, *([HBM] * (2 * n)), pl.BlockSpec(memory_space=pltpu.VMEM)),
        input_output_aliases={i: 2 + i for i in range(2 * n)},
        compiler_params=pltpu.CompilerParams(has_side_effects=EFFECT),
    )(*srcs, *lands)
    return outs[0], outs[1], list(outs[2:2 + n]), list(outs[2 + n:2 + 2 * n]), outs[2 + 2 * n]


def exchange_wait(name, send, recv, parts, lands, after):
    n = len(parts)
    afters = _as_list(after)

    def body(*refs):
        src, got = refs[:n], refs[n:2 * n]
        send_ref, recv_ref = refs[2 * n], refs[2 * n + 1]
        x, y, c, _ = _place()
        sib = (x, y, 1 - c)
        for a in range(n):
            cp = pltpu.make_async_remote_copy(src[a].at[1 - c], got[a], send_ref.at[a], recv_ref.at[a], device_id=sib, device_id_type=MESH)
            cp.wait_send()
            cp.wait_recv()

    outs = pl.pallas_call(
        body, name=name,
        out_shape=(*[pltpu.HBM(p.shape, p.dtype) for p in parts], *[pltpu.HBM(l.shape, l.dtype) for l in lands]),
        in_specs=[HBM] * (2 * n) + [SEM, SEM] + [ANY] * len(afters), out_specs=[HBM] * (2 * n),
        input_output_aliases={i: i for i in range(2 * n)},
        compiler_params=pltpu.CompilerParams(has_side_effects=EFFECT),
    )(*parts, *lands, send, recv, *afters)
    return list(outs[:n]), list(outs[n:])


def scatter_start(name, parts):
    n = len(parts)

    def body(*refs):
        src, land = refs[:n], refs[n:2 * n]
        send, recv = refs[2 * n], refs[2 * n + 1]
        token = refs[4 * n + 2]
        x, y, c, chips = _place()
        for a in range(n):
            for j, (cx, cy) in enumerate(chips):
                for sp, dp in zip(_pieces(src[a].at[2 * cx + cy]), _pieces(land[a].at[j])):
                    pltpu.make_async_remote_copy(sp, dp, send.at[3 * a + j], recv.at[3 * a + j],
                                                 device_id=(cx, cy, c), device_id_type=MESH).start()
        token[...] = jnp.zeros_like(token)

    lands = [pltpu.with_memory_space_constraint(lax.empty((NCHIP - 1,) + p.shape[1:], p.dtype), pltpu.HBM) for p in parts]
    srcs = [pltpu.with_memory_space_constraint(p, pltpu.HBM) for p in parts]
    outs = pl.pallas_call(
        body, name=name,
        out_shape=(pltpu.SemaphoreType.DMA((3 * n,)), pltpu.SemaphoreType.DMA((3 * n,)),
                   *[pltpu.HBM(p.shape, p.dtype) for p in parts], *[pltpu.HBM(l.shape, l.dtype) for l in lands],
                   SDS((8, 128), F32)),
        in_specs=[HBM] * (2 * n), out_specs=(SEM, SEM, *([HBM] * (2 * n)), pl.BlockSpec(memory_space=pltpu.VMEM)),
        input_output_aliases={i: 2 + i for i in range(2 * n)},
        compiler_params=pltpu.CompilerParams(has_side_effects=EFFECT),
    )(*srcs, *lands)
    return outs[0], outs[1], list(outs[2:2 + n]), list(outs[2 + n:2 + 2 * n]), outs[2 + 2 * n]


def scatter_wait(name, send, recv, parts, lands, after):
    n = len(parts)
    afters = _as_list(after)

    def body(*refs):
        src, land = refs[:n], refs[n:2 * n]
        send_ref, recv_ref = refs[2 * n], refs[2 * n + 1]
        x, y, c, chips = _place()
        for a in range(n):
            for j, (cx, cy) in enumerate(chips):
                cp = pltpu.make_async_remote_copy(src[a].at[2 * cx + cy], land[a].at[j], send_ref.at[3 * a + j], recv_ref.at[3 * a + j],
                                                  device_id=(cx, cy, c), device_id_type=MESH)
                cp.wait_send()
                cp.wait_recv()

    outs = pl.pallas_call(
        body, name=name,
        out_shape=(*[pltpu.HBM(p.shape, p.dtype) for p in parts], *[pltpu.HBM(l.shape, l.dtype) for l in lands]),
        in_specs=[HBM] * (2 * n) + [SEM, SEM] + [ANY] * len(afters), out_specs=[HBM] * (2 * n),
        input_output_aliases={i: i for i in range(2 * n)},
        compiler_params=pltpu.CompilerParams(has_side_effects=EFFECT),
    )(*parts, *lands, send, recv, *afters)
    return list(outs[:n]), list(outs[n:])


def join_start(name, halves):
    n = len(halves)

    def body(*refs):
        src, dst = refs[:n], refs[n:2 * n]
        send, recv = refs[2 * n], refs[2 * n + 1]
        token = refs[4 * n + 2]
        x, y, c, _ = _place()
        sib = (x, y, 1 - c)
        for a in range(n):
            for sp, dp in zip(_pieces(src[a]), _pieces(dst[a])):
                pltpu.make_async_remote_copy(sp, dp, send.at[a], recv.at[a], device_id=sib, device_id_type=MESH).start()
        token[...] = jnp.zeros_like(token)

    lands = [pltpu.with_memory_space_constraint(lax.empty(h.shape, h.dtype), pltpu.HBM) for h in halves]
    srcs = [pltpu.with_memory_space_constraint(h, pltpu.HBM) for h in halves]
    outs = pl.pallas_call(
        body, name=name,
        out_shape=(pltpu.SemaphoreType.DMA((n,)), pltpu.SemaphoreType.DMA((n,)),
                   *[pltpu.HBM(h.shape, h.dtype) for h in halves], *[pltpu.HBM(l.shape, l.dtype) for l in lands],
                   SDS((8, 128), F32)),
        in_specs=[HBM] * (2 * n), out_specs=(SEM, SEM, *([HBM] * (2 * n)), pl.BlockSpec(memory_space=pltpu.VMEM)),
        input_output_aliases={i: 2 + i for i in range(2 * n)},
        compiler_params=pltpu.CompilerParams(has_side_effects=EFFECT),
    )(*srcs, *lands)
    return outs[0], outs[1], list(outs[2:2 + n]), list(outs[2 + n:2 + 2 * n]), outs[2 + 2 * n]


def join_wait(name, send, recv, halves, lands, after):
    n = len(halves)
    afters = _as_list(after)

    def body(*refs):
        src, dst = refs[:n], refs[n:2 * n]
        send_ref, recv_ref = refs[2 * n], refs[2 * n + 1]
        x, y, c, _ = _place()
        sib = (x, y, 1 - c)
        for a in range(n):
            cp = pltpu.make_async_remote_copy(src[a], dst[a], send_ref.at[a], recv_ref.at[a], device_id=sib, device_id_type=MESH)
            cp.wait_send()
            cp.wait_recv()

    outs = pl.pallas_call(
        body, name=name,
        out_shape=(*[pltpu.HBM(h.shape, h.dtype) for h in halves], *[pltpu.HBM(l.shape, l.dtype) for l in lands]),
        in_specs=[HBM] * (2 * n) + [SEM, SEM] + [ANY] * len(afters), out_specs=[HBM] * (2 * n),
        input_output_aliases={i: i for i in range(2 * n)},
        compiler_params=pltpu.CompilerParams(has_side_effects=EFFECT),
    )(*halves, *lands, send, recv, *afters)
    return list(outs[:n]), list(outs[n:])


def gather_small(name, xs, reduce, after=None):
    m, ncol = xs.shape
    afters = _as_list(after)

    def body(x_ref, *rest):
        out_ref, all_ref, send, recv, lsem = rest[len(afters):]
        x, y, c, chips = _place()
        me, sib = (x, y, c), (x, y, 1 - c)

        def rows(px, py, pc):
            return all_ref.at[pl.ds((4 * px + 2 * py + pc) * m, m), :]

        def copy(k, block, to, src=None):
            return pltpu.make_async_remote_copy(rows(*block) if src is None else src, rows(*block), send.at[k], recv.at[k],
                                                device_id=to, device_id_type=MESH)

        mine = pltpu.make_async_copy(x_ref, rows(*me), lsem)
        mine.start()
        first = [copy(0, me, sib, src=x_ref)] + [copy(1 + j, me, (*chip, c), src=x_ref) for j, chip in enumerate(chips)]
        for cp in first:
            cp.start()
        passed = [copy(4 + j, (*chip, c), sib) for j, chip in enumerate(chips)]
        for j, chip in enumerate(chips):
            copy(1 + j, (*chip, c), me).wait_recv()
            passed[j].start()
        copy(0, sib, me).wait_recv()
        for j, chip in enumerate(chips):
            copy(4 + j, (*chip, 1 - c), me).wait_recv()
        for cp in first + passed:
            cp.wait_send()
        mine.wait()
        if reduce:
            s = all_ref[0:m, :]
            for dev in range(1, 8):
                s = s + all_ref[dev * m:(dev + 1) * m, :]
            out_ref[...] = s
        else:
            out_ref[...] = all_ref[...]

    vm = pl.BlockSpec(memory_space=pltpu.VMEM)
    return pl.pallas_call(
        body, name=name, in_specs=[vm] + [ANY] * len(afters), out_specs=vm,
        out_shape=SDS((m, ncol) if reduce else (8 * m, ncol), F32),
        scratch_shapes=[pltpu.VMEM((8 * m, ncol), F32), pltpu.SemaphoreType.DMA((7,)), pltpu.SemaphoreType.DMA((7,)),
                        pltpu.SemaphoreType.DMA],
    )(xs, *afters)


RELAYOUT_ROWS = 128


def weights_to_cat(g_in, after=None):
    tm = RELAYOUT_ROWS
    afters = _as_list(after)

    def body(g_ref, *rest):
        o_ref = rest[len(afters)]
        nat = jnp.concatenate([g_ref[j] for j in range(NCHIP)], axis=1)
        pad = jnp.zeros((tm, NCAT - OA - 16), BF16)
        o_ref[...] = jnp.concatenate([nat[:, 3072:7168], nat[:, 7184:11280], nat[:, 0:3072], nat[:, 7168:7184], pad], axis=1)

    return pl.pallas_call(
        body, name="weights_to_cat", grid=(D // tm,),
        in_specs=[pl.BlockSpec((NCHIP, tm, IN_SHARD), lambda i: (0, i, 0))] + [ANY] * len(afters),
        out_specs=pl.BlockSpec((tm, NCAT), lambda i: (i, 0)), out_shape=SDS((D, NCAT), BF16),
        compiler_params=_cparams(40 * 1024 * 1024, ("arbitrary",)),
    )(g_in, *afters)


def grads_from_cat(gw_cat):
    tm = RELAYOUT_ROWS
    nb = (D // 2) // tm

    def body(c_ref, o_ref):
        cat = c_ref[...]
        nat = jnp.concatenate([cat[:, OU:OA], cat[:, OV:OGP], cat[:, OA:OA + 16], cat[:, OGP:OU]], axis=1)
        for j in range(NCHIP):
            o_ref[j] = nat[:, j * IN_SHARD:(j + 1) * IN_SHARD]

    return pl.pallas_call(
        body, name="grads_from_cat", grid=(D // tm,), in_specs=[pl.BlockSpec((tm, NCAT), lambda i: (i, 0))],
        out_specs=pl.BlockSpec((None, NCHIP, tm, IN_SHARD), lambda i: (i // nb, 0, i % nb, 0)),
        out_shape=SDS((2, NCHIP, D // 2, IN_SHARD), BF16), compiler_params=_cparams(40 * 1024 * 1024, ("arbitrary",)),
    )(gw_cat)


def _pad_rows(a, rows):
    return jnp.concatenate([a, jnp.zeros((rows - a.shape[0],) + a.shape[1:], a.dtype)], axis=0)


def local_step(x2d, tgt, gf, g1, pool_scale, wa_pad, b_alpha, ng, g2, get_w, on_grad=None, on_settle=None, tick=None):
    emit = on_grad if on_grad is not None else (lambda group, grads: None)
    settle = on_settle if on_settle is not None else (lambda group, after: None)
    h1 = norm1(x2d, g1)
    wcat, pw = get_w("in", h1)
    pcat = mm_in(h1, wcat)
    dpool, ylin = pool_fwd(pcat, pw)
    pinned = tick("pool", ylin) if tick is not None else None
    og, o, states = gla_fwd(pcat, wa_pad, b_alpha, ng, pinned)
    w_go, w_o = get_w("mid", og)
    mixed, ygla = mm_gla_out(og, w_go, ylin, pcat, pool_scale)
    x2, h2 = mm_out(mixed, w_o, x2d, g2)
    w_up = get_w("up", h2)
    rup, act = mm_up(h2, w_up)
    w_dn = get_w("down", act)
    dx3, dx3b, g_nf, loss_row = mm_down(act, w_dn, x2, tgt, gf)

    gw_down = mm_wgrad("mm_dw_down", act, dx3b, DFF, D, (2, NCHIP, D // 2, D), (None, None, D // 2, D),
                       lambda j, i, k: (i % 2, i // 2, 0, 0), D // 2, D)
    token = emit("down", {"down": gw_down})
    dup = mm_dact(dx3b, w_dn, rup, after=token)
    token = settle("down", dup)
    dx2, dx2b, g_mlp = mm_dh2(dup, w_up, x2, dx3, g2, after=token)
    gw_up = mm_wgrad("mm_dw_up", h2, dup, D, DFF, (2, NCHIP, D // 2, D), (None, None, D // 2, D),
                     lambda j, i, k: (i, j, 0, 0), D // 2, D)
    token = emit("up", {"up": gw_up})
    dylin, dygla, dlgp, dlgg, g_ps = mm_dmixed(dx2b, w_o, pcat, ylin, ygla, pool_scale, after=token)
    token = settle("up", dylin)
    gw_out = mm_wgrad("mm_dw_out", mixed, dx2b, D, D, (2, NCHIP, 256, D), (2, None, 256, D),
                      lambda j, i, k: (0, i, 0, 0), 512, D)
    do, dg, g_ng = mm_dog(dygla, w_go, o, pcat, ng, after=token)
    gw_go = mm_wgrad("mm_dw_gla_out", og, dygla, D, D, (2, NCHIP, 256, D), (2, None, 256, D),
                     lambda j, i, k: (0, i, 0, 0), 512, D)
    token = emit("mix", {"out": gw_out, "gla_out": gw_go})
    dq, dk, dv, dalow, g_wa, g_ba = gla_bwd(do, pcat, states, wa_pad, b_alpha, b_alpha if token is None else token)
    token = settle("mix", dq)
    du, dpw = pool_bwd(dylin, dpool, pw)
    dpcat = jnp.concatenate([dv, dg, dlgp, dlgg, du, dq, dk, dalow, jnp.zeros((T, NCAT - OA - APAD), BF16)], axis=1)
    gw_cat = mm_wgrad("mm_dw_in", h1, dpcat, D, NCAT, (D, NCAT), (1024, 1280), lambda j, i, k: (i, j), 1024, 1280, after=token)
    token = settle("in", emit("in", {"in_cat": gw_cat, "pool": dpw}))
    grad_x, g_mix = mm_dh1(dpcat, wcat, x2d, dx2, g1, after=token)
    return (loss_row[0, 0], grad_x, g_mix, g_ps, g_mlp, g_nf, g_ng, g_ba, g_wa, token,
            gw_cat, dpw, gw_go, gw_out, gw_up, gw_down)


def kernel(x, norm_mix_g, w_in, pool_w, pool_scale, w_alpha, b_alpha, gla_norm_g, w_gla_out, w_out, norm_mlp_g, w_mlp_up, w_mlp_down, norm_final_g, loss_target, m_norm_mix_g, m_w_in, m_pool_w, m_pool_scale, m_w_alpha, m_b_alpha, m_gla_norm_g, m_w_gla_out, m_w_out, m_norm_mlp_g, m_w_mlp_up, m_w_mlp_down, m_norm_final_g, v_norm_mix_g, v_w_in, v_pool_w, v_pool_scale, v_w_alpha, v_b_alpha, v_gla_norm_g, v_w_gla_out, v_w_out, v_norm_mlp_g, v_w_mlp_up, v_w_mlp_down, v_norm_final_g):
    chip = 2 * lax.axis_index("x") + lax.axis_index("y")
    chip_i = chip.astype(jnp.int32).reshape(1)
    core_i = lax.axis_index("c").astype(jnp.int32).reshape(1)
    tgt = loss_target.reshape(T, D)
    gf = norm_final_g.reshape(1, D)

    def halves(w2d):
        r, c = w2d.shape
        return w2d.astype(BF16).reshape(2, r // 2, c)

    pool_shard = pool_w.reshape(4 * PG, PO // NCHIP)
    w_in_r = w_in.reshape(2, D // 2, IN_SHARD)
    sent = {"in": [cast_bf16("cast_w_in", w_in_r), halves(pool_shard)]}
    flight = {}

    def start(group, after=None):
        flight[group] = gather_start("gather_start_" + group, sent[group], after)

    def relay(group, after):
        send, recv, shards, lands = flight[group]
        flight[group] = relay_turn("relay_turn_" + group, send, recv, shards, lands, after)

    def fetch(group, after, then=None):
        send, recv, shards, lands = flight[group]
        lands = relay_wait("relay_wait_" + group, send, recv, lands, after)
        if then is not None:
            then(lands[0])
        return forward_halves("forward_" + group, shards, lands)

    start("in")
    m_in_f, v_in_f, w_go_f, w_o_f, w_up_f, w_dn_f, x_f, wal_f, gng_f = lax.optimization_barrier(
        (m_w_in, v_w_in, w_gla_out, w_out, w_mlp_up, w_mlp_down, x, w_alpha, gla_norm_g, flight["in"][2][0]))[:9]
    m_in_r, v_in_r = m_in_f.reshape(2, D // 2, IN_SHARD), v_in_f.reshape(2, D // 2, IN_SHARD)
    sent["mid"] = [halves(w_go_f[0]), halves(w_o_f[0])]
    relay("in", [m_in_r, v_in_r, *sent["mid"]])
    w_up_f, w_dn_f, x_f, wal_f, gng_f = lax.optimization_barrier(
        (w_up_f, w_dn_f, x_f, wal_f, gng_f, flight["in"][3][0]))[:5]
    sent["up"], sent["down"] = [halves(w_up_f[0])], [halves(w_dn_f[0])]
    x2d = x_f.reshape(T, D)
    big = [w_in_r, w_go_f[0], w_o_f[0], w_up_f[0], w_dn_f[0], pool_shard]

    def tick(point, after):
        if point == "pool":
            relay("mid", after)
            start("down", flight["mid"][3][0])
            return [flight["mid"][3][0], flight["down"][3][0]]

    def get_w(group, after):
        if group == "in":
            after = [after, *sent["up"], *sent["down"], wa_pad]
        if group == "mid":
            relay("up", after)
            after = flight["up"][3][0]
        if group == "up":
            relay("down", after)
            after = flight["down"][3][0]
        if group == "in":
            def next_groups(landed):
                start("mid", landed)
                start("up", flight["mid"][3][0])

            g_in, g_pool = fetch(group, after, next_groups)
            wcat = weights_to_cat(g_in.reshape(NCHIP, D, IN_SHARD), flight["up"][3][0])
            pw = jnp.concatenate([g_pool[j].reshape(4, PG, PO // NCHIP) for j in range(NCHIP)], axis=2)
            return wcat, pw
        whole = fetch(group, after)
        if group == "mid":
            return whole[0].reshape(D, D), whole[1].reshape(D, D)
        if group == "up":
            return whole[0].reshape(NCHIP, D, D)
        return whole[0].reshape(DFF, D)

    small_w = pack_rows("pack_small_w", [wal_f[0].reshape(4, QK),
                                         jnp.concatenate([gng_f[0].reshape(1, 512), jnp.zeros((1, 512), F32)], axis=1)], 8)
    sw_all = gather_small("gather_small_w", small_w, False).reshape(8, 8, QK)
    wa_full = jnp.concatenate([sw_all[2 * j, 0:4].reshape(16, DK) for j in range(NCHIP)], axis=1)
    ng_full = jnp.concatenate([sw_all[2 * j, 4, 0:512].reshape(HEADS, DV // NCHIP) for j in range(NCHIP)], axis=1)
    wa_pad = _pad_rows(wa_full, APAD).astype(BF16)
    ng = ng_full.reshape(1, D)

    pending = {}
    wmv = {"in": (w_in_r, m_in_r, v_in_r), "gla_out": (big[1], m_w_gla_out, v_w_gla_out), "out": (big[2], m_w_out, v_w_out),
           "up": (big[3], m_w_mlp_up, v_w_mlp_up), "down": (big[4], m_w_mlp_down, v_w_mlp_down), "pool": (big[5], m_pool_w, v_pool_w)}
    big_res = {}

    def reduce_group(group, after):
        nms, send, recv, sums, lands = pending[group]
        sums, lands = scatter_wait("scatter_wait_" + group, send, recv, sums, lands, after)
        reduced = [sum_chips("sum_chips_" + nm, a, b, chip_i) for nm, a, b in zip(nms, sums, lands)]
        send, recv, reduced, lands, token = join_start("join_start_" + group, reduced)
        pending[group] = (nms, send, recv, reduced, lands)
        return token

    def update_group(group, after):
        nms, send, recv, reduced, lands = pending[group]
        reduced, from_sib = join_wait("join_wait_" + group, send, recv, reduced, lands, after)
        for nm, g_own, g_sib in zip(nms, reduced, from_sib):
            w, m, v = wmv[nm]
            shp = (2,) + g_own.shape
            big_res[nm] = adamw_halves("adamw_" + nm, w.reshape(shp), g_own, g_sib, m.reshape(shp), v.reshape(shp), core_i)

    def on_grad(group, grads):
        if group == "in":
            gw_in = grads_from_cat(grads["in_cat"])
            gw_pool = jnp.stack([grads["pool"][:, :, j * 128:(j + 1) * 128].reshape(2, 2 * PG, 128)
                                 for j in range(NCHIP)], axis=1)
            grads = {"in": gw_in, "pool": gw_pool}
        nms, parts = list(grads.keys()), list(grads.values())
        send, recv, parts, got, token = exchange_start("exchange_start_" + group, parts)
        pending[group] = (nms, send, recv, parts, got)
        return token

    def on_settle(group, after):
        if group == "in":
            after = reduce_group("down", after)
        nms, send, recv, parts, got = pending[group]
        parts, got = exchange_wait("exchange_wait_" + group, send, recv, parts, got, after)
        sums = [add_pairs("add_pair_" + nm, a, b, core_i) for nm, a, b in zip(nms, parts, got)]
        send, recv, sums, lands, token = scatter_start("scatter_start_" + group, sums)
        pending[group] = (nms, send, recv, sums, lands)
        if group != "in":
            return token
        token = reduce_group("up", token)
        token = reduce_group("mix", token)
        for earlier in ("down", "up", "mix"):
            update_group(earlier, token)
            token = big_res[pending[earlier][0][-1]][1]
        return [big_res[nm][1] for nm in ("down", "up", "out", "gla_out")]

    (loss_local, grad_x, g_mix, g_ps, g_mlp, g_nf, g_ng, g_ba, g_wa) = local_step(
        x2d, tgt, gf, norm_mix_g, pool_scale, wa_pad, b_alpha, ng, norm_mlp_g, get_w, on_grad, on_settle, tick)[:9]
    loss = lax.psum(loss_local, ("x", "y", "c"))
    join_in_token = reduce_group("in", grad_x)

    ROWS = 16

    def wide(a, n):
        return jnp.concatenate([a.reshape(1, n), jnp.zeros((1, D - n), F32)], axis=1)

    packed = pack_rows("pack_small_g", [g_mix, g_ps, g_mlp, g_nf, g_ng, wide(g_ba, QK), g_wa[0:16].reshape(8, D)], ROWS)
    tot = gather_small("reduce_small_g", packed, True, join_in_token)
    t_wa = lax.dynamic_slice(tot[6:14].reshape(16, QK), (0, chip * DK), (16, DK))
    t_ng = lax.dynamic_slice(tot[4].reshape(HEADS, DV), (0, chip * (DV // NCHIP)), (HEADS, DV // NCHIP))

    def pack_small(nm, mix, ps, mlp, nf, ba, wa, gn, after=None):
        return pack_rows(nm, [mix.reshape(1, D), ps.reshape(1, D), mlp.reshape(1, D), nf.reshape(1, D), wide(ba, QK),
                              wa.reshape(2, D), wide(gn, 512)], ROWS, after)

    update_group("in", tot)
    sg = pack_small("pack_g", tot[0], tot[1], tot[2], tot[3], tot[5, 0:QK], t_wa, t_ng, big_res["in"][3])
    sw = pack_small("pack_w", norm_mix_g, pool_scale, norm_mlp_g, norm_final_g, b_alpha, w_alpha, gla_norm_g)
    sm = pack_small("pack_m", m_norm_mix_g, m_pool_scale, m_norm_mlp_g, m_norm_final_g, m_b_alpha, m_w_alpha, m_gla_norm_g)
    sv = pack_small("pack_v", v_norm_mix_g, v_pool_scale, v_norm_mlp_g, v_norm_final_g, v_b_alpha, v_w_alpha, v_gla_norm_g)
    small_res = adamw("adamw_small", sw, sg, sm, sv)

    def unpack(p):
        return {"norm_mix_g": p[0].reshape(1, D), "pool_scale": p[1].reshape(1, D), "norm_mlp_g": p[2].reshape(1, D),
                "norm_final_g": p[3].reshape(D), "b_alpha": p[4, 0:QK].reshape(1, QK), "w_alpha": p[5:7].reshape(1, 16, DK),
                "gla_norm_g": p[7, 0:512].reshape(1, HEADS, DV // NCHIP)}

    order = ["norm_mix_g", "w_in", "pool_w", "pool_scale", "w_alpha", "b_alpha", "gla_norm_g", "w_gla_out", "w_out",
             "norm_mlp_g", "w_mlp_up", "w_mlp_down", "norm_final_g"]
    big_key = {"w_in": ("in", w_in.shape), "pool_w": ("pool", pool_w.shape), "w_gla_out": ("gla_out", w_gla_out.shape),
               "w_out": ("out", w_out.shape), "w_mlp_up": ("up", w_mlp_up.shape), "w_mlp_down": ("down", w_mlp_down.shape)}
    result = [loss, grad_x.reshape(1, T, D)]
    for kind in range(4):
        small = unpack(small_res[kind])
        for nm in order:
            if nm in big_key:
                key, shp = big_key[nm]
                result.append(big_res[key][kind].reshape(shp))
            else:
                result.append(small[nm])
    return tuple(result)
```

```python
import itertools

import jax
import jax.numpy as jnp
from jax import lax
from jax.experimental import pallas as pl
from jax.experimental.pallas import tpu as pltpu

F32 = jnp.float32
BF16 = jnp.bfloat16
SDS = jax.ShapeDtypeStruct
MESH = pl.DeviceIdType.MESH
ANY = pl.BlockSpec(memory_space=pl.ANY)

T = 2048
D = 2048
DFF = 8192
NCHIP = 4
IN_WIDTH = 11280
IN_SHARD = IN_WIDTH // NCHIP
CHUNK = 64
NCHUNK = T // CHUNK
HEADS = 4
DK = 256
DV = 512
QK = HEADS * DK
EPS = 1e-6
POOL_WINDOWS = (2, 4, 8, 16)
PG = 256
PO = 512

OV, OG, OGP, OGG, OU, OQ, OKK, OA = 0, 2048, 4096, 6144, 8192, 9216, 10240, 11264
NCAT = 11520
APAD = 128

VMEM_CAP = 56 * 1024 * 1024

PIECE_BYTES = 384 * 1024

ADAM_LR, ADAM_B1, ADAM_B2, ADAM_EPS, ADAM_WD, ADAM_STEP = 0.001, 0.9, 0.999, 1e-08, 0.01, 10


def _cparams(vmem_bytes=None, sem=None):
    kw = {}
    if vmem_bytes is not None:
        kw["vmem_limit_bytes"] = int(min(max(vmem_bytes, 32 * 1024 * 1024), VMEM_CAP))
    if sem is not None:
        kw["dimension_semantics"] = sem
    return pltpu.CompilerParams(**kw)


def _nbytes(shape, dtype):
    n = 1
    for s in shape:
        if s is not None:
            n *= s
    return n * jnp.dtype(dtype).itemsize


def _sigmoid(x):
    return 0.5 * jnp.tanh(0.5 * x) + 0.5


EPI_COLS = 512


def _as_list(after):
    if after is None:
        return []
    return list(after) if isinstance(after, (list, tuple)) else [after]


def matmul(name, a, b, *, a_spec, b_spec, cdims, grid, acc_shape, outs, extras=(), epi, after=None):
    nj, ni, nk = grid
    ne, no = len(extras), len(outs)
    afters = _as_list(after)
    first_out = 2 + ne + len(afters)

    def body(*refs):
        a_ref, b_ref = refs[0], refs[1]
        ex = refs[2:2 + ne]
        out_refs = refs[first_out:first_out + no]
        i = pl.program_id(1)
        part = lax.dot_general(a_ref[...], b_ref[...], (cdims, ((), ())), preferred_element_type=F32)
        if nk == 1:
            epi(part, ex, out_refs, i)
        else:
            acc_ref = refs[first_out + no]
            k = pl.program_id(2)

            @pl.when(k == 0)
            def _():
                acc_ref[...] = part

            @pl.when(k > 0)
            def _():
                acc_ref[...] += part

            @pl.when(k == nk - 1)
            def _():
                epi(acc_ref[...], ex, out_refs, i)

    in_specs = [pl.BlockSpec(*a_spec), pl.BlockSpec(*b_spec)] + [pl.BlockSpec(bs, im) for _, bs, im in extras]
    in_specs += [ANY] * len(afters)
    out_specs = [pl.BlockSpec(bs, im) for _, _, bs, im in outs]
    out_shape = [SDS(s, dt) for s, dt, _, _ in outs]
    vm = 2 * (_nbytes(a_spec[0], a.dtype) + _nbytes(b_spec[0], b.dtype))
    vm += 2 * sum(_nbytes(bs, arr.dtype) for arr, bs, _ in extras)
    vm += 2 * sum(_nbytes(bs, dt) for _, dt, bs, _ in outs)
    vm += 6 * _nbytes(acc_shape, F32)
    scratch = [pltpu.VMEM(acc_shape, F32)] if nk > 1 else []
    return pl.pallas_call(
        body, name=name, grid=grid, in_specs=in_specs, out_specs=out_specs, out_shape=out_shape,
        scratch_shapes=scratch,
        compiler_params=_cparams(vm, ("arbitrary", "arbitrary", "arbitrary")),
    )(a, b, *[arr for arr, _, _ in extras], *afters)


NN =((1,), (0,))
NT = ((1,), (1,))
TN = ((0,), (0,))


def _row_acc(out_ref, val, i):
    @pl.when(i == 0)
    def _():
        out_ref[...] = val

    @pl.when(i > 0)
    def _():
        out_ref[...] += val


def _rms_bwd(xn, r, dxn):
    return r * (dxn - xn * jnp.mean(dxn * xn, axis=-1, keepdims=True))


def norm1(x, g):
    tm = 256

    def body(x_ref, g_ref, h_ref):
        xv = x_ref[...]
        r = lax.rsqrt(jnp.mean(xv * xv, axis=-1, keepdims=True) + EPS)
        h_ref[...] = (xv * r * g_ref[...]).astype(BF16)

    return pl.pallas_call(
        body, name="norm1", grid=(T // tm,),
        in_specs=[pl.BlockSpec((tm, D), lambda i: (i, 0)), pl.BlockSpec((1, D), lambda i: (0, 0))],
        out_specs=pl.BlockSpec((tm, D), lambda i: (i, 0)), out_shape=SDS((T, D), BF16),
        compiler_params=_cparams(32 * 1024 * 1024, ("arbitrary",)),
    )(x, g)


def mm_in(h1, wcat):
    tm, tn = 1024, 1280

    def epi(acc, ex, outs, i):
        outs[0][...] = acc.astype(BF16)

    return matmul("mm_in", h1, wcat, a_spec=((tm, D), lambda j, i, k: (i, 0)), b_spec=((D, tn), lambda j, i, k: (0, j)),
                  cdims=NN, grid=(NCAT // tn, T // tm, 1), acc_shape=(tm, tn),
                  outs=[((T, NCAT), BF16, (tm, tn), lambda j, i, k: (i, j))], epi=epi)[0]


def _window_sum(x, w, up):
    n = x.shape[0]
    row = lax.broadcasted_iota(jnp.int32, x.shape, 0)
    s, sh = x, 1
    while sh < w:
        if up:
            s = s + jnp.where(row < n - sh, pltpu.roll(s, n - sh, axis=0), 0.0)
        else:
            s = s + jnp.where(row >= sh, pltpu.roll(s, sh, axis=0), 0.0)
        sh *= 2
    return s


def _inv_count(shape, w):
    row = lax.broadcasted_iota(jnp.int32, shape, 0)
    return 1.0 / jnp.minimum(row + 1, w).astype(F32)


def pool_fwd(pcat, pw):
    def body(u_ref, pw_ref, d_ref, y_ref):
        for gi, w in enumerate(POOL_WINDOWS):
            ug = u_ref[:, gi * PG:(gi + 1) * PG].astype(F32)
            dg = _window_sum(ug, w, False) * _inv_count(ug.shape, w) - ug
            db = dg.astype(BF16)
            d_ref[:, gi * PG:(gi + 1) * PG] = db
            y_ref[:, gi * PO:(gi + 1) * PO] = jnp.dot(db, pw_ref[gi], preferred_element_type=F32).astype(BF16)

    return pl.pallas_call(
        body, name="pool_fwd", grid=(1,),
        in_specs=[pl.BlockSpec((T, 4 * PG), lambda i: (0, OU // (4 * PG))), pl.BlockSpec((4, PG, PO), lambda i: (0, 0, 0))],
        out_specs=[pl.BlockSpec((T, 4 * PG), lambda i: (0, 0)), pl.BlockSpec((T, D), lambda i: (0, 0))],
        out_shape=[SDS((T, 4 * PG), BF16), SDS((T, D), BF16)],
        compiler_params=_cparams(48 * 1024 * 1024, ("arbitrary",)),
    )(pcat, pw)


def pool_bwd(dylin, d, pw):
    def body(dy_ref, d_ref, pw_ref, du_ref, dpw_ref):
        for gi, w in enumerate(POOL_WINDOWS):
            dyl = dy_ref[:, gi * PO:(gi + 1) * PO]
            dd = lax.dot_general(dyl, pw_ref[gi], (NT, ((), ())), preferred_element_type=F32)
            du = _window_sum(dd * _inv_count(dd.shape, w), w, True) - dd
            du_ref[:, gi * PG:(gi + 1) * PG] = du.astype(BF16)
            dpw_ref[gi] = lax.dot_general(d_ref[:, gi * PG:(gi + 1) * PG], dyl, (TN, ((), ())),
                                          preferred_element_type=F32).astype(BF16)

    return pl.pallas_call(
        body, name="pool_bwd", grid=(1,),
        in_specs=[pl.BlockSpec((T, D), lambda i: (0, 0)), pl.BlockSpec((T, 4 * PG), lambda i: (0, 0)),
                  pl.BlockSpec((4, PG, PO), lambda i: (0, 0, 0))],
        out_specs=[pl.BlockSpec((T, 4 * PG), lambda i: (0, 0)), pl.BlockSpec((4, PG, PO), lambda i: (0, 0, 0))],
        out_shape=[SDS((T, 4 * PG), BF16), SDS((4, PG, PO), BF16)],
        compiler_params=_cparams(48 * 1024 * 1024, ("arbitrary",)),
    )(dylin, d, pw)


def _gate_decay(alow, wa, ba):
    a = jnp.dot(alow, wa, preferred_element_type=F32) + ba
    ls = jax.nn.log_sigmoid(a) * (1.0 / 16.0)
    r = lax.broadcasted_iota(jnp.int32, (CHUNK, CHUNK), 0)
    c = lax.broadcasted_iota(jnp.int32, (CHUNK, CHUNK), 1)
    tri = jnp.where(c <= r, 1.0, 0.0).astype(F32)
    cum = jnp.dot(tri, ls, preferred_element_type=F32, precision=lax.Precision.HIGHEST)
    last = cum[CHUNK - 1:CHUNK, :]
    return a, jnp.exp(last - cum), jnp.exp(last)


def gla_fwd(pcat, wa, ba, ng, after=None):
    afters = _as_list(after)

    def body(q_ref, k_ref, v_ref, g_ref, al_ref, wa_ref, ba_ref, ng_ref, *rest):
        og_ref, o_ref, st_ref, s_scr = rest[len(afters):]

        @pl.when(pl.program_id(0) == 0)
        def _():
            s_scr[...] = jnp.zeros_like(s_scr)

        _, e, decay = _gate_decay(al_ref[...], wa_ref[...], ba_ref[...])
        kd = (k_ref[...].astype(F32) * e).astype(BF16)
        qs = (q_ref[...].astype(F32) * (DK ** -0.5)).astype(BF16)
        for h in range(HEADS):
            ck = slice(h * DK, (h + 1) * DK)
            cv = slice(h * DV, (h + 1) * DV)
            s_new = s_scr[h] * decay[:, ck] + lax.dot_general(v_ref[:, cv], kd[:, ck], (TN, ((), ())),
                                                               preferred_element_type=F32)
            s_scr[h] = s_new
            sb = s_new.astype(BF16)
            st_ref[h] = sb
            oh = lax.dot_general(qs[:, ck], sb, (NT, ((), ())), preferred_element_type=F32)
            o_ref[:, cv] = oh.astype(BF16)
            on = oh * lax.rsqrt(jnp.mean(oh * oh, axis=-1, keepdims=True) + EPS) * ng_ref[:, cv]
            gv = g_ref[:, cv].astype(F32)
            og_ref[:, cv] = (on * (gv * _sigmoid(gv))).astype(BF16)

    row = lambda c: (c, 0)
    return pl.pallas_call(
        body, name="gla_fwd", grid=(NCHUNK,),
        in_specs=[pl.BlockSpec((CHUNK, QK), lambda c: (c, OQ // QK)), pl.BlockSpec((CHUNK, QK), lambda c: (c, OKK // QK)),
                  pl.BlockSpec((CHUNK, D), lambda c: (c, OV // D)), pl.BlockSpec((CHUNK, D), lambda c: (c, OG // D)),
                  pl.BlockSpec((CHUNK, APAD), lambda c: (c, OA // APAD)),
                  pl.BlockSpec((APAD, QK), lambda c: (0, 0)), pl.BlockSpec((1, QK), lambda c: (0, 0)),
                  pl.BlockSpec((1, D), lambda c: (0, 0))] + [ANY] * len(afters),
        out_specs=[pl.BlockSpec((CHUNK, D), row), pl.BlockSpec((CHUNK, D), row),
                   pl.BlockSpec((None, HEADS, DV, DK), lambda c: (c, 0, 0, 0))],
        out_shape=[SDS((T, D), BF16), SDS((T, D), BF16), SDS((NCHUNK, HEADS, DV, DK), BF16)],
        scratch_shapes=[pltpu.VMEM((HEADS, DV, DK), F32)],
        compiler_params=_cparams(32 * 1024 * 1024, ("arbitrary",)),
    )(pcat, pcat, pcat, pcat, pcat, wa, ba, ng, *afters)


def gla_bwd(do, pcat, states, wa, ba, after):
    def body(do_ref, q_ref, k_ref, v_ref, al_ref, sc_ref, sp_ref, wa_ref, ba_ref, after_ref,
             dq_ref, dk_ref, dv_ref, dal_ref, dwa_ref, dba_ref, ds_scr):
        i = pl.program_id(0)

        @pl.when(i == 0)
        def _():
            ds_scr[...] = jnp.zeros_like(ds_scr)

        has_prev = jnp.where(i < NCHUNK - 1, 1.0, 0.0).astype(F32)
        a, e, decay = _gate_decay(al_ref[...], wa_ref[...], ba_ref[...])
        kf = k_ref[...].astype(F32)
        kdf = kf * e
        kd = kdf.astype(BF16)
        qs = (q_ref[...].astype(F32) * (DK ** -0.5)).astype(BF16)
        dkd_parts, ddecay_parts = [], []
        for h in range(HEADS):
            ck = slice(h * DK, (h + 1) * DK)
            cv = slice(h * DV, (h + 1) * DV)
            doh = do_ref[:, cv]
            ds = ds_scr[h] + lax.dot_general(doh, qs[:, ck], (TN, ((), ())), preferred_element_type=F32)
            dsb = ds.astype(BF16)
            dq_ref[:, ck] = (jnp.dot(doh, sc_ref[h], preferred_element_type=F32) * (DK ** -0.5)).astype(BF16)
            dkd_parts.append(jnp.dot(v_ref[:, cv], dsb, preferred_element_type=F32))
            dv_ref[:, cv] = lax.dot_general(kd[:, ck], dsb, (NT, ((), ())), preferred_element_type=F32).astype(BF16)
            ddecay_parts.append(jnp.sum(ds * sp_ref[h].astype(F32), axis=0, keepdims=True) * has_prev)
            ds_scr[h] = ds * decay[:, ck]
        dkd = jnp.concatenate(dkd_parts, axis=1)
        ddecay = jnp.concatenate(ddecay_parts, axis=1)
        dk_ref[...] = (dkd * e).astype(BF16)
        dearg = dkd * kdf
        dlast = jnp.sum(dearg, axis=0, keepdims=True) + ddecay * decay
        r = lax.broadcasted_iota(jnp.int32, (CHUNK, CHUNK), 0)
        c = lax.broadcasted_iota(jnp.int32, (CHUNK, CHUNK), 1)
        triu = jnp.where(c >= r, 1.0, 0.0).astype(F32)
        dls = dlast - jnp.dot(triu, dearg, preferred_element_type=F32, precision=lax.Precision.HIGHEST)
        da = dls * (1.0 / 16.0) * (1.0 - _sigmoid(a))
        dab = da.astype(BF16)
        dal_ref[...] = lax.dot_general(dab, wa_ref[...], (NT, ((), ())), preferred_element_type=F32).astype(BF16)
        dwa = lax.dot_general(al_ref[...], dab, (TN, ((), ())), preferred_element_type=F32)
        dba = jnp.sum(da, axis=0, keepdims=True)

        @pl.when(i == 0)
        def _():
            dwa_ref[...] = dwa
            dba_ref[...] = dba

        @pl.when(i > 0)
        def _():
            dwa_ref[...] += dwa
            dba_ref[...] += dba

    rev = lambda i: NCHUNK - 1 - i
    return pl.pallas_call(
        body, name="gla_bwd", grid=(NCHUNK,),
        in_specs=[pl.BlockSpec((CHUNK, D), lambda i: (rev(i), 0)),
                  pl.BlockSpec((CHUNK, QK), lambda i: (rev(i), OQ // QK)), pl.BlockSpec((CHUNK, QK), lambda i: (rev(i), OKK // QK)),
                  pl.BlockSpec((CHUNK, D), lambda i: (rev(i), OV // D)), pl.BlockSpec((CHUNK, APAD), lambda i: (rev(i), OA // APAD)),
                  pl.BlockSpec((None, HEADS, DV, DK), lambda i: (rev(i), 0, 0, 0)),
                  pl.BlockSpec((None, HEADS, DV, DK), lambda i: (jnp.maximum(rev(i) - 1, 0), 0, 0, 0)),
                  pl.BlockSpec((APAD, QK), lambda i: (0, 0)), pl.BlockSpec((1, QK), lambda i: (0, 0)), ANY],
        out_specs=[pl.BlockSpec((CHUNK, QK), lambda i: (rev(i), 0)), pl.BlockSpec((CHUNK, QK), lambda i: (rev(i), 0)),
                   pl.BlockSpec((CHUNK, D), lambda i: (rev(i), 0)), pl.BlockSpec((CHUNK, APAD), lambda i: (rev(i), 0)),
                   pl.BlockSpec((APAD, QK), lambda i: (0, 0)), pl.BlockSpec((1, QK), lambda i: (0, 0))],
        out_shape=[SDS((T, QK), BF16), SDS((T, QK), BF16), SDS((T, D), BF16), SDS((T, APAD), BF16),
                   SDS((APAD, QK), F32), SDS((1, QK), F32)],
        scratch_shapes=[pltpu.VMEM((HEADS, DV, DK), F32)],
        compiler_params=_cparams(32 * 1024 * 1024, ("arbitrary",)),
    )(do, pcat, pcat, pcat, pcat, states, states, wa, ba, after)


TMF = 256
TMW = 512
_rowblk = ((TMF, D), lambda j, i, k: (i, 0))
_vec = ((1, D), lambda j, i, k: (0, 0))


def _full_spec(col):
    return ((TMF, D), lambda j, i, k: (i, col))


TBIG = 1024


def square_matmul(name, a, b, *, a_spec, b_spec, cdims, nk, after=None):
    def epi(acc, ex, outs, i):
        outs[0][...] = acc

    return matmul(name, a, b, a_spec=a_spec, b_spec=b_spec, cdims=cdims, grid=(D // TBIG, T // TBIG, nk),
                  acc_shape=(TBIG, TBIG), outs=[((T, D), F32, (TBIG, TBIG), lambda j, i, k: (i, j))], epi=epi,
                  after=after)[0]


def rowwise(name, y, *, extras, outs, epi):
    ne = len(extras)

    def body(*refs):
        epi(refs[0][...], refs[1:1 + ne], refs[1 + ne:], pl.program_id(1))

    in_specs = [pl.BlockSpec(*_rowblk)] + [pl.BlockSpec(bs, im) for _, bs, im in extras]
    return pl.pallas_call(
        body, name=name, grid=(1, T // TMF, 1), in_specs=in_specs,
        out_specs=[pl.BlockSpec(bs, im) for _, _, bs, im in outs], out_shape=[SDS(s, dt) for s, dt, _, _ in outs],
        compiler_params=_cparams(40 * 1024 * 1024, ("arbitrary", "arbitrary", "arbitrary")),
    )(y, *[arr for arr, _, _ in extras])


def mm_gla_out(og, w, ylin, pcat, pscale):
    def epi(acc, ex, outs, i):
        ylin_ref, lgp_ref, lgg_ref, ps_ref = ex
        for c0 in range(0, D, EPI_COLS):
            cs = slice(c0, c0 + EPI_COLS)
            gp = _sigmoid(lgp_ref[:, cs].astype(F32))
            gg = _sigmoid(lgg_ref[:, cs].astype(F32))
            a = acc[:, cs]
            outs[0][:, cs] = (gp * (ylin_ref[:, cs].astype(F32) * ps_ref[:, cs]) + gg * a).astype(BF16)
            outs[1][:, cs] = a.astype(BF16)

    return matmul("mm_gla_out", og, w, a_spec=_rowblk, b_spec=((D, D), lambda j, i, k: (0, 0)), cdims=NN,
                  grid=(1, T // TMF, 1), acc_shape=(TMF, D),
                  extras=[(ylin, *_rowblk), (pcat, *_full_spec(OGP // D)), (pcat, *_full_spec(OGG // D)), (pscale, *_vec)],
                  outs=[((T, D), BF16, *_rowblk), ((T, D), BF16, *_rowblk)], epi=epi)


def mm_out(mixed, w, x, g2):
    def epi(acc, ex, outs, i):
        x_ref, g_ref = ex
        x2 = x_ref[...] + acc
        r = lax.rsqrt(jnp.mean(x2 * x2, axis=-1, keepdims=True) + EPS)
        outs[0][...] = x2
        outs[1][...] = (x2 * r * g_ref[...]).astype(BF16)

    return matmul("mm_out", mixed, w, a_spec=_rowblk, b_spec=((D, D), lambda j, i, k: (0, 0)), cdims=NN,
                  grid=(1, T // TMF, 1), acc_shape=(TMF, D), extras=[(x, *_rowblk), (g2, *_vec)],
                  outs=[((T, D), F32, *_rowblk), ((T, D), BF16, *_rowblk)], epi=epi)


def mm_up(h2, wup):
    def epi(acc, ex, outs, i):
        r = jnp.maximum(acc, 0.0)
        outs[0][...] = r.astype(BF16)
        outs[1][...] = (r * r).astype(BF16)

    blk = ((TMW, D), lambda j, i, k: (i, j))
    return matmul("mm_up", h2, wup, a_spec=((TMW, D), lambda j, i, k: (i, 0)), b_spec=((None, D, D), lambda j, i, k: (j, 0, 0)),
                  cdims=NN, grid=(NCHIP, T // TMW, 1), acc_shape=(TMW, D),
                  outs=[((T, DFF), BF16, *blk), ((T, DFF), BF16, *blk)], epi=epi)


def mm_down(act, wdown, x2, tgt, gf):
    tk = 4096

    def epi(acc, ex, outs, i):
        x2_ref, t_ref, g_ref = ex
        dx_ref, dxb_ref, gnf_ref, loss_ref = outs
        x3 = x2_ref[...] + acc
        r = lax.rsqrt(jnp.mean(x3 * x3, axis=-1, keepdims=True) + EPS)
        xn = x3 * r
        err = xn * g_ref[...] - t_ref[...]
        lsum = 0.5 * jnp.sum(jnp.mean(err * err, axis=-1, keepdims=True), axis=0, keepdims=True)
        dy = err * (1.0 / D)
        _row_acc(gnf_ref, jnp.sum(dy * xn, axis=0, keepdims=True), i)
        _row_acc(loss_ref, jnp.broadcast_to(lsum, (1, 128)), i)
        dx3 = _rms_bwd(xn, r, dy * g_ref[...])
        dx_ref[...] = dx3
        dxb_ref[...] = dx3.astype(BF16)

    y = square_matmul("mm_down", act, wdown, a_spec=((TBIG, tk), lambda j, i, k: (i, k)),
                      b_spec=((tk, TBIG), lambda j, i, k: (k, j)), cdims=NN, nk=DFF // tk)
    return rowwise("rows_final", y, extras=[(x2, *_rowblk), (tgt, *_rowblk), (gf, *_vec)],
                   outs=[((T, D), F32, *_rowblk), ((T, D), BF16, *_rowblk), ((1, D), F32, *_vec),
                         ((1, 128), F32, (1, 128), lambda j, i, k: (0, 0))], epi=epi)


def mm_dact(dx3b, wdown, rup, after=None):
    def epi(acc, ex, outs, i):
        outs[0][...] = (acc * 2.0 * ex[0][...].astype(F32)).astype(BF16)

    blk = ((TMW, D), lambda j, i, k: (i, j))
    return matmul("mm_dact", dx3b, wdown, a_spec=((TMW, D), lambda j, i, k: (i, 0)), b_spec=((D, D), lambda j, i, k: (j, 0)),
                  cdims=NT, grid=(DFF // D, T // TMW, 1), acc_shape=(TMW, D), extras=[(rup, *blk)],
                  outs=[((T, DFF), BF16, *blk)], epi=epi, after=after)[0]


def mm_wgrad(name, a, b, m, n, out_shape, out_block, out_map, tm, tn, after=None):
    def epi(acc, ex, outs, i):
        outs[0][...] = acc.astype(BF16).reshape(outs[0].shape)

    return matmul(name, a, b, a_spec=((T, tm), lambda j, i, k: (0, i)), b_spec=((T, tn), lambda j, i, k: (0, j)),
                  cdims=TN, grid=(n // tn, m // tm, 1), acc_shape=(tm, tn),
                  outs=[(out_shape, BF16, out_block, out_map)], epi=epi, after=after)[0]


def mm_dh2(dup, wup, x2, dx3, g2, after=None):
    def epi(acc, ex, outs, i):
        x2_ref, dx3_ref, g_ref = ex
        x2 = x2_ref[...]
        r = lax.rsqrt(jnp.mean(x2 * x2, axis=-1, keepdims=True) + EPS)
        xn = x2 * r
        _row_acc(outs[2], jnp.sum(acc * xn, axis=0, keepdims=True), i)
        dx2 = dx3_ref[...] + _rms_bwd(xn, r, acc * g_ref[...])
        outs[0][...] = dx2
        outs[1][...] = dx2.astype(BF16)

    y = square_matmul("mm_dh2", dup, wup, a_spec=((TBIG, D), lambda j, i, k: (i, k)),
                      b_spec=((None, TBIG, D), lambda j, i, k: (k, j, 0)), cdims=NT, nk=NCHIP, after=after)
    return rowwise("rows_dh2", y, extras=[(x2, *_rowblk), (dx3, *_rowblk), (g2, *_vec)],
                   outs=[((T, D), F32, *_rowblk), ((T, D), BF16, *_rowblk), ((1, D), F32, *_vec)], epi=epi)


def mm_dmixed(dx2b, wout, pcat, ylin, ygla, pscale, after=None):
    def epi(acc, ex, outs, i):
        lgp_ref, lgg_ref, ylin_ref, ygla_ref, ps_ref = ex
        dps = []
        for c0 in range(0, D, EPI_COLS):
            cs = slice(c0, c0 + EPI_COLS)
            gp = _sigmoid(lgp_ref[:, cs].astype(F32))
            gg = _sigmoid(lgg_ref[:, cs].astype(F32))
            yl = ylin_ref[:, cs].astype(F32)
            ps = ps_ref[:, cs]
            a = acc[:, cs]
            agp = a * gp
            outs[0][:, cs] = (agp * ps).astype(BF16)
            outs[1][:, cs] = (a * gg).astype(BF16)
            outs[2][:, cs] = (agp * (yl * ps) * (1.0 - gp)).astype(BF16)
            outs[3][:, cs] = (a * ygla_ref[:, cs].astype(F32) * gg * (1.0 - gg)).astype(BF16)
            dps.append(jnp.sum(agp * yl, axis=0, keepdims=True))
        _row_acc(outs[4], jnp.concatenate(dps, axis=1), i)

    return matmul("mm_dmixed", dx2b, wout, a_spec=_rowblk, b_spec=((D, D), lambda j, i, k: (0, 0)), cdims=NT,
                  grid=(1, T // TMF, 1), acc_shape=(TMF, D),
                  extras=[(pcat, *_full_spec(OGP // D)), (pcat, *_full_spec(OGG // D)), (ylin, *_rowblk), (ygla, *_rowblk),
                          (pscale, *_vec)],
                  outs=[((T, D), BF16, *_rowblk)] * 4 + [((1, D), F32, *_vec)], epi=epi, after=after)


def mm_dog(dygla, wgo, o, pcat, ng, after=None):
    def epi(acc, ex, outs, i):
        o_ref, g_ref, ng_ref = ex
        do_ref, dg_ref, gng_ref = outs
        gparts = []
        for h in range(HEADS):
            cv = slice(h * DV, (h + 1) * DV)
            oh = o_ref[:, cv].astype(F32)
            r = lax.rsqrt(jnp.mean(oh * oh, axis=-1, keepdims=True) + EPS)
            on = oh * r
            gv = g_ref[:, cv].astype(F32)
            sg = _sigmoid(gv)
            a = acc[:, cv]
            dgain = a * (gv * sg)
            gparts.append(jnp.sum(dgain * on, axis=0, keepdims=True))
            ngh = ng_ref[:, cv]
            do_ref[:, cv] = _rms_bwd(on, r, dgain * ngh).astype(BF16)
            dg_ref[:, cv] = (a * (on * ngh) * (sg * (1.0 + gv * (1.0 - sg)))).astype(BF16)
        _row_acc(gng_ref, jnp.concatenate(gparts, axis=1), i)

    return matmul("mm_dog", dygla, wgo, a_spec=_rowblk, b_spec=((D, D), lambda j, i, k: (0, 0)), cdims=NT,
                  grid=(1, T // TMF, 1), acc_shape=(TMF, D),
                  extras=[(o, *_rowblk), (pcat, *_full_spec(OG // D)), (ng, *_vec)],
                  outs=[((T, D), BF16, *_rowblk), ((T, D), BF16, *_rowblk), ((1, D), F32, *_vec)], epi=epi, after=after)


def mm_dh1(dpcat, wcat, x, dx2, g1, after=None):
    tk = 3840

    def epi(acc, ex, outs, i):
        x_ref, dx2_ref, g_ref = ex
        xv = x_ref[...]
        r = lax.rsqrt(jnp.mean(xv * xv, axis=-1, keepdims=True) + EPS)
        xn = xv * r
        _row_acc(outs[1], jnp.sum(acc * xn, axis=0, keepdims=True), i)
        outs[0][...] = dx2_ref[...] + _rms_bwd(xn, r, acc * g_ref[...])

    y = square_matmul("mm_dh1", dpcat, wcat, a_spec=((TBIG, tk), lambda j, i, k: (i, k)),
                      b_spec=((TBIG, tk), lambda j, i, k: (j, k)), cdims=NT, nk=NCAT // tk, after=after)
    return rowwise("rows_dh1", y, extras=[(x, *_rowblk), (dx2, *_rowblk), (g1, *_vec)],
                   outs=[((T, D), F32, *_rowblk), ((1, D), F32, *_vec)], epi=epi)


def _tile_rows(rows, cols, n_arrays):
    tm = rows
    while tm % 32 == 0 and 2 * n_arrays * tm * cols * 4 > 36 * 1024 * 1024:
        tm //= 2
    return tm


def add_pairs(name, parts, theirs, core):
    _, _, r, c = parts.shape
    tm = _tile_rows(r, c, 3)

    def body(core_ref, a_ref, b_ref, o_ref):
        o_ref[...] = (a_ref[...].astype(F32) + b_ref[...].astype(F32)).astype(BF16)

    spec = pl.BlockSpec((None, tm, c), lambda j, i, core_ref: (j, i, 0))
    grid_spec = pltpu.PrefetchScalarGridSpec(
        num_scalar_prefetch=1, grid=(NCHIP, r // tm),
        in_specs=[pl.BlockSpec((None, None, tm, c), lambda j, i, core_ref: (core_ref[0], j, i, 0)), spec], out_specs=spec)
    return pl.pallas_call(body, name=name, grid_spec=grid_spec, out_shape=SDS((NCHIP, r, c), BF16),
                          compiler_params=_cparams(40 * 1024 * 1024, ("arbitrary", "arbitrary")))(core, parts, theirs)


def sum_chips(name, sums, landed, chip):
    _, r, c = sums.shape
    tm = _tile_rows(r, c, 4)

    def body(chip_ref, own_ref, l_ref, o_ref):
        s = own_ref[...].astype(F32)
        for t in range(NCHIP - 1):
            s = s + l_ref[t].astype(F32)
        o_ref[...] = s

    grid_spec = pltpu.PrefetchScalarGridSpec(
        num_scalar_prefetch=1, grid=(r // tm,),
        in_specs=[pl.BlockSpec((None, tm, c), lambda i, chip_ref: (chip_ref[0], i, 0)),
                  pl.BlockSpec((NCHIP - 1, tm, c), lambda i, chip_ref: (0, i, 0))],
        out_specs=pl.BlockSpec((tm, c), lambda i, chip_ref: (i, 0)))
    return pl.pallas_call(body, name=name, grid_spec=grid_spec, out_shape=SDS((r, c), F32),
                          compiler_params=_cparams(40 * 1024 * 1024, ("arbitrary",)))(chip, sums, landed)


def _adamw_math(wv, gv, mv, vv):
    mn = ADAM_B1 * mv + (1.0 - ADAM_B1) * gv
    vn = ADAM_B2 * vv + (1.0 - ADAM_B2) * (gv * gv)
    mh = mn / (1.0 - ADAM_B1 ** ADAM_STEP)
    vh = vn / (1.0 - ADAM_B2 ** ADAM_STEP)
    return -ADAM_LR * (mh / (jnp.sqrt(vh) + ADAM_EPS) + ADAM_WD * wv), mn, vn


def adamw(name, w, g, m, v):
    def body(w_ref, g_ref, m_ref, v_ref, go_ref, d_ref, mo_ref, vo_ref):
        gv = g_ref[...]
        go_ref[...] = gv
        d_ref[...], mo_ref[...], vo_ref[...] = _adamw_math(w_ref[...], gv, m_ref[...], v_ref[...])

    return pl.pallas_call(body, name=name, out_shape=[SDS(w.shape, F32)] * 4)(w, g, m, v)


def adamw_halves(name, w, g_own, g_sib, m, v, core):
    _, r, c = w.shape
    tm = _tile_rows(r, c, 10)

    def body(core_ref, w_ref, go_ref, gs_ref, m_ref, v_ref, g_out, d_out, m_out, v_out):
        gv = jnp.where(pl.program_id(0) == core_ref[0], go_ref[...], gs_ref[...])
        g_out[...] = gv
        d_out[...], m_out[...], v_out[...] = _adamw_math(w_ref[...], gv, m_ref[...], v_ref[...])

    full = pl.BlockSpec((None, tm, c), lambda h, i, core_ref: (h, i, 0))
    own = pl.BlockSpec((tm, c), lambda h, i, core_ref: (jnp.where(h == core_ref[0], i, 0), 0))
    sib = pl.BlockSpec((tm, c), lambda h, i, core_ref: (jnp.where(h == core_ref[0], 0, i), 0))
    grid_spec = pltpu.PrefetchScalarGridSpec(num_scalar_prefetch=1, grid=(2, r // tm),
                                             in_specs=[full, own, sib, full, full], out_specs=[full] * 4)
    return pl.pallas_call(body, name=name, grid_spec=grid_spec, out_shape=[SDS(w.shape, F32)] * 4,
                          compiler_params=_cparams(48 * 1024 * 1024, ("arbitrary", "arbitrary")))(core, w, g_own, g_sib, m, v)


def cast_bf16(name, w):
    _, r, c = w.shape
    tm = _tile_rows(r, c, 2)

    def body(w_ref, o_ref):
        o_ref[...] = w_ref[...].astype(BF16)

    spec = pl.BlockSpec((None, tm, c), lambda h, i: (h, i, 0))
    return pl.pallas_call(body, name=name, grid=(2, r // tm), in_specs=[spec], out_specs=spec, out_shape=SDS(w.shape, BF16),
                          compiler_params=_cparams(40 * 1024 * 1024, ("arbitrary", "arbitrary")))(w)


def pack_rows(name, parts, rows, after=None):
    width = parts[0].shape[1]
    n = len(parts)
    afters = _as_list(after)

    def body(*refs):
        out_ref = refs[n + len(afters)]
        out_ref[...] = jnp.zeros_like(out_ref)
        off = 0
        for p in refs[:n]:
            out_ref[off:off + p.shape[0], :] = p[...]
            off += p.shape[0]

    vm = pl.BlockSpec(memory_space=pltpu.VMEM)
    return pl.pallas_call(body, name=name, in_specs=[vm] * n + [ANY] * len(afters), out_specs=vm,
                          out_shape=SDS((rows, width), F32))(*parts, *afters)


def _place():
    x, y, c = lax.axis_index("x"), lax.axis_index("y"), lax.axis_index("c")
    chips = [(1 - x, y), (x, 1 - y), (1 - x, 1 - y)]
    return x, y, c, chips


def _row_split(shape, dtype):
    r, c = shape
    n = 1
    while r % (2 * n) == 0 and (r // (2 * n)) % 16 == 0 and (r // n) * c * jnp.dtype(dtype).itemsize > PIECE_BYTES:
        n *= 2
    return [pl.ds(s * (r // n), r // n) for s in range(n)]


def _pieces(ref):
    *lead, r, c = ref.shape
    split = _row_split((r, c), ref.dtype)
    return [ref.at[(*idx, s)] for idx in itertools.product(*[range(d) for d in lead]) for s in split]


HBM = pl.BlockSpec(memory_space=pltpu.HBM)
SEM = pl.BlockSpec(memory_space=pltpu.SEMAPHORE)
EFFECT = pltpu.SideEffectType.DATAFLOW_SIDE_EFFECTING


def gather_start(name, shards, after=None):
    n = len(shards)
    afters = _as_list(after)

    def body(*refs):
        src, land = refs[:n], refs[n:2 * n]
        send, recv = refs[2 * n + len(afters)], refs[2 * n + len(afters) + 1]
        x, y, c, chips = _place()
        me = 2 * x + y
        for a in range(n):
            for j, (cx, cy) in enumerate(chips[:2]):
                for sp, dp in zip(_pieces(src[a].at[c]), _pieces(land[a].at[me, c])):
                    pltpu.make_async_remote_copy(sp, dp, send.at[2 * a + j], recv.at[2 * a + j],
                                                 device_id=(cx, cy, c), device_id_type=MESH).start()

    lands = [pltpu.with_memory_space_constraint(lax.empty((NCHIP,) + s.shape, s.dtype), pltpu.HBM) for s in shards]
    srcs = [pltpu.with_memory_space_constraint(s, pltpu.HBM) for s in shards]
    outs = pl.pallas_call(
        body, name=name,
        out_shape=(pltpu.SemaphoreType.DMA((2 * n,)), pltpu.SemaphoreType.DMA((2 * n,)),
                   *[pltpu.HBM(s.shape, s.dtype) for s in shards], *[pltpu.HBM(l.shape, l.dtype) for l in lands]),
        in_specs=[HBM] * (2 * n) + [ANY] * len(afters), out_specs=(SEM, SEM, *([HBM] * (2 * n))),
        input_output_aliases={i: 2 + i for i in range(2 * n)},
        compiler_params=pltpu.CompilerParams(has_side_effects=EFFECT),
    )(*srcs, *lands, *afters)
    return outs[0], outs[1], list(outs[2:2 + n]), list(outs[2 + n:2 + 2 * n])


def _relay_blocks(land, c, chips):
    (xx, xy), (yx, yy), (dx, dy) = chips
    rows = land.shape[2] // 2
    upper, lower = pl.ds(0, rows), pl.ds(rows, rows)
    return [(land.at[2 * yx + yy, c, lower], land.at[2 * dx + dy, c, lower]),
            (land.at[2 * xx + xy, c, upper], land.at[2 * dx + dy, c, upper])]


def relay_turn(name, send, recv, shards, lands, after):
    n = len(shards)
    afters = _as_list(after)

    def body(*refs):
        src, had = refs[:n], refs[n:2 * n]
        send_ref, recv_ref = refs[2 * n], refs[2 * n + 1]
        rsend, rrecv = refs[2 * n + 2 + len(afters)], refs[2 * n + 3 + len(afters)]
        land = refs[3 * n + 4 + len(afters):4 * n + 4 + len(afters)]
        x, y, c, chips = _place()
        for a in range(n):
            for j, (cx, cy) in enumerate(chips[:2]):
                cp = pltpu.make_async_remote_copy(src[a].at[c], had[a].at[2 * cx + cy, c], send_ref.at[2 * a + j],
                                                  recv_ref.at[2 * a + j], device_id=(cx, cy, c), device_id_type=MESH)
                cp.wait_send()
                cp.wait_recv()
        for a in range(n):
            for j, ((sent, _), (dst, _)) in enumerate(zip(_relay_blocks(had[a], c, chips), _relay_blocks(land[a], c, chips))):
                cx, cy = chips[j]
                for sp, dp in zip(_pieces(sent), _pieces(dst)):
                    pltpu.make_async_remote_copy(sp, dp, rsend.at[2 * a + j], rrecv.at[2 * a + j],
                                                 device_id=(cx, cy, c), device_id_type=MESH).start()

    outs = pl.pallas_call(
        body, name=name,
        out_shape=(pltpu.SemaphoreType.DMA((2 * n,)), pltpu.SemaphoreType.DMA((2 * n,)),
                   *[pltpu.HBM(s.shape, s.dtype) for s in shards], *[pltpu.HBM(l.shape, l.dtype) for l in lands]),
        in_specs=[HBM] * (2 * n) + [SEM, SEM] + [ANY] * len(afters), out_specs=(SEM, SEM, *([HBM] * (2 * n))),
        input_output_aliases={i: 2 + i for i in range(2 * n)},
        compiler_params=pltpu.CompilerParams(has_side_effects=EFFECT),
    )(*shards, *lands, send, recv, *afters)
    return outs[0], outs[1], list(outs[2:2 + n]), list(outs[2 + n:2 + 2 * n])


def relay_wait(name, send, recv, lands, after):
    n = len(lands)
    afters = _as_list(after)

    def body(*refs):
        land = refs[:n]
        send_ref, recv_ref = refs[n], refs[n + 1]
        x, y, c, chips = _place()
        for a in range(n):
            for j, (sent, got) in enumerate(_relay_blocks(land[a], c, chips)):
                cx, cy = chips[j]
                cp = pltpu.make_async_remote_copy(sent, got, send_ref.at[2 * a + j], recv_ref.at[2 * a + j],
                                                  device_id=(cx, cy, c), device_id_type=MESH)
                cp.wait_send()
                cp.wait_recv()

    outs = pl.pallas_call(
        body, name=name, out_shape=tuple(pltpu.HBM(l.shape, l.dtype) for l in lands),
        in_specs=[HBM] * n + [SEM, SEM] + [ANY] * len(afters), out_specs=[HBM] * n,
        input_output_aliases={i: i for i in range(n)},
        compiler_params=pltpu.CompilerParams(has_side_effects=EFFECT),
    )(*lands, send, recv, *afters)
    return list(outs)


def forward_halves(name, shards, lands):
    n = len(lands)

    def body(*refs):
        had, buf = refs[:n], refs[n:2 * n]
        send, recv = refs[2 * n:]
        x, y, c, chips = _place()
        sib = (x, y, 1 - c)
        for a in range(n):
            for j, (cx, cy) in enumerate(chips):
                for sp, dp in zip(_pieces(had[a].at[2 * cx + cy, c]), _pieces(buf[a].at[2 * cx + cy, c])):
                    pltpu.make_async_remote_copy(sp, dp, send.at[3 * a + j], recv.at[3 * a + j], device_id=sib, device_id_type=MESH).start()
        for a in range(n):
            for j, (cx, cy) in enumerate(chips):
                pltpu.make_async_remote_copy(had[a].at[2 * cx + cy, c], buf[a].at[2 * cx + cy, 1 - c], send.at[3 * a + j],
                                             recv.at[3 * a + j], device_id=sib, device_id_type=MESH).wait()

    got = pl.pallas_call(
        body, name=name, in_specs=[ANY] * n, out_specs=[ANY] * n, out_shape=[SDS(l.shape, l.dtype) for l in lands],
        input_output_aliases={i: i for i in range(n)},
        scratch_shapes=[pltpu.SemaphoreType.DMA((3 * n,)), pltpu.SemaphoreType.DMA((3 * n,))],
    )(*lands)
    me = 2 * lax.axis_index("x") + lax.axis_index("y")
    return [lax.dynamic_update_index_in_dim(g, s, me, 0) for g, s in zip(got, shards)]


def forward_turn(name, send, recv, lands, after):
    n = len(lands)
    afters = _as_list(after)

    def body(*refs):
        had = refs[:n]
        send_ref, recv_ref = refs[n], refs[n + 1]
        fsend, frecv = refs[n + 2 + len(afters)], refs[n + 3 + len(afters)]
        buf = refs[n + 4 + len(afters):2 * n + 4 + len(afters)]
        x, y, c, chips = _place()
        sib = (x, y, 1 - c)
        for a in range(n):
            for j, (sent, got) in enumerate(_relay_blocks(had[a], c, chips)):
                cx, cy = chips[j]
                cp = pltpu.make_async_remote_copy(sent, got, send_ref.at[2 * a + j], recv_ref.at[2 * a + j],
                                                  device_id=(cx, cy, c), device_id_type=MESH)
                cp.wait_send()
                cp.wait_recv()
        for a in range(n):
            for j, (cx, cy) in enumerate(chips):
                for sp, dp in zip(_pieces(had[a].at[2 * cx + cy, c]), _pieces(buf[a].at[2 * cx + cy, c])):
                    pltpu.make_async_remote_copy(sp, dp, fsend.at[3 * a + j], frecv.at[3 * a + j], device_id=sib, device_id_type=MESH).start()

    outs = pl.pallas_call(
        body, name=name,
        out_shape=(pltpu.SemaphoreType.DMA((3 * n,)), pltpu.SemaphoreType.DMA((3 * n,)), *[pltpu.HBM(l.shape, l.dtype) for l in lands]),
        in_specs=[HBM] * n + [SEM, SEM] + [ANY] * len(afters), out_specs=(SEM, SEM, *([HBM] * n)),
        input_output_aliases={i: 2 + i for i in range(n)},
        compiler_params=pltpu.CompilerParams(has_side_effects=EFFECT),
    )(*lands, send, recv, *afters)
    return outs[0], outs[1], list(outs[2:])


def forward_wait(name, send, recv, lands, after):
    n = len(lands)
    afters = _as_list(after)

    def body(*refs):
        land = refs[:n]
        send_ref, recv_ref = refs[n], refs[n + 1]
        x, y, c, chips = _place()
        sib = (x, y, 1 - c)
        for a in range(n):
            for j, (cx, cy) in enumerate(chips):
                cp = pltpu.make_async_remote_copy(land[a].at[2 * cx + cy, c], land[a].at[2 * cx + cy, 1 - c], send_ref.at[3 * a + j],
                                                  recv_ref.at[3 * a + j], device_id=sib, device_id_type=MESH)
                cp.wait_send()
                cp.wait_recv()

    outs = pl.pallas_call(
        body, name=name, out_shape=tuple(pltpu.HBM(l.shape, l.dtype) for l in lands),
        in_specs=[HBM] * n + [SEM, SEM] + [ANY] * len(afters), out_specs=[HBM] * n,
        input_output_aliases={i: i for i in range(n)},
        compiler_params=pltpu.CompilerParams(has_side_effects=EFFECT),
    )(*lands, send, recv, *afters)
    return list(outs)


def exchange_start(name, parts):
    n = len(parts)

    def body(*refs):
        src, got = refs[:n], refs[n:2 * n]
        send, recv = refs[2 * n], refs[2 * n + 1]
        token = refs[4 * n + 2]
        x, y, c, _ = _place()
        sib = (x, y, 1 - c)
        for a in range(n):
            for sp, dp in zip(_pieces(src[a].at[1 - c]), _pieces(got[a])):
                pltpu.make_async_remote_copy(sp, dp, send.at[a], recv.at[a], device_id=sib, device_id_type=MESH).start()
        token[...] = jnp.zeros_like(token)

    lands = [pltpu.with_memory_space_constraint(lax.empty(p.shape[1:], p.dtype), pltpu.HBM) for p in parts]
    srcs = [pltpu.with_memory_space_constraint(p, pltpu.HBM) for p in parts]
    outs = pl.pallas_call(
        body, name=name,
        out_shape=(pltpu.SemaphoreType.DMA((n,)), pltpu.SemaphoreType.DMA((n,)),
                   *[pltpu.HBM(p.shape, p.dtype) for p in parts], *[pltpu.HBM(l.shape, l.dtype) for l in lands],
                   SDS((8, 128), F32)),
        in_specs=[HBM] * (2 * n), out_specs=(SEM, SEM, *([HBM] * (2 * n)), pl.BlockSpec(memory_space=pltpu.VMEM)),
        input_output_aliases={i: 2 + i for i in range(2 * n)},
        compiler_params=pltpu.CompilerParams(has_side_effects=EFFECT),
    )(*srcs, *lands)
    return outs[0], outs[1], list(outs[2:2 + n]), list(outs[2 + n:2 + 2 * n]), outs[2 + 2 * n]


def exchange_wait(name, send, recv, parts, lands, after):
    n = len(parts)
    afters = _as_list(after)

    def body(*refs):
        src, got = refs[:n], refs[n:2 * n]
        send_ref, recv_ref = refs[2 * n], refs[2 * n + 1]
        x, y, c, _ = _place()
        sib = (x, y, 1 - c)
        for a in range(n):
            cp = pltpu.make_async_remote_copy(src[a].at[1 - c], got[a], send_ref.at[a], recv_ref.at[a], device_id=sib, device_id_type=MESH)
            cp.wait_send()
            cp.wait_recv()

    outs = pl.pallas_call(
        body, name=name,
        out_shape=(*[pltpu.HBM(p.shape, p.dtype) for p in parts], *[pltpu.HBM(l.shape, l.dtype) for l in lands]),
        in_specs=[HBM] * (2 * n) + [SEM, SEM] + [ANY] * len(afters), out_specs=[HBM] * (2 * n),
        input_output_aliases={i: i for i in range(2 * n)},
        compiler_params=pltpu.CompilerParams(has_side_effects=EFFECT),
    )(*parts, *lands, send, recv, *afters)
    return list(outs[:n]), list(outs[n:])


def scatter_start(name, parts):
    n = len(parts)

    def body(*refs):
        src, land = refs[:n], refs[n:2 * n]
        send, recv = refs[2 * n], refs[2 * n + 1]
        token = refs[4 * n + 2]
        x, y, c, chips = _place()
        for a in range(n):
            for j, (cx, cy) in enumerate(chips):
                for sp, dp in zip(_pieces(src[a].at[2 * cx + cy]), _pieces(land[a].at[j])):
                    pltpu.make_async_remote_copy(sp, dp, send.at[3 * a + j], recv.at[3 * a + j],
                                                 device_id=(cx, cy, c), device_id_type=MESH).start()
        token[...] = jnp.zeros_like(token)

    lands = [pltpu.with_memory_space_constraint(lax.empty((NCHIP - 1,) + p.shape[1:], p.dtype), pltpu.HBM) for p in parts]
    srcs = [pltpu.with_memory_space_constraint(p, pltpu.HBM) for p in parts]
    outs = pl.pallas_call(
        body, name=name,
        out_shape=(pltpu.SemaphoreType.DMA((3 * n,)), pltpu.SemaphoreType.DMA((3 * n,)),
                   *[pltpu.HBM(p.shape, p.dtype) for p in parts], *[pltpu.HBM(l.shape, l.dtype) for l in lands],
                   SDS((8, 128), F32)),
        in_specs=[HBM] * (2 * n), out_specs=(SEM, SEM, *([HBM] * (2 * n)), pl.BlockSpec(memory_space=pltpu.VMEM)),
        input_output_aliases={i: 2 + i for i in range(2 * n)},
        compiler_params=pltpu.CompilerParams(has_side_effects=EFFECT),
    )(*srcs, *lands)
    return outs[0], outs[1], list(outs[2:2 + n]), list(outs[2 + n:2 + 2 * n]), outs[2 + 2 * n]


def scatter_wait(name, send, recv, parts, lands, after):
    n = len(parts)
    afters = _as_list(after)

    def body(*refs):
        src, land = refs[:n], refs[n:2 * n]
        send_ref, recv_ref = refs[2 * n], refs[2 * n + 1]
        x, y, c, chips = _place()
        for a in range(n):
            for j, (cx, cy) in enumerate(chips):
                cp = pltpu.make_async_remote_copy(src[a].at[2 * cx + cy], land[a].at[j], send_ref.at[3 * a + j], recv_ref.at[3 * a + j],
                                                  device_id=(cx, cy, c), device_id_type=MESH)
                cp.wait_send()
                cp.wait_recv()

    outs = pl.pallas_call(
        body, name=name,
        out_shape=(*[pltpu.HBM(p.shape, p.dtype) for p in parts], *[pltpu.HBM(l.shape, l.dtype) for l in lands]),
        in_specs=[HBM] * (2 * n) + [SEM, SEM] + [ANY] * len(afters), out_specs=[HBM] * (2 * n),
        input_output_aliases={i: i for i in range(2 * n)},
        compiler_params=pltpu.CompilerParams(has_side_effects=EFFECT),
    )(*parts, *lands, send, recv, *afters)
    return list(outs[:n]), list(outs[n:])


def join_start(name, halves):
    n = len(halves)

    def body(*refs):
        src, dst = refs[:n], refs[n:2 * n]
        send, recv = refs[2 * n], refs[2 * n + 1]
        token = refs[4 * n + 2]
        x, y, c, _ = _place()
        sib = (x, y, 1 - c)
        for a in range(n):
            for sp, dp in zip(_pieces(src[a]), _pieces(dst[a])):
                pltpu.make_async_remote_copy(sp, dp, send.at[a], recv.at[a], device_id=sib, device_id_type=MESH).start()
        token[...] = jnp.zeros_like(token)

    lands = [pltpu.with_memory_space_constraint(lax.empty(h.shape, h.dtype), pltpu.HBM) for h in halves]
    srcs = [pltpu.with_memory_space_constraint(h, pltpu.HBM) for h in halves]
    outs = pl.pallas_call(
        body, name=name,
        out_shape=(pltpu.SemaphoreType.DMA((n,)), pltpu.SemaphoreType.DMA((n,)),
                   *[pltpu.HBM(h.shape, h.dtype) for h in halves], *[pltpu.HBM(l.shape, l.dtype) for l in lands],
                   SDS((8, 128), F32)),
        in_specs=[HBM] * (2 * n), out_specs=(SEM, SEM, *([HBM] * (2 * n)), pl.BlockSpec(memory_space=pltpu.VMEM)),
        input_output_aliases={i: 2 + i for i in range(2 * n)},
        compiler_params=pltpu.CompilerParams(has_side_effects=EFFECT),
    )(*srcs, *lands)
    return outs[0], outs[1], list(outs[2:2 + n]), list(outs[2 + n:2 + 2 * n]), outs[2 + 2 * n]


def join_wait(name, send, recv, halves, lands, after):
    n = len(halves)
    afters = _as_list(after)

    def body(*refs):
        src, dst = refs[:n], refs[n:2 * n]
        send_ref, recv_ref = refs[2 * n], refs[2 * n + 1]
        x, y, c, _ = _place()
        sib = (x, y, 1 - c)
        for a in range(n):
            cp = pltpu.make_async_remote_copy(src[a], dst[a], send_ref.at[a], recv_ref.at[a], device_id=sib, device_id_type=MESH)
            cp.wait_send()
            cp.wait_recv()

    outs = pl.pallas_call(
        body, name=name,
        out_shape=(*[pltpu.HBM(h.shape, h.dtype) for h in halves], *[pltpu.HBM(l.shape, l.dtype) for l in lands]),
        in_specs=[HBM] * (2 * n) + [SEM, SEM] + [ANY] * len(afters), out_specs=[HBM] * (2 * n),
        input_output_aliases={i: i for i in range(2 * n)},
        compiler_params=pltpu.CompilerParams(has_side_effects=EFFECT),
    )(*halves, *lands, send, recv, *afters)
    return list(outs[:n]), list(outs[n:])


def gather_small(name, xs, reduce, after=None):
    m, ncol = xs.shape
    afters = _as_list(after)

    def body(x_ref, *rest):
        out_ref, all_ref, send, recv, lsem = rest[len(afters):]
        x, y, c, chips = _place()
        me, sib = (x, y, c), (x, y, 1 - c)

        def rows(px, py, pc):
            return all_ref.at[pl.ds((4 * px + 2 * py + pc) * m, m), :]

        def copy(k, block, to, src=None):
            return pltpu.make_async_remote_copy(rows(*block) if src is None else src, rows(*block), send.at[k], recv.at[k],
                                                device_id=to, device_id_type=MESH)

        mine = pltpu.make_async_copy(x_ref, rows(*me), lsem)
        mine.start()
        first = [copy(0, me, sib, src=x_ref)] + [copy(1 + j, me, (*chip, c), src=x_ref) for j, chip in enumerate(chips)]
        for cp in first:
            cp.start()
        passed = [copy(4 + j, (*chip, c), sib) for j, chip in enumerate(chips)]
        for j, chip in enumerate(chips):
            copy(1 + j, (*chip, c), me).wait_recv()
            passed[j].start()
        copy(0, sib, me).wait_recv()
        for j, chip in enumerate(chips):
            copy(4 + j, (*chip, 1 - c), me).wait_recv()
        for cp in first + passed:
            cp.wait_send()
        mine.wait()
        if reduce:
            s = all_ref[0:m, :]
            for dev in range(1, 8):
                s = s + all_ref[dev * m:(dev + 1) * m, :]
            out_ref[...] = s
        else:
            out_ref[...] = all_ref[...]

    vm = pl.BlockSpec(memory_space=pltpu.VMEM)
    return pl.pallas_call(
        body, name=name, in_specs=[vm] + [ANY] * len(afters), out_specs=vm,
        out_shape=SDS((m, ncol) if reduce else (8 * m, ncol), F32),
        scratch_shapes=[pltpu.VMEM((8 * m, ncol), F32), pltpu.SemaphoreType.DMA((7,)), pltpu.SemaphoreType.DMA((7,)),
                        pltpu.SemaphoreType.DMA],
    )(xs, *afters)


RELAYOUT_ROWS = 128


def weights_to_cat(name, land, own, place, other, prev=None, after=None):
    tm = RELAYOUT_ROWS
    nb = (D // 2) // tm
    extra = ([] if prev is None else [prev]) + _as_list(after)

    def half(p):
        return 1 - p[0] if other else p[0]

    def body(p_ref, g_ref, own_ref, *rest):
        o_ref = rest[len(extra)]
        nat = jnp.concatenate([jnp.where(p_ref[1] == j, own_ref[...], g_ref[j]) for j in range(NCHIP)], axis=1)
        pad = jnp.zeros((tm, NCAT - OA - 16), BF16)
        o_ref[...] = jnp.concatenate([nat[:, 3072:7168], nat[:, 7184:11280], nat[:, 0:3072], nat[:, 7168:7184], pad], axis=1)

    grid_spec = pltpu.PrefetchScalarGridSpec(
        num_scalar_prefetch=1, grid=(nb,),
        in_specs=[pl.BlockSpec((NCHIP, None, tm, IN_SHARD), lambda i, p: (0, half(p), i, 0)),
                  pl.BlockSpec((None, tm, IN_SHARD), lambda i, p: (half(p), i, 0))] + [ANY] * len(extra),
        out_specs=pl.BlockSpec((tm, NCAT), lambda i, p: (half(p) * nb + i, 0)))
    return pl.pallas_call(
        body, name=name, grid_spec=grid_spec, out_shape=SDS((D, NCAT), BF16),
        input_output_aliases={} if prev is None else {3: 0},
        compiler_params=_cparams(40 * 1024 * 1024, ("arbitrary",)),
    )(place, land, own, *extra)


def grads_from_cat(gw_cat):
    tm = RELAYOUT_ROWS
    nb = (D // 2) // tm

    def body(c_ref, o_ref):
        cat = c_ref[...]
        nat = jnp.concatenate([cat[:, OU:OA], cat[:, OV:OGP], cat[:, OA:OA + 16], cat[:, OGP:OU]], axis=1)
        for j in range(NCHIP):
            o_ref[j] = nat[:, j * IN_SHARD:(j + 1) * IN_SHARD]

    return pl.pallas_call(
        body, name="grads_from_cat", grid=(D // tm,), in_specs=[pl.BlockSpec((tm, NCAT), lambda i: (i, 0))],
        out_specs=pl.BlockSpec((None, NCHIP, tm, IN_SHARD), lambda i: (i // nb, 0, i % nb, 0)),
        out_shape=SDS((2, NCHIP, D // 2, IN_SHARD), BF16), compiler_params=_cparams(40 * 1024 * 1024, ("arbitrary",)),
    )(gw_cat)


def _pad_rows(a, rows):
    return jnp.concatenate([a, jnp.zeros((rows - a.shape[0],) + a.shape[1:], a.dtype)], axis=0)


def local_step(x2d, tgt, gf, g1, pool_scale, wa_pad, b_alpha, ng, g2, get_w, on_grad=None, on_settle=None, tick=None):
    emit = on_grad if on_grad is not None else (lambda group, grads: None)
    settle = on_settle if on_settle is not None else (lambda group, after: None)
    h1 = norm1(x2d, g1)
    wcat, pw = get_w("in", h1)
    pcat = mm_in(h1, wcat)
    dpool, ylin = pool_fwd(pcat, pw)
    pinned = tick("pool", ylin) if tick is not None else None
    og, o, states = gla_fwd(pcat, wa_pad, b_alpha, ng, pinned)
    w_go, w_o = get_w("mid", og)
    mixed, ygla = mm_gla_out(og, w_go, ylin, pcat, pool_scale)
    x2, h2 = mm_out(mixed, w_o, x2d, g2)
    w_up = get_w("up", h2)
    rup, act = mm_up(h2, w_up)
    w_dn = get_w("down", act)
    dx3, dx3b, g_nf, loss_row = mm_down(act, w_dn, x2, tgt, gf)

    gw_down = mm_wgrad("mm_dw_down", act, dx3b, DFF, D, (2, NCHIP, D // 2, D), (None, None, D // 2, D),
                       lambda j, i, k: (i % 2, i // 2, 0, 0), D // 2, D)
    token = emit("down", {"down": gw_down})
    dup = mm_dact(dx3b, w_dn, rup, after=token)
    token = settle("down", dup)
    dx2, dx2b, g_mlp = mm_dh2(dup, w_up, x2, dx3, g2, after=token)
    gw_up = mm_wgrad("mm_dw_up", h2, dup, D, DFF, (2, NCHIP, D // 2, D), (None, None, D // 2, D),
                     lambda j, i, k: (i, j, 0, 0), D // 2, D)
    token = emit("up", {"up": gw_up})
    dylin, dygla, dlgp, dlgg, g_ps = mm_dmixed(dx2b, w_o, pcat, ylin, ygla, pool_scale, after=token)
    token = settle("up", dylin)
    gw_out = mm_wgrad("mm_dw_out", mixed, dx2b, D, D, (2, NCHIP, 256, D), (2, None, 256, D),
                      lambda j, i, k: (0, i, 0, 0), 512, D)
    do, dg, g_ng = mm_dog(dygla, w_go, o, pcat, ng, after=token)
    gw_go = mm_wgrad("mm_dw_gla_out", og, dygla, D, D, (2, NCHIP, 256, D), (2, None, 256, D),
                     lambda j, i, k: (0, i, 0, 0), 512, D)
    token = emit("mix", {"out": gw_out, "gla_out": gw_go})
    dq, dk, dv, dalow, g_wa, g_ba = gla_bwd(do, pcat, states, wa_pad, b_alpha, b_alpha if token is None else token)
    token = settle("mix", dq)
    du, dpw = pool_bwd(dylin, dpool, pw)
    dpcat = jnp.concatenate([dv, dg, dlgp, dlgg, du, dq, dk, dalow, jnp.zeros((T, NCAT - OA - APAD), BF16)], axis=1)
    gw_cat = mm_wgrad("mm_dw_in", h1, dpcat, D, NCAT, (D, NCAT), (1024, 1280), lambda j, i, k: (i, j), 1024, 1280, after=token)
    token = settle("in", emit("in", {"in_cat": gw_cat, "pool": dpw}))
    grad_x, g_mix = mm_dh1(dpcat, wcat, x2d, dx2, g1, after=token)
    return (loss_row[0, 0], grad_x, g_mix, g_ps, g_mlp, g_nf, g_ng, g_ba, g_wa, token,
            gw_cat, dpw, gw_go, gw_out, gw_up, gw_down)


def kernel(x, norm_mix_g, w_in, pool_w, pool_scale, w_alpha, b_alpha, gla_norm_g, w_gla_out, w_out, norm_mlp_g, w_mlp_up, w_mlp_down, norm_final_g, loss_target, m_norm_mix_g, m_w_in, m_pool_w, m_pool_scale, m_w_alpha, m_b_alpha, m_gla_norm_g, m_w_gla_out, m_w_out, m_norm_mlp_g, m_w_mlp_up, m_w_mlp_down, m_norm_final_g, v_norm_mix_g, v_w_in, v_pool_w, v_pool_scale, v_w_alpha, v_b_alpha, v_gla_norm_g, v_w_gla_out, v_w_out, v_norm_mlp_g, v_w_mlp_up, v_w_mlp_down, v_norm_final_g):
    chip = 2 * lax.axis_index("x") + lax.axis_index("y")
    chip_i = chip.astype(jnp.int32).reshape(1)
    core_i = lax.axis_index("c").astype(jnp.int32).reshape(1)
    place_i = jnp.concatenate([core_i, chip_i])
    tgt = loss_target.reshape(T, D)
    gf = norm_final_g.reshape(1, D)

    def halves(w2d):
        r, c = w2d.shape
        return w2d.astype(BF16).reshape(2, r // 2, c)

    pool_shard = pool_w.reshape(4 * PG, PO // NCHIP)
    w_in_r = w_in.reshape(2, D // 2, IN_SHARD)
    sent = {"in": [cast_bf16("cast_w_in", w_in_r), halves(pool_shard)]}
    flight = {}

    def start(group, after=None):
        flight[group] = gather_start("gather_start_" + group, sent[group], after)

    def relay(group, after):
        send, recv, shards, lands = flight[group]
        flight[group] = relay_turn("relay_turn_" + group, send, recv, shards, lands, after)

    def fetch(group, after):
        send, recv, shards, lands = flight[group]
        lands = relay_wait("relay_wait_" + group, send, recv, lands, after)
        return forward_halves("forward_" + group, shards, lands)

    start("in")
    m_in_f, v_in_f, w_go_f, w_o_f, w_up_f, w_dn_f, x_f, wal_f, gng_f = lax.optimization_barrier(
        (m_w_in, v_w_in, w_gla_out, w_out, w_mlp_up, w_mlp_down, x, w_alpha, gla_norm_g, flight["in"][2][0]))[:9]
    m_in_r, v_in_r = m_in_f.reshape(2, D // 2, IN_SHARD), v_in_f.reshape(2, D // 2, IN_SHARD)
    sent["mid"] = [halves(w_go_f[0]), halves(w_o_f[0])]
    relay("in", [m_in_r, v_in_r, *sent["mid"]])
    w_up_f, w_dn_f, x_f, wal_f, gng_f = lax.optimization_barrier(
        (w_up_f, w_dn_f, x_f, wal_f, gng_f, flight["in"][3][0]))[:5]
    sent["up"], sent["down"] = [halves(w_up_f[0])], [halves(w_dn_f[0])]
    x2d = x_f.reshape(T, D)
    big = [w_in_r, w_go_f[0], w_o_f[0], w_up_f[0], w_dn_f[0], pool_shard]

    def tick(point, after):
        if point == "pool":
            relay("mid", after)
            start("down", flight["mid"][3][0])
            return [flight["mid"][3][0], flight["down"][3][0]]

    def get_w(group, after):
        if group == "in":
            after = [after, *sent["up"], *sent["down"], wa_pad]
        if group == "mid":
            relay("up", after)
            after = flight["up"][3][0]
        if group == "up":
            relay("down", after)
            after = flight["down"][3][0]
        if group == "in":
            send, recv, shards, lands = flight["in"]
            send, recv, lands = forward_turn("forward_turn_in", send, recv, lands, after)
            start("mid", lands[0])
            start("up", flight["mid"][3][0])
            wcat = weights_to_cat("weights_to_cat_mine", lands[0], shards[0], place_i, False, after=flight["up"][3][0])
            lands = forward_wait("forward_wait_in", send, recv, lands, wcat)
            wcat = weights_to_cat("weights_to_cat_sibling", lands[0], shards[0], place_i, True, prev=wcat)
            g_pool = lax.dynamic_update_index_in_dim(lands[1], shards[1], chip, 0)
            pw = jnp.concatenate([g_pool[j].reshape(4, PG, PO // NCHIP) for j in range(NCHIP)], axis=2)
            return wcat, pw
        whole = fetch(group, after)
        if group == "mid":
            return whole[0].reshape(D, D), whole[1].reshape(D, D)
        if group == "up":
            return whole[0].reshape(NCHIP, D, D)
        return whole[0].reshape(DFF, D)

    small_w = pack_rows("pack_small_w", [wal_f[0].reshape(4, QK),
                                         jnp.concatenate([gng_f[0].reshape(1, 512), jnp.zeros((1, 512), F32)], axis=1)], 8)
    sw_all = gather_small("gather_small_w", small_w, False).reshape(8, 8, QK)
    wa_full = jnp.concatenate([sw_all[2 * j, 0:4].reshape(16, DK) for j in range(NCHIP)], axis=1)
    ng_full = jnp.concatenate([sw_all[2 * j, 4, 0:512].reshape(HEADS, DV // NCHIP) for j in range(NCHIP)], axis=1)
    wa_pad = _pad_rows(wa_full, APAD).astype(BF16)
    ng = ng_full.reshape(1, D)

    pending = {}
    wmv = {"in": (w_in_r, m_in_r, v_in_r), "gla_out": (big[1], m_w_gla_out, v_w_gla_out), "out": (big[2], m_w_out, v_w_out),
           "up": (big[3], m_w_mlp_up, v_w_mlp_up), "down": (big[4], m_w_mlp_down, v_w_mlp_down), "pool": (big[5], m_pool_w, v_pool_w)}
    big_res = {}

    def reduce_group(group, after):
        nms, send, recv, sums, lands = pending[group]
        sums, lands = scatter_wait("scatter_wait_" + group, send, recv, sums, lands, after)
        reduced = [sum_chips("sum_chips_" + nm, a, b, chip_i) for nm, a, b in zip(nms, sums, lands)]
        send, recv, reduced, lands, token = join_start("join_start_" + group, reduced)
        pending[group] = (nms, send, recv, reduced, lands)
        return token

    def update_group(group, after):
        nms, send, recv, reduced, lands = pending[group]
        reduced, from_sib = join_wait("join_wait_" + group, send, recv, reduced, lands, after)
        for nm, g_own, g_sib in zip(nms, reduced, from_sib):
            w, m, v = wmv[nm]
            shp = (2,) + g_own.shape
            big_res[nm] = adamw_halves("adamw_" + nm, w.reshape(shp), g_own, g_sib, m.reshape(shp), v.reshape(shp), core_i)

    def on_grad(group, grads):
        if group == "in":
            gw_in = grads_from_cat(grads["in_cat"])
            gw_pool = jnp.stack([grads["pool"][:, :, j * 128:(j + 1) * 128].reshape(2, 2 * PG, 128)
                                 for j in range(NCHIP)], axis=1)
            grads = {"in": gw_in, "pool": gw_pool}
        nms, parts = list(grads.keys()), list(grads.values())
        send, recv, parts, got, token = exchange_start("exchange_start_" + group, parts)
        pending[group] = (nms, send, recv, parts, got)
        return token

    def on_settle(group, after):
        if group == "in":
            after = reduce_group("down", after)
        nms, send, recv, parts, got = pending[group]
        parts, got = exchange_wait("exchange_wait_" + group, send, recv, parts, got, after)
        sums = [add_pairs("add_pair_" + nm, a, b, core_i) for nm, a, b in zip(nms, parts, got)]
        send, recv, sums, lands, token = scatter_start("scatter_start_" + group, sums)
        pending[group] = (nms, send, recv, sums, lands)
        if group != "in":
            return token
        token = reduce_group("up", token)
        token = reduce_group("mix", token)
        for earlier in ("down", "up", "mix"):
            update_group(earlier, token)
            token = big_res[pending[earlier][0][-1]][1]
        return [big_res[nm][1] for nm in ("down", "up", "out", "gla_out")]

    (loss_local, grad_x, g_mix, g_ps, g_mlp, g_nf, g_ng, g_ba, g_wa) = local_step(
        x2d, tgt, gf, norm_mix_g, pool_scale, wa_pad, b_alpha, ng, norm_mlp_g, get_w, on_grad, on_settle, tick)[:9]
    loss = lax.psum(loss_local, ("x", "y", "c"))
    join_in_token = reduce_group("in", grad_x)

    ROWS = 16

    def wide(a, n):
        return jnp.concatenate([a.reshape(1, n), jnp.zeros((1, D - n), F32)], axis=1)

    packed = pack_rows("pack_small_g", [g_mix, g_ps, g_mlp, g_nf, g_ng, wide(g_ba, QK), g_wa[0:16].reshape(8, D)], ROWS)
    tot = gather_small("reduce_small_g", packed, True, join_in_token)
    t_wa = lax.dynamic_slice(tot[6:14].reshape(16, QK), (0, chip * DK), (16, DK))
    t_ng = lax.dynamic_slice(tot[4].reshape(HEADS, DV), (0, chip * (DV // NCHIP)), (HEADS, DV // NCHIP))

    def pack_small(nm, mix, ps, mlp, nf, ba, wa, gn, after=None):
        return pack_rows(nm, [mix.reshape(1, D), ps.reshape(1, D), mlp.reshape(1, D), nf.reshape(1, D), wide(ba, QK),
                              wa.reshape(2, D), wide(gn, 512)], ROWS, after)

    update_group("in", tot)
    sg = pack_small("pack_g", tot[0], tot[1], tot[2], tot[3], tot[5, 0:QK], t_wa, t_ng, big_res["in"][3])
    sw = pack_small("pack_w", norm_mix_g, pool_scale, norm_mlp_g, norm_final_g, b_alpha, w_alpha, gla_norm_g)
    sm = pack_small("pack_m", m_norm_mix_g, m_pool_scale, m_norm_mlp_g, m_norm_final_g, m_b_alpha, m_w_alpha, m_gla_norm_g)
    sv = pack_small("pack_v", v_norm_mix_g, v_pool_scale, v_norm_mlp_g, v_norm_final_g, v_b_alpha, v_w_alpha, v_gla_norm_g)
    small_res = adamw("adamw_small", sw, sg, sm, sv)

    def unpack(p):
        return {"norm_mix_g": p[0].reshape(1, D), "pool_scale": p[1].reshape(1, D), "norm_mlp_g": p[2].reshape(1, D),
                "norm_final_g": p[3].reshape(D), "b_alpha": p[4, 0:QK].reshape(1, QK), "w_alpha": p[5:7].reshape(1, 16, DK),
                "gla_norm_g": p[7, 0:512].reshape(1, HEADS, DV // NCHIP)}

    order = ["norm_mix_g", "w_in", "pool_w", "pool_scale", "w_alpha", "b_alpha", "gla_norm_g", "w_gla_out", "w_out",
             "norm_mlp_g", "w_mlp_up", "w_mlp_down", "norm_final_g"]
    big_key = {"w_in": ("in", w_in.shape), "pool_w": ("pool", pool_w.shape), "w_gla_out": ("gla_out", w_gla_out.shape),
               "w_out": ("out", w_out.shape), "w_mlp_up": ("up", w_mlp_up.shape), "w_mlp_down": ("down", w_mlp_down.shape)}
    result = [loss, grad_x.reshape(1, T, D)]
    for kind in range(4):
        small = unpack(small_res[kind])
        for nm in order:
            if nm in big_key:
                key, shp = big_key[nm]
                result.append(big_res[key][kind].reshape(shp))
            else:
                result.append(small[nm])
    return tuple(result)
```

```python
import itertools

import jax
import jax.numpy as jnp
from jax import lax
from jax.experimental import pallas as pl
from jax.experimental.pallas import tpu as pltpu

F32 = jnp.float32
BF16 = jnp.bfloat16
SDS = jax.ShapeDtypeStruct
MESH = pl.DeviceIdType.MESH
ANY = pl.BlockSpec(memory_space=pl.ANY)

T = 2048
D = 2048
DFF = 8192
NCHIP = 4
IN_WIDTH = 11280
IN_SHARD = IN_WIDTH // NCHIP
CHUNK = 64
NCHUNK = T // CHUNK
HEADS = 4
DK = 256
DV = 512
QK = HEADS * DK
EPS = 1e-6
POOL_WINDOWS = (2, 4, 8, 16)
PG = 256
PO = 512

OV, OG, OGP, OGG, OU, OQ, OKK, OA = 0, 2048, 4096, 6144, 8192, 9216, 10240, 11264
NCAT = 11520
APAD = 128

VMEM_CAP = 56 * 1024 * 1024

PIECE_BYTES = 384 * 1024

ADAM_LR, ADAM_B1, ADAM_B2, ADAM_EPS, ADAM_WD, ADAM_STEP = 0.001, 0.9, 0.999, 1e-08, 0.01, 10


def _cparams(vmem_bytes=None, sem=None):
    kw = {}
    if vmem_bytes is not None:
        kw["vmem_limit_bytes"] = int(min(max(vmem_bytes, 32 * 1024 * 1024), VMEM_CAP))
    if sem is not None:
        kw["dimension_semantics"] = sem
    return pltpu.CompilerParams(**kw)


def _nbytes(shape, dtype):
    n = 1
    for s in shape:
        if s is not None:
            n *= s
    return n * jnp.dtype(dtype).itemsize


def _sigmoid(x):
    return 0.5 * jnp.tanh(0.5 * x) + 0.5


EPI_COLS = 512


def _as_list(after):
    if after is None:
        return []
    return list(after) if isinstance(after, (list, tuple)) else [after]


def matmul(name, a, b, *, a_spec, b_spec, cdims, grid, acc_shape, outs, extras=(), epi, after=None):
    nj, ni, nk = grid
    ne, no = len(extras), len(outs)
    afters = _as_list(after)
    first_out = 2 + ne + len(afters)

    def body(*refs):
        a_ref, b_ref = refs[0], refs[1]
        ex = refs[2:2 + ne]
        out_refs = refs[first_out:first_out + no]
        i = pl.program_id(1)
        part = lax.dot_general(a_ref[...], b_ref[...], (cdims, ((), ())), preferred_element_type=F32)
        if nk == 1:
            epi(part, ex, out_refs, i)
        else:
            acc_ref = refs[first_out + no]
            k = pl.program_id(2)

            @pl.when(k == 0)
            def _():
                acc_ref[...] = part

            @pl.when(k > 0)
            def _():
                acc_ref[...] += part

            @pl.when(k == nk - 1)
            def _():
                epi(acc_ref[...], ex, out_refs, i)

    in_specs = [pl.BlockSpec(*a_spec), pl.BlockSpec(*b_spec)] + [pl.BlockSpec(bs, im) for _, bs, im in extras]
    in_specs += [ANY] * len(afters)
    out_specs = [pl.BlockSpec(bs, im) for _, _, bs, im in outs]
    out_shape = [SDS(s, dt) for s, dt, _, _ in outs]
    vm = 2 * (_nbytes(a_spec[0], a.dtype) + _nbytes(b_spec[0], b.dtype))
    vm += 2 * sum(_nbytes(bs, arr.dtype) for arr, bs, _ in extras)
    vm += 2 * sum(_nbytes(bs, dt) for _, dt, bs, _ in outs)
    vm += 6 * _nbytes(acc_shape, F32)
    scratch = [pltpu.VMEM(acc_shape, F32)] if nk > 1 else []
    return pl.pallas_call(
        body, name=name, grid=grid, in_specs=in_specs, out_specs=out_specs, out_shape=out_shape,
        scratch_shapes=scratch,
        compiler_params=_cparams(vm, ("arbitrary", "arbitrary", "arbitrary")),
    )(a, b, *[arr for arr, _, _ in extras], *afters)


NN =((1,), (0,))
NT = ((1,), (1,))
TN = ((0,), (0,))


def _row_acc(out_ref, val, i):
    @pl.when(i == 0)
    def _():
        out_ref[...] = val

    @pl.when(i > 0)
    def _():
        out_ref[...] += val


def _rms_bwd(xn, r, dxn):
    return r * (dxn - xn * jnp.mean(dxn * xn, axis=-1, keepdims=True))


def norm1(x, g):
    tm = 256

    def body(x_ref, g_ref, h_ref):
        xv = x_ref[...]
        r = lax.rsqrt(jnp.mean(xv * xv, axis=-1, keepdims=True) + EPS)
        h_ref[...] = (xv * r * g_ref[...]).astype(BF16)

    return pl.pallas_call(
        body, name="norm1", grid=(T // tm,),
        in_specs=[pl.BlockSpec((tm, D), lambda i: (i, 0)), pl.BlockSpec((1, D), lambda i: (0, 0))],
        out_specs=pl.BlockSpec((tm, D), lambda i: (i, 0)), out_shape=SDS((T, D), BF16),
        compiler_params=_cparams(32 * 1024 * 1024, ("arbitrary",)),
    )(x, g)


def mm_in(h1, wcat):
    tm, tn = 1024, 1280

    def epi(acc, ex, outs, i):
        outs[0][...] = acc.astype(BF16)

    return matmul("mm_in", h1, wcat, a_spec=((tm, D), lambda j, i, k: (i, 0)), b_spec=((D, tn), lambda j, i, k: (0, j)),
                  cdims=NN, grid=(NCAT // tn, T // tm, 1), acc_shape=(tm, tn),
                  outs=[((T, NCAT), BF16, (tm, tn), lambda j, i, k: (i, j))], epi=epi)[0]


def _window_sum(x, w, up):
    n = x.shape[0]
    row = lax.broadcasted_iota(jnp.int32, x.shape, 0)
    s, sh = x, 1
    while sh < w:
        if up:
            s = s + jnp.where(row < n - sh, pltpu.roll(s, n - sh, axis=0), 0.0)
        else:
            s = s + jnp.where(row >= sh, pltpu.roll(s, sh, axis=0), 0.0)
        sh *= 2
    return s


def _inv_count(shape, w):
    row = lax.broadcasted_iota(jnp.int32, shape, 0)
    return 1.0 / jnp.minimum(row + 1, w).astype(F32)


def pool_fwd(pcat, pw):
    def body(u_ref, pw_ref, d_ref, y_ref):
        for gi, w in enumerate(POOL_WINDOWS):
            ug = u_ref[:, gi * PG:(gi + 1) * PG].astype(F32)
            dg = _window_sum(ug, w, False) * _inv_count(ug.shape, w) - ug
            db = dg.astype(BF16)
            d_ref[:, gi * PG:(gi + 1) * PG] = db
            y_ref[:, gi * PO:(gi + 1) * PO] = jnp.dot(db, pw_ref[gi], preferred_element_type=F32).astype(BF16)

    return pl.pallas_call(
        body, name="pool_fwd", grid=(1,),
        in_specs=[pl.BlockSpec((T, 4 * PG), lambda i: (0, OU // (4 * PG))), pl.BlockSpec((4, PG, PO), lambda i: (0, 0, 0))],
        out_specs=[pl.BlockSpec((T, 4 * PG), lambda i: (0, 0)), pl.BlockSpec((T, D), lambda i: (0, 0))],
        out_shape=[SDS((T, 4 * PG), BF16), SDS((T, D), BF16)],
        compiler_params=_cparams(48 * 1024 * 1024, ("arbitrary",)),
    )(pcat, pw)


def pool_bwd(dylin, d, pw):
    def body(dy_ref, d_ref, pw_ref, du_ref, dpw_ref):
        for gi, w in enumerate(POOL_WINDOWS):
            dyl = dy_ref[:, gi * PO:(gi + 1) * PO]
            dd = lax.dot_general(dyl, pw_ref[gi], (NT, ((), ())), preferred_element_type=F32)
            du = _window_sum(dd * _inv_count(dd.shape, w), w, True) - dd
            du_ref[:, gi * PG:(gi + 1) * PG] = du.astype(BF16)
            dpw_ref[gi] = lax.dot_general(d_ref[:, gi * PG:(gi + 1) * PG], dyl, (TN, ((), ())),
                                          preferred_element_type=F32).astype(BF16)

    return pl.pallas_call(
        body, name="pool_bwd", grid=(1,),
        in_specs=[pl.BlockSpec((T, D), lambda i: (0, 0)), pl.BlockSpec((T, 4 * PG), lambda i: (0, 0)),
                  pl.BlockSpec((4, PG, PO), lambda i: (0, 0, 0))],
        out_specs=[pl.BlockSpec((T, 4 * PG), lambda i: (0, 0)), pl.BlockSpec((4, PG, PO), lambda i: (0, 0, 0))],
        out_shape=[SDS((T, 4 * PG), BF16), SDS((4, PG, PO), BF16)],
        compiler_params=_cparams(48 * 1024 * 1024, ("arbitrary",)),
    )(dylin, d, pw)


def _gate_decay(alow, wa, ba):
    a = jnp.dot(alow, wa, preferred_element_type=F32) + ba
    ls = jax.nn.log_sigmoid(a) * (1.0 / 16.0)
    r = lax.broadcasted_iota(jnp.int32, (CHUNK, CHUNK), 0)
    c = lax.broadcasted_iota(jnp.int32, (CHUNK, CHUNK), 1)
    tri = jnp.where(c <= r, 1.0, 0.0).astype(F32)
    cum = jnp.dot(tri, ls, preferred_element_type=F32, precision=lax.Precision.HIGHEST)
    last = cum[CHUNK - 1:CHUNK, :]
    return a, jnp.exp(last - cum), jnp.exp(last)


def gla_fwd(pcat, wa, ba, ng, after=None):
    afters = _as_list(after)

    def body(q_ref, k_ref, v_ref, g_ref, al_ref, wa_ref, ba_ref, ng_ref, *rest):
        og_ref, o_ref, st_ref, s_scr = rest[len(afters):]

        @pl.when(pl.program_id(0) == 0)
        def _():
            s_scr[...] = jnp.zeros_like(s_scr)

        _, e, decay = _gate_decay(al_ref[...], wa_ref[...], ba_ref[...])
        kd = (k_ref[...].astype(F32) * e).astype(BF16)
        qs = (q_ref[...].astype(F32) * (DK ** -0.5)).astype(BF16)
        for h in range(HEADS):
            ck = slice(h * DK, (h + 1) * DK)
            cv = slice(h * DV, (h + 1) * DV)
            s_new = s_scr[h] * decay[:, ck] + lax.dot_general(v_ref[:, cv], kd[:, ck], (TN, ((), ())),
                                                               preferred_element_type=F32)
            s_scr[h] = s_new
            sb = s_new.astype(BF16)
            st_ref[h] = sb
            oh = lax.dot_general(qs[:, ck], sb, (NT, ((), ())), preferred_element_type=F32)
            o_ref[:, cv] = oh.astype(BF16)
            on = oh * lax.rsqrt(jnp.mean(oh * oh, axis=-1, keepdims=True) + EPS) * ng_ref[:, cv]
            gv = g_ref[:, cv].astype(F32)
            og_ref[:, cv] = (on * (gv * _sigmoid(gv))).astype(BF16)

    row = lambda c: (c, 0)
    return pl.pallas_call(
        body, name="gla_fwd", grid=(NCHUNK,),
        in_specs=[pl.BlockSpec((CHUNK, QK), lambda c: (c, OQ // QK)), pl.BlockSpec((CHUNK, QK), lambda c: (c, OKK // QK)),
                  pl.BlockSpec((CHUNK, D), lambda c: (c, OV // D)), pl.BlockSpec((CHUNK, D), lambda c: (c, OG // D)),
                  pl.BlockSpec((CHUNK, APAD), lambda c: (c, OA // APAD)),
                  pl.BlockSpec((APAD, QK), lambda c: (0, 0)), pl.BlockSpec((1, QK), lambda c: (0, 0)),
                  pl.BlockSpec((1, D), lambda c: (0, 0))] + [ANY] * len(afters),
        out_specs=[pl.BlockSpec((CHUNK, D), row), pl.BlockSpec((CHUNK, D), row),
                   pl.BlockSpec((None, HEADS, DV, DK), lambda c: (c, 0, 0, 0))],
        out_shape=[SDS((T, D), BF16), SDS((T, D), BF16), SDS((NCHUNK, HEADS, DV, DK), BF16)],
        scratch_shapes=[pltpu.VMEM((HEADS, DV, DK), F32)],
        compiler_params=_cparams(32 * 1024 * 1024, ("arbitrary",)),
    )(pcat, pcat, pcat, pcat, pcat, wa, ba, ng, *afters)


def gla_bwd(do, pcat, states, wa, ba, after):
    def body(do_ref, q_ref, k_ref, v_ref, al_ref, sc_ref, sp_ref, wa_ref, ba_ref, after_ref,
             dq_ref, dk_ref, dv_ref, dal_ref, dwa_ref, dba_ref, ds_scr):
        i = pl.program_id(0)

        @pl.when(i == 0)
        def _():
            ds_scr[...] = jnp.zeros_like(ds_scr)

        has_prev = jnp.where(i < NCHUNK - 1, 1.0, 0.0).astype(F32)
        a, e, decay = _gate_decay(al_ref[...], wa_ref[...], ba_ref[...])
        kf = k_ref[...].astype(F32)
        kdf = kf * e
        kd = kdf.astype(BF16)
        qs = (q_ref[...].astype(F32) * (DK ** -0.5)).astype(BF16)
        dkd_parts, ddecay_parts = [], []
        for h in range(HEADS):
            ck = slice(h * DK, (h + 1) * DK)
            cv = slice(h * DV, (h + 1) * DV)
            doh = do_ref[:, cv]
            ds = ds_scr[h] + lax.dot_general(doh, qs[:, ck], (TN, ((), ())), preferred_element_type=F32)
            dsb = ds.astype(BF16)
            dq_ref[:, ck] = (jnp.dot(doh, sc_ref[h], preferred_element_type=F32) * (DK ** -0.5)).astype(BF16)
            dkd_parts.append(jnp.dot(v_ref[:, cv], dsb, preferred_element_type=F32))
            dv_ref[:, cv] = lax.dot_general(kd[:, ck], dsb, (NT, ((), ())), preferred_element_type=F32).astype(BF16)
            ddecay_parts.append(jnp.sum(ds * sp_ref[h].astype(F32), axis=0, keepdims=True) * has_prev)
            ds_scr[h] = ds * decay[:, ck]
        dkd = jnp.concatenate(dkd_parts, axis=1)
        ddecay = jnp.concatenate(ddecay_parts, axis=1)
        dk_ref[...] = (dkd * e).astype(BF16)
        dearg = dkd * kdf
        dlast = jnp.sum(dearg, axis=0, keepdims=True) + ddecay * decay
        r = lax.broadcasted_iota(jnp.int32, (CHUNK, CHUNK), 0)
        c = lax.broadcasted_iota(jnp.int32, (CHUNK, CHUNK), 1)
        triu = jnp.where(c >= r, 1.0, 0.0).astype(F32)
        dls = dlast - jnp.dot(triu, dearg, preferred_element_type=F32, precision=lax.Precision.HIGHEST)
        da = dls * (1.0 / 16.0) * (1.0 - _sigmoid(a))
        dab = da.astype(BF16)
        dal_ref[...] = lax.dot_general(dab, wa_ref[...], (NT, ((), ())), preferred_element_type=F32).astype(BF16)
        dwa = lax.dot_general(al_ref[...], dab, (TN, ((), ())), preferred_element_type=F32)
        dba = jnp.sum(da, axis=0, keepdims=True)

        @pl.when(i == 0)
        def _():
            dwa_ref[...] = dwa
            dba_ref[...] = dba

        @pl.when(i > 0)
        def _():
            dwa_ref[...] += dwa
            dba_ref[...] += dba

    rev = lambda i: NCHUNK - 1 - i
    return pl.pallas_call(
        body, name="gla_bwd", grid=(NCHUNK,),
        in_specs=[pl.BlockSpec((CHUNK, D), lambda i: (rev(i), 0)),
                  pl.BlockSpec((CHUNK, QK), lambda i: (rev(i), OQ // QK)), pl.BlockSpec((CHUNK, QK), lambda i: (rev(i), OKK // QK)),
                  pl.BlockSpec((CHUNK, D), lambda i: (rev(i), OV // D)), pl.BlockSpec((CHUNK, APAD), lambda i: (rev(i), OA // APAD)),
                  pl.BlockSpec((None, HEADS, DV, DK), lambda i: (rev(i), 0, 0, 0)),
                  pl.BlockSpec((None, HEADS, DV, DK), lambda i: (jnp.maximum(rev(i) - 1, 0), 0, 0, 0)),
                  pl.BlockSpec((APAD, QK), lambda i: (0, 0)), pl.BlockSpec((1, QK), lambda i: (0, 0)), ANY],
        out_specs=[pl.BlockSpec((CHUNK, QK), lambda i: (rev(i), 0)), pl.BlockSpec((CHUNK, QK), lambda i: (rev(i), 0)),
                   pl.BlockSpec((CHUNK, D), lambda i: (rev(i), 0)), pl.BlockSpec((CHUNK, APAD), lambda i: (rev(i), 0)),
                   pl.BlockSpec((APAD, QK), lambda i: (0, 0)), pl.BlockSpec((1, QK), lambda i: (0, 0))],
        out_shape=[SDS((T, QK), BF16), SDS((T, QK), BF16), SDS((T, D), BF16), SDS((T, APAD), BF16),
                   SDS((APAD, QK), F32), SDS((1, QK), F32)],
        scratch_shapes=[pltpu.VMEM((HEADS, DV, DK), F32)],
        compiler_params=_cparams(32 * 1024 * 1024, ("arbitrary",)),
    )(do, pcat, pcat, pcat, pcat, states, states, wa, ba, after)


TMF = 256
TMW = 512
_rowblk = ((TMF, D), lambda j, i, k: (i, 0))
_vec = ((1, D), lambda j, i, k: (0, 0))


def _full_spec(col):
    return ((TMF, D), lambda j, i, k: (i, col))


TBIG = 1024


def square_matmul(name, a, b, *, a_spec, b_spec, cdims, nk, after=None):
    def epi(acc, ex, outs, i):
        outs[0][...] = acc

    return matmul(name, a, b, a_spec=a_spec, b_spec=b_spec, cdims=cdims, grid=(D // TBIG, T // TBIG, nk),
                  acc_shape=(TBIG, TBIG), outs=[((T, D), F32, (TBIG, TBIG), lambda j, i, k: (i, j))], epi=epi,
                  after=after)[0]


def rowwise(name, y, *, extras, outs, epi):
    ne = len(extras)

    def body(*refs):
        epi(refs[0][...], refs[1:1 + ne], refs[1 + ne:], pl.program_id(1))

    in_specs = [pl.BlockSpec(*_rowblk)] + [pl.BlockSpec(bs, im) for _, bs, im in extras]
    return pl.pallas_call(
        body, name=name, grid=(1, T // TMF, 1), in_specs=in_specs,
        out_specs=[pl.BlockSpec(bs, im) for _, _, bs, im in outs], out_shape=[SDS(s, dt) for s, dt, _, _ in outs],
        compiler_params=_cparams(40 * 1024 * 1024, ("arbitrary", "arbitrary", "arbitrary")),
    )(y, *[arr for arr, _, _ in extras])


def mm_gla_out(og, w, ylin, pcat, pscale):
    def epi(acc, ex, outs, i):
        ylin_ref, lgp_ref, lgg_ref, ps_ref = ex
        for c0 in range(0, D, EPI_COLS):
            cs = slice(c0, c0 + EPI_COLS)
            gp = _sigmoid(lgp_ref[:, cs].astype(F32))
            gg = _sigmoid(lgg_ref[:, cs].astype(F32))
            a = acc[:, cs]
            outs[0][:, cs] = (gp * (ylin_ref[:, cs].astype(F32) * ps_ref[:, cs]) + gg * a).astype(BF16)
            outs[1][:, cs] = a.astype(BF16)

    return matmul("mm_gla_out", og, w, a_spec=_rowblk, b_spec=((D, D), lambda j, i, k: (0, 0)), cdims=NN,
                  grid=(1, T // TMF, 1), acc_shape=(TMF, D),
                  extras=[(ylin, *_rowblk), (pcat, *_full_spec(OGP // D)), (pcat, *_full_spec(OGG // D)), (pscale, *_vec)],
                  outs=[((T, D), BF16, *_rowblk), ((T, D), BF16, *_rowblk)], epi=epi)


def mm_out(mixed, w, x, g2):
    def epi(acc, ex, outs, i):
        x_ref, g_ref = ex
        x2 = x_ref[...] + acc
        r = lax.rsqrt(jnp.mean(x2 * x2, axis=-1, keepdims=True) + EPS)
        outs[0][...] = x2
        outs[1][...] = (x2 * r * g_ref[...]).astype(BF16)

    return matmul("mm_out", mixed, w, a_spec=_rowblk, b_spec=((D, D), lambda j, i, k: (0, 0)), cdims=NN,
                  grid=(1, T // TMF, 1), acc_shape=(TMF, D), extras=[(x, *_rowblk), (g2, *_vec)],
                  outs=[((T, D), F32, *_rowblk), ((T, D), BF16, *_rowblk)], epi=epi)


def mm_up(h2, wup):
    def epi(acc, ex, outs, i):
        r = jnp.maximum(acc, 0.0)
        outs[0][...] = r.astype(BF16)
        outs[1][...] = (r * r).astype(BF16)

    blk = ((TMW, D), lambda j, i, k: (i, j))
    return matmul("mm_up", h2, wup, a_spec=((TMW, D), lambda j, i, k: (i, 0)), b_spec=((None, D, D), lambda j, i, k: (j, 0, 0)),
                  cdims=NN, grid=(NCHIP, T // TMW, 1), acc_shape=(TMW, D),
                  outs=[((T, DFF), BF16, *blk), ((T, DFF), BF16, *blk)], epi=epi)


def mm_down(act, wdown, x2, tgt, gf):
    tk = 4096

    def epi(acc, ex, outs, i):
        x2_ref, t_ref, g_ref = ex
        dx_ref, dxb_ref, gnf_ref, loss_ref = outs
        x3 = x2_ref[...] + acc
        r = lax.rsqrt(jnp.mean(x3 * x3, axis=-1, keepdims=True) + EPS)
        xn = x3 * r
        err = xn * g_ref[...] - t_ref[...]
        lsum = 0.5 * jnp.sum(jnp.mean(err * err, axis=-1, keepdims=True), axis=0, keepdims=True)
        dy = err * (1.0 / D)
        _row_acc(gnf_ref, jnp.sum(dy * xn, axis=0, keepdims=True), i)
        _row_acc(loss_ref, jnp.broadcast_to(lsum, (1, 128)), i)
        dx3 = _rms_bwd(xn, r, dy * g_ref[...])
        dx_ref[...] = dx3
        dxb_ref[...] = dx3.astype(BF16)

    y = square_matmul("mm_down", act, wdown, a_spec=((TBIG, tk), lambda j, i, k: (i, k)),
                      b_spec=((tk, TBIG), lambda j, i, k: (k, j)), cdims=NN, nk=DFF // tk)
    return rowwise("rows_final", y, extras=[(x2, *_rowblk), (tgt, *_rowblk), (gf, *_vec)],
                   outs=[((T, D), F32, *_rowblk), ((T, D), BF16, *_rowblk), ((1, D), F32, *_vec),
                         ((1, 128), F32, (1, 128), lambda j, i, k: (0, 0))], epi=epi)


def mm_dact(dx3b, wdown, rup, after=None):
    def epi(acc, ex, outs, i):
        outs[0][...] = (acc * 2.0 * ex[0][...].astype(F32)).astype(BF16)

    blk = ((TMW, D), lambda j, i, k: (i, j))
    return matmul("mm_dact", dx3b, wdown, a_spec=((TMW, D), lambda j, i, k: (i, 0)), b_spec=((D, D), lambda j, i, k: (j, 0)),
                  cdims=NT, grid=(DFF // D, T // TMW, 1), acc_shape=(TMW, D), extras=[(rup, *blk)],
                  outs=[((T, DFF), BF16, *blk)], epi=epi, after=after)[0]


def mm_wgrad(name, a, b, m, n, out_shape, out_block, out_map, tm, tn, after=None):
    def epi(acc, ex, outs, i):
        outs[0][...] = acc.astype(BF16).reshape(outs[0].shape)

    return matmul(name, a, b, a_spec=((T, tm), lambda j, i, k: (0, i)), b_spec=((T, tn), lambda j, i, k: (0, j)),
                  cdims=TN, grid=(n // tn, m // tm, 1), acc_shape=(tm, tn),
                  outs=[(out_shape, BF16, out_block, out_map)], epi=epi, after=after)[0]


def mm_dh2(dup, wup, x2, dx3, g2, after=None):
    def epi(acc, ex, outs, i):
        x2_ref, dx3_ref, g_ref = ex
        x2 = x2_ref[...]
        r = lax.rsqrt(jnp.mean(x2 * x2, axis=-1, keepdims=True) + EPS)
        xn = x2 * r
        _row_acc(outs[2], jnp.sum(acc * xn, axis=0, keepdims=True), i)
        dx2 = dx3_ref[...] + _rms_bwd(xn, r, acc * g_ref[...])
        outs[0][...] = dx2
        outs[1][...] = dx2.astype(BF16)

    y = square_matmul("mm_dh2", dup, wup, a_spec=((TBIG, D), lambda j, i, k: (i, k)),
                      b_spec=((None, TBIG, D), lambda j, i, k: (k, j, 0)), cdims=NT, nk=NCHIP, after=after)
    return rowwise("rows_dh2", y, extras=[(x2, *_rowblk), (dx3, *_rowblk), (g2, *_vec)],
                   outs=[((T, D), F32, *_rowblk), ((T, D), BF16, *_rowblk), ((1, D), F32, *_vec)], epi=epi)


def mm_dmixed(dx2b, wout, pcat, ylin, ygla, pscale, after=None):
    def epi(acc, ex, outs, i):
        lgp_ref, lgg_ref, ylin_ref, ygla_ref, ps_ref = ex
        dps = []
        for c0 in range(0, D, EPI_COLS):
            cs = slice(c0, c0 + EPI_COLS)
            gp = _sigmoid(lgp_ref[:, cs].astype(F32))
            gg = _sigmoid(lgg_ref[:, cs].astype(F32))
            yl = ylin_ref[:, cs].astype(F32)
            ps = ps_ref[:, cs]
            a = acc[:, cs]
            agp = a * gp
            outs[0][:, cs] = (agp * ps).astype(BF16)
            outs[1][:, cs] = (a * gg).astype(BF16)
            outs[2][:, cs] = (agp * (yl * ps) * (1.0 - gp)).astype(BF16)
            outs[3][:, cs] = (a * ygla_ref[:, cs].astype(F32) * gg * (1.0 - gg)).astype(BF16)
            dps.append(jnp.sum(agp * yl, axis=0, keepdims=True))
        _row_acc(outs[4], jnp.concatenate(dps, axis=1), i)

    return matmul("mm_dmixed", dx2b, wout, a_spec=_rowblk, b_spec=((D, D), lambda j, i, k: (0, 0)), cdims=NT,
                  grid=(1, T // TMF, 1), acc_shape=(TMF, D),
                  extras=[(pcat, *_full_spec(OGP // D)), (pcat, *_full_spec(OGG // D)), (ylin, *_rowblk), (ygla, *_rowblk),
                          (pscale, *_vec)],
                  outs=[((T, D), BF16, *_rowblk)] * 4 + [((1, D), F32, *_vec)], epi=epi, after=after)


def mm_dog(dygla, wgo, o, pcat, ng, after=None):
    def epi(acc, ex, outs, i):
        o_ref, g_ref, ng_ref = ex
        do_ref, dg_ref, gng_ref = outs
        gparts = []
        for h in range(HEADS):
            cv = slice(h * DV, (h + 1) * DV)
            oh = o_ref[:, cv].astype(F32)
            r = lax.rsqrt(jnp.mean(oh * oh, axis=-1, keepdims=True) + EPS)
            on = oh * r
            gv = g_ref[:, cv].astype(F32)
            sg = _sigmoid(gv)
            a = acc[:, cv]
            dgain = a * (gv * sg)
            gparts.append(jnp.sum(dgain * on, axis=0, keepdims=True))
            ngh = ng_ref[:, cv]
            do_ref[:, cv] = _rms_bwd(on, r, dgain * ngh).astype(BF16)
            dg_ref[:, cv] = (a * (on * ngh) * (sg * (1.0 + gv * (1.0 - sg)))).astype(BF16)
        _row_acc(gng_ref, jnp.concatenate(gparts, axis=1), i)

    return matmul("mm_dog", dygla, wgo, a_spec=_rowblk, b_spec=((D, D), lambda j, i, k: (0, 0)), cdims=NT,
                  grid=(1, T // TMF, 1), acc_shape=(TMF, D),
                  extras=[(o, *_rowblk), (pcat, *_full_spec(OG // D)), (ng, *_vec)],
                  outs=[((T, D), BF16, *_rowblk), ((T, D), BF16, *_rowblk), ((1, D), F32, *_vec)], epi=epi, after=after)


def mm_dh1(dpcat, wcat, x, dx2, g1, after=None):
    tk = 3840

    def epi(acc, ex, outs, i):
        x_ref, dx2_ref, g_ref = ex
        xv = x_ref[...]
        r = lax.rsqrt(jnp.mean(xv * xv, axis=-1, keepdims=True) + EPS)
        xn = xv * r
        _row_acc(outs[1], jnp.sum(acc * xn, axis=0, keepdims=True), i)
        outs[0][...] = dx2_ref[...] + _rms_bwd(xn, r, acc * g_ref[...])

    y = square_matmul("mm_dh1", dpcat, wcat, a_spec=((TBIG, tk), lambda j, i, k: (i, k)),
                      b_spec=((TBIG, tk), lambda j, i, k: (j, k)), cdims=NT, nk=NCAT // tk, after=after)
    return rowwise("rows_dh1", y, extras=[(x, *_rowblk), (dx2, *_rowblk), (g1, *_vec)],
                   outs=[((T, D), F32, *_rowblk), ((1, D), F32, *_vec)], epi=epi)


def _tile_rows(rows, cols, n_arrays):
    tm = rows
    while tm % 32 == 0 and 2 * n_arrays * tm * cols * 4 > 36 * 1024 * 1024:
        tm //= 2
    return tm


def add_pairs(name, parts, theirs, core):
    _, _, r, c = parts.shape
    tm = _tile_rows(r, c, 3)

    def body(core_ref, a_ref, b_ref, o_ref):
        o_ref[...] = (a_ref[...].astype(F32) + b_ref[...].astype(F32)).astype(BF16)

    spec = pl.BlockSpec((None, tm, c), lambda j, i, core_ref: (j, i, 0))
    grid_spec = pltpu.PrefetchScalarGridSpec(
        num_scalar_prefetch=1, grid=(NCHIP, r // tm),
        in_specs=[pl.BlockSpec((None, None, tm, c), lambda j, i, core_ref: (core_ref[0], j, i, 0)), spec], out_specs=spec)
    return pl.pallas_call(body, name=name, grid_spec=grid_spec, out_shape=SDS((NCHIP, r, c), BF16),
                          compiler_params=_cparams(40 * 1024 * 1024, ("arbitrary", "arbitrary")))(core, parts, theirs)


def sum_chips(name, sums, landed, chip):
    _, r, c = sums.shape
    tm = _tile_rows(r, c, 4)

    def body(chip_ref, own_ref, l_ref, o_ref):
        s = own_ref[...].astype(F32)
        for t in range(NCHIP - 1):
            s = s + l_ref[t].astype(F32)
        o_ref[...] = s

    grid_spec = pltpu.PrefetchScalarGridSpec(
        num_scalar_prefetch=1, grid=(r // tm,),
        in_specs=[pl.BlockSpec((None, tm, c), lambda i, chip_ref: (chip_ref[0], i, 0)),
                  pl.BlockSpec((NCHIP - 1, tm, c), lambda i, chip_ref: (0, i, 0))],
        out_specs=pl.BlockSpec((tm, c), lambda i, chip_ref: (i, 0)))
    return pl.pallas_call(body, name=name, grid_spec=grid_spec, out_shape=SDS((r, c), F32),
                          compiler_params=_cparams(40 * 1024 * 1024, ("arbitrary",)))(chip, sums, landed)


def _adamw_math(wv, gv, mv, vv):
    mn = ADAM_B1 * mv + (1.0 - ADAM_B1) * gv
    vn = ADAM_B2 * vv + (1.0 - ADAM_B2) * (gv * gv)
    mh = mn / (1.0 - ADAM_B1 ** ADAM_STEP)
    vh = vn / (1.0 - ADAM_B2 ** ADAM_STEP)
    return -ADAM_LR * (mh / (jnp.sqrt(vh) + ADAM_EPS) + ADAM_WD * wv), mn, vn


def adamw(name, w, g, m, v):
    def body(w_ref, g_ref, m_ref, v_ref, go_ref, d_ref, mo_ref, vo_ref):
        gv = g_ref[...]
        go_ref[...] = gv
        d_ref[...], mo_ref[...], vo_ref[...] = _adamw_math(w_ref[...], gv, m_ref[...], v_ref[...])

    return pl.pallas_call(body, name=name, out_shape=[SDS(w.shape, F32)] * 4)(w, g, m, v)


def adamw_halves(name, w, g_own, g_sib, m, v, core):
    _, r, c = w.shape
    tm = _tile_rows(r, c, 10)

    def body(core_ref, w_ref, go_ref, gs_ref, m_ref, v_ref, g_out, d_out, m_out, v_out):
        gv = jnp.where(pl.program_id(0) == core_ref[0], go_ref[...], gs_ref[...])
        g_out[...] = gv
        d_out[...], m_out[...], v_out[...] = _adamw_math(w_ref[...], gv, m_ref[...], v_ref[...])

    full = pl.BlockSpec((None, tm, c), lambda h, i, core_ref: (h, i, 0))
    own = pl.BlockSpec((tm, c), lambda h, i, core_ref: (jnp.where(h == core_ref[0], i, 0), 0))
    sib = pl.BlockSpec((tm, c), lambda h, i, core_ref: (jnp.where(h == core_ref[0], 0, i), 0))
    grid_spec = pltpu.PrefetchScalarGridSpec(num_scalar_prefetch=1, grid=(2, r // tm),
                                             in_specs=[full, own, sib, full, full], out_specs=[full] * 4)
    return pl.pallas_call(body, name=name, grid_spec=grid_spec, out_shape=[SDS(w.shape, F32)] * 4,
                          compiler_params=_cparams(48 * 1024 * 1024, ("arbitrary", "arbitrary")))(core, w, g_own, g_sib, m, v)


def cast_bf16(name, w):
    _, r, c = w.shape
    tm = _tile_rows(r, c, 2)

    def body(w_ref, o_ref):
        o_ref[...] = w_ref[...].astype(BF16)

    spec = pl.BlockSpec((None, tm, c), lambda h, i: (h, i, 0))
    return pl.pallas_call(body, name=name, grid=(2, r // tm), in_specs=[spec], out_specs=spec, out_shape=SDS(w.shape, BF16),
                          compiler_params=_cparams(40 * 1024 * 1024, ("arbitrary", "arbitrary")))(w)


def pack_rows(name, parts, rows, after=None):
    width = parts[0].shape[1]
    n = len(parts)
    afters = _as_list(after)

    def body(*refs):
        out_ref = refs[n + len(afters)]
        out_ref[...] = jnp.zeros_like(out_ref)
        off = 0
        for p in refs[:n]:
            out_ref[off:off + p.shape[0], :] = p[...]
            off += p.shape[0]

    vm = pl.BlockSpec(memory_space=pltpu.VMEM)
    return pl.pallas_call(body, name=name, in_specs=[vm] * n + [ANY] * len(afters), out_specs=vm,
                          out_shape=SDS((rows, width), F32))(*parts, *afters)


def _place():
    x, y, c = lax.axis_index("x"), lax.axis_index("y"), lax.axis_index("c")
    chips = [(1 - x, y), (x, 1 - y), (1 - x, 1 - y)]
    return x, y, c, chips


def _row_split(shape, dtype):
    r, c = shape
    n = 1
    while r % (2 * n) == 0 and (r // (2 * n)) % 16 == 0 and (r // n) * c * jnp.dtype(dtype).itemsize > PIECE_BYTES:
        n *= 2
    return [pl.ds(s * (r // n), r // n) for s in range(n)]


def _pieces(ref):
    *lead, r, c = ref.shape
    split = _row_split((r, c), ref.dtype)
    return [ref.at[(*idx, s)] for idx in itertools.product(*[range(d) for d in lead]) for s in split]


HBM = pl.BlockSpec(memory_space=pltpu.HBM)
SEM = pl.BlockSpec(memory_space=pltpu.SEMAPHORE)
EFFECT = pltpu.SideEffectType.DATAFLOW_SIDE_EFFECTING


def gather_start(name, shards, after=None):
    n = len(shards)
    afters = _as_list(after)

    def body(*refs):
        src, land = refs[:n], refs[n:2 * n]
        send, recv = refs[2 * n + len(afters)], refs[2 * n + len(afters) + 1]
        x, y, c, chips = _place()
        me = 2 * x + y
        for a in range(n):
            for j, (cx, cy) in enumerate(chips[:2]):
                for sp, dp in zip(_pieces(src[a].at[c]), _pieces(land[a].at[me, c])):
                    pltpu.make_async_remote_copy(sp, dp, send.at[2 * a + j], recv.at[2 * a + j],
                                                 device_id=(cx, cy, c), device_id_type=MESH).start()

    lands = [pltpu.with_memory_space_constraint(lax.empty((NCHIP,) + s.shape, s.dtype), pltpu.HBM) for s in shards]
    srcs = [pltpu.with_memory_space_constraint(s, pltpu.HBM) for s in shards]
    outs = pl.pallas_call(
        body, name=name,
        out_shape=(pltpu.SemaphoreType.DMA((2 * n,)), pltpu.SemaphoreType.DMA((2 * n,)),
                   *[pltpu.HBM(s.shape, s.dtype) for s in shards], *[pltpu.HBM(l.shape, l.dtype) for l in lands]),
        in_specs=[HBM] * (2 * n) + [ANY] * len(afters), out_specs=(SEM, SEM, *([HBM] * (2 * n))),
        input_output_aliases={i: 2 + i for i in range(2 * n)},
        compiler_params=pltpu.CompilerParams(has_side_effects=EFFECT),
    )(*srcs, *lands, *afters)
    return outs[0], outs[1], list(outs[2:2 + n]), list(outs[2 + n:2 + 2 * n])


def _relay_blocks(land, c, chips):
    (xx, xy), (yx, yy), (dx, dy) = chips
    rows = land.shape[2] // 2
    upper, lower = pl.ds(0, rows), pl.ds(rows, rows)
    return [(land.at[2 * yx + yy, c, lower], land.at[2 * dx + dy, c, lower]),
            (land.at[2 * xx + xy, c, upper], land.at[2 * dx + dy, c, upper])]


def relay_turn(name, send, recv, shards, lands, after):
    n = len(shards)
    afters = _as_list(after)

    def body(*refs):
        src, had = refs[:n], refs[n:2 * n]
        send_ref, recv_ref = refs[2 * n], refs[2 * n + 1]
        rsend, rrecv = refs[2 * n + 2 + len(afters)], refs[2 * n + 3 + len(afters)]
        land = refs[3 * n + 4 + len(afters):4 * n + 4 + len(afters)]
        x, y, c, chips = _place()
        for a in range(n):
            for j, (cx, cy) in enumerate(chips[:2]):
                cp = pltpu.make_async_remote_copy(src[a].at[c], had[a].at[2 * cx + cy, c], send_ref.at[2 * a + j],
                                                  recv_ref.at[2 * a + j], device_id=(cx, cy, c), device_id_type=MESH)
                cp.wait_send()
                cp.wait_recv()
        for a in range(n):
            for j, ((sent, _), (dst, _)) in enumerate(zip(_relay_blocks(had[a], c, chips), _relay_blocks(land[a], c, chips))):
                cx, cy = chips[j]
                for sp, dp in zip(_pieces(sent), _pieces(dst)):
                    pltpu.make_async_remote_copy(sp, dp, rsend.at[2 * a + j], rrecv.at[2 * a + j],
                                                 device_id=(cx, cy, c), device_id_type=MESH).start()

    outs = pl.pallas_call(
        body, name=name,
        out_shape=(pltpu.SemaphoreType.DMA((2 * n,)), pltpu.SemaphoreType.DMA((2 * n,)),
                   *[pltpu.HBM(s.shape, s.dtype) for s in shards], *[pltpu.HBM(l.shape, l.dtype) for l in lands]),
        in_specs=[HBM] * (2 * n) + [SEM, SEM] + [ANY] * len(afters), out_specs=(SEM, SEM, *([HBM] * (2 * n))),
        input_output_aliases={i: 2 + i for i in range(2 * n)},
        compiler_params=pltpu.CompilerParams(has_side_effects=EFFECT),
    )(*shards, *lands, send, recv, *afters)
    return outs[0], outs[1], list(outs[2:2 + n]), list(outs[2 + n:2 + 2 * n])


def relay_wait(name, send, recv, lands, after):
    n = len(lands)
    afters = _as_list(after)

    def body(*refs):
        land = refs[:n]
        send_ref, recv_ref = refs[n], refs[n + 1]
        x, y, c, chips = _place()
        for a in range(n):
            for j, (sent, got) in enumerate(_relay_blocks(land[a], c, chips)):
                cx, cy = chips[j]
                cp = pltpu.make_async_remote_copy(sent, got, send_ref.at[2 * a + j], recv_ref.at[2 * a + j],
                                                  device_id=(cx, cy, c), device_id_type=MESH)
                cp.wait_send()
                cp.wait_recv()

    outs = pl.pallas_call(
        body, name=name, out_shape=tuple(pltpu.HBM(l.shape, l.dtype) for l in lands),
        in_specs=[HBM] * n + [SEM, SEM] + [ANY] * len(afters), out_specs=[HBM] * n,
        input_output_aliases={i: i for i in range(n)},
        compiler_params=pltpu.CompilerParams(has_side_effects=EFFECT),
    )(*lands, send, recv, *afters)
    return list(outs)


def forward_halves(name, shards, lands):
    n = len(lands)

    def body(*refs):
        had, buf = refs[:n], refs[n:2 * n]
        send, recv = refs[2 * n:]
        x, y, c, chips = _place()
        sib = (x, y, 1 - c)
        for a in range(n):
            for j, (cx, cy) in enumerate(chips):
                for sp, dp in zip(_pieces(had[a].at[2 * cx + cy, c]), _pieces(buf[a].at[2 * cx + cy, c])):
                    pltpu.make_async_remote_copy(sp, dp, send.at[3 * a + j], recv.at[3 * a + j], device_id=sib, device_id_type=MESH).start()
        for a in range(n):
            for j, (cx, cy) in enumerate(chips):
                pltpu.make_async_remote_copy(had[a].at[2 * cx + cy, c], buf[a].at[2 * cx + cy, 1 - c], send.at[3 * a + j],
                                             recv.at[3 * a + j], device_id=sib, device_id_type=MESH).wait()

    got = pl.pallas_call(
        body, name=name, in_specs=[ANY] * n, out_specs=[ANY] * n, out_shape=[SDS(l.shape, l.dtype) for l in lands],
        input_output_aliases={i: i for i in range(n)},
        scratch_shapes=[pltpu.SemaphoreType.DMA((3 * n,)), pltpu.SemaphoreType.DMA((3 * n,))],
    )(*lands)
    me = 2 * lax.axis_index("x") + lax.axis_index("y")
    return [lax.dynamic_update_index_in_dim(g, s, me, 0) for g, s in zip(got, shards)]


def forward_turn(name, send, recv, lands, after):
    n = len(lands)
    afters = _as_list(after)

    def body(*refs):
        had = refs[:n]
        send_ref, recv_ref = refs[n], refs[n + 1]
        fsend, frecv = refs[n + 2 + len(afters)], refs[n + 3 + len(afters)]
        buf = refs[n + 4 + len(afters):2 * n + 4 + len(afters)]
        x, y, c, chips = _place()
        sib = (x, y, 1 - c)
        for a in range(n):
            for j, (sent, got) in enumerate(_relay_blocks(had[a], c, chips)):
                cx, cy = chips[j]
                cp = pltpu.make_async_remote_copy(sent, got, send_ref.at[2 * a + j], recv_ref.at[2 * a + j],
                                                  device_id=(cx, cy, c), device_id_type=MESH)
                cp.wait_send()
                cp.wait_recv()
        for a in range(n):
            for j, (cx, cy) in enumerate(chips):
                for sp, dp in zip(_pieces(had[a].at[2 * cx + cy, c]), _pieces(buf[a].at[2 * cx + cy, c])):
                    pltpu.make_async_remote_copy(sp, dp, fsend.at[3 * a + j], frecv.at[3 * a + j], device_id=sib, device_id_type=MESH).start()

    outs = pl.pallas_call(
        body, name=name,
        out_shape=(pltpu.SemaphoreType.DMA((3 * n,)), pltpu.SemaphoreType.DMA((3 * n,)), *[pltpu.HBM(l.shape, l.dtype) for l in lands]),
        in_specs=[HBM] * n + [SEM, SEM] + [ANY] * len(afters), out_specs=(SEM, SEM, *([HBM] * n)),
        input_output_aliases={i: 2 + i for i in range(n)},
        compiler_params=pltpu.CompilerParams(has_side_effects=EFFECT),
    )(*lands, send, recv, *afters)
    return outs[0], outs[1], list(outs[2:])


def forward_wait(name, send, recv, lands, after):
    n = len(lands)
    afters = _as_list(after)

    def body(*refs):
        land = refs[:n]
        send_ref, recv_ref = refs[n], refs[n + 1]
        x, y, c, chips = _place()
        sib = (x, y, 1 - c)
        for a in range(n):
            for j, (cx, cy) in enumerate(chips):
                cp = pltpu.make_async_remote_copy(land[a].at[2 * cx + cy, c], land[a].at[2 * cx + cy, 1 - c], send_ref.at[3 * a + j],
                                                  recv_ref.at[3 * a + j], device_id=sib, device_id_type=MESH)
                cp.wait_send()
                cp.wait_recv()

    outs = pl.pallas_call(
        body, name=name, out_shape=tuple(pltpu.HBM(l.shape, l.dtype) for l in lands),
        in_specs=[HBM] * n + [SEM, SEM] + [ANY] * len(afters), out_specs=[HBM] * n,
        input_output_aliases={i: i for i in range(n)},
        compiler_params=pltpu.CompilerParams(has_side_effects=EFFECT),
    )(*lands, send, recv, *afters)
    return list(outs)


def exchange_start(name, parts):
    n = len(parts)

    def body(*refs):
        src, got = refs[:n], refs[n:2 * n]
        send, recv = refs[2 * n], refs[2 * n + 1]
        token = refs[4 * n + 2]
        x, y, c, _ = _place()
        sib = (x, y, 1 - c)
        for a in range(n):
            for sp, dp in zip(_pieces(src[a].at[1 - c]), _pieces(got[a])):
                pltpu.make_async_remote_copy(sp, dp, send.at[a], recv.at[a], device_id=sib, device_id_type=MESH).start()
        token[...] = jnp.zeros_like(token)

    lands = [pltpu.with_memory_space_constraint(lax.empty(p.shape[1:], p.dtype), pltpu.HBM) for p in parts]
    srcs = [pltpu.with_memory_space_constraint(p, pltpu.HBM) for p in parts]
    outs = pl.pallas_call(
        body, name=name,
        out_shape=(pltpu.SemaphoreType.DMA((n,)), pltpu.SemaphoreType.DMA((n,)),
                   *[pltpu.HBM(p.shape, p.dtype) for p in parts], *[pltpu.HBM(l.shape, l.dtype) for l in lands],
                   SDS((8, 128), F32)),
        in_specs=[HBM] * (2 * n), out_specs=(SEM, SEM, *([HBM] * (2 * n)), pl.BlockSpec(memory_space=pltpu.VMEM)),
        input_output_aliases={i: 2 + i for i in range(2 * n)},
        compiler_params=pltpu.CompilerParams(has_side_effects=EFFECT),
    )(*srcs, *lands)
    return outs[0], outs[1], list(outs[2:2 + n]), list(outs[2 + n:2 + 2 * n]), outs[2 + 2 * n]


def exchange_wait(name, send, recv, parts, lands, after):
    n = len(parts)
    afters = _as_list(after)

    def body(*refs):
        src, got = refs[:n], refs[n:2 * n]
        send_ref, recv_ref = refs[2 * n], refs[2 * n + 1]
        x, y, c, _ = _place()
        sib = (x, y, 1 - c)
        for a in range(n):
            cp = pltpu.make_async_remote_copy(src[a].at[1 - c], got[a], send_ref.at[a], recv_ref.at[a], device_id=sib, device_id_type=MESH)
            cp.wait_send()
            cp.wait_recv()

    outs = pl.pallas_call(
        body, name=name,
        out_shape=(*[pltpu.HBM(p.shape, p.dtype) for p in parts], *[pltpu.HBM(l.shape, l.dtype) for l in lands]),
        in_specs=[HBM] * (2 * n) + [SEM, SEM] + [ANY] * len(afters), out_specs=[HBM] * (2 * n),
        input_output_aliases={i: i for i in range(2 * n)},
        compiler_params=pltpu.CompilerParams(has_side_effects=EFFECT),
    )(*parts, *lands, send, recv, *afters)
    return list(outs[:n]), list(outs[n:])


def scatter_start(name, parts):
    n = len(parts)

    def body(*refs):
        src, land = refs[:n], refs[n:2 * n]
        send, recv = refs[2 * n], refs[2 * n + 1]
        token = refs[4 * n + 2]
        x, y, c, chips = _place()
        for a in range(n):
            for j, (cx, cy) in enumerate(chips):
                for sp, dp in zip(_pieces(src[a].at[2 * cx + cy]), _pieces(land[a].at[j])):
                    pltpu.make_async_remote_copy(sp, dp, send.at[3 * a + j], recv.at[3 * a + j],
                                                 device_id=(cx, cy, c), device_id_type=MESH).start()
        token[...] = jnp.zeros_like(token)

    lands = [pltpu.with_memory_space_constraint(lax.empty((NCHIP - 1,) + p.shape[1:], p.dtype), pltpu.HBM) for p in parts]
    srcs = [pltpu.with_memory_space_constraint(p, pltpu.HBM) for p in parts]
    outs = pl.pallas_call(
        body, name=name,
        out_shape=(pltpu.SemaphoreType.DMA((3 * n,)), pltpu.SemaphoreType.DMA((3 * n,)),
                   *[pltpu.HBM(p.shape, p.dtype) for p in parts], *[pltpu.HBM(l.shape, l.dtype) for l in lands],
                   SDS((8, 128), F32)),
        in_specs=[HBM] * (2 * n), out_specs=(SEM, SEM, *([HBM] * (2 * n)), pl.BlockSpec(memory_space=pltpu.VMEM)),
        input_output_aliases={i: 2 + i for i in range(2 * n)},
        compiler_params=pltpu.CompilerParams(has_side_effects=EFFECT),
    )(*srcs, *lands)
    return outs[0], outs[1], list(outs[2:2 + n]), list(outs[2 + n:2 + 2 * n]), outs[2 + 2 * n]


def scatter_wait(name, send, recv, parts, lands, after):
    n = len(parts)
    afters = _as_list(after)

    def body(*refs):
        src, land = refs[:n], refs[n:2 * n]
        send_ref, recv_ref = refs[2 * n], refs[2 * n + 1]
        x, y, c, chips = _place()
        for a in range(n):
            for j, (cx, cy) in enumerate(chips):
                cp = pltpu.make_async_remote_copy(src[a].at[2 * cx + cy], land[a].at[j], send_ref.at[3 * a + j], recv_ref.at[3 * a + j],
                                                  device_id=(cx, cy, c), device_id_type=MESH)
                cp.wait_send()
                cp.wait_recv()

    outs = pl.pallas_call(
        body, name=name,
        out_shape=(*[pltpu.HBM(p.shape, p.dtype) for p in parts], *[pltpu.HBM(l.shape, l.dtype) for l in lands]),
        in_specs=[HBM] * (2 * n) + [SEM, SEM] + [ANY] * len(afters), out_specs=[HBM] * (2 * n),
        input_output_aliases={i: i for i in range(2 * n)},
        compiler_params=pltpu.CompilerParams(has_side_effects=EFFECT),
    )(*parts, *lands, send, recv, *afters)
    return list(outs[:n]), list(outs[n:])


def join_start(name, halves):
    n = len(halves)

    def body(*refs):
        src, dst = refs[:n], refs[n:2 * n]
        send, recv = refs[2 * n], refs[2 * n + 1]
        token = refs[4 * n + 2]
        x, y, c, _ = _place()
        sib = (x, y, 1 - c)
        for a in range(n):
            for sp, dp in zip(_pieces(src[a]), _pieces(dst[a])):
                pltpu.make_async_remote_copy(sp, dp, send.at[a], recv.at[a], device_id=sib, device_id_type=MESH).start()
        token[...] = jnp.zeros_like(token)

    lands = [pltpu.with_memory_space_constraint(lax.empty(h.shape, h.dtype), pltpu.HBM) for h in halves]
    srcs = [pltpu.with_memory_space_constraint(h, pltpu.HBM) for h in halves]
    outs = pl.pallas_call(
        body, name=name,
        out_shape=(pltpu.SemaphoreType.DMA((n,)), pltpu.SemaphoreType.DMA((n,)),
                   *[pltpu.HBM(h.shape, h.dtype) for h in halves], *[pltpu.HBM(l.shape, l.dtype) for l in lands],
                   SDS((8, 128), F32)),
        in_specs=[HBM] * (2 * n), out_specs=(SEM, SEM, *([HBM] * (2 * n)), pl.BlockSpec(memory_space=pltpu.VMEM)),
        input_output_aliases={i: 2 + i for i in range(2 * n)},
        compiler_params=pltpu.CompilerParams(has_side_effects=EFFECT),
    )(*srcs, *lands)
    return outs[0], outs[1], list(outs[2:2 + n]), list(outs[2 + n:2 + 2 * n]), outs[2 + 2 * n]


def join_wait(name, send, recv, halves, lands, after):
    n = len(halves)
    afters = _as_list(after)

    def body(*refs):
        src, dst = refs[:n], refs[n:2 * n]
        send_ref, recv_ref = refs[2 * n], refs[2 * n + 1]
        x, y, c, _ = _place()
        sib = (x, y, 1 - c)
        for a in range(n):
            cp = pltpu.make_async_remote_copy(src[a], dst[a], send_ref.at[a], recv_ref.at[a], device_id=sib, device_id_type=MESH)
            cp.wait_send()
            cp.wait_recv()

    outs = pl.pallas_call(
        body, name=name,
        out_shape=(*[pltpu.HBM(h.shape, h.dtype) for h in halves], *[pltpu.HBM(l.shape, l.dtype) for l in lands]),
        in_specs=[HBM] * (2 * n) + [SEM, SEM] + [ANY] * len(afters), out_specs=[HBM] * (2 * n),
        input_output_aliases={i: i for i in range(2 * n)},
        compiler_params=pltpu.CompilerParams(has_side_effects=EFFECT),
    )(*halves, *lands, send, recv, *afters)
    return list(outs[:n]), list(outs[n:])


def gather_small(name, xs, reduce, after=None):
    m, ncol = xs.shape
    afters = _as_list(after)

    def body(x_ref, *rest):
        out_ref, all_ref, send, recv, lsem = rest[len(afters):]
        x, y, c, chips = _place()
        me, sib = (x, y, c), (x, y, 1 - c)

        def rows(px, py, pc):
            return all_ref.at[pl.ds((4 * px + 2 * py + pc) * m, m), :]

        def copy(k, block, to, src=None):
            return pltpu.make_async_remote_copy(rows(*block) if src is None else src, rows(*block), send.at[k], recv.at[k],
                                                device_id=to, device_id_type=MESH)

        mine = pltpu.make_async_copy(x_ref, rows(*me), lsem)
        mine.start()
        first = [copy(0, me, sib, src=x_ref)] + [copy(1 + j, me, (*chip, c), src=x_ref) for j, chip in enumerate(chips)]
        for cp in first:
            cp.start()
        passed = [copy(4 + j, (*chip, c), sib) for j, chip in enumerate(chips)]
        for j, chip in enumerate(chips):
            copy(1 + j, (*chip, c), me).wait_recv()
            passed[j].start()
        copy(0, sib, me).wait_recv()
        for j, chip in enumerate(chips):
            copy(4 + j, (*chip, 1 - c), me).wait_recv()
        for cp in first + passed:
            cp.wait_send()
        mine.wait()
        if reduce:
            s = all_ref[0:m, :]
            for dev in range(1, 8):
                s = s + all_ref[dev * m:(dev + 1) * m, :]
            out_ref[...] = s
        else:
            out_ref[...] = all_ref[...]

    vm = pl.BlockSpec(memory_space=pltpu.VMEM)
    return pl.pallas_call(
        body, name=name, in_specs=[vm] + [ANY] * len(afters), out_specs=vm,
        out_shape=SDS((m, ncol) if reduce else (8 * m, ncol), F32),
        scratch_shapes=[pltpu.VMEM((8 * m, ncol), F32), pltpu.SemaphoreType.DMA((7,)), pltpu.SemaphoreType.DMA((7,)),
                        pltpu.SemaphoreType.DMA],
    )(xs, *afters)


RELAYOUT_ROWS = 128


def weights_to_cat(name, land, own, place, other, prev=None, after=None):
    tm = RELAYOUT_ROWS
    nb = (D // 2) // tm
    extra = ([] if prev is None else [prev]) + _as_list(after)

    def half(p):
        return 1 - p[0] if other else p[0]

    def body(p_ref, g_ref, own_ref, *rest):
        o_ref = rest[len(extra)]
        nat = jnp.concatenate([jnp.where(p_ref[1] == j, own_ref[...], g_ref[j]) for j in range(NCHIP)], axis=1)
        pad = jnp.zeros((tm, NCAT - OA - 16), BF16)
        o_ref[...] = jnp.concatenate([nat[:, 3072:7168], nat[:, 7184:11280], nat[:, 0:3072], nat[:, 7168:7184], pad], axis=1)

    grid_spec = pltpu.PrefetchScalarGridSpec(
        num_scalar_prefetch=1, grid=(nb,),
        in_specs=[pl.BlockSpec((NCHIP, None, tm, IN_SHARD), lambda i, p: (0, half(p), i, 0)),
                  pl.BlockSpec((None, tm, IN_SHARD), lambda i, p: (half(p), i, 0))] + [ANY] * len(extra),
        out_specs=pl.BlockSpec((tm, NCAT), lambda i, p: (half(p) * nb + i, 0)))
    return pl.pallas_call(
        body, name=name, grid_spec=grid_spec, out_shape=SDS((D, NCAT), BF16),
        input_output_aliases={} if prev is None else {3: 0},
        compiler_params=_cparams(40 * 1024 * 1024, ("arbitrary",)),
    )(place, land, own, *extra)


def grads_from_cat(gw_cat):
    tm = RELAYOUT_ROWS
    nb = (D // 2) // tm

    def body(c_ref, o_ref):
        cat = c_ref[...]
        nat = jnp.concatenate([cat[:, OU:OA], cat[:, OV:OGP], cat[:, OA:OA + 16], cat[:, OGP:OU]], axis=1)
        for j in range(NCHIP):
            o_ref[j] = nat[:, j * IN_SHARD:(j + 1) * IN_SHARD]

    return pl.pallas_call(
        body, name="grads_from_cat", grid=(D // tm,), in_specs=[pl.BlockSpec((tm, NCAT), lambda i: (i, 0))],
        out_specs=pl.BlockSpec((None, NCHIP, tm, IN_SHARD), lambda i: (i // nb, 0, i % nb, 0)),
        out_shape=SDS((2, NCHIP, D // 2, IN_SHARD), BF16), compiler_params=_cparams(40 * 1024 * 1024, ("arbitrary",)),
    )(gw_cat)


def _pad_rows(a, rows):
    return jnp.concatenate([a, jnp.zeros((rows - a.shape[0],) + a.shape[1:], a.dtype)], axis=0)


def local_step(x2d, tgt, gf, g1, pool_scale, wa_pad, b_alpha, ng, g2, get_w, on_grad=None, on_settle=None, tick=None):
    emit = on_grad if on_grad is not None else (lambda group, grads: None)
    settle = on_settle if on_settle is not None else (lambda group, after: None)
    h1 = norm1(x2d, g1)
    wcat, pw = get_w("in", h1)
    pcat = mm_in(h1, wcat)
    dpool, ylin = pool_fwd(pcat, pw)
    pinned = tick("pool", ylin) if tick is not None else None
    og, o, states = gla_fwd(pcat, wa_pad, b_alpha, ng, pinned)
    w_go, w_o = get_w("mid", og)
    mixed, ygla = mm_gla_out(og, w_go, ylin, pcat, pool_scale)
    x2, h2 = mm_out(mixed, w_o, x2d, g2)
    w_up = get_w("up", h2)
    rup, act = mm_up(h2, w_up)
    w_dn = get_w("down", act)
    dx3, dx3b, g_nf, loss_row = mm_down(act, w_dn, x2, tgt, gf)

    gw_down = mm_wgrad("mm_dw_down", act, dx3b, DFF, D, (2, NCHIP, D // 2, D), (None, None, D // 2, D),
                       lambda j, i, k: (i % 2, i // 2, 0, 0), D // 2, D)
    token = emit("down", {"down": gw_down})
    dup = mm_dact(dx3b, w_dn, rup, after=token)
    token = settle("down", dup)
    dx2, dx2b, g_mlp = mm_dh2(dup, w_up, x2, dx3, g2, after=token)
    gw_up = mm_wgrad("mm_dw_up", h2, dup, D, DFF, (2, NCHIP, D // 2, D), (None, None, D // 2, D),
                     lambda j, i, k: (i, j, 0, 0), D // 2, D)
    token = emit("up", {"up": gw_up})
    dylin, dygla, dlgp, dlgg, g_ps = mm_dmixed(dx2b, w_o, pcat, ylin, ygla, pool_scale, after=token)
    token = settle("up", dylin)
    gw_out = mm_wgrad("mm_dw_out", mixed, dx2b, D, D, (2, NCHIP, 256, D), (2, None, 256, D),
                      lambda j, i, k: (0, i, 0, 0), 512, D)
    do, dg, g_ng = mm_dog(dygla, w_go, o, pcat, ng, after=token)
    gw_go = mm_wgrad("mm_dw_gla_out", og, dygla, D, D, (2, NCHIP, 256, D), (2, None, 256, D),
                     lambda j, i, k: (0, i, 0, 0), 512, D)
    token = emit("mix", {"out": gw_out, "gla_out": gw_go})
    dq, dk, dv, dalow, g_wa, g_ba = gla_bwd(do, pcat, states, wa_pad, b_alpha, b_alpha if token is None else token)
    token = settle("mix", dq)
    du, dpw = pool_bwd(dylin, dpool, pw)
    dpcat = jnp.concatenate([dv, dg, dlgp, dlgg, du, dq, dk, dalow, jnp.zeros((T, NCAT - OA - APAD), BF16)], axis=1)
    gw_cat = mm_wgrad("mm_dw_in", h1, dpcat, D, NCAT, (D, NCAT), (1024, 1280), lambda j, i, k: (i, j), 1024, 1280, after=token)
    token = settle("in", emit("in", {"in_cat": gw_cat, "pool": dpw}))
    grad_x, g_mix = mm_dh1(dpcat, wcat, x2d, dx2, g1, after=token)
    return (loss_row[0, 0], grad_x, g_mix, g_ps, g_mlp, g_nf, g_ng, g_ba, g_wa, token,
            gw_cat, dpw, gw_go, gw_out, gw_up, gw_down)


def kernel(x, norm_mix_g, w_in, pool_w, pool_scale, w_alpha, b_alpha, gla_norm_g, w_gla_out, w_out, norm_mlp_g, w_mlp_up, w_mlp_down, norm_final_g, loss_target, m_norm_mix_g, m_w_in, m_pool_w, m_pool_scale, m_w_alpha, m_b_alpha, m_gla_norm_g, m_w_gla_out, m_w_out, m_norm_mlp_g, m_w_mlp_up, m_w_mlp_down, m_norm_final_g, v_norm_mix_g, v_w_in, v_pool_w, v_pool_scale, v_w_alpha, v_b_alpha, v_gla_norm_g, v_w_gla_out, v_w_out, v_norm_mlp_g, v_w_mlp_up, v_w_mlp_down, v_norm_final_g):
    chip = 2 * lax.axis_index("x") + lax.axis_index("y")
    chip_i = chip.astype(jnp.int32).reshape(1)
    core_i = lax.axis_index("c").astype(jnp.int32).reshape(1)
    place_i = jnp.concatenate([core_i, chip_i])
    tgt = loss_target.reshape(T, D)
    gf = norm_final_g.reshape(1, D)

    def halves(w2d):
        r, c = w2d.shape
        return w2d.astype(BF16).reshape(2, r // 2, c)

    pool_shard = pool_w.reshape(4 * PG, PO // NCHIP)
    w_in_r = w_in.reshape(2, D // 2, IN_SHARD)
    sent = {"in": [cast_bf16("cast_w_in", w_in_r), halves(pool_shard)]}
    flight = {}

    def start(group, after=None):
        flight[group] = gather_start("gather_start_" + group, sent[group], after)

    def relay(group, after):
        send, recv, shards, lands = flight[group]
        flight[group] = relay_turn("relay_turn_" + group, send, recv, shards, lands, after)

    def fetch(group, after):
        send, recv, shards, lands = flight[group]
        lands = relay_wait("relay_wait_" + group, send, recv, lands, after)
        return forward_halves("forward_" + group, shards, lands)

    start("in")
    m_in_f, v_in_f, w_go_f, w_o_f, w_up_f, w_dn_f, x_f, wal_f, gng_f = lax.optimization_barrier(
        (m_w_in, v_w_in, w_gla_out, w_out, w_mlp_up, w_mlp_down, x, w_alpha, gla_norm_g, flight["in"][2][0]))[:9]
    m_in_r, v_in_r = m_in_f.reshape(2, D // 2, IN_SHARD), v_in_f.reshape(2, D // 2, IN_SHARD)
    sent["mid"] = [halves(w_go_f[0]), halves(w_o_f[0])]
    relay("in", [m_in_r, v_in_r, *sent["mid"]])
    w_up_f, w_dn_f, x_f, wal_f, gng_f = lax.optimization_barrier(
        (w_up_f, w_dn_f, x_f, wal_f, gng_f, flight["in"][3][0]))[:5]
    sent["up"], sent["down"] = [halves(w_up_f[0])], [halves(w_dn_f[0])]
    x2d = x_f.reshape(T, D)
    big = [w_in_r, w_go_f[0], w_o_f[0], w_up_f[0], w_dn_f[0], pool_shard]

    def tick(point, after):
        if point == "pool":
            relay("mid", after)
            relay("up", flight["mid"][3][0])
            start("down", flight["up"][3][0])
            return [flight["up"][3][0], flight["down"][3][0]]

    def get_w(group, after):
        if group == "in":
            after = [after, *sent["up"], *sent["down"], wa_pad]
        if group == "mid":
            send, recv, shards, lands = flight["up"]
            flight["up"] = (*forward_turn("forward_turn_up", send, recv, lands, after), shards)
            after = flight["up"][2][0]
        if group == "up":
            relay("down", after)
            send, recv, lands, shards = flight["up"]
            lands = forward_wait("forward_wait_up", send, recv, lands, flight["down"][3][0])
            return lax.dynamic_update_index_in_dim(lands[0], shards[0], chip, 0).reshape(NCHIP, D, D)
        if group == "in":
            send, recv, shards, lands = flight["in"]
            send, recv, lands = forward_turn("forward_turn_in", send, recv, lands, after)
            start("mid", lands[0])
            start("up", flight["mid"][3][0])
            wcat = weights_to_cat("weights_to_cat_mine", lands[0], shards[0], place_i, False, after=flight["up"][3][0])
            lands = forward_wait("forward_wait_in", send, recv, lands, wcat)
            wcat = weights_to_cat("weights_to_cat_sibling", lands[0], shards[0], place_i, True, prev=wcat)
            g_pool = lax.dynamic_update_index_in_dim(lands[1], shards[1], chip, 0)
            pw = jnp.concatenate([g_pool[j].reshape(4, PG, PO // NCHIP) for j in range(NCHIP)], axis=2)
            return wcat, pw
        whole = fetch(group, after)
        if group == "mid":
            return whole[0].reshape(D, D), whole[1].reshape(D, D)
        return whole[0].reshape(DFF, D)

    small_w = pack_rows("pack_small_w", [wal_f[0].reshape(4, QK),
                                         jnp.concatenate([gng_f[0].reshape(1, 512), jnp.zeros((1, 512), F32)], axis=1)], 8)
    sw_all = gather_small("gather_small_w", small_w, False).reshape(8, 8, QK)
    wa_full = jnp.concatenate([sw_all[2 * j, 0:4].reshape(16, DK) for j in range(NCHIP)], axis=1)
    ng_full = jnp.concatenate([sw_all[2 * j, 4, 0:512].reshape(HEADS, DV // NCHIP) for j in range(NCHIP)], axis=1)
    wa_pad = _pad_rows(wa_full, APAD).astype(BF16)
    ng = ng_full.reshape(1, D)

    pending = {}
    wmv = {"in": (w_in_r, m_in_r, v_in_r), "gla_out": (big[1], m_w_gla_out, v_w_gla_out), "out": (big[2], m_w_out, v_w_out),
           "up": (big[3], m_w_mlp_up, v_w_mlp_up), "down": (big[4], m_w_mlp_down, v_w_mlp_down), "pool": (big[5], m_pool_w, v_pool_w)}
    big_res = {}

    def reduce_group(group, after):
        nms, send, recv, sums, lands = pending[group]
        sums, lands = scatter_wait("scatter_wait_" + group, send, recv, sums, lands, after)
        reduced = [sum_chips("sum_chips_" + nm, a, b, chip_i) for nm, a, b in zip(nms, sums, lands)]
        send, recv, reduced, lands, token = join_start("join_start_" + group, reduced)
        pending[group] = (nms, send, recv, reduced, lands)
        return token

    def update_group(group, after):
        nms, send, recv, reduced, lands = pending[group]
        reduced, from_sib = join_wait("join_wait_" + group, send, recv, reduced, lands, after)
        for nm, g_own, g_sib in zip(nms, reduced, from_sib):
            w, m, v = wmv[nm]
            shp = (2,) + g_own.shape
            big_res[nm] = adamw_halves("adamw_" + nm, w.reshape(shp), g_own, g_sib, m.reshape(shp), v.reshape(shp), core_i)

    def on_grad(group, grads):
        if group == "in":
            gw_in = grads_from_cat(grads["in_cat"])
            gw_pool = jnp.stack([grads["pool"][:, :, j * 128:(j + 1) * 128].reshape(2, 2 * PG, 128)
                                 for j in range(NCHIP)], axis=1)
            grads = {"in": gw_in, "pool": gw_pool}
        nms, parts = list(grads.keys()), list(grads.values())
        send, recv, parts, got, token = exchange_start("exchange_start_" + group, parts)
        pending[group] = (nms, send, recv, parts, got)
        return token

    def on_settle(group, after):
        if group == "in":
            after = reduce_group("down", after)
        nms, send, recv, parts, got = pending[group]
        parts, got = exchange_wait("exchange_wait_" + group, send, recv, parts, got, after)
        sums = [add_pairs("add_pair_" + nm, a, b, core_i) for nm, a, b in zip(nms, parts, got)]
        send, recv, sums, lands, token = scatter_start("scatter_start_" + group, sums)
        pending[group] = (nms, send, recv, sums, lands)
        if group != "in":
            return token
        token = reduce_group("up", token)
        token = reduce_group("mix", token)
        for earlier in ("down", "up", "mix"):
            update_group(earlier, token)
            token = big_res[pending[earlier][0][-1]][1]
        return [big_res[nm][1] for nm in ("down", "up", "out", "gla_out")]

    (loss_local, grad_x, g_mix, g_ps, g_mlp, g_nf, g_ng, g_ba, g_wa) = local_step(
        x2d, tgt, gf, norm_mix_g, pool_scale, wa_pad, b_alpha, ng, norm_mlp_g, get_w, on_grad, on_settle, tick)[:9]
    loss = lax.psum(loss_local, ("x", "y", "c"))
    join_in_token = reduce_group("in", grad_x)

    ROWS = 16

    def wide(a, n):
        return jnp.concatenate([a.reshape(1, n), jnp.zeros((1, D - n), F32)], axis=1)

    packed = pack_rows("pack_small_g", [g_mix, g_ps, g_mlp, g_nf, g_ng, wide(g_ba, QK), g_wa[0:16].reshape(8, D)], ROWS)
    tot = gather_small("reduce_small_g", packed, True, join_in_token)
    t_wa = lax.dynamic_slice(tot[6:14].reshape(16, QK), (0, chip * DK), (16, DK))
    t_ng = lax.dynamic_slice(tot[4].reshape(HEADS, DV), (0, chip * (DV // NCHIP)), (HEADS, DV // NCHIP))

    def pack_small(nm, mix, ps, mlp, nf, ba, wa, gn, after=None):
        return pack_rows(nm, [mix.reshape(1, D), ps.reshape(1, D), mlp.reshape(1, D), nf.reshape(1, D), wide(ba, QK),
                              wa.reshape(2, D), wide(gn, 512)], ROWS, after)

    update_group("in", tot)
    sg = pack_small("pack_g", tot[0], tot[1], tot[2], tot[3], tot[5, 0:QK], t_wa, t_ng, big_res["in"][3])
    sw = pack_small("pack_w", norm_mix_g, pool_scale, norm_mlp_g, norm_final_g, b_alpha, w_alpha, gla_norm_g)
    sm = pack_small("pack_m", m_norm_mix_g, m_pool_scale, m_norm_mlp_g, m_norm_final_g, m_b_alpha, m_w_alpha, m_gla_norm_g)
    sv = pack_small("pack_v", v_norm_mix_g, v_pool_scale, v_norm_mlp_g, v_norm_final_g, v_b_alpha, v_w_alpha, v_gla_norm_g)
    small_res = adamw("adamw_small", sw, sg, sm, sv)

    def unpack(p):
        return {"norm_mix_g": p[0].reshape(1, D), "pool_scale": p[1].reshape(1, D), "norm_mlp_g": p[2].reshape(1, D),
                "norm_final_g": p[3].reshape(D), "b_alpha": p[4, 0:QK].reshape(1, QK), "w_alpha": p[5:7].reshape(1, 16, DK),
                "gla_norm_g": p[7, 0:512].reshape(1, HEADS, DV // NCHIP)}

    order = ["norm_mix_g", "w_in", "pool_w", "pool_scale", "w_alpha", "b_alpha", "gla_norm_g", "w_gla_out", "w_out",
             "norm_mlp_g", "w_mlp_up", "w_mlp_down", "norm_final_g"]
    big_key = {"w_in": ("in", w_in.shape), "pool_w": ("pool", pool_w.shape), "w_gla_out": ("gla_out", w_gla_out.shape),
               "w_out": ("out", w_out.shape), "w_mlp_up": ("up", w_mlp_up.shape), "w_mlp_down": ("down", w_mlp_down.shape)}
    result = [loss, grad_x.reshape(1, T, D)]
    for kind in range(4):
        small = unpack(small_res[kind])
        for nm in order:
            if nm in big_key:
                key, shp = big_key[nm]
                result.append(big_res[key][kind].reshape(shp))
            else:
                result.append(small[nm])
    return tuple(result)
```

```python
import itertools

import jax
import jax.numpy as jnp
from jax import lax
from jax.experimental import pallas as pl
from jax.experimental.pallas import tpu as pltpu

F32 = jnp.float32
BF16 = jnp.bfloat16
SDS = jax.ShapeDtypeStruct
MESH = pl.DeviceIdType.MESH
ANY = pl.BlockSpec(memory_space=pl.ANY)

T = 2048
D = 2048
DFF = 8192
NCHIP = 4
IN_WIDTH = 11280
IN_SHARD = IN_WIDTH // NCHIP
CHUNK = 64
NCHUNK = T // CHUNK
HEADS = 4
DK = 256
DV = 512
QK = HEADS * DK
EPS = 1e-6
POOL_WINDOWS = (2, 4, 8, 16)
PG = 256
PO = 512

OV, OG, OGP, OGG, OU, OQ, OKK, OA = 0, 2048, 4096, 6144, 8192, 9216, 10240, 11264
NCAT = 11520
APAD = 128

VMEM_CAP = 56 * 1024 * 1024

PIECE_BYTES = 384 * 1024

ADAM_LR, ADAM_B1, ADAM_B2, ADAM_EPS, ADAM_WD, ADAM_STEP = 0.001, 0.9, 0.999, 1e-08, 0.01, 10


def _cparams(vmem_bytes=None, sem=None):
    kw = {}
    if vmem_bytes is not None:
        kw["vmem_limit_bytes"] = int(min(max(vmem_bytes, 32 * 1024 * 1024), VMEM_CAP))
    if sem is not None:
        kw["dimension_semantics"] = sem
    return pltpu.CompilerParams(**kw)


def _nbytes(shape, dtype):
    n = 1
    for s in shape:
        if s is not None:
            n *= s
    return n * jnp.dtype(dtype).itemsize


def _sigmoid(x):
    return 0.5 * jnp.tanh(0.5 * x) + 0.5


EPI_COLS = 512


def _as_list(after):
    if after is None:
        return []
    return list(after) if isinstance(after, (list, tuple)) else [after]


def matmul(name, a, b, *, a_spec, b_spec, cdims, grid, acc_shape, outs, extras=(), epi, after=None):
    nj, ni, nk = grid
    ne, no = len(extras), len(outs)
    afters = _as_list(after)
    first_out = 2 + ne + len(afters)

    def body(*refs):
        a_ref, b_ref = refs[0], refs[1]
        ex = refs[2:2 + ne]
        out_refs = refs[first_out:first_out + no]
        i = pl.program_id(1)
        part = lax.dot_general(a_ref[...], b_ref[...], (cdims, ((), ())), preferred_element_type=F32)
        if nk == 1:
            epi(part, ex, out_refs, i)
        else:
            acc_ref = refs[first_out + no]
            k = pl.program_id(2)

            @pl.when(k == 0)
            def _():
                acc_ref[...] = part

            @pl.when(k > 0)
            def _():
                acc_ref[...] += part

            @pl.when(k == nk - 1)
            def _():
                epi(acc_ref[...], ex, out_refs, i)

    in_specs = [pl.BlockSpec(*a_spec), pl.BlockSpec(*b_spec)] + [pl.BlockSpec(bs, im) for _, bs, im in extras]
    in_specs += [ANY] * len(afters)
    out_specs = [pl.BlockSpec(bs, im) for _, _, bs, im in outs]
    out_shape = [SDS(s, dt) for s, dt, _, _ in outs]
    vm = 2 * (_nbytes(a_spec[0], a.dtype) + _nbytes(b_spec[0], b.dtype))
    vm += 2 * sum(_nbytes(bs, arr.dtype) for arr, bs, _ in extras)
    vm += 2 * sum(_nbytes(bs, dt) for _, dt, bs, _ in outs)
    vm += 6 * _nbytes(acc_shape, F32)
    scratch = [pltpu.VMEM(acc_shape, F32)] if nk > 1 else []
    return pl.pallas_call(
        body, name=name, grid=grid, in_specs=in_specs, out_specs=out_specs, out_shape=out_shape,
        scratch_shapes=scratch,
        compiler_params=_cparams(vm, ("arbitrary", "arbitrary", "arbitrary")),
    )(a, b, *[arr for arr, _, _ in extras], *afters)


NN =((1,), (0,))
NT = ((1,), (1,))
TN = ((0,), (0,))


def _row_acc(out_ref, val, i):
    @pl.when(i == 0)
    def _():
        out_ref[...] = val

    @pl.when(i > 0)
    def _():
        out_ref[...] += val


def _rms_bwd(xn, r, dxn):
    return r * (dxn - xn * jnp.mean(dxn * xn, axis=-1, keepdims=True))


def norm1(x, g):
    tm = 256

    def body(x_ref, g_ref, h_ref):
        xv = x_ref[...]
        r = lax.rsqrt(jnp.mean(xv * xv, axis=-1, keepdims=True) + EPS)
        h_ref[...] = (xv * r * g_ref[...]).astype(BF16)

    return pl.pallas_call(
        body, name="norm1", grid=(T // tm,),
        in_specs=[pl.BlockSpec((tm, D), lambda i: (i, 0)), pl.BlockSpec((1, D), lambda i: (0, 0))],
        out_specs=pl.BlockSpec((tm, D), lambda i: (i, 0)), out_shape=SDS((T, D), BF16),
        compiler_params=_cparams(32 * 1024 * 1024, ("arbitrary",)),
    )(x, g)


def mm_in(h1, wcat):
    tm, tn = 1024, 1280

    def epi(acc, ex, outs, i):
        outs[0][...] = acc.astype(BF16)

    return matmul("mm_in", h1, wcat, a_spec=((tm, D), lambda j, i, k: (i, 0)), b_spec=((D, tn), lambda j, i, k: (0, j)),
                  cdims=NN, grid=(NCAT // tn, T // tm, 1), acc_shape=(tm, tn),
                  outs=[((T, NCAT), BF16, (tm, tn), lambda j, i, k: (i, j))], epi=epi)[0]


def _window_sum(x, w, up):
    n = x.shape[0]
    row = lax.broadcasted_iota(jnp.int32, x.shape, 0)
    s, sh = x, 1
    while sh < w:
        if up:
            s = s + jnp.where(row < n - sh, pltpu.roll(s, n - sh, axis=0), 0.0)
        else:
            s = s + jnp.where(row >= sh, pltpu.roll(s, sh, axis=0), 0.0)
        sh *= 2
    return s


def _inv_count(shape, w):
    row = lax.broadcasted_iota(jnp.int32, shape, 0)
    return 1.0 / jnp.minimum(row + 1, w).astype(F32)


def pool_fwd(pcat, pw):
    def body(u_ref, pw_ref, d_ref, y_ref):
        for gi, w in enumerate(POOL_WINDOWS):
            ug = u_ref[:, gi * PG:(gi + 1) * PG].astype(F32)
            dg = _window_sum(ug, w, False) * _inv_count(ug.shape, w) - ug
            db = dg.astype(BF16)
            d_ref[:, gi * PG:(gi + 1) * PG] = db
            y_ref[:, gi * PO:(gi + 1) * PO] = jnp.dot(db, pw_ref[gi], preferred_element_type=F32).astype(BF16)

    return pl.pallas_call(
        body, name="pool_fwd", grid=(1,),
        in_specs=[pl.BlockSpec((T, 4 * PG), lambda i: (0, OU // (4 * PG))), pl.BlockSpec((4, PG, PO), lambda i: (0, 0, 0))],
        out_specs=[pl.BlockSpec((T, 4 * PG), lambda i: (0, 0)), pl.BlockSpec((T, D), lambda i: (0, 0))],
        out_shape=[SDS((T, 4 * PG), BF16), SDS((T, D), BF16)],
        compiler_params=_cparams(48 * 1024 * 1024, ("arbitrary",)),
    )(pcat, pw)


def pool_bwd(dylin, d, pw):
    def body(dy_ref, d_ref, pw_ref, du_ref, dpw_ref):
        for gi, w in enumerate(POOL_WINDOWS):
            dyl = dy_ref[:, gi * PO:(gi + 1) * PO]
            dd = lax.dot_general(dyl, pw_ref[gi], (NT, ((), ())), preferred_element_type=F32)
            du = _window_sum(dd * _inv_count(dd.shape, w), w, True) - dd
            du_ref[:, gi * PG:(gi + 1) * PG] = du.astype(BF16)
            dpw_ref[gi] = lax.dot_general(d_ref[:, gi * PG:(gi + 1) * PG], dyl, (TN, ((), ())),
                                          preferred_element_type=F32).astype(BF16)

    return pl.pallas_call(
        body, name="pool_bwd", grid=(1,),
        in_specs=[pl.BlockSpec((T, D), lambda i: (0, 0)), pl.BlockSpec((T, 4 * PG), lambda i: (0, 0)),
                  pl.BlockSpec((4, PG, PO), lambda i: (0, 0, 0))],
        out_specs=[pl.BlockSpec((T, 4 * PG), lambda i: (0, 0)), pl.BlockSpec((4, PG, PO), lambda i: (0, 0, 0))],
        out_shape=[SDS((T, 4 * PG), BF16), SDS((4, PG, PO), BF16)],
        compiler_params=_cparams(48 * 1024 * 1024, ("arbitrary",)),
    )(dylin, d, pw)


def _gate_decay(alow, wa, ba):
    a = jnp.dot(alow, wa, preferred_element_type=F32) + ba
    ls = jax.nn.log_sigmoid(a) * (1.0 / 16.0)
    r = lax.broadcasted_iota(jnp.int32, (CHUNK, CHUNK), 0)
    c = lax.broadcasted_iota(jnp.int32, (CHUNK, CHUNK), 1)
    tri = jnp.where(c <= r, 1.0, 0.0).astype(F32)
    cum = jnp.dot(tri, ls, preferred_element_type=F32, precision=lax.Precision.HIGHEST)
    last = cum[CHUNK - 1:CHUNK, :]
    return a, jnp.exp(last - cum), jnp.exp(last)


def gla_fwd(pcat, wa, ba, ng, after=None):
    afters = _as_list(after)

    def body(q_ref, k_ref, v_ref, g_ref, al_ref, wa_ref, ba_ref, ng_ref, *rest):
        og_ref, o_ref, st_ref, s_scr = rest[len(afters):]

        @pl.when(pl.program_id(0) == 0)
        def _():
            s_scr[...] = jnp.zeros_like(s_scr)

        _, e, decay = _gate_decay(al_ref[...], wa_ref[...], ba_ref[...])
        kd = (k_ref[...].astype(F32) * e).astype(BF16)
        qs = (q_ref[...].astype(F32) * (DK ** -0.5)).astype(BF16)
        for h in range(HEADS):
            ck = slice(h * DK, (h + 1) * DK)
            cv = slice(h * DV, (h + 1) * DV)
            s_new = s_scr[h] * decay[:, ck] + lax.dot_general(v_ref[:, cv], kd[:, ck], (TN, ((), ())),
                                                               preferred_element_type=F32)
            s_scr[h] = s_new
            sb = s_new.astype(BF16)
            st_ref[h] = sb
            oh = lax.dot_general(qs[:, ck], sb, (NT, ((), ())), preferred_element_type=F32)
            o_ref[:, cv] = oh.astype(BF16)
            on = oh * lax.rsqrt(jnp.mean(oh * oh, axis=-1, keepdims=True) + EPS) * ng_ref[:, cv]
            gv = g_ref[:, cv].astype(F32)
            og_ref[:, cv] = (on * (gv * _sigmoid(gv))).astype(BF16)

    row = lambda c: (c, 0)
    return pl.pallas_call(
        body, name="gla_fwd", grid=(NCHUNK,),
        in_specs=[pl.BlockSpec((CHUNK, QK), lambda c: (c, OQ // QK)), pl.BlockSpec((CHUNK, QK), lambda c: (c, OKK // QK)),
                  pl.BlockSpec((CHUNK, D), lambda c: (c, OV // D)), pl.BlockSpec((CHUNK, D), lambda c: (c, OG // D)),
                  pl.BlockSpec((CHUNK, APAD), lambda c: (c, OA // APAD)),
                  pl.BlockSpec((APAD, QK), lambda c: (0, 0)), pl.BlockSpec((1, QK), lambda c: (0, 0)),
                  pl.BlockSpec((1, D), lambda c: (0, 0))] + [ANY] * len(afters),
        out_specs=[pl.BlockSpec((CHUNK, D), row), pl.BlockSpec((CHUNK, D), row),
                   pl.BlockSpec((None, HEADS, DV, DK), lambda c: (c, 0, 0, 0))],
        out_shape=[SDS((T, D), BF16), SDS((T, D), BF16), SDS((NCHUNK, HEADS, DV, DK), BF16)],
        scratch_shapes=[pltpu.VMEM((HEADS, DV, DK), F32)],
        compiler_params=_cparams(32 * 1024 * 1024, ("arbitrary",)),
    )(pcat, pcat, pcat, pcat, pcat, wa, ba, ng, *afters)


def gla_bwd(do, pcat, states, wa, ba, after):
    def body(do_ref, q_ref, k_ref, v_ref, al_ref, sc_ref, sp_ref, wa_ref, ba_ref, after_ref,
             dq_ref, dk_ref, dv_ref, dal_ref, dwa_ref, dba_ref, ds_scr):
        i = pl.program_id(0)

        @pl.when(i == 0)
        def _():
            ds_scr[...] = jnp.zeros_like(ds_scr)

        has_prev = jnp.where(i < NCHUNK - 1, 1.0, 0.0).astype(F32)
        a, e, decay = _gate_decay(al_ref[...], wa_ref[...], ba_ref[...])
        kf = k_ref[...].astype(F32)
        kdf = kf * e
        kd = kdf.astype(BF16)
        qs = (q_ref[...].astype(F32) * (DK ** -0.5)).astype(BF16)
        dkd_parts, ddecay_parts = [], []
        for h in range(HEADS):
            ck = slice(h * DK, (h + 1) * DK)
            cv = slice(h * DV, (h + 1) * DV)
            doh = do_ref[:, cv]
            ds = ds_scr[h] + lax.dot_general(doh, qs[:, ck], (TN, ((), ())), preferred_element_type=F32)
            dsb = ds.astype(BF16)
            dq_ref[:, ck] = (jnp.dot(doh, sc_ref[h], preferred_element_type=F32) * (DK ** -0.5)).astype(BF16)
            dkd_parts.append(jnp.dot(v_ref[:, cv], dsb, preferred_element_type=F32))
            dv_ref[:, cv] = lax.dot_general(kd[:, ck], dsb, (NT, ((), ())), preferred_element_type=F32).astype(BF16)
            ddecay_parts.append(jnp.sum(ds * sp_ref[h].astype(F32), axis=0, keepdims=True) * has_prev)
            ds_scr[h] = ds * decay[:, ck]
        dkd = jnp.concatenate(dkd_parts, axis=1)
        ddecay = jnp.concatenate(ddecay_parts, axis=1)
        dk_ref[...] = (dkd * e).astype(BF16)
        dearg = dkd * kdf
        dlast = jnp.sum(dearg, axis=0, keepdims=True) + ddecay * decay
        r = lax.broadcasted_iota(jnp.int32, (CHUNK, CHUNK), 0)
        c = lax.broadcasted_iota(jnp.int32, (CHUNK, CHUNK), 1)
        triu = jnp.where(c >= r, 1.0, 0.0).astype(F32)
        dls = dlast - jnp.dot(triu, dearg, preferred_element_type=F32, precision=lax.Precision.HIGHEST)
        da = dls * (1.0 / 16.0) * (1.0 - _sigmoid(a))
        dab = da.astype(BF16)
        dal_ref[...] = lax.dot_general(dab, wa_ref[...], (NT, ((), ())), preferred_element_type=F32).astype(BF16)
        dwa = lax.dot_general(al_ref[...], dab, (TN, ((), ())), preferred_element_type=F32)
        dba = jnp.sum(da, axis=0, keepdims=True)

        @pl.when(i == 0)
        def _():
            dwa_ref[...] = dwa
            dba_ref[...] = dba

        @pl.when(i > 0)
        def _():
            dwa_ref[...] += dwa
            dba_ref[...] += dba

    rev = lambda i: NCHUNK - 1 - i
    return pl.pallas_call(
        body, name="gla_bwd", grid=(NCHUNK,),
        in_specs=[pl.BlockSpec((CHUNK, D), lambda i: (rev(i), 0)),
                  pl.BlockSpec((CHUNK, QK), lambda i: (rev(i), OQ // QK)), pl.BlockSpec((CHUNK, QK), lambda i: (rev(i), OKK // QK)),
                  pl.BlockSpec((CHUNK, D), lambda i: (rev(i), OV // D)), pl.BlockSpec((CHUNK, APAD), lambda i: (rev(i), OA // APAD)),
                  pl.BlockSpec((None, HEADS, DV, DK), lambda i: (rev(i), 0, 0, 0)),
                  pl.BlockSpec((None, HEADS, DV, DK), lambda i: (jnp.maximum(rev(i) - 1, 0), 0, 0, 0)),
                  pl.BlockSpec((APAD, QK), lambda i: (0, 0)), pl.BlockSpec((1, QK), lambda i: (0, 0)), ANY],
        out_specs=[pl.BlockSpec((CHUNK, QK), lambda i: (rev(i), 0)), pl.BlockSpec((CHUNK, QK), lambda i: (rev(i), 0)),
                   pl.BlockSpec((CHUNK, D), lambda i: (rev(i), 0)), pl.BlockSpec((CHUNK, APAD), lambda i: (rev(i), 0)),
                   pl.BlockSpec((APAD, QK), lambda i: (0, 0)), pl.BlockSpec((1, QK), lambda i: (0, 0))],
        out_shape=[SDS((T, QK), BF16), SDS((T, QK), BF16), SDS((T, D), BF16), SDS((T, APAD), BF16),
                   SDS((APAD, QK), F32), SDS((1, QK), F32)],
        scratch_shapes=[pltpu.VMEM((HEADS, DV, DK), F32)],
        compiler_params=_cparams(32 * 1024 * 1024, ("arbitrary",)),
    )(do, pcat, pcat, pcat, pcat, states, states, wa, ba, after)


TMF = 256
TMW = 512
_rowblk = ((TMF, D), lambda j, i, k: (i, 0))
_vec = ((1, D), lambda j, i, k: (0, 0))


def _full_spec(col):
    return ((TMF, D), lambda j, i, k: (i, col))


TBIG = 1024


def square_matmul(name, a, b, *, a_spec, b_spec, cdims, nk, after=None):
    def epi(acc, ex, outs, i):
        outs[0][...] = acc

    return matmul(name, a, b, a_spec=a_spec, b_spec=b_spec, cdims=cdims, grid=(D // TBIG, T // TBIG, nk),
                  acc_shape=(TBIG, TBIG), outs=[((T, D), F32, (TBIG, TBIG), lambda j, i, k: (i, j))], epi=epi,
                  after=after)[0]


def rowwise(name, y, *, extras, outs, epi):
    ne = len(extras)

    def body(*refs):
        epi(refs[0][...], refs[1:1 + ne], refs[1 + ne:], pl.program_id(1))

    in_specs = [pl.BlockSpec(*_rowblk)] + [pl.BlockSpec(bs, im) for _, bs, im in extras]
    return pl.pallas_call(
        body, name=name, grid=(1, T // TMF, 1), in_specs=in_specs,
        out_specs=[pl.BlockSpec(bs, im) for _, _, bs, im in outs], out_shape=[SDS(s, dt) for s, dt, _, _ in outs],
        compiler_params=_cparams(40 * 1024 * 1024, ("arbitrary", "arbitrary", "arbitrary")),
    )(y, *[arr for arr, _, _ in extras])


def mm_gla_out(og, w, ylin, pcat, pscale):
    def epi(acc, ex, outs, i):
        ylin_ref, lgp_ref, lgg_ref, ps_ref = ex
        for c0 in range(0, D, EPI_COLS):
            cs = slice(c0, c0 + EPI_COLS)
            gp = _sigmoid(lgp_ref[:, cs].astype(F32))
            gg = _sigmoid(lgg_ref[:, cs].astype(F32))
            a = acc[:, cs]
            outs[0][:, cs] = (gp * (ylin_ref[:, cs].astype(F32) * ps_ref[:, cs]) + gg * a).astype(BF16)
            outs[1][:, cs] = a.astype(BF16)

    return matmul("mm_gla_out", og, w, a_spec=_rowblk, b_spec=((D, D), lambda j, i, k: (0, 0)), cdims=NN,
                  grid=(1, T // TMF, 1), acc_shape=(TMF, D),
                  extras=[(ylin, *_rowblk), (pcat, *_full_spec(OGP // D)), (pcat, *_full_spec(OGG // D)), (pscale, *_vec)],
                  outs=[((T, D), BF16, *_rowblk), ((T, D), BF16, *_rowblk)], epi=epi)


def mm_out(mixed, w, x, g2):
    def epi(acc, ex, outs, i):
        x_ref, g_ref = ex
        x2 = x_ref[...] + acc
        r = lax.rsqrt(jnp.mean(x2 * x2, axis=-1, keepdims=True) + EPS)
        outs[0][...] = x2
        outs[1][...] = (x2 * r * g_ref[...]).astype(BF16)

    return matmul("mm_out", mixed, w, a_spec=_rowblk, b_spec=((D, D), lambda j, i, k: (0, 0)), cdims=NN,
                  grid=(1, T // TMF, 1), acc_shape=(TMF, D), extras=[(x, *_rowblk), (g2, *_vec)],
                  outs=[((T, D), F32, *_rowblk), ((T, D), BF16, *_rowblk)], epi=epi)


def mm_up(h2, wup):
    def epi(acc, ex, outs, i):
        r = jnp.maximum(acc, 0.0)
        outs[0][...] = r.astype(BF16)
        outs[1][...] = (r * r).astype(BF16)

    blk = ((TMW, D), lambda j, i, k: (i, j))
    return matmul("mm_up", h2, wup, a_spec=((TMW, D), lambda j, i, k: (i, 0)), b_spec=((None, D, D), lambda j, i, k: (j, 0, 0)),
                  cdims=NN, grid=(NCHIP, T // TMW, 1), acc_shape=(TMW, D),
                  outs=[((T, DFF), BF16, *blk), ((T, DFF), BF16, *blk)], epi=epi)


def mm_down(act, wdown, x2, tgt, gf):
    tk = 4096

    def epi(acc, ex, outs, i):
        x2_ref, t_ref, g_ref = ex
        dx_ref, dxb_ref, gnf_ref, loss_ref = outs
        x3 = x2_ref[...] + acc
        r = lax.rsqrt(jnp.mean(x3 * x3, axis=-1, keepdims=True) + EPS)
        xn = x3 * r
        err = xn * g_ref[...] - t_ref[...]
        lsum = 0.5 * jnp.sum(jnp.mean(err * err, axis=-1, keepdims=True), axis=0, keepdims=True)
        dy = err * (1.0 / D)
        _row_acc(gnf_ref, jnp.sum(dy * xn, axis=0, keepdims=True), i)
        _row_acc(loss_ref, jnp.broadcast_to(lsum, (1, 128)), i)
        dx3 = _rms_bwd(xn, r, dy * g_ref[...])
        dx_ref[...] = dx3
        dxb_ref[...] = dx3.astype(BF16)

    y = square_matmul("mm_down", act, wdown, a_spec=((TBIG, tk), lambda j, i, k: (i, k)),
                      b_spec=((tk, TBIG), lambda j, i, k: (k, j)), cdims=NN, nk=DFF // tk)
    return rowwise("rows_final", y, extras=[(x2, *_rowblk), (tgt, *_rowblk), (gf, *_vec)],
                   outs=[((T, D), F32, *_rowblk), ((T, D), BF16, *_rowblk), ((1, D), F32, *_vec),
                         ((1, 128), F32, (1, 128), lambda j, i, k: (0, 0))], epi=epi)


def mm_dact(dx3b, wdown, rup, after=None):
    def epi(acc, ex, outs, i):
        outs[0][...] = (acc * 2.0 * ex[0][...].astype(F32)).astype(BF16)

    blk = ((TMW, D), lambda j, i, k: (i, j))
    return matmul("mm_dact", dx3b, wdown, a_spec=((TMW, D), lambda j, i, k: (i, 0)), b_spec=((D, D), lambda j, i, k: (j, 0)),
                  cdims=NT, grid=(DFF // D, T // TMW, 1), acc_shape=(TMW, D), extras=[(rup, *blk)],
                  outs=[((T, DFF), BF16, *blk)], epi=epi, after=after)[0]


def mm_wgrad(name, a, b, m, n, out_shape, out_block, out_map, tm, tn, after=None):
    def epi(acc, ex, outs, i):
        outs[0][...] = acc.astype(BF16).reshape(outs[0].shape)

    return matmul(name, a, b, a_spec=((T, tm), lambda j, i, k: (0, i)), b_spec=((T, tn), lambda j, i, k: (0, j)),
                  cdims=TN, grid=(n // tn, m // tm, 1), acc_shape=(tm, tn),
                  outs=[(out_shape, BF16, out_block, out_map)], epi=epi, after=after)[0]


def mm_dh2(dup, wup, x2, dx3, g2, after=None):
    def epi(acc, ex, outs, i):
        x2_ref, dx3_ref, g_ref = ex
        x2 = x2_ref[...]
        r = lax.rsqrt(jnp.mean(x2 * x2, axis=-1, keepdims=True) + EPS)
        xn = x2 * r
        _row_acc(outs[2], jnp.sum(acc * xn, axis=0, keepdims=True), i)
        dx2 = dx3_ref[...] + _rms_bwd(xn, r, acc * g_ref[...])
        outs[0][...] = dx2
        outs[1][...] = dx2.astype(BF16)

    y = square_matmul("mm_dh2", dup, wup, a_spec=((TBIG, D), lambda j, i, k: (i, k)),
                      b_spec=((None, TBIG, D), lambda j, i, k: (k, j, 0)), cdims=NT, nk=NCHIP, after=after)
    return rowwise("rows_dh2", y, extras=[(x2, *_rowblk), (dx3, *_rowblk), (g2, *_vec)],
                   outs=[((T, D), F32, *_rowblk), ((T, D), BF16, *_rowblk), ((1, D), F32, *_vec)], epi=epi)


def mm_dmixed(dx2b, wout, pcat, ylin, ygla, pscale, after=None):
    def epi(acc, ex, outs, i):
        lgp_ref, lgg_ref, ylin_ref, ygla_ref, ps_ref = ex
        dps = []
        for c0 in range(0, D, EPI_COLS):
            cs = slice(c0, c0 + EPI_COLS)
            gp = _sigmoid(lgp_ref[:, cs].astype(F32))
            gg = _sigmoid(lgg_ref[:, cs].astype(F32))
            yl = ylin_ref[:, cs].astype(F32)
            ps = ps_ref[:, cs]
            a = acc[:, cs]
            agp = a * gp
            outs[0][:, cs] = (agp * ps).astype(BF16)
            outs[1][:, cs] = (a * gg).astype(BF16)
            outs[2][:, cs] = (agp * (yl * ps) * (1.0 - gp)).astype(BF16)
            outs[3][:, cs] = (a * ygla_ref[:, cs].astype(F32) * gg * (1.0 - gg)).astype(BF16)
            dps.append(jnp.sum(agp * yl, axis=0, keepdims=True))
        _row_acc(outs[4], jnp.concatenate(dps, axis=1), i)

    return matmul("mm_dmixed", dx2b, wout, a_spec=_rowblk, b_spec=((D, D), lambda j, i, k: (0, 0)), cdims=NT,
                  grid=(1, T // TMF, 1), acc_shape=(TMF, D),
                  extras=[(pcat, *_full_spec(OGP // D)), (pcat, *_full_spec(OGG // D)), (ylin, *_rowblk), (ygla, *_rowblk),
                          (pscale, *_vec)],
                  outs=[((T, D), BF16, *_rowblk)] * 4 + [((1, D), F32, *_vec)], epi=epi, after=after)


def mm_dog(dygla, wgo, o, pcat, ng, after=None):
    def epi(acc, ex, outs, i):
        o_ref, g_ref, ng_ref = ex
        do_ref, dg_ref, gng_ref = outs
        gparts = []
        for h in range(HEADS):
            cv = slice(h * DV, (h + 1) * DV)
            oh = o_ref[:, cv].astype(F32)
            r = lax.rsqrt(jnp.mean(oh * oh, axis=-1, keepdims=True) + EPS)
            on = oh * r
            gv = g_ref[:, cv].astype(F32)
            sg = _sigmoid(gv)
            a = acc[:, cv]
            dgain = a * (gv * sg)
            gparts.append(jnp.sum(dgain * on, axis=0, keepdims=True))
            ngh = ng_ref[:, cv]
            do_ref[:, cv] = _rms_bwd(on, r, dgain * ngh).astype(BF16)
            dg_ref[:, cv] = (a * (on * ngh) * (sg * (1.0 + gv * (1.0 - sg)))).astype(BF16)
        _row_acc(gng_ref, jnp.concatenate(gparts, axis=1), i)

    return matmul("mm_dog", dygla, wgo, a_spec=_rowblk, b_spec=((D, D), lambda j, i, k: (0, 0)), cdims=NT,
                  grid=(1, T // TMF, 1), acc_shape=(TMF, D),
                  extras=[(o, *_rowblk), (pcat, *_full_spec(OG // D)), (ng, *_vec)],
                  outs=[((T, D), BF16, *_rowblk), ((T, D), BF16, *_rowblk), ((1, D), F32, *_vec)], epi=epi, after=after)


def mm_dh1(dpcat, wcat, x, dx2, g1, after=None):
    tk = 3840

    def epi(acc, ex, outs, i):
        x_ref, dx2_ref, g_ref = ex
        xv = x_ref[...]
        r = lax.rsqrt(jnp.mean(xv * xv, axis=-1, keepdims=True) + EPS)
        xn = xv * r
        _row_acc(outs[1], jnp.sum(acc * xn, axis=0, keepdims=True), i)
        outs[0][...] = dx2_ref[...] + _rms_bwd(xn, r, acc * g_ref[...])

    y = square_matmul("mm_dh1", dpcat, wcat, a_spec=((TBIG, tk), lambda j, i, k: (i, k)),
                      b_spec=((TBIG, tk), lambda j, i, k: (j, k)), cdims=NT, nk=NCAT // tk, after=after)
    return rowwise("rows_dh1", y, extras=[(x, *_rowblk), (dx2, *_rowblk), (g1, *_vec)],
                   outs=[((T, D), F32, *_rowblk), ((1, D), F32, *_vec)], epi=epi)


def _tile_rows(rows, cols, n_arrays):
    tm = rows
    while tm % 32 == 0 and 2 * n_arrays * tm * cols * 4 > 36 * 1024 * 1024:
        tm //= 2
    return tm


def add_pairs(name, parts, theirs, core):
    _, _, r, c = parts.shape
    tm = _tile_rows(r, c, 3)

    def body(core_ref, a_ref, b_ref, o_ref):
        o_ref[...] = (a_ref[...].astype(F32) + b_ref[...].astype(F32)).astype(BF16)

    spec = pl.BlockSpec((None, tm, c), lambda j, i, core_ref: (j, i, 0))
    grid_spec = pltpu.PrefetchScalarGridSpec(
        num_scalar_prefetch=1, grid=(NCHIP, r // tm),
        in_specs=[pl.BlockSpec((None, None, tm, c), lambda j, i, core_ref: (core_ref[0], j, i, 0)), spec], out_specs=spec)
    return pl.pallas_call(body, name=name, grid_spec=grid_spec, out_shape=SDS((NCHIP, r, c), BF16),
                          compiler_params=_cparams(40 * 1024 * 1024, ("arbitrary", "arbitrary")))(core, parts, theirs)


def sum_chips(name, sums, landed, chip):
    _, r, c = sums.shape
    tm = _tile_rows(r, c, 4)

    def body(chip_ref, own_ref, l_ref, o_ref):
        s = own_ref[...].astype(F32)
        for t in range(NCHIP - 1):
            s = s + l_ref[t].astype(F32)
        o_ref[...] = s

    grid_spec = pltpu.PrefetchScalarGridSpec(
        num_scalar_prefetch=1, grid=(r // tm,),
        in_specs=[pl.BlockSpec((None, tm, c), lambda i, chip_ref: (chip_ref[0], i, 0)),
                  pl.BlockSpec((NCHIP - 1, tm, c), lambda i, chip_ref: (0, i, 0))],
        out_specs=pl.BlockSpec((tm, c), lambda i, chip_ref: (i, 0)))
    return pl.pallas_call(body, name=name, grid_spec=grid_spec, out_shape=SDS((r, c), F32),
                          compiler_params=_cparams(40 * 1024 * 1024, ("arbitrary",)))(chip, sums, landed)


def _adamw_math(wv, gv, mv, vv):
    mn = ADAM_B1 * mv + (1.0 - ADAM_B1) * gv
    vn = ADAM_B2 * vv + (1.0 - ADAM_B2) * (gv * gv)
    mh = mn / (1.0 - ADAM_B1 ** ADAM_STEP)
    vh = vn / (1.0 - ADAM_B2 ** ADAM_STEP)
    return -ADAM_LR * (mh / (jnp.sqrt(vh) + ADAM_EPS) + ADAM_WD * wv), mn, vn


def adamw(name, w, g, m, v):
    def body(w_ref, g_ref, m_ref, v_ref, go_ref, d_ref, mo_ref, vo_ref):
        gv = g_ref[...]
        go_ref[...] = gv
        d_ref[...], mo_ref[...], vo_ref[...] = _adamw_math(w_ref[...], gv, m_ref[...], v_ref[...])

    return pl.pallas_call(body, name=name, out_shape=[SDS(w.shape, F32)] * 4)(w, g, m, v)


def adamw_halves(name, w, g_own, g_sib, m, v, core):
    _, r, c = w.shape
    tm = _tile_rows(r, c, 10)

    def body(core_ref, w_ref, go_ref, gs_ref, m_ref, v_ref, g_out, d_out, m_out, v_out):
        gv = jnp.where(pl.program_id(0) == core_ref[0], go_ref[...], gs_ref[...])
        g_out[...] = gv
        d_out[...], m_out[...], v_out[...] = _adamw_math(w_ref[...], gv, m_ref[...], v_ref[...])

    full = pl.BlockSpec((None, tm, c), lambda h, i, core_ref: (h, i, 0))
    own = pl.BlockSpec((tm, c), lambda h, i, core_ref: (jnp.where(h == core_ref[0], i, 0), 0))
    sib = pl.BlockSpec((tm, c), lambda h, i, core_ref: (jnp.where(h == core_ref[0], 0, i), 0))
    grid_spec = pltpu.PrefetchScalarGridSpec(num_scalar_prefetch=1, grid=(2, r // tm),
                                             in_specs=[full, own, sib, full, full], out_specs=[full] * 4)
    return pl.pallas_call(body, name=name, grid_spec=grid_spec, out_shape=[SDS(w.shape, F32)] * 4,
                          compiler_params=_cparams(48 * 1024 * 1024, ("arbitrary", "arbitrary")))(core, w, g_own, g_sib, m, v)


def cast_bf16(name, w):
    _, r, c = w.shape
    tm = _tile_rows(r, c, 2)

    def body(w_ref, o_ref):
        o_ref[...] = w_ref[...].astype(BF16)

    spec = pl.BlockSpec((None, tm, c), lambda h, i: (h, i, 0))
    return pl.pallas_call(body, name=name, grid=(2, r // tm), in_specs=[spec], out_specs=spec, out_shape=SDS(w.shape, BF16),
                          compiler_params=_cparams(40 * 1024 * 1024, ("arbitrary", "arbitrary")))(w)


def pack_rows(name, parts, rows, after=None):
    width = parts[0].shape[1]
    n = len(parts)
    afters = _as_list(after)

    def body(*refs):
        out_ref = refs[n + len(afters)]
        out_ref[...] = jnp.zeros_like(out_ref)
        off = 0
        for p in refs[:n]:
            out_ref[off:off + p.shape[0], :] = p[...]
            off += p.shape[0]

    vm = pl.BlockSpec(memory_space=pltpu.VMEM)
    return pl.pallas_call(body, name=name, in_specs=[vm] * n + [ANY] * len(afters), out_specs=vm,
                          out_shape=SDS((rows, width), F32))(*parts, *afters)


def _place():
    x, y, c = lax.axis_index("x"), lax.axis_index("y"), lax.axis_index("c")
    chips = [(1 - x, y), (x, 1 - y), (1 - x, 1 - y)]
    return x, y, c, chips


def _row_split(shape, dtype):
    r, c = shape
    n = 1
    while r % (2 * n) == 0 and (r // (2 * n)) % 16 == 0 and (r // n) * c * jnp.dtype(dtype).itemsize > PIECE_BYTES:
        n *= 2
    return [pl.ds(s * (r // n), r // n) for s in range(n)]


def _pieces(ref):
    *lead, r, c = ref.shape
    split = _row_split((r, c), ref.dtype)
    return [ref.at[(*idx, s)] for idx in itertools.product(*[range(d) for d in lead]) for s in split]


HBM = pl.BlockSpec(memory_space=pltpu.HBM)
SEM = pl.BlockSpec(memory_space=pltpu.SEMAPHORE)
EFFECT = pltpu.SideEffectType.DATAFLOW_SIDE_EFFECTING


def gather_start(name, shards, after=None):
    n = len(shards)
    afters = _as_list(after)

    def body(*refs):
        src, land = refs[:n], refs[n:2 * n]
        send, recv = refs[2 * n + len(afters)], refs[2 * n + len(afters) + 1]
        x, y, c, chips = _place()
        me = 2 * x + y
        for a in range(n):
            for j, (cx, cy) in enumerate(chips[:2]):
                for sp, dp in zip(_pieces(src[a].at[c]), _pieces(land[a].at[me, c])):
                    pltpu.make_async_remote_copy(sp, dp, send.at[2 * a + j], recv.at[2 * a + j],
                                                 device_id=(cx, cy, c), device_id_type=MESH).start()

    lands = [pltpu.with_memory_space_constraint(lax.empty((NCHIP,) + s.shape, s.dtype), pltpu.HBM) for s in shards]
    srcs = [pltpu.with_memory_space_constraint(s, pltpu.HBM) for s in shards]
    outs = pl.pallas_call(
        body, name=name,
        out_shape=(pltpu.SemaphoreType.DMA((2 * n,)), pltpu.SemaphoreType.DMA((2 * n,)),
                   *[pltpu.HBM(s.shape, s.dtype) for s in shards], *[pltpu.HBM(l.shape, l.dtype) for l in lands]),
        in_specs=[HBM] * (2 * n) + [ANY] * len(afters), out_specs=(SEM, SEM, *([HBM] * (2 * n))),
        input_output_aliases={i: 2 + i for i in range(2 * n)},
        compiler_params=pltpu.CompilerParams(has_side_effects=EFFECT),
    )(*srcs, *lands, *afters)
    return outs[0], outs[1], list(outs[2:2 + n]), list(outs[2 + n:2 + 2 * n])


def _relay_blocks(land, c, chips):
    (xx, xy), (yx, yy), (dx, dy) = chips
    rows = land.shape[2] // 2
    upper, lower = pl.ds(0, rows), pl.ds(rows, rows)
    return [(land.at[2 * yx + yy, c, lower], land.at[2 * dx + dy, c, lower]),
            (land.at[2 * xx + xy, c, upper], land.at[2 * dx + dy, c, upper])]


def relay_turn(name, send, recv, shards, lands, after):
    n = len(shards)
    afters = _as_list(after)

    def body(*refs):
        src, had = refs[:n], refs[n:2 * n]
        send_ref, recv_ref = refs[2 * n], refs[2 * n + 1]
        rsend, rrecv = refs[2 * n + 2 + len(afters)], refs[2 * n + 3 + len(afters)]
        land = refs[3 * n + 4 + len(afters):4 * n + 4 + len(afters)]
        x, y, c, chips = _place()
        for a in range(n):
            for j, (cx, cy) in enumerate(chips[:2]):
                cp = pltpu.make_async_remote_copy(src[a].at[c], had[a].at[2 * cx + cy, c], send_ref.at[2 * a + j],
                                                  recv_ref.at[2 * a + j], device_id=(cx, cy, c), device_id_type=MESH)
                cp.wait_send()
                cp.wait_recv()
        for a in range(n):
            for j, ((sent, _), (dst, _)) in enumerate(zip(_relay_blocks(had[a], c, chips), _relay_blocks(land[a], c, chips))):
                cx, cy = chips[j]
                for sp, dp in zip(_pieces(sent), _pieces(dst)):
                    pltpu.make_async_remote_copy(sp, dp, rsend.at[2 * a + j], rrecv.at[2 * a + j],
                                                 device_id=(cx, cy, c), device_id_type=MESH).start()

    outs = pl.pallas_call(
        body, name=name,
        out_shape=(pltpu.SemaphoreType.DMA((2 * n,)), pltpu.SemaphoreType.DMA((2 * n,)),
                   *[pltpu.HBM(s.shape, s.dtype) for s in shards], *[pltpu.HBM(l.shape, l.dtype) for l in lands]),
        in_specs=[HBM] * (2 * n) + [SEM, SEM] + [ANY] * len(afters), out_specs=(SEM, SEM, *([HBM] * (2 * n))),
        input_output_aliases={i: 2 + i for i in range(2 * n)},
        compiler_params=pltpu.CompilerParams(has_side_effects=EFFECT),
    )(*shards, *lands, send, recv, *afters)
    return outs[0], outs[1], list(outs[2:2 + n]), list(outs[2 + n:2 + 2 * n])


def relay_wait(name, send, recv, lands, after):
    n = len(lands)
    afters = _as_list(after)

    def body(*refs):
        land = refs[:n]
        send_ref, recv_ref = refs[n], refs[n + 1]
        x, y, c, chips = _place()
        for a in range(n):
            for j, (sent, got) in enumerate(_relay_blocks(land[a], c, chips)):
                cx, cy = chips[j]
                cp = pltpu.make_async_remote_copy(sent, got, send_ref.at[2 * a + j], recv_ref.at[2 * a + j],
                                                  device_id=(cx, cy, c), device_id_type=MESH)
                cp.wait_send()
                cp.wait_recv()

    outs = pl.pallas_call(
        body, name=name, out_shape=tuple(pltpu.HBM(l.shape, l.dtype) for l in lands),
        in_specs=[HBM] * n + [SEM, SEM] + [ANY] * len(afters), out_specs=[HBM] * n,
        input_output_aliases={i: i for i in range(n)},
        compiler_params=pltpu.CompilerParams(has_side_effects=EFFECT),
    )(*lands, send, recv, *afters)
    return list(outs)


def forward_halves(name, shards, lands):
    n = len(lands)

    def body(*refs):
        had, buf = refs[:n], refs[n:2 * n]
        send, recv = refs[2 * n:]
        x, y, c, chips = _place()
        sib = (x, y, 1 - c)
        for a in range(n):
            for j, (cx, cy) in enumerate(chips):
                for sp, dp in zip(_pieces(had[a].at[2 * cx + cy, c]), _pieces(buf[a].at[2 * cx + cy, c])):
                    pltpu.make_async_remote_copy(sp, dp, send.at[3 * a + j], recv.at[3 * a + j], device_id=sib, device_id_type=MESH).start()
        for a in range(n):
            for j, (cx, cy) in enumerate(chips):
                pltpu.make_async_remote_copy(had[a].at[2 * cx + cy, c], buf[a].at[2 * cx + cy, 1 - c], send.at[3 * a + j],
                                             recv.at[3 * a + j], device_id=sib, device_id_type=MESH).wait()

    got = pl.pallas_call(
        body, name=name, in_specs=[ANY] * n, out_specs=[ANY] * n, out_shape=[SDS(l.shape, l.dtype) for l in lands],
        input_output_aliases={i: i for i in range(n)},
        scratch_shapes=[pltpu.SemaphoreType.DMA((3 * n,)), pltpu.SemaphoreType.DMA((3 * n,))],
    )(*lands)
    me = 2 * lax.axis_index("x") + lax.axis_index("y")
    return [lax.dynamic_update_index_in_dim(g, s, me, 0) for g, s in zip(got, shards)]


def forward_turn(name, send, recv, lands, after):
    n = len(lands)
    afters = _as_list(after)

    def body(*refs):
        had = refs[:n]
        send_ref, recv_ref = refs[n], refs[n + 1]
        fsend, frecv = refs[n + 2 + len(afters)], refs[n + 3 + len(afters)]
        buf = refs[n + 4 + len(afters):2 * n + 4 + len(afters)]
        x, y, c, chips = _place()
        sib = (x, y, 1 - c)
        for a in range(n):
            for j, (sent, got) in enumerate(_relay_blocks(had[a], c, chips)):
                cx, cy = chips[j]
                cp = pltpu.make_async_remote_copy(sent, got, send_ref.at[2 * a + j], recv_ref.at[2 * a + j],
                                                  device_id=(cx, cy, c), device_id_type=MESH)
                cp.wait_send()
                cp.wait_recv()
        for a in range(n):
            for j, (cx, cy) in enumerate(chips):
                for sp, dp in zip(_pieces(had[a].at[2 * cx + cy, c]), _pieces(buf[a].at[2 * cx + cy, c])):
                    pltpu.make_async_remote_copy(sp, dp, fsend.at[3 * a + j], frecv.at[3 * a + j], device_id=sib, device_id_type=MESH).start()

    outs = pl.pallas_call(
        body, name=name,
        out_shape=(pltpu.SemaphoreType.DMA((3 * n,)), pltpu.SemaphoreType.DMA((3 * n,)), *[pltpu.HBM(l.shape, l.dtype) for l in lands]),
        in_specs=[HBM] * n + [SEM, SEM] + [ANY] * len(afters), out_specs=(SEM, SEM, *([HBM] * n)),
        input_output_aliases={i: 2 + i for i in range(n)},
        compiler_params=pltpu.CompilerParams(has_side_effects=EFFECT),
    )(*lands, send, recv, *afters)
    return outs[0], outs[1], list(outs[2:])


def forward_wait(name, send, recv, lands, after):
    n = len(lands)
    afters = _as_list(after)

    def body(*refs):
        land = refs[:n]
        send_ref, recv_ref = refs[n], refs[n + 1]
        x, y, c, chips = _place()
        sib = (x, y, 1 - c)
        for a in range(n):
            for j, (cx, cy) in enumerate(chips):
                cp = pltpu.make_async_remote_copy(land[a].at[2 * cx + cy, c], land[a].at[2 * cx + cy, 1 - c], send_ref.at[3 * a + j],
                                                  recv_ref.at[3 * a + j], device_id=sib, device_id_type=MESH)
                cp.wait_send()
                cp.wait_recv()

    outs = pl.pallas_call(
        body, name=name, out_shape=tuple(pltpu.HBM(l.shape, l.dtype) for l in lands),
        in_specs=[HBM] * n + [SEM, SEM] + [ANY] * len(afters), out_specs=[HBM] * n,
        input_output_aliases={i: i for i in range(n)},
        compiler_params=pltpu.CompilerParams(has_side_effects=EFFECT),
    )(*lands, send, recv, *afters)
    return list(outs)


def exchange_start(name, parts):
    n = len(parts)

    def body(*refs):
        src, got = refs[:n], refs[n:2 * n]
        send, recv = refs[2 * n], refs[2 * n + 1]
        token = refs[4 * n + 2]
        x, y, c, _ = _place()
        sib = (x, y, 1 - c)
        for a in range(n):
            for sp, dp in zip(_pieces(src[a].at[1 - c]), _pieces(got[a])):
                pltpu.make_async_remote_copy(sp, dp, send.at[a], recv.at[a], device_id=sib, device_id_type=MESH).start()
        token[...] = jnp.zeros_like(token)

    lands = [pltpu.with_memory_space_constraint(lax.empty(p.shape[1:], p.dtype), pltpu.HBM) for p in parts]
    srcs = [pltpu.with_memory_space_constraint(p, pltpu.HBM) for p in parts]
    outs = pl.pallas_call(
        body, name=name,
        out_shape=(pltpu.SemaphoreType.DMA((n,)), pltpu.SemaphoreType.DMA((n,)),
                   *[pltpu.HBM(p.shape, p.dtype) for p in parts], *[pltpu.HBM(l.shape, l.dtype) for l in lands],
                   SDS((8, 128), F32)),
        in_specs=[HBM] * (2 * n), out_specs=(SEM, SEM, *([HBM] * (2 * n)), pl.BlockSpec(memory_space=pltpu.VMEM)),
        input_output_aliases={i: 2 + i for i in range(2 * n)},
        compiler_params=pltpu.CompilerParams(has_side_effects=EFFECT),
    )(*srcs, *lands)
    return outs[0], outs[1], list(outs[2:2 + n]), list(outs[2 + n:2 + 2 * n]), outs[2 + 2 * n]


def exchange_wait(name, send, recv, parts, lands, after):
    n = len(parts)
    afters = _as_list(after)

    def body(*refs):
        src, got = refs[:n], refs[n:2 * n]
        send_ref, recv_ref = refs[2 * n], refs[2 * n + 1]
        x, y, c, _ = _place()
        sib = (x, y, 1 - c)
        for a in range(n):
            cp = pltpu.make_async_remote_copy(src[a].at[1 - c], got[a], send_ref.at[a], recv_ref.at[a], device_id=sib, device_id_type=MESH)
            cp.wait_send()
            cp.wait_recv()

    outs = pl.pallas_call(
        body, name=name,
        out_shape=(*[pltpu.HBM(p.shape, p.dtype) for p in parts], *[pltpu.HBM(l.shape, l.dtype) for l in lands]),
        in_specs=[HBM] * (2 * n) + [SEM, SEM] + [ANY] * len(afters), out_specs=[HBM] * (2 * n),
        input_output_aliases={i: i for i in range(2 * n)},
        compiler_params=pltpu.CompilerParams(has_side_effects=EFFECT),
    )(*parts, *lands, send, recv, *afters)
    return list(outs[:n]), list(outs[n:])


def scatter_start(name, parts):
    n = len(parts)

    def body(*refs):
        src, land = refs[:n], refs[n:2 * n]
        send, recv = refs[2 * n], refs[2 * n + 1]
        token = refs[4 * n + 2]
        x, y, c, chips = _place()
        for a in range(n):
            for j, (cx, cy) in enumerate(chips):
                for sp, dp in zip(_pieces(src[a].at[2 * cx + cy]), _pieces(land[a].at[j])):
                    pltpu.make_async_remote_copy(sp, dp, send.at[3 * a + j], recv.at[3 * a + j],
                                                 device_id=(cx, cy, c), device_id_type=MESH).start()
        token[...] = jnp.zeros_like(token)

    lands = [pltpu.with_memory_space_constraint(lax.empty((NCHIP - 1,) + p.shape[1:], p.dtype), pltpu.HBM) for p in parts]
    srcs = [pltpu.with_memory_space_constraint(p, pltpu.HBM) for p in parts]
    outs = pl.pallas_call(
        body, name=name,
        out_shape=(pltpu.SemaphoreType.DMA((3 * n,)), pltpu.SemaphoreType.DMA((3 * n,)),
                   *[pltpu.HBM(p.shape, p.dtype) for p in parts], *[pltpu.HBM(l.shape, l.dtype) for l in lands],
                   SDS((8, 128), F32)),
        in_specs=[HBM] * (2 * n), out_specs=(SEM, SEM, *([HBM] * (2 * n)), pl.BlockSpec(memory_space=pltpu.VMEM)),
        input_output_aliases={i: 2 + i for i in range(2 * n)},
        compiler_params=pltpu.CompilerParams(has_side_effects=EFFECT),
    )(*srcs, *lands)
    return outs[0], outs[1], list(outs[2:2 + n]), list(outs[2 + n:2 + 2 * n]), outs[2 + 2 * n]


def scatter_wait(name, send, recv, parts, lands, after):
    n = len(parts)
    afters = _as_list(after)

    def body(*refs):
        src, land = refs[:n], refs[n:2 * n]
        send_ref, recv_ref = refs[2 * n], refs[2 * n + 1]
        x, y, c, chips = _place()
        for a in range(n):
            for j, (cx, cy) in enumerate(chips):
                cp = pltpu.make_async_remote_copy(src[a].at[2 * cx + cy], land[a].at[j], send_ref.at[3 * a + j], recv_ref.at[3 * a + j],
                                                  device_id=(cx, cy, c), device_id_type=MESH)
                cp.wait_send()
                cp.wait_recv()

    outs = pl.pallas_call(
        body, name=name,
        out_shape=(*[pltpu.HBM(p.shape, p.dtype) for p in parts], *[pltpu.HBM(l.shape, l.dtype) for l in lands]),
        in_specs=[HBM] * (2 * n) + [SEM, SEM] + [ANY] * len(afters), out_specs=[HBM] * (2 * n),
        input_output_aliases={i: i for i in range(2 * n)},
        compiler_params=pltpu.CompilerParams(has_side_effects=EFFECT),
    )(*parts, *lands, send, recv, *afters)
    return list(outs[:n]), list(outs[n:])


def join_start(name, halves):
    n = len(halves)

    def body(*refs):
        src, dst = refs[:n], refs[n:2 * n]
        send, recv = refs[2 * n], refs[2 * n + 1]
        token = refs[4 * n + 2]
        x, y, c, _ = _place()
        sib = (x, y, 1 - c)
        for a in range(n):
            for sp, dp in zip(_pieces(src[a]), _pieces(dst[a])):
                pltpu.make_async_remote_copy(sp, dp, send.at[a], recv.at[a], device_id=sib, device_id_type=MESH).start()
        token[...] = jnp.zeros_like(token)

    lands = [pltpu.with_memory_space_constraint(lax.empty(h.shape, h.dtype), pltpu.HBM) for h in halves]
    srcs = [pltpu.with_memory_space_constraint(h, pltpu.HBM) for h in halves]
    outs = pl.pallas_call(
        body, name=name,
        out_shape=(pltpu.SemaphoreType.DMA((n,)), pltpu.SemaphoreType.DMA((n,)),
                   *[pltpu.HBM(h.shape, h.dtype) for h in halves], *[pltpu.HBM(l.shape, l.dtype) for l in lands],
                   SDS((8, 128), F32)),
        in_specs=[HBM] * (2 * n), out_specs=(SEM, SEM, *([HBM] * (2 * n)), pl.BlockSpec(memory_space=pltpu.VMEM)),
        input_output_aliases={i: 2 + i for i in range(2 * n)},
        compiler_params=pltpu.CompilerParams(has_side_effects=EFFECT),
    )(*srcs, *lands)
    return outs[0], outs[1], list(outs[2:2 + n]), list(outs[2 + n:2 + 2 * n]), outs[2 + 2 * n]


def join_wait(name, send, recv, halves, lands, after):
    n = len(halves)
    afters = _as_list(after)

    def body(*refs):
        src, dst = refs[:n], refs[n:2 * n]
        send_ref, recv_ref = refs[2 * n], refs[2 * n + 1]
        x, y, c, _ = _place()
        sib = (x, y, 1 - c)
        for a in range(n):
            cp = pltpu.make_async_remote_copy(src[a], dst[a], send_ref.at[a], recv_ref.at[a], device_id=sib, device_id_type=MESH)
            cp.wait_send()
            cp.wait_recv()

    outs = pl.pallas_call(
        body, name=name,
        out_shape=(*[pltpu.HBM(h.shape, h.dtype) for h in halves], *[pltpu.HBM(l.shape, l.dtype) for l in lands]),
        in_specs=[HBM] * (2 * n) + [SEM, SEM] + [ANY] * len(afters), out_specs=[HBM] * (2 * n),
        input_output_aliases={i: i for i in range(2 * n)},
        compiler_params=pltpu.CompilerParams(has_side_effects=EFFECT),
    )(*halves, *lands, send, recv, *afters)
    return list(outs[:n]), list(outs[n:])


def gather_small(name, xs, reduce, after=None):
    m, ncol = xs.shape
    afters = _as_list(after)

    def body(x_ref, *rest):
        out_ref, all_ref, send, recv, lsem = rest[len(afters):]
        x, y, c, chips = _place()
        me, sib = (x, y, c), (x, y, 1 - c)

        def rows(px, py, pc):
            return all_ref.at[pl.ds((4 * px + 2 * py + pc) * m, m), :]

        def copy(k, block, to, src=None):
            return pltpu.make_async_remote_copy(rows(*block) if src is None else src, rows(*block), send.at[k], recv.at[k],
                                                device_id=to, device_id_type=MESH)

        mine = pltpu.make_async_copy(x_ref, rows(*me), lsem)
        mine.start()
        first = [copy(0, me, sib, src=x_ref)] + [copy(1 + j, me, (*chip, c), src=x_ref) for j, chip in enumerate(chips)]
        for cp in first:
            cp.start()
        passed = [copy(4 + j, (*chip, c), sib) for j, chip in enumerate(chips)]
        for j, chip in enumerate(chips):
            copy(1 + j, (*chip, c), me).wait_recv()
            passed[j].start()
        copy(0, sib, me).wait_recv()
        for j, chip in enumerate(chips):
            copy(4 + j, (*chip, 1 - c), me).wait_recv()
        for cp in first + passed:
            cp.wait_send()
        mine.wait()
        if reduce:
            s = all_ref[0:m, :]
            for dev in range(1, 8):
                s = s + all_ref[dev * m:(dev + 1) * m, :]
            out_ref[...] = s
        else:
            out_ref[...] = all_ref[...]

    vm = pl.BlockSpec(memory_space=pltpu.VMEM)
    return pl.pallas_call(
        body, name=name, in_specs=[vm] + [ANY] * len(afters), out_specs=vm,
        out_shape=SDS((m, ncol) if reduce else (8 * m, ncol), F32),
        scratch_shapes=[pltpu.VMEM((8 * m, ncol), F32), pltpu.SemaphoreType.DMA((7,)), pltpu.SemaphoreType.DMA((7,)),
                        pltpu.SemaphoreType.DMA],
    )(xs, *afters)


RELAYOUT_ROWS = 128


def weights_to_cat(name, land, own, place, other, prev=None, after=None):
    tm = RELAYOUT_ROWS
    nb = (D // 2) // tm
    extra = ([] if prev is None else [prev]) + _as_list(after)

    def half(p):
        return 1 - p[0] if other else p[0]

    def body(p_ref, g_ref, own_ref, *rest):
        o_ref = rest[len(extra)]
        nat = jnp.concatenate([jnp.where(p_ref[1] == j, own_ref[...], g_ref[j]) for j in range(NCHIP)], axis=1)
        pad = jnp.zeros((tm, NCAT - OA - 16), BF16)
        o_ref[...] = jnp.concatenate([nat[:, 3072:7168], nat[:, 7184:11280], nat[:, 0:3072], nat[:, 7168:7184], pad], axis=1)

    grid_spec = pltpu.PrefetchScalarGridSpec(
        num_scalar_prefetch=1, grid=(nb,),
        in_specs=[pl.BlockSpec((NCHIP, None, tm, IN_SHARD), lambda i, p: (0, half(p), i, 0)),
                  pl.BlockSpec((None, tm, IN_SHARD), lambda i, p: (half(p), i, 0))] + [ANY] * len(extra),
        out_specs=pl.BlockSpec((tm, NCAT), lambda i, p: (half(p) * nb + i, 0)))
    return pl.pallas_call(
        body, name=name, grid_spec=grid_spec, out_shape=SDS((D, NCAT), BF16),
        input_output_aliases={} if prev is None else {3: 0},
        compiler_params=_cparams(40 * 1024 * 1024, ("arbitrary",)),
    )(place, land, own, *extra)


def grads_from_cat(gw_cat):
    tm = RELAYOUT_ROWS
    nb = (D // 2) // tm

    def body(c_ref, o_ref):
        cat = c_ref[...]
        nat = jnp.concatenate([cat[:, OU:OA], cat[:, OV:OGP], cat[:, OA:OA + 16], cat[:, OGP:OU]], axis=1)
        for j in range(NCHIP):
            o_ref[j] = nat[:, j * IN_SHARD:(j + 1) * IN_SHARD]

    return pl.pallas_call(
        body, name="grads_from_cat", grid=(D // tm,), in_specs=[pl.BlockSpec((tm, NCAT), lambda i: (i, 0))],
        out_specs=pl.BlockSpec((None, NCHIP, tm, IN_SHARD), lambda i: (i // nb, 0, i % nb, 0)),
        out_shape=SDS((2, NCHIP, D // 2, IN_SHARD), BF16), compiler_params=_cparams(40 * 1024 * 1024, ("arbitrary",)),
    )(gw_cat)


def _pad_rows(a, rows):
    return jnp.concatenate([a, jnp.zeros((rows - a.shape[0],) + a.shape[1:], a.dtype)], axis=0)


def local_step(x2d, tgt, gf, g1, pool_scale, wa_pad, b_alpha, ng, g2, get_w, on_grad=None, on_settle=None, tick=None):
    emit = on_grad if on_grad is not None else (lambda group, grads: None)
    settle = on_settle if on_settle is not None else (lambda group, after: None)
    h1 = norm1(x2d, g1)
    wcat, pw = get_w("in", h1)
    pcat = mm_in(h1, wcat)
    dpool, ylin = pool_fwd(pcat, pw)
    pinned = tick("pool", ylin) if tick is not None else None
    og, o, states = gla_fwd(pcat, wa_pad, b_alpha, ng, pinned)
    w_go, w_o = get_w("mid", og)
    mixed, ygla = mm_gla_out(og, w_go, ylin, pcat, pool_scale)
    x2, h2 = mm_out(mixed, w_o, x2d, g2)
    w_up = get_w("up", h2)
    rup, act = mm_up(h2, w_up)
    w_dn = get_w("down", act)
    dx3, dx3b, g_nf, loss_row = mm_down(act, w_dn, x2, tgt, gf)

    gw_down = mm_wgrad("mm_dw_down", act, dx3b, DFF, D, (2, NCHIP, D // 2, D), (None, None, D // 2, D),
                       lambda j, i, k: (i % 2, i // 2, 0, 0), D // 2, D)
    token = emit("down", {"down": gw_down})
    dup = mm_dact(dx3b, w_dn, rup, after=token)
    token = settle("down", dup)
    dx2, dx2b, g_mlp = mm_dh2(dup, w_up, x2, dx3, g2, after=token)
    gw_up = mm_wgrad("mm_dw_up", h2, dup, D, DFF, (2, NCHIP, D // 2, D), (None, None, D // 2, D),
                     lambda j, i, k: (i, j, 0, 0), D // 2, D)
    token = emit("up", {"up": gw_up})
    dylin, dygla, dlgp, dlgg, g_ps = mm_dmixed(dx2b, w_o, pcat, ylin, ygla, pool_scale, after=token)
    token = settle("up", dylin)
    gw_out = mm_wgrad("mm_dw_out", mixed, dx2b, D, D, (2, NCHIP, 256, D), (2, None, 256, D),
                      lambda j, i, k: (0, i, 0, 0), 512, D)
    do, dg, g_ng = mm_dog(dygla, w_go, o, pcat, ng, after=token)
    gw_go = mm_wgrad("mm_dw_gla_out", og, dygla, D, D, (2, NCHIP, 256, D), (2, None, 256, D),
                     lambda j, i, k: (0, i, 0, 0), 512, D)
    token = emit("mix", {"out": gw_out, "gla_out": gw_go})
    dq, dk, dv, dalow, g_wa, g_ba = gla_bwd(do, pcat, states, wa_pad, b_alpha, b_alpha if token is None else token)
    token = settle("mix", dq)
    du, dpw = pool_bwd(dylin, dpool, pw)
    dpcat = jnp.concatenate([dv, dg, dlgp, dlgg, du, dq, dk, dalow, jnp.zeros((T, NCAT - OA - APAD), BF16)], axis=1)
    gw_cat = mm_wgrad("mm_dw_in", h1, dpcat, D, NCAT, (D, NCAT), (1024, 1280), lambda j, i, k: (i, j), 1024, 1280, after=token)
    token = settle("in", emit("in", {"in_cat": gw_cat, "pool": dpw}))
    grad_x, g_mix = mm_dh1(dpcat, wcat, x2d, dx2, g1, after=token)
    return (loss_row[0, 0], grad_x, g_mix, g_ps, g_mlp, g_nf, g_ng, g_ba, g_wa, token,
            gw_cat, dpw, gw_go, gw_out, gw_up, gw_down)


def kernel(x, norm_mix_g, w_in, pool_w, pool_scale, w_alpha, b_alpha, gla_norm_g, w_gla_out, w_out, norm_mlp_g, w_mlp_up, w_mlp_down, norm_final_g, loss_target, m_norm_mix_g, m_w_in, m_pool_w, m_pool_scale, m_w_alpha, m_b_alpha, m_gla_norm_g, m_w_gla_out, m_w_out, m_norm_mlp_g, m_w_mlp_up, m_w_mlp_down, m_norm_final_g, v_norm_mix_g, v_w_in, v_pool_w, v_pool_scale, v_w_alpha, v_b_alpha, v_gla_norm_g, v_w_gla_out, v_w_out, v_norm_mlp_g, v_w_mlp_up, v_w_mlp_down, v_norm_final_g):
    chip = 2 * lax.axis_index("x") + lax.axis_index("y")
    chip_i = chip.astype(jnp.int32).reshape(1)
    core_i = lax.axis_index("c").astype(jnp.int32).reshape(1)
    place_i = jnp.concatenate([core_i, chip_i])
    tgt = loss_target.reshape(T, D)
    gf = norm_final_g.reshape(1, D)

    def halves(w2d):
        r, c = w2d.shape
        return w2d.astype(BF16).reshape(2, r // 2, c)

    pool_shard = pool_w.reshape(4 * PG, PO // NCHIP)
    w_in_r = w_in.reshape(2, D // 2, IN_SHARD)
    sent = {"in": [cast_bf16("cast_w_in", w_in_r), halves(pool_shard)]}
    flight = {}

    def start(group, after=None):
        flight[group] = gather_start("gather_start_" + group, sent[group], after)

    def relay(group, after):
        send, recv, shards, lands = flight[group]
        flight[group] = relay_turn("relay_turn_" + group, send, recv, shards, lands, after)

    def fetch(group, after):
        send, recv, shards, lands = flight[group]
        lands = relay_wait("relay_wait_" + group, send, recv, lands, after)
        return forward_halves("forward_" + group, shards, lands)

    start("in")
    m_in_f, v_in_f, w_go_f, w_o_f, w_up_f, w_dn_f, x_f, wal_f, gng_f = lax.optimization_barrier(
        (m_w_in, v_w_in, w_gla_out, w_out, w_mlp_up, w_mlp_down, x, w_alpha, gla_norm_g, flight["in"][2][0]))[:9]
    m_in_r, v_in_r = m_in_f.reshape(2, D // 2, IN_SHARD), v_in_f.reshape(2, D // 2, IN_SHARD)
    sent["mid"] = [halves(w_go_f[0]), halves(w_o_f[0])]
    relay("in", [m_in_r, v_in_r, *sent["mid"]])
    w_up_f, w_dn_f, x_f, wal_f, gng_f = lax.optimization_barrier(
        (w_up_f, w_dn_f, x_f, wal_f, gng_f, flight["in"][3][0]))[:5]
    sent["up"], sent["down"] = [halves(w_up_f[0])], [halves(w_dn_f[0])]
    x2d = x_f.reshape(T, D)
    big = [w_in_r, w_go_f[0], w_o_f[0], w_up_f[0], w_dn_f[0], pool_shard]

    def tick(point, after):
        if point == "pool":
            relay("mid", after)
            relay("up", flight["mid"][3][0])
            start("down", flight["up"][3][0])
            return [flight["up"][3][0], flight["down"][3][0]]

    def get_w(group, after):
        if group == "in":
            after = [after, *sent["up"], *sent["down"], wa_pad]
        if group == "up":
            relay("down", after)
            send, recv, lands, shards = flight["up"]
            lands = forward_wait("forward_wait_up", send, recv, lands, flight["down"][3][0])
            return lax.dynamic_update_index_in_dim(lands[0], shards[0], chip, 0).reshape(NCHIP, D, D)
        if group == "in":
            send, recv, shards, lands = flight["in"]
            send, recv, lands = forward_turn("forward_turn_in", send, recv, lands, after)
            start("mid", lands[0])
            start("up", flight["mid"][3][0])
            wcat = weights_to_cat("weights_to_cat_mine", lands[0], shards[0], place_i, False, after=flight["up"][3][0])
            lands = forward_wait("forward_wait_in", send, recv, lands, wcat)
            wcat = weights_to_cat("weights_to_cat_sibling", lands[0], shards[0], place_i, True, prev=wcat)
            g_pool = lax.dynamic_update_index_in_dim(lands[1], shards[1], chip, 0)
            pw = jnp.concatenate([g_pool[j].reshape(4, PG, PO // NCHIP) for j in range(NCHIP)], axis=2)
            return wcat, pw
        whole = fetch(group, after)
        if group == "mid":
            send, recv, shards, lands = flight["up"]
            flight["up"] = (*forward_turn("forward_turn_up", send, recv, lands, whole[0]), shards)
            w_go, w_o, _ = lax.optimization_barrier((whole[0], whole[1], flight["up"][2][0]))
            return w_go.reshape(D, D), w_o.reshape(D, D)
        return whole[0].reshape(DFF, D)

    small_w = pack_rows("pack_small_w", [wal_f[0].reshape(4, QK),
                                         jnp.concatenate([gng_f[0].reshape(1, 512), jnp.zeros((1, 512), F32)], axis=1)], 8)
    sw_all = gather_small("gather_small_w", small_w, False).reshape(8, 8, QK)
    wa_full = jnp.concatenate([sw_all[2 * j, 0:4].reshape(16, DK) for j in range(NCHIP)], axis=1)
    ng_full = jnp.concatenate([sw_all[2 * j, 4, 0:512].reshape(HEADS, DV // NCHIP) for j in range(NCHIP)], axis=1)
    wa_pad = _pad_rows(wa_full, APAD).astype(BF16)
    ng = ng_full.reshape(1, D)

    pending = {}
    wmv = {"in": (w_in_r, m_in_r, v_in_r), "gla_out": (big[1], m_w_gla_out, v_w_gla_out), "out": (big[2], m_w_out, v_w_out),
           "up": (big[3], m_w_mlp_up, v_w_mlp_up), "down": (big[4], m_w_mlp_down, v_w_mlp_down), "pool": (big[5], m_pool_w, v_pool_w)}
    big_res = {}

    def reduce_group(group, after):
        nms, send, recv, sums, lands = pending[group]
        sums, lands = scatter_wait("scatter_wait_" + group, send, recv, sums, lands, after)
        reduced = [sum_chips("sum_chips_" + nm, a, b, chip_i) for nm, a, b in zip(nms, sums, lands)]
        send, recv, reduced, lands, token = join_start("join_start_" + group, reduced)
        pending[group] = (nms, send, recv, reduced, lands)
        return token

    def update_group(group, after):
        nms, send, recv, reduced, lands = pending[group]
        reduced, from_sib = join_wait("join_wait_" + group, send, recv, reduced, lands, after)
        for nm, g_own, g_sib in zip(nms, reduced, from_sib):
            w, m, v = wmv[nm]
            shp = (2,) + g_own.shape
            big_res[nm] = adamw_halves("adamw_" + nm, w.reshape(shp), g_own, g_sib, m.reshape(shp), v.reshape(shp), core_i)

    def on_grad(group, grads):
        if group == "in":
            gw_in = grads_from_cat(grads["in_cat"])
            gw_pool = jnp.stack([grads["pool"][:, :, j * 128:(j + 1) * 128].reshape(2, 2 * PG, 128)
                                 for j in range(NCHIP)], axis=1)
            grads = {"in": gw_in, "pool": gw_pool}
        nms, parts = list(grads.keys()), list(grads.values())
        send, recv, parts, got, token = exchange_start("exchange_start_" + group, parts)
        pending[group] = (nms, send, recv, parts, got)
        return token

    def on_settle(group, after):
        if group == "in":
            after = reduce_group("down", after)
        nms, send, recv, parts, got = pending[group]
        parts, got = exchange_wait("exchange_wait_" + group, send, recv, parts, got, after)
        sums = [add_pairs("add_pair_" + nm, a, b, core_i) for nm, a, b in zip(nms, parts, got)]
        send, recv, sums, lands, token = scatter_start("scatter_start_" + group, sums)
        pending[group] = (nms, send, recv, sums, lands)
        if group != "in":
            return token
        token = reduce_group("up", token)
        token = reduce_group("mix", token)
        for earlier in ("down", "up", "mix"):
            update_group(earlier, token)
            token = big_res[pending[earlier][0][-1]][1]
        return [big_res[nm][1] for nm in ("down", "up", "out", "gla_out")]

    (loss_local, grad_x, g_mix, g_ps, g_mlp, g_nf, g_ng, g_ba, g_wa) = local_step(
        x2d, tgt, gf, norm_mix_g, pool_scale, wa_pad, b_alpha, ng, norm_mlp_g, get_w, on_grad, on_settle, tick)[:9]
    loss = lax.psum(loss_local, ("x", "y", "c"))
    join_in_token = reduce_group("in", grad_x)

    ROWS = 16

    def wide(a, n):
        return jnp.concatenate([a.reshape(1, n), jnp.zeros((1, D - n), F32)], axis=1)

    packed = pack_rows("pack_small_g", [g_mix, g_ps, g_mlp, g_nf, g_ng, wide(g_ba, QK), g_wa[0:16].reshape(8, D)], ROWS)
    tot = gather_small("reduce_small_g", packed, True, join_in_token)
    t_wa = lax.dynamic_slice(tot[6:14].reshape(16, QK), (0, chip * DK), (16, DK))
    t_ng = lax.dynamic_slice(tot[4].reshape(HEADS, DV), (0, chip * (DV // NCHIP)), (HEADS, DV // NCHIP))

    def pack_small(nm, mix, ps, mlp, nf, ba, wa, gn, after=None):
        return pack_rows(nm, [mix.reshape(1, D), ps.reshape(1, D), mlp.reshape(1, D), nf.reshape(1, D), wide(ba, QK),
                              wa.reshape(2, D), wide(gn, 512)], ROWS, after)

    update_group("in", tot)
    sg = pack_small("pack_g", tot[0], tot[1], tot[2], tot[3], tot[5, 0:QK], t_wa, t_ng, big_res["in"][3])
    sw = pack_small("pack_w", norm_mix_g, pool_scale, norm_mlp_g, norm_final_g, b_alpha, w_alpha, gla_norm_g)
    sm = pack_small("pack_m", m_norm_mix_g, m_pool_scale, m_norm_mlp_g, m_norm_final_g, m_b_alpha, m_w_alpha, m_gla_norm_g)
    sv = pack_small("pack_v", v_norm_mix_g, v_pool_scale, v_norm_mlp_g, v_norm_final_g, v_b_alpha, v_w_alpha, v_gla_norm_g)
    small_res = adamw("adamw_small", sw, sg, sm, sv)

    def unpack(p):
        return {"norm_mix_g": p[0].reshape(1, D), "pool_scale": p[1].reshape(1, D), "norm_mlp_g": p[2].reshape(1, D),
                "norm_final_g": p[3].reshape(D), "b_alpha": p[4, 0:QK].reshape(1, QK), "w_alpha": p[5:7].reshape(1, 16, DK),
                "gla_norm_g": p[7, 0:512].reshape(1, HEADS, DV // NCHIP)}

    order = ["norm_mix_g", "w_in", "pool_w", "pool_scale", "w_alpha", "b_alpha", "gla_norm_g", "w_gla_out", "w_out",
             "norm_mlp_g", "w_mlp_up", "w_mlp_down", "norm_final_g"]
    big_key = {"w_in": ("in", w_in.shape), "pool_w": ("pool", pool_w.shape), "w_gla_out": ("gla_out", w_gla_out.shape),
               "w_out": ("out", w_out.shape), "w_mlp_up": ("up", w_mlp_up.shape), "w_mlp_down": ("down", w_mlp_down.shape)}
    result = [loss, grad_x.reshape(1, T, D)]
    for kind in range(4):
        small = unpack(small_res[kind])
        for nm in order:
            if nm in big_key:
                key, shp = big_key[nm]
                result.append(big_res[key][kind].reshape(shp))
            else:
                result.append(small[nm])
    return tuple(result)
```

```python
import itertools

import jax
import jax.numpy as jnp
from jax import lax
from jax.experimental import pallas as pl
from jax.experimental.pallas import tpu as pltpu

F32 = jnp.float32
BF16 = jnp.bfloat16
SDS = jax.ShapeDtypeStruct
MESH = pl.DeviceIdType.MESH
ANY = pl.BlockSpec(memory_space=pl.ANY)

T = 2048
D = 2048
DFF = 8192
NCHIP = 4
IN_WIDTH = 11280
IN_SHARD = IN_WIDTH // NCHIP
CHUNK = 64
NCHUNK = T // CHUNK
HEADS = 4
DK = 256
DV = 512
QK = HEADS * DK
EPS = 1e-6
POOL_WINDOWS = (2, 4, 8, 16)
PG = 256
PO = 512

OV, OG, OGP, OGG, OU, OQ, OKK, OA = 0, 2048, 4096, 6144, 8192, 9216, 10240, 11264
NCAT = 11520
APAD = 128

VMEM_CAP = 56 * 1024 * 1024

PIECE_BYTES = 384 * 1024

ADAM_LR, ADAM_B1, ADAM_B2, ADAM_EPS, ADAM_WD, ADAM_STEP = 0.001, 0.9, 0.999, 1e-08, 0.01, 10


def _cparams(vmem_bytes=None, sem=None):
    kw = {}
    if vmem_bytes is not None:
        kw["vmem_limit_bytes"] = int(min(max(vmem_bytes, 32 * 1024 * 1024), VMEM_CAP))
    if sem is not None:
        kw["dimension_semantics"] = sem
    return pltpu.CompilerParams(**kw)


def _nbytes(shape, dtype):
    n = 1
    for s in shape:
        if s is not None:
            n *= s
    return n * jnp.dtype(dtype).itemsize


def _sigmoid(x):
    return 0.5 * jnp.tanh(0.5 * x) + 0.5


GLA_STEP = 4
EPI_COLS = 512


def _as_list(after):
    if after is None:
        return []
    return list(after) if isinstance(after, (list, tuple)) else [after]


def matmul(name, a, b, *, a_spec, b_spec, cdims, grid, acc_shape, outs, extras=(), epi, after=None):
    nj, ni, nk = grid
    ne, no = len(extras), len(outs)
    afters = _as_list(after)
    first_out = 2 + ne + len(afters)

    def body(*refs):
        a_ref, b_ref = refs[0], refs[1]
        ex = refs[2:2 + ne]
        out_refs = refs[first_out:first_out + no]
        i = pl.program_id(1)
        part = lax.dot_general(a_ref[...], b_ref[...], (cdims, ((), ())), preferred_element_type=F32)
        if nk == 1:
            epi(part, ex, out_refs, i)
        else:
            acc_ref = refs[first_out + no]
            k = pl.program_id(2)

            @pl.when(k == 0)
            def _():
                acc_ref[...] = part

            @pl.when(k > 0)
            def _():
                acc_ref[...] += part

            @pl.when(k == nk - 1)
            def _():
                epi(acc_ref[...], ex, out_refs, i)

    in_specs = [pl.BlockSpec(*a_spec), pl.BlockSpec(*b_spec)] + [pl.BlockSpec(bs, im) for _, bs, im in extras]
    in_specs += [ANY] * len(afters)
    out_specs = [pl.BlockSpec(bs, im) for _, _, bs, im in outs]
    out_shape = [SDS(s, dt) for s, dt, _, _ in outs]
    vm = 2 * (_nbytes(a_spec[0], a.dtype) + _nbytes(b_spec[0], b.dtype))
    vm += 2 * sum(_nbytes(bs, arr.dtype) for arr, bs, _ in extras)
    vm += 2 * sum(_nbytes(bs, dt) for _, dt, bs, _ in outs)
    vm += 6 * _nbytes(acc_shape, F32)
    scratch = [pltpu.VMEM(acc_shape, F32)] if nk > 1 else []
    return pl.pallas_call(
        body, name=name, grid=grid, in_specs=in_specs, out_specs=out_specs, out_shape=out_shape,
        scratch_shapes=scratch,
        compiler_params=_cparams(vm, ("arbitrary", "arbitrary", "arbitrary")),
    )(a, b, *[arr for arr, _, _ in extras], *afters)


NN =((1,), (0,))
NT = ((1,), (1,))
TN = ((0,), (0,))


def _row_acc(out_ref, val, i):
    @pl.when(i == 0)
    def _():
        out_ref[...] = val

    @pl.when(i > 0)
    def _():
        out_ref[...] += val


def _rms_bwd(xn, r, dxn):
    return r * (dxn - xn * jnp.mean(dxn * xn, axis=-1, keepdims=True))


def norm1(x, g):
    tm = 256

    def body(x_ref, g_ref, h_ref):
        xv = x_ref[...]
        r = lax.rsqrt(jnp.mean(xv * xv, axis=-1, keepdims=True) + EPS)
        h_ref[...] = (xv * r * g_ref[...]).astype(BF16)

    return pl.pallas_call(
        body, name="norm1", grid=(T // tm,),
        in_specs=[pl.BlockSpec((tm, D), lambda i: (i, 0)), pl.BlockSpec((1, D), lambda i: (0, 0))],
        out_specs=pl.BlockSpec((tm, D), lambda i: (i, 0)), out_shape=SDS((T, D), BF16),
        compiler_params=_cparams(32 * 1024 * 1024, ("arbitrary",)),
    )(x, g)


def mm_in(h1, wcat):
    tm, tn = 1024, 1280

    def epi(acc, ex, outs, i):
        outs[0][...] = acc.astype(BF16)

    return matmul("mm_in", h1, wcat, a_spec=((tm, D), lambda j, i, k: (i, 0)), b_spec=((D, tn), lambda j, i, k: (0, j)),
                  cdims=NN, grid=(NCAT // tn, T // tm, 1), acc_shape=(tm, tn),
                  outs=[((T, NCAT), BF16, (tm, tn), lambda j, i, k: (i, j))], epi=epi)[0]


def _window_sum(x, w, up):
    n = x.shape[0]
    row = lax.broadcasted_iota(jnp.int32, x.shape, 0)
    s, sh = x, 1
    while sh < w:
        if up:
            s = s + jnp.where(row < n - sh, pltpu.roll(s, n - sh, axis=0), 0.0)
        else:
            s = s + jnp.where(row >= sh, pltpu.roll(s, sh, axis=0), 0.0)
        sh *= 2
    return s


def _inv_count(shape, w):
    row = lax.broadcasted_iota(jnp.int32, shape, 0)
    return 1.0 / jnp.minimum(row + 1, w).astype(F32)


def pool_fwd(pcat, pw):
    def body(u_ref, pw_ref, d_ref, y_ref):
        for gi, w in enumerate(POOL_WINDOWS):
            ug = u_ref[:, gi * PG:(gi + 1) * PG].astype(F32)
            dg = _window_sum(ug, w, False) * _inv_count(ug.shape, w) - ug
            db = dg.astype(BF16)
            d_ref[:, gi * PG:(gi + 1) * PG] = db
            y_ref[:, gi * PO:(gi + 1) * PO] = jnp.dot(db, pw_ref[gi], preferred_element_type=F32).astype(BF16)

    return pl.pallas_call(
        body, name="pool_fwd", grid=(1,),
        in_specs=[pl.BlockSpec((T, 4 * PG), lambda i: (0, OU // (4 * PG))), pl.BlockSpec((4, PG, PO), lambda i: (0, 0, 0))],
        out_specs=[pl.BlockSpec((T, 4 * PG), lambda i: (0, 0)), pl.BlockSpec((T, D), lambda i: (0, 0))],
        out_shape=[SDS((T, 4 * PG), BF16), SDS((T, D), BF16)],
        compiler_params=_cparams(48 * 1024 * 1024, ("arbitrary",)),
    )(pcat, pw)


def pool_bwd(dylin, d, pw):
    def body(dy_ref, d_ref, pw_ref, du_ref, dpw_ref):
        for gi, w in enumerate(POOL_WINDOWS):
            dyl = dy_ref[:, gi * PO:(gi + 1) * PO]
            dd = lax.dot_general(dyl, pw_ref[gi], (NT, ((), ())), preferred_element_type=F32)
            du = _window_sum(dd * _inv_count(dd.shape, w), w, True) - dd
            du_ref[:, gi * PG:(gi + 1) * PG] = du.astype(BF16)
            dpw_ref[gi] = lax.dot_general(d_ref[:, gi * PG:(gi + 1) * PG], dyl, (TN, ((), ())),
                                          preferred_element_type=F32).astype(BF16)

    return pl.pallas_call(
        body, name="pool_bwd", grid=(1,),
        in_specs=[pl.BlockSpec((T, D), lambda i: (0, 0)), pl.BlockSpec((T, 4 * PG), lambda i: (0, 0)),
                  pl.BlockSpec((4, PG, PO), lambda i: (0, 0, 0))],
        out_specs=[pl.BlockSpec((T, 4 * PG), lambda i: (0, 0)), pl.BlockSpec((4, PG, PO), lambda i: (0, 0, 0))],
        out_shape=[SDS((T, 4 * PG), BF16), SDS((4, PG, PO), BF16)],
        compiler_params=_cparams(48 * 1024 * 1024, ("arbitrary",)),
    )(dylin, d, pw)


def _gate_decay(alow, wa, ba):
    a = jnp.dot(alow, wa, preferred_element_type=F32) + ba
    ls = jax.nn.log_sigmoid(a) * (1.0 / 16.0)
    r = lax.broadcasted_iota(jnp.int32, (CHUNK, CHUNK), 0)
    c = lax.broadcasted_iota(jnp.int32, (CHUNK, CHUNK), 1)
    tri = jnp.where(c <= r, 1.0, 0.0).astype(F32)
    cum = jnp.dot(tri, ls, preferred_element_type=F32, precision=lax.Precision.HIGHEST)
    last = cum[CHUNK - 1:CHUNK, :]
    return a, jnp.exp(last - cum), jnp.exp(last)


def gla_fwd(pcat, wa, ba, ng, after=None):
    afters = _as_list(after)

    def body(q_ref, k_ref, v_ref, g_ref, al_ref, wa_ref, ba_ref, ng_ref, *rest):
        og_ref, o_ref, st_ref, s_scr = rest[len(afters):]

        @pl.when(pl.program_id(0) == 0)
        def _():
            s_scr[...] = jnp.zeros_like(s_scr)

        state = [s_scr[h] for h in range(HEADS)]
        for s in range(GLA_STEP):
            rs = slice(s * CHUNK, (s + 1) * CHUNK)
            _, e, decay = _gate_decay(al_ref[rs, :], wa_ref[...], ba_ref[...])
            kd = (k_ref[rs, :].astype(F32) * e).astype(BF16)
            qs = (q_ref[rs, :].astype(F32) * (DK ** -0.5)).astype(BF16)
            for h in range(HEADS):
                ck = slice(h * DK, (h + 1) * DK)
                cv = slice(h * DV, (h + 1) * DV)
                state[h] = state[h] * decay[:, ck] + lax.dot_general(v_ref[rs, cv], kd[:, ck], (TN, ((), ())),
                                                                     preferred_element_type=F32)
                sb = state[h].astype(BF16)
                st_ref[s, h] = sb
                oh = lax.dot_general(qs[:, ck], sb, (NT, ((), ())), preferred_element_type=F32)
                o_ref[rs, cv] = oh.astype(BF16)
                on = oh * lax.rsqrt(jnp.mean(oh * oh, axis=-1, keepdims=True) + EPS) * ng_ref[:, cv]
                gv = g_ref[rs, cv].astype(F32)
                og_ref[rs, cv] = (on * (gv * _sigmoid(gv))).astype(BF16)
        for h in range(HEADS):
            s_scr[h] = state[h]

    row = lambda c: (c, 0)
    rows = GLA_STEP * CHUNK
    return pl.pallas_call(
        body, name="gla_fwd", grid=(NCHUNK // GLA_STEP,),
        in_specs=[pl.BlockSpec((rows, QK), lambda c: (c, OQ // QK)), pl.BlockSpec((rows, QK), lambda c: (c, OKK // QK)),
                  pl.BlockSpec((rows, D), lambda c: (c, OV // D)), pl.BlockSpec((rows, D), lambda c: (c, OG // D)),
                  pl.BlockSpec((rows, APAD), lambda c: (c, OA // APAD)),
                  pl.BlockSpec((APAD, QK), lambda c: (0, 0)), pl.BlockSpec((1, QK), lambda c: (0, 0)),
                  pl.BlockSpec((1, D), lambda c: (0, 0))] + [ANY] * len(afters),
        out_specs=[pl.BlockSpec((rows, D), row), pl.BlockSpec((rows, D), row),
                   pl.BlockSpec((GLA_STEP, HEADS, DV, DK), lambda c: (c, 0, 0, 0))],
        out_shape=[SDS((T, D), BF16), SDS((T, D), BF16), SDS((NCHUNK, HEADS, DV, DK), BF16)],
        scratch_shapes=[pltpu.VMEM((HEADS, DV, DK), F32)],
        compiler_params=_cparams(32 * 1024 * 1024, ("arbitrary",)),
    )(pcat, pcat, pcat, pcat, pcat, wa, ba, ng, *afters)


def gla_bwd(do, pcat, states, wa, ba, after):
    def body(do_ref, q_ref, k_ref, v_ref, al_ref, sc_ref, sp_ref, wa_ref, ba_ref, after_ref,
             dq_ref, dk_ref, dv_ref, dal_ref, dwa_ref, dba_ref, ds_scr):
        i = pl.program_id(0)

        @pl.when(i == 0)
        def _():
            ds_scr[...] = jnp.zeros_like(ds_scr)

        ds = [ds_scr[h] for h in range(HEADS)]
        dwa, dba = 0.0, 0.0
        for u in reversed(range(GLA_STEP)):
            rs = slice(u * CHUNK, (u + 1) * CHUNK)
            first_chunk = jnp.logical_and(i == NCHUNK // GLA_STEP - 1, u == 0)
            has_prev = jnp.where(first_chunk, 0.0, 1.0).astype(F32)
            a, e, decay = _gate_decay(al_ref[rs, :], wa_ref[...], ba_ref[...])
            kdf = k_ref[rs, :].astype(F32) * e
            kd = kdf.astype(BF16)
            qs = (q_ref[rs, :].astype(F32) * (DK ** -0.5)).astype(BF16)
            dkd_parts, ddecay_parts = [], []
            for h in range(HEADS):
                ck = slice(h * DK, (h + 1) * DK)
                cv = slice(h * DV, (h + 1) * DV)
                doh = do_ref[rs, cv]
                dsh = ds[h] + lax.dot_general(doh, qs[:, ck], (TN, ((), ())), preferred_element_type=F32)
                dsb = dsh.astype(BF16)
                dq_ref[rs, ck] = (jnp.dot(doh, sc_ref[u, h], preferred_element_type=F32) * (DK ** -0.5)).astype(BF16)
                dkd_parts.append(jnp.dot(v_ref[rs, cv], dsb, preferred_element_type=F32))
                dv_ref[rs, cv] = lax.dot_general(kd[:, ck], dsb, (NT, ((), ())), preferred_element_type=F32).astype(BF16)
                s_prev = (sp_ref[h] if u == 0 else sc_ref[u - 1, h]).astype(F32)
                ddecay_parts.append(jnp.sum(dsh * s_prev, axis=0, keepdims=True) * has_prev)
                ds[h] = dsh * decay[:, ck]
            dkd = jnp.concatenate(dkd_parts, axis=1)
            ddecay = jnp.concatenate(ddecay_parts, axis=1)
            dk_ref[rs, :] = (dkd * e).astype(BF16)
            dearg = dkd * kdf
            dlast = jnp.sum(dearg, axis=0, keepdims=True) + ddecay * decay
            r = lax.broadcasted_iota(jnp.int32, (CHUNK, CHUNK), 0)
            c = lax.broadcasted_iota(jnp.int32, (CHUNK, CHUNK), 1)
            triu = jnp.where(c >= r, 1.0, 0.0).astype(F32)
            dls = dlast - jnp.dot(triu, dearg, preferred_element_type=F32, precision=lax.Precision.HIGHEST)
            da = dls * (1.0 / 16.0) * (1.0 - _sigmoid(a))
            dab = da.astype(BF16)
            dal_ref[rs, :] = lax.dot_general(dab, wa_ref[...], (NT, ((), ())), preferred_element_type=F32).astype(BF16)
            dwa = dwa + lax.dot_general(al_ref[rs, :], dab, (TN, ((), ())), preferred_element_type=F32)
            dba = dba + jnp.sum(da, axis=0, keepdims=True)
        for h in range(HEADS):
            ds_scr[h] = ds[h]

        @pl.when(i == 0)
        def _():
            dwa_ref[...] = dwa
            dba_ref[...] = dba

        @pl.when(i > 0)
        def _():
            dwa_ref[...] += dwa
            dba_ref[...] += dba

    rows = GLA_STEP * CHUNK
    rev = lambda i: NCHUNK // GLA_STEP - 1 - i
    return pl.pallas_call(
        body, name="gla_bwd", grid=(NCHUNK // GLA_STEP,),
        in_specs=[pl.BlockSpec((rows, D), lambda i: (rev(i), 0)),
                  pl.BlockSpec((rows, QK), lambda i: (rev(i), OQ // QK)), pl.BlockSpec((rows, QK), lambda i: (rev(i), OKK // QK)),
                  pl.BlockSpec((rows, D), lambda i: (rev(i), OV // D)), pl.BlockSpec((rows, APAD), lambda i: (rev(i), OA // APAD)),
                  pl.BlockSpec((GLA_STEP, HEADS, DV, DK), lambda i: (rev(i), 0, 0, 0)),
                  pl.BlockSpec((None, HEADS, DV, DK), lambda i: (jnp.maximum(rev(i) * GLA_STEP - 1, 0), 0, 0, 0)),
                  pl.BlockSpec((APAD, QK), lambda i: (0, 0)), pl.BlockSpec((1, QK), lambda i: (0, 0)), ANY],
        out_specs=[pl.BlockSpec((rows, QK), lambda i: (rev(i), 0)), pl.BlockSpec((rows, QK), lambda i: (rev(i), 0)),
                   pl.BlockSpec((rows, D), lambda i: (rev(i), 0)), pl.BlockSpec((rows, APAD), lambda i: (rev(i), 0)),
                   pl.BlockSpec((APAD, QK), lambda i: (0, 0)), pl.BlockSpec((1, QK), lambda i: (0, 0))],
        out_shape=[SDS((T, QK), BF16), SDS((T, QK), BF16), SDS((T, D), BF16), SDS((T, APAD), BF16),
                   SDS((APAD, QK), F32), SDS((1, QK), F32)],
        scratch_shapes=[pltpu.VMEM((HEADS, DV, DK), F32)],
        compiler_params=_cparams(32 * 1024 * 1024, ("arbitrary",)),
    )(do, pcat, pcat, pcat, pcat, states, states, wa, ba, after)


TMF = 256
TMW = 512
_rowblk = ((TMF, D), lambda j, i, k: (i, 0))
_vec = ((1, D), lambda j, i, k: (0, 0))


def _full_spec(col):
    return ((TMF, D), lambda j, i, k: (i, col))


TBIG = 1024


def square_matmul(name, a, b, *, a_spec, b_spec, cdims, nk, after=None):
    def epi(acc, ex, outs, i):
        outs[0][...] = acc

    return matmul(name, a, b, a_spec=a_spec, b_spec=b_spec, cdims=cdims, grid=(D // TBIG, T // TBIG, nk),
                  acc_shape=(TBIG, TBIG), outs=[((T, D), F32, (TBIG, TBIG), lambda j, i, k: (i, j))], epi=epi,
                  after=after)[0]


def rowwise(name, y, *, extras, outs, epi):
    ne = len(extras)

    def body(*refs):
        epi(refs[0][...], refs[1:1 + ne], refs[1 + ne:], pl.program_id(1))

    in_specs = [pl.BlockSpec(*_rowblk)] + [pl.BlockSpec(bs, im) for _, bs, im in extras]
    return pl.pallas_call(
        body, name=name, grid=(1, T // TMF, 1), in_specs=in_specs,
        out_specs=[pl.BlockSpec(bs, im) for _, _, bs, im in outs], out_shape=[SDS(s, dt) for s, dt, _, _ in outs],
        compiler_params=_cparams(40 * 1024 * 1024, ("arbitrary", "arbitrary", "arbitrary")),
    )(y, *[arr for arr, _, _ in extras])


def mm_gla_out(og, w, ylin, pcat, pscale):
    def epi(acc, ex, outs, i):
        ylin_ref, lgp_ref, lgg_ref, ps_ref = ex
        for c0 in range(0, D, EPI_COLS):
            cs = slice(c0, c0 + EPI_COLS)
            gp = _sigmoid(lgp_ref[:, cs].astype(F32))
            gg = _sigmoid(lgg_ref[:, cs].astype(F32))
            a = acc[:, cs]
            outs[0][:, cs] = (gp * (ylin_ref[:, cs].astype(F32) * ps_ref[:, cs]) + gg * a).astype(BF16)
            outs[1][:, cs] = a.astype(BF16)

    return matmul("mm_gla_out", og, w, a_spec=_rowblk, b_spec=((D, D), lambda j, i, k: (0, 0)), cdims=NN,
                  grid=(1, T // TMF, 1), acc_shape=(TMF, D),
                  extras=[(ylin, *_rowblk), (pcat, *_full_spec(OGP // D)), (pcat, *_full_spec(OGG // D)), (pscale, *_vec)],
                  outs=[((T, D), BF16, *_rowblk), ((T, D), BF16, *_rowblk)], epi=epi)


def mm_out(mixed, w, x, g2):
    def epi(acc, ex, outs, i):
        x_ref, g_ref = ex
        x2 = x_ref[...] + acc
        r = lax.rsqrt(jnp.mean(x2 * x2, axis=-1, keepdims=True) + EPS)
        outs[0][...] = x2
        outs[1][...] = (x2 * r * g_ref[...]).astype(BF16)

    return matmul("mm_out", mixed, w, a_spec=_rowblk, b_spec=((D, D), lambda j, i, k: (0, 0)), cdims=NN,
                  grid=(1, T // TMF, 1), acc_shape=(TMF, D), extras=[(x, *_rowblk), (g2, *_vec)],
                  outs=[((T, D), F32, *_rowblk), ((T, D), BF16, *_rowblk)], epi=epi)


def mm_up(h2, wup):
    def epi(acc, ex, outs, i):
        r = jnp.maximum(acc, 0.0)
        outs[0][...] = r.astype(BF16)
        outs[1][...] = (r * r).astype(BF16)

    blk = ((TMW, D), lambda j, i, k: (i, j))
    return matmul("mm_up", h2, wup, a_spec=((TMW, D), lambda j, i, k: (i, 0)), b_spec=((None, D, D), lambda j, i, k: (j, 0, 0)),
                  cdims=NN, grid=(NCHIP, T // TMW, 1), acc_shape=(TMW, D),
                  outs=[((T, DFF), BF16, *blk), ((T, DFF), BF16, *blk)], epi=epi)


def mm_down(act, wdown, x2, tgt, gf):
    tk = 4096

    def epi(acc, ex, outs, i):
        x2_ref, t_ref, g_ref = ex
        dx_ref, dxb_ref, gnf_ref, loss_ref = outs
        x3 = x2_ref[...] + acc
        r = lax.rsqrt(jnp.mean(x3 * x3, axis=-1, keepdims=True) + EPS)
        xn = x3 * r
        err = xn * g_ref[...] - t_ref[...]
        lsum = 0.5 * jnp.sum(jnp.mean(err * err, axis=-1, keepdims=True), axis=0, keepdims=True)
        dy = err * (1.0 / D)
        _row_acc(gnf_ref, jnp.sum(dy * xn, axis=0, keepdims=True), i)
        _row_acc(loss_ref, jnp.broadcast_to(lsum, (1, 128)), i)
        dx3 = _rms_bwd(xn, r, dy * g_ref[...])
        dx_ref[...] = dx3
        dxb_ref[...] = dx3.astype(BF16)

    y = square_matmul("mm_down", act, wdown, a_spec=((TBIG, tk), lambda j, i, k: (i, k)),
                      b_spec=((tk, TBIG), lambda j, i, k: (k, j)), cdims=NN, nk=DFF // tk)
    return rowwise("rows_final", y, extras=[(x2, *_rowblk), (tgt, *_rowblk), (gf, *_vec)],
                   outs=[((T, D), F32, *_rowblk), ((T, D), BF16, *_rowblk), ((1, D), F32, *_vec),
                         ((1, 128), F32, (1, 128), lambda j, i, k: (0, 0))], epi=epi)


def mm_dact(dx3b, wdown, rup, after=None):
    def epi(acc, ex, outs, i):
        outs[0][...] = (acc * 2.0 * ex[0][...].astype(F32)).astype(BF16)

    blk = ((TMW, D), lambda j, i, k: (i, j))
    return matmul("mm_dact", dx3b, wdown, a_spec=((TMW, D), lambda j, i, k: (i, 0)), b_spec=((D, D), lambda j, i, k: (j, 0)),
                  cdims=NT, grid=(DFF // D, T // TMW, 1), acc_shape=(TMW, D), extras=[(rup, *blk)],
                  outs=[((T, DFF), BF16, *blk)], epi=epi, after=after)[0]


def mm_wgrad(name, a, b, m, n, out_shape, out_block, out_map, tm, tn, after=None):
    def epi(acc, ex, outs, i):
        outs[0][...] = acc.astype(BF16).reshape(outs[0].shape)

    return matmul(name, a, b, a_spec=((T, tm), lambda j, i, k: (0, i)), b_spec=((T, tn), lambda j, i, k: (0, j)),
                  cdims=TN, grid=(n // tn, m // tm, 1), acc_shape=(tm, tn),
                  outs=[(out_shape, BF16, out_block, out_map)], epi=epi, after=after)[0]


def mm_dh2(dup, wup, x2, dx3, g2, after=None):
    def epi(acc, ex, outs, i):
        x2_ref, dx3_ref, g_ref = ex
        x2 = x2_ref[...]
        r = lax.rsqrt(jnp.mean(x2 * x2, axis=-1, keepdims=True) + EPS)
        xn = x2 * r
        _row_acc(outs[2], jnp.sum(acc * xn, axis=0, keepdims=True), i)
        dx2 = dx3_ref[...] + _rms_bwd(xn, r, acc * g_ref[...])
        outs[0][...] = dx2
        outs[1][...] = dx2.astype(BF16)

    y = square_matmul("mm_dh2", dup, wup, a_spec=((TBIG, D), lambda j, i, k: (i, k)),
                      b_spec=((None, TBIG, D), lambda j, i, k: (k, j, 0)), cdims=NT, nk=NCHIP, after=after)
    return rowwise("rows_dh2", y, extras=[(x2, *_rowblk), (dx3, *_rowblk), (g2, *_vec)],
                   outs=[((T, D), F32, *_rowblk), ((T, D), BF16, *_rowblk), ((1, D), F32, *_vec)], epi=epi)


def mm_dmixed(dx2b, wout, pcat, ylin, ygla, pscale, after=None):
    def epi(acc, ex, outs, i):
        lgp_ref, lgg_ref, ylin_ref, ygla_ref, ps_ref = ex
        dps = []
        for c0 in range(0, D, EPI_COLS):
            cs = slice(c0, c0 + EPI_COLS)
            gp = _sigmoid(lgp_ref[:, cs].astype(F32))
            gg = _sigmoid(lgg_ref[:, cs].astype(F32))
            yl = ylin_ref[:, cs].astype(F32)
            ps = ps_ref[:, cs]
            a = acc[:, cs]
            agp = a * gp
            outs[0][:, cs] = (agp * ps).astype(BF16)
            outs[1][:, cs] = (a * gg).astype(BF16)
            outs[2][:, cs] = (agp * (yl * ps) * (1.0 - gp)).astype(BF16)
            outs[3][:, cs] = (a * ygla_ref[:, cs].astype(F32) * gg * (1.0 - gg)).astype(BF16)
            dps.append(jnp.sum(agp * yl, axis=0, keepdims=True))
        _row_acc(outs[4], jnp.concatenate(dps, axis=1), i)

    return matmul("mm_dmixed", dx2b, wout, a_spec=_rowblk, b_spec=((D, D), lambda j, i, k: (0, 0)), cdims=NT,
                  grid=(1, T // TMF, 1), acc_shape=(TMF, D),
                  extras=[(pcat, *_full_spec(OGP // D)), (pcat, *_full_spec(OGG // D)), (ylin, *_rowblk), (ygla, *_rowblk),
                          (pscale, *_vec)],
                  outs=[((T, D), BF16, *_rowblk)] * 4 + [((1, D), F32, *_vec)], epi=epi, after=after)


def mm_dog(dygla, wgo, o, pcat, ng, after=None):
    def epi(acc, ex, outs, i):
        o_ref, g_ref, ng_ref = ex
        do_ref, dg_ref, gng_ref = outs
        gparts = []
        for h in range(HEADS):
            cv = slice(h * DV, (h + 1) * DV)
            oh = o_ref[:, cv].astype(F32)
            r = lax.rsqrt(jnp.mean(oh * oh, axis=-1, keepdims=True) + EPS)
            on = oh * r
            gv = g_ref[:, cv].astype(F32)
            sg = _sigmoid(gv)
            a = acc[:, cv]
            dgain = a * (gv * sg)
            gparts.append(jnp.sum(dgain * on, axis=0, keepdims=True))
            ngh = ng_ref[:, cv]
            do_ref[:, cv] = _rms_bwd(on, r, dgain * ngh).astype(BF16)
            dg_ref[:, cv] = (a * (on * ngh) * (sg * (1.0 + gv * (1.0 - sg)))).astype(BF16)
        _row_acc(gng_ref, jnp.concatenate(gparts, axis=1), i)

    return matmul("mm_dog", dygla, wgo, a_spec=_rowblk, b_spec=((D, D), lambda j, i, k: (0, 0)), cdims=NT,
                  grid=(1, T // TMF, 1), acc_shape=(TMF, D),
                  extras=[(o, *_rowblk), (pcat, *_full_spec(OG // D)), (ng, *_vec)],
                  outs=[((T, D), BF16, *_rowblk), ((T, D), BF16, *_rowblk), ((1, D), F32, *_vec)], epi=epi, after=after)


def mm_dh1(dpcat, wcat, x, dx2, g1, after=None):
    tk = 3840

    def epi(acc, ex, outs, i):
        x_ref, dx2_ref, g_ref = ex
        xv = x_ref[...]
        r = lax.rsqrt(jnp.mean(xv * xv, axis=-1, keepdims=True) + EPS)
        xn = xv * r
        _row_acc(outs[1], jnp.sum(acc * xn, axis=0, keepdims=True), i)
        outs[0][...] = dx2_ref[...] + _rms_bwd(xn, r, acc * g_ref[...])

    y = square_matmul("mm_dh1", dpcat, wcat, a_spec=((TBIG, tk), lambda j, i, k: (i, k)),
                      b_spec=((TBIG, tk), lambda j, i, k: (j, k)), cdims=NT, nk=NCAT // tk, after=after)
    return rowwise("rows_dh1", y, extras=[(x, *_rowblk), (dx2, *_rowblk), (g1, *_vec)],
                   outs=[((T, D), F32, *_rowblk), ((1, D), F32, *_vec)], epi=epi)


def _tile_rows(rows, cols, n_arrays):
    tm = rows
    while tm % 32 == 0 and 2 * n_arrays * tm * cols * 4 > 36 * 1024 * 1024:
        tm //= 2
    return tm


def add_pairs(name, parts, theirs, core):
    _, _, r, c = parts.shape
    tm = _tile_rows(r, c, 3)

    def body(core_ref, a_ref, b_ref, o_ref):
        o_ref[...] = (a_ref[...].astype(F32) + b_ref[...].astype(F32)).astype(BF16)

    spec = pl.BlockSpec((None, tm, c), lambda j, i, core_ref: (j, i, 0))
    grid_spec = pltpu.PrefetchScalarGridSpec(
        num_scalar_prefetch=1, grid=(NCHIP, r // tm),
        in_specs=[pl.BlockSpec((None, None, tm, c), lambda j, i, core_ref: (core_ref[0], j, i, 0)), spec], out_specs=spec)
    return pl.pallas_call(body, name=name, grid_spec=grid_spec, out_shape=SDS((NCHIP, r, c), BF16),
                          compiler_params=_cparams(40 * 1024 * 1024, ("arbitrary", "arbitrary")))(core, parts, theirs)


def sum_chips(name, sums, landed, chip):
    _, r, c = sums.shape
    tm = _tile_rows(r, c, 4)

    def body(chip_ref, own_ref, l_ref, o_ref):
        s = own_ref[...].astype(F32)
        for t in range(NCHIP - 1):
            s = s + l_ref[t].astype(F32)
        o_ref[...] = s

    grid_spec = pltpu.PrefetchScalarGridSpec(
        num_scalar_prefetch=1, grid=(r // tm,),
        in_specs=[pl.BlockSpec((None, tm, c), lambda i, chip_ref: (chip_ref[0], i, 0)),
                  pl.BlockSpec((NCHIP - 1, tm, c), lambda i, chip_ref: (0, i, 0))],
        out_specs=pl.BlockSpec((tm, c), lambda i, chip_ref: (i, 0)))
    return pl.pallas_call(body, name=name, grid_spec=grid_spec, out_shape=SDS((r, c), F32),
                          compiler_params=_cparams(40 * 1024 * 1024, ("arbitrary",)))(chip, sums, landed)


def _adamw_math(wv, gv, mv, vv):
    mn = ADAM_B1 * mv + (1.0 - ADAM_B1) * gv
    vn = ADAM_B2 * vv + (1.0 - ADAM_B2) * (gv * gv)
    mh = mn / (1.0 - ADAM_B1 ** ADAM_STEP)
    vh = vn / (1.0 - ADAM_B2 ** ADAM_STEP)
    return -ADAM_LR * (mh / (jnp.sqrt(vh) + ADAM_EPS) + ADAM_WD * wv), mn, vn


def adamw(name, w, g, m, v):
    def body(w_ref, g_ref, m_ref, v_ref, go_ref, d_ref, mo_ref, vo_ref):
        gv = g_ref[...]
        go_ref[...] = gv
        d_ref[...], mo_ref[...], vo_ref[...] = _adamw_math(w_ref[...], gv, m_ref[...], v_ref[...])

    return pl.pallas_call(body, name=name, out_shape=[SDS(w.shape, F32)] * 4)(w, g, m, v)


def adamw_halves(name, w, g_own, g_sib, m, v, core):
    _, r, c = w.shape
    tm = _tile_rows(r, c, 10)

    def body(core_ref, w_ref, go_ref, gs_ref, m_ref, v_ref, g_out, d_out, m_out, v_out):
        gv = jnp.where(pl.program_id(0) == core_ref[0], go_ref[...], gs_ref[...])
        g_out[...] = gv
        d_out[...], m_out[...], v_out[...] = _adamw_math(w_ref[...], gv, m_ref[...], v_ref[...])

    full = pl.BlockSpec((None, tm, c), lambda h, i, core_ref: (h, i, 0))
    own = pl.BlockSpec((tm, c), lambda h, i, core_ref: (jnp.where(h == core_ref[0], i, 0), 0))
    sib = pl.BlockSpec((tm, c), lambda h, i, core_ref: (jnp.where(h == core_ref[0], 0, i), 0))
    grid_spec = pltpu.PrefetchScalarGridSpec(num_scalar_prefetch=1, grid=(2, r // tm),
                                             in_specs=[full, own, sib, full, full], out_specs=[full] * 4)
    return pl.pallas_call(body, name=name, grid_spec=grid_spec, out_shape=[SDS(w.shape, F32)] * 4,
                          compiler_params=_cparams(48 * 1024 * 1024, ("arbitrary", "arbitrary")))(core, w, g_own, g_sib, m, v)


def cast_bf16(name, w):
    _, r, c = w.shape
    tm = _tile_rows(r, c, 2)

    def body(w_ref, o_ref):
        o_ref[...] = w_ref[...].astype(BF16)

    spec = pl.BlockSpec((None, tm, c), lambda h, i: (h, i, 0))
    return pl.pallas_call(body, name=name, grid=(2, r // tm), in_specs=[spec], out_specs=spec, out_shape=SDS(w.shape, BF16),
                          compiler_params=_cparams(40 * 1024 * 1024, ("arbitrary", "arbitrary")))(w)


def pack_rows(name, parts, rows, after=None):
    width = parts[0].shape[1]
    n = len(parts)
    afters = _as_list(after)

    def body(*refs):
        out_ref = refs[n + len(afters)]
        out_ref[...] = jnp.zeros_like(out_ref)
        off = 0
        for p in refs[:n]:
            out_ref[off:off + p.shape[0], :] = p[...]
            off += p.shape[0]

    vm = pl.BlockSpec(memory_space=pltpu.VMEM)
    return pl.pallas_call(body, name=name, in_specs=[vm] * n + [ANY] * len(afters), out_specs=vm,
                          out_shape=SDS((rows, width), F32))(*parts, *afters)


def _place():
    x, y, c = lax.axis_index("x"), lax.axis_index("y"), lax.axis_index("c")
    chips = [(1 - x, y), (x, 1 - y), (1 - x, 1 - y)]
    return x, y, c, chips


def _row_split(shape, dtype):
    r, c = shape
    n = 1
    while r % (2 * n) == 0 and (r // (2 * n)) % 16 == 0 and (r // n) * c * jnp.dtype(dtype).itemsize > PIECE_BYTES:
        n *= 2
    return [pl.ds(s * (r // n), r // n) for s in range(n)]


def _pieces(ref):
    *lead, r, c = ref.shape
    split = _row_split((r, c), ref.dtype)
    return [ref.at[(*idx, s)] for idx in itertools.product(*[range(d) for d in lead]) for s in split]


HBM = pl.BlockSpec(memory_space=pltpu.HBM)
SEM = pl.BlockSpec(memory_space=pltpu.SEMAPHORE)
EFFECT = pltpu.SideEffectType.DATAFLOW_SIDE_EFFECTING


def gather_start(name, shards, after=None):
    n = len(shards)
    afters = _as_list(after)

    def body(*refs):
        src, land = refs[:n], refs[n:2 * n]
        send, recv = refs[2 * n + len(afters)], refs[2 * n + len(afters) + 1]
        x, y, c, chips = _place()
        me = 2 * x + y
        for a in range(n):
            for j, (cx, cy) in enumerate(chips[:2]):
                for sp, dp in zip(_pieces(src[a].at[c]), _pieces(land[a].at[me, c])):
                    pltpu.make_async_remote_copy(sp, dp, send.at[2 * a + j], recv.at[2 * a + j],
                                                 device_id=(cx, cy, c), device_id_type=MESH).start()

    lands = [pltpu.with_memory_space_constraint(lax.empty((NCHIP,) + s.shape, s.dtype), pltpu.HBM) for s in shards]
    srcs = [pltpu.with_memory_space_constraint(s, pltpu.HBM) for s in shards]
    outs = pl.pallas_call(
        body, name=name,
        out_shape=(pltpu.SemaphoreType.DMA((2 * n,)), pltpu.SemaphoreType.DMA((2 * n,)),
                   *[pltpu.HBM(s.shape, s.dtype) for s in shards], *[pltpu.HBM(l.shape, l.dtype) for l in lands]),
        in_specs=[HBM] * (2 * n) + [ANY] * len(afters), out_specs=(SEM, SEM, *([HBM] * (2 * n))),
        input_output_aliases={i: 2 + i for i in range(2 * n)},
        compiler_params=pltpu.CompilerParams(has_side_effects=EFFECT),
    )(*srcs, *lands, *afters)
    return outs[0], outs[1], list(outs[2:2 + n]), list(outs[2 + n:2 + 2 * n])


def _relay_blocks(land, c, chips):
    (xx, xy), (yx, yy), (dx, dy) = chips
    rows = land.shape[2] // 2
    upper, lower = pl.ds(0, rows), pl.ds(rows, rows)
    return [(land.at[2 * yx + yy, c, lower], land.at[2 * dx + dy, c, lower]),
            (land.at[2 * xx + xy, c, upper], land.at[2 * dx + dy, c, upper])]


def relay_turn(name, send, recv, shards, lands, after):
    n = len(shards)
    afters = _as_list(after)

    def body(*refs):
        src, had = refs[:n], refs[n:2 * n]
        send_ref, recv_ref = refs[2 * n], refs[2 * n + 1]
        rsend, rrecv = refs[2 * n + 2 + len(afters)], refs[2 * n + 3 + len(afters)]
        land = refs[3 * n + 4 + len(afters):4 * n + 4 + len(afters)]
        x, y, c, chips = _place()
        for a in range(n):
            for j, (cx, cy) in enumerate(chips[:2]):
                cp = pltpu.make_async_remote_copy(src[a].at[c], had[a].at[2 * cx + cy, c], send_ref.at[2 * a + j],
                                                  recv_ref.at[2 * a + j], device_id=(cx, cy, c), device_id_type=MESH)
                cp.wait_send()
                cp.wait_recv()
        for a in range(n):
            for j, ((sent, _), (dst, _)) in enumerate(zip(_relay_blocks(had[a], c, chips), _relay_blocks(land[a], c, chips))):
                cx, cy = chips[j]
                for sp, dp in zip(_pieces(sent), _pieces(dst)):
                    pltpu.make_async_remote_copy(sp, dp, rsend.at[2 * a + j], rrecv.at[2 * a + j],
                                                 device_id=(cx, cy, c), device_id_type=MESH).start()

    outs = pl.pallas_call(
        body, name=name,
        out_shape=(pltpu.SemaphoreType.DMA((2 * n,)), pltpu.SemaphoreType.DMA((2 * n,)),
                   *[pltpu.HBM(s.shape, s.dtype) for s in shards], *[pltpu.HBM(l.shape, l.dtype) for l in lands]),
        in_specs=[HBM] * (2 * n) + [SEM, SEM] + [ANY] * len(afters), out_specs=(SEM, SEM, *([HBM] * (2 * n))),
        input_output_aliases={i: 2 + i for i in range(2 * n)},
        compiler_params=pltpu.CompilerParams(has_side_effects=EFFECT),
    )(*shards, *lands, send, recv, *afters)
    return outs[0], outs[1], list(outs[2:2 + n]), list(outs[2 + n:2 + 2 * n])


def relay_wait(name, send, recv, lands, after):
    n = len(lands)
    afters = _as_list(after)

    def body(*refs):
        land = refs[:n]
        send_ref, recv_ref = refs[n], refs[n + 1]
        x, y, c, chips = _place()
        for a in range(n):
            for j, (sent, got) in enumerate(_relay_blocks(land[a], c, chips)):
                cx, cy = chips[j]
                cp = pltpu.make_async_remote_copy(sent, got, send_ref.at[2 * a + j], recv_ref.at[2 * a + j],
                                                  device_id=(cx, cy, c), device_id_type=MESH)
                cp.wait_send()
                cp.wait_recv()

    outs = pl.pallas_call(
        body, name=name, out_shape=tuple(pltpu.HBM(l.shape, l.dtype) for l in lands),
        in_specs=[HBM] * n + [SEM, SEM] + [ANY] * len(afters), out_specs=[HBM] * n,
        input_output_aliases={i: i for i in range(n)},
        compiler_params=pltpu.CompilerParams(has_side_effects=EFFECT),
    )(*lands, send, recv, *afters)
    return list(outs)


def forward_halves(name, shards, lands):
    n = len(lands)

    def body(*refs):
        had, buf = refs[:n], refs[n:2 * n]
        send, recv = refs[2 * n:]
        x, y, c, chips = _place()
        sib = (x, y, 1 - c)
        for a in range(n):
            for j, (cx, cy) in enumerate(chips):
                for sp, dp in zip(_pieces(had[a].at[2 * cx + cy, c]), _pieces(buf[a].at[2 * cx + cy, c])):
                    pltpu.make_async_remote_copy(sp, dp, send.at[3 * a + j], recv.at[3 * a + j], device_id=sib, device_id_type=MESH).start()
        for a in range(n):
            for j, (cx, cy) in enumerate(chips):
                pltpu.make_async_remote_copy(had[a].at[2 * cx + cy, c], buf[a].at[2 * cx + cy, 1 - c], send.at[3 * a + j],
                                             recv.at[3 * a + j], device_id=sib, device_id_type=MESH).wait()

    got = pl.pallas_call(
        body, name=name, in_specs=[ANY] * n, out_specs=[ANY] * n, out_shape=[SDS(l.shape, l.dtype) for l in lands],
        input_output_aliases={i: i for i in range(n)},
        scratch_shapes=[pltpu.SemaphoreType.DMA((3 * n,)), pltpu.SemaphoreType.DMA((3 * n,))],
    )(*lands)
    me = 2 * lax.axis_index("x") + lax.axis_index("y")
    return [lax.dynamic_update_index_in_dim(g, s, me, 0) for g, s in zip(got, shards)]


def forward_turn(name, send, recv, lands, after):
    n = len(lands)
    afters = _as_list(after)

    def body(*refs):
        had = refs[:n]
        send_ref, recv_ref = refs[n], refs[n + 1]
        fsend, frecv = refs[n + 2 + len(afters)], refs[n + 3 + len(afters)]
        buf = refs[n + 4 + len(afters):2 * n + 4 + len(afters)]
        x, y, c, chips = _place()
        sib = (x, y, 1 - c)
        for a in range(n):
            for j, (sent, got) in enumerate(_relay_blocks(had[a], c, chips)):
                cx, cy = chips[j]
                cp = pltpu.make_async_remote_copy(sent, got, send_ref.at[2 * a + j], recv_ref.at[2 * a + j],
                                                  device_id=(cx, cy, c), device_id_type=MESH)
                cp.wait_send()
                cp.wait_recv()
        for a in range(n):
            for j, (cx, cy) in enumerate(chips):
                for sp, dp in zip(_pieces(had[a].at[2 * cx + cy, c]), _pieces(buf[a].at[2 * cx + cy, c])):
                    pltpu.make_async_remote_copy(sp, dp, fsend.at[3 * a + j], frecv.at[3 * a + j], device_id=sib, device_id_type=MESH).start()

    outs = pl.pallas_call(
        body, name=name,
        out_shape=(pltpu.SemaphoreType.DMA((3 * n,)), pltpu.SemaphoreType.DMA((3 * n,)), *[pltpu.HBM(l.shape, l.dtype) for l in lands]),
        in_specs=[HBM] * n + [SEM, SEM] + [ANY] * len(afters), out_specs=(SEM, SEM, *([HBM] * n)),
        input_output_aliases={i: 2 + i for i in range(n)},
        compiler_params=pltpu.CompilerParams(has_side_effects=EFFECT),
    )(*lands, send, recv, *afters)
    return outs[0], outs[1], list(outs[2:])


def forward_wait(name, send, recv, lands, after):
    n = len(lands)
    afters = _as_list(after)

    def body(*refs):
        land = refs[:n]
        send_ref, recv_ref = refs[n], refs[n + 1]
        x, y, c, chips = _place()
        sib = (x, y, 1 - c)
        for a in range(n):
            for j, (cx, cy) in enumerate(chips):
                cp = pltpu.make_async_remote_copy(land[a].at[2 * cx + cy, c], land[a].at[2 * cx + cy, 1 - c], send_ref.at[3 * a + j],
                                                  recv_ref.at[3 * a + j], device_id=sib, device_id_type=MESH)
                cp.wait_send()
                cp.wait_recv()

    outs = pl.pallas_call(
        body, name=name, out_shape=tuple(pltpu.HBM(l.shape, l.dtype) for l in lands),
        in_specs=[HBM] * n + [SEM, SEM] + [ANY] * len(afters), out_specs=[HBM] * n,
        input_output_aliases={i: i for i in range(n)},
        compiler_params=pltpu.CompilerParams(has_side_effects=EFFECT),
    )(*lands, send, recv, *afters)
    return list(outs)


def exchange_start(name, parts):
    n = len(parts)

    def body(*refs):
        src, got = refs[:n], refs[n:2 * n]
        send, recv = refs[2 * n], refs[2 * n + 1]
        token = refs[4 * n + 2]
        x, y, c, _ = _place()
        sib = (x, y, 1 - c)
        for a in range(n):
            for sp, dp in zip(_pieces(src[a].at[1 - c]), _pieces(got[a])):
                pltpu.make_async_remote_copy(sp, dp, send.at[a], recv.at[a], device_id=sib, device_id_type=MESH).start()
        token[...] = jnp.zeros_like(token)

    lands = [pltpu.with_memory_space_constraint(lax.empty(p.shape[1:], p.dtype), pltpu.HBM) for p in parts]
    srcs = [pltpu.with_memory_space_constraint(p, pltpu.HBM) for p in parts]
    outs = pl.pallas_call(
        body, name=name,
        out_shape=(pltpu.SemaphoreType.DMA((n,)), pltpu.SemaphoreType.DMA((n,)),
                   *[pltpu.HBM(p.shape, p.dtype) for p in parts], *[pltpu.HBM(l.shape, l.dtype) for l in lands],
                   SDS((8, 128), F32)),
        in_specs=[HBM] * (2 * n), out_specs=(SEM, SEM, *([HBM] * (2 * n)), pl.BlockSpec(memory_space=pltpu.VMEM)),
        input_output_aliases={i: 2 + i for i in range(2 * n)},
        compiler_params=pltpu.CompilerParams(has_side_effects=EFFECT),
    )(*srcs, *lands)
    return outs[0], outs[1], list(outs[2:2 + n]), list(outs[2 + n:2 + 2 * n]), outs[2 + 2 * n]


def exchange_wait(name, send, recv, parts, lands, after):
    n = len(parts)
    afters = _as_list(after)

    def body(*refs):
        src, got = refs[:n], refs[n:2 * n]
        send_ref, recv_ref = refs[2 * n], refs[2 * n + 1]
        x, y, c, _ = _place()
        sib = (x, y, 1 - c)
        for a in range(n):
            cp = pltpu.make_async_remote_copy(src[a].at[1 - c], got[a], send_ref.at[a], recv_ref.at[a], device_id=sib, device_id_type=MESH)
            cp.wait_send()
            cp.wait_recv()

    outs = pl.pallas_call(
        body, name=name,
        out_shape=(*[pltpu.HBM(p.shape, p.dtype) for p in parts], *[pltpu.HBM(l.shape, l.dtype) for l in lands]),
        in_specs=[HBM] * (2 * n) + [SEM, SEM] + [ANY] * len(afters), out_specs=[HBM] * (2 * n),
        input_output_aliases={i: i for i in range(2 * n)},
        compiler_params=pltpu.CompilerParams(has_side_effects=EFFECT),
    )(*parts, *lands, send, recv, *afters)
    return list(outs[:n]), list(outs[n:])


def scatter_start(name, parts):
    n = len(parts)

    def body(*refs):
        src, land = refs[:n], refs[n:2 * n]
        send, recv = refs[2 * n], refs[2 * n + 1]
        token = refs[4 * n + 2]
        x, y, c, chips = _place()
        for a in range(n):
            for j, (cx, cy) in enumerate(chips):
                for sp, dp in zip(_pieces(src[a].at[2 * cx + cy]), _pieces(land[a].at[j])):
                    pltpu.make_async_remote_copy(sp, dp, send.at[3 * a + j], recv.at[3 * a + j],
                                                 device_id=(cx, cy, c), device_id_type=MESH).start()
        token[...] = jnp.zeros_like(token)

    lands = [pltpu.with_memory_space_constraint(lax.empty((NCHIP - 1,) + p.shape[1:], p.dtype), pltpu.HBM) for p in parts]
    srcs = [pltpu.with_memory_space_constraint(p, pltpu.HBM) for p in parts]
    outs = pl.pallas_call(
        body, name=name,
        out_shape=(pltpu.SemaphoreType.DMA((3 * n,)), pltpu.SemaphoreType.DMA((3 * n,)),
                   *[pltpu.HBM(p.shape, p.dtype) for p in parts], *[pltpu.HBM(l.shape, l.dtype) for l in lands],
                   SDS((8, 128), F32)),
        in_specs=[HBM] * (2 * n), out_specs=(SEM, SEM, *([HBM] * (2 * n)), pl.BlockSpec(memory_space=pltpu.VMEM)),
        input_output_aliases={i: 2 + i for i in range(2 * n)},
        compiler_params=pltpu.CompilerParams(has_side_effects=EFFECT),
    )(*srcs, *lands)
    return outs[0], outs[1], list(outs[2:2 + n]), list(outs[2 + n:2 + 2 * n]), outs[2 + 2 * n]


def scatter_wait(name, send, recv, parts, lands, after):
    n = len(parts)
    afters = _as_list(after)

    def body(*refs):
        src, land = refs[:n], refs[n:2 * n]
        send_ref, recv_ref = refs[2 * n], refs[2 * n + 1]
        x, y, c, chips = _place()
        for a in range(n):
            for j, (cx, cy) in enumerate(chips):
                cp = pltpu.make_async_remote_copy(src[a].at[2 * cx + cy], land[a].at[j], send_ref.at[3 * a + j], recv_ref.at[3 * a + j],
                                                  device_id=(cx, cy, c), device_id_type=MESH)
                cp.wait_send()
                cp.wait_recv()

    outs = pl.pallas_call(
        body, name=name,
        out_shape=(*[pltpu.HBM(p.shape, p.dtype) for p in parts], *[pltpu.HBM(l.shape, l.dtype) for l in lands]),
        in_specs=[HBM] * (2 * n) + [SEM, SEM] + [ANY] * len(afters), out_specs=[HBM] * (2 * n),
        input_output_aliases={i: i for i in range(2 * n)},
        compiler_params=pltpu.CompilerParams(has_side_effects=EFFECT),
    )(*parts, *lands, send, recv, *afters)
    return list(outs[:n]), list(outs[n:])


def join_start(name, halves):
    n = len(halves)

    def body(*refs):
        src, dst = refs[:n], refs[n:2 * n]
        send, recv = refs[2 * n], refs[2 * n + 1]
        token = refs[4 * n + 2]
        x, y, c, _ = _place()
        sib = (x, y, 1 - c)
        for a in range(n):
            for sp, dp in zip(_pieces(src[a]), _pieces(dst[a])):
                pltpu.make_async_remote_copy(sp, dp, send.at[a], recv.at[a], device_id=sib, device_id_type=MESH).start()
        token[...] = jnp.zeros_like(token)

    lands = [pltpu.with_memory_space_constraint(lax.empty(h.shape, h.dtype), pltpu.HBM) for h in halves]
    srcs = [pltpu.with_memory_space_constraint(h, pltpu.HBM) for h in halves]
    outs = pl.pallas_call(
        body, name=name,
        out_shape=(pltpu.SemaphoreType.DMA((n,)), pltpu.SemaphoreType.DMA((n,)),
                   *[pltpu.HBM(h.shape, h.dtype) for h in halves], *[pltpu.HBM(l.shape, l.dtype) for l in lands],
                   SDS((8, 128), F32)),
        in_specs=[HBM] * (2 * n), out_specs=(SEM, SEM, *([HBM] * (2 * n)), pl.BlockSpec(memory_space=pltpu.VMEM)),
        input_output_aliases={i: 2 + i for i in range(2 * n)},
        compiler_params=pltpu.CompilerParams(has_side_effects=EFFECT),
    )(*srcs, *lands)
    return outs[0], outs[1], list(outs[2:2 + n]), list(outs[2 + n:2 + 2 * n]), outs[2 + 2 * n]


def join_wait(name, send, recv, halves, lands, after):
    n = len(halves)
    afters = _as_list(after)

    def body(*refs):
        src, dst = refs[:n], refs[n:2 * n]
        send_ref, recv_ref = refs[2 * n], refs[2 * n + 1]
        x, y, c, _ = _place()
        sib = (x, y, 1 - c)
        for a in range(n):
            cp = pltpu.make_async_remote_copy(src[a], dst[a], send_ref.at[a], recv_ref.at[a], device_id=sib, device_id_type=MESH)
            cp.wait_send()
            cp.wait_recv()

    outs = pl.pallas_call(
        body, name=name,
        out_shape=(*[pltpu.HBM(h.shape, h.dtype) for h in halves], *[pltpu.HBM(l.shape, l.dtype) for l in lands]),
        in_specs=[HBM] * (2 * n) + [SEM, SEM] + [ANY] * len(afters), out_specs=[HBM] * (2 * n),
        input_output_aliases={i: i for i in range(2 * n)},
        compiler_params=pltpu.CompilerParams(has_side_effects=EFFECT),
    )(*halves, *lands, send, recv, *afters)
    return list(outs[:n]), list(outs[n:])


def gather_small(name, xs, reduce, after=None):
    m, ncol = xs.shape
    afters = _as_list(after)

    def body(x_ref, *rest):
        out_ref, all_ref, send, recv, lsem = rest[len(afters):]
        x, y, c, chips = _place()
        me, sib = (x, y, c), (x, y, 1 - c)

        def rows(px, py, pc):
            return all_ref.at[pl.ds((4 * px + 2 * py + pc) * m, m), :]

        def copy(k, block, to, src=None):
            return pltpu.make_async_remote_copy(rows(*block) if src is None else src, rows(*block), send.at[k], recv.at[k],
                                                device_id=to, device_id_type=MESH)

        mine = pltpu.make_async_copy(x_ref, rows(*me), lsem)
        mine.start()
        first = [copy(0, me, sib, src=x_ref)] + [copy(1 + j, me, (*chip, c), src=x_ref) for j, chip in enumerate(chips)]
        for cp in first:
            cp.start()
        passed = [copy(4 + j, (*chip, c), sib) for j, chip in enumerate(chips)]
        for j, chip in enumerate(chips):
            copy(1 + j, (*chip, c), me).wait_recv()
            passed[j].start()
        copy(0, sib, me).wait_recv()
        for j, chip in enumerate(chips):
            copy(4 + j, (*chip, 1 - c), me).wait_recv()
        for cp in first + passed:
            cp.wait_send()
        mine.wait()
        if reduce:
            s = all_ref[0:m, :]
            for dev in range(1, 8):
                s = s + all_ref[dev * m:(dev + 1) * m, :]
            out_ref[...] = s
        else:
            out_ref[...] = all_ref[...]

    vm = pl.BlockSpec(memory_space=pltpu.VMEM)
    return pl.pallas_call(
        body, name=name, in_specs=[vm] + [ANY] * len(afters), out_specs=vm,
        out_shape=SDS((m, ncol) if reduce else (8 * m, ncol), F32),
        scratch_shapes=[pltpu.VMEM((8 * m, ncol), F32), pltpu.SemaphoreType.DMA((7,)), pltpu.SemaphoreType.DMA((7,)),
                        pltpu.SemaphoreType.DMA],
    )(xs, *afters)


RELAYOUT_ROWS = 128


def weights_to_cat(name, land, own, place, other, prev=None, after=None):
    tm = RELAYOUT_ROWS
    nb = (D // 2) // tm
    extra = ([] if prev is None else [prev]) + _as_list(after)

    def half(p):
        return 1 - p[0] if other else p[0]

    def body(p_ref, g_ref, own_ref, *rest):
        o_ref = rest[len(extra)]
        nat = jnp.concatenate([jnp.where(p_ref[1] == j, own_ref[...], g_ref[j]) for j in range(NCHIP)], axis=1)
        pad = jnp.zeros((tm, NCAT - OA - 16), BF16)
        o_ref[...] = jnp.concatenate([nat[:, 3072:7168], nat[:, 7184:11280], nat[:, 0:3072], nat[:, 7168:7184], pad], axis=1)

    grid_spec = pltpu.PrefetchScalarGridSpec(
        num_scalar_prefetch=1, grid=(nb,),
        in_specs=[pl.BlockSpec((NCHIP, None, tm, IN_SHARD), lambda i, p: (0, half(p), i, 0)),
                  pl.BlockSpec((None, tm, IN_SHARD), lambda i, p: (half(p), i, 0))] + [ANY] * len(extra),
        out_specs=pl.BlockSpec((tm, NCAT), lambda i, p: (half(p) * nb + i, 0)))
    return pl.pallas_call(
        body, name=name, grid_spec=grid_spec, out_shape=SDS((D, NCAT), BF16),
        input_output_aliases={} if prev is None else {3: 0},
        compiler_params=_cparams(40 * 1024 * 1024, ("arbitrary",)),
    )(place, land, own, *extra)


def grads_from_cat(gw_cat):
    tm = RELAYOUT_ROWS
    nb = (D // 2) // tm

    def body(c_ref, o_ref):
        cat = c_ref[...]
        nat = jnp.concatenate([cat[:, OU:OA], cat[:, OV:OGP], cat[:, OA:OA + 16], cat[:, OGP:OU]], axis=1)
        for j in range(NCHIP):
            o_ref[j] = nat[:, j * IN_SHARD:(j + 1) * IN_SHARD]

    return pl.pallas_call(
        body, name="grads_from_cat", grid=(D // tm,), in_specs=[pl.BlockSpec((tm, NCAT), lambda i: (i, 0))],
        out_specs=pl.BlockSpec((None, NCHIP, tm, IN_SHARD), lambda i: (i // nb, 0, i % nb, 0)),
        out_shape=SDS((2, NCHIP, D // 2, IN_SHARD), BF16), compiler_params=_cparams(40 * 1024 * 1024, ("arbitrary",)),
    )(gw_cat)


def _pad_rows(a, rows):
    return jnp.concatenate([a, jnp.zeros((rows - a.shape[0],) + a.shape[1:], a.dtype)], axis=0)


def local_step(x2d, tgt, gf, g1, pool_scale, wa_pad, b_alpha, ng, g2, get_w, on_grad=None, on_settle=None, tick=None):
    emit = on_grad if on_grad is not None else (lambda group, grads: None)
    settle = on_settle if on_settle is not None else (lambda group, after: None)
    h1 = norm1(x2d, g1)
    wcat, pw = get_w("in", h1)
    pcat = mm_in(h1, wcat)
    dpool, ylin = pool_fwd(pcat, pw)
    pinned = tick("pool", ylin) if tick is not None else None
    og, o, states = gla_fwd(pcat, wa_pad, b_alpha, ng, pinned)
    w_go, w_o = get_w("mid", og)
    mixed, ygla = mm_gla_out(og, w_go, ylin, pcat, pool_scale)
    x2, h2 = mm_out(mixed, w_o, x2d, g2)
    w_up = get_w("up", h2)
    rup, act = mm_up(h2, w_up)
    w_dn = get_w("down", act)
    dx3, dx3b, g_nf, loss_row = mm_down(act, w_dn, x2, tgt, gf)

    gw_down = mm_wgrad("mm_dw_down", act, dx3b, DFF, D, (2, NCHIP, D // 2, D), (None, None, D // 2, D),
                       lambda j, i, k: (i % 2, i // 2, 0, 0), D // 2, D)
    token = emit("down", {"down": gw_down})
    dup = mm_dact(dx3b, w_dn, rup, after=token)
    token = settle("down", dup)
    dx2, dx2b, g_mlp = mm_dh2(dup, w_up, x2, dx3, g2, after=token)
    gw_up = mm_wgrad("mm_dw_up", h2, dup, D, DFF, (2, NCHIP, D // 2, D), (None, None, D // 2, D),
                     lambda j, i, k: (i, j, 0, 0), D // 2, D)
    token = emit("up", {"up": gw_up})
    dylin, dygla, dlgp, dlgg, g_ps = mm_dmixed(dx2b, w_o, pcat, ylin, ygla, pool_scale, after=token)
    token = settle("up", dylin)
    gw_out = mm_wgrad("mm_dw_out", mixed, dx2b, D, D, (2, NCHIP, 256, D), (2, None, 256, D),
                      lambda j, i, k: (0, i, 0, 0), 512, D)
    do, dg, g_ng = mm_dog(dygla, w_go, o, pcat, ng, after=token)
    gw_go = mm_wgrad("mm_dw_gla_out", og, dygla, D, D, (2, NCHIP, 256, D), (2, None, 256, D),
                     lambda j, i, k: (0, i, 0, 0), 512, D)
    token = emit("mix", {"out": gw_out, "gla_out": gw_go})
    dq, dk, dv, dalow, g_wa, g_ba = gla_bwd(do, pcat, states, wa_pad, b_alpha, b_alpha if token is None else token)
    token = settle("mix", dq)
    du, dpw = pool_bwd(dylin, dpool, pw)
    dpcat = jnp.concatenate([dv, dg, dlgp, dlgg, du, dq, dk, dalow, jnp.zeros((T, NCAT - OA - APAD), BF16)], axis=1)
    gw_cat = mm_wgrad("mm_dw_in", h1, dpcat, D, NCAT, (D, NCAT), (1024, 1280), lambda j, i, k: (i, j), 1024, 1280, after=token)
    token = settle("in", emit("in", {"in_cat": gw_cat, "pool": dpw}))
    grad_x, g_mix = mm_dh1(dpcat, wcat, x2d, dx2, g1, after=token)
    return (loss_row[0, 0], grad_x, g_mix, g_ps, g_mlp, g_nf, g_ng, g_ba, g_wa, token,
            gw_cat, dpw, gw_go, gw_out, gw_up, gw_down)


def kernel(x, norm_mix_g, w_in, pool_w, pool_scale, w_alpha, b_alpha, gla_norm_g, w_gla_out, w_out, norm_mlp_g, w_mlp_up, w_mlp_down, norm_final_g, loss_target, m_norm_mix_g, m_w_in, m_pool_w, m_pool_scale, m_w_alpha, m_b_alpha, m_gla_norm_g, m_w_gla_out, m_w_out, m_norm_mlp_g, m_w_mlp_up, m_w_mlp_down, m_norm_final_g, v_norm_mix_g, v_w_in, v_pool_w, v_pool_scale, v_w_alpha, v_b_alpha, v_gla_norm_g, v_w_gla_out, v_w_out, v_norm_mlp_g, v_w_mlp_up, v_w_mlp_down, v_norm_final_g):
    chip = 2 * lax.axis_index("x") + lax.axis_index("y")
    chip_i = chip.astype(jnp.int32).reshape(1)
    core_i = lax.axis_index("c").astype(jnp.int32).reshape(1)
    place_i = jnp.concatenate([core_i, chip_i])
    tgt = loss_target.reshape(T, D)
    gf = norm_final_g.reshape(1, D)

    def halves(w2d):
        r, c = w2d.shape
        return w2d.astype(BF16).reshape(2, r // 2, c)

    pool_shard = pool_w.reshape(4 * PG, PO // NCHIP)
    w_in_r = w_in.reshape(2, D // 2, IN_SHARD)
    sent = {"in": [cast_bf16("cast_w_in", w_in_r), halves(pool_shard)]}
    flight = {}

    def start(group, after=None):
        flight[group] = gather_start("gather_start_" + group, sent[group], after)

    def relay(group, after):
        send, recv, shards, lands = flight[group]
        flight[group] = relay_turn("relay_turn_" + group, send, recv, shards, lands, after)

    def fetch(group, after):
        send, recv, shards, lands = flight[group]
        lands = relay_wait("relay_wait_" + group, send, recv, lands, after)
        return forward_halves("forward_" + group, shards, lands)

    start("in")
    m_in_f, v_in_f, w_go_f, w_o_f, w_up_f, w_dn_f, x_f, wal_f, gng_f = lax.optimization_barrier(
        (m_w_in, v_w_in, w_gla_out, w_out, w_mlp_up, w_mlp_down, x, w_alpha, gla_norm_g, flight["in"][2][0]))[:9]
    m_in_r, v_in_r = m_in_f.reshape(2, D // 2, IN_SHARD), v_in_f.reshape(2, D // 2, IN_SHARD)
    sent["mid"] = [halves(w_go_f[0]), halves(w_o_f[0])]
    relay("in", [m_in_r, v_in_r, *sent["mid"]])
    w_up_f, w_dn_f, x_f, wal_f, gng_f = lax.optimization_barrier(
        (w_up_f, w_dn_f, x_f, wal_f, gng_f, flight["in"][3][0]))[:5]
    sent["up"], sent["down"] = [halves(w_up_f[0])], [halves(w_dn_f[0])]
    x2d = x_f.reshape(T, D)
    big = [w_in_r, w_go_f[0], w_o_f[0], w_up_f[0], w_dn_f[0], pool_shard]

    def tick(point, after):
        if point == "pool":
            relay("mid", after)
            relay("up", flight["mid"][3][0])
            start("down", flight["up"][3][0])
            return [flight["up"][3][0], flight["down"][3][0]]

    def get_w(group, after):
        if group == "in":
            after = [after, *sent["up"], *sent["down"], wa_pad]
        if group == "up":
            relay("down", after)
            send, recv, lands, shards = flight["up"]
            lands = forward_wait("forward_wait_up", send, recv, lands, flight["down"][3][0])
            return lax.dynamic_update_index_in_dim(lands[0], shards[0], chip, 0).reshape(NCHIP, D, D)
        if group == "in":
            send, recv, shards, lands = flight["in"]
            send, recv, lands = forward_turn("forward_turn_in", send, recv, lands, after)
            start("mid", lands[0])
            start("up", flight["mid"][3][0])
            wcat = weights_to_cat("weights_to_cat_mine", lands[0], shards[0], place_i, False, after=flight["up"][3][0])
            lands = forward_wait("forward_wait_in", send, recv, lands, wcat)
            wcat = weights_to_cat("weights_to_cat_sibling", lands[0], shards[0], place_i, True, prev=wcat)
            g_pool = lax.dynamic_update_index_in_dim(lands[1], shards[1], chip, 0)
            pw = jnp.concatenate([g_pool[j].reshape(4, PG, PO // NCHIP) for j in range(NCHIP)], axis=2)
            return wcat, pw
        whole = fetch(group, after)
        if group == "mid":
            send, recv, shards, lands = flight["up"]
            flight["up"] = (*forward_turn("forward_turn_up", send, recv, lands, whole[0]), shards)
            w_go, w_o, _ = lax.optimization_barrier((whole[0], whole[1], flight["up"][2][0]))
            return w_go.reshape(D, D), w_o.reshape(D, D)
        return whole[0].reshape(DFF, D)

    small_w = pack_rows("pack_small_w", [wal_f[0].reshape(4, QK),
                                         jnp.concatenate([gng_f[0].reshape(1, 512), jnp.zeros((1, 512), F32)], axis=1)], 8)
    sw_all = gather_small("gather_small_w", small_w, False).reshape(8, 8, QK)
    wa_full = jnp.concatenate([sw_all[2 * j, 0:4].reshape(16, DK) for j in range(NCHIP)], axis=1)
    ng_full = jnp.concatenate([sw_all[2 * j, 4, 0:512].reshape(HEADS, DV // NCHIP) for j in range(NCHIP)], axis=1)
    wa_pad = _pad_rows(wa_full, APAD).astype(BF16)
    ng = ng_full.reshape(1, D)

    pending = {}
    wmv = {"in": (w_in_r, m_in_r, v_in_r), "gla_out": (big[1], m_w_gla_out, v_w_gla_out), "out": (big[2], m_w_out, v_w_out),
           "up": (big[3], m_w_mlp_up, v_w_mlp_up), "down": (big[4], m_w_mlp_down, v_w_mlp_down), "pool": (big[5], m_pool_w, v_pool_w)}
    big_res = {}

    def reduce_group(group, after):
        nms, send, recv, sums, lands = pending[group]
        sums, lands = scatter_wait("scatter_wait_" + group, send, recv, sums, lands, after)
        reduced = [sum_chips("sum_chips_" + nm, a, b, chip_i) for nm, a, b in zip(nms, sums, lands)]
        send, recv, reduced, lands, token = join_start("join_start_" + group, reduced)
        pending[group] = (nms, send, recv, reduced, lands)
        return token

    def update_group(group, after):
        nms, send, recv, reduced, lands = pending[group]
        reduced, from_sib = join_wait("join_wait_" + group, send, recv, reduced, lands, after)
        for nm, g_own, g_sib in zip(nms, reduced, from_sib):
            w, m, v = wmv[nm]
            shp = (2,) + g_own.shape
            big_res[nm] = adamw_halves("adamw_" + nm, w.reshape(shp), g_own, g_sib, m.reshape(shp), v.reshape(shp), core_i)

    def on_grad(group, grads):
        if group == "in":
            gw_in = grads_from_cat(grads["in_cat"])
            gw_pool = jnp.stack([grads["pool"][:, :, j * 128:(j + 1) * 128].reshape(2, 2 * PG, 128)
                                 for j in range(NCHIP)], axis=1)
            grads = {"in": gw_in, "pool": gw_pool}
        nms, parts = list(grads.keys()), list(grads.values())
        send, recv, parts, got, token = exchange_start("exchange_start_" + group, parts)
        pending[group] = (nms, send, recv, parts, got)
        return token

    def on_settle(group, after):
        if group == "in":
            after = reduce_group("down", after)
        nms, send, recv, parts, got = pending[group]
        parts, got = exchange_wait("exchange_wait_" + group, send, recv, parts, got, after)
        sums = [add_pairs("add_pair_" + nm, a, b, core_i) for nm, a, b in zip(nms, parts, got)]
        send, recv, sums, lands, token = scatter_start("scatter_start_" + group, sums)
        pending[group] = (nms, send, recv, sums, lands)
        if group != "in":
            return token
        token = reduce_group("up", token)
        token = reduce_group("mix", token)
        for earlier in ("down", "up", "mix"):
            update_group(earlier, token)
            token = big_res[pending[earlier][0][-1]][1]
        return [big_res[nm][1] for nm in ("down", "up", "out", "gla_out")]

    (loss_local, grad_x, g_mix, g_ps, g_mlp, g_nf, g_ng, g_ba, g_wa) = local_step(
        x2d, tgt, gf, norm_mix_g, pool_scale, wa_pad, b_alpha, ng, norm_mlp_g, get_w, on_grad, on_settle, tick)[:9]
    loss = lax.psum(loss_local, ("x", "y", "c"))
    join_in_token = reduce_group("in", grad_x)

    ROWS = 16

    def wide(a, n):
        return jnp.concatenate([a.reshape(1, n), jnp.zeros((1, D - n), F32)], axis=1)

    packed = pack_rows("pack_small_g", [g_mix, g_ps, g_mlp, g_nf, g_ng, wide(g_ba, QK), g_wa[0:16].reshape(8, D)], ROWS)
    tot = gather_small("reduce_small_g", packed, True, join_in_token)
    t_wa = lax.dynamic_slice(tot[6:14].reshape(16, QK), (0, chip * DK), (16, DK))
    t_ng = lax.dynamic_slice(tot[4].reshape(HEADS, DV), (0, chip * (DV // NCHIP)), (HEADS, DV // NCHIP))

    def pack_small(nm, mix, ps, mlp, nf, ba, wa, gn, after=None):
        return pack_rows(nm, [mix.reshape(1, D), ps.reshape(1, D), mlp.reshape(1, D), nf.reshape(1, D), wide(ba, QK),
                              wa.reshape(2, D), wide(gn, 512)], ROWS, after)

    update_group("in", tot)
    sg = pack_small("pack_g", tot[0], tot[1], tot[2], tot[3], tot[5, 0:QK], t_wa, t_ng, big_res["in"][3])
    sw = pack_small("pack_w", norm_mix_g, pool_scale, norm_mlp_g, norm_final_g, b_alpha, w_alpha, gla_norm_g)
    sm = pack_small("pack_m", m_norm_mix_g, m_pool_scale, m_norm_mlp_g, m_norm_final_g, m_b_alpha, m_w_alpha, m_gla_norm_g)
    sv = pack_small("pack_v", v_norm_mix_g, v_pool_scale, v_norm_mlp_g, v_norm_final_g, v_b_alpha, v_w_alpha, v_gla_norm_g)
    small_res = adamw("adamw_small", sw, sg, sm, sv)

    def unpack(p):
        return {"norm_mix_g": p[0].reshape(1, D), "pool_scale": p[1].reshape(1, D), "norm_mlp_g": p[2].reshape(1, D),
                "norm_final_g": p[3].reshape(D), "b_alpha": p[4, 0:QK].reshape(1, QK), "w_alpha": p[5:7].reshape(1, 16, DK),
                "gla_norm_g": p[7, 0:512].reshape(1, HEADS, DV // NCHIP)}

    order = ["norm_mix_g", "w_in", "pool_w", "pool_scale", "w_alpha", "b_alpha", "gla_norm_g", "w_gla_out", "w_out",
             "norm_mlp_g", "w_mlp_up", "w_mlp_down", "norm_final_g"]
    big_key = {"w_in": ("in", w_in.shape), "pool_w": ("pool", pool_w.shape), "w_gla_out": ("gla_out", w_gla_out.shape),
               "w_out": ("out", w_out.shape), "w_mlp_up": ("up", w_mlp_up.shape), "w_mlp_down": ("down", w_mlp_down.shape)}
    result = [loss, grad_x.reshape(1, T, D)]
    for kind in range(4):
        small = unpack(small_res[kind])
        for nm in order:
            if nm in big_key:
                key, shp = big_key[nm]
                result.append(big_res[key][kind].reshape(shp))
            else:
                result.append(small[nm])
    return tuple(result)
```

```python
import itertools

import jax
import jax.numpy as jnp
from jax import lax
from jax.experimental import pallas as pl
from jax.experimental.pallas import tpu as pltpu

F32 = jnp.float32
BF16 = jnp.bfloat16
SDS = jax.ShapeDtypeStruct
MESH = pl.DeviceIdType.MESH
ANY = pl.BlockSpec(memory_space=pl.ANY)

T = 2048
D = 2048
DFF = 8192
NCHIP = 4
IN_WIDTH = 11280
IN_SHARD = IN_WIDTH // NCHIP
CHUNK = 64
NCHUNK = T // CHUNK
HEADS = 4
DK = 256
DV = 512
QK = HEADS * DK
EPS = 1e-6
POOL_WINDOWS = (2, 4, 8, 16)
PG = 256
PO = 512

OV, OG, OGP, OGG, OU, OQ, OKK, OA = 0, 2048, 4096, 6144, 8192, 9216, 10240, 11264
NCAT = 11520
APAD = 128

VMEM_CAP = 56 * 1024 * 1024

PIECE_BYTES = 384 * 1024

ADAM_LR, ADAM_B1, ADAM_B2, ADAM_EPS, ADAM_WD, ADAM_STEP = 0.001, 0.9, 0.999, 1e-08, 0.01, 10


def _cparams(vmem_bytes=None, sem=None):
    kw = {}
    if vmem_bytes is not None:
        kw["vmem_limit_bytes"] = int(min(max(vmem_bytes, 32 * 1024 * 1024), VMEM_CAP))
    if sem is not None:
        kw["dimension_semantics"] = sem
    return pltpu.CompilerParams(**kw)


def _nbytes(shape, dtype):
    n = 1
    for s in shape:
        if s is not None:
            n *= s
    return n * jnp.dtype(dtype).itemsize


def _sigmoid(x):
    return 0.5 * jnp.tanh(0.5 * x) + 0.5


GLA_STEP = 4
EPI_COLS = 512


def _as_list(after):
    if after is None:
        return []
    return list(after) if isinstance(after, (list, tuple)) else [after]


def matmul(name, a, b, *, a_spec, b_spec, cdims, grid, acc_shape, outs, extras=(), epi, after=None):
    nj, ni, nk = grid
    ne, no = len(extras), len(outs)
    afters = _as_list(after)
    first_out = 2 + ne + len(afters)

    def body(*refs):
        a_ref, b_ref = refs[0], refs[1]
        ex = refs[2:2 + ne]
        out_refs = refs[first_out:first_out + no]
        i = pl.program_id(1)
        part = lax.dot_general(a_ref[...], b_ref[...], (cdims, ((), ())), preferred_element_type=F32)
        if nk == 1:
            epi(part, ex, out_refs, i)
        else:
            acc_ref = refs[first_out + no]
            k = pl.program_id(2)

            @pl.when(k == 0)
            def _():
                acc_ref[...] = part

            @pl.when(k > 0)
            def _():
                acc_ref[...] += part

            @pl.when(k == nk - 1)
            def _():
                epi(acc_ref[...], ex, out_refs, i)

    in_specs = [pl.BlockSpec(*a_spec), pl.BlockSpec(*b_spec)] + [pl.BlockSpec(bs, im) for _, bs, im in extras]
    in_specs += [ANY] * len(afters)
    out_specs = [pl.BlockSpec(bs, im) for _, _, bs, im in outs]
    out_shape = [SDS(s, dt) for s, dt, _, _ in outs]
    vm = 2 * (_nbytes(a_spec[0], a.dtype) + _nbytes(b_spec[0], b.dtype))
    vm += 2 * sum(_nbytes(bs, arr.dtype) for arr, bs, _ in extras)
    vm += 2 * sum(_nbytes(bs, dt) for _, dt, bs, _ in outs)
    vm += 6 * _nbytes(acc_shape, F32)
    scratch = [pltpu.VMEM(acc_shape, F32)] if nk > 1 else []
    return pl.pallas_call(
        body, name=name, grid=grid, in_specs=in_specs, out_specs=out_specs, out_shape=out_shape,
        scratch_shapes=scratch,
        compiler_params=_cparams(vm, ("arbitrary", "arbitrary", "arbitrary")),
    )(a, b, *[arr for arr, _, _ in extras], *afters)


NN =((1,), (0,))
NT = ((1,), (1,))
TN = ((0,), (0,))


def _row_acc(out_ref, val, i):
    @pl.when(i == 0)
    def _():
        out_ref[...] = val

    @pl.when(i > 0)
    def _():
        out_ref[...] += val


def _rms_bwd(xn, r, dxn):
    return r * (dxn - xn * jnp.mean(dxn * xn, axis=-1, keepdims=True))


def norm1(x, g):
    tm = 256

    def body(x_ref, g_ref, h_ref):
        xv = x_ref[...]
        r = lax.rsqrt(jnp.mean(xv * xv, axis=-1, keepdims=True) + EPS)
        h_ref[...] = (xv * r * g_ref[...]).astype(BF16)

    return pl.pallas_call(
        body, name="norm1", grid=(T // tm,),
        in_specs=[pl.BlockSpec((tm, D), lambda i: (i, 0)), pl.BlockSpec((1, D), lambda i: (0, 0))],
        out_specs=pl.BlockSpec((tm, D), lambda i: (i, 0)), out_shape=SDS((T, D), BF16),
        compiler_params=_cparams(32 * 1024 * 1024, ("arbitrary",)),
    )(x, g)


def mm_in(h1, wcat):
    tm, tn = 1024, 1280

    def epi(acc, ex, outs, i):
        outs[0][...] = acc.astype(BF16)

    return matmul("mm_in", h1, wcat, a_spec=((tm, D), lambda j, i, k: (i, 0)), b_spec=((D, tn), lambda j, i, k: (0, j)),
                  cdims=NN, grid=(NCAT // tn, T // tm, 1), acc_shape=(tm, tn),
                  outs=[((T, NCAT), BF16, (tm, tn), lambda j, i, k: (i, j))], epi=epi)[0]


def _window_sum(x, w, up):
    n = x.shape[0]
    row = lax.broadcasted_iota(jnp.int32, x.shape, 0)
    s, sh = x, 1
    while sh < w:
        if up:
            s = s + jnp.where(row < n - sh, pltpu.roll(s, n - sh, axis=0), 0.0)
        else:
            s = s + jnp.where(row >= sh, pltpu.roll(s, sh, axis=0), 0.0)
        sh *= 2
    return s


def _inv_count(shape, w):
    row = lax.broadcasted_iota(jnp.int32, shape, 0)
    return 1.0 / jnp.minimum(row + 1, w).astype(F32)


def pool_fwd(pcat, pw):
    def body(u_ref, pw_ref, d_ref, y_ref):
        for gi, w in enumerate(POOL_WINDOWS):
            ug = u_ref[:, gi * PG:(gi + 1) * PG].astype(F32)
            dg = _window_sum(ug, w, False) * _inv_count(ug.shape, w) - ug
            db = dg.astype(BF16)
            d_ref[:, gi * PG:(gi + 1) * PG] = db
            y_ref[:, gi * PO:(gi + 1) * PO] = jnp.dot(db, pw_ref[gi], preferred_element_type=F32).astype(BF16)

    return pl.pallas_call(
        body, name="pool_fwd", grid=(1,),
        in_specs=[pl.BlockSpec((T, 4 * PG), lambda i: (0, OU // (4 * PG))), pl.BlockSpec((4, PG, PO), lambda i: (0, 0, 0))],
        out_specs=[pl.BlockSpec((T, 4 * PG), lambda i: (0, 0)), pl.BlockSpec((T, D), lambda i: (0, 0))],
        out_shape=[SDS((T, 4 * PG), BF16), SDS((T, D), BF16)],
        compiler_params=_cparams(48 * 1024 * 1024, ("arbitrary",)),
    )(pcat, pw)


def pool_bwd(dylin, d, pw):
    def body(dy_ref, d_ref, pw_ref, du_ref, dpw_ref):
        for gi, w in enumerate(POOL_WINDOWS):
            dyl = dy_ref[:, gi * PO:(gi + 1) * PO]
            dd = lax.dot_general(dyl, pw_ref[gi], (NT, ((), ())), preferred_element_type=F32)
            du = _window_sum(dd * _inv_count(dd.shape, w), w, True) - dd
            du_ref[:, gi * PG:(gi + 1) * PG] = du.astype(BF16)
            dpw_ref[gi] = lax.dot_general(d_ref[:, gi * PG:(gi + 1) * PG], dyl, (TN, ((), ())),
                                          preferred_element_type=F32).astype(BF16)

    return pl.pallas_call(
        body, name="pool_bwd", grid=(1,),
        in_specs=[pl.BlockSpec((T, D), lambda i: (0, 0)), pl.BlockSpec((T, 4 * PG), lambda i: (0, 0)),
                  pl.BlockSpec((4, PG, PO), lambda i: (0, 0, 0))],
        out_specs=[pl.BlockSpec((T, 4 * PG), lambda i: (0, 0)), pl.BlockSpec((4, PG, PO), lambda i: (0, 0, 0))],
        out_shape=[SDS((T, 4 * PG), BF16), SDS((4, PG, PO), BF16)],
        compiler_params=_cparams(48 * 1024 * 1024, ("arbitrary",)),
    )(dylin, d, pw)


def _gate_decay(alow, wa, ba):
    a = jnp.dot(alow, wa, preferred_element_type=F32) + ba
    ls = jax.nn.log_sigmoid(a) * (1.0 / 16.0)
    r = lax.broadcasted_iota(jnp.int32, (CHUNK, CHUNK), 0)
    c = lax.broadcasted_iota(jnp.int32, (CHUNK, CHUNK), 1)
    tri = jnp.where(c <= r, 1.0, 0.0).astype(F32)
    cum = jnp.dot(tri, ls, preferred_element_type=F32, precision=lax.Precision.HIGHEST)
    last = cum[CHUNK - 1:CHUNK, :]
    return a, jnp.exp(last - cum), jnp.exp(last)


def gla_fwd(pcat, wa, ba, ng, after=None):
    afters = _as_list(after)

    def body(q_ref, k_ref, v_ref, g_ref, al_ref, wa_ref, ba_ref, ng_ref, *rest):
        og_ref, o_ref, st_ref, s_scr = rest[len(afters):]

        @pl.when(pl.program_id(0) == 0)
        def _():
            s_scr[...] = jnp.zeros_like(s_scr)

        state = [s_scr[h] for h in range(HEADS)]
        for s in range(GLA_STEP):
            rs = slice(s * CHUNK, (s + 1) * CHUNK)
            _, e, decay = _gate_decay(al_ref[rs, :], wa_ref[...], ba_ref[...])
            kd = (k_ref[rs, :].astype(F32) * e).astype(BF16)
            qs = (q_ref[rs, :].astype(F32) * (DK ** -0.5)).astype(BF16)
            for h in range(HEADS):
                ck = slice(h * DK, (h + 1) * DK)
                cv = slice(h * DV, (h + 1) * DV)
                state[h] = state[h] * decay[:, ck] + lax.dot_general(v_ref[rs, cv], kd[:, ck], (TN, ((), ())),
                                                                     preferred_element_type=F32)
                sb = state[h].astype(BF16)
                st_ref[s, h] = sb
                oh = lax.dot_general(qs[:, ck], sb, (NT, ((), ())), preferred_element_type=F32)
                o_ref[rs, cv] = oh.astype(BF16)
                on = oh * lax.rsqrt(jnp.mean(oh * oh, axis=-1, keepdims=True) + EPS) * ng_ref[:, cv]
                gv = g_ref[rs, cv].astype(F32)
                og_ref[rs, cv] = (on * (gv * _sigmoid(gv))).astype(BF16)
        for h in range(HEADS):
            s_scr[h] = state[h]

    row = lambda c: (c, 0)
    rows = GLA_STEP * CHUNK
    return pl.pallas_call(
        body, name="gla_fwd", grid=(NCHUNK // GLA_STEP,),
        in_specs=[pl.BlockSpec((rows, QK), lambda c: (c, OQ // QK)), pl.BlockSpec((rows, QK), lambda c: (c, OKK // QK)),
                  pl.BlockSpec((rows, D), lambda c: (c, OV // D)), pl.BlockSpec((rows, D), lambda c: (c, OG // D)),
                  pl.BlockSpec((rows, APAD), lambda c: (c, OA // APAD)),
                  pl.BlockSpec((APAD, QK), lambda c: (0, 0)), pl.BlockSpec((1, QK), lambda c: (0, 0)),
                  pl.BlockSpec((1, D), lambda c: (0, 0))] + [ANY] * len(afters),
        out_specs=[pl.BlockSpec((rows, D), row), pl.BlockSpec((rows, D), row),
                   pl.BlockSpec((GLA_STEP, HEADS, DV, DK), lambda c: (c, 0, 0, 0))],
        out_shape=[SDS((T, D), BF16), SDS((T, D), BF16), SDS((NCHUNK, HEADS, DV, DK), BF16)],
        scratch_shapes=[pltpu.VMEM((HEADS, DV, DK), F32)],
        compiler_params=_cparams(32 * 1024 * 1024, ("arbitrary",)),
    )(pcat, pcat, pcat, pcat, pcat, wa, ba, ng, *afters)


def gla_bwd(do, pcat, states, wa, ba, after):
    def body(do_ref, q_ref, k_ref, v_ref, al_ref, sc_ref, sp_ref, wa_ref, ba_ref, after_ref,
             dq_ref, dk_ref, dv_ref, dal_ref, dwa_ref, dba_ref, ds_scr):
        i = pl.program_id(0)

        @pl.when(i == 0)
        def _():
            ds_scr[...] = jnp.zeros_like(ds_scr)

        ds = [ds_scr[h] for h in range(HEADS)]
        dwa, dba = 0.0, 0.0
        for u in reversed(range(GLA_STEP)):
            rs = slice(u * CHUNK, (u + 1) * CHUNK)
            first_chunk = jnp.logical_and(i == NCHUNK // GLA_STEP - 1, u == 0)
            has_prev = jnp.where(first_chunk, 0.0, 1.0).astype(F32)
            a, e, decay = _gate_decay(al_ref[rs, :], wa_ref[...], ba_ref[...])
            kdf = k_ref[rs, :].astype(F32) * e
            kd = kdf.astype(BF16)
            qs = (q_ref[rs, :].astype(F32) * (DK ** -0.5)).astype(BF16)
            dkd_parts, ddecay_parts = [], []
            for h in range(HEADS):
                ck = slice(h * DK, (h + 1) * DK)
                cv = slice(h * DV, (h + 1) * DV)
                doh = do_ref[rs, cv]
                dsh = ds[h] + lax.dot_general(doh, qs[:, ck], (TN, ((), ())), preferred_element_type=F32)
                dsb = dsh.astype(BF16)
                dq_ref[rs, ck] = (jnp.dot(doh, sc_ref[u, h], preferred_element_type=F32) * (DK ** -0.5)).astype(BF16)
                dkd_parts.append(jnp.dot(v_ref[rs, cv], dsb, preferred_element_type=F32))
                dv_ref[rs, cv] = lax.dot_general(kd[:, ck], dsb, (NT, ((), ())), preferred_element_type=F32).astype(BF16)
                s_prev = (sp_ref[h] if u == 0 else sc_ref[u - 1, h]).astype(F32)
                ddecay_parts.append(jnp.sum(dsh * s_prev, axis=0, keepdims=True) * has_prev)
                ds[h] = dsh * decay[:, ck]
            dkd = jnp.concatenate(dkd_parts, axis=1)
            ddecay = jnp.concatenate(ddecay_parts, axis=1)
            dk_ref[rs, :] = (dkd * e).astype(BF16)
            dearg = dkd * kdf
            dlast = jnp.sum(dearg, axis=0, keepdims=True) + ddecay * decay
            r = lax.broadcasted_iota(jnp.int32, (CHUNK, CHUNK), 0)
            c = lax.broadcasted_iota(jnp.int32, (CHUNK, CHUNK), 1)
            triu = jnp.where(c >= r, 1.0, 0.0).astype(F32)
            dls = dlast - jnp.dot(triu, dearg, preferred_element_type=F32, precision=lax.Precision.HIGHEST)
            da = dls * (1.0 / 16.0) * (1.0 - _sigmoid(a))
            dab = da.astype(BF16)
            dal_ref[rs, :] = lax.dot_general(dab, wa_ref[...], (NT, ((), ())), preferred_element_type=F32).astype(BF16)
            dwa = dwa + lax.dot_general(al_ref[rs, :], dab, (TN, ((), ())), preferred_element_type=F32)
            dba = dba + jnp.sum(da, axis=0, keepdims=True)
        for h in range(HEADS):
            ds_scr[h] = ds[h]

        @pl.when(i == 0)
        def _():
            dwa_ref[...] = dwa
            dba_ref[...] = dba

        @pl.when(i > 0)
        def _():
            dwa_ref[...] += dwa
            dba_ref[...] += dba

    rows = GLA_STEP * CHUNK
    rev = lambda i: NCHUNK // GLA_STEP - 1 - i
    return pl.pallas_call(
        body, name="gla_bwd", grid=(NCHUNK // GLA_STEP,),
        in_specs=[pl.BlockSpec((rows, D), lambda i: (rev(i), 0)),
                  pl.BlockSpec((rows, QK), lambda i: (rev(i), OQ // QK)), pl.BlockSpec((rows, QK), lambda i: (rev(i), OKK // QK)),
                  pl.BlockSpec((rows, D), lambda i: (rev(i), OV // D)), pl.BlockSpec((rows, APAD), lambda i: (rev(i), OA // APAD)),
                  pl.BlockSpec((GLA_STEP, HEADS, DV, DK), lambda i: (rev(i), 0, 0, 0)),
                  pl.BlockSpec((None, HEADS, DV, DK), lambda i: (jnp.maximum(rev(i) * GLA_STEP - 1, 0), 0, 0, 0)),
                  pl.BlockSpec((APAD, QK), lambda i: (0, 0)), pl.BlockSpec((1, QK), lambda i: (0, 0)), ANY],
        out_specs=[pl.BlockSpec((rows, QK), lambda i: (rev(i), 0)), pl.BlockSpec((rows, QK), lambda i: (rev(i), 0)),
                   pl.BlockSpec((rows, D), lambda i: (rev(i), 0)), pl.BlockSpec((rows, APAD), lambda i: (rev(i), 0)),
                   pl.BlockSpec((APAD, QK), lambda i: (0, 0)), pl.BlockSpec((1, QK), lambda i: (0, 0))],
        out_shape=[SDS((T, QK), BF16), SDS((T, QK), BF16), SDS((T, D), BF16), SDS((T, APAD), BF16),
                   SDS((APAD, QK), F32), SDS((1, QK), F32)],
        scratch_shapes=[pltpu.VMEM((HEADS, DV, DK), F32)],
        compiler_params=_cparams(32 * 1024 * 1024, ("arbitrary",)),
    )(do, pcat, pcat, pcat, pcat, states, states, wa, ba, after)


TMF = 256
TMW = 512
_rowblk = ((TMF, D), lambda j, i, k: (i, 0))
_vec = ((1, D), lambda j, i, k: (0, 0))


def _full_spec(col):
    return ((TMF, D), lambda j, i, k: (i, col))


TBIG = 1024


def square_matmul(name, a, b, *, a_spec, b_spec, cdims, nk, after=None):
    def epi(acc, ex, outs, i):
        outs[0][...] = acc

    return matmul(name, a, b, a_spec=a_spec, b_spec=b_spec, cdims=cdims, grid=(D // TBIG, T // TBIG, nk),
                  acc_shape=(TBIG, TBIG), outs=[((T, D), F32, (TBIG, TBIG), lambda j, i, k: (i, j))], epi=epi,
                  after=after)[0]


def rowwise(name, y, *, extras, outs, epi):
    ne = len(extras)

    def body(*refs):
        epi(refs[0][...], refs[1:1 + ne], refs[1 + ne:], pl.program_id(1))

    in_specs = [pl.BlockSpec(*_rowblk)] + [pl.BlockSpec(bs, im) for _, bs, im in extras]
    return pl.pallas_call(
        body, name=name, grid=(1, T // TMF, 1), in_specs=in_specs,
        out_specs=[pl.BlockSpec(bs, im) for _, _, bs, im in outs], out_shape=[SDS(s, dt) for s, dt, _, _ in outs],
        compiler_params=_cparams(40 * 1024 * 1024, ("arbitrary", "arbitrary", "arbitrary")),
    )(y, *[arr for arr, _, _ in extras])


def mm_gla_out(og, w, ylin, pcat, pscale):
    def epi(acc, ex, outs, i):
        ylin_ref, lgp_ref, lgg_ref, ps_ref = ex
        for c0 in range(0, D, EPI_COLS):
            cs = slice(c0, c0 + EPI_COLS)
            gp = _sigmoid(lgp_ref[:, cs].astype(F32))
            gg = _sigmoid(lgg_ref[:, cs].astype(F32))
            a = acc[:, cs]
            outs[0][:, cs] = (gp * (ylin_ref[:, cs].astype(F32) * ps_ref[:, cs]) + gg * a).astype(BF16)
            outs[1][:, cs] = a.astype(BF16)

    return matmul("mm_gla_out", og, w, a_spec=_rowblk, b_spec=((D, D), lambda j, i, k: (0, 0)), cdims=NN,
                  grid=(1, T // TMF, 1), acc_shape=(TMF, D),
                  extras=[(ylin, *_rowblk), (pcat, *_full_spec(OGP // D)), (pcat, *_full_spec(OGG // D)), (pscale, *_vec)],
                  outs=[((T, D), BF16, *_rowblk), ((T, D), BF16, *_rowblk)], epi=epi)


def mm_out(mixed, w, x, g2):
    def epi(acc, ex, outs, i):
        x_ref, g_ref = ex
        x2 = x_ref[...] + acc
        r = lax.rsqrt(jnp.mean(x2 * x2, axis=-1, keepdims=True) + EPS)
        outs[0][...] = x2
        outs[1][...] = (x2 * r * g_ref[...]).astype(BF16)

    return matmul("mm_out", mixed, w, a_spec=_rowblk, b_spec=((D, D), lambda j, i, k: (0, 0)), cdims=NN,
                  grid=(1, T // TMF, 1), acc_shape=(TMF, D), extras=[(x, *_rowblk), (g2, *_vec)],
                  outs=[((T, D), F32, *_rowblk), ((T, D), BF16, *_rowblk)], epi=epi)


def mm_up(h2, wup):
    def epi(acc, ex, outs, i):
        r = jnp.maximum(acc, 0.0)
        outs[0][...] = r.astype(BF16)
        outs[1][...] = (r * r).astype(BF16)

    blk = ((TMW, D), lambda j, i, k: (i, j))
    return matmul("mm_up", h2, wup, a_spec=((TMW, D), lambda j, i, k: (i, 0)), b_spec=((None, D, D), lambda j, i, k: (j, 0, 0)),
                  cdims=NN, grid=(NCHIP, T // TMW, 1), acc_shape=(TMW, D),
                  outs=[((T, DFF), BF16, *blk), ((T, DFF), BF16, *blk)], epi=epi)


def mm_down(act, wdown, x2, tgt, gf):
    tk = 4096

    def epi(acc, ex, outs, i):
        x2_ref, t_ref, g_ref = ex
        dx_ref, dxb_ref, gnf_ref, loss_ref = outs
        x3 = x2_ref[...] + acc
        r = lax.rsqrt(jnp.mean(x3 * x3, axis=-1, keepdims=True) + EPS)
        xn = x3 * r
        err = xn * g_ref[...] - t_ref[...]
        lsum = 0.5 * jnp.sum(jnp.mean(err * err, axis=-1, keepdims=True), axis=0, keepdims=True)
        dy = err * (1.0 / D)
        _row_acc(gnf_ref, jnp.sum(dy * xn, axis=0, keepdims=True), i)
        _row_acc(loss_ref, jnp.broadcast_to(lsum, (1, 128)), i)
        dx3 = _rms_bwd(xn, r, dy * g_ref[...])
        dx_ref[...] = dx3
        dxb_ref[...] = dx3.astype(BF16)

    y = square_matmul("mm_down", act, wdown, a_spec=((TBIG, tk), lambda j, i, k: (i, k)),
                      b_spec=((tk, TBIG), lambda j, i, k: (k, j)), cdims=NN, nk=DFF // tk)
    return rowwise("rows_final", y, extras=[(x2, *_rowblk), (tgt, *_rowblk), (gf, *_vec)],
                   outs=[((T, D), F32, *_rowblk), ((T, D), BF16, *_rowblk), ((1, D), F32, *_vec),
                         ((1, 128), F32, (1, 128), lambda j, i, k: (0, 0))], epi=epi)


def mm_dact(dx3b, wdown, rup, after=None):
    def epi(acc, ex, outs, i):
        outs[0][...] = (acc * 2.0 * ex[0][...].astype(F32)).astype(BF16)

    blk = ((TMW, D), lambda j, i, k: (i, j))
    return matmul("mm_dact", dx3b, wdown, a_spec=((TMW, D), lambda j, i, k: (i, 0)), b_spec=((D, D), lambda j, i, k: (j, 0)),
                  cdims=NT, grid=(DFF // D, T // TMW, 1), acc_shape=(TMW, D), extras=[(rup, *blk)],
                  outs=[((T, DFF), BF16, *blk)], epi=epi, after=after)[0]


def mm_wgrad(name, a, b, m, n, out_shape, out_block, out_map, tm, tn, after=None):
    def epi(acc, ex, outs, i):
        outs[0][...] = acc.astype(BF16).reshape(outs[0].shape)

    return matmul(name, a, b, a_spec=((T, tm), lambda j, i, k: (0, i)), b_spec=((T, tn), lambda j, i, k: (0, j)),
                  cdims=TN, grid=(n // tn, m // tm, 1), acc_shape=(tm, tn),
                  outs=[(out_shape, BF16, out_block, out_map)], epi=epi, after=after)[0]


def mm_dh2(dup, wup, x2, dx3, g2, after=None):
    def epi(acc, ex, outs, i):
        x2_ref, dx3_ref, g_ref = ex
        x2 = x2_ref[...]
        r = lax.rsqrt(jnp.mean(x2 * x2, axis=-1, keepdims=True) + EPS)
        xn = x2 * r
        _row_acc(outs[2], jnp.sum(acc * xn, axis=0, keepdims=True), i)
        dx2 = dx3_ref[...] + _rms_bwd(xn, r, acc * g_ref[...])
        outs[0][...] = dx2
        outs[1][...] = dx2.astype(BF16)

    y = square_matmul("mm_dh2", dup, wup, a_spec=((TBIG, D), lambda j, i, k: (i, k)),
                      b_spec=((None, TBIG, D), lambda j, i, k: (k, j, 0)), cdims=NT, nk=NCHIP, after=after)
    return rowwise("rows_dh2", y, extras=[(x2, *_rowblk), (dx3, *_rowblk), (g2, *_vec)],
                   outs=[((T, D), F32, *_rowblk), ((T, D), BF16, *_rowblk), ((1, D), F32, *_vec)], epi=epi)


def mm_dmixed(dx2b, wout, pcat, ylin, ygla, pscale, after=None):
    def epi(acc, ex, outs, i):
        lgp_ref, lgg_ref, ylin_ref, ygla_ref, ps_ref = ex
        dps = []
        for c0 in range(0, D, EPI_COLS):
            cs = slice(c0, c0 + EPI_COLS)
            gp = _sigmoid(lgp_ref[:, cs].astype(F32))
            gg = _sigmoid(lgg_ref[:, cs].astype(F32))
            yl = ylin_ref[:, cs].astype(F32)
            ps = ps_ref[:, cs]
            a = acc[:, cs]
            agp = a * gp
            outs[0][:, cs] = (agp * ps).astype(BF16)
            outs[1][:, cs] = (a * gg).astype(BF16)
            outs[2][:, cs] = (agp * (yl * ps) * (1.0 - gp)).astype(BF16)
            outs[3][:, cs] = (a * ygla_ref[:, cs].astype(F32) * gg * (1.0 - gg)).astype(BF16)
            dps.append(jnp.sum(agp * yl, axis=0, keepdims=True))
        _row_acc(outs[4], jnp.concatenate(dps, axis=1), i)

    return matmul("mm_dmixed", dx2b, wout, a_spec=_rowblk, b_spec=((D, D), lambda j, i, k: (0, 0)), cdims=NT,
                  grid=(1, T // TMF, 1), acc_shape=(TMF, D),
                  extras=[(pcat, *_full_spec(OGP // D)), (pcat, *_full_spec(OGG // D)), (ylin, *_rowblk), (ygla, *_rowblk),
                          (pscale, *_vec)],
                  outs=[((T, D), BF16, *_rowblk)] * 4 + [((1, D), F32, *_vec)], epi=epi, after=after)


def mm_dog(dygla, wgo, o, pcat, ng, after=None):
    def epi(acc, ex, outs, i):
        o_ref, g_ref, ng_ref = ex
        do_ref, dg_ref, gng_ref = outs
        gparts = []
        for h in range(HEADS):
            cv = slice(h * DV, (h + 1) * DV)
            oh = o_ref[:, cv].astype(F32)
            r = lax.rsqrt(jnp.mean(oh * oh, axis=-1, keepdims=True) + EPS)
            on = oh * r
            gv = g_ref[:, cv].astype(F32)
            sg = _sigmoid(gv)
            a = acc[:, cv]
            dgain = a * (gv * sg)
            gparts.append(jnp.sum(dgain * on, axis=0, keepdims=True))
            ngh = ng_ref[:, cv]
            do_ref[:, cv] = _rms_bwd(on, r, dgain * ngh).astype(BF16)
            dg_ref[:, cv] = (a * (on * ngh) * (sg * (1.0 + gv * (1.0 - sg)))).astype(BF16)
        _row_acc(gng_ref, jnp.concatenate(gparts, axis=1), i)

    return matmul("mm_dog", dygla, wgo, a_spec=_rowblk, b_spec=((D, D), lambda j, i, k: (0, 0)), cdims=NT,
                  grid=(1, T // TMF, 1), acc_shape=(TMF, D),
                  extras=[(o, *_rowblk), (pcat, *_full_spec(OG // D)), (ng, *_vec)],
                  outs=[((T, D), BF16, *_rowblk), ((T, D), BF16, *_rowblk), ((1, D), F32, *_vec)], epi=epi, after=after)


def mm_dh1(dpcat, wcat, x, dx2, g1, after=None):
    tk = 3840

    def epi(acc, ex, outs, i):
        x_ref, dx2_ref, g_ref = ex
        xv = x_ref[...]
        r = lax.rsqrt(jnp.mean(xv * xv, axis=-1, keepdims=True) + EPS)
        xn = xv * r
        _row_acc(outs[1], jnp.sum(acc * xn, axis=0, keepdims=True), i)
        outs[0][...] = dx2_ref[...] + _rms_bwd(xn, r, acc * g_ref[...])

    y = square_matmul("mm_dh1", dpcat, wcat, a_spec=((TBIG, tk), lambda j, i, k: (i, k)),
                      b_spec=((TBIG, tk), lambda j, i, k: (j, k)), cdims=NT, nk=NCAT // tk, after=after)
    return rowwise("rows_dh1", y, extras=[(x, *_rowblk), (dx2, *_rowblk), (g1, *_vec)],
                   outs=[((T, D), F32, *_rowblk), ((1, D), F32, *_vec)], epi=epi)


def _tile_rows(rows, cols, n_arrays):
    tm = rows
    while tm % 32 == 0 and 2 * n_arrays * tm * cols * 4 > 36 * 1024 * 1024:
        tm //= 2
    return tm


def add_pairs(name, parts, theirs, core):
    _, _, r, c = parts.shape
    tm = _tile_rows(r, c, 3)

    def body(core_ref, a_ref, b_ref, o_ref):
        o_ref[...] = (a_ref[...].astype(F32) + b_ref[...].astype(F32)).astype(BF16)

    spec = pl.BlockSpec((None, tm, c), lambda j, i, core_ref: (j, i, 0))
    grid_spec = pltpu.PrefetchScalarGridSpec(
        num_scalar_prefetch=1, grid=(NCHIP, r // tm),
        in_specs=[pl.BlockSpec((None, None, tm, c), lambda j, i, core_ref: (core_ref[0], j, i, 0)), spec], out_specs=spec)
    return pl.pallas_call(body, name=name, grid_spec=grid_spec, out_shape=SDS((NCHIP, r, c), BF16),
                          compiler_params=_cparams(40 * 1024 * 1024, ("arbitrary", "arbitrary")))(core, parts, theirs)


def sum_chips(name, sums, landed, chip):
    _, r, c = sums.shape
    tm = _tile_rows(r, c, 4)

    def body(chip_ref, own_ref, l_ref, o_ref):
        s = own_ref[...].astype(F32)
        for t in range(NCHIP - 1):
            s = s + l_ref[t].astype(F32)
        o_ref[...] = s

    grid_spec = pltpu.PrefetchScalarGridSpec(
        num_scalar_prefetch=1, grid=(r // tm,),
        in_specs=[pl.BlockSpec((None, tm, c), lambda i, chip_ref: (chip_ref[0], i, 0)),
                  pl.BlockSpec((NCHIP - 1, tm, c), lambda i, chip_ref: (0, i, 0))],
        out_specs=pl.BlockSpec((tm, c), lambda i, chip_ref: (i, 0)))
    return pl.pallas_call(body, name=name, grid_spec=grid_spec, out_shape=SDS((r, c), F32),
                          compiler_params=_cparams(40 * 1024 * 1024, ("arbitrary",)))(chip, sums, landed)


def _adamw_math(wv, gv, mv, vv):
    mn = ADAM_B1 * mv + (1.0 - ADAM_B1) * gv
    vn = ADAM_B2 * vv + (1.0 - ADAM_B2) * (gv * gv)
    mh = mn / (1.0 - ADAM_B1 ** ADAM_STEP)
    vh = vn / (1.0 - ADAM_B2 ** ADAM_STEP)
    return -ADAM_LR * (mh / (jnp.sqrt(vh) + ADAM_EPS) + ADAM_WD * wv), mn, vn


def adamw(name, w, g, m, v):
    def body(w_ref, g_ref, m_ref, v_ref, go_ref, d_ref, mo_ref, vo_ref):
        gv = g_ref[...]
        go_ref[...] = gv
        d_ref[...], mo_ref[...], vo_ref[...] = _adamw_math(w_ref[...], gv, m_ref[...], v_ref[...])

    return pl.pallas_call(body, name=name, out_shape=[SDS(w.shape, F32)] * 4)(w, g, m, v)


def adamw_halves(name, w, g_own, g_sib, m, v, core):
    _, r, c = w.shape
    tm = _tile_rows(r, c, 10)

    def body(core_ref, w_ref, go_ref, gs_ref, m_ref, v_ref, g_out, d_out, m_out, v_out):
        gv = jnp.where(pl.program_id(0) == core_ref[0], go_ref[...], gs_ref[...])
        g_out[...] = gv
        d_out[...], m_out[...], v_out[...] = _adamw_math(w_ref[...], gv, m_ref[...], v_ref[...])

    full = pl.BlockSpec((None, tm, c), lambda h, i, core_ref: (h, i, 0))
    own = pl.BlockSpec((tm, c), lambda h, i, core_ref: (jnp.where(h == core_ref[0], i, 0), 0))
    sib = pl.BlockSpec((tm, c), lambda h, i, core_ref: (jnp.where(h == core_ref[0], 0, i), 0))
    grid_spec = pltpu.PrefetchScalarGridSpec(num_scalar_prefetch=1, grid=(2, r // tm),
                                             in_specs=[full, own, sib, full, full], out_specs=[full] * 4)
    return pl.pallas_call(body, name=name, grid_spec=grid_spec, out_shape=[SDS(w.shape, F32)] * 4,
                          compiler_params=_cparams(48 * 1024 * 1024, ("arbitrary", "arbitrary")))(core, w, g_own, g_sib, m, v)


def cast_bf16(name, w):
    _, r, c = w.shape
    tm = _tile_rows(r, c, 2)

    def body(w_ref, o_ref):
        o_ref[...] = w_ref[...].astype(BF16)

    spec = pl.BlockSpec((None, tm, c), lambda h, i: (h, i, 0))
    return pl.pallas_call(body, name=name, grid=(2, r // tm), in_specs=[spec], out_specs=spec, out_shape=SDS(w.shape, BF16),
                          compiler_params=_cparams(40 * 1024 * 1024, ("arbitrary", "arbitrary")))(w)


def pack_rows(name, parts, rows, after=None):
    width = parts[0].shape[1]
    n = len(parts)
    afters = _as_list(after)

    def body(*refs):
        out_ref = refs[n + len(afters)]
        out_ref[...] = jnp.zeros_like(out_ref)
        off = 0
        for p in refs[:n]:
            out_ref[off:off + p.shape[0], :] = p[...]
            off += p.shape[0]

    vm = pl.BlockSpec(memory_space=pltpu.VMEM)
    return pl.pallas_call(body, name=name, in_specs=[vm] * n + [ANY] * len(afters), out_specs=vm,
                          out_shape=SDS((rows, width), F32))(*parts, *afters)


def _place():
    x, y, c = lax.axis_index("x"), lax.axis_index("y"), lax.axis_index("c")
    chips = [(1 - x, y), (x, 1 - y), (1 - x, 1 - y)]
    return x, y, c, chips


def _row_split(shape, dtype):
    r, c = shape
    n = 1
    while r % (2 * n) == 0 and (r // (2 * n)) % 16 == 0 and (r // n) * c * jnp.dtype(dtype).itemsize > PIECE_BYTES:
        n *= 2
    return [pl.ds(s * (r // n), r // n) for s in range(n)]


def _pieces(ref):
    *lead, r, c = ref.shape
    split = _row_split((r, c), ref.dtype)
    return [ref.at[(*idx, s)] for idx in itertools.product(*[range(d) for d in lead]) for s in split]


HBM = pl.BlockSpec(memory_space=pltpu.HBM)
SEM = pl.BlockSpec(memory_space=pltpu.SEMAPHORE)
EFFECT = pltpu.SideEffectType.DATAFLOW_SIDE_EFFECTING


def gather_start(name, shards, after=None):
    n = len(shards)
    afters = _as_list(after)

    def body(*refs):
        src, land = refs[:n], refs[n:2 * n]
        send, recv = refs[2 * n + len(afters)], refs[2 * n + len(afters) + 1]
        x, y, c, chips = _place()
        me = 2 * x + y
        for a in range(n):
            for j, (cx, cy) in enumerate(chips[:2]):
                for sp, dp in zip(_pieces(src[a].at[c]), _pieces(land[a].at[me, c])):
                    pltpu.make_async_remote_copy(sp, dp, send.at[2 * a + j], recv.at[2 * a + j],
                                                 device_id=(cx, cy, c), device_id_type=MESH).start()

    lands = [pltpu.with_memory_space_constraint(lax.empty((NCHIP,) + s.shape, s.dtype), pltpu.HBM) for s in shards]
    srcs = [pltpu.with_memory_space_constraint(s, pltpu.HBM) for s in shards]
    outs = pl.pallas_call(
        body, name=name,
        out_shape=(pltpu.SemaphoreType.DMA((2 * n,)), pltpu.SemaphoreType.DMA((2 * n,)),
                   *[pltpu.HBM(s.shape, s.dtype) for s in shards], *[pltpu.HBM(l.shape, l.dtype) for l in lands]),
        in_specs=[HBM] * (2 * n) + [ANY] * len(afters), out_specs=(SEM, SEM, *([HBM] * (2 * n))),
        input_output_aliases={i: 2 + i for i in range(2 * n)},
        compiler_params=pltpu.CompilerParams(has_side_effects=EFFECT),
    )(*srcs, *lands, *afters)
    return outs[0], outs[1], list(outs[2:2 + n]), list(outs[2 + n:2 + 2 * n])


def _relay_blocks(land, c, chips):
    (xx, xy), (yx, yy), (dx, dy) = chips
    rows = land.shape[2] // 2
    upper, lower = pl.ds(0, rows), pl.ds(rows, rows)
    return [(land.at[2 * yx + yy, c, lower], land.at[2 * dx + dy, c, lower]),
            (land.at[2 * xx + xy, c, upper], land.at[2 * dx + dy, c, upper])]


def relay_turn(name, send, recv, shards, lands, after):
    n = len(shards)
    afters = _as_list(after)

    def body(*refs):
        src, had = refs[:n], refs[n:2 * n]
        send_ref, recv_ref = refs[2 * n], refs[2 * n + 1]
        rsend, rrecv = refs[2 * n + 2 + len(afters)], refs[2 * n + 3 + len(afters)]
        land = refs[3 * n + 4 + len(afters):4 * n + 4 + len(afters)]
        x, y, c, chips = _place()
        for a in range(n):
            for j, (cx, cy) in enumerate(chips[:2]):
                cp = pltpu.make_async_remote_copy(src[a].at[c], had[a].at[2 * cx + cy, c], send_ref.at[2 * a + j],
                                                  recv_ref.at[2 * a + j], device_id=(cx, cy, c), device_id_type=MESH)
                cp.wait_send()
                cp.wait_recv()
        for a in range(n):
            for j, ((sent, _), (dst, _)) in enumerate(zip(_relay_blocks(had[a], c, chips), _relay_blocks(land[a], c, chips))):
                cx, cy = chips[j]
                for sp, dp in zip(_pieces(sent), _pieces(dst)):
                    pltpu.make_async_remote_copy(sp, dp, rsend.at[2 * a + j], rrecv.at[2 * a + j],
                                                 device_id=(cx, cy, c), device_id_type=MESH).start()

    outs = pl.pallas_call(
        body, name=name,
        out_shape=(pltpu.SemaphoreType.DMA((2 * n,)), pltpu.SemaphoreType.DMA((2 * n,)),
                   *[pltpu.HBM(s.shape, s.dtype) for s in shards], *[pltpu.HBM(l.shape, l.dtype) for l in lands]),
        in_specs=[HBM] * (2 * n) + [SEM, SEM] + [ANY] * len(afters), out_specs=(SEM, SEM, *([HBM] * (2 * n))),
        input_output_aliases={i: 2 + i for i in range(2 * n)},
        compiler_params=pltpu.CompilerParams(has_side_effects=EFFECT),
    )(*shards, *lands, send, recv, *afters)
    return outs[0], outs[1], list(outs[2:2 + n]), list(outs[2 + n:2 + 2 * n])


def relay_wait(name, send, recv, lands, after):
    n = len(lands)
    afters = _as_list(after)

    def body(*refs):
        land = refs[:n]
        send_ref, recv_ref = refs[n], refs[n + 1]
        x, y, c, chips = _place()
        for a in range(n):
            for j, (sent, got) in enumerate(_relay_blocks(land[a], c, chips)):
                cx, cy = chips[j]
                cp = pltpu.make_async_remote_copy(sent, got, send_ref.at[2 * a + j], recv_ref.at[2 * a + j],
                                                  device_id=(cx, cy, c), device_id_type=MESH)
                cp.wait_send()
                cp.wait_recv()

    outs = pl.pallas_call(
        body, name=name, out_shape=tuple(pltpu.HBM(l.shape, l.dtype) for l in lands),
        in_specs=[HBM] * n + [SEM, SEM] + [ANY] * len(afters), out_specs=[HBM] * n,
        input_output_aliases={i: i for i in range(n)},
        compiler_params=pltpu.CompilerParams(has_side_effects=EFFECT),
    )(*lands, send, recv, *afters)
    return list(outs)


def forward_halves(name, shards, lands):
    n = len(lands)

    def body(*refs):
        had, buf = refs[:n], refs[n:2 * n]
        send, recv = refs[2 * n:]
        x, y, c, chips = _place()
        sib = (x, y, 1 - c)
        for a in range(n):
            for j, (cx, cy) in enumerate(chips):
                for sp, dp in zip(_pieces(had[a].at[2 * cx + cy, c]), _pieces(buf[a].at[2 * cx + cy, c])):
                    pltpu.make_async_remote_copy(sp, dp, send.at[3 * a + j], recv.at[3 * a + j], device_id=sib, device_id_type=MESH).start()
        for a in range(n):
            for j, (cx, cy) in enumerate(chips):
                pltpu.make_async_remote_copy(had[a].at[2 * cx + cy, c], buf[a].at[2 * cx + cy, 1 - c], send.at[3 * a + j],
                                             recv.at[3 * a + j], device_id=sib, device_id_type=MESH).wait()

    got = pl.pallas_call(
        body, name=name, in_specs=[ANY] * n, out_specs=[ANY] * n, out_shape=[SDS(l.shape, l.dtype) for l in lands],
        input_output_aliases={i: i for i in range(n)},
        scratch_shapes=[pltpu.SemaphoreType.DMA((3 * n,)), pltpu.SemaphoreType.DMA((3 * n,))],
    )(*lands)
    me = 2 * lax.axis_index("x") + lax.axis_index("y")
    return [lax.dynamic_update_index_in_dim(g, s, me, 0) for g, s in zip(got, shards)]


def forward_turn(name, send, recv, lands, after):
    n = len(lands)
    afters = _as_list(after)

    def body(*refs):
        had = refs[:n]
        send_ref, recv_ref = refs[n], refs[n + 1]
        fsend, frecv = refs[n + 2 + len(afters)], refs[n + 3 + len(afters)]
        buf = refs[n + 4 + len(afters):2 * n + 4 + len(afters)]
        x, y, c, chips = _place()
        sib = (x, y, 1 - c)
        for a in range(n):
            for j, (sent, got) in enumerate(_relay_blocks(had[a], c, chips)):
                cx, cy = chips[j]
                cp = pltpu.make_async_remote_copy(sent, got, send_ref.at[2 * a + j], recv_ref.at[2 * a + j],
                                                  device_id=(cx, cy, c), device_id_type=MESH)
                cp.wait_send()
                cp.wait_recv()
        for a in range(n):
            for j, (cx, cy) in enumerate(chips):
                for sp, dp in zip(_pieces(had[a].at[2 * cx + cy, c]), _pieces(buf[a].at[2 * cx + cy, c])):
                    pltpu.make_async_remote_copy(sp, dp, fsend.at[3 * a + j], frecv.at[3 * a + j], device_id=sib, device_id_type=MESH).start()

    outs = pl.pallas_call(
        body, name=name,
        out_shape=(pltpu.SemaphoreType.DMA((3 * n,)), pltpu.SemaphoreType.DMA((3 * n,)), *[pltpu.HBM(l.shape, l.dtype) for l in lands]),
        in_specs=[HBM] * n + [SEM, SEM] + [ANY] * len(afters), out_specs=(SEM, SEM, *([HBM] * n)),
        input_output_aliases={i: 2 + i for i in range(n)},
        compiler_params=pltpu.CompilerParams(has_side_effects=EFFECT),
    )(*lands, send, recv, *afters)
    return outs[0], outs[1], list(outs[2:])


def forward_wait(name, send, recv, lands, after):
    n = len(lands)
    afters = _as_list(after)

    def body(*refs):
        land = refs[:n]
        send_ref, recv_ref = refs[n], refs[n + 1]
        x, y, c, chips = _place()
        sib = (x, y, 1 - c)
        for a in range(n):
            for j, (cx, cy) in enumerate(chips):
                cp = pltpu.make_async_remote_copy(land[a].at[2 * cx + cy, c], land[a].at[2 * cx + cy, 1 - c], send_ref.at[3 * a + j],
                                                  recv_ref.at[3 * a + j], device_id=sib, device_id_type=MESH)
                cp.wait_send()
                cp.wait_recv()

    outs = pl.pallas_call(
        body, name=name, out_shape=tuple(pltpu.HBM(l.shape, l.dtype) for l in lands),
        in_specs=[HBM] * n + [SEM, SEM] + [ANY] * len(afters), out_specs=[HBM] * n,
        input_output_aliases={i: i for i in range(n)},
        compiler_params=pltpu.CompilerParams(has_side_effects=EFFECT),
    )(*lands, send, recv, *afters)
    return list(outs)


def exchange_start(name, parts):
    n = len(parts)

    def body(*refs):
        src, got = refs[:n], refs[n:2 * n]
        send, recv = refs[2 * n], refs[2 * n + 1]
        token = refs[4 * n + 2]
        x, y, c, _ = _place()
        sib = (x, y, 1 - c)
        for a in range(n):
            for sp, dp in zip(_pieces(src[a].at[1 - c]), _pieces(got[a])):
                pltpu.make_async_remote_copy(sp, dp, send.at[a], recv.at[a], device_id=sib, device_id_type=MESH).start()
        token[...] = jnp.zeros_like(token)

    lands = [pltpu.with_memory_space_constraint(lax.empty(p.shape[1:], p.dtype), pltpu.HBM) for p in parts]
    srcs = [pltpu.with_memory_space_constraint(p, pltpu.HBM) for p in parts]
    outs = pl.pallas_call(
        body, name=name,
        out_shape=(pltpu.SemaphoreType.DMA((n,)), pltpu.SemaphoreType.DMA((n,)),
                   *[pltpu.HBM(p.shape, p.dtype) for p in parts], *[pltpu.HBM(l.shape, l.dtype) for l in lands],
                   SDS((8, 128), F32)),
        in_specs=[HBM] * (2 * n), out_specs=(SEM, SEM, *([HBM] * (2 * n)), pl.BlockSpec(memory_space=pltpu.VMEM)),
        input_output_aliases={i: 2 + i for i in range(2 * n)},
        compiler_params=pltpu.CompilerParams(has_side_effects=EFFECT),
    )(*srcs, *lands)
    return outs[0], outs[1], list(outs[2:2 + n]), list(outs[2 + n:2 + 2 * n]), outs[2 + 2 * n]


def exchange_wait(name, send, recv, parts, lands, after):
    n = len(parts)
    afters = _as_list(after)

    def body(*refs):
        src, got = refs[:n], refs[n:2 * n]
        send_ref, recv_ref = refs[2 * n], refs[2 * n + 1]
        x, y, c, _ = _place()
        sib = (x, y, 1 - c)
        for a in range(n):
            cp = pltpu.make_async_remote_copy(src[a].at[1 - c], got[a], send_ref.at[a], recv_ref.at[a], device_id=sib, device_id_type=MESH)
            cp.wait_send()
            cp.wait_recv()

    outs = pl.pallas_call(
        body, name=name,
        out_shape=(*[pltpu.HBM(p.shape, p.dtype) for p in parts], *[pltpu.HBM(l.shape, l.dtype) for l in lands]),
        in_specs=[HBM] * (2 * n) + [SEM, SEM] + [ANY] * len(afters), out_specs=[HBM] * (2 * n),
        input_output_aliases={i: i for i in range(2 * n)},
        compiler_params=pltpu.CompilerParams(has_side_effects=EFFECT),
    )(*parts, *lands, send, recv, *afters)
    return list(outs[:n]), list(outs[n:])


def scatter_start(name, parts):
    n = len(parts)

    def body(*refs):
        src, land = refs[:n], refs[n:2 * n]
        send, recv = refs[2 * n], refs[2 * n + 1]
        token = refs[4 * n + 2]
        x, y, c, chips = _place()
        for a in range(n):
            for j, (cx, cy) in enumerate(chips):
                for sp, dp in zip(_pieces(src[a].at[2 * cx + cy]), _pieces(land[a].at[j])):
                    pltpu.make_async_remote_copy(sp, dp, send.at[3 * a + j], recv.at[3 * a + j],
                                                 device_id=(cx, cy, c), device_id_type=MESH).start()
        token[...] = jnp.zeros_like(token)

    lands = [pltpu.with_memory_space_constraint(lax.empty((NCHIP - 1,) + p.shape[1:], p.dtype), pltpu.HBM) for p in parts]
    srcs = [pltpu.with_memory_space_constraint(p, pltpu.HBM) for p in parts]
    outs = pl.pallas_call(
        body, name=name,
        out_shape=(pltpu.SemaphoreType.DMA((3 * n,)), pltpu.SemaphoreType.DMA((3 * n,)),
                   *[pltpu.HBM(p.shape, p.dtype) for p in parts], *[pltpu.HBM(l.shape, l.dtype) for l in lands],
                   SDS((8, 128), F32)),
        in_specs=[HBM] * (2 * n), out_specs=(SEM, SEM, *([HBM] * (2 * n)), pl.BlockSpec(memory_space=pltpu.VMEM)),
        input_output_aliases={i: 2 + i for i in range(2 * n)},
        compiler_params=pltpu.CompilerParams(has_side_effects=EFFECT),
    )(*srcs, *lands)
    return outs[0], outs[1], list(outs[2:2 + n]), list(outs[2 + n:2 + 2 * n]), outs[2 + 2 * n]


def scatter_wait(name, send, recv, parts, lands, after):
    n = len(parts)
    afters = _as_list(after)

    def body(*refs):
        src, land = refs[:n], refs[n:2 * n]
        send_ref, recv_ref = refs[2 * n], refs[2 * n + 1]
        x, y, c, chips = _place()
        for a in range(n):
            for j, (cx, cy) in enumerate(chips):
                cp = pltpu.make_async_remote_copy(src[a].at[2 * cx + cy], land[a].at[j], send_ref.at[3 * a + j], recv_ref.at[3 * a + j],
                                                  device_id=(cx, cy, c), device_id_type=MESH)
                cp.wait_send()
                cp.wait_recv()

    outs = pl.pallas_call(
        body, name=name,
        out_shape=(*[pltpu.HBM(p.shape, p.dtype) for p in parts], *[pltpu.HBM(l.shape, l.dtype) for l in lands]),
        in_specs=[HBM] * (2 * n) + [SEM, SEM] + [ANY] * len(afters), out_specs=[HBM] * (2 * n),
        input_output_aliases={i: i for i in range(2 * n)},
        compiler_params=pltpu.CompilerParams(has_side_effects=EFFECT),
    )(*parts, *lands, send, recv, *afters)
    return list(outs[:n]), list(outs[n:])


def join_start(name, halves):
    n = len(halves)

    def body(*refs):
        src, dst = refs[:n], refs[n:2 * n]
        send, recv = refs[2 * n], refs[2 * n + 1]
        token = refs[4 * n + 2]
        x, y, c, _ = _place()
        sib = (x, y, 1 - c)
        for a in range(n):
            for sp, dp in zip(_pieces(src[a]), _pieces(dst[a])):
                pltpu.make_async_remote_copy(sp, dp, send.at[a], recv.at[a], device_id=sib, device_id_type=MESH).start()
        token[...] = jnp.zeros_like(token)

    lands = [pltpu.with_memory_space_constraint(lax.empty(h.shape, h.dtype), pltpu.HBM) for h in halves]
    srcs = [pltpu.with_memory_space_constraint(h, pltpu.HBM) for h in halves]
    outs = pl.pallas_call(
        body, name=name,
        out_shape=(pltpu.SemaphoreType.DMA((n,)), pltpu.SemaphoreType.DMA((n,)),
                   *[pltpu.HBM(h.shape, h.dtype) for h in halves], *[pltpu.HBM(l.shape, l.dtype) for l in lands],
                   SDS((8, 128), F32)),
        in_specs=[HBM] * (2 * n), out_specs=(SEM, SEM, *([HBM] * (2 * n)), pl.BlockSpec(memory_space=pltpu.VMEM)),
        input_output_aliases={i: 2 + i for i in range(2 * n)},
        compiler_params=pltpu.CompilerParams(has_side_effects=EFFECT),
    )(*srcs, *lands)
    return outs[0], outs[1], list(outs[2:2 + n]), list(outs[2 + n:2 + 2 * n]), outs[2 + 2 * n]


def join_wait(name, send, recv, halves, lands, after):
    n = len(halves)
    afters = _as_list(after)

    def body(*refs):
        src, dst = refs[:n], refs[n:2 * n]
        send_ref, recv_ref = refs[2 * n], refs[2 * n + 1]
        x, y, c, _ = _place()
        sib = (x, y, 1 - c)
        for a in range(n):
            cp = pltpu.make_async_remote_copy(src[a], dst[a], send_ref.at[a], recv_ref.at[a], device_id=sib, device_id_type=MESH)
            cp.wait_send()
            cp.wait_recv()

    outs = pl.pallas_call(
        body, name=name,
        out_shape=(*[pltpu.HBM(h.shape, h.dtype) for h in halves], *[pltpu.HBM(l.shape, l.dtype) for l in lands]),
        in_specs=[HBM] * (2 * n) + [SEM, SEM] + [ANY] * len(afters), out_specs=[HBM] * (2 * n),
        input_output_aliases={i: i for i in range(2 * n)},
        compiler_params=pltpu.CompilerParams(has_side_effects=EFFECT),
    )(*halves, *lands, send, recv, *afters)
    return list(outs[:n]), list(outs[n:])


def gather_small(name, xs, reduce, after=None):
    m, ncol = xs.shape
    afters = _as_list(after)

    def body(x_ref, *rest):
        out_ref, all_ref, send, recv, lsem = rest[len(afters):]
        x, y, c, chips = _place()
        me, sib = (x, y, c), (x, y, 1 - c)

        def rows(px, py, pc):
            return all_ref.at[pl.ds((4 * px + 2 * py + pc) * m, m), :]

        def copy(k, block, to, src=None):
            return pltpu.make_async_remote_copy(rows(*block) if src is None else src, rows(*block), send.at[k], recv.at[k],
                                                device_id=to, device_id_type=MESH)

        mine = pltpu.make_async_copy(x_ref, rows(*me), lsem)
        mine.start()
        first = [copy(0, me, sib, src=x_ref)] + [copy(1 + j, me, (*chip, c), src=x_ref) for j, chip in enumerate(chips)]
        for cp in first:
            cp.start()
        passed = [copy(4 + j, (*chip, c), sib) for j, chip in enumerate(chips)]
        for j, chip in enumerate(chips):
            copy(1 + j, (*chip, c), me).wait_recv()
            passed[j].start()
        copy(0, sib, me).wait_recv()
        for j, chip in enumerate(chips):
            copy(4 + j, (*chip, 1 - c), me).wait_recv()
        for cp in first + passed:
            cp.wait_send()
        mine.wait()
        if reduce:
            s = all_ref[0:m, :]
            for dev in range(1, 8):
                s = s + all_ref[dev * m:(dev + 1) * m, :]
            out_ref[...] = s
        else:
            out_ref[...] = all_ref[...]

    vm = pl.BlockSpec(memory_space=pltpu.VMEM)
    return pl.pallas_call(
        body, name=name, in_specs=[vm] + [ANY] * len(afters), out_specs=vm,
        out_shape=SDS((m, ncol) if reduce else (8 * m, ncol), F32),
        scratch_shapes=[pltpu.VMEM((8 * m, ncol), F32), pltpu.SemaphoreType.DMA((7,)), pltpu.SemaphoreType.DMA((7,)),
                        pltpu.SemaphoreType.DMA],
    )(xs, *afters)


RELAYOUT_ROWS = 128


def weights_to_cat(name, land, own, place, other, prev=None, after=None):
    tm = RELAYOUT_ROWS
    nb = (D // 2) // tm
    extra = ([] if prev is None else [prev]) + _as_list(after)

    def half(p):
        return 1 - p[0] if other else p[0]

    def body(p_ref, g_ref, own_ref, *rest):
        o_ref = rest[len(extra)]
        nat = jnp.concatenate([jnp.where(p_ref[1] == j, own_ref[...], g_ref[j]) for j in range(NCHIP)], axis=1)
        pad = jnp.zeros((tm, NCAT - OA - 16), BF16)
        o_ref[...] = jnp.concatenate([nat[:, 3072:7168], nat[:, 7184:11280], nat[:, 0:3072], nat[:, 7168:7184], pad], axis=1)

    grid_spec = pltpu.PrefetchScalarGridSpec(
        num_scalar_prefetch=1, grid=(nb,),
        in_specs=[pl.BlockSpec((NCHIP, None, tm, IN_SHARD), lambda i, p: (0, half(p), i, 0)),
                  pl.BlockSpec((None, tm, IN_SHARD), lambda i, p: (half(p), i, 0))] + [ANY] * len(extra),
        out_specs=pl.BlockSpec((tm, NCAT), lambda i, p: (half(p) * nb + i, 0)))
    return pl.pallas_call(
        body, name=name, grid_spec=grid_spec, out_shape=SDS((D, NCAT), BF16),
        input_output_aliases={} if prev is None else {3: 0},
        compiler_params=_cparams(40 * 1024 * 1024, ("arbitrary",)),
    )(place, land, own, *extra)


def grads_from_cat(gw_cat):
    tm = RELAYOUT_ROWS
    nb = (D // 2) // tm

    def body(c_ref, o_ref):
        cat = c_ref[...]
        nat = jnp.concatenate([cat[:, OU:OA], cat[:, OV:OGP], cat[:, OA:OA + 16], cat[:, OGP:OU]], axis=1)
        for j in range(NCHIP):
            o_ref[j] = nat[:, j * IN_SHARD:(j + 1) * IN_SHARD]

    return pl.pallas_call(
        body, name="grads_from_cat", grid=(D // tm,), in_specs=[pl.BlockSpec((tm, NCAT), lambda i: (i, 0))],
        out_specs=pl.BlockSpec((None, NCHIP, tm, IN_SHARD), lambda i: (i // nb, 0, i % nb, 0)),
        out_shape=SDS((2, NCHIP, D // 2, IN_SHARD), BF16), compiler_params=_cparams(40 * 1024 * 1024, ("arbitrary",)),
    )(gw_cat)


def _pad_rows(a, rows):
    return jnp.concatenate([a, jnp.zeros((rows - a.shape[0],) + a.shape[1:], a.dtype)], axis=0)


def local_step(x2d, tgt, gf, g1, pool_scale, wa_pad, b_alpha, ng, g2, get_w, on_grad=None, on_settle=None, tick=None):
    emit = on_grad if on_grad is not None else (lambda group, grads: None)
    settle = on_settle if on_settle is not None else (lambda group, after: None)
    h1 = norm1(x2d, g1)
    wcat, pw = get_w("in", h1)
    pcat = mm_in(h1, wcat)
    dpool, ylin = pool_fwd(pcat, pw)
    pinned = tick("pool", ylin) if tick is not None else None
    og, o, states = gla_fwd(pcat, wa_pad, b_alpha, ng, pinned)
    w_go, w_o = get_w("mid", og)
    mixed, ygla = mm_gla_out(og, w_go, ylin, pcat, pool_scale)
    x2, h2 = mm_out(mixed, w_o, x2d, g2)
    w_up = get_w("up", h2)
    rup, act = mm_up(h2, w_up)
    w_dn = get_w("down", act)
    dx3, dx3b, g_nf, loss_row = mm_down(act, w_dn, x2, tgt, gf)

    gw_down = mm_wgrad("mm_dw_down", act, dx3b, DFF, D, (2, NCHIP, D // 2, D), (None, None, D // 2, D),
                       lambda j, i, k: (i % 2, i // 2, 0, 0), D // 2, D)
    token = emit("down", {"down": gw_down})
    dup = mm_dact(dx3b, w_dn, rup, after=token)
    token = settle("down", dup)
    dx2, dx2b, g_mlp = mm_dh2(dup, w_up, x2, dx3, g2, after=token)
    gw_up = mm_wgrad("mm_dw_up", h2, dup, D, DFF, (2, NCHIP, D // 2, D), (None, None, D // 2, D),
                     lambda j, i, k: (i, j, 0, 0), D // 2, D)
    token = emit("up", {"up": gw_up})
    dylin, dygla, dlgp, dlgg, g_ps = mm_dmixed(dx2b, w_o, pcat, ylin, ygla, pool_scale, after=token)
    token = settle("up", dylin)
    gw_out = mm_wgrad("mm_dw_out", mixed, dx2b, D, D, (2, NCHIP, 256, D), (2, None, 256, D),
                      lambda j, i, k: (0, i, 0, 0), 512, D)
    do, dg, g_ng = mm_dog(dygla, w_go, o, pcat, ng, after=token)
    gw_go = mm_wgrad("mm_dw_gla_out", og, dygla, D, D, (2, NCHIP, 256, D), (2, None, 256, D),
                     lambda j, i, k: (0, i, 0, 0), 512, D)
    token = emit("mix", {"out": gw_out, "gla_out": gw_go})
    dq, dk, dv, dalow, g_wa, g_ba = gla_bwd(do, pcat, states, wa_pad, b_alpha, b_alpha if token is None else token)
    token = settle("mix", dq)
    du, dpw = pool_bwd(dylin, dpool, pw)
    dpcat = jnp.concatenate([dv, dg, dlgp, dlgg, du, dq, dk, dalow, jnp.zeros((T, NCAT - OA - APAD), BF16)], axis=1)
    gw_cat = mm_wgrad("mm_dw_in", h1, dpcat, D, NCAT, (D, NCAT), (1024, 1280), lambda j, i, k: (i, j), 1024, 1280, after=token)
    token = settle("in", emit("in", {"in_cat": gw_cat, "pool": dpw}))
    grad_x, g_mix = mm_dh1(dpcat, wcat, x2d, dx2, g1, after=token)
    return (loss_row[0, 0], grad_x, g_mix, g_ps, g_mlp, g_nf, g_ng, g_ba, g_wa, token,
            gw_cat, dpw, gw_go, gw_out, gw_up, gw_down)


def kernel(x, norm_mix_g, w_in, pool_w, pool_scale, w_alpha, b_alpha, gla_norm_g, w_gla_out, w_out, norm_mlp_g, w_mlp_up, w_mlp_down, norm_final_g, loss_target, m_norm_mix_g, m_w_in, m_pool_w, m_pool_scale, m_w_alpha, m_b_alpha, m_gla_norm_g, m_w_gla_out, m_w_out, m_norm_mlp_g, m_w_mlp_up, m_w_mlp_down, m_norm_final_g, v_norm_mix_g, v_w_in, v_pool_w, v_pool_scale, v_w_alpha, v_b_alpha, v_gla_norm_g, v_w_gla_out, v_w_out, v_norm_mlp_g, v_w_mlp_up, v_w_mlp_down, v_norm_final_g):
    chip = 2 * lax.axis_index("x") + lax.axis_index("y")
    chip_i = chip.astype(jnp.int32).reshape(1)
    core_i = lax.axis_index("c").astype(jnp.int32).reshape(1)
    place_i = jnp.concatenate([core_i, chip_i])
    tgt = loss_target.reshape(T, D)
    gf = norm_final_g.reshape(1, D)

    def halves(w2d):
        r, c = w2d.shape
        return w2d.astype(BF16).reshape(2, r // 2, c)

    pool_shard = pool_w.reshape(4 * PG, PO // NCHIP)
    w_in_r = w_in.reshape(2, D // 2, IN_SHARD)
    sent = {"in": [cast_bf16("cast_w_in", w_in_r), halves(pool_shard)]}
    flight = {}

    def start(group, after=None):
        flight[group] = gather_start("gather_start_" + group, sent[group], after)

    def relay(group, after):
        send, recv, shards, lands = flight[group]
        flight[group] = relay_turn("relay_turn_" + group, send, recv, shards, lands, after)

    def fetch(group, after):
        send, recv, shards, lands = flight[group]
        lands = relay_wait("relay_wait_" + group, send, recv, lands, after)
        return forward_halves("forward_" + group, shards, lands)

    start("in")
    m_in_f, v_in_f, w_go_f, w_o_f, w_up_f, w_dn_f, x_f, wal_f, gng_f = lax.optimization_barrier(
        (m_w_in, v_w_in, w_gla_out, w_out, w_mlp_up, w_mlp_down, x, w_alpha, gla_norm_g, flight["in"][2][0]))[:9]
    m_in_r, v_in_r = m_in_f.reshape(2, D // 2, IN_SHARD), v_in_f.reshape(2, D // 2, IN_SHARD)
    sent["mid"] = [halves(w_go_f[0]), halves(w_o_f[0])]
    relay("in", [m_in_r, v_in_r, *sent["mid"]])
    w_up_f, w_dn_f, x_f, wal_f, gng_f = lax.optimization_barrier(
        (w_up_f, w_dn_f, x_f, wal_f, gng_f, flight["in"][3][0]))[:5]
    sent["up"], sent["down"] = [halves(w_up_f[0])], [halves(w_dn_f[0])]
    x2d = x_f.reshape(T, D)
    big = [w_in_r, w_go_f[0], w_o_f[0], w_up_f[0], w_dn_f[0], pool_shard]

    def tick(point, after):
        if point == "pool":
            relay("mid", after)
            relay("up", flight["mid"][3][0])
            start("down", flight["up"][3][0])
            return [flight["up"][3][0], flight["down"][3][0]]

    def get_w(group, after):
        if group == "in":
            after = [after, *sent["up"], *sent["down"], wa_pad]
        if group == "up":
            relay("down", after)
            send, recv, lands, shards = flight["up"]
            lands = forward_wait("forward_wait_up", send, recv, lands, flight["down"][3][0])
            return lax.dynamic_update_index_in_dim(lands[0], shards[0], chip, 0).reshape(NCHIP, D, D)
        if group == "in":
            send, recv, shards, lands = flight["in"]
            send, recv, lands = forward_turn("forward_turn_in", send, recv, lands, after)
            start("mid", lands[0])
            start("up", flight["mid"][3][0])
            wcat = weights_to_cat("weights_to_cat_mine", lands[0], shards[0], place_i, False, after=flight["up"][3][0])
            lands = forward_wait("forward_wait_in", send, recv, lands, wcat)
            wcat = weights_to_cat("weights_to_cat_sibling", lands[0], shards[0], place_i, True, prev=wcat)
            g_pool = lax.dynamic_update_index_in_dim(lands[1], shards[1], chip, 0)
            pw = jnp.concatenate([g_pool[j].reshape(4, PG, PO // NCHIP) for j in range(NCHIP)], axis=2)
            return wcat, pw
        whole = fetch(group, after)
        if group == "mid":
            send, recv, shards, lands = flight["up"]
            flight["up"] = (*forward_turn("forward_turn_up", send, recv, lands, whole[0]), shards)
            w_go, w_o, _ = lax.optimization_barrier((whole[0], whole[1], flight["up"][2][0]))
            return w_go.reshape(D, D), w_o.reshape(D, D)
        return whole[0].reshape(DFF, D)

    small_w = pack_rows("pack_small_w", [wal_f[0].reshape(4, QK),
                                         jnp.concatenate([gng_f[0].reshape(1, 512), jnp.zeros((1, 512), F32)], axis=1)], 8)
    sw_all = gather_small("gather_small_w", small_w, False).reshape(8, 8, QK)
    wa_full = jnp.concatenate([sw_all[2 * j, 0:4].reshape(16, DK) for j in range(NCHIP)], axis=1)
    ng_full = jnp.concatenate([sw_all[2 * j, 4, 0:512].reshape(HEADS, DV // NCHIP) for j in range(NCHIP)], axis=1)
    wa_pad = _pad_rows(wa_full, APAD).astype(BF16)
    ng = ng_full.reshape(1, D)

    pending = {}
    wmv = {"in": (w_in_r, m_in_r, v_in_r), "gla_out": (big[1], m_w_gla_out, v_w_gla_out), "out": (big[2], m_w_out, v_w_out),
           "up": (big[3], m_w_mlp_up, v_w_mlp_up), "down": (big[4], m_w_mlp_down, v_w_mlp_down), "pool": (big[5], m_pool_w, v_pool_w)}
    big_res = {}

    def reduce_group(group, after):
        nms, send, recv, sums, lands = pending[group]
        sums, lands = scatter_wait("scatter_wait_" + group, send, recv, sums, lands, after)
        reduced = [sum_chips("sum_chips_" + nm, a, b, chip_i) for nm, a, b in zip(nms, sums, lands)]
        send, recv, reduced, lands, token = join_start("join_start_" + group, reduced)
        pending[group] = (nms, send, recv, reduced, lands)
        return token

    def update_group(group, after):
        nms, send, recv, reduced, lands = pending[group]
        reduced, from_sib = join_wait("join_wait_" + group, send, recv, reduced, lands, after)
        for nm, g_own, g_sib in zip(nms, reduced, from_sib):
            w, m, v = wmv[nm]
            shp = (2,) + g_own.shape
            big_res[nm] = adamw_halves("adamw_" + nm, w.reshape(shp), g_own, g_sib, m.reshape(shp), v.reshape(shp), core_i)

    def on_grad(group, grads):
        if group == "in":
            gw_in = grads_from_cat(grads["in_cat"])
            gw_pool = jnp.stack([grads["pool"][:, :, j * 128:(j + 1) * 128].reshape(2, 2 * PG, 128)
                                 for j in range(NCHIP)], axis=1)
            grads = {"in": gw_in, "pool": gw_pool}
        nms, parts = list(grads.keys()), list(grads.values())
        send, recv, parts, got, token = exchange_start("exchange_start_" + group, parts)
        pending[group] = (nms, send, recv, parts, got)
        return token

    def on_settle(group, after):
        if group == "in":
            for earlier in ("down", "up", "mix"):
                after = reduce_group(earlier, after)
        nms, send, recv, parts, got = pending[group]
        parts, got = exchange_wait("exchange_wait_" + group, send, recv, parts, got, after)
        sums = [add_pairs("add_pair_" + nm, a, b, core_i) for nm, a, b in zip(nms, parts, got)]
        send, recv, sums, lands, token = scatter_start("scatter_start_" + group, sums)
        pending[group] = (nms, send, recv, sums, lands)
        if group != "in":
            return token
        for earlier in ("down", "up", "mix"):
            update_group(earlier, token)
            token = big_res[pending[earlier][0][-1]][1]
        return [big_res[nm][1] for nm in ("down", "up", "out", "gla_out")]

    (loss_local, grad_x, g_mix, g_ps, g_mlp, g_nf, g_ng, g_ba, g_wa) = local_step(
        x2d, tgt, gf, norm_mix_g, pool_scale, wa_pad, b_alpha, ng, norm_mlp_g, get_w, on_grad, on_settle, tick)[:9]
    loss = lax.psum(loss_local, ("x", "y", "c"))
    join_in_token = reduce_group("in", grad_x)

    ROWS = 16

    def wide(a, n):
        return jnp.concatenate([a.reshape(1, n), jnp.zeros((1, D - n), F32)], axis=1)

    packed = pack_rows("pack_small_g", [g_mix, g_ps, g_mlp, g_nf, g_ng, wide(g_ba, QK), g_wa[0:16].reshape(8, D)], ROWS)
    tot = gather_small("reduce_small_g", packed, True, join_in_token)
    t_wa = lax.dynamic_slice(tot[6:14].reshape(16, QK), (0, chip * DK), (16, DK))
    t_ng = lax.dynamic_slice(tot[4].reshape(HEADS, DV), (0, chip * (DV // NCHIP)), (HEADS, DV // NCHIP))

    def pack_small(nm, mix, ps, mlp, nf, ba, wa, gn, after=None):
        return pack_rows(nm, [mix.reshape(1, D), ps.reshape(1, D), mlp.reshape(1, D), nf.reshape(1, D), wide(ba, QK),
                              wa.reshape(2, D), wide(gn, 512)], ROWS, after)

    update_group("in", tot)
    sg = pack_small("pack_g", tot[0], tot[1], tot[2], tot[3], tot[5, 0:QK], t_wa, t_ng, big_res["in"][3])
    sw = pack_small("pack_w", norm_mix_g, pool_scale, norm_mlp_g, norm_final_g, b_alpha, w_alpha, gla_norm_g)
    sm = pack_small("pack_m", m_norm_mix_g, m_pool_scale, m_norm_mlp_g, m_norm_final_g, m_b_alpha, m_w_alpha, m_gla_norm_g)
    sv = pack_small("pack_v", v_norm_mix_g, v_pool_scale, v_norm_mlp_g, v_norm_final_g, v_b_alpha, v_w_alpha, v_gla_norm_g)
    small_res = adamw("adamw_small", sw, sg, sm, sv)

    def unpack(p):
        return {"norm_mix_g": p[0].reshape(1, D), "pool_scale": p[1].reshape(1, D), "norm_mlp_g": p[2].reshape(1, D),
                "norm_final_g": p[3].reshape(D), "b_alpha": p[4, 0:QK].reshape(1, QK), "w_alpha": p[5:7].reshape(1, 16, DK),
                "gla_norm_g": p[7, 0:512].reshape(1, HEADS, DV // NCHIP)}

    order = ["norm_mix_g", "w_in", "pool_w", "pool_scale", "w_alpha", "b_alpha", "gla_norm_g", "w_gla_out", "w_out",
             "norm_mlp_g", "w_mlp_up", "w_mlp_down", "norm_final_g"]
    big_key = {"w_in": ("in", w_in.shape), "pool_w": ("pool", pool_w.shape), "w_gla_out": ("gla_out", w_gla_out.shape),
               "w_out": ("out", w_out.shape), "w_mlp_up": ("up", w_mlp_up.shape), "w_mlp_down": ("down", w_mlp_down.shape)}
    result = [loss, grad_x.reshape(1, T, D)]
    for kind in range(4):
        small = unpack(small_res[kind])
        for nm in order:
            if nm in big_key:
                key, shp = big_key[nm]
                result.append(big_res[key][kind].reshape(shp))
            else:
                result.append(small[nm])
    return tuple(result)
```

```python
import itertools

import jax
import jax.numpy as jnp
from jax import lax
from jax.experimental import pallas as pl
from jax.experimental.pallas import tpu as pltpu

F32 = jnp.float32
BF16 = jnp.bfloat16
SDS = jax.ShapeDtypeStruct
MESH = pl.DeviceIdType.MESH
ANY = pl.BlockSpec(memory_space=pl.ANY)

T = 2048
D = 2048
DFF = 8192
NCHIP = 4
IN_WIDTH = 11280
IN_SHARD = IN_WIDTH // NCHIP
CHUNK = 64
NCHUNK = T // CHUNK
HEADS = 4
DK = 256
DV = 512
QK = HEADS * DK
EPS = 1e-6
POOL_WINDOWS = (2, 4, 8, 16)
PG = 256
PO = 512

OV, OG, OGP, OGG, OU, OQ, OKK, OA = 0, 2048, 4096, 6144, 8192, 9216, 10240, 11264
NCAT = 11520
APAD = 128

VMEM_CAP = 56 * 1024 * 1024

PIECE_BYTES = 384 * 1024

ADAM_LR, ADAM_B1, ADAM_B2, ADAM_EPS, ADAM_WD, ADAM_STEP = 0.001, 0.9, 0.999, 1e-08, 0.01, 10


def _cparams(vmem_bytes=None, sem=None):
    kw = {}
    if vmem_bytes is not None:
        kw["vmem_limit_bytes"] = int(min(max(vmem_bytes, 32 * 1024 * 1024), VMEM_CAP))
    if sem is not None:
        kw["dimension_semantics"] = sem
    return pltpu.CompilerParams(**kw)


def _nbytes(shape, dtype):
    n = 1
    for s in shape:
        if s is not None:
            n *= s
    return n * jnp.dtype(dtype).itemsize


def _sigmoid(x):
    return 0.5 * jnp.tanh(0.5 * x) + 0.5


GLA_STEP = 4
EPI_COLS = 512


def _as_list(after):
    if after is None:
        return []
    return list(after) if isinstance(after, (list, tuple)) else [after]


def matmul(name, a, b, *, a_spec, b_spec, cdims, grid, acc_shape, outs, extras=(), epi, after=None, into=None):
    nj, ni, nk = grid
    ne, no = len(extras), len(outs)
    afters = _as_list(after) + ([] if into is None else [into[0]])
    first_out = 2 + ne + len(afters)

    def body(*refs):
        a_ref, b_ref = refs[0], refs[1]
        ex = refs[2:2 + ne]
        out_refs = refs[first_out:first_out + no]
        i = pl.program_id(1)
        part = lax.dot_general(a_ref[...], b_ref[...], (cdims, ((), ())), preferred_element_type=F32)
        if nk == 1:
            epi(part, ex, out_refs, i)
        else:
            acc_ref = refs[first_out + no]
            k = pl.program_id(2)

            @pl.when(k == 0)
            def _():
                acc_ref[...] = part

            @pl.when(k > 0)
            def _():
                acc_ref[...] += part

            @pl.when(k == nk - 1)
            def _():
                epi(acc_ref[...], ex, out_refs, i)

    in_specs = [pl.BlockSpec(*a_spec), pl.BlockSpec(*b_spec)] + [pl.BlockSpec(bs, im) for _, bs, im in extras]
    in_specs += [ANY] * len(afters)
    out_specs = [pl.BlockSpec(bs, im) for _, _, bs, im in outs]
    out_shape = [SDS(s, dt) for s, dt, _, _ in outs]
    vm = 2 * (_nbytes(a_spec[0], a.dtype) + _nbytes(b_spec[0], b.dtype))
    vm += 2 * sum(_nbytes(bs, arr.dtype) for arr, bs, _ in extras)
    vm += 2 * sum(_nbytes(bs, dt) for _, dt, bs, _ in outs)
    vm += 6 * _nbytes(acc_shape, F32)
    scratch = [pltpu.VMEM(acc_shape, F32)] if nk > 1 else []
    return pl.pallas_call(
        body, name=name, grid=grid, in_specs=in_specs, out_specs=out_specs, out_shape=out_shape,
        scratch_shapes=scratch,
        input_output_aliases={} if into is None else {first_out - 1: into[1]},
        compiler_params=_cparams(vm, ("arbitrary", "arbitrary", "arbitrary")),
    )(a, b, *[arr for arr, _, _ in extras], *afters)


NN =((1,), (0,))
NT = ((1,), (1,))
TN = ((0,), (0,))


def _row_acc(out_ref, val, i):
    @pl.when(i == 0)
    def _():
        out_ref[...] = val

    @pl.when(i > 0)
    def _():
        out_ref[...] += val


def _rms_bwd(xn, r, dxn):
    return r * (dxn - xn * jnp.mean(dxn * xn, axis=-1, keepdims=True))


def norm1(x, g):
    tm = 256

    def body(x_ref, g_ref, h_ref):
        xv = x_ref[...]
        r = lax.rsqrt(jnp.mean(xv * xv, axis=-1, keepdims=True) + EPS)
        h_ref[...] = (xv * r * g_ref[...]).astype(BF16)

    return pl.pallas_call(
        body, name="norm1", grid=(T // tm,),
        in_specs=[pl.BlockSpec((tm, D), lambda i: (i, 0)), pl.BlockSpec((1, D), lambda i: (0, 0))],
        out_specs=pl.BlockSpec((tm, D), lambda i: (i, 0)), out_shape=SDS((T, D), BF16),
        compiler_params=_cparams(32 * 1024 * 1024, ("arbitrary",)),
    )(x, g)


def mm_in(h1, wcat):
    tm, tn = 1024, 1280

    def epi(acc, ex, outs, i):
        outs[0][...] = acc.astype(BF16)

    return matmul("mm_in", h1, wcat, a_spec=((tm, D), lambda j, i, k: (i, 0)), b_spec=((D, tn), lambda j, i, k: (0, j)),
                  cdims=NN, grid=(NCAT // tn, T // tm, 1), acc_shape=(tm, tn),
                  outs=[((T, NCAT), BF16, (tm, tn), lambda j, i, k: (i, j))], epi=epi)[0]


def _window_sum(x, w, up):
    n = x.shape[0]
    row = lax.broadcasted_iota(jnp.int32, x.shape, 0)
    s, sh = x, 1
    while sh < w:
        if up:
            s = s + jnp.where(row < n - sh, pltpu.roll(s, n - sh, axis=0), 0.0)
        else:
            s = s + jnp.where(row >= sh, pltpu.roll(s, sh, axis=0), 0.0)
        sh *= 2
    return s


def _inv_count(shape, w):
    row = lax.broadcasted_iota(jnp.int32, shape, 0)
    return 1.0 / jnp.minimum(row + 1, w).astype(F32)


def pool_fwd(pcat, pw):
    def body(u_ref, pw_ref, d_ref, y_ref):
        for gi, w in enumerate(POOL_WINDOWS):
            ug = u_ref[:, gi * PG:(gi + 1) * PG].astype(F32)
            dg = _window_sum(ug, w, False) * _inv_count(ug.shape, w) - ug
            db = dg.astype(BF16)
            d_ref[:, gi * PG:(gi + 1) * PG] = db
            y_ref[:, gi * PO:(gi + 1) * PO] = jnp.dot(db, pw_ref[gi], preferred_element_type=F32).astype(BF16)

    return pl.pallas_call(
        body, name="pool_fwd", grid=(1,),
        in_specs=[pl.BlockSpec((T, 4 * PG), lambda i: (0, OU // (4 * PG))), pl.BlockSpec((4, PG, PO), lambda i: (0, 0, 0))],
        out_specs=[pl.BlockSpec((T, 4 * PG), lambda i: (0, 0)), pl.BlockSpec((T, D), lambda i: (0, 0))],
        out_shape=[SDS((T, 4 * PG), BF16), SDS((T, D), BF16)],
        compiler_params=_cparams(48 * 1024 * 1024, ("arbitrary",)),
    )(pcat, pw)


def pool_bwd(dylin, d, pw, dproj):
    assert OU % (4 * PG) == 0

    def body(dy_ref, d_ref, pw_ref, held_ref, du_ref, dpw_ref):
        for gi, w in enumerate(POOL_WINDOWS):
            dyl = dy_ref[:, gi * PO:(gi + 1) * PO]
            dd = lax.dot_general(dyl, pw_ref[gi], (NT, ((), ())), preferred_element_type=F32)
            du = _window_sum(dd * _inv_count(dd.shape, w), w, True) - dd
            du_ref[:, gi * PG:(gi + 1) * PG] = du.astype(BF16)
            dpw_ref[gi] = lax.dot_general(d_ref[:, gi * PG:(gi + 1) * PG], dyl, (TN, ((), ())),
                                          preferred_element_type=F32).astype(BF16)

    return pl.pallas_call(
        body, name="pool_bwd", grid=(1,),
        in_specs=[pl.BlockSpec((T, D), lambda i: (0, 0)), pl.BlockSpec((T, 4 * PG), lambda i: (0, 0)),
                  pl.BlockSpec((4, PG, PO), lambda i: (0, 0, 0)), ANY],
        out_specs=[pl.BlockSpec((T, 4 * PG), lambda i: (0, OU // (4 * PG))), pl.BlockSpec((4, PG, PO), lambda i: (0, 0, 0))],
        out_shape=[SDS((T, NCAT), BF16), SDS((4, PG, PO), BF16)],
        input_output_aliases={3: 0},
        compiler_params=_cparams(48 * 1024 * 1024, ("arbitrary",)),
    )(dylin, d, pw, dproj)


def _gate_decay(alow, wa, ba):
    a = jnp.dot(alow, wa, preferred_element_type=F32) + ba
    ls = jax.nn.log_sigmoid(a) * (1.0 / 16.0)
    r = lax.broadcasted_iota(jnp.int32, (CHUNK, CHUNK), 0)
    c = lax.broadcasted_iota(jnp.int32, (CHUNK, CHUNK), 1)
    tri = jnp.where(c <= r, 1.0, 0.0).astype(F32)
    cum = jnp.dot(tri, ls, preferred_element_type=F32, precision=lax.Precision.HIGHEST)
    last = cum[CHUNK - 1:CHUNK, :]
    return a, jnp.exp(last - cum), jnp.exp(last)


def gla_fwd(pcat, wa, ba, ng, after=None):
    afters = _as_list(after)

    def body(q_ref, k_ref, v_ref, g_ref, al_ref, wa_ref, ba_ref, ng_ref, *rest):
        og_ref, o_ref, st_ref, s_scr = rest[len(afters):]

        @pl.when(pl.program_id(0) == 0)
        def _():
            s_scr[...] = jnp.zeros_like(s_scr)

        state = [s_scr[h] for h in range(HEADS)]
        for s in range(GLA_STEP):
            rs = slice(s * CHUNK, (s + 1) * CHUNK)
            _, e, decay = _gate_decay(al_ref[rs, :], wa_ref[...], ba_ref[...])
            kd = (k_ref[rs, :].astype(F32) * e).astype(BF16)
            qs = (q_ref[rs, :].astype(F32) * (DK ** -0.5)).astype(BF16)
            for h in range(HEADS):
                ck = slice(h * DK, (h + 1) * DK)
                cv = slice(h * DV, (h + 1) * DV)
                state[h] = state[h] * decay[:, ck] + lax.dot_general(v_ref[rs, cv], kd[:, ck], (TN, ((), ())),
                                                                     preferred_element_type=F32)
                sb = state[h].astype(BF16)
                st_ref[s, h] = sb
                oh = lax.dot_general(qs[:, ck], sb, (NT, ((), ())), preferred_element_type=F32)
                o_ref[rs, cv] = oh.astype(BF16)
                on = oh * lax.rsqrt(jnp.mean(oh * oh, axis=-1, keepdims=True) + EPS) * ng_ref[:, cv]
                gv = g_ref[rs, cv].astype(F32)
                og_ref[rs, cv] = (on * (gv * _sigmoid(gv))).astype(BF16)
        for h in range(HEADS):
            s_scr[h] = state[h]

    row = lambda c: (c, 0)
    rows = GLA_STEP * CHUNK
    return pl.pallas_call(
        body, name="gla_fwd", grid=(NCHUNK // GLA_STEP,),
        in_specs=[pl.BlockSpec((rows, QK), lambda c: (c, OQ // QK)), pl.BlockSpec((rows, QK), lambda c: (c, OKK // QK)),
                  pl.BlockSpec((rows, D), lambda c: (c, OV // D)), pl.BlockSpec((rows, D), lambda c: (c, OG // D)),
                  pl.BlockSpec((rows, APAD), lambda c: (c, OA // APAD)),
                  pl.BlockSpec((APAD, QK), lambda c: (0, 0)), pl.BlockSpec((1, QK), lambda c: (0, 0)),
                  pl.BlockSpec((1, D), lambda c: (0, 0))] + [ANY] * len(afters),
        out_specs=[pl.BlockSpec((rows, D), row), pl.BlockSpec((rows, D), row),
                   pl.BlockSpec((GLA_STEP, HEADS, DV, DK), lambda c: (c, 0, 0, 0))],
        out_shape=[SDS((T, D), BF16), SDS((T, D), BF16), SDS((NCHUNK, HEADS, DV, DK), BF16)],
        scratch_shapes=[pltpu.VMEM((HEADS, DV, DK), F32)],
        compiler_params=_cparams(32 * 1024 * 1024, ("arbitrary",)),
    )(pcat, pcat, pcat, pcat, pcat, wa, ba, ng, *afters)


def gla_bwd(do, pcat, states, wa, ba, dproj, after):
    tail = NCAT - OQ
    assert (OKK, OA) == (OQ + QK, OQ + 2 * QK) and OQ % tail == 0

    def body(do_ref, q_ref, k_ref, v_ref, al_ref, sc_ref, sp_ref, wa_ref, ba_ref, after_ref, held_ref,
             dp_ref, dv_ref, dwa_ref, dba_ref, ds_scr):
        i = pl.program_id(0)
        dp_ref[:, 2 * QK + APAD:] = jnp.zeros((GLA_STEP * CHUNK, tail - 2 * QK - APAD), BF16)

        @pl.when(i == 0)
        def _():
            ds_scr[...] = jnp.zeros_like(ds_scr)

        ds = [ds_scr[h] for h in range(HEADS)]
        dwa, dba = 0.0, 0.0
        for u in reversed(range(GLA_STEP)):
            rs = slice(u * CHUNK, (u + 1) * CHUNK)
            first_chunk = jnp.logical_and(i == NCHUNK // GLA_STEP - 1, u == 0)
            has_prev = jnp.where(first_chunk, 0.0, 1.0).astype(F32)
            a, e, decay = _gate_decay(al_ref[rs, :], wa_ref[...], ba_ref[...])
            kdf = k_ref[rs, :].astype(F32) * e
            kd = kdf.astype(BF16)
            qs = (q_ref[rs, :].astype(F32) * (DK ** -0.5)).astype(BF16)
            dkd_parts, ddecay_parts = [], []
            for h in range(HEADS):
                ck = slice(h * DK, (h + 1) * DK)
                cv = slice(h * DV, (h + 1) * DV)
                doh = do_ref[rs, cv]
                dsh = ds[h] + lax.dot_general(doh, qs[:, ck], (TN, ((), ())), preferred_element_type=F32)
                dsb = dsh.astype(BF16)
                dp_ref[rs, ck] = (jnp.dot(doh, sc_ref[u, h], preferred_element_type=F32) * (DK ** -0.5)).astype(BF16)
                dkd_parts.append(jnp.dot(v_ref[rs, cv], dsb, preferred_element_type=F32))
                dv_ref[rs, cv] = lax.dot_general(kd[:, ck], dsb, (NT, ((), ())), preferred_element_type=F32).astype(BF16)
                s_prev = (sp_ref[h] if u == 0 else sc_ref[u - 1, h]).astype(F32)
                ddecay_parts.append(jnp.sum(dsh * s_prev, axis=0, keepdims=True) * has_prev)
                ds[h] = dsh * decay[:, ck]
            dkd = jnp.concatenate(dkd_parts, axis=1)
            ddecay = jnp.concatenate(ddecay_parts, axis=1)
            dp_ref[rs, QK:2 * QK] = (dkd * e).astype(BF16)
            dearg = dkd * kdf
            dlast = jnp.sum(dearg, axis=0, keepdims=True) + ddecay * decay
            r = lax.broadcasted_iota(jnp.int32, (CHUNK, CHUNK), 0)
            c = lax.broadcasted_iota(jnp.int32, (CHUNK, CHUNK), 1)
            triu = jnp.where(c >= r, 1.0, 0.0).astype(F32)
            dls = dlast - jnp.dot(triu, dearg, preferred_element_type=F32, precision=lax.Precision.HIGHEST)
            da = dls * (1.0 / 16.0) * (1.0 - _sigmoid(a))
            dab = da.astype(BF16)
            dp_ref[rs, 2 * QK:2 * QK + APAD] = lax.dot_general(dab, wa_ref[...], (NT, ((), ())),
                                                               preferred_element_type=F32).astype(BF16)
            dwa = dwa + lax.dot_general(al_ref[rs, :], dab, (TN, ((), ())), preferred_element_type=F32)
            dba = dba + jnp.sum(da, axis=0, keepdims=True)
        for h in range(HEADS):
            ds_scr[h] = ds[h]

        @pl.when(i == 0)
        def _():
            dwa_ref[...] = dwa
            dba_ref[...] = dba

        @pl.when(i > 0)
        def _():
            dwa_ref[...] += dwa
            dba_ref[...] += dba

    rows = GLA_STEP * CHUNK
    rev = lambda i: NCHUNK // GLA_STEP - 1 - i
    return pl.pallas_call(
        body, name="gla_bwd", grid=(NCHUNK // GLA_STEP,),
        in_specs=[pl.BlockSpec((rows, D), lambda i: (rev(i), 0)),
                  pl.BlockSpec((rows, QK), lambda i: (rev(i), OQ // QK)), pl.BlockSpec((rows, QK), lambda i: (rev(i), OKK // QK)),
                  pl.BlockSpec((rows, D), lambda i: (rev(i), OV // D)), pl.BlockSpec((rows, APAD), lambda i: (rev(i), OA // APAD)),
                  pl.BlockSpec((GLA_STEP, HEADS, DV, DK), lambda i: (rev(i), 0, 0, 0)),
                  pl.BlockSpec((None, HEADS, DV, DK), lambda i: (jnp.maximum(rev(i) * GLA_STEP - 1, 0), 0, 0, 0)),
                  pl.BlockSpec((APAD, QK), lambda i: (0, 0)), pl.BlockSpec((1, QK), lambda i: (0, 0)), ANY, ANY],
        out_specs=[pl.BlockSpec((rows, tail), lambda i: (rev(i), OQ // tail)), pl.BlockSpec((rows, D), lambda i: (rev(i), 0)),
                   pl.BlockSpec((APAD, QK), lambda i: (0, 0)), pl.BlockSpec((1, QK), lambda i: (0, 0))],
        out_shape=[SDS((T, NCAT), BF16), SDS((T, D), BF16), SDS((APAD, QK), F32), SDS((1, QK), F32)],
        scratch_shapes=[pltpu.VMEM((HEADS, DV, DK), F32)],
        input_output_aliases={10: 0},
        compiler_params=_cparams(32 * 1024 * 1024, ("arbitrary",)),
    )(do, pcat, pcat, pcat, pcat, states, states, wa, ba, after, dproj)


TMF = 256
TMW = 512
_rowblk = ((TMF, D), lambda j, i, k: (i, 0))
_vec = ((1, D), lambda j, i, k: (0, 0))


def _full_spec(col):
    return ((TMF, D), lambda j, i, k: (i, col))


TBIG = 1024


def square_matmul(name, a, b, *, a_spec, b_spec, cdims, nk, after=None):
    def epi(acc, ex, outs, i):
        outs[0][...] = acc

    return matmul(name, a, b, a_spec=a_spec, b_spec=b_spec, cdims=cdims, grid=(D // TBIG, T // TBIG, nk),
                  acc_shape=(TBIG, TBIG), outs=[((T, D), F32, (TBIG, TBIG), lambda j, i, k: (i, j))], epi=epi,
                  after=after)[0]


def rowwise(name, y, *, extras, outs, epi):
    ne = len(extras)

    def body(*refs):
        epi(refs[0][...], refs[1:1 + ne], refs[1 + ne:], pl.program_id(1))

    in_specs = [pl.BlockSpec(*_rowblk)] + [pl.BlockSpec(bs, im) for _, bs, im in extras]
    return pl.pallas_call(
        body, name=name, grid=(1, T // TMF, 1), in_specs=in_specs,
        out_specs=[pl.BlockSpec(bs, im) for _, _, bs, im in outs], out_shape=[SDS(s, dt) for s, dt, _, _ in outs],
        compiler_params=_cparams(40 * 1024 * 1024, ("arbitrary", "arbitrary", "arbitrary")),
    )(y, *[arr for arr, _, _ in extras])


def mm_gla_out(og, w, ylin, pcat, pscale):
    def epi(acc, ex, outs, i):
        ylin_ref, lgp_ref, lgg_ref, ps_ref = ex
        for c0 in range(0, D, EPI_COLS):
            cs = slice(c0, c0 + EPI_COLS)
            gp = _sigmoid(lgp_ref[:, cs].astype(F32))
            gg = _sigmoid(lgg_ref[:, cs].astype(F32))
            a = acc[:, cs]
            outs[0][:, cs] = (gp * (ylin_ref[:, cs].astype(F32) * ps_ref[:, cs]) + gg * a).astype(BF16)
            outs[1][:, cs] = a.astype(BF16)

    return matmul("mm_gla_out", og, w, a_spec=_rowblk, b_spec=((D, D), lambda j, i, k: (0, 0)), cdims=NN,
                  grid=(1, T // TMF, 1), acc_shape=(TMF, D),
                  extras=[(ylin, *_rowblk), (pcat, *_full_spec(OGP // D)), (pcat, *_full_spec(OGG // D)), (pscale, *_vec)],
                  outs=[((T, D), BF16, *_rowblk), ((T, D), BF16, *_rowblk)], epi=epi)


def mm_out(mixed, w, x, g2):
    def epi(acc, ex, outs, i):
        x_ref, g_ref = ex
        x2 = x_ref[...] + acc
        r = lax.rsqrt(jnp.mean(x2 * x2, axis=-1, keepdims=True) + EPS)
        outs[0][...] = x2
        outs[1][...] = (x2 * r * g_ref[...]).astype(BF16)

    return matmul("mm_out", mixed, w, a_spec=_rowblk, b_spec=((D, D), lambda j, i, k: (0, 0)), cdims=NN,
                  grid=(1, T // TMF, 1), acc_shape=(TMF, D), extras=[(x, *_rowblk), (g2, *_vec)],
                  outs=[((T, D), F32, *_rowblk), ((T, D), BF16, *_rowblk)], epi=epi)


def mm_up(h2, wup):
    def epi(acc, ex, outs, i):
        r = jnp.maximum(acc, 0.0)
        outs[0][...] = r.astype(BF16)
        outs[1][...] = (r * r).astype(BF16)

    blk = ((TMW, D), lambda j, i, k: (i, j))
    return matmul("mm_up", h2, wup, a_spec=((TMW, D), lambda j, i, k: (i, 0)), b_spec=((None, D, D), lambda j, i, k: (j, 0, 0)),
                  cdims=NN, grid=(NCHIP, T // TMW, 1), acc_shape=(TMW, D),
                  outs=[((T, DFF), BF16, *blk), ((T, DFF), BF16, *blk)], epi=epi)


def mm_down(act, wdown, x2, tgt, gf):
    tk = 4096

    def epi(acc, ex, outs, i):
        x2_ref, t_ref, g_ref = ex
        dx_ref, dxb_ref, gnf_ref, loss_ref = outs
        x3 = x2_ref[...] + acc
        r = lax.rsqrt(jnp.mean(x3 * x3, axis=-1, keepdims=True) + EPS)
        xn = x3 * r
        err = xn * g_ref[...] - t_ref[...]
        lsum = 0.5 * jnp.sum(jnp.mean(err * err, axis=-1, keepdims=True), axis=0, keepdims=True)
        dy = err * (1.0 / D)
        _row_acc(gnf_ref, jnp.sum(dy * xn, axis=0, keepdims=True), i)
        _row_acc(loss_ref, jnp.broadcast_to(lsum, (1, 128)), i)
        dx3 = _rms_bwd(xn, r, dy * g_ref[...])
        dx_ref[...] = dx3
        dxb_ref[...] = dx3.astype(BF16)

    y = square_matmul("mm_down", act, wdown, a_spec=((TBIG, tk), lambda j, i, k: (i, k)),
                      b_spec=((tk, TBIG), lambda j, i, k: (k, j)), cdims=NN, nk=DFF // tk)
    return rowwise("rows_final", y, extras=[(x2, *_rowblk), (tgt, *_rowblk), (gf, *_vec)],
                   outs=[((T, D), F32, *_rowblk), ((T, D), BF16, *_rowblk), ((1, D), F32, *_vec),
                         ((1, 128), F32, (1, 128), lambda j, i, k: (0, 0))], epi=epi)


def mm_dact(dx3b, wdown, rup, after=None):
    def epi(acc, ex, outs, i):
        outs[0][...] = (acc * 2.0 * ex[0][...].astype(F32)).astype(BF16)

    blk = ((TMW, D), lambda j, i, k: (i, j))
    return matmul("mm_dact", dx3b, wdown, a_spec=((TMW, D), lambda j, i, k: (i, 0)), b_spec=((D, D), lambda j, i, k: (j, 0)),
                  cdims=NT, grid=(DFF // D, T // TMW, 1), acc_shape=(TMW, D), extras=[(rup, *blk)],
                  outs=[((T, DFF), BF16, *blk)], epi=epi, after=after)[0]


def mm_wgrad(name, a, b, m, n, out_shape, out_block, out_map, tm, tn, after=None):
    def epi(acc, ex, outs, i):
        outs[0][...] = acc.astype(BF16).reshape(outs[0].shape)

    return matmul(name, a, b, a_spec=((T, tm), lambda j, i, k: (0, i)), b_spec=((T, tn), lambda j, i, k: (0, j)),
                  cdims=TN, grid=(n // tn, m // tm, 1), acc_shape=(tm, tn),
                  outs=[(out_shape, BF16, out_block, out_map)], epi=epi, after=after)[0]


def mm_dh2(dup, wup, x2, dx3, g2, after=None):
    def epi(acc, ex, outs, i):
        x2_ref, dx3_ref, g_ref = ex
        x2 = x2_ref[...]
        r = lax.rsqrt(jnp.mean(x2 * x2, axis=-1, keepdims=True) + EPS)
        xn = x2 * r
        _row_acc(outs[2], jnp.sum(acc * xn, axis=0, keepdims=True), i)
        dx2 = dx3_ref[...] + _rms_bwd(xn, r, acc * g_ref[...])
        outs[0][...] = dx2
        outs[1][...] = dx2.astype(BF16)

    y = square_matmul("mm_dh2", dup, wup, a_spec=((TBIG, D), lambda j, i, k: (i, k)),
                      b_spec=((None, TBIG, D), lambda j, i, k: (k, j, 0)), cdims=NT, nk=NCHIP, after=after)
    return rowwise("rows_dh2", y, extras=[(x2, *_rowblk), (dx3, *_rowblk), (g2, *_vec)],
                   outs=[((T, D), F32, *_rowblk), ((T, D), BF16, *_rowblk), ((1, D), F32, *_vec)], epi=epi)


def mm_dmixed(dx2b, wout, pcat, ylin, ygla, pscale, after=None):
    assert OGG == OGP + D and OGP % (2 * D) == 0

    def epi(acc, ex, outs, i):
        lgp_ref, lgg_ref, ylin_ref, ygla_ref, ps_ref = ex
        dps = []
        for c0 in range(0, D, EPI_COLS):
            cs = slice(c0, c0 + EPI_COLS)
            gp = _sigmoid(lgp_ref[:, cs].astype(F32))
            gg = _sigmoid(lgg_ref[:, cs].astype(F32))
            yl = ylin_ref[:, cs].astype(F32)
            ps = ps_ref[:, cs]
            a = acc[:, cs]
            agp = a * gp
            outs[0][:, cs] = (agp * ps).astype(BF16)
            outs[1][:, cs] = (a * gg).astype(BF16)
            outs[2][:, cs] = (agp * (yl * ps) * (1.0 - gp)).astype(BF16)
            outs[2][:, D + c0:D + c0 + EPI_COLS] = (a * ygla_ref[:, cs].astype(F32) * gg * (1.0 - gg)).astype(BF16)
            dps.append(jnp.sum(agp * yl, axis=0, keepdims=True))
        _row_acc(outs[3], jnp.concatenate(dps, axis=1), i)

    return matmul("mm_dmixed", dx2b, wout, a_spec=_rowblk, b_spec=((D, D), lambda j, i, k: (0, 0)), cdims=NT,
                  grid=(1, T // TMF, 1), acc_shape=(TMF, D),
                  extras=[(pcat, *_full_spec(OGP // D)), (pcat, *_full_spec(OGG // D)), (ylin, *_rowblk), (ygla, *_rowblk),
                          (pscale, *_vec)],
                  outs=[((T, D), BF16, *_rowblk)] * 2
                       + [((T, NCAT), BF16, (TMF, 2 * D), lambda j, i, k: (i, OGP // (2 * D))), ((1, D), F32, *_vec)],
                  epi=epi, after=after)


def mm_dog(dygla, wgo, o, pcat, ng, dproj, after=None):
    def epi(acc, ex, outs, i):
        o_ref, g_ref, ng_ref = ex
        do_ref, dg_ref, gng_ref = outs
        gparts = []
        for h in range(HEADS):
            cv = slice(h * DV, (h + 1) * DV)
            oh = o_ref[:, cv].astype(F32)
            r = lax.rsqrt(jnp.mean(oh * oh, axis=-1, keepdims=True) + EPS)
            on = oh * r
            gv = g_ref[:, cv].astype(F32)
            sg = _sigmoid(gv)
            a = acc[:, cv]
            dgain = a * (gv * sg)
            gparts.append(jnp.sum(dgain * on, axis=0, keepdims=True))
            ngh = ng_ref[:, cv]
            do_ref[:, cv] = _rms_bwd(on, r, dgain * ngh).astype(BF16)
            dg_ref[:, cv] = (a * (on * ngh) * (sg * (1.0 + gv * (1.0 - sg)))).astype(BF16)
        _row_acc(gng_ref, jnp.concatenate(gparts, axis=1), i)

    return matmul("mm_dog", dygla, wgo, a_spec=_rowblk, b_spec=((D, D), lambda j, i, k: (0, 0)), cdims=NT,
                  grid=(1, T // TMF, 1), acc_shape=(TMF, D),
                  extras=[(o, *_rowblk), (pcat, *_full_spec(OG // D)), (ng, *_vec)],
                  outs=[((T, D), BF16, *_rowblk), ((T, NCAT), BF16, *_full_spec(OG // D)), ((1, D), F32, *_vec)],
                  epi=epi, after=after, into=(dproj, 1))


def mm_dh1(dpcat, wcat, x, dx2, g1, after=None):
    tk = 3840

    def epi(acc, ex, outs, i):
        x_ref, dx2_ref, g_ref = ex
        xv = x_ref[...]
        r = lax.rsqrt(jnp.mean(xv * xv, axis=-1, keepdims=True) + EPS)
        xn = xv * r
        _row_acc(outs[1], jnp.sum(acc * xn, axis=0, keepdims=True), i)
        outs[0][...] = dx2_ref[...] + _rms_bwd(xn, r, acc * g_ref[...])

    y = square_matmul("mm_dh1", dpcat, wcat, a_spec=((TBIG, tk), lambda j, i, k: (i, k)),
                      b_spec=((TBIG, tk), lambda j, i, k: (j, k)), cdims=NT, nk=NCAT // tk, after=after)
    return rowwise("rows_dh1", y, extras=[(x, *_rowblk), (dx2, *_rowblk), (g1, *_vec)],
                   outs=[((T, D), F32, *_rowblk), ((1, D), F32, *_vec)], epi=epi)


def _tile_rows(rows, cols, n_arrays):
    tm = rows
    while tm % 32 == 0 and 2 * n_arrays * tm * cols * 4 > 36 * 1024 * 1024:
        tm //= 2
    return tm


def add_pairs(name, parts, theirs, core):
    _, _, r, c = parts.shape
    tm = _tile_rows(r, c, 3)

    def body(core_ref, a_ref, b_ref, o_ref):
        o_ref[...] = (a_ref[...].astype(F32) + b_ref[...].astype(F32)).astype(BF16)

    spec = pl.BlockSpec((None, tm, c), lambda j, i, core_ref: (j, i, 0))
    grid_spec = pltpu.PrefetchScalarGridSpec(
        num_scalar_prefetch=1, grid=(NCHIP, r // tm),
        in_specs=[pl.BlockSpec((None, None, tm, c), lambda j, i, core_ref: (core_ref[0], j, i, 0)), spec], out_specs=spec)
    return pl.pallas_call(body, name=name, grid_spec=grid_spec, out_shape=SDS((NCHIP, r, c), BF16),
                          compiler_params=_cparams(40 * 1024 * 1024, ("arbitrary", "arbitrary")))(core, parts, theirs)


def sum_chips(name, sums, landed, chip):
    _, r, c = sums.shape
    tm = _tile_rows(r, c, 4)

    def body(chip_ref, own_ref, l_ref, o_ref):
        s = own_ref[...].astype(F32)
        for t in range(NCHIP - 1):
            s = s + l_ref[t].astype(F32)
        o_ref[...] = s

    grid_spec = pltpu.PrefetchScalarGridSpec(
        num_scalar_prefetch=1, grid=(r // tm,),
        in_specs=[pl.BlockSpec((None, tm, c), lambda i, chip_ref: (chip_ref[0], i, 0)),
                  pl.BlockSpec((NCHIP - 1, tm, c), lambda i, chip_ref: (0, i, 0))],
        out_specs=pl.BlockSpec((tm, c), lambda i, chip_ref: (i, 0)))
    return pl.pallas_call(body, name=name, grid_spec=grid_spec, out_shape=SDS((r, c), F32),
                          compiler_params=_cparams(40 * 1024 * 1024, ("arbitrary",)))(chip, sums, landed)


def _adamw_math(wv, gv, mv, vv):
    mn = ADAM_B1 * mv + (1.0 - ADAM_B1) * gv
    vn = ADAM_B2 * vv + (1.0 - ADAM_B2) * (gv * gv)
    mh = mn / (1.0 - ADAM_B1 ** ADAM_STEP)
    vh = vn / (1.0 - ADAM_B2 ** ADAM_STEP)
    return -ADAM_LR * (mh / (jnp.sqrt(vh) + ADAM_EPS) + ADAM_WD * wv), mn, vn


def adamw(name, w, g, m, v):
    def body(w_ref, g_ref, m_ref, v_ref, go_ref, d_ref, mo_ref, vo_ref):
        gv = g_ref[...]
        go_ref[...] = gv
        d_ref[...], mo_ref[...], vo_ref[...] = _adamw_math(w_ref[...], gv, m_ref[...], v_ref[...])

    return pl.pallas_call(body, name=name, out_shape=[SDS(w.shape, F32)] * 4)(w, g, m, v)


def adamw_halves(name, w, g_own, g_sib, m, v, core):
    _, r, c = w.shape
    tm = _tile_rows(r, c, 10)

    def body(core_ref, w_ref, go_ref, gs_ref, m_ref, v_ref, g_out, d_out, m_out, v_out):
        gv = jnp.where(pl.program_id(0) == core_ref[0], go_ref[...], gs_ref[...])
        g_out[...] = gv
        d_out[...], m_out[...], v_out[...] = _adamw_math(w_ref[...], gv, m_ref[...], v_ref[...])

    full = pl.BlockSpec((None, tm, c), lambda h, i, core_ref: (h, i, 0))
    own = pl.BlockSpec((tm, c), lambda h, i, core_ref: (jnp.where(h == core_ref[0], i, 0), 0))
    sib = pl.BlockSpec((tm, c), lambda h, i, core_ref: (jnp.where(h == core_ref[0], 0, i), 0))
    grid_spec = pltpu.PrefetchScalarGridSpec(num_scalar_prefetch=1, grid=(2, r // tm),
                                             in_specs=[full, own, sib, full, full], out_specs=[full] * 4)
    return pl.pallas_call(body, name=name, grid_spec=grid_spec, out_shape=[SDS(w.shape, F32)] * 4,
                          compiler_params=_cparams(48 * 1024 * 1024, ("arbitrary", "arbitrary")))(core, w, g_own, g_sib, m, v)


def cast_bf16(name, w):
    _, r, c = w.shape
    tm = _tile_rows(r, c, 2)

    def body(w_ref, o_ref):
        o_ref[...] = w_ref[...].astype(BF16)

    spec = pl.BlockSpec((None, tm, c), lambda h, i: (h, i, 0))
    return pl.pallas_call(body, name=name, grid=(2, r // tm), in_specs=[spec], out_specs=spec, out_shape=SDS(w.shape, BF16),
                          compiler_params=_cparams(40 * 1024 * 1024, ("arbitrary", "arbitrary")))(w)


def pack_rows(name, parts, rows, after=None):
    width = parts[0].shape[1]
    n = len(parts)
    afters = _as_list(after)

    def body(*refs):
        out_ref = refs[n + len(afters)]
        out_ref[...] = jnp.zeros_like(out_ref)
        off = 0
        for p in refs[:n]:
            out_ref[off:off + p.shape[0], :] = p[...]
            off += p.shape[0]

    vm = pl.BlockSpec(memory_space=pltpu.VMEM)
    return pl.pallas_call(body, name=name, in_specs=[vm] * n + [ANY] * len(afters), out_specs=vm,
                          out_shape=SDS((rows, width), F32))(*parts, *afters)


def _place():
    x, y, c = lax.axis_index("x"), lax.axis_index("y"), lax.axis_index("c")
    chips = [(1 - x, y), (x, 1 - y), (1 - x, 1 - y)]
    return x, y, c, chips


def _row_split(shape, dtype):
    r, c = shape
    n = 1
    while r % (2 * n) == 0 and (r // (2 * n)) % 16 == 0 and (r // n) * c * jnp.dtype(dtype).itemsize > PIECE_BYTES:
        n *= 2
    return [pl.ds(s * (r // n), r // n) for s in range(n)]


def _pieces(ref):
    *lead, r, c = ref.shape
    split = _row_split((r, c), ref.dtype)
    return [ref.at[(*idx, s)] for idx in itertools.product(*[range(d) for d in lead]) for s in split]


HBM = pl.BlockSpec(memory_space=pltpu.HBM)
SEM = pl.BlockSpec(memory_space=pltpu.SEMAPHORE)
EFFECT = pltpu.SideEffectType.DATAFLOW_SIDE_EFFECTING


def gather_start(name, shards, after=None):
    n = len(shards)
    afters = _as_list(after)

    def body(*refs):
        src, land = refs[:n], refs[n:2 * n]
        send, recv = refs[2 * n + len(afters)], refs[2 * n + len(afters) + 1]
        x, y, c, chips = _place()
        me = 2 * x + y
        for a in range(n):
            for j, (cx, cy) in enumerate(chips[:2]):
                for sp, dp in zip(_pieces(src[a].at[c]), _pieces(land[a].at[me, c])):
                    pltpu.make_async_remote_copy(sp, dp, send.at[2 * a + j], recv.at[2 * a + j],
                                                 device_id=(cx, cy, c), device_id_type=MESH).start()

    lands = [pltpu.with_memory_space_constraint(lax.empty((NCHIP,) + s.shape, s.dtype), pltpu.HBM) for s in shards]
    srcs = [pltpu.with_memory_space_constraint(s, pltpu.HBM) for s in shards]
    outs = pl.pallas_call(
        body, name=name,
        out_shape=(pltpu.SemaphoreType.DMA((2 * n,)), pltpu.SemaphoreType.DMA((2 * n,)),
                   *[pltpu.HBM(s.shape, s.dtype) for s in shards], *[pltpu.HBM(l.shape, l.dtype) for l in lands]),
        in_specs=[HBM] * (2 * n) + [ANY] * len(afters), out_specs=(SEM, SEM, *([HBM] * (2 * n))),
        input_output_aliases={i: 2 + i for i in range(2 * n)},
        compiler_params=pltpu.CompilerParams(has_side_effects=EFFECT),
    )(*srcs, *lands, *afters)
    return outs[0], outs[1], list(outs[2:2 + n]), list(outs[2 + n:2 + 2 * n])


def _relay_blocks(land, c, chips):
    (xx, xy), (yx, yy), (dx, dy) = chips
    rows = land.shape[2] // 2
    upper, lower = pl.ds(0, rows), pl.ds(rows, rows)
    return [(land.at[2 * yx + yy, c, lower], land.at[2 * dx + dy, c, lower]),
            (land.at[2 * xx + xy, c, upper], land.at[2 * dx + dy, c, upper])]


def relay_turn(name, send, recv, shards, lands, after):
    n = len(shards)
    afters = _as_list(after)

    def body(*refs):
        src, had = refs[:n], refs[n:2 * n]
        send_ref, recv_ref = refs[2 * n], refs[2 * n + 1]
        rsend, rrecv = refs[2 * n + 2 + len(afters)], refs[2 * n + 3 + len(afters)]
        land = refs[3 * n + 4 + len(afters):4 * n + 4 + len(afters)]
        x, y, c, chips = _place()
        for a in range(n):
            for j, (cx, cy) in enumerate(chips[:2]):
                cp = pltpu.make_async_remote_copy(src[a].at[c], had[a].at[2 * cx + cy, c], send_ref.at[2 * a + j],
                                                  recv_ref.at[2 * a + j], device_id=(cx, cy, c), device_id_type=MESH)
                cp.wait_send()
                cp.wait_recv()
        for a in range(n):
            for j, ((sent, _), (dst, _)) in enumerate(zip(_relay_blocks(had[a], c, chips), _relay_blocks(land[a], c, chips))):
                cx, cy = chips[j]
                for sp, dp in zip(_pieces(sent), _pieces(dst)):
                    pltpu.make_async_remote_copy(sp, dp, rsend.at[2 * a + j], rrecv.at[2 * a + j],
                                                 device_id=(cx, cy, c), device_id_type=MESH).start()

    outs = pl.pallas_call(
        body, name=name,
        out_shape=(pltpu.SemaphoreType.DMA((2 * n,)), pltpu.SemaphoreType.DMA((2 * n,)),
                   *[pltpu.HBM(s.shape, s.dtype) for s in shards], *[pltpu.HBM(l.shape, l.dtype) for l in lands]),
        in_specs=[HBM] * (2 * n) + [SEM, SEM] + [ANY] * len(afters), out_specs=(SEM, SEM, *([HBM] * (2 * n))),
        input_output_aliases={i: 2 + i for i in range(2 * n)},
        compiler_params=pltpu.CompilerParams(has_side_effects=EFFECT),
    )(*shards, *lands, send, recv, *afters)
    return outs[0], outs[1], list(outs[2:2 + n]), list(outs[2 + n:2 + 2 * n])


def relay_wait(name, send, recv, lands, after):
    n = len(lands)
    afters = _as_list(after)

    def body(*refs):
        land = refs[:n]
        send_ref, recv_ref = refs[n], refs[n + 1]
        x, y, c, chips = _place()
        for a in range(n):
            for j, (sent, got) in enumerate(_relay_blocks(land[a], c, chips)):
                cx, cy = chips[j]
                cp = pltpu.make_async_remote_copy(sent, got, send_ref.at[2 * a + j], recv_ref.at[2 * a + j],
                                                  device_id=(cx, cy, c), device_id_type=MESH)
                cp.wait_send()
                cp.wait_recv()

    outs = pl.pallas_call(
        body, name=name, out_shape=tuple(pltpu.HBM(l.shape, l.dtype) for l in lands),
        in_specs=[HBM] * n + [SEM, SEM] + [ANY] * len(afters), out_specs=[HBM] * n,
        input_output_aliases={i: i for i in range(n)},
        compiler_params=pltpu.CompilerParams(has_side_effects=EFFECT),
    )(*lands, send, recv, *afters)
    return list(outs)


def forward_halves(name, shards, lands):
    n = len(lands)

    def body(*refs):
        had, buf = refs[:n], refs[n:2 * n]
        send, recv = refs[2 * n:]
        x, y, c, chips = _place()
        sib = (x, y, 1 - c)
        for a in range(n):
            for j, (cx, cy) in enumerate(chips):
                for sp, dp in zip(_pieces(had[a].at[2 * cx + cy, c]), _pieces(buf[a].at[2 * cx + cy, c])):
                    pltpu.make_async_remote_copy(sp, dp, send.at[3 * a + j], recv.at[3 * a + j], device_id=sib, device_id_type=MESH).start()
        for a in range(n):
            for j, (cx, cy) in enumerate(chips):
                pltpu.make_async_remote_copy(had[a].at[2 * cx + cy, c], buf[a].at[2 * cx + cy, 1 - c], send.at[3 * a + j],
                                             recv.at[3 * a + j], device_id=sib, device_id_type=MESH).wait()

    got = pl.pallas_call(
        body, name=name, in_specs=[ANY] * n, out_specs=[ANY] * n, out_shape=[SDS(l.shape, l.dtype) for l in lands],
        input_output_aliases={i: i for i in range(n)},
        scratch_shapes=[pltpu.SemaphoreType.DMA((3 * n,)), pltpu.SemaphoreType.DMA((3 * n,))],
    )(*lands)
    me = 2 * lax.axis_index("x") + lax.axis_index("y")
    return [lax.dynamic_update_index_in_dim(g, s, me, 0) for g, s in zip(got, shards)]


def forward_turn(name, send, recv, lands, after):
    n = len(lands)
    afters = _as_list(after)

    def body(*refs):
        had = refs[:n]
        send_ref, recv_ref = refs[n], refs[n + 1]
        fsend, frecv = refs[n + 2 + len(afters)], refs[n + 3 + len(afters)]
        buf = refs[n + 4 + len(afters):2 * n + 4 + len(afters)]
        x, y, c, chips = _place()
        sib = (x, y, 1 - c)
        for a in range(n):
            for j, (sent, got) in enumerate(_relay_blocks(had[a], c, chips)):
                cx, cy = chips[j]
                cp = pltpu.make_async_remote_copy(sent, got, send_ref.at[2 * a + j], recv_ref.at[2 * a + j],
                                                  device_id=(cx, cy, c), device_id_type=MESH)
                cp.wait_send()
                cp.wait_recv()
        for a in range(n):
            for j, (cx, cy) in enumerate(chips):
                for sp, dp in zip(_pieces(had[a].at[2 * cx + cy, c]), _pieces(buf[a].at[2 * cx + cy, c])):
                    pltpu.make_async_remote_copy(sp, dp, fsend.at[3 * a + j], frecv.at[3 * a + j], device_id=sib, device_id_type=MESH).start()

    outs = pl.pallas_call(
        body, name=name,
        out_shape=(pltpu.SemaphoreType.DMA((3 * n,)), pltpu.SemaphoreType.DMA((3 * n,)), *[pltpu.HBM(l.shape, l.dtype) for l in lands]),
        in_specs=[HBM] * n + [SEM, SEM] + [ANY] * len(afters), out_specs=(SEM, SEM, *([HBM] * n)),
        input_output_aliases={i: 2 + i for i in range(n)},
        compiler_params=pltpu.CompilerParams(has_side_effects=EFFECT),
    )(*lands, send, recv, *afters)
    return outs[0], outs[1], list(outs[2:])


def forward_wait(name, send, recv, lands, after):
    n = len(lands)
    afters = _as_list(after)

    def body(*refs):
        land = refs[:n]
        send_ref, recv_ref = refs[n], refs[n + 1]
        x, y, c, chips = _place()
        sib = (x, y, 1 - c)
        for a in range(n):
            for j, (cx, cy) in enumerate(chips):
                cp = pltpu.make_async_remote_copy(land[a].at[2 * cx + cy, c], land[a].at[2 * cx + cy, 1 - c], send_ref.at[3 * a + j],
                                                  recv_ref.at[3 * a + j], device_id=sib, device_id_type=MESH)
                cp.wait_send()
                cp.wait_recv()

    outs = pl.pallas_call(
        body, name=name, out_shape=tuple(pltpu.HBM(l.shape, l.dtype) for l in lands),
        in_specs=[HBM] * n + [SEM, SEM] + [ANY] * len(afters), out_specs=[HBM] * n,
        input_output_aliases={i: i for i in range(n)},
        compiler_params=pltpu.CompilerParams(has_side_effects=EFFECT),
    )(*lands, send, recv, *afters)
    return list(outs)


def exchange_start(name, parts):
    n = len(parts)

    def body(*refs):
        src, got = refs[:n], refs[n:2 * n]
        send, recv = refs[2 * n], refs[2 * n + 1]
        token = refs[4 * n + 2]
        x, y, c, _ = _place()
        sib = (x, y, 1 - c)
        for a in range(n):
            for sp, dp in zip(_pieces(src[a].at[1 - c]), _pieces(got[a])):
                pltpu.make_async_remote_copy(sp, dp, send.at[a], recv.at[a], device_id=sib, device_id_type=MESH).start()
        token[...] = jnp.zeros_like(token)

    lands = [pltpu.with_memory_space_constraint(lax.empty(p.shape[1:], p.dtype), pltpu.HBM) for p in parts]
    srcs = [pltpu.with_memory_space_constraint(p, pltpu.HBM) for p in parts]
    outs = pl.pallas_call(
        body, name=name,
        out_shape=(pltpu.SemaphoreType.DMA((n,)), pltpu.SemaphoreType.DMA((n,)),
                   *[pltpu.HBM(p.shape, p.dtype) for p in parts], *[pltpu.HBM(l.shape, l.dtype) for l in lands],
                   SDS((8, 128), F32)),
        in_specs=[HBM] * (2 * n), out_specs=(SEM, SEM, *([HBM] * (2 * n)), pl.BlockSpec(memory_space=pltpu.VMEM)),
        input_output_aliases={i: 2 + i for i in range(2 * n)},
        compiler_params=pltpu.CompilerParams(has_side_effects=EFFECT),
    )(*srcs, *lands)
    return outs[0], outs[1], list(outs[2:2 + n]), list(outs[2 + n:2 + 2 * n]), outs[2 + 2 * n]


def exchange_wait(name, send, recv, parts, lands, after):
    n = len(parts)
    afters = _as_list(after)

    def body(*refs):
        src, got = refs[:n], refs[n:2 * n]
        send_ref, recv_ref = refs[2 * n], refs[2 * n + 1]
        x, y, c, _ = _place()
        sib = (x, y, 1 - c)
        for a in range(n):
            cp = pltpu.make_async_remote_copy(src[a].at[1 - c], got[a], send_ref.at[a], recv_ref.at[a], device_id=sib, device_id_type=MESH)
            cp.wait_send()
            cp.wait_recv()

    outs = pl.pallas_call(
        body, name=name,
        out_shape=(*[pltpu.HBM(p.shape, p.dtype) for p in parts], *[pltpu.HBM(l.shape, l.dtype) for l in lands]),
        in_specs=[HBM] * (2 * n) + [SEM, SEM] + [ANY] * len(afters), out_specs=[HBM] * (2 * n),
        input_output_aliases={i: i for i in range(2 * n)},
        compiler_params=pltpu.CompilerParams(has_side_effects=EFFECT),
    )(*parts, *lands, send, recv, *afters)
    return list(outs[:n]), list(outs[n:])


def scatter_start(name, parts):
    n = len(parts)

    def body(*refs):
        src, land = refs[:n], refs[n:2 * n]
        send, recv = refs[2 * n], refs[2 * n + 1]
        token = refs[4 * n + 2]
        x, y, c, chips = _place()
        for a in range(n):
            for j, (cx, cy) in enumerate(chips):
                for sp, dp in zip(_pieces(src[a].at[2 * cx + cy]), _pieces(land[a].at[j])):
                    pltpu.make_async_remote_copy(sp, dp, send.at[3 * a + j], recv.at[3 * a + j],
                                                 device_id=(cx, cy, c), device_id_type=MESH).start()
        token[...] = jnp.zeros_like(token)

    lands = [pltpu.with_memory_space_constraint(lax.empty((NCHIP - 1,) + p.shape[1:], p.dtype), pltpu.HBM) for p in parts]
    srcs = [pltpu.with_memory_space_constraint(p, pltpu.HBM) for p in parts]
    outs = pl.pallas_call(
        body, name=name,
        out_shape=(pltpu.SemaphoreType.DMA((3 * n,)), pltpu.SemaphoreType.DMA((3 * n,)),
                   *[pltpu.HBM(p.shape, p.dtype) for p in parts], *[pltpu.HBM(l.shape, l.dtype) for l in lands],
                   SDS((8, 128), F32)),
        in_specs=[HBM] * (2 * n), out_specs=(SEM, SEM, *([HBM] * (2 * n)), pl.BlockSpec(memory_space=pltpu.VMEM)),
        input_output_aliases={i: 2 + i for i in range(2 * n)},
        compiler_params=pltpu.CompilerParams(has_side_effects=EFFECT),
    )(*srcs, *lands)
    return outs[0], outs[1], list(outs[2:2 + n]), list(outs[2 + n:2 + 2 * n]), outs[2 + 2 * n]


def scatter_wait(name, send, recv, parts, lands, after):
    n = len(parts)
    afters = _as_list(after)

    def body(*refs):
        src, land = refs[:n], refs[n:2 * n]
        send_ref, recv_ref = refs[2 * n], refs[2 * n + 1]
        x, y, c, chips = _place()
        for a in range(n):
            for j, (cx, cy) in enumerate(chips):
                cp = pltpu.make_async_remote_copy(src[a].at[2 * cx + cy], land[a].at[j], send_ref.at[3 * a + j], recv_ref.at[3 * a + j],
                                                  device_id=(cx, cy, c), device_id_type=MESH)
                cp.wait_send()
                cp.wait_recv()

    outs = pl.pallas_call(
        body, name=name,
        out_shape=(*[pltpu.HBM(p.shape, p.dtype) for p in parts], *[pltpu.HBM(l.shape, l.dtype) for l in lands]),
        in_specs=[HBM] * (2 * n) + [SEM, SEM] + [ANY] * len(afters), out_specs=[HBM] * (2 * n),
        input_output_aliases={i: i for i in range(2 * n)},
        compiler_params=pltpu.CompilerParams(has_side_effects=EFFECT),
    )(*parts, *lands, send, recv, *afters)
    return list(outs[:n]), list(outs[n:])


def join_start(name, halves):
    n = len(halves)

    def body(*refs):
        src, dst = refs[:n], refs[n:2 * n]
        send, recv = refs[2 * n], refs[2 * n + 1]
        token = refs[4 * n + 2]
        x, y, c, _ = _place()
        sib = (x, y, 1 - c)
        for a in range(n):
            for sp, dp in zip(_pieces(src[a]), _pieces(dst[a])):
                pltpu.make_async_remote_copy(sp, dp, send.at[a], recv.at[a], device_id=sib, device_id_type=MESH).start()
        token[...] = jnp.zeros_like(token)

    lands = [pltpu.with_memory_space_constraint(lax.empty(h.shape, h.dtype), pltpu.HBM) for h in halves]
    srcs = [pltpu.with_memory_space_constraint(h, pltpu.HBM) for h in halves]
    outs = pl.pallas_call(
        body, name=name,
        out_shape=(pltpu.SemaphoreType.DMA((n,)), pltpu.SemaphoreType.DMA((n,)),
                   *[pltpu.HBM(h.shape, h.dtype) for h in halves], *[pltpu.HBM(l.shape, l.dtype) for l in lands],
                   SDS((8, 128), F32)),
        in_specs=[HBM] * (2 * n), out_specs=(SEM, SEM, *([HBM] * (2 * n)), pl.BlockSpec(memory_space=pltpu.VMEM)),
        input_output_aliases={i: 2 + i for i in range(2 * n)},
        compiler_params=pltpu.CompilerParams(has_side_effects=EFFECT),
    )(*srcs, *lands)
    return outs[0], outs[1], list(outs[2:2 + n]), list(outs[2 + n:2 + 2 * n]), outs[2 + 2 * n]


def join_wait(name, send, recv, halves, lands, after):
    n = len(halves)
    afters = _as_list(after)

    def body(*refs):
        src, dst = refs[:n], refs[n:2 * n]
        send_ref, recv_ref = refs[2 * n], refs[2 * n + 1]
        x, y, c, _ = _place()
        sib = (x, y, 1 - c)
        for a in range(n):
            cp = pltpu.make_async_remote_copy(src[a], dst[a], send_ref.at[a], recv_ref.at[a], device_id=sib, device_id_type=MESH)
            cp.wait_send()
            cp.wait_recv()

    outs = pl.pallas_call(
        body, name=name,
        out_shape=(*[pltpu.HBM(h.shape, h.dtype) for h in halves], *[pltpu.HBM(l.shape, l.dtype) for l in lands]),
        in_specs=[HBM] * (2 * n) + [SEM, SEM] + [ANY] * len(afters), out_specs=[HBM] * (2 * n),
        input_output_aliases={i: i for i in range(2 * n)},
        compiler_params=pltpu.CompilerParams(has_side_effects=EFFECT),
    )(*halves, *lands, send, recv, *afters)
    return list(outs[:n]), list(outs[n:])


def gather_small(name, xs, reduce, after=None):
    m, ncol = xs.shape
    afters = _as_list(after)

    def body(x_ref, *rest):
        out_ref, all_ref, send, recv, lsem = rest[len(afters):]
        x, y, c, chips = _place()
        me, sib = (x, y, c), (x, y, 1 - c)

        def rows(px, py, pc):
            return all_ref.at[pl.ds((4 * px + 2 * py + pc) * m, m), :]

        def copy(k, block, to, src=None):
            return pltpu.make_async_remote_copy(rows(*block) if src is None else src, rows(*block), send.at[k], recv.at[k],
                                                device_id=to, device_id_type=MESH)

        mine = pltpu.make_async_copy(x_ref, rows(*me), lsem)
        mine.start()
        first = [copy(0, me, sib, src=x_ref)] + [copy(1 + j, me, (*chip, c), src=x_ref) for j, chip in enumerate(chips)]
        for cp in first:
            cp.start()
        passed = [copy(4 + j, (*chip, c), sib) for j, chip in enumerate(chips)]
        for j, chip in enumerate(chips):
            copy(1 + j, (*chip, c), me).wait_recv()
            passed[j].start()
        copy(0, sib, me).wait_recv()
        for j, chip in enumerate(chips):
            copy(4 + j, (*chip, 1 - c), me).wait_recv()
        for cp in first + passed:
            cp.wait_send()
        mine.wait()
        if reduce:
            s = all_ref[0:m, :]
            for dev in range(1, 8):
                s = s + all_ref[dev * m:(dev + 1) * m, :]
            out_ref[...] = s
        else:
            out_ref[...] = all_ref[...]

    vm = pl.BlockSpec(memory_space=pltpu.VMEM)
    return pl.pallas_call(
        body, name=name, in_specs=[vm] + [ANY] * len(afters), out_specs=vm,
        out_shape=SDS((m, ncol) if reduce else (8 * m, ncol), F32),
        scratch_shapes=[pltpu.VMEM((8 * m, ncol), F32), pltpu.SemaphoreType.DMA((7,)), pltpu.SemaphoreType.DMA((7,)),
                        pltpu.SemaphoreType.DMA],
    )(xs, *afters)


RELAYOUT_ROWS = 128


def weights_to_cat(name, land, own, place, other, prev=None, after=None):
    tm = RELAYOUT_ROWS
    nb = (D // 2) // tm
    extra = ([] if prev is None else [prev]) + _as_list(after)

    def half(p):
        return 1 - p[0] if other else p[0]

    def body(p_ref, g_ref, own_ref, *rest):
        o_ref = rest[len(extra)]
        nat = jnp.concatenate([jnp.where(p_ref[1] == j, own_ref[...], g_ref[j]) for j in range(NCHIP)], axis=1)
        pad = jnp.zeros((tm, NCAT - OA - 16), BF16)
        o_ref[...] = jnp.concatenate([nat[:, 3072:7168], nat[:, 7184:11280], nat[:, 0:3072], nat[:, 7168:7184], pad], axis=1)

    grid_spec = pltpu.PrefetchScalarGridSpec(
        num_scalar_prefetch=1, grid=(nb,),
        in_specs=[pl.BlockSpec((NCHIP, None, tm, IN_SHARD), lambda i, p: (0, half(p), i, 0)),
                  pl.BlockSpec((None, tm, IN_SHARD), lambda i, p: (half(p), i, 0))] + [ANY] * len(extra),
        out_specs=pl.BlockSpec((tm, NCAT), lambda i, p: (half(p) * nb + i, 0)))
    return pl.pallas_call(
        body, name=name, grid_spec=grid_spec, out_shape=SDS((D, NCAT), BF16),
        input_output_aliases={} if prev is None else {3: 0},
        compiler_params=_cparams(40 * 1024 * 1024, ("arbitrary",)),
    )(place, land, own, *extra)


def grads_from_cat(gw_cat):
    tm = RELAYOUT_ROWS
    nb = (D // 2) // tm

    def body(c_ref, o_ref):
        cat = c_ref[...]
        nat = jnp.concatenate([cat[:, OU:OA], cat[:, OV:OGP], cat[:, OA:OA + 16], cat[:, OGP:OU]], axis=1)
        for j in range(NCHIP):
            o_ref[j] = nat[:, j * IN_SHARD:(j + 1) * IN_SHARD]

    return pl.pallas_call(
        body, name="grads_from_cat", grid=(D // tm,), in_specs=[pl.BlockSpec((tm, NCAT), lambda i: (i, 0))],
        out_specs=pl.BlockSpec((None, NCHIP, tm, IN_SHARD), lambda i: (i // nb, 0, i % nb, 0)),
        out_shape=SDS((2, NCHIP, D // 2, IN_SHARD), BF16), compiler_params=_cparams(40 * 1024 * 1024, ("arbitrary",)),
    )(gw_cat)


def _pad_rows(a, rows):
    return jnp.concatenate([a, jnp.zeros((rows - a.shape[0],) + a.shape[1:], a.dtype)], axis=0)


def local_step(x2d, tgt, gf, g1, pool_scale, wa_pad, b_alpha, ng, g2, get_w, on_grad=None, on_settle=None, tick=None):
    emit = on_grad if on_grad is not None else (lambda group, grads: None)
    settle = on_settle if on_settle is not None else (lambda group, after: None)
    h1 = norm1(x2d, g1)
    wcat, pw = get_w("in", h1)
    pcat = mm_in(h1, wcat)
    dpool, ylin = pool_fwd(pcat, pw)
    pinned = tick("pool", ylin) if tick is not None else None
    og, o, states = gla_fwd(pcat, wa_pad, b_alpha, ng, pinned)
    w_go, w_o = get_w("mid", og)
    mixed, ygla = mm_gla_out(og, w_go, ylin, pcat, pool_scale)
    x2, h2 = mm_out(mixed, w_o, x2d, g2)
    w_up = get_w("up", h2)
    rup, act = mm_up(h2, w_up)
    w_dn = get_w("down", act)
    dx3, dx3b, g_nf, loss_row = mm_down(act, w_dn, x2, tgt, gf)

    gw_down = mm_wgrad("mm_dw_down", act, dx3b, DFF, D, (2, NCHIP, D // 2, D), (None, None, D // 2, D),
                       lambda j, i, k: (i % 2, i // 2, 0, 0), D // 2, D)
    token = emit("down", {"down": gw_down})
    dup = mm_dact(dx3b, w_dn, rup, after=token)
    token = settle("down", dup)
    dx2, dx2b, g_mlp = mm_dh2(dup, w_up, x2, dx3, g2, after=token)
    gw_up = mm_wgrad("mm_dw_up", h2, dup, D, DFF, (2, NCHIP, D // 2, D), (None, None, D // 2, D),
                     lambda j, i, k: (i, j, 0, 0), D // 2, D)
    token = emit("up", {"up": gw_up})
    dylin, dygla, dpcat, g_ps = mm_dmixed(dx2b, w_o, pcat, ylin, ygla, pool_scale, after=token)
    token = settle("up", dylin)
    gw_out = mm_wgrad("mm_dw_out", mixed, dx2b, D, D, (2, NCHIP, 256, D), (2, None, 256, D),
                      lambda j, i, k: (0, i, 0, 0), 512, D)
    do, dpcat, g_ng = mm_dog(dygla, w_go, o, pcat, ng, dpcat, after=token)
    gw_go = mm_wgrad("mm_dw_gla_out", og, dygla, D, D, (2, NCHIP, 256, D), (2, None, 256, D),
                     lambda j, i, k: (0, i, 0, 0), 512, D)
    token = emit("mix", {"out": gw_out, "gla_out": gw_go})
    dpcat, dv, g_wa, g_ba = gla_bwd(do, pcat, states, wa_pad, b_alpha, dpcat, b_alpha if token is None else token)
    token = settle("mix", dv)
    dpcat, dpw = pool_bwd(dylin, dpool, pw, lax.dynamic_update_slice(dpcat, dv, (0, OV)))
    gw_cat = mm_wgrad("mm_dw_in", h1, dpcat, D, NCAT, (D, NCAT), (1024, 1280), lambda j, i, k: (i, j), 1024, 1280, after=token)
    token = settle("in", emit("in", {"in_cat": gw_cat, "pool": dpw}))
    grad_x, g_mix = mm_dh1(dpcat, wcat, x2d, dx2, g1, after=token)
    return (loss_row[0, 0], grad_x, g_mix, g_ps, g_mlp, g_nf, g_ng, g_ba, g_wa, token,
            gw_cat, dpw, gw_go, gw_out, gw_up, gw_down)


def kernel(x, norm_mix_g, w_in, pool_w, pool_scale, w_alpha, b_alpha, gla_norm_g, w_gla_out, w_out, norm_mlp_g, w_mlp_up, w_mlp_down, norm_final_g, loss_target, m_norm_mix_g, m_w_in, m_pool_w, m_pool_scale, m_w_alpha, m_b_alpha, m_gla_norm_g, m_w_gla_out, m_w_out, m_norm_mlp_g, m_w_mlp_up, m_w_mlp_down, m_norm_final_g, v_norm_mix_g, v_w_in, v_pool_w, v_pool_scale, v_w_alpha, v_b_alpha, v_gla_norm_g, v_w_gla_out, v_w_out, v_norm_mlp_g, v_w_mlp_up, v_w_mlp_down, v_norm_final_g):
    chip = 2 * lax.axis_index("x") + lax.axis_index("y")
    chip_i = chip.astype(jnp.int32).reshape(1)
    core_i = lax.axis_index("c").astype(jnp.int32).reshape(1)
    place_i = jnp.concatenate([core_i, chip_i])
    tgt = loss_target.reshape(T, D)
    gf = norm_final_g.reshape(1, D)

    def halves(w2d):
        r, c = w2d.shape
        return w2d.astype(BF16).reshape(2, r // 2, c)

    pool_shard = pool_w.reshape(4 * PG, PO // NCHIP)
    w_in_r = w_in.reshape(2, D // 2, IN_SHARD)
    sent = {"in": [cast_bf16("cast_w_in", w_in_r), halves(pool_shard)]}
    flight = {}

    def start(group, after=None):
        flight[group] = gather_start("gather_start_" + group, sent[group], after)

    def relay(group, after):
        send, recv, shards, lands = flight[group]
        flight[group] = relay_turn("relay_turn_" + group, send, recv, shards, lands, after)

    def fetch(group, after):
        send, recv, shards, lands = flight[group]
        lands = relay_wait("relay_wait_" + group, send, recv, lands, after)
        return forward_halves("forward_" + group, shards, lands)

    start("in")
    m_in_f, v_in_f, w_go_f, w_o_f, w_up_f, w_dn_f, x_f, wal_f, gng_f = lax.optimization_barrier(
        (m_w_in, v_w_in, w_gla_out, w_out, w_mlp_up, w_mlp_down, x, w_alpha, gla_norm_g, flight["in"][2][0]))[:9]
    m_in_r, v_in_r = m_in_f.reshape(2, D // 2, IN_SHARD), v_in_f.reshape(2, D // 2, IN_SHARD)
    sent["mid"] = [halves(w_go_f[0]), halves(w_o_f[0])]
    relay("in", [m_in_r, v_in_r, *sent["mid"]])
    w_up_f, w_dn_f, x_f, wal_f, gng_f = lax.optimization_barrier(
        (w_up_f, w_dn_f, x_f, wal_f, gng_f, flight["in"][3][0]))[:5]
    sent["up"], sent["down"] = [halves(w_up_f[0])], [halves(w_dn_f[0])]
    x2d = x_f.reshape(T, D)
    big = [w_in_r, w_go_f[0], w_o_f[0], w_up_f[0], w_dn_f[0], pool_shard]

    def tick(point, after):
        if point == "pool":
            relay("mid", after)
            relay("up", flight["mid"][3][0])
            start("down", flight["up"][3][0])
            return [flight["up"][3][0], flight["down"][3][0]]

    def get_w(group, after):
        if group == "in":
            after = [after, *sent["up"], *sent["down"], wa_pad]
        if group == "up":
            relay("down", after)
            send, recv, lands, shards = flight["up"]
            lands = forward_wait("forward_wait_up", send, recv, lands, flight["down"][3][0])
            return lax.dynamic_update_index_in_dim(lands[0], shards[0], chip, 0).reshape(NCHIP, D, D)
        if group == "in":
            send, recv, shards, lands = flight["in"]
            send, recv, lands = forward_turn("forward_turn_in", send, recv, lands, after)
            start("mid", lands[0])
            start("up", flight["mid"][3][0])
            wcat = weights_to_cat("weights_to_cat_mine", lands[0], shards[0], place_i, False, after=flight["up"][3][0])
            lands = forward_wait("forward_wait_in", send, recv, lands, wcat)
            wcat = weights_to_cat("weights_to_cat_sibling", lands[0], shards[0], place_i, True, prev=wcat)
            g_pool = lax.dynamic_update_index_in_dim(lands[1], shards[1], chip, 0)
            pw = jnp.concatenate([g_pool[j].reshape(4, PG, PO // NCHIP) for j in range(NCHIP)], axis=2)
            return wcat, pw
        whole = fetch(group, after)
        if group == "mid":
            send, recv, shards, lands = flight["up"]
            flight["up"] = (*forward_turn("forward_turn_up", send, recv, lands, whole[0]), shards)
            w_go, w_o, _ = lax.optimization_barrier((whole[0], whole[1], flight["up"][2][0]))
            return w_go.reshape(D, D), w_o.reshape(D, D)
        return whole[0].reshape(DFF, D)

    small_w = pack_rows("pack_small_w", [wal_f[0].reshape(4, QK),
                                         jnp.concatenate([gng_f[0].reshape(1, 512), jnp.zeros((1, 512), F32)], axis=1)], 8)
    sw_all = gather_small("gather_small_w", small_w, False).reshape(8, 8, QK)
    wa_full = jnp.concatenate([sw_all[2 * j, 0:4].reshape(16, DK) for j in range(NCHIP)], axis=1)
    ng_full = jnp.concatenate([sw_all[2 * j, 4, 0:512].reshape(HEADS, DV // NCHIP) for j in range(NCHIP)], axis=1)
    wa_pad = _pad_rows(wa_full, APAD).astype(BF16)
    ng = ng_full.reshape(1, D)

    pending = {}
    wmv = {"in": (w_in_r, m_in_r, v_in_r), "gla_out": (big[1], m_w_gla_out, v_w_gla_out), "out": (big[2], m_w_out, v_w_out),
           "up": (big[3], m_w_mlp_up, v_w_mlp_up), "down": (big[4], m_w_mlp_down, v_w_mlp_down), "pool": (big[5], m_pool_w, v_pool_w)}
    big_res = {}

    def reduce_group(group, after):
        nms, send, recv, sums, lands = pending[group]
        sums, lands = scatter_wait("scatter_wait_" + group, send, recv, sums, lands, after)
        reduced = [sum_chips("sum_chips_" + nm, a, b, chip_i) for nm, a, b in zip(nms, sums, lands)]
        send, recv, reduced, lands, token = join_start("join_start_" + group, reduced)
        pending[group] = (nms, send, recv, reduced, lands)
        return token

    def update_group(group, after):
        nms, send, recv, reduced, lands = pending[group]
        reduced, from_sib = join_wait("join_wait_" + group, send, recv, reduced, lands, after)
        for nm, g_own, g_sib in zip(nms, reduced, from_sib):
            w, m, v = wmv[nm]
            shp = (2,) + g_own.shape
            big_res[nm] = adamw_halves("adamw_" + nm, w.reshape(shp), g_own, g_sib, m.reshape(shp), v.reshape(shp), core_i)

    def on_grad(group, grads):
        if group == "in":
            gw_in = grads_from_cat(grads["in_cat"])
            gw_pool = jnp.stack([grads["pool"][:, :, j * 128:(j + 1) * 128].reshape(2, 2 * PG, 128)
                                 for j in range(NCHIP)], axis=1)
            grads = {"in": gw_in, "pool": gw_pool}
        nms, parts = list(grads.keys()), list(grads.values())
        send, recv, parts, got, token = exchange_start("exchange_start_" + group, parts)
        pending[group] = (nms, send, recv, parts, got)
        return token

    def on_settle(group, after):
        if group == "in":
            for earlier in ("down", "up", "mix"):
                after = reduce_group(earlier, after)
        nms, send, recv, parts, got = pending[group]
        parts, got = exchange_wait("exchange_wait_" + group, send, recv, parts, got, after)
        sums = [add_pairs("add_pair_" + nm, a, b, core_i) for nm, a, b in zip(nms, parts, got)]
        send, recv, sums, lands, token = scatter_start("scatter_start_" + group, sums)
        pending[group] = (nms, send, recv, sums, lands)
        if group != "in":
            return token
        for earlier in ("down", "up", "mix"):
            update_group(earlier, token)
            token = big_res[pending[earlier][0][-1]][1]
        return [big_res[nm][1] for nm in ("down", "up", "out", "gla_out")]

    (loss_local, grad_x, g_mix, g_ps, g_mlp, g_nf, g_ng, g_ba, g_wa) = local_step(
        x2d, tgt, gf, norm_mix_g, pool_scale, wa_pad, b_alpha, ng, norm_mlp_g, get_w, on_grad, on_settle, tick)[:9]
    loss = lax.psum(loss_local, ("x", "y", "c"))
    join_in_token = reduce_group("in", grad_x)

    ROWS = 16

    def wide(a, n):
        return jnp.concatenate([a.reshape(1, n), jnp.zeros((1, D - n), F32)], axis=1)

    packed = pack_rows("pack_small_g", [g_mix, g_ps, g_mlp, g_nf, g_ng, wide(g_ba, QK), g_wa[0:16].reshape(8, D)], ROWS)
    tot = gather_small("reduce_small_g", packed, True, join_in_token)
    t_wa = lax.dynamic_slice(tot[6:14].reshape(16, QK), (0, chip * DK), (16, DK))
    t_ng = lax.dynamic_slice(tot[4].reshape(HEADS, DV), (0, chip * (DV // NCHIP)), (HEADS, DV // NCHIP))

    def pack_small(nm, mix, ps, mlp, nf, ba, wa, gn, after=None):
        return pack_rows(nm, [mix.reshape(1, D), ps.reshape(1, D), mlp.reshape(1, D), nf.reshape(1, D), wide(ba, QK),
                              wa.reshape(2, D), wide(gn, 512)], ROWS, after)

    update_group("in", tot)
    sg = pack_small("pack_g", tot[0], tot[1], tot[2], tot[3], tot[5, 0:QK], t_wa, t_ng, big_res["in"][3])
    sw = pack_small("pack_w", norm_mix_g, pool_scale, norm_mlp_g, norm_final_g, b_alpha, w_alpha, gla_norm_g)
    sm = pack_small("pack_m", m_norm_mix_g, m_pool_scale, m_norm_mlp_g, m_norm_final_g, m_b_alpha, m_w_alpha, m_gla_norm_g)
    sv = pack_small("pack_v", v_norm_mix_g, v_pool_scale, v_norm_mlp_g, v_norm_final_g, v_b_alpha, v_w_alpha, v_gla_norm_g)
    small_res = adamw("adamw_small", sw, sg, sm, sv)

    def unpack(p):
        return {"norm_mix_g": p[0].reshape(1, D), "pool_scale": p[1].reshape(1, D), "norm_mlp_g": p[2].reshape(1, D),
                "norm_final_g": p[3].reshape(D), "b_alpha": p[4, 0:QK].reshape(1, QK), "w_alpha": p[5:7].reshape(1, 16, DK),
                "gla_norm_g": p[7, 0:512].reshape(1, HEADS, DV // NCHIP)}

    order = ["norm_mix_g", "w_in", "pool_w", "pool_scale", "w_alpha", "b_alpha", "gla_norm_g", "w_gla_out", "w_out",
             "norm_mlp_g", "w_mlp_up", "w_mlp_down", "norm_final_g"]
    big_key = {"w_in": ("in", w_in.shape), "pool_w": ("pool", pool_w.shape), "w_gla_out": ("gla_out", w_gla_out.shape),
               "w_out": ("out", w_out.shape), "w_mlp_up": ("up", w_mlp_up.shape), "w_mlp_down": ("down", w_mlp_down.shape)}
    result = [loss, grad_x.reshape(1, T, D)]
    for kind in range(4):
        small = unpack(small_res[kind])
        for nm in order:
            if nm in big_key:
                key, shp = big_key[nm]
                result.append(big_res[key][kind].reshape(shp))
            else:
                result.append(small[nm])
    return tuple(result)
```

```python
import itertools

import jax
import jax.numpy as jnp
from jax import lax
from jax.experimental import pallas as pl
from jax.experimental.pallas import tpu as pltpu

F32 = jnp.float32
BF16 = jnp.bfloat16
SDS = jax.ShapeDtypeStruct
PINNED = pltpu.HBM
MESH = pl.DeviceIdType.MESH
ANY = pl.BlockSpec(memory_space=pl.ANY)

T = 2048
D = 2048
DFF = 8192
NCHIP = 4
IN_WIDTH = 11280
IN_SHARD = IN_WIDTH // NCHIP
CHUNK = 64
NCHUNK = T // CHUNK
HEADS = 4
DK = 256
DV = 512
QK = HEADS * DK
EPS = 1e-6
POOL_WINDOWS = (2, 4, 8, 16)
PG = 256
PO = 512

OV, OG, OGP, OGG, OU, OQ, OKK, OA = 0, 2048, 4096, 6144, 8192, 9216, 10240, 11264
NCAT = 11520
APAD = 128

VMEM_CAP = 56 * 1024 * 1024

PIECE_BYTES = 384 * 1024

ADAM_LR, ADAM_B1, ADAM_B2, ADAM_EPS, ADAM_WD, ADAM_STEP = 0.001, 0.9, 0.999, 1e-08, 0.01, 10


def _cparams(vmem_bytes=None, sem=None):
    kw = {}
    if vmem_bytes is not None:
        kw["vmem_limit_bytes"] = int(min(max(vmem_bytes, 32 * 1024 * 1024), VMEM_CAP))
    if sem is not None:
        kw["dimension_semantics"] = sem
    return pltpu.CompilerParams(**kw)


def _nbytes(shape, dtype):
    n = 1
    for s in shape:
        if s is not None:
            n *= s
    return n * jnp.dtype(dtype).itemsize


def _sigmoid(x):
    return 0.5 * jnp.tanh(0.5 * x) + 0.5


GLA_STEP = 4
EPI_COLS = 512


def _as_list(after):
    if after is None:
        return []
    return list(after) if isinstance(after, (list, tuple)) else [after]


def _in_hbm(*arrays):
    return [pltpu.with_memory_space_constraint(a, pltpu.HBM) for a in arrays]


def matmul(name, a, b, *, a_spec, b_spec, cdims, grid, acc_shape, outs, extras=(), epi, after=None, into=None):
    nj, ni, nk = grid
    ne, no = len(extras), len(outs)
    afters = _as_list(after) + ([] if into is None else [into[0]])
    first_out = 2 + ne + len(afters)

    def body(*refs):
        a_ref, b_ref = refs[0], refs[1]
        ex = refs[2:2 + ne]
        out_refs = refs[first_out:first_out + no]
        i = pl.program_id(1)
        part = lax.dot_general(a_ref[...], b_ref[...], (cdims, ((), ())), preferred_element_type=F32)
        if nk == 1:
            epi(part, ex, out_refs, i)
        else:
            acc_ref = refs[first_out + no]
            k = pl.program_id(2)

            @pl.when(k == 0)
            def _():
                acc_ref[...] = part

            @pl.when(k > 0)
            def _():
                acc_ref[...] += part

            @pl.when(k == nk - 1)
            def _():
                epi(acc_ref[...], ex, out_refs, i)

    in_specs = [pl.BlockSpec(*a_spec), pl.BlockSpec(*b_spec)] + [pl.BlockSpec(bs, im) for _, bs, im in extras]
    in_specs += [ANY] * len(afters)
    out_specs = [pl.BlockSpec(bs, im) for _, _, bs, im in outs]
    out_shape = [PINNED(s, dt) for s, dt, _, _ in outs]
    vm = 2 * (_nbytes(a_spec[0], a.dtype) + _nbytes(b_spec[0], b.dtype))
    vm += 2 * sum(_nbytes(bs, arr.dtype) for arr, bs, _ in extras)
    vm += 2 * sum(_nbytes(bs, dt) for _, dt, bs, _ in outs)
    vm += 6 * _nbytes(acc_shape, F32)
    scratch = [pltpu.VMEM(acc_shape, F32)] if nk > 1 else []
    return pl.pallas_call(
        body, name=name, grid=grid, in_specs=in_specs, out_specs=out_specs, out_shape=out_shape,
        scratch_shapes=scratch,
        input_output_aliases={} if into is None else {first_out - 1: into[1]},
        compiler_params=_cparams(vm, ("arbitrary", "arbitrary", "arbitrary")),
    )(*_in_hbm(a, b, *[arr for arr, _, _ in extras]), *afters)


NN =((1,), (0,))
NT = ((1,), (1,))
TN = ((0,), (0,))


def _row_acc(out_ref, val, i):
    @pl.when(i == 0)
    def _():
        out_ref[...] = val

    @pl.when(i > 0)
    def _():
        out_ref[...] += val


def _rms_bwd(xn, r, dxn):
    return r * (dxn - xn * jnp.mean(dxn * xn, axis=-1, keepdims=True))


def norm1(x, g):
    tm = 256

    def body(x_ref, g_ref, h_ref):
        xv = x_ref[...]
        r = lax.rsqrt(jnp.mean(xv * xv, axis=-1, keepdims=True) + EPS)
        h_ref[...] = (xv * r * g_ref[...]).astype(BF16)

    return pl.pallas_call(
        body, name="norm1", grid=(T // tm,),
        in_specs=[pl.BlockSpec((tm, D), lambda i: (i, 0)), pl.BlockSpec((1, D), lambda i: (0, 0))],
        out_specs=pl.BlockSpec((tm, D), lambda i: (i, 0)), out_shape=PINNED((T, D), BF16),
        compiler_params=_cparams(32 * 1024 * 1024, ("arbitrary",)),
    )(*_in_hbm(x, g))


def mm_in(h1, wcat):
    tm, tn = 1024, 1280

    def epi(acc, ex, outs, i):
        outs[0][...] = acc.astype(BF16)

    return matmul("mm_in", h1, wcat, a_spec=((tm, D), lambda j, i, k: (i, 0)), b_spec=((D, tn), lambda j, i, k: (0, j)),
                  cdims=NN, grid=(NCAT // tn, T // tm, 1), acc_shape=(tm, tn),
                  outs=[((T, NCAT), BF16, (tm, tn), lambda j, i, k: (i, j))], epi=epi)[0]


def _window_sum(x, w, up):
    n = x.shape[0]
    row = lax.broadcasted_iota(jnp.int32, x.shape, 0)
    s, sh = x, 1
    while sh < w:
        if up:
            s = s + jnp.where(row < n - sh, pltpu.roll(s, n - sh, axis=0), 0.0)
        else:
            s = s + jnp.where(row >= sh, pltpu.roll(s, sh, axis=0), 0.0)
        sh *= 2
    return s


def _inv_count(shape, w):
    row = lax.broadcasted_iota(jnp.int32, shape, 0)
    return 1.0 / jnp.minimum(row + 1, w).astype(F32)


def pool_fwd(pcat, pw):
    def body(u_ref, pw_ref, d_ref, y_ref):
        for gi, w in enumerate(POOL_WINDOWS):
            ug = u_ref[:, gi * PG:(gi + 1) * PG].astype(F32)
            dg = _window_sum(ug, w, False) * _inv_count(ug.shape, w) - ug
            db = dg.astype(BF16)
            d_ref[:, gi * PG:(gi + 1) * PG] = db
            y_ref[:, gi * PO:(gi + 1) * PO] = jnp.dot(db, pw_ref[gi], preferred_element_type=F32).astype(BF16)

    return pl.pallas_call(
        body, name="pool_fwd", grid=(1,),
        in_specs=[pl.BlockSpec((T, 4 * PG), lambda i: (0, OU // (4 * PG))), pl.BlockSpec((4, PG, PO), lambda i: (0, 0, 0))],
        out_specs=[pl.BlockSpec((T, 4 * PG), lambda i: (0, 0)), pl.BlockSpec((T, D), lambda i: (0, 0))],
        out_shape=[PINNED((T, 4 * PG), BF16), PINNED((T, D), BF16)],
        compiler_params=_cparams(48 * 1024 * 1024, ("arbitrary",)),
    )(pcat, pw)


def pool_bwd(dylin, d, pw, dproj):
    assert OU % (4 * PG) == 0

    def body(dy_ref, d_ref, pw_ref, held_ref, du_ref, dpw_ref):
        for gi, w in enumerate(POOL_WINDOWS):
            dyl = dy_ref[:, gi * PO:(gi + 1) * PO]
            dd = lax.dot_general(dyl, pw_ref[gi], (NT, ((), ())), preferred_element_type=F32)
            du = _window_sum(dd * _inv_count(dd.shape, w), w, True) - dd
            du_ref[:, gi * PG:(gi + 1) * PG] = du.astype(BF16)
            dpw_ref[gi] = lax.dot_general(d_ref[:, gi * PG:(gi + 1) * PG], dyl, (TN, ((), ())),
                                          preferred_element_type=F32).astype(BF16)

    return pl.pallas_call(
        body, name="pool_bwd", grid=(1,),
        in_specs=[pl.BlockSpec((T, D), lambda i: (0, 0)), pl.BlockSpec((T, 4 * PG), lambda i: (0, 0)),
                  pl.BlockSpec((4, PG, PO), lambda i: (0, 0, 0)), ANY],
        out_specs=[pl.BlockSpec((T, 4 * PG), lambda i: (0, OU // (4 * PG))), pl.BlockSpec((4, PG, PO), lambda i: (0, 0, 0))],
        out_shape=[PINNED((T, NCAT), BF16), PINNED((4, PG, PO), BF16)],
        input_output_aliases={3: 0},
        compiler_params=_cparams(48 * 1024 * 1024, ("arbitrary",)),
    )(dylin, d, pw, dproj)


def _gate_decay(alow, wa, ba):
    a = jnp.dot(alow, wa, preferred_element_type=F32) + ba
    ls = jax.nn.log_sigmoid(a) * (1.0 / 16.0)
    r = lax.broadcasted_iota(jnp.int32, (CHUNK, CHUNK), 0)
    c = lax.broadcasted_iota(jnp.int32, (CHUNK, CHUNK), 1)
    tri = jnp.where(c <= r, 1.0, 0.0).astype(F32)
    cum = jnp.dot(tri, ls, preferred_element_type=F32, precision=lax.Precision.HIGHEST)
    last = cum[CHUNK - 1:CHUNK, :]
    return a, jnp.exp(last - cum), jnp.exp(last)


def gla_fwd(pcat, wa, ba, ng, after=None):
    afters = _as_list(after)

    def body(q_ref, k_ref, v_ref, g_ref, al_ref, wa_ref, ba_ref, ng_ref, *rest):
        og_ref, o_ref, st_ref, s_scr = rest[len(afters):]

        @pl.when(pl.program_id(0) == 0)
        def _():
            s_scr[...] = jnp.zeros_like(s_scr)

        state = [s_scr[h] for h in range(HEADS)]
        for s in range(GLA_STEP):
            rs = slice(s * CHUNK, (s + 1) * CHUNK)
            _, e, decay = _gate_decay(al_ref[rs, :], wa_ref[...], ba_ref[...])
            kd = (k_ref[rs, :].astype(F32) * e).astype(BF16)
            qs = (q_ref[rs, :].astype(F32) * (DK ** -0.5)).astype(BF16)
            for h in range(HEADS):
                ck = slice(h * DK, (h + 1) * DK)
                cv = slice(h * DV, (h + 1) * DV)
                state[h] = state[h] * decay[:, ck] + lax.dot_general(v_ref[rs, cv], kd[:, ck], (TN, ((), ())),
                                                                     preferred_element_type=F32)
                sb = state[h].astype(BF16)
                st_ref[s, h] = sb
                oh = lax.dot_general(qs[:, ck], sb, (NT, ((), ())), preferred_element_type=F32)
                o_ref[rs, cv] = oh.astype(BF16)
                on = oh * lax.rsqrt(jnp.mean(oh * oh, axis=-1, keepdims=True) + EPS) * ng_ref[:, cv]
                gv = g_ref[rs, cv].astype(F32)
                og_ref[rs, cv] = (on * (gv * _sigmoid(gv))).astype(BF16)
        for h in range(HEADS):
            s_scr[h] = state[h]

    row = lambda c: (c, 0)
    rows = GLA_STEP * CHUNK
    return pl.pallas_call(
        body, name="gla_fwd", grid=(NCHUNK // GLA_STEP,),
        in_specs=[pl.BlockSpec((rows, QK), lambda c: (c, OQ // QK)), pl.BlockSpec((rows, QK), lambda c: (c, OKK // QK)),
                  pl.BlockSpec((rows, D), lambda c: (c, OV // D)), pl.BlockSpec((rows, D), lambda c: (c, OG // D)),
                  pl.BlockSpec((rows, APAD), lambda c: (c, OA // APAD)),
                  pl.BlockSpec((APAD, QK), lambda c: (0, 0)), pl.BlockSpec((1, QK), lambda c: (0, 0)),
                  pl.BlockSpec((1, D), lambda c: (0, 0))] + [ANY] * len(afters),
        out_specs=[pl.BlockSpec((rows, D), row), pl.BlockSpec((rows, D), row),
                   pl.BlockSpec((GLA_STEP, HEADS, DV, DK), lambda c: (c, 0, 0, 0))],
        out_shape=[PINNED((T, D), BF16), PINNED((T, D), BF16), PINNED((NCHUNK, HEADS, DV, DK), BF16)],
        scratch_shapes=[pltpu.VMEM((HEADS, DV, DK), F32)],
        compiler_params=_cparams(32 * 1024 * 1024, ("arbitrary",)),
    )(*_in_hbm(pcat, pcat, pcat, pcat, pcat, wa, ba, ng), *afters)


def gla_bwd(do, pcat, states, wa, ba, dproj, after):
    tail = NCAT - OQ
    assert (OKK, OA) == (OQ + QK, OQ + 2 * QK) and OQ % tail == 0

    def body(do_ref, q_ref, k_ref, v_ref, al_ref, sc_ref, sp_ref, wa_ref, ba_ref, after_ref, held_ref,
             dp_ref, dv_ref, dwa_ref, dba_ref, ds_scr):
        i = pl.program_id(0)
        dp_ref[:, 2 * QK + APAD:] = jnp.zeros((GLA_STEP * CHUNK, tail - 2 * QK - APAD), BF16)

        @pl.when(i == 0)
        def _():
            ds_scr[...] = jnp.zeros_like(ds_scr)

        ds = [ds_scr[h] for h in range(HEADS)]
        dwa, dba = 0.0, 0.0
        for u in reversed(range(GLA_STEP)):
            rs = slice(u * CHUNK, (u + 1) * CHUNK)
            first_chunk = jnp.logical_and(i == NCHUNK // GLA_STEP - 1, u == 0)
            has_prev = jnp.where(first_chunk, 0.0, 1.0).astype(F32)
            a, e, decay = _gate_decay(al_ref[rs, :], wa_ref[...], ba_ref[...])
            kdf = k_ref[rs, :].astype(F32) * e
            kd = kdf.astype(BF16)
            qs = (q_ref[rs, :].astype(F32) * (DK ** -0.5)).astype(BF16)
            dkd_parts, ddecay_parts = [], []
            for h in range(HEADS):
                ck = slice(h * DK, (h + 1) * DK)
                cv = slice(h * DV, (h + 1) * DV)
                doh = do_ref[rs, cv]
                dsh = ds[h] + lax.dot_general(doh, qs[:, ck], (TN, ((), ())), preferred_element_type=F32)
                dsb = dsh.astype(BF16)
                dp_ref[rs, ck] = (jnp.dot(doh, sc_ref[u, h], preferred_element_type=F32) * (DK ** -0.5)).astype(BF16)
                dkd_parts.append(jnp.dot(v_ref[rs, cv], dsb, preferred_element_type=F32))
                dv_ref[rs, cv] = lax.dot_general(kd[:, ck], dsb, (NT, ((), ())), preferred_element_type=F32).astype(BF16)
                s_prev = (sp_ref[h] if u == 0 else sc_ref[u - 1, h]).astype(F32)
                ddecay_parts.append(jnp.sum(dsh * s_prev, axis=0, keepdims=True) * has_prev)
                ds[h] = dsh * decay[:, ck]
            dkd = jnp.concatenate(dkd_parts, axis=1)
            ddecay = jnp.concatenate(ddecay_parts, axis=1)
            dp_ref[rs, QK:2 * QK] = (dkd * e).astype(BF16)
            dearg = dkd * kdf
            dlast = jnp.sum(dearg, axis=0, keepdims=True) + ddecay * decay
            r = lax.broadcasted_iota(jnp.int32, (CHUNK, CHUNK), 0)
            c = lax.broadcasted_iota(jnp.int32, (CHUNK, CHUNK), 1)
            triu = jnp.where(c >= r, 1.0, 0.0).astype(F32)
            dls = dlast - jnp.dot(triu, dearg, preferred_element_type=F32, precision=lax.Precision.HIGHEST)
            da = dls * (1.0 / 16.0) * (1.0 - _sigmoid(a))
            dab = da.astype(BF16)
            dp_ref[rs, 2 * QK:2 * QK + APAD] = lax.dot_general(dab, wa_ref[...], (NT, ((), ())),
                                                               preferred_element_type=F32).astype(BF16)
            dwa = dwa + lax.dot_general(al_ref[rs, :], dab, (TN, ((), ())), preferred_element_type=F32)
            dba = dba + jnp.sum(da, axis=0, keepdims=True)
        for h in range(HEADS):
            ds_scr[h] = ds[h]

        @pl.when(i == 0)
        def _():
            dwa_ref[...] = dwa
            dba_ref[...] = dba

        @pl.when(i > 0)
        def _():
            dwa_ref[...] += dwa
            dba_ref[...] += dba

    rows = GLA_STEP * CHUNK
    rev = lambda i: NCHUNK // GLA_STEP - 1 - i
    return pl.pallas_call(
        body, name="gla_bwd", grid=(NCHUNK // GLA_STEP,),
        in_specs=[pl.BlockSpec((rows, D), lambda i: (rev(i), 0)),
                  pl.BlockSpec((rows, QK), lambda i: (rev(i), OQ // QK)), pl.BlockSpec((rows, QK), lambda i: (rev(i), OKK // QK)),
                  pl.BlockSpec((rows, D), lambda i: (rev(i), OV // D)), pl.BlockSpec((rows, APAD), lambda i: (rev(i), OA // APAD)),
                  pl.BlockSpec((GLA_STEP, HEADS, DV, DK), lambda i: (rev(i), 0, 0, 0)),
                  pl.BlockSpec((None, HEADS, DV, DK), lambda i: (jnp.maximum(rev(i) * GLA_STEP - 1, 0), 0, 0, 0)),
                  pl.BlockSpec((APAD, QK), lambda i: (0, 0)), pl.BlockSpec((1, QK), lambda i: (0, 0)), ANY, ANY],
        out_specs=[pl.BlockSpec((rows, tail), lambda i: (rev(i), OQ // tail)), pl.BlockSpec((rows, D), lambda i: (rev(i), 0)),
                   pl.BlockSpec((APAD, QK), lambda i: (0, 0)), pl.BlockSpec((1, QK), lambda i: (0, 0))],
        out_shape=[PINNED((T, NCAT), BF16), PINNED((T, D), BF16), PINNED((APAD, QK), F32), PINNED((1, QK), F32)],
        scratch_shapes=[pltpu.VMEM((HEADS, DV, DK), F32)],
        input_output_aliases={10: 0},
        compiler_params=_cparams(32 * 1024 * 1024, ("arbitrary",)),
    )(*_in_hbm(do, pcat, pcat, pcat, pcat, states, states, wa, ba), after, dproj)


TMF = 256
TMW = 512
_rowblk = ((TMF, D), lambda j, i, k: (i, 0))
_vec = ((1, D), lambda j, i, k: (0, 0))


def _full_spec(col):
    return ((TMF, D), lambda j, i, k: (i, col))


TBIG = 1024


def square_matmul(name, a, b, *, a_spec, b_spec, cdims, nk, after=None):
    def epi(acc, ex, outs, i):
        outs[0][...] = acc

    return matmul(name, a, b, a_spec=a_spec, b_spec=b_spec, cdims=cdims, grid=(D // TBIG, T // TBIG, nk),
                  acc_shape=(TBIG, TBIG), outs=[((T, D), F32, (TBIG, TBIG), lambda j, i, k: (i, j))], epi=epi,
                  after=after)[0]


def rowwise(name, y, *, extras, outs, epi):
    ne = len(extras)

    def body(*refs):
        epi(refs[0][...], refs[1:1 + ne], refs[1 + ne:], pl.program_id(1))

    in_specs = [pl.BlockSpec(*_rowblk)] + [pl.BlockSpec(bs, im) for _, bs, im in extras]
    return pl.pallas_call(
        body, name=name, grid=(1, T // TMF, 1), in_specs=in_specs,
        out_specs=[pl.BlockSpec(bs, im) for _, _, bs, im in outs], out_shape=[PINNED(s, dt) for s, dt, _, _ in outs],
        compiler_params=_cparams(40 * 1024 * 1024, ("arbitrary", "arbitrary", "arbitrary")),
    )(*_in_hbm(y, *[arr for arr, _, _ in extras]))


def mm_gla_out(og, w, ylin, pcat, pscale):
    def epi(acc, ex, outs, i):
        ylin_ref, lgp_ref, lgg_ref, ps_ref = ex
        for c0 in range(0, D, EPI_COLS):
            cs = slice(c0, c0 + EPI_COLS)
            gp = _sigmoid(lgp_ref[:, cs].astype(F32))
            gg = _sigmoid(lgg_ref[:, cs].astype(F32))
            a = acc[:, cs]
            outs[0][:, cs] = (gp * (ylin_ref[:, cs].astype(F32) * ps_ref[:, cs]) + gg * a).astype(BF16)
            outs[1][:, cs] = a.astype(BF16)

    return matmul("mm_gla_out", og, w, a_spec=_rowblk, b_spec=((D, D), lambda j, i, k: (0, 0)), cdims=NN,
                  grid=(1, T // TMF, 1), acc_shape=(TMF, D),
                  extras=[(ylin, *_rowblk), (pcat, *_full_spec(OGP // D)), (pcat, *_full_spec(OGG // D)), (pscale, *_vec)],
                  outs=[((T, D), BF16, *_rowblk), ((T, D), BF16, *_rowblk)], epi=epi)


def mm_out(mixed, w, x, g2):
    def epi(acc, ex, outs, i):
        x_ref, g_ref = ex
        x2 = x_ref[...] + acc
        r = lax.rsqrt(jnp.mean(x2 * x2, axis=-1, keepdims=True) + EPS)
        outs[0][...] = x2
        outs[1][...] = (x2 * r * g_ref[...]).astype(BF16)

    return matmul("mm_out", mixed, w, a_spec=_rowblk, b_spec=((D, D), lambda j, i, k: (0, 0)), cdims=NN,
                  grid=(1, T // TMF, 1), acc_shape=(TMF, D), extras=[(x, *_rowblk), (g2, *_vec)],
                  outs=[((T, D), F32, *_rowblk), ((T, D), BF16, *_rowblk)], epi=epi)


def mm_up(h2, wup):
    def epi(acc, ex, outs, i):
        r = jnp.maximum(acc, 0.0)
        outs[0][...] = r.astype(BF16)
        outs[1][...] = (r * r).astype(BF16)

    blk = ((TMW, D), lambda j, i, k: (i, j))
    return matmul("mm_up", h2, wup, a_spec=((TMW, D), lambda j, i, k: (i, 0)), b_spec=((None, D, D), lambda j, i, k: (j, 0, 0)),
                  cdims=NN, grid=(NCHIP, T // TMW, 1), acc_shape=(TMW, D),
                  outs=[((T, DFF), BF16, *blk), ((T, DFF), BF16, *blk)], epi=epi)


def mm_down(act, wdown, x2, tgt, gf):
    tk = 4096

    def epi(acc, ex, outs, i):
        x2_ref, t_ref, g_ref = ex
        dx_ref, dxb_ref, gnf_ref, loss_ref = outs
        x3 = x2_ref[...] + acc
        r = lax.rsqrt(jnp.mean(x3 * x3, axis=-1, keepdims=True) + EPS)
        xn = x3 * r
        err = xn * g_ref[...] - t_ref[...]
        lsum = 0.5 * jnp.sum(jnp.mean(err * err, axis=-1, keepdims=True), axis=0, keepdims=True)
        dy = err * (1.0 / D)
        _row_acc(gnf_ref, jnp.sum(dy * xn, axis=0, keepdims=True), i)
        _row_acc(loss_ref, jnp.broadcast_to(lsum, (1, 128)), i)
        dx3 = _rms_bwd(xn, r, dy * g_ref[...])
        dx_ref[...] = dx3
        dxb_ref[...] = dx3.astype(BF16)

    y = square_matmul("mm_down", act, wdown, a_spec=((TBIG, tk), lambda j, i, k: (i, k)),
                      b_spec=((tk, TBIG), lambda j, i, k: (k, j)), cdims=NN, nk=DFF // tk)
    return rowwise("rows_final", y, extras=[(x2, *_rowblk), (tgt, *_rowblk), (gf, *_vec)],
                   outs=[((T, D), F32, *_rowblk), ((T, D), BF16, *_rowblk), ((1, D), F32, *_vec),
                         ((1, 128), F32, (1, 128), lambda j, i, k: (0, 0))], epi=epi)


def mm_dact(dx3b, wdown, rup, after=None):
    def epi(acc, ex, outs, i):
        outs[0][...] = (acc * 2.0 * ex[0][...].astype(F32)).astype(BF16)

    blk = ((TMW, D), lambda j, i, k: (i, j))
    return matmul("mm_dact", dx3b, wdown, a_spec=((TMW, D), lambda j, i, k: (i, 0)), b_spec=((D, D), lambda j, i, k: (j, 0)),
                  cdims=NT, grid=(DFF // D, T // TMW, 1), acc_shape=(TMW, D), extras=[(rup, *blk)],
                  outs=[((T, DFF), BF16, *blk)], epi=epi, after=after)[0]


def mm_wgrad(name, a, b, m, n, out_shape, out_block, out_map, tm, tn, after=None):
    def epi(acc, ex, outs, i):
        outs[0][...] = acc.astype(BF16).reshape(outs[0].shape)

    return matmul(name, a, b, a_spec=((T, tm), lambda j, i, k: (0, i)), b_spec=((T, tn), lambda j, i, k: (0, j)),
                  cdims=TN, grid=(n // tn, m // tm, 1), acc_shape=(tm, tn),
                  outs=[(out_shape, BF16, out_block, out_map)], epi=epi, after=after)[0]


def mm_dh2(dup, wup, x2, dx3, g2, after=None):
    def epi(acc, ex, outs, i):
        x2_ref, dx3_ref, g_ref = ex
        x2 = x2_ref[...]
        r = lax.rsqrt(jnp.mean(x2 * x2, axis=-1, keepdims=True) + EPS)
        xn = x2 * r
        _row_acc(outs[2], jnp.sum(acc * xn, axis=0, keepdims=True), i)
        dx2 = dx3_ref[...] + _rms_bwd(xn, r, acc * g_ref[...])
        outs[0][...] = dx2
        outs[1][...] = dx2.astype(BF16)

    y = square_matmul("mm_dh2", dup, wup, a_spec=((TBIG, D), lambda j, i, k: (i, k)),
                      b_spec=((None, TBIG, D), lambda j, i, k: (k, j, 0)), cdims=NT, nk=NCHIP, after=after)
    return rowwise("rows_dh2", y, extras=[(x2, *_rowblk), (dx3, *_rowblk), (g2, *_vec)],
                   outs=[((T, D), F32, *_rowblk), ((T, D), BF16, *_rowblk), ((1, D), F32, *_vec)], epi=epi)


def mm_dmixed(dx2b, wout, pcat, ylin, ygla, pscale, after=None):
    assert OGG == OGP + D and OGP % (2 * D) == 0

    def epi(acc, ex, outs, i):
        lgp_ref, lgg_ref, ylin_ref, ygla_ref, ps_ref = ex
        dps = []
        for c0 in range(0, D, EPI_COLS):
            cs = slice(c0, c0 + EPI_COLS)
            gp = _sigmoid(lgp_ref[:, cs].astype(F32))
            gg = _sigmoid(lgg_ref[:, cs].astype(F32))
            yl = ylin_ref[:, cs].astype(F32)
            ps = ps_ref[:, cs]
            a = acc[:, cs]
            agp = a * gp
            outs[0][:, cs] = (agp * ps).astype(BF16)
            outs[1][:, cs] = (a * gg).astype(BF16)
            outs[2][:, cs] = (agp * (yl * ps) * (1.0 - gp)).astype(BF16)
            outs[2][:, D + c0:D + c0 + EPI_COLS] = (a * ygla_ref[:, cs].astype(F32) * gg * (1.0 - gg)).astype(BF16)
            dps.append(jnp.sum(agp * yl, axis=0, keepdims=True))
        _row_acc(outs[3], jnp.concatenate(dps, axis=1), i)

    return matmul("mm_dmixed", dx2b, wout, a_spec=_rowblk, b_spec=((D, D), lambda j, i, k: (0, 0)), cdims=NT,
                  grid=(1, T // TMF, 1), acc_shape=(TMF, D),
                  extras=[(pcat, *_full_spec(OGP // D)), (pcat, *_full_spec(OGG // D)), (ylin, *_rowblk), (ygla, *_rowblk),
                          (pscale, *_vec)],
                  outs=[((T, D), BF16, *_rowblk)] * 2
                       + [((T, NCAT), BF16, (TMF, 2 * D), lambda j, i, k: (i, OGP // (2 * D))), ((1, D), F32, *_vec)],
                  epi=epi, after=after)


def mm_dog(dygla, wgo, o, pcat, ng, dproj, after=None):
    def epi(acc, ex, outs, i):
        o_ref, g_ref, ng_ref = ex
        do_ref, dg_ref, gng_ref = outs
        gparts = []
        for h in range(HEADS):
            cv = slice(h * DV, (h + 1) * DV)
            oh = o_ref[:, cv].astype(F32)
            r = lax.rsqrt(jnp.mean(oh * oh, axis=-1, keepdims=True) + EPS)
            on = oh * r
            gv = g_ref[:, cv].astype(F32)
            sg = _sigmoid(gv)
            a = acc[:, cv]
            dgain = a * (gv * sg)
            gparts.append(jnp.sum(dgain * on, axis=0, keepdims=True))
            ngh = ng_ref[:, cv]
            do_ref[:, cv] = _rms_bwd(on, r, dgain * ngh).astype(BF16)
            dg_ref[:, cv] = (a * (on * ngh) * (sg * (1.0 + gv * (1.0 - sg)))).astype(BF16)
        _row_acc(gng_ref, jnp.concatenate(gparts, axis=1), i)

    return matmul("mm_dog", dygla, wgo, a_spec=_rowblk, b_spec=((D, D), lambda j, i, k: (0, 0)), cdims=NT,
                  grid=(1, T // TMF, 1), acc_shape=(TMF, D),
                  extras=[(o, *_rowblk), (pcat, *_full_spec(OG // D)), (ng, *_vec)],
                  outs=[((T, D), BF16, *_rowblk), ((T, NCAT), BF16, *_full_spec(OG // D)), ((1, D), F32, *_vec)],
                  epi=epi, after=after, into=(dproj, 1))


def mm_dh1(dpcat, wcat, x, dx2, g1, after=None):
    tk = 3840

    def epi(acc, ex, outs, i):
        x_ref, dx2_ref, g_ref = ex
        xv = x_ref[...]
        r = lax.rsqrt(jnp.mean(xv * xv, axis=-1, keepdims=True) + EPS)
        xn = xv * r
        _row_acc(outs[1], jnp.sum(acc * xn, axis=0, keepdims=True), i)
        outs[0][...] = dx2_ref[...] + _rms_bwd(xn, r, acc * g_ref[...])

    y = square_matmul("mm_dh1", dpcat, wcat, a_spec=((TBIG, tk), lambda j, i, k: (i, k)),
                      b_spec=((TBIG, tk), lambda j, i, k: (j, k)), cdims=NT, nk=NCAT // tk, after=after)
    return rowwise("rows_dh1", y, extras=[(x, *_rowblk), (dx2, *_rowblk), (g1, *_vec)],
                   outs=[((T, D), F32, *_rowblk), ((1, D), F32, *_vec)], epi=epi)


def _tile_rows(rows, cols, n_arrays):
    tm = rows
    while tm % 32 == 0 and 2 * n_arrays * tm * cols * 4 > 36 * 1024 * 1024:
        tm //= 2
    return tm


def add_pairs(name, parts, theirs, core):
    _, _, r, c = parts.shape
    tm = _tile_rows(r, c, 3)

    def body(core_ref, a_ref, b_ref, o_ref):
        o_ref[...] = (a_ref[...].astype(F32) + b_ref[...].astype(F32)).astype(BF16)

    spec = pl.BlockSpec((None, tm, c), lambda j, i, core_ref: (j, i, 0))
    grid_spec = pltpu.PrefetchScalarGridSpec(
        num_scalar_prefetch=1, grid=(NCHIP, r // tm),
        in_specs=[pl.BlockSpec((None, None, tm, c), lambda j, i, core_ref: (core_ref[0], j, i, 0)), spec], out_specs=spec)
    return pl.pallas_call(body, name=name, grid_spec=grid_spec, out_shape=PINNED((NCHIP, r, c), BF16),
                          compiler_params=_cparams(40 * 1024 * 1024, ("arbitrary", "arbitrary")))(core, *_in_hbm(parts, theirs))


def sum_chips(name, sums, landed, chip):
    _, r, c = sums.shape
    tm = _tile_rows(r, c, 4)

    def body(chip_ref, own_ref, l_ref, o_ref):
        s = own_ref[...].astype(F32)
        for t in range(NCHIP - 1):
            s = s + l_ref[t].astype(F32)
        o_ref[...] = s

    grid_spec = pltpu.PrefetchScalarGridSpec(
        num_scalar_prefetch=1, grid=(r // tm,),
        in_specs=[pl.BlockSpec((None, tm, c), lambda i, chip_ref: (chip_ref[0], i, 0)),
                  pl.BlockSpec((NCHIP - 1, tm, c), lambda i, chip_ref: (0, i, 0))],
        out_specs=pl.BlockSpec((tm, c), lambda i, chip_ref: (i, 0)))
    return pl.pallas_call(body, name=name, grid_spec=grid_spec, out_shape=PINNED((r, c), F32),
                          compiler_params=_cparams(40 * 1024 * 1024, ("arbitrary",)))(chip, *_in_hbm(sums, landed))


def _adamw_math(wv, gv, mv, vv):
    mn = ADAM_B1 * mv + (1.0 - ADAM_B1) * gv
    vn = ADAM_B2 * vv + (1.0 - ADAM_B2) * (gv * gv)
    mh = mn / (1.0 - ADAM_B1 ** ADAM_STEP)
    vh = vn / (1.0 - ADAM_B2 ** ADAM_STEP)
    return -ADAM_LR * (mh / (jnp.sqrt(vh) + ADAM_EPS) + ADAM_WD * wv), mn, vn


def adamw(name, w, g, m, v):
    def body(w_ref, g_ref, m_ref, v_ref, go_ref, d_ref, mo_ref, vo_ref):
        gv = g_ref[...]
        go_ref[...] = gv
        d_ref[...], mo_ref[...], vo_ref[...] = _adamw_math(w_ref[...], gv, m_ref[...], v_ref[...])

    return pl.pallas_call(body, name=name, out_shape=[SDS(w.shape, F32)] * 4)(w, g, m, v)


def adamw_halves(name, w, g_own, g_sib, m, v, core):
    _, r, c = w.shape
    tm = _tile_rows(r, c, 10)

    def body(core_ref, w_ref, go_ref, gs_ref, m_ref, v_ref, g_out, d_out, m_out, v_out):
        gv = jnp.where(pl.program_id(0) == core_ref[0], go_ref[...], gs_ref[...])
        g_out[...] = gv
        d_out[...], m_out[...], v_out[...] = _adamw_math(w_ref[...], gv, m_ref[...], v_ref[...])

    full = pl.BlockSpec((None, tm, c), lambda h, i, core_ref: (h, i, 0))
    own = pl.BlockSpec((tm, c), lambda h, i, core_ref: (jnp.where(h == core_ref[0], i, 0), 0))
    sib = pl.BlockSpec((tm, c), lambda h, i, core_ref: (jnp.where(h == core_ref[0], 0, i), 0))
    grid_spec = pltpu.PrefetchScalarGridSpec(num_scalar_prefetch=1, grid=(2, r // tm),
                                             in_specs=[full, own, sib, full, full], out_specs=[full] * 4)
    return pl.pallas_call(body, name=name, grid_spec=grid_spec, out_shape=[SDS(w.shape, F32)] * 4,
                          compiler_params=_cparams(48 * 1024 * 1024, ("arbitrary", "arbitrary")))(core, *_in_hbm(w, g_own, g_sib, m, v))


def cast_bf16(name, w):
    _, r, c = w.shape
    tm = _tile_rows(r, c, 2)

    def body(w_ref, o_ref):
        o_ref[...] = w_ref[...].astype(BF16)

    spec = pl.BlockSpec((None, tm, c), lambda h, i: (h, i, 0))
    return pl.pallas_call(body, name=name, grid=(2, r // tm), in_specs=[spec], out_specs=spec, out_shape=PINNED(w.shape, BF16),
                          compiler_params=_cparams(40 * 1024 * 1024, ("arbitrary", "arbitrary")))(w)


def pack_rows(name, parts, rows, after=None):
    width = parts[0].shape[1]
    n = len(parts)
    afters = _as_list(after)

    def body(*refs):
        out_ref = refs[n + len(afters)]
        out_ref[...] = jnp.zeros_like(out_ref)
        off = 0
        for p in refs[:n]:
            out_ref[off:off + p.shape[0], :] = p[...]
            off += p.shape[0]

    vm = pl.BlockSpec(memory_space=pltpu.VMEM)
    return pl.pallas_call(body, name=name, in_specs=[vm] * n + [ANY] * len(afters), out_specs=vm,
                          out_shape=SDS((rows, width), F32))(*parts, *afters)


def _place():
    x, y, c = lax.axis_index("x"), lax.axis_index("y"), lax.axis_index("c")
    chips = [(1 - x, y), (x, 1 - y), (1 - x, 1 - y)]
    return x, y, c, chips


def _row_split(shape, dtype):
    r, c = shape
    n = 1
    while r % (2 * n) == 0 and (r // (2 * n)) % 16 == 0 and (r // n) * c * jnp.dtype(dtype).itemsize > PIECE_BYTES:
        n *= 2
    return [pl.ds(s * (r // n), r // n) for s in range(n)]


def _pieces(ref):
    *lead, r, c = ref.shape
    split = _row_split((r, c), ref.dtype)
    return [ref.at[(*idx, s)] for idx in itertools.product(*[range(d) for d in lead]) for s in split]


HBM = pl.BlockSpec(memory_space=pltpu.HBM)
SEM = pl.BlockSpec(memory_space=pltpu.SEMAPHORE)
EFFECT = pltpu.SideEffectType.DATAFLOW_SIDE_EFFECTING


def gather_start(name, shards, after=None):
    n = len(shards)
    afters = _as_list(after)

    def body(*refs):
        src, land = refs[:n], refs[n:2 * n]
        send, recv = refs[2 * n + len(afters)], refs[2 * n + len(afters) + 1]
        x, y, c, chips = _place()
        me = 2 * x + y
        for a in range(n):
            for j, (cx, cy) in enumerate(chips[:2]):
                for sp, dp in zip(_pieces(src[a].at[c]), _pieces(land[a].at[me, c])):
                    pltpu.make_async_remote_copy(sp, dp, send.at[2 * a + j], recv.at[2 * a + j],
                                                 device_id=(cx, cy, c), device_id_type=MESH).start()

    lands = [pltpu.with_memory_space_constraint(lax.empty((NCHIP,) + s.shape, s.dtype), pltpu.HBM) for s in shards]
    srcs = [pltpu.with_memory_space_constraint(s, pltpu.HBM) for s in shards]
    outs = pl.pallas_call(
        body, name=name,
        out_shape=(pltpu.SemaphoreType.DMA((2 * n,)), pltpu.SemaphoreType.DMA((2 * n,)),
                   *[pltpu.HBM(s.shape, s.dtype) for s in shards], *[pltpu.HBM(l.shape, l.dtype) for l in lands]),
        in_specs=[HBM] * (2 * n) + [ANY] * len(afters), out_specs=(SEM, SEM, *([HBM] * (2 * n))),
        input_output_aliases={i: 2 + i for i in range(2 * n)},
        compiler_params=pltpu.CompilerParams(has_side_effects=EFFECT),
    )(*srcs, *lands, *afters)
    return outs[0], outs[1], list(outs[2:2 + n]), list(outs[2 + n:2 + 2 * n])


def _relay_blocks(land, c, chips):
    (xx, xy), (yx, yy), (dx, dy) = chips
    rows = land.shape[2] // 2
    upper, lower = pl.ds(0, rows), pl.ds(rows, rows)
    return [(land.at[2 * yx + yy, c, lower], land.at[2 * dx + dy, c, lower]),
            (land.at[2 * xx + xy, c, upper], land.at[2 * dx + dy, c, upper])]


def relay_turn(name, send, recv, shards, lands, after):
    n = len(shards)
    afters = _as_list(after)

    def body(*refs):
        src, had = refs[:n], refs[n:2 * n]
        send_ref, recv_ref = refs[2 * n], refs[2 * n + 1]
        rsend, rrecv = refs[2 * n + 2 + len(afters)], refs[2 * n + 3 + len(afters)]
        land = refs[3 * n + 4 + len(afters):4 * n + 4 + len(afters)]
        x, y, c, chips = _place()
        for a in range(n):
            for j, (cx, cy) in enumerate(chips[:2]):
                cp = pltpu.make_async_remote_copy(src[a].at[c], had[a].at[2 * cx + cy, c], send_ref.at[2 * a + j],
                                                  recv_ref.at[2 * a + j], device_id=(cx, cy, c), device_id_type=MESH)
                cp.wait_send()
                cp.wait_recv()
        for a in range(n):
            for j, ((sent, _), (dst, _)) in enumerate(zip(_relay_blocks(had[a], c, chips), _relay_blocks(land[a], c, chips))):
                cx, cy = chips[j]
                for sp, dp in zip(_pieces(sent), _pieces(dst)):
                    pltpu.make_async_remote_copy(sp, dp, rsend.at[2 * a + j], rrecv.at[2 * a + j],
                                                 device_id=(cx, cy, c), device_id_type=MESH).start()

    outs = pl.pallas_call(
        body, name=name,
        out_shape=(pltpu.SemaphoreType.DMA((2 * n,)), pltpu.SemaphoreType.DMA((2 * n,)),
                   *[pltpu.HBM(s.shape, s.dtype) for s in shards], *[pltpu.HBM(l.shape, l.dtype) for l in lands]),
        in_specs=[HBM] * (2 * n) + [SEM, SEM] + [ANY] * len(afters), out_specs=(SEM, SEM, *([HBM] * (2 * n))),
        input_output_aliases={i: 2 + i for i in range(2 * n)},
        compiler_params=pltpu.CompilerParams(has_side_effects=EFFECT),
    )(*shards, *lands, send, recv, *afters)
    return outs[0], outs[1], list(outs[2:2 + n]), list(outs[2 + n:2 + 2 * n])


def relay_wait(name, send, recv, lands, after):
    n = len(lands)
    afters = _as_list(after)

    def body(*refs):
        land = refs[:n]
        send_ref, recv_ref = refs[n], refs[n + 1]
        x, y, c, chips = _place()
        for a in range(n):
            for j, (sent, got) in enumerate(_relay_blocks(land[a], c, chips)):
                cx, cy = chips[j]
                cp = pltpu.make_async_remote_copy(sent, got, send_ref.at[2 * a + j], recv_ref.at[2 * a + j],
                                                  device_id=(cx, cy, c), device_id_type=MESH)
                cp.wait_send()
                cp.wait_recv()

    outs = pl.pallas_call(
        body, name=name, out_shape=tuple(pltpu.HBM(l.shape, l.dtype) for l in lands),
        in_specs=[HBM] * n + [SEM, SEM] + [ANY] * len(afters), out_specs=[HBM] * n,
        input_output_aliases={i: i for i in range(n)},
        compiler_params=pltpu.CompilerParams(has_side_effects=EFFECT),
    )(*lands, send, recv, *afters)
    return list(outs)


def forward_halves(name, shards, lands):
    n = len(lands)

    def body(*refs):
        had, buf = refs[:n], refs[n:2 * n]
        send, recv = refs[2 * n:]
        x, y, c, chips = _place()
        sib = (x, y, 1 - c)
        for a in range(n):
            for j, (cx, cy) in enumerate(chips):
                for sp, dp in zip(_pieces(had[a].at[2 * cx + cy, c]), _pieces(buf[a].at[2 * cx + cy, c])):
                    pltpu.make_async_remote_copy(sp, dp, send.at[3 * a + j], recv.at[3 * a + j], device_id=sib, device_id_type=MESH).start()
        for a in range(n):
            for j, (cx, cy) in enumerate(chips):
                pltpu.make_async_remote_copy(had[a].at[2 * cx + cy, c], buf[a].at[2 * cx + cy, 1 - c], send.at[3 * a + j],
                                             recv.at[3 * a + j], device_id=sib, device_id_type=MESH).wait()

    got = pl.pallas_call(
        body, name=name, in_specs=[ANY] * n, out_specs=[ANY] * n, out_shape=[SDS(l.shape, l.dtype) for l in lands],
        input_output_aliases={i: i for i in range(n)},
        scratch_shapes=[pltpu.SemaphoreType.DMA((3 * n,)), pltpu.SemaphoreType.DMA((3 * n,))],
    )(*lands)
    me = 2 * lax.axis_index("x") + lax.axis_index("y")
    return [lax.dynamic_update_index_in_dim(g, s, me, 0) for g, s in zip(got, shards)]


def forward_turn(name, send, recv, lands, after):
    n = len(lands)
    afters = _as_list(after)

    def body(*refs):
        had = refs[:n]
        send_ref, recv_ref = refs[n], refs[n + 1]
        fsend, frecv = refs[n + 2 + len(afters)], refs[n + 3 + len(afters)]
        buf = refs[n + 4 + len(afters):2 * n + 4 + len(afters)]
        x, y, c, chips = _place()
        sib = (x, y, 1 - c)
        for a in range(n):
            for j, (sent, got) in enumerate(_relay_blocks(had[a], c, chips)):
                cx, cy = chips[j]
                cp = pltpu.make_async_remote_copy(sent, got, send_ref.at[2 * a + j], recv_ref.at[2 * a + j],
                                                  device_id=(cx, cy, c), device_id_type=MESH)
                cp.wait_send()
                cp.wait_recv()
        for a in range(n):
            for j, (cx, cy) in enumerate(chips):
                for sp, dp in zip(_pieces(had[a].at[2 * cx + cy, c]), _pieces(buf[a].at[2 * cx + cy, c])):
                    pltpu.make_async_remote_copy(sp, dp, fsend.at[3 * a + j], frecv.at[3 * a + j], device_id=sib, device_id_type=MESH).start()

    outs = pl.pallas_call(
        body, name=name,
        out_shape=(pltpu.SemaphoreType.DMA((3 * n,)), pltpu.SemaphoreType.DMA((3 * n,)), *[pltpu.HBM(l.shape, l.dtype) for l in lands]),
        in_specs=[HBM] * n + [SEM, SEM] + [ANY] * len(afters), out_specs=(SEM, SEM, *([HBM] * n)),
        input_output_aliases={i: 2 + i for i in range(n)},
        compiler_params=pltpu.CompilerParams(has_side_effects=EFFECT),
    )(*lands, send, recv, *afters)
    return outs[0], outs[1], list(outs[2:])


def forward_wait(name, send, recv, lands, after):
    n = len(lands)
    afters = _as_list(after)

    def body(*refs):
        land = refs[:n]
        send_ref, recv_ref = refs[n], refs[n + 1]
        x, y, c, chips = _place()
        sib = (x, y, 1 - c)
        for a in range(n):
            for j, (cx, cy) in enumerate(chips):
                cp = pltpu.make_async_remote_copy(land[a].at[2 * cx + cy, c], land[a].at[2 * cx + cy, 1 - c], send_ref.at[3 * a + j],
                                                  recv_ref.at[3 * a + j], device_id=sib, device_id_type=MESH)
                cp.wait_send()
                cp.wait_recv()

    outs = pl.pallas_call(
        body, name=name, out_shape=tuple(pltpu.HBM(l.shape, l.dtype) for l in lands),
        in_specs=[HBM] * n + [SEM, SEM] + [ANY] * len(afters), out_specs=[HBM] * n,
        input_output_aliases={i: i for i in range(n)},
        compiler_params=pltpu.CompilerParams(has_side_effects=EFFECT),
    )(*lands, send, recv, *afters)
    return list(outs)


def exchange_start(name, parts):
    n = len(parts)

    def body(*refs):
        src, got = refs[:n], refs[n:2 * n]
        send, recv = refs[2 * n], refs[2 * n + 1]
        token = refs[4 * n + 2]
        x, y, c, _ = _place()
        sib = (x, y, 1 - c)
        for a in range(n):
            for sp, dp in zip(_pieces(src[a].at[1 - c]), _pieces(got[a])):
                pltpu.make_async_remote_copy(sp, dp, send.at[a], recv.at[a], device_id=sib, device_id_type=MESH).start()
        token[...] = jnp.zeros_like(token)

    lands = [pltpu.with_memory_space_constraint(lax.empty(p.shape[1:], p.dtype), pltpu.HBM) for p in parts]
    srcs = [pltpu.with_memory_space_constraint(p, pltpu.HBM) for p in parts]
    outs = pl.pallas_call(
        body, name=name,
        out_shape=(pltpu.SemaphoreType.DMA((n,)), pltpu.SemaphoreType.DMA((n,)),
                   *[pltpu.HBM(p.shape, p.dtype) for p in parts], *[pltpu.HBM(l.shape, l.dtype) for l in lands],
                   SDS((8, 128), F32)),
        in_specs=[HBM] * (2 * n), out_specs=(SEM, SEM, *([HBM] * (2 * n)), pl.BlockSpec(memory_space=pltpu.VMEM)),
        input_output_aliases={i: 2 + i for i in range(2 * n)},
        compiler_params=pltpu.CompilerParams(has_side_effects=EFFECT),
    )(*srcs, *lands)
    return outs[0], outs[1], list(outs[2:2 + n]), list(outs[2 + n:2 + 2 * n]), outs[2 + 2 * n]


def exchange_wait(name, send, recv, parts, lands, after):
    n = len(parts)
    afters = _as_list(after)

    def body(*refs):
        src, got = refs[:n], refs[n:2 * n]
        send_ref, recv_ref = refs[2 * n], refs[2 * n + 1]
        x, y, c, _ = _place()
        sib = (x, y, 1 - c)
        for a in range(n):
            cp = pltpu.make_async_remote_copy(src[a].at[1 - c], got[a], send_ref.at[a], recv_ref.at[a], device_id=sib, device_id_type=MESH)
            cp.wait_send()
            cp.wait_recv()

    outs = pl.pallas_call(
        body, name=name,
        out_shape=(*[pltpu.HBM(p.shape, p.dtype) for p in parts], *[pltpu.HBM(l.shape, l.dtype) for l in lands]),
        in_specs=[HBM] * (2 * n) + [SEM, SEM] + [ANY] * len(afters), out_specs=[HBM] * (2 * n),
        input_output_aliases={i: i for i in range(2 * n)},
        compiler_params=pltpu.CompilerParams(has_side_effects=EFFECT),
    )(*parts, *lands, send, recv, *afters)
    return list(outs[:n]), list(outs[n:])


def scatter_start(name, parts):
    n = len(parts)

    def body(*refs):
        src, land = refs[:n], refs[n:2 * n]
        send, recv = refs[2 * n], refs[2 * n + 1]
        token = refs[4 * n + 2]
        x, y, c, chips = _place()
        for a in range(n):
            for j, (cx, cy) in enumerate(chips):
                for sp, dp in zip(_pieces(src[a].at[2 * cx + cy]), _pieces(land[a].at[j])):
                    pltpu.make_async_remote_copy(sp, dp, send.at[3 * a + j], recv.at[3 * a + j],
                                                 device_id=(cx, cy, c), device_id_type=MESH).start()
        token[...] = jnp.zeros_like(token)

    lands = [pltpu.with_memory_space_constraint(lax.empty((NCHIP - 1,) + p.shape[1:], p.dtype), pltpu.HBM) for p in parts]
    srcs = [pltpu.with_memory_space_constraint(p, pltpu.HBM) for p in parts]
    outs = pl.pallas_call(
        body, name=name,
        out_shape=(pltpu.SemaphoreType.DMA((3 * n,)), pltpu.SemaphoreType.DMA((3 * n,)),
                   *[pltpu.HBM(p.shape, p.dtype) for p in parts], *[pltpu.HBM(l.shape, l.dtype) for l in lands],
                   SDS((8, 128), F32)),
        in_specs=[HBM] * (2 * n), out_specs=(SEM, SEM, *([HBM] * (2 * n)), pl.BlockSpec(memory_space=pltpu.VMEM)),
        input_output_aliases={i: 2 + i for i in range(2 * n)},
        compiler_params=pltpu.CompilerParams(has_side_effects=EFFECT),
    )(*srcs, *lands)
    return outs[0], outs[1], list(outs[2:2 + n]), list(outs[2 + n:2 + 2 * n]), outs[2 + 2 * n]


def scatter_wait(name, send, recv, parts, lands, after):
    n = len(parts)
    afters = _as_list(after)

    def body(*refs):
        src, land = refs[:n], refs[n:2 * n]
        send_ref, recv_ref = refs[2 * n], refs[2 * n + 1]
        x, y, c, chips = _place()
        for a in range(n):
            for j, (cx, cy) in enumerate(chips):
                cp = pltpu.make_async_remote_copy(src[a].at[2 * cx + cy], land[a].at[j], send_ref.at[3 * a + j], recv_ref.at[3 * a + j],
                                                  device_id=(cx, cy, c), device_id_type=MESH)
                cp.wait_send()
                cp.wait_recv()

    outs = pl.pallas_call(
        body, name=name,
        out_shape=(*[pltpu.HBM(p.shape, p.dtype) for p in parts], *[pltpu.HBM(l.shape, l.dtype) for l in lands]),
        in_specs=[HBM] * (2 * n) + [SEM, SEM] + [ANY] * len(afters), out_specs=[HBM] * (2 * n),
        input_output_aliases={i: i for i in range(2 * n)},
        compiler_params=pltpu.CompilerParams(has_side_effects=EFFECT),
    )(*parts, *lands, send, recv, *afters)
    return list(outs[:n]), list(outs[n:])


def join_start(name, halves):
    n = len(halves)

    def body(*refs):
        src, dst = refs[:n], refs[n:2 * n]
        send, recv = refs[2 * n], refs[2 * n + 1]
        token = refs[4 * n + 2]
        x, y, c, _ = _place()
        sib = (x, y, 1 - c)
        for a in range(n):
            for sp, dp in zip(_pieces(src[a]), _pieces(dst[a])):
                pltpu.make_async_remote_copy(sp, dp, send.at[a], recv.at[a], device_id=sib, device_id_type=MESH).start()
        token[...] = jnp.zeros_like(token)

    lands = [pltpu.with_memory_space_constraint(lax.empty(h.shape, h.dtype), pltpu.HBM) for h in halves]
    srcs = [pltpu.with_memory_space_constraint(h, pltpu.HBM) for h in halves]
    outs = pl.pallas_call(
        body, name=name,
        out_shape=(pltpu.SemaphoreType.DMA((n,)), pltpu.SemaphoreType.DMA((n,)),
                   *[pltpu.HBM(h.shape, h.dtype) for h in halves], *[pltpu.HBM(l.shape, l.dtype) for l in lands],
                   SDS((8, 128), F32)),
        in_specs=[HBM] * (2 * n), out_specs=(SEM, SEM, *([HBM] * (2 * n)), pl.BlockSpec(memory_space=pltpu.VMEM)),
        input_output_aliases={i: 2 + i for i in range(2 * n)},
        compiler_params=pltpu.CompilerParams(has_side_effects=EFFECT),
    )(*srcs, *lands)
    return outs[0], outs[1], list(outs[2:2 + n]), list(outs[2 + n:2 + 2 * n]), outs[2 + 2 * n]


def join_wait(name, send, recv, halves, lands, after):
    n = len(halves)
    afters = _as_list(after)

    def body(*refs):
        src, dst = refs[:n], refs[n:2 * n]
        send_ref, recv_ref = refs[2 * n], refs[2 * n + 1]
        x, y, c, _ = _place()
        sib = (x, y, 1 - c)
        for a in range(n):
            cp = pltpu.make_async_remote_copy(src[a], dst[a], send_ref.at[a], recv_ref.at[a], device_id=sib, device_id_type=MESH)
            cp.wait_send()
            cp.wait_recv()

    outs = pl.pallas_call(
        body, name=name,
        out_shape=(*[pltpu.HBM(h.shape, h.dtype) for h in halves], *[pltpu.HBM(l.shape, l.dtype) for l in lands]),
        in_specs=[HBM] * (2 * n) + [SEM, SEM] + [ANY] * len(afters), out_specs=[HBM] * (2 * n),
        input_output_aliases={i: i for i in range(2 * n)},
        compiler_params=pltpu.CompilerParams(has_side_effects=EFFECT),
    )(*halves, *lands, send, recv, *afters)
    return list(outs[:n]), list(outs[n:])


def gather_small(name, xs, reduce, after=None):
    m, ncol = xs.shape
    afters = _as_list(after)

    def body(x_ref, *rest):
        out_ref, all_ref, send, recv, lsem = rest[len(afters):]
        x, y, c, chips = _place()
        me, sib = (x, y, c), (x, y, 1 - c)

        def rows(px, py, pc):
            return all_ref.at[pl.ds((4 * px + 2 * py + pc) * m, m), :]

        def copy(k, block, to, src=None):
            return pltpu.make_async_remote_copy(rows(*block) if src is None else src, rows(*block), send.at[k], recv.at[k],
                                                device_id=to, device_id_type=MESH)

        mine = pltpu.make_async_copy(x_ref, rows(*me), lsem)
        mine.start()
        first = [copy(0, me, sib, src=x_ref)] + [copy(1 + j, me, (*chip, c), src=x_ref) for j, chip in enumerate(chips)]
        for cp in first:
            cp.start()
        passed = [copy(4 + j, (*chip, c), sib) for j, chip in enumerate(chips)]
        for j, chip in enumerate(chips):
            copy(1 + j, (*chip, c), me).wait_recv()
            passed[j].start()
        copy(0, sib, me).wait_recv()
        for j, chip in enumerate(chips):
            copy(4 + j, (*chip, 1 - c), me).wait_recv()
        for cp in first + passed:
            cp.wait_send()
        mine.wait()
        if reduce:
            s = all_ref[0:m, :]
            for dev in range(1, 8):
                s = s + all_ref[dev * m:(dev + 1) * m, :]
            out_ref[...] = s
        else:
            out_ref[...] = all_ref[...]

    vm = pl.BlockSpec(memory_space=pltpu.VMEM)
    return pl.pallas_call(
        body, name=name, in_specs=[vm] + [ANY] * len(afters), out_specs=vm,
        out_shape=SDS((m, ncol) if reduce else (8 * m, ncol), F32),
        scratch_shapes=[pltpu.VMEM((8 * m, ncol), F32), pltpu.SemaphoreType.DMA((7,)), pltpu.SemaphoreType.DMA((7,)),
                        pltpu.SemaphoreType.DMA],
    )(xs, *afters)


RELAYOUT_ROWS = 128


def weights_to_cat(name, land, own, place, other, prev=None, after=None):
    tm = RELAYOUT_ROWS
    nb = (D // 2) // tm
    extra = ([] if prev is None else [prev]) + _as_list(after)

    def half(p):
        return 1 - p[0] if other else p[0]

    def body(p_ref, g_ref, own_ref, *rest):
        o_ref = rest[len(extra)]
        nat = jnp.concatenate([jnp.where(p_ref[1] == j, own_ref[...], g_ref[j]) for j in range(NCHIP)], axis=1)
        pad = jnp.zeros((tm, NCAT - OA - 16), BF16)
        o_ref[...] = jnp.concatenate([nat[:, 3072:7168], nat[:, 7184:11280], nat[:, 0:3072], nat[:, 7168:7184], pad], axis=1)

    grid_spec = pltpu.PrefetchScalarGridSpec(
        num_scalar_prefetch=1, grid=(nb,),
        in_specs=[pl.BlockSpec((NCHIP, None, tm, IN_SHARD), lambda i, p: (0, half(p), i, 0)),
                  pl.BlockSpec((None, tm, IN_SHARD), lambda i, p: (half(p), i, 0))] + [ANY] * len(extra),
        out_specs=pl.BlockSpec((tm, NCAT), lambda i, p: (half(p) * nb + i, 0)))
    return pl.pallas_call(
        body, name=name, grid_spec=grid_spec, out_shape=PINNED((D, NCAT), BF16),
        input_output_aliases={} if prev is None else {3: 0},
        compiler_params=_cparams(40 * 1024 * 1024, ("arbitrary",)),
    )(place, land, own, *extra)


def grads_from_cat(gw_cat):
    tm = RELAYOUT_ROWS
    nb = (D // 2) // tm

    def body(c_ref, o_ref):
        cat = c_ref[...]
        nat = jnp.concatenate([cat[:, OU:OA], cat[:, OV:OGP], cat[:, OA:OA + 16], cat[:, OGP:OU]], axis=1)
        for j in range(NCHIP):
            o_ref[j] = nat[:, j * IN_SHARD:(j + 1) * IN_SHARD]

    return pl.pallas_call(
        body, name="grads_from_cat", grid=(D // tm,), in_specs=[pl.BlockSpec((tm, NCAT), lambda i: (i, 0))],
        out_specs=pl.BlockSpec((None, NCHIP, tm, IN_SHARD), lambda i: (i // nb, 0, i % nb, 0)),
        out_shape=PINNED((2, NCHIP, D // 2, IN_SHARD), BF16), compiler_params=_cparams(40 * 1024 * 1024, ("arbitrary",)),
    )(gw_cat)


def _pad_rows(a, rows):
    return jnp.concatenate([a, jnp.zeros((rows - a.shape[0],) + a.shape[1:], a.dtype)], axis=0)


def local_step(x2d, tgt, gf, g1, pool_scale, wa_pad, b_alpha, ng, g2, get_w, on_grad=None, on_settle=None, tick=None):
    emit = on_grad if on_grad is not None else (lambda group, grads: None)
    settle = on_settle if on_settle is not None else (lambda group, after: None)
    h1 = norm1(x2d, g1)
    wcat, pw = get_w("in", h1)
    pcat = mm_in(h1, wcat)
    dpool, ylin = pool_fwd(pcat, pw)
    pinned = tick("pool", ylin) if tick is not None else None
    og, o, states = gla_fwd(pcat, wa_pad, b_alpha, ng, pinned)
    w_go, w_o = get_w("mid", og)
    mixed, ygla = mm_gla_out(og, w_go, ylin, pcat, pool_scale)
    x2, h2 = mm_out(mixed, w_o, x2d, g2)
    w_up = get_w("up", h2)
    rup, act = mm_up(h2, w_up)
    w_dn = get_w("down", act)
    dx3, dx3b, g_nf, loss_row = mm_down(act, w_dn, x2, tgt, gf)

    gw_down = mm_wgrad("mm_dw_down", act, dx3b, DFF, D, (2, NCHIP, D // 2, D), (None, None, D // 2, D),
                       lambda j, i, k: (i % 2, i // 2, 0, 0), D // 2, D)
    token = emit("down", {"down": gw_down})
    dup = mm_dact(dx3b, w_dn, rup, after=token)
    token = settle("down", dup)
    dx2, dx2b, g_mlp = mm_dh2(dup, w_up, x2, dx3, g2, after=token)
    gw_up = mm_wgrad("mm_dw_up", h2, dup, D, DFF, (2, NCHIP, D // 2, D), (None, None, D // 2, D),
                     lambda j, i, k: (i, j, 0, 0), D // 2, D)
    token = emit("up", {"up": gw_up})
    dylin, dygla, dpcat, g_ps = mm_dmixed(dx2b, w_o, pcat, ylin, ygla, pool_scale, after=token)
    token = settle("up", dylin)
    gw_out = mm_wgrad("mm_dw_out", mixed, dx2b, D, D, (2, NCHIP, 256, D), (2, None, 256, D),
                      lambda j, i, k: (0, i, 0, 0), 512, D)
    do, dpcat, g_ng = mm_dog(dygla, w_go, o, pcat, ng, dpcat, after=token)
    gw_go = mm_wgrad("mm_dw_gla_out", og, dygla, D, D, (2, NCHIP, 256, D), (2, None, 256, D),
                     lambda j, i, k: (0, i, 0, 0), 512, D)
    token = emit("mix", {"out": gw_out, "gla_out": gw_go})
    dpcat, dv, g_wa, g_ba = gla_bwd(do, pcat, states, wa_pad, b_alpha, dpcat, b_alpha if token is None else token)
    token = settle("mix", dv)
    dpcat, dpw = pool_bwd(dylin, dpool, pw, lax.dynamic_update_slice(dpcat, dv, (0, OV)))
    gw_cat = mm_wgrad("mm_dw_in", h1, dpcat, D, NCAT, (D, NCAT), (1024, 1280), lambda j, i, k: (i, j), 1024, 1280, after=token)
    token = settle("in", emit("in", {"in_cat": gw_cat, "pool": dpw}))
    grad_x, g_mix = mm_dh1(dpcat, wcat, x2d, dx2, g1, after=token)
    return (loss_row[0, 0], grad_x, g_mix, g_ps, g_mlp, g_nf, g_ng, g_ba, g_wa, token,
            gw_cat, dpw, gw_go, gw_out, gw_up, gw_down)


def kernel(x, norm_mix_g, w_in, pool_w, pool_scale, w_alpha, b_alpha, gla_norm_g, w_gla_out, w_out, norm_mlp_g, w_mlp_up, w_mlp_down, norm_final_g, loss_target, m_norm_mix_g, m_w_in, m_pool_w, m_pool_scale, m_w_alpha, m_b_alpha, m_gla_norm_g, m_w_gla_out, m_w_out, m_norm_mlp_g, m_w_mlp_up, m_w_mlp_down, m_norm_final_g, v_norm_mix_g, v_w_in, v_pool_w, v_pool_scale, v_w_alpha, v_b_alpha, v_gla_norm_g, v_w_gla_out, v_w_out, v_norm_mlp_g, v_w_mlp_up, v_w_mlp_down, v_norm_final_g):
    chip = 2 * lax.axis_index("x") + lax.axis_index("y")
    chip_i = chip.astype(jnp.int32).reshape(1)
    core_i = lax.axis_index("c").astype(jnp.int32).reshape(1)
    place_i = jnp.concatenate([core_i, chip_i])
    tgt = loss_target.reshape(T, D)
    gf = norm_final_g.reshape(1, D)

    def halves(w2d):
        r, c = w2d.shape
        return w2d.astype(BF16).reshape(2, r // 2, c)

    pool_shard = pool_w.reshape(4 * PG, PO // NCHIP)
    w_in_r = w_in.reshape(2, D // 2, IN_SHARD)
    sent = {"in": [cast_bf16("cast_w_in", w_in_r), halves(pool_shard)]}
    flight = {}

    def start(group, after=None):
        flight[group] = gather_start("gather_start_" + group, sent[group], after)

    def relay(group, after):
        send, recv, shards, lands = flight[group]
        flight[group] = relay_turn("relay_turn_" + group, send, recv, shards, lands, after)

    def fetch(group, after):
        send, recv, shards, lands = flight[group]
        lands = relay_wait("relay_wait_" + group, send, recv, lands, after)
        return forward_halves("forward_" + group, shards, lands)

    start("in")
    m_in_f, v_in_f, w_go_f, w_o_f, w_up_f, w_dn_f, x_f, wal_f, gng_f = lax.optimization_barrier(
        (m_w_in, v_w_in, w_gla_out, w_out, w_mlp_up, w_mlp_down, x, w_alpha, gla_norm_g, flight["in"][2][0]))[:9]
    m_in_r, v_in_r = m_in_f.reshape(2, D // 2, IN_SHARD), v_in_f.reshape(2, D // 2, IN_SHARD)
    sent["mid"] = [halves(w_go_f[0]), halves(w_o_f[0])]
    relay("in", [m_in_r, v_in_r, *sent["mid"]])
    w_up_f, w_dn_f, x_f, wal_f, gng_f = lax.optimization_barrier(
        (w_up_f, w_dn_f, x_f, wal_f, gng_f, flight["in"][3][0]))[:5]
    sent["up"], sent["down"] = [halves(w_up_f[0])], [halves(w_dn_f[0])]
    x2d = x_f.reshape(T, D)
    big = [w_in_r, w_go_f[0], w_o_f[0], w_up_f[0], w_dn_f[0], pool_shard]

    def tick(point, after):
        if point == "pool":
            relay("mid", after)
            relay("up", flight["mid"][3][0])
            start("down", flight["up"][3][0])
            return [flight["up"][3][0], flight["down"][3][0]]

    def get_w(group, after):
        if group == "in":
            after = [after, *sent["up"], *sent["down"], wa_pad]
        if group == "up":
            relay("down", after)
            send, recv, lands, shards = flight["up"]
            lands = forward_wait("forward_wait_up", send, recv, lands, flight["down"][3][0])
            return lax.dynamic_update_index_in_dim(lands[0], shards[0], chip, 0).reshape(NCHIP, D, D)
        if group == "in":
            send, recv, shards, lands = flight["in"]
            send, recv, lands = forward_turn("forward_turn_in", send, recv, lands, after)
            start("mid", lands[0])
            start("up", flight["mid"][3][0])
            wcat = weights_to_cat("weights_to_cat_mine", lands[0], shards[0], place_i, False, after=flight["up"][3][0])
            lands = forward_wait("forward_wait_in", send, recv, lands, wcat)
            wcat = weights_to_cat("weights_to_cat_sibling", lands[0], shards[0], place_i, True, prev=wcat)
            g_pool = lax.dynamic_update_index_in_dim(lands[1], shards[1], chip, 0)
            pw = jnp.concatenate([g_pool[j].reshape(4, PG, PO // NCHIP) for j in range(NCHIP)], axis=2)
            return wcat, pw
        whole = fetch(group, after)
        if group == "mid":
            send, recv, shards, lands = flight["up"]
            flight["up"] = (*forward_turn("forward_turn_up", send, recv, lands, whole[0]), shards)
            w_go, w_o, _ = lax.optimization_barrier((whole[0], whole[1], flight["up"][2][0]))
            return w_go.reshape(D, D), w_o.reshape(D, D)
        return whole[0].reshape(DFF, D)

    small_w = pack_rows("pack_small_w", [wal_f[0].reshape(4, QK),
                                         jnp.concatenate([gng_f[0].reshape(1, 512), jnp.zeros((1, 512), F32)], axis=1)], 8)
    sw_all = gather_small("gather_small_w", small_w, False).reshape(8, 8, QK)
    wa_full = jnp.concatenate([sw_all[2 * j, 0:4].reshape(16, DK) for j in range(NCHIP)], axis=1)
    ng_full = jnp.concatenate([sw_all[2 * j, 4, 0:512].reshape(HEADS, DV // NCHIP) for j in range(NCHIP)], axis=1)
    wa_pad = _pad_rows(wa_full, APAD).astype(BF16)
    ng = ng_full.reshape(1, D)

    pending = {}
    wmv = {"in": (w_in_r, m_in_r, v_in_r), "gla_out": (big[1], m_w_gla_out, v_w_gla_out), "out": (big[2], m_w_out, v_w_out),
           "up": (big[3], m_w_mlp_up, v_w_mlp_up), "down": (big[4], m_w_mlp_down, v_w_mlp_down), "pool": (big[5], m_pool_w, v_pool_w)}
    big_res = {}

    def reduce_group(group, after):
        nms, send, recv, sums, lands = pending[group]
        sums, lands = scatter_wait("scatter_wait_" + group, send, recv, sums, lands, after)
        reduced = [sum_chips("sum_chips_" + nm, a, b, chip_i) for nm, a, b in zip(nms, sums, lands)]
        send, recv, reduced, lands, token = join_start("join_start_" + group, reduced)
        pending[group] = (nms, send, recv, reduced, lands)
        return token

    def update_group(group, after):
        nms, send, recv, reduced, lands = pending[group]
        reduced, from_sib = join_wait("join_wait_" + group, send, recv, reduced, lands, after)
        for nm, g_own, g_sib in zip(nms, reduced, from_sib):
            w, m, v = wmv[nm]
            shp = (2,) + g_own.shape
            big_res[nm] = adamw_halves("adamw_" + nm, w.reshape(shp), g_own, g_sib, m.reshape(shp), v.reshape(shp), core_i)

    def on_grad(group, grads):
        if group == "in":
            gw_in = grads_from_cat(grads["in_cat"])
            gw_pool = jnp.stack([grads["pool"][:, :, j * 128:(j + 1) * 128].reshape(2, 2 * PG, 128)
                                 for j in range(NCHIP)], axis=1)
            grads = {"in": gw_in, "pool": gw_pool}
        nms, parts = list(grads.keys()), list(grads.values())
        send, recv, parts, got, token = exchange_start("exchange_start_" + group, parts)
        pending[group] = (nms, send, recv, parts, got)
        return token

    def on_settle(group, after):
        if group == "in":
            for earlier in ("down", "up", "mix"):
                after = reduce_group(earlier, after)
        nms, send, recv, parts, got = pending[group]
        parts, got = exchange_wait("exchange_wait_" + group, send, recv, parts, got, after)
        sums = [add_pairs("add_pair_" + nm, a, b, core_i) for nm, a, b in zip(nms, parts, got)]
        send, recv, sums, lands, token = scatter_start("scatter_start_" + group, sums)
        pending[group] = (nms, send, recv, sums, lands)
        if group != "in":
            return token
        for earlier in ("down", "up", "mix"):
            update_group(earlier, token)
            token = big_res[pending[earlier][0][-1]][1]
        return [big_res[nm][1] for nm in ("down", "up", "out", "gla_out")]

    (loss_local, grad_x, g_mix, g_ps, g_mlp, g_nf, g_ng, g_ba, g_wa) = local_step(
        x2d, tgt, gf, norm_mix_g, pool_scale, wa_pad, b_alpha, ng, norm_mlp_g, get_w, on_grad, on_settle, tick)[:9]
    loss = lax.psum(loss_local, ("x", "y", "c"))
    join_in_token = reduce_group("in", grad_x)

    ROWS = 16

    def wide(a, n):
        return jnp.concatenate([a.reshape(1, n), jnp.zeros((1, D - n), F32)], axis=1)

    packed = pack_rows("pack_small_g", [g_mix, g_ps, g_mlp, g_nf, g_ng, wide(g_ba, QK), g_wa[0:16].reshape(8, D)], ROWS)
    tot = gather_small("reduce_small_g", packed, True, join_in_token)
    t_wa = lax.dynamic_slice(tot[6:14].reshape(16, QK), (0, chip * DK), (16, DK))
    t_ng = lax.dynamic_slice(tot[4].reshape(HEADS, DV), (0, chip * (DV // NCHIP)), (HEADS, DV // NCHIP))

    def pack_small(nm, mix, ps, mlp, nf, ba, wa, gn, after=None):
        return pack_rows(nm, [mix.reshape(1, D), ps.reshape(1, D), mlp.reshape(1, D), nf.reshape(1, D), wide(ba, QK),
                              wa.reshape(2, D), wide(gn, 512)], ROWS, after)

    update_group("in", tot)
    sg = pack_small("pack_g", tot[0], tot[1], tot[2], tot[3], tot[5, 0:QK], t_wa, t_ng, big_res["in"][3])
    sw = pack_small("pack_w", norm_mix_g, pool_scale, norm_mlp_g, norm_final_g, b_alpha, w_alpha, gla_norm_g)
    sm = pack_small("pack_m", m_norm_mix_g, m_pool_scale, m_norm_mlp_g, m_norm_final_g, m_b_alpha, m_w_alpha, m_gla_norm_g)
    sv = pack_small("pack_v", v_norm_mix_g, v_pool_scale, v_norm_mlp_g, v_norm_final_g, v_b_alpha, v_w_alpha, v_gla_norm_g)
    small_res = adamw("adamw_small", sw, sg, sm, sv)

    def unpack(p):
        return {"norm_mix_g": p[0].reshape(1, D), "pool_scale": p[1].reshape(1, D), "norm_mlp_g": p[2].reshape(1, D),
                "norm_final_g": p[3].reshape(D), "b_alpha": p[4, 0:QK].reshape(1, QK), "w_alpha": p[5:7].reshape(1, 16, DK),
                "gla_norm_g": p[7, 0:512].reshape(1, HEADS, DV // NCHIP)}

    order = ["norm_mix_g", "w_in", "pool_w", "pool_scale", "w_alpha", "b_alpha", "gla_norm_g", "w_gla_out", "w_out",
             "norm_mlp_g", "w_mlp_up", "w_mlp_down", "norm_final_g"]
    big_key = {"w_in": ("in", w_in.shape), "pool_w": ("pool", pool_w.shape), "w_gla_out": ("gla_out", w_gla_out.shape),
               "w_out": ("out", w_out.shape), "w_mlp_up": ("up", w_mlp_up.shape), "w_mlp_down": ("down", w_mlp_down.shape)}
    result = [loss, grad_x.reshape(1, T, D)]
    for kind in range(4):
        small = unpack(small_res[kind])
        for nm in order:
            if nm in big_key:
                key, shp = big_key[nm]
                result.append(big_res[key][kind].reshape(shp))
            else:
                result.append(small[nm])
    return tuple(result)
```

```python
import itertools

import jax
import jax.numpy as jnp
from jax import lax
from jax.experimental import pallas as pl
from jax.experimental.pallas import tpu as pltpu

F32 = jnp.float32
BF16 = jnp.bfloat16
SDS = jax.ShapeDtypeStruct
PINNED = pltpu.HBM
MESH = pl.DeviceIdType.MESH
ANY = pl.BlockSpec(memory_space=pl.ANY)

T = 2048
D = 2048
DFF = 8192
NCHIP = 4
IN_WIDTH = 11280
IN_SHARD = IN_WIDTH // NCHIP
CHUNK = 64
NCHUNK = T // CHUNK
HEADS = 4
DK = 256
DV = 512
QK = HEADS * DK
EPS = 1e-6
POOL_WINDOWS = (2, 4, 8, 16)
PG = 256
PO = 512

OV, OG, OGP, OGG, OU, OQ, OKK, OA = 0, 2048, 4096, 6144, 8192, 9216, 10240, 11264
NCAT = 11520
APAD = 128

VMEM_CAP = 56 * 1024 * 1024

PIECE_BYTES = 384 * 1024

ADAM_LR, ADAM_B1, ADAM_B2, ADAM_EPS, ADAM_WD, ADAM_STEP = 0.001, 0.9, 0.999, 1e-08, 0.01, 10


def _cparams(vmem_bytes=None, sem=None):
    kw = {}
    if vmem_bytes is not None:
        kw["vmem_limit_bytes"] = int(min(max(vmem_bytes, 32 * 1024 * 1024), VMEM_CAP))
    if sem is not None:
        kw["dimension_semantics"] = sem
    return pltpu.CompilerParams(**kw)


def _nbytes(shape, dtype):
    n = 1
    for s in shape:
        if s is not None:
            n *= s
    return n * jnp.dtype(dtype).itemsize


def _sigmoid(x):
    return 0.5 * jnp.tanh(0.5 * x) + 0.5


GLA_STEP = 4
EPI_COLS = 512


def _as_list(after):
    if after is None:
        return []
    return list(after) if isinstance(after, (list, tuple)) else [after]


def _in_hbm(*arrays):
    return [pltpu.with_memory_space_constraint(a, pltpu.HBM) for a in arrays]


def matmul(name, a, b, *, a_spec, b_spec, cdims, grid, acc_shape, outs, extras=(), epi, after=None, into=None):
    nj, ni, nk = grid
    ne, no = len(extras), len(outs)
    afters = _as_list(after) + ([] if into is None else [into[0]])
    first_out = 2 + ne + len(afters)

    def body(*refs):
        a_ref, b_ref = refs[0], refs[1]
        ex = refs[2:2 + ne]
        out_refs = refs[first_out:first_out + no]
        i = pl.program_id(1)
        part = lax.dot_general(a_ref[...], b_ref[...], (cdims, ((), ())), preferred_element_type=F32)
        if nk == 1:
            epi(part, ex, out_refs, i)
        else:
            acc_ref = refs[first_out + no]
            k = pl.program_id(2)

            @pl.when(k == 0)
            def _():
                acc_ref[...] = part

            @pl.when(k > 0)
            def _():
                acc_ref[...] += part

            @pl.when(k == nk - 1)
            def _():
                epi(acc_ref[...], ex, out_refs, i)

    in_specs = [pl.BlockSpec(*a_spec), pl.BlockSpec(*b_spec)] + [pl.BlockSpec(bs, im) for _, bs, im in extras]
    in_specs += [ANY] * len(afters)
    out_specs = [pl.BlockSpec(bs, im) for _, _, bs, im in outs]
    out_shape = [PINNED(s, dt) for s, dt, _, _ in outs]
    vm = 2 * (_nbytes(a_spec[0], a.dtype) + _nbytes(b_spec[0], b.dtype))
    vm += 2 * sum(_nbytes(bs, arr.dtype) for arr, bs, _ in extras)
    vm += 2 * sum(_nbytes(bs, dt) for _, dt, bs, _ in outs)
    vm += 6 * _nbytes(acc_shape, F32)
    scratch = [pltpu.VMEM(acc_shape, F32)] if nk > 1 else []
    return pl.pallas_call(
        body, name=name, grid=grid, in_specs=in_specs, out_specs=out_specs, out_shape=out_shape,
        scratch_shapes=scratch,
        input_output_aliases={} if into is None else {first_out - 1: into[1]},
        compiler_params=_cparams(vm, ("arbitrary", "arbitrary", "arbitrary")),
    )(*_in_hbm(a, b, *[arr for arr, _, _ in extras]), *afters)


NN =((1,), (0,))
NT = ((1,), (1,))
TN = ((0,), (0,))


def _row_acc(out_ref, val, i):
    @pl.when(i == 0)
    def _():
        out_ref[...] = val

    @pl.when(i > 0)
    def _():
        out_ref[...] += val


def _rms_bwd(xn, r, dxn):
    return r * (dxn - xn * jnp.mean(dxn * xn, axis=-1, keepdims=True))


def norm1(x, g):
    tm = 256

    def body(x_ref, g_ref, h_ref):
        xv = x_ref[...]
        r = lax.rsqrt(jnp.mean(xv * xv, axis=-1, keepdims=True) + EPS)
        h_ref[...] = (xv * r * g_ref[...]).astype(BF16)

    return pl.pallas_call(
        body, name="norm1", grid=(T // tm,),
        in_specs=[pl.BlockSpec((tm, D), lambda i: (i, 0)), pl.BlockSpec((1, D), lambda i: (0, 0))],
        out_specs=pl.BlockSpec((tm, D), lambda i: (i, 0)), out_shape=PINNED((T, D), BF16),
        compiler_params=_cparams(32 * 1024 * 1024, ("arbitrary",)),
    )(*_in_hbm(x, g))


def mm_in(h1, wcat):
    tm, tn = 1024, 1280

    def epi(acc, ex, outs, i):
        outs[0][...] = acc.astype(BF16)

    return matmul("mm_in", h1, wcat, a_spec=((tm, D), lambda j, i, k: (i, 0)), b_spec=((D, tn), lambda j, i, k: (0, j)),
                  cdims=NN, grid=(NCAT // tn, T // tm, 1), acc_shape=(tm, tn),
                  outs=[((T, NCAT), BF16, (tm, tn), lambda j, i, k: (i, j))], epi=epi)[0]


def _window_sum(x, w, up):
    n = x.shape[0]
    row = lax.broadcasted_iota(jnp.int32, x.shape, 0)
    s, sh = x, 1
    while sh < w:
        if up:
            s = s + jnp.where(row < n - sh, pltpu.roll(s, n - sh, axis=0), 0.0)
        else:
            s = s + jnp.where(row >= sh, pltpu.roll(s, sh, axis=0), 0.0)
        sh *= 2
    return s


def _inv_count(shape, w):
    row = lax.broadcasted_iota(jnp.int32, shape, 0)
    return 1.0 / jnp.minimum(row + 1, w).astype(F32)


def pool_fwd(pcat, pw):
    def body(u_ref, pw_ref, d_ref, y_ref):
        for gi, w in enumerate(POOL_WINDOWS):
            ug = u_ref[:, gi * PG:(gi + 1) * PG].astype(F32)
            dg = _window_sum(ug, w, False) * _inv_count(ug.shape, w) - ug
            db = dg.astype(BF16)
            d_ref[:, gi * PG:(gi + 1) * PG] = db
            y_ref[:, gi * PO:(gi + 1) * PO] = jnp.dot(db, pw_ref[gi], preferred_element_type=F32).astype(BF16)

    return pl.pallas_call(
        body, name="pool_fwd", grid=(1,),
        in_specs=[pl.BlockSpec((T, 4 * PG), lambda i: (0, OU // (4 * PG))), pl.BlockSpec((4, PG, PO), lambda i: (0, 0, 0))],
        out_specs=[pl.BlockSpec((T, 4 * PG), lambda i: (0, 0)), pl.BlockSpec((T, D), lambda i: (0, 0))],
        out_shape=[PINNED((T, 4 * PG), BF16), PINNED((T, D), BF16)],
        compiler_params=_cparams(48 * 1024 * 1024, ("arbitrary",)),
    )(pcat, pw)


def pool_bwd(dylin, d, pw, dproj):
    assert OU % (4 * PG) == 0

    def body(dy_ref, d_ref, pw_ref, held_ref, du_ref, dpw_ref):
        for gi, w in enumerate(POOL_WINDOWS):
            dyl = dy_ref[:, gi * PO:(gi + 1) * PO]
            dd = lax.dot_general(dyl, pw_ref[gi], (NT, ((), ())), preferred_element_type=F32)
            du = _window_sum(dd * _inv_count(dd.shape, w), w, True) - dd
            du_ref[:, gi * PG:(gi + 1) * PG] = du.astype(BF16)
            dpw_ref[gi] = lax.dot_general(d_ref[:, gi * PG:(gi + 1) * PG], dyl, (TN, ((), ())),
                                          preferred_element_type=F32).astype(BF16)

    return pl.pallas_call(
        body, name="pool_bwd", grid=(1,),
        in_specs=[pl.BlockSpec((T, D), lambda i: (0, 0)), pl.BlockSpec((T, 4 * PG), lambda i: (0, 0)),
                  pl.BlockSpec((4, PG, PO), lambda i: (0, 0, 0)), ANY],
        out_specs=[pl.BlockSpec((T, 4 * PG), lambda i: (0, OU // (4 * PG))), pl.BlockSpec((4, PG, PO), lambda i: (0, 0, 0))],
        out_shape=[PINNED((T, NCAT), BF16), PINNED((4, PG, PO), BF16)],
        input_output_aliases={3: 0},
        compiler_params=_cparams(48 * 1024 * 1024, ("arbitrary",)),
    )(dylin, d, pw, dproj)


def _gate_decay(alow, wa, ba):
    a = jnp.dot(alow, wa, preferred_element_type=F32) + ba
    ls = jax.nn.log_sigmoid(a) * (1.0 / 16.0)
    r = lax.broadcasted_iota(jnp.int32, (CHUNK, CHUNK), 0)
    c = lax.broadcasted_iota(jnp.int32, (CHUNK, CHUNK), 1)
    tri = jnp.where(c <= r, 1.0, 0.0).astype(F32)
    cum = jnp.dot(tri, ls, preferred_element_type=F32, precision=lax.Precision.HIGHEST)
    last = cum[CHUNK - 1:CHUNK, :]
    return a, jnp.exp(last - cum), jnp.exp(last)


def gla_fwd(pcat, wa, ba, ng, after=None):
    afters = _as_list(after)

    def body(q_ref, k_ref, v_ref, g_ref, al_ref, wa_ref, ba_ref, ng_ref, *rest):
        og_ref, o_ref, st_ref, s_scr = rest[len(afters):]

        @pl.when(pl.program_id(0) == 0)
        def _():
            s_scr[...] = jnp.zeros_like(s_scr)

        state = [s_scr[h] for h in range(HEADS)]
        for s in range(GLA_STEP):
            rs = slice(s * CHUNK, (s + 1) * CHUNK)
            _, e, decay = _gate_decay(al_ref[rs, :], wa_ref[...], ba_ref[...])
            kd = (k_ref[rs, :].astype(F32) * e).astype(BF16)
            qs = (q_ref[rs, :].astype(F32) * (DK ** -0.5)).astype(BF16)
            for h in range(HEADS):
                ck = slice(h * DK, (h + 1) * DK)
                cv = slice(h * DV, (h + 1) * DV)
                state[h] = state[h] * decay[:, ck] + lax.dot_general(v_ref[rs, cv], kd[:, ck], (TN, ((), ())),
                                                                     preferred_element_type=F32)
                sb = state[h].astype(BF16)
                st_ref[s, h] = sb
                oh = lax.dot_general(qs[:, ck], sb, (NT, ((), ())), preferred_element_type=F32)
                o_ref[rs, cv] = oh.astype(BF16)
                on = oh * lax.rsqrt(jnp.mean(oh * oh, axis=-1, keepdims=True) + EPS) * ng_ref[:, cv]
                gv = g_ref[rs, cv].astype(F32)
                og_ref[rs, cv] = (on * (gv * _sigmoid(gv))).astype(BF16)
        for h in range(HEADS):
            s_scr[h] = state[h]

    row = lambda c: (c, 0)
    rows = GLA_STEP * CHUNK
    return pl.pallas_call(
        body, name="gla_fwd", grid=(NCHUNK // GLA_STEP,),
        in_specs=[pl.BlockSpec((rows, QK), lambda c: (c, OQ // QK)), pl.BlockSpec((rows, QK), lambda c: (c, OKK // QK)),
                  pl.BlockSpec((rows, D), lambda c: (c, OV // D)), pl.BlockSpec((rows, D), lambda c: (c, OG // D)),
                  pl.BlockSpec((rows, APAD), lambda c: (c, OA // APAD)),
                  pl.BlockSpec((APAD, QK), lambda c: (0, 0)), pl.BlockSpec((1, QK), lambda c: (0, 0)),
                  pl.BlockSpec((1, D), lambda c: (0, 0))] + [ANY] * len(afters),
        out_specs=[pl.BlockSpec((rows, D), row), pl.BlockSpec((rows, D), row),
                   pl.BlockSpec((GLA_STEP, HEADS, DV, DK), lambda c: (c, 0, 0, 0))],
        out_shape=[PINNED((T, D), BF16), PINNED((T, D), BF16), PINNED((NCHUNK, HEADS, DV, DK), BF16)],
        scratch_shapes=[pltpu.VMEM((HEADS, DV, DK), F32)],
        compiler_params=_cparams(32 * 1024 * 1024, ("arbitrary",)),
    )(*_in_hbm(pcat, pcat, pcat, pcat, pcat, wa, ba, ng), *afters)


def gla_bwd(do, pcat, states, wa, ba, dproj, after):
    tail = NCAT - OQ
    assert (OKK, OA) == (OQ + QK, OQ + 2 * QK) and OQ % tail == 0

    def body(do_ref, q_ref, k_ref, v_ref, al_ref, sc_ref, sp_ref, wa_ref, ba_ref, after_ref, held_ref,
             dp_ref, dv_ref, dwa_ref, dba_ref, ds_scr):
        i = pl.program_id(0)
        dp_ref[:, 2 * QK + APAD:] = jnp.zeros((GLA_STEP * CHUNK, tail - 2 * QK - APAD), BF16)

        @pl.when(i == 0)
        def _():
            ds_scr[...] = jnp.zeros_like(ds_scr)

        ds = [ds_scr[h] for h in range(HEADS)]
        dwa, dba = 0.0, 0.0
        for u in reversed(range(GLA_STEP)):
            rs = slice(u * CHUNK, (u + 1) * CHUNK)
            first_chunk = jnp.logical_and(i == NCHUNK // GLA_STEP - 1, u == 0)
            has_prev = jnp.where(first_chunk, 0.0, 1.0).astype(F32)
            a, e, decay = _gate_decay(al_ref[rs, :], wa_ref[...], ba_ref[...])
            kdf = k_ref[rs, :].astype(F32) * e
            kd = kdf.astype(BF16)
            qs = (q_ref[rs, :].astype(F32) * (DK ** -0.5)).astype(BF16)
            dkd_parts, ddecay_parts = [], []
            for h in range(HEADS):
                ck = slice(h * DK, (h + 1) * DK)
                cv = slice(h * DV, (h + 1) * DV)
                doh = do_ref[rs, cv]
                dsh = ds[h] + lax.dot_general(doh, qs[:, ck], (TN, ((), ())), preferred_element_type=F32)
                dsb = dsh.astype(BF16)
                dp_ref[rs, ck] = (jnp.dot(doh, sc_ref[u, h], preferred_element_type=F32) * (DK ** -0.5)).astype(BF16)
                dkd_parts.append(jnp.dot(v_ref[rs, cv], dsb, preferred_element_type=F32))
                dv_ref[rs, cv] = lax.dot_general(kd[:, ck], dsb, (NT, ((), ())), preferred_element_type=F32).astype(BF16)
                s_prev = (sp_ref[h] if u == 0 else sc_ref[u - 1, h]).astype(F32)
                ddecay_parts.append(jnp.sum(dsh * s_prev, axis=0, keepdims=True) * has_prev)
                ds[h] = dsh * decay[:, ck]
            dkd = jnp.concatenate(dkd_parts, axis=1)
            ddecay = jnp.concatenate(ddecay_parts, axis=1)
            dp_ref[rs, QK:2 * QK] = (dkd * e).astype(BF16)
            dearg = dkd * kdf
            dlast = jnp.sum(dearg, axis=0, keepdims=True) + ddecay * decay
            r = lax.broadcasted_iota(jnp.int32, (CHUNK, CHUNK), 0)
            c = lax.broadcasted_iota(jnp.int32, (CHUNK, CHUNK), 1)
            triu = jnp.where(c >= r, 1.0, 0.0).astype(F32)
            dls = dlast - jnp.dot(triu, dearg, preferred_element_type=F32, precision=lax.Precision.HIGHEST)
            da = dls * (1.0 / 16.0) * (1.0 - _sigmoid(a))
            dab = da.astype(BF16)
            dp_ref[rs, 2 * QK:2 * QK + APAD] = lax.dot_general(dab, wa_ref[...], (NT, ((), ())),
                                                               preferred_element_type=F32).astype(BF16)
            dwa = dwa + lax.dot_general(al_ref[rs, :], dab, (TN, ((), ())), preferred_element_type=F32)
            dba = dba + jnp.sum(da, axis=0, keepdims=True)
        for h in range(HEADS):
            ds_scr[h] = ds[h]

        @pl.when(i == 0)
        def _():
            dwa_ref[...] = dwa
            dba_ref[...] = dba

        @pl.when(i > 0)
        def _():
            dwa_ref[...] += dwa
            dba_ref[...] += dba

    rows = GLA_STEP * CHUNK
    rev = lambda i: NCHUNK // GLA_STEP - 1 - i
    return pl.pallas_call(
        body, name="gla_bwd", grid=(NCHUNK // GLA_STEP,),
        in_specs=[pl.BlockSpec((rows, D), lambda i: (rev(i), 0)),
                  pl.BlockSpec((rows, QK), lambda i: (rev(i), OQ // QK)), pl.BlockSpec((rows, QK), lambda i: (rev(i), OKK // QK)),
                  pl.BlockSpec((rows, D), lambda i: (rev(i), OV // D)), pl.BlockSpec((rows, APAD), lambda i: (rev(i), OA // APAD)),
                  pl.BlockSpec((GLA_STEP, HEADS, DV, DK), lambda i: (rev(i), 0, 0, 0)),
                  pl.BlockSpec((None, HEADS, DV, DK), lambda i: (jnp.maximum(rev(i) * GLA_STEP - 1, 0), 0, 0, 0)),
                  pl.BlockSpec((APAD, QK), lambda i: (0, 0)), pl.BlockSpec((1, QK), lambda i: (0, 0)), ANY, ANY],
        out_specs=[pl.BlockSpec((rows, tail), lambda i: (rev(i), OQ // tail)), pl.BlockSpec((rows, D), lambda i: (rev(i), 0)),
                   pl.BlockSpec((APAD, QK), lambda i: (0, 0)), pl.BlockSpec((1, QK), lambda i: (0, 0))],
        out_shape=[PINNED((T, NCAT), BF16), PINNED((T, D), BF16), PINNED((APAD, QK), F32), PINNED((1, QK), F32)],
        scratch_shapes=[pltpu.VMEM((HEADS, DV, DK), F32)],
        input_output_aliases={10: 0},
        compiler_params=_cparams(32 * 1024 * 1024, ("arbitrary",)),
    )(*_in_hbm(do, pcat, pcat, pcat, pcat, states, states, wa, ba), after, dproj)


TMF = 256
TMW = 512
_rowblk = ((TMF, D), lambda j, i, k: (i, 0))
_vec = ((1, D), lambda j, i, k: (0, 0))


def _full_spec(col):
    return ((TMF, D), lambda j, i, k: (i, col))


TBIG = 1024


def square_matmul(name, a, b, *, a_spec, b_spec, cdims, nk, after=None):
    def epi(acc, ex, outs, i):
        outs[0][...] = acc

    return matmul(name, a, b, a_spec=a_spec, b_spec=b_spec, cdims=cdims, grid=(D // TBIG, T // TBIG, nk),
                  acc_shape=(TBIG, TBIG), outs=[((T, D), F32, (TBIG, TBIG), lambda j, i, k: (i, j))], epi=epi,
                  after=after)[0]


def rowwise(name, y, *, extras, outs, epi):
    ne = len(extras)

    def body(*refs):
        epi(refs[0][...], refs[1:1 + ne], refs[1 + ne:], pl.program_id(1))

    in_specs = [pl.BlockSpec(*_rowblk)] + [pl.BlockSpec(bs, im) for _, bs, im in extras]
    return pl.pallas_call(
        body, name=name, grid=(1, T // TMF, 1), in_specs=in_specs,
        out_specs=[pl.BlockSpec(bs, im) for _, _, bs, im in outs], out_shape=[PINNED(s, dt) for s, dt, _, _ in outs],
        compiler_params=_cparams(40 * 1024 * 1024, ("arbitrary", "arbitrary", "arbitrary")),
    )(*_in_hbm(y, *[arr for arr, _, _ in extras]))


def mm_gla_out(og, w, ylin, pcat, pscale):
    def epi(acc, ex, outs, i):
        ylin_ref, lgp_ref, lgg_ref, ps_ref = ex
        for c0 in range(0, D, EPI_COLS):
            cs = slice(c0, c0 + EPI_COLS)
            gp = _sigmoid(lgp_ref[:, cs].astype(F32))
            gg = _sigmoid(lgg_ref[:, cs].astype(F32))
            a = acc[:, cs]
            outs[0][:, cs] = (gp * (ylin_ref[:, cs].astype(F32) * ps_ref[:, cs]) + gg * a).astype(BF16)
            outs[1][:, cs] = a.astype(BF16)

    return matmul("mm_gla_out", og, w, a_spec=_rowblk, b_spec=((D, D), lambda j, i, k: (0, 0)), cdims=NN,
                  grid=(1, T // TMF, 1), acc_shape=(TMF, D),
                  extras=[(ylin, *_rowblk), (pcat, *_full_spec(OGP // D)), (pcat, *_full_spec(OGG // D)), (pscale, *_vec)],
                  outs=[((T, D), BF16, *_rowblk), ((T, D), BF16, *_rowblk)], epi=epi)


def mm_out(mixed, w, x, g2):
    def epi(acc, ex, outs, i):
        x_ref, g_ref = ex
        x2 = x_ref[...] + acc
        r = lax.rsqrt(jnp.mean(x2 * x2, axis=-1, keepdims=True) + EPS)
        outs[0][...] = x2
        outs[1][...] = (x2 * r * g_ref[...]).astype(BF16)

    return matmul("mm_out", mixed, w, a_spec=_rowblk, b_spec=((D, D), lambda j, i, k: (0, 0)), cdims=NN,
                  grid=(1, T // TMF, 1), acc_shape=(TMF, D), extras=[(x, *_rowblk), (g2, *_vec)],
                  outs=[((T, D), F32, *_rowblk), ((T, D), BF16, *_rowblk)], epi=epi)


def mm_up(h2, wup):
    def epi(acc, ex, outs, i):
        r = jnp.maximum(acc, 0.0)
        outs[0][...] = r.astype(BF16)
        outs[1][...] = (r * r).astype(BF16)

    blk = ((TMW, D), lambda j, i, k: (i, j))
    return matmul("mm_up", h2, wup, a_spec=((TMW, D), lambda j, i, k: (i, 0)), b_spec=((None, D, D), lambda j, i, k: (j, 0, 0)),
                  cdims=NN, grid=(NCHIP, T // TMW, 1), acc_shape=(TMW, D),
                  outs=[((T, DFF), BF16, *blk), ((T, DFF), BF16, *blk)], epi=epi)


def mm_down(act, wdown, x2, tgt, gf):
    tk = 4096

    def epi(acc, ex, outs, i):
        x2_ref, t_ref, g_ref = ex
        dx_ref, dxb_ref, gnf_ref, loss_ref = outs
        x3 = x2_ref[...] + acc
        r = lax.rsqrt(jnp.mean(x3 * x3, axis=-1, keepdims=True) + EPS)
        xn = x3 * r
        err = xn * g_ref[...] - t_ref[...]
        lsum = 0.5 * jnp.sum(jnp.mean(err * err, axis=-1, keepdims=True), axis=0, keepdims=True)
        dy = err * (1.0 / D)
        _row_acc(gnf_ref, jnp.sum(dy * xn, axis=0, keepdims=True), i)
        _row_acc(loss_ref, jnp.broadcast_to(lsum, (1, 128)), i)
        dx3 = _rms_bwd(xn, r, dy * g_ref[...])
        dx_ref[...] = dx3
        dxb_ref[...] = dx3.astype(BF16)

    y = square_matmul("mm_down", act, wdown, a_spec=((TBIG, tk), lambda j, i, k: (i, k)),
                      b_spec=((tk, TBIG), lambda j, i, k: (k, j)), cdims=NN, nk=DFF // tk)
    return rowwise("rows_final", y, extras=[(x2, *_rowblk), (tgt, *_rowblk), (gf, *_vec)],
                   outs=[((T, D), F32, *_rowblk), ((T, D), BF16, *_rowblk), ((1, D), F32, *_vec),
                         ((1, 128), F32, (1, 128), lambda j, i, k: (0, 0))], epi=epi)


def mm_dact(dx3b, wdown, rup, after=None):
    def epi(acc, ex, outs, i):
        outs[0][...] = (acc * 2.0 * ex[0][...].astype(F32)).astype(BF16)

    blk = ((TMW, D), lambda j, i, k: (i, j))
    return matmul("mm_dact", dx3b, wdown, a_spec=((TMW, D), lambda j, i, k: (i, 0)), b_spec=((D, D), lambda j, i, k: (j, 0)),
                  cdims=NT, grid=(DFF // D, T // TMW, 1), acc_shape=(TMW, D), extras=[(rup, *blk)],
                  outs=[((T, DFF), BF16, *blk)], epi=epi, after=after)[0]


def mm_wgrad(name, a, b, m, n, out_shape, out_block, out_map, tm, tn, after=None):
    def epi(acc, ex, outs, i):
        outs[0][...] = acc.astype(BF16).reshape(outs[0].shape)

    return matmul(name, a, b, a_spec=((T, tm), lambda j, i, k: (0, i)), b_spec=((T, tn), lambda j, i, k: (0, j)),
                  cdims=TN, grid=(n // tn, m // tm, 1), acc_shape=(tm, tn),
                  outs=[(out_shape, BF16, out_block, out_map)], epi=epi, after=after)[0]


def mm_dh2(dup, wup, x2, dx3, g2, after=None):
    def epi(acc, ex, outs, i):
        x2_ref, dx3_ref, g_ref = ex
        x2 = x2_ref[...]
        r = lax.rsqrt(jnp.mean(x2 * x2, axis=-1, keepdims=True) + EPS)
        xn = x2 * r
        _row_acc(outs[2], jnp.sum(acc * xn, axis=0, keepdims=True), i)
        dx2 = dx3_ref[...] + _rms_bwd(xn, r, acc * g_ref[...])
        outs[0][...] = dx2
        outs[1][...] = dx2.astype(BF16)

    y = square_matmul("mm_dh2", dup, wup, a_spec=((TBIG, D), lambda j, i, k: (i, k)),
                      b_spec=((None, TBIG, D), lambda j, i, k: (k, j, 0)), cdims=NT, nk=NCHIP, after=after)
    return rowwise("rows_dh2", y, extras=[(x2, *_rowblk), (dx3, *_rowblk), (g2, *_vec)],
                   outs=[((T, D), F32, *_rowblk), ((T, D), BF16, *_rowblk), ((1, D), F32, *_vec)], epi=epi)


def mm_dmixed(dx2b, wout, pcat, ylin, ygla, pscale, after=None):
    assert OGG == OGP + D and OGP % (2 * D) == 0

    def epi(acc, ex, outs, i):
        lgp_ref, lgg_ref, ylin_ref, ygla_ref, ps_ref = ex
        dps = []
        for c0 in range(0, D, EPI_COLS):
            cs = slice(c0, c0 + EPI_COLS)
            gp = _sigmoid(lgp_ref[:, cs].astype(F32))
            gg = _sigmoid(lgg_ref[:, cs].astype(F32))
            yl = ylin_ref[:, cs].astype(F32)
            ps = ps_ref[:, cs]
            a = acc[:, cs]
            agp = a * gp
            outs[0][:, cs] = (agp * ps).astype(BF16)
            outs[1][:, cs] = (a * gg).astype(BF16)
            outs[2][:, cs] = (agp * (yl * ps) * (1.0 - gp)).astype(BF16)
            outs[2][:, D + c0:D + c0 + EPI_COLS] = (a * ygla_ref[:, cs].astype(F32) * gg * (1.0 - gg)).astype(BF16)
            dps.append(jnp.sum(agp * yl, axis=0, keepdims=True))
        _row_acc(outs[3], jnp.concatenate(dps, axis=1), i)

    return matmul("mm_dmixed", dx2b, wout, a_spec=_rowblk, b_spec=((D, D), lambda j, i, k: (0, 0)), cdims=NT,
                  grid=(1, T // TMF, 1), acc_shape=(TMF, D),
                  extras=[(pcat, *_full_spec(OGP // D)), (pcat, *_full_spec(OGG // D)), (ylin, *_rowblk), (ygla, *_rowblk),
                          (pscale, *_vec)],
                  outs=[((T, D), BF16, *_rowblk)] * 2
                       + [((T, NCAT), BF16, (TMF, 2 * D), lambda j, i, k: (i, OGP // (2 * D))), ((1, D), F32, *_vec)],
                  epi=epi, after=after)


def mm_dog(dygla, wgo, o, pcat, ng, dproj, after=None):
    def epi(acc, ex, outs, i):
        o_ref, g_ref, ng_ref = ex
        do_ref, dg_ref, gng_ref = outs
        gparts = []
        for h in range(HEADS):
            cv = slice(h * DV, (h + 1) * DV)
            oh = o_ref[:, cv].astype(F32)
            r = lax.rsqrt(jnp.mean(oh * oh, axis=-1, keepdims=True) + EPS)
            on = oh * r
            gv = g_ref[:, cv].astype(F32)
            sg = _sigmoid(gv)
            a = acc[:, cv]
            dgain = a * (gv * sg)
            gparts.append(jnp.sum(dgain * on, axis=0, keepdims=True))
            ngh = ng_ref[:, cv]
            do_ref[:, cv] = _rms_bwd(on, r, dgain * ngh).astype(BF16)
            dg_ref[:, cv] = (a * (on * ngh) * (sg * (1.0 + gv * (1.0 - sg)))).astype(BF16)
        _row_acc(gng_ref, jnp.concatenate(gparts, axis=1), i)

    return matmul("mm_dog", dygla, wgo, a_spec=_rowblk, b_spec=((D, D), lambda j, i, k: (0, 0)), cdims=NT,
                  grid=(1, T // TMF, 1), acc_shape=(TMF, D),
                  extras=[(o, *_rowblk), (pcat, *_full_spec(OG // D)), (ng, *_vec)],
                  outs=[((T, D), BF16, *_rowblk), ((T, NCAT), BF16, *_full_spec(OG // D)), ((1, D), F32, *_vec)],
                  epi=epi, after=after, into=(dproj, 1))


def mm_dh1(dpcat, wcat, x, dx2, g1, after=None):
    tk = 3840

    def epi(acc, ex, outs, i):
        x_ref, dx2_ref, g_ref = ex
        xv = x_ref[...]
        r = lax.rsqrt(jnp.mean(xv * xv, axis=-1, keepdims=True) + EPS)
        xn = xv * r
        _row_acc(outs[1], jnp.sum(acc * xn, axis=0, keepdims=True), i)
        outs[0][...] = dx2_ref[...] + _rms_bwd(xn, r, acc * g_ref[...])

    y = square_matmul("mm_dh1", dpcat, wcat, a_spec=((TBIG, tk), lambda j, i, k: (i, k)),
                      b_spec=((TBIG, tk), lambda j, i, k: (j, k)), cdims=NT, nk=NCAT // tk, after=after)
    return rowwise("rows_dh1", y, extras=[(x, *_rowblk), (dx2, *_rowblk), (g1, *_vec)],
                   outs=[((T, D), F32, *_rowblk), ((1, D), F32, *_vec)], epi=epi)


def _tile_rows(rows, cols, n_arrays):
    tm = rows
    while tm % 32 == 0 and 2 * n_arrays * tm * cols * 4 > 36 * 1024 * 1024:
        tm //= 2
    return tm


def add_pairs(name, parts, theirs, core):
    _, _, r, c = parts.shape
    tm = _tile_rows(r, c, 3)

    def body(core_ref, a_ref, b_ref, o_ref):
        o_ref[...] = (a_ref[...].astype(F32) + b_ref[...].astype(F32)).astype(BF16)

    spec = pl.BlockSpec((None, tm, c), lambda j, i, core_ref: (j, i, 0))
    grid_spec = pltpu.PrefetchScalarGridSpec(
        num_scalar_prefetch=1, grid=(NCHIP, r // tm),
        in_specs=[pl.BlockSpec((None, None, tm, c), lambda j, i, core_ref: (core_ref[0], j, i, 0)), spec], out_specs=spec)
    return pl.pallas_call(body, name=name, grid_spec=grid_spec, out_shape=PINNED((NCHIP, r, c), BF16),
                          compiler_params=_cparams(40 * 1024 * 1024, ("arbitrary", "arbitrary")))(core, *_in_hbm(parts, theirs))


def sum_chips(name, sums, landed, chip):
    _, r, c = sums.shape
    tm = _tile_rows(r, c, 4)

    def body(chip_ref, own_ref, l_ref, o_ref):
        s = own_ref[...].astype(F32)
        for t in range(NCHIP - 1):
            s = s + l_ref[t].astype(F32)
        o_ref[...] = s

    grid_spec = pltpu.PrefetchScalarGridSpec(
        num_scalar_prefetch=1, grid=(r // tm,),
        in_specs=[pl.BlockSpec((None, tm, c), lambda i, chip_ref: (chip_ref[0], i, 0)),
                  pl.BlockSpec((NCHIP - 1, tm, c), lambda i, chip_ref: (0, i, 0))],
        out_specs=pl.BlockSpec((tm, c), lambda i, chip_ref: (i, 0)))
    return pl.pallas_call(body, name=name, grid_spec=grid_spec, out_shape=PINNED((r, c), F32),
                          compiler_params=_cparams(40 * 1024 * 1024, ("arbitrary",)))(chip, *_in_hbm(sums, landed))


def _adamw_math(wv, gv, mv, vv):
    mn = ADAM_B1 * mv + (1.0 - ADAM_B1) * gv
    vn = ADAM_B2 * vv + (1.0 - ADAM_B2) * (gv * gv)
    mh = mn / (1.0 - ADAM_B1 ** ADAM_STEP)
    vh = vn / (1.0 - ADAM_B2 ** ADAM_STEP)
    return -ADAM_LR * (mh / (jnp.sqrt(vh) + ADAM_EPS) + ADAM_WD * wv), mn, vn


def adamw(name, w, g, m, v):
    def body(w_ref, g_ref, m_ref, v_ref, go_ref, d_ref, mo_ref, vo_ref):
        gv = g_ref[...]
        go_ref[...] = gv
        d_ref[...], mo_ref[...], vo_ref[...] = _adamw_math(w_ref[...], gv, m_ref[...], v_ref[...])

    return pl.pallas_call(body, name=name, out_shape=[SDS(w.shape, F32)] * 4)(w, g, m, v)


def adamw_halves(name, w, g_own, g_sib, m, v, core):
    _, r, c = w.shape
    tm = _tile_rows(r, c, 10)

    def body(core_ref, w_ref, go_ref, gs_ref, m_ref, v_ref, g_out, d_out, m_out, v_out):
        gv = jnp.where(pl.program_id(0) == core_ref[0], go_ref[...], gs_ref[...])
        g_out[...] = gv
        d_out[...], m_out[...], v_out[...] = _adamw_math(w_ref[...], gv, m_ref[...], v_ref[...])

    full = pl.BlockSpec((None, tm, c), lambda h, i, core_ref: (h, i, 0))
    own = pl.BlockSpec((tm, c), lambda h, i, core_ref: (jnp.where(h == core_ref[0], i, 0), 0))
    sib = pl.BlockSpec((tm, c), lambda h, i, core_ref: (jnp.where(h == core_ref[0], 0, i), 0))
    grid_spec = pltpu.PrefetchScalarGridSpec(num_scalar_prefetch=1, grid=(2, r // tm),
                                             in_specs=[full, own, sib, full, full], out_specs=[full] * 4)
    return pl.pallas_call(body, name=name, grid_spec=grid_spec, out_shape=[SDS(w.shape, F32)] * 4,
                          compiler_params=_cparams(48 * 1024 * 1024, ("arbitrary", "arbitrary")))(core, *_in_hbm(w, g_own, g_sib, m, v))


def cast_bf16(name, w):
    _, r, c = w.shape
    tm = _tile_rows(r, c, 2)

    def body(w_ref, o_ref):
        o_ref[...] = w_ref[...].astype(BF16)

    spec = pl.BlockSpec((None, tm, c), lambda h, i: (h, i, 0))
    return pl.pallas_call(body, name=name, grid=(2, r // tm), in_specs=[spec], out_specs=spec, out_shape=PINNED(w.shape, BF16),
                          compiler_params=_cparams(40 * 1024 * 1024, ("arbitrary", "arbitrary")))(w)


def cast_to_slot(name, w, place):
    _, r, c = w.shape
    tm = _tile_rows(r, c, 2)

    def body(p_ref, w_ref, o_ref):
        o_ref[...] = w_ref[...].astype(BF16)

    grid_spec = pltpu.PrefetchScalarGridSpec(
        num_scalar_prefetch=1, grid=(2, r // tm), in_specs=[pl.BlockSpec((None, tm, c), lambda h, i, p: (h, i, 0))],
        out_specs=pl.BlockSpec((None, None, tm, c), lambda h, i, p: (p[1], h, i, 0)))
    return pl.pallas_call(body, name=name, grid_spec=grid_spec, out_shape=PINNED((NCHIP, 2, r, c), BF16),
                          compiler_params=_cparams(40 * 1024 * 1024, ("arbitrary", "arbitrary")))(place, *_in_hbm(w))


def pack_rows(name, parts, rows, after=None):
    width = parts[0].shape[1]
    n = len(parts)
    afters = _as_list(after)

    def body(*refs):
        out_ref = refs[n + len(afters)]
        out_ref[...] = jnp.zeros_like(out_ref)
        off = 0
        for p in refs[:n]:
            out_ref[off:off + p.shape[0], :] = p[...]
            off += p.shape[0]

    vm = pl.BlockSpec(memory_space=pltpu.VMEM)
    return pl.pallas_call(body, name=name, in_specs=[vm] * n + [ANY] * len(afters), out_specs=vm,
                          out_shape=SDS((rows, width), F32))(*parts, *afters)


def _place():
    x, y, c = lax.axis_index("x"), lax.axis_index("y"), lax.axis_index("c")
    chips = [(1 - x, y), (x, 1 - y), (1 - x, 1 - y)]
    return x, y, c, chips


def _row_split(shape, dtype):
    r, c = shape
    n = 1
    while r % (2 * n) == 0 and (r // (2 * n)) % 16 == 0 and (r // n) * c * jnp.dtype(dtype).itemsize > PIECE_BYTES:
        n *= 2
    return [pl.ds(s * (r // n), r // n) for s in range(n)]


def _pieces(ref):
    *lead, r, c = ref.shape
    split = _row_split((r, c), ref.dtype)
    return [ref.at[(*idx, s)] for idx in itertools.product(*[range(d) for d in lead]) for s in split]


HBM = pl.BlockSpec(memory_space=pltpu.HBM)
SEM = pl.BlockSpec(memory_space=pltpu.SEMAPHORE)
EFFECT = pltpu.SideEffectType.DATAFLOW_SIDE_EFFECTING


def _own_half(shard_refs, land, a, me, c):
    return land[a].at[me, c] if shard_refs[a] is None else shard_refs[a].at[c]


def _spread(refs, shards):
    it = iter(refs)
    return [None if s is None else next(it) for s in shards]


def gather_start(name, items, after=None):
    n = len(items)
    shards = [s if s.ndim == 3 else None for s in items]
    given = [s for s in shards if s is not None]
    ns = len(given)
    afters = _as_list(after)

    def body(*refs):
        src, land = _spread(refs[:ns], shards), refs[ns:ns + n]
        send, recv = refs[ns + n + len(afters)], refs[ns + n + len(afters) + 1]
        x, y, c, chips = _place()
        me = 2 * x + y
        for a in range(n):
            for j, (cx, cy) in enumerate(chips[:2]):
                for sp, dp in zip(_pieces(_own_half(src, land, a, me, c)), _pieces(land[a].at[me, c])):
                    pltpu.make_async_remote_copy(sp, dp, send.at[2 * a + j], recv.at[2 * a + j],
                                                 device_id=(cx, cy, c), device_id_type=MESH).start()

    lands = [pltpu.with_memory_space_constraint(lax.empty((NCHIP,) + s.shape, s.dtype) if s.ndim == 3 else s, pltpu.HBM)
             for s in items]
    srcs = [pltpu.with_memory_space_constraint(s, pltpu.HBM) for s in given]
    outs = pl.pallas_call(
        body, name=name,
        out_shape=(pltpu.SemaphoreType.DMA((2 * n,)), pltpu.SemaphoreType.DMA((2 * n,)),
                   *[pltpu.HBM(s.shape, s.dtype) for s in given], *[pltpu.HBM(l.shape, l.dtype) for l in lands]),
        in_specs=[HBM] * (ns + n) + [ANY] * len(afters), out_specs=(SEM, SEM, *([HBM] * (ns + n))),
        input_output_aliases={i: 2 + i for i in range(ns + n)},
        compiler_params=pltpu.CompilerParams(has_side_effects=EFFECT),
    )(*srcs, *lands, *afters)
    return outs[0], outs[1], _spread(outs[2:2 + ns], shards), list(outs[2 + ns:2 + ns + n])


def _relay_blocks(land, c, chips):
    (xx, xy), (yx, yy), (dx, dy) = chips
    rows = land.shape[2] // 2
    upper, lower = pl.ds(0, rows), pl.ds(rows, rows)
    return [(land.at[2 * yx + yy, c, lower], land.at[2 * dx + dy, c, lower]),
            (land.at[2 * xx + xy, c, upper], land.at[2 * dx + dy, c, upper])]


def relay_turn(name, send, recv, shards, lands, after):
    n = len(lands)
    given = [s for s in shards if s is not None]
    ns = len(given)
    afters = _as_list(after)

    def body(*refs):
        src, had = _spread(refs[:ns], shards), refs[ns:ns + n]
        send_ref, recv_ref = refs[ns + n], refs[ns + n + 1]
        rsend, rrecv = refs[ns + n + 2 + len(afters)], refs[ns + n + 3 + len(afters)]
        land = refs[2 * ns + n + 4 + len(afters):2 * ns + 2 * n + 4 + len(afters)]
        x, y, c, chips = _place()
        me = 2 * x + y
        for a in range(n):
            for j, (cx, cy) in enumerate(chips[:2]):
                cp = pltpu.make_async_remote_copy(_own_half(src, had, a, me, c), had[a].at[2 * cx + cy, c],
                                                  send_ref.at[2 * a + j], recv_ref.at[2 * a + j],
                                                  device_id=(cx, cy, c), device_id_type=MESH)
                cp.wait_send()
                cp.wait_recv()
        for a in range(n):
            for j, ((sent, _), (dst, _)) in enumerate(zip(_relay_blocks(had[a], c, chips), _relay_blocks(land[a], c, chips))):
                cx, cy = chips[j]
                for sp, dp in zip(_pieces(sent), _pieces(dst)):
                    pltpu.make_async_remote_copy(sp, dp, rsend.at[2 * a + j], rrecv.at[2 * a + j],
                                                 device_id=(cx, cy, c), device_id_type=MESH).start()

    outs = pl.pallas_call(
        body, name=name,
        out_shape=(pltpu.SemaphoreType.DMA((2 * n,)), pltpu.SemaphoreType.DMA((2 * n,)),
                   *[pltpu.HBM(s.shape, s.dtype) for s in given], *[pltpu.HBM(l.shape, l.dtype) for l in lands]),
        in_specs=[HBM] * (ns + n) + [SEM, SEM] + [ANY] * len(afters), out_specs=(SEM, SEM, *([HBM] * (ns + n))),
        input_output_aliases={i: 2 + i for i in range(ns + n)},
        compiler_params=pltpu.CompilerParams(has_side_effects=EFFECT),
    )(*given, *lands, send, recv, *afters)
    return outs[0], outs[1], _spread(outs[2:2 + ns], shards), list(outs[2 + ns:2 + ns + n])


def relay_wait(name, send, recv, lands, after):
    n = len(lands)
    afters = _as_list(after)

    def body(*refs):
        land = refs[:n]
        send_ref, recv_ref = refs[n], refs[n + 1]
        x, y, c, chips = _place()
        for a in range(n):
            for j, (sent, got) in enumerate(_relay_blocks(land[a], c, chips)):
                cx, cy = chips[j]
                cp = pltpu.make_async_remote_copy(sent, got, send_ref.at[2 * a + j], recv_ref.at[2 * a + j],
                                                  device_id=(cx, cy, c), device_id_type=MESH)
                cp.wait_send()
                cp.wait_recv()

    outs = pl.pallas_call(
        body, name=name, out_shape=tuple(pltpu.HBM(l.shape, l.dtype) for l in lands),
        in_specs=[HBM] * n + [SEM, SEM] + [ANY] * len(afters), out_specs=[HBM] * n,
        input_output_aliases={i: i for i in range(n)},
        compiler_params=pltpu.CompilerParams(has_side_effects=EFFECT),
    )(*lands, send, recv, *afters)
    return list(outs)


def forward_halves(name, lands):
    n = len(lands)

    def body(*refs):
        had, buf = refs[:n], refs[n:2 * n]
        send, recv = refs[2 * n:]
        x, y, c, chips = _place()
        sib = (x, y, 1 - c)
        for a in range(n):
            for j, (cx, cy) in enumerate(chips):
                for sp, dp in zip(_pieces(had[a].at[2 * cx + cy, c]), _pieces(buf[a].at[2 * cx + cy, c])):
                    pltpu.make_async_remote_copy(sp, dp, send.at[3 * a + j], recv.at[3 * a + j], device_id=sib, device_id_type=MESH).start()
        for a in range(n):
            for j, (cx, cy) in enumerate(chips):
                pltpu.make_async_remote_copy(had[a].at[2 * cx + cy, c], buf[a].at[2 * cx + cy, 1 - c], send.at[3 * a + j],
                                             recv.at[3 * a + j], device_id=sib, device_id_type=MESH).wait()

    return pl.pallas_call(
        body, name=name, in_specs=[ANY] * n, out_specs=[ANY] * n, out_shape=[SDS(l.shape, l.dtype) for l in lands],
        input_output_aliases={i: i for i in range(n)},
        scratch_shapes=[pltpu.SemaphoreType.DMA((3 * n,)), pltpu.SemaphoreType.DMA((3 * n,))],
    )(*lands)


def forward_turn(name, send, recv, lands, after):
    n = len(lands)
    afters = _as_list(after)

    def body(*refs):
        had = refs[:n]
        send_ref, recv_ref = refs[n], refs[n + 1]
        fsend, frecv = refs[n + 2 + len(afters)], refs[n + 3 + len(afters)]
        buf = refs[n + 4 + len(afters):2 * n + 4 + len(afters)]
        x, y, c, chips = _place()
        sib = (x, y, 1 - c)
        for a in range(n):
            for j, (sent, got) in enumerate(_relay_blocks(had[a], c, chips)):
                cx, cy = chips[j]
                cp = pltpu.make_async_remote_copy(sent, got, send_ref.at[2 * a + j], recv_ref.at[2 * a + j],
                                                  device_id=(cx, cy, c), device_id_type=MESH)
                cp.wait_send()
                cp.wait_recv()
        for a in range(n):
            for j, (cx, cy) in enumerate(chips):
                for sp, dp in zip(_pieces(had[a].at[2 * cx + cy, c]), _pieces(buf[a].at[2 * cx + cy, c])):
                    pltpu.make_async_remote_copy(sp, dp, fsend.at[3 * a + j], frecv.at[3 * a + j], device_id=sib, device_id_type=MESH).start()

    outs = pl.pallas_call(
        body, name=name,
        out_shape=(pltpu.SemaphoreType.DMA((3 * n,)), pltpu.SemaphoreType.DMA((3 * n,)), *[pltpu.HBM(l.shape, l.dtype) for l in lands]),
        in_specs=[HBM] * n + [SEM, SEM] + [ANY] * len(afters), out_specs=(SEM, SEM, *([HBM] * n)),
        input_output_aliases={i: 2 + i for i in range(n)},
        compiler_params=pltpu.CompilerParams(has_side_effects=EFFECT),
    )(*lands, send, recv, *afters)
    return outs[0], outs[1], list(outs[2:])


def forward_wait(name, send, recv, lands, after):
    n = len(lands)
    afters = _as_list(after)

    def body(*refs):
        land = refs[:n]
        send_ref, recv_ref = refs[n], refs[n + 1]
        x, y, c, chips = _place()
        sib = (x, y, 1 - c)
        for a in range(n):
            for j, (cx, cy) in enumerate(chips):
                cp = pltpu.make_async_remote_copy(land[a].at[2 * cx + cy, c], land[a].at[2 * cx + cy, 1 - c], send_ref.at[3 * a + j],
                                                  recv_ref.at[3 * a + j], device_id=sib, device_id_type=MESH)
                cp.wait_send()
                cp.wait_recv()

    outs = pl.pallas_call(
        body, name=name, out_shape=tuple(pltpu.HBM(l.shape, l.dtype) for l in lands),
        in_specs=[HBM] * n + [SEM, SEM] + [ANY] * len(afters), out_specs=[HBM] * n,
        input_output_aliases={i: i for i in range(n)},
        compiler_params=pltpu.CompilerParams(has_side_effects=EFFECT),
    )(*lands, send, recv, *afters)
    return list(outs)


def exchange_start(name, parts):
    n = len(parts)

    def body(*refs):
        src, got = refs[:n], refs[n:2 * n]
        send, recv = refs[2 * n], refs[2 * n + 1]
        token = refs[4 * n + 2]
        x, y, c, _ = _place()
        sib = (x, y, 1 - c)
        for a in range(n):
            for sp, dp in zip(_pieces(src[a].at[1 - c]), _pieces(got[a])):
                pltpu.make_async_remote_copy(sp, dp, send.at[a], recv.at[a], device_id=sib, device_id_type=MESH).start()
        token[...] = jnp.zeros_like(token)

    lands = [pltpu.with_memory_space_constraint(lax.empty(p.shape[1:], p.dtype), pltpu.HBM) for p in parts]
    srcs = [pltpu.with_memory_space_constraint(p, pltpu.HBM) for p in parts]
    outs = pl.pallas_call(
        body, name=name,
        out_shape=(pltpu.SemaphoreType.DMA((n,)), pltpu.SemaphoreType.DMA((n,)),
                   *[pltpu.HBM(p.shape, p.dtype) for p in parts], *[pltpu.HBM(l.shape, l.dtype) for l in lands],
                   SDS((8, 128), F32)),
        in_specs=[HBM] * (2 * n), out_specs=(SEM, SEM, *([HBM] * (2 * n)), pl.BlockSpec(memory_space=pltpu.VMEM)),
        input_output_aliases={i: 2 + i for i in range(2 * n)},
        compiler_params=pltpu.CompilerParams(has_side_effects=EFFECT),
    )(*srcs, *lands)
    return outs[0], outs[1], list(outs[2:2 + n]), list(outs[2 + n:2 + 2 * n]), outs[2 + 2 * n]


def exchange_wait(name, send, recv, parts, lands, after):
    n = len(parts)
    afters = _as_list(after)

    def body(*refs):
        src, got = refs[:n], refs[n:2 * n]
        send_ref, recv_ref = refs[2 * n], refs[2 * n + 1]
        x, y, c, _ = _place()
        sib = (x, y, 1 - c)
        for a in range(n):
            cp = pltpu.make_async_remote_copy(src[a].at[1 - c], got[a], send_ref.at[a], recv_ref.at[a], device_id=sib, device_id_type=MESH)
            cp.wait_send()
            cp.wait_recv()

    outs = pl.pallas_call(
        body, name=name,
        out_shape=(*[pltpu.HBM(p.shape, p.dtype) for p in parts], *[pltpu.HBM(l.shape, l.dtype) for l in lands]),
        in_specs=[HBM] * (2 * n) + [SEM, SEM] + [ANY] * len(afters), out_specs=[HBM] * (2 * n),
        input_output_aliases={i: i for i in range(2 * n)},
        compiler_params=pltpu.CompilerParams(has_side_effects=EFFECT),
    )(*parts, *lands, send, recv, *afters)
    return list(outs[:n]), list(outs[n:])


def scatter_start(name, parts):
    n = len(parts)

    def body(*refs):
        src, land = refs[:n], refs[n:2 * n]
        send, recv = refs[2 * n], refs[2 * n + 1]
        token = refs[4 * n + 2]
        x, y, c, chips = _place()
        for a in range(n):
            for j, (cx, cy) in enumerate(chips):
                for sp, dp in zip(_pieces(src[a].at[2 * cx + cy]), _pieces(land[a].at[j])):
                    pltpu.make_async_remote_copy(sp, dp, send.at[3 * a + j], recv.at[3 * a + j],
                                                 device_id=(cx, cy, c), device_id_type=MESH).start()
        token[...] = jnp.zeros_like(token)

    lands = [pltpu.with_memory_space_constraint(lax.empty((NCHIP - 1,) + p.shape[1:], p.dtype), pltpu.HBM) for p in parts]
    srcs = [pltpu.with_memory_space_constraint(p, pltpu.HBM) for p in parts]
    outs = pl.pallas_call(
        body, name=name,
        out_shape=(pltpu.SemaphoreType.DMA((3 * n,)), pltpu.SemaphoreType.DMA((3 * n,)),
                   *[pltpu.HBM(p.shape, p.dtype) for p in parts], *[pltpu.HBM(l.shape, l.dtype) for l in lands],
                   SDS((8, 128), F32)),
        in_specs=[HBM] * (2 * n), out_specs=(SEM, SEM, *([HBM] * (2 * n)), pl.BlockSpec(memory_space=pltpu.VMEM)),
        input_output_aliases={i: 2 + i for i in range(2 * n)},
        compiler_params=pltpu.CompilerParams(has_side_effects=EFFECT),
    )(*srcs, *lands)
    return outs[0], outs[1], list(outs[2:2 + n]), list(outs[2 + n:2 + 2 * n]), outs[2 + 2 * n]


def scatter_wait(name, send, recv, parts, lands, after):
    n = len(parts)
    afters = _as_list(after)

    def body(*refs):
        src, land = refs[:n], refs[n:2 * n]
        send_ref, recv_ref = refs[2 * n], refs[2 * n + 1]
        x, y, c, chips = _place()
        for a in range(n):
            for j, (cx, cy) in enumerate(chips):
                cp = pltpu.make_async_remote_copy(src[a].at[2 * cx + cy], land[a].at[j], send_ref.at[3 * a + j], recv_ref.at[3 * a + j],
                                                  device_id=(cx, cy, c), device_id_type=MESH)
                cp.wait_send()
                cp.wait_recv()

    outs = pl.pallas_call(
        body, name=name,
        out_shape=(*[pltpu.HBM(p.shape, p.dtype) for p in parts], *[pltpu.HBM(l.shape, l.dtype) for l in lands]),
        in_specs=[HBM] * (2 * n) + [SEM, SEM] + [ANY] * len(afters), out_specs=[HBM] * (2 * n),
        input_output_aliases={i: i for i in range(2 * n)},
        compiler_params=pltpu.CompilerParams(has_side_effects=EFFECT),
    )(*parts, *lands, send, recv, *afters)
    return list(outs[:n]), list(outs[n:])


def join_start(name, halves):
    n = len(halves)

    def body(*refs):
        src, dst = refs[:n], refs[n:2 * n]
        send, recv = refs[2 * n], refs[2 * n + 1]
        token = refs[4 * n + 2]
        x, y, c, _ = _place()
        sib = (x, y, 1 - c)
        for a in range(n):
            for sp, dp in zip(_pieces(src[a]), _pieces(dst[a])):
                pltpu.make_async_remote_copy(sp, dp, send.at[a], recv.at[a], device_id=sib, device_id_type=MESH).start()
        token[...] = jnp.zeros_like(token)

    lands = [pltpu.with_memory_space_constraint(lax.empty(h.shape, h.dtype), pltpu.HBM) for h in halves]
    srcs = [pltpu.with_memory_space_constraint(h, pltpu.HBM) for h in halves]
    outs = pl.pallas_call(
        body, name=name,
        out_shape=(pltpu.SemaphoreType.DMA((n,)), pltpu.SemaphoreType.DMA((n,)),
                   *[pltpu.HBM(h.shape, h.dtype) for h in halves], *[pltpu.HBM(l.shape, l.dtype) for l in lands],
                   SDS((8, 128), F32)),
        in_specs=[HBM] * (2 * n), out_specs=(SEM, SEM, *([HBM] * (2 * n)), pl.BlockSpec(memory_space=pltpu.VMEM)),
        input_output_aliases={i: 2 + i for i in range(2 * n)},
        compiler_params=pltpu.CompilerParams(has_side_effects=EFFECT),
    )(*srcs, *lands)
    return outs[0], outs[1], list(outs[2:2 + n]), list(outs[2 + n:2 + 2 * n]), outs[2 + 2 * n]


def join_wait(name, send, recv, halves, lands, after):
    n = len(halves)
    afters = _as_list(after)

    def body(*refs):
        src, dst = refs[:n], refs[n:2 * n]
        send_ref, recv_ref = refs[2 * n], refs[2 * n + 1]
        x, y, c, _ = _place()
        sib = (x, y, 1 - c)
        for a in range(n):
            cp = pltpu.make_async_remote_copy(src[a], dst[a], send_ref.at[a], recv_ref.at[a], device_id=sib, device_id_type=MESH)
            cp.wait_send()
            cp.wait_recv()

    outs = pl.pallas_call(
        body, name=name,
        out_shape=(*[pltpu.HBM(h.shape, h.dtype) for h in halves], *[pltpu.HBM(l.shape, l.dtype) for l in lands]),
        in_specs=[HBM] * (2 * n) + [SEM, SEM] + [ANY] * len(afters), out_specs=[HBM] * (2 * n),
        input_output_aliases={i: i for i in range(2 * n)},
        compiler_params=pltpu.CompilerParams(has_side_effects=EFFECT),
    )(*halves, *lands, send, recv, *afters)
    return list(outs[:n]), list(outs[n:])


def gather_small(name, xs, reduce, after=None):
    m, ncol = xs.shape
    afters = _as_list(after)

    def body(x_ref, *rest):
        out_ref, all_ref, send, recv, lsem = rest[len(afters):]
        x, y, c, chips = _place()
        me, sib = (x, y, c), (x, y, 1 - c)

        def rows(px, py, pc):
            return all_ref.at[pl.ds((4 * px + 2 * py + pc) * m, m), :]

        def copy(k, block, to, src=None):
            return pltpu.make_async_remote_copy(rows(*block) if src is None else src, rows(*block), send.at[k], recv.at[k],
                                                device_id=to, device_id_type=MESH)

        mine = pltpu.make_async_copy(x_ref, rows(*me), lsem)
        mine.start()
        first = [copy(0, me, sib, src=x_ref)] + [copy(1 + j, me, (*chip, c), src=x_ref) for j, chip in enumerate(chips)]
        for cp in first:
            cp.start()
        passed = [copy(4 + j, (*chip, c), sib) for j, chip in enumerate(chips)]
        for j, chip in enumerate(chips):
            copy(1 + j, (*chip, c), me).wait_recv()
            passed[j].start()
        copy(0, sib, me).wait_recv()
        for j, chip in enumerate(chips):
            copy(4 + j, (*chip, 1 - c), me).wait_recv()
        for cp in first + passed:
            cp.wait_send()
        mine.wait()
        if reduce:
            s = all_ref[0:m, :]
            for dev in range(1, 8):
                s = s + all_ref[dev * m:(dev + 1) * m, :]
            out_ref[...] = s
        else:
            out_ref[...] = all_ref[...]

    vm = pl.BlockSpec(memory_space=pltpu.VMEM)
    return pl.pallas_call(
        body, name=name, in_specs=[vm] + [ANY] * len(afters), out_specs=vm,
        out_shape=SDS((m, ncol) if reduce else (8 * m, ncol), F32),
        scratch_shapes=[pltpu.VMEM((8 * m, ncol), F32), pltpu.SemaphoreType.DMA((7,)), pltpu.SemaphoreType.DMA((7,)),
                        pltpu.SemaphoreType.DMA],
    )(xs, *afters)


RELAYOUT_ROWS = 128


def weights_to_cat(name, land, own, place, other, prev=None, after=None):
    tm = RELAYOUT_ROWS
    nb = (D // 2) // tm
    extra = ([] if prev is None else [prev]) + _as_list(after)

    def half(p):
        return 1 - p[0] if other else p[0]

    def body(p_ref, g_ref, own_ref, *rest):
        o_ref = rest[len(extra)]
        nat = jnp.concatenate([jnp.where(p_ref[1] == j, own_ref[...], g_ref[j]) for j in range(NCHIP)], axis=1)
        pad = jnp.zeros((tm, NCAT - OA - 16), BF16)
        o_ref[...] = jnp.concatenate([nat[:, 3072:7168], nat[:, 7184:11280], nat[:, 0:3072], nat[:, 7168:7184], pad], axis=1)

    grid_spec = pltpu.PrefetchScalarGridSpec(
        num_scalar_prefetch=1, grid=(nb,),
        in_specs=[pl.BlockSpec((NCHIP, None, tm, IN_SHARD), lambda i, p: (0, half(p), i, 0)),
                  pl.BlockSpec((None, tm, IN_SHARD), lambda i, p: (half(p), i, 0))] + [ANY] * len(extra),
        out_specs=pl.BlockSpec((tm, NCAT), lambda i, p: (half(p) * nb + i, 0)))
    return pl.pallas_call(
        body, name=name, grid_spec=grid_spec, out_shape=PINNED((D, NCAT), BF16),
        input_output_aliases={} if prev is None else {3: 0},
        compiler_params=_cparams(40 * 1024 * 1024, ("arbitrary",)),
    )(place, land, own, *extra)


def grads_from_cat(gw_cat):
    tm = RELAYOUT_ROWS
    nb = (D // 2) // tm

    def body(c_ref, o_ref):
        cat = c_ref[...]
        nat = jnp.concatenate([cat[:, OU:OA], cat[:, OV:OGP], cat[:, OA:OA + 16], cat[:, OGP:OU]], axis=1)
        for j in range(NCHIP):
            o_ref[j] = nat[:, j * IN_SHARD:(j + 1) * IN_SHARD]

    return pl.pallas_call(
        body, name="grads_from_cat", grid=(D // tm,), in_specs=[pl.BlockSpec((tm, NCAT), lambda i: (i, 0))],
        out_specs=pl.BlockSpec((None, NCHIP, tm, IN_SHARD), lambda i: (i // nb, 0, i % nb, 0)),
        out_shape=PINNED((2, NCHIP, D // 2, IN_SHARD), BF16), compiler_params=_cparams(40 * 1024 * 1024, ("arbitrary",)),
    )(gw_cat)


def _pad_rows(a, rows):
    return jnp.concatenate([a, jnp.zeros((rows - a.shape[0],) + a.shape[1:], a.dtype)], axis=0)


def local_step(x2d, tgt, gf, g1, pool_scale, wa_pad, b_alpha, ng, g2, get_w, on_grad=None, on_settle=None, tick=None):
    emit = on_grad if on_grad is not None else (lambda group, grads: None)
    settle = on_settle if on_settle is not None else (lambda group, after: None)
    h1 = norm1(x2d, g1)
    wcat, pw = get_w("in", h1)
    pcat = mm_in(h1, wcat)
    dpool, ylin = pool_fwd(pcat, pw)
    pinned = tick("pool", ylin) if tick is not None else None
    og, o, states = gla_fwd(pcat, wa_pad, b_alpha, ng, pinned)
    w_go, w_o = get_w("mid", og)
    mixed, ygla = mm_gla_out(og, w_go, ylin, pcat, pool_scale)
    x2, h2 = mm_out(mixed, w_o, x2d, g2)
    w_up = get_w("up", h2)
    rup, act = mm_up(h2, w_up)
    w_dn = get_w("down", act)
    dx3, dx3b, g_nf, loss_row = mm_down(act, w_dn, x2, tgt, gf)

    gw_down = mm_wgrad("mm_dw_down", act, dx3b, DFF, D, (2, NCHIP, D // 2, D), (None, None, D // 2, D),
                       lambda j, i, k: (i % 2, i // 2, 0, 0), D // 2, D)
    token = emit("down", {"down": gw_down})
    dup = mm_dact(dx3b, w_dn, rup, after=token)
    token = settle("down", dup)
    dx2, dx2b, g_mlp = mm_dh2(dup, w_up, x2, dx3, g2, after=token)
    gw_up = mm_wgrad("mm_dw_up", h2, dup, D, DFF, (2, NCHIP, D // 2, D), (None, None, D // 2, D),
                     lambda j, i, k: (i, j, 0, 0), D // 2, D)
    token = emit("up", {"up": gw_up})
    dylin, dygla, dpcat, g_ps = mm_dmixed(dx2b, w_o, pcat, ylin, ygla, pool_scale, after=token)
    token = settle("up", dylin)
    gw_out = mm_wgrad("mm_dw_out", mixed, dx2b, D, D, (2, NCHIP, 256, D), (2, None, 256, D),
                      lambda j, i, k: (0, i, 0, 0), 512, D)
    do, dpcat, g_ng = mm_dog(dygla, w_go, o, pcat, ng, dpcat, after=token)
    gw_go = mm_wgrad("mm_dw_gla_out", og, dygla, D, D, (2, NCHIP, 256, D), (2, None, 256, D),
                     lambda j, i, k: (0, i, 0, 0), 512, D)
    token = emit("mix", {"out": gw_out, "gla_out": gw_go})
    dpcat, dv, g_wa, g_ba = gla_bwd(do, pcat, states, wa_pad, b_alpha, dpcat, b_alpha if token is None else token)
    token = settle("mix", dv)
    dpcat, dpw = pool_bwd(dylin, dpool, pw, lax.dynamic_update_slice(dpcat, dv, (0, OV)))
    gw_cat = mm_wgrad("mm_dw_in", h1, dpcat, D, NCAT, (D, NCAT), (1024, 1280), lambda j, i, k: (i, j), 1024, 1280, after=token)
    token = settle("in", emit("in", {"in_cat": gw_cat, "pool": dpw}))
    grad_x, g_mix = mm_dh1(dpcat, wcat, x2d, dx2, g1, after=token)
    return (loss_row[0, 0], grad_x, g_mix, g_ps, g_mlp, g_nf, g_ng, g_ba, g_wa, token,
            gw_cat, dpw, gw_go, gw_out, gw_up, gw_down)


def kernel(x, norm_mix_g, w_in, pool_w, pool_scale, w_alpha, b_alpha, gla_norm_g, w_gla_out, w_out, norm_mlp_g, w_mlp_up, w_mlp_down, norm_final_g, loss_target, m_norm_mix_g, m_w_in, m_pool_w, m_pool_scale, m_w_alpha, m_b_alpha, m_gla_norm_g, m_w_gla_out, m_w_out, m_norm_mlp_g, m_w_mlp_up, m_w_mlp_down, m_norm_final_g, v_norm_mix_g, v_w_in, v_pool_w, v_pool_scale, v_w_alpha, v_b_alpha, v_gla_norm_g, v_w_gla_out, v_w_out, v_norm_mlp_g, v_w_mlp_up, v_w_mlp_down, v_norm_final_g):
    chip = 2 * lax.axis_index("x") + lax.axis_index("y")
    chip_i = chip.astype(jnp.int32).reshape(1)
    core_i = lax.axis_index("c").astype(jnp.int32).reshape(1)
    place_i = jnp.concatenate([core_i, chip_i])
    tgt = loss_target.reshape(T, D)
    gf = norm_final_g.reshape(1, D)

    def halves(w2d):
        r, c = w2d.shape
        return lax.dynamic_update_index_in_dim(lax.empty((NCHIP, 2, r // 2, c), BF16), w2d.astype(BF16).reshape(2, r // 2, c),
                                               chip, 0)

    pool_shard = pool_w.reshape(4 * PG, PO // NCHIP)
    w_in_r = w_in.reshape(2, D // 2, IN_SHARD)
    sent = {"in": [cast_bf16("cast_w_in", w_in_r), halves(pool_shard)]}
    flight = {}

    def start(group, after=None):
        flight[group] = gather_start("gather_start_" + group, sent[group], after)

    def relay(group, after):
        send, recv, shards, lands = flight[group]
        flight[group] = relay_turn("relay_turn_" + group, send, recv, shards, lands, after)

    def fetch(group, after):
        send, recv, shards, lands = flight[group]
        lands = relay_wait("relay_wait_" + group, send, recv, lands, after)
        return forward_halves("forward_" + group, lands)

    start("in")
    m_in_f, v_in_f, w_go_f, w_o_f, w_up_f, w_dn_f, x_f, wal_f, gng_f = lax.optimization_barrier(
        (m_w_in, v_w_in, w_gla_out, w_out, w_mlp_up, w_mlp_down, x, w_alpha, gla_norm_g, flight["in"][2][0]))[:9]
    m_in_r, v_in_r = m_in_f.reshape(2, D // 2, IN_SHARD), v_in_f.reshape(2, D // 2, IN_SHARD)
    sent["mid"] = [halves(w_go_f[0]), halves(w_o_f[0])]
    relay("in", [m_in_r, v_in_r, *sent["mid"]])
    w_up_f, w_dn_f, x_f, wal_f, gng_f = lax.optimization_barrier(
        (w_up_f, w_dn_f, x_f, wal_f, gng_f, flight["in"][3][0]))[:5]
    sent["up"] = [cast_to_slot("cast_w_up", w_up_f[0].reshape(2, D // 2, D), place_i)]
    sent["down"] = [cast_to_slot("cast_w_down", w_dn_f[0].reshape(2, DFF // NCHIP // 2, D), place_i)]
    x2d = x_f.reshape(T, D)
    big = [w_in_r, w_go_f[0], w_o_f[0], w_up_f[0], w_dn_f[0], pool_shard]

    def tick(point, after):
        if point == "pool":
            relay("mid", after)
            relay("up", flight["mid"][3][0])
            start("down", flight["up"][3][0])
            return [flight["up"][3][0], flight["down"][3][0]]

    def get_w(group, after):
        if group == "in":
            after = [after, *sent["up"], *sent["down"], wa_pad]
        if group == "up":
            relay("down", after)
            send, recv, lands, shards = flight["up"]
            lands = forward_wait("forward_wait_up", send, recv, lands, flight["down"][3][0])
            return lands[0].reshape(NCHIP, D, D)
        if group == "in":
            send, recv, shards, lands = flight["in"]
            send, recv, lands = forward_turn("forward_turn_in", send, recv, lands, after)
            start("mid", lands[0])
            start("up", flight["mid"][3][0])
            wcat = weights_to_cat("weights_to_cat_mine", lands[0], shards[0], place_i, False, after=flight["up"][3][0])
            lands = forward_wait("forward_wait_in", send, recv, lands, wcat)
            wcat = weights_to_cat("weights_to_cat_sibling", lands[0], shards[0], place_i, True, prev=wcat)
            g_pool = lands[1]
            pw = jnp.concatenate([g_pool[j].reshape(4, PG, PO // NCHIP) for j in range(NCHIP)], axis=2)
            return wcat, pw
        whole = fetch(group, after)
        if group == "mid":
            send, recv, shards, lands = flight["up"]
            flight["up"] = (*forward_turn("forward_turn_up", send, recv, lands, whole[0]), shards)
            w_go, w_o, _ = lax.optimization_barrier((whole[0], whole[1], flight["up"][2][0]))
            return w_go.reshape(D, D), w_o.reshape(D, D)
        return whole[0].reshape(DFF, D)

    small_w = pack_rows("pack_small_w", [wal_f[0].reshape(4, QK),
                                         jnp.concatenate([gng_f[0].reshape(1, 512), jnp.zeros((1, 512), F32)], axis=1)], 8)
    sw_all = gather_small("gather_small_w", small_w, False).reshape(8, 8, QK)
    wa_full = jnp.concatenate([sw_all[2 * j, 0:4].reshape(16, DK) for j in range(NCHIP)], axis=1)
    ng_full = jnp.concatenate([sw_all[2 * j, 4, 0:512].reshape(HEADS, DV // NCHIP) for j in range(NCHIP)], axis=1)
    wa_pad = _pad_rows(wa_full, APAD).astype(BF16)
    ng = ng_full.reshape(1, D)

    pending = {}
    wmv = {"in": (w_in_r, m_in_r, v_in_r), "gla_out": (big[1], m_w_gla_out, v_w_gla_out), "out": (big[2], m_w_out, v_w_out),
           "up": (big[3], m_w_mlp_up, v_w_mlp_up), "down": (big[4], m_w_mlp_down, v_w_mlp_down), "pool": (big[5], m_pool_w, v_pool_w)}
    big_res = {}

    def reduce_group(group, after):
        nms, send, recv, sums, lands = pending[group]
        sums, lands = scatter_wait("scatter_wait_" + group, send, recv, sums, lands, after)
        reduced = [sum_chips("sum_chips_" + nm, a, b, chip_i) for nm, a, b in zip(nms, sums, lands)]
        send, recv, reduced, lands, token = join_start("join_start_" + group, reduced)
        pending[group] = (nms, send, recv, reduced, lands)
        return token

    def update_group(group, after):
        nms, send, recv, reduced, lands = pending[group]
        reduced, from_sib = join_wait("join_wait_" + group, send, recv, reduced, lands, after)
        for nm, g_own, g_sib in zip(nms, reduced, from_sib):
            w, m, v = wmv[nm]
            shp = (2,) + g_own.shape
            big_res[nm] = adamw_halves("adamw_" + nm, w.reshape(shp), g_own, g_sib, m.reshape(shp), v.reshape(shp), core_i)

    def on_grad(group, grads):
        if group == "in":
            gw_in = grads_from_cat(grads["in_cat"])
            gw_pool = jnp.stack([grads["pool"][:, :, j * 128:(j + 1) * 128].reshape(2, 2 * PG, 128)
                                 for j in range(NCHIP)], axis=1)
            grads = {"in": gw_in, "pool": gw_pool}
        nms, parts = list(grads.keys()), list(grads.values())
        send, recv, parts, got, token = exchange_start("exchange_start_" + group, parts)
        pending[group] = (nms, send, recv, parts, got)
        return token

    def on_settle(group, after):
        if group == "in":
            for earlier in ("down", "up", "mix"):
                after = reduce_group(earlier, after)
        nms, send, recv, parts, got = pending[group]
        parts, got = exchange_wait("exchange_wait_" + group, send, recv, parts, got, after)
        sums = [add_pairs("add_pair_" + nm, a, b, core_i) for nm, a, b in zip(nms, parts, got)]
        send, recv, sums, lands, token = scatter_start("scatter_start_" + group, sums)
        pending[group] = (nms, send, recv, sums, lands)
        if group != "in":
            return token
        for earlier in ("down", "up", "mix"):
            update_group(earlier, token)
            token = big_res[pending[earlier][0][-1]][1]
        return [big_res[nm][1] for nm in ("down", "up", "out", "gla_out")]

    (loss_local, grad_x, g_mix, g_ps, g_mlp, g_nf, g_ng, g_ba, g_wa) = local_step(
        x2d, tgt, gf, norm_mix_g, pool_scale, wa_pad, b_alpha, ng, norm_mlp_g, get_w, on_grad, on_settle, tick)[:9]
    loss = lax.psum(loss_local, ("x", "y", "c"))
    join_in_token = reduce_group("in", grad_x)

    ROWS = 16

    def wide(a, n):
        return jnp.concatenate([a.reshape(1, n), jnp.zeros((1, D - n), F32)], axis=1)

    packed = pack_rows("pack_small_g", [g_mix, g_ps, g_mlp, g_nf, g_ng, wide(g_ba, QK), g_wa[0:16].reshape(8, D)], ROWS)
    tot = gather_small("reduce_small_g", packed, True, join_in_token)
    t_wa = lax.dynamic_slice(tot[6:14].reshape(16, QK), (0, chip * DK), (16, DK))
    t_ng = lax.dynamic_slice(tot[4].reshape(HEADS, DV), (0, chip * (DV // NCHIP)), (HEADS, DV // NCHIP))

    def pack_small(nm, mix, ps, mlp, nf, ba, wa, gn, after=None):
        return pack_rows(nm, [mix.reshape(1, D), ps.reshape(1, D), mlp.reshape(1, D), nf.reshape(1, D), wide(ba, QK),
                              wa.reshape(2, D), wide(gn, 512)], ROWS, after)

    update_group("in", tot)
    sg = pack_small("pack_g", tot[0], tot[1], tot[2], tot[3], tot[5, 0:QK], t_wa, t_ng, big_res["in"][3])
    sw = pack_small("pack_w", norm_mix_g, pool_scale, norm_mlp_g, norm_final_g, b_alpha, w_alpha, gla_norm_g)
    sm = pack_small("pack_m", m_norm_mix_g, m_pool_scale, m_norm_mlp_g, m_norm_final_g, m_b_alpha, m_w_alpha, m_gla_norm_g)
    sv = pack_small("pack_v", v_norm_mix_g, v_pool_scale, v_norm_mlp_g, v_norm_final_g, v_b_alpha, v_w_alpha, v_gla_norm_g)
    small_res = adamw("adamw_small", sw, sg, sm, sv)

    def unpack(p):
        return {"norm_mix_g": p[0].reshape(1, D), "pool_scale": p[1].reshape(1, D), "norm_mlp_g": p[2].reshape(1, D),
                "norm_final_g": p[3].reshape(D), "b_alpha": p[4, 0:QK].reshape(1, QK), "w_alpha": p[5:7].reshape(1, 16, DK),
                "gla_norm_g": p[7, 0:512].reshape(1, HEADS, DV // NCHIP)}

    order = ["norm_mix_g", "w_in", "pool_w", "pool_scale", "w_alpha", "b_alpha", "gla_norm_g", "w_gla_out", "w_out",
             "norm_mlp_g", "w_mlp_up", "w_mlp_down", "norm_final_g"]
    big_key = {"w_in": ("in", w_in.shape), "pool_w": ("pool", pool_w.shape), "w_gla_out": ("gla_out", w_gla_out.shape),
               "w_out": ("out", w_out.shape), "w_mlp_up": ("up", w_mlp_up.shape), "w_mlp_down": ("down", w_mlp_down.shape)}
    result = [loss, grad_x.reshape(1, T, D)]
    for kind in range(4):
        small = unpack(small_res[kind])
        for nm in order:
            if nm in big_key:
                key, shp = big_key[nm]
                result.append(big_res[key][kind].reshape(shp))
            else:
                result.append(small[nm])
    return tuple(result)
```

```python
import itertools

import jax
import jax.numpy as jnp
from jax import lax
from jax.experimental import pallas as pl
from jax.experimental.pallas import tpu as pltpu

F32 = jnp.float32
BF16 = jnp.bfloat16
SDS = jax.ShapeDtypeStruct
PINNED = pltpu.HBM
MESH = pl.DeviceIdType.MESH
ANY = pl.BlockSpec(memory_space=pl.ANY)

T = 2048
D = 2048
DFF = 8192
NCHIP = 4
IN_WIDTH = 11280
IN_SHARD = IN_WIDTH // NCHIP
CHUNK = 64
NCHUNK = T // CHUNK
HEADS = 4
DK = 256
DV = 512
QK = HEADS * DK
EPS = 1e-6
POOL_WINDOWS = (2, 4, 8, 16)
PG = 256
PO = 512

OV, OG, OGP, OGG, OU, OQ, OKK, OA = 0, 2048, 4096, 6144, 8192, 9216, 10240, 11264
NCAT = 11520
APAD = 128

VMEM_CAP = 56 * 1024 * 1024

PIECE_BYTES = 384 * 1024

ADAM_LR, ADAM_B1, ADAM_B2, ADAM_EPS, ADAM_WD, ADAM_STEP = 0.001, 0.9, 0.999, 1e-08, 0.01, 10


def _cparams(vmem_bytes=None, sem=None):
    kw = {}
    if vmem_bytes is not None:
        kw["vmem_limit_bytes"] = int(min(max(vmem_bytes, 32 * 1024 * 1024), VMEM_CAP))
    if sem is not None:
        kw["dimension_semantics"] = sem
    return pltpu.CompilerParams(**kw)


def _nbytes(shape, dtype):
    n = 1
    for s in shape:
        if s is not None:
            n *= s
    return n * jnp.dtype(dtype).itemsize


def _sigmoid(x):
    return 0.5 * jnp.tanh(0.5 * x) + 0.5


GLA_STEP = 4
EPI_COLS = 512


def _as_list(after):
    if after is None:
        return []
    return list(after) if isinstance(after, (list, tuple)) else [after]


def _in_hbm(*arrays):
    return [pltpu.with_memory_space_constraint(a, pltpu.HBM) for a in arrays]


def matmul(name, a, b, *, a_spec, b_spec, cdims, grid, acc_shape, outs, extras=(), epi, after=None, into=None):
    nj, ni, nk = grid
    ne, no = len(extras), len(outs)
    afters = _as_list(after) + ([] if into is None else [into[0]])
    first_out = 2 + ne + len(afters)

    def body(*refs):
        a_ref, b_ref = refs[0], refs[1]
        ex = refs[2:2 + ne]
        out_refs = refs[first_out:first_out + no]
        i = pl.program_id(1)
        part = lax.dot_general(a_ref[...], b_ref[...], (cdims, ((), ())), preferred_element_type=F32)
        if nk == 1:
            epi(part, ex, out_refs, i)
        else:
            acc_ref = refs[first_out + no]
            k = pl.program_id(2)

            @pl.when(k == 0)
            def _():
                acc_ref[...] = part

            @pl.when(k > 0)
            def _():
                acc_ref[...] += part

            @pl.when(k == nk - 1)
            def _():
                epi(acc_ref[...], ex, out_refs, i)

    in_specs = [pl.BlockSpec(*a_spec), pl.BlockSpec(*b_spec)] + [pl.BlockSpec(bs, im) for _, bs, im in extras]
    in_specs += [ANY] * len(afters)
    out_specs = [pl.BlockSpec(bs, im) for _, _, bs, im in outs]
    out_shape = [PINNED(s, dt) for s, dt, _, _ in outs]
    vm = 2 * (_nbytes(a_spec[0], a.dtype) + _nbytes(b_spec[0], b.dtype))
    vm += 2 * sum(_nbytes(bs, arr.dtype) for arr, bs, _ in extras)
    vm += 2 * sum(_nbytes(bs, dt) for _, dt, bs, _ in outs)
    vm += 6 * _nbytes(acc_shape, F32)
    scratch = [pltpu.VMEM(acc_shape, F32)] if nk > 1 else []
    return pl.pallas_call(
        body, name=name, grid=grid, in_specs=in_specs, out_specs=out_specs, out_shape=out_shape,
        scratch_shapes=scratch,
        input_output_aliases={} if into is None else {first_out - 1: into[1]},
        compiler_params=_cparams(vm, ("arbitrary", "arbitrary", "arbitrary")),
    )(*_in_hbm(a, b, *[arr for arr, _, _ in extras]), *afters)


NN =((1,), (0,))
NT = ((1,), (1,))
TN = ((0,), (0,))


def _row_acc(out_ref, val, i):
    @pl.when(i == 0)
    def _():
        out_ref[...] = val

    @pl.when(i > 0)
    def _():
        out_ref[...] += val


def _rms_bwd(xn, r, dxn):
    return r * (dxn - xn * jnp.mean(dxn * xn, axis=-1, keepdims=True))


def norm1(x, g):
    tm = 256

    def body(x_ref, g_ref, h_ref):
        xv = x_ref[...]
        r = lax.rsqrt(jnp.mean(xv * xv, axis=-1, keepdims=True) + EPS)
        h_ref[...] = (xv * r * g_ref[...]).astype(BF16)

    return pl.pallas_call(
        body, name="norm1", grid=(T // tm,),
        in_specs=[pl.BlockSpec((tm, D), lambda i: (i, 0)), pl.BlockSpec((1, D), lambda i: (0, 0))],
        out_specs=pl.BlockSpec((tm, D), lambda i: (i, 0)), out_shape=PINNED((T, D), BF16),
        compiler_params=_cparams(32 * 1024 * 1024, ("arbitrary",)),
    )(*_in_hbm(x, g))


def mm_in(h1, wcat):
    tm, tn = 1024, 1280

    def epi(acc, ex, outs, i):
        outs[0][...] = acc.astype(BF16)

    return matmul("mm_in", h1, wcat, a_spec=((tm, D), lambda j, i, k: (i, 0)), b_spec=((D, tn), lambda j, i, k: (0, j)),
                  cdims=NN, grid=(NCAT // tn, T // tm, 1), acc_shape=(tm, tn),
                  outs=[((T, NCAT), BF16, (tm, tn), lambda j, i, k: (i, j))], epi=epi)[0]


def _window_sum(x, w, up):
    n = x.shape[0]
    row = lax.broadcasted_iota(jnp.int32, x.shape, 0)
    s, sh = x, 1
    while sh < w:
        if up:
            s = s + jnp.where(row < n - sh, pltpu.roll(s, n - sh, axis=0), 0.0)
        else:
            s = s + jnp.where(row >= sh, pltpu.roll(s, sh, axis=0), 0.0)
        sh *= 2
    return s


def _inv_count(shape, w):
    row = lax.broadcasted_iota(jnp.int32, shape, 0)
    return 1.0 / jnp.minimum(row + 1, w).astype(F32)


def pool_fwd(pcat, pw):
    def body(u_ref, pw_ref, d_ref, y_ref):
        for gi, w in enumerate(POOL_WINDOWS):
            ug = u_ref[:, gi * PG:(gi + 1) * PG].astype(F32)
            dg = _window_sum(ug, w, False) * _inv_count(ug.shape, w) - ug
            db = dg.astype(BF16)
            d_ref[:, gi * PG:(gi + 1) * PG] = db
            y_ref[:, gi * PO:(gi + 1) * PO] = jnp.dot(db, pw_ref[gi], preferred_element_type=F32).astype(BF16)

    return pl.pallas_call(
        body, name="pool_fwd", grid=(1,),
        in_specs=[pl.BlockSpec((T, 4 * PG), lambda i: (0, OU // (4 * PG))), pl.BlockSpec((4, PG, PO), lambda i: (0, 0, 0))],
        out_specs=[pl.BlockSpec((T, 4 * PG), lambda i: (0, 0)), pl.BlockSpec((T, D), lambda i: (0, 0))],
        out_shape=[PINNED((T, 4 * PG), BF16), PINNED((T, D), BF16)],
        compiler_params=_cparams(48 * 1024 * 1024, ("arbitrary",)),
    )(pcat, pw)


def pool_bwd(dylin, d, pw, dproj):
    assert OU % (4 * PG) == 0

    def body(dy_ref, d_ref, pw_ref, held_ref, du_ref, dpw_ref):
        for gi, w in enumerate(POOL_WINDOWS):
            dyl = dy_ref[:, gi * PO:(gi + 1) * PO]
            dd = lax.dot_general(dyl, pw_ref[gi], (NT, ((), ())), preferred_element_type=F32)
            du = _window_sum(dd * _inv_count(dd.shape, w), w, True) - dd
            du_ref[:, gi * PG:(gi + 1) * PG] = du.astype(BF16)
            dpw_ref[gi] = lax.dot_general(d_ref[:, gi * PG:(gi + 1) * PG], dyl, (TN, ((), ())),
                                          preferred_element_type=F32).astype(BF16)

    return pl.pallas_call(
        body, name="pool_bwd", grid=(1,),
        in_specs=[pl.BlockSpec((T, D), lambda i: (0, 0)), pl.BlockSpec((T, 4 * PG), lambda i: (0, 0)),
                  pl.BlockSpec((4, PG, PO), lambda i: (0, 0, 0)), ANY],
        out_specs=[pl.BlockSpec((T, 4 * PG), lambda i: (0, OU // (4 * PG))), pl.BlockSpec((4, PG, PO), lambda i: (0, 0, 0))],
        out_shape=[PINNED((T, NCAT), BF16), PINNED((4, PG, PO), BF16)],
        input_output_aliases={3: 0},
        compiler_params=_cparams(48 * 1024 * 1024, ("arbitrary",)),
    )(dylin, d, pw, dproj)


def _gate_decay(alow, wa, ba):
    a = jnp.dot(alow, wa, preferred_element_type=F32) + ba
    ls = jax.nn.log_sigmoid(a) * (1.0 / 16.0)
    r = lax.broadcasted_iota(jnp.int32, (CHUNK, CHUNK), 0)
    c = lax.broadcasted_iota(jnp.int32, (CHUNK, CHUNK), 1)
    tri = jnp.where(c <= r, 1.0, 0.0).astype(F32)
    cum = jnp.dot(tri, ls, preferred_element_type=F32, precision=lax.Precision.HIGHEST)
    last = cum[CHUNK - 1:CHUNK, :]
    return a, jnp.exp(last - cum), jnp.exp(last)


def gla_fwd(pcat, wa, ba, ng, after=None):
    afters = _as_list(after)

    def body(q_ref, k_ref, v_ref, g_ref, al_ref, wa_ref, ba_ref, ng_ref, *rest):
        og_ref, o_ref, st_ref, s_scr = rest[len(afters):]

        @pl.when(pl.program_id(0) == 0)
        def _():
            s_scr[...] = jnp.zeros_like(s_scr)

        state = [s_scr[h] for h in range(HEADS)]
        for s in range(GLA_STEP):
            rs = slice(s * CHUNK, (s + 1) * CHUNK)
            _, e, decay = _gate_decay(al_ref[rs, :], wa_ref[...], ba_ref[...])
            kd = (k_ref[rs, :].astype(F32) * e).astype(BF16)
            qs = (q_ref[rs, :].astype(F32) * (DK ** -0.5)).astype(BF16)
            for h in range(HEADS):
                ck = slice(h * DK, (h + 1) * DK)
                cv = slice(h * DV, (h + 1) * DV)
                state[h] = state[h] * decay[:, ck] + lax.dot_general(v_ref[rs, cv], kd[:, ck], (TN, ((), ())),
                                                                     preferred_element_type=F32)
                sb = state[h].astype(BF16)
                st_ref[s, h] = sb
                oh = lax.dot_general(qs[:, ck], sb, (NT, ((), ())), preferred_element_type=F32)
                o_ref[rs, cv] = oh.astype(BF16)
                on = oh * lax.rsqrt(jnp.mean(oh * oh, axis=-1, keepdims=True) + EPS) * ng_ref[:, cv]
                gv = g_ref[rs, cv].astype(F32)
                og_ref[rs, cv] = (on * (gv * _sigmoid(gv))).astype(BF16)
        for h in range(HEADS):
            s_scr[h] = state[h]

    row = lambda c: (c, 0)
    rows = GLA_STEP * CHUNK
    return pl.pallas_call(
        body, name="gla_fwd", grid=(NCHUNK // GLA_STEP,),
        in_specs=[pl.BlockSpec((rows, QK), lambda c: (c, OQ // QK)), pl.BlockSpec((rows, QK), lambda c: (c, OKK // QK)),
                  pl.BlockSpec((rows, D), lambda c: (c, OV // D)), pl.BlockSpec((rows, D), lambda c: (c, OG // D)),
                  pl.BlockSpec((rows, APAD), lambda c: (c, OA // APAD)),
                  pl.BlockSpec((APAD, QK), lambda c: (0, 0)), pl.BlockSpec((1, QK), lambda c: (0, 0)),
                  pl.BlockSpec((1, D), lambda c: (0, 0))] + [ANY] * len(afters),
        out_specs=[pl.BlockSpec((rows, D), row), pl.BlockSpec((rows, D), row),
                   pl.BlockSpec((GLA_STEP, HEADS, DV, DK), lambda c: (c, 0, 0, 0))],
        out_shape=[PINNED((T, D), BF16), PINNED((T, D), BF16), PINNED((NCHUNK, HEADS, DV, DK), BF16)],
        scratch_shapes=[pltpu.VMEM((HEADS, DV, DK), F32)],
        compiler_params=_cparams(32 * 1024 * 1024, ("arbitrary",)),
    )(*_in_hbm(pcat, pcat, pcat, pcat, pcat, wa, ba, ng), *afters)


def gla_bwd(do, pcat, states, wa, ba, dproj, after):
    tail = NCAT - OQ
    assert (OKK, OA) == (OQ + QK, OQ + 2 * QK) and OQ % tail == 0

    def body(do_ref, q_ref, k_ref, v_ref, al_ref, sc_ref, sp_ref, wa_ref, ba_ref, after_ref, held_ref,
             dp_ref, dv_ref, dwa_ref, dba_ref, ds_scr):
        i = pl.program_id(0)
        dp_ref[:, 2 * QK + APAD:] = jnp.zeros((GLA_STEP * CHUNK, tail - 2 * QK - APAD), BF16)

        @pl.when(i == 0)
        def _():
            ds_scr[...] = jnp.zeros_like(ds_scr)

        ds = [ds_scr[h] for h in range(HEADS)]
        dwa, dba = 0.0, 0.0
        for u in reversed(range(GLA_STEP)):
            rs = slice(u * CHUNK, (u + 1) * CHUNK)
            first_chunk = jnp.logical_and(i == NCHUNK // GLA_STEP - 1, u == 0)
            has_prev = jnp.where(first_chunk, 0.0, 1.0).astype(F32)
            a, e, decay = _gate_decay(al_ref[rs, :], wa_ref[...], ba_ref[...])
            kdf = k_ref[rs, :].astype(F32) * e
            kd = kdf.astype(BF16)
            qs = (q_ref[rs, :].astype(F32) * (DK ** -0.5)).astype(BF16)
            dkd_parts, ddecay_parts = [], []
            for h in range(HEADS):
                ck = slice(h * DK, (h + 1) * DK)
                cv = slice(h * DV, (h + 1) * DV)
                doh = do_ref[rs, cv]
                dsh = ds[h] + lax.dot_general(doh, qs[:, ck], (TN, ((), ())), preferred_element_type=F32)
                dsb = dsh.astype(BF16)
                dp_ref[rs, ck] = (jnp.dot(doh, sc_ref[u, h], preferred_element_type=F32) * (DK ** -0.5)).astype(BF16)
                dkd_parts.append(jnp.dot(v_ref[rs, cv], dsb, preferred_element_type=F32))
                dv_ref[rs, cv] = lax.dot_general(kd[:, ck], dsb, (NT, ((), ())), preferred_element_type=F32).astype(BF16)
                s_prev = (sp_ref[h] if u == 0 else sc_ref[u - 1, h]).astype(F32)
                ddecay_parts.append(jnp.sum(dsh * s_prev, axis=0, keepdims=True) * has_prev)
                ds[h] = dsh * decay[:, ck]
            dkd = jnp.concatenate(dkd_parts, axis=1)
            ddecay = jnp.concatenate(ddecay_parts, axis=1)
            dp_ref[rs, QK:2 * QK] = (dkd * e).astype(BF16)
            dearg = dkd * kdf
            dlast = jnp.sum(dearg, axis=0, keepdims=True) + ddecay * decay
            r = lax.broadcasted_iota(jnp.int32, (CHUNK, CHUNK), 0)
            c = lax.broadcasted_iota(jnp.int32, (CHUNK, CHUNK), 1)
            triu = jnp.where(c >= r, 1.0, 0.0).astype(F32)
            dls = dlast - jnp.dot(triu, dearg, preferred_element_type=F32, precision=lax.Precision.HIGHEST)
            da = dls * (1.0 / 16.0) * (1.0 - _sigmoid(a))
            dab = da.astype(BF16)
            dp_ref[rs, 2 * QK:2 * QK + APAD] = lax.dot_general(dab, wa_ref[...], (NT, ((), ())),
                                                               preferred_element_type=F32).astype(BF16)
            dwa = dwa + lax.dot_general(al_ref[rs, :], dab, (TN, ((), ())), preferred_element_type=F32)
            dba = dba + jnp.sum(da, axis=0, keepdims=True)
        for h in range(HEADS):
            ds_scr[h] = ds[h]

        @pl.when(i == 0)
        def _():
            dwa_ref[...] = dwa
            dba_ref[...] = dba

        @pl.when(i > 0)
        def _():
            dwa_ref[...] += dwa
            dba_ref[...] += dba

    rows = GLA_STEP * CHUNK
    rev = lambda i: NCHUNK // GLA_STEP - 1 - i
    return pl.pallas_call(
        body, name="gla_bwd", grid=(NCHUNK // GLA_STEP,),
        in_specs=[pl.BlockSpec((rows, D), lambda i: (rev(i), 0)),
                  pl.BlockSpec((rows, QK), lambda i: (rev(i), OQ // QK)), pl.BlockSpec((rows, QK), lambda i: (rev(i), OKK // QK)),
                  pl.BlockSpec((rows, D), lambda i: (rev(i), OV // D)), pl.BlockSpec((rows, APAD), lambda i: (rev(i), OA // APAD)),
                  pl.BlockSpec((GLA_STEP, HEADS, DV, DK), lambda i: (rev(i), 0, 0, 0)),
                  pl.BlockSpec((None, HEADS, DV, DK), lambda i: (jnp.maximum(rev(i) * GLA_STEP - 1, 0), 0, 0, 0)),
                  pl.BlockSpec((APAD, QK), lambda i: (0, 0)), pl.BlockSpec((1, QK), lambda i: (0, 0)), ANY, ANY],
        out_specs=[pl.BlockSpec((rows, tail), lambda i: (rev(i), OQ // tail)), pl.BlockSpec((rows, D), lambda i: (rev(i), 0)),
                   pl.BlockSpec((APAD, QK), lambda i: (0, 0)), pl.BlockSpec((1, QK), lambda i: (0, 0))],
        out_shape=[PINNED((T, NCAT), BF16), PINNED((T, D), BF16), PINNED((APAD, QK), F32), PINNED((1, QK), F32)],
        scratch_shapes=[pltpu.VMEM((HEADS, DV, DK), F32)],
        input_output_aliases={10: 0},
        compiler_params=_cparams(32 * 1024 * 1024, ("arbitrary",)),
    )(*_in_hbm(do, pcat, pcat, pcat, pcat, states, states, wa, ba), after, dproj)


TMF = 256
TMW = 512
_rowblk = ((TMF, D), lambda j, i, k: (i, 0))
_vec = ((1, D), lambda j, i, k: (0, 0))


def _full_spec(col):
    return ((TMF, D), lambda j, i, k: (i, col))


TBIG = 1024


def square_matmul(name, a, b, *, a_spec, b_spec, cdims, nk, after=None):
    def epi(acc, ex, outs, i):
        outs[0][...] = acc

    return matmul(name, a, b, a_spec=a_spec, b_spec=b_spec, cdims=cdims, grid=(D // TBIG, T // TBIG, nk),
                  acc_shape=(TBIG, TBIG), outs=[((T, D), F32, (TBIG, TBIG), lambda j, i, k: (i, j))], epi=epi,
                  after=after)[0]


def rowwise(name, y, *, extras, outs, epi):
    ne = len(extras)

    def body(*refs):
        epi(refs[0][...], refs[1:1 + ne], refs[1 + ne:], pl.program_id(1))

    in_specs = [pl.BlockSpec(*_rowblk)] + [pl.BlockSpec(bs, im) for _, bs, im in extras]
    return pl.pallas_call(
        body, name=name, grid=(1, T // TMF, 1), in_specs=in_specs,
        out_specs=[pl.BlockSpec(bs, im) for _, _, bs, im in outs], out_shape=[PINNED(s, dt) for s, dt, _, _ in outs],
        compiler_params=_cparams(40 * 1024 * 1024, ("arbitrary", "arbitrary", "arbitrary")),
    )(*_in_hbm(y, *[arr for arr, _, _ in extras]))


def mm_gla_out(og, w, ylin, pcat, pscale):
    def epi(acc, ex, outs, i):
        ylin_ref, lgp_ref, lgg_ref, ps_ref = ex
        for c0 in range(0, D, EPI_COLS):
            cs = slice(c0, c0 + EPI_COLS)
            gp = _sigmoid(lgp_ref[:, cs].astype(F32))
            gg = _sigmoid(lgg_ref[:, cs].astype(F32))
            a = acc[:, cs]
            outs[0][:, cs] = (gp * (ylin_ref[:, cs].astype(F32) * ps_ref[:, cs]) + gg * a).astype(BF16)
            outs[1][:, cs] = a.astype(BF16)

    return matmul("mm_gla_out", og, w, a_spec=_rowblk, b_spec=((D, D), lambda j, i, k: (0, 0)), cdims=NN,
                  grid=(1, T // TMF, 1), acc_shape=(TMF, D),
                  extras=[(ylin, *_rowblk), (pcat, *_full_spec(OGP // D)), (pcat, *_full_spec(OGG // D)), (pscale, *_vec)],
                  outs=[((T, D), BF16, *_rowblk), ((T, D), BF16, *_rowblk)], epi=epi)


def mm_out(mixed, w, x, g2):
    def epi(acc, ex, outs, i):
        x_ref, g_ref = ex
        x2 = x_ref[...] + acc
        r = lax.rsqrt(jnp.mean(x2 * x2, axis=-1, keepdims=True) + EPS)
        outs[0][...] = x2
        outs[1][...] = (x2 * r * g_ref[...]).astype(BF16)

    return matmul("mm_out", mixed, w, a_spec=_rowblk, b_spec=((D, D), lambda j, i, k: (0, 0)), cdims=NN,
                  grid=(1, T // TMF, 1), acc_shape=(TMF, D), extras=[(x, *_rowblk), (g2, *_vec)],
                  outs=[((T, D), F32, *_rowblk), ((T, D), BF16, *_rowblk)], epi=epi)


def mm_up(h2, wup):
    def epi(acc, ex, outs, i):
        r = jnp.maximum(acc, 0.0)
        outs[0][...] = r.astype(BF16)
        outs[1][...] = (r * r).astype(BF16)

    blk = ((TMW, D), lambda j, i, k: (i, j))
    return matmul("mm_up", h2, wup, a_spec=((TMW, D), lambda j, i, k: (i, 0)), b_spec=((None, D, D), lambda j, i, k: (j, 0, 0)),
                  cdims=NN, grid=(NCHIP, T // TMW, 1), acc_shape=(TMW, D),
                  outs=[((T, DFF), BF16, *blk), ((T, DFF), BF16, *blk)], epi=epi)


def mm_down(act, wdown, x2, tgt, gf):
    tk = 4096

    def epi(acc, ex, outs, i):
        x2_ref, t_ref, g_ref = ex
        dx_ref, dxb_ref, gnf_ref, loss_ref = outs
        x3 = x2_ref[...] + acc
        r = lax.rsqrt(jnp.mean(x3 * x3, axis=-1, keepdims=True) + EPS)
        xn = x3 * r
        err = xn * g_ref[...] - t_ref[...]
        lsum = 0.5 * jnp.sum(jnp.mean(err * err, axis=-1, keepdims=True), axis=0, keepdims=True)
        dy = err * (1.0 / D)
        _row_acc(gnf_ref, jnp.sum(dy * xn, axis=0, keepdims=True), i)
        _row_acc(loss_ref, jnp.broadcast_to(lsum, (1, 128)), i)
        dx3 = _rms_bwd(xn, r, dy * g_ref[...])
        dx_ref[...] = dx3
        dxb_ref[...] = dx3.astype(BF16)

    y = square_matmul("mm_down", act, wdown, a_spec=((TBIG, tk), lambda j, i, k: (i, k)),
                      b_spec=((tk, TBIG), lambda j, i, k: (k, j)), cdims=NN, nk=DFF // tk)
    return rowwise("rows_final", y, extras=[(x2, *_rowblk), (tgt, *_rowblk), (gf, *_vec)],
                   outs=[((T, D), F32, *_rowblk), ((T, D), BF16, *_rowblk), ((1, D), F32, *_vec),
                         ((1, 128), F32, (1, 128), lambda j, i, k: (0, 0))], epi=epi)


def mm_dact(dx3b, wdown, rup, after=None):
    def epi(acc, ex, outs, i):
        outs[0][...] = (acc * 2.0 * ex[0][...].astype(F32)).astype(BF16)

    blk = ((TMW, D), lambda j, i, k: (i, j))
    return matmul("mm_dact", dx3b, wdown, a_spec=((TMW, D), lambda j, i, k: (i, 0)), b_spec=((D, D), lambda j, i, k: (j, 0)),
                  cdims=NT, grid=(DFF // D, T // TMW, 1), acc_shape=(TMW, D), extras=[(rup, *blk)],
                  outs=[((T, DFF), BF16, *blk)], epi=epi, after=after)[0]


def mm_wgrad(name, a, b, m, n, out_shape, out_block, out_map, tm, tn, after=None):
    def epi(acc, ex, outs, i):
        outs[0][...] = acc.astype(BF16).reshape(outs[0].shape)

    return matmul(name, a, b, a_spec=((T, tm), lambda j, i, k: (0, i)), b_spec=((T, tn), lambda j, i, k: (0, j)),
                  cdims=TN, grid=(n // tn, m // tm, 1), acc_shape=(tm, tn),
                  outs=[(out_shape, BF16, out_block, out_map)], epi=epi, after=after)[0]


def mm_dh2(dup, wup, x2, dx3, g2, after=None):
    def epi(acc, ex, outs, i):
        x2_ref, dx3_ref, g_ref = ex
        x2 = x2_ref[...]
        r = lax.rsqrt(jnp.mean(x2 * x2, axis=-1, keepdims=True) + EPS)
        xn = x2 * r
        _row_acc(outs[2], jnp.sum(acc * xn, axis=0, keepdims=True), i)
        dx2 = dx3_ref[...] + _rms_bwd(xn, r, acc * g_ref[...])
        outs[0][...] = dx2
        outs[1][...] = dx2.astype(BF16)

    y = square_matmul("mm_dh2", dup, wup, a_spec=((TBIG, D), lambda j, i, k: (i, k)),
                      b_spec=((None, TBIG, D), lambda j, i, k: (k, j, 0)), cdims=NT, nk=NCHIP, after=after)
    return rowwise("rows_dh2", y, extras=[(x2, *_rowblk), (dx3, *_rowblk), (g2, *_vec)],
                   outs=[((T, D), F32, *_rowblk), ((T, D), BF16, *_rowblk), ((1, D), F32, *_vec)], epi=epi)


def mm_dmixed(dx2b, wout, pcat, ylin, ygla, pscale, after=None):
    assert OGG == OGP + D and OGP % (2 * D) == 0

    def epi(acc, ex, outs, i):
        lgp_ref, lgg_ref, ylin_ref, ygla_ref, ps_ref = ex
        dps = []
        for c0 in range(0, D, EPI_COLS):
            cs = slice(c0, c0 + EPI_COLS)
            gp = _sigmoid(lgp_ref[:, cs].astype(F32))
            gg = _sigmoid(lgg_ref[:, cs].astype(F32))
            yl = ylin_ref[:, cs].astype(F32)
            ps = ps_ref[:, cs]
            a = acc[:, cs]
            agp = a * gp
            outs[0][:, cs] = (agp * ps).astype(BF16)
            outs[1][:, cs] = (a * gg).astype(BF16)
            outs[2][:, cs] = (agp * (yl * ps) * (1.0 - gp)).astype(BF16)
            outs[2][:, D + c0:D + c0 + EPI_COLS] = (a * ygla_ref[:, cs].astype(F32) * gg * (1.0 - gg)).astype(BF16)
            dps.append(jnp.sum(agp * yl, axis=0, keepdims=True))
        _row_acc(outs[3], jnp.concatenate(dps, axis=1), i)

    return matmul("mm_dmixed", dx2b, wout, a_spec=_rowblk, b_spec=((D, D), lambda j, i, k: (0, 0)), cdims=NT,
                  grid=(1, T // TMF, 1), acc_shape=(TMF, D),
                  extras=[(pcat, *_full_spec(OGP // D)), (pcat, *_full_spec(OGG // D)), (ylin, *_rowblk), (ygla, *_rowblk),
                          (pscale, *_vec)],
                  outs=[((T, D), BF16, *_rowblk)] * 2
                       + [((T, NCAT), BF16, (TMF, 2 * D), lambda j, i, k: (i, OGP // (2 * D))), ((1, D), F32, *_vec)],
                  epi=epi, after=after)


def mm_dog(dygla, wgo, o, pcat, ng, dproj, after=None):
    def epi(acc, ex, outs, i):
        o_ref, g_ref, ng_ref = ex
        do_ref, dg_ref, gng_ref = outs
        gparts = []
        for h in range(HEADS):
            cv = slice(h * DV, (h + 1) * DV)
            oh = o_ref[:, cv].astype(F32)
            r = lax.rsqrt(jnp.mean(oh * oh, axis=-1, keepdims=True) + EPS)
            on = oh * r
            gv = g_ref[:, cv].astype(F32)
            sg = _sigmoid(gv)
            a = acc[:, cv]
            dgain = a * (gv * sg)
            gparts.append(jnp.sum(dgain * on, axis=0, keepdims=True))
            ngh = ng_ref[:, cv]
            do_ref[:, cv] = _rms_bwd(on, r, dgain * ngh).astype(BF16)
            dg_ref[:, cv] = (a * (on * ngh) * (sg * (1.0 + gv * (1.0 - sg)))).astype(BF16)
        _row_acc(gng_ref, jnp.concatenate(gparts, axis=1), i)

    return matmul("mm_dog", dygla, wgo, a_spec=_rowblk, b_spec=((D, D), lambda j, i, k: (0, 0)), cdims=NT,
                  grid=(1, T // TMF, 1), acc_shape=(TMF, D),
                  extras=[(o, *_rowblk), (pcat, *_full_spec(OG // D)), (ng, *_vec)],
                  outs=[((T, D), BF16, *_rowblk), ((T, NCAT), BF16, *_full_spec(OG // D)), ((1, D), F32, *_vec)],
                  epi=epi, after=after, into=(dproj, 1))


def mm_dh1(dpcat, wcat, x, dx2, g1, after=None):
    tk = 3840

    def epi(acc, ex, outs, i):
        x_ref, dx2_ref, g_ref = ex
        xv = x_ref[...]
        r = lax.rsqrt(jnp.mean(xv * xv, axis=-1, keepdims=True) + EPS)
        xn = xv * r
        _row_acc(outs[1], jnp.sum(acc * xn, axis=0, keepdims=True), i)
        outs[0][...] = dx2_ref[...] + _rms_bwd(xn, r, acc * g_ref[...])

    y = square_matmul("mm_dh1", dpcat, wcat, a_spec=((TBIG, tk), lambda j, i, k: (i, k)),
                      b_spec=((TBIG, tk), lambda j, i, k: (j, k)), cdims=NT, nk=NCAT // tk, after=after)
    return rowwise("rows_dh1", y, extras=[(x, *_rowblk), (dx2, *_rowblk), (g1, *_vec)],
                   outs=[((T, D), F32, *_rowblk), ((1, D), F32, *_vec)], epi=epi)


def _tile_rows(rows, cols, n_arrays):
    tm = rows
    while tm % 32 == 0 and 2 * n_arrays * tm * cols * 4 > 36 * 1024 * 1024:
        tm //= 2
    return tm


def add_pairs(name, parts, theirs, core):
    _, _, r, c = parts.shape
    tm = _tile_rows(r, c, 3)

    def body(core_ref, a_ref, b_ref, o_ref):
        o_ref[...] = (a_ref[...].astype(F32) + b_ref[...].astype(F32)).astype(BF16)

    spec = pl.BlockSpec((None, tm, c), lambda j, i, core_ref: (j, i, 0))
    grid_spec = pltpu.PrefetchScalarGridSpec(
        num_scalar_prefetch=1, grid=(NCHIP, r // tm),
        in_specs=[pl.BlockSpec((None, None, tm, c), lambda j, i, core_ref: (core_ref[0], j, i, 0)), spec], out_specs=spec)
    return pl.pallas_call(body, name=name, grid_spec=grid_spec, out_shape=PINNED((NCHIP, r, c), BF16),
                          compiler_params=_cparams(40 * 1024 * 1024, ("arbitrary", "arbitrary")))(core, *_in_hbm(parts, theirs))


def sum_chips(name, sums, landed, chip):
    _, r, c = sums.shape
    tm = _tile_rows(r, c, 4)

    def body(chip_ref, own_ref, l_ref, o_ref):
        s = own_ref[...].astype(F32)
        for t in range(NCHIP - 1):
            s = s + l_ref[t].astype(F32)
        o_ref[...] = s

    grid_spec = pltpu.PrefetchScalarGridSpec(
        num_scalar_prefetch=1, grid=(r // tm,),
        in_specs=[pl.BlockSpec((None, tm, c), lambda i, chip_ref: (chip_ref[0], i, 0)),
                  pl.BlockSpec((NCHIP - 1, tm, c), lambda i, chip_ref: (0, i, 0))],
        out_specs=pl.BlockSpec((tm, c), lambda i, chip_ref: (i, 0)))
    return pl.pallas_call(body, name=name, grid_spec=grid_spec, out_shape=PINNED((r, c), F32),
                          compiler_params=_cparams(40 * 1024 * 1024, ("arbitrary",)))(chip, *_in_hbm(sums, landed))


def _adamw_math(wv, gv, mv, vv):
    mn = ADAM_B1 * mv + (1.0 - ADAM_B1) * gv
    vn = ADAM_B2 * vv + (1.0 - ADAM_B2) * (gv * gv)
    mh = mn / (1.0 - ADAM_B1 ** ADAM_STEP)
    vh = vn / (1.0 - ADAM_B2 ** ADAM_STEP)
    return -ADAM_LR * (mh / (jnp.sqrt(vh) + ADAM_EPS) + ADAM_WD * wv), mn, vn


def adamw(name, w, g, m, v):
    def body(w_ref, g_ref, m_ref, v_ref, go_ref, d_ref, mo_ref, vo_ref):
        gv = g_ref[...]
        go_ref[...] = gv
        d_ref[...], mo_ref[...], vo_ref[...] = _adamw_math(w_ref[...], gv, m_ref[...], v_ref[...])

    return pl.pallas_call(body, name=name, out_shape=[SDS(w.shape, F32)] * 4)(w, g, m, v)


def adamw_halves(name, w, g_own, g_sib, m, v, core):
    _, r, c = w.shape
    tm = _tile_rows(r, c, 10)

    def body(core_ref, w_ref, go_ref, gs_ref, m_ref, v_ref, g_out, d_out, m_out, v_out):
        gv = jnp.where(pl.program_id(0) == core_ref[0], go_ref[...], gs_ref[...])
        g_out[...] = gv
        d_out[...], m_out[...], v_out[...] = _adamw_math(w_ref[...], gv, m_ref[...], v_ref[...])

    full = pl.BlockSpec((None, tm, c), lambda h, i, core_ref: (h, i, 0))
    own = pl.BlockSpec((tm, c), lambda h, i, core_ref: (jnp.where(h == core_ref[0], i, 0), 0))
    sib = pl.BlockSpec((tm, c), lambda h, i, core_ref: (jnp.where(h == core_ref[0], 0, i), 0))
    grid_spec = pltpu.PrefetchScalarGridSpec(num_scalar_prefetch=1, grid=(2, r // tm),
                                             in_specs=[full, own, sib, full, full], out_specs=[full] * 4)
    return pl.pallas_call(body, name=name, grid_spec=grid_spec, out_shape=[SDS(w.shape, F32)] * 4,
                          compiler_params=_cparams(48 * 1024 * 1024, ("arbitrary", "arbitrary")))(core, *_in_hbm(w, g_own, g_sib, m, v))


def cast_bf16(name, w):
    _, r, c = w.shape
    tm = _tile_rows(r, c, 2)

    def body(w_ref, o_ref):
        o_ref[...] = w_ref[...].astype(BF16)

    spec = pl.BlockSpec((None, tm, c), lambda h, i: (h, i, 0))
    return pl.pallas_call(body, name=name, grid=(2, r // tm), in_specs=[spec], out_specs=spec, out_shape=PINNED(w.shape, BF16),
                          compiler_params=_cparams(40 * 1024 * 1024, ("arbitrary", "arbitrary")))(w)


def cast_to_slot(name, w, place):
    _, r, c = w.shape
    tm = _tile_rows(r, c, 2)

    def body(p_ref, w_ref, o_ref):
        o_ref[...] = w_ref[...].astype(BF16)

    grid_spec = pltpu.PrefetchScalarGridSpec(
        num_scalar_prefetch=1, grid=(2, r // tm), in_specs=[pl.BlockSpec((None, tm, c), lambda h, i, p: (h, i, 0))],
        out_specs=pl.BlockSpec((None, None, tm, c), lambda h, i, p: (p[1], h, i, 0)))
    return pl.pallas_call(body, name=name, grid_spec=grid_spec, out_shape=PINNED((NCHIP, 2, r, c), BF16),
                          compiler_params=_cparams(40 * 1024 * 1024, ("arbitrary", "arbitrary")))(place, *_in_hbm(w))


def pack_rows(name, parts, rows, after=None):
    width = parts[0].shape[1]
    n = len(parts)
    afters = _as_list(after)

    def body(*refs):
        out_ref = refs[n + len(afters)]
        out_ref[...] = jnp.zeros_like(out_ref)
        off = 0
        for p in refs[:n]:
            out_ref[off:off + p.shape[0], :] = p[...]
            off += p.shape[0]

    vm = pl.BlockSpec(memory_space=pltpu.VMEM)
    return pl.pallas_call(body, name=name, in_specs=[vm] * n + [ANY] * len(afters), out_specs=vm,
                          out_shape=SDS((rows, width), F32))(*parts, *afters)


def _place():
    x, y, c = lax.axis_index("x"), lax.axis_index("y"), lax.axis_index("c")
    chips = [(1 - x, y), (x, 1 - y), (1 - x, 1 - y)]
    return x, y, c, chips


def _row_split(shape, dtype):
    r, c = shape
    n = 1
    while r % (2 * n) == 0 and (r // (2 * n)) % 16 == 0 and (r // n) * c * jnp.dtype(dtype).itemsize > PIECE_BYTES:
        n *= 2
    return [pl.ds(s * (r // n), r // n) for s in range(n)]


def _pieces(ref):
    *lead, r, c = ref.shape
    split = _row_split((r, c), ref.dtype)
    return [ref.at[(*idx, s)] for idx in itertools.product(*[range(d) for d in lead]) for s in split]


HBM = pl.BlockSpec(memory_space=pltpu.HBM)
SEM = pl.BlockSpec(memory_space=pltpu.SEMAPHORE)
EFFECT = pltpu.SideEffectType.DATAFLOW_SIDE_EFFECTING


def _own_half(shard_refs, land, a, me, c):
    return land[a].at[me, c] if shard_refs[a] is None else shard_refs[a].at[c]


def _spread(refs, shards):
    it = iter(refs)
    return [None if s is None else next(it) for s in shards]


def gather_start(name, items, after=None):
    n = len(items)
    shards = [s if s.ndim == 3 else None for s in items]
    given = [s for s in shards if s is not None]
    ns = len(given)
    afters = _as_list(after)

    def body(*refs):
        src, land = _spread(refs[:ns], shards), refs[ns:ns + n]
        send, recv = refs[ns + n + len(afters)], refs[ns + n + len(afters) + 1]
        x, y, c, chips = _place()
        me = 2 * x + y
        for a in range(n):
            for j, (cx, cy) in enumerate(chips[:2]):
                for sp, dp in zip(_pieces(_own_half(src, land, a, me, c)), _pieces(land[a].at[me, c])):
                    pltpu.make_async_remote_copy(sp, dp, send.at[2 * a + j], recv.at[2 * a + j],
                                                 device_id=(cx, cy, c), device_id_type=MESH).start()

    lands = [pltpu.with_memory_space_constraint(lax.empty((NCHIP,) + s.shape, s.dtype) if s.ndim == 3 else s, pltpu.HBM)
             for s in items]
    srcs = [pltpu.with_memory_space_constraint(s, pltpu.HBM) for s in given]
    outs = pl.pallas_call(
        body, name=name,
        out_shape=(pltpu.SemaphoreType.DMA((2 * n,)), pltpu.SemaphoreType.DMA((2 * n,)),
                   *[pltpu.HBM(s.shape, s.dtype) for s in given], *[pltpu.HBM(l.shape, l.dtype) for l in lands]),
        in_specs=[HBM] * (ns + n) + [ANY] * len(afters), out_specs=(SEM, SEM, *([HBM] * (ns + n))),
        input_output_aliases={i: 2 + i for i in range(ns + n)},
        compiler_params=pltpu.CompilerParams(has_side_effects=EFFECT),
    )(*srcs, *lands, *afters)
    return outs[0], outs[1], _spread(outs[2:2 + ns], shards), list(outs[2 + ns:2 + ns + n])


def _relay_blocks(land, c, chips):
    (xx, xy), (yx, yy), (dx, dy) = chips
    rows = land.shape[2] // 2
    upper, lower = pl.ds(0, rows), pl.ds(rows, rows)
    return [(land.at[2 * yx + yy, c, lower], land.at[2 * dx + dy, c, lower]),
            (land.at[2 * xx + xy, c, upper], land.at[2 * dx + dy, c, upper])]


def relay_turn(name, send, recv, shards, lands, after):
    n = len(lands)
    given = [s for s in shards if s is not None]
    ns = len(given)
    afters = _as_list(after)

    def body(*refs):
        src, had = _spread(refs[:ns], shards), refs[ns:ns + n]
        send_ref, recv_ref = refs[ns + n], refs[ns + n + 1]
        rsend, rrecv = refs[ns + n + 2 + len(afters)], refs[ns + n + 3 + len(afters)]
        land = refs[2 * ns + n + 4 + len(afters):2 * ns + 2 * n + 4 + len(afters)]
        x, y, c, chips = _place()
        me = 2 * x + y
        for a in range(n):
            for j, (cx, cy) in enumerate(chips[:2]):
                cp = pltpu.make_async_remote_copy(_own_half(src, had, a, me, c), had[a].at[2 * cx + cy, c],
                                                  send_ref.at[2 * a + j], recv_ref.at[2 * a + j],
                                                  device_id=(cx, cy, c), device_id_type=MESH)
                cp.wait_send()
                cp.wait_recv()
        for a in range(n):
            for j, ((sent, _), (dst, _)) in enumerate(zip(_relay_blocks(had[a], c, chips), _relay_blocks(land[a], c, chips))):
                cx, cy = chips[j]
                for sp, dp in zip(_pieces(sent), _pieces(dst)):
                    pltpu.make_async_remote_copy(sp, dp, rsend.at[2 * a + j], rrecv.at[2 * a + j],
                                                 device_id=(cx, cy, c), device_id_type=MESH).start()

    outs = pl.pallas_call(
        body, name=name,
        out_shape=(pltpu.SemaphoreType.DMA((2 * n,)), pltpu.SemaphoreType.DMA((2 * n,)),
                   *[pltpu.HBM(s.shape, s.dtype) for s in given], *[pltpu.HBM(l.shape, l.dtype) for l in lands]),
        in_specs=[HBM] * (ns + n) + [SEM, SEM] + [ANY] * len(afters), out_specs=(SEM, SEM, *([HBM] * (ns + n))),
        input_output_aliases={i: 2 + i for i in range(ns + n)},
        compiler_params=pltpu.CompilerParams(has_side_effects=EFFECT),
    )(*given, *lands, send, recv, *afters)
    return outs[0], outs[1], _spread(outs[2:2 + ns], shards), list(outs[2 + ns:2 + ns + n])


def relay_wait(name, send, recv, lands, after):
    n = len(lands)
    afters = _as_list(after)

    def body(*refs):
        land = refs[:n]
        send_ref, recv_ref = refs[n], refs[n + 1]
        x, y, c, chips = _place()
        for a in range(n):
            for j, (sent, got) in enumerate(_relay_blocks(land[a], c, chips)):
                cx, cy = chips[j]
                cp = pltpu.make_async_remote_copy(sent, got, send_ref.at[2 * a + j], recv_ref.at[2 * a + j],
                                                  device_id=(cx, cy, c), device_id_type=MESH)
                cp.wait_send()
                cp.wait_recv()

    outs = pl.pallas_call(
        body, name=name, out_shape=tuple(pltpu.HBM(l.shape, l.dtype) for l in lands),
        in_specs=[HBM] * n + [SEM, SEM] + [ANY] * len(afters), out_specs=[HBM] * n,
        input_output_aliases={i: i for i in range(n)},
        compiler_params=pltpu.CompilerParams(has_side_effects=EFFECT),
    )(*lands, send, recv, *afters)
    return list(outs)


def forward_halves(name, lands):
    n = len(lands)

    def body(*refs):
        had, buf = refs[:n], refs[n:2 * n]
        send, recv = refs[2 * n:]
        x, y, c, chips = _place()
        sib = (x, y, 1 - c)
        for a in range(n):
            for j, (cx, cy) in enumerate(chips):
                for sp, dp in zip(_pieces(had[a].at[2 * cx + cy, c]), _pieces(buf[a].at[2 * cx + cy, c])):
                    pltpu.make_async_remote_copy(sp, dp, send.at[3 * a + j], recv.at[3 * a + j], device_id=sib, device_id_type=MESH).start()
        for a in range(n):
            for j, (cx, cy) in enumerate(chips):
                pltpu.make_async_remote_copy(had[a].at[2 * cx + cy, c], buf[a].at[2 * cx + cy, 1 - c], send.at[3 * a + j],
                                             recv.at[3 * a + j], device_id=sib, device_id_type=MESH).wait()

    return pl.pallas_call(
        body, name=name, in_specs=[ANY] * n, out_specs=[ANY] * n, out_shape=[SDS(l.shape, l.dtype) for l in lands],
        input_output_aliases={i: i for i in range(n)},
        scratch_shapes=[pltpu.SemaphoreType.DMA((3 * n,)), pltpu.SemaphoreType.DMA((3 * n,))],
    )(*lands)


def forward_turn(name, send, recv, lands, after):
    n = len(lands)
    afters = _as_list(after)

    def body(*refs):
        had = refs[:n]
        send_ref, recv_ref = refs[n], refs[n + 1]
        fsend, frecv = refs[n + 2 + len(afters)], refs[n + 3 + len(afters)]
        buf = refs[n + 4 + len(afters):2 * n + 4 + len(afters)]
        x, y, c, chips = _place()
        sib = (x, y, 1 - c)
        for a in range(n):
            for j, (sent, got) in enumerate(_relay_blocks(had[a], c, chips)):
                cx, cy = chips[j]
                cp = pltpu.make_async_remote_copy(sent, got, send_ref.at[2 * a + j], recv_ref.at[2 * a + j],
                                                  device_id=(cx, cy, c), device_id_type=MESH)
                cp.wait_send()
                cp.wait_recv()
        for a in range(n):
            for j, (cx, cy) in enumerate(chips):
                for sp, dp in zip(_pieces(had[a].at[2 * cx + cy, c]), _pieces(buf[a].at[2 * cx + cy, c])):
                    pltpu.make_async_remote_copy(sp, dp, fsend.at[3 * a + j], frecv.at[3 * a + j], device_id=sib, device_id_type=MESH).start()

    outs = pl.pallas_call(
        body, name=name,
        out_shape=(pltpu.SemaphoreType.DMA((3 * n,)), pltpu.SemaphoreType.DMA((3 * n,)), *[pltpu.HBM(l.shape, l.dtype) for l in lands]),
        in_specs=[HBM] * n + [SEM, SEM] + [ANY] * len(afters), out_specs=(SEM, SEM, *([HBM] * n)),
        input_output_aliases={i: 2 + i for i in range(n)},
        compiler_params=pltpu.CompilerParams(has_side_effects=EFFECT),
    )(*lands, send, recv, *afters)
    return outs[0], outs[1], list(outs[2:])


def forward_wait(name, send, recv, lands, after):
    n = len(lands)
    afters = _as_list(after)

    def body(*refs):
        land = refs[:n]
        send_ref, recv_ref = refs[n], refs[n + 1]
        x, y, c, chips = _place()
        sib = (x, y, 1 - c)
        for a in range(n):
            for j, (cx, cy) in enumerate(chips):
                cp = pltpu.make_async_remote_copy(land[a].at[2 * cx + cy, c], land[a].at[2 * cx + cy, 1 - c], send_ref.at[3 * a + j],
                                                  recv_ref.at[3 * a + j], device_id=sib, device_id_type=MESH)
                cp.wait_send()
                cp.wait_recv()

    outs = pl.pallas_call(
        body, name=name, out_shape=tuple(pltpu.HBM(l.shape, l.dtype) for l in lands),
        in_specs=[HBM] * n + [SEM, SEM] + [ANY] * len(afters), out_specs=[HBM] * n,
        input_output_aliases={i: i for i in range(n)},
        compiler_params=pltpu.CompilerParams(has_side_effects=EFFECT),
    )(*lands, send, recv, *afters)
    return list(outs)


def exchange_start(name, parts):
    n = len(parts)

    def body(*refs):
        src, got = refs[:n], refs[n:2 * n]
        send, recv = refs[2 * n], refs[2 * n + 1]
        token = refs[4 * n + 2]
        x, y, c, _ = _place()
        sib = (x, y, 1 - c)
        for a in range(n):
            for sp, dp in zip(_pieces(src[a].at[1 - c]), _pieces(got[a])):
                pltpu.make_async_remote_copy(sp, dp, send.at[a], recv.at[a], device_id=sib, device_id_type=MESH).start()
        token[...] = jnp.zeros_like(token)

    lands = [pltpu.with_memory_space_constraint(lax.empty(p.shape[1:], p.dtype), pltpu.HBM) for p in parts]
    srcs = [pltpu.with_memory_space_constraint(p, pltpu.HBM) for p in parts]
    outs = pl.pallas_call(
        body, name=name,
        out_shape=(pltpu.SemaphoreType.DMA((n,)), pltpu.SemaphoreType.DMA((n,)),
                   *[pltpu.HBM(p.shape, p.dtype) for p in parts], *[pltpu.HBM(l.shape, l.dtype) for l in lands],
                   SDS((8, 128), F32)),
        in_specs=[HBM] * (2 * n), out_specs=(SEM, SEM, *([HBM] * (2 * n)), pl.BlockSpec(memory_space=pltpu.VMEM)),
        input_output_aliases={i: 2 + i for i in range(2 * n)},
        compiler_params=pltpu.CompilerParams(has_side_effects=EFFECT),
    )(*srcs, *lands)
    return outs[0], outs[1], list(outs[2:2 + n]), list(outs[2 + n:2 + 2 * n]), outs[2 + 2 * n]


def exchange_wait(name, send, recv, parts, lands, after):
    n = len(parts)
    afters = _as_list(after)

    def body(*refs):
        src, got = refs[:n], refs[n:2 * n]
        send_ref, recv_ref = refs[2 * n], refs[2 * n + 1]
        x, y, c, _ = _place()
        sib = (x, y, 1 - c)
        for a in range(n):
            cp = pltpu.make_async_remote_copy(src[a].at[1 - c], got[a], send_ref.at[a], recv_ref.at[a], device_id=sib, device_id_type=MESH)
            cp.wait_send()
            cp.wait_recv()

    outs = pl.pallas_call(
        body, name=name,
        out_shape=(*[pltpu.HBM(p.shape, p.dtype) for p in parts], *[pltpu.HBM(l.shape, l.dtype) for l in lands]),
        in_specs=[HBM] * (2 * n) + [SEM, SEM] + [ANY] * len(afters), out_specs=[HBM] * (2 * n),
        input_output_aliases={i: i for i in range(2 * n)},
        compiler_params=pltpu.CompilerParams(has_side_effects=EFFECT),
    )(*parts, *lands, send, recv, *afters)
    return list(outs[:n]), list(outs[n:])


def scatter_start(name, parts):
    n = len(parts)

    def body(*refs):
        src, land = refs[:n], refs[n:2 * n]
        send, recv = refs[2 * n], refs[2 * n + 1]
        token = refs[4 * n + 2]
        x, y, c, chips = _place()
        for a in range(n):
            for j, (cx, cy) in enumerate(chips):
                for sp, dp in zip(_pieces(src[a].at[2 * cx + cy]), _pieces(land[a].at[j])):
                    pltpu.make_async_remote_copy(sp, dp, send.at[3 * a + j], recv.at[3 * a + j],
                                                 device_id=(cx, cy, c), device_id_type=MESH).start()
        token[...] = jnp.zeros_like(token)

    lands = [pltpu.with_memory_space_constraint(lax.empty((NCHIP - 1,) + p.shape[1:], p.dtype), pltpu.HBM) for p in parts]
    srcs = [pltpu.with_memory_space_constraint(p, pltpu.HBM) for p in parts]
    outs = pl.pallas_call(
        body, name=name,
        out_shape=(pltpu.SemaphoreType.DMA((3 * n,)), pltpu.SemaphoreType.DMA((3 * n,)),
                   *[pltpu.HBM(p.shape, p.dtype) for p in parts], *[pltpu.HBM(l.shape, l.dtype) for l in lands],
                   SDS((8, 128), F32)),
        in_specs=[HBM] * (2 * n), out_specs=(SEM, SEM, *([HBM] * (2 * n)), pl.BlockSpec(memory_space=pltpu.VMEM)),
        input_output_aliases={i: 2 + i for i in range(2 * n)},
        compiler_params=pltpu.CompilerParams(has_side_effects=EFFECT),
    )(*srcs, *lands)
    return outs[0], outs[1], list(outs[2:2 + n]), list(outs[2 + n:2 + 2 * n]), outs[2 + 2 * n]


def scatter_wait(name, send, recv, parts, lands, after):
    n = len(parts)
    afters = _as_list(after)

    def body(*refs):
        src, land = refs[:n], refs[n:2 * n]
        send_ref, recv_ref = refs[2 * n], refs[2 * n + 1]
        x, y, c, chips = _place()
        for a in range(n):
            for j, (cx, cy) in enumerate(chips):
                cp = pltpu.make_async_remote_copy(src[a].at[2 * cx + cy], land[a].at[j], send_ref.at[3 * a + j], recv_ref.at[3 * a + j],
                                                  device_id=(cx, cy, c), device_id_type=MESH)
                cp.wait_send()
                cp.wait_recv()

    outs = pl.pallas_call(
        body, name=name,
        out_shape=(*[pltpu.HBM(p.shape, p.dtype) for p in parts], *[pltpu.HBM(l.shape, l.dtype) for l in lands]),
        in_specs=[HBM] * (2 * n) + [SEM, SEM] + [ANY] * len(afters), out_specs=[HBM] * (2 * n),
        input_output_aliases={i: i for i in range(2 * n)},
        compiler_params=pltpu.CompilerParams(has_side_effects=EFFECT),
    )(*parts, *lands, send, recv, *afters)
    return list(outs[:n]), list(outs[n:])


def join_start(name, halves):
    n = len(halves)

    def body(*refs):
        src, dst = refs[:n], refs[n:2 * n]
        send, recv = refs[2 * n], refs[2 * n + 1]
        token = refs[4 * n + 2]
        x, y, c, _ = _place()
        sib = (x, y, 1 - c)
        for a in range(n):
            for sp, dp in zip(_pieces(src[a]), _pieces(dst[a])):
                pltpu.make_async_remote_copy(sp, dp, send.at[a], recv.at[a], device_id=sib, device_id_type=MESH).start()
        token[...] = jnp.zeros_like(token)

    lands = [pltpu.with_memory_space_constraint(lax.empty(h.shape, h.dtype), pltpu.HBM) for h in halves]
    srcs = [pltpu.with_memory_space_constraint(h, pltpu.HBM) for h in halves]
    outs = pl.pallas_call(
        body, name=name,
        out_shape=(pltpu.SemaphoreType.DMA((n,)), pltpu.SemaphoreType.DMA((n,)),
                   *[pltpu.HBM(h.shape, h.dtype) for h in halves], *[pltpu.HBM(l.shape, l.dtype) for l in lands],
                   SDS((8, 128), F32)),
        in_specs=[HBM] * (2 * n), out_specs=(SEM, SEM, *([HBM] * (2 * n)), pl.BlockSpec(memory_space=pltpu.VMEM)),
        input_output_aliases={i: 2 + i for i in range(2 * n)},
        compiler_params=pltpu.CompilerParams(has_side_effects=EFFECT),
    )(*srcs, *lands)
    return outs[0], outs[1], list(outs[2:2 + n]), list(outs[2 + n:2 + 2 * n]), outs[2 + 2 * n]


def join_wait(name, send, recv, halves, lands, after):
    n = len(halves)
    afters = _as_list(after)

    def body(*refs):
        src, dst = refs[:n], refs[n:2 * n]
        send_ref, recv_ref = refs[2 * n], refs[2 * n + 1]
        x, y, c, _ = _place()
        sib = (x, y, 1 - c)
        for a in range(n):
            cp = pltpu.make_async_remote_copy(src[a], dst[a], send_ref.at[a], recv_ref.at[a], device_id=sib, device_id_type=MESH)
            cp.wait_send()
            cp.wait_recv()

    outs = pl.pallas_call(
        body, name=name,
        out_shape=(*[pltpu.HBM(h.shape, h.dtype) for h in halves], *[pltpu.HBM(l.shape, l.dtype) for l in lands]),
        in_specs=[HBM] * (2 * n) + [SEM, SEM] + [ANY] * len(afters), out_specs=[HBM] * (2 * n),
        input_output_aliases={i: i for i in range(2 * n)},
        compiler_params=pltpu.CompilerParams(has_side_effects=EFFECT),
    )(*halves, *lands, send, recv, *afters)
    return list(outs[:n]), list(outs[n:])


def gather_small(name, xs, reduce, after=None):
    m, ncol = xs.shape
    afters = _as_list(after)

    def body(x_ref, *rest):
        out_ref, all_ref, send, recv, lsem = rest[len(afters):]
        x, y, c, chips = _place()
        me, sib = (x, y, c), (x, y, 1 - c)

        def rows(px, py, pc):
            return all_ref.at[pl.ds((4 * px + 2 * py + pc) * m, m), :]

        def copy(k, block, to, src=None):
            return pltpu.make_async_remote_copy(rows(*block) if src is None else src, rows(*block), send.at[k], recv.at[k],
                                                device_id=to, device_id_type=MESH)

        mine = pltpu.make_async_copy(x_ref, rows(*me), lsem)
        mine.start()
        first = [copy(0, me, sib, src=x_ref)] + [copy(1 + j, me, (*chip, c), src=x_ref) for j, chip in enumerate(chips)]
        for cp in first:
            cp.start()
        passed = [copy(4 + j, (*chip, c), sib) for j, chip in enumerate(chips)]
        for j, chip in enumerate(chips):
            copy(1 + j, (*chip, c), me).wait_recv()
            passed[j].start()
        copy(0, sib, me).wait_recv()
        for j, chip in enumerate(chips):
            copy(4 + j, (*chip, 1 - c), me).wait_recv()
        for cp in first + passed:
            cp.wait_send()
        mine.wait()
        if reduce:
            s = all_ref[0:m, :]
            for dev in range(1, 8):
                s = s + all_ref[dev * m:(dev + 1) * m, :]
            out_ref[...] = s
        else:
            out_ref[...] = all_ref[...]

    vm = pl.BlockSpec(memory_space=pltpu.VMEM)
    return pl.pallas_call(
        body, name=name, in_specs=[vm] + [ANY] * len(afters), out_specs=vm,
        out_shape=SDS((m, ncol) if reduce else (8 * m, ncol), F32),
        scratch_shapes=[pltpu.VMEM((8 * m, ncol), F32), pltpu.SemaphoreType.DMA((7,)), pltpu.SemaphoreType.DMA((7,)),
                        pltpu.SemaphoreType.DMA],
    )(xs, *afters)


RELAYOUT_ROWS = 128


def weights_to_cat(name, land, own, place, other, prev=None, after=None):
    tm = RELAYOUT_ROWS
    nb = (D // 2) // tm
    extra = ([] if prev is None else [prev]) + _as_list(after)

    def half(p):
        return 1 - p[0] if other else p[0]

    def body(p_ref, g_ref, own_ref, *rest):
        o_ref = rest[len(extra)]
        nat = jnp.concatenate([jnp.where(p_ref[1] == j, own_ref[...], g_ref[j]) for j in range(NCHIP)], axis=1)
        pad = jnp.zeros((tm, NCAT - OA - 16), BF16)
        o_ref[...] = jnp.concatenate([nat[:, 3072:7168], nat[:, 7184:11280], nat[:, 0:3072], nat[:, 7168:7184], pad], axis=1)

    grid_spec = pltpu.PrefetchScalarGridSpec(
        num_scalar_prefetch=1, grid=(nb,),
        in_specs=[pl.BlockSpec((NCHIP, None, tm, IN_SHARD), lambda i, p: (0, half(p), i, 0)),
                  pl.BlockSpec((None, tm, IN_SHARD), lambda i, p: (half(p), i, 0))] + [ANY] * len(extra),
        out_specs=pl.BlockSpec((tm, NCAT), lambda i, p: (half(p) * nb + i, 0)))
    return pl.pallas_call(
        body, name=name, grid_spec=grid_spec, out_shape=PINNED((D, NCAT), BF16),
        input_output_aliases={} if prev is None else {3: 0},
        compiler_params=_cparams(40 * 1024 * 1024, ("arbitrary",)),
    )(place, land, own, *extra)


def grads_from_cat(gw_cat):
    tm = RELAYOUT_ROWS
    nb = (D // 2) // tm

    def body(c_ref, o_ref):
        cat = c_ref[...]
        nat = jnp.concatenate([cat[:, OU:OA], cat[:, OV:OGP], cat[:, OA:OA + 16], cat[:, OGP:OU]], axis=1)
        for j in range(NCHIP):
            o_ref[j] = nat[:, j * IN_SHARD:(j + 1) * IN_SHARD]

    return pl.pallas_call(
        body, name="grads_from_cat", grid=(D // tm,), in_specs=[pl.BlockSpec((tm, NCAT), lambda i: (i, 0))],
        out_specs=pl.BlockSpec((None, NCHIP, tm, IN_SHARD), lambda i: (i // nb, 0, i % nb, 0)),
        out_shape=PINNED((2, NCHIP, D // 2, IN_SHARD), BF16), compiler_params=_cparams(40 * 1024 * 1024, ("arbitrary",)),
    )(gw_cat)


def _pad_rows(a, rows):
    return jnp.concatenate([a, jnp.zeros((rows - a.shape[0],) + a.shape[1:], a.dtype)], axis=0)


def local_step(x2d, tgt, gf, g1, pool_scale, wa_pad, b_alpha, ng, g2, get_w, on_grad=None, on_settle=None, tick=None):
    emit = on_grad if on_grad is not None else (lambda group, grads: None)
    settle = on_settle if on_settle is not None else (lambda group, after: None)
    h1 = norm1(x2d, g1)
    wcat, pw = get_w("in", h1)
    pcat = mm_in(h1, wcat)
    dpool, ylin = pool_fwd(pcat, pw)
    pinned = tick("pool", ylin) if tick is not None else None
    og, o, states = gla_fwd(pcat, wa_pad, b_alpha, ng, pinned)
    w_go, w_o = get_w("mid", og)
    mixed, ygla = mm_gla_out(og, w_go, ylin, pcat, pool_scale)
    x2, h2 = mm_out(mixed, w_o, x2d, g2)
    w_up = get_w("up", h2)
    rup, act = mm_up(h2, w_up)
    w_dn = get_w("down", act)
    dx3, dx3b, g_nf, loss_row = mm_down(act, w_dn, x2, tgt, gf)

    gw_down = mm_wgrad("mm_dw_down", act, dx3b, DFF, D, (2, NCHIP, D // 2, D), (None, None, D // 2, D),
                       lambda j, i, k: (i % 2, i // 2, 0, 0), D // 2, D)
    token = emit("down", {"down": gw_down})
    dup = mm_dact(dx3b, w_dn, rup, after=token)
    token = settle("down", dup)
    dx2, dx2b, g_mlp = mm_dh2(dup, w_up, x2, dx3, g2, after=token)
    gw_up = mm_wgrad("mm_dw_up", h2, dup, D, DFF, (2, NCHIP, D // 2, D), (None, None, D // 2, D),
                     lambda j, i, k: (i, j, 0, 0), D // 2, D)
    token = emit("up", {"up": gw_up})
    dylin, dygla, dpcat, g_ps = mm_dmixed(dx2b, w_o, pcat, ylin, ygla, pool_scale, after=token)
    token = settle("up", dylin)
    gw_out = mm_wgrad("mm_dw_out", mixed, dx2b, D, D, (2, NCHIP, 256, D), (2, None, 256, D),
                      lambda j, i, k: (0, i, 0, 0), 512, D)
    do, dpcat, g_ng = mm_dog(dygla, w_go, o, pcat, ng, dpcat, after=token)
    gw_go = mm_wgrad("mm_dw_gla_out", og, dygla, D, D, (2, NCHIP, 256, D), (2, None, 256, D),
                     lambda j, i, k: (0, i, 0, 0), 512, D)
    token = emit("mix", {"out": gw_out, "gla_out": gw_go})
    dpcat, dv, g_wa, g_ba = gla_bwd(do, pcat, states, wa_pad, b_alpha, dpcat, b_alpha if token is None else token)
    token = settle("mix", dv)
    dpcat, dpw = pool_bwd(dylin, dpool, pw, lax.dynamic_update_slice(dpcat, dv, (0, OV)))
    gw_cat = mm_wgrad("mm_dw_in", h1, dpcat, D, NCAT, (D, NCAT), (1024, 1280), lambda j, i, k: (i, j), 1024, 1280, after=token)
    token = settle("in", emit("in", {"in_cat": gw_cat, "pool": dpw}))
    grad_x, g_mix = mm_dh1(dpcat, wcat, x2d, dx2, g1, after=token)
    return (loss_row[0, 0], grad_x, g_mix, g_ps, g_mlp, g_nf, g_ng, g_ba, g_wa, token,
            gw_cat, dpw, gw_go, gw_out, gw_up, gw_down)


def kernel(x, norm_mix_g, w_in, pool_w, pool_scale, w_alpha, b_alpha, gla_norm_g, w_gla_out, w_out, norm_mlp_g, w_mlp_up, w_mlp_down, norm_final_g, loss_target, m_norm_mix_g, m_w_in, m_pool_w, m_pool_scale, m_w_alpha, m_b_alpha, m_gla_norm_g, m_w_gla_out, m_w_out, m_norm_mlp_g, m_w_mlp_up, m_w_mlp_down, m_norm_final_g, v_norm_mix_g, v_w_in, v_pool_w, v_pool_scale, v_w_alpha, v_b_alpha, v_gla_norm_g, v_w_gla_out, v_w_out, v_norm_mlp_g, v_w_mlp_up, v_w_mlp_down, v_norm_final_g):
    chip = 2 * lax.axis_index("x") + lax.axis_index("y")
    chip_i = chip.astype(jnp.int32).reshape(1)
    core_i = lax.axis_index("c").astype(jnp.int32).reshape(1)
    place_i = jnp.concatenate([core_i, chip_i])
    tgt = loss_target.reshape(T, D)
    gf = norm_final_g.reshape(1, D)

    def halves(w2d):
        r, c = w2d.shape
        return lax.dynamic_update_index_in_dim(lax.empty((NCHIP, 2, r // 2, c), BF16), w2d.astype(BF16).reshape(2, r // 2, c),
                                               chip, 0)

    pool_shard = pool_w.reshape(4 * PG, PO // NCHIP)
    w_in_r = w_in.reshape(2, D // 2, IN_SHARD)
    sent = {"in": [cast_bf16("cast_w_in", w_in_r), halves(pool_shard)]}
    flight = {}

    def start(group, after=None):
        flight[group] = gather_start("gather_start_" + group, sent[group], after)

    def relay(group, after):
        send, recv, shards, lands = flight[group]
        flight[group] = relay_turn("relay_turn_" + group, send, recv, shards, lands, after)

    def fetch(group, after):
        send, recv, shards, lands = flight[group]
        lands = relay_wait("relay_wait_" + group, send, recv, lands, after)
        return forward_halves("forward_" + group, lands)

    small_w = pack_rows("pack_small_w", [w_alpha[0].reshape(4, QK),
                                         jnp.concatenate([gla_norm_g[0].reshape(1, 512), jnp.zeros((1, 512), F32)], axis=1)], 8)
    sw_all = gather_small("gather_small_w", small_w, False).reshape(8, 8, QK)
    start("in", sw_all)
    m_in_f, v_in_f, w_go_f, w_o_f, w_up_f, w_dn_f, x_f = lax.optimization_barrier(
        (m_w_in, v_w_in, w_gla_out, w_out, w_mlp_up, w_mlp_down, x, flight["in"][2][0]))[:7]
    m_in_r, v_in_r = m_in_f.reshape(2, D // 2, IN_SHARD), v_in_f.reshape(2, D // 2, IN_SHARD)
    sent["mid"] = [halves(w_go_f[0]), halves(w_o_f[0])]
    relay("in", [m_in_r, v_in_r, *sent["mid"]])
    w_up_f, w_dn_f, x_f = lax.optimization_barrier((w_up_f, w_dn_f, x_f, flight["in"][3][0]))[:3]
    sent["up"] = [cast_to_slot("cast_w_up", w_up_f[0].reshape(2, D // 2, D), place_i)]
    sent["down"] = [cast_to_slot("cast_w_down", w_dn_f[0].reshape(2, DFF // NCHIP // 2, D), place_i)]
    x2d = x_f.reshape(T, D)
    big = [w_in_r, w_go_f[0], w_o_f[0], w_up_f[0], w_dn_f[0], pool_shard]

    def tick(point, after):
        if point == "pool":
            relay("mid", after)
            relay("up", flight["mid"][3][0])
            start("down", flight["up"][3][0])
            return [flight["up"][3][0], flight["down"][3][0]]

    def get_w(group, after):
        if group == "in":
            after = [after, *sent["up"], *sent["down"], wa_pad]
        if group == "up":
            relay("down", after)
            send, recv, lands, shards = flight["up"]
            lands = forward_wait("forward_wait_up", send, recv, lands, flight["down"][3][0])
            return lands[0].reshape(NCHIP, D, D)
        if group == "in":
            send, recv, shards, lands = flight["in"]
            send, recv, lands = forward_turn("forward_turn_in", send, recv, lands, after)
            start("mid", lands[0])
            start("up", flight["mid"][3][0])
            wcat = weights_to_cat("weights_to_cat_mine", lands[0], shards[0], place_i, False, after=flight["up"][3][0])
            lands = forward_wait("forward_wait_in", send, recv, lands, wcat)
            wcat = weights_to_cat("weights_to_cat_sibling", lands[0], shards[0], place_i, True, prev=wcat)
            g_pool = lands[1]
            pw = jnp.concatenate([g_pool[j].reshape(4, PG, PO // NCHIP) for j in range(NCHIP)], axis=2)
            return wcat, pw
        whole = fetch(group, after)
        if group == "mid":
            send, recv, shards, lands = flight["up"]
            flight["up"] = (*forward_turn("forward_turn_up", send, recv, lands, whole[0]), shards)
            w_go, w_o, _ = lax.optimization_barrier((whole[0], whole[1], flight["up"][2][0]))
            return w_go.reshape(D, D), w_o.reshape(D, D)
        return whole[0].reshape(DFF, D)

    wa_full = jnp.concatenate([sw_all[2 * j, 0:4].reshape(16, DK) for j in range(NCHIP)], axis=1)
    ng_full = jnp.concatenate([sw_all[2 * j, 4, 0:512].reshape(HEADS, DV // NCHIP) for j in range(NCHIP)], axis=1)
    wa_pad = _pad_rows(wa_full, APAD).astype(BF16)
    ng = ng_full.reshape(1, D)

    pending = {}
    wmv = {"in": (w_in_r, m_in_r, v_in_r), "gla_out": (big[1], m_w_gla_out, v_w_gla_out), "out": (big[2], m_w_out, v_w_out),
           "up": (big[3], m_w_mlp_up, v_w_mlp_up), "down": (big[4], m_w_mlp_down, v_w_mlp_down), "pool": (big[5], m_pool_w, v_pool_w)}
    big_res = {}

    def reduce_group(group, after):
        nms, send, recv, sums, lands = pending[group]
        sums, lands = scatter_wait("scatter_wait_" + group, send, recv, sums, lands, after)
        reduced = [sum_chips("sum_chips_" + nm, a, b, chip_i) for nm, a, b in zip(nms, sums, lands)]
        send, recv, reduced, lands, token = join_start("join_start_" + group, reduced)
        pending[group] = (nms, send, recv, reduced, lands)
        return token

    def update_group(group, after):
        nms, send, recv, reduced, lands = pending[group]
        reduced, from_sib = join_wait("join_wait_" + group, send, recv, reduced, lands, after)
        for nm, g_own, g_sib in zip(nms, reduced, from_sib):
            w, m, v = wmv[nm]
            shp = (2,) + g_own.shape
            big_res[nm] = adamw_halves("adamw_" + nm, w.reshape(shp), g_own, g_sib, m.reshape(shp), v.reshape(shp), core_i)

    def on_grad(group, grads):
        if group == "in":
            gw_in = grads_from_cat(grads["in_cat"])
            gw_pool = jnp.stack([grads["pool"][:, :, j * 128:(j + 1) * 128].reshape(2, 2 * PG, 128)
                                 for j in range(NCHIP)], axis=1)
            grads = {"in": gw_in, "pool": gw_pool}
        nms, parts = list(grads.keys()), list(grads.values())
        send, recv, parts, got, token = exchange_start("exchange_start_" + group, parts)
        pending[group] = (nms, send, recv, parts, got)
        return token

    def on_settle(group, after):
        if group == "in":
            for earlier in ("down", "up", "mix"):
                after = reduce_group(earlier, after)
        nms, send, recv, parts, got = pending[group]
        parts, got = exchange_wait("exchange_wait_" + group, send, recv, parts, got, after)
        sums = [add_pairs("add_pair_" + nm, a, b, core_i) for nm, a, b in zip(nms, parts, got)]
        send, recv, sums, lands, token = scatter_start("scatter_start_" + group, sums)
        pending[group] = (nms, send, recv, sums, lands)
        if group != "in":
            return token
        for earlier in ("down", "up", "mix"):
            update_group(earlier, token)
            token = big_res[pending[earlier][0][-1]][1]
        return [big_res[nm][1] for nm in ("down", "up", "out", "gla_out")]

    (loss_local, grad_x, g_mix, g_ps, g_mlp, g_nf, g_ng, g_ba, g_wa) = local_step(
        x2d, tgt, gf, norm_mix_g, pool_scale, wa_pad, b_alpha, ng, norm_mlp_g, get_w, on_grad, on_settle, tick)[:9]
    loss = lax.psum(loss_local, ("x", "y", "c"))
    join_in_token = reduce_group("in", grad_x)

    ROWS = 16

    def wide(a, n):
        return jnp.concatenate([a.reshape(1, n), jnp.zeros((1, D - n), F32)], axis=1)

    packed = pack_rows("pack_small_g", [g_mix, g_ps, g_mlp, g_nf, g_ng, wide(g_ba, QK), g_wa[0:16].reshape(8, D)], ROWS)
    tot = gather_small("reduce_small_g", packed, True, join_in_token)
    t_wa = lax.dynamic_slice(tot[6:14].reshape(16, QK), (0, chip * DK), (16, DK))
    t_ng = lax.dynamic_slice(tot[4].reshape(HEADS, DV), (0, chip * (DV // NCHIP)), (HEADS, DV // NCHIP))

    def pack_small(nm, mix, ps, mlp, nf, ba, wa, gn, after=None):
        return pack_rows(nm, [mix.reshape(1, D), ps.reshape(1, D), mlp.reshape(1, D), nf.reshape(1, D), wide(ba, QK),
                              wa.reshape(2, D), wide(gn, 512)], ROWS, after)

    update_group("in", tot)
    sg = pack_small("pack_g", tot[0], tot[1], tot[2], tot[3], tot[5, 0:QK], t_wa, t_ng, big_res["in"][3])
    sw = pack_small("pack_w", norm_mix_g, pool_scale, norm_mlp_g, norm_final_g, b_alpha, w_alpha, gla_norm_g)
    sm = pack_small("pack_m", m_norm_mix_g, m_pool_scale, m_norm_mlp_g, m_norm_final_g, m_b_alpha, m_w_alpha, m_gla_norm_g)
    sv = pack_small("pack_v", v_norm_mix_g, v_pool_scale, v_norm_mlp_g, v_norm_final_g, v_b_alpha, v_w_alpha, v_gla_norm_g)
    small_res = adamw("adamw_small", sw, sg, sm, sv)

    def unpack(p):
        return {"norm_mix_g": p[0].reshape(1, D), "pool_scale": p[1].reshape(1, D), "norm_mlp_g": p[2].reshape(1, D),
                "norm_final_g": p[3].reshape(D), "b_alpha": p[4, 0:QK].reshape(1, QK), "w_alpha": p[5:7].reshape(1, 16, DK),
                "gla_norm_g": p[7, 0:512].reshape(1, HEADS, DV // NCHIP)}

    order = ["norm_mix_g", "w_in", "pool_w", "pool_scale", "w_alpha", "b_alpha", "gla_norm_g", "w_gla_out", "w_out",
             "norm_mlp_g", "w_mlp_up", "w_mlp_down", "norm_final_g"]
    big_key = {"w_in": ("in", w_in.shape), "pool_w": ("pool", pool_w.shape), "w_gla_out": ("gla_out", w_gla_out.shape),
               "w_out": ("out", w_out.shape), "w_mlp_up": ("up", w_mlp_up.shape), "w_mlp_down": ("down", w_mlp_down.shape)}
    result = [loss, grad_x.reshape(1, T, D)]
    for kind in range(4):
        small = unpack(small_res[kind])
        for nm in order:
            if nm in big_key:
                key, shp = big_key[nm]
                result.append(big_res[key][kind].reshape(shp))
            else:
                result.append(small[nm])
    return tuple(result)
```

```python
import itertools

import jax
import jax.numpy as jnp
from jax import lax
from jax.experimental import pallas as pl
from jax.experimental.pallas import tpu as pltpu

F32 = jnp.float32
BF16 = jnp.bfloat16
SDS = jax.ShapeDtypeStruct
PINNED = pltpu.HBM
MESH = pl.DeviceIdType.MESH
ANY = pl.BlockSpec(memory_space=pl.ANY)

T = 2048
D = 2048
DFF = 8192
NCHIP = 4
IN_WIDTH = 11280
IN_SHARD = IN_WIDTH // NCHIP
CHUNK = 64
NCHUNK = T // CHUNK
HEADS = 4
DK = 256
DV = 512
QK = HEADS * DK
EPS = 1e-6
POOL_WINDOWS = (2, 4, 8, 16)
PG = 256
PO = 512

OV, OG, OGP, OGG, OU, OQ, OKK, OA = 0, 2048, 4096, 6144, 8192, 9216, 10240, 11264
NCAT = 11520
APAD = 128

VMEM_CAP = 56 * 1024 * 1024

PIECE_BYTES = 384 * 1024

ADAM_LR, ADAM_B1, ADAM_B2, ADAM_EPS, ADAM_WD, ADAM_STEP = 0.001, 0.9, 0.999, 1e-08, 0.01, 10


def _cparams(vmem_bytes=None, sem=None):
    kw = {}
    if vmem_bytes is not None:
        kw["vmem_limit_bytes"] = int(min(max(vmem_bytes, 32 * 1024 * 1024), VMEM_CAP))
    if sem is not None:
        kw["dimension_semantics"] = sem
    return pltpu.CompilerParams(**kw)


def _nbytes(shape, dtype):
    n = 1
    for s in shape:
        if s is not None:
            n *= s
    return n * jnp.dtype(dtype).itemsize


def _sigmoid(x):
    return 0.5 * jnp.tanh(0.5 * x) + 0.5


GLA_STEP = 4
EPI_COLS = 512


def _as_list(after):
    if after is None:
        return []
    return list(after) if isinstance(after, (list, tuple)) else [after]


def _in_hbm(*arrays):
    return [pltpu.with_memory_space_constraint(a, pltpu.HBM) for a in arrays]


def matmul(name, a, b, *, a_spec, b_spec, cdims, grid, acc_shape, outs, extras=(), epi, after=None, into=None):
    nj, ni, nk = grid
    ne, no = len(extras), len(outs)
    afters = _as_list(after) + ([] if into is None else [into[0]])
    first_out = 2 + ne + len(afters)

    def body(*refs):
        a_ref, b_ref = refs[0], refs[1]
        ex = refs[2:2 + ne]
        out_refs = refs[first_out:first_out + no]
        i = pl.program_id(1)
        part = lax.dot_general(a_ref[...], b_ref[...], (cdims, ((), ())), preferred_element_type=F32)
        if nk == 1:
            epi(part, ex, out_refs, i)
        else:
            acc_ref = refs[first_out + no]
            k = pl.program_id(2)

            @pl.when(k == 0)
            def _():
                acc_ref[...] = part

            @pl.when(k > 0)
            def _():
                acc_ref[...] += part

            @pl.when(k == nk - 1)
            def _():
                epi(acc_ref[...], ex, out_refs, i)

    in_specs = [pl.BlockSpec(*a_spec), pl.BlockSpec(*b_spec)] + [pl.BlockSpec(bs, im) for _, bs, im in extras]
    in_specs += [ANY] * len(afters)
    out_specs = [pl.BlockSpec(bs, im) for _, _, bs, im in outs]
    out_shape = [PINNED(s, dt) for s, dt, _, _ in outs]
    vm = 2 * (_nbytes(a_spec[0], a.dtype) + _nbytes(b_spec[0], b.dtype))
    vm += 2 * sum(_nbytes(bs, arr.dtype) for arr, bs, _ in extras)
    vm += 2 * sum(_nbytes(bs, dt) for _, dt, bs, _ in outs)
    vm += 6 * _nbytes(acc_shape, F32)
    scratch = [pltpu.VMEM(acc_shape, F32)] if nk > 1 else []
    return pl.pallas_call(
        body, name=name, grid=grid, in_specs=in_specs, out_specs=out_specs, out_shape=out_shape,
        scratch_shapes=scratch,
        input_output_aliases={} if into is None else {first_out - 1: into[1]},
        compiler_params=_cparams(vm, ("arbitrary", "arbitrary", "arbitrary")),
    )(*_in_hbm(a, b, *[arr for arr, _, _ in extras]), *afters)


NN =((1,), (0,))
NT = ((1,), (1,))
TN = ((0,), (0,))


def _row_acc(out_ref, val, i):
    @pl.when(i == 0)
    def _():
        out_ref[...] = val

    @pl.when(i > 0)
    def _():
        out_ref[...] += val


def _rms_bwd(xn, r, dxn):
    return r * (dxn - xn * jnp.mean(dxn * xn, axis=-1, keepdims=True))


def norm1(x, g):
    tm = 256

    def body(x_ref, g_ref, h_ref):
        xv = x_ref[...]
        r = lax.rsqrt(jnp.mean(xv * xv, axis=-1, keepdims=True) + EPS)
        h_ref[...] = (xv * r * g_ref[...]).astype(BF16)

    return pl.pallas_call(
        body, name="norm1", grid=(T // tm,),
        in_specs=[pl.BlockSpec((tm, D), lambda i: (i, 0)), pl.BlockSpec((1, D), lambda i: (0, 0))],
        out_specs=pl.BlockSpec((tm, D), lambda i: (i, 0)), out_shape=PINNED((T, D), BF16),
        compiler_params=_cparams(32 * 1024 * 1024, ("arbitrary",)),
    )(*_in_hbm(x, g))


def mm_in(h1, wcat):
    tm, tn = 1024, 1280

    def epi(acc, ex, outs, i):
        outs[0][...] = acc.astype(BF16)

    return matmul("mm_in", h1, wcat, a_spec=((tm, D), lambda j, i, k: (i, 0)), b_spec=((D, tn), lambda j, i, k: (0, j)),
                  cdims=NN, grid=(NCAT // tn, T // tm, 1), acc_shape=(tm, tn),
                  outs=[((T, NCAT), BF16, (tm, tn), lambda j, i, k: (i, j))], epi=epi)[0]


def _window_sum(x, w, up):
    n = x.shape[0]
    row = lax.broadcasted_iota(jnp.int32, x.shape, 0)
    s, sh = x, 1
    while sh < w:
        if up:
            s = s + jnp.where(row < n - sh, pltpu.roll(s, n - sh, axis=0), 0.0)
        else:
            s = s + jnp.where(row >= sh, pltpu.roll(s, sh, axis=0), 0.0)
        sh *= 2
    return s


def _inv_count(shape, w):
    row = lax.broadcasted_iota(jnp.int32, shape, 0)
    return 1.0 / jnp.minimum(row + 1, w).astype(F32)


def pool_fwd(pcat, pw, after=None):
    afters = _as_list(after)

    def body(u_ref, pw_ref, *rest):
        d_ref, y_ref = rest[len(afters):]
        for gi, w in enumerate(POOL_WINDOWS):
            ug = u_ref[:, gi * PG:(gi + 1) * PG].astype(F32)
            dg = _window_sum(ug, w, False) * _inv_count(ug.shape, w) - ug
            db = dg.astype(BF16)
            d_ref[:, gi * PG:(gi + 1) * PG] = db
            y_ref[:, gi * PO:(gi + 1) * PO] = jnp.dot(db, pw_ref[gi], preferred_element_type=F32).astype(BF16)

    return pl.pallas_call(
        body, name="pool_fwd", grid=(1,),
        in_specs=[pl.BlockSpec((T, 4 * PG), lambda i: (0, OU // (4 * PG))), pl.BlockSpec((4, PG, PO), lambda i: (0, 0, 0))]
                 + [ANY] * len(afters),
        out_specs=[pl.BlockSpec((T, 4 * PG), lambda i: (0, 0)), pl.BlockSpec((T, D), lambda i: (0, 0))],
        out_shape=[PINNED((T, 4 * PG), BF16), PINNED((T, D), BF16)],
        compiler_params=_cparams(48 * 1024 * 1024, ("arbitrary",)),
    )(pcat, pw, *afters)


def pool_bwd(dylin, d, pw, dproj):
    assert OU % (4 * PG) == 0

    def body(dy_ref, d_ref, pw_ref, held_ref, du_ref, dpw_ref):
        for gi, w in enumerate(POOL_WINDOWS):
            dyl = dy_ref[:, gi * PO:(gi + 1) * PO]
            dd = lax.dot_general(dyl, pw_ref[gi], (NT, ((), ())), preferred_element_type=F32)
            du = _window_sum(dd * _inv_count(dd.shape, w), w, True) - dd
            du_ref[:, gi * PG:(gi + 1) * PG] = du.astype(BF16)
            dpw_ref[gi] = lax.dot_general(d_ref[:, gi * PG:(gi + 1) * PG], dyl, (TN, ((), ())),
                                          preferred_element_type=F32).astype(BF16)

    return pl.pallas_call(
        body, name="pool_bwd", grid=(1,),
        in_specs=[pl.BlockSpec((T, D), lambda i: (0, 0)), pl.BlockSpec((T, 4 * PG), lambda i: (0, 0)),
                  pl.BlockSpec((4, PG, PO), lambda i: (0, 0, 0)), ANY],
        out_specs=[pl.BlockSpec((T, 4 * PG), lambda i: (0, OU // (4 * PG))), pl.BlockSpec((4, PG, PO), lambda i: (0, 0, 0))],
        out_shape=[PINNED((T, NCAT), BF16), PINNED((4, PG, PO), BF16)],
        input_output_aliases={3: 0},
        compiler_params=_cparams(48 * 1024 * 1024, ("arbitrary",)),
    )(dylin, d, pw, dproj)


def _gate_decay(alow, wa, ba):
    a = jnp.dot(alow, wa, preferred_element_type=F32) + ba
    ls = jax.nn.log_sigmoid(a) * (1.0 / 16.0)
    r = lax.broadcasted_iota(jnp.int32, (CHUNK, CHUNK), 0)
    c = lax.broadcasted_iota(jnp.int32, (CHUNK, CHUNK), 1)
    tri = jnp.where(c <= r, 1.0, 0.0).astype(F32)
    cum = jnp.dot(tri, ls, preferred_element_type=F32, precision=lax.Precision.HIGHEST)
    last = cum[CHUNK - 1:CHUNK, :]
    return a, jnp.exp(last - cum), jnp.exp(last)


def gla_fwd(pcat, wa, ba, ng, after=None):
    afters = _as_list(after)

    def body(q_ref, k_ref, v_ref, g_ref, al_ref, wa_ref, ba_ref, ng_ref, *rest):
        og_ref, o_ref, st_ref, s_scr = rest[len(afters):]

        @pl.when(pl.program_id(0) == 0)
        def _():
            s_scr[...] = jnp.zeros_like(s_scr)

        state = [s_scr[h] for h in range(HEADS)]
        for s in range(GLA_STEP):
            rs = slice(s * CHUNK, (s + 1) * CHUNK)
            _, e, decay = _gate_decay(al_ref[rs, :], wa_ref[...], ba_ref[...])
            kd = (k_ref[rs, :].astype(F32) * e).astype(BF16)
            qs = (q_ref[rs, :].astype(F32) * (DK ** -0.5)).astype(BF16)
            for h in range(HEADS):
                ck = slice(h * DK, (h + 1) * DK)
                cv = slice(h * DV, (h + 1) * DV)
                state[h] = state[h] * decay[:, ck] + lax.dot_general(v_ref[rs, cv], kd[:, ck], (TN, ((), ())),
                                                                     preferred_element_type=F32)
                sb = state[h].astype(BF16)
                st_ref[s, h] = sb
                oh = lax.dot_general(qs[:, ck], sb, (NT, ((), ())), preferred_element_type=F32)
                o_ref[rs, cv] = oh.astype(BF16)
                on = oh * lax.rsqrt(jnp.mean(oh * oh, axis=-1, keepdims=True) + EPS) * ng_ref[:, cv]
                gv = g_ref[rs, cv].astype(F32)
                og_ref[rs, cv] = (on * (gv * _sigmoid(gv))).astype(BF16)
        for h in range(HEADS):
            s_scr[h] = state[h]

    row = lambda c: (c, 0)
    rows = GLA_STEP * CHUNK
    return pl.pallas_call(
        body, name="gla_fwd", grid=(NCHUNK // GLA_STEP,),
        in_specs=[pl.BlockSpec((rows, QK), lambda c: (c, OQ // QK)), pl.BlockSpec((rows, QK), lambda c: (c, OKK // QK)),
                  pl.BlockSpec((rows, D), lambda c: (c, OV // D)), pl.BlockSpec((rows, D), lambda c: (c, OG // D)),
                  pl.BlockSpec((rows, APAD), lambda c: (c, OA // APAD)),
                  pl.BlockSpec((APAD, QK), lambda c: (0, 0)), pl.BlockSpec((1, QK), lambda c: (0, 0)),
                  pl.BlockSpec((1, D), lambda c: (0, 0))] + [ANY] * len(afters),
        out_specs=[pl.BlockSpec((rows, D), row), pl.BlockSpec((rows, D), row),
                   pl.BlockSpec((GLA_STEP, HEADS, DV, DK), lambda c: (c, 0, 0, 0))],
        out_shape=[PINNED((T, D), BF16), PINNED((T, D), BF16), PINNED((NCHUNK, HEADS, DV, DK), BF16)],
        scratch_shapes=[pltpu.VMEM((HEADS, DV, DK), F32)],
        compiler_params=_cparams(32 * 1024 * 1024, ("arbitrary",)),
    )(*_in_hbm(pcat, pcat, pcat, pcat, pcat, wa, ba, ng), *afters)


def gla_bwd(do, pcat, states, wa, ba, dproj, after):
    tail = NCAT - OQ
    assert (OKK, OA) == (OQ + QK, OQ + 2 * QK) and OQ % tail == 0

    def body(do_ref, q_ref, k_ref, v_ref, al_ref, sc_ref, sp_ref, wa_ref, ba_ref, after_ref, held_ref,
             dp_ref, dv_ref, dwa_ref, dba_ref, ds_scr):
        i = pl.program_id(0)
        dp_ref[:, 2 * QK + APAD:] = jnp.zeros((GLA_STEP * CHUNK, tail - 2 * QK - APAD), BF16)

        @pl.when(i == 0)
        def _():
            ds_scr[...] = jnp.zeros_like(ds_scr)

        ds = [ds_scr[h] for h in range(HEADS)]
        dwa, dba = 0.0, 0.0
        for u in reversed(range(GLA_STEP)):
            rs = slice(u * CHUNK, (u + 1) * CHUNK)
            first_chunk = jnp.logical_and(i == NCHUNK // GLA_STEP - 1, u == 0)
            has_prev = jnp.where(first_chunk, 0.0, 1.0).astype(F32)
            a, e, decay = _gate_decay(al_ref[rs, :], wa_ref[...], ba_ref[...])
            kdf = k_ref[rs, :].astype(F32) * e
            kd = kdf.astype(BF16)
            qs = (q_ref[rs, :].astype(F32) * (DK ** -0.5)).astype(BF16)
            dkd_parts, ddecay_parts = [], []
            for h in range(HEADS):
                ck = slice(h * DK, (h + 1) * DK)
                cv = slice(h * DV, (h + 1) * DV)
                doh = do_ref[rs, cv]
                dsh = ds[h] + lax.dot_general(doh, qs[:, ck], (TN, ((), ())), preferred_element_type=F32)
                dsb = dsh.astype(BF16)
                dp_ref[rs, ck] = (jnp.dot(doh, sc_ref[u, h], preferred_element_type=F32) * (DK ** -0.5)).astype(BF16)
                dkd_parts.append(jnp.dot(v_ref[rs, cv], dsb, preferred_element_type=F32))
                dv_ref[rs, cv] = lax.dot_general(kd[:, ck], dsb, (NT, ((), ())), preferred_element_type=F32).astype(BF16)
                s_prev = (sp_ref[h] if u == 0 else sc_ref[u - 1, h]).astype(F32)
                ddecay_parts.append(jnp.sum(dsh * s_prev, axis=0, keepdims=True) * has_prev)
                ds[h] = dsh * decay[:, ck]
            dkd = jnp.concatenate(dkd_parts, axis=1)
            ddecay = jnp.concatenate(ddecay_parts, axis=1)
            dp_ref[rs, QK:2 * QK] = (dkd * e).astype(BF16)
            dearg = dkd * kdf
            dlast = jnp.sum(dearg, axis=0, keepdims=True) + ddecay * decay
            r = lax.broadcasted_iota(jnp.int32, (CHUNK, CHUNK), 0)
            c = lax.broadcasted_iota(jnp.int32, (CHUNK, CHUNK), 1)
            triu = jnp.where(c >= r, 1.0, 0.0).astype(F32)
            dls = dlast - jnp.dot(triu, dearg, preferred_element_type=F32, precision=lax.Precision.HIGHEST)
            da = dls * (1.0 / 16.0) * (1.0 - _sigmoid(a))
            dab = da.astype(BF16)
            dp_ref[rs, 2 * QK:2 * QK + APAD] = lax.dot_general(dab, wa_ref[...], (NT, ((), ())),
                                                               preferred_element_type=F32).astype(BF16)
            dwa = dwa + lax.dot_general(al_ref[rs, :], dab, (TN, ((), ())), preferred_element_type=F32)
            dba = dba + jnp.sum(da, axis=0, keepdims=True)
        for h in range(HEADS):
            ds_scr[h] = ds[h]

        @pl.when(i == 0)
        def _():
            dwa_ref[...] = dwa
            dba_ref[...] = dba

        @pl.when(i > 0)
        def _():
            dwa_ref[...] += dwa
            dba_ref[...] += dba

    rows = GLA_STEP * CHUNK
    rev = lambda i: NCHUNK // GLA_STEP - 1 - i
    return pl.pallas_call(
        body, name="gla_bwd", grid=(NCHUNK // GLA_STEP,),
        in_specs=[pl.BlockSpec((rows, D), lambda i: (rev(i), 0)),
                  pl.BlockSpec((rows, QK), lambda i: (rev(i), OQ // QK)), pl.BlockSpec((rows, QK), lambda i: (rev(i), OKK // QK)),
                  pl.BlockSpec((rows, D), lambda i: (rev(i), OV // D)), pl.BlockSpec((rows, APAD), lambda i: (rev(i), OA // APAD)),
                  pl.BlockSpec((GLA_STEP, HEADS, DV, DK), lambda i: (rev(i), 0, 0, 0)),
                  pl.BlockSpec((None, HEADS, DV, DK), lambda i: (jnp.maximum(rev(i) * GLA_STEP - 1, 0), 0, 0, 0)),
                  pl.BlockSpec((APAD, QK), lambda i: (0, 0)), pl.BlockSpec((1, QK), lambda i: (0, 0)), ANY, ANY],
        out_specs=[pl.BlockSpec((rows, tail), lambda i: (rev(i), OQ // tail)), pl.BlockSpec((rows, D), lambda i: (rev(i), 0)),
                   pl.BlockSpec((APAD, QK), lambda i: (0, 0)), pl.BlockSpec((1, QK), lambda i: (0, 0))],
        out_shape=[PINNED((T, NCAT), BF16), PINNED((T, D), BF16), PINNED((APAD, QK), F32), PINNED((1, QK), F32)],
        scratch_shapes=[pltpu.VMEM((HEADS, DV, DK), F32)],
        input_output_aliases={10: 0},
        compiler_params=_cparams(32 * 1024 * 1024, ("arbitrary",)),
    )(*_in_hbm(do, pcat, pcat, pcat, pcat, states, states, wa, ba), after, dproj)


TMF = 256
TMW = 512
_rowblk = ((TMF, D), lambda j, i, k: (i, 0))
_vec = ((1, D), lambda j, i, k: (0, 0))


def _full_spec(col):
    return ((TMF, D), lambda j, i, k: (i, col))


TBIG = 1024


def square_matmul(name, a, b, *, a_spec, b_spec, cdims, nk, after=None):
    def epi(acc, ex, outs, i):
        outs[0][...] = acc

    return matmul(name, a, b, a_spec=a_spec, b_spec=b_spec, cdims=cdims, grid=(D // TBIG, T // TBIG, nk),
                  acc_shape=(TBIG, TBIG), outs=[((T, D), F32, (TBIG, TBIG), lambda j, i, k: (i, j))], epi=epi,
                  after=after)[0]


def rowwise(name, y, *, extras, outs, epi):
    ne = len(extras)

    def body(*refs):
        epi(refs[0][...], refs[1:1 + ne], refs[1 + ne:], pl.program_id(1))

    in_specs = [pl.BlockSpec(*_rowblk)] + [pl.BlockSpec(bs, im) for _, bs, im in extras]
    return pl.pallas_call(
        body, name=name, grid=(1, T // TMF, 1), in_specs=in_specs,
        out_specs=[pl.BlockSpec(bs, im) for _, _, bs, im in outs], out_shape=[PINNED(s, dt) for s, dt, _, _ in outs],
        compiler_params=_cparams(40 * 1024 * 1024, ("arbitrary", "arbitrary", "arbitrary")),
    )(*_in_hbm(y, *[arr for arr, _, _ in extras]))


def mm_gla_out(og, w, ylin, pcat, pscale):
    def epi(acc, ex, outs, i):
        ylin_ref, lgp_ref, lgg_ref, ps_ref = ex
        for c0 in range(0, D, EPI_COLS):
            cs = slice(c0, c0 + EPI_COLS)
            gp = _sigmoid(lgp_ref[:, cs].astype(F32))
            gg = _sigmoid(lgg_ref[:, cs].astype(F32))
            a = acc[:, cs]
            outs[0][:, cs] = (gp * (ylin_ref[:, cs].astype(F32) * ps_ref[:, cs]) + gg * a).astype(BF16)
            outs[1][:, cs] = a.astype(BF16)

    return matmul("mm_gla_out", og, w, a_spec=_rowblk, b_spec=((D, D), lambda j, i, k: (0, 0)), cdims=NN,
                  grid=(1, T // TMF, 1), acc_shape=(TMF, D),
                  extras=[(ylin, *_rowblk), (pcat, *_full_spec(OGP // D)), (pcat, *_full_spec(OGG // D)), (pscale, *_vec)],
                  outs=[((T, D), BF16, *_rowblk), ((T, D), BF16, *_rowblk)], epi=epi)


def mm_out(mixed, w, x, g2):
    def epi(acc, ex, outs, i):
        x_ref, g_ref = ex
        x2 = x_ref[...] + acc
        r = lax.rsqrt(jnp.mean(x2 * x2, axis=-1, keepdims=True) + EPS)
        outs[0][...] = x2
        outs[1][...] = (x2 * r * g_ref[...]).astype(BF16)

    return matmul("mm_out", mixed, w, a_spec=_rowblk, b_spec=((D, D), lambda j, i, k: (0, 0)), cdims=NN,
                  grid=(1, T // TMF, 1), acc_shape=(TMF, D), extras=[(x, *_rowblk), (g2, *_vec)],
                  outs=[((T, D), F32, *_rowblk), ((T, D), BF16, *_rowblk)], epi=epi)


def mm_up(h2, wup):
    def epi(acc, ex, outs, i):
        r = jnp.maximum(acc, 0.0)
        outs[0][...] = r.astype(BF16)
        outs[1][...] = (r * r).astype(BF16)

    blk = ((TMW, D), lambda j, i, k: (i, j))
    return matmul("mm_up", h2, wup, a_spec=((TMW, D), lambda j, i, k: (i, 0)), b_spec=((None, D, D), lambda j, i, k: (j, 0, 0)),
                  cdims=NN, grid=(NCHIP, T // TMW, 1), acc_shape=(TMW, D),
                  outs=[((T, DFF), BF16, *blk), ((T, DFF), BF16, *blk)], epi=epi)


def mm_down(act, wdown, x2, tgt, gf):
    tk = 4096

    def epi(acc, ex, outs, i):
        x2_ref, t_ref, g_ref = ex
        dx_ref, dxb_ref, gnf_ref, loss_ref = outs
        x3 = x2_ref[...] + acc
        r = lax.rsqrt(jnp.mean(x3 * x3, axis=-1, keepdims=True) + EPS)
        xn = x3 * r
        err = xn * g_ref[...] - t_ref[...]
        lsum = 0.5 * jnp.sum(jnp.mean(err * err, axis=-1, keepdims=True), axis=0, keepdims=True)
        dy = err * (1.0 / D)
        _row_acc(gnf_ref, jnp.sum(dy * xn, axis=0, keepdims=True), i)
        _row_acc(loss_ref, jnp.broadcast_to(lsum, (1, 128)), i)
        dx3 = _rms_bwd(xn, r, dy * g_ref[...])
        dx_ref[...] = dx3
        dxb_ref[...] = dx3.astype(BF16)

    y = square_matmul("mm_down", act, wdown, a_spec=((TBIG, tk), lambda j, i, k: (i, k)),
                      b_spec=((tk, TBIG), lambda j, i, k: (k, j)), cdims=NN, nk=DFF // tk)
    return rowwise("rows_final", y, extras=[(x2, *_rowblk), (tgt, *_rowblk), (gf, *_vec)],
                   outs=[((T, D), F32, *_rowblk), ((T, D), BF16, *_rowblk), ((1, D), F32, *_vec),
                         ((1, 128), F32, (1, 128), lambda j, i, k: (0, 0))], epi=epi)


def mm_dact(dx3b, wdown, rup, after=None):
    def epi(acc, ex, outs, i):
        outs[0][...] = (acc * 2.0 * ex[0][...].astype(F32)).astype(BF16)

    blk = ((TMW, D), lambda j, i, k: (i, j))
    return matmul("mm_dact", dx3b, wdown, a_spec=((TMW, D), lambda j, i, k: (i, 0)), b_spec=((D, D), lambda j, i, k: (j, 0)),
                  cdims=NT, grid=(DFF // D, T // TMW, 1), acc_shape=(TMW, D), extras=[(rup, *blk)],
                  outs=[((T, DFF), BF16, *blk)], epi=epi, after=after)[0]


def mm_wgrad(name, a, b, m, n, out_shape, out_block, out_map, tm, tn, after=None):
    def epi(acc, ex, outs, i):
        outs[0][...] = acc.astype(BF16).reshape(outs[0].shape)

    return matmul(name, a, b, a_spec=((T, tm), lambda j, i, k: (0, i)), b_spec=((T, tn), lambda j, i, k: (0, j)),
                  cdims=TN, grid=(n // tn, m // tm, 1), acc_shape=(tm, tn),
                  outs=[(out_shape, BF16, out_block, out_map)], epi=epi, after=after)[0]


def mm_dh2(dup, wup, x2, dx3, g2, after=None):
    def epi(acc, ex, outs, i):
        x2_ref, dx3_ref, g_ref = ex
        x2 = x2_ref[...]
        r = lax.rsqrt(jnp.mean(x2 * x2, axis=-1, keepdims=True) + EPS)
        xn = x2 * r
        _row_acc(outs[2], jnp.sum(acc * xn, axis=0, keepdims=True), i)
        dx2 = dx3_ref[...] + _rms_bwd(xn, r, acc * g_ref[...])
        outs[0][...] = dx2
        outs[1][...] = dx2.astype(BF16)

    y = square_matmul("mm_dh2", dup, wup, a_spec=((TBIG, D), lambda j, i, k: (i, k)),
                      b_spec=((None, TBIG, D), lambda j, i, k: (k, j, 0)), cdims=NT, nk=NCHIP, after=after)
    return rowwise("rows_dh2", y, extras=[(x2, *_rowblk), (dx3, *_rowblk), (g2, *_vec)],
                   outs=[((T, D), F32, *_rowblk), ((T, D), BF16, *_rowblk), ((1, D), F32, *_vec)], epi=epi)


def mm_dmixed(dx2b, wout, pcat, ylin, ygla, pscale, after=None):
    assert OGG == OGP + D and OGP % (2 * D) == 0

    def epi(acc, ex, outs, i):
        lgp_ref, lgg_ref, ylin_ref, ygla_ref, ps_ref = ex
        dps = []
        for c0 in range(0, D, EPI_COLS):
            cs = slice(c0, c0 + EPI_COLS)
            gp = _sigmoid(lgp_ref[:, cs].astype(F32))
            gg = _sigmoid(lgg_ref[:, cs].astype(F32))
            yl = ylin_ref[:, cs].astype(F32)
            ps = ps_ref[:, cs]
            a = acc[:, cs]
            agp = a * gp
            outs[0][:, cs] = (agp * ps).astype(BF16)
            outs[1][:, cs] = (a * gg).astype(BF16)
            outs[2][:, cs] = (agp * (yl * ps) * (1.0 - gp)).astype(BF16)
            outs[2][:, D + c0:D + c0 + EPI_COLS] = (a * ygla_ref[:, cs].astype(F32) * gg * (1.0 - gg)).astype(BF16)
            dps.append(jnp.sum(agp * yl, axis=0, keepdims=True))
        _row_acc(outs[3], jnp.concatenate(dps, axis=1), i)

    return matmul("mm_dmixed", dx2b, wout, a_spec=_rowblk, b_spec=((D, D), lambda j, i, k: (0, 0)), cdims=NT,
                  grid=(1, T // TMF, 1), acc_shape=(TMF, D),
                  extras=[(pcat, *_full_spec(OGP // D)), (pcat, *_full_spec(OGG // D)), (ylin, *_rowblk), (ygla, *_rowblk),
                          (pscale, *_vec)],
                  outs=[((T, D), BF16, *_rowblk)] * 2
                       + [((T, NCAT), BF16, (TMF, 2 * D), lambda j, i, k: (i, OGP // (2 * D))), ((1, D), F32, *_vec)],
                  epi=epi, after=after)


def mm_dog(dygla, wgo, o, pcat, ng, dproj, after=None):
    def epi(acc, ex, outs, i):
        o_ref, g_ref, ng_ref = ex
        do_ref, dg_ref, gng_ref = outs
        gparts = []
        for h in range(HEADS):
            cv = slice(h * DV, (h + 1) * DV)
            oh = o_ref[:, cv].astype(F32)
            r = lax.rsqrt(jnp.mean(oh * oh, axis=-1, keepdims=True) + EPS)
            on = oh * r
            gv = g_ref[:, cv].astype(F32)
            sg = _sigmoid(gv)
            a = acc[:, cv]
            dgain = a * (gv * sg)
            gparts.append(jnp.sum(dgain * on, axis=0, keepdims=True))
            ngh = ng_ref[:, cv]
            do_ref[:, cv] = _rms_bwd(on, r, dgain * ngh).astype(BF16)
            dg_ref[:, cv] = (a * (on * ngh) * (sg * (1.0 + gv * (1.0 - sg)))).astype(BF16)
        _row_acc(gng_ref, jnp.concatenate(gparts, axis=1), i)

    return matmul("mm_dog", dygla, wgo, a_spec=_rowblk, b_spec=((D, D), lambda j, i, k: (0, 0)), cdims=NT,
                  grid=(1, T // TMF, 1), acc_shape=(TMF, D),
                  extras=[(o, *_rowblk), (pcat, *_full_spec(OG // D)), (ng, *_vec)],
                  outs=[((T, D), BF16, *_rowblk), ((T, NCAT), BF16, *_full_spec(OG // D)), ((1, D), F32, *_vec)],
                  epi=epi, after=after, into=(dproj, 1))


def mm_dh1(dpcat, wcat, x, dx2, g1, after=None):
    tk = 3840

    def epi(acc, ex, outs, i):
        x_ref, dx2_ref, g_ref = ex
        xv = x_ref[...]
        r = lax.rsqrt(jnp.mean(xv * xv, axis=-1, keepdims=True) + EPS)
        xn = xv * r
        _row_acc(outs[1], jnp.sum(acc * xn, axis=0, keepdims=True), i)
        outs[0][...] = dx2_ref[...] + _rms_bwd(xn, r, acc * g_ref[...])

    y = square_matmul("mm_dh1", dpcat, wcat, a_spec=((TBIG, tk), lambda j, i, k: (i, k)),
                      b_spec=((TBIG, tk), lambda j, i, k: (j, k)), cdims=NT, nk=NCAT // tk, after=after)
    return rowwise("rows_dh1", y, extras=[(x, *_rowblk), (dx2, *_rowblk), (g1, *_vec)],
                   outs=[((T, D), F32, *_rowblk), ((1, D), F32, *_vec)], epi=epi)


def _tile_rows(rows, cols, n_arrays):
    tm = rows
    while tm % 32 == 0 and 2 * n_arrays * tm * cols * 4 > 36 * 1024 * 1024:
        tm //= 2
    return tm


def add_pairs(name, parts, theirs, core):
    _, _, r, c = parts.shape
    tm = _tile_rows(r, c, 3)

    def body(core_ref, a_ref, b_ref, o_ref):
        o_ref[...] = (a_ref[...].astype(F32) + b_ref[...].astype(F32)).astype(BF16)

    spec = pl.BlockSpec((None, tm, c), lambda j, i, core_ref: (j, i, 0))
    grid_spec = pltpu.PrefetchScalarGridSpec(
        num_scalar_prefetch=1, grid=(NCHIP, r // tm),
        in_specs=[pl.BlockSpec((None, None, tm, c), lambda j, i, core_ref: (core_ref[0], j, i, 0)), spec], out_specs=spec)
    return pl.pallas_call(body, name=name, grid_spec=grid_spec, out_shape=PINNED((NCHIP, r, c), BF16),
                          compiler_params=_cparams(40 * 1024 * 1024, ("arbitrary", "arbitrary")))(core, *_in_hbm(parts, theirs))


def sum_chips(name, sums, landed, chip):
    _, r, c = sums.shape
    tm = _tile_rows(r, c, 4)

    def body(chip_ref, own_ref, l_ref, o_ref):
        s = own_ref[...].astype(F32)
        for t in range(NCHIP - 1):
            s = s + l_ref[t].astype(F32)
        o_ref[...] = s

    grid_spec = pltpu.PrefetchScalarGridSpec(
        num_scalar_prefetch=1, grid=(r // tm,),
        in_specs=[pl.BlockSpec((None, tm, c), lambda i, chip_ref: (chip_ref[0], i, 0)),
                  pl.BlockSpec((NCHIP - 1, tm, c), lambda i, chip_ref: (0, i, 0))],
        out_specs=pl.BlockSpec((tm, c), lambda i, chip_ref: (i, 0)))
    return pl.pallas_call(body, name=name, grid_spec=grid_spec, out_shape=PINNED((r, c), F32),
                          compiler_params=_cparams(40 * 1024 * 1024, ("arbitrary",)))(chip, *_in_hbm(sums, landed))


def _adamw_math(wv, gv, mv, vv):
    mn = ADAM_B1 * mv + (1.0 - ADAM_B1) * gv
    vn = ADAM_B2 * vv + (1.0 - ADAM_B2) * (gv * gv)
    mh = mn / (1.0 - ADAM_B1 ** ADAM_STEP)
    vh = vn / (1.0 - ADAM_B2 ** ADAM_STEP)
    return -ADAM_LR * (mh / (jnp.sqrt(vh) + ADAM_EPS) + ADAM_WD * wv), mn, vn


def adamw(name, w, g, m, v):
    def body(w_ref, g_ref, m_ref, v_ref, go_ref, d_ref, mo_ref, vo_ref):
        gv = g_ref[...]
        go_ref[...] = gv
        d_ref[...], mo_ref[...], vo_ref[...] = _adamw_math(w_ref[...], gv, m_ref[...], v_ref[...])

    return pl.pallas_call(body, name=name, out_shape=[SDS(w.shape, F32)] * 4)(w, g, m, v)


def adamw_halves(name, w, g_own, g_sib, m, v, core):
    _, r, c = w.shape
    tm = _tile_rows(r, c, 10)

    def body(core_ref, w_ref, go_ref, gs_ref, m_ref, v_ref, g_out, d_out, m_out, v_out):
        gv = jnp.where(pl.program_id(0) == core_ref[0], go_ref[...], gs_ref[...])
        g_out[...] = gv
        d_out[...], m_out[...], v_out[...] = _adamw_math(w_ref[...], gv, m_ref[...], v_ref[...])

    full = pl.BlockSpec((None, tm, c), lambda h, i, core_ref: (h, i, 0))
    own = pl.BlockSpec((tm, c), lambda h, i, core_ref: (jnp.where(h == core_ref[0], i, 0), 0))
    sib = pl.BlockSpec((tm, c), lambda h, i, core_ref: (jnp.where(h == core_ref[0], 0, i), 0))
    grid_spec = pltpu.PrefetchScalarGridSpec(num_scalar_prefetch=1, grid=(2, r // tm),
                                             in_specs=[full, own, sib, full, full], out_specs=[full] * 4)
    return pl.pallas_call(body, name=name, grid_spec=grid_spec, out_shape=[SDS(w.shape, F32)] * 4,
                          compiler_params=_cparams(48 * 1024 * 1024, ("arbitrary", "arbitrary")))(core, *_in_hbm(w, g_own, g_sib, m, v))


def cast_bf16(name, w):
    _, r, c = w.shape
    tm = _tile_rows(r, c, 2)

    def body(w_ref, o_ref):
        o_ref[...] = w_ref[...].astype(BF16)

    spec = pl.BlockSpec((None, tm, c), lambda h, i: (h, i, 0))
    return pl.pallas_call(body, name=name, grid=(2, r // tm), in_specs=[spec], out_specs=spec, out_shape=PINNED(w.shape, BF16),
                          compiler_params=_cparams(40 * 1024 * 1024, ("arbitrary", "arbitrary")))(w)


def cast_to_slot(name, w, place):
    _, r, c = w.shape
    tm = _tile_rows(r, c, 2)

    def body(p_ref, w_ref, o_ref):
        o_ref[...] = w_ref[...].astype(BF16)

    grid_spec = pltpu.PrefetchScalarGridSpec(
        num_scalar_prefetch=1, grid=(2, r // tm), in_specs=[pl.BlockSpec((None, tm, c), lambda h, i, p: (h, i, 0))],
        out_specs=pl.BlockSpec((None, None, tm, c), lambda h, i, p: (p[1], h, i, 0)))
    return pl.pallas_call(body, name=name, grid_spec=grid_spec, out_shape=PINNED((NCHIP, 2, r, c), BF16),
                          compiler_params=_cparams(40 * 1024 * 1024, ("arbitrary", "arbitrary")))(place, *_in_hbm(w))


def pack_rows(name, parts, rows, after=None):
    width = parts[0].shape[1]
    n = len(parts)
    afters = _as_list(after)

    def body(*refs):
        out_ref = refs[n + len(afters)]
        out_ref[...] = jnp.zeros_like(out_ref)
        off = 0
        for p in refs[:n]:
            out_ref[off:off + p.shape[0], :] = p[...]
            off += p.shape[0]

    vm = pl.BlockSpec(memory_space=pltpu.VMEM)
    return pl.pallas_call(body, name=name, in_specs=[vm] * n + [ANY] * len(afters), out_specs=vm,
                          out_shape=SDS((rows, width), F32))(*parts, *afters)


def _place():
    x, y, c = lax.axis_index("x"), lax.axis_index("y"), lax.axis_index("c")
    chips = [(1 - x, y), (x, 1 - y), (1 - x, 1 - y)]
    return x, y, c, chips


def _row_split(shape, dtype):
    r, c = shape
    n = 1
    while r % (2 * n) == 0 and (r // (2 * n)) % 16 == 0 and (r // n) * c * jnp.dtype(dtype).itemsize > PIECE_BYTES:
        n *= 2
    return [pl.ds(s * (r // n), r // n) for s in range(n)]


def _pieces(ref):
    *lead, r, c = ref.shape
    split = _row_split((r, c), ref.dtype)
    return [ref.at[(*idx, s)] for idx in itertools.product(*[range(d) for d in lead]) for s in split]


HBM = pl.BlockSpec(memory_space=pltpu.HBM)
SEM = pl.BlockSpec(memory_space=pltpu.SEMAPHORE)
EFFECT = pltpu.SideEffectType.DATAFLOW_SIDE_EFFECTING


def _own_half(shard_refs, land, a, me, c):
    return land[a].at[me, c] if shard_refs[a] is None else shard_refs[a].at[c]


def _spread(refs, shards):
    it = iter(refs)
    return [None if s is None else next(it) for s in shards]


def gather_start(name, items, after=None):
    n = len(items)
    shards = [s if s.ndim == 3 else None for s in items]
    given = [s for s in shards if s is not None]
    ns = len(given)
    afters = _as_list(after)

    def body(*refs):
        src, land = _spread(refs[:ns], shards), refs[ns:ns + n]
        send, recv = refs[ns + n + len(afters)], refs[ns + n + len(afters) + 1]
        x, y, c, chips = _place()
        me = 2 * x + y
        for a in range(n):
            for j, (cx, cy) in enumerate(chips[:2]):
                for sp, dp in zip(_pieces(_own_half(src, land, a, me, c)), _pieces(land[a].at[me, c])):
                    pltpu.make_async_remote_copy(sp, dp, send.at[2 * a + j], recv.at[2 * a + j],
                                                 device_id=(cx, cy, c), device_id_type=MESH).start()

    lands = [pltpu.with_memory_space_constraint(lax.empty((NCHIP,) + s.shape, s.dtype) if s.ndim == 3 else s, pltpu.HBM)
             for s in items]
    srcs = [pltpu.with_memory_space_constraint(s, pltpu.HBM) for s in given]
    outs = pl.pallas_call(
        body, name=name,
        out_shape=(pltpu.SemaphoreType.DMA((2 * n,)), pltpu.SemaphoreType.DMA((2 * n,)),
                   *[pltpu.HBM(s.shape, s.dtype) for s in given], *[pltpu.HBM(l.shape, l.dtype) for l in lands]),
        in_specs=[HBM] * (ns + n) + [ANY] * len(afters), out_specs=(SEM, SEM, *([HBM] * (ns + n))),
        input_output_aliases={i: 2 + i for i in range(ns + n)},
        compiler_params=pltpu.CompilerParams(has_side_effects=EFFECT),
    )(*srcs, *lands, *afters)
    return outs[0], outs[1], _spread(outs[2:2 + ns], shards), list(outs[2 + ns:2 + ns + n])


def _relay_blocks(land, c, chips):
    (xx, xy), (yx, yy), (dx, dy) = chips
    rows = land.shape[2] // 2
    upper, lower = pl.ds(0, rows), pl.ds(rows, rows)
    return [(land.at[2 * yx + yy, c, lower], land.at[2 * dx + dy, c, lower]),
            (land.at[2 * xx + xy, c, upper], land.at[2 * dx + dy, c, upper])]


def relay_turn(name, send, recv, shards, lands, after):
    n = len(lands)
    given = [s for s in shards if s is not None]
    ns = len(given)
    afters = _as_list(after)

    def body(*refs):
        src, had = _spread(refs[:ns], shards), refs[ns:ns + n]
        send_ref, recv_ref = refs[ns + n], refs[ns + n + 1]
        rsend, rrecv = refs[ns + n + 2 + len(afters)], refs[ns + n + 3 + len(afters)]
        land = refs[2 * ns + n + 4 + len(afters):2 * ns + 2 * n + 4 + len(afters)]
        x, y, c, chips = _place()
        me = 2 * x + y
        for a in range(n):
            for j, (cx, cy) in enumerate(chips[:2]):
                cp = pltpu.make_async_remote_copy(_own_half(src, had, a, me, c), had[a].at[2 * cx + cy, c],
                                                  send_ref.at[2 * a + j], recv_ref.at[2 * a + j],
                                                  device_id=(cx, cy, c), device_id_type=MESH)
                cp.wait_send()
                cp.wait_recv()
        for a in range(n):
            for j, ((sent, _), (dst, _)) in enumerate(zip(_relay_blocks(had[a], c, chips), _relay_blocks(land[a], c, chips))):
                cx, cy = chips[j]
                for sp, dp in zip(_pieces(sent), _pieces(dst)):
                    pltpu.make_async_remote_copy(sp, dp, rsend.at[2 * a + j], rrecv.at[2 * a + j],
                                                 device_id=(cx, cy, c), device_id_type=MESH).start()

    outs = pl.pallas_call(
        body, name=name,
        out_shape=(pltpu.SemaphoreType.DMA((2 * n,)), pltpu.SemaphoreType.DMA((2 * n,)),
                   *[pltpu.HBM(s.shape, s.dtype) for s in given], *[pltpu.HBM(l.shape, l.dtype) for l in lands]),
        in_specs=[HBM] * (ns + n) + [SEM, SEM] + [ANY] * len(afters), out_specs=(SEM, SEM, *([HBM] * (ns + n))),
        input_output_aliases={i: 2 + i for i in range(ns + n)},
        compiler_params=pltpu.CompilerParams(has_side_effects=EFFECT),
    )(*given, *lands, send, recv, *afters)
    return outs[0], outs[1], _spread(outs[2:2 + ns], shards), list(outs[2 + ns:2 + ns + n])


def relay_wait(name, send, recv, lands, after):
    n = len(lands)
    afters = _as_list(after)

    def body(*refs):
        land = refs[:n]
        send_ref, recv_ref = refs[n], refs[n + 1]
        x, y, c, chips = _place()
        for a in range(n):
            for j, (sent, got) in enumerate(_relay_blocks(land[a], c, chips)):
                cx, cy = chips[j]
                cp = pltpu.make_async_remote_copy(sent, got, send_ref.at[2 * a + j], recv_ref.at[2 * a + j],
                                                  device_id=(cx, cy, c), device_id_type=MESH)
                cp.wait_send()
                cp.wait_recv()

    outs = pl.pallas_call(
        body, name=name, out_shape=tuple(pltpu.HBM(l.shape, l.dtype) for l in lands),
        in_specs=[HBM] * n + [SEM, SEM] + [ANY] * len(afters), out_specs=[HBM] * n,
        input_output_aliases={i: i for i in range(n)},
        compiler_params=pltpu.CompilerParams(has_side_effects=EFFECT),
    )(*lands, send, recv, *afters)
    return list(outs)


def forward_halves(name, lands):
    n = len(lands)

    def body(*refs):
        had, buf = refs[:n], refs[n:2 * n]
        send, recv = refs[2 * n:]
        x, y, c, chips = _place()
        sib = (x, y, 1 - c)
        for a in range(n):
            for j, (cx, cy) in enumerate(chips):
                for sp, dp in zip(_pieces(had[a].at[2 * cx + cy, c]), _pieces(buf[a].at[2 * cx + cy, c])):
                    pltpu.make_async_remote_copy(sp, dp, send.at[3 * a + j], recv.at[3 * a + j], device_id=sib, device_id_type=MESH).start()
        for a in range(n):
            for j, (cx, cy) in enumerate(chips):
                pltpu.make_async_remote_copy(had[a].at[2 * cx + cy, c], buf[a].at[2 * cx + cy, 1 - c], send.at[3 * a + j],
                                             recv.at[3 * a + j], device_id=sib, device_id_type=MESH).wait()

    return pl.pallas_call(
        body, name=name, in_specs=[ANY] * n, out_specs=[ANY] * n, out_shape=[SDS(l.shape, l.dtype) for l in lands],
        input_output_aliases={i: i for i in range(n)},
        scratch_shapes=[pltpu.SemaphoreType.DMA((3 * n,)), pltpu.SemaphoreType.DMA((3 * n,))],
    )(*lands)


def forward_turn(name, send, recv, lands, after):
    n = len(lands)
    afters = _as_list(after)

    def body(*refs):
        had = refs[:n]
        send_ref, recv_ref = refs[n], refs[n + 1]
        fsend, frecv = refs[n + 2 + len(afters)], refs[n + 3 + len(afters)]
        buf = refs[n + 4 + len(afters):2 * n + 4 + len(afters)]
        x, y, c, chips = _place()
        sib = (x, y, 1 - c)
        for a in range(n):
            for j, (sent, got) in enumerate(_relay_blocks(had[a], c, chips)):
                cx, cy = chips[j]
                cp = pltpu.make_async_remote_copy(sent, got, send_ref.at[2 * a + j], recv_ref.at[2 * a + j],
                                                  device_id=(cx, cy, c), device_id_type=MESH)
                cp.wait_send()
                cp.wait_recv()
        for a in range(n):
            for j, (cx, cy) in enumerate(chips):
                for sp, dp in zip(_pieces(had[a].at[2 * cx + cy, c]), _pieces(buf[a].at[2 * cx + cy, c])):
                    pltpu.make_async_remote_copy(sp, dp, fsend.at[3 * a + j], frecv.at[3 * a + j], device_id=sib, device_id_type=MESH).start()

    outs = pl.pallas_call(
        body, name=name,
        out_shape=(pltpu.SemaphoreType.DMA((3 * n,)), pltpu.SemaphoreType.DMA((3 * n,)), *[pltpu.HBM(l.shape, l.dtype) for l in lands]),
        in_specs=[HBM] * n + [SEM, SEM] + [ANY] * len(afters), out_specs=(SEM, SEM, *([HBM] * n)),
        input_output_aliases={i: 2 + i for i in range(n)},
        compiler_params=pltpu.CompilerParams(has_side_effects=EFFECT),
    )(*lands, send, recv, *afters)
    return outs[0], outs[1], list(outs[2:])


def forward_wait(name, send, recv, lands, after):
    n = len(lands)
    afters = _as_list(after)

    def body(*refs):
        land = refs[:n]
        send_ref, recv_ref = refs[n], refs[n + 1]
        x, y, c, chips = _place()
        sib = (x, y, 1 - c)
        for a in range(n):
            for j, (cx, cy) in enumerate(chips):
                cp = pltpu.make_async_remote_copy(land[a].at[2 * cx + cy, c], land[a].at[2 * cx + cy, 1 - c], send_ref.at[3 * a + j],
                                                  recv_ref.at[3 * a + j], device_id=sib, device_id_type=MESH)
                cp.wait_send()
                cp.wait_recv()

    outs = pl.pallas_call(
        body, name=name, out_shape=tuple(pltpu.HBM(l.shape, l.dtype) for l in lands),
        in_specs=[HBM] * n + [SEM, SEM] + [ANY] * len(afters), out_specs=[HBM] * n,
        input_output_aliases={i: i for i in range(n)},
        compiler_params=pltpu.CompilerParams(has_side_effects=EFFECT),
    )(*lands, send, recv, *afters)
    return list(outs)


def exchange_start(name, parts):
    n = len(parts)

    def body(*refs):
        src, got = refs[:n], refs[n:2 * n]
        send, recv = refs[2 * n], refs[2 * n + 1]
        token = refs[4 * n + 2]
        x, y, c, _ = _place()
        sib = (x, y, 1 - c)
        for a in range(n):
            for sp, dp in zip(_pieces(src[a].at[1 - c]), _pieces(got[a])):
                pltpu.make_async_remote_copy(sp, dp, send.at[a], recv.at[a], device_id=sib, device_id_type=MESH).start()
        token[...] = jnp.zeros_like(token)

    lands = [pltpu.with_memory_space_constraint(lax.empty(p.shape[1:], p.dtype), pltpu.HBM) for p in parts]
    srcs = [pltpu.with_memory_space_constraint(p, pltpu.HBM) for p in parts]
    outs = pl.pallas_call(
        body, name=name,
        out_shape=(pltpu.SemaphoreType.DMA((n,)), pltpu.SemaphoreType.DMA((n,)),
                   *[pltpu.HBM(p.shape, p.dtype) for p in parts], *[pltpu.HBM(l.shape, l.dtype) for l in lands],
                   SDS((8, 128), F32)),
        in_specs=[HBM] * (2 * n), out_specs=(SEM, SEM, *([HBM] * (2 * n)), pl.BlockSpec(memory_space=pltpu.VMEM)),
        input_output_aliases={i: 2 + i for i in range(2 * n)},
        compiler_params=pltpu.CompilerParams(has_side_effects=EFFECT),
    )(*srcs, *lands)
    return outs[0], outs[1], list(outs[2:2 + n]), list(outs[2 + n:2 + 2 * n]), outs[2 + 2 * n]


def exchange_wait(name, send, recv, parts, lands, after):
    n = len(parts)
    afters = _as_list(after)

    def body(*refs):
        src, got = refs[:n], refs[n:2 * n]
        send_ref, recv_ref = refs[2 * n], refs[2 * n + 1]
        x, y, c, _ = _place()
        sib = (x, y, 1 - c)
        for a in range(n):
            cp = pltpu.make_async_remote_copy(src[a].at[1 - c], got[a], send_ref.at[a], recv_ref.at[a], device_id=sib, device_id_type=MESH)
            cp.wait_send()
            cp.wait_recv()

    outs = pl.pallas_call(
        body, name=name,
        out_shape=(*[pltpu.HBM(p.shape, p.dtype) for p in parts], *[pltpu.HBM(l.shape, l.dtype) for l in lands]),
        in_specs=[HBM] * (2 * n) + [SEM, SEM] + [ANY] * len(afters), out_specs=[HBM] * (2 * n),
        input_output_aliases={i: i for i in range(2 * n)},
        compiler_params=pltpu.CompilerParams(has_side_effects=EFFECT),
    )(*parts, *lands, send, recv, *afters)
    return list(outs[:n]), list(outs[n:])


def scatter_start(name, parts):
    n = len(parts)

    def body(*refs):
        src, land = refs[:n], refs[n:2 * n]
        send, recv = refs[2 * n], refs[2 * n + 1]
        token = refs[4 * n + 2]
        x, y, c, chips = _place()
        for a in range(n):
            for j, (cx, cy) in enumerate(chips):
                for sp, dp in zip(_pieces(src[a].at[2 * cx + cy]), _pieces(land[a].at[j])):
                    pltpu.make_async_remote_copy(sp, dp, send.at[3 * a + j], recv.at[3 * a + j],
                                                 device_id=(cx, cy, c), device_id_type=MESH).start()
        token[...] = jnp.zeros_like(token)

    lands = [pltpu.with_memory_space_constraint(lax.empty((NCHIP - 1,) + p.shape[1:], p.dtype), pltpu.HBM) for p in parts]
    srcs = [pltpu.with_memory_space_constraint(p, pltpu.HBM) for p in parts]
    outs = pl.pallas_call(
        body, name=name,
        out_shape=(pltpu.SemaphoreType.DMA((3 * n,)), pltpu.SemaphoreType.DMA((3 * n,)),
                   *[pltpu.HBM(p.shape, p.dtype) for p in parts], *[pltpu.HBM(l.shape, l.dtype) for l in lands],
                   SDS((8, 128), F32)),
        in_specs=[HBM] * (2 * n), out_specs=(SEM, SEM, *([HBM] * (2 * n)), pl.BlockSpec(memory_space=pltpu.VMEM)),
        input_output_aliases={i: 2 + i for i in range(2 * n)},
        compiler_params=pltpu.CompilerParams(has_side_effects=EFFECT),
    )(*srcs, *lands)
    return outs[0], outs[1], list(outs[2:2 + n]), list(outs[2 + n:2 + 2 * n]), outs[2 + 2 * n]


def scatter_wait(name, send, recv, parts, lands, after):
    n = len(parts)
    afters = _as_list(after)

    def body(*refs):
        src, land = refs[:n], refs[n:2 * n]
        send_ref, recv_ref = refs[2 * n], refs[2 * n + 1]
        x, y, c, chips = _place()
        for a in range(n):
            for j, (cx, cy) in enumerate(chips):
                cp = pltpu.make_async_remote_copy(src[a].at[2 * cx + cy], land[a].at[j], send_ref.at[3 * a + j], recv_ref.at[3 * a + j],
                                                  device_id=(cx, cy, c), device_id_type=MESH)
                cp.wait_send()
                cp.wait_recv()

    outs = pl.pallas_call(
        body, name=name,
        out_shape=(*[pltpu.HBM(p.shape, p.dtype) for p in parts], *[pltpu.HBM(l.shape, l.dtype) for l in lands]),
        in_specs=[HBM] * (2 * n) + [SEM, SEM] + [ANY] * len(afters), out_specs=[HBM] * (2 * n),
        input_output_aliases={i: i for i in range(2 * n)},
        compiler_params=pltpu.CompilerParams(has_side_effects=EFFECT),
    )(*parts, *lands, send, recv, *afters)
    return list(outs[:n]), list(outs[n:])


def join_start(name, halves):
    n = len(halves)

    def body(*refs):
        src, dst = refs[:n], refs[n:2 * n]
        send, recv = refs[2 * n], refs[2 * n + 1]
        token = refs[4 * n + 2]
        x, y, c, _ = _place()
        sib = (x, y, 1 - c)
        for a in range(n):
            for sp, dp in zip(_pieces(src[a]), _pieces(dst[a])):
                pltpu.make_async_remote_copy(sp, dp, send.at[a], recv.at[a], device_id=sib, device_id_type=MESH).start()
        token[...] = jnp.zeros_like(token)

    lands = [pltpu.with_memory_space_constraint(lax.empty(h.shape, h.dtype), pltpu.HBM) for h in halves]
    srcs = [pltpu.with_memory_space_constraint(h, pltpu.HBM) for h in halves]
    outs = pl.pallas_call(
        body, name=name,
        out_shape=(pltpu.SemaphoreType.DMA((n,)), pltpu.SemaphoreType.DMA((n,)),
                   *[pltpu.HBM(h.shape, h.dtype) for h in halves], *[pltpu.HBM(l.shape, l.dtype) for l in lands],
                   SDS((8, 128), F32)),
        in_specs=[HBM] * (2 * n), out_specs=(SEM, SEM, *([HBM] * (2 * n)), pl.BlockSpec(memory_space=pltpu.VMEM)),
        input_output_aliases={i: 2 + i for i in range(2 * n)},
        compiler_params=pltpu.CompilerParams(has_side_effects=EFFECT),
    )(*srcs, *lands)
    return outs[0], outs[1], list(outs[2:2 + n]), list(outs[2 + n:2 + 2 * n]), outs[2 + 2 * n]


def join_wait(name, send, recv, halves, lands, after):
    n = len(halves)
    afters = _as_list(after)

    def body(*refs):
        src, dst = refs[:n], refs[n:2 * n]
        send_ref, recv_ref = refs[2 * n], refs[2 * n + 1]
        x, y, c, _ = _place()
        sib = (x, y, 1 - c)
        for a in range(n):
            cp = pltpu.make_async_remote_copy(src[a], dst[a], send_ref.at[a], recv_ref.at[a], device_id=sib, device_id_type=MESH)
            cp.wait_send()
            cp.wait_recv()

    outs = pl.pallas_call(
        body, name=name,
        out_shape=(*[pltpu.HBM(h.shape, h.dtype) for h in halves], *[pltpu.HBM(l.shape, l.dtype) for l in lands]),
        in_specs=[HBM] * (2 * n) + [SEM, SEM] + [ANY] * len(afters), out_specs=[HBM] * (2 * n),
        input_output_aliases={i: i for i in range(2 * n)},
        compiler_params=pltpu.CompilerParams(has_side_effects=EFFECT),
    )(*halves, *lands, send, recv, *afters)
    return list(outs[:n]), list(outs[n:])


def gather_small(name, xs, reduce, after=None):
    m, ncol = xs.shape
    afters = _as_list(after)

    def body(x_ref, *rest):
        out_ref, all_ref, send, recv, lsem = rest[len(afters):]
        x, y, c, chips = _place()
        me, sib = (x, y, c), (x, y, 1 - c)

        def rows(px, py, pc):
            return all_ref.at[pl.ds((4 * px + 2 * py + pc) * m, m), :]

        def copy(k, block, to, src=None):
            return pltpu.make_async_remote_copy(rows(*block) if src is None else src, rows(*block), send.at[k], recv.at[k],
                                                device_id=to, device_id_type=MESH)

        mine = pltpu.make_async_copy(x_ref, rows(*me), lsem)
        mine.start()
        first = [copy(0, me, sib, src=x_ref)] + [copy(1 + j, me, (*chip, c), src=x_ref) for j, chip in enumerate(chips)]
        for cp in first:
            cp.start()
        passed = [copy(4 + j, (*chip, c), sib) for j, chip in enumerate(chips)]
        for j, chip in enumerate(chips):
            copy(1 + j, (*chip, c), me).wait_recv()
            passed[j].start()
        copy(0, sib, me).wait_recv()
        for j, chip in enumerate(chips):
            copy(4 + j, (*chip, 1 - c), me).wait_recv()
        for cp in first + passed:
            cp.wait_send()
        mine.wait()
        if reduce:
            s = all_ref[0:m, :]
            for dev in range(1, 8):
                s = s + all_ref[dev * m:(dev + 1) * m, :]
            out_ref[...] = s
        else:
            out_ref[...] = all_ref[...]

    vm = pl.BlockSpec(memory_space=pltpu.VMEM)
    return pl.pallas_call(
        body, name=name, in_specs=[vm] + [ANY] * len(afters), out_specs=vm,
        out_shape=SDS((m, ncol) if reduce else (8 * m, ncol), F32),
        scratch_shapes=[pltpu.VMEM((8 * m, ncol), F32), pltpu.SemaphoreType.DMA((7,)), pltpu.SemaphoreType.DMA((7,)),
                        pltpu.SemaphoreType.DMA],
    )(xs, *afters)


RELAYOUT_ROWS = 128


def weights_to_cat(name, land, own, place, other, prev=None, after=None):
    tm = RELAYOUT_ROWS
    nb = (D // 2) // tm
    extra = ([] if prev is None else [prev]) + _as_list(after)

    def half(p):
        return 1 - p[0] if other else p[0]

    def body(p_ref, g_ref, own_ref, *rest):
        o_ref = rest[len(extra)]
        nat = jnp.concatenate([jnp.where(p_ref[1] == j, own_ref[...], g_ref[j]) for j in range(NCHIP)], axis=1)
        pad = jnp.zeros((tm, NCAT - OA - 16), BF16)
        o_ref[...] = jnp.concatenate([nat[:, 3072:7168], nat[:, 7184:11280], nat[:, 0:3072], nat[:, 7168:7184], pad], axis=1)

    grid_spec = pltpu.PrefetchScalarGridSpec(
        num_scalar_prefetch=1, grid=(nb,),
        in_specs=[pl.BlockSpec((NCHIP, None, tm, IN_SHARD), lambda i, p: (0, half(p), i, 0)),
                  pl.BlockSpec((None, tm, IN_SHARD), lambda i, p: (half(p), i, 0))] + [ANY] * len(extra),
        out_specs=pl.BlockSpec((tm, NCAT), lambda i, p: (half(p) * nb + i, 0)))
    return pl.pallas_call(
        body, name=name, grid_spec=grid_spec, out_shape=PINNED((D, NCAT), BF16),
        input_output_aliases={} if prev is None else {3: 0},
        compiler_params=_cparams(40 * 1024 * 1024, ("arbitrary",)),
    )(place, land, own, *extra)


def grads_from_cat(gw_cat):
    tm = RELAYOUT_ROWS
    nb = (D // 2) // tm

    def body(c_ref, o_ref):
        cat = c_ref[...]
        nat = jnp.concatenate([cat[:, OU:OA], cat[:, OV:OGP], cat[:, OA:OA + 16], cat[:, OGP:OU]], axis=1)
        for j in range(NCHIP):
            o_ref[j] = nat[:, j * IN_SHARD:(j + 1) * IN_SHARD]

    return pl.pallas_call(
        body, name="grads_from_cat", grid=(D // tm,), in_specs=[pl.BlockSpec((tm, NCAT), lambda i: (i, 0))],
        out_specs=pl.BlockSpec((None, NCHIP, tm, IN_SHARD), lambda i: (i // nb, 0, i % nb, 0)),
        out_shape=PINNED((2, NCHIP, D // 2, IN_SHARD), BF16), compiler_params=_cparams(40 * 1024 * 1024, ("arbitrary",)),
    )(gw_cat)


def _pad_rows(a, rows):
    return jnp.concatenate([a, jnp.zeros((rows - a.shape[0],) + a.shape[1:], a.dtype)], axis=0)


def local_step(x2d, tgt, gf, g1, pool_scale, wa_pad, b_alpha, ng, g2, get_w, on_grad=None, on_settle=None, tick=None):
    emit = on_grad if on_grad is not None else (lambda group, grads: None)
    settle = on_settle if on_settle is not None else (lambda group, after: None)
    h1 = norm1(x2d, g1)
    wcat, pw = get_w("in", h1)
    pcat = mm_in(h1, wcat)
    pinned = tick("pool", pcat) if tick is not None else None
    dpool, ylin = pool_fwd(pcat, pw, pinned)
    og, o, states = gla_fwd(pcat, wa_pad, b_alpha, ng, ylin)
    w_go, w_o = get_w("mid", og)
    mixed, ygla = mm_gla_out(og, w_go, ylin, pcat, pool_scale)
    x2, h2 = mm_out(mixed, w_o, x2d, g2)
    w_up = get_w("up", h2)
    rup, act = mm_up(h2, w_up)
    w_dn = get_w("down", act)
    dx3, dx3b, g_nf, loss_row = mm_down(act, w_dn, x2, tgt, gf)

    gw_down = mm_wgrad("mm_dw_down", act, dx3b, DFF, D, (2, NCHIP, D // 2, D), (None, None, D // 2, D),
                       lambda j, i, k: (i % 2, i // 2, 0, 0), D // 2, D)
    token = emit("down", {"down": gw_down})
    dup = mm_dact(dx3b, w_dn, rup, after=token)
    token = settle("down", dup)
    dx2, dx2b, g_mlp = mm_dh2(dup, w_up, x2, dx3, g2, after=token)
    gw_up = mm_wgrad("mm_dw_up", h2, dup, D, DFF, (2, NCHIP, D // 2, D), (None, None, D // 2, D),
                     lambda j, i, k: (i, j, 0, 0), D // 2, D)
    token = emit("up", {"up": gw_up})
    dylin, dygla, dpcat, g_ps = mm_dmixed(dx2b, w_o, pcat, ylin, ygla, pool_scale, after=token)
    token = settle("up", dylin)
    gw_out = mm_wgrad("mm_dw_out", mixed, dx2b, D, D, (2, NCHIP, 256, D), (2, None, 256, D),
                      lambda j, i, k: (0, i, 0, 0), 512, D)
    do, dpcat, g_ng = mm_dog(dygla, w_go, o, pcat, ng, dpcat, after=token)
    gw_go = mm_wgrad("mm_dw_gla_out", og, dygla, D, D, (2, NCHIP, 256, D), (2, None, 256, D),
                     lambda j, i, k: (0, i, 0, 0), 512, D)
    token = emit("mix", {"out": gw_out, "gla_out": gw_go})
    dpcat, dv, g_wa, g_ba = gla_bwd(do, pcat, states, wa_pad, b_alpha, dpcat, b_alpha if token is None else token)
    token = settle("mix", dv)
    dpcat, dpw = pool_bwd(dylin, dpool, pw, lax.dynamic_update_slice(dpcat, dv, (0, OV)))
    gw_cat = mm_wgrad("mm_dw_in", h1, dpcat, D, NCAT, (D, NCAT), (1024, 1280), lambda j, i, k: (i, j), 1024, 1280, after=token)
    token = settle("in", emit("in", {"in_cat": gw_cat, "pool": dpw}))
    grad_x, g_mix = mm_dh1(dpcat, wcat, x2d, dx2, g1, after=token)
    return (loss_row[0, 0], grad_x, g_mix, g_ps, g_mlp, g_nf, g_ng, g_ba, g_wa, token,
            gw_cat, dpw, gw_go, gw_out, gw_up, gw_down)


def kernel(x, norm_mix_g, w_in, pool_w, pool_scale, w_alpha, b_alpha, gla_norm_g, w_gla_out, w_out, norm_mlp_g, w_mlp_up, w_mlp_down, norm_final_g, loss_target, m_norm_mix_g, m_w_in, m_pool_w, m_pool_scale, m_w_alpha, m_b_alpha, m_gla_norm_g, m_w_gla_out, m_w_out, m_norm_mlp_g, m_w_mlp_up, m_w_mlp_down, m_norm_final_g, v_norm_mix_g, v_w_in, v_pool_w, v_pool_scale, v_w_alpha, v_b_alpha, v_gla_norm_g, v_w_gla_out, v_w_out, v_norm_mlp_g, v_w_mlp_up, v_w_mlp_down, v_norm_final_g):
    chip = 2 * lax.axis_index("x") + lax.axis_index("y")
    chip_i = chip.astype(jnp.int32).reshape(1)
    core_i = lax.axis_index("c").astype(jnp.int32).reshape(1)
    place_i = jnp.concatenate([core_i, chip_i])
    tgt = loss_target.reshape(T, D)
    gf = norm_final_g.reshape(1, D)

    def halves(w2d):
        r, c = w2d.shape
        return lax.dynamic_update_index_in_dim(lax.empty((NCHIP, 2, r // 2, c), BF16), w2d.astype(BF16).reshape(2, r // 2, c),
                                               chip, 0)

    pool_shard = pool_w.reshape(4 * PG, PO // NCHIP)
    w_in_r = w_in.reshape(2, D // 2, IN_SHARD)
    sent = {"in": [cast_bf16("cast_w_in", w_in_r), halves(pool_shard)]}
    flight = {}

    def start(group, after=None):
        flight[group] = gather_start("gather_start_" + group, sent[group], after)

    def relay(group, after):
        send, recv, shards, lands = flight[group]
        flight[group] = relay_turn("relay_turn_" + group, send, recv, shards, lands, after)

    def fetch(group, after):
        send, recv, shards, lands = flight[group]
        lands = relay_wait("relay_wait_" + group, send, recv, lands, after)
        return forward_halves("forward_" + group, lands)

    small_w = pack_rows("pack_small_w", [w_alpha[0].reshape(4, QK),
                                         jnp.concatenate([gla_norm_g[0].reshape(1, 512), jnp.zeros((1, 512), F32)], axis=1)], 8)
    sw_all = gather_small("gather_small_w", small_w, False).reshape(8, 8, QK)
    start("in", sw_all)
    m_in_f, v_in_f, w_go_f, w_o_f, w_up_f, w_dn_f, x_f = lax.optimization_barrier(
        (m_w_in, v_w_in, w_gla_out, w_out, w_mlp_up, w_mlp_down, x, flight["in"][2][0]))[:7]
    m_in_r, v_in_r = m_in_f.reshape(2, D // 2, IN_SHARD), v_in_f.reshape(2, D // 2, IN_SHARD)
    sent["mid"] = [halves(w_go_f[0]), halves(w_o_f[0])]
    relay("in", [m_in_r, v_in_r, *sent["mid"]])
    w_up_f, w_dn_f, x_f = lax.optimization_barrier((w_up_f, w_dn_f, x_f, flight["in"][3][0]))[:3]
    sent["up"] = [cast_to_slot("cast_w_up", w_up_f[0].reshape(2, D // 2, D), place_i)]
    sent["down"] = [cast_to_slot("cast_w_down", w_dn_f[0].reshape(2, DFF // NCHIP // 2, D), place_i)]
    x2d = x_f.reshape(T, D)
    big = [w_in_r, w_go_f[0], w_o_f[0], w_up_f[0], w_dn_f[0], pool_shard]

    def tick(point, after):
        if point == "pool":
            relay("mid", after)
            relay("up", flight["mid"][3][0])
            start("down", flight["up"][3][0])
            return [flight["up"][3][0], flight["down"][3][0]]

    def get_w(group, after):
        if group == "in":
            after = [after, *sent["up"], *sent["down"], wa_pad]
        if group == "up":
            relay("down", after)
            send, recv, lands, shards = flight["up"]
            lands = forward_wait("forward_wait_up", send, recv, lands, flight["down"][3][0])
            return lands[0].reshape(NCHIP, D, D)
        if group == "in":
            send, recv, shards, lands = flight["in"]
            send, recv, lands = forward_turn("forward_turn_in", send, recv, lands, after)
            start("mid", lands[0])
            start("up", flight["mid"][3][0])
            wcat = weights_to_cat("weights_to_cat_mine", lands[0], shards[0], place_i, False, after=flight["up"][3][0])
            lands = forward_wait("forward_wait_in", send, recv, lands, wcat)
            wcat = weights_to_cat("weights_to_cat_sibling", lands[0], shards[0], place_i, True, prev=wcat)
            g_pool = lands[1]
            pw = jnp.concatenate([g_pool[j].reshape(4, PG, PO // NCHIP) for j in range(NCHIP)], axis=2)
            return wcat, pw
        whole = fetch(group, after)
        if group == "mid":
            send, recv, shards, lands = flight["up"]
            flight["up"] = (*forward_turn("forward_turn_up", send, recv, lands, whole[0]), shards)
            w_go, w_o, _ = lax.optimization_barrier((whole[0], whole[1], flight["up"][2][0]))
            return w_go.reshape(D, D), w_o.reshape(D, D)
        return whole[0].reshape(DFF, D)

    wa_full = jnp.concatenate([sw_all[2 * j, 0:4].reshape(16, DK) for j in range(NCHIP)], axis=1)
    ng_full = jnp.concatenate([sw_all[2 * j, 4, 0:512].reshape(HEADS, DV // NCHIP) for j in range(NCHIP)], axis=1)
    wa_pad = _pad_rows(wa_full, APAD).astype(BF16)
    ng = ng_full.reshape(1, D)

    pending = {}
    wmv = {"in": (w_in_r, m_in_r, v_in_r), "gla_out": (big[1], m_w_gla_out, v_w_gla_out), "out": (big[2], m_w_out, v_w_out),
           "up": (big[3], m_w_mlp_up, v_w_mlp_up), "down": (big[4], m_w_mlp_down, v_w_mlp_down), "pool": (big[5], m_pool_w, v_pool_w)}
    big_res = {}

    def reduce_group(group, after):
        nms, send, recv, sums, lands = pending[group]
        sums, lands = scatter_wait("scatter_wait_" + group, send, recv, sums, lands, after)
        reduced = [sum_chips("sum_chips_" + nm, a, b, chip_i) for nm, a, b in zip(nms, sums, lands)]
        send, recv, reduced, lands, token = join_start("join_start_" + group, reduced)
        pending[group] = (nms, send, recv, reduced, lands)
        return token

    def update_group(group, after):
        nms, send, recv, reduced, lands = pending[group]
        reduced, from_sib = join_wait("join_wait_" + group, send, recv, reduced, lands, after)
        for nm, g_own, g_sib in zip(nms, reduced, from_sib):
            w, m, v = wmv[nm]
            shp = (2,) + g_own.shape
            big_res[nm] = adamw_halves("adamw_" + nm, w.reshape(shp), g_own, g_sib, m.reshape(shp), v.reshape(shp), core_i)

    def on_grad(group, grads):
        if group == "in":
            gw_in = grads_from_cat(grads["in_cat"])
            gw_pool = jnp.stack([grads["pool"][:, :, j * 128:(j + 1) * 128].reshape(2, 2 * PG, 128)
                                 for j in range(NCHIP)], axis=1)
            grads = {"in": gw_in, "pool": gw_pool}
        nms, parts = list(grads.keys()), list(grads.values())
        send, recv, parts, got, token = exchange_start("exchange_start_" + group, parts)
        pending[group] = (nms, send, recv, parts, got)
        return token

    def on_settle(group, after):
        if group == "in":
            for earlier in ("down", "up", "mix"):
                after = reduce_group(earlier, after)
        nms, send, recv, parts, got = pending[group]
        parts, got = exchange_wait("exchange_wait_" + group, send, recv, parts, got, after)
        sums = [add_pairs("add_pair_" + nm, a, b, core_i) for nm, a, b in zip(nms, parts, got)]
        send, recv, sums, lands, token = scatter_start("scatter_start_" + group, sums)
        pending[group] = (nms, send, recv, sums, lands)
        if group != "in":
            return token
        for earlier in ("down", "up", "mix"):
            update_group(earlier, token)
            token = big_res[pending[earlier][0][-1]][1]
        return [big_res[nm][1] for nm in ("down", "up", "out", "gla_out")]

    (loss_local, grad_x, g_mix, g_ps, g_mlp, g_nf, g_ng, g_ba, g_wa) = local_step(
        x2d, tgt, gf, norm_mix_g, pool_scale, wa_pad, b_alpha, ng, norm_mlp_g, get_w, on_grad, on_settle, tick)[:9]
    loss = lax.psum(loss_local, ("x", "y", "c"))
    join_in_token = reduce_group("in", grad_x)

    ROWS = 16

    def wide(a, n):
        return jnp.concatenate([a.reshape(1, n), jnp.zeros((1, D - n), F32)], axis=1)

    packed = pack_rows("pack_small_g", [g_mix, g_ps, g_mlp, g_nf, g_ng, wide(g_ba, QK), g_wa[0:16].reshape(8, D)], ROWS)
    tot = gather_small("reduce_small_g", packed, True, join_in_token)
    t_wa = lax.dynamic_slice(tot[6:14].reshape(16, QK), (0, chip * DK), (16, DK))
    t_ng = lax.dynamic_slice(tot[4].reshape(HEADS, DV), (0, chip * (DV // NCHIP)), (HEADS, DV // NCHIP))

    def pack_small(nm, mix, ps, mlp, nf, ba, wa, gn, after=None):
        return pack_rows(nm, [mix.reshape(1, D), ps.reshape(1, D), mlp.reshape(1, D), nf.reshape(1, D), wide(ba, QK),
                              wa.reshape(2, D), wide(gn, 512)], ROWS, after)

    update_group("in", tot)
    sg = pack_small("pack_g", tot[0], tot[1], tot[2], tot[3], tot[5, 0:QK], t_wa, t_ng, big_res["in"][3])
    sw = pack_small("pack_w", norm_mix_g, pool_scale, norm_mlp_g, norm_final_g, b_alpha, w_alpha, gla_norm_g)
    sm = pack_small("pack_m", m_norm_mix_g, m_pool_scale, m_norm_mlp_g, m_norm_final_g, m_b_alpha, m_w_alpha, m_gla_norm_g)
    sv = pack_small("pack_v", v_norm_mix_g, v_pool_scale, v_norm_mlp_g, v_norm_final_g, v_b_alpha, v_w_alpha, v_gla_norm_g)
    small_res = adamw("adamw_small", sw, sg, sm, sv)

    def unpack(p):
        return {"norm_mix_g": p[0].reshape(1, D), "pool_scale": p[1].reshape(1, D), "norm_mlp_g": p[2].reshape(1, D),
                "norm_final_g": p[3].reshape(D), "b_alpha": p[4, 0:QK].reshape(1, QK), "w_alpha": p[5:7].reshape(1, 16, DK),
                "gla_norm_g": p[7, 0:512].reshape(1, HEADS, DV // NCHIP)}

    order = ["norm_mix_g", "w_in", "pool_w", "pool_scale", "w_alpha", "b_alpha", "gla_norm_g", "w_gla_out", "w_out",
             "norm_mlp_g", "w_mlp_up", "w_mlp_down", "norm_final_g"]
    big_key = {"w_in": ("in", w_in.shape), "pool_w": ("pool", pool_w.shape), "w_gla_out": ("gla_out", w_gla_out.shape),
               "w_out": ("out", w_out.shape), "w_mlp_up": ("up", w_mlp_up.shape), "w_mlp_down": ("down", w_mlp_down.shape)}
    result = [loss, grad_x.reshape(1, T, D)]
    for kind in range(4):
        small = unpack(small_res[kind])
        for nm in order:
            if nm in big_key:
                key, shp = big_key[nm]
                result.append(big_res[key][kind].reshape(shp))
            else:
                result.append(small[nm])
    return tuple(result)
```

```python
import itertools

import jax
import jax.numpy as jnp
from jax import lax
from jax.experimental import pallas as pl
from jax.experimental.pallas import tpu as pltpu

F32 = jnp.float32
BF16 = jnp.bfloat16
SDS = jax.ShapeDtypeStruct
PINNED = pltpu.HBM
MESH = pl.DeviceIdType.MESH
ANY = pl.BlockSpec(memory_space=pl.ANY)

T = 2048
D = 2048
DFF = 8192
NCHIP = 4
IN_WIDTH = 11280
IN_SHARD = IN_WIDTH // NCHIP
CHUNK = 64
NCHUNK = T // CHUNK
HEADS = 4
DK = 256
DV = 512
QK = HEADS * DK
EPS = 1e-6
POOL_WINDOWS = (2, 4, 8, 16)
PG = 256
PO = 512

OV, OG, OGP, OGG, OU, OQ, OKK, OA = 0, 2048, 4096, 6144, 8192, 9216, 10240, 11264
NCAT = 11520
APAD = 128

VMEM_CAP = 56 * 1024 * 1024

PIECE_BYTES = 384 * 1024

ADAM_LR, ADAM_B1, ADAM_B2, ADAM_EPS, ADAM_WD, ADAM_STEP = 0.001, 0.9, 0.999, 1e-08, 0.01, 10


def _cparams(vmem_bytes=None, sem=None):
    kw = {}
    if vmem_bytes is not None:
        kw["vmem_limit_bytes"] = int(min(max(vmem_bytes, 32 * 1024 * 1024), VMEM_CAP))
    if sem is not None:
        kw["dimension_semantics"] = sem
    return pltpu.CompilerParams(**kw)


def _nbytes(shape, dtype):
    n = 1
    for s in shape:
        if s is not None:
            n *= s
    return n * jnp.dtype(dtype).itemsize


def _sigmoid(x):
    return 0.5 * jnp.tanh(0.5 * x) + 0.5


GLA_STEP = 4
EPI_COLS = 512


def _as_list(after):
    if after is None:
        return []
    return list(after) if isinstance(after, (list, tuple)) else [after]


def _in_hbm(*arrays):
    return [pltpu.with_memory_space_constraint(a, pltpu.HBM) for a in arrays]


def matmul(name, a, b, *, a_spec, b_spec, cdims, grid, acc_shape, outs, extras=(), epi, after=None, into=None):
    nj, ni, nk = grid
    ne, no = len(extras), len(outs)
    afters = _as_list(after) + ([] if into is None else [into[0]])
    first_out = 2 + ne + len(afters)

    def body(*refs):
        a_ref, b_ref = refs[0], refs[1]
        ex = refs[2:2 + ne]
        out_refs = refs[first_out:first_out + no]
        i = pl.program_id(1)
        part = lax.dot_general(a_ref[...], b_ref[...], (cdims, ((), ())), preferred_element_type=F32)
        if nk == 1:
            epi(part, ex, out_refs, i)
        else:
            acc_ref = refs[first_out + no]
            k = pl.program_id(2)

            @pl.when(k == 0)
            def _():
                acc_ref[...] = part

            @pl.when(k > 0)
            def _():
                acc_ref[...] += part

            @pl.when(k == nk - 1)
            def _():
                epi(acc_ref[...], ex, out_refs, i)

    in_specs = [pl.BlockSpec(*a_spec), pl.BlockSpec(*b_spec)] + [pl.BlockSpec(bs, im) for _, bs, im in extras]
    in_specs += [ANY] * len(afters)
    out_specs = [pl.BlockSpec(bs, im) for _, _, bs, im in outs]
    out_shape = [PINNED(s, dt) for s, dt, _, _ in outs]
    vm = 2 * (_nbytes(a_spec[0], a.dtype) + _nbytes(b_spec[0], b.dtype))
    vm += 2 * sum(_nbytes(bs, arr.dtype) for arr, bs, _ in extras)
    vm += 2 * sum(_nbytes(bs, dt) for _, dt, bs, _ in outs)
    vm += 6 * _nbytes(acc_shape, F32)
    scratch = [pltpu.VMEM(acc_shape, F32)] if nk > 1 else []
    return pl.pallas_call(
        body, name=name, grid=grid, in_specs=in_specs, out_specs=out_specs, out_shape=out_shape,
        scratch_shapes=scratch,
        input_output_aliases={} if into is None else {first_out - 1: into[1]},
        compiler_params=_cparams(vm, ("arbitrary", "arbitrary", "arbitrary")),
    )(*_in_hbm(a, b, *[arr for arr, _, _ in extras]), *afters)


NN =((1,), (0,))
NT = ((1,), (1,))
TN = ((0,), (0,))


def _row_acc(out_ref, val, i):
    @pl.when(i == 0)
    def _():
        out_ref[...] = val

    @pl.when(i > 0)
    def _():
        out_ref[...] += val


def _rms_bwd(xn, r, dxn):
    return r * (dxn - xn * jnp.mean(dxn * xn, axis=-1, keepdims=True))


def norm1(x, g):
    tm = 256

    def body(x_ref, g_ref, h_ref):
        xv = x_ref[...]
        r = lax.rsqrt(jnp.mean(xv * xv, axis=-1, keepdims=True) + EPS)
        h_ref[...] = (xv * r * g_ref[...]).astype(BF16)

    return pl.pallas_call(
        body, name="norm1", grid=(T // tm,),
        in_specs=[pl.BlockSpec((tm, D), lambda i: (i, 0)), pl.BlockSpec((1, D), lambda i: (0, 0))],
        out_specs=pl.BlockSpec((tm, D), lambda i: (i, 0)), out_shape=PINNED((T, D), BF16),
        compiler_params=_cparams(32 * 1024 * 1024, ("arbitrary",)),
    )(*_in_hbm(x, g))


def mm_in(h1, wcat):
    tm, tn = 1024, 1280

    def epi(acc, ex, outs, i):
        outs[0][...] = acc.astype(BF16)

    return matmul("mm_in", h1, wcat, a_spec=((tm, D), lambda j, i, k: (i, 0)), b_spec=((D, tn), lambda j, i, k: (0, j)),
                  cdims=NN, grid=(NCAT // tn, T // tm, 1), acc_shape=(tm, tn),
                  outs=[((T, NCAT), BF16, (tm, tn), lambda j, i, k: (i, j))], epi=epi)[0]


def _window_sum(x, w, up):
    n = x.shape[0]
    row = lax.broadcasted_iota(jnp.int32, x.shape, 0)
    s, sh = x, 1
    while sh < w:
        if up:
            s = s + jnp.where(row < n - sh, pltpu.roll(s, n - sh, axis=0), 0.0)
        else:
            s = s + jnp.where(row >= sh, pltpu.roll(s, sh, axis=0), 0.0)
        sh *= 2
    return s


def _inv_count(shape, w):
    row = lax.broadcasted_iota(jnp.int32, shape, 0)
    return 1.0 / jnp.minimum(row + 1, w).astype(F32)


def pool_fwd(pcat, pw, after=None):
    afters = _as_list(after)

    def body(u_ref, pw_ref, *rest):
        d_ref, y_ref = rest[len(afters):]
        for gi, w in enumerate(POOL_WINDOWS):
            ug = u_ref[:, gi * PG:(gi + 1) * PG].astype(F32)
            dg = _window_sum(ug, w, False) * _inv_count(ug.shape, w) - ug
            db = dg.astype(BF16)
            d_ref[:, gi * PG:(gi + 1) * PG] = db
            y_ref[:, gi * PO:(gi + 1) * PO] = jnp.dot(db, pw_ref[gi], preferred_element_type=F32).astype(BF16)

    return pl.pallas_call(
        body, name="pool_fwd", grid=(1,),
        in_specs=[pl.BlockSpec((T, 4 * PG), lambda i: (0, OU // (4 * PG))), pl.BlockSpec((4, PG, PO), lambda i: (0, 0, 0))]
                 + [ANY] * len(afters),
        out_specs=[pl.BlockSpec((T, 4 * PG), lambda i: (0, 0)), pl.BlockSpec((T, D), lambda i: (0, 0))],
        out_shape=[PINNED((T, 4 * PG), BF16), PINNED((T, D), BF16)],
        compiler_params=_cparams(48 * 1024 * 1024, ("arbitrary",)),
    )(pcat, pw, *afters)


def pool_bwd(dylin, d, pw, dproj):
    assert OU % (4 * PG) == 0

    def body(dy_ref, d_ref, pw_ref, held_ref, du_ref, dpw_ref):
        for gi, w in enumerate(POOL_WINDOWS):
            dyl = dy_ref[:, gi * PO:(gi + 1) * PO]
            dd = lax.dot_general(dyl, pw_ref[gi], (NT, ((), ())), preferred_element_type=F32)
            du = _window_sum(dd * _inv_count(dd.shape, w), w, True) - dd
            du_ref[:, gi * PG:(gi + 1) * PG] = du.astype(BF16)
            dpw_ref[gi] = lax.dot_general(d_ref[:, gi * PG:(gi + 1) * PG], dyl, (TN, ((), ())),
                                          preferred_element_type=F32).astype(BF16)

    return pl.pallas_call(
        body, name="pool_bwd", grid=(1,),
        in_specs=[pl.BlockSpec((T, D), lambda i: (0, 0)), pl.BlockSpec((T, 4 * PG), lambda i: (0, 0)),
                  pl.BlockSpec((4, PG, PO), lambda i: (0, 0, 0)), ANY],
        out_specs=[pl.BlockSpec((T, 4 * PG), lambda i: (0, OU // (4 * PG))), pl.BlockSpec((4, PG, PO), lambda i: (0, 0, 0))],
        out_shape=[PINNED((T, NCAT), BF16), PINNED((4, PG, PO), BF16)],
        input_output_aliases={3: 0},
        compiler_params=_cparams(48 * 1024 * 1024, ("arbitrary",)),
    )(dylin, d, pw, dproj)


def _gate_decay(alow, wa, ba):
    a = jnp.dot(alow, wa, preferred_element_type=F32) + ba
    ls = jax.nn.log_sigmoid(a) * (1.0 / 16.0)
    r = lax.broadcasted_iota(jnp.int32, (CHUNK, CHUNK), 0)
    c = lax.broadcasted_iota(jnp.int32, (CHUNK, CHUNK), 1)
    tri = jnp.where(c <= r, 1.0, 0.0).astype(F32)
    cum = jnp.dot(tri, ls, preferred_element_type=F32, precision=lax.Precision.HIGHEST)
    last = cum[CHUNK - 1:CHUNK, :]
    return a, jnp.exp(last - cum), jnp.exp(last)


def gla_fwd(pcat, wa, ba, ng, after=None):
    afters = _as_list(after)

    def body(q_ref, k_ref, v_ref, g_ref, al_ref, wa_ref, ba_ref, ng_ref, *rest):
        og_ref, o_ref, st_ref, s_scr = rest[len(afters):]

        @pl.when(pl.program_id(0) == 0)
        def _():
            s_scr[...] = jnp.zeros_like(s_scr)

        state = [s_scr[h] for h in range(HEADS)]
        for s in range(GLA_STEP):
            rs = slice(s * CHUNK, (s + 1) * CHUNK)
            _, e, decay = _gate_decay(al_ref[rs, :], wa_ref[...], ba_ref[...])
            kd = (k_ref[rs, :].astype(F32) * e).astype(BF16)
            qs = (q_ref[rs, :].astype(F32) * (DK ** -0.5)).astype(BF16)
            for h in range(HEADS):
                ck = slice(h * DK, (h + 1) * DK)
                cv = slice(h * DV, (h + 1) * DV)
                state[h] = state[h] * decay[:, ck] + lax.dot_general(v_ref[rs, cv], kd[:, ck], (TN, ((), ())),
                                                                     preferred_element_type=F32)
                sb = state[h].astype(BF16)
                st_ref[s, h] = sb
                oh = lax.dot_general(qs[:, ck], sb, (NT, ((), ())), preferred_element_type=F32)
                o_ref[rs, cv] = oh.astype(BF16)
                on = oh * lax.rsqrt(jnp.mean(oh * oh, axis=-1, keepdims=True) + EPS) * ng_ref[:, cv]
                gv = g_ref[rs, cv].astype(F32)
                og_ref[rs, cv] = (on * (gv * _sigmoid(gv))).astype(BF16)
        for h in range(HEADS):
            s_scr[h] = state[h]

    row = lambda c: (c, 0)
    rows = GLA_STEP * CHUNK
    return pl.pallas_call(
        body, name="gla_fwd", grid=(NCHUNK // GLA_STEP,),
        in_specs=[pl.BlockSpec((rows, QK), lambda c: (c, OQ // QK)), pl.BlockSpec((rows, QK), lambda c: (c, OKK // QK)),
                  pl.BlockSpec((rows, D), lambda c: (c, OV // D)), pl.BlockSpec((rows, D), lambda c: (c, OG // D)),
                  pl.BlockSpec((rows, APAD), lambda c: (c, OA // APAD)),
                  pl.BlockSpec((APAD, QK), lambda c: (0, 0)), pl.BlockSpec((1, QK), lambda c: (0, 0)),
                  pl.BlockSpec((1, D), lambda c: (0, 0))] + [ANY] * len(afters),
        out_specs=[pl.BlockSpec((rows, D), row), pl.BlockSpec((rows, D), row),
                   pl.BlockSpec((GLA_STEP, HEADS, DV, DK), lambda c: (c, 0, 0, 0))],
        out_shape=[PINNED((T, D), BF16), PINNED((T, D), BF16), PINNED((NCHUNK, HEADS, DV, DK), BF16)],
        scratch_shapes=[pltpu.VMEM((HEADS, DV, DK), F32)],
        compiler_params=_cparams(32 * 1024 * 1024, ("arbitrary",)),
    )(*_in_hbm(pcat, pcat, pcat, pcat, pcat, wa, ba, ng), *afters)


def gla_bwd(do, pcat, states, wa, ba, dproj, after):
    tail = NCAT - OQ
    assert (OKK, OA) == (OQ + QK, OQ + 2 * QK) and OQ % tail == 0

    def body(do_ref, q_ref, k_ref, v_ref, al_ref, sc_ref, sp_ref, wa_ref, ba_ref, after_ref, held_ref,
             dp_ref, dv_ref, dwa_ref, dba_ref, ds_scr):
        i = pl.program_id(0)
        dp_ref[:, 2 * QK + APAD:] = jnp.zeros((GLA_STEP * CHUNK, tail - 2 * QK - APAD), BF16)

        @pl.when(i == 0)
        def _():
            ds_scr[...] = jnp.zeros_like(ds_scr)

        ds = [ds_scr[h] for h in range(HEADS)]
        dwa, dba = 0.0, 0.0
        for u in reversed(range(GLA_STEP)):
            rs = slice(u * CHUNK, (u + 1) * CHUNK)
            first_chunk = jnp.logical_and(i == NCHUNK // GLA_STEP - 1, u == 0)
            has_prev = jnp.where(first_chunk, 0.0, 1.0).astype(F32)
            a, e, decay = _gate_decay(al_ref[rs, :], wa_ref[...], ba_ref[...])
            kdf = k_ref[rs, :].astype(F32) * e
            kd = kdf.astype(BF16)
            qs = (q_ref[rs, :].astype(F32) * (DK ** -0.5)).astype(BF16)
            dkd_parts, ddecay_parts = [], []
            for h in range(HEADS):
                ck = slice(h * DK, (h + 1) * DK)
                cv = slice(h * DV, (h + 1) * DV)
                doh = do_ref[rs, cv]
                dsh = ds[h] + lax.dot_general(doh, qs[:, ck], (TN, ((), ())), preferred_element_type=F32)
                dsb = dsh.astype(BF16)
                dp_ref[rs, ck] = (jnp.dot(doh, sc_ref[u, h], preferred_element_type=F32) * (DK ** -0.5)).astype(BF16)
                dkd_parts.append(jnp.dot(v_ref[rs, cv], dsb, preferred_element_type=F32))
                dv_ref[rs, cv] = lax.dot_general(kd[:, ck], dsb, (NT, ((), ())), preferred_element_type=F32).astype(BF16)
                s_prev = (sp_ref[h] if u == 0 else sc_ref[u - 1, h]).astype(F32)
                ddecay_parts.append(jnp.sum(dsh * s_prev, axis=0, keepdims=True) * has_prev)
                ds[h] = dsh * decay[:, ck]
            dkd = jnp.concatenate(dkd_parts, axis=1)
            ddecay = jnp.concatenate(ddecay_parts, axis=1)
            dp_ref[rs, QK:2 * QK] = (dkd * e).astype(BF16)
            dearg = dkd * kdf
            dlast = jnp.sum(dearg, axis=0, keepdims=True) + ddecay * decay
            r = lax.broadcasted_iota(jnp.int32, (CHUNK, CHUNK), 0)
            c = lax.broadcasted_iota(jnp.int32, (CHUNK, CHUNK), 1)
            triu = jnp.where(c >= r, 1.0, 0.0).astype(F32)
            dls = dlast - jnp.dot(triu, dearg, preferred_element_type=F32, precision=lax.Precision.HIGHEST)
            da = dls * (1.0 / 16.0) * (1.0 - _sigmoid(a))
            dab = da.astype(BF16)
            dp_ref[rs, 2 * QK:2 * QK + APAD] = lax.dot_general(dab, wa_ref[...], (NT, ((), ())),
                                                               preferred_element_type=F32).astype(BF16)
            dwa = dwa + lax.dot_general(al_ref[rs, :], dab, (TN, ((), ())), preferred_element_type=F32)
            dba = dba + jnp.sum(da, axis=0, keepdims=True)
        for h in range(HEADS):
            ds_scr[h] = ds[h]

        @pl.when(i == 0)
        def _():
            dwa_ref[...] = dwa
            dba_ref[...] = dba

        @pl.when(i > 0)
        def _():
            dwa_ref[...] += dwa
            dba_ref[...] += dba

    rows = GLA_STEP * CHUNK
    rev = lambda i: NCHUNK // GLA_STEP - 1 - i
    return pl.pallas_call(
        body, name="gla_bwd", grid=(NCHUNK // GLA_STEP,),
        in_specs=[pl.BlockSpec((rows, D), lambda i: (rev(i), 0)),
                  pl.BlockSpec((rows, QK), lambda i: (rev(i), OQ // QK)), pl.BlockSpec((rows, QK), lambda i: (rev(i), OKK // QK)),
                  pl.BlockSpec((rows, D), lambda i: (rev(i), OV // D)), pl.BlockSpec((rows, APAD), lambda i: (rev(i), OA // APAD)),
                  pl.BlockSpec((GLA_STEP, HEADS, DV, DK), lambda i: (rev(i), 0, 0, 0)),
                  pl.BlockSpec((None, HEADS, DV, DK), lambda i: (jnp.maximum(rev(i) * GLA_STEP - 1, 0), 0, 0, 0)),
                  pl.BlockSpec((APAD, QK), lambda i: (0, 0)), pl.BlockSpec((1, QK), lambda i: (0, 0)), ANY, ANY],
        out_specs=[pl.BlockSpec((rows, tail), lambda i: (rev(i), OQ // tail)), pl.BlockSpec((rows, D), lambda i: (rev(i), 0)),
                   pl.BlockSpec((APAD, QK), lambda i: (0, 0)), pl.BlockSpec((1, QK), lambda i: (0, 0))],
        out_shape=[PINNED((T, NCAT), BF16), PINNED((T, D), BF16), PINNED((APAD, QK), F32), PINNED((1, QK), F32)],
        scratch_shapes=[pltpu.VMEM((HEADS, DV, DK), F32)],
        input_output_aliases={10: 0},
        compiler_params=_cparams(32 * 1024 * 1024, ("arbitrary",)),
    )(*_in_hbm(do, pcat, pcat, pcat, pcat, states, states, wa, ba), after, dproj)


TMF = 256
TMW = 512
_rowblk = ((TMF, D), lambda j, i, k: (i, 0))
_vec = ((1, D), lambda j, i, k: (0, 0))


def _full_spec(col):
    return ((TMF, D), lambda j, i, k: (i, col))


TBIG = 1024


def square_matmul(name, a, b, *, a_spec, b_spec, cdims, nk, after=None):
    def epi(acc, ex, outs, i):
        outs[0][...] = acc

    return matmul(name, a, b, a_spec=a_spec, b_spec=b_spec, cdims=cdims, grid=(D // TBIG, T // TBIG, nk),
                  acc_shape=(TBIG, TBIG), outs=[((T, D), F32, (TBIG, TBIG), lambda j, i, k: (i, j))], epi=epi,
                  after=after)[0]


def rowwise(name, y, *, extras, outs, epi):
    ne = len(extras)

    def body(*refs):
        epi(refs[0][...], refs[1:1 + ne], refs[1 + ne:], pl.program_id(1))

    in_specs = [pl.BlockSpec(*_rowblk)] + [pl.BlockSpec(bs, im) for _, bs, im in extras]
    return pl.pallas_call(
        body, name=name, grid=(1, T // TMF, 1), in_specs=in_specs,
        out_specs=[pl.BlockSpec(bs, im) for _, _, bs, im in outs], out_shape=[PINNED(s, dt) for s, dt, _, _ in outs],
        compiler_params=_cparams(40 * 1024 * 1024, ("arbitrary", "arbitrary", "arbitrary")),
    )(*_in_hbm(y, *[arr for arr, _, _ in extras]))


def mm_gla_out(og, w, ylin, pcat, pscale):
    def epi(acc, ex, outs, i):
        ylin_ref, lgp_ref, lgg_ref, ps_ref = ex
        for c0 in range(0, D, EPI_COLS):
            cs = slice(c0, c0 + EPI_COLS)
            gp = _sigmoid(lgp_ref[:, cs].astype(F32))
            gg = _sigmoid(lgg_ref[:, cs].astype(F32))
            a = acc[:, cs]
            outs[0][:, cs] = (gp * (ylin_ref[:, cs].astype(F32) * ps_ref[:, cs]) + gg * a).astype(BF16)
            outs[1][:, cs] = a.astype(BF16)

    return matmul("mm_gla_out", og, w, a_spec=_rowblk, b_spec=((D, D), lambda j, i, k: (0, 0)), cdims=NN,
                  grid=(1, T // TMF, 1), acc_shape=(TMF, D),
                  extras=[(ylin, *_rowblk), (pcat, *_full_spec(OGP // D)), (pcat, *_full_spec(OGG // D)), (pscale, *_vec)],
                  outs=[((T, D), BF16, *_rowblk), ((T, D), BF16, *_rowblk)], epi=epi)


def mm_out(mixed, w, x, g2):
    def epi(acc, ex, outs, i):
        x_ref, g_ref = ex
        x2 = x_ref[...] + acc
        r = lax.rsqrt(jnp.mean(x2 * x2, axis=-1, keepdims=True) + EPS)
        outs[0][...] = x2
        outs[1][...] = (x2 * r * g_ref[...]).astype(BF16)

    return matmul("mm_out", mixed, w, a_spec=_rowblk, b_spec=((D, D), lambda j, i, k: (0, 0)), cdims=NN,
                  grid=(1, T // TMF, 1), acc_shape=(TMF, D), extras=[(x, *_rowblk), (g2, *_vec)],
                  outs=[((T, D), F32, *_rowblk), ((T, D), BF16, *_rowblk)], epi=epi)


def mm_up(h2, wup):
    def epi(acc, ex, outs, i):
        r = jnp.maximum(acc, 0.0)
        outs[0][...] = r.astype(BF16)
        outs[1][...] = (r * r).astype(BF16)

    blk = ((TMW, D), lambda j, i, k: (i, j))
    return matmul("mm_up", h2, wup, a_spec=((TMW, D), lambda j, i, k: (i, 0)), b_spec=((None, D, D), lambda j, i, k: (j, 0, 0)),
                  cdims=NN, grid=(NCHIP, T // TMW, 1), acc_shape=(TMW, D),
                  outs=[((T, DFF), BF16, *blk), ((T, DFF), BF16, *blk)], epi=epi)


def mm_down(act, wdown, x2, tgt, gf):
    tk = 4096

    def epi(acc, ex, outs, i):
        x2_ref, t_ref, g_ref = ex
        dx_ref, dxb_ref, gnf_ref, loss_ref = outs
        x3 = x2_ref[...] + acc
        r = lax.rsqrt(jnp.mean(x3 * x3, axis=-1, keepdims=True) + EPS)
        xn = x3 * r
        err = xn * g_ref[...] - t_ref[...]
        lsum = 0.5 * jnp.sum(jnp.mean(err * err, axis=-1, keepdims=True), axis=0, keepdims=True)
        dy = err * (1.0 / D)
        _row_acc(gnf_ref, jnp.sum(dy * xn, axis=0, keepdims=True), i)
        _row_acc(loss_ref, jnp.broadcast_to(lsum, (1, 128)), i)
        dx3 = _rms_bwd(xn, r, dy * g_ref[...])
        dx_ref[...] = dx3
        dxb_ref[...] = dx3.astype(BF16)

    y = square_matmul("mm_down", act, wdown, a_spec=((TBIG, tk), lambda j, i, k: (i, k)),
                      b_spec=((tk, TBIG), lambda j, i, k: (k, j)), cdims=NN, nk=DFF // tk)
    return rowwise("rows_final", y, extras=[(x2, *_rowblk), (tgt, *_rowblk), (gf, *_vec)],
                   outs=[((T, D), F32, *_rowblk), ((T, D), BF16, *_rowblk), ((1, D), F32, *_vec),
                         ((1, 128), F32, (1, 128), lambda j, i, k: (0, 0))], epi=epi)


def mm_dact(dx3b, wdown, rup, after=None):
    def epi(acc, ex, outs, i):
        outs[0][...] = (acc * 2.0 * ex[0][...].astype(F32)).astype(BF16)

    blk = ((TMW, D), lambda j, i, k: (i, j))
    return matmul("mm_dact", dx3b, wdown, a_spec=((TMW, D), lambda j, i, k: (i, 0)), b_spec=((D, D), lambda j, i, k: (j, 0)),
                  cdims=NT, grid=(DFF // D, T // TMW, 1), acc_shape=(TMW, D), extras=[(rup, *blk)],
                  outs=[((T, DFF), BF16, *blk)], epi=epi, after=after)[0]


def mm_wgrad(name, a, b, m, n, out_shape, out_block, out_map, tm, tn, after=None):
    def epi(acc, ex, outs, i):
        outs[0][...] = acc.astype(BF16).reshape(outs[0].shape)

    return matmul(name, a, b, a_spec=((T, tm), lambda j, i, k: (0, i)), b_spec=((T, tn), lambda j, i, k: (0, j)),
                  cdims=TN, grid=(n // tn, m // tm, 1), acc_shape=(tm, tn),
                  outs=[(out_shape, BF16, out_block, out_map)], epi=epi, after=after)[0]


def mm_dh2(dup, wup, x2, dx3, g2, after=None):
    def epi(acc, ex, outs, i):
        x2_ref, dx3_ref, g_ref = ex
        x2 = x2_ref[...]
        r = lax.rsqrt(jnp.mean(x2 * x2, axis=-1, keepdims=True) + EPS)
        xn = x2 * r
        _row_acc(outs[2], jnp.sum(acc * xn, axis=0, keepdims=True), i)
        dx2 = dx3_ref[...] + _rms_bwd(xn, r, acc * g_ref[...])
        outs[0][...] = dx2
        outs[1][...] = dx2.astype(BF16)

    y = square_matmul("mm_dh2", dup, wup, a_spec=((TBIG, D), lambda j, i, k: (i, k)),
                      b_spec=((None, TBIG, D), lambda j, i, k: (k, j, 0)), cdims=NT, nk=NCHIP, after=after)
    return rowwise("rows_dh2", y, extras=[(x2, *_rowblk), (dx3, *_rowblk), (g2, *_vec)],
                   outs=[((T, D), F32, *_rowblk), ((T, D), BF16, *_rowblk), ((1, D), F32, *_vec)], epi=epi)


def mm_dmixed(dx2b, wout, pcat, ylin, ygla, pscale, after=None):
    assert OGG == OGP + D and OGP % (2 * D) == 0

    def epi(acc, ex, outs, i):
        lgp_ref, lgg_ref, ylin_ref, ygla_ref, ps_ref = ex
        dps = []
        for c0 in range(0, D, EPI_COLS):
            cs = slice(c0, c0 + EPI_COLS)
            gp = _sigmoid(lgp_ref[:, cs].astype(F32))
            gg = _sigmoid(lgg_ref[:, cs].astype(F32))
            yl = ylin_ref[:, cs].astype(F32)
            ps = ps_ref[:, cs]
            a = acc[:, cs]
            agp = a * gp
            outs[0][:, cs] = (agp * ps).astype(BF16)
            outs[1][:, cs] = (a * gg).astype(BF16)
            outs[2][:, cs] = (agp * (yl * ps) * (1.0 - gp)).astype(BF16)
            outs[2][:, D + c0:D + c0 + EPI_COLS] = (a * ygla_ref[:, cs].astype(F32) * gg * (1.0 - gg)).astype(BF16)
            dps.append(jnp.sum(agp * yl, axis=0, keepdims=True))
        _row_acc(outs[3], jnp.concatenate(dps, axis=1), i)

    return matmul("mm_dmixed", dx2b, wout, a_spec=_rowblk, b_spec=((D, D), lambda j, i, k: (0, 0)), cdims=NT,
                  grid=(1, T // TMF, 1), acc_shape=(TMF, D),
                  extras=[(pcat, *_full_spec(OGP // D)), (pcat, *_full_spec(OGG // D)), (ylin, *_rowblk), (ygla, *_rowblk),
                          (pscale, *_vec)],
                  outs=[((T, D), BF16, *_rowblk)] * 2
                       + [((T, NCAT), BF16, (TMF, 2 * D), lambda j, i, k: (i, OGP // (2 * D))), ((1, D), F32, *_vec)],
                  epi=epi, after=after)


def mm_dog(dygla, wgo, o, pcat, ng, dproj, after=None):
    def epi(acc, ex, outs, i):
        o_ref, g_ref, ng_ref = ex
        do_ref, dg_ref, gng_ref = outs
        gparts = []
        for h in range(HEADS):
            cv = slice(h * DV, (h + 1) * DV)
            oh = o_ref[:, cv].astype(F32)
            r = lax.rsqrt(jnp.mean(oh * oh, axis=-1, keepdims=True) + EPS)
            on = oh * r
            gv = g_ref[:, cv].astype(F32)
            sg = _sigmoid(gv)
            a = acc[:, cv]
            dgain = a * (gv * sg)
            gparts.append(jnp.sum(dgain * on, axis=0, keepdims=True))
            ngh = ng_ref[:, cv]
            do_ref[:, cv] = _rms_bwd(on, r, dgain * ngh).astype(BF16)
            dg_ref[:, cv] = (a * (on * ngh) * (sg * (1.0 + gv * (1.0 - sg)))).astype(BF16)
        _row_acc(gng_ref, jnp.concatenate(gparts, axis=1), i)

    return matmul("mm_dog", dygla, wgo, a_spec=_rowblk, b_spec=((D, D), lambda j, i, k: (0, 0)), cdims=NT,
                  grid=(1, T // TMF, 1), acc_shape=(TMF, D),
                  extras=[(o, *_rowblk), (pcat, *_full_spec(OG // D)), (ng, *_vec)],
                  outs=[((T, D), BF16, *_rowblk), ((T, NCAT), BF16, *_full_spec(OG // D)), ((1, D), F32, *_vec)],
                  epi=epi, after=after, into=(dproj, 1))


def mm_dh1(dpcat, wcat, x, dx2, g1, after=None):
    tk = 3840

    def epi(acc, ex, outs, i):
        x_ref, dx2_ref, g_ref = ex
        xv = x_ref[...]
        r = lax.rsqrt(jnp.mean(xv * xv, axis=-1, keepdims=True) + EPS)
        xn = xv * r
        _row_acc(outs[1], jnp.sum(acc * xn, axis=0, keepdims=True), i)
        outs[0][...] = dx2_ref[...] + _rms_bwd(xn, r, acc * g_ref[...])

    y = square_matmul("mm_dh1", dpcat, wcat, a_spec=((TBIG, tk), lambda j, i, k: (i, k)),
                      b_spec=((TBIG, tk), lambda j, i, k: (j, k)), cdims=NT, nk=NCAT // tk, after=after)
    return rowwise("rows_dh1", y, extras=[(x, *_rowblk), (dx2, *_rowblk), (g1, *_vec)],
                   outs=[((T, D), F32, *_rowblk), ((1, D), F32, *_vec)], epi=epi)


def _tile_rows(rows, cols, n_arrays):
    tm = rows
    while tm % 32 == 0 and 2 * n_arrays * tm * cols * 4 > 36 * 1024 * 1024:
        tm //= 2
    return tm


def add_pairs(name, parts, theirs, core):
    _, _, r, c = parts.shape
    tm = _tile_rows(r, c, 3)

    def body(core_ref, a_ref, b_ref, o_ref):
        o_ref[...] = (a_ref[...].astype(F32) + b_ref[...].astype(F32)).astype(BF16)

    spec = pl.BlockSpec((None, tm, c), lambda j, i, core_ref: (j, i, 0))
    grid_spec = pltpu.PrefetchScalarGridSpec(
        num_scalar_prefetch=1, grid=(NCHIP, r // tm),
        in_specs=[pl.BlockSpec((None, None, tm, c), lambda j, i, core_ref: (core_ref[0], j, i, 0)), spec], out_specs=spec)
    return pl.pallas_call(body, name=name, grid_spec=grid_spec, out_shape=PINNED((NCHIP, r, c), BF16),
                          compiler_params=_cparams(40 * 1024 * 1024, ("arbitrary", "arbitrary")))(core, *_in_hbm(parts, theirs))


def sum_chips(name, sums, landed, chip):
    _, r, c = sums.shape
    tm = _tile_rows(r, c, 4)

    def body(chip_ref, own_ref, l_ref, o_ref):
        s = own_ref[...].astype(F32)
        for t in range(NCHIP - 1):
            s = s + l_ref[t].astype(F32)
        o_ref[...] = s

    grid_spec = pltpu.PrefetchScalarGridSpec(
        num_scalar_prefetch=1, grid=(r // tm,),
        in_specs=[pl.BlockSpec((None, tm, c), lambda i, chip_ref: (chip_ref[0], i, 0)),
                  pl.BlockSpec((NCHIP - 1, tm, c), lambda i, chip_ref: (0, i, 0))],
        out_specs=pl.BlockSpec((tm, c), lambda i, chip_ref: (i, 0)))
    return pl.pallas_call(body, name=name, grid_spec=grid_spec, out_shape=PINNED((r, c), F32),
                          compiler_params=_cparams(40 * 1024 * 1024, ("arbitrary",)))(chip, *_in_hbm(sums, landed))


def _adamw_math(wv, gv, mv, vv):
    mn = ADAM_B1 * mv + (1.0 - ADAM_B1) * gv
    vn = ADAM_B2 * vv + (1.0 - ADAM_B2) * (gv * gv)
    mh = mn / (1.0 - ADAM_B1 ** ADAM_STEP)
    vh = vn / (1.0 - ADAM_B2 ** ADAM_STEP)
    return -ADAM_LR * (mh / (jnp.sqrt(vh) + ADAM_EPS) + ADAM_WD * wv), mn, vn


def adamw(name, w, g, m, v):
    def body(w_ref, g_ref, m_ref, v_ref, go_ref, d_ref, mo_ref, vo_ref):
        gv = g_ref[...]
        go_ref[...] = gv
        d_ref[...], mo_ref[...], vo_ref[...] = _adamw_math(w_ref[...], gv, m_ref[...], v_ref[...])

    return pl.pallas_call(body, name=name, out_shape=[SDS(w.shape, F32)] * 4)(w, g, m, v)


def adamw_halves(name, w, g_own, g_sib, m, v, core):
    _, r, c = w.shape
    tm = _tile_rows(r, c, 10)

    def body(core_ref, w_ref, go_ref, gs_ref, m_ref, v_ref, g_out, d_out, m_out, v_out):
        gv = jnp.where(pl.program_id(0) == core_ref[0], go_ref[...], gs_ref[...])
        g_out[...] = gv
        d_out[...], m_out[...], v_out[...] = _adamw_math(w_ref[...], gv, m_ref[...], v_ref[...])

    full = pl.BlockSpec((None, tm, c), lambda h, i, core_ref: (h, i, 0))
    own = pl.BlockSpec((tm, c), lambda h, i, core_ref: (jnp.where(h == core_ref[0], i, 0), 0))
    sib = pl.BlockSpec((tm, c), lambda h, i, core_ref: (jnp.where(h == core_ref[0], 0, i), 0))
    grid_spec = pltpu.PrefetchScalarGridSpec(num_scalar_prefetch=1, grid=(2, r // tm),
                                             in_specs=[full, own, sib, full, full], out_specs=[full] * 4)
    return pl.pallas_call(body, name=name, grid_spec=grid_spec, out_shape=[SDS(w.shape, F32)] * 4,
                          compiler_params=_cparams(48 * 1024 * 1024, ("arbitrary", "arbitrary")))(core, *_in_hbm(w, g_own, g_sib, m, v))


def adamw_halves_staged(name, w, g_own, g_sib, m, v, core):
    _, r, c = w.shape
    tm = _tile_rows(r, c, 10)

    def moments(core_ref, go_ref, gs_ref, m_ref, v_ref, g_out, m_out, v_out):
        gv = jnp.where(pl.program_id(0) == core_ref[0], go_ref[...], gs_ref[...])
        g_out[...] = gv
        m_out[...] = ADAM_B1 * m_ref[...] + (1.0 - ADAM_B1) * gv
        v_out[...] = ADAM_B2 * v_ref[...] + (1.0 - ADAM_B2) * (gv * gv)

    def step(w_ref, mn_ref, vn_ref, d_out):
        mh = mn_ref[...] / (1.0 - ADAM_B1 ** ADAM_STEP)
        vh = vn_ref[...] / (1.0 - ADAM_B2 ** ADAM_STEP)
        d_out[...] = -ADAM_LR * (mh / (jnp.sqrt(vh) + ADAM_EPS) + ADAM_WD * w_ref[...])

    full = pl.BlockSpec((None, tm, c), lambda h, i, core_ref: (h, i, 0))
    own = pl.BlockSpec((tm, c), lambda h, i, core_ref: (jnp.where(h == core_ref[0], i, 0), 0))
    sib = pl.BlockSpec((tm, c), lambda h, i, core_ref: (jnp.where(h == core_ref[0], 0, i), 0))
    grid_spec = pltpu.PrefetchScalarGridSpec(num_scalar_prefetch=1, grid=(2, r // tm),
                                             in_specs=[own, sib, full, full], out_specs=[full] * 3)
    g, mn, vn = pl.pallas_call(moments, name=name + "_moments", grid_spec=grid_spec, out_shape=[SDS(w.shape, F32)] * 3,
                               compiler_params=_cparams(48 * 1024 * 1024, ("arbitrary", "arbitrary")))(
                                   core, *_in_hbm(g_own, g_sib, m, v))
    plain = pl.BlockSpec((None, tm, c), lambda h, i: (h, i, 0))
    d = pl.pallas_call(step, name=name + "_step", grid=(2, r // tm), in_specs=[plain] * 3, out_specs=plain,
                       out_shape=SDS(w.shape, F32),
                       compiler_params=_cparams(48 * 1024 * 1024, ("arbitrary", "arbitrary")))(*_in_hbm(w, mn, vn))
    return g, d, mn, vn


def cast_bf16(name, w):
    _, r, c = w.shape
    tm = _tile_rows(r, c, 2)

    def body(w_ref, o_ref):
        o_ref[...] = w_ref[...].astype(BF16)

    spec = pl.BlockSpec((None, tm, c), lambda h, i: (h, i, 0))
    return pl.pallas_call(body, name=name, grid=(2, r // tm), in_specs=[spec], out_specs=spec, out_shape=PINNED(w.shape, BF16),
                          compiler_params=_cparams(40 * 1024 * 1024, ("arbitrary", "arbitrary")))(w)


def cast_to_slot(name, w, place):
    _, r, c = w.shape
    tm = _tile_rows(r, c, 2)

    def body(p_ref, w_ref, o_ref):
        o_ref[...] = w_ref[...].astype(BF16)

    grid_spec = pltpu.PrefetchScalarGridSpec(
        num_scalar_prefetch=1, grid=(2, r // tm), in_specs=[pl.BlockSpec((None, tm, c), lambda h, i, p: (h, i, 0))],
        out_specs=pl.BlockSpec((None, None, tm, c), lambda h, i, p: (p[1], h, i, 0)))
    return pl.pallas_call(body, name=name, grid_spec=grid_spec, out_shape=PINNED((NCHIP, 2, r, c), BF16),
                          compiler_params=_cparams(40 * 1024 * 1024, ("arbitrary", "arbitrary")))(place, *_in_hbm(w))


def pack_rows(name, parts, rows, after=None):
    width = parts[0].shape[1]
    n = len(parts)
    afters = _as_list(after)

    def body(*refs):
        out_ref = refs[n + len(afters)]
        out_ref[...] = jnp.zeros_like(out_ref)
        off = 0
        for p in refs[:n]:
            out_ref[off:off + p.shape[0], :] = p[...]
            off += p.shape[0]

    vm = pl.BlockSpec(memory_space=pltpu.VMEM)
    return pl.pallas_call(body, name=name, in_specs=[vm] * n + [ANY] * len(afters), out_specs=vm,
                          out_shape=SDS((rows, width), F32))(*parts, *afters)


def _place():
    x, y, c = lax.axis_index("x"), lax.axis_index("y"), lax.axis_index("c")
    chips = [(1 - x, y), (x, 1 - y), (1 - x, 1 - y)]
    return x, y, c, chips


def _row_split(shape, dtype):
    r, c = shape
    n = 1
    while r % (2 * n) == 0 and (r // (2 * n)) % 16 == 0 and (r // n) * c * jnp.dtype(dtype).itemsize > PIECE_BYTES:
        n *= 2
    return [pl.ds(s * (r // n), r // n) for s in range(n)]


def _pieces(ref):
    *lead, r, c = ref.shape
    split = _row_split((r, c), ref.dtype)
    return [ref.at[(*idx, s)] for idx in itertools.product(*[range(d) for d in lead]) for s in split]


HBM = pl.BlockSpec(memory_space=pltpu.HBM)
SEM = pl.BlockSpec(memory_space=pltpu.SEMAPHORE)
EFFECT = pltpu.SideEffectType.DATAFLOW_SIDE_EFFECTING


def _own_half(shard_refs, land, a, me, c):
    return land[a].at[me, c] if shard_refs[a] is None else shard_refs[a].at[c]


def _spread(refs, shards):
    it = iter(refs)
    return [None if s is None else next(it) for s in shards]


def gather_start(name, items, after=None):
    n = len(items)
    shards = [s if s.ndim == 3 else None for s in items]
    given = [s for s in shards if s is not None]
    ns = len(given)
    afters = _as_list(after)

    def body(*refs):
        src, land = _spread(refs[:ns], shards), refs[ns:ns + n]
        send, recv = refs[ns + n + len(afters)], refs[ns + n + len(afters) + 1]
        x, y, c, chips = _place()
        me = 2 * x + y
        for a in range(n):
            for j, (cx, cy) in enumerate(chips[:2]):
                for sp, dp in zip(_pieces(_own_half(src, land, a, me, c)), _pieces(land[a].at[me, c])):
                    pltpu.make_async_remote_copy(sp, dp, send.at[2 * a + j], recv.at[2 * a + j],
                                                 device_id=(cx, cy, c), device_id_type=MESH).start()

    lands = [pltpu.with_memory_space_constraint(lax.empty((NCHIP,) + s.shape, s.dtype) if s.ndim == 3 else s, pltpu.HBM)
             for s in items]
    srcs = [pltpu.with_memory_space_constraint(s, pltpu.HBM) for s in given]
    outs = pl.pallas_call(
        body, name=name,
        out_shape=(pltpu.SemaphoreType.DMA((2 * n,)), pltpu.SemaphoreType.DMA((2 * n,)),
                   *[pltpu.HBM(s.shape, s.dtype) for s in given], *[pltpu.HBM(l.shape, l.dtype) for l in lands]),
        in_specs=[HBM] * (ns + n) + [ANY] * len(afters), out_specs=(SEM, SEM, *([HBM] * (ns + n))),
        input_output_aliases={i: 2 + i for i in range(ns + n)},
        compiler_params=pltpu.CompilerParams(has_side_effects=EFFECT),
    )(*srcs, *lands, *afters)
    return outs[0], outs[1], _spread(outs[2:2 + ns], shards), list(outs[2 + ns:2 + ns + n])


def _relay_blocks(land, c, chips):
    (xx, xy), (yx, yy), (dx, dy) = chips
    rows = land.shape[2] // 2
    upper, lower = pl.ds(0, rows), pl.ds(rows, rows)
    return [(land.at[2 * yx + yy, c, lower], land.at[2 * dx + dy, c, lower]),
            (land.at[2 * xx + xy, c, upper], land.at[2 * dx + dy, c, upper])]


def relay_turn(name, send, recv, shards, lands, after):
    n = len(lands)
    given = [s for s in shards if s is not None]
    ns = len(given)
    afters = _as_list(after)

    def body(*refs):
        src, had = _spread(refs[:ns], shards), refs[ns:ns + n]
        send_ref, recv_ref = refs[ns + n], refs[ns + n + 1]
        rsend, rrecv = refs[ns + n + 2 + len(afters)], refs[ns + n + 3 + len(afters)]
        land = refs[2 * ns + n + 4 + len(afters):2 * ns + 2 * n + 4 + len(afters)]
        x, y, c, chips = _place()
        me = 2 * x + y
        for a in range(n):
            for j, (cx, cy) in enumerate(chips[:2]):
                cp = pltpu.make_async_remote_copy(_own_half(src, had, a, me, c), had[a].at[2 * cx + cy, c],
                                                  send_ref.at[2 * a + j], recv_ref.at[2 * a + j],
                                                  device_id=(cx, cy, c), device_id_type=MESH)
                cp.wait_send()
                cp.wait_recv()
        for a in range(n):
            for j, ((sent, _), (dst, _)) in enumerate(zip(_relay_blocks(had[a], c, chips), _relay_blocks(land[a], c, chips))):
                cx, cy = chips[j]
                for sp, dp in zip(_pieces(sent), _pieces(dst)):
                    pltpu.make_async_remote_copy(sp, dp, rsend.at[2 * a + j], rrecv.at[2 * a + j],
                                                 device_id=(cx, cy, c), device_id_type=MESH).start()

    outs = pl.pallas_call(
        body, name=name,
        out_shape=(pltpu.SemaphoreType.DMA((2 * n,)), pltpu.SemaphoreType.DMA((2 * n,)),
                   *[pltpu.HBM(s.shape, s.dtype) for s in given], *[pltpu.HBM(l.shape, l.dtype) for l in lands]),
        in_specs=[HBM] * (ns + n) + [SEM, SEM] + [ANY] * len(afters), out_specs=(SEM, SEM, *([HBM] * (ns + n))),
        input_output_aliases={i: 2 + i for i in range(ns + n)},
        compiler_params=pltpu.CompilerParams(has_side_effects=EFFECT),
    )(*given, *lands, send, recv, *afters)
    return outs[0], outs[1], _spread(outs[2:2 + ns], shards), list(outs[2 + ns:2 + ns + n])


def relay_wait(name, send, recv, lands, after):
    n = len(lands)
    afters = _as_list(after)

    def body(*refs):
        land = refs[:n]
        send_ref, recv_ref = refs[n], refs[n + 1]
        x, y, c, chips = _place()
        for a in range(n):
            for j, (sent, got) in enumerate(_relay_blocks(land[a], c, chips)):
                cx, cy = chips[j]
                cp = pltpu.make_async_remote_copy(sent, got, send_ref.at[2 * a + j], recv_ref.at[2 * a + j],
                                                  device_id=(cx, cy, c), device_id_type=MESH)
                cp.wait_send()
                cp.wait_recv()

    outs = pl.pallas_call(
        body, name=name, out_shape=tuple(pltpu.HBM(l.shape, l.dtype) for l in lands),
        in_specs=[HBM] * n + [SEM, SEM] + [ANY] * len(afters), out_specs=[HBM] * n,
        input_output_aliases={i: i for i in range(n)},
        compiler_params=pltpu.CompilerParams(has_side_effects=EFFECT),
    )(*lands, send, recv, *afters)
    return list(outs)


def forward_halves(name, lands):
    n = len(lands)

    def body(*refs):
        had, buf = refs[:n], refs[n:2 * n]
        send, recv = refs[2 * n:]
        x, y, c, chips = _place()
        sib = (x, y, 1 - c)
        for a in range(n):
            for j, (cx, cy) in enumerate(chips):
                for sp, dp in zip(_pieces(had[a].at[2 * cx + cy, c]), _pieces(buf[a].at[2 * cx + cy, c])):
                    pltpu.make_async_remote_copy(sp, dp, send.at[3 * a + j], recv.at[3 * a + j], device_id=sib, device_id_type=MESH).start()
        for a in range(n):
            for j, (cx, cy) in enumerate(chips):
                pltpu.make_async_remote_copy(had[a].at[2 * cx + cy, c], buf[a].at[2 * cx + cy, 1 - c], send.at[3 * a + j],
                                             recv.at[3 * a + j], device_id=sib, device_id_type=MESH).wait()

    return pl.pallas_call(
        body, name=name, in_specs=[ANY] * n, out_specs=[ANY] * n, out_shape=[SDS(l.shape, l.dtype) for l in lands],
        input_output_aliases={i: i for i in range(n)},
        scratch_shapes=[pltpu.SemaphoreType.DMA((3 * n,)), pltpu.SemaphoreType.DMA((3 * n,))],
    )(*lands)


def forward_turn(name, send, recv, lands, after):
    n = len(lands)
    afters = _as_list(after)

    def body(*refs):
        had = refs[:n]
        send_ref, recv_ref = refs[n], refs[n + 1]
        fsend, frecv = refs[n + 2 + len(afters)], refs[n + 3 + len(afters)]
        buf = refs[n + 4 + len(afters):2 * n + 4 + len(afters)]
        x, y, c, chips = _place()
        sib = (x, y, 1 - c)
        for a in range(n):
            for j, (sent, got) in enumerate(_relay_blocks(had[a], c, chips)):
                cx, cy = chips[j]
                cp = pltpu.make_async_remote_copy(sent, got, send_ref.at[2 * a + j], recv_ref.at[2 * a + j],
                                                  device_id=(cx, cy, c), device_id_type=MESH)
                cp.wait_send()
                cp.wait_recv()
        for a in range(n):
            for j, (cx, cy) in enumerate(chips):
                for sp, dp in zip(_pieces(had[a].at[2 * cx + cy, c]), _pieces(buf[a].at[2 * cx + cy, c])):
                    pltpu.make_async_remote_copy(sp, dp, fsend.at[3 * a + j], frecv.at[3 * a + j], device_id=sib, device_id_type=MESH).start()

    outs = pl.pallas_call(
        body, name=name,
        out_shape=(pltpu.SemaphoreType.DMA((3 * n,)), pltpu.SemaphoreType.DMA((3 * n,)), *[pltpu.HBM(l.shape, l.dtype) for l in lands]),
        in_specs=[HBM] * n + [SEM, SEM] + [ANY] * len(afters), out_specs=(SEM, SEM, *([HBM] * n)),
        input_output_aliases={i: 2 + i for i in range(n)},
        compiler_params=pltpu.CompilerParams(has_side_effects=EFFECT),
    )(*lands, send, recv, *afters)
    return outs[0], outs[1], list(outs[2:])


def forward_wait(name, send, recv, lands, after):
    n = len(lands)
    afters = _as_list(after)

    def body(*refs):
        land = refs[:n]
        send_ref, recv_ref = refs[n], refs[n + 1]
        x, y, c, chips = _place()
        sib = (x, y, 1 - c)
        for a in range(n):
            for j, (cx, cy) in enumerate(chips):
                cp = pltpu.make_async_remote_copy(land[a].at[2 * cx + cy, c], land[a].at[2 * cx + cy, 1 - c], send_ref.at[3 * a + j],
                                                  recv_ref.at[3 * a + j], device_id=sib, device_id_type=MESH)
                cp.wait_send()
                cp.wait_recv()

    outs = pl.pallas_call(
        body, name=name, out_shape=tuple(pltpu.HBM(l.shape, l.dtype) for l in lands),
        in_specs=[HBM] * n + [SEM, SEM] + [ANY] * len(afters), out_specs=[HBM] * n,
        input_output_aliases={i: i for i in range(n)},
        compiler_params=pltpu.CompilerParams(has_side_effects=EFFECT),
    )(*lands, send, recv, *afters)
    return list(outs)


def exchange_start(name, parts):
    n = len(parts)

    def body(*refs):
        src, got = refs[:n], refs[n:2 * n]
        send, recv = refs[2 * n], refs[2 * n + 1]
        token = refs[4 * n + 2]
        x, y, c, _ = _place()
        sib = (x, y, 1 - c)
        for a in range(n):
            for sp, dp in zip(_pieces(src[a].at[1 - c]), _pieces(got[a])):
                pltpu.make_async_remote_copy(sp, dp, send.at[a], recv.at[a], device_id=sib, device_id_type=MESH).start()
        token[...] = jnp.zeros_like(token)

    lands = [pltpu.with_memory_space_constraint(lax.empty(p.shape[1:], p.dtype), pltpu.HBM) for p in parts]
    srcs = [pltpu.with_memory_space_constraint(p, pltpu.HBM) for p in parts]
    outs = pl.pallas_call(
        body, name=name,
        out_shape=(pltpu.SemaphoreType.DMA((n,)), pltpu.SemaphoreType.DMA((n,)),
                   *[pltpu.HBM(p.shape, p.dtype) for p in parts], *[pltpu.HBM(l.shape, l.dtype) for l in lands],
                   SDS((8, 128), F32)),
        in_specs=[HBM] * (2 * n), out_specs=(SEM, SEM, *([HBM] * (2 * n)), pl.BlockSpec(memory_space=pltpu.VMEM)),
        input_output_aliases={i: 2 + i for i in range(2 * n)},
        compiler_params=pltpu.CompilerParams(has_side_effects=EFFECT),
    )(*srcs, *lands)
    return outs[0], outs[1], list(outs[2:2 + n]), list(outs[2 + n:2 + 2 * n]), outs[2 + 2 * n]


def exchange_wait(name, send, recv, parts, lands, after):
    n = len(parts)
    afters = _as_list(after)

    def body(*refs):
        src, got = refs[:n], refs[n:2 * n]
        send_ref, recv_ref = refs[2 * n], refs[2 * n + 1]
        x, y, c, _ = _place()
        sib = (x, y, 1 - c)
        for a in range(n):
            cp = pltpu.make_async_remote_copy(src[a].at[1 - c], got[a], send_ref.at[a], recv_ref.at[a], device_id=sib, device_id_type=MESH)
            cp.wait_send()
            cp.wait_recv()

    outs = pl.pallas_call(
        body, name=name,
        out_shape=(*[pltpu.HBM(p.shape, p.dtype) for p in parts], *[pltpu.HBM(l.shape, l.dtype) for l in lands]),
        in_specs=[HBM] * (2 * n) + [SEM, SEM] + [ANY] * len(afters), out_specs=[HBM] * (2 * n),
        input_output_aliases={i: i for i in range(2 * n)},
        compiler_params=pltpu.CompilerParams(has_side_effects=EFFECT),
    )(*parts, *lands, send, recv, *afters)
    return list(outs[:n]), list(outs[n:])


def scatter_start(name, parts):
    n = len(parts)

    def body(*refs):
        src, land = refs[:n], refs[n:2 * n]
        send, recv = refs[2 * n], refs[2 * n + 1]
        token = refs[4 * n + 2]
        x, y, c, chips = _place()
        for a in range(n):
            for j, (cx, cy) in enumerate(chips):
                for sp, dp in zip(_pieces(src[a].at[2 * cx + cy]), _pieces(land[a].at[j])):
                    pltpu.make_async_remote_copy(sp, dp, send.at[3 * a + j], recv.at[3 * a + j],
                                                 device_id=(cx, cy, c), device_id_type=MESH).start()
        token[...] = jnp.zeros_like(token)

    lands = [pltpu.with_memory_space_constraint(lax.empty((NCHIP - 1,) + p.shape[1:], p.dtype), pltpu.HBM) for p in parts]
    srcs = [pltpu.with_memory_space_constraint(p, pltpu.HBM) for p in parts]
    outs = pl.pallas_call(
        body, name=name,
        out_shape=(pltpu.SemaphoreType.DMA((3 * n,)), pltpu.SemaphoreType.DMA((3 * n,)),
                   *[pltpu.HBM(p.shape, p.dtype) for p in parts], *[pltpu.HBM(l.shape, l.dtype) for l in lands],
                   SDS((8, 128), F32)),
        in_specs=[HBM] * (2 * n), out_specs=(SEM, SEM, *([HBM] * (2 * n)), pl.BlockSpec(memory_space=pltpu.VMEM)),
        input_output_aliases={i: 2 + i for i in range(2 * n)},
        compiler_params=pltpu.CompilerParams(has_side_effects=EFFECT),
    )(*srcs, *lands)
    return outs[0], outs[1], list(outs[2:2 + n]), list(outs[2 + n:2 + 2 * n]), outs[2 + 2 * n]


def scatter_wait(name, send, recv, parts, lands, after):
    n = len(parts)
    afters = _as_list(after)

    def body(*refs):
        src, land = refs[:n], refs[n:2 * n]
        send_ref, recv_ref = refs[2 * n], refs[2 * n + 1]
        x, y, c, chips = _place()
        for a in range(n):
            for j, (cx, cy) in enumerate(chips):
                cp = pltpu.make_async_remote_copy(src[a].at[2 * cx + cy], land[a].at[j], send_ref.at[3 * a + j], recv_ref.at[3 * a + j],
                                                  device_id=(cx, cy, c), device_id_type=MESH)
                cp.wait_send()
                cp.wait_recv()

    outs = pl.pallas_call(
        body, name=name,
        out_shape=(*[pltpu.HBM(p.shape, p.dtype) for p in parts], *[pltpu.HBM(l.shape, l.dtype) for l in lands]),
        in_specs=[HBM] * (2 * n) + [SEM, SEM] + [ANY] * len(afters), out_specs=[HBM] * (2 * n),
        input_output_aliases={i: i for i in range(2 * n)},
        compiler_params=pltpu.CompilerParams(has_side_effects=EFFECT),
    )(*parts, *lands, send, recv, *afters)
    return list(outs[:n]), list(outs[n:])


def join_start(name, halves):
    n = len(halves)

    def body(*refs):
        src, dst = refs[:n], refs[n:2 * n]
        send, recv = refs[2 * n], refs[2 * n + 1]
        token = refs[4 * n + 2]
        x, y, c, _ = _place()
        sib = (x, y, 1 - c)
        for a in range(n):
            for sp, dp in zip(_pieces(src[a]), _pieces(dst[a])):
                pltpu.make_async_remote_copy(sp, dp, send.at[a], recv.at[a], device_id=sib, device_id_type=MESH).start()
        token[...] = jnp.zeros_like(token)

    lands = [pltpu.with_memory_space_constraint(lax.empty(h.shape, h.dtype), pltpu.HBM) for h in halves]
    srcs = [pltpu.with_memory_space_constraint(h, pltpu.HBM) for h in halves]
    outs = pl.pallas_call(
        body, name=name,
        out_shape=(pltpu.SemaphoreType.DMA((n,)), pltpu.SemaphoreType.DMA((n,)),
                   *[pltpu.HBM(h.shape, h.dtype) for h in halves], *[pltpu.HBM(l.shape, l.dtype) for l in lands],
                   SDS((8, 128), F32)),
        in_specs=[HBM] * (2 * n), out_specs=(SEM, SEM, *([HBM] * (2 * n)), pl.BlockSpec(memory_space=pltpu.VMEM)),
        input_output_aliases={i: 2 + i for i in range(2 * n)},
        compiler_params=pltpu.CompilerParams(has_side_effects=EFFECT),
    )(*srcs, *lands)
    return outs[0], outs[1], list(outs[2:2 + n]), list(outs[2 + n:2 + 2 * n]), outs[2 + 2 * n]


def join_wait(name, send, recv, halves, lands, after):
    n = len(halves)
    afters = _as_list(after)

    def body(*refs):
        src, dst = refs[:n], refs[n:2 * n]
        send_ref, recv_ref = refs[2 * n], refs[2 * n + 1]
        x, y, c, _ = _place()
        sib = (x, y, 1 - c)
        for a in range(n):
            cp = pltpu.make_async_remote_copy(src[a], dst[a], send_ref.at[a], recv_ref.at[a], device_id=sib, device_id_type=MESH)
            cp.wait_send()
            cp.wait_recv()

    outs = pl.pallas_call(
        body, name=name,
        out_shape=(*[pltpu.HBM(h.shape, h.dtype) for h in halves], *[pltpu.HBM(l.shape, l.dtype) for l in lands]),
        in_specs=[HBM] * (2 * n) + [SEM, SEM] + [ANY] * len(afters), out_specs=[HBM] * (2 * n),
        input_output_aliases={i: i for i in range(2 * n)},
        compiler_params=pltpu.CompilerParams(has_side_effects=EFFECT),
    )(*halves, *lands, send, recv, *afters)
    return list(outs[:n]), list(outs[n:])


def gather_small(name, xs, reduce, after=None):
    m, ncol = xs.shape
    afters = _as_list(after)

    def body(x_ref, *rest):
        out_ref, all_ref, send, recv, lsem = rest[len(afters):]
        x, y, c, chips = _place()
        me, sib = (x, y, c), (x, y, 1 - c)

        def rows(px, py, pc):
            return all_ref.at[pl.ds((4 * px + 2 * py + pc) * m, m), :]

        def copy(k, block, to, src=None):
            return pltpu.make_async_remote_copy(rows(*block) if src is None else src, rows(*block), send.at[k], recv.at[k],
                                                device_id=to, device_id_type=MESH)

        mine = pltpu.make_async_copy(x_ref, rows(*me), lsem)
        mine.start()
        first = [copy(0, me, sib, src=x_ref)] + [copy(1 + j, me, (*chip, c), src=x_ref) for j, chip in enumerate(chips)]
        for cp in first:
            cp.start()
        passed = [copy(4 + j, (*chip, c), sib) for j, chip in enumerate(chips)]
        for j, chip in enumerate(chips):
            copy(1 + j, (*chip, c), me).wait_recv()
            passed[j].start()
        copy(0, sib, me).wait_recv()
        for j, chip in enumerate(chips):
            copy(4 + j, (*chip, 1 - c), me).wait_recv()
        for cp in first + passed:
            cp.wait_send()
        mine.wait()
        if reduce:
            s = all_ref[0:m, :]
            for dev in range(1, 8):
                s = s + all_ref[dev * m:(dev + 1) * m, :]
            out_ref[...] = s
        else:
            out_ref[...] = all_ref[...]

    vm = pl.BlockSpec(memory_space=pltpu.VMEM)
    return pl.pallas_call(
        body, name=name, in_specs=[vm] + [ANY] * len(afters), out_specs=vm,
        out_shape=SDS((m, ncol) if reduce else (8 * m, ncol), F32),
        scratch_shapes=[pltpu.VMEM((8 * m, ncol), F32), pltpu.SemaphoreType.DMA((7,)), pltpu.SemaphoreType.DMA((7,)),
                        pltpu.SemaphoreType.DMA],
    )(xs, *afters)


RELAYOUT_ROWS = 128


def weights_to_cat(name, land, own, place, other, prev=None, after=None):
    tm = RELAYOUT_ROWS
    nb = (D // 2) // tm
    extra = ([] if prev is None else [prev]) + _as_list(after)

    def half(p):
        return 1 - p[0] if other else p[0]

    def body(p_ref, g_ref, own_ref, *rest):
        o_ref = rest[len(extra)]
        nat = jnp.concatenate([jnp.where(p_ref[1] == j, own_ref[...], g_ref[j]) for j in range(NCHIP)], axis=1)
        pad = jnp.zeros((tm, NCAT - OA - 16), BF16)
        o_ref[...] = jnp.concatenate([nat[:, 3072:7168], nat[:, 7184:11280], nat[:, 0:3072], nat[:, 7168:7184], pad], axis=1)

    grid_spec = pltpu.PrefetchScalarGridSpec(
        num_scalar_prefetch=1, grid=(nb,),
        in_specs=[pl.BlockSpec((NCHIP, None, tm, IN_SHARD), lambda i, p: (0, half(p), i, 0)),
                  pl.BlockSpec((None, tm, IN_SHARD), lambda i, p: (half(p), i, 0))] + [ANY] * len(extra),
        out_specs=pl.BlockSpec((tm, NCAT), lambda i, p: (half(p) * nb + i, 0)))
    return pl.pallas_call(
        body, name=name, grid_spec=grid_spec, out_shape=PINNED((D, NCAT), BF16),
        input_output_aliases={} if prev is None else {3: 0},
        compiler_params=_cparams(40 * 1024 * 1024, ("arbitrary",)),
    )(place, land, own, *extra)


def grads_from_cat(gw_cat):
    tm = RELAYOUT_ROWS
    nb = (D // 2) // tm

    def body(c_ref, o_ref):
        cat = c_ref[...]
        nat = jnp.concatenate([cat[:, OU:OA], cat[:, OV:OGP], cat[:, OA:OA + 16], cat[:, OGP:OU]], axis=1)
        for j in range(NCHIP):
            o_ref[j] = nat[:, j * IN_SHARD:(j + 1) * IN_SHARD]

    return pl.pallas_call(
        body, name="grads_from_cat", grid=(D // tm,), in_specs=[pl.BlockSpec((tm, NCAT), lambda i: (i, 0))],
        out_specs=pl.BlockSpec((None, NCHIP, tm, IN_SHARD), lambda i: (i // nb, 0, i % nb, 0)),
        out_shape=PINNED((2, NCHIP, D // 2, IN_SHARD), BF16), compiler_params=_cparams(40 * 1024 * 1024, ("arbitrary",)),
    )(gw_cat)


def _pad_rows(a, rows):
    return jnp.concatenate([a, jnp.zeros((rows - a.shape[0],) + a.shape[1:], a.dtype)], axis=0)


def local_step(x2d, tgt, gf, g1, pool_scale, wa_pad, b_alpha, ng, g2, get_w, on_grad=None, on_settle=None, tick=None):
    emit = on_grad if on_grad is not None else (lambda group, grads: None)
    settle = on_settle if on_settle is not None else (lambda group, after: None)
    h1 = norm1(x2d, g1)
    wcat, pw = get_w("in", h1)
    pcat = mm_in(h1, wcat)
    pinned = tick("pool", pcat) if tick is not None else None
    dpool, ylin = pool_fwd(pcat, pw, pinned)
    og, o, states = gla_fwd(pcat, wa_pad, b_alpha, ng, ylin)
    w_go, w_o = get_w("mid", og)
    mixed, ygla = mm_gla_out(og, w_go, ylin, pcat, pool_scale)
    x2, h2 = mm_out(mixed, w_o, x2d, g2)
    w_up = get_w("up", h2)
    rup, act = mm_up(h2, w_up)
    w_dn = get_w("down", act)
    dx3, dx3b, g_nf, loss_row = mm_down(act, w_dn, x2, tgt, gf)

    gw_down = mm_wgrad("mm_dw_down", act, dx3b, DFF, D, (2, NCHIP, D // 2, D), (None, None, D // 2, D),
                       lambda j, i, k: (i % 2, i // 2, 0, 0), D // 2, D)
    token = emit("down", {"down": gw_down})
    dup = mm_dact(dx3b, w_dn, rup, after=token)
    token = settle("down", dup)
    dx2, dx2b, g_mlp = mm_dh2(dup, w_up, x2, dx3, g2, after=token)
    gw_up = mm_wgrad("mm_dw_up", h2, dup, D, DFF, (2, NCHIP, D // 2, D), (None, None, D // 2, D),
                     lambda j, i, k: (i, j, 0, 0), D // 2, D)
    token = emit("up", {"up": gw_up})
    dylin, dygla, dpcat, g_ps = mm_dmixed(dx2b, w_o, pcat, ylin, ygla, pool_scale, after=token)
    token = settle("up", dylin)
    gw_out = mm_wgrad("mm_dw_out", mixed, dx2b, D, D, (2, NCHIP, 256, D), (2, None, 256, D),
                      lambda j, i, k: (0, i, 0, 0), 512, D)
    do, dpcat, g_ng = mm_dog(dygla, w_go, o, pcat, ng, dpcat, after=token)
    gw_go = mm_wgrad("mm_dw_gla_out", og, dygla, D, D, (2, NCHIP, 256, D), (2, None, 256, D),
                     lambda j, i, k: (0, i, 0, 0), 512, D)
    token = emit("mix", {"out": gw_out, "gla_out": gw_go})
    dpcat, dv, g_wa, g_ba = gla_bwd(do, pcat, states, wa_pad, b_alpha, dpcat, b_alpha if token is None else token)
    token = settle("mix", dv)
    dpcat, dpw = pool_bwd(dylin, dpool, pw, lax.dynamic_update_slice(dpcat, dv, (0, OV)))
    gw_cat = mm_wgrad("mm_dw_in", h1, dpcat, D, NCAT, (D, NCAT), (1024, 1280), lambda j, i, k: (i, j), 1024, 1280, after=token)
    token = settle("in", emit("in", {"in_cat": gw_cat, "pool": dpw}))
    grad_x, g_mix = mm_dh1(dpcat, wcat, x2d, dx2, g1, after=token)
    return (loss_row[0, 0], grad_x, g_mix, g_ps, g_mlp, g_nf, g_ng, g_ba, g_wa, token,
            gw_cat, dpw, gw_go, gw_out, gw_up, gw_down)


def kernel(x, norm_mix_g, w_in, pool_w, pool_scale, w_alpha, b_alpha, gla_norm_g, w_gla_out, w_out, norm_mlp_g, w_mlp_up, w_mlp_down, norm_final_g, loss_target, m_norm_mix_g, m_w_in, m_pool_w, m_pool_scale, m_w_alpha, m_b_alpha, m_gla_norm_g, m_w_gla_out, m_w_out, m_norm_mlp_g, m_w_mlp_up, m_w_mlp_down, m_norm_final_g, v_norm_mix_g, v_w_in, v_pool_w, v_pool_scale, v_w_alpha, v_b_alpha, v_gla_norm_g, v_w_gla_out, v_w_out, v_norm_mlp_g, v_w_mlp_up, v_w_mlp_down, v_norm_final_g):
    chip = 2 * lax.axis_index("x") + lax.axis_index("y")
    chip_i = chip.astype(jnp.int32).reshape(1)
    core_i = lax.axis_index("c").astype(jnp.int32).reshape(1)
    place_i = jnp.concatenate([core_i, chip_i])
    tgt = loss_target.reshape(T, D)
    gf = norm_final_g.reshape(1, D)

    def halves(w2d):
        r, c = w2d.shape
        return lax.dynamic_update_index_in_dim(lax.empty((NCHIP, 2, r // 2, c), BF16), w2d.astype(BF16).reshape(2, r // 2, c),
                                               chip, 0)

    pool_shard = pool_w.reshape(4 * PG, PO // NCHIP)
    w_in_r = w_in.reshape(2, D // 2, IN_SHARD)
    sent = {"in": [cast_bf16("cast_w_in", w_in_r), halves(pool_shard)]}
    flight = {}

    def start(group, after=None):
        flight[group] = gather_start("gather_start_" + group, sent[group], after)

    def relay(group, after):
        send, recv, shards, lands = flight[group]
        flight[group] = relay_turn("relay_turn_" + group, send, recv, shards, lands, after)

    def fetch(group, after):
        send, recv, shards, lands = flight[group]
        lands = relay_wait("relay_wait_" + group, send, recv, lands, after)
        return forward_halves("forward_" + group, lands)

    small_w = pack_rows("pack_small_w", [w_alpha[0].reshape(4, QK),
                                         jnp.concatenate([gla_norm_g[0].reshape(1, 512), jnp.zeros((1, 512), F32)], axis=1)], 8)
    sw_all = gather_small("gather_small_w", small_w, False).reshape(8, 8, QK)
    start("in", sw_all)
    m_in_f, v_in_f, w_go_f, w_o_f, w_up_f, w_dn_f, x_f = lax.optimization_barrier(
        (m_w_in, v_w_in, w_gla_out, w_out, w_mlp_up, w_mlp_down, x, flight["in"][2][0]))[:7]
    m_in_r, v_in_r = m_in_f.reshape(2, D // 2, IN_SHARD), v_in_f.reshape(2, D // 2, IN_SHARD)
    sent["mid"] = [halves(w_go_f[0]), halves(w_o_f[0])]
    relay("in", [m_in_r, v_in_r, *sent["mid"]])
    w_up_f, w_dn_f, x_f = lax.optimization_barrier((w_up_f, w_dn_f, x_f, flight["in"][3][0]))[:3]
    sent["up"] = [cast_to_slot("cast_w_up", w_up_f[0].reshape(2, D // 2, D), place_i)]
    sent["down"] = [cast_to_slot("cast_w_down", w_dn_f[0].reshape(2, DFF // NCHIP // 2, D), place_i)]
    x2d = x_f.reshape(T, D)
    big = [w_in_r, w_go_f[0], w_o_f[0], w_up_f[0], w_dn_f[0], pool_shard]

    def tick(point, after):
        if point == "pool":
            relay("mid", after)
            relay("up", flight["mid"][3][0])
            start("down", flight["up"][3][0])
            return [flight["up"][3][0], flight["down"][3][0]]

    def get_w(group, after):
        if group == "in":
            after = [after, *sent["up"], *sent["down"], wa_pad]
        if group == "up":
            relay("down", after)
            send, recv, lands, shards = flight["up"]
            lands = forward_wait("forward_wait_up", send, recv, lands, flight["down"][3][0])
            return lands[0].reshape(NCHIP, D, D)
        if group == "in":
            send, recv, shards, lands = flight["in"]
            send, recv, lands = forward_turn("forward_turn_in", send, recv, lands, after)
            start("mid", lands[0])
            start("up", flight["mid"][3][0])
            wcat = weights_to_cat("weights_to_cat_mine", lands[0], shards[0], place_i, False, after=flight["up"][3][0])
            lands = forward_wait("forward_wait_in", send, recv, lands, wcat)
            wcat = weights_to_cat("weights_to_cat_sibling", lands[0], shards[0], place_i, True, prev=wcat)
            g_pool = lands[1]
            pw = jnp.concatenate([g_pool[j].reshape(4, PG, PO // NCHIP) for j in range(NCHIP)], axis=2)
            return wcat, pw
        whole = fetch(group, after)
        if group == "mid":
            send, recv, shards, lands = flight["up"]
            flight["up"] = (*forward_turn("forward_turn_up", send, recv, lands, whole[0]), shards)
            w_go, w_o, _ = lax.optimization_barrier((whole[0], whole[1], flight["up"][2][0]))
            return w_go.reshape(D, D), w_o.reshape(D, D)
        return whole[0].reshape(DFF, D)

    wa_full = jnp.concatenate([sw_all[2 * j, 0:4].reshape(16, DK) for j in range(NCHIP)], axis=1)
    ng_full = jnp.concatenate([sw_all[2 * j, 4, 0:512].reshape(HEADS, DV // NCHIP) for j in range(NCHIP)], axis=1)
    wa_pad = _pad_rows(wa_full, APAD).astype(BF16)
    ng = ng_full.reshape(1, D)

    pending = {}
    wmv = {"in": (w_in_r, m_in_r, v_in_r), "gla_out": (big[1], m_w_gla_out, v_w_gla_out), "out": (big[2], m_w_out, v_w_out),
           "up": (big[3], m_w_mlp_up, v_w_mlp_up), "down": (big[4], m_w_mlp_down, v_w_mlp_down), "pool": (big[5], m_pool_w, v_pool_w)}
    big_res = {}

    def reduce_group(group, after):
        nms, send, recv, sums, lands = pending[group]
        sums, lands = scatter_wait("scatter_wait_" + group, send, recv, sums, lands, after)
        reduced = [sum_chips("sum_chips_" + nm, a, b, chip_i) for nm, a, b in zip(nms, sums, lands)]
        send, recv, reduced, lands, token = join_start("join_start_" + group, reduced)
        pending[group] = (nms, send, recv, reduced, lands)
        return token

    def update_group(group, after):
        nms, send, recv, reduced, lands = pending[group]
        reduced, from_sib = join_wait("join_wait_" + group, send, recv, reduced, lands, after)
        for nm, g_own, g_sib in zip(nms, reduced, from_sib):
            w, m, v = wmv[nm]
            shp = (2,) + g_own.shape
            update = adamw_halves_staged if nm == "in" else adamw_halves
            big_res[nm] = update("adamw_" + nm, w.reshape(shp), g_own, g_sib, m.reshape(shp), v.reshape(shp), core_i)

    def on_grad(group, grads):
        if group == "in":
            gw_in = grads_from_cat(grads["in_cat"])
            gw_pool = jnp.stack([grads["pool"][:, :, j * 128:(j + 1) * 128].reshape(2, 2 * PG, 128)
                                 for j in range(NCHIP)], axis=1)
            grads = {"in": gw_in, "pool": gw_pool}
        nms, parts = list(grads.keys()), list(grads.values())
        send, recv, parts, got, token = exchange_start("exchange_start_" + group, parts)
        pending[group] = (nms, send, recv, parts, got)
        return token

    def on_settle(group, after):
        if group == "in":
            for earlier in ("down", "up", "mix"):
                after = reduce_group(earlier, after)
        nms, send, recv, parts, got = pending[group]
        parts, got = exchange_wait("exchange_wait_" + group, send, recv, parts, got, after)
        sums = [add_pairs("add_pair_" + nm, a, b, core_i) for nm, a, b in zip(nms, parts, got)]
        send, recv, sums, lands, token = scatter_start("scatter_start_" + group, sums)
        pending[group] = (nms, send, recv, sums, lands)
        if group != "in":
            return token
        for earlier in ("down", "up", "mix"):
            update_group(earlier, token)
            token = big_res[pending[earlier][0][-1]][1]
        return [big_res[nm][1] for nm in ("down", "up", "out", "gla_out")]

    (loss_local, grad_x, g_mix, g_ps, g_mlp, g_nf, g_ng, g_ba, g_wa) = local_step(
        x2d, tgt, gf, norm_mix_g, pool_scale, wa_pad, b_alpha, ng, norm_mlp_g, get_w, on_grad, on_settle, tick)[:9]
    loss = lax.psum(loss_local, ("x", "y", "c"))
    join_in_token = reduce_group("in", grad_x)

    ROWS = 16

    def wide(a, n):
        return jnp.concatenate([a.reshape(1, n), jnp.zeros((1, D - n), F32)], axis=1)

    packed = pack_rows("pack_small_g", [g_mix, g_ps, g_mlp, g_nf, g_ng, wide(g_ba, QK), g_wa[0:16].reshape(8, D)], ROWS)
    tot = gather_small("reduce_small_g", packed, True, join_in_token)
    t_wa = lax.dynamic_slice(tot[6:14].reshape(16, QK), (0, chip * DK), (16, DK))
    t_ng = lax.dynamic_slice(tot[4].reshape(HEADS, DV), (0, chip * (DV // NCHIP)), (HEADS, DV // NCHIP))

    def pack_small(nm, mix, ps, mlp, nf, ba, wa, gn, after=None):
        return pack_rows(nm, [mix.reshape(1, D), ps.reshape(1, D), mlp.reshape(1, D), nf.reshape(1, D), wide(ba, QK),
                              wa.reshape(2, D), wide(gn, 512)], ROWS, after)

    update_group("in", tot)
    sg = pack_small("pack_g", tot[0], tot[1], tot[2], tot[3], tot[5, 0:QK], t_wa, t_ng, big_res["in"][3])
    sw = pack_small("pack_w", norm_mix_g, pool_scale, norm_mlp_g, norm_final_g, b_alpha, w_alpha, gla_norm_g)
    sm = pack_small("pack_m", m_norm_mix_g, m_pool_scale, m_norm_mlp_g, m_norm_final_g, m_b_alpha, m_w_alpha, m_gla_norm_g)
    sv = pack_small("pack_v", v_norm_mix_g, v_pool_scale, v_norm_mlp_g, v_norm_final_g, v_b_alpha, v_w_alpha, v_gla_norm_g)
    small_res = adamw("adamw_small", sw, sg, sm, sv)

    def unpack(p):
        return {"norm_mix_g": p[0].reshape(1, D), "pool_scale": p[1].reshape(1, D), "norm_mlp_g": p[2].reshape(1, D),
                "norm_final_g": p[3].reshape(D), "b_alpha": p[4, 0:QK].reshape(1, QK), "w_alpha": p[5:7].reshape(1, 16, DK),
                "gla_norm_g": p[7, 0:512].reshape(1, HEADS, DV // NCHIP)}

    order = ["norm_mix_g", "w_in", "pool_w", "pool_scale", "w_alpha", "b_alpha", "gla_norm_g", "w_gla_out", "w_out",
             "norm_mlp_g", "w_mlp_up", "w_mlp_down", "norm_final_g"]
    big_key = {"w_in": ("in", w_in.shape), "pool_w": ("pool", pool_w.shape), "w_gla_out": ("gla_out", w_gla_out.shape),
               "w_out": ("out", w_out.shape), "w_mlp_up": ("up", w_mlp_up.shape), "w_mlp_down": ("down", w_mlp_down.shape)}
    result = [loss, grad_x.reshape(1, T, D)]
    for kind in range(4):
        small = unpack(small_res[kind])
        for nm in order:
            if nm in big_key:
                key, shp = big_key[nm]
                result.append(big_res[key][kind].reshape(shp))
            else:
                result.append(small[nm])
    return tuple(result)
```

```python
import itertools

import jax
import jax.numpy as jnp
from jax import lax
from jax.experimental import pallas as pl
from jax.experimental.pallas import tpu as pltpu

F32 = jnp.float32
BF16 = jnp.bfloat16
SDS = jax.ShapeDtypeStruct
PINNED = pltpu.HBM
MESH = pl.DeviceIdType.MESH
ANY = pl.BlockSpec(memory_space=pl.ANY)

T = 2048
D = 2048
DFF = 8192
NCHIP = 4
IN_WIDTH = 11280
IN_SHARD = IN_WIDTH // NCHIP
CHUNK = 64
NCHUNK = T // CHUNK
HEADS = 4
DK = 256
DV = 512
QK = HEADS * DK
EPS = 1e-6
POOL_WINDOWS = (2, 4, 8, 16)
PG = 256
PO = 512

OV, OG, OGP, OGG, OU, OQ, OKK, OA = 0, 2048, 4096, 6144, 8192, 9216, 10240, 11264
NCAT = 11520
APAD = 128

VMEM_CAP = 56 * 1024 * 1024

PIECE_BYTES = 384 * 1024

ADAM_LR, ADAM_B1, ADAM_B2, ADAM_EPS, ADAM_WD, ADAM_STEP = 0.001, 0.9, 0.999, 1e-08, 0.01, 10


def _cparams(vmem_bytes=None, sem=None):
    kw = {}
    if vmem_bytes is not None:
        kw["vmem_limit_bytes"] = int(min(max(vmem_bytes, 32 * 1024 * 1024), VMEM_CAP))
    if sem is not None:
        kw["dimension_semantics"] = sem
    return pltpu.CompilerParams(**kw)


def _nbytes(shape, dtype):
    n = 1
    for s in shape:
        if s is not None:
            n *= s
    return n * jnp.dtype(dtype).itemsize


def _sigmoid(x):
    return 0.5 * jnp.tanh(0.5 * x) + 0.5


GLA_STEP = 4
EPI_COLS = 512


def _as_list(after):
    if after is None:
        return []
    return list(after) if isinstance(after, (list, tuple)) else [after]


def _in_hbm(*arrays):
    return [pltpu.with_memory_space_constraint(a, pltpu.HBM) for a in arrays]


def matmul(name, a, b, *, a_spec, b_spec, cdims, grid, acc_shape, outs, extras=(), epi, after=None, into=None):
    nj, ni, nk = grid
    ne, no = len(extras), len(outs)
    afters = _as_list(after) + ([] if into is None else [into[0]])
    first_out = 2 + ne + len(afters)

    def body(*refs):
        a_ref, b_ref = refs[0], refs[1]
        ex = refs[2:2 + ne]
        out_refs = refs[first_out:first_out + no]
        i = pl.program_id(1)
        part = lax.dot_general(a_ref[...], b_ref[...], (cdims, ((), ())), preferred_element_type=F32)
        if nk == 1:
            epi(part, ex, out_refs, i)
        else:
            acc_ref = refs[first_out + no]
            k = pl.program_id(2)

            @pl.when(k == 0)
            def _():
                acc_ref[...] = part

            @pl.when(k > 0)
            def _():
                acc_ref[...] += part

            @pl.when(k == nk - 1)
            def _():
                epi(acc_ref[...], ex, out_refs, i)

    in_specs = [pl.BlockSpec(*a_spec), pl.BlockSpec(*b_spec)] + [pl.BlockSpec(bs, im) for _, bs, im in extras]
    in_specs += [ANY] * len(afters)
    out_specs = [pl.BlockSpec(bs, im) for _, _, bs, im in outs]
    out_shape = [PINNED(s, dt) for s, dt, _, _ in outs]
    vm = 2 * (_nbytes(a_spec[0], a.dtype) + _nbytes(b_spec[0], b.dtype))
    vm += 2 * sum(_nbytes(bs, arr.dtype) for arr, bs, _ in extras)
    vm += 2 * sum(_nbytes(bs, dt) for _, dt, bs, _ in outs)
    vm += 6 * _nbytes(acc_shape, F32)
    scratch = [pltpu.VMEM(acc_shape, F32)] if nk > 1 else []
    return pl.pallas_call(
        body, name=name, grid=grid, in_specs=in_specs, out_specs=out_specs, out_shape=out_shape,
        scratch_shapes=scratch,
        input_output_aliases={} if into is None else {first_out - 1: into[1]},
        compiler_params=_cparams(vm, ("arbitrary", "arbitrary", "arbitrary")),
    )(*_in_hbm(a, b, *[arr for arr, _, _ in extras]), *afters)


NN =((1,), (0,))
NT = ((1,), (1,))
TN = ((0,), (0,))


def _row_acc(out_ref, val, i):
    @pl.when(i == 0)
    def _():
        out_ref[...] = val

    @pl.when(i > 0)
    def _():
        out_ref[...] += val


def _rms_bwd(xn, r, dxn):
    return r * (dxn - xn * jnp.mean(dxn * xn, axis=-1, keepdims=True))


def norm1(x, g):
    tm = 256

    def body(x_ref, g_ref, h_ref):
        xv = x_ref[...]
        r = lax.rsqrt(jnp.mean(xv * xv, axis=-1, keepdims=True) + EPS)
        h_ref[...] = (xv * r * g_ref[...]).astype(BF16)

    return pl.pallas_call(
        body, name="norm1", grid=(T // tm,),
        in_specs=[pl.BlockSpec((tm, D), lambda i: (i, 0)), pl.BlockSpec((1, D), lambda i: (0, 0))],
        out_specs=pl.BlockSpec((tm, D), lambda i: (i, 0)), out_shape=PINNED((T, D), BF16),
        compiler_params=_cparams(32 * 1024 * 1024, ("arbitrary",)),
    )(*_in_hbm(x, g))


def mm_in(h1, wcat):
    tm, tn = 1024, 1280

    def epi(acc, ex, outs, i):
        outs[0][...] = acc.astype(BF16)

    return matmul("mm_in", h1, wcat, a_spec=((tm, D), lambda j, i, k: (i, 0)), b_spec=((D, tn), lambda j, i, k: (0, j)),
                  cdims=NN, grid=(NCAT // tn, T // tm, 1), acc_shape=(tm, tn),
                  outs=[((T, NCAT), BF16, (tm, tn), lambda j, i, k: (i, j))], epi=epi)[0]


def _window_sum(x, w, up):
    n = x.shape[0]
    row = lax.broadcasted_iota(jnp.int32, x.shape, 0)
    s, sh = x, 1
    while sh < w:
        if up:
            s = s + jnp.where(row < n - sh, pltpu.roll(s, n - sh, axis=0), 0.0)
        else:
            s = s + jnp.where(row >= sh, pltpu.roll(s, sh, axis=0), 0.0)
        sh *= 2
    return s


def _inv_count(shape, w):
    row = lax.broadcasted_iota(jnp.int32, shape, 0)
    return 1.0 / jnp.minimum(row + 1, w).astype(F32)


def pool_fwd(pcat, pw, after=None):
    afters = _as_list(after)

    def body(u_ref, pw_ref, *rest):
        d_ref, y_ref = rest[len(afters):]
        for gi, w in enumerate(POOL_WINDOWS):
            ug = u_ref[:, gi * PG:(gi + 1) * PG].astype(F32)
            dg = _window_sum(ug, w, False) * _inv_count(ug.shape, w) - ug
            db = dg.astype(BF16)
            d_ref[:, gi * PG:(gi + 1) * PG] = db
            y_ref[:, gi * PO:(gi + 1) * PO] = jnp.dot(db, pw_ref[gi], preferred_element_type=F32).astype(BF16)

    return pl.pallas_call(
        body, name="pool_fwd", grid=(1,),
        in_specs=[pl.BlockSpec((T, 4 * PG), lambda i: (0, OU // (4 * PG))), pl.BlockSpec((4, PG, PO), lambda i: (0, 0, 0))]
                 + [ANY] * len(afters),
        out_specs=[pl.BlockSpec((T, 4 * PG), lambda i: (0, 0)), pl.BlockSpec((T, D), lambda i: (0, 0))],
        out_shape=[PINNED((T, 4 * PG), BF16), PINNED((T, D), BF16)],
        compiler_params=_cparams(48 * 1024 * 1024, ("arbitrary",)),
    )(pcat, pw, *afters)


def pool_bwd(dylin, d, pw, dproj):
    assert OU % (4 * PG) == 0

    def body(dy_ref, d_ref, pw_ref, held_ref, du_ref, dpw_ref):
        for gi, w in enumerate(POOL_WINDOWS):
            dyl = dy_ref[:, gi * PO:(gi + 1) * PO]
            dd = lax.dot_general(dyl, pw_ref[gi], (NT, ((), ())), preferred_element_type=F32)
            du = _window_sum(dd * _inv_count(dd.shape, w), w, True) - dd
            du_ref[:, gi * PG:(gi + 1) * PG] = du.astype(BF16)
            dpw_ref[gi] = lax.dot_general(d_ref[:, gi * PG:(gi + 1) * PG], dyl, (TN, ((), ())),
                                          preferred_element_type=F32).astype(BF16)

    return pl.pallas_call(
        body, name="pool_bwd", grid=(1,),
        in_specs=[pl.BlockSpec((T, D), lambda i: (0, 0)), pl.BlockSpec((T, 4 * PG), lambda i: (0, 0)),
                  pl.BlockSpec((4, PG, PO), lambda i: (0, 0, 0)), ANY],
        out_specs=[pl.BlockSpec((T, 4 * PG), lambda i: (0, OU // (4 * PG))), pl.BlockSpec((4, PG, PO), lambda i: (0, 0, 0))],
        out_shape=[PINNED((T, NCAT), BF16), PINNED((4, PG, PO), BF16)],
        input_output_aliases={3: 0},
        compiler_params=_cparams(48 * 1024 * 1024, ("arbitrary",)),
    )(dylin, d, pw, dproj)


def _gate_decay(alow, wa, ba):
    a = jnp.dot(alow, wa, preferred_element_type=F32) + ba
    ls = jax.nn.log_sigmoid(a) * (1.0 / 16.0)
    r = lax.broadcasted_iota(jnp.int32, (CHUNK, CHUNK), 0)
    c = lax.broadcasted_iota(jnp.int32, (CHUNK, CHUNK), 1)
    tri = jnp.where(c <= r, 1.0, 0.0).astype(F32)
    cum = jnp.dot(tri, ls, preferred_element_type=F32, precision=lax.Precision.HIGHEST)
    last = cum[CHUNK - 1:CHUNK, :]
    return a, jnp.exp(last - cum), jnp.exp(last)


def gla_fwd(pcat, wa, ba, ng, after=None):
    afters = _as_list(after)

    def body(q_ref, k_ref, v_ref, g_ref, al_ref, wa_ref, ba_ref, ng_ref, *rest):
        og_ref, o_ref, st_ref, s_scr = rest[len(afters):]

        @pl.when(pl.program_id(0) == 0)
        def _():
            s_scr[...] = jnp.zeros_like(s_scr)

        state = [s_scr[h] for h in range(HEADS)]
        for s in range(GLA_STEP):
            rs = slice(s * CHUNK, (s + 1) * CHUNK)
            _, e, decay = _gate_decay(al_ref[rs, :], wa_ref[...], ba_ref[...])
            kd = (k_ref[rs, :].astype(F32) * e).astype(BF16)
            qs = (q_ref[rs, :].astype(F32) * (DK ** -0.5)).astype(BF16)
            for h in range(HEADS):
                ck = slice(h * DK, (h + 1) * DK)
                cv = slice(h * DV, (h + 1) * DV)
                state[h] = state[h] * decay[:, ck] + lax.dot_general(v_ref[rs, cv], kd[:, ck], (TN, ((), ())),
                                                                     preferred_element_type=F32)
                sb = state[h].astype(BF16)
                st_ref[s, h] = sb
                oh = lax.dot_general(qs[:, ck], sb, (NT, ((), ())), preferred_element_type=F32)
                o_ref[rs, cv] = oh.astype(BF16)
                on = oh * lax.rsqrt(jnp.mean(oh * oh, axis=-1, keepdims=True) + EPS) * ng_ref[:, cv]
                gv = g_ref[rs, cv].astype(F32)
                og_ref[rs, cv] = (on * (gv * _sigmoid(gv))).astype(BF16)
        for h in range(HEADS):
            s_scr[h] = state[h]

    row = lambda c: (c, 0)
    rows = GLA_STEP * CHUNK
    return pl.pallas_call(
        body, name="gla_fwd", grid=(NCHUNK // GLA_STEP,),
        in_specs=[pl.BlockSpec((rows, QK), lambda c: (c, OQ // QK)), pl.BlockSpec((rows, QK), lambda c: (c, OKK // QK)),
                  pl.BlockSpec((rows, D), lambda c: (c, OV // D)), pl.BlockSpec((rows, D), lambda c: (c, OG // D)),
                  pl.BlockSpec((rows, APAD), lambda c: (c, OA // APAD)),
                  pl.BlockSpec((APAD, QK), lambda c: (0, 0)), pl.BlockSpec((1, QK), lambda c: (0, 0)),
                  pl.BlockSpec((1, D), lambda c: (0, 0))] + [ANY] * len(afters),
        out_specs=[pl.BlockSpec((rows, D), row), pl.BlockSpec((rows, D), row),
                   pl.BlockSpec((GLA_STEP, HEADS, DV, DK), lambda c: (c, 0, 0, 0))],
        out_shape=[PINNED((T, D), BF16), PINNED((T, D), BF16), PINNED((NCHUNK, HEADS, DV, DK), BF16)],
        scratch_shapes=[pltpu.VMEM((HEADS, DV, DK), F32)],
        compiler_params=_cparams(32 * 1024 * 1024, ("arbitrary",)),
    )(*_in_hbm(pcat, pcat, pcat, pcat, pcat, wa, ba, ng, *afters))


def gla_bwd(do, pcat, states, wa, ba, dproj, after):
    tail = NCAT - OQ
    assert (OKK, OA) == (OQ + QK, OQ + 2 * QK) and OQ % tail == 0

    def body(do_ref, q_ref, k_ref, v_ref, al_ref, sc_ref, sp_ref, wa_ref, ba_ref, after_ref, held_ref,
             dp_ref, dv_ref, dwa_ref, dba_ref, ds_scr):
        i = pl.program_id(0)
        dp_ref[:, 2 * QK + APAD:] = jnp.zeros((GLA_STEP * CHUNK, tail - 2 * QK - APAD), BF16)

        @pl.when(i == 0)
        def _():
            ds_scr[...] = jnp.zeros_like(ds_scr)

        ds = [ds_scr[h] for h in range(HEADS)]
        dwa, dba = 0.0, 0.0
        for u in reversed(range(GLA_STEP)):
            rs = slice(u * CHUNK, (u + 1) * CHUNK)
            first_chunk = jnp.logical_and(i == NCHUNK // GLA_STEP - 1, u == 0)
            has_prev = jnp.where(first_chunk, 0.0, 1.0).astype(F32)
            a, e, decay = _gate_decay(al_ref[rs, :], wa_ref[...], ba_ref[...])
            kdf = k_ref[rs, :].astype(F32) * e
            kd = kdf.astype(BF16)
            qs = (q_ref[rs, :].astype(F32) * (DK ** -0.5)).astype(BF16)
            dkd_parts, ddecay_parts = [], []
            for h in range(HEADS):
                ck = slice(h * DK, (h + 1) * DK)
                cv = slice(h * DV, (h + 1) * DV)
                doh = do_ref[rs, cv]
                dsh = ds[h] + lax.dot_general(doh, qs[:, ck], (TN, ((), ())), preferred_element_type=F32)
                dsb = dsh.astype(BF16)
                dp_ref[rs, ck] = (jnp.dot(doh, sc_ref[u, h], preferred_element_type=F32) * (DK ** -0.5)).astype(BF16)
                dkd_parts.append(jnp.dot(v_ref[rs, cv], dsb, preferred_element_type=F32))
                dv_ref[rs, cv] = lax.dot_general(kd[:, ck], dsb, (NT, ((), ())), preferred_element_type=F32).astype(BF16)
                s_prev = (sp_ref[h] if u == 0 else sc_ref[u - 1, h]).astype(F32)
                ddecay_parts.append(jnp.sum(dsh * s_prev, axis=0, keepdims=True) * has_prev)
                ds[h] = dsh * decay[:, ck]
            dkd = jnp.concatenate(dkd_parts, axis=1)
            ddecay = jnp.concatenate(ddecay_parts, axis=1)
            dp_ref[rs, QK:2 * QK] = (dkd * e).astype(BF16)
            dearg = dkd * kdf
            dlast = jnp.sum(dearg, axis=0, keepdims=True) + ddecay * decay
            r = lax.broadcasted_iota(jnp.int32, (CHUNK, CHUNK), 0)
            c = lax.broadcasted_iota(jnp.int32, (CHUNK, CHUNK), 1)
            triu = jnp.where(c >= r, 1.0, 0.0).astype(F32)
            dls = dlast - jnp.dot(triu, dearg, preferred_element_type=F32, precision=lax.Precision.HIGHEST)
            da = dls * (1.0 / 16.0) * (1.0 - _sigmoid(a))
            dab = da.astype(BF16)
            dp_ref[rs, 2 * QK:2 * QK + APAD] = lax.dot_general(dab, wa_ref[...], (NT, ((), ())),
                                                               preferred_element_type=F32).astype(BF16)
            dwa = dwa + lax.dot_general(al_ref[rs, :], dab, (TN, ((), ())), preferred_element_type=F32)
            dba = dba + jnp.sum(da, axis=0, keepdims=True)
        for h in range(HEADS):
            ds_scr[h] = ds[h]

        @pl.when(i == 0)
        def _():
            dwa_ref[...] = dwa
            dba_ref[...] = dba

        @pl.when(i > 0)
        def _():
            dwa_ref[...] += dwa
            dba_ref[...] += dba

    rows = GLA_STEP * CHUNK
    rev = lambda i: NCHUNK // GLA_STEP - 1 - i
    return pl.pallas_call(
        body, name="gla_bwd", grid=(NCHUNK // GLA_STEP,),
        in_specs=[pl.BlockSpec((rows, D), lambda i: (rev(i), 0)),
                  pl.BlockSpec((rows, QK), lambda i: (rev(i), OQ // QK)), pl.BlockSpec((rows, QK), lambda i: (rev(i), OKK // QK)),
                  pl.BlockSpec((rows, D), lambda i: (rev(i), OV // D)), pl.BlockSpec((rows, APAD), lambda i: (rev(i), OA // APAD)),
                  pl.BlockSpec((GLA_STEP, HEADS, DV, DK), lambda i: (rev(i), 0, 0, 0)),
                  pl.BlockSpec((None, HEADS, DV, DK), lambda i: (jnp.maximum(rev(i) * GLA_STEP - 1, 0), 0, 0, 0)),
                  pl.BlockSpec((APAD, QK), lambda i: (0, 0)), pl.BlockSpec((1, QK), lambda i: (0, 0)), ANY, ANY],
        out_specs=[pl.BlockSpec((rows, tail), lambda i: (rev(i), OQ // tail)), pl.BlockSpec((rows, D), lambda i: (rev(i), 0)),
                   pl.BlockSpec((APAD, QK), lambda i: (0, 0)), pl.BlockSpec((1, QK), lambda i: (0, 0))],
        out_shape=[PINNED((T, NCAT), BF16), PINNED((T, D), BF16), PINNED((APAD, QK), F32), PINNED((1, QK), F32)],
        scratch_shapes=[pltpu.VMEM((HEADS, DV, DK), F32)],
        input_output_aliases={10: 0},
        compiler_params=_cparams(32 * 1024 * 1024, ("arbitrary",)),
    )(*_in_hbm(do, pcat, pcat, pcat, pcat, states, states, wa, ba), after, dproj)


TMF = 256
TMW = 512
_rowblk = ((TMF, D), lambda j, i, k: (i, 0))
_vec = ((1, D), lambda j, i, k: (0, 0))


def _full_spec(col):
    return ((TMF, D), lambda j, i, k: (i, col))


TBIG = 1024


def square_matmul(name, a, b, *, a_spec, b_spec, cdims, nk, after=None):
    def epi(acc, ex, outs, i):
        outs[0][...] = acc

    return matmul(name, a, b, a_spec=a_spec, b_spec=b_spec, cdims=cdims, grid=(D // TBIG, T // TBIG, nk),
                  acc_shape=(TBIG, TBIG), outs=[((T, D), F32, (TBIG, TBIG), lambda j, i, k: (i, j))], epi=epi,
                  after=after)[0]


def rowwise(name, y, *, extras, outs, epi):
    ne = len(extras)

    def body(*refs):
        epi(refs[0][...], refs[1:1 + ne], refs[1 + ne:], pl.program_id(1))

    in_specs = [pl.BlockSpec(*_rowblk)] + [pl.BlockSpec(bs, im) for _, bs, im in extras]
    return pl.pallas_call(
        body, name=name, grid=(1, T // TMF, 1), in_specs=in_specs,
        out_specs=[pl.BlockSpec(bs, im) for _, _, bs, im in outs], out_shape=[PINNED(s, dt) for s, dt, _, _ in outs],
        compiler_params=_cparams(40 * 1024 * 1024, ("arbitrary", "arbitrary", "arbitrary")),
    )(*_in_hbm(y, *[arr for arr, _, _ in extras]))


def mm_gla_out(og, w, ylin, pcat, pscale):
    def epi(acc, ex, outs, i):
        ylin_ref, lgp_ref, lgg_ref, ps_ref = ex
        for c0 in range(0, D, EPI_COLS):
            cs = slice(c0, c0 + EPI_COLS)
            gp = _sigmoid(lgp_ref[:, cs].astype(F32))
            gg = _sigmoid(lgg_ref[:, cs].astype(F32))
            a = acc[:, cs]
            outs[0][:, cs] = (gp * (ylin_ref[:, cs].astype(F32) * ps_ref[:, cs]) + gg * a).astype(BF16)
            outs[1][:, cs] = a.astype(BF16)

    return matmul("mm_gla_out", og, w, a_spec=_rowblk, b_spec=((D, D), lambda j, i, k: (0, 0)), cdims=NN,
                  grid=(1, T // TMF, 1), acc_shape=(TMF, D),
                  extras=[(ylin, *_rowblk), (pcat, *_full_spec(OGP // D)), (pcat, *_full_spec(OGG // D)), (pscale, *_vec)],
                  outs=[((T, D), BF16, *_rowblk), ((T, D), BF16, *_rowblk)], epi=epi)


def mm_out(mixed, w, x, g2):
    def epi(acc, ex, outs, i):
        x_ref, g_ref = ex
        x2 = x_ref[...] + acc
        r = lax.rsqrt(jnp.mean(x2 * x2, axis=-1, keepdims=True) + EPS)
        outs[0][...] = x2
        outs[1][...] = (x2 * r * g_ref[...]).astype(BF16)

    return matmul("mm_out", mixed, w, a_spec=_rowblk, b_spec=((D, D), lambda j, i, k: (0, 0)), cdims=NN,
                  grid=(1, T // TMF, 1), acc_shape=(TMF, D), extras=[(x, *_rowblk), (g2, *_vec)],
                  outs=[((T, D), F32, *_rowblk), ((T, D), BF16, *_rowblk)], epi=epi)


def mm_up(h2, wup):
    def epi(acc, ex, outs, i):
        r = jnp.maximum(acc, 0.0)
        outs[0][...] = r.astype(BF16)
        outs[1][...] = (r * r).astype(BF16)

    blk = ((TMW, D), lambda j, i, k: (i, j))
    return matmul("mm_up", h2, wup, a_spec=((TMW, D), lambda j, i, k: (i, 0)), b_spec=((None, D, D), lambda j, i, k: (j, 0, 0)),
                  cdims=NN, grid=(NCHIP, T // TMW, 1), acc_shape=(TMW, D),
                  outs=[((T, DFF), BF16, *blk), ((T, DFF), BF16, *blk)], epi=epi)


def mm_down(act, wdown, x2, tgt, gf):
    tk = 4096

    def epi(acc, ex, outs, i):
        x2_ref, t_ref, g_ref = ex
        dx_ref, dxb_ref, gnf_ref, loss_ref = outs
        x3 = x2_ref[...] + acc
        r = lax.rsqrt(jnp.mean(x3 * x3, axis=-1, keepdims=True) + EPS)
        xn = x3 * r
        err = xn * g_ref[...] - t_ref[...]
        lsum = 0.5 * jnp.sum(jnp.mean(err * err, axis=-1, keepdims=True), axis=0, keepdims=True)
        dy = err * (1.0 / D)
        _row_acc(gnf_ref, jnp.sum(dy * xn, axis=0, keepdims=True), i)
        _row_acc(loss_ref, jnp.broadcast_to(lsum, (1, 128)), i)
        dx3 = _rms_bwd(xn, r, dy * g_ref[...])
        dx_ref[...] = dx3
        dxb_ref[...] = dx3.astype(BF16)

    y = square_matmul("mm_down", act, wdown, a_spec=((TBIG, tk), lambda j, i, k: (i, k)),
                      b_spec=((tk, TBIG), lambda j, i, k: (k, j)), cdims=NN, nk=DFF // tk)
    return rowwise("rows_final", y, extras=[(x2, *_rowblk), (tgt, *_rowblk), (gf, *_vec)],
                   outs=[((T, D), F32, *_rowblk), ((T, D), BF16, *_rowblk), ((1, D), F32, *_vec),
                         ((1, 128), F32, (1, 128), lambda j, i, k: (0, 0))], epi=epi)


def mm_dact(dx3b, wdown, rup, after=None):
    def epi(acc, ex, outs, i):
        outs[0][...] = (acc * 2.0 * ex[0][...].astype(F32)).astype(BF16)

    blk = ((TMW, D), lambda j, i, k: (i, j))
    return matmul("mm_dact", dx3b, wdown, a_spec=((TMW, D), lambda j, i, k: (i, 0)), b_spec=((D, D), lambda j, i, k: (j, 0)),
                  cdims=NT, grid=(DFF // D, T // TMW, 1), acc_shape=(TMW, D), extras=[(rup, *blk)],
                  outs=[((T, DFF), BF16, *blk)], epi=epi, after=after)[0]


def mm_wgrad(name, a, b, m, n, out_shape, out_block, out_map, tm, tn, after=None):
    def epi(acc, ex, outs, i):
        outs[0][...] = acc.astype(BF16).reshape(outs[0].shape)

    return matmul(name, a, b, a_spec=((T, tm), lambda j, i, k: (0, i)), b_spec=((T, tn), lambda j, i, k: (0, j)),
                  cdims=TN, grid=(n // tn, m // tm, 1), acc_shape=(tm, tn),
                  outs=[(out_shape, BF16, out_block, out_map)], epi=epi, after=after)[0]


def mm_dh2(dup, wup, x2, dx3, g2, after=None):
    def epi(acc, ex, outs, i):
        x2_ref, dx3_ref, g_ref = ex
        x2 = x2_ref[...]
        r = lax.rsqrt(jnp.mean(x2 * x2, axis=-1, keepdims=True) + EPS)
        xn = x2 * r
        _row_acc(outs[2], jnp.sum(acc * xn, axis=0, keepdims=True), i)
        dx2 = dx3_ref[...] + _rms_bwd(xn, r, acc * g_ref[...])
        outs[0][...] = dx2
        outs[1][...] = dx2.astype(BF16)

    y = square_matmul("mm_dh2", dup, wup, a_spec=((TBIG, D), lambda j, i, k: (i, k)),
                      b_spec=((None, TBIG, D), lambda j, i, k: (k, j, 0)), cdims=NT, nk=NCHIP, after=after)
    return rowwise("rows_dh2", y, extras=[(x2, *_rowblk), (dx3, *_rowblk), (g2, *_vec)],
                   outs=[((T, D), F32, *_rowblk), ((T, D), BF16, *_rowblk), ((1, D), F32, *_vec)], epi=epi)


def mm_dmixed(dx2b, wout, pcat, ylin, ygla, pscale, after=None):
    assert OGG == OGP + D and OGP % (2 * D) == 0

    def epi(acc, ex, outs, i):
        lgp_ref, lgg_ref, ylin_ref, ygla_ref, ps_ref = ex
        dps = []
        for c0 in range(0, D, EPI_COLS):
            cs = slice(c0, c0 + EPI_COLS)
            gp = _sigmoid(lgp_ref[:, cs].astype(F32))
            gg = _sigmoid(lgg_ref[:, cs].astype(F32))
            yl = ylin_ref[:, cs].astype(F32)
            ps = ps_ref[:, cs]
            a = acc[:, cs]
            agp = a * gp
            outs[0][:, cs] = (agp * ps).astype(BF16)
            outs[1][:, cs] = (a * gg).astype(BF16)
            outs[2][:, cs] = (agp * (yl * ps) * (1.0 - gp)).astype(BF16)
            outs[2][:, D + c0:D + c0 + EPI_COLS] = (a * ygla_ref[:, cs].astype(F32) * gg * (1.0 - gg)).astype(BF16)
            dps.append(jnp.sum(agp * yl, axis=0, keepdims=True))
        _row_acc(outs[3], jnp.concatenate(dps, axis=1), i)

    return matmul("mm_dmixed", dx2b, wout, a_spec=_rowblk, b_spec=((D, D), lambda j, i, k: (0, 0)), cdims=NT,
                  grid=(1, T // TMF, 1), acc_shape=(TMF, D),
                  extras=[(pcat, *_full_spec(OGP // D)), (pcat, *_full_spec(OGG // D)), (ylin, *_rowblk), (ygla, *_rowblk),
                          (pscale, *_vec)],
                  outs=[((T, D), BF16, *_rowblk)] * 2
                       + [((T, NCAT), BF16, (TMF, 2 * D), lambda j, i, k: (i, OGP // (2 * D))), ((1, D), F32, *_vec)],
                  epi=epi, after=after)


def mm_dog(dygla, wgo, o, pcat, ng, dproj, after=None):
    def epi(acc, ex, outs, i):
        o_ref, g_ref, ng_ref = ex
        do_ref, dg_ref, gng_ref = outs
        gparts = []
        for h in range(HEADS):
            cv = slice(h * DV, (h + 1) * DV)
            oh = o_ref[:, cv].astype(F32)
            r = lax.rsqrt(jnp.mean(oh * oh, axis=-1, keepdims=True) + EPS)
            on = oh * r
            gv = g_ref[:, cv].astype(F32)
            sg = _sigmoid(gv)
            a = acc[:, cv]
            dgain = a * (gv * sg)
            gparts.append(jnp.sum(dgain * on, axis=0, keepdims=True))
            ngh = ng_ref[:, cv]
            do_ref[:, cv] = _rms_bwd(on, r, dgain * ngh).astype(BF16)
            dg_ref[:, cv] = (a * (on * ngh) * (sg * (1.0 + gv * (1.0 - sg)))).astype(BF16)
        _row_acc(gng_ref, jnp.concatenate(gparts, axis=1), i)

    return matmul("mm_dog", dygla, wgo, a_spec=_rowblk, b_spec=((D, D), lambda j, i, k: (0, 0)), cdims=NT,
                  grid=(1, T // TMF, 1), acc_shape=(TMF, D),
                  extras=[(o, *_rowblk), (pcat, *_full_spec(OG // D)), (ng, *_vec)],
                  outs=[((T, D), BF16, *_rowblk), ((T, NCAT), BF16, *_full_spec(OG // D)), ((1, D), F32, *_vec)],
                  epi=epi, after=after, into=(dproj, 1))


def mm_dh1(dpcat, wcat, x, dx2, g1, after=None):
    tk = 3840

    def epi(acc, ex, outs, i):
        x_ref, dx2_ref, g_ref = ex
        xv = x_ref[...]
        r = lax.rsqrt(jnp.mean(xv * xv, axis=-1, keepdims=True) + EPS)
        xn = xv * r
        _row_acc(outs[1], jnp.sum(acc * xn, axis=0, keepdims=True), i)
        outs[0][...] = dx2_ref[...] + _rms_bwd(xn, r, acc * g_ref[...])

    y = square_matmul("mm_dh1", dpcat, wcat, a_spec=((TBIG, tk), lambda j, i, k: (i, k)),
                      b_spec=((TBIG, tk), lambda j, i, k: (j, k)), cdims=NT, nk=NCAT // tk, after=after)
    return rowwise("rows_dh1", y, extras=[(x, *_rowblk), (dx2, *_rowblk), (g1, *_vec)],
                   outs=[((T, D), F32, *_rowblk), ((1, D), F32, *_vec)], epi=epi)


def _tile_rows(rows, cols, n_arrays):
    tm = rows
    while tm % 32 == 0 and 2 * n_arrays * tm * cols * 4 > 36 * 1024 * 1024:
        tm //= 2
    return tm


def add_pairs(name, parts, theirs, core):
    _, _, r, c = parts.shape
    tm = _tile_rows(r, c, 3)

    def body(core_ref, a_ref, b_ref, o_ref):
        o_ref[...] = (a_ref[...].astype(F32) + b_ref[...].astype(F32)).astype(BF16)

    spec = pl.BlockSpec((None, tm, c), lambda j, i, core_ref: (j, i, 0))
    grid_spec = pltpu.PrefetchScalarGridSpec(
        num_scalar_prefetch=1, grid=(NCHIP, r // tm),
        in_specs=[pl.BlockSpec((None, None, tm, c), lambda j, i, core_ref: (core_ref[0], j, i, 0)), spec], out_specs=spec)
    return pl.pallas_call(body, name=name, grid_spec=grid_spec, out_shape=PINNED((NCHIP, r, c), BF16),
                          compiler_params=_cparams(40 * 1024 * 1024, ("arbitrary", "arbitrary")))(core, *_in_hbm(parts, theirs))


def sum_chips(name, sums, landed, chip):
    _, r, c = sums.shape
    tm = _tile_rows(r, c, 4)

    def body(chip_ref, own_ref, l_ref, o_ref):
        s = own_ref[...].astype(F32)
        for t in range(NCHIP - 1):
            s = s + l_ref[t].astype(F32)
        o_ref[...] = s

    grid_spec = pltpu.PrefetchScalarGridSpec(
        num_scalar_prefetch=1, grid=(r // tm,),
        in_specs=[pl.BlockSpec((None, tm, c), lambda i, chip_ref: (chip_ref[0], i, 0)),
                  pl.BlockSpec((NCHIP - 1, tm, c), lambda i, chip_ref: (0, i, 0))],
        out_specs=pl.BlockSpec((tm, c), lambda i, chip_ref: (i, 0)))
    return pl.pallas_call(body, name=name, grid_spec=grid_spec, out_shape=PINNED((r, c), F32),
                          compiler_params=_cparams(40 * 1024 * 1024, ("arbitrary",)))(chip, *_in_hbm(sums, landed))


def _adamw_math(wv, gv, mv, vv):
    mn = ADAM_B1 * mv + (1.0 - ADAM_B1) * gv
    vn = ADAM_B2 * vv + (1.0 - ADAM_B2) * (gv * gv)
    mh = mn / (1.0 - ADAM_B1 ** ADAM_STEP)
    vh = vn / (1.0 - ADAM_B2 ** ADAM_STEP)
    return -ADAM_LR * (mh / (jnp.sqrt(vh) + ADAM_EPS) + ADAM_WD * wv), mn, vn


def adamw(name, w, g, m, v):
    def body(w_ref, g_ref, m_ref, v_ref, go_ref, d_ref, mo_ref, vo_ref):
        gv = g_ref[...]
        go_ref[...] = gv
        d_ref[...], mo_ref[...], vo_ref[...] = _adamw_math(w_ref[...], gv, m_ref[...], v_ref[...])

    return pl.pallas_call(body, name=name, out_shape=[SDS(w.shape, F32)] * 4)(w, g, m, v)


def adamw_halves(name, w, g_own, g_sib, m, v, core):
    _, r, c = w.shape
    tm = _tile_rows(r, c, 10)

    def body(core_ref, w_ref, go_ref, gs_ref, m_ref, v_ref, g_out, d_out, m_out, v_out):
        gv = jnp.where(pl.program_id(0) == core_ref[0], go_ref[...], gs_ref[...])
        g_out[...] = gv
        d_out[...], m_out[...], v_out[...] = _adamw_math(w_ref[...], gv, m_ref[...], v_ref[...])

    full = pl.BlockSpec((None, tm, c), lambda h, i, core_ref: (h, i, 0))
    own = pl.BlockSpec((tm, c), lambda h, i, core_ref: (jnp.where(h == core_ref[0], i, 0), 0))
    sib = pl.BlockSpec((tm, c), lambda h, i, core_ref: (jnp.where(h == core_ref[0], 0, i), 0))
    grid_spec = pltpu.PrefetchScalarGridSpec(num_scalar_prefetch=1, grid=(2, r // tm),
                                             in_specs=[full, own, sib, full, full], out_specs=[full] * 4)
    return pl.pallas_call(body, name=name, grid_spec=grid_spec, out_shape=[SDS(w.shape, F32)] * 4,
                          compiler_params=_cparams(48 * 1024 * 1024, ("arbitrary", "arbitrary")))(core, *_in_hbm(w, g_own, g_sib, m, v))


def cast_bf16(name, w):
    _, r, c = w.shape
    tm = _tile_rows(r, c, 2)

    def body(w_ref, o_ref):
        o_ref[...] = w_ref[...].astype(BF16)

    spec = pl.BlockSpec((None, tm, c), lambda h, i: (h, i, 0))
    return pl.pallas_call(body, name=name, grid=(2, r // tm), in_specs=[spec], out_specs=spec, out_shape=PINNED(w.shape, BF16),
                          compiler_params=_cparams(40 * 1024 * 1024, ("arbitrary", "arbitrary")))(w)


def cast_to_slot(name, w, place):
    _, r, c = w.shape
    tm = _tile_rows(r, c, 2)

    def body(p_ref, w_ref, o_ref):
        o_ref[...] = w_ref[...].astype(BF16)

    grid_spec = pltpu.PrefetchScalarGridSpec(
        num_scalar_prefetch=1, grid=(2, r // tm), in_specs=[pl.BlockSpec((None, tm, c), lambda h, i, p: (h, i, 0))],
        out_specs=pl.BlockSpec((None, None, tm, c), lambda h, i, p: (p[1], h, i, 0)))
    return pl.pallas_call(body, name=name, grid_spec=grid_spec, out_shape=PINNED((NCHIP, 2, r, c), BF16),
                          compiler_params=_cparams(40 * 1024 * 1024, ("arbitrary", "arbitrary")))(place, *_in_hbm(w))


def pack_rows(name, parts, rows, after=None):
    width = parts[0].shape[1]
    n = len(parts)
    afters = _as_list(after)

    def body(*refs):
        out_ref = refs[n + len(afters)]
        out_ref[...] = jnp.zeros_like(out_ref)
        off = 0
        for p in refs[:n]:
            out_ref[off:off + p.shape[0], :] = p[...]
            off += p.shape[0]

    vm = pl.BlockSpec(memory_space=pltpu.VMEM)
    return pl.pallas_call(body, name=name, in_specs=[vm] * n + [ANY] * len(afters), out_specs=vm,
                          out_shape=SDS((rows, width), F32))(*parts, *afters)


def _place():
    x, y, c = lax.axis_index("x"), lax.axis_index("y"), lax.axis_index("c")
    chips = [(1 - x, y), (x, 1 - y), (1 - x, 1 - y)]
    return x, y, c, chips


def _row_split(shape, dtype):
    r, c = shape
    n = 1
    while r % (2 * n) == 0 and (r // (2 * n)) % 16 == 0 and (r // n) * c * jnp.dtype(dtype).itemsize > PIECE_BYTES:
        n *= 2
    return [pl.ds(s * (r // n), r // n) for s in range(n)]


def _pieces(ref):
    *lead, r, c = ref.shape
    split = _row_split((r, c), ref.dtype)
    return [ref.at[(*idx, s)] for idx in itertools.product(*[range(d) for d in lead]) for s in split]


HBM = pl.BlockSpec(memory_space=pltpu.HBM)
SEM = pl.BlockSpec(memory_space=pltpu.SEMAPHORE)
EFFECT = pltpu.SideEffectType.DATAFLOW_SIDE_EFFECTING


def _own_half(shard_refs, land, a, me, c):
    return land[a].at[me, c] if shard_refs[a] is None else shard_refs[a].at[c]


def _spread(refs, shards):
    it = iter(refs)
    return [None if s is None else next(it) for s in shards]


def gather_start(name, items, after=None):
    n = len(items)
    shards = [s if s.ndim == 3 else None for s in items]
    given = [s for s in shards if s is not None]
    ns = len(given)
    afters = _as_list(after)

    def body(*refs):
        src, land = _spread(refs[:ns], shards), refs[ns:ns + n]
        send, recv = refs[ns + n + len(afters)], refs[ns + n + len(afters) + 1]
        x, y, c, chips = _place()
        me = 2 * x + y
        for a in range(n):
            for j, (cx, cy) in enumerate(chips[:2]):
                for sp, dp in zip(_pieces(_own_half(src, land, a, me, c)), _pieces(land[a].at[me, c])):
                    pltpu.make_async_remote_copy(sp, dp, send.at[2 * a + j], recv.at[2 * a + j],
                                                 device_id=(cx, cy, c), device_id_type=MESH).start()

    lands = [pltpu.with_memory_space_constraint(lax.empty((NCHIP,) + s.shape, s.dtype) if s.ndim == 3 else s, pltpu.HBM)
             for s in items]
    srcs = [pltpu.with_memory_space_constraint(s, pltpu.HBM) for s in given]
    outs = pl.pallas_call(
        body, name=name,
        out_shape=(pltpu.SemaphoreType.DMA((2 * n,)), pltpu.SemaphoreType.DMA((2 * n,)),
                   *[pltpu.HBM(s.shape, s.dtype) for s in given], *[pltpu.HBM(l.shape, l.dtype) for l in lands]),
        in_specs=[HBM] * (ns + n) + [ANY] * len(afters), out_specs=(SEM, SEM, *([HBM] * (ns + n))),
        input_output_aliases={i: 2 + i for i in range(ns + n)},
        compiler_params=pltpu.CompilerParams(has_side_effects=EFFECT),
    )(*srcs, *lands, *afters)
    return outs[0], outs[1], _spread(outs[2:2 + ns], shards), list(outs[2 + ns:2 + ns + n])


def _relay_blocks(land, c, chips):
    (xx, xy), (yx, yy), (dx, dy) = chips
    rows = land.shape[2] // 2
    upper, lower = pl.ds(0, rows), pl.ds(rows, rows)
    return [(land.at[2 * yx + yy, c, lower], land.at[2 * dx + dy, c, lower]),
            (land.at[2 * xx + xy, c, upper], land.at[2 * dx + dy, c, upper])]


def relay_turn(name, send, recv, shards, lands, after):
    n = len(lands)
    given = [s for s in shards if s is not None]
    ns = len(given)
    afters = _as_list(after)

    def body(*refs):
        src, had = _spread(refs[:ns], shards), refs[ns:ns + n]
        send_ref, recv_ref = refs[ns + n], refs[ns + n + 1]
        rsend, rrecv = refs[ns + n + 2 + len(afters)], refs[ns + n + 3 + len(afters)]
        land = refs[2 * ns + n + 4 + len(afters):2 * ns + 2 * n + 4 + len(afters)]
        x, y, c, chips = _place()
        me = 2 * x + y
        for a in range(n):
            for j, (cx, cy) in enumerate(chips[:2]):
                cp = pltpu.make_async_remote_copy(_own_half(src, had, a, me, c), had[a].at[2 * cx + cy, c],
                                                  send_ref.at[2 * a + j], recv_ref.at[2 * a + j],
                                                  device_id=(cx, cy, c), device_id_type=MESH)
                cp.wait_send()
                cp.wait_recv()
        for a in range(n):
            for j, ((sent, _), (dst, _)) in enumerate(zip(_relay_blocks(had[a], c, chips), _relay_blocks(land[a], c, chips))):
                cx, cy = chips[j]
                for sp, dp in zip(_pieces(sent), _pieces(dst)):
                    pltpu.make_async_remote_copy(sp, dp, rsend.at[2 * a + j], rrecv.at[2 * a + j],
                                                 device_id=(cx, cy, c), device_id_type=MESH).start()

    outs = pl.pallas_call(
        body, name=name,
        out_shape=(pltpu.SemaphoreType.DMA((2 * n,)), pltpu.SemaphoreType.DMA((2 * n,)),
                   *[pltpu.HBM(s.shape, s.dtype) for s in given], *[pltpu.HBM(l.shape, l.dtype) for l in lands]),
        in_specs=[HBM] * (ns + n) + [SEM, SEM] + [ANY] * len(afters), out_specs=(SEM, SEM, *([HBM] * (ns + n))),
        input_output_aliases={i: 2 + i for i in range(ns + n)},
        compiler_params=pltpu.CompilerParams(has_side_effects=EFFECT),
    )(*given, *lands, send, recv, *afters)
    return outs[0], outs[1], _spread(outs[2:2 + ns], shards), list(outs[2 + ns:2 + ns + n])


def relay_wait(name, send, recv, lands, after):
    n = len(lands)
    afters = _as_list(after)

    def body(*refs):
        land = refs[:n]
        send_ref, recv_ref = refs[n], refs[n + 1]
        x, y, c, chips = _place()
        for a in range(n):
            for j, (sent, got) in enumerate(_relay_blocks(land[a], c, chips)):
                cx, cy = chips[j]
                cp = pltpu.make_async_remote_copy(sent, got, send_ref.at[2 * a + j], recv_ref.at[2 * a + j],
                                                  device_id=(cx, cy, c), device_id_type=MESH)
                cp.wait_send()
                cp.wait_recv()

    outs = pl.pallas_call(
        body, name=name, out_shape=tuple(pltpu.HBM(l.shape, l.dtype) for l in lands),
        in_specs=[HBM] * n + [SEM, SEM] + [ANY] * len(afters), out_specs=[HBM] * n,
        input_output_aliases={i: i for i in range(n)},
        compiler_params=pltpu.CompilerParams(has_side_effects=EFFECT),
    )(*lands, send, recv, *afters)
    return list(outs)


def forward_halves(name, lands):
    n = len(lands)

    def body(*refs):
        had, buf = refs[:n], refs[n:2 * n]
        send, recv = refs[2 * n:]
        x, y, c, chips = _place()
        sib = (x, y, 1 - c)
        for a in range(n):
            for j, (cx, cy) in enumerate(chips):
                for sp, dp in zip(_pieces(had[a].at[2 * cx + cy, c]), _pieces(buf[a].at[2 * cx + cy, c])):
                    pltpu.make_async_remote_copy(sp, dp, send.at[3 * a + j], recv.at[3 * a + j], device_id=sib, device_id_type=MESH).start()
        for a in range(n):
            for j, (cx, cy) in enumerate(chips):
                pltpu.make_async_remote_copy(had[a].at[2 * cx + cy, c], buf[a].at[2 * cx + cy, 1 - c], send.at[3 * a + j],
                                             recv.at[3 * a + j], device_id=sib, device_id_type=MESH).wait()

    return pl.pallas_call(
        body, name=name, in_specs=[ANY] * n, out_specs=[ANY] * n, out_shape=[SDS(l.shape, l.dtype) for l in lands],
        input_output_aliases={i: i for i in range(n)},
        scratch_shapes=[pltpu.SemaphoreType.DMA((3 * n,)), pltpu.SemaphoreType.DMA((3 * n,))],
    )(*lands)


def forward_turn(name, send, recv, lands, after):
    n = len(lands)
    afters = _as_list(after)

    def body(*refs):
        had = refs[:n]
        send_ref, recv_ref = refs[n], refs[n + 1]
        fsend, frecv = refs[n + 2 + len(afters)], refs[n + 3 + len(afters)]
        buf = refs[n + 4 + len(afters):2 * n + 4 + len(afters)]
        x, y, c, chips = _place()
        sib = (x, y, 1 - c)
        for a in range(n):
            for j, (sent, got) in enumerate(_relay_blocks(had[a], c, chips)):
                cx, cy = chips[j]
                cp = pltpu.make_async_remote_copy(sent, got, send_ref.at[2 * a + j], recv_ref.at[2 * a + j],
                                                  device_id=(cx, cy, c), device_id_type=MESH)
                cp.wait_send()
                cp.wait_recv()
        for a in range(n):
            for j, (cx, cy) in enumerate(chips):
                for sp, dp in zip(_pieces(had[a].at[2 * cx + cy, c]), _pieces(buf[a].at[2 * cx + cy, c])):
                    pltpu.make_async_remote_copy(sp, dp, fsend.at[3 * a + j], frecv.at[3 * a + j], device_id=sib, device_id_type=MESH).start()

    outs = pl.pallas_call(
        body, name=name,
        out_shape=(pltpu.SemaphoreType.DMA((3 * n,)), pltpu.SemaphoreType.DMA((3 * n,)), *[pltpu.HBM(l.shape, l.dtype) for l in lands]),
        in_specs=[HBM] * n + [SEM, SEM] + [ANY] * len(afters), out_specs=(SEM, SEM, *([HBM] * n)),
        input_output_aliases={i: 2 + i for i in range(n)},
        compiler_params=pltpu.CompilerParams(has_side_effects=EFFECT),
    )(*lands, send, recv, *afters)
    return outs[0], outs[1], list(outs[2:])


def forward_wait(name, send, recv, lands, after):
    n = len(lands)
    afters = _as_list(after)

    def body(*refs):
        land = refs[:n]
        send_ref, recv_ref = refs[n], refs[n + 1]
        x, y, c, chips = _place()
        sib = (x, y, 1 - c)
        for a in range(n):
            for j, (cx, cy) in enumerate(chips):
                cp = pltpu.make_async_remote_copy(land[a].at[2 * cx + cy, c], land[a].at[2 * cx + cy, 1 - c], send_ref.at[3 * a + j],
                                                  recv_ref.at[3 * a + j], device_id=sib, device_id_type=MESH)
                cp.wait_send()
                cp.wait_recv()

    outs = pl.pallas_call(
        body, name=name, out_shape=tuple(pltpu.HBM(l.shape, l.dtype) for l in lands),
        in_specs=[HBM] * n + [SEM, SEM] + [ANY] * len(afters), out_specs=[HBM] * n,
        input_output_aliases={i: i for i in range(n)},
        compiler_params=pltpu.CompilerParams(has_side_effects=EFFECT),
    )(*lands, send, recv, *afters)
    return list(outs)


def exchange_start(name, parts):
    n = len(parts)

    def body(*refs):
        src, got = refs[:n], refs[n:2 * n]
        send, recv = refs[2 * n], refs[2 * n + 1]
        token = refs[4 * n + 2]
        x, y, c, _ = _place()
        sib = (x, y, 1 - c)
        for a in range(n):
            for sp, dp in zip(_pieces(src[a].at[1 - c]), _pieces(got[a])):
                pltpu.make_async_remote_copy(sp, dp, send.at[a], recv.at[a], device_id=sib, device_id_type=MESH).start()
        token[...] = jnp.zeros_like(token)

    lands = [pltpu.with_memory_space_constraint(lax.empty(p.shape[1:], p.dtype), pltpu.HBM) for p in parts]
    srcs = [pltpu.with_memory_space_constraint(p, pltpu.HBM) for p in parts]
    outs = pl.pallas_call(
        body, name=name,
        out_shape=(pltpu.SemaphoreType.DMA((n,)), pltpu.SemaphoreType.DMA((n,)),
                   *[pltpu.HBM(p.shape, p.dtype) for p in parts], *[pltpu.HBM(l.shape, l.dtype) for l in lands],
                   SDS((8, 128), F32)),
        in_specs=[HBM] * (2 * n), out_specs=(SEM, SEM, *([HBM] * (2 * n)), pl.BlockSpec(memory_space=pltpu.VMEM)),
        input_output_aliases={i: 2 + i for i in range(2 * n)},
        compiler_params=pltpu.CompilerParams(has_side_effects=EFFECT),
    )(*srcs, *lands)
    return outs[0], outs[1], list(outs[2:2 + n]), list(outs[2 + n:2 + 2 * n]), outs[2 + 2 * n]


def exchange_wait(name, send, recv, parts, lands, after):
    n = len(parts)
    afters = _as_list(after)

    def body(*refs):
        src, got = refs[:n], refs[n:2 * n]
        send_ref, recv_ref = refs[2 * n], refs[2 * n + 1]
        x, y, c, _ = _place()
        sib = (x, y, 1 - c)
        for a in range(n):
            cp = pltpu.make_async_remote_copy(src[a].at[1 - c], got[a], send_ref.at[a], recv_ref.at[a], device_id=sib, device_id_type=MESH)
            cp.wait_send()
            cp.wait_recv()

    outs = pl.pallas_call(
        body, name=name,
        out_shape=(*[pltpu.HBM(p.shape, p.dtype) for p in parts], *[pltpu.HBM(l.shape, l.dtype) for l in lands]),
        in_specs=[HBM] * (2 * n) + [SEM, SEM] + [ANY] * len(afters), out_specs=[HBM] * (2 * n),
        input_output_aliases={i: i for i in range(2 * n)},
        compiler_params=pltpu.CompilerParams(has_side_effects=EFFECT),
    )(*parts, *lands, send, recv, *afters)
    return list(outs[:n]), list(outs[n:])


def scatter_start(name, parts):
    n = len(parts)

    def body(*refs):
        src, land = refs[:n], refs[n:2 * n]
        send, recv = refs[2 * n], refs[2 * n + 1]
        token = refs[4 * n + 2]
        x, y, c, chips = _place()
        for a in range(n):
            for j, (cx, cy) in enumerate(chips):
                for sp, dp in zip(_pieces(src[a].at[2 * cx + cy]), _pieces(land[a].at[j])):
                    pltpu.make_async_remote_copy(sp, dp, send.at[3 * a + j], recv.at[3 * a + j],
                                                 device_id=(cx, cy, c), device_id_type=MESH).start()
        token[...] = jnp.zeros_like(token)

    lands = [pltpu.with_memory_space_constraint(lax.empty((NCHIP - 1,) + p.shape[1:], p.dtype), pltpu.HBM) for p in parts]
    srcs = [pltpu.with_memory_space_constraint(p, pltpu.HBM) for p in parts]
    outs = pl.pallas_call(
        body, name=name,
        out_shape=(pltpu.SemaphoreType.DMA((3 * n,)), pltpu.SemaphoreType.DMA((3 * n,)),
                   *[pltpu.HBM(p.shape, p.dtype) for p in parts], *[pltpu.HBM(l.shape, l.dtype) for l in lands],
                   SDS((8, 128), F32)),
        in_specs=[HBM] * (2 * n), out_specs=(SEM, SEM, *([HBM] * (2 * n)), pl.BlockSpec(memory_space=pltpu.VMEM)),
        input_output_aliases={i: 2 + i for i in range(2 * n)},
        compiler_params=pltpu.CompilerParams(has_side_effects=EFFECT),
    )(*srcs, *lands)
    return outs[0], outs[1], list(outs[2:2 + n]), list(outs[2 + n:2 + 2 * n]), outs[2 + 2 * n]


def scatter_wait(name, send, recv, parts, lands, after):
    n = len(parts)
    afters = _as_list(after)

    def body(*refs):
        src, land = refs[:n], refs[n:2 * n]
        send_ref, recv_ref = refs[2 * n], refs[2 * n + 1]
        x, y, c, chips = _place()
        for a in range(n):
            for j, (cx, cy) in enumerate(chips):
                cp = pltpu.make_async_remote_copy(src[a].at[2 * cx + cy], land[a].at[j], send_ref.at[3 * a + j], recv_ref.at[3 * a + j],
                                                  device_id=(cx, cy, c), device_id_type=MESH)
                cp.wait_send()
                cp.wait_recv()

    outs = pl.pallas_call(
        body, name=name,
        out_shape=(*[pltpu.HBM(p.shape, p.dtype) for p in parts], *[pltpu.HBM(l.shape, l.dtype) for l in lands]),
        in_specs=[HBM] * (2 * n) + [SEM, SEM] + [ANY] * len(afters), out_specs=[HBM] * (2 * n),
        input_output_aliases={i: i for i in range(2 * n)},
        compiler_params=pltpu.CompilerParams(has_side_effects=EFFECT),
    )(*parts, *lands, send, recv, *afters)
    return list(outs[:n]), list(outs[n:])


def join_start(name, halves):
    n = len(halves)

    def body(*refs):
        src, dst = refs[:n], refs[n:2 * n]
        send, recv = refs[2 * n], refs[2 * n + 1]
        token = refs[4 * n + 2]
        x, y, c, _ = _place()
        sib = (x, y, 1 - c)
        for a in range(n):
            for sp, dp in zip(_pieces(src[a]), _pieces(dst[a])):
                pltpu.make_async_remote_copy(sp, dp, send.at[a], recv.at[a], device_id=sib, device_id_type=MESH).start()
        token[...] = jnp.zeros_like(token)

    lands = [pltpu.with_memory_space_constraint(lax.empty(h.shape, h.dtype), pltpu.HBM) for h in halves]
    srcs = [pltpu.with_memory_space_constraint(h, pltpu.HBM) for h in halves]
    outs = pl.pallas_call(
        body, name=name,
        out_shape=(pltpu.SemaphoreType.DMA((n,)), pltpu.SemaphoreType.DMA((n,)),
                   *[pltpu.HBM(h.shape, h.dtype) for h in halves], *[pltpu.HBM(l.shape, l.dtype) for l in lands],
                   SDS((8, 128), F32)),
        in_specs=[HBM] * (2 * n), out_specs=(SEM, SEM, *([HBM] * (2 * n)), pl.BlockSpec(memory_space=pltpu.VMEM)),
        input_output_aliases={i: 2 + i for i in range(2 * n)},
        compiler_params=pltpu.CompilerParams(has_side_effects=EFFECT),
    )(*srcs, *lands)
    return outs[0], outs[1], list(outs[2:2 + n]), list(outs[2 + n:2 + 2 * n]), outs[2 + 2 * n]


def join_wait(name, send, recv, halves, lands, after):
    n = len(halves)
    afters = _as_list(after)

    def body(*refs):
        src, dst = refs[:n], refs[n:2 * n]
        send_ref, recv_ref = refs[2 * n], refs[2 * n + 1]
        x, y, c, _ = _place()
        sib = (x, y, 1 - c)
        for a in range(n):
            cp = pltpu.make_async_remote_copy(src[a], dst[a], send_ref.at[a], recv_ref.at[a], device_id=sib, device_id_type=MESH)
            cp.wait_send()
            cp.wait_recv()

    outs = pl.pallas_call(
        body, name=name,
        out_shape=(*[pltpu.HBM(h.shape, h.dtype) for h in halves], *[pltpu.HBM(l.shape, l.dtype) for l in lands]),
        in_specs=[HBM] * (2 * n) + [SEM, SEM] + [ANY] * len(afters), out_specs=[HBM] * (2 * n),
        input_output_aliases={i: i for i in range(2 * n)},
        compiler_params=pltpu.CompilerParams(has_side_effects=EFFECT),
    )(*halves, *lands, send, recv, *afters)
    return list(outs[:n]), list(outs[n:])


def gather_small(name, xs, reduce, after=None):
    m, ncol = xs.shape
    afters = _as_list(after)

    def body(x_ref, *rest):
        out_ref, all_ref, send, recv, lsem = rest[len(afters):]
        x, y, c, chips = _place()
        me, sib = (x, y, c), (x, y, 1 - c)

        def rows(px, py, pc):
            return all_ref.at[pl.ds((4 * px + 2 * py + pc) * m, m), :]

        def copy(k, block, to, src=None):
            return pltpu.make_async_remote_copy(rows(*block) if src is None else src, rows(*block), send.at[k], recv.at[k],
                                                device_id=to, device_id_type=MESH)

        mine = pltpu.make_async_copy(x_ref, rows(*me), lsem)
        mine.start()
        first = [copy(0, me, sib, src=x_ref)] + [copy(1 + j, me, (*chip, c), src=x_ref) for j, chip in enumerate(chips)]
        for cp in first:
            cp.start()
        passed = [copy(4 + j, (*chip, c), sib) for j, chip in enumerate(chips)]
        for j, chip in enumerate(chips):
            copy(1 + j, (*chip, c), me).wait_recv()
            passed[j].start()
        copy(0, sib, me).wait_recv()
        for j, chip in enumerate(chips):
            copy(4 + j, (*chip, 1 - c), me).wait_recv()
        for cp in first + passed:
            cp.wait_send()
        mine.wait()
        if reduce:
            s = all_ref[0:m, :]
            for dev in range(1, 8):
                s = s + all_ref[dev * m:(dev + 1) * m, :]
            out_ref[...] = s
        else:
            out_ref[...] = all_ref[...]

    vm = pl.BlockSpec(memory_space=pltpu.VMEM)
    return pl.pallas_call(
        body, name=name, in_specs=[vm] + [ANY] * len(afters), out_specs=vm,
        out_shape=SDS((m, ncol) if reduce else (8 * m, ncol), F32),
        scratch_shapes=[pltpu.VMEM((8 * m, ncol), F32), pltpu.SemaphoreType.DMA((7,)), pltpu.SemaphoreType.DMA((7,)),
                        pltpu.SemaphoreType.DMA],
    )(xs, *afters)


RELAYOUT_ROWS = 128


def weights_to_cat(name, land, own, place, other, prev=None, after=None):
    tm = RELAYOUT_ROWS
    nb = (D // 2) // tm
    extra = ([] if prev is None else [prev]) + _as_list(after)

    def half(p):
        return 1 - p[0] if other else p[0]

    def body(p_ref, g_ref, own_ref, *rest):
        o_ref = rest[len(extra)]
        nat = jnp.concatenate([jnp.where(p_ref[1] == j, own_ref[...], g_ref[j]) for j in range(NCHIP)], axis=1)
        pad = jnp.zeros((tm, NCAT - OA - 16), BF16)
        o_ref[...] = jnp.concatenate([nat[:, 3072:7168], nat[:, 7184:11280], nat[:, 0:3072], nat[:, 7168:7184], pad], axis=1)

    grid_spec = pltpu.PrefetchScalarGridSpec(
        num_scalar_prefetch=1, grid=(nb,),
        in_specs=[pl.BlockSpec((NCHIP, None, tm, IN_SHARD), lambda i, p: (0, half(p), i, 0)),
                  pl.BlockSpec((None, tm, IN_SHARD), lambda i, p: (half(p), i, 0))] + [ANY] * len(extra),
        out_specs=pl.BlockSpec((tm, NCAT), lambda i, p: (half(p) * nb + i, 0)))
    return pl.pallas_call(
        body, name=name, grid_spec=grid_spec, out_shape=PINNED((D, NCAT), BF16),
        input_output_aliases={} if prev is None else {3: 0},
        compiler_params=_cparams(40 * 1024 * 1024, ("arbitrary",)),
    )(place, land, own, *extra)


def grads_from_cat(gw_cat):
    tm = RELAYOUT_ROWS
    nb = (D // 2) // tm

    def body(c_ref, o_ref):
        cat = c_ref[...]
        nat = jnp.concatenate([cat[:, OU:OA], cat[:, OV:OGP], cat[:, OA:OA + 16], cat[:, OGP:OU]], axis=1)
        for j in range(NCHIP):
            o_ref[j] = nat[:, j * IN_SHARD:(j + 1) * IN_SHARD]

    return pl.pallas_call(
        body, name="grads_from_cat", grid=(D // tm,), in_specs=[pl.BlockSpec((tm, NCAT), lambda i: (i, 0))],
        out_specs=pl.BlockSpec((None, NCHIP, tm, IN_SHARD), lambda i: (i // nb, 0, i % nb, 0)),
        out_shape=PINNED((2, NCHIP, D // 2, IN_SHARD), BF16), compiler_params=_cparams(40 * 1024 * 1024, ("arbitrary",)),
    )(gw_cat)


def _pad_rows(a, rows):
    return jnp.concatenate([a, jnp.zeros((rows - a.shape[0],) + a.shape[1:], a.dtype)], axis=0)


def local_step(x2d, tgt, gf, g1, pool_scale, wa_pad, b_alpha, ng, g2, get_w, on_grad=None, on_settle=None, tick=None):
    emit = on_grad if on_grad is not None else (lambda group, grads: None)
    settle = on_settle if on_settle is not None else (lambda group, after: None)
    h1 = norm1(x2d, g1)
    wcat, pw = get_w("in", h1)
    pcat = mm_in(h1, wcat)
    pinned = tick("pool", pcat) if tick is not None else None
    dpool, ylin = pool_fwd(pcat, pw, pinned)
    og, o, states = gla_fwd(pcat, wa_pad, b_alpha, ng, ylin)
    w_go, w_o = get_w("mid", og)
    mixed, ygla = mm_gla_out(og, w_go, ylin, pcat, pool_scale)
    x2, h2 = mm_out(mixed, w_o, x2d, g2)
    w_up = get_w("up", h2)
    rup, act = mm_up(h2, w_up)
    w_dn = get_w("down", act)
    dx3, dx3b, g_nf, loss_row = mm_down(act, w_dn, x2, tgt, gf)

    gw_down = mm_wgrad("mm_dw_down", act, dx3b, DFF, D, (2, NCHIP, D // 2, D), (None, None, D // 2, D),
                       lambda j, i, k: (i % 2, i // 2, 0, 0), D // 2, D)
    token = emit("down", {"down": gw_down})
    dup = mm_dact(dx3b, w_dn, rup, after=token)
    token = settle("down", dup)
    dx2, dx2b, g_mlp = mm_dh2(dup, w_up, x2, dx3, g2, after=token)
    gw_up = mm_wgrad("mm_dw_up", h2, dup, D, DFF, (2, NCHIP, D // 2, D), (None, None, D // 2, D),
                     lambda j, i, k: (i, j, 0, 0), D // 2, D)
    token = emit("up", {"up": gw_up})
    dylin, dygla, dpcat, g_ps = mm_dmixed(dx2b, w_o, pcat, ylin, ygla, pool_scale, after=token)
    token = settle("up", dylin)
    gw_out = mm_wgrad("mm_dw_out", mixed, dx2b, D, D, (2, NCHIP, 256, D), (2, None, 256, D),
                      lambda j, i, k: (0, i, 0, 0), 512, D)
    do, dpcat, g_ng = mm_dog(dygla, w_go, o, pcat, ng, dpcat, after=token)
    gw_go = mm_wgrad("mm_dw_gla_out", og, dygla, D, D, (2, NCHIP, 256, D), (2, None, 256, D),
                     lambda j, i, k: (0, i, 0, 0), 512, D)
    token = emit("mix", {"out": gw_out, "gla_out": gw_go})
    dpcat, dv, g_wa, g_ba = gla_bwd(do, pcat, states, wa_pad, b_alpha, dpcat, b_alpha if token is None else token)
    token = settle("mix", dv)
    dpcat, dpw = pool_bwd(dylin, dpool, pw, lax.dynamic_update_slice(dpcat, dv, (0, OV)))
    gw_cat = mm_wgrad("mm_dw_in", h1, dpcat, D, NCAT, (D, NCAT), (1024, 1280), lambda j, i, k: (i, j), 1024, 1280, after=token)
    token = settle("in", emit("in", {"in_cat": gw_cat, "pool": dpw}))
    grad_x, g_mix = mm_dh1(dpcat, wcat, x2d, dx2, g1, after=token)
    return (loss_row[0, 0], grad_x, g_mix, g_ps, g_mlp, g_nf, g_ng, g_ba, g_wa, token,
            gw_cat, dpw, gw_go, gw_out, gw_up, gw_down)


def kernel(x, norm_mix_g, w_in, pool_w, pool_scale, w_alpha, b_alpha, gla_norm_g, w_gla_out, w_out, norm_mlp_g, w_mlp_up, w_mlp_down, norm_final_g, loss_target, m_norm_mix_g, m_w_in, m_pool_w, m_pool_scale, m_w_alpha, m_b_alpha, m_gla_norm_g, m_w_gla_out, m_w_out, m_norm_mlp_g, m_w_mlp_up, m_w_mlp_down, m_norm_final_g, v_norm_mix_g, v_w_in, v_pool_w, v_pool_scale, v_w_alpha, v_b_alpha, v_gla_norm_g, v_w_gla_out, v_w_out, v_norm_mlp_g, v_w_mlp_up, v_w_mlp_down, v_norm_final_g):
    chip = 2 * lax.axis_index("x") + lax.axis_index("y")
    chip_i = chip.astype(jnp.int32).reshape(1)
    core_i = lax.axis_index("c").astype(jnp.int32).reshape(1)
    place_i = jnp.concatenate([core_i, chip_i])
    tgt = loss_target.reshape(T, D)
    gf = norm_final_g.reshape(1, D)

    def halves(w2d):
        r, c = w2d.shape
        return lax.dynamic_update_index_in_dim(lax.empty((NCHIP, 2, r // 2, c), BF16), w2d.astype(BF16).reshape(2, r // 2, c),
                                               chip, 0)

    pool_shard = pool_w.reshape(4 * PG, PO // NCHIP)
    w_in_r = w_in.reshape(2, D // 2, IN_SHARD)
    sent = {"in": [cast_bf16("cast_w_in", w_in_r), halves(pool_shard)]}
    flight = {}

    def start(group, after=None):
        flight[group] = gather_start("gather_start_" + group, sent[group], after)

    def relay(group, after):
        send, recv, shards, lands = flight[group]
        flight[group] = relay_turn("relay_turn_" + group, send, recv, shards, lands, after)

    def fetch(group, after):
        send, recv, shards, lands = flight[group]
        lands = relay_wait("relay_wait_" + group, send, recv, lands, after)
        return forward_halves("forward_" + group, lands)

    small_w = pack_rows("pack_small_w", [w_alpha[0].reshape(4, QK),
                                         jnp.concatenate([gla_norm_g[0].reshape(1, 512), jnp.zeros((1, 512), F32)], axis=1)], 8)
    sw_all = gather_small("gather_small_w", small_w, False).reshape(8, 8, QK)
    start("in", sw_all)
    m_in_f, v_in_f, w_go_f, w_o_f, w_up_f, w_dn_f, x_f = lax.optimization_barrier(
        (m_w_in, v_w_in, w_gla_out, w_out, w_mlp_up, w_mlp_down, x, flight["in"][2][0]))[:7]
    m_in_r, v_in_r = m_in_f.reshape(2, D // 2, IN_SHARD), v_in_f.reshape(2, D // 2, IN_SHARD)
    sent["mid"] = [halves(w_go_f[0]), halves(w_o_f[0])]
    relay("in", [m_in_r, v_in_r, *sent["mid"]])
    w_up_f, w_dn_f, x_f = lax.optimization_barrier((w_up_f, w_dn_f, x_f, flight["in"][3][0]))[:3]
    sent["up"] = [cast_to_slot("cast_w_up", w_up_f[0].reshape(2, D // 2, D), place_i)]
    sent["down"] = [cast_to_slot("cast_w_down", w_dn_f[0].reshape(2, DFF // NCHIP // 2, D), place_i)]
    x2d = x_f.reshape(T, D)
    big = [w_in_r, w_go_f[0], w_o_f[0], w_up_f[0], w_dn_f[0], pool_shard]

    def tick(point, after):
        if point == "pool":
            relay("mid", after)
            relay("up", flight["mid"][3][0])
            start("down", flight["up"][3][0])
            return [flight["up"][3][0], flight["down"][3][0]]

    def get_w(group, after):
        if group == "in":
            after = [after, *sent["up"], *sent["down"], wa_pad]
        if group == "up":
            relay("down", after)
            send, recv, lands, shards = flight["up"]
            lands = forward_wait("forward_wait_up", send, recv, lands, flight["down"][3][0])
            return lands[0].reshape(NCHIP, D, D)
        if group == "in":
            send, recv, shards, lands = flight["in"]
            send, recv, lands = forward_turn("forward_turn_in", send, recv, lands, after)
            start("mid", lands[0])
            start("up", flight["mid"][3][0])
            wcat = weights_to_cat("weights_to_cat_mine", lands[0], shards[0], place_i, False, after=flight["up"][3][0])
            lands = forward_wait("forward_wait_in", send, recv, lands, wcat)
            wcat = weights_to_cat("weights_to_cat_sibling", lands[0], shards[0], place_i, True, prev=wcat)
            g_pool = lands[1]
            pw = jnp.concatenate([g_pool[j].reshape(4, PG, PO // NCHIP) for j in range(NCHIP)], axis=2)
            return wcat, pw
        whole = fetch(group, after)
        if group == "mid":
            send, recv, shards, lands = flight["up"]
            flight["up"] = (*forward_turn("forward_turn_up", send, recv, lands, whole[0]), shards)
            w_go, w_o, _ = lax.optimization_barrier((whole[0], whole[1], flight["up"][2][0]))
            return w_go.reshape(D, D), w_o.reshape(D, D)
        return whole[0].reshape(DFF, D)

    wa_full = jnp.concatenate([sw_all[2 * j, 0:4].reshape(16, DK) for j in range(NCHIP)], axis=1)
    ng_full = jnp.concatenate([sw_all[2 * j, 4, 0:512].reshape(HEADS, DV // NCHIP) for j in range(NCHIP)], axis=1)
    wa_pad = _pad_rows(wa_full, APAD).astype(BF16)
    ng = ng_full.reshape(1, D)

    pending = {}
    wmv = {"in": (w_in_r, m_in_r, v_in_r), "gla_out": (big[1], m_w_gla_out, v_w_gla_out), "out": (big[2], m_w_out, v_w_out),
           "up": (big[3], m_w_mlp_up, v_w_mlp_up), "down": (big[4], m_w_mlp_down, v_w_mlp_down), "pool": (big[5], m_pool_w, v_pool_w)}
    big_res = {}

    def reduce_group(group, after):
        nms, send, recv, sums, lands = pending[group]
        sums, lands = scatter_wait("scatter_wait_" + group, send, recv, sums, lands, after)
        reduced = [sum_chips("sum_chips_" + nm, a, b, chip_i) for nm, a, b in zip(nms, sums, lands)]
        send, recv, reduced, lands, token = join_start("join_start_" + group, reduced)
        pending[group] = (nms, send, recv, reduced, lands)
        return token

    def update_group(group, after):
        nms, send, recv, reduced, lands = pending[group]
        reduced, from_sib = join_wait("join_wait_" + group, send, recv, reduced, lands, after)
        for nm, g_own, g_sib in zip(nms, reduced, from_sib):
            w, m, v = wmv[nm]
            shp = (2,) + g_own.shape
            big_res[nm] = adamw_halves("adamw_" + nm, w.reshape(shp), g_own, g_sib, m.reshape(shp), v.reshape(shp), core_i)

    def on_grad(group, grads):
        if group == "in":
            gw_in = grads_from_cat(grads["in_cat"])
            gw_pool = jnp.stack([grads["pool"][:, :, j * 128:(j + 1) * 128].reshape(2, 2 * PG, 128)
                                 for j in range(NCHIP)], axis=1)
            grads = {"in": gw_in, "pool": gw_pool}
        nms, parts = list(grads.keys()), list(grads.values())
        send, recv, parts, got, token = exchange_start("exchange_start_" + group, parts)
        pending[group] = (nms, send, recv, parts, got)
        return token

    def on_settle(group, after):
        if group == "in":
            for earlier in ("down", "up", "mix"):
                after = reduce_group(earlier, after)
        nms, send, recv, parts, got = pending[group]
        parts, got = exchange_wait("exchange_wait_" + group, send, recv, parts, got, after)
        sums = [add_pairs("add_pair_" + nm, a, b, core_i) for nm, a, b in zip(nms, parts, got)]
        send, recv, sums, lands, token = scatter_start("scatter_start_" + group, sums)
        pending[group] = (nms, send, recv, sums, lands)
        if group != "in":
            return token
        for earlier in ("down", "up", "mix"):
            update_group(earlier, token)
            token = big_res[pending[earlier][0][-1]][1]
        return [big_res[nm][1] for nm in ("down", "up", "out", "gla_out")]

    (loss_local, grad_x, g_mix, g_ps, g_mlp, g_nf, g_ng, g_ba, g_wa) = local_step(
        x2d, tgt, gf, norm_mix_g, pool_scale, wa_pad, b_alpha, ng, norm_mlp_g, get_w, on_grad, on_settle, tick)[:9]
    loss = lax.psum(loss_local, ("x", "y", "c"))
    join_in_token = reduce_group("in", grad_x)

    ROWS = 16

    def wide(a, n):
        return jnp.concatenate([a.reshape(1, n), jnp.zeros((1, D - n), F32)], axis=1)

    packed = pack_rows("pack_small_g", [g_mix, g_ps, g_mlp, g_nf, g_ng, wide(g_ba, QK), g_wa[0:16].reshape(8, D)], ROWS)
    tot = gather_small("reduce_small_g", packed, True, join_in_token)
    t_wa = lax.dynamic_slice(tot[6:14].reshape(16, QK), (0, chip * DK), (16, DK))
    t_ng = lax.dynamic_slice(tot[4].reshape(HEADS, DV), (0, chip * (DV // NCHIP)), (HEADS, DV // NCHIP))

    def pack_small(nm, mix, ps, mlp, nf, ba, wa, gn, after=None):
        return pack_rows(nm, [mix.reshape(1, D), ps.reshape(1, D), mlp.reshape(1, D), nf.reshape(1, D), wide(ba, QK),
                              wa.reshape(2, D), wide(gn, 512)], ROWS, after)

    update_group("in", tot)
    sg = pack_small("pack_g", tot[0], tot[1], tot[2], tot[3], tot[5, 0:QK], t_wa, t_ng, big_res["in"][3])
    sw = pack_small("pack_w", norm_mix_g, pool_scale, norm_mlp_g, norm_final_g, b_alpha, w_alpha, gla_norm_g)
    sm = pack_small("pack_m", m_norm_mix_g, m_pool_scale, m_norm_mlp_g, m_norm_final_g, m_b_alpha, m_w_alpha, m_gla_norm_g)
    sv = pack_small("pack_v", v_norm_mix_g, v_pool_scale, v_norm_mlp_g, v_norm_final_g, v_b_alpha, v_w_alpha, v_gla_norm_g)
    small_res = adamw("adamw_small", sw, sg, sm, sv)

    def unpack(p):
        return {"norm_mix_g": p[0].reshape(1, D), "pool_scale": p[1].reshape(1, D), "norm_mlp_g": p[2].reshape(1, D),
                "norm_final_g": p[3].reshape(D), "b_alpha": p[4, 0:QK].reshape(1, QK), "w_alpha": p[5:7].reshape(1, 16, DK),
                "gla_norm_g": p[7, 0:512].reshape(1, HEADS, DV // NCHIP)}

    order = ["norm_mix_g", "w_in", "pool_w", "pool_scale", "w_alpha", "b_alpha", "gla_norm_g", "w_gla_out", "w_out",
             "norm_mlp_g", "w_mlp_up", "w_mlp_down", "norm_final_g"]
    big_key = {"w_in": ("in", w_in.shape), "pool_w": ("pool", pool_w.shape), "w_gla_out": ("gla_out", w_gla_out.shape),
               "w_out": ("out", w_out.shape), "w_mlp_up": ("up", w_mlp_up.shape), "w_mlp_down": ("down", w_mlp_down.shape)}
    result = [loss, grad_x.reshape(1, T, D)]
    for kind in range(4):
        small = unpack(small_res[kind])
        for nm in order:
            if nm in big_key:
                key, shp = big_key[nm]
                result.append(big_res[key][kind].reshape(shp))
            else:
                result.append(small[nm])
    return tuple(result)
```
